```python
import jax, jax.numpy as jnp
from jax import lax
import numpy as np

D_MODEL = 1024
BATCH = 8
SEQ = 4096
DEPTH = 2

N_MIXERS = 2
N_LRU = (DEPTH + 1) // 2
N_FOX = DEPTH // 2
D_RNN = D_MODEL
LRU_BLOCKS = 16
LRU_BLOCK_DIM = D_RNN // LRU_BLOCKS
CONV_WIDTH = 4
LRU_C = 8.0
N_HEADS = 16
HEAD_DIM = D_MODEL // N_HEADS
Q_BLOCK = 128
D_FF = 4 * D_MODEL
EPS = 1e-6
NEG_INF = -1e30

kernel_name = "hawk_fox_interleaved_trunk"


def rms_norm(x, g):
    xf = x.astype(jnp.float32)
    y = xf * lax.rsqrt(jnp.mean(xf * xf, axis=-1, keepdims=True) + EPS)
    return (y * g.astype(jnp.float32)).astype(x.dtype)


def causal_depthwise_conv(x, w, b):
    S = x.shape[1]
    xp = jnp.pad(x, ((0, 0), (CONV_WIDTH - 1, 0), (0, 0)))
    y = b
    for k in range(CONV_WIDTH):
        y = y + xp[:, k:k + S] * w[k]
    return y


def rglru_mixer(h, w_in, conv_w, conv_b, w_r, b_r, w_i, b_i, lam, w_out):
    B, S, _ = h.shape
    u = h @ w_in
    gate_branch, x_branch = jnp.split(u, 2, axis=-1)
    gate_branch = jax.nn.gelu(gate_branch)
    xc = causal_depthwise_conv(x_branch, conv_w, conv_b)
    xb = xc.reshape(B, S, LRU_BLOCKS, LRU_BLOCK_DIM)
    r = jax.nn.sigmoid((jnp.einsum('bsnd,nde->bsne', xb, w_r) + b_r).astype(jnp.float32))
    i = jax.nn.sigmoid((jnp.einsum('bsnd,nde->bsne', xb, w_i) + b_i).astype(jnp.float32))
    r = r.reshape(B, S, D_RNN)
    i = i.reshape(B, S, D_RNN)
    log_a = LRU_C * r * jax.nn.log_sigmoid(lam.astype(jnp.float32))
    a = jnp.exp(log_a)
    mult = jnp.sqrt(-jnp.expm1(2.0 * log_a))
    bt = mult * (i * xc.astype(jnp.float32))

    def combine(left, right):
        a_l, b_l = left
        a_r, b_r_ = right
        return a_l * a_r, a_r * b_l + b_r_

    _, hs = lax.associative_scan(combine, (a, bt), axis=1)
    y = gate_branch * hs.astype(h.dtype)
    return y @ w_out


def fox_mixer(h, w_in, b_f, q_gain, k_gain, w_out):
    B, S, _ = h.shape
    u = h @ w_in
    q, k, v, f_logit = jnp.split(u, [D_MODEL, 2 * D_MODEL, 3 * D_MODEL], axis=-1)

    def heads(t):
        return t.reshape(B, S, N_HEADS, HEAD_DIM).transpose(0, 2, 1, 3)

    q = rms_norm(heads(q), q_gain)
    k = rms_norm(heads(k), k_gain)
    v = heads(v)
    log_f = jax.nn.log_sigmoid((f_logit + b_f).astype(jnp.float32))
    c = jnp.cumsum(log_f, axis=1).transpose(0, 2, 1)
    scale = HEAD_DIM ** -0.5
    nb = S // Q_BLOCK
    q_blocks = q.reshape(B, N_HEADS, nb, Q_BLOCK, HEAD_DIM).transpose(2, 0, 1, 3, 4)
    c_blocks = c.reshape(B, N_HEADS, nb, Q_BLOCK).transpose(2, 0, 1, 3)
    k_pos = jnp.arange(S)

    def attend(args):
        qb, cb, blk = args
        q_pos = blk * Q_BLOCK + jnp.arange(Q_BLOCK)
        s = jnp.einsum('bhqd,bhkd->bhqk', qb, k).astype(jnp.float32) * scale
        s = s + cb[..., :, None] - c[:, :, None, :]
        s = jnp.where(k_pos[None, :] <= q_pos[:, None], s, NEG_INF)
        p = jax.nn.softmax(s, axis=-1)
        return jnp.einsum('bhqk,bhkd->bhqd', p.astype(v.dtype), v)

    o = lax.map(attend, (q_blocks, c_blocks, jnp.arange(nb)))
    o = o.transpose(1, 0, 3, 2, 4).reshape(B, S, D_MODEL)
    return o @ w_out


def sq_relu_mlp(h, w1, w2):
    return jnp.square(jax.nn.relu(h @ w1)) @ w2


def _fwd_setup_inputs(seed: int = 0) -> dict:
    key = jax.random.key(seed)
    ks = jax.random.split(key, 20)
    f32 = jnp.float32
    nrm = lambda k, shape, fan_in: jax.random.normal(k, shape, f32) * (fan_in ** -0.5)
    a0 = jax.random.uniform(ks[12], (N_LRU, D_RNN), f32, minval=0.9, maxval=0.999)
    return {
        "x": jax.random.normal(ks[0], (BATCH, SEQ, D_MODEL), f32),
        "mix_norm": 1.0 + 0.02 * jax.random.normal(ks[1], (DEPTH, D_MODEL), f32),
        "mlp_norm": 1.0 + 0.02 * jax.random.normal(ks[2], (DEPTH, D_MODEL), f32),
        "mlp_w1": nrm(ks[3], (DEPTH, D_MODEL, D_FF), D_MODEL),
        "mlp_w2": nrm(ks[4], (DEPTH, D_FF, D_MODEL), D_FF),
        "lru_w_in": nrm(ks[5], (N_LRU, D_MODEL, 2 * D_RNN), D_MODEL),
        "lru_conv_w": nrm(ks[6], (N_LRU, CONV_WIDTH, D_RNN), CONV_WIDTH),
        "lru_conv_b": 0.02 * jax.random.normal(ks[7], (N_LRU, D_RNN), f32),
        "lru_w_r": nrm(ks[8], (N_LRU, LRU_BLOCKS, LRU_BLOCK_DIM, LRU_BLOCK_DIM), LRU_BLOCK_DIM),
        "lru_b_r": 0.1 * jax.random.normal(ks[9], (N_LRU, LRU_BLOCKS, LRU_BLOCK_DIM), f32),
        "lru_w_i": nrm(ks[10], (N_LRU, LRU_BLOCKS, LRU_BLOCK_DIM, LRU_BLOCK_DIM), LRU_BLOCK_DIM),
        "lru_b_i": 0.1 * jax.random.normal(ks[11], (N_LRU, LRU_BLOCKS, LRU_BLOCK_DIM), f32),
        "lru_lambda": jnp.log(a0) - jnp.log1p(-a0),
        "lru_w_out": nrm(ks[13], (N_LRU, D_RNN, D_MODEL), D_RNN),
        "fox_w_in": nrm(ks[14], (N_FOX, D_MODEL, 3 * D_MODEL + N_HEADS), D_MODEL),
        "fox_b_f": jax.random.uniform(ks[15], (N_FOX, N_HEADS), f32, minval=1.0, maxval=6.0),
        "fox_q_gain": 1.0 + 0.02 * jax.random.normal(ks[16], (N_FOX, HEAD_DIM), f32),
        "fox_k_gain": 1.0 + 0.02 * jax.random.normal(ks[17], (N_FOX, HEAD_DIM), f32),
        "fox_w_out": nrm(ks[18], (N_FOX, D_MODEL, D_MODEL), D_MODEL),
    }


def _fwd_reference(x, mix_norm, mlp_norm, mlp_w1, mlp_w2, lru_w_in, lru_conv_w, lru_conv_b,
              lru_w_r, lru_b_r, lru_w_i, lru_b_i, lru_lambda, lru_w_out,
              fox_w_in, fox_b_f, fox_q_gain, fox_k_gain, fox_w_out):
    for layer in range(DEPTH):
        h = rms_norm(x, mix_norm[layer])
        j = layer // N_MIXERS
        if layer % N_MIXERS == 0:
            mixed = rglru_mixer(h, lru_w_in[j], lru_conv_w[j], lru_conv_b[j], lru_w_r[j],
                                lru_b_r[j], lru_w_i[j], lru_b_i[j], lru_lambda[j], lru_w_out[j])
        else:
            mixed = fox_mixer(h, fox_w_in[j], fox_b_f[j], fox_q_gain[j], fox_k_gain[j],
                              fox_w_out[j])
        x = x + mixed
        x = x + sq_relu_mlp(rms_norm(x, mlp_norm[layer]), mlp_w1[layer], mlp_w2[layer])
    return x


import jax as _jax
import jax.numpy as _jnp

TWIN_FORMAT = 'train_step'
FWD_PARAMS = ['x', 'mix_norm', 'mlp_norm', 'mlp_w1', 'mlp_w2', 'lru_w_in', 'lru_conv_w', 'lru_conv_b', 'lru_w_r', 'lru_b_r', 'lru_w_i', 'lru_b_i', 'lru_lambda', 'lru_w_out', 'fox_w_in', 'fox_b_f', 'fox_q_gain', 'fox_k_gain', 'fox_w_out']
TWIN_WEIGHTS = ['mix_norm', 'mlp_norm', 'mlp_w1', 'mlp_w2', 'lru_w_in', 'lru_conv_w', 'lru_conv_b', 'lru_w_r', 'lru_b_r', 'lru_w_i', 'lru_b_i', 'lru_lambda', 'lru_w_out', 'fox_w_in', 'fox_b_f', 'fox_q_gain', 'fox_k_gain', 'fox_w_out']
TWIN_DIFF_INPUT = 'x'
TWIN_INPUTS = ['x', 'mix_norm', 'mlp_norm', 'mlp_w1', 'mlp_w2', 'lru_w_in', 'lru_conv_w', 'lru_conv_b', 'lru_w_r', 'lru_b_r', 'lru_w_i', 'lru_b_i', 'lru_lambda', 'lru_w_out', 'fox_w_in', 'fox_b_f', 'fox_q_gain', 'fox_k_gain', 'fox_w_out', 'loss_target', 'm_mix_norm', 'm_mlp_norm', 'm_mlp_w1', 'm_mlp_w2', 'm_lru_w_in', 'm_lru_conv_w', 'm_lru_conv_b', 'm_lru_w_r', 'm_lru_b_r', 'm_lru_w_i', 'm_lru_b_i', 'm_lru_lambda', 'm_lru_w_out', 'm_fox_w_in', 'm_fox_b_f', 'm_fox_q_gain', 'm_fox_k_gain', 'm_fox_w_out', 'v_mix_norm', 'v_mlp_norm', 'v_mlp_w1', 'v_mlp_w2', 'v_lru_w_in', 'v_lru_conv_w', 'v_lru_conv_b', 'v_lru_w_r', 'v_lru_b_r', 'v_lru_w_i', 'v_lru_b_i', 'v_lru_lambda', 'v_lru_w_out', 'v_fox_w_in', 'v_fox_b_f', 'v_fox_q_gain', 'v_fox_k_gain', 'v_fox_w_out']
TWIN_OUTPUTS = ['loss', 'grad_x', 'grad_mix_norm', 'grad_mlp_norm', 'grad_mlp_w1', 'grad_mlp_w2', 'grad_lru_w_in', 'grad_lru_conv_w', 'grad_lru_conv_b', 'grad_lru_w_r', 'grad_lru_b_r', 'grad_lru_w_i', 'grad_lru_b_i', 'grad_lru_lambda', 'grad_lru_w_out', 'grad_fox_w_in', 'grad_fox_b_f', 'grad_fox_q_gain', 'grad_fox_k_gain', 'grad_fox_w_out', 'delta_mix_norm', 'delta_mlp_norm', 'delta_mlp_w1', 'delta_mlp_w2', 'delta_lru_w_in', 'delta_lru_conv_w', 'delta_lru_conv_b', 'delta_lru_w_r', 'delta_lru_b_r', 'delta_lru_w_i', 'delta_lru_b_i', 'delta_lru_lambda', 'delta_lru_w_out', 'delta_fox_w_in', 'delta_fox_b_f', 'delta_fox_q_gain', 'delta_fox_k_gain', 'delta_fox_w_out', 'new_m_mix_norm', 'new_m_mlp_norm', 'new_m_mlp_w1', 'new_m_mlp_w2', 'new_m_lru_w_in', 'new_m_lru_conv_w', 'new_m_lru_conv_b', 'new_m_lru_w_r', 'new_m_lru_b_r', 'new_m_lru_w_i', 'new_m_lru_b_i', 'new_m_lru_lambda', 'new_m_lru_w_out', 'new_m_fox_w_in', 'new_m_fox_b_f', 'new_m_fox_q_gain', 'new_m_fox_k_gain', 'new_m_fox_w_out', 'new_v_mix_norm', 'new_v_mlp_norm', 'new_v_mlp_w1', 'new_v_mlp_w2', 'new_v_lru_w_in', 'new_v_lru_conv_w', 'new_v_lru_conv_b', 'new_v_lru_w_r', 'new_v_lru_b_r', 'new_v_lru_w_i', 'new_v_lru_b_i', 'new_v_lru_lambda', 'new_v_lru_w_out', 'new_v_fox_w_in', 'new_v_fox_b_f', 'new_v_fox_q_gain', 'new_v_fox_k_gain', 'new_v_fox_w_out']
TWIN_LEAF_KINDS = {'loss': 'loss', 'grad_x': 'grad_x', 'grad_mix_norm': 'grad_w', 'grad_mlp_norm': 'grad_w', 'grad_mlp_w1': 'grad_w', 'grad_mlp_w2': 'grad_w', 'grad_lru_w_in': 'grad_w', 'grad_lru_conv_w': 'grad_w', 'grad_lru_conv_b': 'grad_w', 'grad_lru_w_r': 'grad_w', 'grad_lru_b_r': 'grad_w', 'grad_lru_w_i': 'grad_w', 'grad_lru_b_i': 'grad_w', 'grad_lru_lambda': 'grad_w', 'grad_lru_w_out': 'grad_w', 'grad_fox_w_in': 'grad_w', 'grad_fox_b_f': 'grad_w', 'grad_fox_q_gain': 'grad_w', 'grad_fox_k_gain': 'grad_w', 'grad_fox_w_out': 'grad_w', 'delta_mix_norm': 'delta_w', 'delta_mlp_norm': 'delta_w', 'delta_mlp_w1': 'delta_w', 'delta_mlp_w2': 'delta_w', 'delta_lru_w_in': 'delta_w', 'delta_lru_conv_w': 'delta_w', 'delta_lru_conv_b': 'delta_w', 'delta_lru_w_r': 'delta_w', 'delta_lru_b_r': 'delta_w', 'delta_lru_w_i': 'delta_w', 'delta_lru_b_i': 'delta_w', 'delta_lru_lambda': 'delta_w', 'delta_lru_w_out': 'delta_w', 'delta_fox_w_in': 'delta_w', 'delta_fox_b_f': 'delta_w', 'delta_fox_q_gain': 'delta_w', 'delta_fox_k_gain': 'delta_w', 'delta_fox_w_out': 'delta_w', 'new_m_mix_norm': 'new_m', 'new_m_mlp_norm': 'new_m', 'new_m_mlp_w1': 'new_m', 'new_m_mlp_w2': 'new_m', 'new_m_lru_w_in': 'new_m', 'new_m_lru_conv_w': 'new_m', 'new_m_lru_conv_b': 'new_m', 'new_m_lru_w_r': 'new_m', 'new_m_lru_b_r': 'new_m', 'new_m_lru_w_i': 'new_m', 'new_m_lru_b_i': 'new_m', 'new_m_lru_lambda': 'new_m', 'new_m_lru_w_out': 'new_m', 'new_m_fox_w_in': 'new_m', 'new_m_fox_b_f': 'new_m', 'new_m_fox_q_gain': 'new_m', 'new_m_fox_k_gain': 'new_m', 'new_m_fox_w_out': 'new_m', 'new_v_mix_norm': 'new_v', 'new_v_mlp_norm': 'new_v', 'new_v_mlp_w1': 'new_v', 'new_v_mlp_w2': 'new_v', 'new_v_lru_w_in': 'new_v', 'new_v_lru_conv_w': 'new_v', 'new_v_lru_conv_b': 'new_v', 'new_v_lru_w_r': 'new_v', 'new_v_lru_b_r': 'new_v', 'new_v_lru_w_i': 'new_v', 'new_v_lru_b_i': 'new_v', 'new_v_lru_lambda': 'new_v', 'new_v_lru_w_out': 'new_v', 'new_v_fox_w_in': 'new_v', 'new_v_fox_b_f': 'new_v', 'new_v_fox_q_gain': 'new_v', 'new_v_fox_k_gain': 'new_v', 'new_v_fox_w_out': 'new_v'}


def _forward(args):
    return _fwd_reference(*[args[k] for k in FWD_PARAMS])


def _output_shape():
    out = _jax.eval_shape(lambda: _forward(_fwd_setup_inputs(0)))
    return out.shape, out.dtype

N_MICROBATCH = 1
ADAM_LR = 0.001
ADAM_B1 = 0.9
ADAM_B2 = 0.999
ADAM_EPS = 1e-08
ADAM_WD = 0.01
ADAM_STEP = 10
PER_EXAMPLE_BATCH_AXIS = {'x': 0, 'loss_target': 0}
SHARED_INPUTS = []
_WEIGHT_DTYPES = {'mix_norm': _jnp.float32, 'mlp_norm': _jnp.float32, 'mlp_w1': _jnp.float32, 'mlp_w2': _jnp.float32, 'lru_w_in': _jnp.float32, 'lru_conv_w': _jnp.float32, 'lru_conv_b': _jnp.float32, 'lru_w_r': _jnp.float32, 'lru_b_r': _jnp.float32, 'lru_w_i': _jnp.float32, 'lru_b_i': _jnp.float32, 'lru_lambda': _jnp.float32, 'lru_w_out': _jnp.float32, 'fox_w_in': _jnp.float32, 'fox_b_f': _jnp.float32, 'fox_q_gain': _jnp.float32, 'fox_k_gain': _jnp.float32, 'fox_w_out': _jnp.float32}
MOMENT_SCALE = {'mix_norm': 1.185142e+01, 'mlp_norm': 9.717601e+01, 'mlp_w1': 4.678510e+00, 'mlp_w2': 1.870375e+01, 'lru_w_in': 5.860755e-01, 'lru_conv_w': 3.678281e+00, 'lru_conv_b': 3.561785e+01, 'lru_w_r': 1.348159e+00, 'lru_b_r': 1.052183e+00, 'lru_w_i': 2.639353e+00, 'lru_b_i': 2.286195e+00, 'lru_lambda': 1.560663e+00, 'lru_w_out': 2.359443e+00, 'fox_w_in': 6.544826e+00, 'fox_b_f': 4.492106e+01, 'fox_q_gain': 1.066866e+01, 'fox_k_gain': 1.068983e+01, 'fox_w_out': 1.054585e+01}


def _to_microbatches(a, axis):
    t = _jnp.moveaxis(a, axis, 0)
    t = t.reshape((N_MICROBATCH, t.shape[0] // N_MICROBATCH) + t.shape[1:])
    return _jnp.moveaxis(t, 1, axis + 1)


def setup_inputs(seed: int = 0) -> dict:
    inp = _fwd_setup_inputs(seed)
    key = _jax.random.fold_in(_jax.random.key(seed), 7919)
    shape, _ = _output_shape()
    out = dict(inp)
    out["loss_target"] = _jax.random.normal(_jax.random.fold_in(key, 0), shape, _jnp.float32)
    for i, name in enumerate(TWIN_WEIGHTS):
        w = inp[name].astype(_jnp.float32)
        if MOMENT_SCALE is None:
            s = _jnp.sqrt(_jnp.mean(_jnp.square(w)) + 1e-30)
        else:
            s = MOMENT_SCALE[name]
        km, kv = _jax.random.split(_jax.random.fold_in(key, i + 1))
        out[name] = w
        out["m_" + name] = s * _jax.random.normal(km, w.shape, _jnp.float32)
        out["v_" + name] = (s * s) * _jax.random.uniform(kv, w.shape, _jnp.float32, 0.5, 1.5)
    if N_MICROBATCH > 1:
        for name, axis in PER_EXAMPLE_BATCH_AXIS.items():
            out[name] = _to_microbatches(out[name], axis)
    return {'x': out['x'], 'mix_norm': out['mix_norm'], 'mlp_norm': out['mlp_norm'], 'mlp_w1': out['mlp_w1'], 'mlp_w2': out['mlp_w2'], 'lru_w_in': out['lru_w_in'], 'lru_conv_w': out['lru_conv_w'], 'lru_conv_b': out['lru_conv_b'], 'lru_w_r': out['lru_w_r'], 'lru_b_r': out['lru_b_r'], 'lru_w_i': out['lru_w_i'], 'lru_b_i': out['lru_b_i'], 'lru_lambda': out['lru_lambda'], 'lru_w_out': out['lru_w_out'], 'fox_w_in': out['fox_w_in'], 'fox_b_f': out['fox_b_f'], 'fox_q_gain': out['fox_q_gain'], 'fox_k_gain': out['fox_k_gain'], 'fox_w_out': out['fox_w_out'], 'loss_target': out['loss_target'], 'm_mix_norm': out['m_mix_norm'], 'm_mlp_norm': out['m_mlp_norm'], 'm_mlp_w1': out['m_mlp_w1'], 'm_mlp_w2': out['m_mlp_w2'], 'm_lru_w_in': out['m_lru_w_in'], 'm_lru_conv_w': out['m_lru_conv_w'], 'm_lru_conv_b': out['m_lru_conv_b'], 'm_lru_w_r': out['m_lru_w_r'], 'm_lru_b_r': out['m_lru_b_r'], 'm_lru_w_i': out['m_lru_w_i'], 'm_lru_b_i': out['m_lru_b_i'], 'm_lru_lambda': out['m_lru_lambda'], 'm_lru_w_out': out['m_lru_w_out'], 'm_fox_w_in': out['m_fox_w_in'], 'm_fox_b_f': out['m_fox_b_f'], 'm_fox_q_gain': out['m_fox_q_gain'], 'm_fox_k_gain': out['m_fox_k_gain'], 'm_fox_w_out': out['m_fox_w_out'], 'v_mix_norm': out['v_mix_norm'], 'v_mlp_norm': out['v_mlp_norm'], 'v_mlp_w1': out['v_mlp_w1'], 'v_mlp_w2': out['v_mlp_w2'], 'v_lru_w_in': out['v_lru_w_in'], 'v_lru_conv_w': out['v_lru_conv_w'], 'v_lru_conv_b': out['v_lru_conv_b'], 'v_lru_w_r': out['v_lru_w_r'], 'v_lru_b_r': out['v_lru_b_r'], 'v_lru_w_i': out['v_lru_w_i'], 'v_lru_b_i': out['v_lru_b_i'], 'v_lru_lambda': out['v_lru_lambda'], 'v_lru_w_out': out['v_lru_w_out'], 'v_fox_w_in': out['v_fox_w_in'], 'v_fox_b_f': out['v_fox_b_f'], 'v_fox_q_gain': out['v_fox_q_gain'], 'v_fox_k_gain': out['v_fox_k_gain'], 'v_fox_w_out': out['v_fox_w_out']}


def _loss(weights, diff, rest, loss_target):
    with _jax.named_scope("forward"):
        args = {**rest, TWIN_DIFF_INPUT: diff, **{k: w.astype(_WEIGHT_DTYPES[k]) for k, w in weights.items()}}
        y = _forward(args)
    with _jax.named_scope("loss_head"):
        err = _jnp.square(y.astype(_jnp.float32) - loss_target)
        return 0.5 * _jnp.sum(_jnp.mean(err, axis=-1)) if err.ndim else 0.5 * err


def _adamw(w, g, m, v):
    m = ADAM_B1 * m + (1.0 - ADAM_B1) * g
    v = ADAM_B2 * v + (1.0 - ADAM_B2) * _jnp.square(g)
    m_hat = m / (1.0 - ADAM_B1 ** ADAM_STEP)
    v_hat = v / (1.0 - ADAM_B2 ** ADAM_STEP)
    delta = -ADAM_LR * (m_hat / (_jnp.sqrt(v_hat) + ADAM_EPS) + ADAM_WD * w)
    return delta, m, v


def reference(x, mix_norm, mlp_norm, mlp_w1, mlp_w2, lru_w_in, lru_conv_w, lru_conv_b, lru_w_r, lru_b_r, lru_w_i, lru_b_i, lru_lambda, lru_w_out, fox_w_in, fox_b_f, fox_q_gain, fox_k_gain, fox_w_out, loss_target, m_mix_norm, m_mlp_norm, m_mlp_w1, m_mlp_w2, m_lru_w_in, m_lru_conv_w, m_lru_conv_b, m_lru_w_r, m_lru_b_r, m_lru_w_i, m_lru_b_i, m_lru_lambda, m_lru_w_out, m_fox_w_in, m_fox_b_f, m_fox_q_gain, m_fox_k_gain, m_fox_w_out, v_mix_norm, v_mlp_norm, v_mlp_w1, v_mlp_w2, v_lru_w_in, v_lru_conv_w, v_lru_conv_b, v_lru_w_r, v_lru_b_r, v_lru_w_i, v_lru_b_i, v_lru_lambda, v_lru_w_out, v_fox_w_in, v_fox_b_f, v_fox_q_gain, v_fox_k_gain, v_fox_w_out):
    given = dict(x=x, mix_norm=mix_norm, mlp_norm=mlp_norm, mlp_w1=mlp_w1, mlp_w2=mlp_w2, lru_w_in=lru_w_in, lru_conv_w=lru_conv_w, lru_conv_b=lru_conv_b, lru_w_r=lru_w_r, lru_b_r=lru_b_r, lru_w_i=lru_w_i, lru_b_i=lru_b_i, lru_lambda=lru_lambda, lru_w_out=lru_w_out, fox_w_in=fox_w_in, fox_b_f=fox_b_f, fox_q_gain=fox_q_gain, fox_k_gain=fox_k_gain, fox_w_out=fox_w_out, loss_target=loss_target, m_mix_norm=m_mix_norm, m_mlp_norm=m_mlp_norm, m_mlp_w1=m_mlp_w1, m_mlp_w2=m_mlp_w2, m_lru_w_in=m_lru_w_in, m_lru_conv_w=m_lru_conv_w, m_lru_conv_b=m_lru_conv_b, m_lru_w_r=m_lru_w_r, m_lru_b_r=m_lru_b_r, m_lru_w_i=m_lru_w_i, m_lru_b_i=m_lru_b_i, m_lru_lambda=m_lru_lambda, m_lru_w_out=m_lru_w_out, m_fox_w_in=m_fox_w_in, m_fox_b_f=m_fox_b_f, m_fox_q_gain=m_fox_q_gain, m_fox_k_gain=m_fox_k_gain, m_fox_w_out=m_fox_w_out, v_mix_norm=v_mix_norm, v_mlp_norm=v_mlp_norm, v_mlp_w1=v_mlp_w1, v_mlp_w2=v_mlp_w2, v_lru_w_in=v_lru_w_in, v_lru_conv_w=v_lru_conv_w, v_lru_conv_b=v_lru_conv_b, v_lru_w_r=v_lru_w_r, v_lru_b_r=v_lru_b_r, v_lru_w_i=v_lru_w_i, v_lru_b_i=v_lru_b_i, v_lru_lambda=v_lru_lambda, v_lru_w_out=v_lru_w_out, v_fox_w_in=v_fox_w_in, v_fox_b_f=v_fox_b_f, v_fox_q_gain=v_fox_q_gain, v_fox_k_gain=v_fox_k_gain, v_fox_w_out=v_fox_w_out)
    weights = {n: given[n] for n in TWIN_WEIGHTS}
    shared = {n: given[n] for n in SHARED_INPUTS}
    per_example = {n: given[n] for n in ['x']}
    grad_fn = _jax.value_and_grad(_loss, argnums=(0, 1))

    def one_microbatch(ex, loss_target):
        ex = dict(ex)
        diff = ex.pop(TWIN_DIFF_INPUT)
        return grad_fn(weights, diff, {**shared, **ex}, loss_target)

    if N_MICROBATCH == 1:
        loss, (grad_w, grad_x) = one_microbatch(per_example, given["loss_target"])
    else:
        def body(carry, xs):
            loss_sum, grad_sum = carry
            l_k, (gw_k, gx_k) = one_microbatch(xs[0], xs[1])
            with _jax.named_scope("update"):
                return (loss_sum + l_k, _jax.tree.map(_jnp.add, grad_sum, gw_k)), gx_k

        init = (_jnp.zeros((), _jnp.float32), _jax.tree.map(_jnp.zeros_like, weights))
        (loss, grad_w), grad_x = _jax.lax.scan(body, init, (per_example, given["loss_target"]))
    with _jax.named_scope("update"):
        delta_w, new_m, new_v = {}, {}, {}
        for n in TWIN_WEIGHTS:
            delta_w[n], new_m[n], new_v[n] = _adamw(weights[n], grad_w[n], given["m_" + n], given["v_" + n])
    return (loss, grad_x, *[grad_w[n] for n in TWIN_WEIGHTS], *[delta_w[n] for n in TWIN_WEIGHTS],
            *[new_m[n] for n in TWIN_WEIGHTS], *[new_v[n] for n in TWIN_WEIGHTS])
```

```python
import functools

import numpy as np
import jax
import jax.numpy as jnp
from jax import lax
from jax.experimental import pallas as pl
from jax.experimental.pallas import tpu as pltpu

F32 = jnp.float32
BF16 = jnp.bfloat16

HEAD_DIM = 64
LRU_BLOCK_DIM = 64
CONV_WIDTH = 4
LRU_C = 8.0
EPS = 1e-6
NEG_INF = -1e30
ADAM_LR = 0.001
ADAM_B1 = 0.9
ADAM_B2 = 0.999
ADAM_EPS = 1e-08
ADAM_WD = 0.01
ADAM_STEP = 10

N_CHIPS = 4
LANES = 128
SUBLANES = 8
MXU_DIM = 256
VMEM_LIMIT = 52 * 1024 * 1024
MESH = pl.DeviceIdType.MESH
ANY = pl.BlockSpec(memory_space=pl.ANY)


def _pick(n, prefs):
    for p in prefs:
        if p <= n and n % p == 0:
            return p
    return n


def _params(sem=None):
    return pltpu.CompilerParams(dimension_semantics=sem, vmem_limit_bytes=VMEM_LIMIT)


class _View:
    def __init__(self, arr, kind="plain", r0=0, rows=None, shape=None, dtype=None):
        self.arr = arr
        self.kind = kind
        self.r0 = r0
        self.shape = tuple(arr.shape) if arr is not None else tuple(shape)
        self.dtype = arr.dtype if arr is not None else dtype
        self.rows = rows if rows is not None else self.shape[-2]

    def limits(self):
        if self.kind == "plain":
            return 0, 0
        rows = int(np.gcd(self.rows, self.r0))
        return rows, (self.shape[-1] if self.kind == "cs" else 0)

    def spec(self, br, bc, fr, fc):
        if self.kind == "plain":
            return pl.BlockSpec((br, bc), lambda *g: (fr(*g), fc(*g)))
        ncol = self.shape[-1]
        r0b = self.r0 // br
        assert self.r0 % br == 0 and self.rows % br == 0 and ncol % bc == 0, (self.shape, self.r0, br, bc)
        if self.kind == "cs":
            per = ncol // bc
            return pl.BlockSpec((None, br, bc), lambda *g: (fc(*g) // per, r0b + fr(*g), fc(*g) % per))
        per = self.rows // br
        return pl.BlockSpec((None, br, bc), lambda *g: (fr(*g) // per, r0b + fr(*g) % per, fc(*g)))


def _bf(x):
    return x if x.dtype == BF16 else x.astype(BF16)


def _matmul(name, A, B, M, N, K, *, ta=False, tb=False, outs, epilogue, extras=(), tm=None, tn=None, tk=None):
    lim = {"m": [M], "n": [N], "k": [K]}
    for view, (rdim, cdim) in ([(A, "km" if ta else "mk"), (B, "nk" if tb else "kn")]
                               + [(e, "mn") for e in extras] + [(o, "mn") for o in outs]):
        r_lim, c_lim = view.limits()
        lim[rdim].append(r_lim)
        lim[cdim].append(c_lim)
    tm = tm or _pick(int(np.gcd.reduce(lim["m"])), (1024, 512, 256, 128))
    tn = tn or _pick(int(np.gcd.reduce(lim["n"])), (1024, 640, 512, 256, 128))
    tk = tk or _pick(int(np.gcd.reduce(lim["k"])), (1024, 640, 512, 256, 128))
    nk = K // tk
    gi = lambda i, j, k: i
    gj = lambda i, j, k: j
    gk = lambda i, j, k: k
    a_spec = A.spec(tk, tm, gk, gi) if ta else A.spec(tm, tk, gi, gk)
    b_spec = B.spec(tn, tk, gj, gk) if tb else B.spec(tk, tn, gk, gj)
    ca = 0 if ta else 1
    cb = 1 if tb else 0
    ne, no = len(extras), len(outs)
    in_specs = [a_spec, b_spec] + [e.spec(tm, tn, gi, gj) for e in extras]
    operands = [A.arr, B.arr] + [e.arr for e in extras]
    aliases = {}
    for oi, o in enumerate(outs):
        if o.arr is not None:
            aliases[len(operands)] = oi
            in_specs.append(ANY)
            operands.append(o.arr)
    nalias = len(aliases)
    out_specs = [o.spec(tm, tn, gi, gj) for o in outs]
    out_shape = [jax.ShapeDtypeStruct(o.shape, o.dtype) for o in outs]

    def body(*refs):
        a_ref, b_ref = refs[0], refs[1]
        ex = refs[2:2 + ne]
        o_refs = refs[2 + ne + nalias:2 + ne + nalias + no]

        def prod():
            return lax.dot_general(_bf(a_ref[...]), _bf(b_ref[...]), (((ca,), (cb,)), ((), ())),
                                   preferred_element_type=F32)

        def finish(acc):
            res = epilogue(acc, *[e[...] for e in ex])
            for o_ref, r in zip(o_refs, res):
                o_ref[...] = r.astype(o_ref.dtype)

        if nk == 1:
            finish(prod())
        else:
            acc_ref = refs[-1]
            k = pl.program_id(2)

            @pl.when(k == 0)
            def _():
                acc_ref[...] = jnp.zeros_like(acc_ref)

            acc_ref[...] += prod()

            @pl.when(k == nk - 1)
            def _():
                finish(acc_ref[...])

    res = pl.pallas_call(
        body, name=name, grid=(M // tm, N // tn, nk), in_specs=in_specs, out_specs=out_specs, out_shape=out_shape,
        scratch_shapes=[pltpu.VMEM((tm, tn), F32)] if nk > 1 else [],
        input_output_aliases=aliases,
        compiler_params=_params(("parallel", "parallel", "arbitrary")),
    )(*operands)
    return res


def _ep_store(acc):
    return (acc,)


def _ep_resid(acc, res):
    return (res + acc,)


def _ep_relu2(acc):
    zp = jnp.maximum(acc, 0.0)
    return (acc, zp * zp)


def _ep_drelu2(acc, z):
    return (acc * (2.0 * jnp.maximum(z.astype(F32), 0.0)),)


def _fresh(M, N, dtype):
    return _View(None, shape=(M, N), dtype=dtype)


def _rms_fwd(name, x, g, S, D):
    T = _pick(S, (512, 256, 128))

    def body(x_ref, g_ref, h_ref):
        x = x_ref[...]
        r = lax.rsqrt(jnp.mean(x * x, axis=-1, keepdims=True) + EPS)
        h_ref[...] = ((x * r) * g_ref[...]).astype(BF16)

    return pl.pallas_call(
        body, name=name, grid=(S // T,),
        in_specs=[pl.BlockSpec((T, D), lambda i: (i, 0)), pl.BlockSpec((1, D), lambda i: (0, 0))],
        out_specs=pl.BlockSpec((T, D), lambda i: (i, 0)),
        out_shape=jax.ShapeDtypeStruct((S, D), BF16),
        compiler_params=_params(("arbitrary",)),
    )(x, g)


def _rms_bwd(name, dh, x, g, dres, S, D):
    T = _pick(S, (512, 256, 128))

    def body(dh_ref, x_ref, g_ref, dres_ref, dx_ref, dxb_ref, dg_ref):
        @pl.when(pl.program_id(0) == 0)
        def _():
            dg_ref[...] = jnp.zeros_like(dg_ref)

        x = x_ref[...]
        dh = dh_ref[...]
        r = lax.rsqrt(jnp.mean(x * x, axis=-1, keepdims=True) + EPS)
        xhat = x * r
        dg_ref[...] += jnp.sum(dh * xhat, axis=0, keepdims=True)
        dxn = dh * g_ref[...]
        dx = r * (dxn - xhat * jnp.mean(dxn * xhat, axis=-1, keepdims=True))
        tot = dres_ref[...] + dx
        dx_ref[...] = tot
        dxb_ref[...] = tot.astype(BF16)

    row = pl.BlockSpec((T, D), lambda i: (i, 0))
    vec = pl.BlockSpec((1, D), lambda i: (0, 0))
    return pl.pallas_call(
        body, name=name, grid=(S // T,), in_specs=[row, row, vec, row], out_specs=[row, row, vec],
        out_shape=[jax.ShapeDtypeStruct((S, D), F32), jax.ShapeDtypeStruct((S, D), BF16),
                   jax.ShapeDtypeStruct((1, D), F32)],
        compiler_params=_params(("arbitrary",)),
    )(dh, x, g, dres)


def _loss_head(x, tgt, S, D):
    T = _pick(S, (512, 256, 128))

    def body(x_ref, t_ref, loss_ref, d_ref, db_ref):
        @pl.when(pl.program_id(0) == 0)
        def _():
            loss_ref[...] = jnp.zeros_like(loss_ref)

        e = x_ref[...] - t_ref[...]
        loss_ref[...] += 0.5 * jnp.sum(jnp.mean(e * e, axis=-1, keepdims=True), axis=0, keepdims=True)
        d = e * (1.0 / D)
        d_ref[...] = d
        db_ref[...] = d.astype(BF16)

    row = pl.BlockSpec((T, D), lambda i: (i, 0))
    return pl.pallas_call(
        body, name="loss_head", grid=(S // T,), in_specs=[row, row],
        out_specs=[pl.BlockSpec((1, 1), lambda i: (0, 0)), row, row],
        out_shape=[jax.ShapeDtypeStruct((1, 1), F32), jax.ShapeDtypeStruct((S, D), F32),
                   jax.ShapeDtypeStruct((S, D), BF16)],
        compiler_params=_params(("arbitrary",)),
    )(x, tgt)


def _sigmoid(z):
    return 1.0 / (1.0 + jnp.exp(-z))


def _log_sigmoid(z):
    return jnp.minimum(z, 0.0) - jnp.log(1.0 + jnp.exp(-jnp.abs(z)))


_GELU_K = 0.7978845608028654
_GELU_C = 0.044715


def _gelu(x):
    t = jnp.tanh(_GELU_K * (x + _GELU_C * (x * x * x)))
    return 0.5 * x * (1.0 + t)


def _gelu_and_grad(x):
    x2 = x * x
    t = jnp.tanh(_GELU_K * (x + _GELU_C * (x2 * x)))
    g = 0.5 * x * (1.0 + t)
    dg = 0.5 * (1.0 + t) + 0.5 * x * (1.0 - t * t) * (_GELU_K * (1.0 + 3.0 * _GELU_C * x2))
    return g, dg


def _decay_terms(r, ls):
    la = LRU_C * r * ls
    a = jnp.exp(la)
    a2 = jnp.exp(2.0 * la)
    mult = jnp.sqrt(-jnp.tanh(la) * (a2 + 1.0))
    return a, a2, mult


def _lru_fwd(u0, conv_w, conv_b, wr_bd, b_r, wi_bd, b_i, lam, S, D):
    T = _pick(S, (256, 128))
    GT = wr_bd.shape[-1]
    nG = D // GT

    def body(gb_ref, xb_ref, cw_ref, cb_ref, wr_ref, br_ref, wi_ref, bi_ref, lam_ref,
             y_ref, xc_ref, r_ref, i_ref, hs_ref, ext, a_scr, hcar):
        @pl.when(pl.program_id(0) == 0)
        def _():
            ext[0:SUBLANES, :] = jnp.zeros((SUBLANES, D), F32)
            hcar[...] = jnp.zeros_like(hcar)

        xb = xb_ref[...]
        ext[SUBLANES:SUBLANES + T, :] = xb
        xc = cb_ref[...]
        for k in range(CONV_WIDTH):
            xc = xc + ext[pl.ds(SUBLANES - (CONV_WIDTH - 1) + k, T), :] * cw_ref[k:k + 1, :]
        ext[0:SUBLANES, :] = xb[T - SUBLANES:T, :]
        xc_ref[...] = xc
        xcb = xc.astype(BF16)
        for g in range(nG):
            sl = slice(g * GT, (g + 1) * GT)
            zr = jnp.dot(xcb[:, sl], wr_ref[g], preferred_element_type=F32) + br_ref[:, sl]
            zi = jnp.dot(xcb[:, sl], wi_ref[g], preferred_element_type=F32) + bi_ref[:, sl]
            r_ref[:, sl] = _sigmoid(zr)
            i_ref[:, sl] = _sigmoid(zi)
        r = r_ref[...]
        a, _, mult = _decay_terms(r, _log_sigmoid(lam_ref[...]))
        a_scr[...] = a
        hs_ref[...] = mult * (i_ref[...] * xc)

        def step(t, h):
            h = a_scr[pl.ds(t, 1), :] * h + hs_ref[pl.ds(t, 1), :]
            hs_ref[pl.ds(t, 1), :] = h
            return h

        hcar[...] = lax.fori_loop(0, T, step, hcar[...], unroll=8)
        y_ref[...] = (_gelu(gb_ref[...]) * hs_ref[...]).astype(BF16)

    row = pl.BlockSpec((T, D), lambda i: (i, 0))
    vec = pl.BlockSpec((1, D), lambda i: (0, 0))
    bd = pl.BlockSpec((nG, GT, GT), lambda i: (0, 0, 0))
    f32o = jax.ShapeDtypeStruct((S, D), F32)
    return pl.pallas_call(
        body, name="lru_fwd", grid=(S // T,),
        in_specs=[row, pl.BlockSpec((T, D), lambda i: (i, 1)), pl.BlockSpec((CONV_WIDTH, D), lambda i: (0, 0)), vec,
                  bd, vec, bd, vec, vec],
        out_specs=[row, row, row, row, row],
        out_shape=[jax.ShapeDtypeStruct((S, D), BF16), f32o, f32o, f32o, f32o],
        scratch_shapes=[pltpu.VMEM((T + SUBLANES, D), F32), pltpu.VMEM((T, D), F32), pltpu.VMEM((1, D), F32)],
        compiler_params=_params(("arbitrary",)),
    )(u0, u0, conv_w, conv_b, wr_bd, b_r, wi_bd, b_i, lam)


def _lru_bwd(dy, u0, xc, r, ig, hs, conv_w, wr_bd, wi_bd, lam, S, D):
    T = _pick(S, (128,))
    nT = S // T
    GT = wr_bd.shape[-1]
    nG = D // GT
    W = CONV_WIDTH

    def body(dy_ref, gb_ref, xb_ref, xbp_ref, xc_ref, r_ref, i_ref, hs_ref, hsp_ref, cw_ref, wr_ref, wi_ref, lam_ref,
             du_ref, dcw_ref, dcb_ref, dlam_ref, dbr_ref, dbi_ref, dwr_ref, dwi_ref,
             a_scr, dh_scr, exth, extx, extd, dxc_scr, dz_scr, carry):
        step = pl.program_id(0)
        first_tile = step == nT - 1

        @pl.when(step == 0)
        def _():
            for ref in (dcw_ref, dcb_ref, dlam_ref, dbr_ref, dbi_ref, dwr_ref, dwi_ref, carry):
                ref[...] = jnp.zeros_like(ref)
            extd[T:T + SUBLANES, :] = jnp.zeros((SUBLANES, D), F32)

        hs = hs_ref[...]
        dy = dy_ref[...]
        g, dgelu = _gelu_and_grad(gb_ref[...])
        du_ref[:, 0:D] = (dy * hs * dgelu).astype(BF16)
        r = r_ref[...]
        lam = lam_ref[...]
        ls = _log_sigmoid(lam)
        a, a2, mult = _decay_terms(r, ls)
        a_scr[...] = a
        dh_scr[...] = dy * g

        def rstep(j, c):
            t = T - 1 - j
            d = dh_scr[pl.ds(t, 1), :] + c
            dh_scr[pl.ds(t, 1), :] = d
            return a_scr[pl.ds(t, 1), :] * d

        carry[...] = lax.fori_loop(0, T, rstep, carry[...], unroll=8)
        dh = dh_scr[...]
        keep = jnp.where(first_tile, 0.0, 1.0)
        exth[0:SUBLANES, :] = hsp_ref[...] * keep
        exth[SUBLANES:SUBLANES + T, :] = hs
        hprev = exth[pl.ds(SUBLANES - 1, T), :]
        xc = xc_ref[...]
        ig = i_ref[...]
        da = dh * hprev
        dmult = dh * (ig * xc)
        dla = da * a - dmult * (a2 / mult)
        dlam_ref[...] += jnp.sum(dla * r, axis=0, keepdims=True) * (LRU_C * _sigmoid(-lam))
        dzr = (dla * (LRU_C * ls)) * (r * (1.0 - r))
        dzi = (dh * (mult * xc)) * (ig * (1.0 - ig))
        dbr_ref[...] += jnp.sum(dzr, axis=0, keepdims=True)
        dbi_ref[...] += jnp.sum(dzi, axis=0, keepdims=True)
        dxc_scr[...] = dh * (mult * ig)
        xcb = xc.astype(BF16)
        dz_scr[0] = dzr.astype(BF16)
        dz_scr[1] = dzi.astype(BF16)
        nt_dims = (((1,), (1,)), ((), ()))
        tn_dims = (((0,), (0,)), ((), ()))
        for gq in range(nG):
            sl = slice(gq * GT, (gq + 1) * GT)
            zr_g = dz_scr[0, :, sl]
            zi_g = dz_scr[1, :, sl]
            dxc_scr[:, sl] += (lax.dot_general(zr_g, wr_ref[gq], nt_dims, preferred_element_type=F32)
                               + lax.dot_general(zi_g, wi_ref[gq], nt_dims, preferred_element_type=F32))
            dwr_ref[gq] += lax.dot_general(xcb[:, sl], zr_g, tn_dims, preferred_element_type=F32)
            dwi_ref[gq] += lax.dot_general(xcb[:, sl], zi_g, tn_dims, preferred_element_type=F32)
        dxc = dxc_scr[...]
        dcb_ref[...] += jnp.sum(dxc, axis=0, keepdims=True)
        extx[0:SUBLANES, :] = xbp_ref[...] * keep
        extx[SUBLANES:SUBLANES + T, :] = xb_ref[...]
        extd[0:T, :] = dxc
        dxb = jnp.zeros((T, D), F32)
        for k in range(W):
            dxb = dxb + extd[pl.ds(W - 1 - k, T), :] * cw_ref[k:k + 1, :]
            dcw_ref[k:k + 1, :] += jnp.sum(dxc * extx[pl.ds(SUBLANES - (W - 1) + k, T), :], axis=0, keepdims=True)
        extd[T:T + SUBLANES, :] = dxc[0:SUBLANES, :]
        du_ref[:, D:2 * D] = dxb.astype(BF16)

    rev = lambda i: nT - 1 - i
    tpb = T // SUBLANES
    prev8 = lambda i: jnp.maximum(rev(i) * tpb - 1, 0)
    row = pl.BlockSpec((T, D), lambda i: (rev(i), 0))
    vec = pl.BlockSpec((1, D), lambda i: (0, 0))
    bd = pl.BlockSpec((nG, GT, GT), lambda i: (0, 0, 0))
    vec_o = jax.ShapeDtypeStruct((1, D), F32)
    bd_o = jax.ShapeDtypeStruct((nG, GT, GT), F32)
    return pl.pallas_call(
        body, name="lru_bwd", grid=(nT,),
        in_specs=[row, row, pl.BlockSpec((T, D), lambda i: (rev(i), 1)),
                  pl.BlockSpec((SUBLANES, D), lambda i: (prev8(i), 1)),
                  row, row, row, row, pl.BlockSpec((SUBLANES, D), lambda i: (prev8(i), 0)),
                  pl.BlockSpec((W, D), lambda i: (0, 0)), bd, bd, vec],
        out_specs=[pl.BlockSpec((T, 2 * D), lambda i: (rev(i), 0)), pl.BlockSpec((W, D), lambda i: (0, 0)),
                   vec, vec, vec, vec, bd, bd],
        out_shape=[jax.ShapeDtypeStruct((S, 2 * D), BF16), jax.ShapeDtypeStruct((W, D), F32),
                   vec_o, vec_o, vec_o, vec_o, bd_o, bd_o],
        scratch_shapes=[pltpu.VMEM((T, D), F32), pltpu.VMEM((T, D), F32), pltpu.VMEM((T + SUBLANES, D), F32),
                        pltpu.VMEM((T + SUBLANES, D), F32), pltpu.VMEM((T + SUBLANES, D), F32),
                        pltpu.VMEM((T, D), F32), pltpu.VMEM((2, T, D), BF16), pltpu.VMEM((1, D), F32)],
        compiler_params=_params(("arbitrary",)),
    )(dy, u0, u0, u0, xc, r, ig, hs, hs, conv_w, wr_bd, wi_bd, lam)


def _head_mean(v, bm_ref):
    hi = v.astype(BF16)
    lo = (v - hi.astype(F32)).astype(BF16)
    bm = bm_ref[...]
    return jnp.dot(hi, bm, preferred_element_type=F32) + jnp.dot(lo, bm, preferred_element_type=F32)


def _head_mean_matrix():
    blk = np.arange(LANES) // HEAD_DIM
    return jnp.asarray((blk[:, None] == blk[None, :]).astype(np.float32) / HEAD_DIM, BF16)


def _fox_pre(u, b_f, qg, kg, S, D, H):
    T = _pick(S, (256, 128))
    scale = HEAD_DIM ** -0.5

    def body(q_ref, k_ref, v_ref, f_ref, bf_ref, qg_ref, kg_ref, bm_ref, qn_ref, kn_ref, vb_ref, c_ref, ccar):
        @pl.when(pl.program_id(0) == 0)
        def _():
            ccar[...] = jnp.zeros_like(ccar)

        for src, gain, dst, mul in ((q_ref, qg_ref, qn_ref, scale), (k_ref, kg_ref, kn_ref, 1.0)):
            for g in range(D // LANES):
                sl = slice(g * LANES, (g + 1) * LANES)
                x = src[:, sl]
                rs = lax.rsqrt(_head_mean(x * x, bm_ref) + EPS)
                dst[:, sl] = (((x * rs) * gain[:, sl]) * mul).astype(BF16)
        vb_ref[...] = v_ref[...].astype(BF16)
        c_ref[...] = _log_sigmoid(f_ref[...] + bf_ref[...])

        def step(t, c):
            c = c + c_ref[pl.ds(t, 1), :]
            c_ref[pl.ds(t, 1), :] = c
            return c

        ccar[...] = lax.fori_loop(0, T, step, ccar[...], unroll=8)

    nD = D // LANES
    col = lambda j: pl.BlockSpec((T, D), lambda i: (i, j))
    row = pl.BlockSpec((T, D), lambda i: (i, 0))
    vec = pl.BlockSpec((1, D), lambda i: (0, 0))
    lane_row = pl.BlockSpec((T, LANES), lambda i: (i, 0))
    return pl.pallas_call(
        body, name="fox_pre", grid=(S // T,),
        in_specs=[col(0), col(1), col(2), pl.BlockSpec((T, LANES), lambda i: (i, 3 * nD)),
                  pl.BlockSpec((1, LANES), lambda i: (0, 0)), vec, vec, pl.BlockSpec((LANES, LANES), lambda i: (0, 0))],
        out_specs=[row, row, row, lane_row],
        out_shape=[jax.ShapeDtypeStruct((S, D), BF16)] * 3 + [jax.ShapeDtypeStruct((S, LANES), F32)],
        scratch_shapes=[pltpu.VMEM((1, LANES), F32)],
        compiler_params=_params(("arbitrary",)),
    )(u, u, u, u, b_f, qg, kg, _head_mean_matrix())


def _fox_pre_bwd(u, dqn, dkn, dv, dcq, dck, b_f, qg, kg, S, D, H):
    T = _pick(S, (256, 128))
    nT = S // T
    scale = HEAD_DIM ** -0.5
    nD = D // LANES

    def body(q_ref, k_ref, f_ref, dqn_ref, dkn_ref, dv_ref, dcq_ref, dck_ref, bf_ref, qg_ref, kg_ref, bm_ref,
             du_ref, dbf_ref, dqg_ref, dkg_ref, gacc, fcar, dlf):
        step = pl.program_id(0)

        @pl.when(step == 0)
        def _():
            gacc[...] = jnp.zeros_like(gacc)
            fcar[...] = jnp.zeros_like(fcar)
            dbf_ref[...] = jnp.zeros_like(dbf_ref)

        for idx, (src, dsrc, gain, mul) in enumerate(((q_ref, dqn_ref, qg_ref, scale), (k_ref, dkn_ref, kg_ref, 1.0))):
            for g in range(nD):
                sl = slice(g * LANES, (g + 1) * LANES)
                x = src[:, sl]
                rs = lax.rsqrt(_head_mean(x * x, bm_ref) + EPS)
                xhat = x * rs
                dn = dsrc[:, sl] * mul
                gacc[idx:idx + 1, sl] += jnp.sum(dn * xhat, axis=0, keepdims=True)
                dxh = dn * gain[:, sl]
                dx = rs * (dxh - xhat * _head_mean(dxh * xhat, bm_ref))
                du_ref[:, idx * D + g * LANES: idx * D + (g + 1) * LANES] = dx.astype(BF16)
        du_ref[:, 2 * D:3 * D] = dv_ref[...].astype(BF16)

        dlf[...] = dcq_ref[...] - dck_ref[...]

        def rstep(j, c):
            t = T - 1 - j
            c = c + dlf[pl.ds(t, 1), :]
            dlf[pl.ds(t, 1), :] = c
            return c

        fcar[...] = lax.fori_loop(0, T, rstep, fcar[...], unroll=8)
        lane = lax.broadcasted_iota(jnp.int32, (T, LANES), 1)
        dfl = jnp.where(lane < H, dlf[...] * _sigmoid(-(f_ref[...] + bf_ref[...])), 0.0)
        dbf_ref[...] += jnp.sum(dfl, axis=0, keepdims=True)
        du_ref[:, 3 * D:3 * D + LANES] = dfl.astype(BF16)

        @pl.when(step == nT - 1)
        def _():
            for idx, ref in enumerate((dqg_ref, dkg_ref)):
                tot = jnp.zeros((1, HEAD_DIM), F32)
                for h in range(D // HEAD_DIM):
                    tot = tot + gacc[idx:idx + 1, h * HEAD_DIM:(h + 1) * HEAD_DIM]
                ref[...] = tot

    rev = lambda i: nT - 1 - i
    col = lambda j: pl.BlockSpec((T, D), lambda i: (rev(i), j))
    row = pl.BlockSpec((T, D), lambda i: (rev(i), 0))
    vec = pl.BlockSpec((1, D), lambda i: (0, 0))
    lane_row = pl.BlockSpec((T, LANES), lambda i: (rev(i), 0))
    lane_vec = pl.BlockSpec((1, LANES), lambda i: (0, 0))
    head_vec = pl.BlockSpec((1, HEAD_DIM), lambda i: (0, 0))
    return pl.pallas_call(
        body, name="fox_pre_bwd", grid=(nT,),
        in_specs=[col(0), col(1), pl.BlockSpec((T, LANES), lambda i: (rev(i), 3 * nD)), row, row, row, lane_row,
                  lane_row, lane_vec, vec, vec, pl.BlockSpec((LANES, LANES), lambda i: (0, 0))],
        out_specs=[pl.BlockSpec((T, 3 * D + LANES), lambda i: (rev(i), 0)), lane_vec, head_vec, head_vec],
        out_shape=[jax.ShapeDtypeStruct((S, 3 * D + LANES), BF16), jax.ShapeDtypeStruct((1, LANES), F32),
                   jax.ShapeDtypeStruct((1, HEAD_DIM), F32), jax.ShapeDtypeStruct((1, HEAD_DIM), F32)],
        scratch_shapes=[pltpu.VMEM((2, D), F32), pltpu.VMEM((1, LANES), F32), pltpu.VMEM((T, LANES), F32)],
        compiler_params=_params(("arbitrary",)),
    )(u, u, u, dqn, dkn, dv, dcq, dck, b_f, qg, kg, _head_mean_matrix())


def _attn_tables(nq, q_major):
    if q_major:
        steps = [(qi, ki) for qi in range(nq) for ki in range(qi + 1)]
    else:
        steps = [(qi, ki) for ki in range(nq) for qi in range(ki, nq)]
    return (jnp.asarray(np.array([s[0] for s in steps], np.int32)),
            jnp.asarray(np.array([s[1] for s in steps], np.int32)), len(steps))


def _logits(q_ref, k_ref, c_ref, ct_ref, e, h, qi, ki, tq):
    hs = slice(e * HEAD_DIM, (e + 1) * HEAD_DIM)
    s = lax.dot_general(q_ref[:, hs], k_ref[:, hs], (((1,), (1,)), ((), ())), preferred_element_type=F32)
    lane = lax.broadcasted_iota(jnp.int32, (tq, LANES), 1)
    cq = jnp.sum(jnp.where(lane == h, c_ref[...], 0.0), axis=1, keepdims=True)
    s = (s + cq) - ct_ref[pl.ds(h, 1), :]
    rows = qi * tq + lax.broadcasted_iota(jnp.int32, (tq, tq), 0)
    cols = ki * tq + lax.broadcasted_iota(jnp.int32, (tq, tq), 1)
    return jnp.where(cols <= rows, s, NEG_INF)


def _attn_specs(tq, with_bwd):
    qrow = pl.BlockSpec((tq, LANES), lambda p, s, qt, kt: (qt[s], p))
    krow = pl.BlockSpec((tq, LANES), lambda p, s, qt, kt: (kt[s], p))
    c_q = pl.BlockSpec((tq, LANES), lambda p, s, qt, kt: (qt[s], 0))
    c_k = pl.BlockSpec((LANES, tq), lambda p, s, qt, kt: (0, kt[s]))
    lse = pl.BlockSpec((2, tq, LANES), lambda p, s, qt, kt: (p, qt[s], 0))
    specs = [qrow, krow, krow, c_q, c_k]
    if with_bwd:
        specs += [qrow, qrow, lse]
    return specs, qrow, krow, lse


def _attn_fwd(qn, kn, vb, c, ct, S, D, tq):
    nP = D // LANES
    qt, kt, nsteps = _attn_tables(S // tq, True)
    specs, qrow, _, lse_spec = _attn_specs(tq, False)

    def body(qt_ref, kt_ref, q_ref, k_ref, v_ref, c_ref, ct_ref, o_ref, o32_ref, lse_ref, m_scr, l_scr, acc_scr):
        p = pl.program_id(0)
        s_id = pl.program_id(1)
        qi, ki = qt_ref[s_id], kt_ref[s_id]

        @pl.when(ki == 0)
        def _():
            m_scr[...] = jnp.full_like(m_scr, NEG_INF)
            l_scr[...] = jnp.zeros_like(l_scr)
            acc_scr[...] = jnp.zeros_like(acc_scr)

        for e in range(2):
            s = _logits(q_ref, k_ref, c_ref, ct_ref, e, 2 * p + e, qi, ki, tq)
            m_prev = m_scr[e]
            m_new = jnp.maximum(m_prev, jnp.max(s, axis=1, keepdims=True))
            alpha = jnp.exp(m_prev - m_new)
            pe = jnp.exp(s - m_new)
            l_scr[e] = alpha * l_scr[e] + jnp.sum(pe, axis=1, keepdims=True)
            acc_scr[e] = alpha * acc_scr[e] + jnp.dot(pe.astype(BF16), v_ref[:, e * HEAD_DIM:(e + 1) * HEAD_DIM],
                                                      preferred_element_type=F32)
            m_scr[e] = m_new

        @pl.when(ki == qi)
        def _():
            for e in range(2):
                o_e = acc_scr[e] / l_scr[e]
                o_ref[:, e * HEAD_DIM:(e + 1) * HEAD_DIM] = o_e.astype(BF16)
                o32_ref[:, e * HEAD_DIM:(e + 1) * HEAD_DIM] = o_e
                lse_ref[e] = jnp.broadcast_to(m_scr[e] + jnp.log(l_scr[e]), (tq, LANES))

    grid_spec = pltpu.PrefetchScalarGridSpec(
        num_scalar_prefetch=2, grid=(nP, nsteps), in_specs=specs, out_specs=[qrow, qrow, lse_spec],
        scratch_shapes=[pltpu.VMEM((2, tq, 1), F32), pltpu.VMEM((2, tq, 1), F32), pltpu.VMEM((2, tq, HEAD_DIM), F32)])
    return pl.pallas_call(
        body, name="attn_fwd", grid_spec=grid_spec,
        out_shape=[jax.ShapeDtypeStruct((S, D), BF16), jax.ShapeDtypeStruct((S, D), F32),
                   jax.ShapeDtypeStruct((2 * nP, S, LANES), F32)],
        compiler_params=_params(("arbitrary", "arbitrary")),
    )(qt, kt, qn, kn, vb, c, ct)


def _attn_bwd_dq(qn, kn, vb, c, ct, o, do, lse, S, D, tq):
    nP = D // LANES
    qt, kt, nsteps = _attn_tables(S // tq, True)
    specs, qrow, _, _ = _attn_specs(tq, True)

    def body(qt_ref, kt_ref, q_ref, k_ref, v_ref, c_ref, ct_ref, o_ref, do_ref, lse_ref, dq_ref, dcq_ref,
             acc_scr, dl_scr, rs_scr):
        p = pl.program_id(0)
        s_id = pl.program_id(1)
        qi, ki = qt_ref[s_id], kt_ref[s_id]

        @pl.when(ki == 0)
        def _():
            acc_scr[...] = jnp.zeros_like(acc_scr)
            rs_scr[...] = jnp.zeros_like(rs_scr)
            for e in range(2):
                hs = slice(e * HEAD_DIM, (e + 1) * HEAD_DIM)
                dl_scr[e] = jnp.sum(do_ref[:, hs].astype(F32) * o_ref[:, hs].astype(F32), axis=1, keepdims=True)

        for e in range(2):
            hs = slice(e * HEAD_DIM, (e + 1) * HEAD_DIM)
            s = _logits(q_ref, k_ref, c_ref, ct_ref, e, 2 * p + e, qi, ki, tq)
            pe = jnp.exp(s - lse_ref[e][:, 0:1])
            dp = lax.dot_general(do_ref[:, hs], v_ref[:, hs], (((1,), (1,)), ((), ())), preferred_element_type=F32)
            ds = pe * (dp - dl_scr[e])
            rs_scr[e] += jnp.sum(ds, axis=1, keepdims=True)
            acc_scr[e] += jnp.dot(ds.astype(BF16), k_ref[:, hs], preferred_element_type=F32)

        @pl.when(ki == qi)
        def _():
            for e in range(2):
                dq_ref[:, e * HEAD_DIM:(e + 1) * HEAD_DIM] = acc_scr[e]
            lane = lax.broadcasted_iota(jnp.int32, (tq, LANES), 1)
            dcq_ref[...] = jnp.where(lane == 0, rs_scr[0], jnp.where(lane == 1, rs_scr[1], 0.0))

    grid_spec = pltpu.PrefetchScalarGridSpec(
        num_scalar_prefetch=2, grid=(nP, nsteps), in_specs=specs,
        out_specs=[qrow, pl.BlockSpec((None, tq, LANES), lambda p, s, qt, kt: (p, qt[s], 0))],
        scratch_shapes=[pltpu.VMEM((2, tq, HEAD_DIM), F32), pltpu.VMEM((2, tq, 1), F32), pltpu.VMEM((2, tq, 1), F32)])
    return pl.pallas_call(
        body, name="attn_bwd_dq", grid_spec=grid_spec,
        out_shape=[jax.ShapeDtypeStruct((S, D), F32), jax.ShapeDtypeStruct((nP, S, LANES), F32)],
        compiler_params=_params(("arbitrary", "arbitrary")),
    )(qt, kt, qn, kn, vb, c, ct, o, do, lse)


def _attn_bwd_dkv(qn, kn, vb, c, ct, o, do, lse, S, D, tq):
    nP = D // LANES
    nq = S // tq
    qt, kt, nsteps = _attn_tables(nq, False)
    specs, _, krow, _ = _attn_specs(tq, True)
    tn_dims = (((0,), (0,)), ((), ()))

    def body(qt_ref, kt_ref, q_ref, k_ref, v_ref, c_ref, ct_ref, o_ref, do_ref, lse_ref, dk_ref, dv_ref, dck_ref,
             dk_scr, dv_scr, dck_scr):
        p = pl.program_id(0)
        s_id = pl.program_id(1)
        qi, ki = qt_ref[s_id], kt_ref[s_id]

        @pl.when(qi == ki)
        def _():
            dk_scr[...] = jnp.zeros_like(dk_scr)
            dv_scr[...] = jnp.zeros_like(dv_scr)
            dck_scr[...] = jnp.zeros_like(dck_scr)

        for e in range(2):
            hs = slice(e * HEAD_DIM, (e + 1) * HEAD_DIM)
            s = _logits(q_ref, k_ref, c_ref, ct_ref, e, 2 * p + e, qi, ki, tq)
            pe = jnp.exp(s - lse_ref[e][:, 0:1])
            do_e = do_ref[:, hs]
            dv_scr[e] += lax.dot_general(pe.astype(BF16), do_e, tn_dims, preferred_element_type=F32)
            dp = lax.dot_general(do_e, v_ref[:, hs], (((1,), (1,)), ((), ())), preferred_element_type=F32)
            delta = jnp.sum(do_e.astype(F32) * o_ref[:, hs].astype(F32), axis=1, keepdims=True)
            ds = pe * (dp - delta)
            dk_scr[e] += lax.dot_general(ds.astype(BF16), q_ref[:, hs], tn_dims, preferred_element_type=F32)
            dck_scr[e:e + 1, :] += jnp.sum(ds, axis=0, keepdims=True)

        @pl.when(qi == nq - 1)
        def _():
            for e in range(2):
                hs = slice(e * HEAD_DIM, (e + 1) * HEAD_DIM)
                dk_ref[:, hs] = dk_scr[e]
                dv_ref[:, hs] = dv_scr[e].astype(BF16)
            dck_ref[...] = dck_scr[...]

    grid_spec = pltpu.PrefetchScalarGridSpec(
        num_scalar_prefetch=2, grid=(nP, nsteps), in_specs=specs,
        out_specs=[krow, krow, pl.BlockSpec((None, 2, tq), lambda p, s, qt, kt: (p, 0, kt[s]))],
        scratch_shapes=[pltpu.VMEM((2, tq, HEAD_DIM), F32), pltpu.VMEM((2, tq, HEAD_DIM), F32), pltpu.VMEM((2, tq), F32)])
    return pl.pallas_call(
        body, name="attn_bwd_dkv", grid_spec=grid_spec,
        out_shape=[jax.ShapeDtypeStruct((S, D), F32), jax.ShapeDtypeStruct((S, D), BF16),
                   jax.ShapeDtypeStruct((nP, 2, S), F32)],
        compiler_params=_params(("arbitrary", "arbitrary")),
    )(qt, kt, qn, kn, vb, c, ct, o, do, lse)


def _block_diag_tiles(w):
    n = w.shape[0]
    per = min(MXU_DIM, n * LRU_BLOCK_DIM) // LRU_BLOCK_DIM
    eye = jnp.eye(per, dtype=w.dtype)
    w5 = w.reshape(n // per, per, LRU_BLOCK_DIM, 1, LRU_BLOCK_DIM) * eye[None, :, None, :, None]
    return w5.reshape(n // per, per * LRU_BLOCK_DIM, per * LRU_BLOCK_DIM).astype(BF16)


def _block_diag_extract(t, n):
    per = t.shape[-1] // LRU_BLOCK_DIM
    eye = jnp.eye(per, dtype=t.dtype)
    t5 = t.reshape(n // per, per, LRU_BLOCK_DIM, per, LRU_BLOCK_DIM) * eye[None, :, None, :, None]
    return t5.sum(axis=3).reshape(n, LRU_BLOCK_DIM, LRU_BLOCK_DIM)


def _local_step(x, tgt, small, wv, grad_view):
    S, D = x.shape
    F = 4 * D
    H = D // HEAD_DIM
    nblk = D // LRU_BLOCK_DIM
    NU = 3 * D + LANES
    tq = _pick(S, (512, 256, 128))
    vec = lambda a: a.reshape(1, -1).astype(F32)
    mix_g, mlp_g = small["mix_norm"], small["mlp_norm"]
    conv_w, conv_b = small["conv_w"], vec(small["lru_conv_b"])
    wr_bd, wi_bd = _block_diag_tiles(small["lru_w_r"][0]), _block_diag_tiles(small["lru_w_i"][0])
    b_r, b_i, lam = vec(small["lru_b_r"]), vec(small["lru_b_i"]), vec(small["lru_lambda"])
    b_f = jnp.pad(vec(small["fox_b_f"]), ((0, 0), (0, LANES - H)))
    qg, kg = jnp.tile(vec(small["fox_q_gain"]), (1, H)), jnp.tile(vec(small["fox_k_gain"]), (1, H))
    X = lambda a: _View(a)
    grads = {}
    gout = functools.partial(grad_view, grads)

    def mlp_fwd(l, xin):
        hm = _rms_fwd(f"mlp{l}_norm", xin, mlp_g[l:l + 1], S, D)
        z, act = _matmul(f"mlp{l}_up", X(hm), wv[f"w1_{l}"], S, F, D, outs=[_fresh(S, F, BF16), _fresh(S, F, BF16)],
                         epilogue=_ep_relu2)
        (xout,) = _matmul(f"mlp{l}_down", X(act), wv[f"w2_{l}"], S, D, F, outs=[_fresh(S, D, F32)],
                          epilogue=_ep_resid, extras=[X(xin)])
        return hm, z, act, xout

    def mlp_bwd(l, xin, hm, z, act, d, db):
        (dz,) = _matmul(f"mlp{l}_dact", X(db), wv[f"w2_{l}"], S, F, D, tb=True, outs=[_fresh(S, F, BF16)],
                        epilogue=_ep_drelu2, extras=[X(z)])
        (grads[f"w2_{l}"],) = _matmul(f"mlp{l}_dw2", X(act), X(db), F, D, S, ta=True, outs=[gout(f"w2_{l}")],
                                      epilogue=_ep_store)
        (grads[f"w1_{l}"],) = _matmul(f"mlp{l}_dw1", X(hm), X(dz), D, F, S, ta=True, outs=[gout(f"w1_{l}")],
                                      epilogue=_ep_store)
        (dhm,) = _matmul(f"mlp{l}_dhm", X(dz), wv[f"w1_{l}"], S, D, F, tb=True, outs=[_fresh(S, D, F32)],
                         epilogue=_ep_store)
        return _rms_bwd(f"mlp{l}_norm_bwd", dhm, xin, mlp_g[l:l + 1], d, S, D)

    h0 = _rms_fwd("mix0_norm", x, mix_g[0:1], S, D)
    (u0,) = _matmul("lru_in", X(h0), wv["lru_in"], S, 2 * D, D, outs=[_fresh(S, 2 * D, F32)], epilogue=_ep_store)
    y, xc, r, ig, hs = _lru_fwd(u0, conv_w, conv_b, wr_bd, b_r, wi_bd, b_i, lam, S, D)
    (x1,) = _matmul("lru_out", X(y), wv["lru_out"], S, D, D, outs=[_fresh(S, D, F32)], epilogue=_ep_resid,
                    extras=[X(x)])
    hm0, z0, act0, x2 = mlp_fwd(0, x1)
    h1 = _rms_fwd("mix1_norm", x2, mix_g[1:2], S, D)
    (u1,) = _matmul("fox_in", X(h1), wv["fox_in"], S, NU, D, outs=[_fresh(S, NU, F32)], epilogue=_ep_store)
    qn, kn, vb, c = _fox_pre(u1, b_f, qg, kg, S, D, H)
    ct = c.T
    o, o32, lse = _attn_fwd(qn, kn, vb, c, ct, S, D, tq)
    (x3,) = _matmul("fox_out", X(o), wv["fox_out"], S, D, D, outs=[_fresh(S, D, F32)], epilogue=_ep_resid,
                    extras=[X(x2)])
    hm1, z1, act1, x4 = mlp_fwd(1, x3)
    loss, d4, d4b = _loss_head(x4, tgt, S, D)

    d3, d3b, dg_mlp1 = mlp_bwd(1, x3, hm1, z1, act1, d4, d4b)
    (do,) = _matmul("fox_dout", X(d3b), wv["fox_out"], S, D, D, tb=True, outs=[_fresh(S, D, BF16)], epilogue=_ep_store)
    (grads["fox_out"],) = _matmul("fox_dwout", X(o), X(d3b), D, D, S, ta=True, outs=[gout("fox_out")],
                                  epilogue=_ep_store)
    dqn, dcq = _attn_bwd_dq(qn, kn, vb, c, ct, o32, do, lse, S, D, tq)
    dkn, dv, dck = _attn_bwd_dkv(qn, kn, vb, c, ct, o32, do, lse, S, D, tq)
    dcq_col = jnp.pad(jnp.transpose(dcq[:, :, 0:2], (1, 0, 2)).reshape(S, H), ((0, 0), (0, LANES - H)))
    dck_col = jnp.pad(dck.reshape(H, S).T, ((0, 0), (0, LANES - H)))
    du1, dbf, dqg, dkg = _fox_pre_bwd(u1, dqn, dkn, dv, dcq_col, dck_col, b_f, qg, kg, S, D, H)
    (grads["fox_in"],) = _matmul("fox_dwin", X(h1), X(du1), D, NU, S, ta=True, outs=[gout("fox_in")],
                                 epilogue=_ep_store)
    (dh1,) = _matmul("fox_dh", X(du1), wv["fox_in"], S, D, NU, tb=True, outs=[_fresh(S, D, F32)], epilogue=_ep_store)
    d2, d2b, dg_mix1 = _rms_bwd("mix1_norm_bwd", dh1, x2, mix_g[1:2], d3, S, D)
    d1, d1b, dg_mlp0 = mlp_bwd(0, x1, hm0, z0, act0, d2, d2b)
    (dy,) = _matmul("lru_dout", X(d1b), wv["lru_out"], S, D, D, tb=True, outs=[_fresh(S, D, F32)], epilogue=_ep_store)
    (grads["lru_out"],) = _matmul("lru_dwout", X(y), X(d1b), D, D, S, ta=True, outs=[gout("lru_out")],
                                  epilogue=_ep_store)
    du0, dcw, dcb, dlam, dbr, dbi, dwr, dwi = _lru_bwd(dy, u0, xc, r, ig, hs, conv_w, wr_bd, wi_bd, lam, S, D)
    (grads["lru_in"],) = _matmul("lru_dwin", X(h0), X(du0), D, 2 * D, S, ta=True, outs=[gout("lru_in")],
                                 epilogue=_ep_store)
    (dh0,) = _matmul("lru_dh", X(du0), wv["lru_in"], S, D, 2 * D, tb=True, outs=[_fresh(S, D, F32)], epilogue=_ep_store)
    gx, _, dg_mix0 = _rms_bwd("mix0_norm_bwd", dh0, x, mix_g[0:1], d1, S, D)

    grads.update(
        mix_norm=jnp.concatenate([dg_mix0, dg_mix1], axis=0), mlp_norm=jnp.concatenate([dg_mlp0, dg_mlp1], axis=0),
        conv_w=dcw, lru_conv_b=dcb, lru_w_r=_block_diag_extract(dwr, nblk)[None], lru_b_r=dbr.reshape(1, nblk, -1),
        lru_w_i=_block_diag_extract(dwi, nblk)[None], lru_b_i=dbi.reshape(1, nblk, -1), lru_lambda=dlam,
        fox_b_f=dbf[:, :H], fox_q_gain=dqg, fox_k_gain=dkg)
    return loss, gx, grads


def _place():
    x, y, c = lax.axis_index("x"), lax.axis_index("y"), lax.axis_index("c")
    chips = [(1 - x, y), (x, 1 - y), (1 - x, 1 - y)]
    return x, y, c, 2 * x + y, chips


def _hbm_call(body, name, arrays, out_shape, n_dma_sems):
    return pl.pallas_call(
        body, name=name, in_specs=[ANY] * len(arrays), out_specs=[ANY] * len(out_shape), out_shape=out_shape,
        scratch_shapes=[pltpu.SemaphoreType.DMA((k,)) for k in n_dma_sems],
        compiler_params=pltpu.CompilerParams(has_side_effects=True),
    )(*arrays)


def _all_gather(name, shards):
    n = len(shards)

    def body(*refs):
        ins, outs = refs[:n], refs[n:2 * n]
        send, recv, fsend, frecv, lsem = refs[2 * n:]
        x, y, c, s, chips = _place()
        sibling = (x, y, 1 - c)

        def rows(a, chip_idx, which):
            hr = ins[a].shape[0] // 2
            return outs[a].at[chip_idx, pl.ds(which * hr, hr)]

        def ici(a, j, src, dst, to):
            return pltpu.make_async_remote_copy(src_ref=src, dst_ref=dst, send_sem=send.at[3 * a + j],
                                                recv_sem=recv.at[3 * a + j], device_id=to, device_id_type=MESH)

        def d2d(a, j, src, dst):
            return pltpu.make_async_remote_copy(src_ref=src, dst_ref=dst, send_sem=fsend.at[3 * a + j],
                                                recv_sem=frecv.at[3 * a + j], device_id=sibling, device_id_type=MESH)

        started = []
        for a in range(n):
            local = pltpu.make_async_copy(ins[a], outs[a].at[s], lsem.at[a])
            local.start()
            started.append(local)
        for a in range(n):
            hr = ins[a].shape[0] // 2
            for j, chip in enumerate(chips):
                cp = ici(a, j, ins[a].at[pl.ds(c * hr, hr)], rows(a, s, c), (*chip, c))
                cp.start()
                started.append(cp)
        for a in range(n):
            for j, chip in enumerate(chips):
                got = rows(a, 2 * chip[0] + chip[1], c)
                ici(a, j, got, got, (*chip, c)).wait_recv()
                fw = d2d(a, j, got, got)
                fw.start()
                started.append(fw)
        for a in range(n):
            for j, chip in enumerate(chips):
                theirs = rows(a, 2 * chip[0] + chip[1], 1 - c)
                d2d(a, j, theirs, theirs).wait_recv()
        for cp in started[:n]:
            cp.wait()
        for cp in started[n:]:
            cp.wait_send()

    out_shape = [jax.ShapeDtypeStruct((N_CHIPS,) + tuple(a.shape), a.dtype) for a in shards]
    return _hbm_call(body, name, shards, out_shape, (3 * n, 3 * n, 3 * n, 3 * n, n))


def _pair_swap(name, arrs):
    n = len(arrs)

    def body(*refs):
        ins, outs = refs[:n], refs[n:2 * n]
        send, recv = refs[2 * n:]
        x, y, c, _, _ = _place()
        cps = []
        for a in range(n):
            hr = ins[a].shape[1] // 2
            cp = pltpu.make_async_remote_copy(
                src_ref=ins[a].at[:, pl.ds((1 - c) * hr, hr)], dst_ref=outs[a], send_sem=send.at[a],
                recv_sem=recv.at[a], device_id=(x, y, 1 - c), device_id_type=MESH)
            cp.start()
            cps.append(cp)
        for cp in cps:
            cp.wait()

    out_shape = [jax.ShapeDtypeStruct((a.shape[0], a.shape[1] // 2, a.shape[2]), a.dtype) for a in arrs]
    return _hbm_call(body, name, arrs, out_shape, (n, n))


def _chip_scatter(name, parts):
    n = len(parts)

    def body(*refs):
        ins, outs = refs[:n], refs[n:2 * n]
        send, recv, lsem = refs[2 * n:]
        x, y, c, s, chips = _place()
        cps = []
        for a in range(n):
            local = pltpu.make_async_copy(ins[a].at[s], outs[a].at[s], lsem.at[a])
            local.start()
            cps.append(local)
        for a in range(n):
            for j, chip in enumerate(chips):
                t = 2 * chip[0] + chip[1]
                cp = pltpu.make_async_remote_copy(
                    src_ref=ins[a].at[t], dst_ref=outs[a].at[s], send_sem=send.at[3 * a + j],
                    recv_sem=recv.at[3 * a + j], device_id=(*chip, c), device_id_type=MESH)
                cp.start()
                cps.append(cp)
        for a in range(n):
            for j, chip in enumerate(chips):
                t = 2 * chip[0] + chip[1]
                pltpu.make_async_remote_copy(
                    src_ref=ins[a].at[t], dst_ref=outs[a].at[t], send_sem=send.at[3 * a + j],
                    recv_sem=recv.at[3 * a + j], device_id=(*chip, c), device_id_type=MESH).wait_recv()
        for cp in cps[:n]:
            cp.wait()
        for cp in cps[n:]:
            cp.wait_send()

    out_shape = [jax.ShapeDtypeStruct(a.shape, a.dtype) for a in parts]
    return _hbm_call(body, name, parts, out_shape, (3 * n, 3 * n, n))


def _pair_gather(name, halves):
    n = len(halves)

    def body(*refs):
        ins, outs = refs[:n], refs[n:2 * n]
        send, recv, lsem = refs[2 * n:]
        x, y, c, _, _ = _place()
        cps = []
        for a in range(n):
            hr = ins[a].shape[0]
            local = pltpu.make_async_copy(ins[a], outs[a].at[pl.ds(c * hr, hr)], lsem.at[a])
            local.start()
            cp = pltpu.make_async_remote_copy(
                src_ref=ins[a], dst_ref=outs[a].at[pl.ds(c * hr, hr)], send_sem=send.at[a], recv_sem=recv.at[a],
                device_id=(x, y, 1 - c), device_id_type=MESH)
            cp.start()
            cps.append((local, cp, hr))
        for a, (local, cp, hr) in enumerate(cps):
            local.wait()
            cp.wait_send()
            theirs = outs[a].at[pl.ds((1 - c) * hr, hr)]
            pltpu.make_async_remote_copy(src_ref=theirs, dst_ref=theirs, send_sem=send.at[a], recv_sem=recv.at[a],
                                         device_id=(x, y, 1 - c), device_id_type=MESH).wait_recv()

    out_shape = [jax.ShapeDtypeStruct((2 * a.shape[0], a.shape[1]), a.dtype) for a in halves]
    return _hbm_call(body, name, halves, out_shape, (n, n, n))


def _row_tile(rows, cols, itemsize, n_bufs):
    budget = VMEM_LIMIT // 2
    for t in (1024, 512, 256, 128, 64, 32, 16):
        if rows % t == 0 and 2 * n_bufs * t * cols * itemsize <= budget:
            return t
    return rows


def _pair_add(name, g, gsib, core, out_dtype):
    _, r, cols = g.shape
    hr = r // 2
    t = _row_tile(hr, cols, 4, 3)
    per = hr // t

    def body(core_ref, a_ref, b_ref, o_ref):
        o_ref[...] = (a_ref[...].astype(F32) + b_ref[...].astype(F32)).astype(o_ref.dtype)

    grid_spec = pltpu.PrefetchScalarGridSpec(
        num_scalar_prefetch=1, grid=(N_CHIPS, per),
        in_specs=[pl.BlockSpec((None, t, cols), lambda s, i, core: (s, core[0] * per + i, 0)),
                  pl.BlockSpec((None, t, cols), lambda s, i, core: (s, i, 0))],
        out_specs=pl.BlockSpec((None, t, cols), lambda s, i, core: (s, i, 0)))
    return pl.pallas_call(body, name=name, grid_spec=grid_spec,
                          out_shape=jax.ShapeDtypeStruct((N_CHIPS, hr, cols), out_dtype),
                          compiler_params=_params(("arbitrary", "arbitrary")))(core, g, gsib)


def _chip_sum(name, parts):
    _, hr, cols = parts.shape
    t = _row_tile(hr, cols, 4, 5)

    def body(p_ref, o_ref):
        o_ref[...] = ((p_ref[0].astype(F32) + p_ref[1].astype(F32)) + p_ref[2].astype(F32)) + p_ref[3].astype(F32)

    return pl.pallas_call(
        body, name=name, grid=(hr // t,), in_specs=[pl.BlockSpec((N_CHIPS, t, cols), lambda i: (0, i, 0))],
        out_specs=pl.BlockSpec((t, cols), lambda i: (i, 0)), out_shape=jax.ShapeDtypeStruct((hr, cols), F32),
        compiler_params=_params(("arbitrary",)))(parts)


def _reduce_scatter(tag, arrs, wire_dtypes, core):
    sib = _pair_swap(f"{tag}_pair_swap", arrs)
    parts = [_pair_add(f"{tag}_pair_add{i}", g, gs, core, dt) for i, (g, gs, dt) in enumerate(zip(arrs, sib, wire_dtypes))]
    got = _chip_scatter(f"{tag}_chip_scatter", parts)
    halves = [_chip_sum(f"{tag}_chip_sum{i}", p) for i, p in enumerate(got)]
    return _pair_gather(f"{tag}_pair_gather", halves)


def _adamw(name, w, g, m, v):
    rows, cols = w.shape
    t = _row_tile(rows, cols, 4, 7)
    c1 = 1.0 - ADAM_B1 ** ADAM_STEP
    c2 = 1.0 - ADAM_B2 ** ADAM_STEP

    def body(w_ref, g_ref, m_ref, v_ref, d_ref, nm_ref, nv_ref):
        g = g_ref[...]
        m = ADAM_B1 * m_ref[...] + (1.0 - ADAM_B1) * g
        v = ADAM_B2 * v_ref[...] + (1.0 - ADAM_B2) * (g * g)
        nm_ref[...] = m
        nv_ref[...] = v
        d_ref[...] = -ADAM_LR * ((m / c1) / (jnp.sqrt(v / c2) + ADAM_EPS) + ADAM_WD * w_ref[...])

    spec = pl.BlockSpec((t, cols), lambda i: (i, 0))
    shp = jax.ShapeDtypeStruct((rows, cols), F32)
    return pl.pallas_call(body, name=name, grid=(rows // t,), in_specs=[spec] * 4, out_specs=[spec] * 3,
                          out_shape=[shp] * 3, compiler_params=_params(("arbitrary",)))(w, g, m, v)


_WEIGHTS = ["mix_norm", "mlp_norm", "mlp_w1", "mlp_w2", "lru_w_in", "lru_conv_w", "lru_conv_b", "lru_w_r", "lru_b_r",
            "lru_w_i", "lru_b_i", "lru_lambda", "lru_w_out", "fox_w_in", "fox_b_f", "fox_q_gain", "fox_k_gain",
            "fox_w_out"]
_REPLICATED = ["mix_norm", "mlp_norm", "lru_conv_b", "lru_w_r", "lru_b_r", "lru_w_i", "lru_b_i", "lru_lambda",
               "fox_b_f", "fox_q_gain", "fox_k_gain"]
_PACK_TILE = 2 * SUBLANES * LANES


def _as2d(a):
    return a.reshape(-1, a.shape[-1])


def kernel(x, mix_norm, mlp_norm, mlp_w1, mlp_w2, lru_w_in, lru_conv_w, lru_conv_b, lru_w_r, lru_b_r, lru_w_i, lru_b_i, lru_lambda, lru_w_out, fox_w_in, fox_b_f, fox_q_gain, fox_k_gain, fox_w_out, loss_target, m_mix_norm, m_mlp_norm, m_mlp_w1, m_mlp_w2, m_lru_w_in, m_lru_conv_w, m_lru_conv_b, m_lru_w_r, m_lru_b_r, m_lru_w_i, m_lru_b_i, m_lru_lambda, m_lru_w_out, m_fox_w_in, m_fox_b_f, m_fox_q_gain, m_fox_k_gain, m_fox_w_out, v_mix_norm, v_mlp_norm, v_mlp_w1, v_mlp_w2, v_lru_w_in, v_lru_conv_w, v_lru_conv_b, v_lru_w_r, v_lru_b_r, v_lru_w_i, v_lru_b_i, v_lru_lambda, v_lru_w_out, v_fox_w_in, v_fox_b_f, v_fox_q_gain, v_fox_k_gain, v_fox_w_out):
    args = dict(locals())
    W = {n: args[n] for n in _WEIGHTS}
    Mo = {n: args["m_" + n] for n in _WEIGHTS}
    Vo = {n: args["v_" + n] for n in _WEIGHTS}
    S, D = x.shape[1], x.shape[2]
    F = 4 * D
    H = D // HEAD_DIM
    NU = 3 * D + LANES
    FQ, DQ = F // N_CHIPS, D // N_CHIPS
    nfox = fox_w_in.shape[-1]
    chip = 2 * lax.axis_index("x") + lax.axis_index("y")
    core = lax.axis_index("c").astype(jnp.int32).reshape(1)

    cw_flat = jnp.pad(lru_conv_w.reshape(-1), (0, _PACK_TILE - CONV_WIDTH * DQ)).reshape(2 * SUBLANES, LANES)
    g_w1, g_w2, g_lin, g_lout, g_fin, g_fout = _all_gather(
        "gather_weights",
        [_as2d(mlp_w1).astype(BF16), _as2d(mlp_w2).astype(BF16), lru_w_in[0].astype(BF16), lru_w_out[0].astype(BF16),
         fox_w_in[0].astype(BF16), fox_w_out[0].astype(BF16)])
    (g_cw,) = _all_gather("gather_conv", [cw_flat])
    conv_w_full = jnp.transpose(g_cw.reshape(N_CHIPS, -1)[:, :CONV_WIDTH * DQ].reshape(N_CHIPS, CONV_WIDTH, DQ),
                                (1, 0, 2)).reshape(CONV_WIDTH, D)
    fox_full = jnp.concatenate([g_fin[s] for s in range(N_CHIPS)], axis=1)
    fox_full = jnp.pad(fox_full, ((0, 0), (0, NU - fox_full.shape[1])))
    wv = {"w1_0": _View(g_w1, "cs", 0, D), "w1_1": _View(g_w1, "cs", D, D),
          "w2_0": _View(g_w2, "rs", 0, FQ), "w2_1": _View(g_w2, "rs", FQ, FQ),
          "lru_in": _View(g_lin, "cs"), "lru_out": _View(g_lout, "rs"),
          "fox_in": _View(fox_full), "fox_out": _View(g_fout, "rs")}

    def grad_view(grads, name):
        if name in ("w1_0", "w1_1"):
            return _View(grads.get("w1_1"), "cs", D * int(name[-1]), D, shape=(N_CHIPS, 2 * D, FQ), dtype=BF16)
        if name in ("w2_0", "w2_1"):
            return _View(grads.get("w2_1"), "rs", FQ * int(name[-1]), FQ, shape=(N_CHIPS, 2 * FQ, D), dtype=BF16)
        if name == "lru_in":
            return _View(None, "cs", shape=(N_CHIPS, D, 2 * D // N_CHIPS), dtype=BF16)
        if name in ("lru_out", "fox_out"):
            return _View(None, "rs", shape=(N_CHIPS, DQ, D), dtype=BF16)
        return _View(None, shape=(D, NU), dtype=BF16)

    small = {n: W[n] for n in _REPLICATED}
    small["conv_w"] = conv_w_full

    loss, gx, grads = _local_step(x[0], loss_target[0], small, wv, grad_view)

    g_fox = jnp.transpose(grads["fox_in"][:, :nfox * N_CHIPS].reshape(D, N_CHIPS, nfox), (1, 0, 2))
    big = [grads["w1_0"], grads["w2_0"], grads["lru_in"], grads["lru_out"], g_fox, grads["fox_out"]]
    pack_names = _REPLICATED + ["conv_w"]
    flat = jnp.concatenate([grads[n].reshape(-1).astype(F32) for n in pack_names])
    per_chip = -(-flat.shape[0] // (N_CHIPS * _PACK_TILE)) * _PACK_TILE
    pack = jnp.pad(flat, (0, N_CHIPS * per_chip - flat.shape[0])).reshape(N_CHIPS, per_chip // LANES, LANES)
    red = _reduce_scatter("grads", big + [pack], [BF16] * len(big) + [F32], core)
    r_w1, r_w2, r_lin, r_lout, r_fin, r_fout, r_pack = red
    (all_pack,) = _all_gather("gather_small_grads", [r_pack])
    all_flat = all_pack.reshape(-1)
    G = {}
    off = 0
    for n in pack_names:
        shape = grads[n].shape if n == "conv_w" else W[n].shape
        size = int(np.prod(shape))
        G[n] = all_flat[off:off + size].reshape(shape)
        off += size
    G["lru_conv_w"] = lax.dynamic_slice_in_dim(G.pop("conv_w"), chip * DQ, DQ, axis=1)[None]
    G.update(mlp_w1=r_w1.reshape(mlp_w1.shape), mlp_w2=r_w2.reshape(mlp_w2.shape), lru_w_in=r_lin[None],
             lru_w_out=r_lout[None], fox_w_in=r_fin[None], fox_w_out=r_fout[None])

    delta, new_m, new_v = {}, {}, {}
    for n in _WEIGHTS:
        d, nm, nv = _adamw(f"adamw_{n}", _as2d(W[n]), _as2d(G[n]), _as2d(Mo[n]), _as2d(Vo[n]))
        delta[n], new_m[n], new_v[n] = d.reshape(W[n].shape), nm.reshape(W[n].shape), nv.reshape(W[n].shape)

    total = lax.psum(loss[0, 0], ("x", "y", "c"))
    return (total, gx[None], *[G[n] for n in _WEIGHTS], *[delta[n] for n in _WEIGHTS],
            *[new_m[n] for n in _WEIGHTS], *[new_v[n] for n in _WEIGHTS])
```

```python
import functools

import numpy as np
import jax
import jax.numpy as jnp
from jax import lax
from jax.experimental import pallas as pl
from jax.experimental.pallas import tpu as pltpu

F32 = jnp.float32
BF16 = jnp.bfloat16

HEAD_DIM = 64
LRU_BLOCK_DIM = 64
CONV_WIDTH = 4
LRU_C = 8.0
EPS = 1e-6
NEG_INF = -1e30
ADAM_LR = 0.001
ADAM_B1 = 0.9
ADAM_B2 = 0.999
ADAM_EPS = 1e-08
ADAM_WD = 0.01
ADAM_STEP = 10

N_CHIPS = 4
LANES = 128
SUBLANES = 8
MXU_DIM = 256
VMEM_LIMIT = 52 * 1024 * 1024
MESH = pl.DeviceIdType.MESH
ANY = pl.BlockSpec(memory_space=pl.ANY)


def _pick(n, prefs):
    for p in prefs:
        if p <= n and n % p == 0:
            return p
    return n


def _params(sem=None):
    return pltpu.CompilerParams(dimension_semantics=sem, vmem_limit_bytes=VMEM_LIMIT)


class _View:
    def __init__(self, arr, kind="plain", r0=0, rows=None, shape=None, dtype=None):
        self.arr = arr
        self.kind = kind
        self.r0 = r0
        self.shape = tuple(arr.shape) if arr is not None else tuple(shape)
        self.dtype = arr.dtype if arr is not None else dtype
        self.rows = rows if rows is not None else self.shape[-2]

    def limits(self):
        if self.kind == "plain":
            return 0, 0
        rows = int(np.gcd(self.rows, self.r0))
        return rows, (self.shape[-1] if self.kind == "cs" else 0)

    def spec(self, br, bc, fr, fc):
        if self.kind == "plain":
            return pl.BlockSpec((br, bc), lambda *g: (fr(*g), fc(*g)))
        ncol = self.shape[-1]
        r0b = self.r0 // br
        assert self.r0 % br == 0 and self.rows % br == 0 and ncol % bc == 0, (self.shape, self.r0, br, bc)
        if self.kind == "cs":
            per = ncol // bc
            return pl.BlockSpec((None, br, bc), lambda *g: (fc(*g) // per, r0b + fr(*g), fc(*g) % per))
        per = self.rows // br
        return pl.BlockSpec((None, br, bc), lambda *g: (fr(*g) // per, r0b + fr(*g) % per, fc(*g)))


def _bf(x):
    return x if x.dtype == BF16 else x.astype(BF16)


def _matmul(name, A, B, M, N, K, *, ta=False, tb=False, outs, epilogue, extras=(), tm=None, tn=None, tk=None):
    lim = {"m": [M], "n": [N], "k": [K]}
    for view, (rdim, cdim) in ([(A, "km" if ta else "mk"), (B, "nk" if tb else "kn")]
                               + [(e, "mn") for e in extras] + [(o, "mn") for o in outs]):
        r_lim, c_lim = view.limits()
        lim[rdim].append(r_lim)
        lim[cdim].append(c_lim)
    tm = tm or _pick(int(np.gcd.reduce(lim["m"])), (1024, 512, 256, 128))
    tn = tn or _pick(int(np.gcd.reduce(lim["n"])), (1024, 640, 512, 256, 128))
    tk = tk or _pick(int(np.gcd.reduce(lim["k"])), (1024, 640, 512, 256, 128))
    nk = K // tk
    gi = lambda i, j, k: i
    gj = lambda i, j, k: j
    gk = lambda i, j, k: k
    a_spec = A.spec(tk, tm, gk, gi) if ta else A.spec(tm, tk, gi, gk)
    b_spec = B.spec(tn, tk, gj, gk) if tb else B.spec(tk, tn, gk, gj)
    ca = 0 if ta else 1
    cb = 1 if tb else 0
    ne, no = len(extras), len(outs)
    in_specs = [a_spec, b_spec] + [e.spec(tm, tn, gi, gj) for e in extras]
    operands = [A.arr, B.arr] + [e.arr for e in extras]
    aliases = {}
    for oi, o in enumerate(outs):
        if o.arr is not None:
            aliases[len(operands)] = oi
            in_specs.append(ANY)
            operands.append(o.arr)
    nalias = len(aliases)
    out_specs = [o.spec(tm, tn, gi, gj) for o in outs]
    out_shape = [jax.ShapeDtypeStruct(o.shape, o.dtype) for o in outs]

    def body(*refs):
        a_ref, b_ref = refs[0], refs[1]
        ex = refs[2:2 + ne]
        o_refs = refs[2 + ne + nalias:2 + ne + nalias + no]

        def prod():
            return lax.dot_general(_bf(a_ref[...]), _bf(b_ref[...]), (((ca,), (cb,)), ((), ())),
                                   preferred_element_type=F32)

        def finish(acc):
            res = epilogue(acc, *[e[...] for e in ex])
            for o_ref, r in zip(o_refs, res):
                o_ref[...] = r.astype(o_ref.dtype)

        if nk == 1:
            finish(prod())
        else:
            acc_ref = refs[-1]
            k = pl.program_id(2)

            @pl.when(k == 0)
            def _():
                acc_ref[...] = jnp.zeros_like(acc_ref)

            acc_ref[...] += prod()

            @pl.when(k == nk - 1)
            def _():
                finish(acc_ref[...])

    res = pl.pallas_call(
        body, name=name, grid=(M // tm, N // tn, nk), in_specs=in_specs, out_specs=out_specs, out_shape=out_shape,
        scratch_shapes=[pltpu.VMEM((tm, tn), F32)] if nk > 1 else [],
        input_output_aliases=aliases,
        compiler_params=_params(("parallel", "parallel", "arbitrary")),
    )(*operands)
    return res


def _ep_store(acc):
    return (acc,)


def _ep_resid(acc, res):
    return (res + acc,)


def _ep_relu2(acc):
    zp = jnp.maximum(acc, 0.0)
    return (acc, zp * zp)


def _ep_drelu2(acc, z):
    return (acc * (2.0 * jnp.maximum(z.astype(F32), 0.0)),)


def _fresh(M, N, dtype):
    return _View(None, shape=(M, N), dtype=dtype)


def _rms_fwd(name, x, g, S, D):
    T = _pick(S, (512, 256, 128))

    def body(x_ref, g_ref, h_ref):
        x = x_ref[...]
        r = lax.rsqrt(jnp.mean(x * x, axis=-1, keepdims=True) + EPS)
        h_ref[...] = ((x * r) * g_ref[...]).astype(BF16)

    return pl.pallas_call(
        body, name=name, grid=(S // T,),
        in_specs=[pl.BlockSpec((T, D), lambda i: (i, 0)), pl.BlockSpec((1, D), lambda i: (0, 0))],
        out_specs=pl.BlockSpec((T, D), lambda i: (i, 0)),
        out_shape=jax.ShapeDtypeStruct((S, D), BF16),
        compiler_params=_params(("arbitrary",)),
    )(x, g)


def _rms_bwd(name, dh, x, g, dres, S, D):
    T = _pick(S, (512, 256, 128))

    def body(dh_ref, x_ref, g_ref, dres_ref, dx_ref, dxb_ref, dg_ref):
        @pl.when(pl.program_id(0) == 0)
        def _():
            dg_ref[...] = jnp.zeros_like(dg_ref)

        x = x_ref[...]
        dh = dh_ref[...]
        r = lax.rsqrt(jnp.mean(x * x, axis=-1, keepdims=True) + EPS)
        xhat = x * r
        dg_ref[...] += jnp.sum(dh * xhat, axis=0, keepdims=True)
        dxn = dh * g_ref[...]
        dx = r * (dxn - xhat * jnp.mean(dxn * xhat, axis=-1, keepdims=True))
        tot = dres_ref[...] + dx
        dx_ref[...] = tot
        dxb_ref[...] = tot.astype(BF16)

    row = pl.BlockSpec((T, D), lambda i: (i, 0))
    vec = pl.BlockSpec((1, D), lambda i: (0, 0))
    return pl.pallas_call(
        body, name=name, grid=(S // T,), in_specs=[row, row, vec, row], out_specs=[row, row, vec],
        out_shape=[jax.ShapeDtypeStruct((S, D), F32), jax.ShapeDtypeStruct((S, D), BF16),
                   jax.ShapeDtypeStruct((1, D), F32)],
        compiler_params=_params(("arbitrary",)),
    )(dh, x, g, dres)


def _loss_head(x, tgt, S, D):
    T = _pick(S, (512, 256, 128))

    def body(x_ref, t_ref, loss_ref, d_ref, db_ref):
        @pl.when(pl.program_id(0) == 0)
        def _():
            loss_ref[...] = jnp.zeros_like(loss_ref)

        e = x_ref[...] - t_ref[...]
        loss_ref[...] += 0.5 * jnp.sum(jnp.mean(e * e, axis=-1, keepdims=True), axis=0, keepdims=True)
        d = e * (1.0 / D)
        d_ref[...] = d
        db_ref[...] = d.astype(BF16)

    row = pl.BlockSpec((T, D), lambda i: (i, 0))
    return pl.pallas_call(
        body, name="loss_head", grid=(S // T,), in_specs=[row, row],
        out_specs=[pl.BlockSpec((1, 1), lambda i: (0, 0)), row, row],
        out_shape=[jax.ShapeDtypeStruct((1, 1), F32), jax.ShapeDtypeStruct((S, D), F32),
                   jax.ShapeDtypeStruct((S, D), BF16)],
        compiler_params=_params(("arbitrary",)),
    )(x, tgt)


def _sigmoid(z):
    return 1.0 / (1.0 + jnp.exp(-z))


def _log_sigmoid(z):
    return jnp.minimum(z, 0.0) - jnp.log(1.0 + jnp.exp(-jnp.abs(z)))


_GELU_K = 0.7978845608028654
_GELU_C = 0.044715


def _gelu(x):
    t = jnp.tanh(_GELU_K * (x + _GELU_C * (x * x * x)))
    return 0.5 * x * (1.0 + t)


def _gelu_and_grad(x):
    x2 = x * x
    t = jnp.tanh(_GELU_K * (x + _GELU_C * (x2 * x)))
    g = 0.5 * x * (1.0 + t)
    dg = 0.5 * (1.0 + t) + 0.5 * x * (1.0 - t * t) * (_GELU_K * (1.0 + 3.0 * _GELU_C * x2))
    return g, dg


def _decay_terms(r, ls):
    la = LRU_C * r * ls
    a = jnp.exp(la)
    a2 = jnp.exp(2.0 * la)
    mult = jnp.sqrt(-jnp.tanh(la) * (a2 + 1.0))
    return a, a2, mult


def _lru_fwd(u0, conv_w, conv_b, wr_bd, b_r, wi_bd, b_i, lam, S, D):
    T = _pick(S, (256, 128))
    GT = wr_bd.shape[-1]
    nG = D // GT

    def body(gb_ref, xb_ref, cw_ref, cb_ref, wr_ref, br_ref, wi_ref, bi_ref, lam_ref,
             y_ref, xc_ref, r_ref, i_ref, hs_ref, ext, a_scr, hcar):
        @pl.when(pl.program_id(0) == 0)
        def _():
            ext[0:SUBLANES, :] = jnp.zeros((SUBLANES, D), F32)
            hcar[...] = jnp.zeros_like(hcar)

        xb = xb_ref[...]
        ext[SUBLANES:SUBLANES + T, :] = xb
        xc = cb_ref[...]
        for k in range(CONV_WIDTH):
            xc = xc + ext[pl.ds(SUBLANES - (CONV_WIDTH - 1) + k, T), :] * cw_ref[k:k + 1, :]
        ext[0:SUBLANES, :] = xb[T - SUBLANES:T, :]
        xc_ref[...] = xc
        xcb = xc.astype(BF16)
        for g in range(nG):
            sl = slice(g * GT, (g + 1) * GT)
            zr = jnp.dot(xcb[:, sl], wr_ref[g], preferred_element_type=F32) + br_ref[:, sl]
            zi = jnp.dot(xcb[:, sl], wi_ref[g], preferred_element_type=F32) + bi_ref[:, sl]
            r_ref[:, sl] = _sigmoid(zr)
            i_ref[:, sl] = _sigmoid(zi)
        r = r_ref[...]
        a, _, mult = _decay_terms(r, _log_sigmoid(lam_ref[...]))
        a_scr[...] = a
        hs_ref[...] = mult * (i_ref[...] * xc)

        def step(t, h):
            h = a_scr[pl.ds(t, 1), :] * h + hs_ref[pl.ds(t, 1), :]
            hs_ref[pl.ds(t, 1), :] = h
            return h

        hcar[...] = lax.fori_loop(0, T, step, hcar[...], unroll=8)
        y_ref[...] = (_gelu(gb_ref[...]) * hs_ref[...]).astype(BF16)

    row = pl.BlockSpec((T, D), lambda i: (i, 0))
    vec = pl.BlockSpec((1, D), lambda i: (0, 0))
    bd = pl.BlockSpec((nG, GT, GT), lambda i: (0, 0, 0))
    f32o = jax.ShapeDtypeStruct((S, D), F32)
    return pl.pallas_call(
        body, name="lru_fwd", grid=(S // T,),
        in_specs=[row, pl.BlockSpec((T, D), lambda i: (i, 1)), pl.BlockSpec((CONV_WIDTH, D), lambda i: (0, 0)), vec,
                  bd, vec, bd, vec, vec],
        out_specs=[row, row, row, row, row],
        out_shape=[jax.ShapeDtypeStruct((S, D), BF16), f32o, f32o, f32o, f32o],
        scratch_shapes=[pltpu.VMEM((T + SUBLANES, D), F32), pltpu.VMEM((T, D), F32), pltpu.VMEM((1, D), F32)],
        compiler_params=_params(("arbitrary",)),
    )(u0, u0, conv_w, conv_b, wr_bd, b_r, wi_bd, b_i, lam)


def _lru_bwd(dy, u0, xc, r, ig, hs, conv_w, wr_bd, wi_bd, lam, S, D):
    T = _pick(S, (128,))
    nT = S // T
    GT = wr_bd.shape[-1]
    nG = D // GT
    W = CONV_WIDTH

    def body(dy_ref, gb_ref, xb_ref, xbp_ref, xc_ref, r_ref, i_ref, hs_ref, hsp_ref, cw_ref, wr_ref, wi_ref, lam_ref,
             du_ref, dcw_ref, dcb_ref, dlam_ref, dbr_ref, dbi_ref, dwr_ref, dwi_ref,
             a_scr, dh_scr, exth, extx, extd, dxc_scr, dz_scr, carry):
        step = pl.program_id(0)
        first_tile = step == nT - 1

        @pl.when(step == 0)
        def _():
            for ref in (dcw_ref, dcb_ref, dlam_ref, dbr_ref, dbi_ref, dwr_ref, dwi_ref, carry):
                ref[...] = jnp.zeros_like(ref)
            extd[T:T + SUBLANES, :] = jnp.zeros((SUBLANES, D), F32)

        hs = hs_ref[...]
        dy = dy_ref[...]
        g, dgelu = _gelu_and_grad(gb_ref[...])
        du_ref[:, 0:D] = (dy * hs * dgelu).astype(BF16)
        r = r_ref[...]
        lam = lam_ref[...]
        ls = _log_sigmoid(lam)
        a, a2, mult = _decay_terms(r, ls)
        a_scr[...] = a
        dh_scr[...] = dy * g

        def rstep(j, c):
            t = T - 1 - j
            d = dh_scr[pl.ds(t, 1), :] + c
            dh_scr[pl.ds(t, 1), :] = d
            return a_scr[pl.ds(t, 1), :] * d

        carry[...] = lax.fori_loop(0, T, rstep, carry[...], unroll=8)
        dh = dh_scr[...]
        keep = jnp.where(first_tile, 0.0, 1.0)
        exth[0:SUBLANES, :] = hsp_ref[...] * keep
        exth[SUBLANES:SUBLANES + T, :] = hs
        hprev = exth[pl.ds(SUBLANES - 1, T), :]
        xc = xc_ref[...]
        ig = i_ref[...]
        da = dh * hprev
        dmult = dh * (ig * xc)
        dla = da * a - dmult * (a2 / mult)
        dlam_ref[...] += jnp.sum(dla * r, axis=0, keepdims=True) * (LRU_C * _sigmoid(-lam))
        dzr = (dla * (LRU_C * ls)) * (r * (1.0 - r))
        dzi = (dh * (mult * xc)) * (ig * (1.0 - ig))
        dbr_ref[...] += jnp.sum(dzr, axis=0, keepdims=True)
        dbi_ref[...] += jnp.sum(dzi, axis=0, keepdims=True)
        dxc_scr[...] = dh * (mult * ig)
        xcb = xc.astype(BF16)
        dz_scr[0] = dzr.astype(BF16)
        dz_scr[1] = dzi.astype(BF16)
        nt_dims = (((1,), (1,)), ((), ()))
        tn_dims = (((0,), (0,)), ((), ()))
        for gq in range(nG):
            sl = slice(gq * GT, (gq + 1) * GT)
            zr_g = dz_scr[0, :, sl]
            zi_g = dz_scr[1, :, sl]
            dxc_scr[:, sl] += (lax.dot_general(zr_g, wr_ref[gq], nt_dims, preferred_element_type=F32)
                               + lax.dot_general(zi_g, wi_ref[gq], nt_dims, preferred_element_type=F32))
            dwr_ref[gq] += lax.dot_general(xcb[:, sl], zr_g, tn_dims, preferred_element_type=F32)
            dwi_ref[gq] += lax.dot_general(xcb[:, sl], zi_g, tn_dims, preferred_element_type=F32)
        dxc = dxc_scr[...]
        dcb_ref[...] += jnp.sum(dxc, axis=0, keepdims=True)
        extx[0:SUBLANES, :] = xbp_ref[...] * keep
        extx[SUBLANES:SUBLANES + T, :] = xb_ref[...]
        extd[0:T, :] = dxc
        dxb = jnp.zeros((T, D), F32)
        for k in range(W):
            dxb = dxb + extd[pl.ds(W - 1 - k, T), :] * cw_ref[k:k + 1, :]
            dcw_ref[k:k + 1, :] += jnp.sum(dxc * extx[pl.ds(SUBLANES - (W - 1) + k, T), :], axis=0, keepdims=True)
        extd[T:T + SUBLANES, :] = dxc[0:SUBLANES, :]
        du_ref[:, D:2 * D] = dxb.astype(BF16)

    rev = lambda i: nT - 1 - i
    tpb = T // SUBLANES
    prev8 = lambda i: jnp.maximum(rev(i) * tpb - 1, 0)
    row = pl.BlockSpec((T, D), lambda i: (rev(i), 0))
    vec = pl.BlockSpec((1, D), lambda i: (0, 0))
    bd = pl.BlockSpec((nG, GT, GT), lambda i: (0, 0, 0))
    vec_o = jax.ShapeDtypeStruct((1, D), F32)
    bd_o = jax.ShapeDtypeStruct((nG, GT, GT), F32)
    return pl.pallas_call(
        body, name="lru_bwd", grid=(nT,),
        in_specs=[row, row, pl.BlockSpec((T, D), lambda i: (rev(i), 1)),
                  pl.BlockSpec((SUBLANES, D), lambda i: (prev8(i), 1)),
                  row, row, row, row, pl.BlockSpec((SUBLANES, D), lambda i: (prev8(i), 0)),
                  pl.BlockSpec((W, D), lambda i: (0, 0)), bd, bd, vec],
        out_specs=[pl.BlockSpec((T, 2 * D), lambda i: (rev(i), 0)), pl.BlockSpec((W, D), lambda i: (0, 0)),
                   vec, vec, vec, vec, bd, bd],
        out_shape=[jax.ShapeDtypeStruct((S, 2 * D), BF16), jax.ShapeDtypeStruct((W, D), F32),
                   vec_o, vec_o, vec_o, vec_o, bd_o, bd_o],
        scratch_shapes=[pltpu.VMEM((T, D), F32), pltpu.VMEM((T, D), F32), pltpu.VMEM((T + SUBLANES, D), F32),
                        pltpu.VMEM((T + SUBLANES, D), F32), pltpu.VMEM((T + SUBLANES, D), F32),
                        pltpu.VMEM((T, D), F32), pltpu.VMEM((2, T, D), BF16), pltpu.VMEM((1, D), F32)],
        compiler_params=_params(("arbitrary",)),
    )(dy, u0, u0, u0, xc, r, ig, hs, hs, conv_w, wr_bd, wi_bd, lam)


def _head_mean(v, bm_ref):
    hi = v.astype(BF16)
    lo = (v - hi.astype(F32)).astype(BF16)
    bm = bm_ref[...]
    return jnp.dot(hi, bm, preferred_element_type=F32) + jnp.dot(lo, bm, preferred_element_type=F32)


def _head_mean_matrix():
    blk = np.arange(LANES) // HEAD_DIM
    return jnp.asarray((blk[:, None] == blk[None, :]).astype(np.float32) / HEAD_DIM, BF16)


def _fox_pre(u, b_f, qg, kg, S, D, H):
    T = _pick(S, (256, 128))
    scale = HEAD_DIM ** -0.5

    def body(q_ref, k_ref, v_ref, f_ref, bf_ref, qg_ref, kg_ref, bm_ref, qn_ref, kn_ref, vb_ref, c_ref, ccar):
        @pl.when(pl.program_id(0) == 0)
        def _():
            ccar[...] = jnp.zeros_like(ccar)

        for src, gain, dst, mul in ((q_ref, qg_ref, qn_ref, scale), (k_ref, kg_ref, kn_ref, 1.0)):
            for g in range(D // LANES):
                sl = slice(g * LANES, (g + 1) * LANES)
                x = src[:, sl]
                rs = lax.rsqrt(_head_mean(x * x, bm_ref) + EPS)
                dst[:, sl] = (((x * rs) * gain[:, sl]) * mul).astype(BF16)
        vb_ref[...] = v_ref[...].astype(BF16)
        c_ref[...] = _log_sigmoid(f_ref[...] + bf_ref[...])

        def step(t, c):
            c = c + c_ref[pl.ds(t, 1), :]
            c_ref[pl.ds(t, 1), :] = c
            return c

        ccar[...] = lax.fori_loop(0, T, step, ccar[...], unroll=8)

    nD = D // LANES
    col = lambda j: pl.BlockSpec((T, D), lambda i: (i, j))
    row = pl.BlockSpec((T, D), lambda i: (i, 0))
    vec = pl.BlockSpec((1, D), lambda i: (0, 0))
    lane_row = pl.BlockSpec((T, LANES), lambda i: (i, 0))
    return pl.pallas_call(
        body, name="fox_pre", grid=(S // T,),
        in_specs=[col(0), col(1), col(2), pl.BlockSpec((T, LANES), lambda i: (i, 3 * nD)),
                  pl.BlockSpec((1, LANES), lambda i: (0, 0)), vec, vec, pl.BlockSpec((LANES, LANES), lambda i: (0, 0))],
        out_specs=[row, row, row, lane_row],
        out_shape=[jax.ShapeDtypeStruct((S, D), BF16)] * 3 + [jax.ShapeDtypeStruct((S, LANES), F32)],
        scratch_shapes=[pltpu.VMEM((1, LANES), F32)],
        compiler_params=_params(("arbitrary",)),
    )(u, u, u, u, b_f, qg, kg, _head_mean_matrix())


def _fox_pre_bwd(u, dqn, dkn, dv, dcq, dck, b_f, qg, kg, S, D, H):
    T = _pick(S, (256, 128))
    nT = S // T
    scale = HEAD_DIM ** -0.5
    nD = D // LANES

    def body(q_ref, k_ref, f_ref, dqn_ref, dkn_ref, dv_ref, dcq_ref, dck_ref, bf_ref, qg_ref, kg_ref, bm_ref,
             du_ref, dbf_ref, dqg_ref, dkg_ref, gacc, fcar, dlf):
        step = pl.program_id(0)

        @pl.when(step == 0)
        def _():
            gacc[...] = jnp.zeros_like(gacc)
            fcar[...] = jnp.zeros_like(fcar)
            dbf_ref[...] = jnp.zeros_like(dbf_ref)

        for idx, (src, dsrc, gain, mul) in enumerate(((q_ref, dqn_ref, qg_ref, scale), (k_ref, dkn_ref, kg_ref, 1.0))):
            for g in range(nD):
                sl = slice(g * LANES, (g + 1) * LANES)
                x = src[:, sl]
                rs = lax.rsqrt(_head_mean(x * x, bm_ref) + EPS)
                xhat = x * rs
                dn = dsrc[:, sl] * mul
                gacc[idx:idx + 1, sl] += jnp.sum(dn * xhat, axis=0, keepdims=True)
                dxh = dn * gain[:, sl]
                dx = rs * (dxh - xhat * _head_mean(dxh * xhat, bm_ref))
                du_ref[:, idx * D + g * LANES: idx * D + (g + 1) * LANES] = dx.astype(BF16)
        du_ref[:, 2 * D:3 * D] = dv_ref[...].astype(BF16)

        dlf[...] = dcq_ref[...] - dck_ref[...]

        def rstep(j, c):
            t = T - 1 - j
            c = c + dlf[pl.ds(t, 1), :]
            dlf[pl.ds(t, 1), :] = c
            return c

        fcar[...] = lax.fori_loop(0, T, rstep, fcar[...], unroll=8)
        lane = lax.broadcasted_iota(jnp.int32, (T, LANES), 1)
        dfl = jnp.where(lane < H, dlf[...] * _sigmoid(-(f_ref[...] + bf_ref[...])), 0.0)
        dbf_ref[...] += jnp.sum(dfl, axis=0, keepdims=True)
        du_ref[:, 3 * D:3 * D + LANES] = dfl.astype(BF16)

        @pl.when(step == nT - 1)
        def _():
            for idx, ref in enumerate((dqg_ref, dkg_ref)):
                tot = jnp.zeros((1, HEAD_DIM), F32)
                for h in range(D // HEAD_DIM):
                    tot = tot + gacc[idx:idx + 1, h * HEAD_DIM:(h + 1) * HEAD_DIM]
                ref[...] = tot

    rev = lambda i: nT - 1 - i
    col = lambda j: pl.BlockSpec((T, D), lambda i: (rev(i), j))
    row = pl.BlockSpec((T, D), lambda i: (rev(i), 0))
    vec = pl.BlockSpec((1, D), lambda i: (0, 0))
    lane_row = pl.BlockSpec((T, LANES), lambda i: (rev(i), 0))
    lane_vec = pl.BlockSpec((1, LANES), lambda i: (0, 0))
    head_vec = pl.BlockSpec((1, HEAD_DIM), lambda i: (0, 0))
    return pl.pallas_call(
        body, name="fox_pre_bwd", grid=(nT,),
        in_specs=[col(0), col(1), pl.BlockSpec((T, LANES), lambda i: (rev(i), 3 * nD)), row, row, row, lane_row,
                  lane_row, lane_vec, vec, vec, pl.BlockSpec((LANES, LANES), lambda i: (0, 0))],
        out_specs=[pl.BlockSpec((T, 3 * D + LANES), lambda i: (rev(i), 0)), lane_vec, head_vec, head_vec],
        out_shape=[jax.ShapeDtypeStruct((S, 3 * D + LANES), BF16), jax.ShapeDtypeStruct((1, LANES), F32),
                   jax.ShapeDtypeStruct((1, HEAD_DIM), F32), jax.ShapeDtypeStruct((1, HEAD_DIM), F32)],
        scratch_shapes=[pltpu.VMEM((2, D), F32), pltpu.VMEM((1, LANES), F32), pltpu.VMEM((T, LANES), F32)],
        compiler_params=_params(("arbitrary",)),
    )(u, u, u, dqn, dkn, dv, dcq, dck, b_f, qg, kg, _head_mean_matrix())


def _attn_tables(nq, q_major):
    if q_major:
        steps = [(qi, ki) for qi in range(nq) for ki in range(qi + 1)]
    else:
        steps = [(qi, ki) for ki in range(nq) for qi in range(ki, nq)]
    return (jnp.asarray(np.array([s[0] for s in steps], np.int32)),
            jnp.asarray(np.array([s[1] for s in steps], np.int32)), len(steps))


def _logits(q_ref, k_ref, c_ref, ct_ref, e, h, qi, ki, tq):
    hs = slice(e * HEAD_DIM, (e + 1) * HEAD_DIM)
    s = lax.dot_general(q_ref[:, hs], k_ref[:, hs], (((1,), (1,)), ((), ())), preferred_element_type=F32)
    lane = lax.broadcasted_iota(jnp.int32, (tq, LANES), 1)
    cq = jnp.sum(jnp.where(lane == h, c_ref[...], 0.0), axis=1, keepdims=True)
    s = (s + cq) - ct_ref[pl.ds(h, 1), :]
    rows = qi * tq + lax.broadcasted_iota(jnp.int32, (tq, tq), 0)
    cols = ki * tq + lax.broadcasted_iota(jnp.int32, (tq, tq), 1)
    return jnp.where(cols <= rows, s, NEG_INF)


def _attn_specs(tq, with_bwd):
    qrow = pl.BlockSpec((tq, LANES), lambda p, s, qt, kt: (qt[s], p))
    krow = pl.BlockSpec((tq, LANES), lambda p, s, qt, kt: (kt[s], p))
    c_q = pl.BlockSpec((tq, LANES), lambda p, s, qt, kt: (qt[s], 0))
    c_k = pl.BlockSpec((LANES, tq), lambda p, s, qt, kt: (0, kt[s]))
    lse = pl.BlockSpec((2, tq, LANES), lambda p, s, qt, kt: (p, qt[s], 0))
    specs = [qrow, krow, krow, c_q, c_k]
    if with_bwd:
        specs += [qrow, qrow, lse]
    return specs, qrow, krow, lse


def _attn_fwd(qn, kn, vb, c, ct, S, D, tq):
    nP = D // LANES
    qt, kt, nsteps = _attn_tables(S // tq, True)
    specs, qrow, _, lse_spec = _attn_specs(tq, False)

    def body(qt_ref, kt_ref, q_ref, k_ref, v_ref, c_ref, ct_ref, o_ref, o32_ref, lse_ref, m_scr, l_scr, acc_scr):
        p = pl.program_id(0)
        s_id = pl.program_id(1)
        qi, ki = qt_ref[s_id], kt_ref[s_id]

        @pl.when(ki == 0)
        def _():
            m_scr[...] = jnp.full_like(m_scr, NEG_INF)
            l_scr[...] = jnp.zeros_like(l_scr)
            acc_scr[...] = jnp.zeros_like(acc_scr)

        for e in range(2):
            s = _logits(q_ref, k_ref, c_ref, ct_ref, e, 2 * p + e, qi, ki, tq)
            m_prev = m_scr[e]
            m_new = jnp.maximum(m_prev, jnp.max(s, axis=1, keepdims=True))
            alpha = jnp.exp(m_prev - m_new)
            pe = jnp.exp(s - m_new)
            l_scr[e] = alpha * l_scr[e] + jnp.sum(pe, axis=1, keepdims=True)
            acc_scr[e] = alpha * acc_scr[e] + jnp.dot(pe.astype(BF16), v_ref[:, e * HEAD_DIM:(e + 1) * HEAD_DIM],
                                                      preferred_element_type=F32)
            m_scr[e] = m_new

        @pl.when(ki == qi)
        def _():
            for e in range(2):
                o_e = acc_scr[e] / l_scr[e]
                o_ref[:, e * HEAD_DIM:(e + 1) * HEAD_DIM] = o_e.astype(BF16)
                o32_ref[:, e * HEAD_DIM:(e + 1) * HEAD_DIM] = o_e
                lse_ref[e] = jnp.broadcast_to(m_scr[e] + jnp.log(l_scr[e]), (tq, LANES))

    grid_spec = pltpu.PrefetchScalarGridSpec(
        num_scalar_prefetch=2, grid=(nP, nsteps), in_specs=specs, out_specs=[qrow, qrow, lse_spec],
        scratch_shapes=[pltpu.VMEM((2, tq, 1), F32), pltpu.VMEM((2, tq, 1), F32), pltpu.VMEM((2, tq, HEAD_DIM), F32)])
    return pl.pallas_call(
        body, name="attn_fwd", grid_spec=grid_spec,
        out_shape=[jax.ShapeDtypeStruct((S, D), BF16), jax.ShapeDtypeStruct((S, D), F32),
                   jax.ShapeDtypeStruct((2 * nP, S, LANES), F32)],
        compiler_params=_params(("arbitrary", "arbitrary")),
    )(qt, kt, qn, kn, vb, c, ct)


def _attn_bwd_dq(qn, kn, vb, c, ct, o, do, lse, S, D, tq):
    nP = D // LANES
    qt, kt, nsteps = _attn_tables(S // tq, True)
    specs, qrow, _, _ = _attn_specs(tq, True)

    def body(qt_ref, kt_ref, q_ref, k_ref, v_ref, c_ref, ct_ref, o_ref, do_ref, lse_ref, dq_ref, dcq_ref,
             acc_scr, dl_scr, rs_scr):
        p = pl.program_id(0)
        s_id = pl.program_id(1)
        qi, ki = qt_ref[s_id], kt_ref[s_id]

        @pl.when(ki == 0)
        def _():
            acc_scr[...] = jnp.zeros_like(acc_scr)
            rs_scr[...] = jnp.zeros_like(rs_scr)
            for e in range(2):
                hs = slice(e * HEAD_DIM, (e + 1) * HEAD_DIM)
                dl_scr[e] = jnp.sum(do_ref[:, hs].astype(F32) * o_ref[:, hs].astype(F32), axis=1, keepdims=True)

        for e in range(2):
            hs = slice(e * HEAD_DIM, (e + 1) * HEAD_DIM)
            s = _logits(q_ref, k_ref, c_ref, ct_ref, e, 2 * p + e, qi, ki, tq)
            pe = jnp.exp(s - lse_ref[e][:, 0:1])
            dp = lax.dot_general(do_ref[:, hs], v_ref[:, hs], (((1,), (1,)), ((), ())), preferred_element_type=F32)
            ds = pe * (dp - dl_scr[e])
            rs_scr[e] += jnp.sum(ds, axis=1, keepdims=True)
            acc_scr[e] += jnp.dot(ds.astype(BF16), k_ref[:, hs], preferred_element_type=F32)

        @pl.when(ki == qi)
        def _():
            for e in range(2):
                dq_ref[:, e * HEAD_DIM:(e + 1) * HEAD_DIM] = acc_scr[e]
            lane = lax.broadcasted_iota(jnp.int32, (tq, LANES), 1)
            dcq_ref[...] = jnp.where(lane == 0, rs_scr[0], jnp.where(lane == 1, rs_scr[1], 0.0))

    grid_spec = pltpu.PrefetchScalarGridSpec(
        num_scalar_prefetch=2, grid=(nP, nsteps), in_specs=specs,
        out_specs=[qrow, pl.BlockSpec((None, tq, LANES), lambda p, s, qt, kt: (p, qt[s], 0))],
        scratch_shapes=[pltpu.VMEM((2, tq, HEAD_DIM), F32), pltpu.VMEM((2, tq, 1), F32), pltpu.VMEM((2, tq, 1), F32)])
    return pl.pallas_call(
        body, name="attn_bwd_dq", grid_spec=grid_spec,
        out_shape=[jax.ShapeDtypeStruct((S, D), F32), jax.ShapeDtypeStruct((nP, S, LANES), F32)],
        compiler_params=_params(("arbitrary", "arbitrary")),
    )(qt, kt, qn, kn, vb, c, ct, o, do, lse)


def _attn_bwd_dkv(qn, kn, vb, c, ct, o, do, lse, S, D, tq):
    nP = D // LANES
    nq = S // tq
    qt, kt, nsteps = _attn_tables(nq, False)
    specs, _, krow, _ = _attn_specs(tq, True)
    tn_dims = (((0,), (0,)), ((), ()))

    def body(qt_ref, kt_ref, q_ref, k_ref, v_ref, c_ref, ct_ref, o_ref, do_ref, lse_ref, dk_ref, dv_ref, dck_ref,
             dk_scr, dv_scr, dck_scr):
        p = pl.program_id(0)
        s_id = pl.program_id(1)
        qi, ki = qt_ref[s_id], kt_ref[s_id]

        @pl.when(qi == ki)
        def _():
            dk_scr[...] = jnp.zeros_like(dk_scr)
            dv_scr[...] = jnp.zeros_like(dv_scr)
            dck_scr[...] = jnp.zeros_like(dck_scr)

        for e in range(2):
            hs = slice(e * HEAD_DIM, (e + 1) * HEAD_DIM)
            s = _logits(q_ref, k_ref, c_ref, ct_ref, e, 2 * p + e, qi, ki, tq)
            pe = jnp.exp(s - lse_ref[e][:, 0:1])
            do_e = do_ref[:, hs]
            dv_scr[e] += lax.dot_general(pe.astype(BF16), do_e, tn_dims, preferred_element_type=F32)
            dp = lax.dot_general(do_e, v_ref[:, hs], (((1,), (1,)), ((), ())), preferred_element_type=F32)
            delta = jnp.sum(do_e.astype(F32) * o_ref[:, hs].astype(F32), axis=1, keepdims=True)
            ds = pe * (dp - delta)
            dk_scr[e] += lax.dot_general(ds.astype(BF16), q_ref[:, hs], tn_dims, preferred_element_type=F32)
            dck_scr[e:e + 1, :] += jnp.sum(ds, axis=0, keepdims=True)

        @pl.when(qi == nq - 1)
        def _():
            for e in range(2):
                hs = slice(e * HEAD_DIM, (e + 1) * HEAD_DIM)
                dk_ref[:, hs] = dk_scr[e]
                dv_ref[:, hs] = dv_scr[e].astype(BF16)
            dck_ref[...] = dck_scr[...]

    grid_spec = pltpu.PrefetchScalarGridSpec(
        num_scalar_prefetch=2, grid=(nP, nsteps), in_specs=specs,
        out_specs=[krow, krow, pl.BlockSpec((None, 2, tq), lambda p, s, qt, kt: (p, 0, kt[s]))],
        scratch_shapes=[pltpu.VMEM((2, tq, HEAD_DIM), F32), pltpu.VMEM((2, tq, HEAD_DIM), F32), pltpu.VMEM((2, tq), F32)])
    return pl.pallas_call(
        body, name="attn_bwd_dkv", grid_spec=grid_spec,
        out_shape=[jax.ShapeDtypeStruct((S, D), F32), jax.ShapeDtypeStruct((S, D), BF16),
                   jax.ShapeDtypeStruct((nP, 2, S), F32)],
        compiler_params=_params(("arbitrary", "arbitrary")),
    )(qt, kt, qn, kn, vb, c, ct, o, do, lse)


def _block_diag_tiles(w):
    n = w.shape[0]
    per = min(MXU_DIM, n * LRU_BLOCK_DIM) // LRU_BLOCK_DIM
    eye = jnp.eye(per, dtype=w.dtype)
    w5 = w.reshape(n // per, per, LRU_BLOCK_DIM, 1, LRU_BLOCK_DIM) * eye[None, :, None, :, None]
    return w5.reshape(n // per, per * LRU_BLOCK_DIM, per * LRU_BLOCK_DIM).astype(BF16)


def _block_diag_extract(t, n):
    per = t.shape[-1] // LRU_BLOCK_DIM
    eye = jnp.eye(per, dtype=t.dtype)
    t5 = t.reshape(n // per, per, LRU_BLOCK_DIM, per, LRU_BLOCK_DIM) * eye[None, :, None, :, None]
    return t5.sum(axis=3).reshape(n, LRU_BLOCK_DIM, LRU_BLOCK_DIM)


def _local_step(x, tgt, small, wv, grad_view):
    S, D = x.shape
    F = 4 * D
    H = D // HEAD_DIM
    nblk = D // LRU_BLOCK_DIM
    NU = 3 * D + LANES
    tq = _pick(S, (512, 256, 128))
    vec = lambda a: a.reshape(1, -1).astype(F32)
    mix_g, mlp_g = small["mix_norm"], small["mlp_norm"]
    conv_w, conv_b = small["conv_w"], vec(small["lru_conv_b"])
    wr_bd, wi_bd = _block_diag_tiles(small["lru_w_r"][0]), _block_diag_tiles(small["lru_w_i"][0])
    b_r, b_i, lam = vec(small["lru_b_r"]), vec(small["lru_b_i"]), vec(small["lru_lambda"])
    b_f = jnp.pad(vec(small["fox_b_f"]), ((0, 0), (0, LANES - H)))
    qg, kg = jnp.tile(vec(small["fox_q_gain"]), (1, H)), jnp.tile(vec(small["fox_k_gain"]), (1, H))
    X = lambda a: _View(a)
    grads = {}
    gout = functools.partial(grad_view, grads)

    def mlp_fwd(l, xin):
        hm = _rms_fwd(f"mlp{l}_norm", xin, mlp_g[l:l + 1], S, D)
        z, act = _matmul(f"mlp{l}_up", X(hm), wv[f"w1_{l}"], S, F, D, outs=[_fresh(S, F, BF16), _fresh(S, F, BF16)],
                         epilogue=_ep_relu2)
        (xout,) = _matmul(f"mlp{l}_down", X(act), wv[f"w2_{l}"], S, D, F, outs=[_fresh(S, D, F32)],
                          epilogue=_ep_resid, extras=[X(xin)])
        return hm, z, act, xout

    def mlp_bwd(l, xin, hm, z, act, d, db):
        (dz,) = _matmul(f"mlp{l}_dact", X(db), wv[f"w2_{l}"], S, F, D, tb=True, outs=[_fresh(S, F, BF16)],
                        epilogue=_ep_drelu2, extras=[X(z)])
        (grads[f"w2_{l}"],) = _matmul(f"mlp{l}_dw2", X(act), X(db), F, D, S, ta=True, outs=[gout(f"w2_{l}")],
                                      epilogue=_ep_store)
        (grads[f"w1_{l}"],) = _matmul(f"mlp{l}_dw1", X(hm), X(dz), D, F, S, ta=True, outs=[gout(f"w1_{l}")],
                                      epilogue=_ep_store)
        (dhm,) = _matmul(f"mlp{l}_dhm", X(dz), wv[f"w1_{l}"], S, D, F, tb=True, outs=[_fresh(S, D, F32)],
                         epilogue=_ep_store)
        return _rms_bwd(f"mlp{l}_norm_bwd", dhm, xin, mlp_g[l:l + 1], d, S, D)

    h0 = _rms_fwd("mix0_norm", x, mix_g[0:1], S, D)
    (u0,) = _matmul("lru_in", X(h0), wv["lru_in"], S, 2 * D, D, outs=[_fresh(S, 2 * D, F32)], epilogue=_ep_store)
    y, xc, r, ig, hs = _lru_fwd(u0, conv_w, conv_b, wr_bd, b_r, wi_bd, b_i, lam, S, D)
    (x1,) = _matmul("lru_out", X(y), wv["lru_out"], S, D, D, outs=[_fresh(S, D, F32)], epilogue=_ep_resid,
                    extras=[X(x)])
    hm0, z0, act0, x2 = mlp_fwd(0, x1)
    h1 = _rms_fwd("mix1_norm", x2, mix_g[1:2], S, D)
    (u1,) = _matmul("fox_in", X(h1), wv["fox_in"], S, NU, D, outs=[_fresh(S, NU, F32)], epilogue=_ep_store)
    qn, kn, vb, c = _fox_pre(u1, b_f, qg, kg, S, D, H)
    ct = c.T
    o, o32, lse = _attn_fwd(qn, kn, vb, c, ct, S, D, tq)
    (x3,) = _matmul("fox_out", X(o), wv["fox_out"], S, D, D, outs=[_fresh(S, D, F32)], epilogue=_ep_resid,
                    extras=[X(x2)])
    hm1, z1, act1, x4 = mlp_fwd(1, x3)
    loss, d4, d4b = _loss_head(x4, tgt, S, D)

    d3, d3b, dg_mlp1 = mlp_bwd(1, x3, hm1, z1, act1, d4, d4b)
    (do,) = _matmul("fox_dout", X(d3b), wv["fox_out"], S, D, D, tb=True, outs=[_fresh(S, D, BF16)], epilogue=_ep_store)
    (grads["fox_out"],) = _matmul("fox_dwout", X(o), X(d3b), D, D, S, ta=True, outs=[gout("fox_out")],
                                  epilogue=_ep_store)
    dqn, dcq = _attn_bwd_dq(qn, kn, vb, c, ct, o32, do, lse, S, D, tq)
    dkn, dv, dck = _attn_bwd_dkv(qn, kn, vb, c, ct, o32, do, lse, S, D, tq)
    dcq_col = jnp.pad(jnp.transpose(dcq[:, :, 0:2], (1, 0, 2)).reshape(S, H), ((0, 0), (0, LANES - H)))
    dck_col = jnp.pad(dck.reshape(H, S).T, ((0, 0), (0, LANES - H)))
    du1, dbf, dqg, dkg = _fox_pre_bwd(u1, dqn, dkn, dv, dcq_col, dck_col, b_f, qg, kg, S, D, H)
    (grads["fox_in"],) = _matmul("fox_dwin", X(h1), X(du1), D, NU, S, ta=True, outs=[gout("fox_in")],
                                 epilogue=_ep_store)
    (dh1,) = _matmul("fox_dh", X(du1), wv["fox_in"], S, D, NU, tb=True, outs=[_fresh(S, D, F32)], epilogue=_ep_store)
    d2, d2b, dg_mix1 = _rms_bwd("mix1_norm_bwd", dh1, x2, mix_g[1:2], d3, S, D)
    d1, d1b, dg_mlp0 = mlp_bwd(0, x1, hm0, z0, act0, d2, d2b)
    (dy,) = _matmul("lru_dout", X(d1b), wv["lru_out"], S, D, D, tb=True, outs=[_fresh(S, D, F32)], epilogue=_ep_store)
    (grads["lru_out"],) = _matmul("lru_dwout", X(y), X(d1b), D, D, S, ta=True, outs=[gout("lru_out")],
                                  epilogue=_ep_store)
    du0, dcw, dcb, dlam, dbr, dbi, dwr, dwi = _lru_bwd(dy, u0, xc, r, ig, hs, conv_w, wr_bd, wi_bd, lam, S, D)
    (grads["lru_in"],) = _matmul("lru_dwin", X(h0), X(du0), D, 2 * D, S, ta=True, outs=[gout("lru_in")],
                                 epilogue=_ep_store)
    (dh0,) = _matmul("lru_dh", X(du0), wv["lru_in"], S, D, 2 * D, tb=True, outs=[_fresh(S, D, F32)], epilogue=_ep_store)
    gx, _, dg_mix0 = _rms_bwd("mix0_norm_bwd", dh0, x, mix_g[0:1], d1, S, D)

    grads.update(
        mix_norm=jnp.concatenate([dg_mix0, dg_mix1], axis=0), mlp_norm=jnp.concatenate([dg_mlp0, dg_mlp1], axis=0),
        conv_w=dcw, lru_conv_b=dcb, lru_w_r=_block_diag_extract(dwr, nblk)[None], lru_b_r=dbr.reshape(1, nblk, -1),
        lru_w_i=_block_diag_extract(dwi, nblk)[None], lru_b_i=dbi.reshape(1, nblk, -1), lru_lambda=dlam,
        fox_b_f=dbf[:, :H], fox_q_gain=dqg, fox_k_gain=dkg)
    return loss, gx, grads


def _place():
    x, y, c = lax.axis_index("x"), lax.axis_index("y"), lax.axis_index("c")
    chips = [(1 - x, y), (x, 1 - y), (1 - x, 1 - y)]
    return x, y, c, 2 * x + y, chips


BOUNCE_BYTES = 1 << 20


def _bounce_shape(rows, cols, dtype):
    chunk = rows
    while chunk % 2 == 0 and chunk > 16 and chunk * cols * jnp.dtype(dtype).itemsize > BOUNCE_BYTES:
        chunk //= 2
    return pltpu.VMEM((2, chunk, cols), dtype)


def _bounce_copy(src, dst, buf, sem):
    chunk = buf.shape[1]
    n = src.shape[0] // chunk
    cin = lambda i: pltpu.make_async_copy(src.at[pl.ds(i * chunk, chunk)], buf.at[i % 2], sem.at[i % 2])
    cout = lambda i: pltpu.make_async_copy(buf.at[i % 2], dst.at[pl.ds(i * chunk, chunk)], sem.at[2 + i % 2])
    cin(0).start()
    for i in range(n):
        cin(i).wait()
        if i + 1 < n:
            if i >= 1:
                cout(i - 1).wait()
            cin(i + 1).start()
        cout(i).start()
    if n >= 2:
        cout(n - 2).wait()
    cout(n - 1).wait()


def _hbm_call(body, name, arrays, out_shape, n_dma_sems, bounce=()):
    scratch = [pltpu.SemaphoreType.DMA((k,)) for k in n_dma_sems]
    for rows, cols, dtype in bounce:
        scratch += [_bounce_shape(rows, cols, dtype), pltpu.SemaphoreType.DMA((4,))]
    return pl.pallas_call(
        body, name=name, in_specs=[ANY] * len(arrays), out_specs=[ANY] * len(out_shape), out_shape=out_shape,
        scratch_shapes=scratch,
        compiler_params=pltpu.CompilerParams(has_side_effects=True, vmem_limit_bytes=VMEM_LIMIT),
    )(*arrays)


def _all_gather(name, shards):
    n = len(shards)

    def body(*refs):
        ins, outs = refs[:n], refs[n:2 * n]
        send, recv, fsend, frecv = refs[2 * n:2 * n + 4]
        stage = refs[2 * n + 4:]
        x, y, c, s, chips = _place()
        sibling = (x, y, 1 - c)

        def rows(a, chip_idx, which):
            hr = ins[a].shape[0] // 2
            return outs[a].at[chip_idx, pl.ds(which * hr, hr)]

        def ici(a, j, src, dst, to):
            return pltpu.make_async_remote_copy(src_ref=src, dst_ref=dst, send_sem=send.at[3 * a + j],
                                                recv_sem=recv.at[3 * a + j], device_id=to, device_id_type=MESH)

        def d2d(a, j, src, dst):
            return pltpu.make_async_remote_copy(src_ref=src, dst_ref=dst, send_sem=fsend.at[3 * a + j],
                                                recv_sem=frecv.at[3 * a + j], device_id=sibling, device_id_type=MESH)

        started = []
        for a in range(n):
            hr = ins[a].shape[0] // 2
            for j, chip in enumerate(chips):
                cp = ici(a, j, ins[a].at[pl.ds(c * hr, hr)], rows(a, s, c), (*chip, c))
                cp.start()
                started.append(cp)
        for a in range(n):
            _bounce_copy(ins[a], outs[a].at[s], stage[2 * a], stage[2 * a + 1])
        for a in range(n):
            for j, chip in enumerate(chips):
                got = rows(a, 2 * chip[0] + chip[1], c)
                ici(a, j, got, got, (*chip, c)).wait_recv()
                fw = d2d(a, j, got, got)
                fw.start()
                started.append(fw)
        for a in range(n):
            for j, chip in enumerate(chips):
                theirs = rows(a, 2 * chip[0] + chip[1], 1 - c)
                d2d(a, j, theirs, theirs).wait_recv()
        for cp in started:
            cp.wait_send()

    out_shape = [jax.ShapeDtypeStruct((N_CHIPS,) + tuple(a.shape), a.dtype) for a in shards]
    return _hbm_call(body, name, shards, out_shape, (3 * n, 3 * n, 3 * n, 3 * n),
                     bounce=[(a.shape[0], a.shape[1], a.dtype) for a in shards])


def _pair_swap(name, arrs):
    n = len(arrs)

    def body(*refs):
        ins, outs = refs[:n], refs[n:2 * n]
        send, recv = refs[2 * n:]
        x, y, c, _, _ = _place()
        cps = []
        for a in range(n):
            hr = ins[a].shape[1] // 2
            cp = pltpu.make_async_remote_copy(
                src_ref=ins[a].at[:, pl.ds((1 - c) * hr, hr)], dst_ref=outs[a], send_sem=send.at[a],
                recv_sem=recv.at[a], device_id=(x, y, 1 - c), device_id_type=MESH)
            cp.start()
            cps.append(cp)
        for cp in cps:
            cp.wait()

    out_shape = [jax.ShapeDtypeStruct((a.shape[0], a.shape[1] // 2, a.shape[2]), a.dtype) for a in arrs]
    return _hbm_call(body, name, arrs, out_shape, (n, n))


def _chip_scatter(name, parts):
    n = len(parts)

    def body(*refs):
        ins, outs = refs[:n], refs[n:2 * n]
        send, recv = refs[2 * n:2 * n + 2]
        stage = refs[2 * n + 2:]
        x, y, c, s, chips = _place()
        cps = []
        for a in range(n):
            for j, chip in enumerate(chips):
                t = 2 * chip[0] + chip[1]
                cp = pltpu.make_async_remote_copy(
                    src_ref=ins[a].at[t], dst_ref=outs[a].at[s], send_sem=send.at[3 * a + j],
                    recv_sem=recv.at[3 * a + j], device_id=(*chip, c), device_id_type=MESH)
                cp.start()
                cps.append(cp)
        for a in range(n):
            _bounce_copy(ins[a].at[s], outs[a].at[s], stage[2 * a], stage[2 * a + 1])
        for a in range(n):
            for j, chip in enumerate(chips):
                t = 2 * chip[0] + chip[1]
                pltpu.make_async_remote_copy(
                    src_ref=ins[a].at[t], dst_ref=outs[a].at[t], send_sem=send.at[3 * a + j],
                    recv_sem=recv.at[3 * a + j], device_id=(*chip, c), device_id_type=MESH).wait_recv()
        for cp in cps:
            cp.wait_send()

    out_shape = [jax.ShapeDtypeStruct(a.shape, a.dtype) for a in parts]
    return _hbm_call(body, name, parts, out_shape, (3 * n, 3 * n),
                     bounce=[(a.shape[1], a.shape[2], a.dtype) for a in parts])


def _pair_gather(name, halves):
    n = len(halves)

    def body(*refs):
        ins, outs = refs[:n], refs[n:2 * n]
        send, recv = refs[2 * n:2 * n + 2]
        stage = refs[2 * n + 2:]
        x, y, c, _, _ = _place()
        cps = []
        for a in range(n):
            hr = ins[a].shape[0]
            cp = pltpu.make_async_remote_copy(
                src_ref=ins[a], dst_ref=outs[a].at[pl.ds(c * hr, hr)], send_sem=send.at[a], recv_sem=recv.at[a],
                device_id=(x, y, 1 - c), device_id_type=MESH)
            cp.start()
            cps.append((cp, hr))
        for a, (cp, hr) in enumerate(cps):
            _bounce_copy(ins[a], outs[a].at[pl.ds(c * hr, hr)], stage[2 * a], stage[2 * a + 1])
        for a, (cp, hr) in enumerate(cps):
            cp.wait_send()
            theirs = outs[a].at[pl.ds((1 - c) * hr, hr)]
            pltpu.make_async_remote_copy(src_ref=theirs, dst_ref=theirs, send_sem=send.at[a], recv_sem=recv.at[a],
                                         device_id=(x, y, 1 - c), device_id_type=MESH).wait_recv()

    out_shape = [jax.ShapeDtypeStruct((2 * a.shape[0], a.shape[1]), a.dtype) for a in halves]
    return _hbm_call(body, name, halves, out_shape, (n, n),
                     bounce=[(a.shape[0], a.shape[1], a.dtype) for a in halves])


def _row_tile(rows, cols, itemsize, n_bufs):
    budget = VMEM_LIMIT // 2
    for t in (1024, 512, 256, 128, 64, 32, 16):
        if rows % t == 0 and 2 * n_bufs * t * cols * itemsize <= budget:
            return t
    return rows


def _pair_add(name, g, gsib, core, out_dtype):
    _, r, cols = g.shape
    hr = r // 2
    t = _row_tile(hr, cols, 4, 3)
    per = hr // t

    def body(core_ref, a_ref, b_ref, o_ref):
        o_ref[...] = (a_ref[...].astype(F32) + b_ref[...].astype(F32)).astype(o_ref.dtype)

    grid_spec = pltpu.PrefetchScalarGridSpec(
        num_scalar_prefetch=1, grid=(N_CHIPS, per),
        in_specs=[pl.BlockSpec((None, t, cols), lambda s, i, core: (s, core[0] * per + i, 0)),
                  pl.BlockSpec((None, t, cols), lambda s, i, core: (s, i, 0))],
        out_specs=pl.BlockSpec((None, t, cols), lambda s, i, core: (s, i, 0)))
    return pl.pallas_call(body, name=name, grid_spec=grid_spec,
                          out_shape=jax.ShapeDtypeStruct((N_CHIPS, hr, cols), out_dtype),
                          compiler_params=_params(("arbitrary", "arbitrary")))(core, g, gsib)


def _chip_sum(name, parts):
    _, hr, cols = parts.shape
    t = _row_tile(hr, cols, 4, 5)

    def body(p_ref, o_ref):
        o_ref[...] = ((p_ref[0].astype(F32) + p_ref[1].astype(F32)) + p_ref[2].astype(F32)) + p_ref[3].astype(F32)

    return pl.pallas_call(
        body, name=name, grid=(hr // t,), in_specs=[pl.BlockSpec((N_CHIPS, t, cols), lambda i: (0, i, 0))],
        out_specs=pl.BlockSpec((t, cols), lambda i: (i, 0)), out_shape=jax.ShapeDtypeStruct((hr, cols), F32),
        compiler_params=_params(("arbitrary",)))(parts)


def _reduce_scatter(tag, arrs, wire_dtypes, core):
    sib = _pair_swap(f"{tag}_pair_swap", arrs)
    parts = [_pair_add(f"{tag}_pair_add{i}", g, gs, core, dt) for i, (g, gs, dt) in enumerate(zip(arrs, sib, wire_dtypes))]
    got = _chip_scatter(f"{tag}_chip_scatter", parts)
    halves = [_chip_sum(f"{tag}_chip_sum{i}", p) for i, p in enumerate(got)]
    return _pair_gather(f"{tag}_pair_gather", halves)


def _adamw(name, w, g, m, v):
    rows, cols = w.shape
    t = _row_tile(rows, cols, 4, 8)
    c1 = 1.0 - ADAM_B1 ** ADAM_STEP
    c2 = 1.0 - ADAM_B2 ** ADAM_STEP

    def body(w_ref, g_ref, m_ref, v_ref, go_ref, d_ref, nm_ref, nv_ref):
        g = g_ref[...]
        go_ref[...] = g
        m = ADAM_B1 * m_ref[...] + (1.0 - ADAM_B1) * g
        v = ADAM_B2 * v_ref[...] + (1.0 - ADAM_B2) * (g * g)
        nm_ref[...] = m
        nv_ref[...] = v
        d_ref[...] = -ADAM_LR * ((m / c1) / (jnp.sqrt(v / c2) + ADAM_EPS) + ADAM_WD * w_ref[...])

    spec = pl.BlockSpec((t, cols), lambda i: (i, 0))
    shp = jax.ShapeDtypeStruct((rows, cols), F32)
    return pl.pallas_call(body, name=name, grid=(rows // t,), in_specs=[spec] * 4, out_specs=[spec] * 4,
                          out_shape=[shp] * 4, compiler_params=_params(("arbitrary",)))(w, g, m, v)


_WEIGHTS = ["mix_norm", "mlp_norm", "mlp_w1", "mlp_w2", "lru_w_in", "lru_conv_w", "lru_conv_b", "lru_w_r", "lru_b_r",
            "lru_w_i", "lru_b_i", "lru_lambda", "lru_w_out", "fox_w_in", "fox_b_f", "fox_q_gain", "fox_k_gain",
            "fox_w_out"]
_REPLICATED = ["mix_norm", "mlp_norm", "lru_conv_b", "lru_w_r", "lru_b_r", "lru_w_i", "lru_b_i", "lru_lambda",
               "fox_b_f", "fox_q_gain", "fox_k_gain"]
_PACK_TILE = 2 * SUBLANES * LANES


def _as2d(a):
    return a.reshape(-1, a.shape[-1])


def kernel(x, mix_norm, mlp_norm, mlp_w1, mlp_w2, lru_w_in, lru_conv_w, lru_conv_b, lru_w_r, lru_b_r, lru_w_i, lru_b_i, lru_lambda, lru_w_out, fox_w_in, fox_b_f, fox_q_gain, fox_k_gain, fox_w_out, loss_target, m_mix_norm, m_mlp_norm, m_mlp_w1, m_mlp_w2, m_lru_w_in, m_lru_conv_w, m_lru_conv_b, m_lru_w_r, m_lru_b_r, m_lru_w_i, m_lru_b_i, m_lru_lambda, m_lru_w_out, m_fox_w_in, m_fox_b_f, m_fox_q_gain, m_fox_k_gain, m_fox_w_out, v_mix_norm, v_mlp_norm, v_mlp_w1, v_mlp_w2, v_lru_w_in, v_lru_conv_w, v_lru_conv_b, v_lru_w_r, v_lru_b_r, v_lru_w_i, v_lru_b_i, v_lru_lambda, v_lru_w_out, v_fox_w_in, v_fox_b_f, v_fox_q_gain, v_fox_k_gain, v_fox_w_out):
    args = dict(locals())
    W = {n: args[n] for n in _WEIGHTS}
    Mo = {n: args["m_" + n] for n in _WEIGHTS}
    Vo = {n: args["v_" + n] for n in _WEIGHTS}
    S, D = x.shape[1], x.shape[2]
    F = 4 * D
    H = D // HEAD_DIM
    NU = 3 * D + LANES
    FQ, DQ = F // N_CHIPS, D // N_CHIPS
    nfox = fox_w_in.shape[-1]
    chip = 2 * lax.axis_index("x") + lax.axis_index("y")
    core = lax.axis_index("c").astype(jnp.int32).reshape(1)

    cw_flat = jnp.pad(lru_conv_w.reshape(-1), (0, _PACK_TILE - CONV_WIDTH * DQ)).reshape(2 * SUBLANES, LANES)
    g_w1, g_w2, g_lin, g_lout, g_fin, g_fout = _all_gather(
        "gather_weights",
        [_as2d(mlp_w1).astype(BF16), _as2d(mlp_w2).astype(BF16), lru_w_in[0].astype(BF16), lru_w_out[0].astype(BF16),
         fox_w_in[0].astype(BF16), fox_w_out[0].astype(BF16)])
    (g_cw,) = _all_gather("gather_conv", [cw_flat])
    conv_w_full = jnp.transpose(g_cw.reshape(N_CHIPS, -1)[:, :CONV_WIDTH * DQ].reshape(N_CHIPS, CONV_WIDTH, DQ),
                                (1, 0, 2)).reshape(CONV_WIDTH, D)
    fox_full = jnp.concatenate([g_fin[s] for s in range(N_CHIPS)], axis=1)
    fox_full = jnp.pad(fox_full, ((0, 0), (0, NU - fox_full.shape[1])))
    wv = {"w1_0": _View(g_w1, "cs", 0, D), "w1_1": _View(g_w1, "cs", D, D),
          "w2_0": _View(g_w2, "rs", 0, FQ), "w2_1": _View(g_w2, "rs", FQ, FQ),
          "lru_in": _View(g_lin, "cs"), "lru_out": _View(g_lout, "rs"),
          "fox_in": _View(fox_full), "fox_out": _View(g_fout, "rs")}

    def grad_view(grads, name):
        if name in ("w1_0", "w1_1"):
            return _View(grads.get("w1_1"), "cs", D * int(name[-1]), D, shape=(N_CHIPS, 2 * D, FQ), dtype=BF16)
        if name in ("w2_0", "w2_1"):
            return _View(grads.get("w2_1"), "rs", FQ * int(name[-1]), FQ, shape=(N_CHIPS, 2 * FQ, D), dtype=BF16)
        if name == "lru_in":
            return _View(None, "cs", shape=(N_CHIPS, D, 2 * D // N_CHIPS), dtype=BF16)
        if name in ("lru_out", "fox_out"):
            return _View(None, "rs", shape=(N_CHIPS, DQ, D), dtype=BF16)
        return _View(None, shape=(D, NU), dtype=BF16)

    small = {n: W[n] for n in _REPLICATED}
    small["conv_w"] = conv_w_full

    loss, gx, grads = _local_step(x[0], loss_target[0], small, wv, grad_view)

    g_fox = jnp.transpose(grads["fox_in"][:, :nfox * N_CHIPS].reshape(D, N_CHIPS, nfox), (1, 0, 2))
    big = [grads["w1_0"], grads["w2_0"], grads["lru_in"], grads["lru_out"], g_fox, grads["fox_out"]]
    pack_names = _REPLICATED + ["conv_w"]
    flat = jnp.concatenate([grads[n].reshape(-1).astype(F32) for n in pack_names])
    per_chip = -(-flat.shape[0] // (N_CHIPS * _PACK_TILE)) * _PACK_TILE
    pack = jnp.pad(flat, (0, N_CHIPS * per_chip - flat.shape[0])).reshape(N_CHIPS, per_chip // LANES, LANES)
    red = _reduce_scatter("grads", big + [pack], [BF16] * len(big) + [F32], core)
    r_w1, r_w2, r_lin, r_lout, r_fin, r_fout, r_pack = red
    (all_pack,) = _all_gather("gather_small_grads", [r_pack])
    all_flat = all_pack.reshape(-1)
    G = {}
    off = 0
    for n in pack_names:
        shape = grads[n].shape if n == "conv_w" else W[n].shape
        size = int(np.prod(shape))
        G[n] = all_flat[off:off + size].reshape(shape)
        off += size
    G["lru_conv_w"] = lax.dynamic_slice_in_dim(G.pop("conv_w"), chip * DQ, DQ, axis=1)[None]
    G.update(mlp_w1=r_w1.reshape(mlp_w1.shape), mlp_w2=r_w2.reshape(mlp_w2.shape), lru_w_in=r_lin[None],
             lru_w_out=r_lout[None], fox_w_in=r_fin[None], fox_w_out=r_fout[None])

    delta, new_m, new_v = {}, {}, {}
    for n in _WEIGHTS:
        go, d, nm, nv = _adamw(f"adamw_{n}", _as2d(W[n]), _as2d(G[n]), _as2d(Mo[n]), _as2d(Vo[n]))
        G[n], delta[n], new_m[n], new_v[n] = (t.reshape(W[n].shape) for t in (go, d, nm, nv))

    total = lax.psum(loss[0, 0], ("x", "y", "c"))
    return (total, gx[None], *[G[n] for n in _WEIGHTS], *[delta[n] for n in _WEIGHTS],
            *[new_m[n] for n in _WEIGHTS], *[new_v[n] for n in _WEIGHTS])
```

```python
import functools

import numpy as np
import jax
import jax.numpy as jnp
from jax import lax
from jax.experimental import pallas as pl
from jax.experimental.pallas import tpu as pltpu

F32 = jnp.float32
BF16 = jnp.bfloat16

HEAD_DIM = 64
LRU_BLOCK_DIM = 64
CONV_WIDTH = 4
LRU_C = 8.0
EPS = 1e-6
NEG_INF = -1e30
ADAM_LR = 0.001
ADAM_B1 = 0.9
ADAM_B2 = 0.999
ADAM_EPS = 1e-08
ADAM_WD = 0.01
ADAM_STEP = 10

N_CHIPS = 4
LANES = 128
SUBLANES = 8
MXU_DIM = 256
VMEM_LIMIT = 52 * 1024 * 1024
MESH = pl.DeviceIdType.MESH
ANY = pl.BlockSpec(memory_space=pl.ANY)


def _pick(n, prefs):
    for p in prefs:
        if p <= n and n % p == 0:
            return p
    return n


def _params(sem=None):
    return pltpu.CompilerParams(dimension_semantics=sem, vmem_limit_bytes=VMEM_LIMIT)


class _View:
    def __init__(self, arr, kind="plain", r0=0, rows=None, shape=None, dtype=None):
        self.arr = arr
        self.kind = kind
        self.r0 = r0
        self.shape = tuple(arr.shape) if arr is not None else tuple(shape)
        self.dtype = arr.dtype if arr is not None else dtype
        self.rows = rows if rows is not None else self.shape[-2]

    def limits(self):
        if self.kind == "plain":
            return 0, 0
        rows = int(np.gcd(self.rows, self.r0))
        return rows, (self.shape[-1] if self.kind == "cs" else 0)

    def spec(self, br, bc, fr, fc):
        if self.kind == "plain":
            return pl.BlockSpec((br, bc), lambda *g: (fr(*g), fc(*g)))
        ncol = self.shape[-1]
        r0b = self.r0 // br
        assert self.r0 % br == 0 and self.rows % br == 0 and ncol % bc == 0, (self.shape, self.r0, br, bc)
        if self.kind == "cs":
            per = ncol // bc
            return pl.BlockSpec((None, br, bc), lambda *g: (fc(*g) // per, r0b + fr(*g), fc(*g) % per))
        per = self.rows // br
        return pl.BlockSpec((None, br, bc), lambda *g: (fr(*g) // per, r0b + fr(*g) % per, fc(*g)))


def _bf(x):
    return x if x.dtype == BF16 else x.astype(BF16)


def _matmul(name, A, B, M, N, K, *, ta=False, tb=False, outs, epilogue, extras=(), tm=None, tn=None, tk=None):
    lim = {"m": [M], "n": [N], "k": [K]}
    for view, (rdim, cdim) in ([(A, "km" if ta else "mk"), (B, "nk" if tb else "kn")]
                               + [(e, "mn") for e in extras] + [(o, "mn") for o in outs]):
        r_lim, c_lim = view.limits()
        lim[rdim].append(r_lim)
        lim[cdim].append(c_lim)
    tm = tm or _pick(int(np.gcd.reduce(lim["m"])), (1024, 512, 256, 128))
    tn = tn or _pick(int(np.gcd.reduce(lim["n"])), (1024, 640, 512, 256, 128))
    tk = tk or _pick(int(np.gcd.reduce(lim["k"])), (1024, 640, 512, 256, 128))
    nk = K // tk
    gi = lambda i, j, k: i
    gj = lambda i, j, k: j
    gk = lambda i, j, k: k
    a_spec = A.spec(tk, tm, gk, gi) if ta else A.spec(tm, tk, gi, gk)
    b_spec = B.spec(tn, tk, gj, gk) if tb else B.spec(tk, tn, gk, gj)
    ca = 0 if ta else 1
    cb = 1 if tb else 0
    ne, no = len(extras), len(outs)
    in_specs = [a_spec, b_spec] + [e.spec(tm, tn, gi, gj) for e in extras]
    operands = [A.arr, B.arr] + [e.arr for e in extras]
    aliases = {}
    for oi, o in enumerate(outs):
        if o.arr is not None:
            aliases[len(operands)] = oi
            in_specs.append(ANY)
            operands.append(o.arr)
    nalias = len(aliases)
    out_specs = [o.spec(tm, tn, gi, gj) for o in outs]
    out_shape = [jax.ShapeDtypeStruct(o.shape, o.dtype) for o in outs]

    def body(*refs):
        a_ref, b_ref = refs[0], refs[1]
        ex = refs[2:2 + ne]
        o_refs = refs[2 + ne + nalias:2 + ne + nalias + no]

        def prod():
            return lax.dot_general(_bf(a_ref[...]), _bf(b_ref[...]), (((ca,), (cb,)), ((), ())),
                                   preferred_element_type=F32)

        def finish(acc):
            res = epilogue(acc, *[e[...] for e in ex])
            for o_ref, r in zip(o_refs, res):
                o_ref[...] = r.astype(o_ref.dtype)

        if nk == 1:
            finish(prod())
        else:
            acc_ref = refs[-1]
            k = pl.program_id(2)

            @pl.when(k == 0)
            def _():
                acc_ref[...] = jnp.zeros_like(acc_ref)

            acc_ref[...] += prod()

            @pl.when(k == nk - 1)
            def _():
                finish(acc_ref[...])

    res = pl.pallas_call(
        body, name=name, grid=(M // tm, N // tn, nk), in_specs=in_specs, out_specs=out_specs, out_shape=out_shape,
        scratch_shapes=[pltpu.VMEM((tm, tn), F32)] if nk > 1 else [],
        input_output_aliases=aliases,
        compiler_params=_params(("parallel", "parallel", "arbitrary")),
    )(*operands)
    return res


def _ep_store(acc):
    return (acc,)


def _ep_resid(acc, res):
    return (res + acc,)


def _ep_relu2(acc):
    zp = jnp.maximum(acc, 0.0)
    return (acc, zp * zp)


def _ep_drelu2(acc, z):
    return (acc * (2.0 * jnp.maximum(z.astype(F32), 0.0)),)


def _fresh(M, N, dtype):
    return _View(None, shape=(M, N), dtype=dtype)


def _rms_fwd(name, x, g, S, D):
    T = _pick(S, (512, 256, 128))

    def body(x_ref, g_ref, h_ref):
        x = x_ref[...]
        r = lax.rsqrt(jnp.mean(x * x, axis=-1, keepdims=True) + EPS)
        h_ref[...] = ((x * r) * g_ref[...]).astype(BF16)

    return pl.pallas_call(
        body, name=name, grid=(S // T,),
        in_specs=[pl.BlockSpec((T, D), lambda i: (i, 0)), pl.BlockSpec((1, D), lambda i: (0, 0))],
        out_specs=pl.BlockSpec((T, D), lambda i: (i, 0)),
        out_shape=jax.ShapeDtypeStruct((S, D), BF16),
        compiler_params=_params(("arbitrary",)),
    )(x, g)


def _rms_bwd(name, dh, x, g, dres, S, D):
    T = _pick(S, (512, 256, 128))

    def body(dh_ref, x_ref, g_ref, dres_ref, dx_ref, dxb_ref, dg_ref):
        @pl.when(pl.program_id(0) == 0)
        def _():
            dg_ref[...] = jnp.zeros_like(dg_ref)

        x = x_ref[...]
        dh = dh_ref[...]
        r = lax.rsqrt(jnp.mean(x * x, axis=-1, keepdims=True) + EPS)
        xhat = x * r
        dg_ref[...] += jnp.sum(dh * xhat, axis=0, keepdims=True)
        dxn = dh * g_ref[...]
        dx = r * (dxn - xhat * jnp.mean(dxn * xhat, axis=-1, keepdims=True))
        tot = dres_ref[...] + dx
        dx_ref[...] = tot
        dxb_ref[...] = tot.astype(BF16)

    row = pl.BlockSpec((T, D), lambda i: (i, 0))
    vec = pl.BlockSpec((1, D), lambda i: (0, 0))
    return pl.pallas_call(
        body, name=name, grid=(S // T,), in_specs=[row, row, vec, row], out_specs=[row, row, vec],
        out_shape=[jax.ShapeDtypeStruct((S, D), F32), jax.ShapeDtypeStruct((S, D), BF16),
                   jax.ShapeDtypeStruct((1, D), F32)],
        compiler_params=_params(("arbitrary",)),
    )(dh, x, g, dres)


def _loss_head(x, tgt, S, D):
    T = _pick(S, (512, 256, 128))

    def body(x_ref, t_ref, loss_ref, d_ref, db_ref):
        @pl.when(pl.program_id(0) == 0)
        def _():
            loss_ref[...] = jnp.zeros_like(loss_ref)

        e = x_ref[...] - t_ref[...]
        loss_ref[...] += 0.5 * jnp.sum(jnp.mean(e * e, axis=-1, keepdims=True), axis=0, keepdims=True)
        d = e * (1.0 / D)
        d_ref[...] = d
        db_ref[...] = d.astype(BF16)

    row = pl.BlockSpec((T, D), lambda i: (i, 0))
    return pl.pallas_call(
        body, name="loss_head", grid=(S // T,), in_specs=[row, row],
        out_specs=[pl.BlockSpec((1, 1), lambda i: (0, 0)), row, row],
        out_shape=[jax.ShapeDtypeStruct((1, 1), F32), jax.ShapeDtypeStruct((S, D), F32),
                   jax.ShapeDtypeStruct((S, D), BF16)],
        compiler_params=_params(("arbitrary",)),
    )(x, tgt)


def _sigmoid(z):
    return 1.0 / (1.0 + jnp.exp(-z))


def _log_sigmoid(z):
    return jnp.minimum(z, 0.0) - jnp.log(1.0 + jnp.exp(-jnp.abs(z)))


_GELU_K = 0.7978845608028654
_GELU_C = 0.044715


def _gelu(x):
    t = jnp.tanh(_GELU_K * (x + _GELU_C * (x * x * x)))
    return 0.5 * x * (1.0 + t)


def _gelu_and_grad(x):
    x2 = x * x
    t = jnp.tanh(_GELU_K * (x + _GELU_C * (x2 * x)))
    g = 0.5 * x * (1.0 + t)
    dg = 0.5 * (1.0 + t) + 0.5 * x * (1.0 - t * t) * (_GELU_K * (1.0 + 3.0 * _GELU_C * x2))
    return g, dg


def _decay_terms(r, ls):
    la = LRU_C * r * ls
    a = jnp.exp(la)
    a2 = jnp.exp(2.0 * la)
    mult = jnp.sqrt(-jnp.tanh(la) * (a2 + 1.0))
    return a, a2, mult


def _lru_fwd(u0, conv_w, conv_b, wr_bd, b_r, wi_bd, b_i, lam, S, D):
    T = _pick(S, (256, 128))
    GT = wr_bd.shape[-1]
    nG = D // GT

    def body(gb_ref, xb_ref, cw_ref, cb_ref, wr_ref, br_ref, wi_ref, bi_ref, lam_ref,
             y_ref, xc_ref, r_ref, i_ref, hs_ref, ext, a_scr, hcar):
        @pl.when(pl.program_id(0) == 0)
        def _():
            ext[0:SUBLANES, :] = jnp.zeros((SUBLANES, D), F32)
            hcar[...] = jnp.zeros_like(hcar)

        xb = xb_ref[...]
        ext[SUBLANES:SUBLANES + T, :] = xb
        xc = cb_ref[...]
        for k in range(CONV_WIDTH):
            xc = xc + ext[pl.ds(SUBLANES - (CONV_WIDTH - 1) + k, T), :] * cw_ref[k:k + 1, :]
        ext[0:SUBLANES, :] = xb[T - SUBLANES:T, :]
        xc_ref[...] = xc
        xcb = xc.astype(BF16)
        for g in range(nG):
            sl = slice(g * GT, (g + 1) * GT)
            zr = jnp.dot(xcb[:, sl], wr_ref[g], preferred_element_type=F32) + br_ref[:, sl]
            zi = jnp.dot(xcb[:, sl], wi_ref[g], preferred_element_type=F32) + bi_ref[:, sl]
            r_ref[:, sl] = _sigmoid(zr)
            i_ref[:, sl] = _sigmoid(zi)
        r = r_ref[...]
        a, _, mult = _decay_terms(r, _log_sigmoid(lam_ref[...]))
        a_scr[...] = a
        hs_ref[...] = mult * (i_ref[...] * xc)

        def step(t, h):
            h = a_scr[pl.ds(t, 1), :] * h + hs_ref[pl.ds(t, 1), :]
            hs_ref[pl.ds(t, 1), :] = h
            return h

        hcar[...] = lax.fori_loop(0, T, step, hcar[...], unroll=8)
        y_ref[...] = (_gelu(gb_ref[...]) * hs_ref[...]).astype(BF16)

    row = pl.BlockSpec((T, D), lambda i: (i, 0))
    vec = pl.BlockSpec((1, D), lambda i: (0, 0))
    bd = pl.BlockSpec((nG, GT, GT), lambda i: (0, 0, 0))
    f32o = jax.ShapeDtypeStruct((S, D), F32)
    return pl.pallas_call(
        body, name="lru_fwd", grid=(S // T,),
        in_specs=[row, pl.BlockSpec((T, D), lambda i: (i, 1)), pl.BlockSpec((CONV_WIDTH, D), lambda i: (0, 0)), vec,
                  bd, vec, bd, vec, vec],
        out_specs=[row, row, row, row, row],
        out_shape=[jax.ShapeDtypeStruct((S, D), BF16), f32o, f32o, f32o, f32o],
        scratch_shapes=[pltpu.VMEM((T + SUBLANES, D), F32), pltpu.VMEM((T, D), F32), pltpu.VMEM((1, D), F32)],
        compiler_params=_params(("arbitrary",)),
    )(u0, u0, conv_w, conv_b, wr_bd, b_r, wi_bd, b_i, lam)


def _lru_bwd(dy, u0, xc, r, ig, hs, conv_w, wr_bd, wi_bd, lam, S, D):
    T = _pick(S, (128,))
    nT = S // T
    GT = wr_bd.shape[-1]
    nG = D // GT
    W = CONV_WIDTH

    def body(dy_ref, gb_ref, xb_ref, xbp_ref, xc_ref, r_ref, i_ref, hs_ref, hsp_ref, cw_ref, wr_ref, wi_ref, lam_ref,
             du_ref, dcw_ref, dcb_ref, dlam_ref, dbr_ref, dbi_ref, dwr_ref, dwi_ref,
             a_scr, dh_scr, exth, extx, extd, dxc_scr, dz_scr, carry):
        step = pl.program_id(0)
        first_tile = step == nT - 1

        @pl.when(step == 0)
        def _():
            for ref in (dcw_ref, dcb_ref, dlam_ref, dbr_ref, dbi_ref, dwr_ref, dwi_ref, carry):
                ref[...] = jnp.zeros_like(ref)
            extd[T:T + SUBLANES, :] = jnp.zeros((SUBLANES, D), F32)

        hs = hs_ref[...]
        dy = dy_ref[...]
        g, dgelu = _gelu_and_grad(gb_ref[...])
        du_ref[:, 0:D] = (dy * hs * dgelu).astype(BF16)
        r = r_ref[...]
        lam = lam_ref[...]
        ls = _log_sigmoid(lam)
        a, a2, mult = _decay_terms(r, ls)
        a_scr[...] = a
        dh_scr[...] = dy * g

        def rstep(j, c):
            t = T - 1 - j
            d = dh_scr[pl.ds(t, 1), :] + c
            dh_scr[pl.ds(t, 1), :] = d
            return a_scr[pl.ds(t, 1), :] * d

        carry[...] = lax.fori_loop(0, T, rstep, carry[...], unroll=8)
        dh = dh_scr[...]
        keep = jnp.where(first_tile, 0.0, 1.0)
        exth[0:SUBLANES, :] = hsp_ref[...] * keep
        exth[SUBLANES:SUBLANES + T, :] = hs
        hprev = exth[pl.ds(SUBLANES - 1, T), :]
        xc = xc_ref[...]
        ig = i_ref[...]
        da = dh * hprev
        dmult = dh * (ig * xc)
        dla = da * a - dmult * (a2 / mult)
        dlam_ref[...] += jnp.sum(dla * r, axis=0, keepdims=True) * (LRU_C * _sigmoid(-lam))
        dzr = (dla * (LRU_C * ls)) * (r * (1.0 - r))
        dzi = (dh * (mult * xc)) * (ig * (1.0 - ig))
        dbr_ref[...] += jnp.sum(dzr, axis=0, keepdims=True)
        dbi_ref[...] += jnp.sum(dzi, axis=0, keepdims=True)
        dxc_scr[...] = dh * (mult * ig)
        xcb = xc.astype(BF16)
        dz_scr[0] = dzr.astype(BF16)
        dz_scr[1] = dzi.astype(BF16)
        nt_dims = (((1,), (1,)), ((), ()))
        tn_dims = (((0,), (0,)), ((), ()))
        for gq in range(nG):
            sl = slice(gq * GT, (gq + 1) * GT)
            zr_g = dz_scr[0, :, sl]
            zi_g = dz_scr[1, :, sl]
            dxc_scr[:, sl] += (lax.dot_general(zr_g, wr_ref[gq], nt_dims, preferred_element_type=F32)
                               + lax.dot_general(zi_g, wi_ref[gq], nt_dims, preferred_element_type=F32))
            dwr_ref[gq] += lax.dot_general(xcb[:, sl], zr_g, tn_dims, preferred_element_type=F32)
            dwi_ref[gq] += lax.dot_general(xcb[:, sl], zi_g, tn_dims, preferred_element_type=F32)
        dxc = dxc_scr[...]
        dcb_ref[...] += jnp.sum(dxc, axis=0, keepdims=True)
        extx[0:SUBLANES, :] = xbp_ref[...] * keep
        extx[SUBLANES:SUBLANES + T, :] = xb_ref[...]
        extd[0:T, :] = dxc
        dxb = jnp.zeros((T, D), F32)
        for k in range(W):
            dxb = dxb + extd[pl.ds(W - 1 - k, T), :] * cw_ref[k:k + 1, :]
            dcw_ref[k:k + 1, :] += jnp.sum(dxc * extx[pl.ds(SUBLANES - (W - 1) + k, T), :], axis=0, keepdims=True)
        extd[T:T + SUBLANES, :] = dxc[0:SUBLANES, :]
        du_ref[:, D:2 * D] = dxb.astype(BF16)

    rev = lambda i: nT - 1 - i
    tpb = T // SUBLANES
    prev8 = lambda i: jnp.maximum(rev(i) * tpb - 1, 0)
    row = pl.BlockSpec((T, D), lambda i: (rev(i), 0))
    vec = pl.BlockSpec((1, D), lambda i: (0, 0))
    bd = pl.BlockSpec((nG, GT, GT), lambda i: (0, 0, 0))
    vec_o = jax.ShapeDtypeStruct((1, D), F32)
    bd_o = jax.ShapeDtypeStruct((nG, GT, GT), F32)
    return pl.pallas_call(
        body, name="lru_bwd", grid=(nT,),
        in_specs=[row, row, pl.BlockSpec((T, D), lambda i: (rev(i), 1)),
                  pl.BlockSpec((SUBLANES, D), lambda i: (prev8(i), 1)),
                  row, row, row, row, pl.BlockSpec((SUBLANES, D), lambda i: (prev8(i), 0)),
                  pl.BlockSpec((W, D), lambda i: (0, 0)), bd, bd, vec],
        out_specs=[pl.BlockSpec((T, 2 * D), lambda i: (rev(i), 0)), pl.BlockSpec((W, D), lambda i: (0, 0)),
                   vec, vec, vec, vec, bd, bd],
        out_shape=[jax.ShapeDtypeStruct((S, 2 * D), BF16), jax.ShapeDtypeStruct((W, D), F32),
                   vec_o, vec_o, vec_o, vec_o, bd_o, bd_o],
        scratch_shapes=[pltpu.VMEM((T, D), F32), pltpu.VMEM((T, D), F32), pltpu.VMEM((T + SUBLANES, D), F32),
                        pltpu.VMEM((T + SUBLANES, D), F32), pltpu.VMEM((T + SUBLANES, D), F32),
                        pltpu.VMEM((T, D), F32), pltpu.VMEM((2, T, D), BF16), pltpu.VMEM((1, D), F32)],
        compiler_params=_params(("arbitrary",)),
    )(dy, u0, u0, u0, xc, r, ig, hs, hs, conv_w, wr_bd, wi_bd, lam)


AUG_ROWS = 16
HEAD_ROWS = 128
LSE_ROW = HEAD_DIM + 6


def _split3(x):
    b1 = x.astype(BF16).astype(F32)
    r = x - b1
    b2 = r.astype(BF16).astype(F32)
    return b1, b2, r - b2


def _head_block(x, aug, T):
    row = lax.broadcasted_iota(jnp.int32, (AUG_ROWS, T), 0)
    blk = jnp.zeros((AUG_ROWS, T), F32)
    for i, e in enumerate(aug):
        blk = jnp.where(row == i, e, blk)
    return jnp.concatenate([x, blk, jnp.zeros((HEAD_ROWS - HEAD_DIM - AUG_ROWS, T), F32)], axis=0)


def _tri_matrix(lower):
    i = np.arange(LANES)
    m = (i[:, None] >= i[None, :]) if lower else (i[:, None] <= i[None, :])
    return jnp.asarray(m.astype(np.float32), BF16)


def _lane_cumsum(x, tri_ref, carry, reverse):
    n = x.shape[1] // LANES
    tri = tri_ref[...]
    out = [None] * n
    for j in (range(n - 1, -1, -1) if reverse else range(n)):
        cs = carry
        for part in _split3(x[:, j * LANES:(j + 1) * LANES]):
            cs = cs + jnp.dot(part.astype(BF16), tri, preferred_element_type=F32)
        out[j] = cs
        carry = cs[:, 0:1] if reverse else cs[:, LANES - 1:LANES]
    return jnp.concatenate(out, axis=1), carry


def _head_rows(h):
    return pl.ds(pl.multiple_of(h * HEAD_DIM, HEAD_DIM), HEAD_DIM)


def _fox_prep(ut, b_f, qg, kg, S, D, tq):
    H = D // HEAD_DIM
    T = min(tq, 256)
    per = tq // T
    scale = HEAD_DIM ** -0.5

    def body(q_ref, k_ref, v_ref, f_ref, bf_ref, qg_ref, kg_ref, tri_ref,
             qat_ref, kat_ref, vat_ref, ka_ref, va_ref, vt_ref, c_scr, ccar):
        @pl.when(pl.program_id(0) == 0)
        def _():
            ccar[...] = jnp.zeros_like(ccar)

        c, carry = _lane_cumsum(_log_sigmoid(f_ref[...] + bf_ref[...]), tri_ref, ccar[...], False)
        c_scr[...] = c
        ccar[...] = carry

        def head(h, _):
            rows = _head_rows(h)
            c1, c2, c3 = _split3(c_scr[pl.ds(h, 1), :])

            def normed(src, gain, mul):
                x = src[rows, :]
                rs = lax.rsqrt(jnp.mean(x * x, axis=0, keepdims=True) + EPS)
                return ((x * rs) * gain[rows, :]) * mul

            qat_ref[h] = _head_block(normed(q_ref, qg_ref, scale), [c1, c2, c3, 1.0, 1.0, 1.0], T).astype(BF16)
            kb = _head_block(normed(k_ref, kg_ref, 1.0), [1.0, 1.0, 1.0, -c1, -c2, -c3, 1.0, 1.0, 1.0], T)
            kat_ref[h] = kb.astype(BF16)
            ka_ref[h] = kb.T.astype(BF16)
            v = v_ref[rows, :]
            vt_ref[h] = v.astype(BF16)
            vb = _head_block(v, [1.0, 1.0, 1.0], T)
            vat_ref[h] = vb.astype(BF16)
            va_ref[h] = vb.T.astype(BF16)
            return 0

        lax.fori_loop(0, H, head, 0)

    part = lambda j: pl.BlockSpec((D, T), lambda i: (j, i))
    colv = lambda n: pl.BlockSpec((n, 1), lambda i: (0, 0))
    tmaj = lambda r: pl.BlockSpec((H, None, r, T), lambda i: (0, i // per, 0, i % per))
    norm = pl.BlockSpec((H, T, HEAD_ROWS), lambda i: (0, i, 0))
    tshape = lambda r: jax.ShapeDtypeStruct((H, S // tq, r, tq), BF16)
    nshape = jax.ShapeDtypeStruct((H, S, HEAD_ROWS), BF16)
    return pl.pallas_call(
        body, name="fox_prep", grid=(S // T,),
        in_specs=[part(0), part(1), part(2), pl.BlockSpec((LANES, T), lambda i: (3 * D // LANES, i)),
                  colv(LANES), colv(D), colv(D), pl.BlockSpec((LANES, LANES), lambda i: (0, 0))],
        out_specs=[tmaj(HEAD_ROWS), tmaj(HEAD_ROWS), tmaj(HEAD_ROWS), norm, norm, tmaj(HEAD_DIM)],
        out_shape=[tshape(HEAD_ROWS), tshape(HEAD_ROWS), tshape(HEAD_ROWS), nshape, nshape, tshape(HEAD_DIM)],
        scratch_shapes=[pltpu.VMEM((LANES, T), F32), pltpu.VMEM((LANES, 1), F32)],
        compiler_params=_params(("arbitrary",)),
    )(ut, ut, ut, ut, b_f, qg, kg, _tri_matrix(False))


def _fox_bwd_prep(dot, ot, lse, qat, S, D, tq):
    H = D // HEAD_DIM
    T = min(tq, 256)
    per = tq // T

    def body(do_ref, o_ref, lse_ref, qat_ref, doat_ref, doa_ref, qat1_ref, qa1_ref):
        row = lax.broadcasted_iota(jnp.int32, (HEAD_ROWS, T), 0)

        def head(h, _):
            rows = _head_rows(h)
            do = do_ref[rows, :].astype(F32)
            delta = jnp.sum(do * o_ref[rows, :], axis=0, keepdims=True)
            db = _head_block(do, list(_split3(-delta)), T)
            doat_ref[h] = db.astype(BF16)
            doa_ref[h] = db.T.astype(BF16)
            qb = qat_ref[h].astype(F32)
            for i, e in enumerate(_split3(-lse_ref[h])):
                qb = jnp.where(row == LSE_ROW + i, e, qb)
            qat1_ref[h] = qb.astype(BF16)
            qa1_ref[h] = qb.T.astype(BF16)
            return 0

        lax.fori_loop(0, H, head, 0)

    chan = pl.BlockSpec((D, T), lambda i: (0, i))
    tmaj = pl.BlockSpec((H, None, HEAD_ROWS, T), lambda i: (0, i // per, 0, i % per))
    norm = pl.BlockSpec((H, T, HEAD_ROWS), lambda i: (0, i, 0))
    tshape = jax.ShapeDtypeStruct((H, S // tq, HEAD_ROWS, tq), BF16)
    nshape = jax.ShapeDtypeStruct((H, S, HEAD_ROWS), BF16)
    return pl.pallas_call(
        body, name="fox_bwd_prep", grid=(S // T,),
        in_specs=[chan, chan, pl.BlockSpec((H, 1, T), lambda i: (0, 0, i)), tmaj],
        out_specs=[tmaj, norm, tmaj, norm], out_shape=[tshape, nshape, tshape, nshape],
        compiler_params=_params(("arbitrary",)),
    )(dot, ot, lse, qat)


def _causal(s, k_axis):
    ki = lax.broadcasted_iota(jnp.int32, s.shape, k_axis)
    qi = lax.broadcasted_iota(jnp.int32, s.shape, 1 - k_axis)
    return jnp.where(ki <= qi, s, NEG_INF)


def _seq_tile(i, t):
    return pl.ds(pl.multiple_of(i * t, t), t)


def _attn_forward(ka, qat, vt, S, D, tq):
    H = D // HEAD_DIM
    nq = S // tq

    def body(ka_ref, qat_ref, vt_ref, o_ref, o32_ref, lse_ref, m_scr, l_scr, acc_scr):
        qi = pl.program_id(1)
        m_scr[...] = jnp.full_like(m_scr, NEG_INF)
        l_scr[...] = jnp.zeros_like(l_scr)
        acc_scr[...] = jnp.zeros_like(acc_scr)
        qa = qat_ref[...]

        def tile(ki, diagonal):
            s = jnp.dot(ka_ref[_seq_tile(ki, tq), :], qa, preferred_element_type=F32)
            if diagonal:
                s = _causal(s, 0)
            m_prev = m_scr[...]
            m_new = jnp.maximum(m_prev, jnp.max(s, axis=0, keepdims=True))
            alpha = jnp.exp(m_prev - m_new)
            p = jnp.exp(s - m_new)
            l_scr[...] = alpha * l_scr[...] + jnp.sum(p, axis=0, keepdims=True)
            acc_scr[...] = alpha * acc_scr[...] + jnp.dot(vt_ref[ki], p.astype(BF16), preferred_element_type=F32)
            m_scr[...] = m_new

        def off_diagonal(ki, _):
            tile(ki, False)
            return 0

        lax.fori_loop(0, qi, off_diagonal, 0)
        tile(qi, True)
        o = acc_scr[...] / l_scr[...]
        o_ref[...] = o.astype(BF16)
        o32_ref[...] = o
        lse_ref[...] = m_scr[...] + jnp.log(l_scr[...])

    chan = pl.BlockSpec((HEAD_DIM, tq), lambda h, i: (h, i))
    stat = pl.BlockSpec((None, 1, tq), lambda h, i: (h, 0, i))
    return pl.pallas_call(
        body, name="attn_forward", grid=(H, nq),
        in_specs=[pl.BlockSpec((None, S, HEAD_ROWS), lambda h, i: (h, 0, 0)),
                  pl.BlockSpec((None, None, HEAD_ROWS, tq), lambda h, i: (h, i, 0, 0)),
                  pl.BlockSpec((None, nq, HEAD_DIM, tq), lambda h, i: (h, 0, 0, 0))],
        out_specs=[chan, chan, stat],
        out_shape=[jax.ShapeDtypeStruct((D, S), BF16), jax.ShapeDtypeStruct((D, S), F32),
                   jax.ShapeDtypeStruct((H, 1, S), F32)],
        scratch_shapes=[pltpu.VMEM((1, tq), F32), pltpu.VMEM((1, tq), F32), pltpu.VMEM((HEAD_DIM, tq), F32)],
        compiler_params=_params(("arbitrary", "arbitrary")),
    )(ka, qat, vt)


def _attn_backward_q(ka, va, kat, qat, doat, S, D, tq):
    H = D // HEAD_DIM
    nq = S // tq

    def body(ka_ref, va_ref, kat_ref, qat_ref, doat_ref, dq_ref, dcq_ref, dq_scr, rs_scr):
        qi = pl.program_id(1)
        dq_scr[...] = jnp.zeros_like(dq_scr)
        rs_scr[...] = jnp.zeros_like(rs_scr)
        qa = qat_ref[...]
        doa = doat_ref[...]

        def tile(ki, diagonal):
            rows = _seq_tile(ki, tq)
            s = jnp.dot(ka_ref[rows, :], qa, preferred_element_type=F32)
            if diagonal:
                s = _causal(s, 0)
            ds = jnp.exp(s) * jnp.dot(va_ref[rows, :], doa, preferred_element_type=F32)
            rs_scr[...] += jnp.sum(ds, axis=0, keepdims=True)
            dq_scr[...] += jnp.dot(kat_ref[ki, 0:HEAD_DIM, :], ds.astype(BF16), preferred_element_type=F32)

        def off_diagonal(ki, _):
            tile(ki, False)
            return 0

        lax.fori_loop(0, qi, off_diagonal, 0)
        tile(qi, True)
        dq_ref[...] = dq_scr[...]
        dcq_ref[...] = rs_scr[...]

    whole = pl.BlockSpec((None, S, HEAD_ROWS), lambda h, i: (h, 0, 0))
    one = pl.BlockSpec((None, None, HEAD_ROWS, tq), lambda h, i: (h, i, 0, 0))
    return pl.pallas_call(
        body, name="attn_backward_q", grid=(H, nq),
        in_specs=[whole, whole, pl.BlockSpec((None, nq, HEAD_ROWS, tq), lambda h, i: (h, 0, 0, 0)), one, one],
        out_specs=[pl.BlockSpec((HEAD_DIM, tq), lambda h, i: (h, i)), pl.BlockSpec((None, 1, tq), lambda h, i: (h, 0, i))],
        out_shape=[jax.ShapeDtypeStruct((D, S), F32), jax.ShapeDtypeStruct((H, 1, S), F32)],
        scratch_shapes=[pltpu.VMEM((HEAD_DIM, tq), F32), pltpu.VMEM((1, tq), F32)],
        compiler_params=_params(("arbitrary", "arbitrary")),
    )(ka, va, kat, qat, doat)


def _attn_backward_kv(qa, doa, qat, doat, kat, vat, S, D, tq):
    H = D // HEAD_DIM
    nq = S // tq

    def body(qa_ref, doa_ref, qat_ref, doat_ref, kat_ref, vat_ref, dk_ref, dv_ref, dck_ref, dk_scr, dv_scr, cs_scr):
        ki = pl.program_id(1)
        dk_scr[...] = jnp.zeros_like(dk_scr)
        dv_scr[...] = jnp.zeros_like(dv_scr)
        cs_scr[...] = jnp.zeros_like(cs_scr)
        ka = kat_ref[...]
        va = vat_ref[...]

        def tile(qi, diagonal):
            rows = _seq_tile(qi, tq)
            s = jnp.dot(qa_ref[rows, :], ka, preferred_element_type=F32)
            if diagonal:
                s = _causal(s, 1)
            p = jnp.exp(s)
            ds = p * jnp.dot(doa_ref[rows, :], va, preferred_element_type=F32)
            dv_scr[...] += jnp.dot(doat_ref[qi, 0:HEAD_DIM, :], p.astype(BF16), preferred_element_type=F32)
            dk_scr[...] += jnp.dot(qat_ref[qi, 0:HEAD_DIM, :], ds.astype(BF16), preferred_element_type=F32)
            cs_scr[...] += jnp.sum(ds, axis=0, keepdims=True)

        def off_diagonal(qi, _):
            tile(qi, False)
            return 0

        tile(ki, True)
        lax.fori_loop(ki + 1, nq, off_diagonal, 0)
        dk_ref[...] = dk_scr[...]
        dv_ref[...] = dv_scr[...].astype(BF16)
        dck_ref[...] = cs_scr[...]

    whole = pl.BlockSpec((None, S, HEAD_ROWS), lambda h, i: (h, 0, 0))
    tiles = pl.BlockSpec((None, nq, HEAD_ROWS, tq), lambda h, i: (h, 0, 0, 0))
    one = pl.BlockSpec((None, None, HEAD_ROWS, tq), lambda h, i: (h, i, 0, 0))
    chan = pl.BlockSpec((HEAD_DIM, tq), lambda h, i: (h, i))
    return pl.pallas_call(
        body, name="attn_backward_kv", grid=(H, nq),
        in_specs=[whole, whole, tiles, tiles, one, one],
        out_specs=[chan, chan, pl.BlockSpec((None, 1, tq), lambda h, i: (h, 0, i))],
        out_shape=[jax.ShapeDtypeStruct((D, S), F32), jax.ShapeDtypeStruct((D, S), BF16),
                   jax.ShapeDtypeStruct((H, 1, S), F32)],
        scratch_shapes=[pltpu.VMEM((HEAD_DIM, tq), F32), pltpu.VMEM((HEAD_DIM, tq), F32), pltpu.VMEM((1, tq), F32)],
        compiler_params=_params(("arbitrary", "arbitrary")),
    )(qa, doa, qat, doat, kat, vat)


def _fox_prep_bwd(ut, dqt, dkt, dvt, dcq, dck, b_f, qg, kg, S, D, tq):
    H = D // HEAD_DIM
    T = min(tq, 256)
    nT = S // T
    NU = 3 * D + LANES
    scale = HEAD_DIM ** -0.5

    def body(q_ref, k_ref, f_ref, dq_ref, dk_ref, dv_ref, dcq_ref, dck_ref, bf_ref, qg_ref, kg_ref, tri_ref,
             du_ref, dbf_ref, dqg_ref, dkg_ref, gq_acc, gk_acc, fcar, dc_scr):
        step = pl.program_id(0)

        @pl.when(step == 0)
        def _():
            for ref in (gq_acc, gk_acc, fcar, dbf_ref):
                ref[...] = jnp.zeros_like(ref)

        dc_scr[...] = jnp.zeros_like(dc_scr)

        def head(h, _):
            rows = _head_rows(h)
            dc_scr[pl.ds(h, 1), :] = dcq_ref[h] - dck_ref[h]
            for src, dsrc, gain, acc, mul, base in ((q_ref, dq_ref, qg_ref, gq_acc, scale, 0),
                                                    (k_ref, dk_ref, kg_ref, gk_acc, 1.0, D)):
                x = src[rows, :]
                rs = lax.rsqrt(jnp.mean(x * x, axis=0, keepdims=True) + EPS)
                xhat = x * rs
                dn = dsrc[rows, :] * mul
                acc[rows, :] += jnp.sum(dn * xhat, axis=1, keepdims=True)
                dxh = dn * gain[rows, :]
                dx = rs * (dxh - xhat * jnp.mean(dxh * xhat, axis=0, keepdims=True))
                du_ref[pl.ds(pl.multiple_of(base + h * HEAD_DIM, HEAD_DIM), HEAD_DIM), :] = dx.astype(BF16)
            return 0

        lax.fori_loop(0, H, head, 0)
        du_ref[2 * D:3 * D, :] = dv_ref[...]
        dlf, carry = _lane_cumsum(dc_scr[...], tri_ref, fcar[...], True)
        fcar[...] = carry
        dfl = dlf * _sigmoid(-(f_ref[...] + bf_ref[...]))
        dbf_ref[...] += jnp.sum(dfl, axis=1, keepdims=True)
        du_ref[3 * D:NU, :] = dfl.astype(BF16)

        @pl.when(step == nT - 1)
        def _():
            for acc, ref in ((gq_acc, dqg_ref), (gk_acc, dkg_ref)):
                tot = jnp.zeros((HEAD_DIM, 1), F32)
                for h in range(H):
                    tot = tot + acc[h * HEAD_DIM:(h + 1) * HEAD_DIM, :]
                ref[...] = tot

    rev = lambda i: nT - 1 - i
    part = lambda j: pl.BlockSpec((D, T), lambda i: (j, rev(i)))
    chan = pl.BlockSpec((D, T), lambda i: (0, rev(i)))
    stat = pl.BlockSpec((H, 1, T), lambda i: (0, 0, rev(i)))
    colv = lambda n: pl.BlockSpec((n, 1), lambda i: (0, 0))
    return pl.pallas_call(
        body, name="fox_prep_bwd", grid=(nT,),
        in_specs=[part(0), part(1), pl.BlockSpec((LANES, T), lambda i: (3 * D // LANES, rev(i))), chan, chan, chan,
                  stat, stat, colv(LANES), colv(D), colv(D), pl.BlockSpec((LANES, LANES), lambda i: (0, 0))],
        out_specs=[pl.BlockSpec((NU, T), lambda i: (0, rev(i))), colv(LANES), colv(HEAD_DIM), colv(HEAD_DIM)],
        out_shape=[jax.ShapeDtypeStruct((NU, S), BF16), jax.ShapeDtypeStruct((LANES, 1), F32),
                   jax.ShapeDtypeStruct((HEAD_DIM, 1), F32), jax.ShapeDtypeStruct((HEAD_DIM, 1), F32)],
        scratch_shapes=[pltpu.VMEM((D, 1), F32), pltpu.VMEM((D, 1), F32), pltpu.VMEM((LANES, 1), F32),
                        pltpu.VMEM((LANES, T), F32)],
        compiler_params=_params(("arbitrary",)),
    )(ut, ut, ut, dqt, dkt, dvt, dcq, dck, b_f, qg, kg, _tri_matrix(True))


def _block_diag_tiles(w):
    n = w.shape[0]
    per = min(MXU_DIM, n * LRU_BLOCK_DIM) // LRU_BLOCK_DIM
    eye = jnp.eye(per, dtype=w.dtype)
    w5 = w.reshape(n // per, per, LRU_BLOCK_DIM, 1, LRU_BLOCK_DIM) * eye[None, :, None, :, None]
    return w5.reshape(n // per, per * LRU_BLOCK_DIM, per * LRU_BLOCK_DIM).astype(BF16)


def _block_diag_extract(t, n):
    per = t.shape[-1] // LRU_BLOCK_DIM
    eye = jnp.eye(per, dtype=t.dtype)
    t5 = t.reshape(n // per, per, LRU_BLOCK_DIM, per, LRU_BLOCK_DIM) * eye[None, :, None, :, None]
    return t5.sum(axis=3).reshape(n, LRU_BLOCK_DIM, LRU_BLOCK_DIM)


def _local_step(x, tgt, small, wv, grad_view):
    S, D = x.shape
    F = 4 * D
    H = D // HEAD_DIM
    nblk = D // LRU_BLOCK_DIM
    NU = 3 * D + LANES
    tq = max(LANES, min(512, S // 4))
    assert S % tq == 0
    vec = lambda a: a.reshape(1, -1).astype(F32)
    col = lambda a: a.reshape(-1, 1).astype(F32)
    mix_g, mlp_g = small["mix_norm"], small["mlp_norm"]
    conv_w, conv_b = small["conv_w"], vec(small["lru_conv_b"])
    wr_bd, wi_bd = _block_diag_tiles(small["lru_w_r"][0]), _block_diag_tiles(small["lru_w_i"][0])
    b_r, b_i, lam = vec(small["lru_b_r"]), vec(small["lru_b_i"]), vec(small["lru_lambda"])
    b_f = jnp.pad(col(small["fox_b_f"]), ((0, LANES - H), (0, 0)))
    qg, kg = jnp.tile(col(small["fox_q_gain"]), (H, 1)), jnp.tile(col(small["fox_k_gain"]), (H, 1))
    X = lambda a: _View(a)
    grads = {}
    gout = functools.partial(grad_view, grads)

    def mlp_fwd(l, xin):
        hm = _rms_fwd(f"mlp{l}_norm", xin, mlp_g[l:l + 1], S, D)
        z, act = _matmul(f"mlp{l}_up", X(hm), wv[f"w1_{l}"], S, F, D, outs=[_fresh(S, F, BF16), _fresh(S, F, BF16)],
                         epilogue=_ep_relu2)
        (xout,) = _matmul(f"mlp{l}_down", X(act), wv[f"w2_{l}"], S, D, F, outs=[_fresh(S, D, F32)],
                          epilogue=_ep_resid, extras=[X(xin)])
        return hm, z, act, xout

    def mlp_bwd(l, xin, hm, z, act, d, db):
        (dz,) = _matmul(f"mlp{l}_dact", X(db), wv[f"w2_{l}"], S, F, D, tb=True, outs=[_fresh(S, F, BF16)],
                        epilogue=_ep_drelu2, extras=[X(z)])
        (grads[f"w2_{l}"],) = _matmul(f"mlp{l}_dw2", X(act), X(db), F, D, S, ta=True, outs=[gout(f"w2_{l}")],
                                      epilogue=_ep_store)
        (grads[f"w1_{l}"],) = _matmul(f"mlp{l}_dw1", X(hm), X(dz), D, F, S, ta=True, outs=[gout(f"w1_{l}")],
                                      epilogue=_ep_store)
        (dhm,) = _matmul(f"mlp{l}_dhm", X(dz), wv[f"w1_{l}"], S, D, F, tb=True, outs=[_fresh(S, D, F32)],
                         epilogue=_ep_store)
        return _rms_bwd(f"mlp{l}_norm_bwd", dhm, xin, mlp_g[l:l + 1], d, S, D)

    h0 = _rms_fwd("mix0_norm", x, mix_g[0:1], S, D)
    (u0,) = _matmul("lru_in", X(h0), wv["lru_in"], S, 2 * D, D, outs=[_fresh(S, 2 * D, F32)], epilogue=_ep_store)
    y, xc, r, ig, hs = _lru_fwd(u0, conv_w, conv_b, wr_bd, b_r, wi_bd, b_i, lam, S, D)
    (x1,) = _matmul("lru_out", X(y), wv["lru_out"], S, D, D, outs=[_fresh(S, D, F32)], epilogue=_ep_resid,
                    extras=[X(x)])
    hm0, z0, act0, x2 = mlp_fwd(0, x1)
    h1 = _rms_fwd("mix1_norm", x2, mix_g[1:2], S, D)
    (u1,) = _matmul("fox_in", wv["fox_in"], X(h1), NU, S, D, ta=True, tb=True, outs=[_fresh(NU, S, F32)],
                    epilogue=_ep_store)
    qat, kat, vat, ka, va, vt = _fox_prep(u1, b_f, qg, kg, S, D, tq)
    o, o32, lse = _attn_forward(ka, qat, vt, S, D, tq)
    (x3,) = _matmul("fox_out", X(o), wv["fox_out"], S, D, D, ta=True, outs=[_fresh(S, D, F32)], epilogue=_ep_resid,
                    extras=[X(x2)])
    hm1, z1, act1, x4 = mlp_fwd(1, x3)
    loss, d4, d4b = _loss_head(x4, tgt, S, D)

    d3, d3b, dg_mlp1 = mlp_bwd(1, x3, hm1, z1, act1, d4, d4b)
    (do,) = _matmul("fox_dout", wv["fox_out"], X(d3b), D, S, D, tb=True, outs=[_fresh(D, S, BF16)], epilogue=_ep_store)
    (grads["fox_out"],) = _matmul("fox_dwout", X(o), X(d3b), D, D, S, outs=[gout("fox_out")], epilogue=_ep_store)
    doat, doa, qat1, qa1 = _fox_bwd_prep(do, o32, lse, qat, S, D, tq)
    dqn, dcq = _attn_backward_q(ka, va, kat, qat1, doat, S, D, tq)
    dkn, dv, dck = _attn_backward_kv(qa1, doa, qat1, doat, kat, vat, S, D, tq)
    du1, dbf, dqg, dkg = _fox_prep_bwd(u1, dqn, dkn, dv, dcq, dck, b_f, qg, kg, S, D, tq)
    (grads["fox_in"],) = _matmul("fox_dwin", X(h1), X(du1), D, NU, S, ta=True, tb=True, outs=[gout("fox_in")],
                                 epilogue=_ep_store)
    (dh1,) = _matmul("fox_dh", X(du1), wv["fox_in"], S, D, NU, ta=True, tb=True, outs=[_fresh(S, D, F32)],
                     epilogue=_ep_store)
    d2, d2b, dg_mix1 = _rms_bwd("mix1_norm_bwd", dh1, x2, mix_g[1:2], d3, S, D)
    d1, d1b, dg_mlp0 = mlp_bwd(0, x1, hm0, z0, act0, d2, d2b)
    (dy,) = _matmul("lru_dout", X(d1b), wv["lru_out"], S, D, D, tb=True, outs=[_fresh(S, D, F32)], epilogue=_ep_store)
    (grads["lru_out"],) = _matmul("lru_dwout", X(y), X(d1b), D, D, S, ta=True, outs=[gout("lru_out")],
                                  epilogue=_ep_store)
    du0, dcw, dcb, dlam, dbr, dbi, dwr, dwi = _lru_bwd(dy, u0, xc, r, ig, hs, conv_w, wr_bd, wi_bd, lam, S, D)
    (grads["lru_in"],) = _matmul("lru_dwin", X(h0), X(du0), D, 2 * D, S, ta=True, outs=[gout("lru_in")],
                                 epilogue=_ep_store)
    (dh0,) = _matmul("lru_dh", X(du0), wv["lru_in"], S, D, 2 * D, tb=True, outs=[_fresh(S, D, F32)], epilogue=_ep_store)
    gx, _, dg_mix0 = _rms_bwd("mix0_norm_bwd", dh0, x, mix_g[0:1], d1, S, D)

    grads.update(
        mix_norm=jnp.concatenate([dg_mix0, dg_mix1], axis=0), mlp_norm=jnp.concatenate([dg_mlp0, dg_mlp1], axis=0),
        conv_w=dcw, lru_conv_b=dcb, lru_w_r=_block_diag_extract(dwr, nblk)[None], lru_b_r=dbr.reshape(1, nblk, -1),
        lru_w_i=_block_diag_extract(dwi, nblk)[None], lru_b_i=dbi.reshape(1, nblk, -1), lru_lambda=dlam,
        fox_b_f=dbf[:H].reshape(1, H), fox_q_gain=dqg.reshape(1, -1), fox_k_gain=dkg.reshape(1, -1))
    return loss, gx, grads


def _place():
    x, y, c = lax.axis_index("x"), lax.axis_index("y"), lax.axis_index("c")
    chips = [(1 - x, y), (x, 1 - y), (1 - x, 1 - y)]
    return x, y, c, 2 * x + y, chips


BOUNCE_BYTES = 1 << 20


def _bounce_shape(rows, cols, dtype):
    chunk = rows
    while chunk % 2 == 0 and chunk > 16 and chunk * cols * jnp.dtype(dtype).itemsize > BOUNCE_BYTES:
        chunk //= 2
    return pltpu.VMEM((2, chunk, cols), dtype)


def _bounce_copy(src, dst, buf, sem):
    chunk = buf.shape[1]
    n = src.shape[0] // chunk
    cin = lambda i: pltpu.make_async_copy(src.at[pl.ds(i * chunk, chunk)], buf.at[i % 2], sem.at[i % 2])
    cout = lambda i: pltpu.make_async_copy(buf.at[i % 2], dst.at[pl.ds(i * chunk, chunk)], sem.at[2 + i % 2])
    cin(0).start()
    for i in range(n):
        cin(i).wait()
        if i + 1 < n:
            if i >= 1:
                cout(i - 1).wait()
            cin(i + 1).start()
        cout(i).start()
    if n >= 2:
        cout(n - 2).wait()
    cout(n - 1).wait()


def _hbm_call(body, name, arrays, out_shape, n_dma_sems, bounce=()):
    scratch = [pltpu.SemaphoreType.DMA((k,)) for k in n_dma_sems]
    for rows, cols, dtype in bounce:
        scratch += [_bounce_shape(rows, cols, dtype), pltpu.SemaphoreType.DMA((4,))]
    return pl.pallas_call(
        body, name=name, in_specs=[ANY] * len(arrays), out_specs=[ANY] * len(out_shape), out_shape=out_shape,
        scratch_shapes=scratch,
        compiler_params=pltpu.CompilerParams(has_side_effects=True, vmem_limit_bytes=VMEM_LIMIT),
    )(*arrays)


def _all_gather(name, shards):
    n = len(shards)

    def body(*refs):
        ins, outs = refs[:n], refs[n:2 * n]
        send, recv, fsend, frecv = refs[2 * n:2 * n + 4]
        stage = refs[2 * n + 4:]
        x, y, c, s, chips = _place()
        sibling = (x, y, 1 - c)

        def rows(a, chip_idx, which):
            hr = ins[a].shape[0] // 2
            return outs[a].at[chip_idx, pl.ds(which * hr, hr)]

        def ici(a, j, src, dst, to):
            return pltpu.make_async_remote_copy(src_ref=src, dst_ref=dst, send_sem=send.at[3 * a + j],
                                                recv_sem=recv.at[3 * a + j], device_id=to, device_id_type=MESH)

        def d2d(a, j, src, dst):
            return pltpu.make_async_remote_copy(src_ref=src, dst_ref=dst, send_sem=fsend.at[3 * a + j],
                                                recv_sem=frecv.at[3 * a + j], device_id=sibling, device_id_type=MESH)

        started = []
        for a in range(n):
            hr = ins[a].shape[0] // 2
            for j, chip in enumerate(chips):
                cp = ici(a, j, ins[a].at[pl.ds(c * hr, hr)], rows(a, s, c), (*chip, c))
                cp.start()
                started.append(cp)
        for a in range(n):
            _bounce_copy(ins[a], outs[a].at[s], stage[2 * a], stage[2 * a + 1])
        for a in range(n):
            for j, chip in enumerate(chips):
                got = rows(a, 2 * chip[0] + chip[1], c)
                ici(a, j, got, got, (*chip, c)).wait_recv()
                fw = d2d(a, j, got, got)
                fw.start()
                started.append(fw)
        for a in range(n):
            for j, chip in enumerate(chips):
                theirs = rows(a, 2 * chip[0] + chip[1], 1 - c)
                d2d(a, j, theirs, theirs).wait_recv()
        for cp in started:
            cp.wait_send()

    out_shape = [jax.ShapeDtypeStruct((N_CHIPS,) + tuple(a.shape), a.dtype) for a in shards]
    return _hbm_call(body, name, shards, out_shape, (3 * n, 3 * n, 3 * n, 3 * n),
                     bounce=[(a.shape[0], a.shape[1], a.dtype) for a in shards])


def _pair_swap(name, arrs):
    n = len(arrs)

    def body(*refs):
        ins, outs = refs[:n], refs[n:2 * n]
        send, recv = refs[2 * n:]
        x, y, c, _, _ = _place()
        cps = []
        for a in range(n):
            hr = ins[a].shape[1] // 2
            cp = pltpu.make_async_remote_copy(
                src_ref=ins[a].at[:, pl.ds((1 - c) * hr, hr)], dst_ref=outs[a], send_sem=send.at[a],
                recv_sem=recv.at[a], device_id=(x, y, 1 - c), device_id_type=MESH)
            cp.start()
            cps.append(cp)
        for cp in cps:
            cp.wait()

    out_shape = [jax.ShapeDtypeStruct((a.shape[0], a.shape[1] // 2, a.shape[2]), a.dtype) for a in arrs]
    return _hbm_call(body, name, arrs, out_shape, (n, n))


def _chip_scatter(name, parts):
    n = len(parts)

    def body(*refs):
        ins, outs = refs[:n], refs[n:2 * n]
        send, recv = refs[2 * n:2 * n + 2]
        stage = refs[2 * n + 2:]
        x, y, c, s, chips = _place()
        cps = []
        for a in range(n):
            for j, chip in enumerate(chips):
                t = 2 * chip[0] + chip[1]
                cp = pltpu.make_async_remote_copy(
                    src_ref=ins[a].at[t], dst_ref=outs[a].at[s], send_sem=send.at[3 * a + j],
                    recv_sem=recv.at[3 * a + j], device_id=(*chip, c), device_id_type=MESH)
                cp.start()
                cps.append(cp)
        for a in range(n):
            _bounce_copy(ins[a].at[s], outs[a].at[s], stage[2 * a], stage[2 * a + 1])
        for a in range(n):
            for j, chip in enumerate(chips):
                t = 2 * chip[0] + chip[1]
                pltpu.make_async_remote_copy(
                    src_ref=ins[a].at[t], dst_ref=outs[a].at[t], send_sem=send.at[3 * a + j],
                    recv_sem=recv.at[3 * a + j], device_id=(*chip, c), device_id_type=MESH).wait_recv()
        for cp in cps:
            cp.wait_send()

    out_shape = [jax.ShapeDtypeStruct(a.shape, a.dtype) for a in parts]
    return _hbm_call(body, name, parts, out_shape, (3 * n, 3 * n),
                     bounce=[(a.shape[1], a.shape[2], a.dtype) for a in parts])


def _pair_gather(name, halves):
    n = len(halves)

    def body(*refs):
        ins, outs = refs[:n], refs[n:2 * n]
        send, recv = refs[2 * n:2 * n + 2]
        stage = refs[2 * n + 2:]
        x, y, c, _, _ = _place()
        cps = []
        for a in range(n):
            hr = ins[a].shape[0]
            cp = pltpu.make_async_remote_copy(
                src_ref=ins[a], dst_ref=outs[a].at[pl.ds(c * hr, hr)], send_sem=send.at[a], recv_sem=recv.at[a],
                device_id=(x, y, 1 - c), device_id_type=MESH)
            cp.start()
            cps.append((cp, hr))
        for a, (cp, hr) in enumerate(cps):
            _bounce_copy(ins[a], outs[a].at[pl.ds(c * hr, hr)], stage[2 * a], stage[2 * a + 1])
        for a, (cp, hr) in enumerate(cps):
            cp.wait_send()
            theirs = outs[a].at[pl.ds((1 - c) * hr, hr)]
            pltpu.make_async_remote_copy(src_ref=theirs, dst_ref=theirs, send_sem=send.at[a], recv_sem=recv.at[a],
                                         device_id=(x, y, 1 - c), device_id_type=MESH).wait_recv()

    out_shape = [jax.ShapeDtypeStruct((2 * a.shape[0], a.shape[1]), a.dtype) for a in halves]
    return _hbm_call(body, name, halves, out_shape, (n, n),
                     bounce=[(a.shape[0], a.shape[1], a.dtype) for a in halves])


def _row_tile(rows, cols, itemsize, n_bufs):
    budget = VMEM_LIMIT // 2
    for t in (1024, 512, 256, 128, 64, 32, 16):
        if rows % t == 0 and 2 * n_bufs * t * cols * itemsize <= budget:
            return t
    return rows


def _pair_add(name, g, gsib, core, out_dtype):
    _, r, cols = g.shape
    hr = r // 2
    t = _row_tile(hr, cols, 4, 3)
    per = hr // t

    def body(core_ref, a_ref, b_ref, o_ref):
        o_ref[...] = (a_ref[...].astype(F32) + b_ref[...].astype(F32)).astype(o_ref.dtype)

    grid_spec = pltpu.PrefetchScalarGridSpec(
        num_scalar_prefetch=1, grid=(N_CHIPS, per),
        in_specs=[pl.BlockSpec((None, t, cols), lambda s, i, core: (s, core[0] * per + i, 0)),
                  pl.BlockSpec((None, t, cols), lambda s, i, core: (s, i, 0))],
        out_specs=pl.BlockSpec((None, t, cols), lambda s, i, core: (s, i, 0)))
    return pl.pallas_call(body, name=name, grid_spec=grid_spec,
                          out_shape=jax.ShapeDtypeStruct((N_CHIPS, hr, cols), out_dtype),
                          compiler_params=_params(("arbitrary", "arbitrary")))(core, g, gsib)


def _chip_sum(name, parts):
    _, hr, cols = parts.shape
    t = _row_tile(hr, cols, 4, 5)

    def body(p_ref, o_ref):
        o_ref[...] = ((p_ref[0].astype(F32) + p_ref[1].astype(F32)) + p_ref[2].astype(F32)) + p_ref[3].astype(F32)

    return pl.pallas_call(
        body, name=name, grid=(hr // t,), in_specs=[pl.BlockSpec((N_CHIPS, t, cols), lambda i: (0, i, 0))],
        out_specs=pl.BlockSpec((t, cols), lambda i: (i, 0)), out_shape=jax.ShapeDtypeStruct((hr, cols), F32),
        compiler_params=_params(("arbitrary",)))(parts)


def _reduce_scatter(tag, arrs, wire_dtypes, core):
    sib = _pair_swap(f"{tag}_pair_swap", arrs)
    parts = [_pair_add(f"{tag}_pair_add{i}", g, gs, core, dt) for i, (g, gs, dt) in enumerate(zip(arrs, sib, wire_dtypes))]
    got = _chip_scatter(f"{tag}_chip_scatter", parts)
    halves = [_chip_sum(f"{tag}_chip_sum{i}", p) for i, p in enumerate(got)]
    return _pair_gather(f"{tag}_pair_gather", halves)


def _adamw(name, w, g, m, v):
    rows, cols = w.shape
    t = _row_tile(rows, cols, 4, 8)
    c1 = 1.0 - ADAM_B1 ** ADAM_STEP
    c2 = 1.0 - ADAM_B2 ** ADAM_STEP

    def body(w_ref, g_ref, m_ref, v_ref, go_ref, d_ref, nm_ref, nv_ref):
        g = g_ref[...]
        go_ref[...] = g
        m = ADAM_B1 * m_ref[...] + (1.0 - ADAM_B1) * g
        v = ADAM_B2 * v_ref[...] + (1.0 - ADAM_B2) * (g * g)
        nm_ref[...] = m
        nv_ref[...] = v
        d_ref[...] = -ADAM_LR * ((m / c1) / (jnp.sqrt(v / c2) + ADAM_EPS) + ADAM_WD * w_ref[...])

    spec = pl.BlockSpec((t, cols), lambda i: (i, 0))
    shp = jax.ShapeDtypeStruct((rows, cols), F32)
    return pl.pallas_call(body, name=name, grid=(rows // t,), in_specs=[spec] * 4, out_specs=[spec] * 4,
                          out_shape=[shp] * 4, compiler_params=_params(("arbitrary",)))(w, g, m, v)


_WEIGHTS = ["mix_norm", "mlp_norm", "mlp_w1", "mlp_w2", "lru_w_in", "lru_conv_w", "lru_conv_b", "lru_w_r", "lru_b_r",
            "lru_w_i", "lru_b_i", "lru_lambda", "lru_w_out", "fox_w_in", "fox_b_f", "fox_q_gain", "fox_k_gain",
            "fox_w_out"]
_REPLICATED = ["mix_norm", "mlp_norm", "lru_conv_b", "lru_w_r", "lru_b_r", "lru_w_i", "lru_b_i", "lru_lambda",
               "fox_b_f", "fox_q_gain", "fox_k_gain"]
_PACK_TILE = 2 * SUBLANES * LANES


def _as2d(a):
    return a.reshape(-1, a.shape[-1])


def kernel(x, mix_norm, mlp_norm, mlp_w1, mlp_w2, lru_w_in, lru_conv_w, lru_conv_b, lru_w_r, lru_b_r, lru_w_i, lru_b_i, lru_lambda, lru_w_out, fox_w_in, fox_b_f, fox_q_gain, fox_k_gain, fox_w_out, loss_target, m_mix_norm, m_mlp_norm, m_mlp_w1, m_mlp_w2, m_lru_w_in, m_lru_conv_w, m_lru_conv_b, m_lru_w_r, m_lru_b_r, m_lru_w_i, m_lru_b_i, m_lru_lambda, m_lru_w_out, m_fox_w_in, m_fox_b_f, m_fox_q_gain, m_fox_k_gain, m_fox_w_out, v_mix_norm, v_mlp_norm, v_mlp_w1, v_mlp_w2, v_lru_w_in, v_lru_conv_w, v_lru_conv_b, v_lru_w_r, v_lru_b_r, v_lru_w_i, v_lru_b_i, v_lru_lambda, v_lru_w_out, v_fox_w_in, v_fox_b_f, v_fox_q_gain, v_fox_k_gain, v_fox_w_out):
    args = dict(locals())
    W = {n: args[n] for n in _WEIGHTS}
    Mo = {n: args["m_" + n] for n in _WEIGHTS}
    Vo = {n: args["v_" + n] for n in _WEIGHTS}
    S, D = x.shape[1], x.shape[2]
    F = 4 * D
    H = D // HEAD_DIM
    NU = 3 * D + LANES
    FQ, DQ = F // N_CHIPS, D // N_CHIPS
    nfox = fox_w_in.shape[-1]
    chip = 2 * lax.axis_index("x") + lax.axis_index("y")
    core = lax.axis_index("c").astype(jnp.int32).reshape(1)

    cw_flat = jnp.pad(lru_conv_w.reshape(-1), (0, _PACK_TILE - CONV_WIDTH * DQ)).reshape(2 * SUBLANES, LANES)
    g_w1, g_w2, g_lin, g_lout, g_fin, g_fout = _all_gather(
        "gather_weights",
        [_as2d(mlp_w1).astype(BF16), _as2d(mlp_w2).astype(BF16), lru_w_in[0].astype(BF16), lru_w_out[0].astype(BF16),
         fox_w_in[0].astype(BF16), fox_w_out[0].astype(BF16)])
    (g_cw,) = _all_gather("gather_conv", [cw_flat])
    conv_w_full = jnp.transpose(g_cw.reshape(N_CHIPS, -1)[:, :CONV_WIDTH * DQ].reshape(N_CHIPS, CONV_WIDTH, DQ),
                                (1, 0, 2)).reshape(CONV_WIDTH, D)
    fox_full = jnp.concatenate([g_fin[s] for s in range(N_CHIPS)], axis=1)
    fox_full = jnp.pad(fox_full, ((0, 0), (0, NU - fox_full.shape[1])))
    wv = {"w1_0": _View(g_w1, "cs", 0, D), "w1_1": _View(g_w1, "cs", D, D),
          "w2_0": _View(g_w2, "rs", 0, FQ), "w2_1": _View(g_w2, "rs", FQ, FQ),
          "lru_in": _View(g_lin, "cs"), "lru_out": _View(g_lout, "rs"),
          "fox_in": _View(fox_full), "fox_out": _View(g_fout, "rs")}

    def grad_view(grads, name):
        if name in ("w1_0", "w1_1"):
            return _View(grads.get("w1_1"), "cs", D * int(name[-1]), D, shape=(N_CHIPS, 2 * D, FQ), dtype=BF16)
        if name in ("w2_0", "w2_1"):
            return _View(grads.get("w2_1"), "rs", FQ * int(name[-1]), FQ, shape=(N_CHIPS, 2 * FQ, D), dtype=BF16)
        if name == "lru_in":
            return _View(None, "cs", shape=(N_CHIPS, D, 2 * D // N_CHIPS), dtype=BF16)
        if name in ("lru_out", "fox_out"):
            return _View(None, "rs", shape=(N_CHIPS, DQ, D), dtype=BF16)
        return _View(None, shape=(D, NU), dtype=BF16)

    small = {n: W[n] for n in _REPLICATED}
    small["conv_w"] = conv_w_full

    loss, gx, grads = _local_step(x[0], loss_target[0], small, wv, grad_view)

    g_fox = jnp.transpose(grads["fox_in"][:, :nfox * N_CHIPS].reshape(D, N_CHIPS, nfox), (1, 0, 2))
    big = [grads["w1_0"], grads["w2_0"], grads["lru_in"], grads["lru_out"], g_fox, grads["fox_out"]]
    pack_names = _REPLICATED + ["conv_w"]
    flat = jnp.concatenate([grads[n].reshape(-1).astype(F32) for n in pack_names])
    per_chip = -(-flat.shape[0] // (N_CHIPS * _PACK_TILE)) * _PACK_TILE
    pack = jnp.pad(flat, (0, N_CHIPS * per_chip - flat.shape[0])).reshape(N_CHIPS, per_chip // LANES, LANES)
    red = _reduce_scatter("grads", big + [pack], [BF16] * len(big) + [F32], core)
    r_w1, r_w2, r_lin, r_lout, r_fin, r_fout, r_pack = red
    (all_pack,) = _all_gather("gather_small_grads", [r_pack])
    all_flat = all_pack.reshape(-1)
    G = {}
    off = 0
    for n in pack_names:
        shape = grads[n].shape if n == "conv_w" else W[n].shape
        size = int(np.prod(shape))
        G[n] = all_flat[off:off + size].reshape(shape)
        off += size
    G["lru_conv_w"] = lax.dynamic_slice_in_dim(G.pop("conv_w"), chip * DQ, DQ, axis=1)[None]
    G.update(mlp_w1=r_w1.reshape(mlp_w1.shape), mlp_w2=r_w2.reshape(mlp_w2.shape), lru_w_in=r_lin[None],
             lru_w_out=r_lout[None], fox_w_in=r_fin[None], fox_w_out=r_fout[None])

    delta, new_m, new_v = {}, {}, {}
    for n in _WEIGHTS:
        go, d, nm, nv = _adamw(f"adamw_{n}", _as2d(W[n]), _as2d(G[n]), _as2d(Mo[n]), _as2d(Vo[n]))
        G[n], delta[n], new_m[n], new_v[n] = (t.reshape(W[n].shape) for t in (go, d, nm, nv))

    total = lax.psum(loss[0, 0], ("x", "y", "c"))
    return (total, gx[None], *[G[n] for n in _WEIGHTS], *[delta[n] for n in _WEIGHTS],
            *[new_m[n] for n in _WEIGHTS], *[new_v[n] for n in _WEIGHTS])
```

```python
import functools

import numpy as np
import jax
import jax.numpy as jnp
from jax import lax
from jax.experimental import pallas as pl
from jax.experimental.pallas import tpu as pltpu

F32 = jnp.float32
BF16 = jnp.bfloat16

HEAD_DIM = 64
LRU_BLOCK_DIM = 64
CONV_WIDTH = 4
LRU_C = 8.0
EPS = 1e-6
NEG_INF = -1e30
ADAM_LR = 0.001
ADAM_B1 = 0.9
ADAM_B2 = 0.999
ADAM_EPS = 1e-08
ADAM_WD = 0.01
ADAM_STEP = 10

N_CHIPS = 4
LANES = 128
SUBLANES = 8
MXU_DIM = 256
VMEM_LIMIT = 52 * 1024 * 1024
MESH = pl.DeviceIdType.MESH
ANY = pl.BlockSpec(memory_space=pl.ANY)


def _pick(n, prefs):
    for p in prefs:
        if p <= n and n % p == 0:
            return p
    return n


def _params(sem=None):
    return pltpu.CompilerParams(dimension_semantics=sem, vmem_limit_bytes=VMEM_LIMIT)


class _View:
    def __init__(self, arr, kind="plain", r0=0, rows=None, shape=None, dtype=None):
        self.arr = arr
        self.kind = kind
        self.r0 = r0
        self.shape = tuple(arr.shape) if arr is not None else tuple(shape)
        self.dtype = arr.dtype if arr is not None else dtype
        self.rows = rows if rows is not None else self.shape[-2]

    def limits(self):
        if self.kind == "plain":
            return 0, 0
        rows = int(np.gcd(self.rows, self.r0))
        return rows, (self.shape[-1] if self.kind == "cs" else 0)

    def spec(self, br, bc, fr, fc):
        if self.kind == "plain":
            return pl.BlockSpec((br, bc), lambda *g: (fr(*g), fc(*g)))
        ncol = self.shape[-1]
        r0b = self.r0 // br
        assert self.r0 % br == 0 and self.rows % br == 0 and ncol % bc == 0, (self.shape, self.r0, br, bc)
        if self.kind == "cs":
            per = ncol // bc
            return pl.BlockSpec((None, br, bc), lambda *g: (fc(*g) // per, r0b + fr(*g), fc(*g) % per))
        per = self.rows // br
        return pl.BlockSpec((None, br, bc), lambda *g: (fr(*g) // per, r0b + fr(*g) % per, fc(*g)))


def _bf(x):
    return x if x.dtype == BF16 else x.astype(BF16)


def _matmul(name, A, B, M, N, K, *, ta=False, tb=False, outs, epilogue, extras=(), tm=None, tn=None, tk=None):
    lim = {"m": [M], "n": [N], "k": [K]}
    for view, (rdim, cdim) in ([(A, "km" if ta else "mk"), (B, "nk" if tb else "kn")]
                               + [(e, "mn") for e in extras] + [(o, "mn") for o in outs]):
        r_lim, c_lim = view.limits()
        lim[rdim].append(r_lim)
        lim[cdim].append(c_lim)
    tm = tm or _pick(int(np.gcd.reduce(lim["m"])), (1024, 512, 256, 128))
    tn = tn or _pick(int(np.gcd.reduce(lim["n"])), (1024, 640, 512, 256, 128))
    tk = tk or _pick(int(np.gcd.reduce(lim["k"])), (1024, 640, 512, 256, 128))
    nk = K // tk
    gi = lambda i, j, k: i
    gj = lambda i, j, k: j
    gk = lambda i, j, k: k
    a_spec = A.spec(tk, tm, gk, gi) if ta else A.spec(tm, tk, gi, gk)
    b_spec = B.spec(tn, tk, gj, gk) if tb else B.spec(tk, tn, gk, gj)
    ca = 0 if ta else 1
    cb = 1 if tb else 0
    ne, no = len(extras), len(outs)
    in_specs = [a_spec, b_spec] + [e.spec(tm, tn, gi, gj) for e in extras]
    operands = [A.arr, B.arr] + [e.arr for e in extras]
    aliases = {}
    for oi, o in enumerate(outs):
        if o.arr is not None:
            aliases[len(operands)] = oi
            in_specs.append(ANY)
            operands.append(o.arr)
    nalias = len(aliases)
    out_specs = [o.spec(tm, tn, gi, gj) for o in outs]
    out_shape = [jax.ShapeDtypeStruct(o.shape, o.dtype) for o in outs]

    def body(*refs):
        a_ref, b_ref = refs[0], refs[1]
        ex = refs[2:2 + ne]
        o_refs = refs[2 + ne + nalias:2 + ne + nalias + no]

        def prod():
            return lax.dot_general(_bf(a_ref[...]), _bf(b_ref[...]), (((ca,), (cb,)), ((), ())),
                                   preferred_element_type=F32)

        def finish(acc):
            res = epilogue(acc, *[e[...] for e in ex])
            for o_ref, r in zip(o_refs, res):
                o_ref[...] = r.astype(o_ref.dtype)

        if nk == 1:
            finish(prod())
        else:
            acc_ref = refs[-1]
            k = pl.program_id(2)

            @pl.when(k == 0)
            def _():
                acc_ref[...] = jnp.zeros_like(acc_ref)

            acc_ref[...] += prod()

            @pl.when(k == nk - 1)
            def _():
                finish(acc_ref[...])

    res = pl.pallas_call(
        body, name=name, grid=(M // tm, N // tn, nk), in_specs=in_specs, out_specs=out_specs, out_shape=out_shape,
        scratch_shapes=[pltpu.VMEM((tm, tn), F32)] if nk > 1 else [],
        input_output_aliases=aliases,
        compiler_params=_params(("parallel", "parallel", "arbitrary")),
    )(*operands)
    return res


def _ep_store(acc):
    return (acc,)


def _ep_resid(acc, res):
    return (res + acc,)


def _ep_relu2(acc):
    zp = jnp.maximum(acc, 0.0)
    return (acc, zp * zp)


def _ep_drelu2(acc, z):
    return (acc * (2.0 * jnp.maximum(z.astype(F32), 0.0)),)


def _fresh(M, N, dtype):
    return _View(None, shape=(M, N), dtype=dtype)


def _rms_fwd(name, x, g, S, D):
    T = _pick(S, (512, 256, 128))

    def body(x_ref, g_ref, h_ref):
        x = x_ref[...]
        r = lax.rsqrt(jnp.mean(x * x, axis=-1, keepdims=True) + EPS)
        h_ref[...] = ((x * r) * g_ref[...]).astype(BF16)

    return pl.pallas_call(
        body, name=name, grid=(S // T,),
        in_specs=[pl.BlockSpec((T, D), lambda i: (i, 0)), pl.BlockSpec((1, D), lambda i: (0, 0))],
        out_specs=pl.BlockSpec((T, D), lambda i: (i, 0)),
        out_shape=jax.ShapeDtypeStruct((S, D), BF16),
        compiler_params=_params(("arbitrary",)),
    )(x, g)


def _rms_bwd(name, dh, x, g, dres, S, D):
    T = _pick(S, (512, 256, 128))

    def body(dh_ref, x_ref, g_ref, dres_ref, dx_ref, dxb_ref, dg_ref):
        @pl.when(pl.program_id(0) == 0)
        def _():
            dg_ref[...] = jnp.zeros_like(dg_ref)

        x = x_ref[...]
        dh = dh_ref[...]
        r = lax.rsqrt(jnp.mean(x * x, axis=-1, keepdims=True) + EPS)
        xhat = x * r
        dg_ref[...] += jnp.sum(dh * xhat, axis=0, keepdims=True)
        dxn = dh * g_ref[...]
        dx = r * (dxn - xhat * jnp.mean(dxn * xhat, axis=-1, keepdims=True))
        tot = dres_ref[...] + dx
        dx_ref[...] = tot
        dxb_ref[...] = tot.astype(BF16)

    row = pl.BlockSpec((T, D), lambda i: (i, 0))
    vec = pl.BlockSpec((1, D), lambda i: (0, 0))
    return pl.pallas_call(
        body, name=name, grid=(S // T,), in_specs=[row, row, vec, row], out_specs=[row, row, vec],
        out_shape=[jax.ShapeDtypeStruct((S, D), F32), jax.ShapeDtypeStruct((S, D), BF16),
                   jax.ShapeDtypeStruct((1, D), F32)],
        compiler_params=_params(("arbitrary",)),
    )(dh, x, g, dres)


def _loss_head(x, tgt, S, D):
    T = _pick(S, (512, 256, 128))

    def body(x_ref, t_ref, loss_ref, d_ref, db_ref):
        @pl.when(pl.program_id(0) == 0)
        def _():
            loss_ref[...] = jnp.zeros_like(loss_ref)

        e = x_ref[...] - t_ref[...]
        loss_ref[...] += 0.5 * jnp.sum(jnp.mean(e * e, axis=-1, keepdims=True), axis=0, keepdims=True)
        d = e * (1.0 / D)
        d_ref[...] = d
        db_ref[...] = d.astype(BF16)

    row = pl.BlockSpec((T, D), lambda i: (i, 0))
    return pl.pallas_call(
        body, name="loss_head", grid=(S // T,), in_specs=[row, row],
        out_specs=[pl.BlockSpec((1, 1), lambda i: (0, 0)), row, row],
        out_shape=[jax.ShapeDtypeStruct((1, 1), F32), jax.ShapeDtypeStruct((S, D), F32),
                   jax.ShapeDtypeStruct((S, D), BF16)],
        compiler_params=_params(("arbitrary",)),
    )(x, tgt)


def _sigmoid(z):
    return 1.0 / (1.0 + jnp.exp(-z))


def _log_sigmoid(z):
    return jnp.minimum(z, 0.0) - jnp.log(1.0 + jnp.exp(-jnp.abs(z)))


_GELU_K = 0.7978845608028654
_GELU_C = 0.044715


def _gelu(x):
    t = jnp.tanh(_GELU_K * (x + _GELU_C * (x * x * x)))
    return 0.5 * x * (1.0 + t)


def _gelu_and_grad(x):
    x2 = x * x
    t = jnp.tanh(_GELU_K * (x + _GELU_C * (x2 * x)))
    g = 0.5 * x * (1.0 + t)
    dg = 0.5 * (1.0 + t) + 0.5 * x * (1.0 - t * t) * (_GELU_K * (1.0 + 3.0 * _GELU_C * x2))
    return g, dg


def _decay_terms(r, ls):
    la = LRU_C * r * ls
    a = jnp.exp(la)
    a2 = jnp.exp(2.0 * la)
    mult = jnp.sqrt(-jnp.tanh(la) * (a2 + 1.0))
    return a, a2, mult


def _lru_fwd(u0, conv_w, conv_b, wr_bd, b_r, wi_bd, b_i, lam, S, D):
    T = _pick(S, (256, 128))
    GT = wr_bd.shape[-1]
    nG = D // GT

    def body(gb_ref, xb_ref, cw_ref, cb_ref, wr_ref, br_ref, wi_ref, bi_ref, lam_ref,
             y_ref, xc_ref, r_ref, i_ref, hs_ref, ext, a_scr, hcar):
        @pl.when(pl.program_id(0) == 0)
        def _():
            ext[0:SUBLANES, :] = jnp.zeros((SUBLANES, D), F32)
            hcar[...] = jnp.zeros_like(hcar)

        xb = xb_ref[...]
        ext[SUBLANES:SUBLANES + T, :] = xb
        xc = cb_ref[...]
        for k in range(CONV_WIDTH):
            xc = xc + ext[pl.ds(SUBLANES - (CONV_WIDTH - 1) + k, T), :] * cw_ref[k:k + 1, :]
        ext[0:SUBLANES, :] = xb[T - SUBLANES:T, :]
        xc_ref[...] = xc
        xcb = xc.astype(BF16)
        for g in range(nG):
            sl = slice(g * GT, (g + 1) * GT)
            zr = jnp.dot(xcb[:, sl], wr_ref[g], preferred_element_type=F32) + br_ref[:, sl]
            zi = jnp.dot(xcb[:, sl], wi_ref[g], preferred_element_type=F32) + bi_ref[:, sl]
            r_ref[:, sl] = _sigmoid(zr)
            i_ref[:, sl] = _sigmoid(zi)
        r = r_ref[...]
        a, _, mult = _decay_terms(r, _log_sigmoid(lam_ref[...]))
        a_scr[...] = a
        hs_ref[...] = mult * (i_ref[...] * xc)

        def step(t, h):
            h = a_scr[pl.ds(t, 1), :] * h + hs_ref[pl.ds(t, 1), :]
            hs_ref[pl.ds(t, 1), :] = h
            return h

        hcar[...] = lax.fori_loop(0, T, step, hcar[...], unroll=8)
        y_ref[...] = (_gelu(gb_ref[...]) * hs_ref[...]).astype(BF16)

    row = pl.BlockSpec((T, D), lambda i: (i, 0))
    vec = pl.BlockSpec((1, D), lambda i: (0, 0))
    bd = pl.BlockSpec((nG, GT, GT), lambda i: (0, 0, 0))
    f32o = jax.ShapeDtypeStruct((S, D), F32)
    return pl.pallas_call(
        body, name="lru_fwd", grid=(S // T,),
        in_specs=[row, pl.BlockSpec((T, D), lambda i: (i, 1)), pl.BlockSpec((CONV_WIDTH, D), lambda i: (0, 0)), vec,
                  bd, vec, bd, vec, vec],
        out_specs=[row, row, row, row, row],
        out_shape=[jax.ShapeDtypeStruct((S, D), BF16), f32o, f32o, f32o, f32o],
        scratch_shapes=[pltpu.VMEM((T + SUBLANES, D), F32), pltpu.VMEM((T, D), F32), pltpu.VMEM((1, D), F32)],
        compiler_params=_params(("arbitrary",)),
    )(u0, u0, conv_w, conv_b, wr_bd, b_r, wi_bd, b_i, lam)


def _lru_bwd(dy, u0, xc, r, ig, hs, conv_w, wr_bd, wi_bd, lam, S, D):
    T = _pick(S, (128,))
    nT = S // T
    GT = wr_bd.shape[-1]
    nG = D // GT
    W = CONV_WIDTH

    def body(dy_ref, gb_ref, xb_ref, xbp_ref, xc_ref, r_ref, i_ref, hs_ref, hsp_ref, cw_ref, wr_ref, wi_ref, lam_ref,
             du_ref, dcw_ref, dcb_ref, dlam_ref, dbr_ref, dbi_ref, dwr_ref, dwi_ref,
             a_scr, dh_scr, exth, extx, extd, dxc_scr, dz_scr, carry):
        step = pl.program_id(0)
        first_tile = step == nT - 1

        @pl.when(step == 0)
        def _():
            for ref in (dcw_ref, dcb_ref, dlam_ref, dbr_ref, dbi_ref, dwr_ref, dwi_ref, carry):
                ref[...] = jnp.zeros_like(ref)
            extd[T:T + SUBLANES, :] = jnp.zeros((SUBLANES, D), F32)

        hs = hs_ref[...]
        dy = dy_ref[...]
        g, dgelu = _gelu_and_grad(gb_ref[...])
        du_ref[:, 0:D] = (dy * hs * dgelu).astype(BF16)
        r = r_ref[...]
        lam = lam_ref[...]
        ls = _log_sigmoid(lam)
        a, a2, mult = _decay_terms(r, ls)
        a_scr[...] = a
        dh_scr[...] = dy * g

        def rstep(j, c):
            t = T - 1 - j
            d = dh_scr[pl.ds(t, 1), :] + c
            dh_scr[pl.ds(t, 1), :] = d
            return a_scr[pl.ds(t, 1), :] * d

        carry[...] = lax.fori_loop(0, T, rstep, carry[...], unroll=8)
        dh = dh_scr[...]
        keep = jnp.where(first_tile, 0.0, 1.0)
        exth[0:SUBLANES, :] = hsp_ref[...] * keep
        exth[SUBLANES:SUBLANES + T, :] = hs
        hprev = exth[pl.ds(SUBLANES - 1, T), :]
        xc = xc_ref[...]
        ig = i_ref[...]
        da = dh * hprev
        dmult = dh * (ig * xc)
        dla = da * a - dmult * (a2 / mult)
        dlam_ref[...] += jnp.sum(dla * r, axis=0, keepdims=True) * (LRU_C * _sigmoid(-lam))
        dzr = (dla * (LRU_C * ls)) * (r * (1.0 - r))
        dzi = (dh * (mult * xc)) * (ig * (1.0 - ig))
        dbr_ref[...] += jnp.sum(dzr, axis=0, keepdims=True)
        dbi_ref[...] += jnp.sum(dzi, axis=0, keepdims=True)
        dxc_scr[...] = dh * (mult * ig)
        xcb = xc.astype(BF16)
        dz_scr[0] = dzr.astype(BF16)
        dz_scr[1] = dzi.astype(BF16)
        nt_dims = (((1,), (1,)), ((), ()))
        tn_dims = (((0,), (0,)), ((), ()))
        for gq in range(nG):
            sl = slice(gq * GT, (gq + 1) * GT)
            zr_g = dz_scr[0, :, sl]
            zi_g = dz_scr[1, :, sl]
            dxc_scr[:, sl] += (lax.dot_general(zr_g, wr_ref[gq], nt_dims, preferred_element_type=F32)
                               + lax.dot_general(zi_g, wi_ref[gq], nt_dims, preferred_element_type=F32))
            dwr_ref[gq] += lax.dot_general(xcb[:, sl], zr_g, tn_dims, preferred_element_type=F32)
            dwi_ref[gq] += lax.dot_general(xcb[:, sl], zi_g, tn_dims, preferred_element_type=F32)
        dxc = dxc_scr[...]
        dcb_ref[...] += jnp.sum(dxc, axis=0, keepdims=True)
        extx[0:SUBLANES, :] = xbp_ref[...] * keep
        extx[SUBLANES:SUBLANES + T, :] = xb_ref[...]
        extd[0:T, :] = dxc
        dxb = jnp.zeros((T, D), F32)
        for k in range(W):
            dxb = dxb + extd[pl.ds(W - 1 - k, T), :] * cw_ref[k:k + 1, :]
            dcw_ref[k:k + 1, :] += jnp.sum(dxc * extx[pl.ds(SUBLANES - (W - 1) + k, T), :], axis=0, keepdims=True)
        extd[T:T + SUBLANES, :] = dxc[0:SUBLANES, :]
        du_ref[:, D:2 * D] = dxb.astype(BF16)

    rev = lambda i: nT - 1 - i
    tpb = T // SUBLANES
    prev8 = lambda i: jnp.maximum(rev(i) * tpb - 1, 0)
    row = pl.BlockSpec((T, D), lambda i: (rev(i), 0))
    vec = pl.BlockSpec((1, D), lambda i: (0, 0))
    bd = pl.BlockSpec((nG, GT, GT), lambda i: (0, 0, 0))
    vec_o = jax.ShapeDtypeStruct((1, D), F32)
    bd_o = jax.ShapeDtypeStruct((nG, GT, GT), F32)
    return pl.pallas_call(
        body, name="lru_bwd", grid=(nT,),
        in_specs=[row, row, pl.BlockSpec((T, D), lambda i: (rev(i), 1)),
                  pl.BlockSpec((SUBLANES, D), lambda i: (prev8(i), 1)),
                  row, row, row, row, pl.BlockSpec((SUBLANES, D), lambda i: (prev8(i), 0)),
                  pl.BlockSpec((W, D), lambda i: (0, 0)), bd, bd, vec],
        out_specs=[pl.BlockSpec((T, 2 * D), lambda i: (rev(i), 0)), pl.BlockSpec((W, D), lambda i: (0, 0)),
                   vec, vec, vec, vec, bd, bd],
        out_shape=[jax.ShapeDtypeStruct((S, 2 * D), BF16), jax.ShapeDtypeStruct((W, D), F32),
                   vec_o, vec_o, vec_o, vec_o, bd_o, bd_o],
        scratch_shapes=[pltpu.VMEM((T, D), F32), pltpu.VMEM((T, D), F32), pltpu.VMEM((T + SUBLANES, D), F32),
                        pltpu.VMEM((T + SUBLANES, D), F32), pltpu.VMEM((T + SUBLANES, D), F32),
                        pltpu.VMEM((T, D), F32), pltpu.VMEM((2, T, D), BF16), pltpu.VMEM((1, D), F32)],
        compiler_params=_params(("arbitrary",)),
    )(dy, u0, u0, u0, xc, r, ig, hs, hs, conv_w, wr_bd, wi_bd, lam)


AUG_ROWS = 16
HEAD_ROWS = 128
LSE_ROW = HEAD_DIM + 6


def _split3(x):
    b1 = x.astype(BF16).astype(F32)
    r = x - b1
    b2 = r.astype(BF16).astype(F32)
    return b1, b2, r - b2


def _head_block(x, aug, T):
    row = lax.broadcasted_iota(jnp.int32, (AUG_ROWS, T), 0)
    blk = jnp.zeros((AUG_ROWS, T), F32)
    for i, e in enumerate(aug):
        blk = jnp.where(row == i, e, blk)
    return jnp.concatenate([x, blk, jnp.zeros((HEAD_ROWS - HEAD_DIM - AUG_ROWS, T), F32)], axis=0)


def _tri_matrix(lower):
    i = np.arange(LANES)
    m = (i[:, None] >= i[None, :]) if lower else (i[:, None] <= i[None, :])
    return jnp.asarray(m.astype(np.float32), BF16)


def _lane_cumsum(x, tri_ref, carry, reverse):
    n = x.shape[1] // LANES
    tri = tri_ref[...]
    out = [None] * n
    for j in (range(n - 1, -1, -1) if reverse else range(n)):
        cs = carry
        for part in _split3(x[:, j * LANES:(j + 1) * LANES]):
            cs = cs + jnp.dot(part.astype(BF16), tri, preferred_element_type=F32)
        out[j] = cs
        carry = cs[:, 0:1] if reverse else cs[:, LANES - 1:LANES]
    return jnp.concatenate(out, axis=1), carry


def _head_rows(h):
    return pl.ds(pl.multiple_of(h * HEAD_DIM, HEAD_DIM), HEAD_DIM)


def _fox_prep(ut, b_f, qg, kg, S, D, tq):
    H = D // HEAD_DIM
    T = min(tq, 256)
    per = tq // T
    scale = HEAD_DIM ** -0.5

    def body(q_ref, k_ref, v_ref, f_ref, bf_ref, qg_ref, kg_ref, tri_ref,
             qat_ref, kat_ref, vat_ref, ka_ref, va_ref, vt_ref, c_scr, ccar):
        @pl.when(pl.program_id(0) == 0)
        def _():
            ccar[...] = jnp.zeros_like(ccar)

        c, carry = _lane_cumsum(_log_sigmoid(f_ref[...] + bf_ref[...]), tri_ref, ccar[...], False)
        c_scr[...] = c
        ccar[...] = carry

        def head(h, _):
            rows = _head_rows(h)
            c1, c2, c3 = _split3(c_scr[pl.ds(h, 1), :])

            def normed(src, gain, mul):
                x = src[rows, :]
                rs = lax.rsqrt(jnp.mean(x * x, axis=0, keepdims=True) + EPS)
                return ((x * rs) * gain[rows, :]) * mul

            qat_ref[h] = _head_block(normed(q_ref, qg_ref, scale), [c1, c2, c3, 1.0, 1.0, 1.0], T).astype(BF16)
            kb = _head_block(normed(k_ref, kg_ref, 1.0), [1.0, 1.0, 1.0, -c1, -c2, -c3, 1.0, 1.0, 1.0], T)
            kat_ref[h] = kb.astype(BF16)
            ka_ref[h] = kb.T.astype(BF16)
            v = v_ref[rows, :]
            vt_ref[h] = v.astype(BF16)
            vb = _head_block(v, [1.0, 1.0, 1.0], T)
            vat_ref[h] = vb.astype(BF16)
            va_ref[h] = vb.T.astype(BF16)
            return 0

        lax.fori_loop(0, H, head, 0)

    part = lambda j: pl.BlockSpec((D, T), lambda i: (j, i))
    colv = lambda n: pl.BlockSpec((n, 1), lambda i: (0, 0))
    tmaj = lambda r: pl.BlockSpec((H, None, r, T), lambda i: (0, i // per, 0, i % per))
    norm = pl.BlockSpec((H, T, HEAD_ROWS), lambda i: (0, i, 0))
    tshape = lambda r: jax.ShapeDtypeStruct((H, S // tq, r, tq), BF16)
    nshape = jax.ShapeDtypeStruct((H, S, HEAD_ROWS), BF16)
    return pl.pallas_call(
        body, name="fox_prep", grid=(S // T,),
        in_specs=[part(0), part(1), part(2), pl.BlockSpec((LANES, T), lambda i: (3 * D // LANES, i)),
                  colv(LANES), colv(D), colv(D), pl.BlockSpec((LANES, LANES), lambda i: (0, 0))],
        out_specs=[tmaj(HEAD_ROWS), tmaj(HEAD_ROWS), tmaj(HEAD_ROWS), norm, norm, tmaj(HEAD_DIM)],
        out_shape=[tshape(HEAD_ROWS), tshape(HEAD_ROWS), tshape(HEAD_ROWS), nshape, nshape, tshape(HEAD_DIM)],
        scratch_shapes=[pltpu.VMEM((LANES, T), F32), pltpu.VMEM((LANES, 1), F32)],
        compiler_params=_params(("arbitrary",)),
    )(ut, ut, ut, ut, b_f, qg, kg, _tri_matrix(False))


def _fox_bwd_prep(dot, ot, lse, qat, S, D, tq):
    H = D // HEAD_DIM
    T = min(tq, 256)
    per = tq // T

    def body(do_ref, o_ref, lse_ref, qat_ref, doat_ref, doa_ref, qat1_ref, qa1_ref):
        row = lax.broadcasted_iota(jnp.int32, (HEAD_ROWS, T), 0)

        def head(h, _):
            rows = _head_rows(h)
            do = do_ref[rows, :].astype(F32)
            delta = jnp.sum(do * o_ref[rows, :], axis=0, keepdims=True)
            db = _head_block(do, list(_split3(-delta)), T)
            doat_ref[h] = db.astype(BF16)
            doa_ref[h] = db.T.astype(BF16)
            qb = qat_ref[h].astype(F32)
            for i, e in enumerate(_split3(-lse_ref[h])):
                qb = jnp.where(row == LSE_ROW + i, e, qb)
            qat1_ref[h] = qb.astype(BF16)
            qa1_ref[h] = qb.T.astype(BF16)
            return 0

        lax.fori_loop(0, H, head, 0)

    chan = pl.BlockSpec((D, T), lambda i: (0, i))
    tmaj = pl.BlockSpec((H, None, HEAD_ROWS, T), lambda i: (0, i // per, 0, i % per))
    norm = pl.BlockSpec((H, T, HEAD_ROWS), lambda i: (0, i, 0))
    tshape = jax.ShapeDtypeStruct((H, S // tq, HEAD_ROWS, tq), BF16)
    nshape = jax.ShapeDtypeStruct((H, S, HEAD_ROWS), BF16)
    return pl.pallas_call(
        body, name="fox_bwd_prep", grid=(S // T,),
        in_specs=[chan, chan, pl.BlockSpec((H, 1, T), lambda i: (0, 0, i)), tmaj],
        out_specs=[tmaj, norm, tmaj, norm], out_shape=[tshape, nshape, tshape, nshape],
        compiler_params=_params(("arbitrary",)),
    )(dot, ot, lse, qat)


def _causal(s, k_axis):
    ki = lax.broadcasted_iota(jnp.int32, s.shape, k_axis)
    qi = lax.broadcasted_iota(jnp.int32, s.shape, 1 - k_axis)
    return jnp.where(ki <= qi, s, NEG_INF)


def _seq_tile(i, t):
    return pl.ds(pl.multiple_of(i * t, t), t)


def _attn_forward(ka, qat, vt, S, D, tq):
    H = D // HEAD_DIM
    nq = S // tq

    def body(ka_ref, qat_ref, vt_ref, o_ref, o32_ref, lse_ref, m_scr, l_scr, acc_scr):
        qi = pl.program_id(1)
        m_scr[...] = jnp.full_like(m_scr, NEG_INF)
        l_scr[...] = jnp.zeros_like(l_scr)
        acc_scr[...] = jnp.zeros_like(acc_scr)
        qa = qat_ref[...]

        def logits(ki):
            return jnp.dot(ka_ref[_seq_tile(ki, tq), :], qa, preferred_element_type=F32)

        def consume(s, ki):
            m_prev = m_scr[...]
            m_new = jnp.maximum(m_prev, jnp.max(s, axis=0, keepdims=True))
            alpha = jnp.exp(m_prev - m_new)
            p = jnp.exp(s - m_new)
            l_scr[...] = alpha * l_scr[...] + jnp.sum(p, axis=0, keepdims=True)
            acc_scr[...] = alpha * acc_scr[...] + jnp.dot(vt_ref[ki], p.astype(BF16), preferred_element_type=F32)
            m_scr[...] = m_new

        def step(ki, s):
            s_next = logits(ki + 1)
            consume(s, ki)
            return s_next

        consume(_causal(lax.fori_loop(0, qi, step, logits(0)), 0), qi)
        o = acc_scr[...] / l_scr[...]
        o_ref[...] = o.astype(BF16)
        o32_ref[...] = o
        lse_ref[...] = m_scr[...] + jnp.log(l_scr[...])

    chan = pl.BlockSpec((HEAD_DIM, tq), lambda h, i: (h, i))
    stat = pl.BlockSpec((None, 1, tq), lambda h, i: (h, 0, i))
    return pl.pallas_call(
        body, name="attn_forward", grid=(H, nq),
        in_specs=[pl.BlockSpec((None, S, HEAD_ROWS), lambda h, i: (h, 0, 0)),
                  pl.BlockSpec((None, None, HEAD_ROWS, tq), lambda h, i: (h, i, 0, 0)),
                  pl.BlockSpec((None, nq, HEAD_DIM, tq), lambda h, i: (h, 0, 0, 0))],
        out_specs=[chan, chan, stat],
        out_shape=[jax.ShapeDtypeStruct((D, S), BF16), jax.ShapeDtypeStruct((D, S), F32),
                   jax.ShapeDtypeStruct((H, 1, S), F32)],
        scratch_shapes=[pltpu.VMEM((1, tq), F32), pltpu.VMEM((1, tq), F32), pltpu.VMEM((HEAD_DIM, tq), F32)],
        compiler_params=_params(("arbitrary", "arbitrary")),
    )(ka, qat, vt)


def _attn_backward_q(ka, va, kat, qat, doat, S, D, tq):
    H = D // HEAD_DIM
    nq = S // tq

    def body(ka_ref, va_ref, kat_ref, qat_ref, doat_ref, dq_ref, dcq_ref, dq_scr, rs_scr):
        qi = pl.program_id(1)
        dq_scr[...] = jnp.zeros_like(dq_scr)
        rs_scr[...] = jnp.zeros_like(rs_scr)
        qa = qat_ref[...]
        doa = doat_ref[...]

        def products(ki):
            rows = _seq_tile(ki, tq)
            return (jnp.dot(ka_ref[rows, :], qa, preferred_element_type=F32),
                    jnp.dot(va_ref[rows, :], doa, preferred_element_type=F32))

        def consume(s, dpd, ki):
            ds = jnp.exp(s) * dpd
            rs_scr[...] += jnp.sum(ds, axis=0, keepdims=True)
            dq_scr[...] += jnp.dot(kat_ref[ki, 0:HEAD_DIM, :], ds.astype(BF16), preferred_element_type=F32)

        def step(ki, carry):
            nxt = products(ki + 1)
            consume(*carry, ki)
            return nxt

        s, dpd = lax.fori_loop(0, qi, step, products(0))
        consume(_causal(s, 0), dpd, qi)
        dq_ref[...] = dq_scr[...]
        dcq_ref[...] = rs_scr[...]

    whole = pl.BlockSpec((None, S, HEAD_ROWS), lambda h, i: (h, 0, 0))
    one = pl.BlockSpec((None, None, HEAD_ROWS, tq), lambda h, i: (h, i, 0, 0))
    return pl.pallas_call(
        body, name="attn_backward_q", grid=(H, nq),
        in_specs=[whole, whole, pl.BlockSpec((None, nq, HEAD_ROWS, tq), lambda h, i: (h, 0, 0, 0)), one, one],
        out_specs=[pl.BlockSpec((HEAD_DIM, tq), lambda h, i: (h, i)), pl.BlockSpec((None, 1, tq), lambda h, i: (h, 0, i))],
        out_shape=[jax.ShapeDtypeStruct((D, S), F32), jax.ShapeDtypeStruct((H, 1, S), F32)],
        scratch_shapes=[pltpu.VMEM((HEAD_DIM, tq), F32), pltpu.VMEM((1, tq), F32)],
        compiler_params=_params(("arbitrary", "arbitrary")),
    )(ka, va, kat, qat, doat)


def _attn_backward_kv(qa, doa, qat, doat, kat, vat, S, D, tq):
    H = D // HEAD_DIM
    nq = S // tq

    def body(qa_ref, doa_ref, qat_ref, doat_ref, kat_ref, vat_ref, dk_ref, dv_ref, dck_ref, dk_scr, dv_scr, cs_scr):
        ki = pl.program_id(1)
        dk_scr[...] = jnp.zeros_like(dk_scr)
        dv_scr[...] = jnp.zeros_like(dv_scr)
        cs_scr[...] = jnp.zeros_like(cs_scr)
        ka = kat_ref[...]
        va = vat_ref[...]

        def products(qi):
            rows = _seq_tile(qi, tq)
            return (jnp.dot(qa_ref[rows, :], ka, preferred_element_type=F32),
                    jnp.dot(doa_ref[rows, :], va, preferred_element_type=F32))

        def consume(s, dpd, qi):
            p = jnp.exp(s)
            ds = p * dpd
            dv_scr[...] += jnp.dot(doat_ref[qi, 0:HEAD_DIM, :], p.astype(BF16), preferred_element_type=F32)
            dk_scr[...] += jnp.dot(qat_ref[qi, 0:HEAD_DIM, :], ds.astype(BF16), preferred_element_type=F32)
            cs_scr[...] += jnp.sum(ds, axis=0, keepdims=True)

        def step(j, carry):
            qi = nq - 1 - j
            nxt = products(qi - 1)
            consume(*carry, qi)
            return nxt

        s, dpd = lax.fori_loop(0, nq - 1 - ki, step, products(nq - 1))
        consume(_causal(s, 1), dpd, ki)
        dk_ref[...] = dk_scr[...]
        dv_ref[...] = dv_scr[...].astype(BF16)
        dck_ref[...] = cs_scr[...]

    whole = pl.BlockSpec((None, S, HEAD_ROWS), lambda h, i: (h, 0, 0))
    tiles = pl.BlockSpec((None, nq, HEAD_ROWS, tq), lambda h, i: (h, 0, 0, 0))
    one = pl.BlockSpec((None, None, HEAD_ROWS, tq), lambda h, i: (h, i, 0, 0))
    chan = pl.BlockSpec((HEAD_DIM, tq), lambda h, i: (h, i))
    return pl.pallas_call(
        body, name="attn_backward_kv", grid=(H, nq),
        in_specs=[whole, whole, tiles, tiles, one, one],
        out_specs=[chan, chan, pl.BlockSpec((None, 1, tq), lambda h, i: (h, 0, i))],
        out_shape=[jax.ShapeDtypeStruct((D, S), F32), jax.ShapeDtypeStruct((D, S), BF16),
                   jax.ShapeDtypeStruct((H, 1, S), F32)],
        scratch_shapes=[pltpu.VMEM((HEAD_DIM, tq), F32), pltpu.VMEM((HEAD_DIM, tq), F32), pltpu.VMEM((1, tq), F32)],
        compiler_params=_params(("arbitrary", "arbitrary")),
    )(qa, doa, qat, doat, kat, vat)


def _fox_prep_bwd(ut, dqt, dkt, dvt, dcq, dck, b_f, qg, kg, S, D, tq):
    H = D // HEAD_DIM
    T = min(tq, 256)
    nT = S // T
    NU = 3 * D + LANES
    scale = HEAD_DIM ** -0.5

    def body(q_ref, k_ref, f_ref, dq_ref, dk_ref, dv_ref, dcq_ref, dck_ref, bf_ref, qg_ref, kg_ref, tri_ref,
             du_ref, dbf_ref, dqg_ref, dkg_ref, gq_acc, gk_acc, fcar, dc_scr):
        step = pl.program_id(0)

        @pl.when(step == 0)
        def _():
            for ref in (gq_acc, gk_acc, fcar, dbf_ref):
                ref[...] = jnp.zeros_like(ref)

        dc_scr[...] = jnp.zeros_like(dc_scr)

        def head(h, _):
            rows = _head_rows(h)
            dc_scr[pl.ds(h, 1), :] = dcq_ref[h] - dck_ref[h]
            for src, dsrc, gain, acc, mul, base in ((q_ref, dq_ref, qg_ref, gq_acc, scale, 0),
                                                    (k_ref, dk_ref, kg_ref, gk_acc, 1.0, D)):
                x = src[rows, :]
                rs = lax.rsqrt(jnp.mean(x * x, axis=0, keepdims=True) + EPS)
                xhat = x * rs
                dn = dsrc[rows, :] * mul
                acc[rows, :] += jnp.sum(dn * xhat, axis=1, keepdims=True)
                dxh = dn * gain[rows, :]
                dx = rs * (dxh - xhat * jnp.mean(dxh * xhat, axis=0, keepdims=True))
                du_ref[pl.ds(pl.multiple_of(base + h * HEAD_DIM, HEAD_DIM), HEAD_DIM), :] = dx.astype(BF16)
            return 0

        lax.fori_loop(0, H, head, 0)
        du_ref[2 * D:3 * D, :] = dv_ref[...]
        dlf, carry = _lane_cumsum(dc_scr[...], tri_ref, fcar[...], True)
        fcar[...] = carry
        dfl = dlf * _sigmoid(-(f_ref[...] + bf_ref[...]))
        dbf_ref[...] += jnp.sum(dfl, axis=1, keepdims=True)
        du_ref[3 * D:NU, :] = dfl.astype(BF16)

        @pl.when(step == nT - 1)
        def _():
            for acc, ref in ((gq_acc, dqg_ref), (gk_acc, dkg_ref)):
                tot = jnp.zeros((HEAD_DIM, 1), F32)
                for h in range(H):
                    tot = tot + acc[h * HEAD_DIM:(h + 1) * HEAD_DIM, :]
                ref[...] = tot

    rev = lambda i: nT - 1 - i
    part = lambda j: pl.BlockSpec((D, T), lambda i: (j, rev(i)))
    chan = pl.BlockSpec((D, T), lambda i: (0, rev(i)))
    stat = pl.BlockSpec((H, 1, T), lambda i: (0, 0, rev(i)))
    colv = lambda n: pl.BlockSpec((n, 1), lambda i: (0, 0))
    return pl.pallas_call(
        body, name="fox_prep_bwd", grid=(nT,),
        in_specs=[part(0), part(1), pl.BlockSpec((LANES, T), lambda i: (3 * D // LANES, rev(i))), chan, chan, chan,
                  stat, stat, colv(LANES), colv(D), colv(D), pl.BlockSpec((LANES, LANES), lambda i: (0, 0))],
        out_specs=[pl.BlockSpec((NU, T), lambda i: (0, rev(i))), colv(LANES), colv(HEAD_DIM), colv(HEAD_DIM)],
        out_shape=[jax.ShapeDtypeStruct((NU, S), BF16), jax.ShapeDtypeStruct((LANES, 1), F32),
                   jax.ShapeDtypeStruct((HEAD_DIM, 1), F32), jax.ShapeDtypeStruct((HEAD_DIM, 1), F32)],
        scratch_shapes=[pltpu.VMEM((D, 1), F32), pltpu.VMEM((D, 1), F32), pltpu.VMEM((LANES, 1), F32),
                        pltpu.VMEM((LANES, T), F32)],
        compiler_params=_params(("arbitrary",)),
    )(ut, ut, ut, dqt, dkt, dvt, dcq, dck, b_f, qg, kg, _tri_matrix(True))


def _block_diag_tiles(w):
    n = w.shape[0]
    per = min(MXU_DIM, n * LRU_BLOCK_DIM) // LRU_BLOCK_DIM
    eye = jnp.eye(per, dtype=w.dtype)
    w5 = w.reshape(n // per, per, LRU_BLOCK_DIM, 1, LRU_BLOCK_DIM) * eye[None, :, None, :, None]
    return w5.reshape(n // per, per * LRU_BLOCK_DIM, per * LRU_BLOCK_DIM).astype(BF16)


def _block_diag_extract(t, n):
    per = t.shape[-1] // LRU_BLOCK_DIM
    eye = jnp.eye(per, dtype=t.dtype)
    t5 = t.reshape(n // per, per, LRU_BLOCK_DIM, per, LRU_BLOCK_DIM) * eye[None, :, None, :, None]
    return t5.sum(axis=3).reshape(n, LRU_BLOCK_DIM, LRU_BLOCK_DIM)


def _local_step(x, tgt, small, wv, grad_view):
    S, D = x.shape
    F = 4 * D
    H = D // HEAD_DIM
    nblk = D // LRU_BLOCK_DIM
    NU = 3 * D + LANES
    tq = max(LANES, min(512, S // 4))
    assert S % tq == 0
    vec = lambda a: a.reshape(1, -1).astype(F32)
    col = lambda a: a.reshape(-1, 1).astype(F32)
    mix_g, mlp_g = small["mix_norm"], small["mlp_norm"]
    conv_w, conv_b = small["conv_w"], vec(small["lru_conv_b"])
    wr_bd, wi_bd = _block_diag_tiles(small["lru_w_r"][0]), _block_diag_tiles(small["lru_w_i"][0])
    b_r, b_i, lam = vec(small["lru_b_r"]), vec(small["lru_b_i"]), vec(small["lru_lambda"])
    b_f = jnp.pad(col(small["fox_b_f"]), ((0, LANES - H), (0, 0)))
    qg, kg = jnp.tile(col(small["fox_q_gain"]), (H, 1)), jnp.tile(col(small["fox_k_gain"]), (H, 1))
    X = lambda a: _View(a)
    grads = {}
    gout = functools.partial(grad_view, grads)

    def mlp_fwd(l, xin):
        hm = _rms_fwd(f"mlp{l}_norm", xin, mlp_g[l:l + 1], S, D)
        z, act = _matmul(f"mlp{l}_up", X(hm), wv[f"w1_{l}"], S, F, D, outs=[_fresh(S, F, BF16), _fresh(S, F, BF16)],
                         epilogue=_ep_relu2)
        (xout,) = _matmul(f"mlp{l}_down", X(act), wv[f"w2_{l}"], S, D, F, outs=[_fresh(S, D, F32)],
                          epilogue=_ep_resid, extras=[X(xin)])
        return hm, z, act, xout

    def mlp_bwd(l, xin, hm, z, act, d, db):
        (dz,) = _matmul(f"mlp{l}_dact", X(db), wv[f"w2_{l}"], S, F, D, tb=True, outs=[_fresh(S, F, BF16)],
                        epilogue=_ep_drelu2, extras=[X(z)])
        (grads[f"w2_{l}"],) = _matmul(f"mlp{l}_dw2", X(act), X(db), F, D, S, ta=True, outs=[gout(f"w2_{l}")],
                                      epilogue=_ep_store)
        (grads[f"w1_{l}"],) = _matmul(f"mlp{l}_dw1", X(hm), X(dz), D, F, S, ta=True, outs=[gout(f"w1_{l}")],
                                      epilogue=_ep_store)
        (dhm,) = _matmul(f"mlp{l}_dhm", X(dz), wv[f"w1_{l}"], S, D, F, tb=True, outs=[_fresh(S, D, F32)],
                         epilogue=_ep_store)
        return _rms_bwd(f"mlp{l}_norm_bwd", dhm, xin, mlp_g[l:l + 1], d, S, D)

    h0 = _rms_fwd("mix0_norm", x, mix_g[0:1], S, D)
    (u0,) = _matmul("lru_in", X(h0), wv["lru_in"], S, 2 * D, D, outs=[_fresh(S, 2 * D, F32)], epilogue=_ep_store)
    y, xc, r, ig, hs = _lru_fwd(u0, conv_w, conv_b, wr_bd, b_r, wi_bd, b_i, lam, S, D)
    (x1,) = _matmul("lru_out", X(y), wv["lru_out"], S, D, D, outs=[_fresh(S, D, F32)], epilogue=_ep_resid,
                    extras=[X(x)])
    hm0, z0, act0, x2 = mlp_fwd(0, x1)
    h1 = _rms_fwd("mix1_norm", x2, mix_g[1:2], S, D)
    (u1,) = _matmul("fox_in", wv["fox_in"], X(h1), NU, S, D, tb=True, outs=[_fresh(NU, S, F32)], epilogue=_ep_store)
    qat, kat, vat, ka, va, vt = _fox_prep(u1, b_f, qg, kg, S, D, tq)
    o, o32, lse = _attn_forward(ka, qat, vt, S, D, tq)
    (x3,) = _matmul("fox_out", X(o), wv["fox_out"], S, D, D, ta=True, outs=[_fresh(S, D, F32)], epilogue=_ep_resid,
                    extras=[X(x2)])
    hm1, z1, act1, x4 = mlp_fwd(1, x3)
    loss, d4, d4b = _loss_head(x4, tgt, S, D)

    d3, d3b, dg_mlp1 = mlp_bwd(1, x3, hm1, z1, act1, d4, d4b)
    (do,) = _matmul("fox_dout", wv["fox_out"], X(d3b), D, S, D, tb=True, outs=[_fresh(D, S, BF16)], epilogue=_ep_store)
    (grads["fox_out"],) = _matmul("fox_dwout", X(o), X(d3b), D, D, S, outs=[gout("fox_out")], epilogue=_ep_store)
    doat, doa, qat1, qa1 = _fox_bwd_prep(do, o32, lse, qat, S, D, tq)
    dqn, dcq = _attn_backward_q(ka, va, kat, qat1, doat, S, D, tq)
    dkn, dv, dck = _attn_backward_kv(qa1, doa, qat1, doat, kat, vat, S, D, tq)
    du1, dbf, dqg, dkg = _fox_prep_bwd(u1, dqn, dkn, dv, dcq, dck, b_f, qg, kg, S, D, tq)
    (grads["fox_in"],) = _matmul("fox_dwin", X(h1), X(du1), D, NU, S, ta=True, tb=True, outs=[gout("fox_in")],
                                 epilogue=_ep_store)
    (dh1,) = _matmul("fox_dh", X(du1), wv["fox_in"], S, D, NU, ta=True, outs=[_fresh(S, D, F32)], epilogue=_ep_store)
    d2, d2b, dg_mix1 = _rms_bwd("mix1_norm_bwd", dh1, x2, mix_g[1:2], d3, S, D)
    d1, d1b, dg_mlp0 = mlp_bwd(0, x1, hm0, z0, act0, d2, d2b)
    (dy,) = _matmul("lru_dout", X(d1b), wv["lru_out"], S, D, D, tb=True, outs=[_fresh(S, D, F32)], epilogue=_ep_store)
    (grads["lru_out"],) = _matmul("lru_dwout", X(y), X(d1b), D, D, S, ta=True, outs=[gout("lru_out")],
                                  epilogue=_ep_store)
    du0, dcw, dcb, dlam, dbr, dbi, dwr, dwi = _lru_bwd(dy, u0, xc, r, ig, hs, conv_w, wr_bd, wi_bd, lam, S, D)
    (grads["lru_in"],) = _matmul("lru_dwin", X(h0), X(du0), D, 2 * D, S, ta=True, outs=[gout("lru_in")],
                                 epilogue=_ep_store)
    (dh0,) = _matmul("lru_dh", X(du0), wv["lru_in"], S, D, 2 * D, tb=True, outs=[_fresh(S, D, F32)], epilogue=_ep_store)
    gx, _, dg_mix0 = _rms_bwd("mix0_norm_bwd", dh0, x, mix_g[0:1], d1, S, D)

    grads.update(
        mix_norm=jnp.concatenate([dg_mix0, dg_mix1], axis=0), mlp_norm=jnp.concatenate([dg_mlp0, dg_mlp1], axis=0),
        conv_w=dcw, lru_conv_b=dcb, lru_w_r=_block_diag_extract(dwr, nblk)[None], lru_b_r=dbr.reshape(1, nblk, -1),
        lru_w_i=_block_diag_extract(dwi, nblk)[None], lru_b_i=dbi.reshape(1, nblk, -1), lru_lambda=dlam,
        fox_b_f=dbf[:H].reshape(1, H), fox_q_gain=dqg.reshape(1, -1), fox_k_gain=dkg.reshape(1, -1))
    return loss, gx, grads


def _place():
    x, y, c = lax.axis_index("x"), lax.axis_index("y"), lax.axis_index("c")
    chips = [(1 - x, y), (x, 1 - y), (1 - x, 1 - y)]
    return x, y, c, 2 * x + y, chips


BOUNCE_BYTES = 1 << 20


def _bounce_shape(rows, cols, dtype):
    chunk = rows
    while chunk % 2 == 0 and chunk > 16 and chunk * cols * jnp.dtype(dtype).itemsize > BOUNCE_BYTES:
        chunk //= 2
    return pltpu.VMEM((2, chunk, cols), dtype)


def _bounce_copy(src, dst, buf, sem):
    chunk = buf.shape[1]
    n = src.shape[0] // chunk
    cin = lambda i: pltpu.make_async_copy(src.at[pl.ds(i * chunk, chunk)], buf.at[i % 2], sem.at[i % 2])
    cout = lambda i: pltpu.make_async_copy(buf.at[i % 2], dst.at[pl.ds(i * chunk, chunk)], sem.at[2 + i % 2])
    cin(0).start()
    for i in range(n):
        cin(i).wait()
        if i + 1 < n:
            if i >= 1:
                cout(i - 1).wait()
            cin(i + 1).start()
        cout(i).start()
    if n >= 2:
        cout(n - 2).wait()
    cout(n - 1).wait()


def _hbm_call(body, name, arrays, out_shape, n_dma_sems, bounce=()):
    scratch = [pltpu.SemaphoreType.DMA((k,)) for k in n_dma_sems]
    for rows, cols, dtype in bounce:
        scratch += [_bounce_shape(rows, cols, dtype), pltpu.SemaphoreType.DMA((4,))]
    return pl.pallas_call(
        body, name=name, in_specs=[ANY] * len(arrays), out_specs=[ANY] * len(out_shape), out_shape=out_shape,
        scratch_shapes=scratch,
        compiler_params=pltpu.CompilerParams(has_side_effects=True, vmem_limit_bytes=VMEM_LIMIT),
    )(*arrays)


def _all_gather(name, shards):
    n = len(shards)

    def body(*refs):
        ins, outs = refs[:n], refs[n:2 * n]
        send, recv, fsend, frecv = refs[2 * n:2 * n + 4]
        stage = refs[2 * n + 4:]
        x, y, c, s, chips = _place()
        sibling = (x, y, 1 - c)

        def rows(a, chip_idx, which):
            hr = ins[a].shape[0] // 2
            return outs[a].at[chip_idx, pl.ds(which * hr, hr)]

        def ici(a, j, src, dst, to):
            return pltpu.make_async_remote_copy(src_ref=src, dst_ref=dst, send_sem=send.at[3 * a + j],
                                                recv_sem=recv.at[3 * a + j], device_id=to, device_id_type=MESH)

        def d2d(a, j, src, dst):
            return pltpu.make_async_remote_copy(src_ref=src, dst_ref=dst, send_sem=fsend.at[3 * a + j],
                                                recv_sem=frecv.at[3 * a + j], device_id=sibling, device_id_type=MESH)

        started = []
        for a in range(n):
            hr = ins[a].shape[0] // 2
            for j, chip in enumerate(chips):
                cp = ici(a, j, ins[a].at[pl.ds(c * hr, hr)], rows(a, s, c), (*chip, c))
                cp.start()
                started.append(cp)
        for a in range(n):
            _bounce_copy(ins[a], outs[a].at[s], stage[2 * a], stage[2 * a + 1])
        for a in range(n):
            for j, chip in enumerate(chips):
                got = rows(a, 2 * chip[0] + chip[1], c)
                ici(a, j, got, got, (*chip, c)).wait_recv()
                fw = d2d(a, j, got, got)
                fw.start()
                started.append(fw)
        for a in range(n):
            for j, chip in enumerate(chips):
                theirs = rows(a, 2 * chip[0] + chip[1], 1 - c)
                d2d(a, j, theirs, theirs).wait_recv()
        for cp in started:
            cp.wait_send()

    out_shape = [jax.ShapeDtypeStruct((N_CHIPS,) + tuple(a.shape), a.dtype) for a in shards]
    return _hbm_call(body, name, shards, out_shape, (3 * n, 3 * n, 3 * n, 3 * n),
                     bounce=[(a.shape[0], a.shape[1], a.dtype) for a in shards])


def _pair_swap(name, arrs):
    n = len(arrs)

    def body(*refs):
        ins, outs = refs[:n], refs[n:2 * n]
        send, recv = refs[2 * n:]
        x, y, c, _, _ = _place()
        cps = []
        for a in range(n):
            hr = ins[a].shape[1] // 2
            cp = pltpu.make_async_remote_copy(
                src_ref=ins[a].at[:, pl.ds((1 - c) * hr, hr)], dst_ref=outs[a], send_sem=send.at[a],
                recv_sem=recv.at[a], device_id=(x, y, 1 - c), device_id_type=MESH)
            cp.start()
            cps.append(cp)
        for cp in cps:
            cp.wait()

    out_shape = [jax.ShapeDtypeStruct((a.shape[0], a.shape[1] // 2, a.shape[2]), a.dtype) for a in arrs]
    return _hbm_call(body, name, arrs, out_shape, (n, n))


def _chip_scatter(name, parts):
    n = len(parts)

    def body(*refs):
        ins, outs = refs[:n], refs[n:2 * n]
        send, recv = refs[2 * n:2 * n + 2]
        stage = refs[2 * n + 2:]
        x, y, c, s, chips = _place()
        cps = []
        for a in range(n):
            for j, chip in enumerate(chips):
                t = 2 * chip[0] + chip[1]
                cp = pltpu.make_async_remote_copy(
                    src_ref=ins[a].at[t], dst_ref=outs[a].at[s], send_sem=send.at[3 * a + j],
                    recv_sem=recv.at[3 * a + j], device_id=(*chip, c), device_id_type=MESH)
                cp.start()
                cps.append(cp)
        for a in range(n):
            _bounce_copy(ins[a].at[s], outs[a].at[s], stage[2 * a], stage[2 * a + 1])
        for a in range(n):
            for j, chip in enumerate(chips):
                t = 2 * chip[0] + chip[1]
                pltpu.make_async_remote_copy(
                    src_ref=ins[a].at[t], dst_ref=outs[a].at[t], send_sem=send.at[3 * a + j],
                    recv_sem=recv.at[3 * a + j], device_id=(*chip, c), device_id_type=MESH).wait_recv()
        for cp in cps:
            cp.wait_send()

    out_shape = [jax.ShapeDtypeStruct(a.shape, a.dtype) for a in parts]
    return _hbm_call(body, name, parts, out_shape, (3 * n, 3 * n),
                     bounce=[(a.shape[1], a.shape[2], a.dtype) for a in parts])


def _pair_gather(name, halves):
    n = len(halves)

    def body(*refs):
        ins, outs = refs[:n], refs[n:2 * n]
        send, recv = refs[2 * n:2 * n + 2]
        stage = refs[2 * n + 2:]
        x, y, c, _, _ = _place()
        cps = []
        for a in range(n):
            hr = ins[a].shape[0]
            cp = pltpu.make_async_remote_copy(
                src_ref=ins[a], dst_ref=outs[a].at[pl.ds(c * hr, hr)], send_sem=send.at[a], recv_sem=recv.at[a],
                device_id=(x, y, 1 - c), device_id_type=MESH)
            cp.start()
            cps.append((cp, hr))
        for a, (cp, hr) in enumerate(cps):
            _bounce_copy(ins[a], outs[a].at[pl.ds(c * hr, hr)], stage[2 * a], stage[2 * a + 1])
        for a, (cp, hr) in enumerate(cps):
            cp.wait_send()
            theirs = outs[a].at[pl.ds((1 - c) * hr, hr)]
            pltpu.make_async_remote_copy(src_ref=theirs, dst_ref=theirs, send_sem=send.at[a], recv_sem=recv.at[a],
                                         device_id=(x, y, 1 - c), device_id_type=MESH).wait_recv()

    out_shape = [jax.ShapeDtypeStruct((2 * a.shape[0], a.shape[1]), a.dtype) for a in halves]
    return _hbm_call(body, name, halves, out_shape, (n, n),
                     bounce=[(a.shape[0], a.shape[1], a.dtype) for a in halves])


def _row_tile(rows, cols, itemsize, n_bufs):
    budget = VMEM_LIMIT // 2
    for t in (1024, 512, 256, 128, 64, 32, 16):
        if rows % t == 0 and 2 * n_bufs * t * cols * itemsize <= budget:
            return t
    return rows


def _pair_add(name, g, gsib, core, out_dtype):
    _, r, cols = g.shape
    hr = r // 2
    t = _row_tile(hr, cols, 4, 3)
    per = hr // t

    def body(core_ref, a_ref, b_ref, o_ref):
        o_ref[...] = (a_ref[...].astype(F32) + b_ref[...].astype(F32)).astype(o_ref.dtype)

    grid_spec = pltpu.PrefetchScalarGridSpec(
        num_scalar_prefetch=1, grid=(N_CHIPS, per),
        in_specs=[pl.BlockSpec((None, t, cols), lambda s, i, core: (s, core[0] * per + i, 0)),
                  pl.BlockSpec((None, t, cols), lambda s, i, core: (s, i, 0))],
        out_specs=pl.BlockSpec((None, t, cols), lambda s, i, core: (s, i, 0)))
    return pl.pallas_call(body, name=name, grid_spec=grid_spec,
                          out_shape=jax.ShapeDtypeStruct((N_CHIPS, hr, cols), out_dtype),
                          compiler_params=_params(("arbitrary", "arbitrary")))(core, g, gsib)


def _chip_sum(name, parts):
    _, hr, cols = parts.shape
    t = _row_tile(hr, cols, 4, 5)

    def body(p_ref, o_ref):
        o_ref[...] = ((p_ref[0].astype(F32) + p_ref[1].astype(F32)) + p_ref[2].astype(F32)) + p_ref[3].astype(F32)

    return pl.pallas_call(
        body, name=name, grid=(hr // t,), in_specs=[pl.BlockSpec((N_CHIPS, t, cols), lambda i: (0, i, 0))],
        out_specs=pl.BlockSpec((t, cols), lambda i: (i, 0)), out_shape=jax.ShapeDtypeStruct((hr, cols), F32),
        compiler_params=_params(("arbitrary",)))(parts)


def _reduce_scatter(tag, arrs, wire_dtypes, core):
    sib = _pair_swap(f"{tag}_pair_swap", arrs)
    parts = [_pair_add(f"{tag}_pair_add{i}", g, gs, core, dt) for i, (g, gs, dt) in enumerate(zip(arrs, sib, wire_dtypes))]
    got = _chip_scatter(f"{tag}_chip_scatter", parts)
    halves = [_chip_sum(f"{tag}_chip_sum{i}", p) for i, p in enumerate(got)]
    return _pair_gather(f"{tag}_pair_gather", halves)


def _adamw(name, w, g, m, v):
    rows, cols = w.shape
    t = _row_tile(rows, cols, 4, 8)
    c1 = 1.0 - ADAM_B1 ** ADAM_STEP
    c2 = 1.0 - ADAM_B2 ** ADAM_STEP

    def body(w_ref, g_ref, m_ref, v_ref, go_ref, d_ref, nm_ref, nv_ref):
        g = g_ref[...]
        go_ref[...] = g
        m = ADAM_B1 * m_ref[...] + (1.0 - ADAM_B1) * g
        v = ADAM_B2 * v_ref[...] + (1.0 - ADAM_B2) * (g * g)
        nm_ref[...] = m
        nv_ref[...] = v
        d_ref[...] = -ADAM_LR * ((m / c1) / (jnp.sqrt(v / c2) + ADAM_EPS) + ADAM_WD * w_ref[...])

    spec = pl.BlockSpec((t, cols), lambda i: (i, 0))
    shp = jax.ShapeDtypeStruct((rows, cols), F32)
    return pl.pallas_call(body, name=name, grid=(rows // t,), in_specs=[spec] * 4, out_specs=[spec] * 4,
                          out_shape=[shp] * 4, compiler_params=_params(("arbitrary",)))(w, g, m, v)


_WEIGHTS = ["mix_norm", "mlp_norm", "mlp_w1", "mlp_w2", "lru_w_in", "lru_conv_w", "lru_conv_b", "lru_w_r", "lru_b_r",
            "lru_w_i", "lru_b_i", "lru_lambda", "lru_w_out", "fox_w_in", "fox_b_f", "fox_q_gain", "fox_k_gain",
            "fox_w_out"]
_REPLICATED = ["mix_norm", "mlp_norm", "lru_conv_b", "lru_w_r", "lru_b_r", "lru_w_i", "lru_b_i", "lru_lambda",
               "fox_b_f", "fox_q_gain", "fox_k_gain"]
_PACK_TILE = 2 * SUBLANES * LANES


def _as2d(a):
    return a.reshape(-1, a.shape[-1])


def kernel(x, mix_norm, mlp_norm, mlp_w1, mlp_w2, lru_w_in, lru_conv_w, lru_conv_b, lru_w_r, lru_b_r, lru_w_i, lru_b_i, lru_lambda, lru_w_out, fox_w_in, fox_b_f, fox_q_gain, fox_k_gain, fox_w_out, loss_target, m_mix_norm, m_mlp_norm, m_mlp_w1, m_mlp_w2, m_lru_w_in, m_lru_conv_w, m_lru_conv_b, m_lru_w_r, m_lru_b_r, m_lru_w_i, m_lru_b_i, m_lru_lambda, m_lru_w_out, m_fox_w_in, m_fox_b_f, m_fox_q_gain, m_fox_k_gain, m_fox_w_out, v_mix_norm, v_mlp_norm, v_mlp_w1, v_mlp_w2, v_lru_w_in, v_lru_conv_w, v_lru_conv_b, v_lru_w_r, v_lru_b_r, v_lru_w_i, v_lru_b_i, v_lru_lambda, v_lru_w_out, v_fox_w_in, v_fox_b_f, v_fox_q_gain, v_fox_k_gain, v_fox_w_out):
    args = dict(locals())
    W = {n: args[n] for n in _WEIGHTS}
    Mo = {n: args["m_" + n] for n in _WEIGHTS}
    Vo = {n: args["v_" + n] for n in _WEIGHTS}
    S, D = x.shape[1], x.shape[2]
    F = 4 * D
    H = D // HEAD_DIM
    NU = 3 * D + LANES
    FQ, DQ = F // N_CHIPS, D // N_CHIPS
    nfox = fox_w_in.shape[-1]
    chip = 2 * lax.axis_index("x") + lax.axis_index("y")
    core = lax.axis_index("c").astype(jnp.int32).reshape(1)

    cw_flat = jnp.pad(lru_conv_w.reshape(-1), (0, _PACK_TILE - CONV_WIDTH * DQ)).reshape(2 * SUBLANES, LANES)
    g_w1, g_w2, g_lin, g_lout, g_fin, g_fout = _all_gather(
        "gather_weights",
        [_as2d(mlp_w1).astype(BF16), _as2d(mlp_w2).astype(BF16), lru_w_in[0].astype(BF16), lru_w_out[0].astype(BF16),
         fox_w_in[0].astype(BF16), fox_w_out[0].astype(BF16)])
    (g_cw,) = _all_gather("gather_conv", [cw_flat])
    conv_w_full = jnp.transpose(g_cw.reshape(N_CHIPS, -1)[:, :CONV_WIDTH * DQ].reshape(N_CHIPS, CONV_WIDTH, DQ),
                                (1, 0, 2)).reshape(CONV_WIDTH, D)
    fox_full = jnp.concatenate([g_fin[s] for s in range(N_CHIPS)], axis=1)
    fox_full = jnp.pad(fox_full, ((0, 0), (0, NU - fox_full.shape[1])))
    wv = {"w1_0": _View(g_w1, "cs", 0, D), "w1_1": _View(g_w1, "cs", D, D),
          "w2_0": _View(g_w2, "rs", 0, FQ), "w2_1": _View(g_w2, "rs", FQ, FQ),
          "lru_in": _View(g_lin, "cs"), "lru_out": _View(g_lout, "rs"),
          "fox_in": _View(fox_full.T), "fox_out": _View(g_fout, "rs")}

    def grad_view(grads, name):
        if name in ("w1_0", "w1_1"):
            return _View(grads.get("w1_1"), "cs", D * int(name[-1]), D, shape=(N_CHIPS, 2 * D, FQ), dtype=BF16)
        if name in ("w2_0", "w2_1"):
            return _View(grads.get("w2_1"), "rs", FQ * int(name[-1]), FQ, shape=(N_CHIPS, 2 * FQ, D), dtype=BF16)
        if name == "lru_in":
            return _View(None, "cs", shape=(N_CHIPS, D, 2 * D // N_CHIPS), dtype=BF16)
        if name in ("lru_out", "fox_out"):
            return _View(None, "rs", shape=(N_CHIPS, DQ, D), dtype=BF16)
        return _View(None, shape=(D, NU), dtype=BF16)

    small = {n: W[n] for n in _REPLICATED}
    small["conv_w"] = conv_w_full

    loss, gx, grads = _local_step(x[0], loss_target[0], small, wv, grad_view)

    g_fox = jnp.transpose(grads["fox_in"][:, :nfox * N_CHIPS].reshape(D, N_CHIPS, nfox), (1, 0, 2))
    big = [grads["w1_0"], grads["w2_0"], grads["lru_in"], grads["lru_out"], g_fox, grads["fox_out"]]
    pack_names = _REPLICATED + ["conv_w"]
    flat = jnp.concatenate([grads[n].reshape(-1).astype(F32) for n in pack_names])
    per_chip = -(-flat.shape[0] // (N_CHIPS * _PACK_TILE)) * _PACK_TILE
    pack = jnp.pad(flat, (0, N_CHIPS * per_chip - flat.shape[0])).reshape(N_CHIPS, per_chip // LANES, LANES)
    red = _reduce_scatter("grads", big + [pack], [BF16] * len(big) + [F32], core)
    r_w1, r_w2, r_lin, r_lout, r_fin, r_fout, r_pack = red
    (all_pack,) = _all_gather("gather_small_grads", [r_pack])
    all_flat = all_pack.reshape(-1)
    G = {}
    off = 0
    for n in pack_names:
        shape = grads[n].shape if n == "conv_w" else W[n].shape
        size = int(np.prod(shape))
        G[n] = all_flat[off:off + size].reshape(shape)
        off += size
    G["lru_conv_w"] = lax.dynamic_slice_in_dim(G.pop("conv_w"), chip * DQ, DQ, axis=1)[None]
    G.update(mlp_w1=r_w1.reshape(mlp_w1.shape), mlp_w2=r_w2.reshape(mlp_w2.shape), lru_w_in=r_lin[None],
             lru_w_out=r_lout[None], fox_w_in=r_fin[None], fox_w_out=r_fout[None])

    delta, new_m, new_v = {}, {}, {}
    for n in _WEIGHTS:
        go, d, nm, nv = _adamw(f"adamw_{n}", _as2d(W[n]), _as2d(G[n]), _as2d(Mo[n]), _as2d(Vo[n]))
        G[n], delta[n], new_m[n], new_v[n] = (t.reshape(W[n].shape) for t in (go, d, nm, nv))

    total = lax.psum(loss[0, 0], ("x", "y", "c"))
    return (total, gx[None], *[G[n] for n in _WEIGHTS], *[delta[n] for n in _WEIGHTS],
            *[new_m[n] for n in _WEIGHTS], *[new_v[n] for n in _WEIGHTS])
```

```python
import functools

import numpy as np
import jax
import jax.numpy as jnp
from jax import lax
from jax.experimental import pallas as pl
from jax.experimental.pallas import tpu as pltpu

F32 = jnp.float32
BF16 = jnp.bfloat16

HEAD_DIM = 64
LRU_BLOCK_DIM = 64
CONV_WIDTH = 4
LRU_C = 8.0
EPS = 1e-6
NEG_INF = -1e30
ADAM_LR = 0.001
ADAM_B1 = 0.9
ADAM_B2 = 0.999
ADAM_EPS = 1e-08
ADAM_WD = 0.01
ADAM_STEP = 10

N_CHIPS = 4
LANES = 128
SUBLANES = 8
MXU_DIM = 256
VMEM_LIMIT = 52 * 1024 * 1024
MESH = pl.DeviceIdType.MESH
ANY = pl.BlockSpec(memory_space=pl.ANY)


def _pick(n, prefs):
    for p in prefs:
        if p <= n and n % p == 0:
            return p
    return n


def _params(sem=None):
    return pltpu.CompilerParams(dimension_semantics=sem, vmem_limit_bytes=VMEM_LIMIT)


class _View:
    def __init__(self, arr, kind="plain", r0=0, rows=None, shape=None, dtype=None):
        self.arr = arr
        self.kind = kind
        self.r0 = r0
        self.shape = tuple(arr.shape) if arr is not None else tuple(shape)
        self.dtype = arr.dtype if arr is not None else dtype
        self.rows = rows if rows is not None else self.shape[-2]

    def limits(self):
        if self.kind == "plain":
            return 0, 0
        rows = int(np.gcd(self.rows, self.r0))
        return rows, (self.shape[-1] if self.kind == "cs" else 0)

    def spec(self, br, bc, fr, fc):
        if self.kind == "plain":
            return pl.BlockSpec((br, bc), lambda *g: (fr(*g), fc(*g)))
        ncol = self.shape[-1]
        r0b = self.r0 // br
        assert self.r0 % br == 0 and self.rows % br == 0 and ncol % bc == 0, (self.shape, self.r0, br, bc)
        if self.kind == "cs":
            per = ncol // bc
            return pl.BlockSpec((None, br, bc), lambda *g: (fc(*g) // per, r0b + fr(*g), fc(*g) % per))
        per = self.rows // br
        return pl.BlockSpec((None, br, bc), lambda *g: (fr(*g) // per, r0b + fr(*g) % per, fc(*g)))


def _bf(x):
    return x if x.dtype == BF16 else x.astype(BF16)


def _matmul(name, A, B, M, N, K, *, ta=False, tb=False, outs, epilogue, extras=(), tm=None, tn=None, tk=None):
    lim = {"m": [M], "n": [N], "k": [K]}
    for view, (rdim, cdim) in ([(A, "km" if ta else "mk"), (B, "nk" if tb else "kn")]
                               + [(e, "mn") for e in extras] + [(o, "mn") for o in outs]):
        r_lim, c_lim = view.limits()
        lim[rdim].append(r_lim)
        lim[cdim].append(c_lim)
    tm = tm or _pick(int(np.gcd.reduce(lim["m"])), (1024, 640, 512, 256, 128))
    tn = tn or _pick(int(np.gcd.reduce(lim["n"])), (1024, 640, 512, 256, 128))
    tk = tk or _pick(int(np.gcd.reduce(lim["k"])), (1024, 640, 512, 256, 128))
    nk = K // tk
    gi = lambda i, j, k: i
    gj = lambda i, j, k: j
    gk = lambda i, j, k: k
    a_spec = A.spec(tk, tm, gk, gi) if ta else A.spec(tm, tk, gi, gk)
    b_spec = B.spec(tn, tk, gj, gk) if tb else B.spec(tk, tn, gk, gj)
    ca = 0 if ta else 1
    cb = 1 if tb else 0
    ne, no = len(extras), len(outs)
    in_specs = [a_spec, b_spec] + [e.spec(tm, tn, gi, gj) for e in extras]
    operands = [A.arr, B.arr] + [e.arr for e in extras]
    aliases = {}
    for oi, o in enumerate(outs):
        if o.arr is not None:
            aliases[len(operands)] = oi
            in_specs.append(ANY)
            operands.append(o.arr)
    nalias = len(aliases)
    out_specs = [o.spec(tm, tn, gi, gj) for o in outs]
    out_shape = [jax.ShapeDtypeStruct(o.shape, o.dtype) for o in outs]

    def body(*refs):
        a_ref, b_ref = refs[0], refs[1]
        ex = refs[2:2 + ne]
        o_refs = refs[2 + ne + nalias:2 + ne + nalias + no]

        def prod():
            return lax.dot_general(_bf(a_ref[...]), _bf(b_ref[...]), (((ca,), (cb,)), ((), ())),
                                   preferred_element_type=F32)

        def finish(acc):
            res = epilogue(acc, *[e[...] for e in ex])
            for o_ref, r in zip(o_refs, res):
                o_ref[...] = r.astype(o_ref.dtype)

        if nk == 1:
            finish(prod())
        else:
            acc_ref = refs[-1]
            k = pl.program_id(2)

            @pl.when(k == 0)
            def _():
                acc_ref[...] = jnp.zeros_like(acc_ref)

            acc_ref[...] += prod()

            @pl.when(k == nk - 1)
            def _():
                finish(acc_ref[...])

    res = pl.pallas_call(
        body, name=name, grid=(M // tm, N // tn, nk), in_specs=in_specs, out_specs=out_specs, out_shape=out_shape,
        scratch_shapes=[pltpu.VMEM((tm, tn), F32)] if nk > 1 else [],
        input_output_aliases=aliases,
        compiler_params=_params(("parallel", "parallel", "arbitrary")),
    )(*operands)
    return res


def _ep_store(acc):
    return (acc,)


def _ep_resid(acc, res):
    return (res + acc,)


def _ep_relu2(acc):
    zp = jnp.maximum(acc, 0.0)
    return (acc, zp * zp)


def _ep_drelu2(acc, z):
    return (acc * (2.0 * jnp.maximum(z.astype(F32), 0.0)),)


def _fresh(M, N, dtype):
    return _View(None, shape=(M, N), dtype=dtype)


def _rms_fwd(name, x, g, S, D):
    T = _pick(S, (512, 256, 128))

    def body(x_ref, g_ref, h_ref):
        x = x_ref[...]
        r = lax.rsqrt(jnp.mean(x * x, axis=-1, keepdims=True) + EPS)
        h_ref[...] = ((x * r) * g_ref[...]).astype(BF16)

    return pl.pallas_call(
        body, name=name, grid=(S // T,),
        in_specs=[pl.BlockSpec((T, D), lambda i: (i, 0)), pl.BlockSpec((1, D), lambda i: (0, 0))],
        out_specs=pl.BlockSpec((T, D), lambda i: (i, 0)),
        out_shape=jax.ShapeDtypeStruct((S, D), BF16),
        compiler_params=_params(("arbitrary",)),
    )(x, g)


def _rms_bwd(name, dh, x, g, dres, S, D):
    T = _pick(S, (512, 256, 128))

    def body(dh_ref, x_ref, g_ref, dres_ref, dx_ref, dxb_ref, dg_ref):
        @pl.when(pl.program_id(0) == 0)
        def _():
            dg_ref[...] = jnp.zeros_like(dg_ref)

        x = x_ref[...]
        dh = dh_ref[...]
        r = lax.rsqrt(jnp.mean(x * x, axis=-1, keepdims=True) + EPS)
        xhat = x * r
        dg_ref[...] += jnp.sum(dh * xhat, axis=0, keepdims=True)
        dxn = dh * g_ref[...]
        dx = r * (dxn - xhat * jnp.mean(dxn * xhat, axis=-1, keepdims=True))
        tot = dres_ref[...] + dx
        dx_ref[...] = tot
        dxb_ref[...] = tot.astype(BF16)

    row = pl.BlockSpec((T, D), lambda i: (i, 0))
    vec = pl.BlockSpec((1, D), lambda i: (0, 0))
    return pl.pallas_call(
        body, name=name, grid=(S // T,), in_specs=[row, row, vec, row], out_specs=[row, row, vec],
        out_shape=[jax.ShapeDtypeStruct((S, D), F32), jax.ShapeDtypeStruct((S, D), BF16),
                   jax.ShapeDtypeStruct((1, D), F32)],
        compiler_params=_params(("arbitrary",)),
    )(dh, x, g, dres)


def _loss_head(x, tgt, S, D):
    T = _pick(S, (512, 256, 128))

    def body(x_ref, t_ref, loss_ref, d_ref, db_ref):
        @pl.when(pl.program_id(0) == 0)
        def _():
            loss_ref[...] = jnp.zeros_like(loss_ref)

        e = x_ref[...] - t_ref[...]
        loss_ref[...] += 0.5 * jnp.sum(jnp.mean(e * e, axis=-1, keepdims=True), axis=0, keepdims=True)
        d = e * (1.0 / D)
        d_ref[...] = d
        db_ref[...] = d.astype(BF16)

    row = pl.BlockSpec((T, D), lambda i: (i, 0))
    return pl.pallas_call(
        body, name="loss_head", grid=(S // T,), in_specs=[row, row],
        out_specs=[pl.BlockSpec((1, 1), lambda i: (0, 0)), row, row],
        out_shape=[jax.ShapeDtypeStruct((1, 1), F32), jax.ShapeDtypeStruct((S, D), F32),
                   jax.ShapeDtypeStruct((S, D), BF16)],
        compiler_params=_params(("arbitrary",)),
    )(x, tgt)


def _sigmoid(z):
    return 1.0 / (1.0 + jnp.exp(-z))


def _log_sigmoid(z):
    return jnp.minimum(z, 0.0) - jnp.log(1.0 + jnp.exp(-jnp.abs(z)))


_GELU_K = 0.7978845608028654
_GELU_C = 0.044715


def _gelu(x):
    t = jnp.tanh(_GELU_K * (x + _GELU_C * (x * x * x)))
    return 0.5 * x * (1.0 + t)


def _gelu_and_grad(x):
    x2 = x * x
    t = jnp.tanh(_GELU_K * (x + _GELU_C * (x2 * x)))
    g = 0.5 * x * (1.0 + t)
    dg = 0.5 * (1.0 + t) + 0.5 * x * (1.0 - t * t) * (_GELU_K * (1.0 + 3.0 * _GELU_C * x2))
    return g, dg


def _decay_terms(r, ls):
    la = LRU_C * r * ls
    a = jnp.exp(la)
    a2 = jnp.exp(2.0 * la)
    mult = jnp.sqrt(-jnp.tanh(la) * (a2 + 1.0))
    return a, a2, mult


def _lru_fwd(u0, conv_w, conv_b, wr_bd, b_r, wi_bd, b_i, lam, S, D):
    T = _pick(S, (256, 128))
    GT = wr_bd.shape[-1]
    nG = D // GT

    def body(gb_ref, xb_ref, cw_ref, cb_ref, wr_ref, br_ref, wi_ref, bi_ref, lam_ref,
             y_ref, xc_ref, r_ref, i_ref, hs_ref, ext, a_scr, hcar):
        @pl.when(pl.program_id(0) == 0)
        def _():
            ext[0:SUBLANES, :] = jnp.zeros((SUBLANES, D), F32)
            hcar[...] = jnp.zeros_like(hcar)

        xb = xb_ref[...]
        ext[SUBLANES:SUBLANES + T, :] = xb
        xc = cb_ref[...]
        for k in range(CONV_WIDTH):
            xc = xc + ext[pl.ds(SUBLANES - (CONV_WIDTH - 1) + k, T), :] * cw_ref[k:k + 1, :]
        ext[0:SUBLANES, :] = xb[T - SUBLANES:T, :]
        xc_ref[...] = xc
        xcb = xc.astype(BF16)
        for g in range(nG):
            sl = slice(g * GT, (g + 1) * GT)
            zr = jnp.dot(xcb[:, sl], wr_ref[g], preferred_element_type=F32) + br_ref[:, sl]
            zi = jnp.dot(xcb[:, sl], wi_ref[g], preferred_element_type=F32) + bi_ref[:, sl]
            r_ref[:, sl] = _sigmoid(zr)
            i_ref[:, sl] = _sigmoid(zi)
        r = r_ref[...]
        a, _, mult = _decay_terms(r, _log_sigmoid(lam_ref[...]))
        a_scr[...] = a
        hs_ref[...] = mult * (i_ref[...] * xc)

        def step(t, h):
            h = a_scr[pl.ds(t, 1), :] * h + hs_ref[pl.ds(t, 1), :]
            hs_ref[pl.ds(t, 1), :] = h
            return h

        hcar[...] = lax.fori_loop(0, T, step, hcar[...], unroll=8)
        y_ref[...] = (_gelu(gb_ref[...]) * hs_ref[...]).astype(BF16)

    row = pl.BlockSpec((T, D), lambda i: (i, 0))
    vec = pl.BlockSpec((1, D), lambda i: (0, 0))
    bd = pl.BlockSpec((nG, GT, GT), lambda i: (0, 0, 0))
    f32o = jax.ShapeDtypeStruct((S, D), F32)
    return pl.pallas_call(
        body, name="lru_fwd", grid=(S // T,),
        in_specs=[row, pl.BlockSpec((T, D), lambda i: (i, 1)), pl.BlockSpec((CONV_WIDTH, D), lambda i: (0, 0)), vec,
                  bd, vec, bd, vec, vec],
        out_specs=[row, row, row, row, row],
        out_shape=[jax.ShapeDtypeStruct((S, D), BF16), f32o, f32o, f32o, f32o],
        scratch_shapes=[pltpu.VMEM((T + SUBLANES, D), F32), pltpu.VMEM((T, D), F32), pltpu.VMEM((1, D), F32)],
        compiler_params=_params(("arbitrary",)),
    )(u0, u0, conv_w, conv_b, wr_bd, b_r, wi_bd, b_i, lam)


def _lru_bwd(dy, u0, xc, r, ig, hs, conv_w, wr_bd, wi_bd, lam, S, D):
    T = _pick(S, (128,))
    nT = S // T
    GT = wr_bd.shape[-1]
    nG = D // GT
    W = CONV_WIDTH

    def body(dy_ref, gb_ref, xb_ref, xbp_ref, xc_ref, r_ref, i_ref, hs_ref, hsp_ref, cw_ref, wr_ref, wi_ref, lam_ref,
             du_ref, dcw_ref, dcb_ref, dlam_ref, dbr_ref, dbi_ref, dwr_ref, dwi_ref,
             a_scr, dh_scr, exth, extx, extd, dxc_scr, dz_scr, carry):
        step = pl.program_id(0)
        first_tile = step == nT - 1

        @pl.when(step == 0)
        def _():
            for ref in (dcw_ref, dcb_ref, dlam_ref, dbr_ref, dbi_ref, dwr_ref, dwi_ref, carry):
                ref[...] = jnp.zeros_like(ref)
            extd[T:T + SUBLANES, :] = jnp.zeros((SUBLANES, D), F32)

        hs = hs_ref[...]
        dy = dy_ref[...]
        g, dgelu = _gelu_and_grad(gb_ref[...])
        du_ref[:, 0:D] = (dy * hs * dgelu).astype(BF16)
        r = r_ref[...]
        lam = lam_ref[...]
        ls = _log_sigmoid(lam)
        a, a2, mult = _decay_terms(r, ls)
        a_scr[...] = a
        dh_scr[...] = dy * g

        def rstep(j, c):
            t = T - 1 - j
            d = dh_scr[pl.ds(t, 1), :] + c
            dh_scr[pl.ds(t, 1), :] = d
            return a_scr[pl.ds(t, 1), :] * d

        carry[...] = lax.fori_loop(0, T, rstep, carry[...], unroll=8)
        dh = dh_scr[...]
        keep = jnp.where(first_tile, 0.0, 1.0)
        exth[0:SUBLANES, :] = hsp_ref[...] * keep
        exth[SUBLANES:SUBLANES + T, :] = hs
        hprev = exth[pl.ds(SUBLANES - 1, T), :]
        xc = xc_ref[...]
        ig = i_ref[...]
        da = dh * hprev
        dmult = dh * (ig * xc)
        dla = da * a - dmult * (a2 / mult)
        dlam_ref[...] += jnp.sum(dla * r, axis=0, keepdims=True) * (LRU_C * _sigmoid(-lam))
        dzr = (dla * (LRU_C * ls)) * (r * (1.0 - r))
        dzi = (dh * (mult * xc)) * (ig * (1.0 - ig))
        dbr_ref[...] += jnp.sum(dzr, axis=0, keepdims=True)
        dbi_ref[...] += jnp.sum(dzi, axis=0, keepdims=True)
        dxc_scr[...] = dh * (mult * ig)
        xcb = xc.astype(BF16)
        dz_scr[0] = dzr.astype(BF16)
        dz_scr[1] = dzi.astype(BF16)
        nt_dims = (((1,), (1,)), ((), ()))
        tn_dims = (((0,), (0,)), ((), ()))
        for gq in range(nG):
            sl = slice(gq * GT, (gq + 1) * GT)
            zr_g = dz_scr[0, :, sl]
            zi_g = dz_scr[1, :, sl]
            dxc_scr[:, sl] += (lax.dot_general(zr_g, wr_ref[gq], nt_dims, preferred_element_type=F32)
                               + lax.dot_general(zi_g, wi_ref[gq], nt_dims, preferred_element_type=F32))
            dwr_ref[gq] += lax.dot_general(xcb[:, sl], zr_g, tn_dims, preferred_element_type=F32)
            dwi_ref[gq] += lax.dot_general(xcb[:, sl], zi_g, tn_dims, preferred_element_type=F32)
        dxc = dxc_scr[...]
        dcb_ref[...] += jnp.sum(dxc, axis=0, keepdims=True)
        extx[0:SUBLANES, :] = xbp_ref[...] * keep
        extx[SUBLANES:SUBLANES + T, :] = xb_ref[...]
        extd[0:T, :] = dxc
        dxb = jnp.zeros((T, D), F32)
        for k in range(W):
            dxb = dxb + extd[pl.ds(W - 1 - k, T), :] * cw_ref[k:k + 1, :]
            dcw_ref[k:k + 1, :] += jnp.sum(dxc * extx[pl.ds(SUBLANES - (W - 1) + k, T), :], axis=0, keepdims=True)
        extd[T:T + SUBLANES, :] = dxc[0:SUBLANES, :]
        du_ref[:, D:2 * D] = dxb.astype(BF16)

    rev = lambda i: nT - 1 - i
    tpb = T // SUBLANES
    prev8 = lambda i: jnp.maximum(rev(i) * tpb - 1, 0)
    row = pl.BlockSpec((T, D), lambda i: (rev(i), 0))
    vec = pl.BlockSpec((1, D), lambda i: (0, 0))
    bd = pl.BlockSpec((nG, GT, GT), lambda i: (0, 0, 0))
    vec_o = jax.ShapeDtypeStruct((1, D), F32)
    bd_o = jax.ShapeDtypeStruct((nG, GT, GT), F32)
    return pl.pallas_call(
        body, name="lru_bwd", grid=(nT,),
        in_specs=[row, row, pl.BlockSpec((T, D), lambda i: (rev(i), 1)),
                  pl.BlockSpec((SUBLANES, D), lambda i: (prev8(i), 1)),
                  row, row, row, row, pl.BlockSpec((SUBLANES, D), lambda i: (prev8(i), 0)),
                  pl.BlockSpec((W, D), lambda i: (0, 0)), bd, bd, vec],
        out_specs=[pl.BlockSpec((T, 2 * D), lambda i: (rev(i), 0)), pl.BlockSpec((W, D), lambda i: (0, 0)),
                   vec, vec, vec, vec, bd, bd],
        out_shape=[jax.ShapeDtypeStruct((S, 2 * D), BF16), jax.ShapeDtypeStruct((W, D), F32),
                   vec_o, vec_o, vec_o, vec_o, bd_o, bd_o],
        scratch_shapes=[pltpu.VMEM((T, D), F32), pltpu.VMEM((T, D), F32), pltpu.VMEM((T + SUBLANES, D), F32),
                        pltpu.VMEM((T + SUBLANES, D), F32), pltpu.VMEM((T + SUBLANES, D), F32),
                        pltpu.VMEM((T, D), F32), pltpu.VMEM((2, T, D), BF16), pltpu.VMEM((1, D), F32)],
        compiler_params=_params(("arbitrary",)),
    )(dy, u0, u0, u0, xc, r, ig, hs, hs, conv_w, wr_bd, wi_bd, lam)


AUG_ROWS = 16
HEAD_ROWS = 128
LSE_ROW = HEAD_DIM + 6


def _split3(x):
    b1 = x.astype(BF16).astype(F32)
    r = x - b1
    b2 = r.astype(BF16).astype(F32)
    return b1, b2, r - b2


def _head_block(x, aug, T):
    row = lax.broadcasted_iota(jnp.int32, (AUG_ROWS, T), 0)
    blk = jnp.zeros((AUG_ROWS, T), F32)
    for i, e in enumerate(aug):
        blk = jnp.where(row == i, e, blk)
    return jnp.concatenate([x, blk, jnp.zeros((HEAD_ROWS - HEAD_DIM - AUG_ROWS, T), F32)], axis=0)


def _tri_matrix(lower):
    i = np.arange(LANES)
    m = (i[:, None] >= i[None, :]) if lower else (i[:, None] <= i[None, :])
    return jnp.asarray(m.astype(np.float32), BF16)


def _lane_cumsum(x, tri_ref, carry, reverse):
    n = x.shape[1] // LANES
    tri = tri_ref[...]
    out = [None] * n
    for j in (range(n - 1, -1, -1) if reverse else range(n)):
        cs = carry
        for part in _split3(x[:, j * LANES:(j + 1) * LANES]):
            cs = cs + jnp.dot(part.astype(BF16), tri, preferred_element_type=F32)
        out[j] = cs
        carry = cs[:, 0:1] if reverse else cs[:, LANES - 1:LANES]
    return jnp.concatenate(out, axis=1), carry


def _head_rows(h):
    return pl.ds(pl.multiple_of(h * HEAD_DIM, HEAD_DIM), HEAD_DIM)


def _fox_prep(ut, b_f, qg, kg, S, D, tq):
    H = D // HEAD_DIM
    T = min(tq, 256)
    per = tq // T
    scale = HEAD_DIM ** -0.5

    def body(q_ref, k_ref, v_ref, f_ref, bf_ref, qg_ref, kg_ref, tri_ref,
             qat_ref, kat_ref, vat_ref, ka_ref, va_ref, vt_ref, c_scr, ccar):
        @pl.when(pl.program_id(0) == 0)
        def _():
            ccar[...] = jnp.zeros_like(ccar)

        c, carry = _lane_cumsum(_log_sigmoid(f_ref[...] + bf_ref[...]), tri_ref, ccar[...], False)
        c_scr[...] = c
        ccar[...] = carry

        def head(h, _):
            rows = _head_rows(h)
            c1, c2, c3 = _split3(c_scr[pl.ds(h, 1), :])

            def normed(src, gain, mul):
                x = src[rows, :]
                rs = lax.rsqrt(jnp.mean(x * x, axis=0, keepdims=True) + EPS)
                return ((x * rs) * gain[rows, :]) * mul

            qat_ref[h] = _head_block(normed(q_ref, qg_ref, scale), [c1, c2, c3, 1.0, 1.0, 1.0], T).astype(BF16)
            kb = _head_block(normed(k_ref, kg_ref, 1.0), [1.0, 1.0, 1.0, -c1, -c2, -c3, 1.0, 1.0, 1.0], T)
            kat_ref[h] = kb.astype(BF16)
            ka_ref[h] = kb.T.astype(BF16)
            v = v_ref[rows, :]
            vt_ref[h] = v.astype(BF16)
            vb = _head_block(v, [1.0, 1.0, 1.0], T)
            vat_ref[h] = vb.astype(BF16)
            va_ref[h] = vb.T.astype(BF16)
            return 0

        lax.fori_loop(0, H, head, 0)

    part = lambda j: pl.BlockSpec((D, T), lambda i: (j, i))
    colv = lambda n: pl.BlockSpec((n, 1), lambda i: (0, 0))
    tmaj = lambda r: pl.BlockSpec((H, None, r, T), lambda i: (0, i // per, 0, i % per))
    norm = pl.BlockSpec((H, T, HEAD_ROWS), lambda i: (0, i, 0))
    tshape = lambda r: jax.ShapeDtypeStruct((H, S // tq, r, tq), BF16)
    nshape = jax.ShapeDtypeStruct((H, S, HEAD_ROWS), BF16)
    return pl.pallas_call(
        body, name="fox_prep", grid=(S // T,),
        in_specs=[part(0), part(1), part(2), pl.BlockSpec((LANES, T), lambda i: (3 * D // LANES, i)),
                  colv(LANES), colv(D), colv(D), pl.BlockSpec((LANES, LANES), lambda i: (0, 0))],
        out_specs=[tmaj(HEAD_ROWS), tmaj(HEAD_ROWS), tmaj(HEAD_ROWS), norm, norm, tmaj(HEAD_DIM)],
        out_shape=[tshape(HEAD_ROWS), tshape(HEAD_ROWS), tshape(HEAD_ROWS), nshape, nshape, tshape(HEAD_DIM)],
        scratch_shapes=[pltpu.VMEM((LANES, T), F32), pltpu.VMEM((LANES, 1), F32)],
        compiler_params=_params(("arbitrary",)),
    )(ut, ut, ut, ut, b_f, qg, kg, _tri_matrix(False))


def _fox_bwd_prep(dot, ot, lse, qat, S, D, tq):
    H = D // HEAD_DIM
    T = min(tq, 256)
    per = tq // T

    def body(do_ref, o_ref, lse_ref, qat_ref, doat_ref, doa_ref, qat1_ref, qa1_ref):
        row = lax.broadcasted_iota(jnp.int32, (HEAD_ROWS, T), 0)

        def head(h, _):
            rows = _head_rows(h)
            do = do_ref[rows, :].astype(F32)
            delta = jnp.sum(do * o_ref[rows, :], axis=0, keepdims=True)
            db = _head_block(do, list(_split3(-delta)), T)
            doat_ref[h] = db.astype(BF16)
            doa_ref[h] = db.T.astype(BF16)
            qb = qat_ref[h].astype(F32)
            for i, e in enumerate(_split3(-lse_ref[h])):
                qb = jnp.where(row == LSE_ROW + i, e, qb)
            qat1_ref[h] = qb.astype(BF16)
            qa1_ref[h] = qb.T.astype(BF16)
            return 0

        lax.fori_loop(0, H, head, 0)

    chan = pl.BlockSpec((D, T), lambda i: (0, i))
    tmaj = pl.BlockSpec((H, None, HEAD_ROWS, T), lambda i: (0, i // per, 0, i % per))
    norm = pl.BlockSpec((H, T, HEAD_ROWS), lambda i: (0, i, 0))
    tshape = jax.ShapeDtypeStruct((H, S // tq, HEAD_ROWS, tq), BF16)
    nshape = jax.ShapeDtypeStruct((H, S, HEAD_ROWS), BF16)
    return pl.pallas_call(
        body, name="fox_bwd_prep", grid=(S // T,),
        in_specs=[chan, chan, pl.BlockSpec((H, 1, T), lambda i: (0, 0, i)), tmaj],
        out_specs=[tmaj, norm, tmaj, norm], out_shape=[tshape, nshape, tshape, nshape],
        compiler_params=_params(("arbitrary",)),
    )(dot, ot, lse, qat)


def _causal(s, k_axis):
    ki = lax.broadcasted_iota(jnp.int32, s.shape, k_axis)
    qi = lax.broadcasted_iota(jnp.int32, s.shape, 1 - k_axis)
    return jnp.where(ki <= qi, s, NEG_INF)


def _seq_tile(i, t):
    return pl.ds(pl.multiple_of(i * t, t), t)


def _attn_forward(ka, qat, vt, S, D, tq):
    H = D // HEAD_DIM
    nq = S // tq

    def body(ka_ref, qat_ref, vt_ref, o_ref, o32_ref, lse_ref, m_scr, l_scr, acc_scr):
        qi = pl.program_id(1)
        m_scr[...] = jnp.full_like(m_scr, NEG_INF)
        l_scr[...] = jnp.zeros_like(l_scr)
        acc_scr[...] = jnp.zeros_like(acc_scr)
        qa = qat_ref[...]

        def tile(ki, diagonal):
            s = jnp.dot(ka_ref[_seq_tile(ki, tq), :], qa, preferred_element_type=F32)
            if diagonal:
                s = _causal(s, 0)
            m_prev = m_scr[...]
            m_new = jnp.maximum(m_prev, jnp.max(s, axis=0, keepdims=True))
            alpha = jnp.exp(m_prev - m_new)
            p = jnp.exp(s - m_new)
            l_scr[...] = alpha * l_scr[...] + jnp.sum(p, axis=0, keepdims=True)
            acc_scr[...] = alpha * acc_scr[...] + jnp.dot(vt_ref[ki], p.astype(BF16), preferred_element_type=F32)
            m_scr[...] = m_new

        def off_diagonal(ki, _):
            tile(ki, False)
            return 0

        lax.fori_loop(0, qi, off_diagonal, 0)
        tile(qi, True)
        o = acc_scr[...] / l_scr[...]
        o_ref[...] = o.astype(BF16)
        o32_ref[...] = o
        lse_ref[...] = m_scr[...] + jnp.log(l_scr[...])

    chan = pl.BlockSpec((HEAD_DIM, tq), lambda h, i: (h, i))
    stat = pl.BlockSpec((None, 1, tq), lambda h, i: (h, 0, i))
    return pl.pallas_call(
        body, name="attn_forward", grid=(H, nq),
        in_specs=[pl.BlockSpec((None, S, HEAD_ROWS), lambda h, i: (h, 0, 0)),
                  pl.BlockSpec((None, None, HEAD_ROWS, tq), lambda h, i: (h, i, 0, 0)),
                  pl.BlockSpec((None, nq, HEAD_DIM, tq), lambda h, i: (h, 0, 0, 0))],
        out_specs=[chan, chan, stat],
        out_shape=[jax.ShapeDtypeStruct((D, S), BF16), jax.ShapeDtypeStruct((D, S), F32),
                   jax.ShapeDtypeStruct((H, 1, S), F32)],
        scratch_shapes=[pltpu.VMEM((1, tq), F32), pltpu.VMEM((1, tq), F32), pltpu.VMEM((HEAD_DIM, tq), F32)],
        compiler_params=_params(("arbitrary", "arbitrary")),
    )(ka, qat, vt)


def _attn_backward_q(ka, va, kat, qat, doat, S, D, tq):
    H = D // HEAD_DIM
    nq = S // tq

    def body(ka_ref, va_ref, kat_ref, qat_ref, doat_ref, dq_ref, dcq_ref, dq_scr, rs_scr):
        qi = pl.program_id(1)
        dq_scr[...] = jnp.zeros_like(dq_scr)
        rs_scr[...] = jnp.zeros_like(rs_scr)
        qa = qat_ref[...]
        doa = doat_ref[...]

        def tile(ki, diagonal):
            rows = _seq_tile(ki, tq)
            s = jnp.dot(ka_ref[rows, :], qa, preferred_element_type=F32)
            if diagonal:
                s = _causal(s, 0)
            ds = jnp.exp(s) * jnp.dot(va_ref[rows, :], doa, preferred_element_type=F32)
            rs_scr[...] += jnp.sum(ds, axis=0, keepdims=True)
            dq_scr[...] += jnp.dot(kat_ref[ki, 0:HEAD_DIM, :], ds.astype(BF16), preferred_element_type=F32)

        def off_diagonal(ki, _):
            tile(ki, False)
            return 0

        lax.fori_loop(0, qi, off_diagonal, 0)
        tile(qi, True)
        dq_ref[...] = dq_scr[...]
        dcq_ref[...] = rs_scr[...]

    whole = pl.BlockSpec((None, S, HEAD_ROWS), lambda h, i: (h, 0, 0))
    one = pl.BlockSpec((None, None, HEAD_ROWS, tq), lambda h, i: (h, i, 0, 0))
    return pl.pallas_call(
        body, name="attn_backward_q", grid=(H, nq),
        in_specs=[whole, whole, pl.BlockSpec((None, nq, HEAD_ROWS, tq), lambda h, i: (h, 0, 0, 0)), one, one],
        out_specs=[pl.BlockSpec((HEAD_DIM, tq), lambda h, i: (h, i)), pl.BlockSpec((None, 1, tq), lambda h, i: (h, 0, i))],
        out_shape=[jax.ShapeDtypeStruct((D, S), F32), jax.ShapeDtypeStruct((H, 1, S), F32)],
        scratch_shapes=[pltpu.VMEM((HEAD_DIM, tq), F32), pltpu.VMEM((1, tq), F32)],
        compiler_params=_params(("arbitrary", "arbitrary")),
    )(ka, va, kat, qat, doat)


def _attn_backward_kv(qa, doa, qat, doat, kat, vat, S, D, tq):
    H = D // HEAD_DIM
    nq = S // tq

    def body(qa_ref, doa_ref, qat_ref, doat_ref, kat_ref, vat_ref, dk_ref, dv_ref, dck_ref, dk_scr, dv_scr, cs_scr):
        ki = pl.program_id(1)
        dk_scr[...] = jnp.zeros_like(dk_scr)
        dv_scr[...] = jnp.zeros_like(dv_scr)
        cs_scr[...] = jnp.zeros_like(cs_scr)
        ka = kat_ref[...]
        va = vat_ref[...]

        def tile(qi, diagonal):
            rows = _seq_tile(qi, tq)
            s = jnp.dot(qa_ref[rows, :], ka, preferred_element_type=F32)
            if diagonal:
                s = _causal(s, 1)
            p = jnp.exp(s)
            ds = p * jnp.dot(doa_ref[rows, :], va, preferred_element_type=F32)
            dv_scr[...] += jnp.dot(doat_ref[qi, 0:HEAD_DIM, :], p.astype(BF16), preferred_element_type=F32)
            dk_scr[...] += jnp.dot(qat_ref[qi, 0:HEAD_DIM, :], ds.astype(BF16), preferred_element_type=F32)
            cs_scr[...] += jnp.sum(ds, axis=0, keepdims=True)

        def off_diagonal(qi, _):
            tile(qi, False)
            return 0

        tile(ki, True)
        lax.fori_loop(ki + 1, nq, off_diagonal, 0)
        dk_ref[...] = dk_scr[...]
        dv_ref[...] = dv_scr[...].astype(BF16)
        dck_ref[...] = cs_scr[...]

    whole = pl.BlockSpec((None, S, HEAD_ROWS), lambda h, i: (h, 0, 0))
    tiles = pl.BlockSpec((None, nq, HEAD_ROWS, tq), lambda h, i: (h, 0, 0, 0))
    one = pl.BlockSpec((None, None, HEAD_ROWS, tq), lambda h, i: (h, i, 0, 0))
    chan = pl.BlockSpec((HEAD_DIM, tq), lambda h, i: (h, i))
    return pl.pallas_call(
        body, name="attn_backward_kv", grid=(H, nq),
        in_specs=[whole, whole, tiles, tiles, one, one],
        out_specs=[chan, chan, pl.BlockSpec((None, 1, tq), lambda h, i: (h, 0, i))],
        out_shape=[jax.ShapeDtypeStruct((D, S), F32), jax.ShapeDtypeStruct((D, S), BF16),
                   jax.ShapeDtypeStruct((H, 1, S), F32)],
        scratch_shapes=[pltpu.VMEM((HEAD_DIM, tq), F32), pltpu.VMEM((HEAD_DIM, tq), F32), pltpu.VMEM((1, tq), F32)],
        compiler_params=_params(("arbitrary", "arbitrary")),
    )(qa, doa, qat, doat, kat, vat)


def _fox_prep_bwd(ut, dqt, dkt, dvt, dcq, dck, b_f, qg, kg, S, D, tq):
    H = D // HEAD_DIM
    T = min(tq, 256)
    nT = S // T
    NU = 3 * D + LANES
    scale = HEAD_DIM ** -0.5

    def body(q_ref, k_ref, f_ref, dq_ref, dk_ref, dv_ref, dcq_ref, dck_ref, bf_ref, qg_ref, kg_ref, tri_ref,
             du_ref, dbf_ref, dqg_ref, dkg_ref, gq_acc, gk_acc, fcar, dc_scr):
        step = pl.program_id(0)

        @pl.when(step == 0)
        def _():
            for ref in (gq_acc, gk_acc, fcar, dbf_ref):
                ref[...] = jnp.zeros_like(ref)

        dc_scr[...] = jnp.zeros_like(dc_scr)

        def head(h, _):
            rows = _head_rows(h)
            dc_scr[pl.ds(h, 1), :] = dcq_ref[h] - dck_ref[h]
            for src, dsrc, gain, acc, mul, base in ((q_ref, dq_ref, qg_ref, gq_acc, scale, 0),
                                                    (k_ref, dk_ref, kg_ref, gk_acc, 1.0, D)):
                x = src[rows, :]
                rs = lax.rsqrt(jnp.mean(x * x, axis=0, keepdims=True) + EPS)
                xhat = x * rs
                dn = dsrc[rows, :] * mul
                acc[rows, :] += jnp.sum(dn * xhat, axis=1, keepdims=True)
                dxh = dn * gain[rows, :]
                dx = rs * (dxh - xhat * jnp.mean(dxh * xhat, axis=0, keepdims=True))
                du_ref[pl.ds(pl.multiple_of(base + h * HEAD_DIM, HEAD_DIM), HEAD_DIM), :] = dx.astype(BF16)
            return 0

        lax.fori_loop(0, H, head, 0)
        du_ref[2 * D:3 * D, :] = dv_ref[...]
        dlf, carry = _lane_cumsum(dc_scr[...], tri_ref, fcar[...], True)
        fcar[...] = carry
        dfl = dlf * _sigmoid(-(f_ref[...] + bf_ref[...]))
        dbf_ref[...] += jnp.sum(dfl, axis=1, keepdims=True)
        du_ref[3 * D:NU, :] = dfl.astype(BF16)

        @pl.when(step == nT - 1)
        def _():
            for acc, ref in ((gq_acc, dqg_ref), (gk_acc, dkg_ref)):
                tot = jnp.zeros((HEAD_DIM, 1), F32)
                for h in range(H):
                    tot = tot + acc[h * HEAD_DIM:(h + 1) * HEAD_DIM, :]
                ref[...] = tot

    rev = lambda i: nT - 1 - i
    part = lambda j: pl.BlockSpec((D, T), lambda i: (j, rev(i)))
    chan = pl.BlockSpec((D, T), lambda i: (0, rev(i)))
    stat = pl.BlockSpec((H, 1, T), lambda i: (0, 0, rev(i)))
    colv = lambda n: pl.BlockSpec((n, 1), lambda i: (0, 0))
    return pl.pallas_call(
        body, name="fox_prep_bwd", grid=(nT,),
        in_specs=[part(0), part(1), pl.BlockSpec((LANES, T), lambda i: (3 * D // LANES, rev(i))), chan, chan, chan,
                  stat, stat, colv(LANES), colv(D), colv(D), pl.BlockSpec((LANES, LANES), lambda i: (0, 0))],
        out_specs=[pl.BlockSpec((NU, T), lambda i: (0, rev(i))), colv(LANES), colv(HEAD_DIM), colv(HEAD_DIM)],
        out_shape=[jax.ShapeDtypeStruct((NU, S), BF16), jax.ShapeDtypeStruct((LANES, 1), F32),
                   jax.ShapeDtypeStruct((HEAD_DIM, 1), F32), jax.ShapeDtypeStruct((HEAD_DIM, 1), F32)],
        scratch_shapes=[pltpu.VMEM((D, 1), F32), pltpu.VMEM((D, 1), F32), pltpu.VMEM((LANES, 1), F32),
                        pltpu.VMEM((LANES, T), F32)],
        compiler_params=_params(("arbitrary",)),
    )(ut, ut, ut, dqt, dkt, dvt, dcq, dck, b_f, qg, kg, _tri_matrix(True))


def _block_diag_tiles(w):
    n = w.shape[0]
    per = min(MXU_DIM, n * LRU_BLOCK_DIM) // LRU_BLOCK_DIM
    eye = jnp.eye(per, dtype=w.dtype)
    w5 = w.reshape(n // per, per, LRU_BLOCK_DIM, 1, LRU_BLOCK_DIM) * eye[None, :, None, :, None]
    return w5.reshape(n // per, per * LRU_BLOCK_DIM, per * LRU_BLOCK_DIM).astype(BF16)


def _block_diag_extract(t, n):
    per = t.shape[-1] // LRU_BLOCK_DIM
    eye = jnp.eye(per, dtype=t.dtype)
    t5 = t.reshape(n // per, per, LRU_BLOCK_DIM, per, LRU_BLOCK_DIM) * eye[None, :, None, :, None]
    return t5.sum(axis=3).reshape(n, LRU_BLOCK_DIM, LRU_BLOCK_DIM)


def _local_step(x, tgt, small, wv, grad_view):
    S, D = x.shape
    F = 4 * D
    H = D // HEAD_DIM
    nblk = D // LRU_BLOCK_DIM
    NU = 3 * D + LANES
    tq = max(LANES, min(512, S // 4))
    assert S % tq == 0
    vec = lambda a: a.reshape(1, -1).astype(F32)
    col = lambda a: a.reshape(-1, 1).astype(F32)
    mix_g, mlp_g = small["mix_norm"], small["mlp_norm"]
    conv_w, conv_b = small["conv_w"], vec(small["lru_conv_b"])
    wr_bd, wi_bd = _block_diag_tiles(small["lru_w_r"][0]), _block_diag_tiles(small["lru_w_i"][0])
    b_r, b_i, lam = vec(small["lru_b_r"]), vec(small["lru_b_i"]), vec(small["lru_lambda"])
    b_f = jnp.pad(col(small["fox_b_f"]), ((0, LANES - H), (0, 0)))
    qg, kg = jnp.tile(col(small["fox_q_gain"]), (H, 1)), jnp.tile(col(small["fox_k_gain"]), (H, 1))
    X = lambda a: _View(a)
    grads = {}
    gout = functools.partial(grad_view, grads)

    def mlp_fwd(l, xin):
        hm = _rms_fwd(f"mlp{l}_norm", xin, mlp_g[l:l + 1], S, D)
        z, act = _matmul(f"mlp{l}_up", X(hm), wv[f"w1_{l}"], S, F, D, outs=[_fresh(S, F, BF16), _fresh(S, F, BF16)],
                         epilogue=_ep_relu2)
        (xout,) = _matmul(f"mlp{l}_down", X(act), wv[f"w2_{l}"], S, D, F, outs=[_fresh(S, D, F32)],
                          epilogue=_ep_resid, extras=[X(xin)])
        return hm, z, act, xout

    def mlp_bwd(l, xin, hm, z, act, d, db):
        (dz,) = _matmul(f"mlp{l}_dact", X(db), wv[f"w2_{l}"], S, F, D, tb=True, outs=[_fresh(S, F, BF16)],
                        epilogue=_ep_drelu2, extras=[X(z)])
        (grads[f"w2_{l}"],) = _matmul(f"mlp{l}_dw2", X(act), X(db), F, D, S, ta=True, outs=[gout(f"w2_{l}")],
                                      epilogue=_ep_store)
        (grads[f"w1_{l}"],) = _matmul(f"mlp{l}_dw1", X(hm), X(dz), D, F, S, ta=True, outs=[gout(f"w1_{l}")],
                                      epilogue=_ep_store)
        (dhm,) = _matmul(f"mlp{l}_dhm", X(dz), wv[f"w1_{l}"], S, D, F, tb=True, outs=[_fresh(S, D, F32)],
                         epilogue=_ep_store)
        return _rms_bwd(f"mlp{l}_norm_bwd", dhm, xin, mlp_g[l:l + 1], d, S, D)

    h0 = _rms_fwd("mix0_norm", x, mix_g[0:1], S, D)
    (u0,) = _matmul("lru_in", X(h0), wv["lru_in"], S, 2 * D, D, outs=[_fresh(S, 2 * D, F32)], epilogue=_ep_store)
    y, xc, r, ig, hs = _lru_fwd(u0, conv_w, conv_b, wr_bd, b_r, wi_bd, b_i, lam, S, D)
    (x1,) = _matmul("lru_out", X(y), wv["lru_out"], S, D, D, outs=[_fresh(S, D, F32)], epilogue=_ep_resid,
                    extras=[X(x)])
    hm0, z0, act0, x2 = mlp_fwd(0, x1)
    h1 = _rms_fwd("mix1_norm", x2, mix_g[1:2], S, D)
    (u1,) = _matmul("fox_in", wv["fox_in"], X(h1), NU, S, D, tb=True, outs=[_fresh(NU, S, F32)], epilogue=_ep_store)
    qat, kat, vat, ka, va, vt = _fox_prep(u1, b_f, qg, kg, S, D, tq)
    o, o32, lse = _attn_forward(ka, qat, vt, S, D, tq)
    (x3,) = _matmul("fox_out", X(o), wv["fox_out"], S, D, D, ta=True, outs=[_fresh(S, D, F32)], epilogue=_ep_resid,
                    extras=[X(x2)])
    hm1, z1, act1, x4 = mlp_fwd(1, x3)
    loss, d4, d4b = _loss_head(x4, tgt, S, D)

    d3, d3b, dg_mlp1 = mlp_bwd(1, x3, hm1, z1, act1, d4, d4b)
    (do,) = _matmul("fox_dout", wv["fox_out"], X(d3b), D, S, D, tb=True, outs=[_fresh(D, S, BF16)], epilogue=_ep_store)
    (grads["fox_out"],) = _matmul("fox_dwout", X(o), X(d3b), D, D, S, outs=[gout("fox_out")], epilogue=_ep_store)
    doat, doa, qat1, qa1 = _fox_bwd_prep(do, o32, lse, qat, S, D, tq)
    dqn, dcq = _attn_backward_q(ka, va, kat, qat1, doat, S, D, tq)
    dkn, dv, dck = _attn_backward_kv(qa1, doa, qat1, doat, kat, vat, S, D, tq)
    du1, dbf, dqg, dkg = _fox_prep_bwd(u1, dqn, dkn, dv, dcq, dck, b_f, qg, kg, S, D, tq)
    (grads["fox_in"],) = _matmul("fox_dwin", X(h1), X(du1), D, NU, S, ta=True, tb=True, outs=[gout("fox_in")],
                                 epilogue=_ep_store)
    (dh1,) = _matmul("fox_dh", X(du1), wv["fox_in"], S, D, NU, ta=True, outs=[_fresh(S, D, F32)], epilogue=_ep_store)
    d2, d2b, dg_mix1 = _rms_bwd("mix1_norm_bwd", dh1, x2, mix_g[1:2], d3, S, D)
    d1, d1b, dg_mlp0 = mlp_bwd(0, x1, hm0, z0, act0, d2, d2b)
    (dy,) = _matmul("lru_dout", X(d1b), wv["lru_out"], S, D, D, tb=True, outs=[_fresh(S, D, F32)], epilogue=_ep_store)
    (grads["lru_out"],) = _matmul("lru_dwout", X(y), X(d1b), D, D, S, ta=True, outs=[gout("lru_out")],
                                  epilogue=_ep_store)
    du0, dcw, dcb, dlam, dbr, dbi, dwr, dwi = _lru_bwd(dy, u0, xc, r, ig, hs, conv_w, wr_bd, wi_bd, lam, S, D)
    (grads["lru_in"],) = _matmul("lru_dwin", X(h0), X(du0), D, 2 * D, S, ta=True, outs=[gout("lru_in")],
                                 epilogue=_ep_store)
    (dh0,) = _matmul("lru_dh", X(du0), wv["lru_in"], S, D, 2 * D, tb=True, outs=[_fresh(S, D, F32)], epilogue=_ep_store)
    gx, _, dg_mix0 = _rms_bwd("mix0_norm_bwd", dh0, x, mix_g[0:1], d1, S, D)

    grads.update(
        mix_norm=jnp.concatenate([dg_mix0, dg_mix1], axis=0), mlp_norm=jnp.concatenate([dg_mlp0, dg_mlp1], axis=0),
        conv_w=dcw, lru_conv_b=dcb, lru_w_r=_block_diag_extract(dwr, nblk)[None], lru_b_r=dbr.reshape(1, nblk, -1),
        lru_w_i=_block_diag_extract(dwi, nblk)[None], lru_b_i=dbi.reshape(1, nblk, -1), lru_lambda=dlam,
        fox_b_f=dbf[:H].reshape(1, H), fox_q_gain=dqg.reshape(1, -1), fox_k_gain=dkg.reshape(1, -1))
    return loss, gx, grads


def _place():
    x, y, c = lax.axis_index("x"), lax.axis_index("y"), lax.axis_index("c")
    chips = [(1 - x, y), (x, 1 - y), (1 - x, 1 - y)]
    return x, y, c, 2 * x + y, chips


BOUNCE_BYTES = 1 << 20


def _bounce_shape(rows, cols, dtype):
    chunk = rows
    while chunk % 2 == 0 and chunk > 16 and chunk * cols * jnp.dtype(dtype).itemsize > BOUNCE_BYTES:
        chunk //= 2
    return pltpu.VMEM((2, chunk, cols), dtype)


def _bounce_copy(src, dst, buf, sem):
    chunk = buf.shape[1]
    n = src.shape[0] // chunk
    cin = lambda i: pltpu.make_async_copy(src.at[pl.ds(i * chunk, chunk)], buf.at[i % 2], sem.at[i % 2])
    cout = lambda i: pltpu.make_async_copy(buf.at[i % 2], dst.at[pl.ds(i * chunk, chunk)], sem.at[2 + i % 2])
    cin(0).start()
    for i in range(n):
        cin(i).wait()
        if i + 1 < n:
            if i >= 1:
                cout(i - 1).wait()
            cin(i + 1).start()
        cout(i).start()
    if n >= 2:
        cout(n - 2).wait()
    cout(n - 1).wait()


def _hbm_call(body, name, arrays, out_shape, n_dma_sems, bounce=()):
    scratch = [pltpu.SemaphoreType.DMA((k,)) for k in n_dma_sems]
    for rows, cols, dtype in bounce:
        scratch += [_bounce_shape(rows, cols, dtype), pltpu.SemaphoreType.DMA((4,))]
    return pl.pallas_call(
        body, name=name, in_specs=[ANY] * len(arrays), out_specs=[ANY] * len(out_shape), out_shape=out_shape,
        scratch_shapes=scratch,
        compiler_params=pltpu.CompilerParams(has_side_effects=True, vmem_limit_bytes=VMEM_LIMIT),
    )(*arrays)


def _all_gather(name, shards):
    n = len(shards)

    def body(*refs):
        ins, outs = refs[:n], refs[n:2 * n]
        send, recv, fsend, frecv = refs[2 * n:2 * n + 4]
        stage = refs[2 * n + 4:]
        x, y, c, s, chips = _place()
        sibling = (x, y, 1 - c)

        def rows(a, chip_idx, which):
            hr = ins[a].shape[0] // 2
            return outs[a].at[chip_idx, pl.ds(which * hr, hr)]

        def ici(a, j, src, dst, to):
            return pltpu.make_async_remote_copy(src_ref=src, dst_ref=dst, send_sem=send.at[3 * a + j],
                                                recv_sem=recv.at[3 * a + j], device_id=to, device_id_type=MESH)

        def d2d(a, j, src, dst):
            return pltpu.make_async_remote_copy(src_ref=src, dst_ref=dst, send_sem=fsend.at[3 * a + j],
                                                recv_sem=frecv.at[3 * a + j], device_id=sibling, device_id_type=MESH)

        started = []
        for a in range(n):
            hr = ins[a].shape[0] // 2
            for j, chip in enumerate(chips):
                cp = ici(a, j, ins[a].at[pl.ds(c * hr, hr)], rows(a, s, c), (*chip, c))
                cp.start()
                started.append(cp)
        for a in range(n):
            _bounce_copy(ins[a], outs[a].at[s], stage[2 * a], stage[2 * a + 1])
        for a in range(n):
            for j, chip in enumerate(chips):
                got = rows(a, 2 * chip[0] + chip[1], c)
                ici(a, j, got, got, (*chip, c)).wait_recv()
                fw = d2d(a, j, got, got)
                fw.start()
                started.append(fw)
        for a in range(n):
            for j, chip in enumerate(chips):
                theirs = rows(a, 2 * chip[0] + chip[1], 1 - c)
                d2d(a, j, theirs, theirs).wait_recv()
        for cp in started:
            cp.wait_send()

    out_shape = [jax.ShapeDtypeStruct((N_CHIPS,) + tuple(a.shape), a.dtype) for a in shards]
    return _hbm_call(body, name, shards, out_shape, (3 * n, 3 * n, 3 * n, 3 * n),
                     bounce=[(a.shape[0], a.shape[1], a.dtype) for a in shards])


def _pair_swap(name, arrs):
    n = len(arrs)

    def body(*refs):
        ins, outs = refs[:n], refs[n:2 * n]
        send, recv = refs[2 * n:]
        x, y, c, _, _ = _place()
        cps = []
        for a in range(n):
            hr = ins[a].shape[1] // 2
            cp = pltpu.make_async_remote_copy(
                src_ref=ins[a].at[:, pl.ds((1 - c) * hr, hr)], dst_ref=outs[a], send_sem=send.at[a],
                recv_sem=recv.at[a], device_id=(x, y, 1 - c), device_id_type=MESH)
            cp.start()
            cps.append(cp)
        for cp in cps:
            cp.wait()

    out_shape = [jax.ShapeDtypeStruct((a.shape[0], a.shape[1] // 2, a.shape[2]), a.dtype) for a in arrs]
    return _hbm_call(body, name, arrs, out_shape, (n, n))


def _chip_scatter(name, parts):
    n = len(parts)

    def body(*refs):
        ins, outs = refs[:n], refs[n:2 * n]
        send, recv = refs[2 * n:2 * n + 2]
        stage = refs[2 * n + 2:]
        x, y, c, s, chips = _place()
        cps = []
        for a in range(n):
            for j, chip in enumerate(chips):
                t = 2 * chip[0] + chip[1]
                cp = pltpu.make_async_remote_copy(
                    src_ref=ins[a].at[t], dst_ref=outs[a].at[s], send_sem=send.at[3 * a + j],
                    recv_sem=recv.at[3 * a + j], device_id=(*chip, c), device_id_type=MESH)
                cp.start()
                cps.append(cp)
        for a in range(n):
            _bounce_copy(ins[a].at[s], outs[a].at[s], stage[2 * a], stage[2 * a + 1])
        for a in range(n):
            for j, chip in enumerate(chips):
                t = 2 * chip[0] + chip[1]
                pltpu.make_async_remote_copy(
                    src_ref=ins[a].at[t], dst_ref=outs[a].at[t], send_sem=send.at[3 * a + j],
                    recv_sem=recv.at[3 * a + j], device_id=(*chip, c), device_id_type=MESH).wait_recv()
        for cp in cps:
            cp.wait_send()

    out_shape = [jax.ShapeDtypeStruct(a.shape, a.dtype) for a in parts]
    return _hbm_call(body, name, parts, out_shape, (3 * n, 3 * n),
                     bounce=[(a.shape[1], a.shape[2], a.dtype) for a in parts])


def _pair_gather(name, halves):
    n = len(halves)

    def body(*refs):
        ins, outs = refs[:n], refs[n:2 * n]
        send, recv = refs[2 * n:2 * n + 2]
        stage = refs[2 * n + 2:]
        x, y, c, _, _ = _place()
        cps = []
        for a in range(n):
            hr = ins[a].shape[0]
            cp = pltpu.make_async_remote_copy(
                src_ref=ins[a], dst_ref=outs[a].at[pl.ds(c * hr, hr)], send_sem=send.at[a], recv_sem=recv.at[a],
                device_id=(x, y, 1 - c), device_id_type=MESH)
            cp.start()
            cps.append((cp, hr))
        for a, (cp, hr) in enumerate(cps):
            _bounce_copy(ins[a], outs[a].at[pl.ds(c * hr, hr)], stage[2 * a], stage[2 * a + 1])
        for a, (cp, hr) in enumerate(cps):
            cp.wait_send()
            theirs = outs[a].at[pl.ds((1 - c) * hr, hr)]
            pltpu.make_async_remote_copy(src_ref=theirs, dst_ref=theirs, send_sem=send.at[a], recv_sem=recv.at[a],
                                         device_id=(x, y, 1 - c), device_id_type=MESH).wait_recv()

    out_shape = [jax.ShapeDtypeStruct((2 * a.shape[0], a.shape[1]), a.dtype) for a in halves]
    return _hbm_call(body, name, halves, out_shape, (n, n),
                     bounce=[(a.shape[0], a.shape[1], a.dtype) for a in halves])


def _row_tile(rows, cols, itemsize, n_bufs):
    budget = VMEM_LIMIT // 2
    for t in (1024, 512, 256, 128, 64, 32, 16):
        if rows % t == 0 and 2 * n_bufs * t * cols * itemsize <= budget:
            return t
    return rows


def _pair_add(name, g, gsib, core, out_dtype):
    _, r, cols = g.shape
    hr = r // 2
    t = _row_tile(hr, cols, 4, 3)
    per = hr // t

    def body(core_ref, a_ref, b_ref, o_ref):
        o_ref[...] = (a_ref[...].astype(F32) + b_ref[...].astype(F32)).astype(o_ref.dtype)

    grid_spec = pltpu.PrefetchScalarGridSpec(
        num_scalar_prefetch=1, grid=(N_CHIPS, per),
        in_specs=[pl.BlockSpec((None, t, cols), lambda s, i, core: (s, core[0] * per + i, 0)),
                  pl.BlockSpec((None, t, cols), lambda s, i, core: (s, i, 0))],
        out_specs=pl.BlockSpec((None, t, cols), lambda s, i, core: (s, i, 0)))
    return pl.pallas_call(body, name=name, grid_spec=grid_spec,
                          out_shape=jax.ShapeDtypeStruct((N_CHIPS, hr, cols), out_dtype),
                          compiler_params=_params(("arbitrary", "arbitrary")))(core, g, gsib)


def _chip_sum(name, parts):
    _, hr, cols = parts.shape
    t = _row_tile(hr, cols, 4, 5)

    def body(p_ref, o_ref):
        o_ref[...] = ((p_ref[0].astype(F32) + p_ref[1].astype(F32)) + p_ref[2].astype(F32)) + p_ref[3].astype(F32)

    return pl.pallas_call(
        body, name=name, grid=(hr // t,), in_specs=[pl.BlockSpec((N_CHIPS, t, cols), lambda i: (0, i, 0))],
        out_specs=pl.BlockSpec((t, cols), lambda i: (i, 0)), out_shape=jax.ShapeDtypeStruct((hr, cols), F32),
        compiler_params=_params(("arbitrary",)))(parts)


def _reduce_scatter(tag, arrs, wire_dtypes, core):
    sib = _pair_swap(f"{tag}_pair_swap", arrs)
    parts = [_pair_add(f"{tag}_pair_add{i}", g, gs, core, dt) for i, (g, gs, dt) in enumerate(zip(arrs, sib, wire_dtypes))]
    got = _chip_scatter(f"{tag}_chip_scatter", parts)
    halves = [_chip_sum(f"{tag}_chip_sum{i}", p) for i, p in enumerate(got)]
    return _pair_gather(f"{tag}_pair_gather", halves)


def _adamw(name, w, g, m, v):
    rows, cols = w.shape
    t = _row_tile(rows, cols, 4, 8)
    c1 = 1.0 - ADAM_B1 ** ADAM_STEP
    c2 = 1.0 - ADAM_B2 ** ADAM_STEP

    def body(w_ref, g_ref, m_ref, v_ref, go_ref, d_ref, nm_ref, nv_ref):
        g = g_ref[...]
        go_ref[...] = g
        m = ADAM_B1 * m_ref[...] + (1.0 - ADAM_B1) * g
        v = ADAM_B2 * v_ref[...] + (1.0 - ADAM_B2) * (g * g)
        nm_ref[...] = m
        nv_ref[...] = v
        d_ref[...] = -ADAM_LR * ((m / c1) / (jnp.sqrt(v / c2) + ADAM_EPS) + ADAM_WD * w_ref[...])

    spec = pl.BlockSpec((t, cols), lambda i: (i, 0))
    shp = jax.ShapeDtypeStruct((rows, cols), F32)
    return pl.pallas_call(body, name=name, grid=(rows // t,), in_specs=[spec] * 4, out_specs=[spec] * 4,
                          out_shape=[shp] * 4, compiler_params=_params(("arbitrary",)))(w, g, m, v)


_WEIGHTS = ["mix_norm", "mlp_norm", "mlp_w1", "mlp_w2", "lru_w_in", "lru_conv_w", "lru_conv_b", "lru_w_r", "lru_b_r",
            "lru_w_i", "lru_b_i", "lru_lambda", "lru_w_out", "fox_w_in", "fox_b_f", "fox_q_gain", "fox_k_gain",
            "fox_w_out"]
_REPLICATED = ["mix_norm", "mlp_norm", "lru_conv_b", "lru_w_r", "lru_b_r", "lru_w_i", "lru_b_i", "lru_lambda",
               "fox_b_f", "fox_q_gain", "fox_k_gain"]
_PACK_TILE = 2 * SUBLANES * LANES


def _as2d(a):
    return a.reshape(-1, a.shape[-1])


def kernel(x, mix_norm, mlp_norm, mlp_w1, mlp_w2, lru_w_in, lru_conv_w, lru_conv_b, lru_w_r, lru_b_r, lru_w_i, lru_b_i, lru_lambda, lru_w_out, fox_w_in, fox_b_f, fox_q_gain, fox_k_gain, fox_w_out, loss_target, m_mix_norm, m_mlp_norm, m_mlp_w1, m_mlp_w2, m_lru_w_in, m_lru_conv_w, m_lru_conv_b, m_lru_w_r, m_lru_b_r, m_lru_w_i, m_lru_b_i, m_lru_lambda, m_lru_w_out, m_fox_w_in, m_fox_b_f, m_fox_q_gain, m_fox_k_gain, m_fox_w_out, v_mix_norm, v_mlp_norm, v_mlp_w1, v_mlp_w2, v_lru_w_in, v_lru_conv_w, v_lru_conv_b, v_lru_w_r, v_lru_b_r, v_lru_w_i, v_lru_b_i, v_lru_lambda, v_lru_w_out, v_fox_w_in, v_fox_b_f, v_fox_q_gain, v_fox_k_gain, v_fox_w_out):
    args = dict(locals())
    W = {n: args[n] for n in _WEIGHTS}
    Mo = {n: args["m_" + n] for n in _WEIGHTS}
    Vo = {n: args["v_" + n] for n in _WEIGHTS}
    S, D = x.shape[1], x.shape[2]
    F = 4 * D
    H = D // HEAD_DIM
    NU = 3 * D + LANES
    FQ, DQ = F // N_CHIPS, D // N_CHIPS
    nfox = fox_w_in.shape[-1]
    chip = 2 * lax.axis_index("x") + lax.axis_index("y")
    core = lax.axis_index("c").astype(jnp.int32).reshape(1)

    cw_flat = jnp.pad(lru_conv_w.reshape(-1), (0, _PACK_TILE - CONV_WIDTH * DQ)).reshape(2 * SUBLANES, LANES)
    g_w1, g_w2, g_lin, g_lout, g_fin, g_fout, g_cw = _all_gather(
        "gather_weights",
        [_as2d(mlp_w1).astype(BF16), _as2d(mlp_w2).astype(BF16), lru_w_in[0].astype(BF16), lru_w_out[0].astype(BF16),
         fox_w_in[0].astype(BF16), fox_w_out[0].astype(BF16), cw_flat])
    conv_w_full = jnp.transpose(g_cw.reshape(N_CHIPS, -1)[:, :CONV_WIDTH * DQ].reshape(N_CHIPS, CONV_WIDTH, DQ),
                                (1, 0, 2)).reshape(CONV_WIDTH, D)
    fox_full = jnp.concatenate([g_fin[s] for s in range(N_CHIPS)], axis=1)
    fox_full = jnp.pad(fox_full, ((0, 0), (0, NU - fox_full.shape[1])))
    wv = {"w1_0": _View(g_w1, "cs", 0, D), "w1_1": _View(g_w1, "cs", D, D),
          "w2_0": _View(g_w2, "rs", 0, FQ), "w2_1": _View(g_w2, "rs", FQ, FQ),
          "lru_in": _View(g_lin, "cs"), "lru_out": _View(g_lout, "rs"),
          "fox_in": _View(fox_full.T), "fox_out": _View(g_fout, "rs")}

    def grad_view(grads, name):
        if name in ("w1_0", "w1_1"):
            return _View(grads.get("w1_1"), "cs", D * int(name[-1]), D, shape=(N_CHIPS, 2 * D, FQ), dtype=BF16)
        if name in ("w2_0", "w2_1"):
            return _View(grads.get("w2_1"), "rs", FQ * int(name[-1]), FQ, shape=(N_CHIPS, 2 * FQ, D), dtype=BF16)
        if name == "lru_in":
            return _View(None, "cs", shape=(N_CHIPS, D, 2 * D // N_CHIPS), dtype=BF16)
        if name in ("lru_out", "fox_out"):
            return _View(None, "rs", shape=(N_CHIPS, DQ, D), dtype=BF16)
        return _View(None, shape=(D, NU), dtype=BF16)

    small = {n: W[n] for n in _REPLICATED}
    small["conv_w"] = conv_w_full

    loss, gx, grads = _local_step(x[0], loss_target[0], small, wv, grad_view)

    g_fox = jnp.transpose(grads["fox_in"][:, :nfox * N_CHIPS].reshape(D, N_CHIPS, nfox), (1, 0, 2))
    big = [grads["w1_0"], grads["w2_0"], grads["lru_in"], grads["lru_out"], g_fox, grads["fox_out"]]
    pack_names = _REPLICATED + ["conv_w"]
    flat = jnp.concatenate([grads[n].reshape(-1).astype(F32) for n in pack_names] + [loss.reshape(-1)])
    per_chip = -(-flat.shape[0] // (N_CHIPS * _PACK_TILE)) * _PACK_TILE
    pack = jnp.pad(flat, (0, N_CHIPS * per_chip - flat.shape[0])).reshape(N_CHIPS, per_chip // LANES, LANES)
    red = _reduce_scatter("grads", big + [pack], [BF16] * len(big) + [F32], core)
    r_w1, r_w2, r_lin, r_lout, r_fin, r_fout, r_pack = red
    (all_pack,) = _all_gather("gather_small_grads", [r_pack])
    all_flat = all_pack.reshape(-1)
    G = {}
    off = 0
    for n in pack_names:
        shape = grads[n].shape if n == "conv_w" else W[n].shape
        size = int(np.prod(shape))
        G[n] = all_flat[off:off + size].reshape(shape)
        off += size
    total = all_flat[off]
    G["lru_conv_w"] = lax.dynamic_slice_in_dim(G.pop("conv_w"), chip * DQ, DQ, axis=1)[None]
    G.update(mlp_w1=r_w1.reshape(mlp_w1.shape), mlp_w2=r_w2.reshape(mlp_w2.shape), lru_w_in=r_lin[None],
             lru_w_out=r_lout[None], fox_w_in=r_fin[None], fox_w_out=r_fout[None])

    delta, new_m, new_v = {}, {}, {}
    for n in _WEIGHTS:
        go, d, nm, nv = _adamw(f"adamw_{n}", _as2d(W[n]), _as2d(G[n]), _as2d(Mo[n]), _as2d(Vo[n]))
        G[n], delta[n], new_m[n], new_v[n] = (t.reshape(W[n].shape) for t in (go, d, nm, nv))

    return (total, gx[None], *[G[n] for n in _WEIGHTS], *[delta[n] for n in _WEIGHTS],
            *[new_m[n] for n in _WEIGHTS], *[new_v[n] for n in _WEIGHTS])
```

```python
import functools

import numpy as np
import jax
import jax.numpy as jnp
from jax import lax
from jax.experimental import pallas as pl
from jax.experimental.pallas import tpu as pltpu

F32 = jnp.float32
BF16 = jnp.bfloat16

HEAD_DIM = 64
LRU_BLOCK_DIM = 64
CONV_WIDTH = 4
LRU_C = 8.0
EPS = 1e-6
NEG_INF = -1e30
ADAM_LR = 0.001
ADAM_B1 = 0.9
ADAM_B2 = 0.999
ADAM_EPS = 1e-08
ADAM_WD = 0.01
ADAM_STEP = 10

N_CHIPS = 4
LANES = 128
SUBLANES = 8
MXU_DIM = 256
VMEM_LIMIT = 52 * 1024 * 1024
MESH = pl.DeviceIdType.MESH
ANY = pl.BlockSpec(memory_space=pl.ANY)


def _pick(n, prefs):
    for p in prefs:
        if p <= n and n % p == 0:
            return p
    return n


def _params(sem=None):
    return pltpu.CompilerParams(dimension_semantics=sem, vmem_limit_bytes=VMEM_LIMIT)


class _View:
    def __init__(self, arr, kind="plain", r0=0, rows=None, shape=None, dtype=None):
        self.arr = arr
        self.kind = kind
        self.r0 = r0
        self.shape = tuple(arr.shape) if arr is not None else tuple(shape)
        self.dtype = arr.dtype if arr is not None else dtype
        self.rows = rows if rows is not None else self.shape[-2]

    def limits(self):
        if self.kind == "plain":
            return 0, 0
        rows = int(np.gcd(self.rows, self.r0))
        return rows, (self.shape[-1] if self.kind == "cs" else 0)

    def spec(self, br, bc, fr, fc):
        if self.kind == "plain":
            return pl.BlockSpec((br, bc), lambda *g: (fr(*g), fc(*g)))
        ncol = self.shape[-1]
        r0b = self.r0 // br
        assert self.r0 % br == 0 and self.rows % br == 0 and ncol % bc == 0, (self.shape, self.r0, br, bc)
        if self.kind == "cs":
            per = ncol // bc
            return pl.BlockSpec((None, br, bc), lambda *g: (fc(*g) // per, r0b + fr(*g), fc(*g) % per))
        per = self.rows // br
        return pl.BlockSpec((None, br, bc), lambda *g: (fr(*g) // per, r0b + fr(*g) % per, fc(*g)))


def _bf(x):
    return x if x.dtype == BF16 else x.astype(BF16)


def _matmul(name, A, B, M, N, K, *, ta=False, tb=False, outs, epilogue, extras=(), tm=None, tn=None, tk=None):
    lim = {"m": [M], "n": [N], "k": [K]}
    for view, (rdim, cdim) in ([(A, "km" if ta else "mk"), (B, "nk" if tb else "kn")]
                               + [(e, "mn") for e in extras] + [(o, "mn") for o in outs]):
        r_lim, c_lim = view.limits()
        lim[rdim].append(r_lim)
        lim[cdim].append(c_lim)
    tm = tm or _pick(int(np.gcd.reduce(lim["m"])), (1024, 640, 512, 256, 128))
    tn = tn or _pick(int(np.gcd.reduce(lim["n"])), (1024, 640, 512, 256, 128))
    tk = tk or _pick(int(np.gcd.reduce(lim["k"])), (1024, 640, 512, 256, 128))
    nk = K // tk
    gi = lambda i, j, k: i
    gj = lambda i, j, k: j
    gk = lambda i, j, k: k
    a_spec = A.spec(tk, tm, gk, gi) if ta else A.spec(tm, tk, gi, gk)
    b_spec = B.spec(tn, tk, gj, gk) if tb else B.spec(tk, tn, gk, gj)
    ca = 0 if ta else 1
    cb = 1 if tb else 0
    ne, no = len(extras), len(outs)
    in_specs = [a_spec, b_spec] + [e.spec(tm, tn, gi, gj) for e in extras]
    operands = [A.arr, B.arr] + [e.arr for e in extras]
    aliases = {}
    for oi, o in enumerate(outs):
        if o.arr is not None:
            aliases[len(operands)] = oi
            in_specs.append(ANY)
            operands.append(o.arr)
    nalias = len(aliases)
    out_specs = [o.spec(tm, tn, gi, gj) for o in outs]
    out_shape = [jax.ShapeDtypeStruct(o.shape, o.dtype) for o in outs]

    def body(*refs):
        a_ref, b_ref = refs[0], refs[1]
        ex = refs[2:2 + ne]
        o_refs = refs[2 + ne + nalias:2 + ne + nalias + no]

        def prod():
            return lax.dot_general(_bf(a_ref[...]), _bf(b_ref[...]), (((ca,), (cb,)), ((), ())),
                                   preferred_element_type=F32)

        def finish(acc):
            res = epilogue(acc, *[e[...] for e in ex])
            for o_ref, r in zip(o_refs, res):
                o_ref[...] = r.astype(o_ref.dtype)

        if nk == 1:
            finish(prod())
        else:
            acc_ref = refs[-1]
            k = pl.program_id(2)

            @pl.when(k == 0)
            def _():
                acc_ref[...] = jnp.zeros_like(acc_ref)

            acc_ref[...] += prod()

            @pl.when(k == nk - 1)
            def _():
                finish(acc_ref[...])

    res = pl.pallas_call(
        body, name=name, grid=(M // tm, N // tn, nk), in_specs=in_specs, out_specs=out_specs, out_shape=out_shape,
        scratch_shapes=[pltpu.VMEM((tm, tn), F32)] if nk > 1 else [],
        input_output_aliases=aliases,
        compiler_params=_params(("parallel", "parallel", "arbitrary")),
    )(*operands)
    return res


def _ep_store(acc):
    return (acc,)


def _ep_resid(acc, res):
    return (res + acc,)


def _ep_relu2(acc):
    zp = jnp.maximum(acc, 0.0)
    return (acc, zp * zp)


def _ep_drelu2(acc, z):
    return (acc * (2.0 * jnp.maximum(z.astype(F32), 0.0)),)


def _fresh(M, N, dtype):
    return _View(None, shape=(M, N), dtype=dtype)


def _rms_fwd(name, x, g, S, D):
    T = _pick(S, (512, 256, 128))

    def body(x_ref, g_ref, h_ref):
        x = x_ref[...]
        r = lax.rsqrt(jnp.mean(x * x, axis=-1, keepdims=True) + EPS)
        h_ref[...] = ((x * r) * g_ref[...]).astype(BF16)

    return pl.pallas_call(
        body, name=name, grid=(S // T,),
        in_specs=[pl.BlockSpec((T, D), lambda i: (i, 0)), pl.BlockSpec((1, D), lambda i: (0, 0))],
        out_specs=pl.BlockSpec((T, D), lambda i: (i, 0)),
        out_shape=jax.ShapeDtypeStruct((S, D), BF16),
        compiler_params=_params(("arbitrary",)),
    )(x, g)


def _rms_bwd(name, dh, x, g, dres, S, D):
    T = _pick(S, (512, 256, 128))

    def body(dh_ref, x_ref, g_ref, dres_ref, dx_ref, dxb_ref, dg_ref):
        @pl.when(pl.program_id(0) == 0)
        def _():
            dg_ref[...] = jnp.zeros_like(dg_ref)

        x = x_ref[...]
        dh = dh_ref[...]
        r = lax.rsqrt(jnp.mean(x * x, axis=-1, keepdims=True) + EPS)
        xhat = x * r
        dg_ref[...] += jnp.sum(dh * xhat, axis=0, keepdims=True)
        dxn = dh * g_ref[...]
        dx = r * (dxn - xhat * jnp.mean(dxn * xhat, axis=-1, keepdims=True))
        tot = dres_ref[...] + dx
        dx_ref[...] = tot
        dxb_ref[...] = tot.astype(BF16)

    row = pl.BlockSpec((T, D), lambda i: (i, 0))
    vec = pl.BlockSpec((1, D), lambda i: (0, 0))
    return pl.pallas_call(
        body, name=name, grid=(S // T,), in_specs=[row, row, vec, row], out_specs=[row, row, vec],
        out_shape=[jax.ShapeDtypeStruct((S, D), F32), jax.ShapeDtypeStruct((S, D), BF16),
                   jax.ShapeDtypeStruct((1, D), F32)],
        compiler_params=_params(("arbitrary",)),
    )(dh, x, g, dres)


def _loss_head(x, tgt, S, D):
    T = _pick(S, (512, 256, 128))

    def body(x_ref, t_ref, loss_ref, d_ref, db_ref):
        @pl.when(pl.program_id(0) == 0)
        def _():
            loss_ref[...] = jnp.zeros_like(loss_ref)

        e = x_ref[...] - t_ref[...]
        loss_ref[...] += 0.5 * jnp.sum(jnp.mean(e * e, axis=-1, keepdims=True), axis=0, keepdims=True)
        d = e * (1.0 / D)
        d_ref[...] = d
        db_ref[...] = d.astype(BF16)

    row = pl.BlockSpec((T, D), lambda i: (i, 0))
    return pl.pallas_call(
        body, name="loss_head", grid=(S // T,), in_specs=[row, row],
        out_specs=[pl.BlockSpec((1, 1), lambda i: (0, 0)), row, row],
        out_shape=[jax.ShapeDtypeStruct((1, 1), F32), jax.ShapeDtypeStruct((S, D), F32),
                   jax.ShapeDtypeStruct((S, D), BF16)],
        compiler_params=_params(("arbitrary",)),
    )(x, tgt)


def _sigmoid(z):
    return 1.0 / (1.0 + jnp.exp(-z))


def _log_sigmoid(z):
    return jnp.minimum(z, 0.0) - jnp.log(1.0 + jnp.exp(-jnp.abs(z)))


_GELU_K = 0.7978845608028654
_GELU_C = 0.044715


def _gelu(x):
    t = jnp.tanh(_GELU_K * (x + _GELU_C * (x * x * x)))
    return 0.5 * x * (1.0 + t)


def _gelu_and_grad(x):
    x2 = x * x
    t = jnp.tanh(_GELU_K * (x + _GELU_C * (x2 * x)))
    g = 0.5 * x * (1.0 + t)
    dg = 0.5 * (1.0 + t) + 0.5 * x * (1.0 - t * t) * (_GELU_K * (1.0 + 3.0 * _GELU_C * x2))
    return g, dg


def _decay_terms(r, ls):
    la = LRU_C * r * ls
    a = jnp.exp(la)
    a2 = jnp.exp(2.0 * la)
    mult = jnp.sqrt(-jnp.tanh(la) * (a2 + 1.0))
    return a, a2, mult


def _lru_fwd(u0, conv_w, conv_b, wr_bd, b_r, wi_bd, b_i, lam, S, D, plan=None):
    T = _pick(S, (256, 128))
    GT = wr_bd.shape[-1]
    nG = D // GT

    def body(gb_ref, xb_ref, cw_ref, cb_ref, wr_ref, br_ref, wi_ref, bi_ref, lam_ref,
             y_ref, xc_ref, r_ref, i_ref, hs_ref, ext, a_scr, hcar):
        @pl.when(pl.program_id(0) == 0)
        def _():
            ext[0:SUBLANES, :] = jnp.zeros((SUBLANES, D), F32)
            hcar[...] = jnp.zeros_like(hcar)

        xb = xb_ref[...]
        ext[SUBLANES:SUBLANES + T, :] = xb
        xc = cb_ref[...]
        for k in range(CONV_WIDTH):
            xc = xc + ext[pl.ds(SUBLANES - (CONV_WIDTH - 1) + k, T), :] * cw_ref[k:k + 1, :]
        ext[0:SUBLANES, :] = xb[T - SUBLANES:T, :]
        xc_ref[...] = xc
        xcb = xc.astype(BF16)
        for g in range(nG):
            sl = slice(g * GT, (g + 1) * GT)
            zr = jnp.dot(xcb[:, sl], wr_ref[g], preferred_element_type=F32) + br_ref[:, sl]
            zi = jnp.dot(xcb[:, sl], wi_ref[g], preferred_element_type=F32) + bi_ref[:, sl]
            r_ref[:, sl] = _sigmoid(zr)
            i_ref[:, sl] = _sigmoid(zi)
        r = r_ref[...]
        a, _, mult = _decay_terms(r, _log_sigmoid(lam_ref[...]))
        a_scr[...] = a
        hs_ref[...] = mult * (i_ref[...] * xc)

        def step(t, h):
            h = a_scr[pl.ds(t, 1), :] * h + hs_ref[pl.ds(t, 1), :]
            hs_ref[pl.ds(t, 1), :] = h
            return h

        hcar[...] = lax.fori_loop(0, T, step, hcar[...], unroll=8)
        y_ref[...] = (_gelu(gb_ref[...]) * hs_ref[...]).astype(BF16)

    row = pl.BlockSpec((T, D), lambda i: (i, 0))
    vec = pl.BlockSpec((1, D), lambda i: (0, 0))
    bd = pl.BlockSpec((nG, GT, GT), lambda i: (0, 0, 0))
    f32o = jax.ShapeDtypeStruct((S, D), F32)
    return _hosted_call(
        body, "lru_fwd", (S // T,),
        [row, pl.BlockSpec((T, D), lambda i: (i, 1)), pl.BlockSpec((CONV_WIDTH, D), lambda i: (0, 0)), vec,
         bd, vec, bd, vec, vec],
        [row, row, row, row, row], [jax.ShapeDtypeStruct((S, D), BF16), f32o, f32o, f32o, f32o],
        [pltpu.VMEM((T + SUBLANES, D), F32), pltpu.VMEM((T, D), F32), pltpu.VMEM((1, D), F32)],
        (u0, u0, conv_w, conv_b, wr_bd, b_r, wi_bd, b_i, lam), ("arbitrary",), plan)


def _lru_bwd(dy, u0, xc, r, ig, hs, conv_w, wr_bd, wi_bd, lam, S, D, plan=None):
    T = _pick(S, (128,))
    nT = S // T
    GT = wr_bd.shape[-1]
    nG = D // GT
    W = CONV_WIDTH

    def body(dy_ref, gb_ref, xb_ref, xbp_ref, xc_ref, r_ref, i_ref, hs_ref, hsp_ref, cw_ref, wr_ref, wi_ref, lam_ref,
             du_ref, dcw_ref, dcb_ref, dlam_ref, dbr_ref, dbi_ref, dwr_ref, dwi_ref,
             a_scr, dh_scr, exth, extx, extd, dxc_scr, dz_scr, carry):
        step = pl.program_id(0)
        first_tile = step == nT - 1

        @pl.when(step == 0)
        def _():
            for ref in (dcw_ref, dcb_ref, dlam_ref, dbr_ref, dbi_ref, dwr_ref, dwi_ref, carry):
                ref[...] = jnp.zeros_like(ref)
            extd[T:T + SUBLANES, :] = jnp.zeros((SUBLANES, D), F32)

        hs = hs_ref[...]
        dy = dy_ref[...]
        g, dgelu = _gelu_and_grad(gb_ref[...])
        du_ref[:, 0:D] = (dy * hs * dgelu).astype(BF16)
        r = r_ref[...]
        lam = lam_ref[...]
        ls = _log_sigmoid(lam)
        a, a2, mult = _decay_terms(r, ls)
        a_scr[...] = a
        dh_scr[...] = dy * g

        def rstep(j, c):
            t = T - 1 - j
            d = dh_scr[pl.ds(t, 1), :] + c
            dh_scr[pl.ds(t, 1), :] = d
            return a_scr[pl.ds(t, 1), :] * d

        carry[...] = lax.fori_loop(0, T, rstep, carry[...], unroll=8)
        dh = dh_scr[...]
        keep = jnp.where(first_tile, 0.0, 1.0)
        exth[0:SUBLANES, :] = hsp_ref[...] * keep
        exth[SUBLANES:SUBLANES + T, :] = hs
        hprev = exth[pl.ds(SUBLANES - 1, T), :]
        xc = xc_ref[...]
        ig = i_ref[...]
        da = dh * hprev
        dmult = dh * (ig * xc)
        dla = da * a - dmult * (a2 / mult)
        dlam_ref[...] += jnp.sum(dla * r, axis=0, keepdims=True) * (LRU_C * _sigmoid(-lam))
        dzr = (dla * (LRU_C * ls)) * (r * (1.0 - r))
        dzi = (dh * (mult * xc)) * (ig * (1.0 - ig))
        dbr_ref[...] += jnp.sum(dzr, axis=0, keepdims=True)
        dbi_ref[...] += jnp.sum(dzi, axis=0, keepdims=True)
        dxc_scr[...] = dh * (mult * ig)
        xcb = xc.astype(BF16)
        dz_scr[0] = dzr.astype(BF16)
        dz_scr[1] = dzi.astype(BF16)
        nt_dims = (((1,), (1,)), ((), ()))
        tn_dims = (((0,), (0,)), ((), ()))
        for gq in range(nG):
            sl = slice(gq * GT, (gq + 1) * GT)
            zr_g = dz_scr[0, :, sl]
            zi_g = dz_scr[1, :, sl]
            dxc_scr[:, sl] += (lax.dot_general(zr_g, wr_ref[gq], nt_dims, preferred_element_type=F32)
                               + lax.dot_general(zi_g, wi_ref[gq], nt_dims, preferred_element_type=F32))
            dwr_ref[gq] += lax.dot_general(xcb[:, sl], zr_g, tn_dims, preferred_element_type=F32)
            dwi_ref[gq] += lax.dot_general(xcb[:, sl], zi_g, tn_dims, preferred_element_type=F32)
        dxc = dxc_scr[...]
        dcb_ref[...] += jnp.sum(dxc, axis=0, keepdims=True)
        extx[0:SUBLANES, :] = xbp_ref[...] * keep
        extx[SUBLANES:SUBLANES + T, :] = xb_ref[...]
        extd[0:T, :] = dxc
        dxb = jnp.zeros((T, D), F32)
        for k in range(W):
            dxb = dxb + extd[pl.ds(W - 1 - k, T), :] * cw_ref[k:k + 1, :]
            dcw_ref[k:k + 1, :] += jnp.sum(dxc * extx[pl.ds(SUBLANES - (W - 1) + k, T), :], axis=0, keepdims=True)
        extd[T:T + SUBLANES, :] = dxc[0:SUBLANES, :]
        du_ref[:, D:2 * D] = dxb.astype(BF16)

    rev = lambda i: nT - 1 - i
    tpb = T // SUBLANES
    prev8 = lambda i: jnp.maximum(rev(i) * tpb - 1, 0)
    row = pl.BlockSpec((T, D), lambda i: (rev(i), 0))
    vec = pl.BlockSpec((1, D), lambda i: (0, 0))
    bd = pl.BlockSpec((nG, GT, GT), lambda i: (0, 0, 0))
    vec_o = jax.ShapeDtypeStruct((1, D), F32)
    bd_o = jax.ShapeDtypeStruct((nG, GT, GT), F32)
    return _hosted_call(
        body, "lru_bwd", (nT,),
        [row, row, pl.BlockSpec((T, D), lambda i: (rev(i), 1)), pl.BlockSpec((SUBLANES, D), lambda i: (prev8(i), 1)),
         row, row, row, row, pl.BlockSpec((SUBLANES, D), lambda i: (prev8(i), 0)),
         pl.BlockSpec((W, D), lambda i: (0, 0)), bd, bd, vec],
        [pl.BlockSpec((T, 2 * D), lambda i: (rev(i), 0)), pl.BlockSpec((W, D), lambda i: (0, 0)),
         vec, vec, vec, vec, bd, bd],
        [jax.ShapeDtypeStruct((S, 2 * D), BF16), jax.ShapeDtypeStruct((W, D), F32), vec_o, vec_o, vec_o, vec_o, bd_o, bd_o],
        [pltpu.VMEM((T, D), F32), pltpu.VMEM((T, D), F32), pltpu.VMEM((T + SUBLANES, D), F32),
         pltpu.VMEM((T + SUBLANES, D), F32), pltpu.VMEM((T + SUBLANES, D), F32),
         pltpu.VMEM((T, D), F32), pltpu.VMEM((2, T, D), BF16), pltpu.VMEM((1, D), F32)],
        (dy, u0, u0, u0, xc, r, ig, hs, hs, conv_w, wr_bd, wi_bd, lam), ("arbitrary",), plan)


AUG_ROWS = 16
HEAD_ROWS = 128
LSE_ROW = HEAD_DIM + 6


def _split3(x):
    b1 = x.astype(BF16).astype(F32)
    r = x - b1
    b2 = r.astype(BF16).astype(F32)
    return b1, b2, r - b2


def _head_block(x, aug, T):
    row = lax.broadcasted_iota(jnp.int32, (AUG_ROWS, T), 0)
    blk = jnp.zeros((AUG_ROWS, T), F32)
    for i, e in enumerate(aug):
        blk = jnp.where(row == i, e, blk)
    return jnp.concatenate([x, blk, jnp.zeros((HEAD_ROWS - HEAD_DIM - AUG_ROWS, T), F32)], axis=0)


def _tri_matrix(lower):
    i = np.arange(LANES)
    m = (i[:, None] >= i[None, :]) if lower else (i[:, None] <= i[None, :])
    return jnp.asarray(m.astype(np.float32), BF16)


def _lane_cumsum(x, tri_ref, carry, reverse):
    n = x.shape[1] // LANES
    tri = tri_ref[...]
    out = [None] * n
    for j in (range(n - 1, -1, -1) if reverse else range(n)):
        cs = carry
        for part in _split3(x[:, j * LANES:(j + 1) * LANES]):
            cs = cs + jnp.dot(part.astype(BF16), tri, preferred_element_type=F32)
        out[j] = cs
        carry = cs[:, 0:1] if reverse else cs[:, LANES - 1:LANES]
    return jnp.concatenate(out, axis=1), carry


def _head_rows(h):
    return pl.ds(pl.multiple_of(h * HEAD_DIM, HEAD_DIM), HEAD_DIM)


def _fox_prep(ut, b_f, qg, kg, S, D, tq):
    H = D // HEAD_DIM
    T = min(tq, 256)
    per = tq // T
    scale = HEAD_DIM ** -0.5

    def body(q_ref, k_ref, v_ref, f_ref, bf_ref, qg_ref, kg_ref, tri_ref,
             qat_ref, kat_ref, vat_ref, ka_ref, va_ref, vt_ref, c_scr, ccar):
        @pl.when(pl.program_id(0) == 0)
        def _():
            ccar[...] = jnp.zeros_like(ccar)

        c, carry = _lane_cumsum(_log_sigmoid(f_ref[...] + bf_ref[...]), tri_ref, ccar[...], False)
        c_scr[...] = c
        ccar[...] = carry

        def head(h, _):
            rows = _head_rows(h)
            c1, c2, c3 = _split3(c_scr[pl.ds(h, 1), :])

            def normed(src, gain, mul):
                x = src[rows, :]
                rs = lax.rsqrt(jnp.mean(x * x, axis=0, keepdims=True) + EPS)
                return ((x * rs) * gain[rows, :]) * mul

            qat_ref[h] = _head_block(normed(q_ref, qg_ref, scale), [c1, c2, c3, 1.0, 1.0, 1.0], T).astype(BF16)
            kb = _head_block(normed(k_ref, kg_ref, 1.0), [1.0, 1.0, 1.0, -c1, -c2, -c3, 1.0, 1.0, 1.0], T)
            kat_ref[h] = kb.astype(BF16)
            ka_ref[h] = kb.T.astype(BF16)
            v = v_ref[rows, :]
            vt_ref[h] = v.astype(BF16)
            vb = _head_block(v, [1.0, 1.0, 1.0], T)
            vat_ref[h] = vb.astype(BF16)
            va_ref[h] = vb.T.astype(BF16)
            return 0

        lax.fori_loop(0, H, head, 0)

    part = lambda j: pl.BlockSpec((D, T), lambda i: (j, i))
    colv = lambda n: pl.BlockSpec((n, 1), lambda i: (0, 0))
    tmaj = lambda r: pl.BlockSpec((H, None, r, T), lambda i: (0, i // per, 0, i % per))
    norm = pl.BlockSpec((H, T, HEAD_ROWS), lambda i: (0, i, 0))
    tshape = lambda r: jax.ShapeDtypeStruct((H, S // tq, r, tq), BF16)
    nshape = jax.ShapeDtypeStruct((H, S, HEAD_ROWS), BF16)
    return pl.pallas_call(
        body, name="fox_prep", grid=(S // T,),
        in_specs=[part(0), part(1), part(2), pl.BlockSpec((LANES, T), lambda i: (3 * D // LANES, i)),
                  colv(LANES), colv(D), colv(D), pl.BlockSpec((LANES, LANES), lambda i: (0, 0))],
        out_specs=[tmaj(HEAD_ROWS), tmaj(HEAD_ROWS), tmaj(HEAD_ROWS), norm, norm, tmaj(HEAD_DIM)],
        out_shape=[tshape(HEAD_ROWS), tshape(HEAD_ROWS), tshape(HEAD_ROWS), nshape, nshape, tshape(HEAD_DIM)],
        scratch_shapes=[pltpu.VMEM((LANES, T), F32), pltpu.VMEM((LANES, 1), F32)],
        compiler_params=_params(("arbitrary",)),
    )(ut, ut, ut, ut, b_f, qg, kg, _tri_matrix(False))


def _fox_bwd_prep(dot, ot, lse, qat, S, D, tq):
    H = D // HEAD_DIM
    T = min(tq, 256)
    per = tq // T

    def body(do_ref, o_ref, lse_ref, qat_ref, doat_ref, doa_ref, qat1_ref, qa1_ref):
        row = lax.broadcasted_iota(jnp.int32, (HEAD_ROWS, T), 0)

        def head(h, _):
            rows = _head_rows(h)
            do = do_ref[rows, :].astype(F32)
            delta = jnp.sum(do * o_ref[rows, :], axis=0, keepdims=True)
            db = _head_block(do, list(_split3(-delta)), T)
            doat_ref[h] = db.astype(BF16)
            doa_ref[h] = db.T.astype(BF16)
            qb = qat_ref[h].astype(F32)
            for i, e in enumerate(_split3(-lse_ref[h])):
                qb = jnp.where(row == LSE_ROW + i, e, qb)
            qat1_ref[h] = qb.astype(BF16)
            qa1_ref[h] = qb.T.astype(BF16)
            return 0

        lax.fori_loop(0, H, head, 0)

    chan = pl.BlockSpec((D, T), lambda i: (0, i))
    tmaj = pl.BlockSpec((H, None, HEAD_ROWS, T), lambda i: (0, i // per, 0, i % per))
    norm = pl.BlockSpec((H, T, HEAD_ROWS), lambda i: (0, i, 0))
    tshape = jax.ShapeDtypeStruct((H, S // tq, HEAD_ROWS, tq), BF16)
    nshape = jax.ShapeDtypeStruct((H, S, HEAD_ROWS), BF16)
    return pl.pallas_call(
        body, name="fox_bwd_prep", grid=(S // T,),
        in_specs=[chan, chan, pl.BlockSpec((H, 1, T), lambda i: (0, 0, i)), tmaj],
        out_specs=[tmaj, norm, tmaj, norm], out_shape=[tshape, nshape, tshape, nshape],
        compiler_params=_params(("arbitrary",)),
    )(dot, ot, lse, qat)


def _causal(s, k_axis):
    ki = lax.broadcasted_iota(jnp.int32, s.shape, k_axis)
    qi = lax.broadcasted_iota(jnp.int32, s.shape, 1 - k_axis)
    return jnp.where(ki <= qi, s, NEG_INF)


def _seq_tile(i, t):
    return pl.ds(pl.multiple_of(i * t, t), t)


def _attn_forward(ka, qat, vt, S, D, tq, plan=None):
    H = D // HEAD_DIM
    nq = S // tq

    def body(ka_ref, qat_ref, vt_ref, o_ref, o32_ref, lse_ref, m_scr, l_scr, acc_scr):
        qi = pl.program_id(1)
        m_scr[...] = jnp.full_like(m_scr, NEG_INF)
        l_scr[...] = jnp.zeros_like(l_scr)
        acc_scr[...] = jnp.zeros_like(acc_scr)
        qa = qat_ref[...]

        def tile(ki, diagonal):
            s = jnp.dot(ka_ref[_seq_tile(ki, tq), :], qa, preferred_element_type=F32)
            if diagonal:
                s = _causal(s, 0)
            m_prev = m_scr[...]
            m_new = jnp.maximum(m_prev, jnp.max(s, axis=0, keepdims=True))
            alpha = jnp.exp(m_prev - m_new)
            p = jnp.exp(s - m_new)
            l_scr[...] = alpha * l_scr[...] + jnp.sum(p, axis=0, keepdims=True)
            acc_scr[...] = alpha * acc_scr[...] + jnp.dot(vt_ref[ki], p.astype(BF16), preferred_element_type=F32)
            m_scr[...] = m_new

        def off_diagonal(ki, _):
            tile(ki, False)
            return 0

        lax.fori_loop(0, qi, off_diagonal, 0)
        tile(qi, True)
        o = acc_scr[...] / l_scr[...]
        o_ref[...] = o.astype(BF16)
        o32_ref[...] = o
        lse_ref[...] = m_scr[...] + jnp.log(l_scr[...])

    chan = pl.BlockSpec((HEAD_DIM, tq), lambda h, i: (h, i))
    stat = pl.BlockSpec((None, 1, tq), lambda h, i: (h, 0, i))
    return _hosted_call(
        body, "attn_forward", (H, nq),
        [pl.BlockSpec((None, S, HEAD_ROWS), lambda h, i: (h, 0, 0)),
         pl.BlockSpec((None, None, HEAD_ROWS, tq), lambda h, i: (h, i, 0, 0)),
         pl.BlockSpec((None, nq, HEAD_DIM, tq), lambda h, i: (h, 0, 0, 0))],
        [chan, chan, stat],
        [jax.ShapeDtypeStruct((D, S), BF16), jax.ShapeDtypeStruct((D, S), F32), jax.ShapeDtypeStruct((H, 1, S), F32)],
        [pltpu.VMEM((1, tq), F32), pltpu.VMEM((1, tq), F32), pltpu.VMEM((HEAD_DIM, tq), F32)],
        (ka, qat, vt), ("arbitrary", "arbitrary"), plan)


def _attn_backward_q(ka, va, kat, qat, doat, S, D, tq, plan=None):
    H = D // HEAD_DIM
    nq = S // tq

    def body(ka_ref, va_ref, kat_ref, qat_ref, doat_ref, dq_ref, dcq_ref, dq_scr, rs_scr):
        qi = pl.program_id(1)
        dq_scr[...] = jnp.zeros_like(dq_scr)
        rs_scr[...] = jnp.zeros_like(rs_scr)
        qa = qat_ref[...]
        doa = doat_ref[...]

        def tile(ki, diagonal):
            rows = _seq_tile(ki, tq)
            s = jnp.dot(ka_ref[rows, :], qa, preferred_element_type=F32)
            if diagonal:
                s = _causal(s, 0)
            ds = jnp.exp(s) * jnp.dot(va_ref[rows, :], doa, preferred_element_type=F32)
            rs_scr[...] += jnp.sum(ds, axis=0, keepdims=True)
            dq_scr[...] += jnp.dot(kat_ref[ki, 0:HEAD_DIM, :], ds.astype(BF16), preferred_element_type=F32)

        def off_diagonal(ki, _):
            tile(ki, False)
            return 0

        lax.fori_loop(0, qi, off_diagonal, 0)
        tile(qi, True)
        dq_ref[...] = dq_scr[...]
        dcq_ref[...] = rs_scr[...]

    whole = pl.BlockSpec((None, S, HEAD_ROWS), lambda h, i: (h, 0, 0))
    one = pl.BlockSpec((None, None, HEAD_ROWS, tq), lambda h, i: (h, i, 0, 0))
    return _hosted_call(
        body, "attn_backward_q", (H, nq),
        [whole, whole, pl.BlockSpec((None, nq, HEAD_ROWS, tq), lambda h, i: (h, 0, 0, 0)), one, one],
        [pl.BlockSpec((HEAD_DIM, tq), lambda h, i: (h, i)), pl.BlockSpec((None, 1, tq), lambda h, i: (h, 0, i))],
        [jax.ShapeDtypeStruct((D, S), F32), jax.ShapeDtypeStruct((H, 1, S), F32)],
        [pltpu.VMEM((HEAD_DIM, tq), F32), pltpu.VMEM((1, tq), F32)],
        (ka, va, kat, qat, doat), ("arbitrary", "arbitrary"), plan)


def _attn_backward_kv(qa, doa, qat, doat, kat, vat, S, D, tq):
    H = D // HEAD_DIM
    nq = S // tq

    def body(qa_ref, doa_ref, qat_ref, doat_ref, kat_ref, vat_ref, dk_ref, dv_ref, dck_ref, dk_scr, dv_scr, cs_scr):
        ki = pl.program_id(1)
        dk_scr[...] = jnp.zeros_like(dk_scr)
        dv_scr[...] = jnp.zeros_like(dv_scr)
        cs_scr[...] = jnp.zeros_like(cs_scr)
        ka = kat_ref[...]
        va = vat_ref[...]

        def tile(qi, diagonal):
            rows = _seq_tile(qi, tq)
            s = jnp.dot(qa_ref[rows, :], ka, preferred_element_type=F32)
            if diagonal:
                s = _causal(s, 1)
            p = jnp.exp(s)
            ds = p * jnp.dot(doa_ref[rows, :], va, preferred_element_type=F32)
            dv_scr[...] += jnp.dot(doat_ref[qi, 0:HEAD_DIM, :], p.astype(BF16), preferred_element_type=F32)
            dk_scr[...] += jnp.dot(qat_ref[qi, 0:HEAD_DIM, :], ds.astype(BF16), preferred_element_type=F32)
            cs_scr[...] += jnp.sum(ds, axis=0, keepdims=True)

        def off_diagonal(qi, _):
            tile(qi, False)
            return 0

        tile(ki, True)
        lax.fori_loop(ki + 1, nq, off_diagonal, 0)
        dk_ref[...] = dk_scr[...]
        dv_ref[...] = dv_scr[...].astype(BF16)
        dck_ref[...] = cs_scr[...]

    whole = pl.BlockSpec((None, S, HEAD_ROWS), lambda h, i: (h, 0, 0))
    tiles = pl.BlockSpec((None, nq, HEAD_ROWS, tq), lambda h, i: (h, 0, 0, 0))
    one = pl.BlockSpec((None, None, HEAD_ROWS, tq), lambda h, i: (h, i, 0, 0))
    chan = pl.BlockSpec((HEAD_DIM, tq), lambda h, i: (h, i))
    return pl.pallas_call(
        body, name="attn_backward_kv", grid=(H, nq),
        in_specs=[whole, whole, tiles, tiles, one, one],
        out_specs=[chan, chan, pl.BlockSpec((None, 1, tq), lambda h, i: (h, 0, i))],
        out_shape=[jax.ShapeDtypeStruct((D, S), F32), jax.ShapeDtypeStruct((D, S), BF16),
                   jax.ShapeDtypeStruct((H, 1, S), F32)],
        scratch_shapes=[pltpu.VMEM((HEAD_DIM, tq), F32), pltpu.VMEM((HEAD_DIM, tq), F32), pltpu.VMEM((1, tq), F32)],
        compiler_params=_params(("arbitrary", "arbitrary")),
    )(qa, doa, qat, doat, kat, vat)


def _fox_prep_bwd(ut, dqt, dkt, dvt, dcq, dck, b_f, qg, kg, S, D, tq):
    H = D // HEAD_DIM
    T = min(tq, 256)
    nT = S // T
    NU = 3 * D + LANES
    scale = HEAD_DIM ** -0.5

    def body(q_ref, k_ref, f_ref, dq_ref, dk_ref, dv_ref, dcq_ref, dck_ref, bf_ref, qg_ref, kg_ref, tri_ref,
             du_ref, dbf_ref, dqg_ref, dkg_ref, gq_acc, gk_acc, fcar, dc_scr):
        step = pl.program_id(0)

        @pl.when(step == 0)
        def _():
            for ref in (gq_acc, gk_acc, fcar, dbf_ref):
                ref[...] = jnp.zeros_like(ref)

        dc_scr[...] = jnp.zeros_like(dc_scr)

        def head(h, _):
            rows = _head_rows(h)
            dc_scr[pl.ds(h, 1), :] = dcq_ref[h] - dck_ref[h]
            for src, dsrc, gain, acc, mul, base in ((q_ref, dq_ref, qg_ref, gq_acc, scale, 0),
                                                    (k_ref, dk_ref, kg_ref, gk_acc, 1.0, D)):
                x = src[rows, :]
                rs = lax.rsqrt(jnp.mean(x * x, axis=0, keepdims=True) + EPS)
                xhat = x * rs
                dn = dsrc[rows, :] * mul
                acc[rows, :] += jnp.sum(dn * xhat, axis=1, keepdims=True)
                dxh = dn * gain[rows, :]
                dx = rs * (dxh - xhat * jnp.mean(dxh * xhat, axis=0, keepdims=True))
                du_ref[pl.ds(pl.multiple_of(base + h * HEAD_DIM, HEAD_DIM), HEAD_DIM), :] = dx.astype(BF16)
            return 0

        lax.fori_loop(0, H, head, 0)
        du_ref[2 * D:3 * D, :] = dv_ref[...]
        dlf, carry = _lane_cumsum(dc_scr[...], tri_ref, fcar[...], True)
        fcar[...] = carry
        dfl = dlf * _sigmoid(-(f_ref[...] + bf_ref[...]))
        dbf_ref[...] += jnp.sum(dfl, axis=1, keepdims=True)
        du_ref[3 * D:NU, :] = dfl.astype(BF16)

        @pl.when(step == nT - 1)
        def _():
            for acc, ref in ((gq_acc, dqg_ref), (gk_acc, dkg_ref)):
                tot = jnp.zeros((HEAD_DIM, 1), F32)
                for h in range(H):
                    tot = tot + acc[h * HEAD_DIM:(h + 1) * HEAD_DIM, :]
                ref[...] = tot

    rev = lambda i: nT - 1 - i
    part = lambda j: pl.BlockSpec((D, T), lambda i: (j, rev(i)))
    chan = pl.BlockSpec((D, T), lambda i: (0, rev(i)))
    stat = pl.BlockSpec((H, 1, T), lambda i: (0, 0, rev(i)))
    colv = lambda n: pl.BlockSpec((n, 1), lambda i: (0, 0))
    return pl.pallas_call(
        body, name="fox_prep_bwd", grid=(nT,),
        in_specs=[part(0), part(1), pl.BlockSpec((LANES, T), lambda i: (3 * D // LANES, rev(i))), chan, chan, chan,
                  stat, stat, colv(LANES), colv(D), colv(D), pl.BlockSpec((LANES, LANES), lambda i: (0, 0))],
        out_specs=[pl.BlockSpec((NU, T), lambda i: (0, rev(i))), colv(LANES), colv(HEAD_DIM), colv(HEAD_DIM)],
        out_shape=[jax.ShapeDtypeStruct((NU, S), BF16), jax.ShapeDtypeStruct((LANES, 1), F32),
                   jax.ShapeDtypeStruct((HEAD_DIM, 1), F32), jax.ShapeDtypeStruct((HEAD_DIM, 1), F32)],
        scratch_shapes=[pltpu.VMEM((D, 1), F32), pltpu.VMEM((D, 1), F32), pltpu.VMEM((LANES, 1), F32),
                        pltpu.VMEM((LANES, T), F32)],
        compiler_params=_params(("arbitrary",)),
    )(ut, ut, ut, dqt, dkt, dvt, dcq, dck, b_f, qg, kg, _tri_matrix(True))


def _block_diag_tiles(w):
    n = w.shape[0]
    per = min(MXU_DIM, n * LRU_BLOCK_DIM) // LRU_BLOCK_DIM
    eye = jnp.eye(per, dtype=w.dtype)
    w5 = w.reshape(n // per, per, LRU_BLOCK_DIM, 1, LRU_BLOCK_DIM) * eye[None, :, None, :, None]
    return w5.reshape(n // per, per * LRU_BLOCK_DIM, per * LRU_BLOCK_DIM).astype(BF16)


def _block_diag_extract(t, n):
    per = t.shape[-1] // LRU_BLOCK_DIM
    eye = jnp.eye(per, dtype=t.dtype)
    t5 = t.reshape(n // per, per, LRU_BLOCK_DIM, per, LRU_BLOCK_DIM) * eye[None, :, None, :, None]
    return t5.sum(axis=3).reshape(n, LRU_BLOCK_DIM, LRU_BLOCK_DIM)


def _local_step(x, tgt, small, wv, grad_view, comm=None):
    S, D = x.shape
    F = 4 * D
    H = D // HEAD_DIM
    nblk = D // LRU_BLOCK_DIM
    NU = 3 * D + LANES
    tq = max(LANES, min(512, S // 4))
    assert S % tq == 0
    vec = lambda a: a.reshape(1, -1).astype(F32)
    col = lambda a: a.reshape(-1, 1).astype(F32)
    mix_g, mlp_g = small["mix_norm"], small["mlp_norm"]
    conv_w, conv_b = small["conv_w"], vec(small["lru_conv_b"])
    wr_bd, wi_bd = _block_diag_tiles(small["lru_w_r"][0]), _block_diag_tiles(small["lru_w_i"][0])
    b_r, b_i, lam = vec(small["lru_b_r"]), vec(small["lru_b_i"]), vec(small["lru_lambda"])
    b_f = jnp.pad(col(small["fox_b_f"]), ((0, LANES - H), (0, 0)))
    qg, kg = jnp.tile(col(small["fox_q_gain"]), (H, 1)), jnp.tile(col(small["fox_k_gain"]), (H, 1))
    X = lambda a: _View(a)
    grads = {}
    gout = functools.partial(grad_view, grads)

    def hosted(name, fn, *args):
        plan = comm.before(name, grads) if comm is not None else None
        res, side = fn(*args, plan=plan)
        if plan is not None:
            comm.after(name, side, wv)
        return res

    def mlp_fwd(l, xin):
        hm = _rms_fwd(f"mlp{l}_norm", xin, mlp_g[l:l + 1], S, D)
        z, act = _matmul(f"mlp{l}_up", X(hm), wv[f"w1_{l}"], S, F, D, outs=[_fresh(S, F, BF16), _fresh(S, F, BF16)],
                         epilogue=_ep_relu2)
        (xout,) = _matmul(f"mlp{l}_down", X(act), wv[f"w2_{l}"], S, D, F, outs=[_fresh(S, D, F32)],
                          epilogue=_ep_resid, extras=[X(xin)])
        return hm, z, act, xout

    def mlp_bwd(l, xin, hm, z, act, d, db):
        (dz,) = _matmul(f"mlp{l}_dact", X(db), wv[f"w2_{l}"], S, F, D, tb=True, outs=[_fresh(S, F, BF16)],
                        epilogue=_ep_drelu2, extras=[X(z)])
        (grads[f"w2_{l}"],) = _matmul(f"mlp{l}_dw2", X(act), X(db), F, D, S, ta=True, outs=[gout(f"w2_{l}")],
                                      epilogue=_ep_store)
        (grads[f"w1_{l}"],) = _matmul(f"mlp{l}_dw1", X(hm), X(dz), D, F, S, ta=True, outs=[gout(f"w1_{l}")],
                                      epilogue=_ep_store)
        (dhm,) = _matmul(f"mlp{l}_dhm", X(dz), wv[f"w1_{l}"], S, D, F, tb=True, outs=[_fresh(S, D, F32)],
                         epilogue=_ep_store)
        return _rms_bwd(f"mlp{l}_norm_bwd", dhm, xin, mlp_g[l:l + 1], d, S, D)

    h0 = _rms_fwd("mix0_norm", x, mix_g[0:1], S, D)
    (u0,) = _matmul("lru_in", X(h0), wv["lru_in"], S, 2 * D, D, outs=[_fresh(S, 2 * D, F32)], epilogue=_ep_store)
    y, xc, r, ig, hs = hosted("lru_fwd", _lru_fwd, u0, conv_w, conv_b, wr_bd, b_r, wi_bd, b_i, lam, S, D)
    (x1,) = _matmul("lru_out", X(y), wv["lru_out"], S, D, D, outs=[_fresh(S, D, F32)], epilogue=_ep_resid,
                    extras=[X(x)])
    hm0, z0, act0, x2 = mlp_fwd(0, x1)
    h1 = _rms_fwd("mix1_norm", x2, mix_g[1:2], S, D)
    (u1,) = _matmul("fox_in", wv["fox_in"], X(h1), NU, S, D, tb=True, outs=[_fresh(NU, S, F32)], epilogue=_ep_store)
    qat, kat, vat, ka, va, vt = _fox_prep(u1, b_f, qg, kg, S, D, tq)
    o, o32, lse = hosted("attn_forward", _attn_forward, ka, qat, vt, S, D, tq)
    (x3,) = _matmul("fox_out", X(o), wv["fox_out"], S, D, D, ta=True, outs=[_fresh(S, D, F32)], epilogue=_ep_resid,
                    extras=[X(x2)])
    hm1, z1, act1, x4 = mlp_fwd(1, x3)
    loss, d4, d4b = _loss_head(x4, tgt, S, D)

    d3, d3b, dg_mlp1 = mlp_bwd(1, x3, hm1, z1, act1, d4, d4b)
    (do,) = _matmul("fox_dout", wv["fox_out"], X(d3b), D, S, D, tb=True, outs=[_fresh(D, S, BF16)], epilogue=_ep_store)
    (grads["fox_out"],) = _matmul("fox_dwout", X(o), X(d3b), D, D, S, outs=[gout("fox_out")], epilogue=_ep_store)
    doat, doa, qat1, qa1 = _fox_bwd_prep(do, o32, lse, qat, S, D, tq)
    dqn, dcq = hosted("attn_backward_q", _attn_backward_q, ka, va, kat, qat1, doat, S, D, tq)
    dkn, dv, dck = _attn_backward_kv(qa1, doa, qat1, doat, kat, vat, S, D, tq)
    du1, dbf, dqg, dkg = _fox_prep_bwd(u1, dqn, dkn, dv, dcq, dck, b_f, qg, kg, S, D, tq)
    (grads["fox_in"],) = _matmul("fox_dwin", X(h1), X(du1), D, NU, S, ta=True, tb=True, outs=[gout("fox_in")],
                                 epilogue=_ep_store)
    (dh1,) = _matmul("fox_dh", X(du1), wv["fox_in"], S, D, NU, ta=True, outs=[_fresh(S, D, F32)], epilogue=_ep_store)
    d2, d2b, dg_mix1 = _rms_bwd("mix1_norm_bwd", dh1, x2, mix_g[1:2], d3, S, D)
    d1, d1b, dg_mlp0 = mlp_bwd(0, x1, hm0, z0, act0, d2, d2b)
    (dy,) = _matmul("lru_dout", X(d1b), wv["lru_out"], S, D, D, tb=True, outs=[_fresh(S, D, F32)], epilogue=_ep_store)
    (grads["lru_out"],) = _matmul("lru_dwout", X(y), X(d1b), D, D, S, ta=True, outs=[gout("lru_out")],
                                  epilogue=_ep_store)
    du0, dcw, dcb, dlam, dbr, dbi, dwr, dwi = hosted("lru_bwd", _lru_bwd, dy, u0, xc, r, ig, hs, conv_w, wr_bd, wi_bd,
                                                     lam, S, D)
    (grads["lru_in"],) = _matmul("lru_dwin", X(h0), X(du0), D, 2 * D, S, ta=True, outs=[gout("lru_in")],
                                 epilogue=_ep_store)
    (dh0,) = _matmul("lru_dh", X(du0), wv["lru_in"], S, D, 2 * D, tb=True, outs=[_fresh(S, D, F32)], epilogue=_ep_store)
    gx, _, dg_mix0 = _rms_bwd("mix0_norm_bwd", dh0, x, mix_g[0:1], d1, S, D)

    grads.update(
        mix_norm=jnp.concatenate([dg_mix0, dg_mix1], axis=0), mlp_norm=jnp.concatenate([dg_mlp0, dg_mlp1], axis=0),
        conv_w=dcw, lru_conv_b=dcb, lru_w_r=_block_diag_extract(dwr, nblk)[None], lru_b_r=dbr.reshape(1, nblk, -1),
        lru_w_i=_block_diag_extract(dwi, nblk)[None], lru_b_i=dbi.reshape(1, nblk, -1), lru_lambda=dlam,
        fox_b_f=dbf[:H].reshape(1, H), fox_q_gain=dqg.reshape(1, -1), fox_k_gain=dkg.reshape(1, -1))
    return loss, gx, grads


def _place():
    x, y, c = lax.axis_index("x"), lax.axis_index("y"), lax.axis_index("c")
    chips = [(1 - x, y), (x, 1 - y), (1 - x, 1 - y)]
    return x, y, c, 2 * x + y, chips


BOUNCE_BYTES = 1 << 20


def _bounce_shape(rows, cols, dtype):
    chunk = rows
    while chunk % 2 == 0 and chunk > 16 and chunk * cols * jnp.dtype(dtype).itemsize > BOUNCE_BYTES:
        chunk //= 2
    return pltpu.VMEM((2, chunk, cols), dtype)


def _bounce_copy(src, dst, buf, sem):
    chunk = buf.shape[1]
    n = src.shape[0] // chunk
    cin = lambda i: pltpu.make_async_copy(src.at[pl.ds(i * chunk, chunk)], buf.at[i % 2], sem.at[i % 2])
    cout = lambda i: pltpu.make_async_copy(buf.at[i % 2], dst.at[pl.ds(i * chunk, chunk)], sem.at[2 + i % 2])
    cin(0).start()
    for i in range(n):
        cin(i).wait()
        if i + 1 < n:
            if i >= 1:
                cout(i - 1).wait()
            cin(i + 1).start()
        cout(i).start()
    if n >= 2:
        cout(n - 2).wait()
    cout(n - 1).wait()


def _hbm_call(body, name, arrays, out_shape, n_dma_sems, bounce=()):
    scratch = [pltpu.SemaphoreType.DMA((k,)) for k in n_dma_sems]
    for rows, cols, dtype in bounce:
        scratch += [_bounce_shape(rows, cols, dtype), pltpu.SemaphoreType.DMA((4,))]
    return pl.pallas_call(
        body, name=name, in_specs=[ANY] * len(arrays), out_specs=[ANY] * len(out_shape), out_shape=out_shape,
        scratch_shapes=scratch,
        compiler_params=pltpu.CompilerParams(has_side_effects=True, vmem_limit_bytes=VMEM_LIMIT),
    )(*arrays)


class _Gather:
    def __init__(self, shards):
        n = self.n = len(shards)
        self.operands = list(shards)
        self.out_shape = [jax.ShapeDtypeStruct((N_CHIPS,) + tuple(a.shape), a.dtype) for a in shards]
        self.scratch = [pltpu.SemaphoreType.DMA((3 * n,)) for _ in range(4)]
        for a in shards:
            self.scratch += [_bounce_shape(a.shape[0], a.shape[1], a.dtype), pltpu.SemaphoreType.DMA((4,))]

    def _copies(self, ins, outs, scr):
        send, recv, fsend, frecv = scr[:4]
        x, y, c, s, chips = _place()

        def rows(a, chip_idx, which):
            hr = ins[a].shape[0] // 2
            return outs[a].at[chip_idx, pl.ds(which * hr, hr)]

        def landed(a, j, core):
            return rows(a, 2 * chips[j][0] + chips[j][1], core)

        def ici(a, j, mine):
            hr = ins[a].shape[0] // 2
            src, dst = (ins[a].at[pl.ds(c * hr, hr)], rows(a, s, c)) if mine else (landed(a, j, c),) * 2
            return pltpu.make_async_remote_copy(src_ref=src, dst_ref=dst, send_sem=send.at[3 * a + j],
                                                recv_sem=recv.at[3 * a + j], device_id=(*chips[j], c),
                                                device_id_type=MESH)

        def d2d(a, j, mine):
            ref = landed(a, j, c if mine else 1 - c)
            return pltpu.make_async_remote_copy(src_ref=ref, dst_ref=ref, send_sem=fsend.at[3 * a + j],
                                                recv_sem=frecv.at[3 * a + j], device_id=(x, y, 1 - c),
                                                device_id_type=MESH)

        return ici, d2d, s

    def start(self, ins, outs, scr):
        ici, _, _ = self._copies(ins, outs, scr)
        for a in range(self.n):
            for j in range(3):
                ici(a, j, True).start()

    def middle(self, ins, outs, scr):
        ici, d2d, s = self._copies(ins, outs, scr)
        for a in range(self.n):
            _bounce_copy(ins[a], outs[a].at[s], scr[4 + 2 * a], scr[5 + 2 * a])
        for a in range(self.n):
            for j in range(3):
                ici(a, j, False).wait_recv()
                d2d(a, j, True).start()

    def finish(self, ins, outs, scr):
        ici, d2d, _ = self._copies(ins, outs, scr)
        for a in range(self.n):
            for j in range(3):
                d2d(a, j, False).wait_recv()
        for a in range(self.n):
            for j in range(3):
                ici(a, j, True).wait_send()
                d2d(a, j, True).wait_send()


def _run_plan(name, plan):
    k_in, k_out = len(plan.operands), len(plan.out_shape)

    def body(*refs):
        parts = (refs[:k_in], refs[k_in:k_in + k_out], refs[k_in + k_out:])
        plan.start(*parts)
        plan.middle(*parts)
        plan.finish(*parts)

    return pl.pallas_call(
        body, name=name, in_specs=[ANY] * k_in, out_specs=[ANY] * k_out, out_shape=plan.out_shape,
        scratch_shapes=plan.scratch,
        compiler_params=pltpu.CompilerParams(has_side_effects=True, vmem_limit_bytes=VMEM_LIMIT),
    )(*plan.operands)


def _hosted_call(body, name, grid, in_specs, out_specs, out_shape, scratch_shapes, operands, sem, plan=None):
    if plan is None:
        res = pl.pallas_call(body, name=name, grid=grid, in_specs=in_specs, out_specs=out_specs, out_shape=out_shape,
                             scratch_shapes=scratch_shapes, compiler_params=_params(sem))(*operands)
        return res, None
    n_in, n_out, n_scr = len(in_specs), len(out_specs), len(scratch_shapes)
    k_in, k_out = len(plan.operands), len(plan.out_shape)
    total = int(np.prod(grid))

    def hosted(*refs):
        ins, refs = refs[:n_in], refs[n_in:]
        p_ins, refs = refs[:k_in], refs[k_in:]
        outs, refs = refs[:n_out], refs[n_out:]
        p_outs, refs = refs[:k_out], refs[k_out:]
        scr, p_scr = refs[:n_scr], refs[n_scr:]
        step = pl.program_id(0)
        for d in range(1, len(grid)):
            step = step * grid[d] + pl.program_id(d)
        pl.when(step == 0)(lambda: plan.start(p_ins, p_outs, p_scr))
        body(*ins, *outs, *scr)
        pl.when(step == total // 2)(lambda: plan.middle(p_ins, p_outs, p_scr))
        pl.when(step == total - 1)(lambda: plan.finish(p_ins, p_outs, p_scr))

    res = pl.pallas_call(
        hosted, name=name, grid=grid, in_specs=list(in_specs) + [ANY] * k_in, out_specs=list(out_specs) + [ANY] * k_out,
        out_shape=list(out_shape) + plan.out_shape, scratch_shapes=list(scratch_shapes) + plan.scratch,
        compiler_params=pltpu.CompilerParams(dimension_semantics=sem, vmem_limit_bytes=VMEM_LIMIT,
                                             has_side_effects=True),
    )(*operands, *plan.operands)
    return res[:n_out], res[n_out:]


def _all_gather(name, shards):
    return _run_plan(name, _Gather(shards))


def _pair_swap(name, arrs):
    n = len(arrs)

    def body(*refs):
        ins, outs = refs[:n], refs[n:2 * n]
        send, recv = refs[2 * n:]
        x, y, c, _, _ = _place()
        cps = []
        for a in range(n):
            hr = ins[a].shape[1] // 2
            cp = pltpu.make_async_remote_copy(
                src_ref=ins[a].at[:, pl.ds((1 - c) * hr, hr)], dst_ref=outs[a], send_sem=send.at[a],
                recv_sem=recv.at[a], device_id=(x, y, 1 - c), device_id_type=MESH)
            cp.start()
            cps.append(cp)
        for cp in cps:
            cp.wait()

    out_shape = [jax.ShapeDtypeStruct((a.shape[0], a.shape[1] // 2, a.shape[2]), a.dtype) for a in arrs]
    return _hbm_call(body, name, arrs, out_shape, (n, n))


class _Scatter:
    def __init__(self, parts):
        n = self.n = len(parts)
        self.operands = list(parts)
        self.out_shape = [jax.ShapeDtypeStruct(a.shape, a.dtype) for a in parts]
        self.scratch = [pltpu.SemaphoreType.DMA((3 * n,)) for _ in range(2)]
        for a in parts:
            self.scratch += [_bounce_shape(a.shape[1], a.shape[2], a.dtype), pltpu.SemaphoreType.DMA((4,))]

    def _copy(self, ins, outs, scr, a, j, mine):
        x, y, c, s, chips = _place()
        t = 2 * chips[j][0] + chips[j][1]
        return pltpu.make_async_remote_copy(
            src_ref=ins[a].at[t], dst_ref=outs[a].at[s if mine else t], send_sem=scr[0].at[3 * a + j],
            recv_sem=scr[1].at[3 * a + j], device_id=(*chips[j], c), device_id_type=MESH)

    def start(self, ins, outs, scr):
        for a in range(self.n):
            for j in range(3):
                self._copy(ins, outs, scr, a, j, True).start()

    def middle(self, ins, outs, scr):
        s = _place()[3]
        for a in range(self.n):
            _bounce_copy(ins[a].at[s], outs[a].at[s], scr[2 + 2 * a], scr[3 + 2 * a])

    def finish(self, ins, outs, scr):
        for a in range(self.n):
            for j in range(3):
                self._copy(ins, outs, scr, a, j, False).wait_recv()
        for a in range(self.n):
            for j in range(3):
                self._copy(ins, outs, scr, a, j, True).wait_send()


def _pair_gather(name, halves):
    n = len(halves)

    def body(*refs):
        ins, outs = refs[:n], refs[n:2 * n]
        send, recv = refs[2 * n:2 * n + 2]
        stage = refs[2 * n + 2:]
        x, y, c, _, _ = _place()
        cps = []
        for a in range(n):
            hr = ins[a].shape[0]
            cp = pltpu.make_async_remote_copy(
                src_ref=ins[a], dst_ref=outs[a].at[pl.ds(c * hr, hr)], send_sem=send.at[a], recv_sem=recv.at[a],
                device_id=(x, y, 1 - c), device_id_type=MESH)
            cp.start()
            cps.append((cp, hr))
        for a, (cp, hr) in enumerate(cps):
            _bounce_copy(ins[a], outs[a].at[pl.ds(c * hr, hr)], stage[2 * a], stage[2 * a + 1])
        for a, (cp, hr) in enumerate(cps):
            cp.wait_send()
            theirs = outs[a].at[pl.ds((1 - c) * hr, hr)]
            pltpu.make_async_remote_copy(src_ref=theirs, dst_ref=theirs, send_sem=send.at[a], recv_sem=recv.at[a],
                                         device_id=(x, y, 1 - c), device_id_type=MESH).wait_recv()

    out_shape = [jax.ShapeDtypeStruct((2 * a.shape[0], a.shape[1]), a.dtype) for a in halves]
    return _hbm_call(body, name, halves, out_shape, (n, n),
                     bounce=[(a.shape[0], a.shape[1], a.dtype) for a in halves])


def _row_tile(rows, cols, itemsize, n_bufs):
    budget = VMEM_LIMIT // 2
    for t in (1024, 512, 256, 128, 64, 32, 16):
        if rows % t == 0 and 2 * n_bufs * t * cols * itemsize <= budget:
            return t
    return rows


def _pair_add(name, g, gsib, core, out_dtype):
    _, r, cols = g.shape
    hr = r // 2
    t = _row_tile(hr, cols, 4, 3)
    per = hr // t

    def body(core_ref, a_ref, b_ref, o_ref):
        o_ref[...] = (a_ref[...].astype(F32) + b_ref[...].astype(F32)).astype(o_ref.dtype)

    grid_spec = pltpu.PrefetchScalarGridSpec(
        num_scalar_prefetch=1, grid=(N_CHIPS, per),
        in_specs=[pl.BlockSpec((None, t, cols), lambda s, i, core: (s, core[0] * per + i, 0)),
                  pl.BlockSpec((None, t, cols), lambda s, i, core: (s, i, 0))],
        out_specs=pl.BlockSpec((None, t, cols), lambda s, i, core: (s, i, 0)))
    return pl.pallas_call(body, name=name, grid_spec=grid_spec,
                          out_shape=jax.ShapeDtypeStruct((N_CHIPS, hr, cols), out_dtype),
                          compiler_params=_params(("arbitrary", "arbitrary")))(core, g, gsib)


def _chip_sum(name, parts):
    _, hr, cols = parts.shape
    t = _row_tile(hr, cols, 4, 5)

    def body(p_ref, o_ref):
        o_ref[...] = ((p_ref[0].astype(F32) + p_ref[1].astype(F32)) + p_ref[2].astype(F32)) + p_ref[3].astype(F32)

    return pl.pallas_call(
        body, name=name, grid=(hr // t,), in_specs=[pl.BlockSpec((N_CHIPS, t, cols), lambda i: (0, i, 0))],
        out_specs=pl.BlockSpec((t, cols), lambda i: (i, 0)), out_shape=jax.ShapeDtypeStruct((hr, cols), F32),
        compiler_params=_params(("arbitrary",)))(parts)


def _pair_partials(tag, arrs, wire_dtypes, core):
    sib = _pair_swap(f"{tag}_pair_swap", arrs)
    return _Scatter([_pair_add(f"{tag}_pair_add{i}", g, gs, core, dt)
                     for i, (g, gs, dt) in enumerate(zip(arrs, sib, wire_dtypes))])


def _finish_reduce(tag, scattered):
    halves = [_chip_sum(f"{tag}_chip_sum{i}", p) for i, p in enumerate(scattered)]
    return _pair_gather(f"{tag}_pair_gather", halves)


def _adamw(name, w, g_parts, m, v):
    rows, cols = w.shape
    n_parts = len(g_parts)
    part_rows = rows // n_parts
    t = _row_tile(part_rows, cols, 4, 7 + n_parts)
    per = part_rows // t
    c1 = 1.0 - ADAM_B1 ** ADAM_STEP
    c2 = 1.0 - ADAM_B2 ** ADAM_STEP

    def body(w_ref, m_ref, v_ref, *refs):
        g_refs, (go_ref, d_ref, nm_ref, nv_ref) = refs[:n_parts], refs[n_parts:]
        g = g_refs[0][...]
        for k in range(1, n_parts):
            g = jnp.where(pl.program_id(0) >= k * per, g_refs[k][...], g)
        go_ref[...] = g
        m = ADAM_B1 * m_ref[...] + (1.0 - ADAM_B1) * g
        v = ADAM_B2 * v_ref[...] + (1.0 - ADAM_B2) * (g * g)
        nm_ref[...] = m
        nv_ref[...] = v
        d_ref[...] = -ADAM_LR * ((m / c1) / (jnp.sqrt(v / c2) + ADAM_EPS) + ADAM_WD * w_ref[...])

    spec = pl.BlockSpec((t, cols), lambda i: (i, 0))
    g_specs = [pl.BlockSpec((t, cols), lambda i, k=k: (jnp.clip(i - k * per, 0, per - 1), 0)) for k in range(n_parts)]
    shp = jax.ShapeDtypeStruct((rows, cols), F32)
    return pl.pallas_call(body, name=name, grid=(rows // t,), in_specs=[spec] * 3 + g_specs, out_specs=[spec] * 4,
                          out_shape=[shp] * 4, compiler_params=_params(("arbitrary",)))(w, m, v, *g_parts)


_WEIGHTS = ["mix_norm", "mlp_norm", "mlp_w1", "mlp_w2", "lru_w_in", "lru_conv_w", "lru_conv_b", "lru_w_r", "lru_b_r",
            "lru_w_i", "lru_b_i", "lru_lambda", "lru_w_out", "fox_w_in", "fox_b_f", "fox_q_gain", "fox_k_gain",
            "fox_w_out"]
_REPLICATED = ["mix_norm", "mlp_norm", "lru_conv_b", "lru_w_r", "lru_b_r", "lru_w_i", "lru_b_i", "lru_lambda",
               "fox_b_f", "fox_q_gain", "fox_k_gain"]
_PACK_TILE = 2 * SUBLANES * LANES


def _as2d(a):
    return a.reshape(-1, a.shape[-1])


def kernel(x, mix_norm, mlp_norm, mlp_w1, mlp_w2, lru_w_in, lru_conv_w, lru_conv_b, lru_w_r, lru_b_r, lru_w_i, lru_b_i, lru_lambda, lru_w_out, fox_w_in, fox_b_f, fox_q_gain, fox_k_gain, fox_w_out, loss_target, m_mix_norm, m_mlp_norm, m_mlp_w1, m_mlp_w2, m_lru_w_in, m_lru_conv_w, m_lru_conv_b, m_lru_w_r, m_lru_b_r, m_lru_w_i, m_lru_b_i, m_lru_lambda, m_lru_w_out, m_fox_w_in, m_fox_b_f, m_fox_q_gain, m_fox_k_gain, m_fox_w_out, v_mix_norm, v_mlp_norm, v_mlp_w1, v_mlp_w2, v_lru_w_in, v_lru_conv_w, v_lru_conv_b, v_lru_w_r, v_lru_b_r, v_lru_w_i, v_lru_b_i, v_lru_lambda, v_lru_w_out, v_fox_w_in, v_fox_b_f, v_fox_q_gain, v_fox_k_gain, v_fox_w_out):
    args = dict(locals())
    W = {n: args[n] for n in _WEIGHTS}
    Mo = {n: args["m_" + n] for n in _WEIGHTS}
    Vo = {n: args["v_" + n] for n in _WEIGHTS}
    S, D = x.shape[1], x.shape[2]
    F = 4 * D
    H = D // HEAD_DIM
    NU = 3 * D + LANES
    FQ, DQ = F // N_CHIPS, D // N_CHIPS
    nfox = fox_w_in.shape[-1]
    chip = 2 * lax.axis_index("x") + lax.axis_index("y")
    core = lax.axis_index("c").astype(jnp.int32).reshape(1)

    cw_flat = jnp.pad(lru_conv_w.reshape(-1), (0, _PACK_TILE - CONV_WIDTH * DQ)).reshape(2 * SUBLANES, LANES)
    w1s, w2s = mlp_w1.astype(BF16), mlp_w2.astype(BF16)
    g_lin, g_lout, g_cw = _all_gather("gather_lru", [lru_w_in[0].astype(BF16), lru_w_out[0].astype(BF16), cw_flat])
    conv_w_full = jnp.transpose(g_cw.reshape(N_CHIPS, -1)[:, :CONV_WIDTH * DQ].reshape(N_CHIPS, CONV_WIDTH, DQ),
                                (1, 0, 2)).reshape(CONV_WIDTH, D)
    wv = {"lru_in": _View(g_lin, "cs"), "lru_out": _View(g_lout, "rs")}
    scattered = {}
    riding = {"attn_backward_q": ["w2_1", "w1_1", "fox_out"], "lru_bwd": ["fox_in", "w2_0", "w1_0", "lru_out"]}

    def shard_major(name, g):
        if name == "fox_in":
            return jnp.transpose(g[:, :nfox * N_CHIPS].reshape(D, N_CHIPS, nfox), (1, 0, 2))
        return g

    class Comm:
        @staticmethod
        def before(name, grads):
            if name == "lru_fwd":
                return _Gather([w1s[0], w2s[0], fox_w_in[0].astype(BF16)])
            if name == "attn_forward":
                return _Gather([fox_w_out[0].astype(BF16), w1s[1], w2s[1]])
            if name in riding:
                arrs = [shard_major(n, grads[n]) for n in riding[name]]
                return _pair_partials(f"{name}_grads", arrs, [BF16] * len(arrs), core)
            return None

        @staticmethod
        def after(name, res, wv):
            if name == "lru_fwd":
                fox_full = jnp.concatenate([res[2][s] for s in range(N_CHIPS)], axis=1)
                fox_full = jnp.pad(fox_full, ((0, 0), (0, NU - fox_full.shape[1])))
                wv.update(w1_0=_View(res[0], "cs"), w2_0=_View(res[1], "rs"), fox_in=_View(fox_full.T))
            elif name == "attn_forward":
                wv.update(fox_out=_View(res[0], "rs"), w1_1=_View(res[1], "cs"), w2_1=_View(res[2], "rs"))
            else:
                scattered.update(zip(riding[name], res))

    def grad_view(grads, name):
        if name in ("w1_0", "w1_1"):
            return _View(None, "cs", shape=(N_CHIPS, D, FQ), dtype=BF16)
        if name in ("w2_0", "w2_1"):
            return _View(None, "rs", shape=(N_CHIPS, FQ, D), dtype=BF16)
        if name == "lru_in":
            return _View(None, "cs", shape=(N_CHIPS, D, 2 * D // N_CHIPS), dtype=BF16)
        if name in ("lru_out", "fox_out"):
            return _View(None, "rs", shape=(N_CHIPS, DQ, D), dtype=BF16)
        return _View(None, shape=(D, NU), dtype=BF16)

    small = {n: W[n] for n in _REPLICATED}
    small["conv_w"] = conv_w_full

    loss, gx, grads = _local_step(x[0], loss_target[0], small, wv, grad_view, Comm)

    pack_names = _REPLICATED + ["conv_w"]
    flat = jnp.concatenate([grads[n].reshape(-1).astype(F32) for n in pack_names] + [loss.reshape(-1)])
    per_chip = -(-flat.shape[0] // (N_CHIPS * _PACK_TILE)) * _PACK_TILE
    pack = jnp.pad(flat, (0, N_CHIPS * per_chip - flat.shape[0])).reshape(N_CHIPS, per_chip // LANES, LANES)
    tail = _run_plan("tail_grads_chip_scatter", _pair_partials("tail_grads", [grads["lru_in"], pack], [BF16, F32], core))
    scattered.update(lru_in=tail[0], pack=tail[1])
    order = ["w1_0", "w1_1", "w2_0", "w2_1", "lru_in", "lru_out", "fox_in", "fox_out", "pack"]
    red = dict(zip(order, _finish_reduce("grads", [scattered[n] for n in order])))
    (all_pack,) = _all_gather("gather_small_grads", [red["pack"]])
    all_flat = all_pack.reshape(-1)
    G = {}
    off = 0
    for n in pack_names:
        shape = grads[n].shape if n == "conv_w" else W[n].shape
        size = int(np.prod(shape))
        G[n] = all_flat[off:off + size].reshape(shape)
        off += size
    total = all_flat[off]
    G["lru_conv_w"] = lax.dynamic_slice_in_dim(G.pop("conv_w"), chip * DQ, DQ, axis=1)[None]
    parts = {n: [_as2d(G[n])] for n in G}
    parts.update(mlp_w1=[red["w1_0"], red["w1_1"]], mlp_w2=[red["w2_0"], red["w2_1"]], lru_w_in=[red["lru_in"]],
                 lru_w_out=[red["lru_out"]], fox_w_in=[red["fox_in"]], fox_w_out=[red["fox_out"]])

    delta, new_m, new_v = {}, {}, {}
    for n in _WEIGHTS:
        go, d, nm, nv = _adamw(f"adamw_{n}", _as2d(W[n]), parts[n], _as2d(Mo[n]), _as2d(Vo[n]))
        G[n], delta[n], new_m[n], new_v[n] = (t.reshape(W[n].shape) for t in (go, d, nm, nv))

    return (total, gx[None], *[G[n] for n in _WEIGHTS], *[delta[n] for n in _WEIGHTS],
            *[new_m[n] for n in _WEIGHTS], *[new_v[n] for n in _WEIGHTS])
```

```python
import functools

import numpy as np
import jax
import jax.numpy as jnp
from jax import lax
from jax.experimental import pallas as pl
from jax.experimental.pallas import tpu as pltpu

F32 = jnp.float32
BF16 = jnp.bfloat16

HEAD_DIM = 64
LRU_BLOCK_DIM = 64
CONV_WIDTH = 4
LRU_C = 8.0
EPS = 1e-6
NEG_INF = -1e30
ADAM_LR = 0.001
ADAM_B1 = 0.9
ADAM_B2 = 0.999
ADAM_EPS = 1e-08
ADAM_WD = 0.01
ADAM_STEP = 10

N_CHIPS = 4
LANES = 128
SUBLANES = 8
MXU_DIM = 256
VMEM_LIMIT = 52 * 1024 * 1024
MESH = pl.DeviceIdType.MESH
ANY = pl.BlockSpec(memory_space=pl.ANY)


def _pick(n, prefs):
    for p in prefs:
        if p <= n and n % p == 0:
            return p
    return n


def _params(sem=None):
    return pltpu.CompilerParams(dimension_semantics=sem, vmem_limit_bytes=VMEM_LIMIT)


class _View:
    def __init__(self, arr, kind="plain", r0=0, rows=None, shape=None, dtype=None):
        self.arr = arr
        self.kind = kind
        self.r0 = r0
        self.shape = tuple(arr.shape) if arr is not None else tuple(shape)
        self.dtype = arr.dtype if arr is not None else dtype
        self.rows = rows if rows is not None else self.shape[-2]

    def limits(self):
        if self.kind == "plain":
            return 0, 0
        rows = int(np.gcd(self.rows, self.r0))
        return rows, (self.shape[-1] if self.kind == "cs" else 0)

    def spec(self, br, bc, fr, fc):
        if self.kind == "plain":
            return pl.BlockSpec((br, bc), lambda *g: (fr(*g), fc(*g)))
        ncol = self.shape[-1]
        r0b = self.r0 // br
        assert self.r0 % br == 0 and self.rows % br == 0 and ncol % bc == 0, (self.shape, self.r0, br, bc)
        if self.kind == "cs":
            per = ncol // bc
            return pl.BlockSpec((None, br, bc), lambda *g: (fc(*g) // per, r0b + fr(*g), fc(*g) % per))
        per = self.rows // br
        return pl.BlockSpec((None, br, bc), lambda *g: (fr(*g) // per, r0b + fr(*g) % per, fc(*g)))


def _bf(x):
    return x if x.dtype == BF16 else x.astype(BF16)


def _matmul(name, A, B, M, N, K, *, ta=False, tb=False, outs, epilogue, extras=(), vecs=(), n_sums=0,
            tm=None, tn=None, tk=None, plan=None):
    lim = {"m": [M], "n": [N], "k": [K]}
    for view, (rdim, cdim) in ([(A, "km" if ta else "mk"), (B, "nk" if tb else "kn")]
                               + [(e, "mn") for e in extras] + [(o, "mn") for o in outs]):
        r_lim, c_lim = view.limits()
        lim[rdim].append(r_lim)
        lim[cdim].append(c_lim)
    tm = tm or _pick(int(np.gcd.reduce(lim["m"])), (1024, 640, 512, 256, 128))
    tn = tn or _pick(int(np.gcd.reduce(lim["n"])), (1024, 640, 512, 256, 128))
    tk = tk or _pick(int(np.gcd.reduce(lim["k"])), (1024, 640, 512, 256, 128))
    nk = K // tk
    gi = lambda i, j, k: i
    gj = lambda i, j, k: j
    gk = lambda i, j, k: k
    a_spec = A.spec(tk, tm, gk, gi) if ta else A.spec(tm, tk, gi, gk)
    b_spec = B.spec(tn, tk, gj, gk) if tb else B.spec(tk, tn, gk, gj)
    ca = 0 if ta else 1
    cb = 1 if tb else 0
    ne, no = len(extras) + len(vecs), len(outs)
    assert n_sums == 0 or tn == N
    row_spec = pl.BlockSpec((1, tn), lambda i, j, k: (0, j))
    in_specs = [a_spec, b_spec] + [e.spec(tm, tn, gi, gj) for e in extras] + [row_spec] * len(vecs)
    operands = [A.arr, B.arr] + [e.arr for e in extras] + list(vecs)
    out_specs = [o.spec(tm, tn, gi, gj) for o in outs] + [row_spec] * n_sums
    out_shape = ([jax.ShapeDtypeStruct(o.shape, o.dtype) for o in outs]
                 + [jax.ShapeDtypeStruct((1, N), F32)] * n_sums)

    def body(*refs):
        a_ref, b_ref = refs[0], refs[1]
        ex = refs[2:2 + ne]
        o_refs = refs[2 + ne:2 + ne + no]
        s_refs = refs[2 + ne + no:2 + ne + no + n_sums]
        first_row_tile = pl.program_id(0) == 0

        def prod():
            return lax.dot_general(_bf(a_ref[...]), _bf(b_ref[...]), (((ca,), (cb,)), ((), ())),
                                   preferred_element_type=F32)

        def finish(acc):
            res = epilogue(acc, *[e[...] for e in ex])
            for o_ref, r in zip(o_refs, res[:no]):
                o_ref[...] = r.astype(o_ref.dtype)
            for s_ref, r in zip(s_refs, res[no:]):
                def assign(s_ref=s_ref, r=r):
                    s_ref[...] = r

                def accumulate(s_ref=s_ref, r=r):
                    s_ref[...] += r

                pl.when(first_row_tile)(assign)
                pl.when(jnp.logical_not(first_row_tile))(accumulate)

        if nk == 1:
            finish(prod())
        else:
            acc_ref = refs[-1]
            k = pl.program_id(2)

            @pl.when(k == 0)
            def _():
                acc_ref[...] = jnp.zeros_like(acc_ref)

            acc_ref[...] += prod()

            @pl.when(k == nk - 1)
            def _():
                finish(acc_ref[...])

    res, side = _hosted_call(body, name, (M // tm, N // tn, nk), in_specs, out_specs, out_shape,
                             [pltpu.VMEM((tm, tn), F32)] if nk > 1 else [], operands,
                             ("arbitrary", "arbitrary", "arbitrary"), plan)
    return res if plan is None else (res, side)


def _ep_store(acc):
    return (acc,)


def _ep_resid(acc, res):
    return (res + acc,)


def _ep_resid_norm(acc, res, g):
    xo = res + acc
    r = lax.rsqrt(jnp.mean(xo * xo, axis=-1, keepdims=True) + EPS)
    return (xo, (xo * r) * g)


def _ep_norm_bwd(acc, x, dres, g):
    r = lax.rsqrt(jnp.mean(x * x, axis=-1, keepdims=True) + EPS)
    xhat = x * r
    dxn = acc * g
    tot = dres + r * (dxn - xhat * jnp.mean(dxn * xhat, axis=-1, keepdims=True))
    return (tot, tot, jnp.sum(acc * xhat, axis=0, keepdims=True))


def _ep_relu2(acc):
    zp = jnp.maximum(acc, 0.0)
    return (acc, zp * zp)


def _ep_drelu2(acc, z):
    return (acc * (2.0 * jnp.maximum(z.astype(F32), 0.0)),)


def _fresh(M, N, dtype):
    return _View(None, shape=(M, N), dtype=dtype)


def _rms_fwd(name, x, g, S, D):
    T = _pick(S, (512, 256, 128))

    def body(x_ref, g_ref, h_ref):
        x = x_ref[...]
        r = lax.rsqrt(jnp.mean(x * x, axis=-1, keepdims=True) + EPS)
        h_ref[...] = ((x * r) * g_ref[...]).astype(BF16)

    return pl.pallas_call(
        body, name=name, grid=(S // T,),
        in_specs=[pl.BlockSpec((T, D), lambda i: (i, 0)), pl.BlockSpec((1, D), lambda i: (0, 0))],
        out_specs=pl.BlockSpec((T, D), lambda i: (i, 0)),
        out_shape=jax.ShapeDtypeStruct((S, D), BF16),
        compiler_params=_params(("arbitrary",)),
    )(x, g)


def _loss_head(x, tgt, S, D):
    T = _pick(S, (512, 256, 128))

    def body(x_ref, t_ref, loss_ref, d_ref, db_ref):
        @pl.when(pl.program_id(0) == 0)
        def _():
            loss_ref[...] = jnp.zeros_like(loss_ref)

        e = x_ref[...] - t_ref[...]
        loss_ref[...] += 0.5 * jnp.sum(jnp.mean(e * e, axis=-1, keepdims=True), axis=0, keepdims=True)
        d = e * (1.0 / D)
        d_ref[...] = d
        db_ref[...] = d.astype(BF16)

    row = pl.BlockSpec((T, D), lambda i: (i, 0))
    return pl.pallas_call(
        body, name="loss_head", grid=(S // T,), in_specs=[row, row],
        out_specs=[pl.BlockSpec((1, 1), lambda i: (0, 0)), row, row],
        out_shape=[jax.ShapeDtypeStruct((1, 1), F32), jax.ShapeDtypeStruct((S, D), F32),
                   jax.ShapeDtypeStruct((S, D), BF16)],
        compiler_params=_params(("arbitrary",)),
    )(x, tgt)


def _sigmoid(z):
    return 1.0 / (1.0 + jnp.exp(-z))


def _log_sigmoid(z):
    return jnp.minimum(z, 0.0) - jnp.log(1.0 + jnp.exp(-jnp.abs(z)))


_GELU_K = 0.7978845608028654
_GELU_C = 0.044715


def _gelu(x):
    t = jnp.tanh(_GELU_K * (x + _GELU_C * (x * x * x)))
    return 0.5 * x * (1.0 + t)


def _gelu_and_grad(x):
    x2 = x * x
    t = jnp.tanh(_GELU_K * (x + _GELU_C * (x2 * x)))
    g = 0.5 * x * (1.0 + t)
    dg = 0.5 * (1.0 + t) + 0.5 * x * (1.0 - t * t) * (_GELU_K * (1.0 + 3.0 * _GELU_C * x2))
    return g, dg


def _decay_terms(r, ls):
    la = LRU_C * r * ls
    a = jnp.exp(la)
    a2 = jnp.exp(2.0 * la)
    mult = jnp.sqrt(-jnp.tanh(la) * (a2 + 1.0))
    return a, a2, mult


def _lru_fwd(u0, conv_w, conv_b, wr_bd, b_r, wi_bd, b_i, lam, S, D, plan=None):
    T = _pick(S, (256, 128))
    GT = wr_bd.shape[-1]
    nG = D // GT

    def body(gb_ref, xb_ref, cw_ref, cb_ref, wr_ref, br_ref, wi_ref, bi_ref, lam_ref,
             y_ref, xc_ref, r_ref, i_ref, hs_ref, ext, a_scr, hcar):
        @pl.when(pl.program_id(0) == 0)
        def _():
            ext[0:SUBLANES, :] = jnp.zeros((SUBLANES, D), F32)
            hcar[...] = jnp.zeros_like(hcar)

        xb = xb_ref[...]
        ext[SUBLANES:SUBLANES + T, :] = xb
        xc = cb_ref[...]
        for k in range(CONV_WIDTH):
            xc = xc + ext[pl.ds(SUBLANES - (CONV_WIDTH - 1) + k, T), :] * cw_ref[k:k + 1, :]
        ext[0:SUBLANES, :] = xb[T - SUBLANES:T, :]
        xc_ref[...] = xc
        xcb = xc.astype(BF16)
        for g in range(nG):
            sl = slice(g * GT, (g + 1) * GT)
            zr = jnp.dot(xcb[:, sl], wr_ref[g], preferred_element_type=F32) + br_ref[:, sl]
            zi = jnp.dot(xcb[:, sl], wi_ref[g], preferred_element_type=F32) + bi_ref[:, sl]
            r_ref[:, sl] = _sigmoid(zr)
            i_ref[:, sl] = _sigmoid(zi)
        r = r_ref[...]
        a, _, mult = _decay_terms(r, _log_sigmoid(lam_ref[...]))
        a_scr[...] = a
        hs_ref[...] = mult * (i_ref[...] * xc)

        def step(t, h):
            h = a_scr[pl.ds(t, 1), :] * h + hs_ref[pl.ds(t, 1), :]
            hs_ref[pl.ds(t, 1), :] = h
            return h

        hcar[...] = lax.fori_loop(0, T, step, hcar[...], unroll=8)
        y_ref[...] = (_gelu(gb_ref[...]) * hs_ref[...]).astype(BF16)

    row = pl.BlockSpec((T, D), lambda i: (i, 0))
    vec = pl.BlockSpec((1, D), lambda i: (0, 0))
    bd = pl.BlockSpec((nG, GT, GT), lambda i: (0, 0, 0))
    f32o = jax.ShapeDtypeStruct((S, D), F32)
    return _hosted_call(
        body, "lru_fwd", (S // T,),
        [row, pl.BlockSpec((T, D), lambda i: (i, 1)), pl.BlockSpec((CONV_WIDTH, D), lambda i: (0, 0)), vec,
         bd, vec, bd, vec, vec],
        [row, row, row, row, row], [jax.ShapeDtypeStruct((S, D), BF16), f32o, f32o, f32o, f32o],
        [pltpu.VMEM((T + SUBLANES, D), F32), pltpu.VMEM((T, D), F32), pltpu.VMEM((1, D), F32)],
        (u0, u0, conv_w, conv_b, wr_bd, b_r, wi_bd, b_i, lam), ("arbitrary",), plan)


def _lru_bwd(dy, u0, xc, r, ig, hs, conv_w, wr_bd, wi_bd, lam, S, D, plan=None):
    T = _pick(S, (128,))
    nT = S // T
    GT = wr_bd.shape[-1]
    nG = D // GT
    W = CONV_WIDTH

    def body(dy_ref, gb_ref, xb_ref, xbp_ref, xc_ref, r_ref, i_ref, hs_ref, hsp_ref, cw_ref, wr_ref, wi_ref, lam_ref,
             du_ref, dcw_ref, dcb_ref, dlam_ref, dbr_ref, dbi_ref, dwr_ref, dwi_ref,
             a_scr, dh_scr, exth, extx, extd, dxc_scr, dz_scr, carry):
        step = pl.program_id(0)
        first_tile = step == nT - 1

        @pl.when(step == 0)
        def _():
            for ref in (dcw_ref, dcb_ref, dlam_ref, dbr_ref, dbi_ref, dwr_ref, dwi_ref, carry):
                ref[...] = jnp.zeros_like(ref)
            extd[T:T + SUBLANES, :] = jnp.zeros((SUBLANES, D), F32)

        hs = hs_ref[...]
        dy = dy_ref[...]
        g, dgelu = _gelu_and_grad(gb_ref[...])
        du_ref[:, 0:D] = (dy * hs * dgelu).astype(BF16)
        r = r_ref[...]
        lam = lam_ref[...]
        ls = _log_sigmoid(lam)
        a, a2, mult = _decay_terms(r, ls)
        a_scr[...] = a
        dh_scr[...] = dy * g

        def rstep(j, c):
            t = T - 1 - j
            d = dh_scr[pl.ds(t, 1), :] + c
            dh_scr[pl.ds(t, 1), :] = d
            return a_scr[pl.ds(t, 1), :] * d

        carry[...] = lax.fori_loop(0, T, rstep, carry[...], unroll=8)
        dh = dh_scr[...]
        keep = jnp.where(first_tile, 0.0, 1.0)
        exth[0:SUBLANES, :] = hsp_ref[...] * keep
        exth[SUBLANES:SUBLANES + T, :] = hs
        hprev = exth[pl.ds(SUBLANES - 1, T), :]
        xc = xc_ref[...]
        ig = i_ref[...]
        da = dh * hprev
        dmult = dh * (ig * xc)
        dla = da * a - dmult * (a2 / mult)
        dlam_ref[...] += jnp.sum(dla * r, axis=0, keepdims=True) * (LRU_C * _sigmoid(-lam))
        dzr = (dla * (LRU_C * ls)) * (r * (1.0 - r))
        dzi = (dh * (mult * xc)) * (ig * (1.0 - ig))
        dbr_ref[...] += jnp.sum(dzr, axis=0, keepdims=True)
        dbi_ref[...] += jnp.sum(dzi, axis=0, keepdims=True)
        dxc_scr[...] = dh * (mult * ig)
        xcb = xc.astype(BF16)
        dz_scr[0] = dzr.astype(BF16)
        dz_scr[1] = dzi.astype(BF16)
        nt_dims = (((1,), (1,)), ((), ()))
        tn_dims = (((0,), (0,)), ((), ()))
        for gq in range(nG):
            sl = slice(gq * GT, (gq + 1) * GT)
            zr_g = dz_scr[0, :, sl]
            zi_g = dz_scr[1, :, sl]
            dxc_scr[:, sl] += (lax.dot_general(zr_g, wr_ref[gq], nt_dims, preferred_element_type=F32)
                               + lax.dot_general(zi_g, wi_ref[gq], nt_dims, preferred_element_type=F32))
            dwr_ref[gq] += lax.dot_general(xcb[:, sl], zr_g, tn_dims, preferred_element_type=F32)
            dwi_ref[gq] += lax.dot_general(xcb[:, sl], zi_g, tn_dims, preferred_element_type=F32)
        dxc = dxc_scr[...]
        dcb_ref[...] += jnp.sum(dxc, axis=0, keepdims=True)
        extx[0:SUBLANES, :] = xbp_ref[...] * keep
        extx[SUBLANES:SUBLANES + T, :] = xb_ref[...]
        extd[0:T, :] = dxc
        dxb = jnp.zeros((T, D), F32)
        for k in range(W):
            dxb = dxb + extd[pl.ds(W - 1 - k, T), :] * cw_ref[k:k + 1, :]
            dcw_ref[k:k + 1, :] += jnp.sum(dxc * extx[pl.ds(SUBLANES - (W - 1) + k, T), :], axis=0, keepdims=True)
        extd[T:T + SUBLANES, :] = dxc[0:SUBLANES, :]
        du_ref[:, D:2 * D] = dxb.astype(BF16)

    rev = lambda i: nT - 1 - i
    tpb = T // SUBLANES
    prev8 = lambda i: jnp.maximum(rev(i) * tpb - 1, 0)
    row = pl.BlockSpec((T, D), lambda i: (rev(i), 0))
    vec = pl.BlockSpec((1, D), lambda i: (0, 0))
    bd = pl.BlockSpec((nG, GT, GT), lambda i: (0, 0, 0))
    vec_o = jax.ShapeDtypeStruct((1, D), F32)
    bd_o = jax.ShapeDtypeStruct((nG, GT, GT), F32)
    return _hosted_call(
        body, "lru_bwd", (nT,),
        [row, row, pl.BlockSpec((T, D), lambda i: (rev(i), 1)), pl.BlockSpec((SUBLANES, D), lambda i: (prev8(i), 1)),
         row, row, row, row, pl.BlockSpec((SUBLANES, D), lambda i: (prev8(i), 0)),
         pl.BlockSpec((W, D), lambda i: (0, 0)), bd, bd, vec],
        [pl.BlockSpec((T, 2 * D), lambda i: (rev(i), 0)), pl.BlockSpec((W, D), lambda i: (0, 0)),
         vec, vec, vec, vec, bd, bd],
        [jax.ShapeDtypeStruct((S, 2 * D), BF16), jax.ShapeDtypeStruct((W, D), F32), vec_o, vec_o, vec_o, vec_o, bd_o, bd_o],
        [pltpu.VMEM((T, D), F32), pltpu.VMEM((T, D), F32), pltpu.VMEM((T + SUBLANES, D), F32),
         pltpu.VMEM((T + SUBLANES, D), F32), pltpu.VMEM((T + SUBLANES, D), F32),
         pltpu.VMEM((T, D), F32), pltpu.VMEM((2, T, D), BF16), pltpu.VMEM((1, D), F32)],
        (dy, u0, u0, u0, xc, r, ig, hs, hs, conv_w, wr_bd, wi_bd, lam), ("arbitrary",), plan)


AUG_ROWS = 16
HEAD_ROWS = 128
LSE_ROW = HEAD_DIM + 6


def _split3(x):
    b1 = x.astype(BF16).astype(F32)
    r = x - b1
    b2 = r.astype(BF16).astype(F32)
    return b1, b2, r - b2


def _head_block(x, aug, T):
    row = lax.broadcasted_iota(jnp.int32, (AUG_ROWS, T), 0)
    blk = jnp.zeros((AUG_ROWS, T), F32)
    for i, e in enumerate(aug):
        blk = jnp.where(row == i, e, blk)
    return jnp.concatenate([x, blk, jnp.zeros((HEAD_ROWS - HEAD_DIM - AUG_ROWS, T), F32)], axis=0)


def _tri_matrix(lower):
    i = np.arange(LANES)
    m = (i[:, None] >= i[None, :]) if lower else (i[:, None] <= i[None, :])
    return jnp.asarray(m.astype(np.float32), BF16)


def _lane_cumsum(x, tri_ref, carry, reverse):
    n = x.shape[1] // LANES
    tri = tri_ref[...]
    out = [None] * n
    for j in (range(n - 1, -1, -1) if reverse else range(n)):
        cs = carry
        for part in _split3(x[:, j * LANES:(j + 1) * LANES]):
            cs = cs + jnp.dot(part.astype(BF16), tri, preferred_element_type=F32)
        out[j] = cs
        carry = cs[:, 0:1] if reverse else cs[:, LANES - 1:LANES]
    return jnp.concatenate(out, axis=1), carry


def _head_rows(h):
    return pl.ds(pl.multiple_of(h * HEAD_DIM, HEAD_DIM), HEAD_DIM)


def _fox_prep(ut, b_f, qg, kg, S, D, tq):
    H = D // HEAD_DIM
    T = min(tq, 256)
    per = tq // T
    scale = HEAD_DIM ** -0.5

    def body(q_ref, k_ref, v_ref, f_ref, bf_ref, qg_ref, kg_ref, tri_ref,
             qat_ref, kat_ref, vat_ref, ka_ref, va_ref, vt_ref, c_scr, ccar):
        @pl.when(pl.program_id(0) == 0)
        def _():
            ccar[...] = jnp.zeros_like(ccar)

        c, carry = _lane_cumsum(_log_sigmoid(f_ref[...] + bf_ref[...]), tri_ref, ccar[...], False)
        c_scr[...] = c
        ccar[...] = carry

        def head(h, _):
            rows = _head_rows(h)
            c1, c2, c3 = _split3(c_scr[pl.ds(h, 1), :])

            def normed(src, gain, mul):
                x = src[rows, :]
                rs = lax.rsqrt(jnp.mean(x * x, axis=0, keepdims=True) + EPS)
                return ((x * rs) * gain[rows, :]) * mul

            qat_ref[h] = _head_block(normed(q_ref, qg_ref, scale), [c1, c2, c3, 1.0, 1.0, 1.0], T).astype(BF16)
            kb = _head_block(normed(k_ref, kg_ref, 1.0), [1.0, 1.0, 1.0, -c1, -c2, -c3, 1.0, 1.0, 1.0], T)
            kat_ref[h] = kb.astype(BF16)
            ka_ref[h] = kb.T.astype(BF16)
            v = v_ref[rows, :]
            vt_ref[h] = v.astype(BF16)
            vb = _head_block(v, [1.0, 1.0, 1.0], T)
            vat_ref[h] = vb.astype(BF16)
            va_ref[h] = vb.T.astype(BF16)
            return 0

        lax.fori_loop(0, H, head, 0)

    part = lambda j: pl.BlockSpec((D, T), lambda i: (j, i))
    colv = lambda n: pl.BlockSpec((n, 1), lambda i: (0, 0))
    tmaj = lambda r: pl.BlockSpec((H, None, r, T), lambda i: (0, i // per, 0, i % per))
    norm = pl.BlockSpec((H, T, HEAD_ROWS), lambda i: (0, i, 0))
    tshape = lambda r: jax.ShapeDtypeStruct((H, S // tq, r, tq), BF16)
    nshape = jax.ShapeDtypeStruct((H, S, HEAD_ROWS), BF16)
    return pl.pallas_call(
        body, name="fox_prep", grid=(S // T,),
        in_specs=[part(0), part(1), part(2), pl.BlockSpec((LANES, T), lambda i: (3 * D // LANES, i)),
                  colv(LANES), colv(D), colv(D), pl.BlockSpec((LANES, LANES), lambda i: (0, 0))],
        out_specs=[tmaj(HEAD_ROWS), tmaj(HEAD_ROWS), tmaj(HEAD_ROWS), norm, norm, tmaj(HEAD_DIM)],
        out_shape=[tshape(HEAD_ROWS), tshape(HEAD_ROWS), tshape(HEAD_ROWS), nshape, nshape, tshape(HEAD_DIM)],
        scratch_shapes=[pltpu.VMEM((LANES, T), F32), pltpu.VMEM((LANES, 1), F32)],
        compiler_params=_params(("arbitrary",)),
    )(ut, ut, ut, ut, b_f, qg, kg, _tri_matrix(False))


def _fox_bwd_prep(dot, ot, lse, qat, S, D, tq):
    H = D // HEAD_DIM
    T = min(tq, 256)
    per = tq // T

    def body(do_ref, o_ref, lse_ref, qat_ref, doat_ref, doa_ref, qat1_ref, qa1_ref):
        row = lax.broadcasted_iota(jnp.int32, (HEAD_ROWS, T), 0)

        def head(h, _):
            rows = _head_rows(h)
            do = do_ref[rows, :].astype(F32)
            delta = jnp.sum(do * o_ref[rows, :], axis=0, keepdims=True)
            db = _head_block(do, list(_split3(-delta)), T)
            doat_ref[h] = db.astype(BF16)
            doa_ref[h] = db.T.astype(BF16)
            qb = qat_ref[h].astype(F32)
            for i, e in enumerate(_split3(-lse_ref[h])):
                qb = jnp.where(row == LSE_ROW + i, e, qb)
            qat1_ref[h] = qb.astype(BF16)
            qa1_ref[h] = qb.T.astype(BF16)
            return 0

        lax.fori_loop(0, H, head, 0)

    chan = pl.BlockSpec((D, T), lambda i: (0, i))
    tmaj = pl.BlockSpec((H, None, HEAD_ROWS, T), lambda i: (0, i // per, 0, i % per))
    norm = pl.BlockSpec((H, T, HEAD_ROWS), lambda i: (0, i, 0))
    tshape = jax.ShapeDtypeStruct((H, S // tq, HEAD_ROWS, tq), BF16)
    nshape = jax.ShapeDtypeStruct((H, S, HEAD_ROWS), BF16)
    return pl.pallas_call(
        body, name="fox_bwd_prep", grid=(S // T,),
        in_specs=[chan, chan, pl.BlockSpec((H, 1, T), lambda i: (0, 0, i)), tmaj],
        out_specs=[tmaj, norm, tmaj, norm], out_shape=[tshape, nshape, tshape, nshape],
        compiler_params=_params(("arbitrary",)),
    )(dot, ot, lse, qat)


def _causal(s, k_axis):
    ki = lax.broadcasted_iota(jnp.int32, s.shape, k_axis)
    qi = lax.broadcasted_iota(jnp.int32, s.shape, 1 - k_axis)
    return jnp.where(ki <= qi, s, NEG_INF)


def _seq_tile(i, t):
    return pl.ds(pl.multiple_of(i * t, t), t)


def _attn_forward(ka, qat, vt, S, D, tq, plan=None):
    H = D // HEAD_DIM
    nq = S // tq

    def body(ka_ref, qat_ref, vt_ref, o_ref, o32_ref, lse_ref, m_scr, l_scr, acc_scr):
        qi = pl.program_id(1)
        m_scr[...] = jnp.full_like(m_scr, NEG_INF)
        l_scr[...] = jnp.zeros_like(l_scr)
        acc_scr[...] = jnp.zeros_like(acc_scr)
        qa = qat_ref[...]

        def tile(ki, diagonal):
            s = jnp.dot(ka_ref[_seq_tile(ki, tq), :], qa, preferred_element_type=F32)
            if diagonal:
                s = _causal(s, 0)
            m_prev = m_scr[...]
            m_new = jnp.maximum(m_prev, jnp.max(s, axis=0, keepdims=True))
            alpha = jnp.exp(m_prev - m_new)
            p = jnp.exp(s - m_new)
            l_scr[...] = alpha * l_scr[...] + jnp.sum(p, axis=0, keepdims=True)
            acc_scr[...] = alpha * acc_scr[...] + jnp.dot(vt_ref[ki], p.astype(BF16), preferred_element_type=F32)
            m_scr[...] = m_new

        def off_diagonal_pair(j, _):
            tile(2 * j, False)
            tile(2 * j + 1, False)
            return 0

        lax.fori_loop(0, qi // 2, off_diagonal_pair, 0)
        pl.when(qi % 2 == 1)(lambda: tile(qi - 1, False))
        tile(qi, True)
        o = acc_scr[...] / l_scr[...]
        o_ref[...] = o.astype(BF16)
        o32_ref[...] = o
        lse_ref[...] = m_scr[...] + jnp.log(l_scr[...])

    chan = pl.BlockSpec((HEAD_DIM, tq), lambda h, i: (h, i))
    stat = pl.BlockSpec((None, 1, tq), lambda h, i: (h, 0, i))
    return _hosted_call(
        body, "attn_forward", (H, nq),
        [pl.BlockSpec((None, S, HEAD_ROWS), lambda h, i: (h, 0, 0)),
         pl.BlockSpec((None, None, HEAD_ROWS, tq), lambda h, i: (h, i, 0, 0)),
         pl.BlockSpec((None, nq, HEAD_DIM, tq), lambda h, i: (h, 0, 0, 0))],
        [chan, chan, stat],
        [jax.ShapeDtypeStruct((D, S), BF16), jax.ShapeDtypeStruct((D, S), F32), jax.ShapeDtypeStruct((H, 1, S), F32)],
        [pltpu.VMEM((1, tq), F32), pltpu.VMEM((1, tq), F32), pltpu.VMEM((HEAD_DIM, tq), F32)],
        (ka, qat, vt), ("arbitrary", "arbitrary"), plan)


def _attn_backward_q(ka, va, kat, qat, doat, S, D, tq, plan=None):
    H = D // HEAD_DIM
    nq = S // tq

    def body(ka_ref, va_ref, kat_ref, qat_ref, doat_ref, dq_ref, dcq_ref, dq_scr, rs_scr):
        qi = pl.program_id(1)
        dq_scr[...] = jnp.zeros_like(dq_scr)
        rs_scr[...] = jnp.zeros_like(rs_scr)
        qa = qat_ref[...]
        doa = doat_ref[...]

        def tile(ki, diagonal):
            rows = _seq_tile(ki, tq)
            s = jnp.dot(ka_ref[rows, :], qa, preferred_element_type=F32)
            if diagonal:
                s = _causal(s, 0)
            ds = jnp.exp(s) * jnp.dot(va_ref[rows, :], doa, preferred_element_type=F32)
            rs_scr[...] += jnp.sum(ds, axis=0, keepdims=True)
            dq_scr[...] += jnp.dot(kat_ref[ki, 0:HEAD_DIM, :], ds.astype(BF16), preferred_element_type=F32)

        def off_diagonal(ki, _):
            tile(ki, False)
            return 0

        lax.fori_loop(0, qi, off_diagonal, 0)
        tile(qi, True)
        dq_ref[...] = dq_scr[...]
        dcq_ref[...] = rs_scr[...]

    whole = pl.BlockSpec((None, S, HEAD_ROWS), lambda h, i: (h, 0, 0))
    one = pl.BlockSpec((None, None, HEAD_ROWS, tq), lambda h, i: (h, i, 0, 0))
    return _hosted_call(
        body, "attn_backward_q", (H, nq),
        [whole, whole, pl.BlockSpec((None, nq, HEAD_ROWS, tq), lambda h, i: (h, 0, 0, 0)), one, one],
        [pl.BlockSpec((HEAD_DIM, tq), lambda h, i: (h, i)), pl.BlockSpec((None, 1, tq), lambda h, i: (h, 0, i))],
        [jax.ShapeDtypeStruct((D, S), F32), jax.ShapeDtypeStruct((H, 1, S), F32)],
        [pltpu.VMEM((HEAD_DIM, tq), F32), pltpu.VMEM((1, tq), F32)],
        (ka, va, kat, qat, doat), ("arbitrary", "arbitrary"), plan)


def _attn_backward_kv(qa, doa, qat, doat, kat, vat, S, D, tq):
    H = D // HEAD_DIM
    nq = S // tq

    def body(qa_ref, doa_ref, qat_ref, doat_ref, kat_ref, vat_ref, dk_ref, dv_ref, dck_ref, dk_scr, dv_scr, cs_scr):
        ki = pl.program_id(1)
        dk_scr[...] = jnp.zeros_like(dk_scr)
        dv_scr[...] = jnp.zeros_like(dv_scr)
        cs_scr[...] = jnp.zeros_like(cs_scr)
        ka = kat_ref[...]
        va = vat_ref[...]

        def tile(qi, diagonal):
            rows = _seq_tile(qi, tq)
            s = jnp.dot(qa_ref[rows, :], ka, preferred_element_type=F32)
            if diagonal:
                s = _causal(s, 1)
            p = jnp.exp(s)
            ds = p * jnp.dot(doa_ref[rows, :], va, preferred_element_type=F32)
            dv_scr[...] += jnp.dot(doat_ref[qi, 0:HEAD_DIM, :], p.astype(BF16), preferred_element_type=F32)
            dk_scr[...] += jnp.dot(qat_ref[qi, 0:HEAD_DIM, :], ds.astype(BF16), preferred_element_type=F32)
            cs_scr[...] += jnp.sum(ds, axis=0, keepdims=True)

        def off_diagonal(qi, _):
            tile(qi, False)
            return 0

        tile(ki, True)
        lax.fori_loop(ki + 1, nq, off_diagonal, 0)
        dk_ref[...] = dk_scr[...]
        dv_ref[...] = dv_scr[...].astype(BF16)
        dck_ref[...] = cs_scr[...]

    whole = pl.BlockSpec((None, S, HEAD_ROWS), lambda h, i: (h, 0, 0))
    tiles = pl.BlockSpec((None, nq, HEAD_ROWS, tq), lambda h, i: (h, 0, 0, 0))
    one = pl.BlockSpec((None, None, HEAD_ROWS, tq), lambda h, i: (h, i, 0, 0))
    chan = pl.BlockSpec((HEAD_DIM, tq), lambda h, i: (h, i))
    return pl.pallas_call(
        body, name="attn_backward_kv", grid=(H, nq),
        in_specs=[whole, whole, tiles, tiles, one, one],
        out_specs=[chan, chan, pl.BlockSpec((None, 1, tq), lambda h, i: (h, 0, i))],
        out_shape=[jax.ShapeDtypeStruct((D, S), F32), jax.ShapeDtypeStruct((D, S), BF16),
                   jax.ShapeDtypeStruct((H, 1, S), F32)],
        scratch_shapes=[pltpu.VMEM((HEAD_DIM, tq), F32), pltpu.VMEM((HEAD_DIM, tq), F32), pltpu.VMEM((1, tq), F32)],
        compiler_params=_params(("arbitrary", "arbitrary")),
    )(qa, doa, qat, doat, kat, vat)


def _fox_prep_bwd(ut, dqt, dkt, dvt, dcq, dck, b_f, qg, kg, S, D, tq):
    H = D // HEAD_DIM
    T = min(tq, 256)
    nT = S // T
    NU = 3 * D + LANES
    scale = HEAD_DIM ** -0.5

    def body(q_ref, k_ref, f_ref, dq_ref, dk_ref, dv_ref, dcq_ref, dck_ref, bf_ref, qg_ref, kg_ref, tri_ref,
             du_ref, dbf_ref, dqg_ref, dkg_ref, gq_acc, gk_acc, fcar, dc_scr):
        step = pl.program_id(0)

        @pl.when(step == 0)
        def _():
            for ref in (gq_acc, gk_acc, fcar, dbf_ref):
                ref[...] = jnp.zeros_like(ref)

        dc_scr[...] = jnp.zeros_like(dc_scr)

        def head(h, _):
            rows = _head_rows(h)
            dc_scr[pl.ds(h, 1), :] = dcq_ref[h] - dck_ref[h]
            for src, dsrc, gain, acc, mul, base in ((q_ref, dq_ref, qg_ref, gq_acc, scale, 0),
                                                    (k_ref, dk_ref, kg_ref, gk_acc, 1.0, D)):
                x = src[rows, :]
                rs = lax.rsqrt(jnp.mean(x * x, axis=0, keepdims=True) + EPS)
                xhat = x * rs
                dn = dsrc[rows, :] * mul
                acc[rows, :] += jnp.sum(dn * xhat, axis=1, keepdims=True)
                dxh = dn * gain[rows, :]
                dx = rs * (dxh - xhat * jnp.mean(dxh * xhat, axis=0, keepdims=True))
                du_ref[pl.ds(pl.multiple_of(base + h * HEAD_DIM, HEAD_DIM), HEAD_DIM), :] = dx.astype(BF16)
            return 0

        lax.fori_loop(0, H, head, 0)
        du_ref[2 * D:3 * D, :] = dv_ref[...]
        dlf, carry = _lane_cumsum(dc_scr[...], tri_ref, fcar[...], True)
        fcar[...] = carry
        dfl = dlf * _sigmoid(-(f_ref[...] + bf_ref[...]))
        dbf_ref[...] += jnp.sum(dfl, axis=1, keepdims=True)
        du_ref[3 * D:NU, :] = dfl.astype(BF16)

        @pl.when(step == nT - 1)
        def _():
            for acc, ref in ((gq_acc, dqg_ref), (gk_acc, dkg_ref)):
                tot = jnp.zeros((HEAD_DIM, 1), F32)
                for h in range(H):
                    tot = tot + acc[h * HEAD_DIM:(h + 1) * HEAD_DIM, :]
                ref[...] = tot

    rev = lambda i: nT - 1 - i
    part = lambda j: pl.BlockSpec((D, T), lambda i: (j, rev(i)))
    chan = pl.BlockSpec((D, T), lambda i: (0, rev(i)))
    stat = pl.BlockSpec((H, 1, T), lambda i: (0, 0, rev(i)))
    colv = lambda n: pl.BlockSpec((n, 1), lambda i: (0, 0))
    return pl.pallas_call(
        body, name="fox_prep_bwd", grid=(nT,),
        in_specs=[part(0), part(1), pl.BlockSpec((LANES, T), lambda i: (3 * D // LANES, rev(i))), chan, chan, chan,
                  stat, stat, colv(LANES), colv(D), colv(D), pl.BlockSpec((LANES, LANES), lambda i: (0, 0))],
        out_specs=[pl.BlockSpec((NU, T), lambda i: (0, rev(i))), colv(LANES), colv(HEAD_DIM), colv(HEAD_DIM)],
        out_shape=[jax.ShapeDtypeStruct((NU, S), BF16), jax.ShapeDtypeStruct((LANES, 1), F32),
                   jax.ShapeDtypeStruct((HEAD_DIM, 1), F32), jax.ShapeDtypeStruct((HEAD_DIM, 1), F32)],
        scratch_shapes=[pltpu.VMEM((D, 1), F32), pltpu.VMEM((D, 1), F32), pltpu.VMEM((LANES, 1), F32),
                        pltpu.VMEM((LANES, T), F32)],
        compiler_params=_params(("arbitrary",)),
    )(ut, ut, ut, dqt, dkt, dvt, dcq, dck, b_f, qg, kg, _tri_matrix(True))


def _block_diag_tiles(w):
    n = w.shape[0]
    per = min(MXU_DIM, n * LRU_BLOCK_DIM) // LRU_BLOCK_DIM
    eye = jnp.eye(per, dtype=w.dtype)
    w5 = w.reshape(n // per, per, LRU_BLOCK_DIM, 1, LRU_BLOCK_DIM) * eye[None, :, None, :, None]
    return w5.reshape(n // per, per * LRU_BLOCK_DIM, per * LRU_BLOCK_DIM).astype(BF16)


def _block_diag_extract(t, n):
    per = t.shape[-1] // LRU_BLOCK_DIM
    eye = jnp.eye(per, dtype=t.dtype)
    t5 = t.reshape(n // per, per, LRU_BLOCK_DIM, per, LRU_BLOCK_DIM) * eye[None, :, None, :, None]
    return t5.sum(axis=3).reshape(n, LRU_BLOCK_DIM, LRU_BLOCK_DIM)


def _local_step(x, tgt, small, wv, grad_view, comm=None):
    S, D = x.shape
    F = 4 * D
    H = D // HEAD_DIM
    nblk = D // LRU_BLOCK_DIM
    NU = 3 * D + LANES
    tq = max(LANES, min(512, S // 4))
    assert S % tq == 0
    vec = lambda a: a.reshape(1, -1).astype(F32)
    col = lambda a: a.reshape(-1, 1).astype(F32)
    mix_g, mlp_g = small["mix_norm"], small["mlp_norm"]
    conv_w, conv_b = small["conv_w"], vec(small["lru_conv_b"])
    wr_bd, wi_bd = _block_diag_tiles(small["lru_w_r"][0]), _block_diag_tiles(small["lru_w_i"][0])
    b_r, b_i, lam = vec(small["lru_b_r"]), vec(small["lru_b_i"]), vec(small["lru_lambda"])
    b_f = jnp.pad(col(small["fox_b_f"]), ((0, LANES - H), (0, 0)))
    qg, kg = jnp.tile(col(small["fox_q_gain"]), (H, 1)), jnp.tile(col(small["fox_k_gain"]), (H, 1))
    X = lambda a: _View(a)
    grads = {}
    gout = functools.partial(grad_view, grads)

    def hosted(name, fn, *args):
        plan = comm.before(name, grads) if comm is not None else None
        res, side = fn(*args, plan=plan)
        if plan is not None:
            comm.after(name, side, wv)
        return res

    def hosted_mm(name, *args, **kw):
        plan = comm.before(name, grads) if comm is not None else None
        if plan is None:
            return _matmul(name, *args, **kw)
        res, side = _matmul(name, *args, plan=plan, **kw)
        comm.after(name, side, wv)
        return res

    norm_rows = _pick(S, (512, 256, 128))
    two = lambda: [_fresh(S, D, F32), _fresh(S, D, BF16)]

    def mlp_up(l, hm):
        return hosted_mm(f"mlp{l}_up", X(hm), wv[f"w1_{l}"], S, F, D, outs=[_fresh(S, F, BF16), _fresh(S, F, BF16)],
                         epilogue=_ep_relu2)

    def mlp_bwd(l, xin, hm, z, act, d, db):
        (dz,) = hosted_mm(f"mlp{l}_dact", X(db), wv[f"w2_{l}"], S, F, D, tb=True, outs=[_fresh(S, F, BF16)],
                          epilogue=_ep_drelu2, extras=[X(z)])
        (grads[f"w2_{l}"],) = _matmul(f"mlp{l}_dw2", X(act), X(db), F, D, S, ta=True, outs=[gout(f"w2_{l}")],
                                      epilogue=_ep_store)
        (grads[f"w1_{l}"],) = _matmul(f"mlp{l}_dw1", X(hm), X(dz), D, F, S, ta=True, outs=[gout(f"w1_{l}")],
                                      epilogue=_ep_store)
        return _matmul(f"mlp{l}_dhm", X(dz), wv[f"w1_{l}"], S, D, F, tb=True, outs=two(), n_sums=1,
                       epilogue=_ep_norm_bwd, extras=[X(xin), X(d)], vecs=[mlp_g[l:l + 1]], tm=norm_rows)

    h0 = _rms_fwd("mix0_norm", x, mix_g[0:1], S, D)
    (u0,) = _matmul("lru_in", X(h0), wv["lru_in"], S, 2 * D, D, outs=[_fresh(S, 2 * D, F32)], epilogue=_ep_store)
    y, xc, r, ig, hs = hosted("lru_fwd", _lru_fwd, u0, conv_w, conv_b, wr_bd, b_r, wi_bd, b_i, lam, S, D)
    x1, hm0 = _matmul("lru_out", X(y), wv["lru_out"], S, D, D, outs=two(), epilogue=_ep_resid_norm, extras=[X(x)],
                      vecs=[mlp_g[0:1]], tm=norm_rows)
    z0, act0 = mlp_up(0, hm0)
    x2, h1 = hosted_mm("mlp0_down", X(act0), wv["w2_0"], S, D, F, outs=two(), epilogue=_ep_resid_norm, extras=[X(x1)],
                       vecs=[mix_g[1:2]], tm=norm_rows)
    (u1,) = _matmul("fox_in", wv["fox_in"], X(h1), NU, S, D, tb=True, outs=[_fresh(NU, S, F32)], epilogue=_ep_store)
    qat, kat, vat, ka, va, vt = _fox_prep(u1, b_f, qg, kg, S, D, tq)
    o, o32, lse = hosted("attn_forward", _attn_forward, ka, qat, vt, S, D, tq)
    x3, hm1 = _matmul("fox_out", X(o), wv["fox_out"], S, D, D, ta=True, outs=two(), epilogue=_ep_resid_norm,
                      extras=[X(x2)], vecs=[mlp_g[1:2]], tm=norm_rows)
    z1, act1 = mlp_up(1, hm1)
    (x4,) = _matmul("mlp1_down", X(act1), wv["w2_1"], S, D, F, outs=[_fresh(S, D, F32)], epilogue=_ep_resid,
                    extras=[X(x3)])
    loss, d4, d4b = _loss_head(x4, tgt, S, D)

    d3, d3b, dg_mlp1 = mlp_bwd(1, x3, hm1, z1, act1, d4, d4b)
    (do,) = _matmul("fox_dout", wv["fox_out"], X(d3b), D, S, D, tb=True, outs=[_fresh(D, S, BF16)], epilogue=_ep_store)
    (grads["fox_out"],) = _matmul("fox_dwout", X(o), X(d3b), D, D, S, outs=[gout("fox_out")], epilogue=_ep_store)
    doat, doa, qat1, qa1 = _fox_bwd_prep(do, o32, lse, qat, S, D, tq)
    dqn, dcq = hosted("attn_backward_q", _attn_backward_q, ka, va, kat, qat1, doat, S, D, tq)
    dkn, dv, dck = _attn_backward_kv(qa1, doa, qat1, doat, kat, vat, S, D, tq)
    du1, dbf, dqg, dkg = _fox_prep_bwd(u1, dqn, dkn, dv, dcq, dck, b_f, qg, kg, S, D, tq)
    (grads["fox_in"],) = _matmul("fox_dwin", X(h1), X(du1), D, NU, S, ta=True, tb=True, outs=[gout("fox_in")],
                                 epilogue=_ep_store)
    d2, d2b, dg_mix1 = _matmul("fox_dh", X(du1), wv["fox_in"], S, D, NU, ta=True, outs=two(), n_sums=1,
                               epilogue=_ep_norm_bwd, extras=[X(x2), X(d3)], vecs=[mix_g[1:2]], tm=norm_rows)
    d1, d1b, dg_mlp0 = mlp_bwd(0, x1, hm0, z0, act0, d2, d2b)
    (dy,) = _matmul("lru_dout", X(d1b), wv["lru_out"], S, D, D, tb=True, outs=[_fresh(S, D, F32)], epilogue=_ep_store)
    (grads["lru_out"],) = _matmul("lru_dwout", X(y), X(d1b), D, D, S, ta=True, outs=[gout("lru_out")],
                                  epilogue=_ep_store)
    du0, dcw, dcb, dlam, dbr, dbi, dwr, dwi = hosted("lru_bwd", _lru_bwd, dy, u0, xc, r, ig, hs, conv_w, wr_bd, wi_bd,
                                                     lam, S, D)
    (grads["lru_in"],) = _matmul("lru_dwin", X(h0), X(du0), D, 2 * D, S, ta=True, outs=[gout("lru_in")],
                                 epilogue=_ep_store)
    gx, dg_mix0 = _matmul("lru_dh", X(du0), wv["lru_in"], S, D, 2 * D, tb=True, outs=[_fresh(S, D, F32)], n_sums=1,
                          epilogue=lambda *a: _ep_norm_bwd(*a)[::2], extras=[X(x), X(d1)], vecs=[mix_g[0:1]],
                          tm=norm_rows)

    grads.update(
        mix_norm=jnp.concatenate([dg_mix0, dg_mix1], axis=0), mlp_norm=jnp.concatenate([dg_mlp0, dg_mlp1], axis=0),
        conv_w=dcw, lru_conv_b=dcb, lru_w_r=_block_diag_extract(dwr, nblk)[None], lru_b_r=dbr.reshape(1, nblk, -1),
        lru_w_i=_block_diag_extract(dwi, nblk)[None], lru_b_i=dbi.reshape(1, nblk, -1), lru_lambda=dlam,
        fox_b_f=dbf[:H].reshape(1, H), fox_q_gain=dqg.reshape(1, -1), fox_k_gain=dkg.reshape(1, -1))
    return loss, gx, grads


def _place():
    x, y, c = lax.axis_index("x"), lax.axis_index("y"), lax.axis_index("c")
    chips = [(1 - x, y), (x, 1 - y), (1 - x, 1 - y)]
    return x, y, c, 2 * x + y, chips


BOUNCE_BYTES = 1 << 20


def _bounce_shape(rows, cols, dtype):
    chunk = rows
    while chunk % 2 == 0 and chunk > 16 and chunk * cols * jnp.dtype(dtype).itemsize > BOUNCE_BYTES:
        chunk //= 2
    return pltpu.VMEM((2, chunk, cols), dtype)


def _bounce_copy(src, dst, buf, sem):
    chunk = buf.shape[1]
    n = src.shape[0] // chunk
    cin = lambda i: pltpu.make_async_copy(src.at[pl.ds(i * chunk, chunk)], buf.at[i % 2], sem.at[i % 2])
    cout = lambda i: pltpu.make_async_copy(buf.at[i % 2], dst.at[pl.ds(i * chunk, chunk)], sem.at[2 + i % 2])
    cin(0).start()
    for i in range(n):
        cin(i).wait()
        if i + 1 < n:
            if i >= 1:
                cout(i - 1).wait()
            cin(i + 1).start()
        cout(i).start()
    if n >= 2:
        cout(n - 2).wait()
    cout(n - 1).wait()


def _hbm_call(body, name, arrays, out_shape, n_dma_sems, bounce=()):
    scratch = [pltpu.SemaphoreType.DMA((k,)) for k in n_dma_sems]
    for rows, cols, dtype in bounce:
        scratch += [_bounce_shape(rows, cols, dtype), pltpu.SemaphoreType.DMA((4,))]
    return pl.pallas_call(
        body, name=name, in_specs=[ANY] * len(arrays), out_specs=[ANY] * len(out_shape), out_shape=out_shape,
        scratch_shapes=scratch,
        compiler_params=pltpu.CompilerParams(has_side_effects=True, vmem_limit_bytes=VMEM_LIMIT),
    )(*arrays)


class _Gather:
    def __init__(self, shards):
        n = self.n = len(shards)
        self.operands = list(shards)
        self.out_shape = [jax.ShapeDtypeStruct((N_CHIPS,) + tuple(a.shape), a.dtype) for a in shards]
        self.scratch = [pltpu.SemaphoreType.DMA((3 * n,)) for _ in range(4)]
        for a in shards:
            self.scratch += [_bounce_shape(a.shape[0], a.shape[1], a.dtype), pltpu.SemaphoreType.DMA((4,))]

    def _copies(self, ins, outs, scr):
        send, recv, fsend, frecv = scr[:4]
        x, y, c, s, chips = _place()

        def rows(a, chip_idx, which):
            hr = ins[a].shape[0] // 2
            return outs[a].at[chip_idx, pl.ds(which * hr, hr)]

        def landed(a, j, core):
            return rows(a, 2 * chips[j][0] + chips[j][1], core)

        def ici(a, j, mine):
            hr = ins[a].shape[0] // 2
            src, dst = (ins[a].at[pl.ds(c * hr, hr)], rows(a, s, c)) if mine else (landed(a, j, c),) * 2
            return pltpu.make_async_remote_copy(src_ref=src, dst_ref=dst, send_sem=send.at[3 * a + j],
                                                recv_sem=recv.at[3 * a + j], device_id=(*chips[j], c),
                                                device_id_type=MESH)

        def d2d(a, j, mine):
            ref = landed(a, j, c if mine else 1 - c)
            return pltpu.make_async_remote_copy(src_ref=ref, dst_ref=ref, send_sem=fsend.at[3 * a + j],
                                                recv_sem=frecv.at[3 * a + j], device_id=(x, y, 1 - c),
                                                device_id_type=MESH)

        return ici, d2d, s

    def start(self, ins, outs, scr):
        ici, _, _ = self._copies(ins, outs, scr)
        for a in range(self.n):
            for j in range(3):
                ici(a, j, True).start()

    def middle(self, ins, outs, scr):
        ici, d2d, s = self._copies(ins, outs, scr)
        for a in range(self.n):
            _bounce_copy(ins[a], outs[a].at[s], scr[4 + 2 * a], scr[5 + 2 * a])
        for a in range(self.n):
            for j in range(3):
                ici(a, j, False).wait_recv()
                d2d(a, j, True).start()

    def finish(self, ins, outs, scr):
        ici, d2d, _ = self._copies(ins, outs, scr)
        for a in range(self.n):
            for j in range(3):
                d2d(a, j, False).wait_recv()
        for a in range(self.n):
            for j in range(3):
                ici(a, j, True).wait_send()
                d2d(a, j, True).wait_send()


def _run_plan(name, plan):
    k_in, k_out = len(plan.operands), len(plan.out_shape)

    def body(*refs):
        parts = (refs[:k_in], refs[k_in:k_in + k_out], refs[k_in + k_out:])
        plan.start(*parts)
        plan.middle(*parts)
        plan.finish(*parts)

    return pl.pallas_call(
        body, name=name, in_specs=[ANY] * k_in, out_specs=[ANY] * k_out, out_shape=plan.out_shape,
        scratch_shapes=plan.scratch,
        compiler_params=pltpu.CompilerParams(has_side_effects=True, vmem_limit_bytes=VMEM_LIMIT),
    )(*plan.operands)


def _hosted_call(body, name, grid, in_specs, out_specs, out_shape, scratch_shapes, operands, sem, plan=None):
    if plan is None:
        res = pl.pallas_call(body, name=name, grid=grid, in_specs=in_specs, out_specs=out_specs, out_shape=out_shape,
                             scratch_shapes=scratch_shapes, compiler_params=_params(sem))(*operands)
        return res, None
    n_in, n_out, n_scr = len(in_specs), len(out_specs), len(scratch_shapes)
    k_in, k_out = len(plan.operands), len(plan.out_shape)
    total = int(np.prod(grid))

    def hosted(*refs):
        ins, refs = refs[:n_in], refs[n_in:]
        p_ins, refs = refs[:k_in], refs[k_in:]
        outs, refs = refs[:n_out], refs[n_out:]
        p_outs, refs = refs[:k_out], refs[k_out:]
        scr, p_scr = refs[:n_scr], refs[n_scr:]
        step = pl.program_id(0)
        for d in range(1, len(grid)):
            step = step * grid[d] + pl.program_id(d)
        pl.when(step == 0)(lambda: plan.start(p_ins, p_outs, p_scr))
        body(*ins, *outs, *scr)
        pl.when(step == total // 2)(lambda: plan.middle(p_ins, p_outs, p_scr))
        pl.when(step == total - 1)(lambda: plan.finish(p_ins, p_outs, p_scr))

    res = pl.pallas_call(
        hosted, name=name, grid=grid, in_specs=list(in_specs) + [ANY] * k_in, out_specs=list(out_specs) + [ANY] * k_out,
        out_shape=list(out_shape) + plan.out_shape, scratch_shapes=list(scratch_shapes) + plan.scratch,
        compiler_params=pltpu.CompilerParams(dimension_semantics=sem, vmem_limit_bytes=VMEM_LIMIT,
                                             has_side_effects=True),
    )(*operands, *plan.operands)
    return res[:n_out], res[n_out:]


def _all_gather(name, shards):
    return _run_plan(name, _Gather(shards))


def _pair_swap(name, arrs):
    n = len(arrs)

    def body(*refs):
        ins, outs = refs[:n], refs[n:2 * n]
        send, recv = refs[2 * n:]
        x, y, c, _, _ = _place()
        cps = []
        for a in range(n):
            hr = ins[a].shape[1] // 2
            cp = pltpu.make_async_remote_copy(
                src_ref=ins[a].at[:, pl.ds((1 - c) * hr, hr)], dst_ref=outs[a], send_sem=send.at[a],
                recv_sem=recv.at[a], device_id=(x, y, 1 - c), device_id_type=MESH)
            cp.start()
            cps.append(cp)
        for cp in cps:
            cp.wait()

    out_shape = [jax.ShapeDtypeStruct((a.shape[0], a.shape[1] // 2, a.shape[2]), a.dtype) for a in arrs]
    return _hbm_call(body, name, arrs, out_shape, (n, n))


class _Scatter:
    def __init__(self, parts):
        n = self.n = len(parts)
        self.operands = list(parts)
        self.out_shape = [jax.ShapeDtypeStruct(a.shape, a.dtype) for a in parts]
        self.scratch = [pltpu.SemaphoreType.DMA((3 * n,)) for _ in range(2)]
        for a in parts:
            self.scratch += [_bounce_shape(a.shape[1], a.shape[2], a.dtype), pltpu.SemaphoreType.DMA((4,))]

    def _copy(self, ins, outs, scr, a, j, mine):
        x, y, c, s, chips = _place()
        t = 2 * chips[j][0] + chips[j][1]
        return pltpu.make_async_remote_copy(
            src_ref=ins[a].at[t], dst_ref=outs[a].at[s if mine else t], send_sem=scr[0].at[3 * a + j],
            recv_sem=scr[1].at[3 * a + j], device_id=(*chips[j], c), device_id_type=MESH)

    def start(self, ins, outs, scr):
        for a in range(self.n):
            for j in range(3):
                self._copy(ins, outs, scr, a, j, True).start()

    def middle(self, ins, outs, scr):
        s = _place()[3]
        for a in range(self.n):
            _bounce_copy(ins[a].at[s], outs[a].at[s], scr[2 + 2 * a], scr[3 + 2 * a])

    def finish(self, ins, outs, scr):
        for a in range(self.n):
            for j in range(3):
                self._copy(ins, outs, scr, a, j, False).wait_recv()
        for a in range(self.n):
            for j in range(3):
                self._copy(ins, outs, scr, a, j, True).wait_send()


def _pair_gather(name, halves):
    n = len(halves)

    def body(*refs):
        ins, outs = refs[:n], refs[n:2 * n]
        send, recv = refs[2 * n:2 * n + 2]
        stage = refs[2 * n + 2:]
        x, y, c, _, _ = _place()
        cps = []
        for a in range(n):
            hr = ins[a].shape[0]
            cp = pltpu.make_async_remote_copy(
                src_ref=ins[a], dst_ref=outs[a].at[pl.ds(c * hr, hr)], send_sem=send.at[a], recv_sem=recv.at[a],
                device_id=(x, y, 1 - c), device_id_type=MESH)
            cp.start()
            cps.append((cp, hr))
        for a, (cp, hr) in enumerate(cps):
            _bounce_copy(ins[a], outs[a].at[pl.ds(c * hr, hr)], stage[2 * a], stage[2 * a + 1])
        for a, (cp, hr) in enumerate(cps):
            cp.wait_send()
            theirs = outs[a].at[pl.ds((1 - c) * hr, hr)]
            pltpu.make_async_remote_copy(src_ref=theirs, dst_ref=theirs, send_sem=send.at[a], recv_sem=recv.at[a],
                                         device_id=(x, y, 1 - c), device_id_type=MESH).wait_recv()

    out_shape = [jax.ShapeDtypeStruct((2 * a.shape[0], a.shape[1]), a.dtype) for a in halves]
    return _hbm_call(body, name, halves, out_shape, (n, n),
                     bounce=[(a.shape[0], a.shape[1], a.dtype) for a in halves])


def _row_tile(rows, cols, itemsize, n_bufs):
    budget = VMEM_LIMIT // 2
    for t in (1024, 512, 256, 128, 64, 32, 16):
        if rows % t == 0 and 2 * n_bufs * t * cols * itemsize <= budget:
            return t
    return rows


def _pair_add(name, g, gsib, core, out_dtype):
    _, r, cols = g.shape
    hr = r // 2
    t = _row_tile(hr, cols, 4, 3)
    per = hr // t

    def body(core_ref, a_ref, b_ref, o_ref):
        o_ref[...] = (a_ref[...].astype(F32) + b_ref[...].astype(F32)).astype(o_ref.dtype)

    grid_spec = pltpu.PrefetchScalarGridSpec(
        num_scalar_prefetch=1, grid=(N_CHIPS, per),
        in_specs=[pl.BlockSpec((None, t, cols), lambda s, i, core: (s, core[0] * per + i, 0)),
                  pl.BlockSpec((None, t, cols), lambda s, i, core: (s, i, 0))],
        out_specs=pl.BlockSpec((None, t, cols), lambda s, i, core: (s, i, 0)))
    return pl.pallas_call(body, name=name, grid_spec=grid_spec,
                          out_shape=jax.ShapeDtypeStruct((N_CHIPS, hr, cols), out_dtype),
                          compiler_params=_params(("arbitrary", "arbitrary")))(core, g, gsib)


def _chip_sum(name, parts):
    _, hr, cols = parts.shape
    t = _row_tile(hr, cols, 4, 5)

    def body(p_ref, o_ref):
        o_ref[...] = ((p_ref[0].astype(F32) + p_ref[1].astype(F32)) + p_ref[2].astype(F32)) + p_ref[3].astype(F32)

    return pl.pallas_call(
        body, name=name, grid=(hr // t,), in_specs=[pl.BlockSpec((N_CHIPS, t, cols), lambda i: (0, i, 0))],
        out_specs=pl.BlockSpec((t, cols), lambda i: (i, 0)), out_shape=jax.ShapeDtypeStruct((hr, cols), F32),
        compiler_params=_params(("arbitrary",)))(parts)


def _pair_partials(tag, arrs, wire_dtypes, core):
    sib = _pair_swap(f"{tag}_pair_swap", arrs)
    return _Scatter([_pair_add(f"{tag}_pair_add{i}", g, gs, core, dt)
                     for i, (g, gs, dt) in enumerate(zip(arrs, sib, wire_dtypes))])


def _finish_reduce(tag, scattered):
    halves = [_chip_sum(f"{tag}_chip_sum{i}", p) for i, p in enumerate(scattered)]
    return _pair_gather(f"{tag}_pair_gather", halves)


def _adamw(name, w, g_parts, m, v):
    rows, cols = w.shape
    n_parts = len(g_parts)
    part_rows = rows // n_parts
    t = _row_tile(part_rows, cols, 4, 7 + n_parts)
    per = part_rows // t
    c1 = 1.0 - ADAM_B1 ** ADAM_STEP
    c2 = 1.0 - ADAM_B2 ** ADAM_STEP

    def body(w_ref, m_ref, v_ref, *refs):
        g_refs, (go_ref, d_ref, nm_ref, nv_ref) = refs[:n_parts], refs[n_parts:]
        g = g_refs[0][...]
        for k in range(1, n_parts):
            g = jnp.where(pl.program_id(0) >= k * per, g_refs[k][...], g)
        go_ref[...] = g
        m = ADAM_B1 * m_ref[...] + (1.0 - ADAM_B1) * g
        v = ADAM_B2 * v_ref[...] + (1.0 - ADAM_B2) * (g * g)
        nm_ref[...] = m
        nv_ref[...] = v
        d_ref[...] = -ADAM_LR * ((m / c1) / (jnp.sqrt(v / c2) + ADAM_EPS) + ADAM_WD * w_ref[...])

    spec = pl.BlockSpec((t, cols), lambda i: (i, 0))
    g_specs = [pl.BlockSpec((t, cols), lambda i, k=k: (jnp.clip(i - k * per, 0, per - 1), 0)) for k in range(n_parts)]
    shp = jax.ShapeDtypeStruct((rows, cols), F32)
    return pl.pallas_call(body, name=name, grid=(rows // t,), in_specs=[spec] * 3 + g_specs, out_specs=[spec] * 4,
                          out_shape=[shp] * 4, compiler_params=_params(("arbitrary",)))(w, m, v, *g_parts)


_WEIGHTS = ["mix_norm", "mlp_norm", "mlp_w1", "mlp_w2", "lru_w_in", "lru_conv_w", "lru_conv_b", "lru_w_r", "lru_b_r",
            "lru_w_i", "lru_b_i", "lru_lambda", "lru_w_out", "fox_w_in", "fox_b_f", "fox_q_gain", "fox_k_gain",
            "fox_w_out"]
_REPLICATED = ["mix_norm", "mlp_norm", "lru_conv_b", "lru_w_r", "lru_b_r", "lru_w_i", "lru_b_i", "lru_lambda",
               "fox_b_f", "fox_q_gain", "fox_k_gain"]
_PACK_TILE = 2 * SUBLANES * LANES


def _as2d(a):
    return a.reshape(-1, a.shape[-1])


def kernel(x, mix_norm, mlp_norm, mlp_w1, mlp_w2, lru_w_in, lru_conv_w, lru_conv_b, lru_w_r, lru_b_r, lru_w_i, lru_b_i, lru_lambda, lru_w_out, fox_w_in, fox_b_f, fox_q_gain, fox_k_gain, fox_w_out, loss_target, m_mix_norm, m_mlp_norm, m_mlp_w1, m_mlp_w2, m_lru_w_in, m_lru_conv_w, m_lru_conv_b, m_lru_w_r, m_lru_b_r, m_lru_w_i, m_lru_b_i, m_lru_lambda, m_lru_w_out, m_fox_w_in, m_fox_b_f, m_fox_q_gain, m_fox_k_gain, m_fox_w_out, v_mix_norm, v_mlp_norm, v_mlp_w1, v_mlp_w2, v_lru_w_in, v_lru_conv_w, v_lru_conv_b, v_lru_w_r, v_lru_b_r, v_lru_w_i, v_lru_b_i, v_lru_lambda, v_lru_w_out, v_fox_w_in, v_fox_b_f, v_fox_q_gain, v_fox_k_gain, v_fox_w_out):
    args = dict(locals())
    W = {n: args[n] for n in _WEIGHTS}
    Mo = {n: args["m_" + n] for n in _WEIGHTS}
    Vo = {n: args["v_" + n] for n in _WEIGHTS}
    S, D = x.shape[1], x.shape[2]
    F = 4 * D
    H = D // HEAD_DIM
    NU = 3 * D + LANES
    FQ, DQ = F // N_CHIPS, D // N_CHIPS
    nfox = fox_w_in.shape[-1]
    chip = 2 * lax.axis_index("x") + lax.axis_index("y")
    core = lax.axis_index("c").astype(jnp.int32).reshape(1)

    cw_flat = jnp.pad(lru_conv_w.reshape(-1), (0, _PACK_TILE - CONV_WIDTH * DQ)).reshape(2 * SUBLANES, LANES)
    w1s, w2s = mlp_w1.astype(BF16), mlp_w2.astype(BF16)
    g_lin, g_lout, g_cw = _all_gather("gather_lru", [lru_w_in[0].astype(BF16), lru_w_out[0].astype(BF16), cw_flat])
    conv_w_full = jnp.transpose(g_cw.reshape(N_CHIPS, -1)[:, :CONV_WIDTH * DQ].reshape(N_CHIPS, CONV_WIDTH, DQ),
                                (1, 0, 2)).reshape(CONV_WIDTH, D)
    wv = {"lru_in": _View(g_lin, "cs"), "lru_out": _View(g_lout, "rs")}
    scattered = {}
    riding = {"attn_backward_q": ["w2_1", "w1_1", "fox_out"], "mlp0_dact": ["fox_in"],
              "lru_bwd": ["w2_0", "w1_0", "lru_out"]}

    def shard_major(name, g):
        if name == "fox_in":
            return jnp.transpose(g[:, :nfox * N_CHIPS].reshape(D, N_CHIPS, nfox), (1, 0, 2))
        return g

    class Comm:
        @staticmethod
        def before(name, grads):
            if name == "lru_fwd":
                return _Gather([w1s[0]])
            if name == "mlp0_up":
                return _Gather([w2s[0]])
            if name == "mlp0_down":
                return _Gather([fox_w_in[0].astype(BF16)])
            if name == "attn_forward":
                return _Gather([fox_w_out[0].astype(BF16), w1s[1], w2s[1]])
            if name in riding:
                arrs = [shard_major(n, grads[n]) for n in riding[name]]
                return _pair_partials(f"{name}_grads", arrs, [BF16] * len(arrs), core)
            return None

        @staticmethod
        def after(name, res, wv):
            if name == "lru_fwd":
                wv.update(w1_0=_View(res[0], "cs"))
            elif name == "mlp0_up":
                wv.update(w2_0=_View(res[0], "rs"))
            elif name == "mlp0_down":
                fox_full = jnp.concatenate([res[0][s] for s in range(N_CHIPS)], axis=1)
                fox_full = jnp.pad(fox_full, ((0, 0), (0, NU - fox_full.shape[1])))
                wv.update(fox_in=_View(fox_full.T))
            elif name == "attn_forward":
                wv.update(fox_out=_View(res[0], "rs"), w1_1=_View(res[1], "cs"), w2_1=_View(res[2], "rs"))
            else:
                scattered.update(zip(riding[name], res))

    def grad_view(grads, name):
        if name in ("w1_0", "w1_1"):
            return _View(None, "cs", shape=(N_CHIPS, D, FQ), dtype=BF16)
        if name in ("w2_0", "w2_1"):
            return _View(None, "rs", shape=(N_CHIPS, FQ, D), dtype=BF16)
        if name == "lru_in":
            return _View(None, "cs", shape=(N_CHIPS, D, 2 * D // N_CHIPS), dtype=BF16)
        if name in ("lru_out", "fox_out"):
            return _View(None, "rs", shape=(N_CHIPS, DQ, D), dtype=BF16)
        return _View(None, shape=(D, NU), dtype=BF16)

    small = {n: W[n] for n in _REPLICATED}
    small["conv_w"] = conv_w_full

    loss, gx, grads = _local_step(x[0], loss_target[0], small, wv, grad_view, Comm)

    pack_names = _REPLICATED + ["conv_w"]
    flat = jnp.concatenate([grads[n].reshape(-1).astype(F32) for n in pack_names] + [loss.reshape(-1)])
    per_chip = -(-flat.shape[0] // (N_CHIPS * _PACK_TILE)) * _PACK_TILE
    pack = jnp.pad(flat, (0, N_CHIPS * per_chip - flat.shape[0])).reshape(N_CHIPS, per_chip // LANES, LANES)
    tail = _run_plan("tail_grads_chip_scatter", _pair_partials("tail_grads", [grads["lru_in"], pack], [BF16, F32], core))
    scattered.update(lru_in=tail[0], pack=tail[1])
    order = ["w1_0", "w1_1", "w2_0", "w2_1", "lru_in", "lru_out", "fox_in", "fox_out", "pack"]
    red = dict(zip(order, _finish_reduce("grads", [scattered[n] for n in order])))
    (all_pack,) = _all_gather("gather_small_grads", [red["pack"]])
    all_flat = all_pack.reshape(-1)
    G = {}
    off = 0
    for n in pack_names:
        shape = grads[n].shape if n == "conv_w" else W[n].shape
        size = int(np.prod(shape))
        G[n] = all_flat[off:off + size].reshape(shape)
        off += size
    total = all_flat[off]
    G["lru_conv_w"] = lax.dynamic_slice_in_dim(G.pop("conv_w"), chip * DQ, DQ, axis=1)[None]
    parts = {n: [_as2d(G[n])] for n in G}
    parts.update(mlp_w1=[red["w1_0"], red["w1_1"]], mlp_w2=[red["w2_0"], red["w2_1"]], lru_w_in=[red["lru_in"]],
                 lru_w_out=[red["lru_out"]], fox_w_in=[red["fox_in"]], fox_w_out=[red["fox_out"]])

    delta, new_m, new_v = {}, {}, {}
    for n in _WEIGHTS:
        go, d, nm, nv = _adamw(f"adamw_{n}", _as2d(W[n]), parts[n], _as2d(Mo[n]), _as2d(Vo[n]))
        G[n], delta[n], new_m[n], new_v[n] = (t.reshape(W[n].shape) for t in (go, d, nm, nv))

    return (total, gx[None], *[G[n] for n in _WEIGHTS], *[delta[n] for n in _WEIGHTS],
            *[new_m[n] for n in _WEIGHTS], *[new_v[n] for n in _WEIGHTS])
```

```python
import functools

import numpy as np
import jax
import jax.numpy as jnp
from jax import lax
from jax.experimental import pallas as pl
from jax.experimental.pallas import tpu as pltpu

F32 = jnp.float32
BF16 = jnp.bfloat16

HEAD_DIM = 64
LRU_BLOCK_DIM = 64
CONV_WIDTH = 4
LRU_C = 8.0
EPS = 1e-6
NEG_INF = -1e30
ADAM_LR = 0.001
ADAM_B1 = 0.9
ADAM_B2 = 0.999
ADAM_EPS = 1e-08
ADAM_WD = 0.01
ADAM_STEP = 10

N_CHIPS = 4
LANES = 128
SUBLANES = 8
MXU_DIM = 256
VMEM_LIMIT = 52 * 1024 * 1024
MESH = pl.DeviceIdType.MESH
ANY = pl.BlockSpec(memory_space=pl.ANY)


def _pick(n, prefs):
    for p in prefs:
        if p <= n and n % p == 0:
            return p
    return n


def _params(sem=None):
    return pltpu.CompilerParams(dimension_semantics=sem, vmem_limit_bytes=VMEM_LIMIT)


class _View:
    def __init__(self, arr, kind="plain", r0=0, rows=None, shape=None, dtype=None):
        self.arr = arr
        self.kind = kind
        self.r0 = r0
        self.shape = tuple(arr.shape) if arr is not None else tuple(shape)
        self.dtype = arr.dtype if arr is not None else dtype
        self.rows = rows if rows is not None else self.shape[-2]

    def limits(self):
        if self.kind == "plain":
            return 0, 0
        rows = int(np.gcd(self.rows, self.r0))
        return rows, (self.shape[-1] if self.kind == "cs" else 0)

    def spec(self, br, bc, fr, fc):
        if self.kind == "plain":
            return pl.BlockSpec((br, bc), lambda *g: (fr(*g), fc(*g)))
        ncol = self.shape[-1]
        r0b = self.r0 // br
        assert self.r0 % br == 0 and self.rows % br == 0 and ncol % bc == 0, (self.shape, self.r0, br, bc)
        if self.kind == "cs":
            per = ncol // bc
            return pl.BlockSpec((None, br, bc), lambda *g: (fc(*g) // per, r0b + fr(*g), fc(*g) % per))
        per = self.rows // br
        return pl.BlockSpec((None, br, bc), lambda *g: (fr(*g) // per, r0b + fr(*g) % per, fc(*g)))


def _bf(x):
    return x if x.dtype == BF16 else x.astype(BF16)


def _matmul(name, A, B, M, N, K, *, ta=False, tb=False, outs, epilogue, extras=(), vecs=(), n_sums=0,
            tm=None, tn=None, tk=None, plan=None):
    lim = {"m": [M], "n": [N], "k": [K]}
    for view, (rdim, cdim) in ([(A, "km" if ta else "mk"), (B, "nk" if tb else "kn")]
                               + [(e, "mn") for e in extras] + [(o, "mn") for o in outs]):
        r_lim, c_lim = view.limits()
        lim[rdim].append(r_lim)
        lim[cdim].append(c_lim)
    tm = tm or _pick(int(np.gcd.reduce(lim["m"])), (1024, 640, 512, 256, 128))
    tn = tn or _pick(int(np.gcd.reduce(lim["n"])), (1024, 640, 512, 256, 128))
    tk = tk or _pick(int(np.gcd.reduce(lim["k"])), (1024, 640, 512, 256, 128))
    nk = K // tk
    gi = lambda i, j, k: i
    gj = lambda i, j, k: j
    gk = lambda i, j, k: k
    a_spec = A.spec(tk, tm, gk, gi) if ta else A.spec(tm, tk, gi, gk)
    b_spec = B.spec(tn, tk, gj, gk) if tb else B.spec(tk, tn, gk, gj)
    ca = 0 if ta else 1
    cb = 1 if tb else 0
    ne, no = len(extras) + len(vecs), len(outs)
    assert n_sums == 0 or tn == N
    row_spec = pl.BlockSpec((1, tn), lambda i, j, k: (0, j))
    in_specs = [a_spec, b_spec] + [e.spec(tm, tn, gi, gj) for e in extras] + [row_spec] * len(vecs)
    operands = [A.arr, B.arr] + [e.arr for e in extras] + list(vecs)
    out_specs = [o.spec(tm, tn, gi, gj) for o in outs] + [row_spec] * n_sums
    out_shape = ([jax.ShapeDtypeStruct(o.shape, o.dtype) for o in outs]
                 + [jax.ShapeDtypeStruct((1, N), F32)] * n_sums)

    def body(*refs):
        a_ref, b_ref = refs[0], refs[1]
        ex = refs[2:2 + ne]
        o_refs = refs[2 + ne:2 + ne + no]
        s_refs = refs[2 + ne + no:2 + ne + no + n_sums]
        first_row_tile = pl.program_id(0) == 0

        def prod():
            return lax.dot_general(_bf(a_ref[...]), _bf(b_ref[...]), (((ca,), (cb,)), ((), ())),
                                   preferred_element_type=F32)

        def finish(acc):
            res = epilogue(acc, *[e[...] for e in ex])
            for o_ref, r in zip(o_refs, res[:no]):
                o_ref[...] = r.astype(o_ref.dtype)
            for s_ref, r in zip(s_refs, res[no:]):
                def assign(s_ref=s_ref, r=r):
                    s_ref[...] = r

                def accumulate(s_ref=s_ref, r=r):
                    s_ref[...] += r

                pl.when(first_row_tile)(assign)
                pl.when(jnp.logical_not(first_row_tile))(accumulate)

        if nk == 1:
            finish(prod())
        else:
            acc_ref = refs[-1]
            k = pl.program_id(2)

            @pl.when(k == 0)
            def _():
                acc_ref[...] = jnp.zeros_like(acc_ref)

            acc_ref[...] += prod()

            @pl.when(k == nk - 1)
            def _():
                finish(acc_ref[...])

    res, side = _hosted_call(body, name, (M // tm, N // tn, nk), in_specs, out_specs, out_shape,
                             [pltpu.VMEM((tm, tn), F32)] if nk > 1 else [], operands,
                             ("arbitrary", "arbitrary", "arbitrary"), plan)
    return res if plan is None else (res, side)


def _ep_store(acc):
    return (acc,)


def _ep_resid(acc, res):
    return (res + acc,)


def _ep_resid_norm(acc, res, g):
    xo = res + acc
    r = lax.rsqrt(jnp.mean(xo * xo, axis=-1, keepdims=True) + EPS)
    return (xo, (xo * r) * g)


def _ep_norm_bwd(acc, x, dres, g):
    r = lax.rsqrt(jnp.mean(x * x, axis=-1, keepdims=True) + EPS)
    xhat = x * r
    dxn = acc * g
    tot = dres + r * (dxn - xhat * jnp.mean(dxn * xhat, axis=-1, keepdims=True))
    return (tot, tot, jnp.sum(acc * xhat, axis=0, keepdims=True))


def _ep_relu2(acc):
    zp = jnp.maximum(acc, 0.0)
    return (acc, zp * zp)


def _ep_drelu2(acc, z):
    return (acc * (2.0 * jnp.maximum(z.astype(F32), 0.0)),)


def _fresh(M, N, dtype):
    return _View(None, shape=(M, N), dtype=dtype)


def _rms_fwd(name, x, g, S, D):
    T = _pick(S, (512, 256, 128))

    def body(x_ref, g_ref, h_ref):
        x = x_ref[...]
        r = lax.rsqrt(jnp.mean(x * x, axis=-1, keepdims=True) + EPS)
        h_ref[...] = ((x * r) * g_ref[...]).astype(BF16)

    return pl.pallas_call(
        body, name=name, grid=(S // T,),
        in_specs=[pl.BlockSpec((T, D), lambda i: (i, 0)), pl.BlockSpec((1, D), lambda i: (0, 0))],
        out_specs=pl.BlockSpec((T, D), lambda i: (i, 0)),
        out_shape=jax.ShapeDtypeStruct((S, D), BF16),
        compiler_params=_params(("arbitrary",)),
    )(x, g)


def _loss_head(x, tgt, S, D):
    T = _pick(S, (512, 256, 128))

    def body(x_ref, t_ref, loss_ref, d_ref, db_ref):
        @pl.when(pl.program_id(0) == 0)
        def _():
            loss_ref[...] = jnp.zeros_like(loss_ref)

        e = x_ref[...] - t_ref[...]
        loss_ref[...] += 0.5 * jnp.sum(jnp.mean(e * e, axis=-1, keepdims=True), axis=0, keepdims=True)
        d = e * (1.0 / D)
        d_ref[...] = d
        db_ref[...] = d.astype(BF16)

    row = pl.BlockSpec((T, D), lambda i: (i, 0))
    return pl.pallas_call(
        body, name="loss_head", grid=(S // T,), in_specs=[row, row],
        out_specs=[pl.BlockSpec((1, 1), lambda i: (0, 0)), row, row],
        out_shape=[jax.ShapeDtypeStruct((1, 1), F32), jax.ShapeDtypeStruct((S, D), F32),
                   jax.ShapeDtypeStruct((S, D), BF16)],
        compiler_params=_params(("arbitrary",)),
    )(x, tgt)


def _sigmoid(z):
    return 1.0 / (1.0 + jnp.exp(-z))


def _log_sigmoid(z):
    return jnp.minimum(z, 0.0) - jnp.log(1.0 + jnp.exp(-jnp.abs(z)))


_GELU_K = 0.7978845608028654
_GELU_C = 0.044715


def _gelu(x):
    t = jnp.tanh(_GELU_K * (x + _GELU_C * (x * x * x)))
    return 0.5 * x * (1.0 + t)


def _gelu_and_grad(x):
    x2 = x * x
    t = jnp.tanh(_GELU_K * (x + _GELU_C * (x2 * x)))
    g = 0.5 * x * (1.0 + t)
    dg = 0.5 * (1.0 + t) + 0.5 * x * (1.0 - t * t) * (_GELU_K * (1.0 + 3.0 * _GELU_C * x2))
    return g, dg


def _decay_terms(r, ls):
    la = LRU_C * r * ls
    a = jnp.exp(la)
    a2 = jnp.exp(2.0 * la)
    mult = jnp.sqrt(-jnp.tanh(la) * (a2 + 1.0))
    return a, a2, mult


def _lru_fwd(u0, conv_w, conv_b, wr_bd, b_r, wi_bd, b_i, lam, S, D, plan=None):
    T = _pick(S, (256, 128))
    GT = wr_bd.shape[-1]
    nG = D // GT

    def body(gb_ref, xb_ref, cw_ref, cb_ref, wr_ref, br_ref, wi_ref, bi_ref, lam_ref,
             y_ref, xc_ref, r_ref, i_ref, hs_ref, ext, a_scr, hcar):
        @pl.when(pl.program_id(0) == 0)
        def _():
            ext[0:SUBLANES, :] = jnp.zeros((SUBLANES, D), F32)
            hcar[...] = jnp.zeros_like(hcar)

        xb = xb_ref[...]
        ext[SUBLANES:SUBLANES + T, :] = xb
        xc = cb_ref[...]
        for k in range(CONV_WIDTH):
            xc = xc + ext[pl.ds(SUBLANES - (CONV_WIDTH - 1) + k, T), :] * cw_ref[k:k + 1, :]
        ext[0:SUBLANES, :] = xb[T - SUBLANES:T, :]
        xc_ref[...] = xc
        xcb = xc.astype(BF16)
        for g in range(nG):
            sl = slice(g * GT, (g + 1) * GT)
            zr = jnp.dot(xcb[:, sl], wr_ref[g], preferred_element_type=F32) + br_ref[:, sl]
            zi = jnp.dot(xcb[:, sl], wi_ref[g], preferred_element_type=F32) + bi_ref[:, sl]
            r_ref[:, sl] = _sigmoid(zr)
            i_ref[:, sl] = _sigmoid(zi)
        r = r_ref[...]
        a, _, mult = _decay_terms(r, _log_sigmoid(lam_ref[...]))
        a_scr[...] = a
        hs_ref[...] = mult * (i_ref[...] * xc)

        def step(t, h):
            h = a_scr[pl.ds(t, 1), :] * h + hs_ref[pl.ds(t, 1), :]
            hs_ref[pl.ds(t, 1), :] = h
            return h

        hcar[...] = lax.fori_loop(0, T, step, hcar[...], unroll=8)
        y_ref[...] = (_gelu(gb_ref[...]) * hs_ref[...]).astype(BF16)

    row = pl.BlockSpec((T, D), lambda i: (i, 0))
    vec = pl.BlockSpec((1, D), lambda i: (0, 0))
    bd = pl.BlockSpec((nG, GT, GT), lambda i: (0, 0, 0))
    f32o = jax.ShapeDtypeStruct((S, D), F32)
    return _hosted_call(
        body, "lru_fwd", (S // T,),
        [row, pl.BlockSpec((T, D), lambda i: (i, 1)), pl.BlockSpec((CONV_WIDTH, D), lambda i: (0, 0)), vec,
         bd, vec, bd, vec, vec],
        [row, row, row, row, row], [jax.ShapeDtypeStruct((S, D), BF16), f32o, f32o, f32o, f32o],
        [pltpu.VMEM((T + SUBLANES, D), F32), pltpu.VMEM((T, D), F32), pltpu.VMEM((1, D), F32)],
        (u0, u0, conv_w, conv_b, wr_bd, b_r, wi_bd, b_i, lam), ("arbitrary",), plan)


def _lru_bwd(dy, u0, xc, r, ig, hs, conv_w, wr_bd, wi_bd, lam, S, D, plan=None):
    T = _pick(S, (128,))
    nT = S // T
    GT = wr_bd.shape[-1]
    nG = D // GT
    W = CONV_WIDTH

    def body(dy_ref, gb_ref, xb_ref, xbp_ref, xc_ref, r_ref, i_ref, hs_ref, hsp_ref, cw_ref, wr_ref, wi_ref, lam_ref,
             du_ref, dcw_ref, dcb_ref, dlam_ref, dbr_ref, dbi_ref, dwr_ref, dwi_ref,
             a_scr, dh_scr, exth, extx, extd, dxc_scr, dz_scr, carry):
        step = pl.program_id(0)
        first_tile = step == nT - 1

        @pl.when(step == 0)
        def _():
            for ref in (dcw_ref, dcb_ref, dlam_ref, dbr_ref, dbi_ref, dwr_ref, dwi_ref, carry):
                ref[...] = jnp.zeros_like(ref)
            extd[T:T + SUBLANES, :] = jnp.zeros((SUBLANES, D), F32)

        hs = hs_ref[...]
        dy = dy_ref[...]
        g, dgelu = _gelu_and_grad(gb_ref[...])
        du_ref[:, 0:D] = (dy * hs * dgelu).astype(BF16)
        r = r_ref[...]
        lam = lam_ref[...]
        ls = _log_sigmoid(lam)
        a, a2, mult = _decay_terms(r, ls)
        a_scr[...] = a
        dh_scr[...] = dy * g

        def rstep(j, c):
            t = T - 1 - j
            d = dh_scr[pl.ds(t, 1), :] + c
            dh_scr[pl.ds(t, 1), :] = d
            return a_scr[pl.ds(t, 1), :] * d

        carry[...] = lax.fori_loop(0, T, rstep, carry[...], unroll=8)
        dh = dh_scr[...]
        keep = jnp.where(first_tile, 0.0, 1.0)
        exth[0:SUBLANES, :] = hsp_ref[...] * keep
        exth[SUBLANES:SUBLANES + T, :] = hs
        hprev = exth[pl.ds(SUBLANES - 1, T), :]
        xc = xc_ref[...]
        ig = i_ref[...]
        da = dh * hprev
        dmult = dh * (ig * xc)
        dla = da * a - dmult * (a2 / mult)
        dlam_ref[...] += jnp.sum(dla * r, axis=0, keepdims=True) * (LRU_C * _sigmoid(-lam))
        dzr = (dla * (LRU_C * ls)) * (r * (1.0 - r))
        dzi = (dh * (mult * xc)) * (ig * (1.0 - ig))
        dbr_ref[...] += jnp.sum(dzr, axis=0, keepdims=True)
        dbi_ref[...] += jnp.sum(dzi, axis=0, keepdims=True)
        dxc_scr[...] = dh * (mult * ig)
        xcb = xc.astype(BF16)
        dz_scr[0] = dzr.astype(BF16)
        dz_scr[1] = dzi.astype(BF16)
        nt_dims = (((1,), (1,)), ((), ()))
        tn_dims = (((0,), (0,)), ((), ()))
        for gq in range(nG):
            sl = slice(gq * GT, (gq + 1) * GT)
            zr_g = dz_scr[0, :, sl]
            zi_g = dz_scr[1, :, sl]
            dxc_scr[:, sl] += (lax.dot_general(zr_g, wr_ref[gq], nt_dims, preferred_element_type=F32)
                               + lax.dot_general(zi_g, wi_ref[gq], nt_dims, preferred_element_type=F32))
            dwr_ref[gq] += lax.dot_general(xcb[:, sl], zr_g, tn_dims, preferred_element_type=F32)
            dwi_ref[gq] += lax.dot_general(xcb[:, sl], zi_g, tn_dims, preferred_element_type=F32)
        dxc = dxc_scr[...]
        dcb_ref[...] += jnp.sum(dxc, axis=0, keepdims=True)
        extx[0:SUBLANES, :] = xbp_ref[...] * keep
        extx[SUBLANES:SUBLANES + T, :] = xb_ref[...]
        extd[0:T, :] = dxc
        dxb = jnp.zeros((T, D), F32)
        for k in range(W):
            dxb = dxb + extd[pl.ds(W - 1 - k, T), :] * cw_ref[k:k + 1, :]
            dcw_ref[k:k + 1, :] += jnp.sum(dxc * extx[pl.ds(SUBLANES - (W - 1) + k, T), :], axis=0, keepdims=True)
        extd[T:T + SUBLANES, :] = dxc[0:SUBLANES, :]
        du_ref[:, D:2 * D] = dxb.astype(BF16)

    rev = lambda i: nT - 1 - i
    tpb = T // SUBLANES
    prev8 = lambda i: jnp.maximum(rev(i) * tpb - 1, 0)
    row = pl.BlockSpec((T, D), lambda i: (rev(i), 0))
    vec = pl.BlockSpec((1, D), lambda i: (0, 0))
    bd = pl.BlockSpec((nG, GT, GT), lambda i: (0, 0, 0))
    vec_o = jax.ShapeDtypeStruct((1, D), F32)
    bd_o = jax.ShapeDtypeStruct((nG, GT, GT), F32)
    return _hosted_call(
        body, "lru_bwd", (nT,),
        [row, row, pl.BlockSpec((T, D), lambda i: (rev(i), 1)), pl.BlockSpec((SUBLANES, D), lambda i: (prev8(i), 1)),
         row, row, row, row, pl.BlockSpec((SUBLANES, D), lambda i: (prev8(i), 0)),
         pl.BlockSpec((W, D), lambda i: (0, 0)), bd, bd, vec],
        [pl.BlockSpec((T, 2 * D), lambda i: (rev(i), 0)), pl.BlockSpec((W, D), lambda i: (0, 0)),
         vec, vec, vec, vec, bd, bd],
        [jax.ShapeDtypeStruct((S, 2 * D), BF16), jax.ShapeDtypeStruct((W, D), F32), vec_o, vec_o, vec_o, vec_o, bd_o, bd_o],
        [pltpu.VMEM((T, D), F32), pltpu.VMEM((T, D), F32), pltpu.VMEM((T + SUBLANES, D), F32),
         pltpu.VMEM((T + SUBLANES, D), F32), pltpu.VMEM((T + SUBLANES, D), F32),
         pltpu.VMEM((T, D), F32), pltpu.VMEM((2, T, D), BF16), pltpu.VMEM((1, D), F32)],
        (dy, u0, u0, u0, xc, r, ig, hs, hs, conv_w, wr_bd, wi_bd, lam), ("arbitrary",), plan)


AUG_ROWS = 16
HEAD_ROWS = 128
LSE_ROW = HEAD_DIM + 6


def _split3(x):
    b1 = x.astype(BF16).astype(F32)
    r = x - b1
    b2 = r.astype(BF16).astype(F32)
    return b1, b2, r - b2


def _head_block(x, aug, T):
    row = lax.broadcasted_iota(jnp.int32, (AUG_ROWS, T), 0)
    blk = jnp.zeros((AUG_ROWS, T), F32)
    for i, e in enumerate(aug):
        blk = jnp.where(row == i, e, blk)
    return jnp.concatenate([x, blk, jnp.zeros((HEAD_ROWS - HEAD_DIM - AUG_ROWS, T), F32)], axis=0)


def _tri_matrix(lower):
    i = np.arange(LANES)
    m = (i[:, None] >= i[None, :]) if lower else (i[:, None] <= i[None, :])
    return jnp.asarray(m.astype(np.float32), BF16)


def _lane_cumsum(x, tri_ref, carry, reverse):
    n = x.shape[1] // LANES
    tri = tri_ref[...]
    out = [None] * n
    for j in (range(n - 1, -1, -1) if reverse else range(n)):
        cs = carry
        for part in _split3(x[:, j * LANES:(j + 1) * LANES]):
            cs = cs + jnp.dot(part.astype(BF16), tri, preferred_element_type=F32)
        out[j] = cs
        carry = cs[:, 0:1] if reverse else cs[:, LANES - 1:LANES]
    return jnp.concatenate(out, axis=1), carry


def _head_rows(h):
    return pl.ds(pl.multiple_of(h * HEAD_DIM, HEAD_DIM), HEAD_DIM)


def _fox_prep(ut, b_f, qg, kg, S, D, tq):
    H = D // HEAD_DIM
    T = min(tq, 256)
    per = tq // T
    scale = HEAD_DIM ** -0.5

    def body(q_ref, k_ref, v_ref, f_ref, bf_ref, qg_ref, kg_ref, tri_ref,
             qat_ref, kat_ref, vat_ref, ka_ref, vt_ref, c_scr, ccar):
        @pl.when(pl.program_id(0) == 0)
        def _():
            ccar[...] = jnp.zeros_like(ccar)

        c, carry = _lane_cumsum(_log_sigmoid(f_ref[...] + bf_ref[...]), tri_ref, ccar[...], False)
        c_scr[...] = c
        ccar[...] = carry

        def head(h, _):
            rows = _head_rows(h)
            c1, c2, c3 = _split3(c_scr[pl.ds(h, 1), :])

            def normed(src, gain, mul):
                x = src[rows, :]
                rs = lax.rsqrt(jnp.mean(x * x, axis=0, keepdims=True) + EPS)
                return ((x * rs) * gain[rows, :]) * mul

            qat_ref[h] = _head_block(normed(q_ref, qg_ref, scale), [c1, c2, c3, 1.0, 1.0, 1.0], T).astype(BF16)
            kb = _head_block(normed(k_ref, kg_ref, 1.0), [1.0, 1.0, 1.0, -c1, -c2, -c3, 1.0, 1.0, 1.0], T)
            kat_ref[h] = kb.astype(BF16)
            ka_ref[h] = kb.T.astype(BF16)
            v = v_ref[rows, :]
            vt_ref[h] = v.astype(BF16)
            vat_ref[h] = _head_block(v, [1.0, 1.0, 1.0], T).astype(BF16)
            return 0

        lax.fori_loop(0, H, head, 0)

    part = lambda j: pl.BlockSpec((D, T), lambda i: (j, i))
    colv = lambda n: pl.BlockSpec((n, 1), lambda i: (0, 0))
    tmaj = lambda r: pl.BlockSpec((H, None, r, T), lambda i: (0, i // per, 0, i % per))
    norm = pl.BlockSpec((H, T, HEAD_ROWS), lambda i: (0, i, 0))
    tshape = lambda r: jax.ShapeDtypeStruct((H, S // tq, r, tq), BF16)
    nshape = jax.ShapeDtypeStruct((H, S, HEAD_ROWS), BF16)
    return pl.pallas_call(
        body, name="fox_prep", grid=(S // T,),
        in_specs=[part(0), part(1), part(2), pl.BlockSpec((LANES, T), lambda i: (3 * D // LANES, i)),
                  colv(LANES), colv(D), colv(D), pl.BlockSpec((LANES, LANES), lambda i: (0, 0))],
        out_specs=[tmaj(HEAD_ROWS), tmaj(HEAD_ROWS), tmaj(HEAD_ROWS), norm, tmaj(HEAD_DIM)],
        out_shape=[tshape(HEAD_ROWS), tshape(HEAD_ROWS), tshape(HEAD_ROWS), nshape, tshape(HEAD_DIM)],
        scratch_shapes=[pltpu.VMEM((LANES, T), F32), pltpu.VMEM((LANES, 1), F32)],
        compiler_params=_params(("arbitrary",)),
    )(ut, ut, ut, ut, b_f, qg, kg, _tri_matrix(False))


def _fox_bwd_prep(dot, ot, lse, qat, S, D, tq):
    H = D // HEAD_DIM
    T = min(tq, 256)
    per = tq // T

    def body(do_ref, o_ref, lse_ref, qat_ref, doat_ref, doa_ref, qat1_ref, qa1_ref):
        row = lax.broadcasted_iota(jnp.int32, (HEAD_ROWS, T), 0)

        def head(h, _):
            rows = _head_rows(h)
            do = do_ref[rows, :].astype(F32)
            delta = jnp.sum(do * o_ref[rows, :], axis=0, keepdims=True)
            db = _head_block(do, list(_split3(-delta)), T)
            doat_ref[h] = db.astype(BF16)
            doa_ref[h] = db.T.astype(BF16)
            qb = qat_ref[h].astype(F32)
            for i, e in enumerate(_split3(-lse_ref[h])):
                qb = jnp.where(row == LSE_ROW + i, e, qb)
            qat1_ref[h] = qb.astype(BF16)
            qa1_ref[h] = qb.T.astype(BF16)
            return 0

        lax.fori_loop(0, H, head, 0)

    chan = pl.BlockSpec((D, T), lambda i: (0, i))
    tmaj = pl.BlockSpec((H, None, HEAD_ROWS, T), lambda i: (0, i // per, 0, i % per))
    norm = pl.BlockSpec((H, T, HEAD_ROWS), lambda i: (0, i, 0))
    tshape = jax.ShapeDtypeStruct((H, S // tq, HEAD_ROWS, tq), BF16)
    nshape = jax.ShapeDtypeStruct((H, S, HEAD_ROWS), BF16)
    return pl.pallas_call(
        body, name="fox_bwd_prep", grid=(S // T,),
        in_specs=[chan, chan, pl.BlockSpec((H, 1, T), lambda i: (0, 0, i)), tmaj],
        out_specs=[tmaj, norm, tmaj, norm], out_shape=[tshape, nshape, tshape, nshape],
        compiler_params=_params(("arbitrary",)),
    )(dot, ot, lse, qat)


def _causal(s, k_axis):
    ki = lax.broadcasted_iota(jnp.int32, s.shape, k_axis)
    qi = lax.broadcasted_iota(jnp.int32, s.shape, 1 - k_axis)
    return jnp.where(ki <= qi, s, NEG_INF)


def _seq_tile(i, t):
    return pl.ds(pl.multiple_of(i * t, t), t)


def _attn_forward(ka, qat, vt, S, D, tq, plan=None):
    H = D // HEAD_DIM
    nq = S // tq

    def body(ka_ref, qat_ref, vt_ref, o_ref, o32_ref, lse_ref, m_scr, l_scr, acc_scr):
        qi = pl.program_id(1)
        m_scr[...] = jnp.full_like(m_scr, NEG_INF)
        l_scr[...] = jnp.zeros_like(l_scr)
        acc_scr[...] = jnp.zeros_like(acc_scr)
        qa = qat_ref[...]

        def tile(ki, diagonal):
            s = jnp.dot(ka_ref[_seq_tile(ki, tq), :], qa, preferred_element_type=F32)
            if diagonal:
                s = _causal(s, 0)
            m_prev = m_scr[...]
            m_new = jnp.maximum(m_prev, jnp.max(s, axis=0, keepdims=True))
            alpha = jnp.exp(m_prev - m_new)
            p = jnp.exp(s - m_new)
            l_scr[...] = alpha * l_scr[...] + jnp.sum(p, axis=0, keepdims=True)
            acc_scr[...] = alpha * acc_scr[...] + jnp.dot(vt_ref[ki], p.astype(BF16), preferred_element_type=F32)
            m_scr[...] = m_new

        def off_diagonal_pair(j, _):
            tile(2 * j, False)
            tile(2 * j + 1, False)
            return 0

        lax.fori_loop(0, qi // 2, off_diagonal_pair, 0)
        pl.when(qi % 2 == 1)(lambda: tile(qi - 1, False))
        tile(qi, True)
        o = acc_scr[...] / l_scr[...]
        o_ref[...] = o.astype(BF16)
        o32_ref[...] = o
        lse_ref[...] = m_scr[...] + jnp.log(l_scr[...])

    chan = pl.BlockSpec((HEAD_DIM, tq), lambda h, i: (h, i))
    stat = pl.BlockSpec((None, 1, tq), lambda h, i: (h, 0, i))
    return _hosted_call(
        body, "attn_forward", (H, nq),
        [pl.BlockSpec((None, S, HEAD_ROWS), lambda h, i: (h, 0, 0)),
         pl.BlockSpec((None, None, HEAD_ROWS, tq), lambda h, i: (h, i, 0, 0)),
         pl.BlockSpec((None, nq, HEAD_DIM, tq), lambda h, i: (h, 0, 0, 0))],
        [chan, chan, stat],
        [jax.ShapeDtypeStruct((D, S), BF16), jax.ShapeDtypeStruct((D, S), F32), jax.ShapeDtypeStruct((H, 1, S), F32)],
        [pltpu.VMEM((1, tq), F32), pltpu.VMEM((1, tq), F32), pltpu.VMEM((HEAD_DIM, tq), F32)],
        (ka, qat, vt), ("arbitrary", "arbitrary"), plan)


ONES_ROW_Q = HEAD_DIM + 3
ONES_COL_K = HEAD_DIM


def _attn_backward(qa, doa, qat, doat, ka, kat, vat, S, D, tq, plan=None):
    H = D // HEAD_DIM
    nq = S // tq

    def body(qa_ref, doa_ref, qat_ref, doat_ref, ka_ref, kat_ref, vat_ref, dq_ref, dk_ref, dv_ref, dk_scr, dv_scr):
        ki = pl.program_id(1)

        @pl.when(ki == 0)
        def _():
            dq_ref[...] = jnp.zeros_like(dq_ref)

        dk_scr[...] = jnp.zeros_like(dk_scr)
        dv_scr[...] = jnp.zeros_like(dv_scr)
        kt = kat_ref[...]
        vt = vat_ref[...]
        kn = ka_ref[...]

        def tile(qi, diagonal):
            rows = _seq_tile(qi, tq)
            s = jnp.dot(qa_ref[rows, :], kt, preferred_element_type=F32)
            if diagonal:
                s = _causal(s, 1)
            p = jnp.exp(s)
            ds = (p * jnp.dot(doa_ref[rows, :], vt, preferred_element_type=F32)).astype(BF16)
            dv_scr[...] += jnp.dot(doat_ref[qi, 0:HEAD_DIM, :], p.astype(BF16), preferred_element_type=F32)
            dk_scr[...] += jnp.dot(qat_ref[qi], ds, preferred_element_type=F32)
            dq_ref[rows, :] += jnp.dot(ds, kn, preferred_element_type=F32)

        def off_diagonal(qi, _):
            tile(qi, False)
            return 0

        tile(ki, True)
        lax.fori_loop(ki + 1, nq, off_diagonal, 0)
        dk_ref[...] = dk_scr[...]
        dv_ref[...] = dv_scr[...].astype(BF16)

    whole = pl.BlockSpec((None, S, HEAD_ROWS), lambda h, i: (h, 0, 0))
    tiles = pl.BlockSpec((None, nq, HEAD_ROWS, tq), lambda h, i: (h, 0, 0, 0))
    one = pl.BlockSpec((None, None, HEAD_ROWS, tq), lambda h, i: (h, i, 0, 0))
    return _hosted_call(
        body, "attn_backward", (H, nq),
        [whole, whole, tiles, tiles, pl.BlockSpec((None, tq, HEAD_ROWS), lambda h, i: (h, i, 0)), one, one],
        [whole, pl.BlockSpec((None, HEAD_ROWS, tq), lambda h, i: (h, 0, i)),
         pl.BlockSpec((HEAD_DIM, tq), lambda h, i: (h, i))],
        [jax.ShapeDtypeStruct((H, S, HEAD_ROWS), F32), jax.ShapeDtypeStruct((H, HEAD_ROWS, S), F32),
         jax.ShapeDtypeStruct((D, S), BF16)],
        [pltpu.VMEM((HEAD_ROWS, tq), F32), pltpu.VMEM((HEAD_DIM, tq), F32)],
        (qa, doa, qat, doat, ka, kat, vat), ("arbitrary", "arbitrary"), plan)


def _fox_prep_bwd(ut, dq, dkt, dvt, b_f, qg, kg, S, D, tq):
    H = D // HEAD_DIM
    T = min(tq, 256)
    nT = S // T
    NU = 3 * D + LANES
    scale = HEAD_DIM ** -0.5

    def body(q_ref, k_ref, f_ref, dq_ref, dk_ref, dv_ref, bf_ref, qg_ref, kg_ref, tri_ref,
             du_ref, dbf_ref, dqg_ref, dkg_ref, gq_acc, gk_acc, fcar, dc_scr):
        step = pl.program_id(0)

        @pl.when(step == 0)
        def _():
            for ref in (gq_acc, gk_acc, fcar, dbf_ref):
                ref[...] = jnp.zeros_like(ref)

        dc_scr[...] = jnp.zeros_like(dc_scr)

        def head(h, _):
            rows = _head_rows(h)
            dqb = dq_ref[h].T
            dkb = dk_ref[h]
            dc_scr[pl.ds(h, 1), :] = dqb[ONES_COL_K:ONES_COL_K + 1, :] - dkb[ONES_ROW_Q:ONES_ROW_Q + 1, :]
            for src, dsrc, gain, acc, mul, base in ((q_ref, dqb, qg_ref, gq_acc, scale, 0),
                                                    (k_ref, dkb, kg_ref, gk_acc, 1.0, D)):
                x = src[rows, :]
                rs = lax.rsqrt(jnp.mean(x * x, axis=0, keepdims=True) + EPS)
                xhat = x * rs
                dn = dsrc[0:HEAD_DIM, :] * mul
                acc[rows, :] += jnp.sum(dn * xhat, axis=1, keepdims=True)
                dxh = dn * gain[rows, :]
                dx = rs * (dxh - xhat * jnp.mean(dxh * xhat, axis=0, keepdims=True))
                du_ref[pl.ds(pl.multiple_of(base + h * HEAD_DIM, HEAD_DIM), HEAD_DIM), :] = dx.astype(BF16)
            return 0

        lax.fori_loop(0, H, head, 0)
        du_ref[2 * D:3 * D, :] = dv_ref[...]
        dlf, carry = _lane_cumsum(dc_scr[...], tri_ref, fcar[...], True)
        fcar[...] = carry
        dfl = dlf * _sigmoid(-(f_ref[...] + bf_ref[...]))
        dbf_ref[...] += jnp.sum(dfl, axis=1, keepdims=True)
        du_ref[3 * D:NU, :] = dfl.astype(BF16)

        @pl.when(step == nT - 1)
        def _():
            for acc, ref in ((gq_acc, dqg_ref), (gk_acc, dkg_ref)):
                tot = jnp.zeros((HEAD_DIM, 1), F32)
                for h in range(H):
                    tot = tot + acc[h * HEAD_DIM:(h + 1) * HEAD_DIM, :]
                ref[...] = tot

    rev = lambda i: nT - 1 - i
    part = lambda j: pl.BlockSpec((D, T), lambda i: (j, rev(i)))
    colv = lambda n: pl.BlockSpec((n, 1), lambda i: (0, 0))
    return pl.pallas_call(
        body, name="fox_prep_bwd", grid=(nT,),
        in_specs=[part(0), part(1), pl.BlockSpec((LANES, T), lambda i: (3 * D // LANES, rev(i))),
                  pl.BlockSpec((H, T, HEAD_ROWS), lambda i: (0, rev(i), 0)),
                  pl.BlockSpec((H, HEAD_ROWS, T), lambda i: (0, 0, rev(i))), pl.BlockSpec((D, T), lambda i: (0, rev(i))),
                  colv(LANES), colv(D), colv(D), pl.BlockSpec((LANES, LANES), lambda i: (0, 0))],
        out_specs=[pl.BlockSpec((NU, T), lambda i: (0, rev(i))), colv(LANES), colv(HEAD_DIM), colv(HEAD_DIM)],
        out_shape=[jax.ShapeDtypeStruct((NU, S), BF16), jax.ShapeDtypeStruct((LANES, 1), F32),
                   jax.ShapeDtypeStruct((HEAD_DIM, 1), F32), jax.ShapeDtypeStruct((HEAD_DIM, 1), F32)],
        scratch_shapes=[pltpu.VMEM((D, 1), F32), pltpu.VMEM((D, 1), F32), pltpu.VMEM((LANES, 1), F32),
                        pltpu.VMEM((LANES, T), F32)],
        compiler_params=_params(("arbitrary",)),
    )(ut, ut, ut, dq, dkt, dvt, b_f, qg, kg, _tri_matrix(True))


def _block_diag_tiles(w):
    n = w.shape[0]
    per = min(MXU_DIM, n * LRU_BLOCK_DIM) // LRU_BLOCK_DIM
    eye = jnp.eye(per, dtype=w.dtype)
    w5 = w.reshape(n // per, per, LRU_BLOCK_DIM, 1, LRU_BLOCK_DIM) * eye[None, :, None, :, None]
    return w5.reshape(n // per, per * LRU_BLOCK_DIM, per * LRU_BLOCK_DIM).astype(BF16)


def _block_diag_extract(t, n):
    per = t.shape[-1] // LRU_BLOCK_DIM
    eye = jnp.eye(per, dtype=t.dtype)
    t5 = t.reshape(n // per, per, LRU_BLOCK_DIM, per, LRU_BLOCK_DIM) * eye[None, :, None, :, None]
    return t5.sum(axis=3).reshape(n, LRU_BLOCK_DIM, LRU_BLOCK_DIM)


def _local_step(x, tgt, small, wv, grad_view, comm=None):
    S, D = x.shape
    F = 4 * D
    H = D // HEAD_DIM
    nblk = D // LRU_BLOCK_DIM
    NU = 3 * D + LANES
    tq = max(LANES, min(512, S // 4))
    assert S % tq == 0
    vec = lambda a: a.reshape(1, -1).astype(F32)
    col = lambda a: a.reshape(-1, 1).astype(F32)
    mix_g, mlp_g = small["mix_norm"], small["mlp_norm"]
    conv_w, conv_b = small["conv_w"], vec(small["lru_conv_b"])
    wr_bd, wi_bd = _block_diag_tiles(small["lru_w_r"][0]), _block_diag_tiles(small["lru_w_i"][0])
    b_r, b_i, lam = vec(small["lru_b_r"]), vec(small["lru_b_i"]), vec(small["lru_lambda"])
    b_f = jnp.pad(col(small["fox_b_f"]), ((0, LANES - H), (0, 0)))
    qg, kg = jnp.tile(col(small["fox_q_gain"]), (H, 1)), jnp.tile(col(small["fox_k_gain"]), (H, 1))
    X = lambda a: _View(a)
    grads = {}
    gout = functools.partial(grad_view, grads)

    def hosted(name, fn, *args):
        plan = comm.before(name, grads) if comm is not None else None
        res, side = fn(*args, plan=plan)
        if plan is not None:
            comm.after(name, side, wv)
        return res

    def hosted_mm(name, *args, **kw):
        plan = comm.before(name, grads) if comm is not None else None
        if plan is None:
            return _matmul(name, *args, **kw)
        res, side = _matmul(name, *args, plan=plan, **kw)
        comm.after(name, side, wv)
        return res

    norm_rows = _pick(S, (512, 256, 128))
    two = lambda: [_fresh(S, D, F32), _fresh(S, D, BF16)]

    def mlp_up(l, hm):
        return hosted_mm(f"mlp{l}_up", X(hm), wv[f"w1_{l}"], S, F, D, outs=[_fresh(S, F, BF16), _fresh(S, F, BF16)],
                         epilogue=_ep_relu2)

    def mlp_bwd(l, xin, hm, z, act, d, db):
        (dz,) = hosted_mm(f"mlp{l}_dact", X(db), wv[f"w2_{l}"], S, F, D, tb=True, outs=[_fresh(S, F, BF16)],
                          epilogue=_ep_drelu2, extras=[X(z)])
        (grads[f"w2_{l}"],) = _matmul(f"mlp{l}_dw2", X(act), X(db), F, D, S, ta=True, outs=[gout(f"w2_{l}")],
                                      epilogue=_ep_store)
        (grads[f"w1_{l}"],) = _matmul(f"mlp{l}_dw1", X(hm), X(dz), D, F, S, ta=True, outs=[gout(f"w1_{l}")],
                                      epilogue=_ep_store)
        return _matmul(f"mlp{l}_dhm", X(dz), wv[f"w1_{l}"], S, D, F, tb=True, outs=two(), n_sums=1,
                       epilogue=_ep_norm_bwd, extras=[X(xin), X(d)], vecs=[mlp_g[l:l + 1]], tm=norm_rows)

    h0 = _rms_fwd("mix0_norm", x, mix_g[0:1], S, D)
    (u0,) = _matmul("lru_in", X(h0), wv["lru_in"], S, 2 * D, D, outs=[_fresh(S, 2 * D, F32)], epilogue=_ep_store)
    y, xc, r, ig, hs = hosted("lru_fwd", _lru_fwd, u0, conv_w, conv_b, wr_bd, b_r, wi_bd, b_i, lam, S, D)
    x1, hm0 = _matmul("lru_out", X(y), wv["lru_out"], S, D, D, outs=two(), epilogue=_ep_resid_norm, extras=[X(x)],
                      vecs=[mlp_g[0:1]], tm=norm_rows)
    z0, act0 = mlp_up(0, hm0)
    x2, h1 = hosted_mm("mlp0_down", X(act0), wv["w2_0"], S, D, F, outs=two(), epilogue=_ep_resid_norm, extras=[X(x1)],
                       vecs=[mix_g[1:2]], tm=norm_rows)
    (u1,) = _matmul("fox_in", wv["fox_in"], X(h1), NU, S, D, tb=True, outs=[_fresh(NU, S, F32)], epilogue=_ep_store)
    qat, kat, vat, ka, vt = _fox_prep(u1, b_f, qg, kg, S, D, tq)
    o, o32, lse = hosted("attn_forward", _attn_forward, ka, qat, vt, S, D, tq)
    x3, hm1 = _matmul("fox_out", X(o), wv["fox_out"], S, D, D, ta=True, outs=two(), epilogue=_ep_resid_norm,
                      extras=[X(x2)], vecs=[mlp_g[1:2]], tm=norm_rows)
    z1, act1 = mlp_up(1, hm1)
    (x4,) = _matmul("mlp1_down", X(act1), wv["w2_1"], S, D, F, outs=[_fresh(S, D, F32)], epilogue=_ep_resid,
                    extras=[X(x3)])
    loss, d4, d4b = _loss_head(x4, tgt, S, D)

    d3, d3b, dg_mlp1 = mlp_bwd(1, x3, hm1, z1, act1, d4, d4b)
    (do,) = _matmul("fox_dout", wv["fox_out"], X(d3b), D, S, D, tb=True, outs=[_fresh(D, S, BF16)], epilogue=_ep_store)
    (grads["fox_out"],) = _matmul("fox_dwout", X(o), X(d3b), D, D, S, outs=[gout("fox_out")], epilogue=_ep_store)
    doat, doa, qat1, qa1 = _fox_bwd_prep(do, o32, lse, qat, S, D, tq)
    dqn, dkn, dv = hosted("attn_backward", _attn_backward, qa1, doa, qat1, doat, ka, kat, vat, S, D, tq)
    du1, dbf, dqg, dkg = _fox_prep_bwd(u1, dqn, dkn, dv, b_f, qg, kg, S, D, tq)
    (grads["fox_in"],) = _matmul("fox_dwin", X(h1), X(du1), D, NU, S, ta=True, tb=True, outs=[gout("fox_in")],
                                 epilogue=_ep_store)
    d2, d2b, dg_mix1 = _matmul("fox_dh", X(du1), wv["fox_in"], S, D, NU, ta=True, outs=two(), n_sums=1,
                               epilogue=_ep_norm_bwd, extras=[X(x2), X(d3)], vecs=[mix_g[1:2]], tm=norm_rows)
    d1, d1b, dg_mlp0 = mlp_bwd(0, x1, hm0, z0, act0, d2, d2b)
    (dy,) = _matmul("lru_dout", X(d1b), wv["lru_out"], S, D, D, tb=True, outs=[_fresh(S, D, F32)], epilogue=_ep_store)
    (grads["lru_out"],) = _matmul("lru_dwout", X(y), X(d1b), D, D, S, ta=True, outs=[gout("lru_out")],
                                  epilogue=_ep_store)
    du0, dcw, dcb, dlam, dbr, dbi, dwr, dwi = hosted("lru_bwd", _lru_bwd, dy, u0, xc, r, ig, hs, conv_w, wr_bd, wi_bd,
                                                     lam, S, D)
    (grads["lru_in"],) = _matmul("lru_dwin", X(h0), X(du0), D, 2 * D, S, ta=True, outs=[gout("lru_in")],
                                 epilogue=_ep_store)
    gx, dg_mix0 = _matmul("lru_dh", X(du0), wv["lru_in"], S, D, 2 * D, tb=True, outs=[_fresh(S, D, F32)], n_sums=1,
                          epilogue=lambda *a: _ep_norm_bwd(*a)[::2], extras=[X(x), X(d1)], vecs=[mix_g[0:1]],
                          tm=norm_rows)

    grads.update(
        mix_norm=jnp.concatenate([dg_mix0, dg_mix1], axis=0), mlp_norm=jnp.concatenate([dg_mlp0, dg_mlp1], axis=0),
        conv_w=dcw, lru_conv_b=dcb, lru_w_r=_block_diag_extract(dwr, nblk)[None], lru_b_r=dbr.reshape(1, nblk, -1),
        lru_w_i=_block_diag_extract(dwi, nblk)[None], lru_b_i=dbi.reshape(1, nblk, -1), lru_lambda=dlam,
        fox_b_f=dbf[:H].reshape(1, H), fox_q_gain=dqg.reshape(1, -1), fox_k_gain=dkg.reshape(1, -1))
    return loss, gx, grads


def _place():
    x, y, c = lax.axis_index("x"), lax.axis_index("y"), lax.axis_index("c")
    chips = [(1 - x, y), (x, 1 - y), (1 - x, 1 - y)]
    return x, y, c, 2 * x + y, chips


BOUNCE_BYTES = 1 << 20


def _bounce_shape(rows, cols, dtype):
    chunk = rows
    while chunk % 2 == 0 and chunk > 16 and chunk * cols * jnp.dtype(dtype).itemsize > BOUNCE_BYTES:
        chunk //= 2
    return pltpu.VMEM((2, chunk, cols), dtype)


def _bounce_copy(src, dst, buf, sem):
    chunk = buf.shape[1]
    n = src.shape[0] // chunk
    cin = lambda i: pltpu.make_async_copy(src.at[pl.ds(i * chunk, chunk)], buf.at[i % 2], sem.at[i % 2])
    cout = lambda i: pltpu.make_async_copy(buf.at[i % 2], dst.at[pl.ds(i * chunk, chunk)], sem.at[2 + i % 2])
    cin(0).start()
    for i in range(n):
        cin(i).wait()
        if i + 1 < n:
            if i >= 1:
                cout(i - 1).wait()
            cin(i + 1).start()
        cout(i).start()
    if n >= 2:
        cout(n - 2).wait()
    cout(n - 1).wait()


def _hbm_call(body, name, arrays, out_shape, n_dma_sems, bounce=()):
    scratch = [pltpu.SemaphoreType.DMA((k,)) for k in n_dma_sems]
    for rows, cols, dtype in bounce:
        scratch += [_bounce_shape(rows, cols, dtype), pltpu.SemaphoreType.DMA((4,))]
    return pl.pallas_call(
        body, name=name, in_specs=[ANY] * len(arrays), out_specs=[ANY] * len(out_shape), out_shape=out_shape,
        scratch_shapes=scratch,
        compiler_params=pltpu.CompilerParams(has_side_effects=True, vmem_limit_bytes=VMEM_LIMIT),
    )(*arrays)


class _Gather:
    def __init__(self, shards):
        n = self.n = len(shards)
        self.operands = list(shards)
        self.out_shape = [jax.ShapeDtypeStruct((N_CHIPS,) + tuple(a.shape), a.dtype) for a in shards]
        self.scratch = [pltpu.SemaphoreType.DMA((3 * n,)) for _ in range(4)]
        for a in shards:
            self.scratch += [_bounce_shape(a.shape[0], a.shape[1], a.dtype), pltpu.SemaphoreType.DMA((4,))]

    def _copies(self, ins, outs, scr):
        send, recv, fsend, frecv = scr[:4]
        x, y, c, s, chips = _place()

        def rows(a, chip_idx, which):
            hr = ins[a].shape[0] // 2
            return outs[a].at[chip_idx, pl.ds(which * hr, hr)]

        def landed(a, j, core):
            return rows(a, 2 * chips[j][0] + chips[j][1], core)

        def ici(a, j, mine):
            hr = ins[a].shape[0] // 2
            src, dst = (ins[a].at[pl.ds(c * hr, hr)], rows(a, s, c)) if mine else (landed(a, j, c),) * 2
            return pltpu.make_async_remote_copy(src_ref=src, dst_ref=dst, send_sem=send.at[3 * a + j],
                                                recv_sem=recv.at[3 * a + j], device_id=(*chips[j], c),
                                                device_id_type=MESH)

        def d2d(a, j, mine):
            ref = landed(a, j, c if mine else 1 - c)
            return pltpu.make_async_remote_copy(src_ref=ref, dst_ref=ref, send_sem=fsend.at[3 * a + j],
                                                recv_sem=frecv.at[3 * a + j], device_id=(x, y, 1 - c),
                                                device_id_type=MESH)

        return ici, d2d, s

    def start(self, ins, outs, scr):
        ici, _, _ = self._copies(ins, outs, scr)
        for a in range(self.n):
            for j in range(3):
                ici(a, j, True).start()

    def middle(self, ins, outs, scr):
        ici, d2d, s = self._copies(ins, outs, scr)
        for a in range(self.n):
            _bounce_copy(ins[a], outs[a].at[s], scr[4 + 2 * a], scr[5 + 2 * a])
        for a in range(self.n):
            for j in range(3):
                ici(a, j, False).wait_recv()
                d2d(a, j, True).start()

    def finish(self, ins, outs, scr):
        ici, d2d, _ = self._copies(ins, outs, scr)
        for a in range(self.n):
            for j in range(3):
                d2d(a, j, False).wait_recv()
        for a in range(self.n):
            for j in range(3):
                ici(a, j, True).wait_send()
                d2d(a, j, True).wait_send()


def _run_plan(name, plan):
    k_in, k_out = len(plan.operands), len(plan.out_shape)

    def body(*refs):
        parts = (refs[:k_in], refs[k_in:k_in + k_out], refs[k_in + k_out:])
        plan.start(*parts)
        plan.middle(*parts)
        plan.finish(*parts)

    return pl.pallas_call(
        body, name=name, in_specs=[ANY] * k_in, out_specs=[ANY] * k_out, out_shape=plan.out_shape,
        scratch_shapes=plan.scratch,
        compiler_params=pltpu.CompilerParams(has_side_effects=True, vmem_limit_bytes=VMEM_LIMIT),
    )(*plan.operands)


def _hosted_call(body, name, grid, in_specs, out_specs, out_shape, scratch_shapes, operands, sem, plan=None):
    if plan is None:
        res = pl.pallas_call(body, name=name, grid=grid, in_specs=in_specs, out_specs=out_specs, out_shape=out_shape,
                             scratch_shapes=scratch_shapes, compiler_params=_params(sem))(*operands)
        return res, None
    n_in, n_out, n_scr = len(in_specs), len(out_specs), len(scratch_shapes)
    k_in, k_out = len(plan.operands), len(plan.out_shape)
    total = int(np.prod(grid))

    def hosted(*refs):
        ins, refs = refs[:n_in], refs[n_in:]
        p_ins, refs = refs[:k_in], refs[k_in:]
        outs, refs = refs[:n_out], refs[n_out:]
        p_outs, refs = refs[:k_out], refs[k_out:]
        scr, p_scr = refs[:n_scr], refs[n_scr:]
        step = pl.program_id(0)
        for d in range(1, len(grid)):
            step = step * grid[d] + pl.program_id(d)
        pl.when(step == 0)(lambda: plan.start(p_ins, p_outs, p_scr))
        body(*ins, *outs, *scr)
        pl.when(step == total // 2)(lambda: plan.middle(p_ins, p_outs, p_scr))
        pl.when(step == total - 1)(lambda: plan.finish(p_ins, p_outs, p_scr))

    res = pl.pallas_call(
        hosted, name=name, grid=grid, in_specs=list(in_specs) + [ANY] * k_in, out_specs=list(out_specs) + [ANY] * k_out,
        out_shape=list(out_shape) + plan.out_shape, scratch_shapes=list(scratch_shapes) + plan.scratch,
        compiler_params=pltpu.CompilerParams(dimension_semantics=sem, vmem_limit_bytes=VMEM_LIMIT,
                                             has_side_effects=True),
    )(*operands, *plan.operands)
    return res[:n_out], res[n_out:]


def _all_gather(name, shards):
    return _run_plan(name, _Gather(shards))


def _pair_swap(name, arrs):
    n = len(arrs)

    def body(*refs):
        ins, outs = refs[:n], refs[n:2 * n]
        send, recv = refs[2 * n:]
        x, y, c, _, _ = _place()
        cps = []
        for a in range(n):
            hr = ins[a].shape[1] // 2
            cp = pltpu.make_async_remote_copy(
                src_ref=ins[a].at[:, pl.ds((1 - c) * hr, hr)], dst_ref=outs[a], send_sem=send.at[a],
                recv_sem=recv.at[a], device_id=(x, y, 1 - c), device_id_type=MESH)
            cp.start()
            cps.append(cp)
        for cp in cps:
            cp.wait()

    out_shape = [jax.ShapeDtypeStruct((a.shape[0], a.shape[1] // 2, a.shape[2]), a.dtype) for a in arrs]
    return _hbm_call(body, name, arrs, out_shape, (n, n))


class _Scatter:
    def __init__(self, parts):
        n = self.n = len(parts)
        self.operands = list(parts)
        self.out_shape = [jax.ShapeDtypeStruct(a.shape, a.dtype) for a in parts]
        self.scratch = [pltpu.SemaphoreType.DMA((3 * n,)) for _ in range(2)]
        for a in parts:
            self.scratch += [_bounce_shape(a.shape[1], a.shape[2], a.dtype), pltpu.SemaphoreType.DMA((4,))]

    def _copy(self, ins, outs, scr, a, j, mine):
        x, y, c, s, chips = _place()
        t = 2 * chips[j][0] + chips[j][1]
        return pltpu.make_async_remote_copy(
            src_ref=ins[a].at[t], dst_ref=outs[a].at[s if mine else t], send_sem=scr[0].at[3 * a + j],
            recv_sem=scr[1].at[3 * a + j], device_id=(*chips[j], c), device_id_type=MESH)

    def start(self, ins, outs, scr):
        for a in range(self.n):
            for j in range(3):
                self._copy(ins, outs, scr, a, j, True).start()

    def middle(self, ins, outs, scr):
        s = _place()[3]
        for a in range(self.n):
            _bounce_copy(ins[a].at[s], outs[a].at[s], scr[2 + 2 * a], scr[3 + 2 * a])

    def finish(self, ins, outs, scr):
        for a in range(self.n):
            for j in range(3):
                self._copy(ins, outs, scr, a, j, False).wait_recv()
        for a in range(self.n):
            for j in range(3):
                self._copy(ins, outs, scr, a, j, True).wait_send()


def _pair_gather(name, halves):
    n = len(halves)

    def body(*refs):
        ins, outs = refs[:n], refs[n:2 * n]
        send, recv = refs[2 * n:2 * n + 2]
        stage = refs[2 * n + 2:]
        x, y, c, _, _ = _place()
        cps = []
        for a in range(n):
            hr = ins[a].shape[0]
            cp = pltpu.make_async_remote_copy(
                src_ref=ins[a], dst_ref=outs[a].at[pl.ds(c * hr, hr)], send_sem=send.at[a], recv_sem=recv.at[a],
                device_id=(x, y, 1 - c), device_id_type=MESH)
            cp.start()
            cps.append((cp, hr))
        for a, (cp, hr) in enumerate(cps):
            _bounce_copy(ins[a], outs[a].at[pl.ds(c * hr, hr)], stage[2 * a], stage[2 * a + 1])
        for a, (cp, hr) in enumerate(cps):
            cp.wait_send()
            theirs = outs[a].at[pl.ds((1 - c) * hr, hr)]
            pltpu.make_async_remote_copy(src_ref=theirs, dst_ref=theirs, send_sem=send.at[a], recv_sem=recv.at[a],
                                         device_id=(x, y, 1 - c), device_id_type=MESH).wait_recv()

    out_shape = [jax.ShapeDtypeStruct((2 * a.shape[0], a.shape[1]), a.dtype) for a in halves]
    return _hbm_call(body, name, halves, out_shape, (n, n),
                     bounce=[(a.shape[0], a.shape[1], a.dtype) for a in halves])


def _row_tile(rows, cols, itemsize, n_bufs):
    budget = VMEM_LIMIT // 2
    for t in (1024, 512, 256, 128, 64, 32, 16):
        if rows % t == 0 and 2 * n_bufs * t * cols * itemsize <= budget:
            return t
    return rows


def _pair_add(name, g, gsib, core, out_dtype):
    _, r, cols = g.shape
    hr = r // 2
    t = _row_tile(hr, cols, 4, 3)
    per = hr // t

    def body(core_ref, a_ref, b_ref, o_ref):
        o_ref[...] = (a_ref[...].astype(F32) + b_ref[...].astype(F32)).astype(o_ref.dtype)

    grid_spec = pltpu.PrefetchScalarGridSpec(
        num_scalar_prefetch=1, grid=(N_CHIPS, per),
        in_specs=[pl.BlockSpec((None, t, cols), lambda s, i, core: (s, core[0] * per + i, 0)),
                  pl.BlockSpec((None, t, cols), lambda s, i, core: (s, i, 0))],
        out_specs=pl.BlockSpec((None, t, cols), lambda s, i, core: (s, i, 0)))
    return pl.pallas_call(body, name=name, grid_spec=grid_spec,
                          out_shape=jax.ShapeDtypeStruct((N_CHIPS, hr, cols), out_dtype),
                          compiler_params=_params(("arbitrary", "arbitrary")))(core, g, gsib)


def _chip_sum(name, parts):
    _, hr, cols = parts.shape
    t = _row_tile(hr, cols, 4, 5)

    def body(p_ref, o_ref):
        o_ref[...] = ((p_ref[0].astype(F32) + p_ref[1].astype(F32)) + p_ref[2].astype(F32)) + p_ref[3].astype(F32)

    return pl.pallas_call(
        body, name=name, grid=(hr // t,), in_specs=[pl.BlockSpec((N_CHIPS, t, cols), lambda i: (0, i, 0))],
        out_specs=pl.BlockSpec((t, cols), lambda i: (i, 0)), out_shape=jax.ShapeDtypeStruct((hr, cols), F32),
        compiler_params=_params(("arbitrary",)))(parts)


def _pair_partials(tag, arrs, wire_dtypes, core):
    sib = _pair_swap(f"{tag}_pair_swap", arrs)
    return _Scatter([_pair_add(f"{tag}_pair_add{i}", g, gs, core, dt)
                     for i, (g, gs, dt) in enumerate(zip(arrs, sib, wire_dtypes))])


def _finish_reduce(tag, scattered):
    halves = [_chip_sum(f"{tag}_chip_sum{i}", p) for i, p in enumerate(scattered)]
    return _pair_gather(f"{tag}_pair_gather", halves)


def _adamw(name, w, g_parts, m, v):
    rows, cols = w.shape
    n_parts = len(g_parts)
    part_rows = rows // n_parts
    t = _row_tile(part_rows, cols, 4, 7 + n_parts)
    per = part_rows // t
    c1 = 1.0 - ADAM_B1 ** ADAM_STEP
    c2 = 1.0 - ADAM_B2 ** ADAM_STEP

    def body(w_ref, m_ref, v_ref, *refs):
        g_refs, (go_ref, d_ref, nm_ref, nv_ref) = refs[:n_parts], refs[n_parts:]
        g = g_refs[0][...]
        for k in range(1, n_parts):
            g = jnp.where(pl.program_id(0) >= k * per, g_refs[k][...], g)
        go_ref[...] = g
        m = ADAM_B1 * m_ref[...] + (1.0 - ADAM_B1) * g
        v = ADAM_B2 * v_ref[...] + (1.0 - ADAM_B2) * (g * g)
        nm_ref[...] = m
        nv_ref[...] = v
        d_ref[...] = -ADAM_LR * ((m / c1) / (jnp.sqrt(v / c2) + ADAM_EPS) + ADAM_WD * w_ref[...])

    spec = pl.BlockSpec((t, cols), lambda i: (i, 0))
    g_specs = [pl.BlockSpec((t, cols), lambda i, k=k: (jnp.clip(i - k * per, 0, per - 1), 0)) for k in range(n_parts)]
    shp = jax.ShapeDtypeStruct((rows, cols), F32)
    return pl.pallas_call(body, name=name, grid=(rows // t,), in_specs=[spec] * 3 + g_specs, out_specs=[spec] * 4,
                          out_shape=[shp] * 4, compiler_params=_params(("arbitrary",)))(w, m, v, *g_parts)


_WEIGHTS = ["mix_norm", "mlp_norm", "mlp_w1", "mlp_w2", "lru_w_in", "lru_conv_w", "lru_conv_b", "lru_w_r", "lru_b_r",
            "lru_w_i", "lru_b_i", "lru_lambda", "lru_w_out", "fox_w_in", "fox_b_f", "fox_q_gain", "fox_k_gain",
            "fox_w_out"]
_REPLICATED = ["mix_norm", "mlp_norm", "lru_conv_b", "lru_w_r", "lru_b_r", "lru_w_i", "lru_b_i", "lru_lambda",
               "fox_b_f", "fox_q_gain", "fox_k_gain"]
_PACK_TILE = 2 * SUBLANES * LANES


def _as2d(a):
    return a.reshape(-1, a.shape[-1])


def kernel(x, mix_norm, mlp_norm, mlp_w1, mlp_w2, lru_w_in, lru_conv_w, lru_conv_b, lru_w_r, lru_b_r, lru_w_i, lru_b_i, lru_lambda, lru_w_out, fox_w_in, fox_b_f, fox_q_gain, fox_k_gain, fox_w_out, loss_target, m_mix_norm, m_mlp_norm, m_mlp_w1, m_mlp_w2, m_lru_w_in, m_lru_conv_w, m_lru_conv_b, m_lru_w_r, m_lru_b_r, m_lru_w_i, m_lru_b_i, m_lru_lambda, m_lru_w_out, m_fox_w_in, m_fox_b_f, m_fox_q_gain, m_fox_k_gain, m_fox_w_out, v_mix_norm, v_mlp_norm, v_mlp_w1, v_mlp_w2, v_lru_w_in, v_lru_conv_w, v_lru_conv_b, v_lru_w_r, v_lru_b_r, v_lru_w_i, v_lru_b_i, v_lru_lambda, v_lru_w_out, v_fox_w_in, v_fox_b_f, v_fox_q_gain, v_fox_k_gain, v_fox_w_out):
    args = dict(locals())
    W = {n: args[n] for n in _WEIGHTS}
    Mo = {n: args["m_" + n] for n in _WEIGHTS}
    Vo = {n: args["v_" + n] for n in _WEIGHTS}
    S, D = x.shape[1], x.shape[2]
    F = 4 * D
    H = D // HEAD_DIM
    NU = 3 * D + LANES
    FQ, DQ = F // N_CHIPS, D // N_CHIPS
    nfox = fox_w_in.shape[-1]
    chip = 2 * lax.axis_index("x") + lax.axis_index("y")
    core = lax.axis_index("c").astype(jnp.int32).reshape(1)

    cw_flat = jnp.pad(lru_conv_w.reshape(-1), (0, _PACK_TILE - CONV_WIDTH * DQ)).reshape(2 * SUBLANES, LANES)
    w1s, w2s = mlp_w1.astype(BF16), mlp_w2.astype(BF16)
    g_lin, g_lout, g_cw = _all_gather("gather_lru", [lru_w_in[0].astype(BF16), lru_w_out[0].astype(BF16), cw_flat])
    conv_w_full = jnp.transpose(g_cw.reshape(N_CHIPS, -1)[:, :CONV_WIDTH * DQ].reshape(N_CHIPS, CONV_WIDTH, DQ),
                                (1, 0, 2)).reshape(CONV_WIDTH, D)
    wv = {"lru_in": _View(g_lin, "cs"), "lru_out": _View(g_lout, "rs")}
    scattered = {}
    riding = {"attn_backward": ["w2_1", "w1_1", "fox_out"], "mlp0_dact": ["fox_in"],
              "lru_bwd": ["w2_0", "w1_0", "lru_out"]}

    def shard_major(name, g):
        if name == "fox_in":
            return jnp.transpose(g[:, :nfox * N_CHIPS].reshape(D, N_CHIPS, nfox), (1, 0, 2))
        return g

    class Comm:
        @staticmethod
        def before(name, grads):
            if name == "lru_fwd":
                return _Gather([w1s[0]])
            if name == "mlp0_up":
                return _Gather([w2s[0]])
            if name == "mlp0_down":
                return _Gather([fox_w_in[0].astype(BF16)])
            if name == "attn_forward":
                return _Gather([fox_w_out[0].astype(BF16), w1s[1], w2s[1]])
            if name in riding:
                arrs = [shard_major(n, grads[n]) for n in riding[name]]
                return _pair_partials(f"{name}_grads", arrs, [BF16] * len(arrs), core)
            return None

        @staticmethod
        def after(name, res, wv):
            if name == "lru_fwd":
                wv.update(w1_0=_View(res[0], "cs"))
            elif name == "mlp0_up":
                wv.update(w2_0=_View(res[0], "rs"))
            elif name == "mlp0_down":
                fox_full = jnp.concatenate([res[0][s] for s in range(N_CHIPS)], axis=1)
                fox_full = jnp.pad(fox_full, ((0, 0), (0, NU - fox_full.shape[1])))
                wv.update(fox_in=_View(fox_full.T))
            elif name == "attn_forward":
                wv.update(fox_out=_View(res[0], "rs"), w1_1=_View(res[1], "cs"), w2_1=_View(res[2], "rs"))
            else:
                scattered.update(zip(riding[name], res))

    def grad_view(grads, name):
        if name in ("w1_0", "w1_1"):
            return _View(None, "cs", shape=(N_CHIPS, D, FQ), dtype=BF16)
        if name in ("w2_0", "w2_1"):
            return _View(None, "rs", shape=(N_CHIPS, FQ, D), dtype=BF16)
        if name == "lru_in":
            return _View(None, "cs", shape=(N_CHIPS, D, 2 * D // N_CHIPS), dtype=BF16)
        if name in ("lru_out", "fox_out"):
            return _View(None, "rs", shape=(N_CHIPS, DQ, D), dtype=BF16)
        return _View(None, shape=(D, NU), dtype=BF16)

    small = {n: W[n] for n in _REPLICATED}
    small["conv_w"] = conv_w_full

    loss, gx, grads = _local_step(x[0], loss_target[0], small, wv, grad_view, Comm)

    pack_names = _REPLICATED + ["conv_w"]
    flat = jnp.concatenate([grads[n].reshape(-1).astype(F32) for n in pack_names] + [loss.reshape(-1)])
    per_chip = -(-flat.shape[0] // (N_CHIPS * _PACK_TILE)) * _PACK_TILE
    pack = jnp.pad(flat, (0, N_CHIPS * per_chip - flat.shape[0])).reshape(N_CHIPS, per_chip // LANES, LANES)
    tail = _run_plan("tail_grads_chip_scatter", _pair_partials("tail_grads", [grads["lru_in"], pack], [BF16, F32], core))
    scattered.update(lru_in=tail[0], pack=tail[1])
    order = ["w1_0", "w1_1", "w2_0", "w2_1", "lru_in", "lru_out", "fox_in", "fox_out", "pack"]
    red = dict(zip(order, _finish_reduce("grads", [scattered[n] for n in order])))
    (all_pack,) = _all_gather("gather_small_grads", [red["pack"]])
    all_flat = all_pack.reshape(-1)
    G = {}
    off = 0
    for n in pack_names:
        shape = grads[n].shape if n == "conv_w" else W[n].shape
        size = int(np.prod(shape))
        G[n] = all_flat[off:off + size].reshape(shape)
        off += size
    total = all_flat[off]
    G["lru_conv_w"] = lax.dynamic_slice_in_dim(G.pop("conv_w"), chip * DQ, DQ, axis=1)[None]
    parts = {n: [_as2d(G[n])] for n in G}
    parts.update(mlp_w1=[red["w1_0"], red["w1_1"]], mlp_w2=[red["w2_0"], red["w2_1"]], lru_w_in=[red["lru_in"]],
                 lru_w_out=[red["lru_out"]], fox_w_in=[red["fox_in"]], fox_w_out=[red["fox_out"]])

    delta, new_m, new_v = {}, {}, {}
    for n in _WEIGHTS:
        go, d, nm, nv = _adamw(f"adamw_{n}", _as2d(W[n]), parts[n], _as2d(Mo[n]), _as2d(Vo[n]))
        G[n], delta[n], new_m[n], new_v[n] = (t.reshape(W[n].shape) for t in (go, d, nm, nv))

    return (total, gx[None], *[G[n] for n in _WEIGHTS], *[delta[n] for n in _WEIGHTS],
            *[new_m[n] for n in _WEIGHTS], *[new_v[n] for n in _WEIGHTS])
```

```python
import functools

import numpy as np
import jax
import jax.numpy as jnp
from jax import lax
from jax.experimental import pallas as pl
from jax.experimental.pallas import tpu as pltpu

F32 = jnp.float32
BF16 = jnp.bfloat16

HEAD_DIM = 64
LRU_BLOCK_DIM = 64
CONV_WIDTH = 4
LRU_C = 8.0
EPS = 1e-6
NEG_INF = -1e30
ADAM_LR = 0.001
ADAM_B1 = 0.9
ADAM_B2 = 0.999
ADAM_EPS = 1e-08
ADAM_WD = 0.01
ADAM_STEP = 10

N_CHIPS = 4
LANES = 128
SUBLANES = 8
MXU_DIM = 256
VMEM_LIMIT = 52 * 1024 * 1024
MESH = pl.DeviceIdType.MESH
ANY = pl.BlockSpec(memory_space=pl.ANY)


def _pick(n, prefs):
    for p in prefs:
        if p <= n and n % p == 0:
            return p
    return n


def _params(sem=None):
    return pltpu.CompilerParams(dimension_semantics=sem, vmem_limit_bytes=VMEM_LIMIT)


class _View:
    def __init__(self, arr, kind="plain", r0=0, rows=None, shape=None, dtype=None):
        self.arr = arr
        self.kind = kind
        self.r0 = r0
        self.shape = tuple(arr.shape) if arr is not None else tuple(shape)
        self.dtype = arr.dtype if arr is not None else dtype
        self.rows = rows if rows is not None else self.shape[-2]

    def limits(self):
        if self.kind == "plain":
            return 0, 0
        rows = int(np.gcd(self.rows, self.r0))
        return rows, (self.shape[-1] if self.kind == "cs" else 0)

    def spec(self, br, bc, fr, fc):
        if self.kind == "plain":
            return pl.BlockSpec((br, bc), lambda *g: (fr(*g), fc(*g)))
        ncol = self.shape[-1]
        r0b = self.r0 // br
        assert self.r0 % br == 0 and self.rows % br == 0 and ncol % bc == 0, (self.shape, self.r0, br, bc)
        if self.kind == "cs":
            per = ncol // bc
            return pl.BlockSpec((None, br, bc), lambda *g: (fc(*g) // per, r0b + fr(*g), fc(*g) % per))
        per = self.rows // br
        return pl.BlockSpec((None, br, bc), lambda *g: (fr(*g) // per, r0b + fr(*g) % per, fc(*g)))


def _bf(x):
    return x if x.dtype == BF16 else x.astype(BF16)


def _matmul(name, A, B, M, N, K, *, ta=False, tb=False, outs, epilogue, extras=(), vecs=(), n_sums=0,
            tm=None, tn=None, tk=None, plan=None):
    lim = {"m": [M], "n": [N], "k": [K]}
    for view, (rdim, cdim) in ([(A, "km" if ta else "mk"), (B, "nk" if tb else "kn")]
                               + [(e, "mn") for e in extras] + [(o, "mn") for o in outs]):
        r_lim, c_lim = view.limits()
        lim[rdim].append(r_lim)
        lim[cdim].append(c_lim)
    tm = tm or _pick(int(np.gcd.reduce(lim["m"])), (1024, 640, 512, 256, 128))
    tn = tn or _pick(int(np.gcd.reduce(lim["n"])), (1024, 640, 512, 256, 128))
    tk = tk or _pick(int(np.gcd.reduce(lim["k"])), (1024, 640, 512, 256, 128))
    nk = K // tk
    gi = lambda i, j, k: i
    gj = lambda i, j, k: j
    gk = lambda i, j, k: k
    a_spec = A.spec(tk, tm, gk, gi) if ta else A.spec(tm, tk, gi, gk)
    b_spec = B.spec(tn, tk, gj, gk) if tb else B.spec(tk, tn, gk, gj)
    ca = 0 if ta else 1
    cb = 1 if tb else 0
    ne, no = len(extras) + len(vecs), len(outs)
    assert n_sums == 0 or tn == N
    row_spec = pl.BlockSpec((1, tn), lambda i, j, k: (0, j))
    in_specs = [a_spec, b_spec] + [e.spec(tm, tn, gi, gj) for e in extras] + [row_spec] * len(vecs)
    operands = [A.arr, B.arr] + [e.arr for e in extras] + list(vecs)
    out_specs = [o.spec(tm, tn, gi, gj) for o in outs] + [row_spec] * n_sums
    out_shape = ([jax.ShapeDtypeStruct(o.shape, o.dtype) for o in outs]
                 + [jax.ShapeDtypeStruct((1, N), F32)] * n_sums)

    def body(*refs):
        a_ref, b_ref = refs[0], refs[1]
        ex = refs[2:2 + ne]
        o_refs = refs[2 + ne:2 + ne + no]
        s_refs = refs[2 + ne + no:2 + ne + no + n_sums]
        first_row_tile = pl.program_id(0) == 0

        def prod():
            return lax.dot_general(_bf(a_ref[...]), _bf(b_ref[...]), (((ca,), (cb,)), ((), ())),
                                   preferred_element_type=F32)

        def finish(acc):
            res = epilogue(acc, *[e[...] for e in ex])
            for o_ref, r in zip(o_refs, res[:no]):
                o_ref[...] = r.astype(o_ref.dtype)
            for s_ref, r in zip(s_refs, res[no:]):
                def assign(s_ref=s_ref, r=r):
                    s_ref[...] = r

                def accumulate(s_ref=s_ref, r=r):
                    s_ref[...] += r

                pl.when(first_row_tile)(assign)
                pl.when(jnp.logical_not(first_row_tile))(accumulate)

        if nk == 1:
            finish(prod())
        else:
            acc_ref = refs[-1]
            k = pl.program_id(2)

            @pl.when(k == 0)
            def _():
                acc_ref[...] = jnp.zeros_like(acc_ref)

            acc_ref[...] += prod()

            @pl.when(k == nk - 1)
            def _():
                finish(acc_ref[...])

    res, side = _hosted_call(body, name, (M // tm, N // tn, nk), in_specs, out_specs, out_shape,
                             [pltpu.VMEM((tm, tn), F32)] if nk > 1 else [], operands,
                             ("arbitrary", "arbitrary", "arbitrary"), plan)
    return res if plan is None else (res, side)


def _ep_store(acc):
    return (acc,)


def _ep_resid(acc, res):
    return (res + acc,)


def _ep_resid_norm(acc, res, g):
    xo = res + acc
    r = lax.rsqrt(jnp.mean(xo * xo, axis=-1, keepdims=True) + EPS)
    return (xo, (xo * r) * g)


def _ep_norm_bwd(acc, x, dres, g):
    r = lax.rsqrt(jnp.mean(x * x, axis=-1, keepdims=True) + EPS)
    xhat = x * r
    dxn = acc * g
    tot = dres + r * (dxn - xhat * jnp.mean(dxn * xhat, axis=-1, keepdims=True))
    return (tot, tot, jnp.sum(acc * xhat, axis=0, keepdims=True))


def _ep_relu2(acc):
    zp = jnp.maximum(acc, 0.0)
    return (acc, zp * zp)


def _ep_drelu2(acc, z):
    return (acc * (2.0 * jnp.maximum(z.astype(F32), 0.0)),)


def _fresh(M, N, dtype):
    return _View(None, shape=(M, N), dtype=dtype)


def _rms_fwd(name, x, g, S, D):
    T = _pick(S, (512, 256, 128))

    def body(x_ref, g_ref, h_ref):
        x = x_ref[...]
        r = lax.rsqrt(jnp.mean(x * x, axis=-1, keepdims=True) + EPS)
        h_ref[...] = ((x * r) * g_ref[...]).astype(BF16)

    return pl.pallas_call(
        body, name=name, grid=(S // T,),
        in_specs=[pl.BlockSpec((T, D), lambda i: (i, 0)), pl.BlockSpec((1, D), lambda i: (0, 0))],
        out_specs=pl.BlockSpec((T, D), lambda i: (i, 0)),
        out_shape=jax.ShapeDtypeStruct((S, D), BF16),
        compiler_params=_params(("arbitrary",)),
    )(x, g)


def _loss_head(x, tgt, S, D):
    T = _pick(S, (512, 256, 128))

    def body(x_ref, t_ref, loss_ref, d_ref, db_ref):
        @pl.when(pl.program_id(0) == 0)
        def _():
            loss_ref[...] = jnp.zeros_like(loss_ref)

        e = x_ref[...] - t_ref[...]
        loss_ref[...] += 0.5 * jnp.sum(jnp.mean(e * e, axis=-1, keepdims=True), axis=0, keepdims=True)
        d = e * (1.0 / D)
        d_ref[...] = d
        db_ref[...] = d.astype(BF16)

    row = pl.BlockSpec((T, D), lambda i: (i, 0))
    return pl.pallas_call(
        body, name="loss_head", grid=(S // T,), in_specs=[row, row],
        out_specs=[pl.BlockSpec((1, 1), lambda i: (0, 0)), row, row],
        out_shape=[jax.ShapeDtypeStruct((1, 1), F32), jax.ShapeDtypeStruct((S, D), F32),
                   jax.ShapeDtypeStruct((S, D), BF16)],
        compiler_params=_params(("arbitrary",)),
    )(x, tgt)


def _sigmoid(z):
    return 1.0 / (1.0 + jnp.exp(-z))


def _log_sigmoid(z):
    return jnp.minimum(z, 0.0) - jnp.log(1.0 + jnp.exp(-jnp.abs(z)))


_GELU_K = 0.7978845608028654
_GELU_C = 0.044715


def _gelu(x):
    t = jnp.tanh(_GELU_K * (x + _GELU_C * (x * x * x)))
    return 0.5 * x * (1.0 + t)


def _gelu_and_grad(x):
    x2 = x * x
    t = jnp.tanh(_GELU_K * (x + _GELU_C * (x2 * x)))
    g = 0.5 * x * (1.0 + t)
    dg = 0.5 * (1.0 + t) + 0.5 * x * (1.0 - t * t) * (_GELU_K * (1.0 + 3.0 * _GELU_C * x2))
    return g, dg


def _decay_terms(r, ls):
    la = LRU_C * r * ls
    a = jnp.exp(la)
    a2 = jnp.exp(2.0 * la)
    mult = jnp.sqrt(-jnp.tanh(la) * (a2 + 1.0))
    return a, a2, mult


def _lru_fwd(u0, conv_w, conv_b, wr_bd, b_r, wi_bd, b_i, lam, S, D, plan=None):
    T = _pick(S, (256, 128))
    GT = wr_bd.shape[-1]
    nG = D // GT

    def body(gb_ref, xb_ref, cw_ref, cb_ref, wr_ref, br_ref, wi_ref, bi_ref, lam_ref,
             y_ref, xc_ref, r_ref, i_ref, hs_ref, ext, a_scr, hcar):
        @pl.when(pl.program_id(0) == 0)
        def _():
            ext[0:SUBLANES, :] = jnp.zeros((SUBLANES, D), F32)
            hcar[...] = jnp.zeros_like(hcar)

        xb = xb_ref[...]
        ext[SUBLANES:SUBLANES + T, :] = xb
        xc = cb_ref[...]
        for k in range(CONV_WIDTH):
            xc = xc + ext[pl.ds(SUBLANES - (CONV_WIDTH - 1) + k, T), :] * cw_ref[k:k + 1, :]
        ext[0:SUBLANES, :] = xb[T - SUBLANES:T, :]
        xc_ref[...] = xc
        xcb = xc.astype(BF16)
        for g in range(nG):
            sl = slice(g * GT, (g + 1) * GT)
            zr = jnp.dot(xcb[:, sl], wr_ref[g], preferred_element_type=F32) + br_ref[:, sl]
            zi = jnp.dot(xcb[:, sl], wi_ref[g], preferred_element_type=F32) + bi_ref[:, sl]
            r_ref[:, sl] = _sigmoid(zr)
            i_ref[:, sl] = _sigmoid(zi)
        r = r_ref[...]
        a, _, mult = _decay_terms(r, _log_sigmoid(lam_ref[...]))
        a_scr[...] = a
        hs_ref[...] = mult * (i_ref[...] * xc)

        def step(t, h):
            h = a_scr[pl.ds(t, 1), :] * h + hs_ref[pl.ds(t, 1), :]
            hs_ref[pl.ds(t, 1), :] = h
            return h

        hcar[...] = lax.fori_loop(0, T, step, hcar[...], unroll=8)
        y_ref[...] = (_gelu(gb_ref[...]) * hs_ref[...]).astype(BF16)

    row = pl.BlockSpec((T, D), lambda i: (i, 0))
    vec = pl.BlockSpec((1, D), lambda i: (0, 0))
    bd = pl.BlockSpec((nG, GT, GT), lambda i: (0, 0, 0))
    f32o = jax.ShapeDtypeStruct((S, D), F32)
    return _hosted_call(
        body, "lru_fwd", (S // T,),
        [row, pl.BlockSpec((T, D), lambda i: (i, 1)), pl.BlockSpec((CONV_WIDTH, D), lambda i: (0, 0)), vec,
         bd, vec, bd, vec, vec],
        [row, row, row, row, row], [jax.ShapeDtypeStruct((S, D), BF16), f32o, f32o, f32o, f32o],
        [pltpu.VMEM((T + SUBLANES, D), F32), pltpu.VMEM((T, D), F32), pltpu.VMEM((1, D), F32)],
        (u0, u0, conv_w, conv_b, wr_bd, b_r, wi_bd, b_i, lam), ("arbitrary",), plan)


def _lru_bwd(dy, u0, xc, r, ig, hs, conv_w, wr_bd, wi_bd, lam, S, D, plan=None):
    T = _pick(S, (128,))
    nT = S // T
    GT = wr_bd.shape[-1]
    nG = D // GT
    W = CONV_WIDTH

    def body(dy_ref, gb_ref, xb_ref, xbp_ref, xc_ref, r_ref, i_ref, hs_ref, hsp_ref, cw_ref, wr_ref, wi_ref, lam_ref,
             du_ref, dcw_ref, dcb_ref, dlam_ref, dbr_ref, dbi_ref, dwr_ref, dwi_ref,
             a_scr, dh_scr, exth, extx, extd, dxc_scr, dz_scr, carry):
        step = pl.program_id(0)
        first_tile = step == nT - 1

        @pl.when(step == 0)
        def _():
            for ref in (dcw_ref, dcb_ref, dlam_ref, dbr_ref, dbi_ref, dwr_ref, dwi_ref, carry):
                ref[...] = jnp.zeros_like(ref)
            extd[T:T + SUBLANES, :] = jnp.zeros((SUBLANES, D), F32)

        hs = hs_ref[...]
        dy = dy_ref[...]
        g, dgelu = _gelu_and_grad(gb_ref[...])
        du_ref[:, 0:D] = (dy * hs * dgelu).astype(BF16)
        r = r_ref[...]
        lam = lam_ref[...]
        ls = _log_sigmoid(lam)
        a, a2, mult = _decay_terms(r, ls)
        a_scr[...] = a
        dh_scr[...] = dy * g

        def rstep(j, c):
            t = T - 1 - j
            d = dh_scr[pl.ds(t, 1), :] + c
            dh_scr[pl.ds(t, 1), :] = d
            return a_scr[pl.ds(t, 1), :] * d

        carry[...] = lax.fori_loop(0, T, rstep, carry[...], unroll=8)
        dh = dh_scr[...]
        keep = jnp.where(first_tile, 0.0, 1.0)
        exth[0:SUBLANES, :] = hsp_ref[...] * keep
        exth[SUBLANES:SUBLANES + T, :] = hs
        hprev = exth[pl.ds(SUBLANES - 1, T), :]
        xc = xc_ref[...]
        ig = i_ref[...]
        da = dh * hprev
        dmult = dh * (ig * xc)
        dla = da * a - dmult * (a2 / mult)
        dlam_ref[...] += jnp.sum(dla * r, axis=0, keepdims=True) * (LRU_C * _sigmoid(-lam))
        dzr = (dla * (LRU_C * ls)) * (r * (1.0 - r))
        dzi = (dh * (mult * xc)) * (ig * (1.0 - ig))
        dbr_ref[...] += jnp.sum(dzr, axis=0, keepdims=True)
        dbi_ref[...] += jnp.sum(dzi, axis=0, keepdims=True)
        dxc_scr[...] = dh * (mult * ig)
        xcb = xc.astype(BF16)
        dz_scr[0] = dzr.astype(BF16)
        dz_scr[1] = dzi.astype(BF16)
        nt_dims = (((1,), (1,)), ((), ()))
        tn_dims = (((0,), (0,)), ((), ()))
        for gq in range(nG):
            sl = slice(gq * GT, (gq + 1) * GT)
            zr_g = dz_scr[0, :, sl]
            zi_g = dz_scr[1, :, sl]
            dxc_scr[:, sl] += (lax.dot_general(zr_g, wr_ref[gq], nt_dims, preferred_element_type=F32)
                               + lax.dot_general(zi_g, wi_ref[gq], nt_dims, preferred_element_type=F32))
            dwr_ref[gq] += lax.dot_general(xcb[:, sl], zr_g, tn_dims, preferred_element_type=F32)
            dwi_ref[gq] += lax.dot_general(xcb[:, sl], zi_g, tn_dims, preferred_element_type=F32)
        dxc = dxc_scr[...]
        dcb_ref[...] += jnp.sum(dxc, axis=0, keepdims=True)
        extx[0:SUBLANES, :] = xbp_ref[...] * keep
        extx[SUBLANES:SUBLANES + T, :] = xb_ref[...]
        extd[0:T, :] = dxc
        dxb = jnp.zeros((T, D), F32)
        for k in range(W):
            dxb = dxb + extd[pl.ds(W - 1 - k, T), :] * cw_ref[k:k + 1, :]
            dcw_ref[k:k + 1, :] += jnp.sum(dxc * extx[pl.ds(SUBLANES - (W - 1) + k, T), :], axis=0, keepdims=True)
        extd[T:T + SUBLANES, :] = dxc[0:SUBLANES, :]
        du_ref[:, D:2 * D] = dxb.astype(BF16)

    rev = lambda i: nT - 1 - i
    tpb = T // SUBLANES
    prev8 = lambda i: jnp.maximum(rev(i) * tpb - 1, 0)
    row = pl.BlockSpec((T, D), lambda i: (rev(i), 0))
    vec = pl.BlockSpec((1, D), lambda i: (0, 0))
    bd = pl.BlockSpec((nG, GT, GT), lambda i: (0, 0, 0))
    vec_o = jax.ShapeDtypeStruct((1, D), F32)
    bd_o = jax.ShapeDtypeStruct((nG, GT, GT), F32)
    return _hosted_call(
        body, "lru_bwd", (nT,),
        [row, row, pl.BlockSpec((T, D), lambda i: (rev(i), 1)), pl.BlockSpec((SUBLANES, D), lambda i: (prev8(i), 1)),
         row, row, row, row, pl.BlockSpec((SUBLANES, D), lambda i: (prev8(i), 0)),
         pl.BlockSpec((W, D), lambda i: (0, 0)), bd, bd, vec],
        [pl.BlockSpec((T, 2 * D), lambda i: (rev(i), 0)), pl.BlockSpec((W, D), lambda i: (0, 0)),
         vec, vec, vec, vec, bd, bd],
        [jax.ShapeDtypeStruct((S, 2 * D), BF16), jax.ShapeDtypeStruct((W, D), F32), vec_o, vec_o, vec_o, vec_o, bd_o, bd_o],
        [pltpu.VMEM((T, D), F32), pltpu.VMEM((T, D), F32), pltpu.VMEM((T + SUBLANES, D), F32),
         pltpu.VMEM((T + SUBLANES, D), F32), pltpu.VMEM((T + SUBLANES, D), F32),
         pltpu.VMEM((T, D), F32), pltpu.VMEM((2, T, D), BF16), pltpu.VMEM((1, D), F32)],
        (dy, u0, u0, u0, xc, r, ig, hs, hs, conv_w, wr_bd, wi_bd, lam), ("arbitrary",), plan)


AUG_ROWS = 16
HEAD_ROWS = 128
LSE_ROW = HEAD_DIM + 6
ONES_ROW_Q = HEAD_DIM + 3
ONES_COL_K = HEAD_DIM
ONES_ROW_V = HEAD_DIM


def _split3(x):
    b1 = x.astype(BF16).astype(F32)
    r = x - b1
    b2 = r.astype(BF16).astype(F32)
    return b1, b2, r - b2


def _head_block(x, aug, T):
    row = lax.broadcasted_iota(jnp.int32, (AUG_ROWS, T), 0)
    blk = jnp.zeros((AUG_ROWS, T), F32)
    for i, e in enumerate(aug):
        blk = jnp.where(row == i, e, blk)
    return jnp.concatenate([x, blk, jnp.zeros((HEAD_ROWS - HEAD_DIM - AUG_ROWS, T), F32)], axis=0)


def _tri_matrix(lower):
    i = np.arange(LANES)
    m = (i[:, None] >= i[None, :]) if lower else (i[:, None] <= i[None, :])
    return jnp.asarray(m.astype(np.float32), BF16)


def _lane_cumsum(x, tri_ref, carry, reverse):
    n = x.shape[1] // LANES
    tri = tri_ref[...]
    out = [None] * n
    for j in (range(n - 1, -1, -1) if reverse else range(n)):
        cs = carry
        for part in _split3(x[:, j * LANES:(j + 1) * LANES]):
            cs = cs + jnp.dot(part.astype(BF16), tri, preferred_element_type=F32)
        out[j] = cs
        carry = cs[:, 0:1] if reverse else cs[:, LANES - 1:LANES]
    return jnp.concatenate(out, axis=1), carry


def _head_rows(h):
    return pl.ds(pl.multiple_of(h * HEAD_DIM, HEAD_DIM), HEAD_DIM)


def _fox_prep(ut, b_f, qg, kg, S, D, tq):
    H = D // HEAD_DIM
    T = min(tq, 256)
    per = tq // T
    scale = HEAD_DIM ** -0.5

    def body(q_ref, k_ref, v_ref, f_ref, bf_ref, qg_ref, kg_ref, tri_ref,
             qat_ref, kat_ref, vat_ref, ka_ref, c_scr, ccar):
        @pl.when(pl.program_id(0) == 0)
        def _():
            ccar[...] = jnp.zeros_like(ccar)

        c, carry = _lane_cumsum(_log_sigmoid(f_ref[...] + bf_ref[...]), tri_ref, ccar[...], False)
        c_scr[...] = c
        ccar[...] = carry

        def head(h, _):
            rows = _head_rows(h)
            c1, c2, c3 = _split3(c_scr[pl.ds(h, 1), :])

            def normed(src, gain, mul):
                x = src[rows, :]
                rs = lax.rsqrt(jnp.mean(x * x, axis=0, keepdims=True) + EPS)
                return ((x * rs) * gain[rows, :]) * mul

            qat_ref[h] = _head_block(normed(q_ref, qg_ref, scale), [c1, c2, c3, 1.0, 1.0, 1.0], T).astype(BF16)
            kb = _head_block(normed(k_ref, kg_ref, 1.0), [1.0, 1.0, 1.0, -c1, -c2, -c3, 1.0, 1.0, 1.0], T)
            kat_ref[h] = kb.astype(BF16)
            ka_ref[h] = kb.T.astype(BF16)
            vat_ref[h] = _head_block(v_ref[rows, :], [1.0, 1.0, 1.0], T).astype(BF16)
            return 0

        lax.fori_loop(0, H, head, 0)

    part = lambda j: pl.BlockSpec((D, T), lambda i: (j, i))
    colv = lambda n: pl.BlockSpec((n, 1), lambda i: (0, 0))
    tmaj = lambda r: pl.BlockSpec((H, None, r, T), lambda i: (0, i // per, 0, i % per))
    norm = pl.BlockSpec((H, T, HEAD_ROWS), lambda i: (0, i, 0))
    tshape = lambda r: jax.ShapeDtypeStruct((H, S // tq, r, tq), BF16)
    nshape = jax.ShapeDtypeStruct((H, S, HEAD_ROWS), BF16)
    return pl.pallas_call(
        body, name="fox_prep", grid=(S // T,),
        in_specs=[part(0), part(1), part(2), pl.BlockSpec((LANES, T), lambda i: (3 * D // LANES, i)),
                  colv(LANES), colv(D), colv(D), pl.BlockSpec((LANES, LANES), lambda i: (0, 0))],
        out_specs=[tmaj(HEAD_ROWS), tmaj(HEAD_ROWS), tmaj(HEAD_ROWS), norm],
        out_shape=[tshape(HEAD_ROWS), tshape(HEAD_ROWS), tshape(HEAD_ROWS), nshape],
        scratch_shapes=[pltpu.VMEM((LANES, T), F32), pltpu.VMEM((LANES, 1), F32)],
        compiler_params=_params(("arbitrary",)),
    )(ut, ut, ut, ut, b_f, qg, kg, _tri_matrix(False))


def _fox_bwd_prep(dot, ot, lse, qat, S, D, tq):
    H = D // HEAD_DIM
    T = min(tq, 256)
    per = tq // T

    def body(do_ref, o_ref, lse_ref, qat_ref, doat_ref, doa_ref, qat1_ref, qa1_ref):
        row = lax.broadcasted_iota(jnp.int32, (HEAD_ROWS, T), 0)

        def head(h, _):
            rows = _head_rows(h)
            do = do_ref[rows, :].astype(F32)
            delta = jnp.sum(do * o_ref[rows, :], axis=0, keepdims=True)
            db = _head_block(do, list(_split3(-delta)), T)
            doat_ref[h] = db.astype(BF16)
            doa_ref[h] = db.T.astype(BF16)
            qb = qat_ref[h].astype(F32)
            for i, e in enumerate(_split3(-lse_ref[h])):
                qb = jnp.where(row == LSE_ROW + i, e, qb)
            qat1_ref[h] = qb.astype(BF16)
            qa1_ref[h] = qb.T.astype(BF16)
            return 0

        lax.fori_loop(0, H, head, 0)

    chan = pl.BlockSpec((D, T), lambda i: (0, i))
    tmaj = pl.BlockSpec((H, None, HEAD_ROWS, T), lambda i: (0, i // per, 0, i % per))
    norm = pl.BlockSpec((H, T, HEAD_ROWS), lambda i: (0, i, 0))
    tshape = jax.ShapeDtypeStruct((H, S // tq, HEAD_ROWS, tq), BF16)
    nshape = jax.ShapeDtypeStruct((H, S, HEAD_ROWS), BF16)
    return pl.pallas_call(
        body, name="fox_bwd_prep", grid=(S // T,),
        in_specs=[chan, chan, pl.BlockSpec((H, 1, T), lambda i: (0, 0, i)), tmaj],
        out_specs=[tmaj, norm, tmaj, norm], out_shape=[tshape, nshape, tshape, nshape],
        compiler_params=_params(("arbitrary",)),
    )(dot, ot, lse, qat)


def _causal(s, k_axis):
    ki = lax.broadcasted_iota(jnp.int32, s.shape, k_axis)
    qi = lax.broadcasted_iota(jnp.int32, s.shape, 1 - k_axis)
    return jnp.where(ki <= qi, s, NEG_INF)


def _seq_tile(i, t):
    return pl.ds(pl.multiple_of(i * t, t), t)


def _attn_forward(ka, qat, vat, S, D, tq, plan=None):
    H = D // HEAD_DIM
    nq = S // tq

    def body(ka_ref, qat_ref, vat_ref, o_ref, o32_ref, lse_ref, m_scr, acc_scr):
        qi = pl.program_id(1)
        m_scr[...] = jnp.full_like(m_scr, NEG_INF)
        acc_scr[...] = jnp.zeros_like(acc_scr)
        qa = qat_ref[...]

        def span(k0, n, diagonal):
            s = jnp.dot(ka_ref[pl.ds(pl.multiple_of(k0 * tq, tq), n * tq), :], qa, preferred_element_type=F32)
            if diagonal:
                s = _causal(s, 0)
            m_prev = m_scr[...]
            m_new = jnp.maximum(m_prev, jnp.max(s, axis=0, keepdims=True))
            p = jnp.exp(s - m_new).astype(BF16)
            upd = jnp.dot(vat_ref[k0], p[0:tq], preferred_element_type=F32)
            for i in range(1, n):
                upd = upd + jnp.dot(vat_ref[k0 + i], p[i * tq:(i + 1) * tq], preferred_element_type=F32)
            acc_scr[...] = jnp.exp(m_prev - m_new) * acc_scr[...] + upd
            m_scr[...] = m_new

        def off_diagonal_pair(j, _):
            span(2 * j, 2, False)
            return 0

        lax.fori_loop(0, qi // 2, off_diagonal_pair, 0)
        pl.when(qi % 2 == 1)(lambda: span(qi - 1, 1, False))
        span(qi, 1, True)
        l = acc_scr[ONES_ROW_V:ONES_ROW_V + 1, :]
        o = acc_scr[0:HEAD_DIM, :] / l
        o_ref[...] = o.astype(BF16)
        o32_ref[...] = o
        lse_ref[...] = m_scr[...] + jnp.log(l)

    chan = pl.BlockSpec((HEAD_DIM, tq), lambda h, i: (h, i))
    stat = pl.BlockSpec((None, 1, tq), lambda h, i: (h, 0, i))
    return _hosted_call(
        body, "attn_forward", (H, nq),
        [pl.BlockSpec((None, S, HEAD_ROWS), lambda h, i: (h, 0, 0)),
         pl.BlockSpec((None, None, HEAD_ROWS, tq), lambda h, i: (h, i, 0, 0)),
         pl.BlockSpec((None, nq, HEAD_ROWS, tq), lambda h, i: (h, 0, 0, 0))],
        [chan, chan, stat],
        [jax.ShapeDtypeStruct((D, S), BF16), jax.ShapeDtypeStruct((D, S), F32), jax.ShapeDtypeStruct((H, 1, S), F32)],
        [pltpu.VMEM((1, tq), F32), pltpu.VMEM((HEAD_ROWS, tq), F32)],
        (ka, qat, vat), ("arbitrary", "arbitrary"), plan)


def _attn_backward(qa, doa, qat, doat, ka, kat, vat, S, D, tq, plan=None):
    H = D // HEAD_DIM
    nq = S // tq

    def body(qa_ref, doa_ref, qat_ref, doat_ref, ka_ref, kat_ref, vat_ref, dq_ref, dk_ref, dv_ref, dk_scr, dv_scr):
        ki = pl.program_id(1)

        @pl.when(ki == 0)
        def _():
            dq_ref[...] = jnp.zeros_like(dq_ref)

        dk_scr[...] = jnp.zeros_like(dk_scr)
        dv_scr[...] = jnp.zeros_like(dv_scr)
        kt = kat_ref[...]
        vt = vat_ref[...]
        kn = ka_ref[...]

        def span(q0, n, diagonal):
            rows = pl.ds(pl.multiple_of(q0 * tq, tq), n * tq)
            s = jnp.dot(qa_ref[rows, :], kt, preferred_element_type=F32)
            if diagonal:
                s = _causal(s, 1)
            p = jnp.exp(s)
            ds = (p * jnp.dot(doa_ref[rows, :], vt, preferred_element_type=F32)).astype(BF16)
            p = p.astype(BF16)
            for i in range(n):
                part = slice(i * tq, (i + 1) * tq)
                dv_scr[...] += jnp.dot(doat_ref[q0 + i, 0:HEAD_DIM, :], p[part], preferred_element_type=F32)
                dk_scr[...] += jnp.dot(qat_ref[q0 + i], ds[part], preferred_element_type=F32)
            dq_ref[rows, :] += jnp.dot(ds, kn, preferred_element_type=F32)

        def off_diagonal_pair(j, _):
            span(ki + 1 + 2 * j, 2, False)
            return 0

        span(ki, 1, True)
        n_off = nq - 1 - ki
        lax.fori_loop(0, n_off // 2, off_diagonal_pair, 0)
        pl.when(n_off % 2 == 1)(lambda: span(nq - 1, 1, False))
        dk_ref[...] = dk_scr[...]
        dv_ref[...] = dv_scr[...].astype(BF16)

    whole = pl.BlockSpec((None, S, HEAD_ROWS), lambda h, i: (h, 0, 0))
    tiles = pl.BlockSpec((None, nq, HEAD_ROWS, tq), lambda h, i: (h, 0, 0, 0))
    one = pl.BlockSpec((None, None, HEAD_ROWS, tq), lambda h, i: (h, i, 0, 0))
    return _hosted_call(
        body, "attn_backward", (H, nq),
        [whole, whole, tiles, tiles, pl.BlockSpec((None, tq, HEAD_ROWS), lambda h, i: (h, i, 0)), one, one],
        [whole, pl.BlockSpec((None, HEAD_ROWS, tq), lambda h, i: (h, 0, i)),
         pl.BlockSpec((HEAD_DIM, tq), lambda h, i: (h, i))],
        [jax.ShapeDtypeStruct((H, S, HEAD_ROWS), F32), jax.ShapeDtypeStruct((H, HEAD_ROWS, S), F32),
         jax.ShapeDtypeStruct((D, S), BF16)],
        [pltpu.VMEM((HEAD_ROWS, tq), F32), pltpu.VMEM((HEAD_DIM, tq), F32)],
        (qa, doa, qat, doat, ka, kat, vat), ("arbitrary", "arbitrary"), plan)


def _fox_prep_bwd(ut, dq, dkt, dvt, b_f, qg, kg, S, D, tq):
    H = D // HEAD_DIM
    T = min(tq, 256)
    nT = S // T
    NU = 3 * D + LANES
    scale = HEAD_DIM ** -0.5

    def body(q_ref, k_ref, f_ref, dq_ref, dk_ref, dv_ref, bf_ref, qg_ref, kg_ref, tri_ref,
             du_ref, dbf_ref, dqg_ref, dkg_ref, gq_acc, gk_acc, fcar, dc_scr):
        step = pl.program_id(0)

        @pl.when(step == 0)
        def _():
            for ref in (gq_acc, gk_acc, fcar, dbf_ref):
                ref[...] = jnp.zeros_like(ref)

        dc_scr[...] = jnp.zeros_like(dc_scr)

        def head(h, _):
            rows = _head_rows(h)
            dqb = dq_ref[h].T
            dkb = dk_ref[h]
            dc_scr[pl.ds(h, 1), :] = dqb[ONES_COL_K:ONES_COL_K + 1, :] - dkb[ONES_ROW_Q:ONES_ROW_Q + 1, :]
            for src, dsrc, gain, acc, mul, base in ((q_ref, dqb, qg_ref, gq_acc, scale, 0),
                                                    (k_ref, dkb, kg_ref, gk_acc, 1.0, D)):
                x = src[rows, :]
                rs = lax.rsqrt(jnp.mean(x * x, axis=0, keepdims=True) + EPS)
                xhat = x * rs
                dn = dsrc[0:HEAD_DIM, :] * mul
                acc[rows, :] += jnp.sum(dn * xhat, axis=1, keepdims=True)
                dxh = dn * gain[rows, :]
                dx = rs * (dxh - xhat * jnp.mean(dxh * xhat, axis=0, keepdims=True))
                du_ref[pl.ds(pl.multiple_of(base + h * HEAD_DIM, HEAD_DIM), HEAD_DIM), :] = dx.astype(BF16)
            return 0

        lax.fori_loop(0, H, head, 0)
        du_ref[2 * D:3 * D, :] = dv_ref[...]
        dlf, carry = _lane_cumsum(dc_scr[...], tri_ref, fcar[...], True)
        fcar[...] = carry
        dfl = dlf * _sigmoid(-(f_ref[...] + bf_ref[...]))
        dbf_ref[...] += jnp.sum(dfl, axis=1, keepdims=True)
        du_ref[3 * D:NU, :] = dfl.astype(BF16)

        @pl.when(step == nT - 1)
        def _():
            for acc, ref in ((gq_acc, dqg_ref), (gk_acc, dkg_ref)):
                tot = jnp.zeros((HEAD_DIM, 1), F32)
                for h in range(H):
                    tot = tot + acc[h * HEAD_DIM:(h + 1) * HEAD_DIM, :]
                ref[...] = tot

    rev = lambda i: nT - 1 - i
    part = lambda j: pl.BlockSpec((D, T), lambda i: (j, rev(i)))
    colv = lambda n: pl.BlockSpec((n, 1), lambda i: (0, 0))
    return pl.pallas_call(
        body, name="fox_prep_bwd", grid=(nT,),
        in_specs=[part(0), part(1), pl.BlockSpec((LANES, T), lambda i: (3 * D // LANES, rev(i))),
                  pl.BlockSpec((H, T, HEAD_ROWS), lambda i: (0, rev(i), 0)),
                  pl.BlockSpec((H, HEAD_ROWS, T), lambda i: (0, 0, rev(i))), pl.BlockSpec((D, T), lambda i: (0, rev(i))),
                  colv(LANES), colv(D), colv(D), pl.BlockSpec((LANES, LANES), lambda i: (0, 0))],
        out_specs=[pl.BlockSpec((NU, T), lambda i: (0, rev(i))), colv(LANES), colv(HEAD_DIM), colv(HEAD_DIM)],
        out_shape=[jax.ShapeDtypeStruct((NU, S), BF16), jax.ShapeDtypeStruct((LANES, 1), F32),
                   jax.ShapeDtypeStruct((HEAD_DIM, 1), F32), jax.ShapeDtypeStruct((HEAD_DIM, 1), F32)],
        scratch_shapes=[pltpu.VMEM((D, 1), F32), pltpu.VMEM((D, 1), F32), pltpu.VMEM((LANES, 1), F32),
                        pltpu.VMEM((LANES, T), F32)],
        compiler_params=_params(("arbitrary",)),
    )(ut, ut, ut, dq, dkt, dvt, b_f, qg, kg, _tri_matrix(True))


def _block_diag_tiles(w):
    n = w.shape[0]
    per = min(MXU_DIM, n * LRU_BLOCK_DIM) // LRU_BLOCK_DIM
    eye = jnp.eye(per, dtype=w.dtype)
    w5 = w.reshape(n // per, per, LRU_BLOCK_DIM, 1, LRU_BLOCK_DIM) * eye[None, :, None, :, None]
    return w5.reshape(n // per, per * LRU_BLOCK_DIM, per * LRU_BLOCK_DIM).astype(BF16)


def _block_diag_extract(t, n):
    per = t.shape[-1] // LRU_BLOCK_DIM
    eye = jnp.eye(per, dtype=t.dtype)
    t5 = t.reshape(n // per, per, LRU_BLOCK_DIM, per, LRU_BLOCK_DIM) * eye[None, :, None, :, None]
    return t5.sum(axis=3).reshape(n, LRU_BLOCK_DIM, LRU_BLOCK_DIM)


def _local_step(x, tgt, small, wv, grad_view, comm=None):
    S, D = x.shape
    F = 4 * D
    H = D // HEAD_DIM
    nblk = D // LRU_BLOCK_DIM
    NU = 3 * D + LANES
    tq = max(LANES, min(512, S // 4))
    assert S % tq == 0
    vec = lambda a: a.reshape(1, -1).astype(F32)
    col = lambda a: a.reshape(-1, 1).astype(F32)
    mix_g, mlp_g = small["mix_norm"], small["mlp_norm"]
    conv_w, conv_b = small["conv_w"], vec(small["lru_conv_b"])
    wr_bd, wi_bd = _block_diag_tiles(small["lru_w_r"][0]), _block_diag_tiles(small["lru_w_i"][0])
    b_r, b_i, lam = vec(small["lru_b_r"]), vec(small["lru_b_i"]), vec(small["lru_lambda"])
    b_f = jnp.pad(col(small["fox_b_f"]), ((0, LANES - H), (0, 0)))
    qg, kg = jnp.tile(col(small["fox_q_gain"]), (H, 1)), jnp.tile(col(small["fox_k_gain"]), (H, 1))
    X = lambda a: _View(a)
    grads = {}
    gout = functools.partial(grad_view, grads)

    def hosted(name, fn, *args):
        plan = comm.before(name, grads) if comm is not None else None
        res, side = fn(*args, plan=plan)
        if plan is not None:
            comm.after(name, side, wv)
        return res

    def hosted_mm(name, *args, **kw):
        plan = comm.before(name, grads) if comm is not None else None
        if plan is None:
            return _matmul(name, *args, **kw)
        res, side = _matmul(name, *args, plan=plan, **kw)
        comm.after(name, side, wv)
        return res

    norm_rows = _pick(S, (512, 256, 128))
    two = lambda: [_fresh(S, D, F32), _fresh(S, D, BF16)]

    def mlp_up(l, hm):
        return hosted_mm(f"mlp{l}_up", X(hm), wv[f"w1_{l}"], S, F, D, outs=[_fresh(S, F, BF16), _fresh(S, F, BF16)],
                         epilogue=_ep_relu2)

    def mlp_bwd(l, xin, hm, z, act, d, db):
        (dz,) = hosted_mm(f"mlp{l}_dact", X(db), wv[f"w2_{l}"], S, F, D, tb=True, outs=[_fresh(S, F, BF16)],
                          epilogue=_ep_drelu2, extras=[X(z)])
        (grads[f"w2_{l}"],) = _matmul(f"mlp{l}_dw2", X(act), X(db), F, D, S, ta=True, outs=[gout(f"w2_{l}")],
                                      epilogue=_ep_store)
        (grads[f"w1_{l}"],) = _matmul(f"mlp{l}_dw1", X(hm), X(dz), D, F, S, ta=True, outs=[gout(f"w1_{l}")],
                                      epilogue=_ep_store)
        return _matmul(f"mlp{l}_dhm", X(dz), wv[f"w1_{l}"], S, D, F, tb=True, outs=two(), n_sums=1,
                       epilogue=_ep_norm_bwd, extras=[X(xin), X(d)], vecs=[mlp_g[l:l + 1]], tm=norm_rows)

    h0 = _rms_fwd("mix0_norm", x, mix_g[0:1], S, D)
    (u0,) = _matmul("lru_in", X(h0), wv["lru_in"], S, 2 * D, D, outs=[_fresh(S, 2 * D, F32)], epilogue=_ep_store)
    y, xc, r, ig, hs = hosted("lru_fwd", _lru_fwd, u0, conv_w, conv_b, wr_bd, b_r, wi_bd, b_i, lam, S, D)
    x1, hm0 = _matmul("lru_out", X(y), wv["lru_out"], S, D, D, outs=two(), epilogue=_ep_resid_norm, extras=[X(x)],
                      vecs=[mlp_g[0:1]], tm=norm_rows)
    z0, act0 = mlp_up(0, hm0)
    x2, h1 = hosted_mm("mlp0_down", X(act0), wv["w2_0"], S, D, F, outs=two(), epilogue=_ep_resid_norm, extras=[X(x1)],
                       vecs=[mix_g[1:2]], tm=norm_rows)
    (u1,) = _matmul("fox_in", wv["fox_in"], X(h1), NU, S, D, tb=True, outs=[_fresh(NU, S, F32)], epilogue=_ep_store)
    qat, kat, vat, ka = _fox_prep(u1, b_f, qg, kg, S, D, tq)
    o, o32, lse = hosted("attn_forward", _attn_forward, ka, qat, vat, S, D, tq)
    x3, hm1 = _matmul("fox_out", X(o), wv["fox_out"], S, D, D, ta=True, outs=two(), epilogue=_ep_resid_norm,
                      extras=[X(x2)], vecs=[mlp_g[1:2]], tm=norm_rows)
    z1, act1 = mlp_up(1, hm1)
    (x4,) = _matmul("mlp1_down", X(act1), wv["w2_1"], S, D, F, outs=[_fresh(S, D, F32)], epilogue=_ep_resid,
                    extras=[X(x3)])
    loss, d4, d4b = _loss_head(x4, tgt, S, D)

    d3, d3b, dg_mlp1 = mlp_bwd(1, x3, hm1, z1, act1, d4, d4b)
    (do,) = _matmul("fox_dout", wv["fox_out"], X(d3b), D, S, D, tb=True, outs=[_fresh(D, S, BF16)], epilogue=_ep_store)
    (grads["fox_out"],) = _matmul("fox_dwout", X(o), X(d3b), D, D, S, outs=[gout("fox_out")], epilogue=_ep_store)
    doat, doa, qat1, qa1 = _fox_bwd_prep(do, o32, lse, qat, S, D, tq)
    dqn, dkn, dv = hosted("attn_backward", _attn_backward, qa1, doa, qat1, doat, ka, kat, vat, S, D, tq)
    du1, dbf, dqg, dkg = _fox_prep_bwd(u1, dqn, dkn, dv, b_f, qg, kg, S, D, tq)
    (grads["fox_in"],) = _matmul("fox_dwin", X(h1), X(du1), D, NU, S, ta=True, tb=True, outs=[gout("fox_in")],
                                 epilogue=_ep_store)
    d2, d2b, dg_mix1 = _matmul("fox_dh", X(du1), wv["fox_in"], S, D, NU, ta=True, outs=two(), n_sums=1,
                               epilogue=_ep_norm_bwd, extras=[X(x2), X(d3)], vecs=[mix_g[1:2]], tm=norm_rows)
    d1, d1b, dg_mlp0 = mlp_bwd(0, x1, hm0, z0, act0, d2, d2b)
    (dy,) = _matmul("lru_dout", X(d1b), wv["lru_out"], S, D, D, tb=True, outs=[_fresh(S, D, F32)], epilogue=_ep_store)
    (grads["lru_out"],) = _matmul("lru_dwout", X(y), X(d1b), D, D, S, ta=True, outs=[gout("lru_out")],
                                  epilogue=_ep_store)
    du0, dcw, dcb, dlam, dbr, dbi, dwr, dwi = hosted("lru_bwd", _lru_bwd, dy, u0, xc, r, ig, hs, conv_w, wr_bd, wi_bd,
                                                     lam, S, D)
    (grads["lru_in"],) = _matmul("lru_dwin", X(h0), X(du0), D, 2 * D, S, ta=True, outs=[gout("lru_in")],
                                 epilogue=_ep_store)
    gx, dg_mix0 = _matmul("lru_dh", X(du0), wv["lru_in"], S, D, 2 * D, tb=True, outs=[_fresh(S, D, F32)], n_sums=1,
                          epilogue=lambda *a: _ep_norm_bwd(*a)[::2], extras=[X(x), X(d1)], vecs=[mix_g[0:1]],
                          tm=norm_rows)

    grads.update(
        mix_norm=jnp.concatenate([dg_mix0, dg_mix1], axis=0), mlp_norm=jnp.concatenate([dg_mlp0, dg_mlp1], axis=0),
        conv_w=dcw, lru_conv_b=dcb, lru_w_r=_block_diag_extract(dwr, nblk)[None], lru_b_r=dbr.reshape(1, nblk, -1),
        lru_w_i=_block_diag_extract(dwi, nblk)[None], lru_b_i=dbi.reshape(1, nblk, -1), lru_lambda=dlam,
        fox_b_f=dbf[:H].reshape(1, H), fox_q_gain=dqg.reshape(1, -1), fox_k_gain=dkg.reshape(1, -1))
    return loss, gx, grads


def _place():
    x, y, c = lax.axis_index("x"), lax.axis_index("y"), lax.axis_index("c")
    chips = [(1 - x, y), (x, 1 - y), (1 - x, 1 - y)]
    return x, y, c, 2 * x + y, chips


BOUNCE_BYTES = 1 << 20


def _bounce_shape(rows, cols, dtype):
    chunk = rows
    while chunk % 2 == 0 and chunk > 16 and chunk * cols * jnp.dtype(dtype).itemsize > BOUNCE_BYTES:
        chunk //= 2
    return pltpu.VMEM((2, chunk, cols), dtype)


def _bounce_copy(src, dst, buf, sem):
    chunk = buf.shape[1]
    n = src.shape[0] // chunk
    cin = lambda i: pltpu.make_async_copy(src.at[pl.ds(i * chunk, chunk)], buf.at[i % 2], sem.at[i % 2])
    cout = lambda i: pltpu.make_async_copy(buf.at[i % 2], dst.at[pl.ds(i * chunk, chunk)], sem.at[2 + i % 2])
    cin(0).start()
    for i in range(n):
        cin(i).wait()
        if i + 1 < n:
            if i >= 1:
                cout(i - 1).wait()
            cin(i + 1).start()
        cout(i).start()
    if n >= 2:
        cout(n - 2).wait()
    cout(n - 1).wait()


def _hbm_call(body, name, arrays, out_shape, n_dma_sems, bounce=()):
    scratch = [pltpu.SemaphoreType.DMA((k,)) for k in n_dma_sems]
    for rows, cols, dtype in bounce:
        scratch += [_bounce_shape(rows, cols, dtype), pltpu.SemaphoreType.DMA((4,))]
    return pl.pallas_call(
        body, name=name, in_specs=[ANY] * len(arrays), out_specs=[ANY] * len(out_shape), out_shape=out_shape,
        scratch_shapes=scratch,
        compiler_params=pltpu.CompilerParams(has_side_effects=True, vmem_limit_bytes=VMEM_LIMIT),
    )(*arrays)


class _Gather:
    def __init__(self, shards):
        n = self.n = len(shards)
        self.operands = list(shards)
        self.out_shape = [jax.ShapeDtypeStruct((N_CHIPS,) + tuple(a.shape), a.dtype) for a in shards]
        self.scratch = [pltpu.SemaphoreType.DMA((3 * n,)) for _ in range(4)]
        for a in shards:
            self.scratch += [_bounce_shape(a.shape[0], a.shape[1], a.dtype), pltpu.SemaphoreType.DMA((4,))]

    def _copies(self, ins, outs, scr):
        send, recv, fsend, frecv = scr[:4]
        x, y, c, s, chips = _place()

        def rows(a, chip_idx, which):
            hr = ins[a].shape[0] // 2
            return outs[a].at[chip_idx, pl.ds(which * hr, hr)]

        def landed(a, j, core):
            return rows(a, 2 * chips[j][0] + chips[j][1], core)

        def ici(a, j, mine):
            hr = ins[a].shape[0] // 2
            src, dst = (ins[a].at[pl.ds(c * hr, hr)], rows(a, s, c)) if mine else (landed(a, j, c),) * 2
            return pltpu.make_async_remote_copy(src_ref=src, dst_ref=dst, send_sem=send.at[3 * a + j],
                                                recv_sem=recv.at[3 * a + j], device_id=(*chips[j], c),
                                                device_id_type=MESH)

        def d2d(a, j, mine):
            ref = landed(a, j, c if mine else 1 - c)
            return pltpu.make_async_remote_copy(src_ref=ref, dst_ref=ref, send_sem=fsend.at[3 * a + j],
                                                recv_sem=frecv.at[3 * a + j], device_id=(x, y, 1 - c),
                                                device_id_type=MESH)

        return ici, d2d, s

    def start(self, ins, outs, scr):
        ici, _, _ = self._copies(ins, outs, scr)
        for a in range(self.n):
            for j in range(3):
                ici(a, j, True).start()

    def middle(self, ins, outs, scr):
        ici, d2d, s = self._copies(ins, outs, scr)
        for a in range(self.n):
            _bounce_copy(ins[a], outs[a].at[s], scr[4 + 2 * a], scr[5 + 2 * a])
        for a in range(self.n):
            for j in range(3):
                ici(a, j, False).wait_recv()
                d2d(a, j, True).start()

    def finish(self, ins, outs, scr):
        ici, d2d, _ = self._copies(ins, outs, scr)
        for a in range(self.n):
            for j in range(3):
                d2d(a, j, False).wait_recv()
        for a in range(self.n):
            for j in range(3):
                ici(a, j, True).wait_send()
                d2d(a, j, True).wait_send()


def _run_plan(name, plan):
    k_in, k_out = len(plan.operands), len(plan.out_shape)

    def body(*refs):
        parts = (refs[:k_in], refs[k_in:k_in + k_out], refs[k_in + k_out:])
        plan.start(*parts)
        plan.middle(*parts)
        plan.finish(*parts)

    return pl.pallas_call(
        body, name=name, in_specs=[ANY] * k_in, out_specs=[ANY] * k_out, out_shape=plan.out_shape,
        scratch_shapes=plan.scratch,
        compiler_params=pltpu.CompilerParams(has_side_effects=True, vmem_limit_bytes=VMEM_LIMIT),
    )(*plan.operands)


def _hosted_call(body, name, grid, in_specs, out_specs, out_shape, scratch_shapes, operands, sem, plan=None):
    if plan is None:
        res = pl.pallas_call(body, name=name, grid=grid, in_specs=in_specs, out_specs=out_specs, out_shape=out_shape,
                             scratch_shapes=scratch_shapes, compiler_params=_params(sem))(*operands)
        return res, None
    n_in, n_out, n_scr = len(in_specs), len(out_specs), len(scratch_shapes)
    k_in, k_out = len(plan.operands), len(plan.out_shape)
    total = int(np.prod(grid))

    def hosted(*refs):
        ins, refs = refs[:n_in], refs[n_in:]
        p_ins, refs = refs[:k_in], refs[k_in:]
        outs, refs = refs[:n_out], refs[n_out:]
        p_outs, refs = refs[:k_out], refs[k_out:]
        scr, p_scr = refs[:n_scr], refs[n_scr:]
        step = pl.program_id(0)
        for d in range(1, len(grid)):
            step = step * grid[d] + pl.program_id(d)
        pl.when(step == 0)(lambda: plan.start(p_ins, p_outs, p_scr))
        body(*ins, *outs, *scr)
        pl.when(step == total // 2)(lambda: plan.middle(p_ins, p_outs, p_scr))
        pl.when(step == total - 1)(lambda: plan.finish(p_ins, p_outs, p_scr))

    res = pl.pallas_call(
        hosted, name=name, grid=grid, in_specs=list(in_specs) + [ANY] * k_in, out_specs=list(out_specs) + [ANY] * k_out,
        out_shape=list(out_shape) + plan.out_shape, scratch_shapes=list(scratch_shapes) + plan.scratch,
        compiler_params=pltpu.CompilerParams(dimension_semantics=sem, vmem_limit_bytes=VMEM_LIMIT,
                                             has_side_effects=True),
    )(*operands, *plan.operands)
    return res[:n_out], res[n_out:]


def _all_gather(name, shards):
    return _run_plan(name, _Gather(shards))


def _pair_swap(name, arrs):
    n = len(arrs)

    def body(*refs):
        ins, outs = refs[:n], refs[n:2 * n]
        send, recv = refs[2 * n:]
        x, y, c, _, _ = _place()
        cps = []
        for a in range(n):
            hr = ins[a].shape[1] // 2
            cp = pltpu.make_async_remote_copy(
                src_ref=ins[a].at[:, pl.ds((1 - c) * hr, hr)], dst_ref=outs[a], send_sem=send.at[a],
                recv_sem=recv.at[a], device_id=(x, y, 1 - c), device_id_type=MESH)
            cp.start()
            cps.append(cp)
        for cp in cps:
            cp.wait()

    out_shape = [jax.ShapeDtypeStruct((a.shape[0], a.shape[1] // 2, a.shape[2]), a.dtype) for a in arrs]
    return _hbm_call(body, name, arrs, out_shape, (n, n))


class _Scatter:
    def __init__(self, parts):
        n = self.n = len(parts)
        self.operands = list(parts)
        self.out_shape = [jax.ShapeDtypeStruct(a.shape, a.dtype) for a in parts]
        self.scratch = [pltpu.SemaphoreType.DMA((3 * n,)) for _ in range(2)]
        for a in parts:
            self.scratch += [_bounce_shape(a.shape[1], a.shape[2], a.dtype), pltpu.SemaphoreType.DMA((4,))]

    def _copy(self, ins, outs, scr, a, j, mine):
        x, y, c, s, chips = _place()
        t = 2 * chips[j][0] + chips[j][1]
        return pltpu.make_async_remote_copy(
            src_ref=ins[a].at[t], dst_ref=outs[a].at[s if mine else t], send_sem=scr[0].at[3 * a + j],
            recv_sem=scr[1].at[3 * a + j], device_id=(*chips[j], c), device_id_type=MESH)

    def start(self, ins, outs, scr):
        for a in range(self.n):
            for j in range(3):
                self._copy(ins, outs, scr, a, j, True).start()

    def middle(self, ins, outs, scr):
        s = _place()[3]
        for a in range(self.n):
            _bounce_copy(ins[a].at[s], outs[a].at[s], scr[2 + 2 * a], scr[3 + 2 * a])

    def finish(self, ins, outs, scr):
        for a in range(self.n):
            for j in range(3):
                self._copy(ins, outs, scr, a, j, False).wait_recv()
        for a in range(self.n):
            for j in range(3):
                self._copy(ins, outs, scr, a, j, True).wait_send()


def _pair_gather(name, halves):
    n = len(halves)

    def body(*refs):
        ins, outs = refs[:n], refs[n:2 * n]
        send, recv = refs[2 * n:2 * n + 2]
        stage = refs[2 * n + 2:]
        x, y, c, _, _ = _place()
        cps = []
        for a in range(n):
            hr = ins[a].shape[0]
            cp = pltpu.make_async_remote_copy(
                src_ref=ins[a], dst_ref=outs[a].at[pl.ds(c * hr, hr)], send_sem=send.at[a], recv_sem=recv.at[a],
                device_id=(x, y, 1 - c), device_id_type=MESH)
            cp.start()
            cps.append((cp, hr))
        for a, (cp, hr) in enumerate(cps):
            _bounce_copy(ins[a], outs[a].at[pl.ds(c * hr, hr)], stage[2 * a], stage[2 * a + 1])
        for a, (cp, hr) in enumerate(cps):
            cp.wait_send()
            theirs = outs[a].at[pl.ds((1 - c) * hr, hr)]
            pltpu.make_async_remote_copy(src_ref=theirs, dst_ref=theirs, send_sem=send.at[a], recv_sem=recv.at[a],
                                         device_id=(x, y, 1 - c), device_id_type=MESH).wait_recv()

    out_shape = [jax.ShapeDtypeStruct((2 * a.shape[0], a.shape[1]), a.dtype) for a in halves]
    return _hbm_call(body, name, halves, out_shape, (n, n),
                     bounce=[(a.shape[0], a.shape[1], a.dtype) for a in halves])


def _row_tile(rows, cols, itemsize, n_bufs):
    budget = VMEM_LIMIT // 2
    for t in (1024, 512, 256, 128, 64, 32, 16):
        if rows % t == 0 and 2 * n_bufs * t * cols * itemsize <= budget:
            return t
    return rows


def _pair_add(name, g, gsib, core, out_dtype):
    _, r, cols = g.shape
    hr = r // 2
    t = _row_tile(hr, cols, 4, 3)
    per = hr // t

    def body(core_ref, a_ref, b_ref, o_ref):
        o_ref[...] = (a_ref[...].astype(F32) + b_ref[...].astype(F32)).astype(o_ref.dtype)

    grid_spec = pltpu.PrefetchScalarGridSpec(
        num_scalar_prefetch=1, grid=(N_CHIPS, per),
        in_specs=[pl.BlockSpec((None, t, cols), lambda s, i, core: (s, core[0] * per + i, 0)),
                  pl.BlockSpec((None, t, cols), lambda s, i, core: (s, i, 0))],
        out_specs=pl.BlockSpec((None, t, cols), lambda s, i, core: (s, i, 0)))
    return pl.pallas_call(body, name=name, grid_spec=grid_spec,
                          out_shape=jax.ShapeDtypeStruct((N_CHIPS, hr, cols), out_dtype),
                          compiler_params=_params(("arbitrary", "arbitrary")))(core, g, gsib)


def _chip_sum(name, parts):
    _, hr, cols = parts.shape
    t = _row_tile(hr, cols, 4, 5)

    def body(p_ref, o_ref):
        o_ref[...] = ((p_ref[0].astype(F32) + p_ref[1].astype(F32)) + p_ref[2].astype(F32)) + p_ref[3].astype(F32)

    return pl.pallas_call(
        body, name=name, grid=(hr // t,), in_specs=[pl.BlockSpec((N_CHIPS, t, cols), lambda i: (0, i, 0))],
        out_specs=pl.BlockSpec((t, cols), lambda i: (i, 0)), out_shape=jax.ShapeDtypeStruct((hr, cols), F32),
        compiler_params=_params(("arbitrary",)))(parts)


def _pair_partials(tag, arrs, wire_dtypes, core):
    sib = _pair_swap(f"{tag}_pair_swap", arrs)
    return _Scatter([_pair_add(f"{tag}_pair_add{i}", g, gs, core, dt)
                     for i, (g, gs, dt) in enumerate(zip(arrs, sib, wire_dtypes))])


def _finish_reduce(tag, scattered):
    halves = [_chip_sum(f"{tag}_chip_sum{i}", p) for i, p in enumerate(scattered)]
    return _pair_gather(f"{tag}_pair_gather", halves)


def _adamw(name, w, g_parts, m, v):
    rows, cols = w.shape
    n_parts = len(g_parts)
    part_rows = rows // n_parts
    t = _row_tile(part_rows, cols, 4, 7 + n_parts)
    per = part_rows // t
    c1 = 1.0 - ADAM_B1 ** ADAM_STEP
    c2 = 1.0 - ADAM_B2 ** ADAM_STEP

    def body(w_ref, m_ref, v_ref, *refs):
        g_refs, (go_ref, d_ref, nm_ref, nv_ref) = refs[:n_parts], refs[n_parts:]
        g = g_refs[0][...]
        for k in range(1, n_parts):
            g = jnp.where(pl.program_id(0) >= k * per, g_refs[k][...], g)
        go_ref[...] = g
        m = ADAM_B1 * m_ref[...] + (1.0 - ADAM_B1) * g
        v = ADAM_B2 * v_ref[...] + (1.0 - ADAM_B2) * (g * g)
        nm_ref[...] = m
        nv_ref[...] = v
        d_ref[...] = -ADAM_LR * ((m / c1) / (jnp.sqrt(v / c2) + ADAM_EPS) + ADAM_WD * w_ref[...])

    spec = pl.BlockSpec((t, cols), lambda i: (i, 0))
    g_specs = [pl.BlockSpec((t, cols), lambda i, k=k: (jnp.clip(i - k * per, 0, per - 1), 0)) for k in range(n_parts)]
    shp = jax.ShapeDtypeStruct((rows, cols), F32)
    return pl.pallas_call(body, name=name, grid=(rows // t,), in_specs=[spec] * 3 + g_specs, out_specs=[spec] * 4,
                          out_shape=[shp] * 4, compiler_params=_params(("arbitrary",)))(w, m, v, *g_parts)


_WEIGHTS = ["mix_norm", "mlp_norm", "mlp_w1", "mlp_w2", "lru_w_in", "lru_conv_w", "lru_conv_b", "lru_w_r", "lru_b_r",
            "lru_w_i", "lru_b_i", "lru_lambda", "lru_w_out", "fox_w_in", "fox_b_f", "fox_q_gain", "fox_k_gain",
            "fox_w_out"]
_REPLICATED = ["mix_norm", "mlp_norm", "lru_conv_b", "lru_w_r", "lru_b_r", "lru_w_i", "lru_b_i", "lru_lambda",
               "fox_b_f", "fox_q_gain", "fox_k_gain"]
_PACK_TILE = 2 * SUBLANES * LANES


def _as2d(a):
    return a.reshape(-1, a.shape[-1])


def kernel(x, mix_norm, mlp_norm, mlp_w1, mlp_w2, lru_w_in, lru_conv_w, lru_conv_b, lru_w_r, lru_b_r, lru_w_i, lru_b_i, lru_lambda, lru_w_out, fox_w_in, fox_b_f, fox_q_gain, fox_k_gain, fox_w_out, loss_target, m_mix_norm, m_mlp_norm, m_mlp_w1, m_mlp_w2, m_lru_w_in, m_lru_conv_w, m_lru_conv_b, m_lru_w_r, m_lru_b_r, m_lru_w_i, m_lru_b_i, m_lru_lambda, m_lru_w_out, m_fox_w_in, m_fox_b_f, m_fox_q_gain, m_fox_k_gain, m_fox_w_out, v_mix_norm, v_mlp_norm, v_mlp_w1, v_mlp_w2, v_lru_w_in, v_lru_conv_w, v_lru_conv_b, v_lru_w_r, v_lru_b_r, v_lru_w_i, v_lru_b_i, v_lru_lambda, v_lru_w_out, v_fox_w_in, v_fox_b_f, v_fox_q_gain, v_fox_k_gain, v_fox_w_out):
    args = dict(locals())
    W = {n: args[n] for n in _WEIGHTS}
    Mo = {n: args["m_" + n] for n in _WEIGHTS}
    Vo = {n: args["v_" + n] for n in _WEIGHTS}
    S, D = x.shape[1], x.shape[2]
    F = 4 * D
    H = D // HEAD_DIM
    NU = 3 * D + LANES
    FQ, DQ = F // N_CHIPS, D // N_CHIPS
    nfox = fox_w_in.shape[-1]
    chip = 2 * lax.axis_index("x") + lax.axis_index("y")
    core = lax.axis_index("c").astype(jnp.int32).reshape(1)

    cw_flat = jnp.pad(lru_conv_w.reshape(-1), (0, _PACK_TILE - CONV_WIDTH * DQ)).reshape(2 * SUBLANES, LANES)
    w1s, w2s = mlp_w1.astype(BF16), mlp_w2.astype(BF16)
    g_lin, g_lout, g_cw = _all_gather("gather_lru", [lru_w_in[0].astype(BF16), lru_w_out[0].astype(BF16), cw_flat])
    conv_w_full = jnp.transpose(g_cw.reshape(N_CHIPS, -1)[:, :CONV_WIDTH * DQ].reshape(N_CHIPS, CONV_WIDTH, DQ),
                                (1, 0, 2)).reshape(CONV_WIDTH, D)
    wv = {"lru_in": _View(g_lin, "cs"), "lru_out": _View(g_lout, "rs")}
    scattered = {}
    riding = {"attn_backward": ["w2_1", "w1_1", "fox_out"], "mlp0_dact": ["fox_in"],
              "lru_bwd": ["w2_0", "w1_0", "lru_out"]}

    def shard_major(name, g):
        if name == "fox_in":
            return jnp.transpose(g[:, :nfox * N_CHIPS].reshape(D, N_CHIPS, nfox), (1, 0, 2))
        return g

    class Comm:
        @staticmethod
        def before(name, grads):
            if name == "lru_fwd":
                return _Gather([w1s[0]])
            if name == "mlp0_up":
                return _Gather([w2s[0]])
            if name == "mlp0_down":
                return _Gather([fox_w_in[0].astype(BF16)])
            if name == "attn_forward":
                return _Gather([fox_w_out[0].astype(BF16), w1s[1], w2s[1]])
            if name in riding:
                arrs = [shard_major(n, grads[n]) for n in riding[name]]
                return _pair_partials(f"{name}_grads", arrs, [BF16] * len(arrs), core)
            return None

        @staticmethod
        def after(name, res, wv):
            if name == "lru_fwd":
                wv.update(w1_0=_View(res[0], "cs"))
            elif name == "mlp0_up":
                wv.update(w2_0=_View(res[0], "rs"))
            elif name == "mlp0_down":
                fox_full = jnp.concatenate([res[0][s] for s in range(N_CHIPS)], axis=1)
                fox_full = jnp.pad(fox_full, ((0, 0), (0, NU - fox_full.shape[1])))
                wv.update(fox_in=_View(fox_full.T))
            elif name == "attn_forward":
                wv.update(fox_out=_View(res[0], "rs"), w1_1=_View(res[1], "cs"), w2_1=_View(res[2], "rs"))
            else:
                scattered.update(zip(riding[name], res))

    def grad_view(grads, name):
        if name in ("w1_0", "w1_1"):
            return _View(None, "cs", shape=(N_CHIPS, D, FQ), dtype=BF16)
        if name in ("w2_0", "w2_1"):
            return _View(None, "rs", shape=(N_CHIPS, FQ, D), dtype=BF16)
        if name == "lru_in":
            return _View(None, "cs", shape=(N_CHIPS, D, 2 * D // N_CHIPS), dtype=BF16)
        if name in ("lru_out", "fox_out"):
            return _View(None, "rs", shape=(N_CHIPS, DQ, D), dtype=BF16)
        return _View(None, shape=(D, NU), dtype=BF16)

    small = {n: W[n] for n in _REPLICATED}
    small["conv_w"] = conv_w_full

    loss, gx, grads = _local_step(x[0], loss_target[0], small, wv, grad_view, Comm)

    pack_names = _REPLICATED + ["conv_w"]
    flat = jnp.concatenate([grads[n].reshape(-1).astype(F32) for n in pack_names] + [loss.reshape(-1)])
    per_chip = -(-flat.shape[0] // (N_CHIPS * _PACK_TILE)) * _PACK_TILE
    pack = jnp.pad(flat, (0, N_CHIPS * per_chip - flat.shape[0])).reshape(N_CHIPS, per_chip // LANES, LANES)
    tail = _run_plan("tail_grads_chip_scatter", _pair_partials("tail_grads", [grads["lru_in"], pack], [BF16, F32], core))
    scattered.update(lru_in=tail[0], pack=tail[1])
    order = ["w1_0", "w1_1", "w2_0", "w2_1", "lru_in", "lru_out", "fox_in", "fox_out", "pack"]
    red = dict(zip(order, _finish_reduce("grads", [scattered[n] for n in order])))
    (all_pack,) = _all_gather("gather_small_grads", [red["pack"]])
    all_flat = all_pack.reshape(-1)
    G = {}
    off = 0
    for n in pack_names:
        shape = grads[n].shape if n == "conv_w" else W[n].shape
        size = int(np.prod(shape))
        G[n] = all_flat[off:off + size].reshape(shape)
        off += size
    total = all_flat[off]
    G["lru_conv_w"] = lax.dynamic_slice_in_dim(G.pop("conv_w"), chip * DQ, DQ, axis=1)[None]
    parts = {n: [_as2d(G[n])] for n in G}
    parts.update(mlp_w1=[red["w1_0"], red["w1_1"]], mlp_w2=[red["w2_0"], red["w2_1"]], lru_w_in=[red["lru_in"]],
                 lru_w_out=[red["lru_out"]], fox_w_in=[red["fox_in"]], fox_w_out=[red["fox_out"]])

    delta, new_m, new_v = {}, {}, {}
    for n in _WEIGHTS:
        go, d, nm, nv = _adamw(f"adamw_{n}", _as2d(W[n]), parts[n], _as2d(Mo[n]), _as2d(Vo[n]))
        G[n], delta[n], new_m[n], new_v[n] = (t.reshape(W[n].shape) for t in (go, d, nm, nv))

    return (total, gx[None], *[G[n] for n in _WEIGHTS], *[delta[n] for n in _WEIGHTS],
            *[new_m[n] for n in _WEIGHTS], *[new_v[n] for n in _WEIGHTS])
```

```python
import functools

import numpy as np
import jax
import jax.numpy as jnp
from jax import lax
from jax.experimental import pallas as pl
from jax.experimental.pallas import tpu as pltpu

F32 = jnp.float32
BF16 = jnp.bfloat16

HEAD_DIM = 64
LRU_BLOCK_DIM = 64
CONV_WIDTH = 4
LRU_C = 8.0
EPS = 1e-6
NEG_INF = -1e30
ADAM_LR = 0.001
ADAM_B1 = 0.9
ADAM_B2 = 0.999
ADAM_EPS = 1e-08
ADAM_WD = 0.01
ADAM_STEP = 10

N_CHIPS = 4
LANES = 128
SUBLANES = 8
MXU_DIM = 256
VMEM_LIMIT = 52 * 1024 * 1024
MESH = pl.DeviceIdType.MESH
ANY = pl.BlockSpec(memory_space=pl.ANY)


def _pick(n, prefs):
    for p in prefs:
        if p <= n and n % p == 0:
            return p
    return n


def _params(sem=None):
    return pltpu.CompilerParams(dimension_semantics=sem, vmem_limit_bytes=VMEM_LIMIT)


class _View:
    def __init__(self, arr, kind="plain", r0=0, rows=None, shape=None, dtype=None):
        self.arr = arr
        self.kind = kind
        self.r0 = r0
        self.shape = tuple(arr.shape) if arr is not None else tuple(shape)
        self.dtype = arr.dtype if arr is not None else dtype
        self.rows = rows if rows is not None else self.shape[-2]

    def limits(self):
        if self.kind == "plain":
            return 0, 0
        rows = int(np.gcd(self.rows, self.r0))
        return rows, (self.shape[-1] if self.kind == "cs" else 0)

    def spec(self, br, bc, fr, fc):
        if self.kind == "plain":
            return pl.BlockSpec((br, bc), lambda *g: (fr(*g), fc(*g)))
        ncol = self.shape[-1]
        r0b = self.r0 // br
        assert self.r0 % br == 0 and self.rows % br == 0 and ncol % bc == 0, (self.shape, self.r0, br, bc)
        if self.kind == "cs":
            per = ncol // bc
            return pl.BlockSpec((None, br, bc), lambda *g: (fc(*g) // per, r0b + fr(*g), fc(*g) % per))
        per = self.rows // br
        return pl.BlockSpec((None, br, bc), lambda *g: (fr(*g) // per, r0b + fr(*g) % per, fc(*g)))


def _bf(x):
    return x if x.dtype == BF16 else x.astype(BF16)


def _matmul(name, A, B, M, N, K, *, ta=False, tb=False, outs, epilogue, extras=(), vecs=(), n_sums=0,
            tm=None, tn=None, tk=None, plan=None):
    lim = {"m": [M], "n": [N], "k": [K]}
    for view, (rdim, cdim) in ([(A, "km" if ta else "mk"), (B, "nk" if tb else "kn")]
                               + [(e, "mn") for e in extras] + [(o, "mn") for o in outs]):
        r_lim, c_lim = view.limits()
        lim[rdim].append(r_lim)
        lim[cdim].append(c_lim)
    tm = tm or _pick(int(np.gcd.reduce(lim["m"])), (1024, 640, 512, 256, 128))
    tn = tn or _pick(int(np.gcd.reduce(lim["n"])), (1024, 640, 512, 256, 128))
    tk = tk or _pick(int(np.gcd.reduce(lim["k"])), (1024, 640, 512, 256, 128))
    nk = K // tk
    gi = lambda i, j, k: i
    gj = lambda i, j, k: j
    gk = lambda i, j, k: k
    a_spec = A.spec(tk, tm, gk, gi) if ta else A.spec(tm, tk, gi, gk)
    b_spec = B.spec(tn, tk, gj, gk) if tb else B.spec(tk, tn, gk, gj)
    ca = 0 if ta else 1
    cb = 1 if tb else 0
    ne, no = len(extras) + len(vecs), len(outs)
    assert n_sums == 0 or tn == N
    row_spec = pl.BlockSpec((1, tn), lambda i, j, k: (0, j))
    in_specs = [a_spec, b_spec] + [e.spec(tm, tn, gi, gj) for e in extras] + [row_spec] * len(vecs)
    operands = [A.arr, B.arr] + [e.arr for e in extras] + list(vecs)
    out_specs = [o.spec(tm, tn, gi, gj) for o in outs] + [row_spec] * n_sums
    out_shape = ([jax.ShapeDtypeStruct(o.shape, o.dtype) for o in outs]
                 + [jax.ShapeDtypeStruct((1, N), F32)] * n_sums)

    def body(*refs):
        a_ref, b_ref = refs[0], refs[1]
        ex = refs[2:2 + ne]
        o_refs = refs[2 + ne:2 + ne + no]
        s_refs = refs[2 + ne + no:2 + ne + no + n_sums]
        first_row_tile = pl.program_id(0) == 0

        def prod():
            return lax.dot_general(_bf(a_ref[...]), _bf(b_ref[...]), (((ca,), (cb,)), ((), ())),
                                   preferred_element_type=F32)

        def finish(acc):
            res = epilogue(acc, *[e[...] for e in ex])
            for o_ref, r in zip(o_refs, res[:no]):
                o_ref[...] = r.astype(o_ref.dtype)
            for s_ref, r in zip(s_refs, res[no:]):
                def assign(s_ref=s_ref, r=r):
                    s_ref[...] = r

                def accumulate(s_ref=s_ref, r=r):
                    s_ref[...] += r

                pl.when(first_row_tile)(assign)
                pl.when(jnp.logical_not(first_row_tile))(accumulate)

        if nk == 1:
            finish(prod())
        else:
            acc_ref = refs[-1]
            k = pl.program_id(2)

            @pl.when(k == 0)
            def _():
                acc_ref[...] = jnp.zeros_like(acc_ref)

            acc_ref[...] += prod()

            @pl.when(k == nk - 1)
            def _():
                finish(acc_ref[...])

    res, side = _hosted_call(body, name, (M // tm, N // tn, nk), in_specs, out_specs, out_shape,
                             [pltpu.VMEM((tm, tn), F32)] if nk > 1 else [], operands,
                             ("arbitrary", "arbitrary", "arbitrary"), plan)
    return res if plan is None else (res, side)


def _ep_store(acc):
    return (acc,)


def _ep_resid(acc, res):
    return (res + acc,)


def _ep_resid_norm(acc, res, g):
    xo = res + acc
    r = lax.rsqrt(jnp.mean(xo * xo, axis=-1, keepdims=True) + EPS)
    return (xo, (xo * r) * g)


def _ep_norm_bwd(acc, x, dres, g):
    r = lax.rsqrt(jnp.mean(x * x, axis=-1, keepdims=True) + EPS)
    xhat = x * r
    dxn = acc * g
    tot = dres + r * (dxn - xhat * jnp.mean(dxn * xhat, axis=-1, keepdims=True))
    return (tot, tot, jnp.sum(acc * xhat, axis=0, keepdims=True))


def _ep_relu2(acc):
    zp = jnp.maximum(acc, 0.0)
    return (acc, zp * zp)


def _ep_drelu2(acc, z):
    return (acc * (2.0 * jnp.maximum(z.astype(F32), 0.0)),)


def _fresh(M, N, dtype):
    return _View(None, shape=(M, N), dtype=dtype)


def _rms_fwd(name, x, g, S, D):
    T = _pick(S, (512, 256, 128))

    def body(x_ref, g_ref, h_ref):
        x = x_ref[...]
        r = lax.rsqrt(jnp.mean(x * x, axis=-1, keepdims=True) + EPS)
        h_ref[...] = ((x * r) * g_ref[...]).astype(BF16)

    return pl.pallas_call(
        body, name=name, grid=(S // T,),
        in_specs=[pl.BlockSpec((T, D), lambda i: (i, 0)), pl.BlockSpec((1, D), lambda i: (0, 0))],
        out_specs=pl.BlockSpec((T, D), lambda i: (i, 0)),
        out_shape=jax.ShapeDtypeStruct((S, D), BF16),
        compiler_params=_params(("arbitrary",)),
    )(x, g)


def _loss_head(x, tgt, S, D):
    T = _pick(S, (512, 256, 128))

    def body(x_ref, t_ref, loss_ref, d_ref, db_ref):
        @pl.when(pl.program_id(0) == 0)
        def _():
            loss_ref[...] = jnp.zeros_like(loss_ref)

        e = x_ref[...] - t_ref[...]
        loss_ref[...] += 0.5 * jnp.sum(jnp.mean(e * e, axis=-1, keepdims=True), axis=0, keepdims=True)
        d = e * (1.0 / D)
        d_ref[...] = d
        db_ref[...] = d.astype(BF16)

    row = pl.BlockSpec((T, D), lambda i: (i, 0))
    return pl.pallas_call(
        body, name="loss_head", grid=(S // T,), in_specs=[row, row],
        out_specs=[pl.BlockSpec((1, 1), lambda i: (0, 0)), row, row],
        out_shape=[jax.ShapeDtypeStruct((1, 1), F32), jax.ShapeDtypeStruct((S, D), F32),
                   jax.ShapeDtypeStruct((S, D), BF16)],
        compiler_params=_params(("arbitrary",)),
    )(x, tgt)


def _sigmoid(z):
    return 1.0 / (1.0 + jnp.exp(-z))


def _log_sigmoid(z):
    return jnp.minimum(z, 0.0) - jnp.log(1.0 + jnp.exp(-jnp.abs(z)))


_GELU_K = 0.7978845608028654
_GELU_C = 0.044715


def _gelu(x):
    t = jnp.tanh(_GELU_K * (x + _GELU_C * (x * x * x)))
    return 0.5 * x * (1.0 + t)


def _gelu_and_grad(x):
    x2 = x * x
    t = jnp.tanh(_GELU_K * (x + _GELU_C * (x2 * x)))
    g = 0.5 * x * (1.0 + t)
    dg = 0.5 * (1.0 + t) + 0.5 * x * (1.0 - t * t) * (_GELU_K * (1.0 + 3.0 * _GELU_C * x2))
    return g, dg


def _decay_terms(r, ls):
    la = LRU_C * r * ls
    a = jnp.exp(la)
    a2 = jnp.exp(2.0 * la)
    mult = jnp.sqrt(-jnp.tanh(la) * (a2 + 1.0))
    return a, a2, mult


def _lru_fwd(u0, conv_w, conv_b, wr_bd, b_r, wi_bd, b_i, lam, S, D, plan=None):
    T = _pick(S, (256, 128))
    GT = wr_bd.shape[-1]
    nG = D // GT

    def body(gb_ref, xb_ref, cw_ref, cb_ref, wr_ref, br_ref, wi_ref, bi_ref, lam_ref,
             y_ref, xc_ref, r_ref, i_ref, hs_ref, ext, a_scr, hcar):
        @pl.when(pl.program_id(0) == 0)
        def _():
            ext[0:SUBLANES, :] = jnp.zeros((SUBLANES, D), F32)
            hcar[...] = jnp.zeros_like(hcar)

        xb = xb_ref[...]
        ext[SUBLANES:SUBLANES + T, :] = xb
        xc = cb_ref[...]
        for k in range(CONV_WIDTH):
            xc = xc + ext[pl.ds(SUBLANES - (CONV_WIDTH - 1) + k, T), :] * cw_ref[k:k + 1, :]
        ext[0:SUBLANES, :] = xb[T - SUBLANES:T, :]
        xc_ref[...] = xc
        xcb = xc.astype(BF16)
        for g in range(nG):
            sl = slice(g * GT, (g + 1) * GT)
            zr = jnp.dot(xcb[:, sl], wr_ref[g], preferred_element_type=F32) + br_ref[:, sl]
            zi = jnp.dot(xcb[:, sl], wi_ref[g], preferred_element_type=F32) + bi_ref[:, sl]
            r_ref[:, sl] = _sigmoid(zr)
            i_ref[:, sl] = _sigmoid(zi)
        r = r_ref[...]
        a, _, mult = _decay_terms(r, _log_sigmoid(lam_ref[...]))
        a_scr[...] = a
        hs_ref[...] = mult * (i_ref[...] * xc)

        def step(t, h):
            h = a_scr[pl.ds(t, 1), :] * h + hs_ref[pl.ds(t, 1), :]
            hs_ref[pl.ds(t, 1), :] = h
            return h

        hcar[...] = lax.fori_loop(0, T, step, hcar[...], unroll=8)
        y_ref[...] = (_gelu(gb_ref[...]) * hs_ref[...]).astype(BF16)

    row = pl.BlockSpec((T, D), lambda i: (i, 0))
    vec = pl.BlockSpec((1, D), lambda i: (0, 0))
    bd = pl.BlockSpec((nG, GT, GT), lambda i: (0, 0, 0))
    f32o = jax.ShapeDtypeStruct((S, D), F32)
    return _hosted_call(
        body, "lru_fwd", (S // T,),
        [row, pl.BlockSpec((T, D), lambda i: (i, 1)), pl.BlockSpec((CONV_WIDTH, D), lambda i: (0, 0)), vec,
         bd, vec, bd, vec, vec],
        [row, row, row, row, row], [jax.ShapeDtypeStruct((S, D), BF16), f32o, f32o, f32o, f32o],
        [pltpu.VMEM((T + SUBLANES, D), F32), pltpu.VMEM((T, D), F32), pltpu.VMEM((1, D), F32)],
        (u0, u0, conv_w, conv_b, wr_bd, b_r, wi_bd, b_i, lam), ("arbitrary",), plan)


def _lru_bwd(dy, u0, xc, r, ig, hs, conv_w, wr_bd, wi_bd, lam, S, D, plan=None):
    T = _pick(S, (128,))
    nT = S // T
    GT = wr_bd.shape[-1]
    nG = D // GT
    W = CONV_WIDTH

    def body(dy_ref, gb_ref, xb_ref, xbp_ref, xc_ref, r_ref, i_ref, hs_ref, hsp_ref, cw_ref, wr_ref, wi_ref, lam_ref,
             du_ref, dcw_ref, dcb_ref, dlam_ref, dbr_ref, dbi_ref, dwr_ref, dwi_ref,
             a_scr, dh_scr, exth, extx, extd, dxc_scr, dz_scr, carry):
        step = pl.program_id(0)
        first_tile = step == nT - 1

        @pl.when(step == 0)
        def _():
            for ref in (dcw_ref, dcb_ref, dlam_ref, dbr_ref, dbi_ref, dwr_ref, dwi_ref, carry):
                ref[...] = jnp.zeros_like(ref)
            extd[T:T + SUBLANES, :] = jnp.zeros((SUBLANES, D), F32)

        hs = hs_ref[...]
        dy = dy_ref[...]
        g, dgelu = _gelu_and_grad(gb_ref[...])
        du_ref[:, 0:D] = (dy * hs * dgelu).astype(BF16)
        r = r_ref[...]
        lam = lam_ref[...]
        ls = _log_sigmoid(lam)
        a, a2, mult = _decay_terms(r, ls)
        a_scr[...] = a
        dh_scr[...] = dy * g

        def rstep(j, c):
            t = T - 1 - j
            d = dh_scr[pl.ds(t, 1), :] + c
            dh_scr[pl.ds(t, 1), :] = d
            return a_scr[pl.ds(t, 1), :] * d

        carry[...] = lax.fori_loop(0, T, rstep, carry[...], unroll=8)
        dh = dh_scr[...]
        keep = jnp.where(first_tile, 0.0, 1.0)
        exth[0:SUBLANES, :] = hsp_ref[...] * keep
        exth[SUBLANES:SUBLANES + T, :] = hs
        hprev = exth[pl.ds(SUBLANES - 1, T), :]
        xc = xc_ref[...]
        ig = i_ref[...]
        da = dh * hprev
        dmult = dh * (ig * xc)
        dla = da * a - dmult * (a2 / mult)
        dlam_ref[...] += jnp.sum(dla * r, axis=0, keepdims=True) * (LRU_C * _sigmoid(-lam))
        dzr = (dla * (LRU_C * ls)) * (r * (1.0 - r))
        dzi = (dh * (mult * xc)) * (ig * (1.0 - ig))
        dbr_ref[...] += jnp.sum(dzr, axis=0, keepdims=True)
        dbi_ref[...] += jnp.sum(dzi, axis=0, keepdims=True)
        dxc_scr[...] = dh * (mult * ig)
        xcb = xc.astype(BF16)
        dz_scr[0] = dzr.astype(BF16)
        dz_scr[1] = dzi.astype(BF16)
        nt_dims = (((1,), (1,)), ((), ()))
        tn_dims = (((0,), (0,)), ((), ()))
        for gq in range(nG):
            sl = slice(gq * GT, (gq + 1) * GT)
            zr_g = dz_scr[0, :, sl]
            zi_g = dz_scr[1, :, sl]
            dxc_scr[:, sl] += (lax.dot_general(zr_g, wr_ref[gq], nt_dims, preferred_element_type=F32)
                               + lax.dot_general(zi_g, wi_ref[gq], nt_dims, preferred_element_type=F32))
            dwr_ref[gq] += lax.dot_general(xcb[:, sl], zr_g, tn_dims, preferred_element_type=F32)
            dwi_ref[gq] += lax.dot_general(xcb[:, sl], zi_g, tn_dims, preferred_element_type=F32)
        dxc = dxc_scr[...]
        dcb_ref[...] += jnp.sum(dxc, axis=0, keepdims=True)
        extx[0:SUBLANES, :] = xbp_ref[...] * keep
        extx[SUBLANES:SUBLANES + T, :] = xb_ref[...]
        extd[0:T, :] = dxc
        dxb = jnp.zeros((T, D), F32)
        for k in range(W):
            dxb = dxb + extd[pl.ds(W - 1 - k, T), :] * cw_ref[k:k + 1, :]
            dcw_ref[k:k + 1, :] += jnp.sum(dxc * extx[pl.ds(SUBLANES - (W - 1) + k, T), :], axis=0, keepdims=True)
        extd[T:T + SUBLANES, :] = dxc[0:SUBLANES, :]
        du_ref[:, D:2 * D] = dxb.astype(BF16)

    rev = lambda i: nT - 1 - i
    tpb = T // SUBLANES
    prev8 = lambda i: jnp.maximum(rev(i) * tpb - 1, 0)
    row = pl.BlockSpec((T, D), lambda i: (rev(i), 0))
    vec = pl.BlockSpec((1, D), lambda i: (0, 0))
    bd = pl.BlockSpec((nG, GT, GT), lambda i: (0, 0, 0))
    vec_o = jax.ShapeDtypeStruct((1, D), F32)
    bd_o = jax.ShapeDtypeStruct((nG, GT, GT), F32)
    return _hosted_call(
        body, "lru_bwd", (nT,),
        [row, row, pl.BlockSpec((T, D), lambda i: (rev(i), 1)), pl.BlockSpec((SUBLANES, D), lambda i: (prev8(i), 1)),
         row, row, row, row, pl.BlockSpec((SUBLANES, D), lambda i: (prev8(i), 0)),
         pl.BlockSpec((W, D), lambda i: (0, 0)), bd, bd, vec],
        [pl.BlockSpec((T, 2 * D), lambda i: (rev(i), 0)), pl.BlockSpec((W, D), lambda i: (0, 0)),
         vec, vec, vec, vec, bd, bd],
        [jax.ShapeDtypeStruct((S, 2 * D), BF16), jax.ShapeDtypeStruct((W, D), F32), vec_o, vec_o, vec_o, vec_o, bd_o, bd_o],
        [pltpu.VMEM((T, D), F32), pltpu.VMEM((T, D), F32), pltpu.VMEM((T + SUBLANES, D), F32),
         pltpu.VMEM((T + SUBLANES, D), F32), pltpu.VMEM((T + SUBLANES, D), F32),
         pltpu.VMEM((T, D), F32), pltpu.VMEM((2, T, D), BF16), pltpu.VMEM((1, D), F32)],
        (dy, u0, u0, u0, xc, r, ig, hs, hs, conv_w, wr_bd, wi_bd, lam), ("arbitrary",), plan)


AUG_ROWS = 16
HEAD_ROWS = 128
LSE_ROW = HEAD_DIM + 6
ONES_ROW_Q = HEAD_DIM + 3
ONES_COL_K = HEAD_DIM
ONES_ROW_V = HEAD_DIM


def _split3(x):
    b1 = x.astype(BF16).astype(F32)
    r = x - b1
    b2 = r.astype(BF16).astype(F32)
    return b1, b2, r - b2


def _head_block(x, aug, T):
    row = lax.broadcasted_iota(jnp.int32, (AUG_ROWS, T), 0)
    blk = jnp.zeros((AUG_ROWS, T), F32)
    for i, e in enumerate(aug):
        blk = jnp.where(row == i, e, blk)
    return jnp.concatenate([x, blk, jnp.zeros((HEAD_ROWS - HEAD_DIM - AUG_ROWS, T), F32)], axis=0)


def _tri_matrix(lower):
    i = np.arange(LANES)
    m = (i[:, None] >= i[None, :]) if lower else (i[:, None] <= i[None, :])
    return jnp.asarray(m.astype(np.float32), BF16)


def _lane_cumsum(x, tri_ref, carry, reverse):
    n = x.shape[1] // LANES
    tri = tri_ref[...]
    out = [None] * n
    for j in (range(n - 1, -1, -1) if reverse else range(n)):
        cs = carry
        for part in _split3(x[:, j * LANES:(j + 1) * LANES]):
            cs = cs + jnp.dot(part.astype(BF16), tri, preferred_element_type=F32)
        out[j] = cs
        carry = cs[:, 0:1] if reverse else cs[:, LANES - 1:LANES]
    return jnp.concatenate(out, axis=1), carry


def _head_rows(h):
    return pl.ds(pl.multiple_of(h * HEAD_DIM, HEAD_DIM), HEAD_DIM)


def _fox_prep(ut, b_f, qg, kg, S, D, tq):
    H = D // HEAD_DIM
    T = min(tq, 256)
    per = tq // T
    scale = HEAD_DIM ** -0.5

    def body(q_ref, k_ref, v_ref, f_ref, bf_ref, qg_ref, kg_ref, tri_ref,
             qat_ref, kat_ref, vat_ref, ka_ref, c_scr, ccar):
        @pl.when(pl.program_id(0) == 0)
        def _():
            ccar[...] = jnp.zeros_like(ccar)

        c, carry = _lane_cumsum(_log_sigmoid(f_ref[...] + bf_ref[...]), tri_ref, ccar[...], False)
        c_scr[...] = c
        ccar[...] = carry

        def head(h, _):
            rows = _head_rows(h)
            c1, c2, c3 = _split3(c_scr[pl.ds(h, 1), :])

            def normed(src, gain, mul):
                x = src[rows, :]
                rs = lax.rsqrt(jnp.mean(x * x, axis=0, keepdims=True) + EPS)
                return ((x * rs) * gain[rows, :]) * mul

            qat_ref[h] = _head_block(normed(q_ref, qg_ref, scale), [c1, c2, c3, 1.0, 1.0, 1.0], T).astype(BF16)
            kb = _head_block(normed(k_ref, kg_ref, 1.0), [1.0, 1.0, 1.0, -c1, -c2, -c3, 1.0, 1.0, 1.0], T)
            kat_ref[h] = kb.astype(BF16)
            ka_ref[h] = kb.T.astype(BF16)
            vat_ref[h] = _head_block(v_ref[rows, :], [1.0, 1.0, 1.0], T).astype(BF16)
            return 0

        lax.fori_loop(0, H, head, 0)

    part = lambda j: pl.BlockSpec((D, T), lambda i: (j, i))
    colv = lambda n: pl.BlockSpec((n, 1), lambda i: (0, 0))
    tmaj = lambda r: pl.BlockSpec((H, None, r, T), lambda i: (0, i // per, 0, i % per))
    norm = pl.BlockSpec((H, T, HEAD_ROWS), lambda i: (0, i, 0))
    tshape = lambda r: jax.ShapeDtypeStruct((H, S // tq, r, tq), BF16)
    nshape = jax.ShapeDtypeStruct((H, S, HEAD_ROWS), BF16)
    return pl.pallas_call(
        body, name="fox_prep", grid=(S // T,),
        in_specs=[part(0), part(1), part(2), pl.BlockSpec((LANES, T), lambda i: (3 * D // LANES, i)),
                  colv(LANES), colv(D), colv(D), pl.BlockSpec((LANES, LANES), lambda i: (0, 0))],
        out_specs=[tmaj(HEAD_ROWS), tmaj(HEAD_ROWS), tmaj(HEAD_ROWS), norm],
        out_shape=[tshape(HEAD_ROWS), tshape(HEAD_ROWS), tshape(HEAD_ROWS), nshape],
        scratch_shapes=[pltpu.VMEM((LANES, T), F32), pltpu.VMEM((LANES, 1), F32)],
        compiler_params=_params(("arbitrary",)),
    )(ut, ut, ut, ut, b_f, qg, kg, _tri_matrix(False))


def _fox_bwd_prep(dot, ot, lse, qat, S, D, tq, plan=None):
    H = D // HEAD_DIM
    T = min(tq, 256)
    per = tq // T

    def body(do_ref, o_ref, lse_ref, qat_ref, doat_ref, doa_ref, qat1_ref, qa1_ref):
        row = lax.broadcasted_iota(jnp.int32, (HEAD_ROWS, T), 0)

        def head(h, _):
            rows = _head_rows(h)
            do = do_ref[rows, :].astype(F32)
            delta = jnp.sum(do * o_ref[rows, :], axis=0, keepdims=True)
            db = _head_block(do, list(_split3(-delta)), T)
            doat_ref[h] = db.astype(BF16)
            doa_ref[h] = db.T.astype(BF16)
            qb = qat_ref[h].astype(F32)
            for i, e in enumerate(_split3(-lse_ref[h])):
                qb = jnp.where(row == LSE_ROW + i, e, qb)
            qat1_ref[h] = qb.astype(BF16)
            qa1_ref[h] = qb.T.astype(BF16)
            return 0

        lax.fori_loop(0, H, head, 0)

    chan = pl.BlockSpec((D, T), lambda i: (0, i))
    tmaj = pl.BlockSpec((H, None, HEAD_ROWS, T), lambda i: (0, i // per, 0, i % per))
    norm = pl.BlockSpec((H, T, HEAD_ROWS), lambda i: (0, i, 0))
    tshape = jax.ShapeDtypeStruct((H, S // tq, HEAD_ROWS, tq), BF16)
    nshape = jax.ShapeDtypeStruct((H, S, HEAD_ROWS), BF16)
    return _hosted_call(body, "fox_bwd_prep", (S // T,), [chan, chan, pl.BlockSpec((H, 1, T), lambda i: (0, 0, i)), tmaj],
                        [tmaj, norm, tmaj, norm], [tshape, nshape, tshape, nshape], [], (dot, ot, lse, qat),
                        ("arbitrary",), plan)


def _causal(s, k_axis):
    ki = lax.broadcasted_iota(jnp.int32, s.shape, k_axis)
    qi = lax.broadcasted_iota(jnp.int32, s.shape, 1 - k_axis)
    return jnp.where(ki <= qi, s, NEG_INF)


def _seq_tile(i, t):
    return pl.ds(pl.multiple_of(i * t, t), t)


def _attn_forward(ka, qat, vat, S, D, tq, plan=None):
    H = D // HEAD_DIM
    nq = S // tq

    def body(ka_ref, qat_ref, vat_ref, o_ref, o32_ref, lse_ref, m_scr, acc_scr):
        qi = pl.program_id(1)
        m_scr[...] = jnp.full_like(m_scr, NEG_INF)
        acc_scr[...] = jnp.zeros_like(acc_scr)
        qa = qat_ref[...]

        def span(k0, n, diagonal):
            s = jnp.dot(ka_ref[pl.ds(pl.multiple_of(k0 * tq, tq), n * tq), :], qa, preferred_element_type=F32)
            if diagonal:
                s = _causal(s, 0)
            m_prev = m_scr[...]
            m_new = jnp.maximum(m_prev, jnp.max(s, axis=0, keepdims=True))
            p = jnp.exp(s - m_new).astype(BF16)
            upd = jnp.dot(vat_ref[k0], p[0:tq], preferred_element_type=F32)
            for i in range(1, n):
                upd = upd + jnp.dot(vat_ref[k0 + i], p[i * tq:(i + 1) * tq], preferred_element_type=F32)
            acc_scr[...] = jnp.exp(m_prev - m_new) * acc_scr[...] + upd
            m_scr[...] = m_new

        def off_diagonal_pair(j, _):
            span(2 * j, 2, False)
            return 0

        lax.fori_loop(0, qi // 2, off_diagonal_pair, 0)
        pl.when(qi % 2 == 1)(lambda: span(qi - 1, 1, False))
        span(qi, 1, True)
        l = acc_scr[ONES_ROW_V:ONES_ROW_V + 1, :]
        o = acc_scr[0:HEAD_DIM, :] / l
        o_ref[...] = o.astype(BF16)
        o32_ref[...] = o
        lse_ref[...] = m_scr[...] + jnp.log(l)

    chan = pl.BlockSpec((HEAD_DIM, tq), lambda h, i: (h, i))
    stat = pl.BlockSpec((None, 1, tq), lambda h, i: (h, 0, i))
    return _hosted_call(
        body, "attn_forward", (H, nq),
        [pl.BlockSpec((None, S, HEAD_ROWS), lambda h, i: (h, 0, 0)),
         pl.BlockSpec((None, None, HEAD_ROWS, tq), lambda h, i: (h, i, 0, 0)),
         pl.BlockSpec((None, nq, HEAD_ROWS, tq), lambda h, i: (h, 0, 0, 0))],
        [chan, chan, stat],
        [jax.ShapeDtypeStruct((D, S), BF16), jax.ShapeDtypeStruct((D, S), F32), jax.ShapeDtypeStruct((H, 1, S), F32)],
        [pltpu.VMEM((1, tq), F32), pltpu.VMEM((HEAD_ROWS, tq), F32)],
        (ka, qat, vat), ("arbitrary", "arbitrary"), plan)


def _attn_backward(qa, doa, qat, doat, ka, kat, vat, S, D, tq, plan=None):
    H = D // HEAD_DIM
    nq = S // tq

    def body(qa_ref, doa_ref, qat_ref, doat_ref, ka_ref, kat_ref, vat_ref, dq_ref, dk_ref, dv_ref, dk_scr, dv_scr):
        ki = pl.program_id(1)

        @pl.when(ki == 0)
        def _():
            dq_ref[...] = jnp.zeros_like(dq_ref)

        dk_scr[...] = jnp.zeros_like(dk_scr)
        dv_scr[...] = jnp.zeros_like(dv_scr)
        kt = kat_ref[...]
        vt = vat_ref[...]
        kn = ka_ref[...]

        def span(q0, n, diagonal):
            rows = pl.ds(pl.multiple_of(q0 * tq, tq), n * tq)
            s = jnp.dot(qa_ref[rows, :], kt, preferred_element_type=F32)
            if diagonal:
                s = _causal(s, 1)
            p = jnp.exp(s)
            ds = (p * jnp.dot(doa_ref[rows, :], vt, preferred_element_type=F32)).astype(BF16)
            p = p.astype(BF16)
            for i in range(n):
                part = slice(i * tq, (i + 1) * tq)
                dv_scr[...] += jnp.dot(doat_ref[q0 + i, 0:HEAD_DIM, :], p[part], preferred_element_type=F32)
                dk_scr[...] += jnp.dot(qat_ref[q0 + i], ds[part], preferred_element_type=F32)
            dq_ref[rows, :] += jnp.dot(ds, kn, preferred_element_type=F32)

        def off_diagonal_pair(j, _):
            span(ki + 1 + 2 * j, 2, False)
            return 0

        span(ki, 1, True)
        n_off = nq - 1 - ki
        lax.fori_loop(0, n_off // 2, off_diagonal_pair, 0)
        pl.when(n_off % 2 == 1)(lambda: span(nq - 1, 1, False))
        dk_ref[...] = dk_scr[...]
        dv_ref[...] = dv_scr[...].astype(BF16)

    whole = pl.BlockSpec((None, S, HEAD_ROWS), lambda h, i: (h, 0, 0))
    tiles = pl.BlockSpec((None, nq, HEAD_ROWS, tq), lambda h, i: (h, 0, 0, 0))
    one = pl.BlockSpec((None, None, HEAD_ROWS, tq), lambda h, i: (h, i, 0, 0))
    return _hosted_call(
        body, "attn_backward", (H, nq),
        [whole, whole, tiles, tiles, pl.BlockSpec((None, tq, HEAD_ROWS), lambda h, i: (h, i, 0)), one, one],
        [whole, pl.BlockSpec((None, HEAD_ROWS, tq), lambda h, i: (h, 0, i)),
         pl.BlockSpec((HEAD_DIM, tq), lambda h, i: (h, i))],
        [jax.ShapeDtypeStruct((H, S, HEAD_ROWS), F32), jax.ShapeDtypeStruct((H, HEAD_ROWS, S), F32),
         jax.ShapeDtypeStruct((D, S), BF16)],
        [pltpu.VMEM((HEAD_ROWS, tq), F32), pltpu.VMEM((HEAD_DIM, tq), F32)],
        (qa, doa, qat, doat, ka, kat, vat), ("arbitrary", "arbitrary"), plan)


def _fox_prep_bwd(ut, dq, dkt, dvt, b_f, qg, kg, S, D, tq):
    H = D // HEAD_DIM
    T = min(tq, 256)
    nT = S // T
    NU = 3 * D + LANES
    scale = HEAD_DIM ** -0.5

    def body(q_ref, k_ref, f_ref, dq_ref, dk_ref, dv_ref, bf_ref, qg_ref, kg_ref, tri_ref,
             du_ref, dbf_ref, dqg_ref, dkg_ref, gq_acc, gk_acc, fcar, dc_scr):
        step = pl.program_id(0)

        @pl.when(step == 0)
        def _():
            for ref in (gq_acc, gk_acc, fcar, dbf_ref):
                ref[...] = jnp.zeros_like(ref)

        dc_scr[...] = jnp.zeros_like(dc_scr)

        def head(h, _):
            rows = _head_rows(h)
            dqb = dq_ref[h].T
            dkb = dk_ref[h]
            dc_scr[pl.ds(h, 1), :] = dqb[ONES_COL_K:ONES_COL_K + 1, :] - dkb[ONES_ROW_Q:ONES_ROW_Q + 1, :]
            for src, dsrc, gain, acc, mul, base in ((q_ref, dqb, qg_ref, gq_acc, scale, 0),
                                                    (k_ref, dkb, kg_ref, gk_acc, 1.0, D)):
                x = src[rows, :]
                rs = lax.rsqrt(jnp.mean(x * x, axis=0, keepdims=True) + EPS)
                xhat = x * rs
                dn = dsrc[0:HEAD_DIM, :] * mul
                acc[rows, :] += jnp.sum(dn * xhat, axis=1, keepdims=True)
                dxh = dn * gain[rows, :]
                dx = rs * (dxh - xhat * jnp.mean(dxh * xhat, axis=0, keepdims=True))
                du_ref[pl.ds(pl.multiple_of(base + h * HEAD_DIM, HEAD_DIM), HEAD_DIM), :] = dx.astype(BF16)
            return 0

        lax.fori_loop(0, H, head, 0)
        du_ref[2 * D:3 * D, :] = dv_ref[...]
        dlf, carry = _lane_cumsum(dc_scr[...], tri_ref, fcar[...], True)
        fcar[...] = carry
        dfl = dlf * _sigmoid(-(f_ref[...] + bf_ref[...]))
        dbf_ref[...] += jnp.sum(dfl, axis=1, keepdims=True)
        du_ref[3 * D:NU, :] = dfl.astype(BF16)

        @pl.when(step == nT - 1)
        def _():
            for acc, ref in ((gq_acc, dqg_ref), (gk_acc, dkg_ref)):
                tot = jnp.zeros((HEAD_DIM, 1), F32)
                for h in range(H):
                    tot = tot + acc[h * HEAD_DIM:(h + 1) * HEAD_DIM, :]
                ref[...] = tot

    rev = lambda i: nT - 1 - i
    part = lambda j: pl.BlockSpec((D, T), lambda i: (j, rev(i)))
    colv = lambda n: pl.BlockSpec((n, 1), lambda i: (0, 0))
    return pl.pallas_call(
        body, name="fox_prep_bwd", grid=(nT,),
        in_specs=[part(0), part(1), pl.BlockSpec((LANES, T), lambda i: (3 * D // LANES, rev(i))),
                  pl.BlockSpec((H, T, HEAD_ROWS), lambda i: (0, rev(i), 0)),
                  pl.BlockSpec((H, HEAD_ROWS, T), lambda i: (0, 0, rev(i))), pl.BlockSpec((D, T), lambda i: (0, rev(i))),
                  colv(LANES), colv(D), colv(D), pl.BlockSpec((LANES, LANES), lambda i: (0, 0))],
        out_specs=[pl.BlockSpec((NU, T), lambda i: (0, rev(i))), colv(LANES), colv(HEAD_DIM), colv(HEAD_DIM)],
        out_shape=[jax.ShapeDtypeStruct((NU, S), BF16), jax.ShapeDtypeStruct((LANES, 1), F32),
                   jax.ShapeDtypeStruct((HEAD_DIM, 1), F32), jax.ShapeDtypeStruct((HEAD_DIM, 1), F32)],
        scratch_shapes=[pltpu.VMEM((D, 1), F32), pltpu.VMEM((D, 1), F32), pltpu.VMEM((LANES, 1), F32),
                        pltpu.VMEM((LANES, T), F32)],
        compiler_params=_params(("arbitrary",)),
    )(ut, ut, ut, dq, dkt, dvt, b_f, qg, kg, _tri_matrix(True))


def _block_diag_tiles(w):
    n = w.shape[0]
    per = min(MXU_DIM, n * LRU_BLOCK_DIM) // LRU_BLOCK_DIM
    eye = jnp.eye(per, dtype=w.dtype)
    w5 = w.reshape(n // per, per, LRU_BLOCK_DIM, 1, LRU_BLOCK_DIM) * eye[None, :, None, :, None]
    return w5.reshape(n // per, per * LRU_BLOCK_DIM, per * LRU_BLOCK_DIM).astype(BF16)


def _block_diag_extract(t, n):
    per = t.shape[-1] // LRU_BLOCK_DIM
    eye = jnp.eye(per, dtype=t.dtype)
    t5 = t.reshape(n // per, per, LRU_BLOCK_DIM, per, LRU_BLOCK_DIM) * eye[None, :, None, :, None]
    return t5.sum(axis=3).reshape(n, LRU_BLOCK_DIM, LRU_BLOCK_DIM)


def _local_step(x, tgt, small, wv, grad_view, comm=None):
    S, D = x.shape
    F = 4 * D
    H = D // HEAD_DIM
    nblk = D // LRU_BLOCK_DIM
    NU = 3 * D + LANES
    tq = max(LANES, min(512, S // 4))
    assert S % tq == 0
    vec = lambda a: a.reshape(1, -1).astype(F32)
    col = lambda a: a.reshape(-1, 1).astype(F32)
    mix_g, mlp_g = small["mix_norm"], small["mlp_norm"]
    conv_w, conv_b = small["conv_w"], vec(small["lru_conv_b"])
    wr_bd, wi_bd = _block_diag_tiles(small["lru_w_r"][0]), _block_diag_tiles(small["lru_w_i"][0])
    b_r, b_i, lam = vec(small["lru_b_r"]), vec(small["lru_b_i"]), vec(small["lru_lambda"])
    b_f = jnp.pad(col(small["fox_b_f"]), ((0, LANES - H), (0, 0)))
    qg, kg = jnp.tile(col(small["fox_q_gain"]), (H, 1)), jnp.tile(col(small["fox_k_gain"]), (H, 1))
    X = lambda a: _View(a)
    grads = {}
    gout = functools.partial(grad_view, grads)

    def hosted(name, fn, *args):
        plan = comm.before(name, grads) if comm is not None else None
        res, side = fn(*args, plan=plan)
        if plan is not None:
            comm.after(name, side, wv)
        return res

    def hosted_mm(name, *args, **kw):
        plan = comm.before(name, grads) if comm is not None else None
        if plan is None:
            return _matmul(name, *args, **kw)
        res, side = _matmul(name, *args, plan=plan, **kw)
        comm.after(name, side, wv)
        return res

    norm_rows = _pick(S, (512, 256, 128))
    two = lambda: [_fresh(S, D, F32), _fresh(S, D, BF16)]

    def mlp_up(l, hm):
        return hosted_mm(f"mlp{l}_up", X(hm), wv[f"w1_{l}"], S, F, D, outs=[_fresh(S, F, BF16), _fresh(S, F, BF16)],
                         epilogue=_ep_relu2)

    def mlp_bwd(l, xin, hm, z, act, d, db):
        (dz,) = hosted_mm(f"mlp{l}_dact", X(db), wv[f"w2_{l}"], S, F, D, tb=True, outs=[_fresh(S, F, BF16)],
                          epilogue=_ep_drelu2, extras=[X(z)])
        (grads[f"w2_{l}"],) = _matmul(f"mlp{l}_dw2", X(act), X(db), F, D, S, ta=True, outs=[gout(f"w2_{l}")],
                                      epilogue=_ep_store)
        (grads[f"w1_{l}"],) = _matmul(f"mlp{l}_dw1", X(hm), X(dz), D, F, S, ta=True, outs=[gout(f"w1_{l}")],
                                      epilogue=_ep_store)
        return _matmul(f"mlp{l}_dhm", X(dz), wv[f"w1_{l}"], S, D, F, tb=True, outs=two(), n_sums=1,
                       epilogue=_ep_norm_bwd, extras=[X(xin), X(d)], vecs=[mlp_g[l:l + 1]], tm=norm_rows)

    h0 = _rms_fwd("mix0_norm", x, mix_g[0:1], S, D)
    (u0,) = _matmul("lru_in", X(h0), wv["lru_in"], S, 2 * D, D, outs=[_fresh(S, 2 * D, F32)], epilogue=_ep_store)
    y, xc, r, ig, hs = hosted("lru_fwd", _lru_fwd, u0, conv_w, conv_b, wr_bd, b_r, wi_bd, b_i, lam, S, D)
    x1, hm0 = _matmul("lru_out", X(y), wv["lru_out"], S, D, D, outs=two(), epilogue=_ep_resid_norm, extras=[X(x)],
                      vecs=[mlp_g[0:1]], tm=norm_rows)
    z0, act0 = mlp_up(0, hm0)
    x2, h1 = hosted_mm("mlp0_down", X(act0), wv["w2_0"], S, D, F, outs=two(), epilogue=_ep_resid_norm, extras=[X(x1)],
                       vecs=[mix_g[1:2]], tm=norm_rows)
    (u1,) = _matmul("fox_in", wv["fox_in"], X(h1), NU, S, D, tb=True, outs=[_fresh(NU, S, F32)], epilogue=_ep_store)
    qat, kat, vat, ka = _fox_prep(u1, b_f, qg, kg, S, D, tq)
    o, o32, lse = hosted("attn_forward", _attn_forward, ka, qat, vat, S, D, tq)
    x3, hm1 = _matmul("fox_out", X(o), wv["fox_out"], S, D, D, ta=True, outs=two(), epilogue=_ep_resid_norm,
                      extras=[X(x2)], vecs=[mlp_g[1:2]], tm=norm_rows)
    z1, act1 = mlp_up(1, hm1)
    (x4,) = _matmul("mlp1_down", X(act1), wv["w2_1"], S, D, F, outs=[_fresh(S, D, F32)], epilogue=_ep_resid,
                    extras=[X(x3)])
    loss, d4, d4b = _loss_head(x4, tgt, S, D)

    d3, d3b, dg_mlp1 = mlp_bwd(1, x3, hm1, z1, act1, d4, d4b)
    (do,) = _matmul("fox_dout", wv["fox_out"], X(d3b), D, S, D, tb=True, outs=[_fresh(D, S, BF16)], epilogue=_ep_store)
    (grads["fox_out"],) = _matmul("fox_dwout", X(o), X(d3b), D, D, S, outs=[gout("fox_out")], epilogue=_ep_store)
    doat, doa, qat1, qa1 = hosted("fox_bwd_prep", _fox_bwd_prep, do, o32, lse, qat, S, D, tq)
    dqn, dkn, dv = hosted("attn_backward", _attn_backward, qa1, doa, qat1, doat, ka, kat, vat, S, D, tq)
    du1, dbf, dqg, dkg = _fox_prep_bwd(u1, dqn, dkn, dv, b_f, qg, kg, S, D, tq)
    (grads["fox_in"],) = _matmul("fox_dwin", X(h1), X(du1), D, NU, S, ta=True, tb=True, outs=[gout("fox_in")],
                                 epilogue=_ep_store)
    d2, d2b, dg_mix1 = hosted_mm("fox_dh", X(du1), wv["fox_in"], S, D, NU, ta=True, outs=two(), n_sums=1,
                               epilogue=_ep_norm_bwd, extras=[X(x2), X(d3)], vecs=[mix_g[1:2]], tm=norm_rows)
    d1, d1b, dg_mlp0 = mlp_bwd(0, x1, hm0, z0, act0, d2, d2b)
    (grads["lru_out"],) = _matmul("lru_dwout", X(y), X(d1b), D, D, S, ta=True, outs=[gout("lru_out")],
                                  epilogue=_ep_store)
    (dy,) = hosted_mm("lru_dout", X(d1b), wv["lru_out"], S, D, D, tb=True, outs=[_fresh(S, D, F32)],
                      epilogue=_ep_store)
    du0, dcw, dcb, dlam, dbr, dbi, dwr, dwi = hosted("lru_bwd", _lru_bwd, dy, u0, xc, r, ig, hs, conv_w, wr_bd, wi_bd,
                                                     lam, S, D)
    (grads["lru_in"],) = _matmul("lru_dwin", X(h0), X(du0), D, 2 * D, S, ta=True, outs=[gout("lru_in")],
                                 epilogue=_ep_store)
    gx, dg_mix0 = hosted_mm("lru_dh", X(du0), wv["lru_in"], S, D, 2 * D, tb=True, outs=[_fresh(S, D, F32)], n_sums=1,
                            epilogue=lambda *a: _ep_norm_bwd(*a)[::2], extras=[X(x), X(d1)], vecs=[mix_g[0:1]],
                            tm=norm_rows)

    grads.update(
        mix_norm=jnp.concatenate([dg_mix0, dg_mix1], axis=0), mlp_norm=jnp.concatenate([dg_mlp0, dg_mlp1], axis=0),
        conv_w=dcw, lru_conv_b=dcb, lru_w_r=_block_diag_extract(dwr, nblk)[None], lru_b_r=dbr.reshape(1, nblk, -1),
        lru_w_i=_block_diag_extract(dwi, nblk)[None], lru_b_i=dbi.reshape(1, nblk, -1), lru_lambda=dlam,
        fox_b_f=dbf[:H].reshape(1, H), fox_q_gain=dqg.reshape(1, -1), fox_k_gain=dkg.reshape(1, -1))
    return loss, gx, grads


def _place():
    x, y, c = lax.axis_index("x"), lax.axis_index("y"), lax.axis_index("c")
    chips = [(1 - x, y), (x, 1 - y), (1 - x, 1 - y)]
    return x, y, c, 2 * x + y, chips


BOUNCE_BYTES = 1 << 20


def _bounce_shape(rows, cols, dtype):
    chunk = rows
    while chunk % 2 == 0 and chunk > 16 and chunk * cols * jnp.dtype(dtype).itemsize > BOUNCE_BYTES:
        chunk //= 2
    return pltpu.VMEM((2, chunk, cols), dtype)


def _bounce_copy(src, dst, buf, sem):
    chunk = buf.shape[1]
    n = src.shape[0] // chunk
    cin = lambda i: pltpu.make_async_copy(src.at[pl.ds(i * chunk, chunk)], buf.at[i % 2], sem.at[i % 2])
    cout = lambda i: pltpu.make_async_copy(buf.at[i % 2], dst.at[pl.ds(i * chunk, chunk)], sem.at[2 + i % 2])
    cin(0).start()
    for i in range(n):
        cin(i).wait()
        if i + 1 < n:
            if i >= 1:
                cout(i - 1).wait()
            cin(i + 1).start()
        cout(i).start()
    if n >= 2:
        cout(n - 2).wait()
    cout(n - 1).wait()


def _hbm_call(body, name, arrays, out_shape, n_dma_sems, bounce=()):
    scratch = [pltpu.SemaphoreType.DMA((k,)) for k in n_dma_sems]
    for rows, cols, dtype in bounce:
        scratch += [_bounce_shape(rows, cols, dtype), pltpu.SemaphoreType.DMA((4,))]
    return pl.pallas_call(
        body, name=name, in_specs=[ANY] * len(arrays), out_specs=[ANY] * len(out_shape), out_shape=out_shape,
        scratch_shapes=scratch,
        compiler_params=pltpu.CompilerParams(has_side_effects=True, vmem_limit_bytes=VMEM_LIMIT),
    )(*arrays)


class _Gather:
    def __init__(self, shards):
        n = self.n = len(shards)
        self.operands = list(shards)
        self.out_shape = [jax.ShapeDtypeStruct((N_CHIPS,) + tuple(a.shape), a.dtype) for a in shards]
        self.scratch = [pltpu.SemaphoreType.DMA((3 * n,)) for _ in range(4)]
        for a in shards:
            self.scratch += [_bounce_shape(a.shape[0], a.shape[1], a.dtype), pltpu.SemaphoreType.DMA((4,))]

    def _copies(self, ins, outs, scr):
        send, recv, fsend, frecv = scr[:4]
        x, y, c, s, chips = _place()

        def rows(a, chip_idx, which):
            hr = ins[a].shape[0] // 2
            return outs[a].at[chip_idx, pl.ds(which * hr, hr)]

        def landed(a, j, core):
            return rows(a, 2 * chips[j][0] + chips[j][1], core)

        def ici(a, j, mine):
            hr = ins[a].shape[0] // 2
            src, dst = (ins[a].at[pl.ds(c * hr, hr)], rows(a, s, c)) if mine else (landed(a, j, c),) * 2
            return pltpu.make_async_remote_copy(src_ref=src, dst_ref=dst, send_sem=send.at[3 * a + j],
                                                recv_sem=recv.at[3 * a + j], device_id=(*chips[j], c),
                                                device_id_type=MESH)

        def d2d(a, j, mine):
            ref = landed(a, j, c if mine else 1 - c)
            return pltpu.make_async_remote_copy(src_ref=ref, dst_ref=ref, send_sem=fsend.at[3 * a + j],
                                                recv_sem=frecv.at[3 * a + j], device_id=(x, y, 1 - c),
                                                device_id_type=MESH)

        return ici, d2d, s

    def start(self, ins, outs, scr):
        ici, _, _ = self._copies(ins, outs, scr)
        for a in range(self.n):
            for j in range(3):
                ici(a, j, True).start()

    def middle(self, ins, outs, scr):
        ici, d2d, s = self._copies(ins, outs, scr)
        for a in range(self.n):
            _bounce_copy(ins[a], outs[a].at[s], scr[4 + 2 * a], scr[5 + 2 * a])
        for a in range(self.n):
            for j in range(3):
                ici(a, j, False).wait_recv()
                d2d(a, j, True).start()

    def finish(self, ins, outs, scr):
        ici, d2d, _ = self._copies(ins, outs, scr)
        for a in range(self.n):
            for j in range(3):
                d2d(a, j, False).wait_recv()
        for a in range(self.n):
            for j in range(3):
                ici(a, j, True).wait_send()
                d2d(a, j, True).wait_send()


def _run_plan(name, plan):
    k_in, k_out = len(plan.operands), len(plan.out_shape)

    def body(*refs):
        parts = (refs[:k_in], refs[k_in:k_in + k_out], refs[k_in + k_out:])
        plan.start(*parts)
        plan.middle(*parts)
        plan.finish(*parts)

    return pl.pallas_call(
        body, name=name, in_specs=[ANY] * k_in, out_specs=[ANY] * k_out, out_shape=plan.out_shape,
        scratch_shapes=plan.scratch,
        compiler_params=pltpu.CompilerParams(has_side_effects=True, vmem_limit_bytes=VMEM_LIMIT),
    )(*plan.operands)


def _hosted_call(body, name, grid, in_specs, out_specs, out_shape, scratch_shapes, operands, sem, plan=None):
    if plan is None:
        res = pl.pallas_call(body, name=name, grid=grid, in_specs=in_specs, out_specs=out_specs, out_shape=out_shape,
                             scratch_shapes=scratch_shapes, compiler_params=_params(sem))(*operands)
        return res, None
    n_in, n_out, n_scr = len(in_specs), len(out_specs), len(scratch_shapes)
    k_in, k_out = len(plan.operands), len(plan.out_shape)
    total = int(np.prod(grid))

    def hosted(*refs):
        ins, refs = refs[:n_in], refs[n_in:]
        p_ins, refs = refs[:k_in], refs[k_in:]
        outs, refs = refs[:n_out], refs[n_out:]
        p_outs, refs = refs[:k_out], refs[k_out:]
        scr, p_scr = refs[:n_scr], refs[n_scr:]
        step = pl.program_id(0)
        for d in range(1, len(grid)):
            step = step * grid[d] + pl.program_id(d)
        pl.when(step == 0)(lambda: plan.start(p_ins, p_outs, p_scr))
        body(*ins, *outs, *scr)
        pl.when(step == total // 2)(lambda: plan.middle(p_ins, p_outs, p_scr))
        pl.when(step == total - 1)(lambda: plan.finish(p_ins, p_outs, p_scr))

    res = pl.pallas_call(
        hosted, name=name, grid=grid, in_specs=list(in_specs) + [ANY] * k_in, out_specs=list(out_specs) + [ANY] * k_out,
        out_shape=list(out_shape) + plan.out_shape, scratch_shapes=list(scratch_shapes) + plan.scratch,
        compiler_params=pltpu.CompilerParams(dimension_semantics=sem, vmem_limit_bytes=VMEM_LIMIT,
                                             has_side_effects=True),
    )(*operands, *plan.operands)
    return res[:n_out], res[n_out:]


def _all_gather(name, shards):
    return _run_plan(name, _Gather(shards))


class _Swap:
    def __init__(self, arrs):
        self.n = len(arrs)
        self.operands = list(arrs)
        self.out_shape = [jax.ShapeDtypeStruct((a.shape[0], a.shape[1] // 2, a.shape[2]), a.dtype) for a in arrs]
        self.scratch = [pltpu.SemaphoreType.DMA((self.n,)) for _ in range(2)]

    def _copy(self, ins, outs, scr, a):
        x, y, c, _, _ = _place()
        hr = ins[a].shape[1] // 2
        return pltpu.make_async_remote_copy(
            src_ref=ins[a].at[:, pl.ds((1 - c) * hr, hr)], dst_ref=outs[a], send_sem=scr[0].at[a],
            recv_sem=scr[1].at[a], device_id=(x, y, 1 - c), device_id_type=MESH)

    def start(self, ins, outs, scr):
        for a in range(self.n):
            self._copy(ins, outs, scr, a).start()

    def middle(self, ins, outs, scr):
        pass

    def finish(self, ins, outs, scr):
        for a in range(self.n):
            self._copy(ins, outs, scr, a).wait()


class _Scatter:
    def __init__(self, parts):
        n = self.n = len(parts)
        self.operands = list(parts)
        self.out_shape = [jax.ShapeDtypeStruct(a.shape, a.dtype) for a in parts]
        self.scratch = [pltpu.SemaphoreType.DMA((3 * n,)) for _ in range(2)]
        for a in parts:
            self.scratch += [_bounce_shape(a.shape[1], a.shape[2], a.dtype), pltpu.SemaphoreType.DMA((4,))]

    def _copy(self, ins, outs, scr, a, j, mine):
        x, y, c, s, chips = _place()
        t = 2 * chips[j][0] + chips[j][1]
        return pltpu.make_async_remote_copy(
            src_ref=ins[a].at[t], dst_ref=outs[a].at[s if mine else t], send_sem=scr[0].at[3 * a + j],
            recv_sem=scr[1].at[3 * a + j], device_id=(*chips[j], c), device_id_type=MESH)

    def start(self, ins, outs, scr):
        for a in range(self.n):
            for j in range(3):
                self._copy(ins, outs, scr, a, j, True).start()

    def middle(self, ins, outs, scr):
        s = _place()[3]
        for a in range(self.n):
            _bounce_copy(ins[a].at[s], outs[a].at[s], scr[2 + 2 * a], scr[3 + 2 * a])

    def finish(self, ins, outs, scr):
        for a in range(self.n):
            for j in range(3):
                self._copy(ins, outs, scr, a, j, False).wait_recv()
        for a in range(self.n):
            for j in range(3):
                self._copy(ins, outs, scr, a, j, True).wait_send()


def _pair_gather(name, halves):
    n = len(halves)

    def body(*refs):
        ins, outs = refs[:n], refs[n:2 * n]
        send, recv = refs[2 * n:2 * n + 2]
        stage = refs[2 * n + 2:]
        x, y, c, _, _ = _place()
        cps = []
        for a in range(n):
            hr = ins[a].shape[0]
            cp = pltpu.make_async_remote_copy(
                src_ref=ins[a], dst_ref=outs[a].at[pl.ds(c * hr, hr)], send_sem=send.at[a], recv_sem=recv.at[a],
                device_id=(x, y, 1 - c), device_id_type=MESH)
            cp.start()
            cps.append((cp, hr))
        for a, (cp, hr) in enumerate(cps):
            _bounce_copy(ins[a], outs[a].at[pl.ds(c * hr, hr)], stage[2 * a], stage[2 * a + 1])
        for a, (cp, hr) in enumerate(cps):
            cp.wait_send()
            theirs = outs[a].at[pl.ds((1 - c) * hr, hr)]
            pltpu.make_async_remote_copy(src_ref=theirs, dst_ref=theirs, send_sem=send.at[a], recv_sem=recv.at[a],
                                         device_id=(x, y, 1 - c), device_id_type=MESH).wait_recv()

    out_shape = [jax.ShapeDtypeStruct((2 * a.shape[0], a.shape[1]), a.dtype) for a in halves]
    return _hbm_call(body, name, halves, out_shape, (n, n),
                     bounce=[(a.shape[0], a.shape[1], a.dtype) for a in halves])


def _row_tile(rows, cols, itemsize, n_bufs):
    budget = VMEM_LIMIT // 2
    for t in (1024, 512, 256, 128, 64, 32, 16):
        if rows % t == 0 and 2 * n_bufs * t * cols * itemsize <= budget:
            return t
    return rows


def _pair_add(name, g, gsib, core, out_dtype):
    _, r, cols = g.shape
    hr = r // 2
    t = _row_tile(hr, cols, 4, 3)
    per = hr // t

    def body(core_ref, a_ref, b_ref, o_ref):
        o_ref[...] = (a_ref[...].astype(F32) + b_ref[...].astype(F32)).astype(o_ref.dtype)

    grid_spec = pltpu.PrefetchScalarGridSpec(
        num_scalar_prefetch=1, grid=(N_CHIPS, per),
        in_specs=[pl.BlockSpec((None, t, cols), lambda s, i, core: (s, core[0] * per + i, 0)),
                  pl.BlockSpec((None, t, cols), lambda s, i, core: (s, i, 0))],
        out_specs=pl.BlockSpec((None, t, cols), lambda s, i, core: (s, i, 0)))
    return pl.pallas_call(body, name=name, grid_spec=grid_spec,
                          out_shape=jax.ShapeDtypeStruct((N_CHIPS, hr, cols), out_dtype),
                          compiler_params=_params(("arbitrary", "arbitrary")))(core, g, gsib)


def _chip_sum(name, parts):
    _, hr, cols = parts.shape
    t = _row_tile(hr, cols, 4, 5)

    def body(p_ref, o_ref):
        o_ref[...] = ((p_ref[0].astype(F32) + p_ref[1].astype(F32)) + p_ref[2].astype(F32)) + p_ref[3].astype(F32)

    return pl.pallas_call(
        body, name=name, grid=(hr // t,), in_specs=[pl.BlockSpec((N_CHIPS, t, cols), lambda i: (0, i, 0))],
        out_specs=pl.BlockSpec((t, cols), lambda i: (i, 0)), out_shape=jax.ShapeDtypeStruct((hr, cols), F32),
        compiler_params=_params(("arbitrary",)))(parts)


def _pair_partials(tag, arrs, sib, wire_dtypes, core):
    return _Scatter([_pair_add(f"{tag}_pair_add{i}", g, gs, core, dt)
                     for i, (g, gs, dt) in enumerate(zip(arrs, sib, wire_dtypes))])


def _finish_reduce(tag, scattered):
    halves = [_chip_sum(f"{tag}_chip_sum{i}", p) for i, p in enumerate(scattered)]
    return _pair_gather(f"{tag}_pair_gather", halves)


def _adamw(name, w, g_parts, m, v):
    rows, cols = w.shape
    n_parts = len(g_parts)
    part_rows = rows // n_parts
    t = _row_tile(part_rows, cols, 4, 7 + n_parts)
    per = part_rows // t
    c1 = 1.0 - ADAM_B1 ** ADAM_STEP
    c2 = 1.0 - ADAM_B2 ** ADAM_STEP

    def body(w_ref, m_ref, v_ref, *refs):
        g_refs, (go_ref, d_ref, nm_ref, nv_ref) = refs[:n_parts], refs[n_parts:]
        g = g_refs[0][...]
        for k in range(1, n_parts):
            g = jnp.where(pl.program_id(0) >= k * per, g_refs[k][...], g)
        go_ref[...] = g
        m = ADAM_B1 * m_ref[...] + (1.0 - ADAM_B1) * g
        v = ADAM_B2 * v_ref[...] + (1.0 - ADAM_B2) * (g * g)
        nm_ref[...] = m
        nv_ref[...] = v
        d_ref[...] = -ADAM_LR * ((m / c1) / (jnp.sqrt(v / c2) + ADAM_EPS) + ADAM_WD * w_ref[...])

    spec = pl.BlockSpec((t, cols), lambda i: (i, 0))
    g_specs = [pl.BlockSpec((t, cols), lambda i, k=k: (jnp.clip(i - k * per, 0, per - 1), 0)) for k in range(n_parts)]
    shp = jax.ShapeDtypeStruct((rows, cols), F32)
    return pl.pallas_call(body, name=name, grid=(rows // t,), in_specs=[spec] * 3 + g_specs, out_specs=[spec] * 4,
                          out_shape=[shp] * 4, compiler_params=_params(("arbitrary",)))(w, m, v, *g_parts)


_WEIGHTS = ["mix_norm", "mlp_norm", "mlp_w1", "mlp_w2", "lru_w_in", "lru_conv_w", "lru_conv_b", "lru_w_r", "lru_b_r",
            "lru_w_i", "lru_b_i", "lru_lambda", "lru_w_out", "fox_w_in", "fox_b_f", "fox_q_gain", "fox_k_gain",
            "fox_w_out"]
_REPLICATED = ["mix_norm", "mlp_norm", "lru_conv_b", "lru_w_r", "lru_b_r", "lru_w_i", "lru_b_i", "lru_lambda",
               "fox_b_f", "fox_q_gain", "fox_k_gain"]
_PACK_TILE = 2 * SUBLANES * LANES


def _as2d(a):
    return a.reshape(-1, a.shape[-1])


def kernel(x, mix_norm, mlp_norm, mlp_w1, mlp_w2, lru_w_in, lru_conv_w, lru_conv_b, lru_w_r, lru_b_r, lru_w_i, lru_b_i, lru_lambda, lru_w_out, fox_w_in, fox_b_f, fox_q_gain, fox_k_gain, fox_w_out, loss_target, m_mix_norm, m_mlp_norm, m_mlp_w1, m_mlp_w2, m_lru_w_in, m_lru_conv_w, m_lru_conv_b, m_lru_w_r, m_lru_b_r, m_lru_w_i, m_lru_b_i, m_lru_lambda, m_lru_w_out, m_fox_w_in, m_fox_b_f, m_fox_q_gain, m_fox_k_gain, m_fox_w_out, v_mix_norm, v_mlp_norm, v_mlp_w1, v_mlp_w2, v_lru_w_in, v_lru_conv_w, v_lru_conv_b, v_lru_w_r, v_lru_b_r, v_lru_w_i, v_lru_b_i, v_lru_lambda, v_lru_w_out, v_fox_w_in, v_fox_b_f, v_fox_q_gain, v_fox_k_gain, v_fox_w_out):
    args = dict(locals())
    W = {n: args[n] for n in _WEIGHTS}
    Mo = {n: args["m_" + n] for n in _WEIGHTS}
    Vo = {n: args["v_" + n] for n in _WEIGHTS}
    S, D = x.shape[1], x.shape[2]
    F = 4 * D
    H = D // HEAD_DIM
    NU = 3 * D + LANES
    FQ, DQ = F // N_CHIPS, D // N_CHIPS
    nfox = fox_w_in.shape[-1]
    chip = 2 * lax.axis_index("x") + lax.axis_index("y")
    core = lax.axis_index("c").astype(jnp.int32).reshape(1)

    cw_flat = jnp.pad(lru_conv_w.reshape(-1), (0, _PACK_TILE - CONV_WIDTH * DQ)).reshape(2 * SUBLANES, LANES)
    w1s, w2s = mlp_w1.astype(BF16), mlp_w2.astype(BF16)
    g_lin, g_lout, g_cw = _all_gather("gather_lru", [lru_w_in[0].astype(BF16), lru_w_out[0].astype(BF16), cw_flat])
    conv_w_full = jnp.transpose(g_cw.reshape(N_CHIPS, -1)[:, :CONV_WIDTH * DQ].reshape(N_CHIPS, CONV_WIDTH, DQ),
                                (1, 0, 2)).reshape(CONV_WIDTH, D)
    wv = {"lru_in": _View(g_lin, "cs"), "lru_out": _View(g_lout, "rs")}
    scattered = {}
    members = {"g1": ["w2_1", "w1_1", "fox_out"], "g2": ["fox_in"], "g3": ["w2_0", "w1_0", "lru_out"], "g4": ["lru_in"]}
    swap_at = {"fox_bwd_prep": "g1", "fox_dh": "g2", "lru_dout": "g3"}
    scatter_at = {"attn_backward": "g1", "mlp0_dact": "g2", "lru_bwd": "g3", "lru_dh": "g4"}
    swapped = {}

    def shard_major(name, g):
        if name == "fox_in":
            return jnp.transpose(g[:, :nfox * N_CHIPS].reshape(D, N_CHIPS, nfox), (1, 0, 2))
        return g

    class Comm:
        @staticmethod
        def before(name, grads):
            if name == "lru_fwd":
                return _Gather([w1s[0]])
            if name == "mlp0_up":
                return _Gather([w2s[0]])
            if name == "mlp0_down":
                return _Gather([fox_w_in[0].astype(BF16)])
            if name == "attn_forward":
                return _Gather([fox_w_out[0].astype(BF16), w1s[1], w2s[1]])
            if name in swap_at:
                group = swap_at[name]
                swapped[group] = [[shard_major(n, grads[n]) for n in members[group]], None]
                return _Swap(swapped[group][0])
            if name in scatter_at:
                group = scatter_at[name]
                if group not in swapped:
                    arrs = [shard_major(n, grads[n]) for n in members[group]]
                    swapped[group] = [arrs, _run_plan(f"{group}_pair_swap", _Swap(arrs))]
                arrs, sib = swapped[group]
                return _pair_partials(group, arrs, sib, [BF16] * len(arrs), core)
            return None

        @staticmethod
        def after(name, res, wv):
            if name == "lru_fwd":
                wv.update(w1_0=_View(res[0], "cs"))
            elif name == "mlp0_up":
                wv.update(w2_0=_View(res[0], "rs"))
            elif name == "mlp0_down":
                fox_full = jnp.concatenate([res[0][s] for s in range(N_CHIPS)], axis=1)
                fox_full = jnp.pad(fox_full, ((0, 0), (0, NU - fox_full.shape[1])))
                wv.update(fox_in=_View(fox_full.T))
            elif name == "attn_forward":
                wv.update(fox_out=_View(res[0], "rs"), w1_1=_View(res[1], "cs"), w2_1=_View(res[2], "rs"))
            elif name in swap_at:
                swapped[swap_at[name]][1] = res
            else:
                scattered.update(zip(members[scatter_at[name]], res))

    def grad_view(grads, name):
        if name in ("w1_0", "w1_1"):
            return _View(None, "cs", shape=(N_CHIPS, D, FQ), dtype=BF16)
        if name in ("w2_0", "w2_1"):
            return _View(None, "rs", shape=(N_CHIPS, FQ, D), dtype=BF16)
        if name == "lru_in":
            return _View(None, "cs", shape=(N_CHIPS, D, 2 * D // N_CHIPS), dtype=BF16)
        if name in ("lru_out", "fox_out"):
            return _View(None, "rs", shape=(N_CHIPS, DQ, D), dtype=BF16)
        return _View(None, shape=(D, NU), dtype=BF16)

    small = {n: W[n] for n in _REPLICATED}
    small["conv_w"] = conv_w_full

    loss, gx, grads = _local_step(x[0], loss_target[0], small, wv, grad_view, Comm)

    pack_names = _REPLICATED + ["conv_w"]
    flat = jnp.concatenate([grads[n].reshape(-1).astype(F32) for n in pack_names] + [loss.reshape(-1)])
    per_chip = -(-flat.shape[0] // (N_CHIPS * _PACK_TILE)) * _PACK_TILE
    pack = jnp.pad(flat, (0, N_CHIPS * per_chip - flat.shape[0])).reshape(N_CHIPS, per_chip // LANES, LANES)
    pack_sib = _run_plan("pack_pair_swap", _Swap([pack]))
    (scattered["pack"],) = _run_plan("pack_chip_scatter", _pair_partials("pack", [pack], pack_sib, [F32], core))
    order = ["w1_0", "w1_1", "w2_0", "w2_1", "lru_in", "lru_out", "fox_in", "fox_out", "pack"]
    red = dict(zip(order, _finish_reduce("grads", [scattered[n] for n in order])))
    (all_pack,) = _all_gather("gather_small_grads", [red["pack"]])
    all_flat = all_pack.reshape(-1)
    G = {}
    off = 0
    for n in pack_names:
        shape = grads[n].shape if n == "conv_w" else W[n].shape
        size = int(np.prod(shape))
        G[n] = all_flat[off:off + size].reshape(shape)
        off += size
    total = all_flat[off]
    G["lru_conv_w"] = lax.dynamic_slice_in_dim(G.pop("conv_w"), chip * DQ, DQ, axis=1)[None]
    parts = {n: [_as2d(G[n])] for n in G}
    parts.update(mlp_w1=[red["w1_0"], red["w1_1"]], mlp_w2=[red["w2_0"], red["w2_1"]], lru_w_in=[red["lru_in"]],
                 lru_w_out=[red["lru_out"]], fox_w_in=[red["fox_in"]], fox_w_out=[red["fox_out"]])

    delta, new_m, new_v = {}, {}, {}
    for n in _WEIGHTS:
        go, d, nm, nv = _adamw(f"adamw_{n}", _as2d(W[n]), parts[n], _as2d(Mo[n]), _as2d(Vo[n]))
        G[n], delta[n], new_m[n], new_v[n] = (t.reshape(W[n].shape) for t in (go, d, nm, nv))

    return (total, gx[None], *[G[n] for n in _WEIGHTS], *[delta[n] for n in _WEIGHTS],
            *[new_m[n] for n in _WEIGHTS], *[new_v[n] for n in _WEIGHTS])
```

```python
import functools

import numpy as np
import jax
import jax.numpy as jnp
from jax import lax
from jax.experimental import pallas as pl
from jax.experimental.pallas import tpu as pltpu

F32 = jnp.float32
BF16 = jnp.bfloat16

HEAD_DIM = 64
LRU_BLOCK_DIM = 64
CONV_WIDTH = 4
LRU_C = 8.0
EPS = 1e-6
NEG_INF = -1e30
ADAM_LR = 0.001
ADAM_B1 = 0.9
ADAM_B2 = 0.999
ADAM_EPS = 1e-08
ADAM_WD = 0.01
ADAM_STEP = 10

N_CHIPS = 4
LANES = 128
SUBLANES = 8
MXU_DIM = 256
VMEM_LIMIT = 52 * 1024 * 1024
MESH = pl.DeviceIdType.MESH
ANY = pl.BlockSpec(memory_space=pl.ANY)


def _pick(n, prefs):
    for p in prefs:
        if p <= n and n % p == 0:
            return p
    return n


def _params(sem=None):
    return pltpu.CompilerParams(dimension_semantics=sem, vmem_limit_bytes=VMEM_LIMIT)


class _View:
    def __init__(self, arr, kind="plain", r0=0, rows=None, shape=None, dtype=None):
        self.arr = arr
        self.kind = kind
        self.r0 = r0
        self.shape = tuple(arr.shape) if arr is not None else tuple(shape)
        self.dtype = arr.dtype if arr is not None else dtype
        self.rows = rows if rows is not None else self.shape[-2]

    def limits(self):
        if self.kind == "plain":
            return 0, 0
        rows = int(np.gcd(self.rows, self.r0))
        return rows, (self.shape[-1] if self.kind == "cs" else 0)

    def spec(self, br, bc, fr, fc):
        if self.kind == "plain":
            return pl.BlockSpec((br, bc), lambda *g: (fr(*g), fc(*g)))
        ncol = self.shape[-1]
        r0b = self.r0 // br
        assert self.r0 % br == 0 and self.rows % br == 0 and ncol % bc == 0, (self.shape, self.r0, br, bc)
        if self.kind == "cs":
            per = ncol // bc
            return pl.BlockSpec((None, br, bc), lambda *g: (fc(*g) // per, r0b + fr(*g), fc(*g) % per))
        per = self.rows // br
        return pl.BlockSpec((None, br, bc), lambda *g: (fr(*g) // per, r0b + fr(*g) % per, fc(*g)))


def _bf(x):
    return x if x.dtype == BF16 else x.astype(BF16)


def _matmul(name, A, B, M, N, K, *, ta=False, tb=False, outs, epilogue, extras=(), vecs=(), n_sums=0,
            tm=None, tn=None, tk=None, plan=None):
    lim = {"m": [M], "n": [N], "k": [K]}
    for view, (rdim, cdim) in ([(A, "km" if ta else "mk"), (B, "nk" if tb else "kn")]
                               + [(e, "mn") for e in extras] + [(o, "mn") for o in outs]):
        r_lim, c_lim = view.limits()
        lim[rdim].append(r_lim)
        lim[cdim].append(c_lim)
    tm = tm or _pick(int(np.gcd.reduce(lim["m"])), (1024, 640, 512, 256, 128))
    tn = tn or _pick(int(np.gcd.reduce(lim["n"])), (1024, 640, 512, 256, 128))
    tk = tk or _pick(int(np.gcd.reduce(lim["k"])), (1024, 640, 512, 256, 128))
    nk = K // tk
    gi = lambda i, j, k: i
    gj = lambda i, j, k: j
    gk = lambda i, j, k: k
    a_spec = A.spec(tk, tm, gk, gi) if ta else A.spec(tm, tk, gi, gk)
    b_spec = B.spec(tn, tk, gj, gk) if tb else B.spec(tk, tn, gk, gj)
    ca = 0 if ta else 1
    cb = 1 if tb else 0
    ne, no = len(extras) + len(vecs), len(outs)
    assert n_sums == 0 or tn == N
    row_spec = pl.BlockSpec((1, tn), lambda i, j, k: (0, j))
    in_specs = [a_spec, b_spec] + [e.spec(tm, tn, gi, gj) for e in extras] + [row_spec] * len(vecs)
    operands = [A.arr, B.arr] + [e.arr for e in extras] + list(vecs)
    out_specs = [o.spec(tm, tn, gi, gj) for o in outs] + [row_spec] * n_sums
    out_shape = ([jax.ShapeDtypeStruct(o.shape, o.dtype) for o in outs]
                 + [jax.ShapeDtypeStruct((1, N), F32)] * n_sums)

    def body(*refs):
        a_ref, b_ref = refs[0], refs[1]
        ex = refs[2:2 + ne]
        o_refs = refs[2 + ne:2 + ne + no]
        s_refs = refs[2 + ne + no:2 + ne + no + n_sums]
        first_row_tile = pl.program_id(0) == 0

        def prod():
            return lax.dot_general(_bf(a_ref[...]), _bf(b_ref[...]), (((ca,), (cb,)), ((), ())),
                                   preferred_element_type=F32)

        def finish(acc):
            res = epilogue(acc, *[e[...] for e in ex])
            for o_ref, r in zip(o_refs, res[:no]):
                o_ref[...] = r.astype(o_ref.dtype)
            for s_ref, r in zip(s_refs, res[no:]):
                def assign(s_ref=s_ref, r=r):
                    s_ref[...] = r

                def accumulate(s_ref=s_ref, r=r):
                    s_ref[...] += r

                pl.when(first_row_tile)(assign)
                pl.when(jnp.logical_not(first_row_tile))(accumulate)

        if nk == 1:
            finish(prod())
        else:
            acc_ref = refs[-1]
            k = pl.program_id(2)

            @pl.when(k == 0)
            def _():
                acc_ref[...] = jnp.zeros_like(acc_ref)

            acc_ref[...] += prod()

            @pl.when(k == nk - 1)
            def _():
                finish(acc_ref[...])

    res, side = _hosted_call(body, name, (M // tm, N // tn, nk), in_specs, out_specs, out_shape,
                             [pltpu.VMEM((tm, tn), F32)] if nk > 1 else [], operands,
                             ("arbitrary", "arbitrary", "arbitrary"), plan)
    return res if plan is None else (res, side)


def _ep_store(acc):
    return (acc,)


def _ep_resid(acc, res):
    return (res + acc,)


def _ep_resid_norm(acc, res, g):
    xo = res + acc
    r = lax.rsqrt(jnp.mean(xo * xo, axis=-1, keepdims=True) + EPS)
    return (xo, (xo * r) * g)


def _ep_norm_bwd(acc, x, dres, g):
    r = lax.rsqrt(jnp.mean(x * x, axis=-1, keepdims=True) + EPS)
    xhat = x * r
    dxn = acc * g
    tot = dres + r * (dxn - xhat * jnp.mean(dxn * xhat, axis=-1, keepdims=True))
    return (tot, tot, jnp.sum(acc * xhat, axis=0, keepdims=True))


def _ep_relu2(acc):
    zp = jnp.maximum(acc, 0.0)
    return (acc, zp * zp)


def _ep_drelu2(acc, z):
    return (acc * (2.0 * jnp.maximum(z.astype(F32), 0.0)),)


def _fresh(M, N, dtype):
    return _View(None, shape=(M, N), dtype=dtype)


def _rms_fwd(name, x, g, S, D, plan=None):
    T = _pick(S, (512, 256, 128))

    def body(x_ref, g_ref, h_ref):
        x = x_ref[...]
        r = lax.rsqrt(jnp.mean(x * x, axis=-1, keepdims=True) + EPS)
        h_ref[...] = ((x * r) * g_ref[...]).astype(BF16)

    return _hosted_call(body, name, (S // T,),
                        [pl.BlockSpec((T, D), lambda i: (i, 0)), pl.BlockSpec((1, D), lambda i: (0, 0))],
                        [pl.BlockSpec((T, D), lambda i: (i, 0))], [jax.ShapeDtypeStruct((S, D), BF16)], [], (x, g),
                        ("arbitrary",), plan)


def _loss_head(x, tgt, S, D):
    T = _pick(S, (512, 256, 128))

    def body(x_ref, t_ref, loss_ref, d_ref, db_ref):
        @pl.when(pl.program_id(0) == 0)
        def _():
            loss_ref[...] = jnp.zeros_like(loss_ref)

        e = x_ref[...] - t_ref[...]
        loss_ref[...] += 0.5 * jnp.sum(jnp.mean(e * e, axis=-1, keepdims=True), axis=0, keepdims=True)
        d = e * (1.0 / D)
        d_ref[...] = d
        db_ref[...] = d.astype(BF16)

    row = pl.BlockSpec((T, D), lambda i: (i, 0))
    return pl.pallas_call(
        body, name="loss_head", grid=(S // T,), in_specs=[row, row],
        out_specs=[pl.BlockSpec((1, 1), lambda i: (0, 0)), row, row],
        out_shape=[jax.ShapeDtypeStruct((1, 1), F32), jax.ShapeDtypeStruct((S, D), F32),
                   jax.ShapeDtypeStruct((S, D), BF16)],
        compiler_params=_params(("arbitrary",)),
    )(x, tgt)


def _sigmoid(z):
    return 1.0 / (1.0 + jnp.exp(-z))


def _log_sigmoid(z):
    return jnp.minimum(z, 0.0) - jnp.log(1.0 + jnp.exp(-jnp.abs(z)))


_GELU_K = 0.7978845608028654
_GELU_C = 0.044715


def _gelu(x):
    t = jnp.tanh(_GELU_K * (x + _GELU_C * (x * x * x)))
    return 0.5 * x * (1.0 + t)


def _gelu_and_grad(x):
    x2 = x * x
    t = jnp.tanh(_GELU_K * (x + _GELU_C * (x2 * x)))
    g = 0.5 * x * (1.0 + t)
    dg = 0.5 * (1.0 + t) + 0.5 * x * (1.0 - t * t) * (_GELU_K * (1.0 + 3.0 * _GELU_C * x2))
    return g, dg


def _decay_terms(r, ls):
    la = LRU_C * r * ls
    a = jnp.exp(la)
    a2 = jnp.exp(2.0 * la)
    mult = jnp.sqrt(-jnp.tanh(la) * (a2 + 1.0))
    return a, a2, mult


def _lru_fwd(u0, conv_w, conv_b, wr_bd, b_r, wi_bd, b_i, lam, S, D, plan=None):
    T = _pick(S, (256, 128))
    GT = wr_bd.shape[-1]
    nG = D // GT

    def body(gb_ref, xb_ref, cw_ref, cb_ref, wr_ref, br_ref, wi_ref, bi_ref, lam_ref,
             y_ref, xc_ref, r_ref, i_ref, hs_ref, ext, a_scr, hcar):
        @pl.when(pl.program_id(0) == 0)
        def _():
            ext[0:SUBLANES, :] = jnp.zeros((SUBLANES, D), F32)
            hcar[...] = jnp.zeros_like(hcar)

        xb = xb_ref[...]
        ext[SUBLANES:SUBLANES + T, :] = xb
        xc = cb_ref[...]
        for k in range(CONV_WIDTH):
            xc = xc + ext[pl.ds(SUBLANES - (CONV_WIDTH - 1) + k, T), :] * cw_ref[k:k + 1, :]
        ext[0:SUBLANES, :] = xb[T - SUBLANES:T, :]
        xc_ref[...] = xc
        xcb = xc.astype(BF16)
        for g in range(nG):
            sl = slice(g * GT, (g + 1) * GT)
            zr = jnp.dot(xcb[:, sl], wr_ref[g], preferred_element_type=F32) + br_ref[:, sl]
            zi = jnp.dot(xcb[:, sl], wi_ref[g], preferred_element_type=F32) + bi_ref[:, sl]
            r_ref[:, sl] = _sigmoid(zr)
            i_ref[:, sl] = _sigmoid(zi)
        r = r_ref[...]
        a, _, mult = _decay_terms(r, _log_sigmoid(lam_ref[...]))
        a_scr[...] = a
        hs_ref[...] = mult * (i_ref[...] * xc)

        def step(t, h):
            h = a_scr[pl.ds(t, 1), :] * h + hs_ref[pl.ds(t, 1), :]
            hs_ref[pl.ds(t, 1), :] = h
            return h

        hcar[...] = lax.fori_loop(0, T, step, hcar[...], unroll=8)
        y_ref[...] = (_gelu(gb_ref[...]) * hs_ref[...]).astype(BF16)

    row = pl.BlockSpec((T, D), lambda i: (i, 0))
    vec = pl.BlockSpec((1, D), lambda i: (0, 0))
    bd = pl.BlockSpec((nG, GT, GT), lambda i: (0, 0, 0))
    f32o = jax.ShapeDtypeStruct((S, D), F32)
    return _hosted_call(
        body, "lru_fwd", (S // T,),
        [row, pl.BlockSpec((T, D), lambda i: (i, 1)), pl.BlockSpec((CONV_WIDTH, D), lambda i: (0, 0)), vec,
         bd, vec, bd, vec, vec],
        [row, row, row, row, row], [jax.ShapeDtypeStruct((S, D), BF16), f32o, f32o, f32o, f32o],
        [pltpu.VMEM((T + SUBLANES, D), F32), pltpu.VMEM((T, D), F32), pltpu.VMEM((1, D), F32)],
        (u0, u0, conv_w, conv_b, wr_bd, b_r, wi_bd, b_i, lam), ("arbitrary",), plan)


def _lru_bwd(dy, u0, xc, r, ig, hs, conv_w, wr_bd, wi_bd, lam, S, D, plan=None):
    T = _pick(S, (128,))
    nT = S // T
    GT = wr_bd.shape[-1]
    nG = D // GT
    W = CONV_WIDTH

    def body(dy_ref, gb_ref, xb_ref, xbp_ref, xc_ref, r_ref, i_ref, hs_ref, hsp_ref, cw_ref, wr_ref, wi_ref, lam_ref,
             du_ref, dcw_ref, dcb_ref, dlam_ref, dbr_ref, dbi_ref, dwr_ref, dwi_ref,
             a_scr, dh_scr, exth, extx, extd, dxc_scr, dz_scr, carry):
        step = pl.program_id(0)
        first_tile = step == nT - 1

        @pl.when(step == 0)
        def _():
            for ref in (dcw_ref, dcb_ref, dlam_ref, dbr_ref, dbi_ref, dwr_ref, dwi_ref, carry):
                ref[...] = jnp.zeros_like(ref)
            extd[T:T + SUBLANES, :] = jnp.zeros((SUBLANES, D), F32)

        hs = hs_ref[...]
        dy = dy_ref[...]
        g, dgelu = _gelu_and_grad(gb_ref[...])
        du_ref[:, 0:D] = (dy * hs * dgelu).astype(BF16)
        r = r_ref[...]
        lam = lam_ref[...]
        ls = _log_sigmoid(lam)
        a, a2, mult = _decay_terms(r, ls)
        a_scr[...] = a
        dh_scr[...] = dy * g

        def rstep(j, c):
            t = T - 1 - j
            d = dh_scr[pl.ds(t, 1), :] + c
            dh_scr[pl.ds(t, 1), :] = d
            return a_scr[pl.ds(t, 1), :] * d

        carry[...] = lax.fori_loop(0, T, rstep, carry[...], unroll=8)
        dh = dh_scr[...]
        keep = jnp.where(first_tile, 0.0, 1.0)
        exth[0:SUBLANES, :] = hsp_ref[...] * keep
        exth[SUBLANES:SUBLANES + T, :] = hs
        hprev = exth[pl.ds(SUBLANES - 1, T), :]
        xc = xc_ref[...]
        ig = i_ref[...]
        da = dh * hprev
        dmult = dh * (ig * xc)
        dla = da * a - dmult * (a2 / mult)
        dlam_ref[...] += jnp.sum(dla * r, axis=0, keepdims=True) * (LRU_C * _sigmoid(-lam))
        dzr = (dla * (LRU_C * ls)) * (r * (1.0 - r))
        dzi = (dh * (mult * xc)) * (ig * (1.0 - ig))
        dbr_ref[...] += jnp.sum(dzr, axis=0, keepdims=True)
        dbi_ref[...] += jnp.sum(dzi, axis=0, keepdims=True)
        dxc_scr[...] = dh * (mult * ig)
        xcb = xc.astype(BF16)
        dz_scr[0] = dzr.astype(BF16)
        dz_scr[1] = dzi.astype(BF16)
        nt_dims = (((1,), (1,)), ((), ()))
        tn_dims = (((0,), (0,)), ((), ()))
        for gq in range(nG):
            sl = slice(gq * GT, (gq + 1) * GT)
            zr_g = dz_scr[0, :, sl]
            zi_g = dz_scr[1, :, sl]
            dxc_scr[:, sl] += (lax.dot_general(zr_g, wr_ref[gq], nt_dims, preferred_element_type=F32)
                               + lax.dot_general(zi_g, wi_ref[gq], nt_dims, preferred_element_type=F32))
            dwr_ref[gq] += lax.dot_general(xcb[:, sl], zr_g, tn_dims, preferred_element_type=F32)
            dwi_ref[gq] += lax.dot_general(xcb[:, sl], zi_g, tn_dims, preferred_element_type=F32)
        dxc = dxc_scr[...]
        dcb_ref[...] += jnp.sum(dxc, axis=0, keepdims=True)
        extx[0:SUBLANES, :] = xbp_ref[...] * keep
        extx[SUBLANES:SUBLANES + T, :] = xb_ref[...]
        extd[0:T, :] = dxc
        dxb = jnp.zeros((T, D), F32)
        for k in range(W):
            dxb = dxb + extd[pl.ds(W - 1 - k, T), :] * cw_ref[k:k + 1, :]
            dcw_ref[k:k + 1, :] += jnp.sum(dxc * extx[pl.ds(SUBLANES - (W - 1) + k, T), :], axis=0, keepdims=True)
        extd[T:T + SUBLANES, :] = dxc[0:SUBLANES, :]
        du_ref[:, D:2 * D] = dxb.astype(BF16)

    rev = lambda i: nT - 1 - i
    tpb = T // SUBLANES
    prev8 = lambda i: jnp.maximum(rev(i) * tpb - 1, 0)
    row = pl.BlockSpec((T, D), lambda i: (rev(i), 0))
    vec = pl.BlockSpec((1, D), lambda i: (0, 0))
    bd = pl.BlockSpec((nG, GT, GT), lambda i: (0, 0, 0))
    vec_o = jax.ShapeDtypeStruct((1, D), F32)
    bd_o = jax.ShapeDtypeStruct((nG, GT, GT), F32)
    return _hosted_call(
        body, "lru_bwd", (nT,),
        [row, row, pl.BlockSpec((T, D), lambda i: (rev(i), 1)), pl.BlockSpec((SUBLANES, D), lambda i: (prev8(i), 1)),
         row, row, row, row, pl.BlockSpec((SUBLANES, D), lambda i: (prev8(i), 0)),
         pl.BlockSpec((W, D), lambda i: (0, 0)), bd, bd, vec],
        [pl.BlockSpec((T, 2 * D), lambda i: (rev(i), 0)), pl.BlockSpec((W, D), lambda i: (0, 0)),
         vec, vec, vec, vec, bd, bd],
        [jax.ShapeDtypeStruct((S, 2 * D), BF16), jax.ShapeDtypeStruct((W, D), F32), vec_o, vec_o, vec_o, vec_o, bd_o, bd_o],
        [pltpu.VMEM((T, D), F32), pltpu.VMEM((T, D), F32), pltpu.VMEM((T + SUBLANES, D), F32),
         pltpu.VMEM((T + SUBLANES, D), F32), pltpu.VMEM((T + SUBLANES, D), F32),
         pltpu.VMEM((T, D), F32), pltpu.VMEM((2, T, D), BF16), pltpu.VMEM((1, D), F32)],
        (dy, u0, u0, u0, xc, r, ig, hs, hs, conv_w, wr_bd, wi_bd, lam), ("arbitrary",), plan)


AUG_ROWS = 16
HEAD_ROWS = 128
LSE_ROW = HEAD_DIM + 6
ONES_ROW_Q = HEAD_DIM + 3
ONES_COL_K = HEAD_DIM
ONES_ROW_V = HEAD_DIM


def _split3(x):
    b1 = x.astype(BF16).astype(F32)
    r = x - b1
    b2 = r.astype(BF16).astype(F32)
    return b1, b2, r - b2


def _head_block(x, aug, T):
    row = lax.broadcasted_iota(jnp.int32, (AUG_ROWS, T), 0)
    blk = jnp.zeros((AUG_ROWS, T), F32)
    for i, e in enumerate(aug):
        blk = jnp.where(row == i, e, blk)
    return jnp.concatenate([x, blk, jnp.zeros((HEAD_ROWS - HEAD_DIM - AUG_ROWS, T), F32)], axis=0)


def _tri_matrix(lower):
    i = np.arange(LANES)
    m = (i[:, None] >= i[None, :]) if lower else (i[:, None] <= i[None, :])
    return jnp.asarray(m.astype(np.float32), BF16)


def _lane_cumsum(x, tri_ref, carry, reverse):
    n = x.shape[1] // LANES
    tri = tri_ref[...]
    out = [None] * n
    for j in (range(n - 1, -1, -1) if reverse else range(n)):
        cs = carry
        for part in _split3(x[:, j * LANES:(j + 1) * LANES]):
            cs = cs + jnp.dot(part.astype(BF16), tri, preferred_element_type=F32)
        out[j] = cs
        carry = cs[:, 0:1] if reverse else cs[:, LANES - 1:LANES]
    return jnp.concatenate(out, axis=1), carry


def _head_rows(h):
    return pl.ds(pl.multiple_of(h * HEAD_DIM, HEAD_DIM), HEAD_DIM)


def _fox_prep(ut, b_f, qg, kg, S, D, tq):
    H = D // HEAD_DIM
    T = min(tq, 256)
    per = tq // T
    scale = HEAD_DIM ** -0.5

    def body(q_ref, k_ref, v_ref, f_ref, bf_ref, qg_ref, kg_ref, tri_ref,
             qat_ref, kat_ref, vat_ref, ka_ref, c_scr, ccar):
        @pl.when(pl.program_id(0) == 0)
        def _():
            ccar[...] = jnp.zeros_like(ccar)

        c, carry = _lane_cumsum(_log_sigmoid(f_ref[...] + bf_ref[...]), tri_ref, ccar[...], False)
        c_scr[...] = c
        ccar[...] = carry

        def head(h, _):
            rows = _head_rows(h)
            c1, c2, c3 = _split3(c_scr[pl.ds(h, 1), :])

            def normed(src, gain, mul):
                x = src[rows, :]
                rs = lax.rsqrt(jnp.mean(x * x, axis=0, keepdims=True) + EPS)
                return ((x * rs) * gain[rows, :]) * mul

            qat_ref[h] = _head_block(normed(q_ref, qg_ref, scale), [c1, c2, c3, 1.0, 1.0, 1.0], T).astype(BF16)
            kb = _head_block(normed(k_ref, kg_ref, 1.0), [1.0, 1.0, 1.0, -c1, -c2, -c3, 1.0, 1.0, 1.0], T)
            kat_ref[h] = kb.astype(BF16)
            ka_ref[h] = kb.T.astype(BF16)
            vat_ref[h] = _head_block(v_ref[rows, :], [1.0, 1.0, 1.0], T).astype(BF16)
            return 0

        lax.fori_loop(0, H, head, 0)

    part = lambda j: pl.BlockSpec((D, T), lambda i: (j, i))
    colv = lambda n: pl.BlockSpec((n, 1), lambda i: (0, 0))
    tmaj = lambda r: pl.BlockSpec((H, None, r, T), lambda i: (0, i // per, 0, i % per))
    norm = pl.BlockSpec((H, T, HEAD_ROWS), lambda i: (0, i, 0))
    tshape = lambda r: jax.ShapeDtypeStruct((H, S // tq, r, tq), BF16)
    nshape = jax.ShapeDtypeStruct((H, S, HEAD_ROWS), BF16)
    return pl.pallas_call(
        body, name="fox_prep", grid=(S // T,),
        in_specs=[part(0), part(1), part(2), pl.BlockSpec((LANES, T), lambda i: (3 * D // LANES, i)),
                  colv(LANES), colv(D), colv(D), pl.BlockSpec((LANES, LANES), lambda i: (0, 0))],
        out_specs=[tmaj(HEAD_ROWS), tmaj(HEAD_ROWS), tmaj(HEAD_ROWS), norm],
        out_shape=[tshape(HEAD_ROWS), tshape(HEAD_ROWS), tshape(HEAD_ROWS), nshape],
        scratch_shapes=[pltpu.VMEM((LANES, T), F32), pltpu.VMEM((LANES, 1), F32)],
        compiler_params=_params(("arbitrary",)),
    )(ut, ut, ut, ut, b_f, qg, kg, _tri_matrix(False))


def _fox_bwd_prep(dot, ot, lse, qat, S, D, tq, plan=None):
    H = D // HEAD_DIM
    T = min(tq, 256)
    per = tq // T

    def body(do_ref, o_ref, lse_ref, qat_ref, doat_ref, doa_ref, qat1_ref, qa1_ref):
        row = lax.broadcasted_iota(jnp.int32, (HEAD_ROWS, T), 0)

        def head(h, _):
            rows = _head_rows(h)
            do = do_ref[rows, :].astype(F32)
            delta = jnp.sum(do * o_ref[rows, :], axis=0, keepdims=True)
            db = _head_block(do, list(_split3(-delta)), T)
            doat_ref[h] = db.astype(BF16)
            doa_ref[h] = db.T.astype(BF16)
            qb = qat_ref[h].astype(F32)
            for i, e in enumerate(_split3(-lse_ref[h])):
                qb = jnp.where(row == LSE_ROW + i, e, qb)
            qat1_ref[h] = qb.astype(BF16)
            qa1_ref[h] = qb.T.astype(BF16)
            return 0

        lax.fori_loop(0, H, head, 0)

    chan = pl.BlockSpec((D, T), lambda i: (0, i))
    tmaj = pl.BlockSpec((H, None, HEAD_ROWS, T), lambda i: (0, i // per, 0, i % per))
    norm = pl.BlockSpec((H, T, HEAD_ROWS), lambda i: (0, i, 0))
    tshape = jax.ShapeDtypeStruct((H, S // tq, HEAD_ROWS, tq), BF16)
    nshape = jax.ShapeDtypeStruct((H, S, HEAD_ROWS), BF16)
    return _hosted_call(body, "fox_bwd_prep", (S // T,), [chan, chan, pl.BlockSpec((H, 1, T), lambda i: (0, 0, i)), tmaj],
                        [tmaj, norm, tmaj, norm], [tshape, nshape, tshape, nshape], [], (dot, ot, lse, qat),
                        ("arbitrary",), plan)


def _causal(s, k_axis):
    ki = lax.broadcasted_iota(jnp.int32, s.shape, k_axis)
    qi = lax.broadcasted_iota(jnp.int32, s.shape, 1 - k_axis)
    return jnp.where(ki <= qi, s, NEG_INF)


def _seq_tile(i, t):
    return pl.ds(pl.multiple_of(i * t, t), t)


def _attn_forward(ka, qat, vat, S, D, tq, plan=None):
    H = D // HEAD_DIM
    nq = S // tq

    def body(ka_ref, qat_ref, vat_ref, o_ref, o32_ref, lse_ref, m_scr, acc_scr):
        qi = pl.program_id(1)
        m_scr[...] = jnp.full_like(m_scr, NEG_INF)
        acc_scr[...] = jnp.zeros_like(acc_scr)
        qa = qat_ref[...]

        def span(k0, n, diagonal):
            s = jnp.dot(ka_ref[pl.ds(pl.multiple_of(k0 * tq, tq), n * tq), :], qa, preferred_element_type=F32)
            if diagonal:
                s = _causal(s, 0)
            m_prev = m_scr[...]
            m_new = jnp.maximum(m_prev, jnp.max(s, axis=0, keepdims=True))
            p = jnp.exp(s - m_new).astype(BF16)
            upd = jnp.dot(vat_ref[k0], p[0:tq], preferred_element_type=F32)
            for i in range(1, n):
                upd = upd + jnp.dot(vat_ref[k0 + i], p[i * tq:(i + 1) * tq], preferred_element_type=F32)
            acc_scr[...] = jnp.exp(m_prev - m_new) * acc_scr[...] + upd
            m_scr[...] = m_new

        def off_diagonal_pair(j, _):
            span(2 * j, 2, False)
            return 0

        lax.fori_loop(0, qi // 2, off_diagonal_pair, 0)
        pl.when(qi % 2 == 1)(lambda: span(qi - 1, 1, False))
        span(qi, 1, True)
        l = acc_scr[ONES_ROW_V:ONES_ROW_V + 1, :]
        o = acc_scr[0:HEAD_DIM, :] / l
        o_ref[...] = o.astype(BF16)
        o32_ref[...] = o
        lse_ref[...] = m_scr[...] + jnp.log(l)

    chan = pl.BlockSpec((HEAD_DIM, tq), lambda h, i: (h, i))
    stat = pl.BlockSpec((None, 1, tq), lambda h, i: (h, 0, i))
    return _hosted_call(
        body, "attn_forward", (H, nq),
        [pl.BlockSpec((None, S, HEAD_ROWS), lambda h, i: (h, 0, 0)),
         pl.BlockSpec((None, None, HEAD_ROWS, tq), lambda h, i: (h, i, 0, 0)),
         pl.BlockSpec((None, nq, HEAD_ROWS, tq), lambda h, i: (h, 0, 0, 0))],
        [chan, chan, stat],
        [jax.ShapeDtypeStruct((D, S), BF16), jax.ShapeDtypeStruct((D, S), F32), jax.ShapeDtypeStruct((H, 1, S), F32)],
        [pltpu.VMEM((1, tq), F32), pltpu.VMEM((HEAD_ROWS, tq), F32)],
        (ka, qat, vat), ("arbitrary", "arbitrary"), plan)


def _attn_backward(qa, doa, qat, doat, ka, kat, vat, S, D, tq, plan=None):
    H = D // HEAD_DIM
    nq = S // tq

    def body(qa_ref, doa_ref, qat_ref, doat_ref, ka_ref, kat_ref, vat_ref, dq_ref, dk_ref, dv_ref, dk_scr, dv_scr):
        ki = pl.program_id(1)

        @pl.when(ki == 0)
        def _():
            dq_ref[...] = jnp.zeros_like(dq_ref)

        dk_scr[...] = jnp.zeros_like(dk_scr)
        dv_scr[...] = jnp.zeros_like(dv_scr)
        kt = kat_ref[...]
        vt = vat_ref[...]
        kn = ka_ref[...]

        def span(q0, n, diagonal):
            rows = pl.ds(pl.multiple_of(q0 * tq, tq), n * tq)
            s = jnp.dot(qa_ref[rows, :], kt, preferred_element_type=F32)
            if diagonal:
                s = _causal(s, 1)
            p = jnp.exp(s)
            ds = (p * jnp.dot(doa_ref[rows, :], vt, preferred_element_type=F32)).astype(BF16)
            p = p.astype(BF16)
            for i in range(n):
                part = slice(i * tq, (i + 1) * tq)
                dv_scr[...] += jnp.dot(doat_ref[q0 + i, 0:HEAD_DIM, :], p[part], preferred_element_type=F32)
                dk_scr[...] += jnp.dot(qat_ref[q0 + i], ds[part], preferred_element_type=F32)
            dq_ref[rows, :] += jnp.dot(ds, kn, preferred_element_type=F32)

        def off_diagonal_pair(j, _):
            span(ki + 1 + 2 * j, 2, False)
            return 0

        span(ki, 1, True)
        n_off = nq - 1 - ki
        lax.fori_loop(0, n_off // 2, off_diagonal_pair, 0)
        pl.when(n_off % 2 == 1)(lambda: span(nq - 1, 1, False))
        dk_ref[...] = dk_scr[...]
        dv_ref[...] = dv_scr[...].astype(BF16)

    whole = pl.BlockSpec((None, S, HEAD_ROWS), lambda h, i: (h, 0, 0))
    tiles = pl.BlockSpec((None, nq, HEAD_ROWS, tq), lambda h, i: (h, 0, 0, 0))
    one = pl.BlockSpec((None, None, HEAD_ROWS, tq), lambda h, i: (h, i, 0, 0))
    return _hosted_call(
        body, "attn_backward", (H, nq),
        [whole, whole, tiles, tiles, pl.BlockSpec((None, tq, HEAD_ROWS), lambda h, i: (h, i, 0)), one, one],
        [whole, pl.BlockSpec((None, HEAD_ROWS, tq), lambda h, i: (h, 0, i)),
         pl.BlockSpec((HEAD_DIM, tq), lambda h, i: (h, i))],
        [jax.ShapeDtypeStruct((H, S, HEAD_ROWS), F32), jax.ShapeDtypeStruct((H, HEAD_ROWS, S), F32),
         jax.ShapeDtypeStruct((D, S), BF16)],
        [pltpu.VMEM((HEAD_ROWS, tq), F32), pltpu.VMEM((HEAD_DIM, tq), F32)],
        (qa, doa, qat, doat, ka, kat, vat), ("arbitrary", "arbitrary"), plan)


def _fox_prep_bwd(ut, dq, dkt, dvt, b_f, qg, kg, S, D, tq):
    H = D // HEAD_DIM
    T = min(tq, 256)
    nT = S // T
    NU = 3 * D + LANES
    scale = HEAD_DIM ** -0.5

    def body(q_ref, k_ref, f_ref, dq_ref, dk_ref, dv_ref, bf_ref, qg_ref, kg_ref, tri_ref,
             du_ref, dbf_ref, dqg_ref, dkg_ref, gq_acc, gk_acc, fcar, dc_scr):
        step = pl.program_id(0)

        @pl.when(step == 0)
        def _():
            for ref in (gq_acc, gk_acc, fcar, dbf_ref):
                ref[...] = jnp.zeros_like(ref)

        dc_scr[...] = jnp.zeros_like(dc_scr)

        def head(h, _):
            rows = _head_rows(h)
            dqb = dq_ref[h].T
            dkb = dk_ref[h]
            dc_scr[pl.ds(h, 1), :] = dqb[ONES_COL_K:ONES_COL_K + 1, :] - dkb[ONES_ROW_Q:ONES_ROW_Q + 1, :]
            for src, dsrc, gain, acc, mul, base in ((q_ref, dqb, qg_ref, gq_acc, scale, 0),
                                                    (k_ref, dkb, kg_ref, gk_acc, 1.0, D)):
                x = src[rows, :]
                rs = lax.rsqrt(jnp.mean(x * x, axis=0, keepdims=True) + EPS)
                xhat = x * rs
                dn = dsrc[0:HEAD_DIM, :] * mul
                acc[rows, :] += jnp.sum(dn * xhat, axis=1, keepdims=True)
                dxh = dn * gain[rows, :]
                dx = rs * (dxh - xhat * jnp.mean(dxh * xhat, axis=0, keepdims=True))
                du_ref[pl.ds(pl.multiple_of(base + h * HEAD_DIM, HEAD_DIM), HEAD_DIM), :] = dx.astype(BF16)
            return 0

        lax.fori_loop(0, H, head, 0)
        du_ref[2 * D:3 * D, :] = dv_ref[...]
        dlf, carry = _lane_cumsum(dc_scr[...], tri_ref, fcar[...], True)
        fcar[...] = carry
        dfl = dlf * _sigmoid(-(f_ref[...] + bf_ref[...]))
        dbf_ref[...] += jnp.sum(dfl, axis=1, keepdims=True)
        du_ref[3 * D:NU, :] = dfl.astype(BF16)

        @pl.when(step == nT - 1)
        def _():
            for acc, ref in ((gq_acc, dqg_ref), (gk_acc, dkg_ref)):
                tot = jnp.zeros((HEAD_DIM, 1), F32)
                for h in range(H):
                    tot = tot + acc[h * HEAD_DIM:(h + 1) * HEAD_DIM, :]
                ref[...] = tot

    rev = lambda i: nT - 1 - i
    part = lambda j: pl.BlockSpec((D, T), lambda i: (j, rev(i)))
    colv = lambda n: pl.BlockSpec((n, 1), lambda i: (0, 0))
    return pl.pallas_call(
        body, name="fox_prep_bwd", grid=(nT,),
        in_specs=[part(0), part(1), pl.BlockSpec((LANES, T), lambda i: (3 * D // LANES, rev(i))),
                  pl.BlockSpec((H, T, HEAD_ROWS), lambda i: (0, rev(i), 0)),
                  pl.BlockSpec((H, HEAD_ROWS, T), lambda i: (0, 0, rev(i))), pl.BlockSpec((D, T), lambda i: (0, rev(i))),
                  colv(LANES), colv(D), colv(D), pl.BlockSpec((LANES, LANES), lambda i: (0, 0))],
        out_specs=[pl.BlockSpec((NU, T), lambda i: (0, rev(i))), colv(LANES), colv(HEAD_DIM), colv(HEAD_DIM)],
        out_shape=[jax.ShapeDtypeStruct((NU, S), BF16), jax.ShapeDtypeStruct((LANES, 1), F32),
                   jax.ShapeDtypeStruct((HEAD_DIM, 1), F32), jax.ShapeDtypeStruct((HEAD_DIM, 1), F32)],
        scratch_shapes=[pltpu.VMEM((D, 1), F32), pltpu.VMEM((D, 1), F32), pltpu.VMEM((LANES, 1), F32),
                        pltpu.VMEM((LANES, T), F32)],
        compiler_params=_params(("arbitrary",)),
    )(ut, ut, ut, dq, dkt, dvt, b_f, qg, kg, _tri_matrix(True))


def _block_diag_tiles(w):
    n = w.shape[0]
    per = min(MXU_DIM, n * LRU_BLOCK_DIM) // LRU_BLOCK_DIM
    eye = jnp.eye(per, dtype=w.dtype)
    w5 = w.reshape(n // per, per, LRU_BLOCK_DIM, 1, LRU_BLOCK_DIM) * eye[None, :, None, :, None]
    return w5.reshape(n // per, per * LRU_BLOCK_DIM, per * LRU_BLOCK_DIM).astype(BF16)


def _block_diag_extract(t, n):
    per = t.shape[-1] // LRU_BLOCK_DIM
    eye = jnp.eye(per, dtype=t.dtype)
    t5 = t.reshape(n // per, per, LRU_BLOCK_DIM, per, LRU_BLOCK_DIM) * eye[None, :, None, :, None]
    return t5.sum(axis=3).reshape(n, LRU_BLOCK_DIM, LRU_BLOCK_DIM)


def _local_step(x, tgt, small, wv, grad_view, comm=None):
    S, D = x.shape
    F = 4 * D
    H = D // HEAD_DIM
    nblk = D // LRU_BLOCK_DIM
    NU = 3 * D + LANES
    tq = max(LANES, min(512, S // 4))
    assert S % tq == 0
    vec = lambda a: a.reshape(1, -1).astype(F32)
    col = lambda a: a.reshape(-1, 1).astype(F32)
    mix_g, mlp_g = small["mix_norm"], small["mlp_norm"]
    conv_b = vec(small["lru_conv_b"])
    wr_bd, wi_bd = _block_diag_tiles(small["lru_w_r"][0]), _block_diag_tiles(small["lru_w_i"][0])
    b_r, b_i, lam = vec(small["lru_b_r"]), vec(small["lru_b_i"]), vec(small["lru_lambda"])
    b_f = jnp.pad(col(small["fox_b_f"]), ((0, LANES - H), (0, 0)))
    qg, kg = jnp.tile(col(small["fox_q_gain"]), (H, 1)), jnp.tile(col(small["fox_k_gain"]), (H, 1))
    X = lambda a: _View(a)
    grads = {}
    gout = functools.partial(grad_view, grads)

    def hosted(name, fn, *args):
        plan = comm.before(name, grads) if comm is not None else None
        res, side = fn(*args, plan=plan)
        if plan is not None:
            comm.after(name, side, wv)
        return res

    def hosted_mm(name, *args, **kw):
        plan = comm.before(name, grads) if comm is not None else None
        if plan is None:
            return _matmul(name, *args, **kw)
        res, side = _matmul(name, *args, plan=plan, **kw)
        comm.after(name, side, wv)
        return res

    norm_rows = _pick(S, (512, 256, 128))
    two = lambda: [_fresh(S, D, F32), _fresh(S, D, BF16)]

    def mlp_up(l, hm):
        return hosted_mm(f"mlp{l}_up", X(hm), wv[f"w1_{l}"], S, F, D, outs=[_fresh(S, F, BF16), _fresh(S, F, BF16)],
                         epilogue=_ep_relu2)

    def mlp_bwd(l, xin, hm, z, act, d, db):
        (dz,) = hosted_mm(f"mlp{l}_dact", X(db), wv[f"w2_{l}"], S, F, D, tb=True, outs=[_fresh(S, F, BF16)],
                          epilogue=_ep_drelu2, extras=[X(z)])
        (grads[f"w2_{l}"],) = _matmul(f"mlp{l}_dw2", X(act), X(db), F, D, S, ta=True, outs=[gout(f"w2_{l}")],
                                      epilogue=_ep_store)
        (grads[f"w1_{l}"],) = _matmul(f"mlp{l}_dw1", X(hm), X(dz), D, F, S, ta=True, outs=[gout(f"w1_{l}")],
                                      epilogue=_ep_store)
        return _matmul(f"mlp{l}_dhm", X(dz), wv[f"w1_{l}"], S, D, F, tb=True, outs=two(), n_sums=1,
                       epilogue=_ep_norm_bwd, extras=[X(xin), X(d)], vecs=[mlp_g[l:l + 1]], tm=norm_rows)

    (h0,) = hosted("mix0_norm", _rms_fwd, "mix0_norm", x, mix_g[0:1], S, D)
    conv_w = small["conv_w"]
    (u0,) = _matmul("lru_in", X(h0), wv["lru_in"], S, 2 * D, D, outs=[_fresh(S, 2 * D, F32)], epilogue=_ep_store)
    y, xc, r, ig, hs = hosted("lru_fwd", _lru_fwd, u0, conv_w, conv_b, wr_bd, b_r, wi_bd, b_i, lam, S, D)
    x1, hm0 = _matmul("lru_out", X(y), wv["lru_out"], S, D, D, outs=two(), epilogue=_ep_resid_norm, extras=[X(x)],
                      vecs=[mlp_g[0:1]], tm=norm_rows)
    z0, act0 = mlp_up(0, hm0)
    x2, h1 = hosted_mm("mlp0_down", X(act0), wv["w2_0"], S, D, F, outs=two(), epilogue=_ep_resid_norm, extras=[X(x1)],
                       vecs=[mix_g[1:2]], tm=norm_rows)
    (u1,) = _matmul("fox_in", wv["fox_in"], X(h1), NU, S, D, tb=True, outs=[_fresh(NU, S, F32)], epilogue=_ep_store)
    qat, kat, vat, ka = _fox_prep(u1, b_f, qg, kg, S, D, tq)
    o, o32, lse = hosted("attn_forward", _attn_forward, ka, qat, vat, S, D, tq)
    x3, hm1 = _matmul("fox_out", X(o), wv["fox_out"], S, D, D, ta=True, outs=two(), epilogue=_ep_resid_norm,
                      extras=[X(x2)], vecs=[mlp_g[1:2]], tm=norm_rows)
    z1, act1 = mlp_up(1, hm1)
    (x4,) = _matmul("mlp1_down", X(act1), wv["w2_1"], S, D, F, outs=[_fresh(S, D, F32)], epilogue=_ep_resid,
                    extras=[X(x3)])
    loss, d4, d4b = _loss_head(x4, tgt, S, D)

    d3, d3b, dg_mlp1 = mlp_bwd(1, x3, hm1, z1, act1, d4, d4b)
    (do,) = _matmul("fox_dout", wv["fox_out"], X(d3b), D, S, D, tb=True, outs=[_fresh(D, S, BF16)], epilogue=_ep_store)
    (grads["fox_out"],) = _matmul("fox_dwout", X(o), X(d3b), D, D, S, outs=[gout("fox_out")], epilogue=_ep_store)
    doat, doa, qat1, qa1 = hosted("fox_bwd_prep", _fox_bwd_prep, do, o32, lse, qat, S, D, tq)
    dqn, dkn, dv = hosted("attn_backward", _attn_backward, qa1, doa, qat1, doat, ka, kat, vat, S, D, tq)
    du1, dbf, dqg, dkg = _fox_prep_bwd(u1, dqn, dkn, dv, b_f, qg, kg, S, D, tq)
    (grads["fox_in"],) = _matmul("fox_dwin", X(h1), X(du1), D, NU, S, ta=True, tb=True, outs=[gout("fox_in")],
                                 epilogue=_ep_store)
    d2, d2b, dg_mix1 = hosted_mm("fox_dh", X(du1), wv["fox_in"], S, D, NU, ta=True, outs=two(), n_sums=1,
                               epilogue=_ep_norm_bwd, extras=[X(x2), X(d3)], vecs=[mix_g[1:2]], tm=norm_rows)
    d1, d1b, dg_mlp0 = mlp_bwd(0, x1, hm0, z0, act0, d2, d2b)
    (grads["lru_out"],) = _matmul("lru_dwout", X(y), X(d1b), D, D, S, ta=True, outs=[gout("lru_out")],
                                  epilogue=_ep_store)
    (dy,) = hosted_mm("lru_dout", X(d1b), wv["lru_out"], S, D, D, tb=True, outs=[_fresh(S, D, F32)],
                      epilogue=_ep_store)
    du0, dcw, dcb, dlam, dbr, dbi, dwr, dwi = hosted("lru_bwd", _lru_bwd, dy, u0, xc, r, ig, hs, conv_w, wr_bd, wi_bd,
                                                     lam, S, D)
    (grads["lru_in"],) = _matmul("lru_dwin", X(h0), X(du0), D, 2 * D, S, ta=True, outs=[gout("lru_in")],
                                 epilogue=_ep_store)
    gx, dg_mix0 = hosted_mm("lru_dh", X(du0), wv["lru_in"], S, D, 2 * D, tb=True, outs=[_fresh(S, D, F32)], n_sums=1,
                            epilogue=lambda *a: _ep_norm_bwd(*a)[::2], extras=[X(x), X(d1)], vecs=[mix_g[0:1]],
                            tm=norm_rows)

    grads.update(
        mix_norm=jnp.concatenate([dg_mix0, dg_mix1], axis=0), mlp_norm=jnp.concatenate([dg_mlp0, dg_mlp1], axis=0),
        conv_w=dcw, lru_conv_b=dcb, lru_w_r=_block_diag_extract(dwr, nblk)[None], lru_b_r=dbr.reshape(1, nblk, -1),
        lru_w_i=_block_diag_extract(dwi, nblk)[None], lru_b_i=dbi.reshape(1, nblk, -1), lru_lambda=dlam,
        fox_b_f=dbf[:H].reshape(1, H), fox_q_gain=dqg.reshape(1, -1), fox_k_gain=dkg.reshape(1, -1))
    return loss, gx, grads


def _place():
    x, y, c = lax.axis_index("x"), lax.axis_index("y"), lax.axis_index("c")
    chips = [(1 - x, y), (x, 1 - y), (1 - x, 1 - y)]
    return x, y, c, 2 * x + y, chips


BOUNCE_BYTES = 1 << 20


def _bounce_shape(rows, cols, dtype):
    chunk = rows
    while chunk % 2 == 0 and chunk > 16 and chunk * cols * jnp.dtype(dtype).itemsize > BOUNCE_BYTES:
        chunk //= 2
    return pltpu.VMEM((2, chunk, cols), dtype)


def _bounce_copy(src, dst, buf, sem):
    chunk = buf.shape[1]
    n = src.shape[0] // chunk
    cin = lambda i: pltpu.make_async_copy(src.at[pl.ds(i * chunk, chunk)], buf.at[i % 2], sem.at[i % 2])
    cout = lambda i: pltpu.make_async_copy(buf.at[i % 2], dst.at[pl.ds(i * chunk, chunk)], sem.at[2 + i % 2])
    cin(0).start()
    for i in range(n):
        cin(i).wait()
        if i + 1 < n:
            if i >= 1:
                cout(i - 1).wait()
            cin(i + 1).start()
        cout(i).start()
    if n >= 2:
        cout(n - 2).wait()
    cout(n - 1).wait()


def _hbm_call(body, name, arrays, out_shape, n_dma_sems, bounce=()):
    scratch = [pltpu.SemaphoreType.DMA((k,)) for k in n_dma_sems]
    for rows, cols, dtype in bounce:
        scratch += [_bounce_shape(rows, cols, dtype), pltpu.SemaphoreType.DMA((4,))]
    return pl.pallas_call(
        body, name=name, in_specs=[ANY] * len(arrays), out_specs=[ANY] * len(out_shape), out_shape=out_shape,
        scratch_shapes=scratch,
        compiler_params=pltpu.CompilerParams(has_side_effects=True, vmem_limit_bytes=VMEM_LIMIT),
    )(*arrays)


class _Gather:
    def __init__(self, shards):
        n = self.n = len(shards)
        self.operands = list(shards)
        self.out_shape = [jax.ShapeDtypeStruct((N_CHIPS,) + tuple(a.shape), a.dtype) for a in shards]
        self.scratch = [pltpu.SemaphoreType.DMA((3 * n,)) for _ in range(4)]
        for a in shards:
            self.scratch += [_bounce_shape(a.shape[0], a.shape[1], a.dtype), pltpu.SemaphoreType.DMA((4,))]

    def _copies(self, ins, outs, scr):
        send, recv, fsend, frecv = scr[:4]
        x, y, c, s, chips = _place()

        def rows(a, chip_idx, which):
            hr = ins[a].shape[0] // 2
            return outs[a].at[chip_idx, pl.ds(which * hr, hr)]

        def landed(a, j, core):
            return rows(a, 2 * chips[j][0] + chips[j][1], core)

        def ici(a, j, mine):
            hr = ins[a].shape[0] // 2
            src, dst = (ins[a].at[pl.ds(c * hr, hr)], rows(a, s, c)) if mine else (landed(a, j, c),) * 2
            return pltpu.make_async_remote_copy(src_ref=src, dst_ref=dst, send_sem=send.at[3 * a + j],
                                                recv_sem=recv.at[3 * a + j], device_id=(*chips[j], c),
                                                device_id_type=MESH)

        def d2d(a, j, mine):
            ref = landed(a, j, c if mine else 1 - c)
            return pltpu.make_async_remote_copy(src_ref=ref, dst_ref=ref, send_sem=fsend.at[3 * a + j],
                                                recv_sem=frecv.at[3 * a + j], device_id=(x, y, 1 - c),
                                                device_id_type=MESH)

        return ici, d2d, s

    def start(self, ins, outs, scr):
        ici, _, _ = self._copies(ins, outs, scr)
        for a in range(self.n):
            for j in range(3):
                ici(a, j, True).start()

    def middle(self, ins, outs, scr):
        ici, d2d, s = self._copies(ins, outs, scr)
        for a in range(self.n):
            _bounce_copy(ins[a], outs[a].at[s], scr[4 + 2 * a], scr[5 + 2 * a])
        for a in range(self.n):
            for j in range(3):
                ici(a, j, False).wait_recv()
                d2d(a, j, True).start()

    def finish(self, ins, outs, scr):
        ici, d2d, _ = self._copies(ins, outs, scr)
        for a in range(self.n):
            for j in range(3):
                d2d(a, j, False).wait_recv()
        for a in range(self.n):
            for j in range(3):
                ici(a, j, True).wait_send()
                d2d(a, j, True).wait_send()


def _run_plan(name, plan):
    k_in, k_out = len(plan.operands), len(plan.out_shape)

    def body(*refs):
        parts = (refs[:k_in], refs[k_in:k_in + k_out], refs[k_in + k_out:])
        plan.start(*parts)
        plan.middle(*parts)
        plan.finish(*parts)

    return pl.pallas_call(
        body, name=name, in_specs=[ANY] * k_in, out_specs=[ANY] * k_out, out_shape=plan.out_shape,
        scratch_shapes=plan.scratch,
        compiler_params=pltpu.CompilerParams(has_side_effects=True, vmem_limit_bytes=VMEM_LIMIT),
    )(*plan.operands)


def _hosted_call(body, name, grid, in_specs, out_specs, out_shape, scratch_shapes, operands, sem, plan=None):
    if plan is None:
        res = pl.pallas_call(body, name=name, grid=grid, in_specs=in_specs, out_specs=out_specs, out_shape=out_shape,
                             scratch_shapes=scratch_shapes, compiler_params=_params(sem))(*operands)
        return res, None
    n_in, n_out, n_scr = len(in_specs), len(out_specs), len(scratch_shapes)
    k_in, k_out = len(plan.operands), len(plan.out_shape)
    total = int(np.prod(grid))

    def hosted(*refs):
        ins, refs = refs[:n_in], refs[n_in:]
        p_ins, refs = refs[:k_in], refs[k_in:]
        outs, refs = refs[:n_out], refs[n_out:]
        p_outs, refs = refs[:k_out], refs[k_out:]
        scr, p_scr = refs[:n_scr], refs[n_scr:]
        step = pl.program_id(0)
        for d in range(1, len(grid)):
            step = step * grid[d] + pl.program_id(d)
        pl.when(step == 0)(lambda: plan.start(p_ins, p_outs, p_scr))
        body(*ins, *outs, *scr)
        pl.when(step == total // 2)(lambda: plan.middle(p_ins, p_outs, p_scr))
        pl.when(step == total - 1)(lambda: plan.finish(p_ins, p_outs, p_scr))

    res = pl.pallas_call(
        hosted, name=name, grid=grid, in_specs=list(in_specs) + [ANY] * k_in, out_specs=list(out_specs) + [ANY] * k_out,
        out_shape=list(out_shape) + plan.out_shape, scratch_shapes=list(scratch_shapes) + plan.scratch,
        compiler_params=pltpu.CompilerParams(dimension_semantics=sem, vmem_limit_bytes=VMEM_LIMIT,
                                             has_side_effects=True),
    )(*operands, *plan.operands)
    return res[:n_out], res[n_out:]


def _all_gather(name, shards):
    return _run_plan(name, _Gather(shards))


class _Swap:
    def __init__(self, arrs):
        self.n = len(arrs)
        self.operands = list(arrs)
        self.out_shape = [jax.ShapeDtypeStruct((a.shape[0], a.shape[1] // 2, a.shape[2]), a.dtype) for a in arrs]
        self.scratch = [pltpu.SemaphoreType.DMA((self.n,)) for _ in range(2)]

    def _copy(self, ins, outs, scr, a):
        x, y, c, _, _ = _place()
        hr = ins[a].shape[1] // 2
        return pltpu.make_async_remote_copy(
            src_ref=ins[a].at[:, pl.ds((1 - c) * hr, hr)], dst_ref=outs[a], send_sem=scr[0].at[a],
            recv_sem=scr[1].at[a], device_id=(x, y, 1 - c), device_id_type=MESH)

    def start(self, ins, outs, scr):
        for a in range(self.n):
            self._copy(ins, outs, scr, a).start()

    def middle(self, ins, outs, scr):
        pass

    def finish(self, ins, outs, scr):
        for a in range(self.n):
            self._copy(ins, outs, scr, a).wait()


class _Scatter:
    def __init__(self, parts):
        n = self.n = len(parts)
        self.operands = list(parts)
        self.out_shape = [jax.ShapeDtypeStruct(a.shape, a.dtype) for a in parts]
        self.scratch = [pltpu.SemaphoreType.DMA((3 * n,)) for _ in range(2)]
        for a in parts:
            self.scratch += [_bounce_shape(a.shape[1], a.shape[2], a.dtype), pltpu.SemaphoreType.DMA((4,))]

    def _copy(self, ins, outs, scr, a, j, mine):
        x, y, c, s, chips = _place()
        t = 2 * chips[j][0] + chips[j][1]
        return pltpu.make_async_remote_copy(
            src_ref=ins[a].at[t], dst_ref=outs[a].at[s if mine else t], send_sem=scr[0].at[3 * a + j],
            recv_sem=scr[1].at[3 * a + j], device_id=(*chips[j], c), device_id_type=MESH)

    def start(self, ins, outs, scr):
        for a in range(self.n):
            for j in range(3):
                self._copy(ins, outs, scr, a, j, True).start()

    def middle(self, ins, outs, scr):
        s = _place()[3]
        for a in range(self.n):
            _bounce_copy(ins[a].at[s], outs[a].at[s], scr[2 + 2 * a], scr[3 + 2 * a])

    def finish(self, ins, outs, scr):
        for a in range(self.n):
            for j in range(3):
                self._copy(ins, outs, scr, a, j, False).wait_recv()
        for a in range(self.n):
            for j in range(3):
                self._copy(ins, outs, scr, a, j, True).wait_send()


def _pair_gather(name, halves):
    n = len(halves)

    def body(*refs):
        ins, outs = refs[:n], refs[n:2 * n]
        send, recv = refs[2 * n:2 * n + 2]
        stage = refs[2 * n + 2:]
        x, y, c, _, _ = _place()
        cps = []
        for a in range(n):
            hr = ins[a].shape[0]
            cp = pltpu.make_async_remote_copy(
                src_ref=ins[a], dst_ref=outs[a].at[pl.ds(c * hr, hr)], send_sem=send.at[a], recv_sem=recv.at[a],
                device_id=(x, y, 1 - c), device_id_type=MESH)
            cp.start()
            cps.append((cp, hr))
        for a, (cp, hr) in enumerate(cps):
            _bounce_copy(ins[a], outs[a].at[pl.ds(c * hr, hr)], stage[2 * a], stage[2 * a + 1])
        for a, (cp, hr) in enumerate(cps):
            cp.wait_send()
            theirs = outs[a].at[pl.ds((1 - c) * hr, hr)]
            pltpu.make_async_remote_copy(src_ref=theirs, dst_ref=theirs, send_sem=send.at[a], recv_sem=recv.at[a],
                                         device_id=(x, y, 1 - c), device_id_type=MESH).wait_recv()

    out_shape = [jax.ShapeDtypeStruct((2 * a.shape[0], a.shape[1]), a.dtype) for a in halves]
    return _hbm_call(body, name, halves, out_shape, (n, n),
                     bounce=[(a.shape[0], a.shape[1], a.dtype) for a in halves])


def _row_tile(rows, cols, itemsize, n_bufs):
    budget = VMEM_LIMIT // 2
    for t in (1024, 512, 256, 128, 64, 32, 16):
        if rows % t == 0 and 2 * n_bufs * t * cols * itemsize <= budget:
            return t
    return rows


def _pair_add(name, g, gsib, core, out_dtype):
    _, r, cols = g.shape
    hr = r // 2
    t = _row_tile(hr, cols, 4, 3)
    per = hr // t

    def body(core_ref, a_ref, b_ref, o_ref):
        o_ref[...] = (a_ref[...].astype(F32) + b_ref[...].astype(F32)).astype(o_ref.dtype)

    grid_spec = pltpu.PrefetchScalarGridSpec(
        num_scalar_prefetch=1, grid=(N_CHIPS, per),
        in_specs=[pl.BlockSpec((None, t, cols), lambda s, i, core: (s, core[0] * per + i, 0)),
                  pl.BlockSpec((None, t, cols), lambda s, i, core: (s, i, 0))],
        out_specs=pl.BlockSpec((None, t, cols), lambda s, i, core: (s, i, 0)))
    return pl.pallas_call(body, name=name, grid_spec=grid_spec,
                          out_shape=jax.ShapeDtypeStruct((N_CHIPS, hr, cols), out_dtype),
                          compiler_params=_params(("arbitrary", "arbitrary")))(core, g, gsib)


def _chip_sum(name, parts):
    _, hr, cols = parts.shape
    t = _row_tile(hr, cols, 4, 5)

    def body(p_ref, o_ref):
        o_ref[...] = ((p_ref[0].astype(F32) + p_ref[1].astype(F32)) + p_ref[2].astype(F32)) + p_ref[3].astype(F32)

    return pl.pallas_call(
        body, name=name, grid=(hr // t,), in_specs=[pl.BlockSpec((N_CHIPS, t, cols), lambda i: (0, i, 0))],
        out_specs=pl.BlockSpec((t, cols), lambda i: (i, 0)), out_shape=jax.ShapeDtypeStruct((hr, cols), F32),
        compiler_params=_params(("arbitrary",)))(parts)


def _pair_partials(tag, arrs, sib, wire_dtypes, core):
    return _Scatter([_pair_add(f"{tag}_pair_add{i}", g, gs, core, dt)
                     for i, (g, gs, dt) in enumerate(zip(arrs, sib, wire_dtypes))])


def _finish_reduce(tag, scattered):
    halves = [_chip_sum(f"{tag}_chip_sum{i}", p) for i, p in enumerate(scattered)]
    return _pair_gather(f"{tag}_pair_gather", halves)


def _adamw(name, w, g_parts, m, v):
    thin = w.ndim == 3
    rows, cols = w.shape[0], w.shape[-1]
    n_parts = len(g_parts)
    part_rows = rows // n_parts
    t = max(d for d in range(1, 257) if part_rows % d == 0) if thin else _row_tile(part_rows, cols, 4, 7 + n_parts)
    per = part_rows // t
    c1 = 1.0 - ADAM_B1 ** ADAM_STEP
    c2 = 1.0 - ADAM_B2 ** ADAM_STEP

    def body(w_ref, m_ref, v_ref, *refs):
        g_refs, (go_ref, d_ref, nm_ref, nv_ref) = refs[:n_parts], refs[n_parts:]
        g = g_refs[0][...]
        for k in range(1, n_parts):
            g = jnp.where(pl.program_id(0) >= k * per, g_refs[k][...], g)
        go_ref[...] = g
        m = ADAM_B1 * m_ref[...] + (1.0 - ADAM_B1) * g
        v = ADAM_B2 * v_ref[...] + (1.0 - ADAM_B2) * (g * g)
        nm_ref[...] = m
        nv_ref[...] = v
        d_ref[...] = -ADAM_LR * ((m / c1) / (jnp.sqrt(v / c2) + ADAM_EPS) + ADAM_WD * w_ref[...])

    block = (t, 1, cols) if thin else (t, cols)
    at = lambda r: (r, 0, 0) if thin else (r, 0)
    spec = pl.BlockSpec(block, lambda i: at(i))
    g_specs = [pl.BlockSpec(block, lambda i, k=k: at(jnp.clip(i - k * per, 0, per - 1))) for k in range(n_parts)]
    shp = jax.ShapeDtypeStruct(w.shape, F32)
    return pl.pallas_call(body, name=name, grid=(rows // t,), in_specs=[spec] * 3 + g_specs, out_specs=[spec] * 4,
                          out_shape=[shp] * 4, compiler_params=_params(("arbitrary",)))(w, m, v, *g_parts)


_WEIGHTS = ["mix_norm", "mlp_norm", "mlp_w1", "mlp_w2", "lru_w_in", "lru_conv_w", "lru_conv_b", "lru_w_r", "lru_b_r",
            "lru_w_i", "lru_b_i", "lru_lambda", "lru_w_out", "fox_w_in", "fox_b_f", "fox_q_gain", "fox_k_gain",
            "fox_w_out"]
_REPLICATED = ["mix_norm", "mlp_norm", "lru_conv_b", "lru_w_r", "lru_b_r", "lru_w_i", "lru_b_i", "lru_lambda",
               "fox_b_f", "fox_q_gain", "fox_k_gain"]
_PACK_TILE = 2 * SUBLANES * LANES


def _as2d(a):
    return a.reshape(-1, a.shape[-1])


def kernel(x, mix_norm, mlp_norm, mlp_w1, mlp_w2, lru_w_in, lru_conv_w, lru_conv_b, lru_w_r, lru_b_r, lru_w_i, lru_b_i, lru_lambda, lru_w_out, fox_w_in, fox_b_f, fox_q_gain, fox_k_gain, fox_w_out, loss_target, m_mix_norm, m_mlp_norm, m_mlp_w1, m_mlp_w2, m_lru_w_in, m_lru_conv_w, m_lru_conv_b, m_lru_w_r, m_lru_b_r, m_lru_w_i, m_lru_b_i, m_lru_lambda, m_lru_w_out, m_fox_w_in, m_fox_b_f, m_fox_q_gain, m_fox_k_gain, m_fox_w_out, v_mix_norm, v_mlp_norm, v_mlp_w1, v_mlp_w2, v_lru_w_in, v_lru_conv_w, v_lru_conv_b, v_lru_w_r, v_lru_b_r, v_lru_w_i, v_lru_b_i, v_lru_lambda, v_lru_w_out, v_fox_w_in, v_fox_b_f, v_fox_q_gain, v_fox_k_gain, v_fox_w_out):
    args = dict(locals())
    W = {n: args[n] for n in _WEIGHTS}
    Mo = {n: args["m_" + n] for n in _WEIGHTS}
    Vo = {n: args["v_" + n] for n in _WEIGHTS}
    S, D = x.shape[1], x.shape[2]
    F = 4 * D
    H = D // HEAD_DIM
    NU = 3 * D + LANES
    FQ, DQ = F // N_CHIPS, D // N_CHIPS
    nfox = fox_w_in.shape[-1]
    chip = 2 * lax.axis_index("x") + lax.axis_index("y")
    core = lax.axis_index("c").astype(jnp.int32).reshape(1)

    cw_flat = jnp.pad(lru_conv_w.reshape(-1), (0, _PACK_TILE - CONV_WIDTH * DQ)).reshape(2 * SUBLANES, LANES)
    w1s, w2s = mlp_w1.astype(BF16), mlp_w2.astype(BF16)
    wv = {}
    small = {n: W[n] for n in _REPLICATED}
    scattered = {}
    members = {"g1": ["w2_1", "w1_1", "fox_out"], "g2": ["fox_in"], "g3": ["w2_0", "w1_0", "lru_out"], "g4": ["lru_in"]}
    swap_at = {"fox_bwd_prep": "g1", "fox_dh": "g2", "lru_dout": "g3"}
    scatter_at = {"attn_backward": "g1", "mlp0_dact": "g2", "lru_bwd": "g3", "lru_dh": "g4"}
    swapped = {}

    def shard_major(name, g):
        if name == "fox_in":
            return jnp.transpose(g[:, :nfox * N_CHIPS].reshape(D, N_CHIPS, nfox), (1, 0, 2))
        return g

    class Comm:
        @staticmethod
        def before(name, grads):
            if name == "mix0_norm":
                return _Gather([lru_w_in[0].astype(BF16), lru_w_out[0].astype(BF16), cw_flat])
            if name == "lru_fwd":
                return _Gather([w1s[0]])
            if name == "mlp0_up":
                return _Gather([w2s[0]])
            if name == "mlp0_down":
                return _Gather([fox_w_in[0].astype(BF16)])
            if name == "attn_forward":
                return _Gather([fox_w_out[0].astype(BF16), w1s[1], w2s[1]])
            if name in swap_at:
                group = swap_at[name]
                swapped[group] = [[shard_major(n, grads[n]) for n in members[group]], None]
                return _Swap(swapped[group][0])
            if name in scatter_at:
                group = scatter_at[name]
                if group not in swapped:
                    arrs = [shard_major(n, grads[n]) for n in members[group]]
                    swapped[group] = [arrs, _run_plan(f"{group}_pair_swap", _Swap(arrs))]
                arrs, sib = swapped[group]
                return _pair_partials(group, arrs, sib, [BF16] * len(arrs), core)
            return None

        @staticmethod
        def after(name, res, wv):
            if name == "mix0_norm":
                wv.update(lru_in=_View(res[0], "cs"), lru_out=_View(res[1], "rs"))
                taps = res[2].reshape(N_CHIPS, -1)[:, :CONV_WIDTH * DQ].reshape(N_CHIPS, CONV_WIDTH, DQ)
                small["conv_w"] = jnp.transpose(taps, (1, 0, 2)).reshape(CONV_WIDTH, D)
            elif name == "lru_fwd":
                wv.update(w1_0=_View(res[0], "cs"))
            elif name == "mlp0_up":
                wv.update(w2_0=_View(res[0], "rs"))
            elif name == "mlp0_down":
                fox_full = jnp.concatenate([res[0][s] for s in range(N_CHIPS)], axis=1)
                fox_full = jnp.pad(fox_full, ((0, 0), (0, NU - fox_full.shape[1])))
                wv.update(fox_in=_View(fox_full.T))
            elif name == "attn_forward":
                wv.update(fox_out=_View(res[0], "rs"), w1_1=_View(res[1], "cs"), w2_1=_View(res[2], "rs"))
            elif name in swap_at:
                swapped[swap_at[name]][1] = res
            else:
                scattered.update(zip(members[scatter_at[name]], res))

    def grad_view(grads, name):
        if name in ("w1_0", "w1_1"):
            return _View(None, "cs", shape=(N_CHIPS, D, FQ), dtype=BF16)
        if name in ("w2_0", "w2_1"):
            return _View(None, "rs", shape=(N_CHIPS, FQ, D), dtype=BF16)
        if name == "lru_in":
            return _View(None, "cs", shape=(N_CHIPS, D, 2 * D // N_CHIPS), dtype=BF16)
        if name in ("lru_out", "fox_out"):
            return _View(None, "rs", shape=(N_CHIPS, DQ, D), dtype=BF16)
        return _View(None, shape=(D, NU), dtype=BF16)

    loss, gx, grads = _local_step(x[0], loss_target[0], small, wv, grad_view, Comm)

    pack_names = _REPLICATED + ["conv_w"]
    flat = jnp.concatenate([grads[n].reshape(-1).astype(F32) for n in pack_names] + [loss.reshape(-1)])
    per_chip = -(-flat.shape[0] // (N_CHIPS * _PACK_TILE)) * _PACK_TILE
    pack = jnp.pad(flat, (0, N_CHIPS * per_chip - flat.shape[0])).reshape(N_CHIPS, per_chip // LANES, LANES)
    pack_sib = _run_plan("pack_pair_swap", _Swap([pack]))
    (scattered["pack"],) = _run_plan("pack_chip_scatter", _pair_partials("pack", [pack], pack_sib, [F32], core))
    order = ["w1_0", "w1_1", "w2_0", "w2_1", "lru_in", "lru_out", "fox_in", "fox_out", "pack"]
    red = dict(zip(order, _finish_reduce("grads", [scattered[n] for n in order])))
    (all_pack,) = _all_gather("gather_small_grads", [red["pack"]])
    all_flat = all_pack.reshape(-1)
    G = {}
    off = 0
    for n in pack_names:
        shape = grads[n].shape if n == "conv_w" else W[n].shape
        size = int(np.prod(shape))
        G[n] = all_flat[off:off + size].reshape(shape)
        off += size
    total = all_flat[off]
    G["lru_conv_w"] = lax.dynamic_slice_in_dim(G.pop("conv_w"), chip * DQ, DQ, axis=1)[None]
    parts = {n: [_as2d(G[n])] for n in G}
    parts.update(mlp_w1=[red["w1_0"], red["w1_1"]], mlp_w2=[red["w2_0"], red["w2_1"]], lru_w_in=[red["lru_in"]],
                 lru_w_out=[red["lru_out"]], fox_w_in=[red["fox_in"]], fox_w_out=[red["fox_out"]])

    delta, new_m, new_v = {}, {}, {}
    for n in _WEIGHTS:
        if W[n].shape[-1] % LANES and W[n].shape[-2] % LANES == 0:
            to_thin = lambda a: jnp.transpose(a, (2, 0, 1))
            res = _adamw(f"adamw_{n}", to_thin(W[n]), [to_thin(parts[n][0][None])], to_thin(Mo[n]), to_thin(Vo[n]))
            G[n], delta[n], new_m[n], new_v[n] = (jnp.transpose(t, (1, 2, 0)) for t in res)
            continue
        go, d, nm, nv = _adamw(f"adamw_{n}", _as2d(W[n]), parts[n], _as2d(Mo[n]), _as2d(Vo[n]))
        G[n], delta[n], new_m[n], new_v[n] = (t.reshape(W[n].shape) for t in (go, d, nm, nv))

    return (total, gx[None], *[G[n] for n in _WEIGHTS], *[delta[n] for n in _WEIGHTS],
            *[new_m[n] for n in _WEIGHTS], *[new_v[n] for n in _WEIGHTS])
```

```python
import functools

import numpy as np
import jax
import jax.numpy as jnp
from jax import lax
from jax.experimental import pallas as pl
from jax.experimental.pallas import tpu as pltpu

F32 = jnp.float32
BF16 = jnp.bfloat16

HEAD_DIM = 64
LRU_BLOCK_DIM = 64
CONV_WIDTH = 4
LRU_C = 8.0
EPS = 1e-6
NEG_INF = -1e30
ADAM_LR = 0.001
ADAM_B1 = 0.9
ADAM_B2 = 0.999
ADAM_EPS = 1e-08
ADAM_WD = 0.01
ADAM_STEP = 10

N_CHIPS = 4
LANES = 128
SUBLANES = 8
MXU_DIM = 256
VMEM_LIMIT = 52 * 1024 * 1024
MESH = pl.DeviceIdType.MESH
ANY = pl.BlockSpec(memory_space=pl.ANY)


def _pick(n, prefs):
    for p in prefs:
        if p <= n and n % p == 0:
            return p
    return n


def _params(sem=None):
    return pltpu.CompilerParams(dimension_semantics=sem, vmem_limit_bytes=VMEM_LIMIT)


class _View:
    def __init__(self, arr, kind="plain", r0=0, rows=None, shape=None, dtype=None):
        self.arr = arr
        self.kind = kind
        self.r0 = r0
        self.shape = tuple(arr.shape) if arr is not None else tuple(shape)
        self.dtype = arr.dtype if arr is not None else dtype
        self.rows = rows if rows is not None else self.shape[-2]

    def limits(self):
        if self.kind == "plain":
            return 0, 0
        rows = int(np.gcd(self.rows, self.r0))
        return rows, (self.shape[-1] if self.kind == "cs" else 0)

    def spec(self, br, bc, fr, fc):
        if self.kind == "plain":
            return pl.BlockSpec((br, bc), lambda *g: (fr(*g), fc(*g)))
        ncol = self.shape[-1]
        r0b = self.r0 // br
        assert self.r0 % br == 0 and self.rows % br == 0 and ncol % bc == 0, (self.shape, self.r0, br, bc)
        if self.kind == "cs":
            per = ncol // bc
            return pl.BlockSpec((None, br, bc), lambda *g: (fc(*g) // per, r0b + fr(*g), fc(*g) % per))
        per = self.rows // br
        return pl.BlockSpec((None, br, bc), lambda *g: (fr(*g) // per, r0b + fr(*g) % per, fc(*g)))


def _bf(x):
    return x if x.dtype == BF16 else x.astype(BF16)


def _matmul(name, A, B, M, N, K, *, ta=False, tb=False, outs, epilogue, extras=(), vecs=(), n_sums=0,
            tm=None, tn=None, tk=None, plan=None):
    lim = {"m": [M], "n": [N], "k": [K]}
    for view, (rdim, cdim) in ([(A, "km" if ta else "mk"), (B, "nk" if tb else "kn")]
                               + [(e, "mn") for e in extras] + [(o, "mn") for o in outs]):
        r_lim, c_lim = view.limits()
        lim[rdim].append(r_lim)
        lim[cdim].append(c_lim)
    tm = tm or _pick(int(np.gcd.reduce(lim["m"])), (1024, 640, 512, 256, 128))
    tn = tn or _pick(int(np.gcd.reduce(lim["n"])), (1024, 640, 512, 256, 128))
    tk = tk or _pick(int(np.gcd.reduce(lim["k"])), (1024, 640, 512, 256, 128))
    nk = K // tk
    gi = lambda i, j, k: i
    gj = lambda i, j, k: j
    gk = lambda i, j, k: k
    a_spec = A.spec(tk, tm, gk, gi) if ta else A.spec(tm, tk, gi, gk)
    b_spec = B.spec(tn, tk, gj, gk) if tb else B.spec(tk, tn, gk, gj)
    ca = 0 if ta else 1
    cb = 1 if tb else 0
    ne, no = len(extras) + len(vecs), len(outs)
    assert n_sums == 0 or tn == N
    row_spec = pl.BlockSpec((1, tn), lambda i, j, k: (0, j))
    in_specs = [a_spec, b_spec] + [e.spec(tm, tn, gi, gj) for e in extras] + [row_spec] * len(vecs)
    operands = [A.arr, B.arr] + [e.arr for e in extras] + list(vecs)
    out_specs = [o.spec(tm, tn, gi, gj) for o in outs] + [row_spec] * n_sums
    out_shape = ([jax.ShapeDtypeStruct(o.shape, o.dtype) for o in outs]
                 + [jax.ShapeDtypeStruct((1, N), F32)] * n_sums)

    def body(*refs):
        a_ref, b_ref = refs[0], refs[1]
        ex = refs[2:2 + ne]
        o_refs = refs[2 + ne:2 + ne + no]
        s_refs = refs[2 + ne + no:2 + ne + no + n_sums]
        first_row_tile = pl.program_id(0) == 0

        def prod():
            return lax.dot_general(_bf(a_ref[...]), _bf(b_ref[...]), (((ca,), (cb,)), ((), ())),
                                   preferred_element_type=F32)

        def finish(acc):
            res = epilogue(acc, *[e[...] for e in ex])
            for o_ref, r in zip(o_refs, res[:no]):
                o_ref[...] = r.astype(o_ref.dtype)
            for s_ref, r in zip(s_refs, res[no:]):
                def assign(s_ref=s_ref, r=r):
                    s_ref[...] = r

                def accumulate(s_ref=s_ref, r=r):
                    s_ref[...] += r

                pl.when(first_row_tile)(assign)
                pl.when(jnp.logical_not(first_row_tile))(accumulate)

        if nk == 1:
            finish(prod())
        else:
            acc_ref = refs[-1]
            k = pl.program_id(2)

            @pl.when(k == 0)
            def _():
                acc_ref[...] = jnp.zeros_like(acc_ref)

            acc_ref[...] += prod()

            @pl.when(k == nk - 1)
            def _():
                finish(acc_ref[...])

    res, side = _hosted_call(body, name, (M // tm, N // tn, nk), in_specs, out_specs, out_shape,
                             [pltpu.VMEM((tm, tn), F32)] if nk > 1 else [], operands,
                             ("arbitrary", "arbitrary", "arbitrary"), plan)
    return res if plan is None else (res, side)


def _ep_store(acc):
    return (acc,)


def _ep_resid(acc, res):
    return (res + acc,)


def _ep_resid_norm(acc, res, g):
    xo = res + acc
    r = lax.rsqrt(jnp.mean(xo * xo, axis=-1, keepdims=True) + EPS)
    return (xo, (xo * r) * g)


def _ep_norm_bwd(acc, x, dres, g):
    r = lax.rsqrt(jnp.mean(x * x, axis=-1, keepdims=True) + EPS)
    xhat = x * r
    dxn = acc * g
    tot = dres + r * (dxn - xhat * jnp.mean(dxn * xhat, axis=-1, keepdims=True))
    return (tot, tot, jnp.sum(acc * xhat, axis=0, keepdims=True))


def _ep_relu2(acc):
    zp = jnp.maximum(acc, 0.0)
    return (acc, zp * zp)


def _ep_drelu2(acc, z):
    return (acc * (2.0 * jnp.maximum(z.astype(F32), 0.0)),)


def _fresh(M, N, dtype):
    return _View(None, shape=(M, N), dtype=dtype)


def _rms_fwd(name, x, g, S, D, plan=None):
    T = _pick(S, (512, 256, 128))

    def body(x_ref, g_ref, h_ref):
        x = x_ref[...]
        r = lax.rsqrt(jnp.mean(x * x, axis=-1, keepdims=True) + EPS)
        h_ref[...] = ((x * r) * g_ref[...]).astype(BF16)

    return _hosted_call(body, name, (S // T,),
                        [pl.BlockSpec((T, D), lambda i: (i, 0)), pl.BlockSpec((1, D), lambda i: (0, 0))],
                        [pl.BlockSpec((T, D), lambda i: (i, 0))], [jax.ShapeDtypeStruct((S, D), BF16)], [], (x, g),
                        ("arbitrary",), plan)


def _loss_head(x, tgt, S, D):
    T = _pick(S, (512, 256, 128))

    def body(x_ref, t_ref, loss_ref, d_ref, db_ref):
        @pl.when(pl.program_id(0) == 0)
        def _():
            loss_ref[...] = jnp.zeros_like(loss_ref)

        e = x_ref[...] - t_ref[...]
        loss_ref[...] += 0.5 * jnp.sum(jnp.mean(e * e, axis=-1, keepdims=True), axis=0, keepdims=True)
        d = e * (1.0 / D)
        d_ref[...] = d
        db_ref[...] = d.astype(BF16)

    row = pl.BlockSpec((T, D), lambda i: (i, 0))
    return pl.pallas_call(
        body, name="loss_head", grid=(S // T,), in_specs=[row, row],
        out_specs=[pl.BlockSpec((1, 1), lambda i: (0, 0)), row, row],
        out_shape=[jax.ShapeDtypeStruct((1, 1), F32), jax.ShapeDtypeStruct((S, D), F32),
                   jax.ShapeDtypeStruct((S, D), BF16)],
        compiler_params=_params(("arbitrary",)),
    )(x, tgt)


def _sigmoid(z):
    return 1.0 / (1.0 + jnp.exp(-z))


def _log_sigmoid(z):
    return jnp.minimum(z, 0.0) - jnp.log(1.0 + jnp.exp(-jnp.abs(z)))


_GELU_K = 0.7978845608028654
_GELU_C = 0.044715


def _gelu(x):
    t = jnp.tanh(_GELU_K * (x + _GELU_C * (x * x * x)))
    return 0.5 * x * (1.0 + t)


def _gelu_and_grad(x):
    x2 = x * x
    t = jnp.tanh(_GELU_K * (x + _GELU_C * (x2 * x)))
    g = 0.5 * x * (1.0 + t)
    dg = 0.5 * (1.0 + t) + 0.5 * x * (1.0 - t * t) * (_GELU_K * (1.0 + 3.0 * _GELU_C * x2))
    return g, dg


def _decay_terms(r, ls):
    la = LRU_C * r * ls
    a = jnp.exp(la)
    a2 = a * a
    mult = jnp.sqrt(-jnp.tanh(la) * (a2 + 1.0))
    return a, a2, mult


def _lru_fwd(u0, conv_w, conv_b, wr_bd, b_r, wi_bd, b_i, lam, S, D, plan=None):
    T = _pick(S, (256, 128))
    GT = wr_bd.shape[-1]
    nG = D // GT

    def body(gb_ref, xb_ref, cw_ref, cb_ref, wr_ref, br_ref, wi_ref, bi_ref, lam_ref,
             y_ref, xc_ref, r_ref, i_ref, hs_ref, ext, a_scr, hcar):
        @pl.when(pl.program_id(0) == 0)
        def _():
            ext[0:SUBLANES, :] = jnp.zeros((SUBLANES, D), F32)
            hcar[...] = jnp.zeros_like(hcar)

        xb = xb_ref[...]
        ext[SUBLANES:SUBLANES + T, :] = xb
        xc = cb_ref[...]
        for k in range(CONV_WIDTH):
            xc = xc + ext[pl.ds(SUBLANES - (CONV_WIDTH - 1) + k, T), :] * cw_ref[k:k + 1, :]
        ext[0:SUBLANES, :] = xb[T - SUBLANES:T, :]
        xc_ref[...] = xc
        xcb = xc.astype(BF16)
        for g in range(nG):
            sl = slice(g * GT, (g + 1) * GT)
            zr = jnp.dot(xcb[:, sl], wr_ref[g], preferred_element_type=F32) + br_ref[:, sl]
            zi = jnp.dot(xcb[:, sl], wi_ref[g], preferred_element_type=F32) + bi_ref[:, sl]
            r_ref[:, sl] = _sigmoid(zr)
            i_ref[:, sl] = _sigmoid(zi)
        r = r_ref[...]
        a, _, mult = _decay_terms(r, _log_sigmoid(lam_ref[...]))
        a_scr[...] = a
        hs_ref[...] = mult * (i_ref[...] * xc)

        def step(t, h):
            h = a_scr[pl.ds(t, 1), :] * h + hs_ref[pl.ds(t, 1), :]
            hs_ref[pl.ds(t, 1), :] = h
            return h

        hcar[...] = lax.fori_loop(0, T, step, hcar[...], unroll=8)
        y_ref[...] = (_gelu(gb_ref[...]) * hs_ref[...]).astype(BF16)

    row = pl.BlockSpec((T, D), lambda i: (i, 0))
    vec = pl.BlockSpec((1, D), lambda i: (0, 0))
    bd = pl.BlockSpec((nG, GT, GT), lambda i: (0, 0, 0))
    f32o = jax.ShapeDtypeStruct((S, D), F32)
    return _hosted_call(
        body, "lru_fwd", (S // T,),
        [row, pl.BlockSpec((T, D), lambda i: (i, 1)), pl.BlockSpec((CONV_WIDTH, D), lambda i: (0, 0)), vec,
         bd, vec, bd, vec, vec],
        [row, row, row, row, row], [jax.ShapeDtypeStruct((S, D), BF16), f32o, f32o, f32o, f32o],
        [pltpu.VMEM((T + SUBLANES, D), F32), pltpu.VMEM((T, D), F32), pltpu.VMEM((1, D), F32)],
        (u0, u0, conv_w, conv_b, wr_bd, b_r, wi_bd, b_i, lam), ("arbitrary",), plan)


def _lru_bwd(dy, u0, xc, r, ig, hs, conv_w, wr_bd, wi_bd, lam, S, D, plan=None):
    T = _pick(S, (128,))
    nT = S // T
    GT = wr_bd.shape[-1]
    nG = D // GT
    W = CONV_WIDTH

    def body(dy_ref, gb_ref, xb_ref, xbp_ref, xc_ref, r_ref, i_ref, hs_ref, hsp_ref, cw_ref, wr_ref, wi_ref, lam_ref,
             du_ref, dcw_ref, dcb_ref, dlam_ref, dbr_ref, dbi_ref, dwr_ref, dwi_ref,
             a_scr, dh_scr, exth, extx, extd, dxc_scr, dz_scr, carry):
        step = pl.program_id(0)
        first_tile = step == nT - 1

        @pl.when(step == 0)
        def _():
            for ref in (dcw_ref, dcb_ref, dlam_ref, dbr_ref, dbi_ref, dwr_ref, dwi_ref, carry):
                ref[...] = jnp.zeros_like(ref)
            extd[T:T + SUBLANES, :] = jnp.zeros((SUBLANES, D), F32)

        hs = hs_ref[...]
        dy = dy_ref[...]
        g, dgelu = _gelu_and_grad(gb_ref[...])
        du_ref[:, 0:D] = (dy * hs * dgelu).astype(BF16)
        r = r_ref[...]
        lam = lam_ref[...]
        ls = _log_sigmoid(lam)
        a, a2, mult = _decay_terms(r, ls)
        a_scr[...] = a
        dh_scr[...] = dy * g

        def rstep(j, c):
            t = T - 1 - j
            d = dh_scr[pl.ds(t, 1), :] + c
            dh_scr[pl.ds(t, 1), :] = d
            return a_scr[pl.ds(t, 1), :] * d

        carry[...] = lax.fori_loop(0, T, rstep, carry[...], unroll=8)
        dh = dh_scr[...]
        keep = jnp.where(first_tile, 0.0, 1.0)
        exth[0:SUBLANES, :] = hsp_ref[...] * keep
        exth[SUBLANES:SUBLANES + T, :] = hs
        hprev = exth[pl.ds(SUBLANES - 1, T), :]
        xc = xc_ref[...]
        ig = i_ref[...]
        da = dh * hprev
        dmult = dh * (ig * xc)
        dla = da * a - dmult * (a2 / mult)
        dlam_ref[...] += jnp.sum(dla * r, axis=0, keepdims=True) * (LRU_C * _sigmoid(-lam))
        dzr = (dla * (LRU_C * ls)) * (r * (1.0 - r))
        dzi = (dh * (mult * xc)) * (ig * (1.0 - ig))
        dbr_ref[...] += jnp.sum(dzr, axis=0, keepdims=True)
        dbi_ref[...] += jnp.sum(dzi, axis=0, keepdims=True)
        dxc_scr[...] = dh * (mult * ig)
        xcb = xc.astype(BF16)
        dz_scr[0] = dzr.astype(BF16)
        dz_scr[1] = dzi.astype(BF16)
        nt_dims = (((1,), (1,)), ((), ()))
        tn_dims = (((0,), (0,)), ((), ()))
        for gq in range(nG):
            sl = slice(gq * GT, (gq + 1) * GT)
            zr_g = dz_scr[0, :, sl]
            zi_g = dz_scr[1, :, sl]
            dxc_scr[:, sl] += (lax.dot_general(zr_g, wr_ref[gq], nt_dims, preferred_element_type=F32)
                               + lax.dot_general(zi_g, wi_ref[gq], nt_dims, preferred_element_type=F32))
            dwr_ref[gq] += lax.dot_general(xcb[:, sl], zr_g, tn_dims, preferred_element_type=F32)
            dwi_ref[gq] += lax.dot_general(xcb[:, sl], zi_g, tn_dims, preferred_element_type=F32)
        dxc = dxc_scr[...]
        dcb_ref[...] += jnp.sum(dxc, axis=0, keepdims=True)
        extx[0:SUBLANES, :] = xbp_ref[...] * keep
        extx[SUBLANES:SUBLANES + T, :] = xb_ref[...]
        extd[0:T, :] = dxc
        dxb = jnp.zeros((T, D), F32)
        for k in range(W):
            dxb = dxb + extd[pl.ds(W - 1 - k, T), :] * cw_ref[k:k + 1, :]
            dcw_ref[k:k + 1, :] += jnp.sum(dxc * extx[pl.ds(SUBLANES - (W - 1) + k, T), :], axis=0, keepdims=True)
        extd[T:T + SUBLANES, :] = dxc[0:SUBLANES, :]
        du_ref[:, D:2 * D] = dxb.astype(BF16)

    rev = lambda i: nT - 1 - i
    tpb = T // SUBLANES
    prev8 = lambda i: jnp.maximum(rev(i) * tpb - 1, 0)
    row = pl.BlockSpec((T, D), lambda i: (rev(i), 0))
    vec = pl.BlockSpec((1, D), lambda i: (0, 0))
    bd = pl.BlockSpec((nG, GT, GT), lambda i: (0, 0, 0))
    vec_o = jax.ShapeDtypeStruct((1, D), F32)
    bd_o = jax.ShapeDtypeStruct((nG, GT, GT), F32)
    return _hosted_call(
        body, "lru_bwd", (nT,),
        [row, row, pl.BlockSpec((T, D), lambda i: (rev(i), 1)), pl.BlockSpec((SUBLANES, D), lambda i: (prev8(i), 1)),
         row, row, row, row, pl.BlockSpec((SUBLANES, D), lambda i: (prev8(i), 0)),
         pl.BlockSpec((W, D), lambda i: (0, 0)), bd, bd, vec],
        [pl.BlockSpec((T, 2 * D), lambda i: (rev(i), 0)), pl.BlockSpec((W, D), lambda i: (0, 0)),
         vec, vec, vec, vec, bd, bd],
        [jax.ShapeDtypeStruct((S, 2 * D), BF16), jax.ShapeDtypeStruct((W, D), F32), vec_o, vec_o, vec_o, vec_o, bd_o, bd_o],
        [pltpu.VMEM((T, D), F32), pltpu.VMEM((T, D), F32), pltpu.VMEM((T + SUBLANES, D), F32),
         pltpu.VMEM((T + SUBLANES, D), F32), pltpu.VMEM((T + SUBLANES, D), F32),
         pltpu.VMEM((T, D), F32), pltpu.VMEM((2, T, D), BF16), pltpu.VMEM((1, D), F32)],
        (dy, u0, u0, u0, xc, r, ig, hs, hs, conv_w, wr_bd, wi_bd, lam), ("arbitrary",), plan)


AUG_ROWS = 16
HEAD_ROWS = 128
LSE_ROW = HEAD_DIM + 6
ONES_ROW_Q = HEAD_DIM + 3
ONES_COL_K = HEAD_DIM
ONES_ROW_V = HEAD_DIM


def _split3(x):
    b1 = x.astype(BF16).astype(F32)
    r = x - b1
    b2 = r.astype(BF16).astype(F32)
    return b1, b2, r - b2


def _head_block(x, aug, T):
    row = lax.broadcasted_iota(jnp.int32, (AUG_ROWS, T), 0)
    blk = jnp.zeros((AUG_ROWS, T), F32)
    for i, e in enumerate(aug):
        blk = jnp.where(row == i, e, blk)
    return jnp.concatenate([x, blk, jnp.zeros((HEAD_ROWS - HEAD_DIM - AUG_ROWS, T), F32)], axis=0)


def _tri_matrix(lower):
    i = np.arange(LANES)
    m = (i[:, None] >= i[None, :]) if lower else (i[:, None] <= i[None, :])
    return jnp.asarray(m.astype(np.float32), BF16)


def _lane_cumsum(x, tri_ref, carry, reverse):
    n = x.shape[1] // LANES
    tri = tri_ref[...]
    out = [None] * n
    for j in (range(n - 1, -1, -1) if reverse else range(n)):
        cs = carry
        for part in _split3(x[:, j * LANES:(j + 1) * LANES]):
            cs = cs + jnp.dot(part.astype(BF16), tri, preferred_element_type=F32)
        out[j] = cs
        carry = cs[:, 0:1] if reverse else cs[:, LANES - 1:LANES]
    return jnp.concatenate(out, axis=1), carry


def _head_rows(h):
    return pl.ds(pl.multiple_of(h * HEAD_DIM, HEAD_DIM), HEAD_DIM)


def _fox_prep(ut, b_f, qg, kg, S, D, tq):
    H = D // HEAD_DIM
    T = min(tq, 256)
    per = tq // T
    scale = HEAD_DIM ** -0.5

    def body(q_ref, k_ref, v_ref, f_ref, bf_ref, qg_ref, kg_ref, tri_ref,
             qat_ref, kat_ref, vat_ref, ka_ref, c_scr, ccar):
        @pl.when(pl.program_id(0) == 0)
        def _():
            ccar[...] = jnp.zeros_like(ccar)

        c, carry = _lane_cumsum(_log_sigmoid(f_ref[...] + bf_ref[...]), tri_ref, ccar[...], False)
        c_scr[...] = c
        ccar[...] = carry

        def head(h, _):
            rows = _head_rows(h)
            c1, c2, c3 = _split3(c_scr[pl.ds(h, 1), :])

            def normed(src, gain, mul):
                x = src[rows, :]
                rs = lax.rsqrt(jnp.mean(x * x, axis=0, keepdims=True) + EPS)
                return ((x * rs) * gain[rows, :]) * mul

            qat_ref[h] = _head_block(normed(q_ref, qg_ref, scale), [c1, c2, c3, 1.0, 1.0, 1.0], T).astype(BF16)
            kb = _head_block(normed(k_ref, kg_ref, 1.0), [1.0, 1.0, 1.0, -c1, -c2, -c3, 1.0, 1.0, 1.0], T)
            kat_ref[h] = kb.astype(BF16)
            ka_ref[h] = kb.T.astype(BF16)
            vat_ref[h] = _head_block(v_ref[rows, :], [1.0, 1.0, 1.0], T).astype(BF16)
            return 0

        lax.fori_loop(0, H, head, 0)

    part = lambda j: pl.BlockSpec((D, T), lambda i: (j, i))
    colv = lambda n: pl.BlockSpec((n, 1), lambda i: (0, 0))
    tmaj = lambda r: pl.BlockSpec((H, None, r, T), lambda i: (0, i // per, 0, i % per))
    norm = pl.BlockSpec((H, T, HEAD_ROWS), lambda i: (0, i, 0))
    tshape = lambda r: jax.ShapeDtypeStruct((H, S // tq, r, tq), BF16)
    nshape = jax.ShapeDtypeStruct((H, S, HEAD_ROWS), BF16)
    return pl.pallas_call(
        body, name="fox_prep", grid=(S // T,),
        in_specs=[part(0), part(1), part(2), pl.BlockSpec((LANES, T), lambda i: (3 * D // LANES, i)),
                  colv(LANES), colv(D), colv(D), pl.BlockSpec((LANES, LANES), lambda i: (0, 0))],
        out_specs=[tmaj(HEAD_ROWS), tmaj(HEAD_ROWS), tmaj(HEAD_ROWS), norm],
        out_shape=[tshape(HEAD_ROWS), tshape(HEAD_ROWS), tshape(HEAD_ROWS), nshape],
        scratch_shapes=[pltpu.VMEM((LANES, T), F32), pltpu.VMEM((LANES, 1), F32)],
        compiler_params=_params(("arbitrary",)),
    )(ut, ut, ut, ut, b_f, qg, kg, _tri_matrix(False))


def _fox_bwd_prep(dot, ot, lse, qat, S, D, tq, plan=None):
    H = D // HEAD_DIM
    T = min(tq, 256)
    per = tq // T

    def body(do_ref, o_ref, lse_ref, qat_ref, doat_ref, doa_ref, qat1_ref, qa1_ref):
        row = lax.broadcasted_iota(jnp.int32, (HEAD_ROWS, T), 0)

        def head(h, _):
            rows = _head_rows(h)
            do = do_ref[rows, :].astype(F32)
            delta = jnp.sum(do * o_ref[rows, :], axis=0, keepdims=True)
            db = _head_block(do, list(_split3(-delta)), T)
            doat_ref[h] = db.astype(BF16)
            doa_ref[h] = db.T.astype(BF16)
            qb = qat_ref[h].astype(F32)
            for i, e in enumerate(_split3(-lse_ref[h])):
                qb = jnp.where(row == LSE_ROW + i, e, qb)
            qat1_ref[h] = qb.astype(BF16)
            qa1_ref[h] = qb.T.astype(BF16)
            return 0

        lax.fori_loop(0, H, head, 0)

    chan = pl.BlockSpec((D, T), lambda i: (0, i))
    tmaj = pl.BlockSpec((H, None, HEAD_ROWS, T), lambda i: (0, i // per, 0, i % per))
    norm = pl.BlockSpec((H, T, HEAD_ROWS), lambda i: (0, i, 0))
    tshape = jax.ShapeDtypeStruct((H, S // tq, HEAD_ROWS, tq), BF16)
    nshape = jax.ShapeDtypeStruct((H, S, HEAD_ROWS), BF16)
    return _hosted_call(body, "fox_bwd_prep", (S // T,), [chan, chan, pl.BlockSpec((H, 1, T), lambda i: (0, 0, i)), tmaj],
                        [tmaj, norm, tmaj, norm], [tshape, nshape, tshape, nshape], [], (dot, ot, lse, qat),
                        ("arbitrary",), plan)


def _causal(s, k_axis):
    ki = lax.broadcasted_iota(jnp.int32, s.shape, k_axis)
    qi = lax.broadcasted_iota(jnp.int32, s.shape, 1 - k_axis)
    return jnp.where(ki <= qi, s, NEG_INF)


def _seq_tile(i, t):
    return pl.ds(pl.multiple_of(i * t, t), t)


def _attn_forward(ka, qat, vat, S, D, tq, plan=None):
    H = D // HEAD_DIM
    nq = S // tq

    def body(ka_ref, qat_ref, vat_ref, o_ref, o32_ref, lse_ref, m_scr, acc_scr):
        qi = pl.program_id(1)
        m_scr[...] = jnp.full_like(m_scr, NEG_INF)
        acc_scr[...] = jnp.zeros_like(acc_scr)
        qa = qat_ref[...]

        def span(k0, n, diagonal):
            s = jnp.dot(ka_ref[pl.ds(pl.multiple_of(k0 * tq, tq), n * tq), :], qa, preferred_element_type=F32)
            if diagonal:
                s = _causal(s, 0)
            m_prev = m_scr[...]
            m_new = jnp.maximum(m_prev, jnp.max(s, axis=0, keepdims=True))
            p = jnp.exp(s - m_new).astype(BF16)
            upd = jnp.dot(vat_ref[k0], p[0:tq], preferred_element_type=F32)
            for i in range(1, n):
                upd = upd + jnp.dot(vat_ref[k0 + i], p[i * tq:(i + 1) * tq], preferred_element_type=F32)
            acc_scr[...] = jnp.exp(m_prev - m_new) * acc_scr[...] + upd
            m_scr[...] = m_new

        def off_diagonal_pair(j, _):
            span(2 * j, 2, False)
            return 0

        lax.fori_loop(0, qi // 2, off_diagonal_pair, 0)
        pl.when(qi % 2 == 1)(lambda: span(qi - 1, 1, False))
        span(qi, 1, True)
        l = acc_scr[ONES_ROW_V:ONES_ROW_V + 1, :]
        o = acc_scr[0:HEAD_DIM, :] / l
        o_ref[...] = o.astype(BF16)
        o32_ref[...] = o
        lse_ref[...] = m_scr[...] + jnp.log(l)

    chan = pl.BlockSpec((HEAD_DIM, tq), lambda h, i: (h, i))
    stat = pl.BlockSpec((None, 1, tq), lambda h, i: (h, 0, i))
    return _hosted_call(
        body, "attn_forward", (H, nq),
        [pl.BlockSpec((None, S, HEAD_ROWS), lambda h, i: (h, 0, 0)),
         pl.BlockSpec((None, None, HEAD_ROWS, tq), lambda h, i: (h, i, 0, 0)),
         pl.BlockSpec((None, nq, HEAD_ROWS, tq), lambda h, i: (h, 0, 0, 0))],
        [chan, chan, stat],
        [jax.ShapeDtypeStruct((D, S), BF16), jax.ShapeDtypeStruct((D, S), F32), jax.ShapeDtypeStruct((H, 1, S), F32)],
        [pltpu.VMEM((1, tq), F32), pltpu.VMEM((HEAD_ROWS, tq), F32)],
        (ka, qat, vat), ("arbitrary", "arbitrary"), plan)


def _attn_backward(qa, doa, qat, doat, ka, kat, vat, S, D, tq, plan=None):
    H = D // HEAD_DIM
    nq = S // tq

    def body(qa_ref, doa_ref, qat_ref, doat_ref, ka_ref, kat_ref, vat_ref, dq_ref, dk_ref, dv_ref, dk_scr, dv_scr):
        ki = pl.program_id(1)

        @pl.when(ki == 0)
        def _():
            dq_ref[...] = jnp.zeros_like(dq_ref)

        dk_scr[...] = jnp.zeros_like(dk_scr)
        dv_scr[...] = jnp.zeros_like(dv_scr)
        kt = kat_ref[...]
        vt = vat_ref[...]
        kn = ka_ref[...]

        def span(q0, n, diagonal):
            rows = pl.ds(pl.multiple_of(q0 * tq, tq), n * tq)
            s = jnp.dot(qa_ref[rows, :], kt, preferred_element_type=F32)
            if diagonal:
                s = _causal(s, 1)
            p = jnp.exp(s)
            ds = (p * jnp.dot(doa_ref[rows, :], vt, preferred_element_type=F32)).astype(BF16)
            p = p.astype(BF16)
            for i in range(n):
                part = slice(i * tq, (i + 1) * tq)
                dv_scr[...] += jnp.dot(doat_ref[q0 + i, 0:HEAD_DIM, :], p[part], preferred_element_type=F32)
                dk_scr[...] += jnp.dot(qat_ref[q0 + i], ds[part], preferred_element_type=F32)
            dq_ref[rows, :] += jnp.dot(ds, kn, preferred_element_type=F32)

        def off_diagonal_pair(j, _):
            span(ki + 1 + 2 * j, 2, False)
            return 0

        span(ki, 1, True)
        n_off = nq - 1 - ki
        lax.fori_loop(0, n_off // 2, off_diagonal_pair, 0)
        pl.when(n_off % 2 == 1)(lambda: span(nq - 1, 1, False))
        dk_ref[...] = dk_scr[...]
        dv_ref[...] = dv_scr[...].astype(BF16)

    whole = pl.BlockSpec((None, S, HEAD_ROWS), lambda h, i: (h, 0, 0))
    tiles = pl.BlockSpec((None, nq, HEAD_ROWS, tq), lambda h, i: (h, 0, 0, 0))
    one = pl.BlockSpec((None, None, HEAD_ROWS, tq), lambda h, i: (h, i, 0, 0))
    return _hosted_call(
        body, "attn_backward", (H, nq),
        [whole, whole, tiles, tiles, pl.BlockSpec((None, tq, HEAD_ROWS), lambda h, i: (h, i, 0)), one, one],
        [whole, pl.BlockSpec((None, HEAD_ROWS, tq), lambda h, i: (h, 0, i)),
         pl.BlockSpec((HEAD_DIM, tq), lambda h, i: (h, i))],
        [jax.ShapeDtypeStruct((H, S, HEAD_ROWS), F32), jax.ShapeDtypeStruct((H, HEAD_ROWS, S), F32),
         jax.ShapeDtypeStruct((D, S), BF16)],
        [pltpu.VMEM((HEAD_ROWS, tq), F32), pltpu.VMEM((HEAD_DIM, tq), F32)],
        (qa, doa, qat, doat, ka, kat, vat), ("arbitrary", "arbitrary"), plan)


def _fox_prep_bwd(ut, dq, dkt, dvt, b_f, qg, kg, S, D, tq):
    H = D // HEAD_DIM
    T = min(tq, 256)
    nT = S // T
    NU = 3 * D + LANES
    scale = HEAD_DIM ** -0.5

    def body(q_ref, k_ref, f_ref, dq_ref, dk_ref, dv_ref, bf_ref, qg_ref, kg_ref, tri_ref,
             du_ref, dbf_ref, dqg_ref, dkg_ref, gq_acc, gk_acc, fcar, dc_scr):
        step = pl.program_id(0)

        @pl.when(step == 0)
        def _():
            for ref in (gq_acc, gk_acc, fcar, dbf_ref):
                ref[...] = jnp.zeros_like(ref)

        dc_scr[...] = jnp.zeros_like(dc_scr)

        def head(h, _):
            rows = _head_rows(h)
            dqb = dq_ref[h].T
            dkb = dk_ref[h]
            dc_scr[pl.ds(h, 1), :] = dqb[ONES_COL_K:ONES_COL_K + 1, :] - dkb[ONES_ROW_Q:ONES_ROW_Q + 1, :]
            for src, dsrc, gain, acc, mul, base in ((q_ref, dqb, qg_ref, gq_acc, scale, 0),
                                                    (k_ref, dkb, kg_ref, gk_acc, 1.0, D)):
                x = src[rows, :]
                rs = lax.rsqrt(jnp.mean(x * x, axis=0, keepdims=True) + EPS)
                xhat = x * rs
                dn = dsrc[0:HEAD_DIM, :] * mul
                acc[rows, :] += jnp.sum(dn * xhat, axis=1, keepdims=True)
                dxh = dn * gain[rows, :]
                dx = rs * (dxh - xhat * jnp.mean(dxh * xhat, axis=0, keepdims=True))
                du_ref[pl.ds(pl.multiple_of(base + h * HEAD_DIM, HEAD_DIM), HEAD_DIM), :] = dx.astype(BF16)
            return 0

        lax.fori_loop(0, H, head, 0)
        du_ref[2 * D:3 * D, :] = dv_ref[...]
        dlf, carry = _lane_cumsum(dc_scr[...], tri_ref, fcar[...], True)
        fcar[...] = carry
        dfl = dlf * _sigmoid(-(f_ref[...] + bf_ref[...]))
        dbf_ref[...] += jnp.sum(dfl, axis=1, keepdims=True)
        du_ref[3 * D:NU, :] = dfl.astype(BF16)

        @pl.when(step == nT - 1)
        def _():
            for acc, ref in ((gq_acc, dqg_ref), (gk_acc, dkg_ref)):
                tot = jnp.zeros((HEAD_DIM, 1), F32)
                for h in range(H):
                    tot = tot + acc[h * HEAD_DIM:(h + 1) * HEAD_DIM, :]
                ref[...] = tot

    rev = lambda i: nT - 1 - i
    part = lambda j: pl.BlockSpec((D, T), lambda i: (j, rev(i)))
    colv = lambda n: pl.BlockSpec((n, 1), lambda i: (0, 0))
    return pl.pallas_call(
        body, name="fox_prep_bwd", grid=(nT,),
        in_specs=[part(0), part(1), pl.BlockSpec((LANES, T), lambda i: (3 * D // LANES, rev(i))),
                  pl.BlockSpec((H, T, HEAD_ROWS), lambda i: (0, rev(i), 0)),
                  pl.BlockSpec((H, HEAD_ROWS, T), lambda i: (0, 0, rev(i))), pl.BlockSpec((D, T), lambda i: (0, rev(i))),
                  colv(LANES), colv(D), colv(D), pl.BlockSpec((LANES, LANES), lambda i: (0, 0))],
        out_specs=[pl.BlockSpec((NU, T), lambda i: (0, rev(i))), colv(LANES), colv(HEAD_DIM), colv(HEAD_DIM)],
        out_shape=[jax.ShapeDtypeStruct((NU, S), BF16), jax.ShapeDtypeStruct((LANES, 1), F32),
                   jax.ShapeDtypeStruct((HEAD_DIM, 1), F32), jax.ShapeDtypeStruct((HEAD_DIM, 1), F32)],
        scratch_shapes=[pltpu.VMEM((D, 1), F32), pltpu.VMEM((D, 1), F32), pltpu.VMEM((LANES, 1), F32),
                        pltpu.VMEM((LANES, T), F32)],
        compiler_params=_params(("arbitrary",)),
    )(ut, ut, ut, dq, dkt, dvt, b_f, qg, kg, _tri_matrix(True))


def _block_diag_tiles(w):
    n = w.shape[0]
    per = min(MXU_DIM, n * LRU_BLOCK_DIM) // LRU_BLOCK_DIM
    eye = jnp.eye(per, dtype=w.dtype)
    w5 = w.reshape(n // per, per, LRU_BLOCK_DIM, 1, LRU_BLOCK_DIM) * eye[None, :, None, :, None]
    return w5.reshape(n // per, per * LRU_BLOCK_DIM, per * LRU_BLOCK_DIM).astype(BF16)


def _block_diag_extract(t, n):
    per = t.shape[-1] // LRU_BLOCK_DIM
    eye = jnp.eye(per, dtype=t.dtype)
    t5 = t.reshape(n // per, per, LRU_BLOCK_DIM, per, LRU_BLOCK_DIM) * eye[None, :, None, :, None]
    return t5.sum(axis=3).reshape(n, LRU_BLOCK_DIM, LRU_BLOCK_DIM)


def _local_step(x, tgt, small, wv, grad_view, comm=None):
    S, D = x.shape
    F = 4 * D
    H = D // HEAD_DIM
    nblk = D // LRU_BLOCK_DIM
    NU = 3 * D + LANES
    tq = max(LANES, min(512, S // 4))
    assert S % tq == 0
    vec = lambda a: a.reshape(1, -1).astype(F32)
    col = lambda a: a.reshape(-1, 1).astype(F32)
    mix_g, mlp_g = small["mix_norm"], small["mlp_norm"]
    conv_b = vec(small["lru_conv_b"])
    wr_bd, wi_bd = _block_diag_tiles(small["lru_w_r"][0]), _block_diag_tiles(small["lru_w_i"][0])
    b_r, b_i, lam = vec(small["lru_b_r"]), vec(small["lru_b_i"]), vec(small["lru_lambda"])
    b_f = jnp.pad(col(small["fox_b_f"]), ((0, LANES - H), (0, 0)))
    qg, kg = jnp.tile(col(small["fox_q_gain"]), (H, 1)), jnp.tile(col(small["fox_k_gain"]), (H, 1))
    X = lambda a: _View(a)
    grads = {}
    gout = functools.partial(grad_view, grads)

    def hosted(name, fn, *args):
        plan = comm.before(name, grads) if comm is not None else None
        res, side = fn(*args, plan=plan)
        if plan is not None:
            comm.after(name, side, wv)
        return res

    def hosted_mm(name, *args, **kw):
        plan = comm.before(name, grads) if comm is not None else None
        if plan is None:
            return _matmul(name, *args, **kw)
        res, side = _matmul(name, *args, plan=plan, **kw)
        comm.after(name, side, wv)
        return res

    norm_rows = _pick(S, (512, 256, 128))
    two = lambda: [_fresh(S, D, F32), _fresh(S, D, BF16)]

    def mlp_up(l, hm):
        return hosted_mm(f"mlp{l}_up", X(hm), wv[f"w1_{l}"], S, F, D, outs=[_fresh(S, F, BF16), _fresh(S, F, BF16)],
                         epilogue=_ep_relu2)

    def mlp_bwd(l, xin, hm, z, act, d, db):
        (dz,) = hosted_mm(f"mlp{l}_dact", X(db), wv[f"w2_{l}"], S, F, D, tb=True, outs=[_fresh(S, F, BF16)],
                          epilogue=_ep_drelu2, extras=[X(z)])
        (grads[f"w2_{l}"],) = _matmul(f"mlp{l}_dw2", X(act), X(db), F, D, S, ta=True, outs=[gout(f"w2_{l}")],
                                      epilogue=_ep_store)
        (grads[f"w1_{l}"],) = _matmul(f"mlp{l}_dw1", X(hm), X(dz), D, F, S, ta=True, outs=[gout(f"w1_{l}")],
                                      epilogue=_ep_store)
        return _matmul(f"mlp{l}_dhm", X(dz), wv[f"w1_{l}"], S, D, F, tb=True, outs=two(), n_sums=1,
                       epilogue=_ep_norm_bwd, extras=[X(xin), X(d)], vecs=[mlp_g[l:l + 1]], tm=norm_rows)

    (h0,) = hosted("mix0_norm", _rms_fwd, "mix0_norm", x, mix_g[0:1], S, D)
    conv_w = small["conv_w"]
    (u0,) = _matmul("lru_in", X(h0), wv["lru_in"], S, 2 * D, D, outs=[_fresh(S, 2 * D, F32)], epilogue=_ep_store)
    y, xc, r, ig, hs = hosted("lru_fwd", _lru_fwd, u0, conv_w, conv_b, wr_bd, b_r, wi_bd, b_i, lam, S, D)
    x1, hm0 = _matmul("lru_out", X(y), wv["lru_out"], S, D, D, outs=two(), epilogue=_ep_resid_norm, extras=[X(x)],
                      vecs=[mlp_g[0:1]], tm=norm_rows)
    z0, act0 = mlp_up(0, hm0)
    x2, h1 = hosted_mm("mlp0_down", X(act0), wv["w2_0"], S, D, F, outs=two(), epilogue=_ep_resid_norm, extras=[X(x1)],
                       vecs=[mix_g[1:2]], tm=norm_rows)
    (u1,) = _matmul("fox_in", wv["fox_in"], X(h1), NU, S, D, tb=True, outs=[_fresh(NU, S, F32)], epilogue=_ep_store)
    qat, kat, vat, ka = _fox_prep(u1, b_f, qg, kg, S, D, tq)
    o, o32, lse = hosted("attn_forward", _attn_forward, ka, qat, vat, S, D, tq)
    x3, hm1 = _matmul("fox_out", X(o), wv["fox_out"], S, D, D, ta=True, outs=two(), epilogue=_ep_resid_norm,
                      extras=[X(x2)], vecs=[mlp_g[1:2]], tm=norm_rows)
    z1, act1 = mlp_up(1, hm1)
    (x4,) = _matmul("mlp1_down", X(act1), wv["w2_1"], S, D, F, outs=[_fresh(S, D, F32)], epilogue=_ep_resid,
                    extras=[X(x3)])
    loss, d4, d4b = _loss_head(x4, tgt, S, D)

    d3, d3b, dg_mlp1 = mlp_bwd(1, x3, hm1, z1, act1, d4, d4b)
    (do,) = _matmul("fox_dout", wv["fox_out"], X(d3b), D, S, D, tb=True, outs=[_fresh(D, S, BF16)], epilogue=_ep_store)
    (grads["fox_out"],) = _matmul("fox_dwout", X(o), X(d3b), D, D, S, outs=[gout("fox_out")], epilogue=_ep_store)
    doat, doa, qat1, qa1 = hosted("fox_bwd_prep", _fox_bwd_prep, do, o32, lse, qat, S, D, tq)
    dqn, dkn, dv = hosted("attn_backward", _attn_backward, qa1, doa, qat1, doat, ka, kat, vat, S, D, tq)
    du1, dbf, dqg, dkg = _fox_prep_bwd(u1, dqn, dkn, dv, b_f, qg, kg, S, D, tq)
    (grads["fox_in"],) = _matmul("fox_dwin", X(h1), X(du1), D, NU, S, ta=True, tb=True, outs=[gout("fox_in")],
                                 epilogue=_ep_store)
    d2, d2b, dg_mix1 = hosted_mm("fox_dh", X(du1), wv["fox_in"], S, D, NU, ta=True, outs=two(), n_sums=1,
                               epilogue=_ep_norm_bwd, extras=[X(x2), X(d3)], vecs=[mix_g[1:2]], tm=norm_rows)
    d1, d1b, dg_mlp0 = mlp_bwd(0, x1, hm0, z0, act0, d2, d2b)
    (grads["lru_out"],) = _matmul("lru_dwout", X(y), X(d1b), D, D, S, ta=True, outs=[gout("lru_out")],
                                  epilogue=_ep_store)
    (dy,) = hosted_mm("lru_dout", X(d1b), wv["lru_out"], S, D, D, tb=True, outs=[_fresh(S, D, F32)],
                      epilogue=_ep_store)
    du0, dcw, dcb, dlam, dbr, dbi, dwr, dwi = hosted("lru_bwd", _lru_bwd, dy, u0, xc, r, ig, hs, conv_w, wr_bd, wi_bd,
                                                     lam, S, D)
    (grads["lru_in"],) = _matmul("lru_dwin", X(h0), X(du0), D, 2 * D, S, ta=True, outs=[gout("lru_in")],
                                 epilogue=_ep_store)
    gx, dg_mix0 = hosted_mm("lru_dh", X(du0), wv["lru_in"], S, D, 2 * D, tb=True, outs=[_fresh(S, D, F32)], n_sums=1,
                            epilogue=lambda *a: _ep_norm_bwd(*a)[::2], extras=[X(x), X(d1)], vecs=[mix_g[0:1]],
                            tm=norm_rows)

    grads.update(
        mix_norm=jnp.concatenate([dg_mix0, dg_mix1], axis=0), mlp_norm=jnp.concatenate([dg_mlp0, dg_mlp1], axis=0),
        conv_w=dcw, lru_conv_b=dcb, lru_w_r=_block_diag_extract(dwr, nblk)[None], lru_b_r=dbr.reshape(1, nblk, -1),
        lru_w_i=_block_diag_extract(dwi, nblk)[None], lru_b_i=dbi.reshape(1, nblk, -1), lru_lambda=dlam,
        fox_b_f=dbf[:H].reshape(1, H), fox_q_gain=dqg.reshape(1, -1), fox_k_gain=dkg.reshape(1, -1))
    return loss, gx, grads


def _place():
    x, y, c = lax.axis_index("x"), lax.axis_index("y"), lax.axis_index("c")
    chips = [(1 - x, y), (x, 1 - y), (1 - x, 1 - y)]
    return x, y, c, 2 * x + y, chips


BOUNCE_BYTES = 1 << 20


def _bounce_shape(rows, cols, dtype):
    chunk = rows
    while chunk % 2 == 0 and chunk > 16 and chunk * cols * jnp.dtype(dtype).itemsize > BOUNCE_BYTES:
        chunk //= 2
    return pltpu.VMEM((2, chunk, cols), dtype)


def _bounce_copy(src, dst, buf, sem):
    chunk = buf.shape[1]
    n = src.shape[0] // chunk
    cin = lambda i: pltpu.make_async_copy(src.at[pl.ds(i * chunk, chunk)], buf.at[i % 2], sem.at[i % 2])
    cout = lambda i: pltpu.make_async_copy(buf.at[i % 2], dst.at[pl.ds(i * chunk, chunk)], sem.at[2 + i % 2])
    cin(0).start()
    for i in range(n):
        cin(i).wait()
        if i + 1 < n:
            if i >= 1:
                cout(i - 1).wait()
            cin(i + 1).start()
        cout(i).start()
    if n >= 2:
        cout(n - 2).wait()
    cout(n - 1).wait()


def _hbm_call(body, name, arrays, out_shape, n_dma_sems, bounce=()):
    scratch = [pltpu.SemaphoreType.DMA((k,)) for k in n_dma_sems]
    for rows, cols, dtype in bounce:
        scratch += [_bounce_shape(rows, cols, dtype), pltpu.SemaphoreType.DMA((4,))]
    return pl.pallas_call(
        body, name=name, in_specs=[ANY] * len(arrays), out_specs=[ANY] * len(out_shape), out_shape=out_shape,
        scratch_shapes=scratch,
        compiler_params=pltpu.CompilerParams(has_side_effects=True, vmem_limit_bytes=VMEM_LIMIT),
    )(*arrays)


class _Gather:
    def __init__(self, shards):
        n = self.n = len(shards)
        self.operands = list(shards)
        self.out_shape = [jax.ShapeDtypeStruct((N_CHIPS,) + tuple(a.shape), a.dtype) for a in shards]
        self.scratch = [pltpu.SemaphoreType.DMA((3 * n,)) for _ in range(4)]
        for a in shards:
            self.scratch += [_bounce_shape(a.shape[0], a.shape[1], a.dtype), pltpu.SemaphoreType.DMA((4,))]

    def _copies(self, ins, outs, scr):
        send, recv, fsend, frecv = scr[:4]
        x, y, c, s, chips = _place()

        def rows(a, chip_idx, which):
            hr = ins[a].shape[0] // 2
            return outs[a].at[chip_idx, pl.ds(which * hr, hr)]

        def landed(a, j, core):
            return rows(a, 2 * chips[j][0] + chips[j][1], core)

        def ici(a, j, mine):
            hr = ins[a].shape[0] // 2
            src, dst = (ins[a].at[pl.ds(c * hr, hr)], rows(a, s, c)) if mine else (landed(a, j, c),) * 2
            return pltpu.make_async_remote_copy(src_ref=src, dst_ref=dst, send_sem=send.at[3 * a + j],
                                                recv_sem=recv.at[3 * a + j], device_id=(*chips[j], c),
                                                device_id_type=MESH)

        def d2d(a, j, mine):
            ref = landed(a, j, c if mine else 1 - c)
            return pltpu.make_async_remote_copy(src_ref=ref, dst_ref=ref, send_sem=fsend.at[3 * a + j],
                                                recv_sem=frecv.at[3 * a + j], device_id=(x, y, 1 - c),
                                                device_id_type=MESH)

        return ici, d2d, s

    def start(self, ins, outs, scr):
        ici, _, _ = self._copies(ins, outs, scr)
        for a in range(self.n):
            for j in range(3):
                ici(a, j, True).start()

    def middle(self, ins, outs, scr):
        ici, d2d, s = self._copies(ins, outs, scr)
        for a in range(self.n):
            _bounce_copy(ins[a], outs[a].at[s], scr[4 + 2 * a], scr[5 + 2 * a])
        for a in range(self.n):
            for j in range(3):
                ici(a, j, False).wait_recv()
                d2d(a, j, True).start()

    def finish(self, ins, outs, scr):
        ici, d2d, _ = self._copies(ins, outs, scr)
        for a in range(self.n):
            for j in range(3):
                d2d(a, j, False).wait_recv()
        for a in range(self.n):
            for j in range(3):
                ici(a, j, True).wait_send()
                d2d(a, j, True).wait_send()


def _run_plan(name, plan):
    k_in, k_out = len(plan.operands), len(plan.out_shape)

    def body(*refs):
        parts = (refs[:k_in], refs[k_in:k_in + k_out], refs[k_in + k_out:])
        plan.start(*parts)
        plan.middle(*parts)
        plan.finish(*parts)

    return pl.pallas_call(
        body, name=name, in_specs=[ANY] * k_in, out_specs=[ANY] * k_out, out_shape=plan.out_shape,
        scratch_shapes=plan.scratch,
        compiler_params=pltpu.CompilerParams(has_side_effects=True, vmem_limit_bytes=VMEM_LIMIT),
    )(*plan.operands)


def _hosted_call(body, name, grid, in_specs, out_specs, out_shape, scratch_shapes, operands, sem, plan=None):
    if plan is None:
        res = pl.pallas_call(body, name=name, grid=grid, in_specs=in_specs, out_specs=out_specs, out_shape=out_shape,
                             scratch_shapes=scratch_shapes, compiler_params=_params(sem))(*operands)
        return res, None
    n_in, n_out, n_scr = len(in_specs), len(out_specs), len(scratch_shapes)
    k_in, k_out = len(plan.operands), len(plan.out_shape)
    total = int(np.prod(grid))
    late = max(0, total - 1 - max(1, total // 8))

    def hosted(*refs):
        ins, refs = refs[:n_in], refs[n_in:]
        p_ins, refs = refs[:k_in], refs[k_in:]
        outs, refs = refs[:n_out], refs[n_out:]
        p_outs, refs = refs[:k_out], refs[k_out:]
        scr, p_scr = refs[:n_scr], refs[n_scr:]
        step = pl.program_id(0)
        for d in range(1, len(grid)):
            step = step * grid[d] + pl.program_id(d)
        pl.when(step == 0)(lambda: plan.start(p_ins, p_outs, p_scr))
        body(*ins, *outs, *scr)
        pl.when(step == late)(lambda: plan.middle(p_ins, p_outs, p_scr))
        pl.when(step == total - 1)(lambda: plan.finish(p_ins, p_outs, p_scr))

    res = pl.pallas_call(
        hosted, name=name, grid=grid, in_specs=list(in_specs) + [ANY] * k_in, out_specs=list(out_specs) + [ANY] * k_out,
        out_shape=list(out_shape) + plan.out_shape, scratch_shapes=list(scratch_shapes) + plan.scratch,
        compiler_params=pltpu.CompilerParams(dimension_semantics=sem, vmem_limit_bytes=VMEM_LIMIT,
                                             has_side_effects=True),
    )(*operands, *plan.operands)
    return res[:n_out], res[n_out:]


def _all_gather(name, shards):
    return _run_plan(name, _Gather(shards))


class _Swap:
    def __init__(self, arrs):
        self.n = len(arrs)
        self.operands = list(arrs)
        self.out_shape = [jax.ShapeDtypeStruct((a.shape[0], a.shape[1] // 2, a.shape[2]), a.dtype) for a in arrs]
        self.scratch = [pltpu.SemaphoreType.DMA((self.n,)) for _ in range(2)]

    def _copy(self, ins, outs, scr, a):
        x, y, c, _, _ = _place()
        hr = ins[a].shape[1] // 2
        return pltpu.make_async_remote_copy(
            src_ref=ins[a].at[:, pl.ds((1 - c) * hr, hr)], dst_ref=outs[a], send_sem=scr[0].at[a],
            recv_sem=scr[1].at[a], device_id=(x, y, 1 - c), device_id_type=MESH)

    def start(self, ins, outs, scr):
        for a in range(self.n):
            self._copy(ins, outs, scr, a).start()

    def middle(self, ins, outs, scr):
        pass

    def finish(self, ins, outs, scr):
        for a in range(self.n):
            self._copy(ins, outs, scr, a).wait()


class _Scatter:
    def __init__(self, parts):
        n = self.n = len(parts)
        self.operands = list(parts)
        self.out_shape = [jax.ShapeDtypeStruct(a.shape, a.dtype) for a in parts]
        self.scratch = [pltpu.SemaphoreType.DMA((3 * n,)) for _ in range(2)]
        for a in parts:
            self.scratch += [_bounce_shape(a.shape[1], a.shape[2], a.dtype), pltpu.SemaphoreType.DMA((4,))]

    def _copy(self, ins, outs, scr, a, j, mine):
        x, y, c, s, chips = _place()
        t = 2 * chips[j][0] + chips[j][1]
        return pltpu.make_async_remote_copy(
            src_ref=ins[a].at[t], dst_ref=outs[a].at[s if mine else t], send_sem=scr[0].at[3 * a + j],
            recv_sem=scr[1].at[3 * a + j], device_id=(*chips[j], c), device_id_type=MESH)

    def start(self, ins, outs, scr):
        for a in range(self.n):
            for j in range(3):
                self._copy(ins, outs, scr, a, j, True).start()

    def middle(self, ins, outs, scr):
        s = _place()[3]
        for a in range(self.n):
            _bounce_copy(ins[a].at[s], outs[a].at[s], scr[2 + 2 * a], scr[3 + 2 * a])

    def finish(self, ins, outs, scr):
        for a in range(self.n):
            for j in range(3):
                self._copy(ins, outs, scr, a, j, False).wait_recv()
        for a in range(self.n):
            for j in range(3):
                self._copy(ins, outs, scr, a, j, True).wait_send()


def _pair_gather(name, halves):
    n = len(halves)

    def body(*refs):
        ins, outs = refs[:n], refs[n:2 * n]
        send, recv = refs[2 * n:2 * n + 2]
        stage = refs[2 * n + 2:]
        x, y, c, _, _ = _place()
        cps = []
        for a in range(n):
            hr = ins[a].shape[0]
            cp = pltpu.make_async_remote_copy(
                src_ref=ins[a], dst_ref=outs[a].at[pl.ds(c * hr, hr)], send_sem=send.at[a], recv_sem=recv.at[a],
                device_id=(x, y, 1 - c), device_id_type=MESH)
            cp.start()
            cps.append((cp, hr))
        for a, (cp, hr) in enumerate(cps):
            _bounce_copy(ins[a], outs[a].at[pl.ds(c * hr, hr)], stage[2 * a], stage[2 * a + 1])
        for a, (cp, hr) in enumerate(cps):
            cp.wait_send()
            theirs = outs[a].at[pl.ds((1 - c) * hr, hr)]
            pltpu.make_async_remote_copy(src_ref=theirs, dst_ref=theirs, send_sem=send.at[a], recv_sem=recv.at[a],
                                         device_id=(x, y, 1 - c), device_id_type=MESH).wait_recv()

    out_shape = [jax.ShapeDtypeStruct((2 * a.shape[0], a.shape[1]), a.dtype) for a in halves]
    return _hbm_call(body, name, halves, out_shape, (n, n),
                     bounce=[(a.shape[0], a.shape[1], a.dtype) for a in halves])


def _row_tile(rows, cols, itemsize, n_bufs):
    budget = VMEM_LIMIT // 2
    for t in (1024, 512, 256, 128, 64, 32, 16):
        if rows % t == 0 and 2 * n_bufs * t * cols * itemsize <= budget:
            return t
    return rows


def _pair_add(name, g, gsib, core, out_dtype):
    _, r, cols = g.shape
    hr = r // 2
    t = _row_tile(hr, cols, 4, 3)
    per = hr // t

    def body(core_ref, a_ref, b_ref, o_ref):
        o_ref[...] = (a_ref[...].astype(F32) + b_ref[...].astype(F32)).astype(o_ref.dtype)

    grid_spec = pltpu.PrefetchScalarGridSpec(
        num_scalar_prefetch=1, grid=(N_CHIPS, per),
        in_specs=[pl.BlockSpec((None, t, cols), lambda s, i, core: (s, core[0] * per + i, 0)),
                  pl.BlockSpec((None, t, cols), lambda s, i, core: (s, i, 0))],
        out_specs=pl.BlockSpec((None, t, cols), lambda s, i, core: (s, i, 0)))
    return pl.pallas_call(body, name=name, grid_spec=grid_spec,
                          out_shape=jax.ShapeDtypeStruct((N_CHIPS, hr, cols), out_dtype),
                          compiler_params=_params(("arbitrary", "arbitrary")))(core, g, gsib)


def _chip_sum(name, parts):
    _, hr, cols = parts.shape
    t = _row_tile(hr, cols, 4, 5)

    def body(p_ref, o_ref):
        o_ref[...] = ((p_ref[0].astype(F32) + p_ref[1].astype(F32)) + p_ref[2].astype(F32)) + p_ref[3].astype(F32)

    return pl.pallas_call(
        body, name=name, grid=(hr // t,), in_specs=[pl.BlockSpec((N_CHIPS, t, cols), lambda i: (0, i, 0))],
        out_specs=pl.BlockSpec((t, cols), lambda i: (i, 0)), out_shape=jax.ShapeDtypeStruct((hr, cols), F32),
        compiler_params=_params(("arbitrary",)))(parts)


def _pair_partials(tag, arrs, sib, wire_dtypes, core):
    return _Scatter([_pair_add(f"{tag}_pair_add{i}", g, gs, core, dt)
                     for i, (g, gs, dt) in enumerate(zip(arrs, sib, wire_dtypes))])


def _finish_reduce(tag, scattered):
    halves = [_chip_sum(f"{tag}_chip_sum{i}", p) for i, p in enumerate(scattered)]
    return _pair_gather(f"{tag}_pair_gather", halves)


def _adamw(name, w, g_parts, m, v):
    thin = w.ndim == 3
    rows, cols = w.shape[0], w.shape[-1]
    n_parts = len(g_parts)
    part_rows = rows // n_parts
    t = max(d for d in range(1, 257) if part_rows % d == 0) if thin else _row_tile(part_rows, cols, 4, 7 + n_parts)
    per = part_rows // t
    c1 = 1.0 - ADAM_B1 ** ADAM_STEP
    c2 = 1.0 - ADAM_B2 ** ADAM_STEP

    def body(w_ref, m_ref, v_ref, *refs):
        g_refs, (go_ref, d_ref, nm_ref, nv_ref) = refs[:n_parts], refs[n_parts:]
        g = g_refs[0][...]
        for k in range(1, n_parts):
            g = jnp.where(pl.program_id(0) >= k * per, g_refs[k][...], g)
        go_ref[...] = g
        m = ADAM_B1 * m_ref[...] + (1.0 - ADAM_B1) * g
        v = ADAM_B2 * v_ref[...] + (1.0 - ADAM_B2) * (g * g)
        nm_ref[...] = m
        nv_ref[...] = v
        d_ref[...] = -ADAM_LR * ((m / c1) / (jnp.sqrt(v / c2) + ADAM_EPS) + ADAM_WD * w_ref[...])

    block = (t, 1, cols) if thin else (t, cols)
    at = lambda r: (r, 0, 0) if thin else (r, 0)
    spec = pl.BlockSpec(block, lambda i: at(i))
    g_specs = [pl.BlockSpec(block, lambda i, k=k: at(jnp.clip(i - k * per, 0, per - 1))) for k in range(n_parts)]
    shp = jax.ShapeDtypeStruct(w.shape, F32)
    return pl.pallas_call(body, name=name, grid=(rows // t,), in_specs=[spec] * 3 + g_specs, out_specs=[spec] * 4,
                          out_shape=[shp] * 4, compiler_params=_params(("arbitrary",)))(w, m, v, *g_parts)


_WEIGHTS = ["mix_norm", "mlp_norm", "mlp_w1", "mlp_w2", "lru_w_in", "lru_conv_w", "lru_conv_b", "lru_w_r", "lru_b_r",
            "lru_w_i", "lru_b_i", "lru_lambda", "lru_w_out", "fox_w_in", "fox_b_f", "fox_q_gain", "fox_k_gain",
            "fox_w_out"]
_REPLICATED = ["mix_norm", "mlp_norm", "lru_conv_b", "lru_w_r", "lru_b_r", "lru_w_i", "lru_b_i", "lru_lambda",
               "fox_b_f", "fox_q_gain", "fox_k_gain"]
_PACK_TILE = 2 * SUBLANES * LANES


def _as2d(a):
    return a.reshape(-1, a.shape[-1])


def kernel(x, mix_norm, mlp_norm, mlp_w1, mlp_w2, lru_w_in, lru_conv_w, lru_conv_b, lru_w_r, lru_b_r, lru_w_i, lru_b_i, lru_lambda, lru_w_out, fox_w_in, fox_b_f, fox_q_gain, fox_k_gain, fox_w_out, loss_target, m_mix_norm, m_mlp_norm, m_mlp_w1, m_mlp_w2, m_lru_w_in, m_lru_conv_w, m_lru_conv_b, m_lru_w_r, m_lru_b_r, m_lru_w_i, m_lru_b_i, m_lru_lambda, m_lru_w_out, m_fox_w_in, m_fox_b_f, m_fox_q_gain, m_fox_k_gain, m_fox_w_out, v_mix_norm, v_mlp_norm, v_mlp_w1, v_mlp_w2, v_lru_w_in, v_lru_conv_w, v_lru_conv_b, v_lru_w_r, v_lru_b_r, v_lru_w_i, v_lru_b_i, v_lru_lambda, v_lru_w_out, v_fox_w_in, v_fox_b_f, v_fox_q_gain, v_fox_k_gain, v_fox_w_out):
    args = dict(locals())
    W = {n: args[n] for n in _WEIGHTS}
    Mo = {n: args["m_" + n] for n in _WEIGHTS}
    Vo = {n: args["v_" + n] for n in _WEIGHTS}
    S, D = x.shape[1], x.shape[2]
    F = 4 * D
    H = D // HEAD_DIM
    NU = 3 * D + LANES
    FQ, DQ = F // N_CHIPS, D // N_CHIPS
    nfox = fox_w_in.shape[-1]
    chip = 2 * lax.axis_index("x") + lax.axis_index("y")
    core = lax.axis_index("c").astype(jnp.int32).reshape(1)

    cw_flat = jnp.pad(lru_conv_w.reshape(-1), (0, _PACK_TILE - CONV_WIDTH * DQ)).reshape(2 * SUBLANES, LANES)
    w1s, w2s = mlp_w1.astype(BF16), mlp_w2.astype(BF16)
    wv = {}
    small = {n: W[n] for n in _REPLICATED}
    scattered = {}
    members = {"g1": ["w2_1", "w1_1", "fox_out"], "g2": ["fox_in"], "g3": ["w2_0", "w1_0"], "g4": ["lru_out", "lru_in"]}
    swap_at = {"fox_bwd_prep": "g1", "fox_dh": "g2", "lru_dout": "g3"}
    scatter_at = {"attn_backward": "g1", "mlp0_dact": "g2", "lru_bwd": "g3", "lru_dh": "g4"}
    swapped = {}

    def shard_major(name, g):
        if name == "fox_in":
            return jnp.transpose(g[:, :nfox * N_CHIPS].reshape(D, N_CHIPS, nfox), (1, 0, 2))
        return g

    class Comm:
        @staticmethod
        def before(name, grads):
            if name == "mix0_norm":
                return _Gather([lru_w_in[0].astype(BF16), lru_w_out[0].astype(BF16), cw_flat])
            if name == "lru_fwd":
                return _Gather([w1s[0]])
            if name == "mlp0_up":
                return _Gather([w2s[0]])
            if name == "mlp0_down":
                return _Gather([fox_w_in[0].astype(BF16)])
            if name == "attn_forward":
                return _Gather([fox_w_out[0].astype(BF16), w1s[1], w2s[1]])
            if name in swap_at:
                group = swap_at[name]
                swapped[group] = [[shard_major(n, grads[n]) for n in members[group]], None]
                return _Swap(swapped[group][0])
            if name in scatter_at:
                group = scatter_at[name]
                if group not in swapped:
                    arrs = [shard_major(n, grads[n]) for n in members[group]]
                    swapped[group] = [arrs, _run_plan(f"{group}_pair_swap", _Swap(arrs))]
                arrs, sib = swapped[group]
                return _pair_partials(group, arrs, sib, [BF16] * len(arrs), core)
            return None

        @staticmethod
        def after(name, res, wv):
            if name == "mix0_norm":
                wv.update(lru_in=_View(res[0], "cs"), lru_out=_View(res[1], "rs"))
                taps = res[2].reshape(N_CHIPS, -1)[:, :CONV_WIDTH * DQ].reshape(N_CHIPS, CONV_WIDTH, DQ)
                small["conv_w"] = jnp.transpose(taps, (1, 0, 2)).reshape(CONV_WIDTH, D)
            elif name == "lru_fwd":
                wv.update(w1_0=_View(res[0], "cs"))
            elif name == "mlp0_up":
                wv.update(w2_0=_View(res[0], "rs"))
            elif name == "mlp0_down":
                fox_full = jnp.concatenate([res[0][s] for s in range(N_CHIPS)], axis=1)
                fox_full = jnp.pad(fox_full, ((0, 0), (0, NU - fox_full.shape[1])))
                wv.update(fox_in=_View(fox_full.T))
            elif name == "attn_forward":
                wv.update(fox_out=_View(res[0], "rs"), w1_1=_View(res[1], "cs"), w2_1=_View(res[2], "rs"))
            elif name in swap_at:
                swapped[swap_at[name]][1] = res
            else:
                scattered.update(zip(members[scatter_at[name]], res))

    def grad_view(grads, name):
        if name in ("w1_0", "w1_1"):
            return _View(None, "cs", shape=(N_CHIPS, D, FQ), dtype=BF16)
        if name in ("w2_0", "w2_1"):
            return _View(None, "rs", shape=(N_CHIPS, FQ, D), dtype=BF16)
        if name == "lru_in":
            return _View(None, "cs", shape=(N_CHIPS, D, 2 * D // N_CHIPS), dtype=BF16)
        if name in ("lru_out", "fox_out"):
            return _View(None, "rs", shape=(N_CHIPS, DQ, D), dtype=BF16)
        return _View(None, shape=(D, NU), dtype=BF16)

    loss, gx, grads = _local_step(x[0], loss_target[0], small, wv, grad_view, Comm)

    pack_names = _REPLICATED + ["conv_w"]
    flat = jnp.concatenate([grads[n].reshape(-1).astype(F32) for n in pack_names] + [loss.reshape(-1)])
    per_chip = -(-flat.shape[0] // (N_CHIPS * _PACK_TILE)) * _PACK_TILE
    pack = jnp.pad(flat, (0, N_CHIPS * per_chip - flat.shape[0])).reshape(N_CHIPS, per_chip // LANES, LANES)
    pack_sib = _run_plan("pack_pair_swap", _Swap([pack]))
    (scattered["pack"],) = _run_plan("pack_chip_scatter", _pair_partials("pack", [pack], pack_sib, [F32], core))
    order = ["w1_0", "w1_1", "w2_0", "w2_1", "lru_in", "lru_out", "fox_in", "fox_out", "pack"]
    red = dict(zip(order, _finish_reduce("grads", [scattered[n] for n in order])))
    (all_pack,) = _all_gather("gather_small_grads", [red["pack"]])
    all_flat = all_pack.reshape(-1)
    G = {}
    off = 0
    for n in pack_names:
        shape = grads[n].shape if n == "conv_w" else W[n].shape
        size = int(np.prod(shape))
        G[n] = all_flat[off:off + size].reshape(shape)
        off += size
    total = all_flat[off]
    G["lru_conv_w"] = lax.dynamic_slice_in_dim(G.pop("conv_w"), chip * DQ, DQ, axis=1)[None]
    parts = {n: [_as2d(G[n])] for n in G}
    parts.update(mlp_w1=[red["w1_0"], red["w1_1"]], mlp_w2=[red["w2_0"], red["w2_1"]], lru_w_in=[red["lru_in"]],
                 lru_w_out=[red["lru_out"]], fox_w_in=[red["fox_in"]], fox_w_out=[red["fox_out"]])

    delta, new_m, new_v = {}, {}, {}
    for n in _WEIGHTS:
        if W[n].shape[-1] % LANES and W[n].shape[-2] % LANES == 0:
            to_thin = lambda a: jnp.transpose(a, (2, 0, 1))
            res = _adamw(f"adamw_{n}", to_thin(W[n]), [to_thin(parts[n][0][None])], to_thin(Mo[n]), to_thin(Vo[n]))
            G[n], delta[n], new_m[n], new_v[n] = (jnp.transpose(t, (1, 2, 0)) for t in res)
            continue
        go, d, nm, nv = _adamw(f"adamw_{n}", _as2d(W[n]), parts[n], _as2d(Mo[n]), _as2d(Vo[n]))
        G[n], delta[n], new_m[n], new_v[n] = (t.reshape(W[n].shape) for t in (go, d, nm, nv))

    return (total, gx[None], *[G[n] for n in _WEIGHTS], *[delta[n] for n in _WEIGHTS],
            *[new_m[n] for n in _WEIGHTS], *[new_v[n] for n in _WEIGHTS])
```

```python
import functools

import numpy as np
import jax
import jax.numpy as jnp
from jax import lax
from jax.experimental import pallas as pl
from jax.experimental.pallas import tpu as pltpu

F32 = jnp.float32
BF16 = jnp.bfloat16

HEAD_DIM = 64
LRU_BLOCK_DIM = 64
CONV_WIDTH = 4
LRU_C = 8.0
EPS = 1e-6
NEG_INF = -1e30
ADAM_LR = 0.001
ADAM_B1 = 0.9
ADAM_B2 = 0.999
ADAM_EPS = 1e-08
ADAM_WD = 0.01
ADAM_STEP = 10

N_CHIPS = 4
LANES = 128
SUBLANES = 8
MXU_DIM = 256
VMEM_LIMIT = 52 * 1024 * 1024
MESH = pl.DeviceIdType.MESH
ANY = pl.BlockSpec(memory_space=pl.ANY)


def _pick(n, prefs):
    for p in prefs:
        if p <= n and n % p == 0:
            return p
    return n


def _params(sem=None):
    return pltpu.CompilerParams(dimension_semantics=sem, vmem_limit_bytes=VMEM_LIMIT)


class _View:
    def __init__(self, arr, kind="plain", r0=0, rows=None, shape=None, dtype=None):
        self.arr = arr
        self.kind = kind
        self.r0 = r0
        self.shape = tuple(arr.shape) if arr is not None else tuple(shape)
        self.dtype = arr.dtype if arr is not None else dtype
        self.rows = rows if rows is not None else self.shape[-2]

    def limits(self):
        if self.kind == "plain":
            return 0, 0
        rows = int(np.gcd(self.rows, self.r0))
        return rows, (self.shape[-1] if self.kind == "cs" else 0)

    def spec(self, br, bc, fr, fc):
        if self.kind == "plain":
            return pl.BlockSpec((br, bc), lambda *g: (fr(*g), fc(*g)))
        ncol = self.shape[-1]
        r0b = self.r0 // br
        assert self.r0 % br == 0 and self.rows % br == 0 and ncol % bc == 0, (self.shape, self.r0, br, bc)
        if self.kind == "cs":
            per = ncol // bc
            return pl.BlockSpec((None, br, bc), lambda *g: (fc(*g) // per, r0b + fr(*g), fc(*g) % per))
        per = self.rows // br
        return pl.BlockSpec((None, br, bc), lambda *g: (fr(*g) // per, r0b + fr(*g) % per, fc(*g)))


def _bf(x):
    return x if x.dtype == BF16 else x.astype(BF16)


def _matmul(name, A, B, M, N, K, *, ta=False, tb=False, outs, epilogue, extras=(), vecs=(), n_sums=0,
            tm=None, tn=None, tk=None, plan=None):
    lim = {"m": [M], "n": [N], "k": [K]}
    for view, (rdim, cdim) in ([(A, "km" if ta else "mk"), (B, "nk" if tb else "kn")]
                               + [(e, "mn") for e in extras] + [(o, "mn") for o in outs]):
        r_lim, c_lim = view.limits()
        lim[rdim].append(r_lim)
        lim[cdim].append(c_lim)
    tm = tm or _pick(int(np.gcd.reduce(lim["m"])), (1024, 640, 512, 256, 128))
    tn = tn or _pick(int(np.gcd.reduce(lim["n"])), (1024, 640, 512, 256, 128))
    tk = tk or _pick(int(np.gcd.reduce(lim["k"])), (1024, 640, 512, 256, 128))
    nk = K // tk
    gi = lambda i, j, k: i
    gj = lambda i, j, k: j
    gk = lambda i, j, k: k
    a_spec = A.spec(tk, tm, gk, gi) if ta else A.spec(tm, tk, gi, gk)
    b_spec = B.spec(tn, tk, gj, gk) if tb else B.spec(tk, tn, gk, gj)
    ca = 0 if ta else 1
    cb = 1 if tb else 0
    ne, no = len(extras) + len(vecs), len(outs)
    assert n_sums == 0 or tn == N
    row_spec = pl.BlockSpec((1, tn), lambda i, j, k: (0, j))
    in_specs = [a_spec, b_spec] + [e.spec(tm, tn, gi, gj) for e in extras] + [row_spec] * len(vecs)
    operands = [A.arr, B.arr] + [e.arr for e in extras] + list(vecs)
    out_specs = [o.spec(tm, tn, gi, gj) for o in outs] + [row_spec] * n_sums
    out_shape = ([jax.ShapeDtypeStruct(o.shape, o.dtype) for o in outs]
                 + [jax.ShapeDtypeStruct((1, N), F32)] * n_sums)

    def body(*refs):
        a_ref, b_ref = refs[0], refs[1]
        ex = refs[2:2 + ne]
        o_refs = refs[2 + ne:2 + ne + no]
        s_refs = refs[2 + ne + no:2 + ne + no + n_sums]
        first_row_tile = pl.program_id(0) == 0

        def prod():
            return lax.dot_general(_bf(a_ref[...]), _bf(b_ref[...]), (((ca,), (cb,)), ((), ())),
                                   preferred_element_type=F32)

        def finish(acc):
            res = epilogue(acc, *[e[...] for e in ex])
            for o_ref, r in zip(o_refs, res[:no]):
                o_ref[...] = r.astype(o_ref.dtype)
            for s_ref, r in zip(s_refs, res[no:]):
                def assign(s_ref=s_ref, r=r):
                    s_ref[...] = r

                def accumulate(s_ref=s_ref, r=r):
                    s_ref[...] += r

                pl.when(first_row_tile)(assign)
                pl.when(jnp.logical_not(first_row_tile))(accumulate)

        if nk == 1:
            finish(prod())
        else:
            acc_ref = refs[-1]
            k = pl.program_id(2)

            @pl.when(k == 0)
            def _():
                acc_ref[...] = jnp.zeros_like(acc_ref)

            acc_ref[...] += prod()

            @pl.when(k == nk - 1)
            def _():
                finish(acc_ref[...])

    res, side = _hosted_call(body, name, (M // tm, N // tn, nk), in_specs, out_specs, out_shape,
                             [pltpu.VMEM((tm, tn), F32)] if nk > 1 else [], operands,
                             ("arbitrary", "arbitrary", "arbitrary"), plan)
    return res if plan is None else (res, side)


def _ep_store(acc):
    return (acc,)


def _ep_resid(acc, res):
    return (res + acc,)


def _ep_resid_norm(acc, res, g):
    xo = res + acc
    r = lax.rsqrt(jnp.mean(xo * xo, axis=-1, keepdims=True) + EPS)
    return (xo, (xo * r) * g)


def _ep_norm_bwd(acc, x, dres, g):
    r = lax.rsqrt(jnp.mean(x * x, axis=-1, keepdims=True) + EPS)
    xhat = x * r
    dxn = acc * g
    tot = dres + r * (dxn - xhat * jnp.mean(dxn * xhat, axis=-1, keepdims=True))
    return (tot, tot, jnp.sum(acc * xhat, axis=0, keepdims=True))


def _ep_relu2(acc):
    zp = jnp.maximum(acc, 0.0)
    return (acc, zp * zp)


def _ep_drelu2(acc, z):
    return (acc * (2.0 * jnp.maximum(z.astype(F32), 0.0)),)


def _fresh(M, N, dtype):
    return _View(None, shape=(M, N), dtype=dtype)


def _rms_fwd(name, x, g, S, D, plan=None):
    T = _pick(S, (512, 256, 128))

    def body(x_ref, g_ref, h_ref):
        x = x_ref[...]
        r = lax.rsqrt(jnp.mean(x * x, axis=-1, keepdims=True) + EPS)
        h_ref[...] = ((x * r) * g_ref[...]).astype(BF16)

    return _hosted_call(body, name, (S // T,),
                        [pl.BlockSpec((T, D), lambda i: (i, 0)), pl.BlockSpec((1, D), lambda i: (0, 0))],
                        [pl.BlockSpec((T, D), lambda i: (i, 0))], [jax.ShapeDtypeStruct((S, D), BF16)], [], (x, g),
                        ("arbitrary",), plan)


def _loss_head(x, tgt, S, D):
    T = _pick(S, (512, 256, 128))

    def body(x_ref, t_ref, loss_ref, d_ref, db_ref):
        @pl.when(pl.program_id(0) == 0)
        def _():
            loss_ref[...] = jnp.zeros_like(loss_ref)

        e = x_ref[...] - t_ref[...]
        loss_ref[...] += 0.5 * jnp.sum(jnp.mean(e * e, axis=-1, keepdims=True), axis=0, keepdims=True)
        d = e * (1.0 / D)
        d_ref[...] = d
        db_ref[...] = d.astype(BF16)

    row = pl.BlockSpec((T, D), lambda i: (i, 0))
    return pl.pallas_call(
        body, name="loss_head", grid=(S // T,), in_specs=[row, row],
        out_specs=[pl.BlockSpec((1, 1), lambda i: (0, 0)), row, row],
        out_shape=[jax.ShapeDtypeStruct((1, 1), F32), jax.ShapeDtypeStruct((S, D), F32),
                   jax.ShapeDtypeStruct((S, D), BF16)],
        compiler_params=_params(("arbitrary",)),
    )(x, tgt)


def _sigmoid(z):
    return 1.0 / (1.0 + jnp.exp(-z))


def _log_sigmoid(z):
    return jnp.minimum(z, 0.0) - jnp.log(1.0 + jnp.exp(-jnp.abs(z)))


_GELU_K = 0.7978845608028654
_GELU_C = 0.044715


def _gelu(x):
    t = jnp.tanh(_GELU_K * (x + _GELU_C * (x * x * x)))
    return 0.5 * x * (1.0 + t)


def _gelu_and_grad(x):
    x2 = x * x
    t = jnp.tanh(_GELU_K * (x + _GELU_C * (x2 * x)))
    g = 0.5 * x * (1.0 + t)
    dg = 0.5 * (1.0 + t) + 0.5 * x * (1.0 - t * t) * (_GELU_K * (1.0 + 3.0 * _GELU_C * x2))
    return g, dg


def _decay_terms(r, ls):
    la = LRU_C * r * ls
    a = jnp.exp(la)
    a2 = a * a
    mult = jnp.sqrt(-jnp.tanh(la) * (a2 + 1.0))
    return a, a2, mult


def _lru_fwd(u0, conv_w, conv_b, wr_bd, b_r, wi_bd, b_i, lam, S, D, plan=None):
    T = _pick(S, (256, 128))
    GT = wr_bd.shape[-1]
    nG = D // GT

    def body(gb_ref, xb_ref, cw_ref, cb_ref, wr_ref, br_ref, wi_ref, bi_ref, lam_ref,
             y_ref, xc_ref, r_ref, i_ref, hs_ref, ext, a_scr, hcar):
        @pl.when(pl.program_id(0) == 0)
        def _():
            ext[0:SUBLANES, :] = jnp.zeros((SUBLANES, D), F32)
            hcar[...] = jnp.zeros_like(hcar)

        xb = xb_ref[...]
        ext[SUBLANES:SUBLANES + T, :] = xb
        xc = cb_ref[...]
        for k in range(CONV_WIDTH):
            xc = xc + ext[pl.ds(SUBLANES - (CONV_WIDTH - 1) + k, T), :] * cw_ref[k:k + 1, :]
        ext[0:SUBLANES, :] = xb[T - SUBLANES:T, :]
        xc_ref[...] = xc
        xcb = xc.astype(BF16)
        for g in range(nG):
            sl = slice(g * GT, (g + 1) * GT)
            zr = jnp.dot(xcb[:, sl], wr_ref[g], preferred_element_type=F32) + br_ref[:, sl]
            zi = jnp.dot(xcb[:, sl], wi_ref[g], preferred_element_type=F32) + bi_ref[:, sl]
            r_ref[:, sl] = _sigmoid(zr)
            i_ref[:, sl] = _sigmoid(zi)
        r = r_ref[...]
        a, _, mult = _decay_terms(r, _log_sigmoid(lam_ref[...]))
        a_scr[...] = a
        hs_ref[...] = mult * (i_ref[...] * xc)

        def step(t, h):
            h = a_scr[pl.ds(t, 1), :] * h + hs_ref[pl.ds(t, 1), :]
            hs_ref[pl.ds(t, 1), :] = h
            return h

        hcar[...] = lax.fori_loop(0, T, step, hcar[...], unroll=8)
        y_ref[...] = (_gelu(gb_ref[...]) * hs_ref[...]).astype(BF16)

    row = pl.BlockSpec((T, D), lambda i: (i, 0))
    vec = pl.BlockSpec((1, D), lambda i: (0, 0))
    bd = pl.BlockSpec((nG, GT, GT), lambda i: (0, 0, 0))
    f32o = jax.ShapeDtypeStruct((S, D), F32)
    return _hosted_call(
        body, "lru_fwd", (S // T,),
        [row, pl.BlockSpec((T, D), lambda i: (i, 1)), pl.BlockSpec((CONV_WIDTH, D), lambda i: (0, 0)), vec,
         bd, vec, bd, vec, vec],
        [row, row, row, row, row], [jax.ShapeDtypeStruct((S, D), BF16), f32o, f32o, f32o, f32o],
        [pltpu.VMEM((T + SUBLANES, D), F32), pltpu.VMEM((T, D), F32), pltpu.VMEM((1, D), F32)],
        (u0, u0, conv_w, conv_b, wr_bd, b_r, wi_bd, b_i, lam), ("arbitrary",), plan)


def _lru_bwd(dy, u0, xc, r, ig, hs, conv_w, wr_bd, wi_bd, lam, S, D, plan=None):
    T = _pick(S, (128,))
    nT = S // T
    GT = wr_bd.shape[-1]
    nG = D // GT
    W = CONV_WIDTH

    def body(dy_ref, gb_ref, xb_ref, xbp_ref, xc_ref, r_ref, i_ref, hs_ref, hsp_ref, cw_ref, wr_ref, wi_ref, lam_ref,
             du_ref, dcw_ref, dcb_ref, dlam_ref, dbr_ref, dbi_ref, dwr_ref, dwi_ref,
             a_scr, dh_scr, exth, extx, extd, dxc_scr, dz_scr, carry):
        step = pl.program_id(0)
        first_tile = step == nT - 1

        @pl.when(step == 0)
        def _():
            for ref in (dcw_ref, dcb_ref, dlam_ref, dbr_ref, dbi_ref, dwr_ref, dwi_ref, carry):
                ref[...] = jnp.zeros_like(ref)
            extd[T:T + SUBLANES, :] = jnp.zeros((SUBLANES, D), F32)

        hs = hs_ref[...]
        dy = dy_ref[...]
        g, dgelu = _gelu_and_grad(gb_ref[...])
        du_ref[:, 0:D] = (dy * hs * dgelu).astype(BF16)
        r = r_ref[...]
        lam = lam_ref[...]
        ls = _log_sigmoid(lam)
        a, a2, mult = _decay_terms(r, ls)
        a_scr[...] = a
        dh_scr[...] = dy * g

        def rstep(j, c):
            t = T - 1 - j
            d = dh_scr[pl.ds(t, 1), :] + c
            dh_scr[pl.ds(t, 1), :] = d
            return a_scr[pl.ds(t, 1), :] * d

        carry[...] = lax.fori_loop(0, T, rstep, carry[...], unroll=8)
        dh = dh_scr[...]
        keep = jnp.where(first_tile, 0.0, 1.0)
        exth[0:SUBLANES, :] = hsp_ref[...] * keep
        exth[SUBLANES:SUBLANES + T, :] = hs
        hprev = exth[pl.ds(SUBLANES - 1, T), :]
        xc = xc_ref[...]
        ig = i_ref[...]
        da = dh * hprev
        dmult = dh * (ig * xc)
        dla = da * a - dmult * (a2 / mult)
        dlam_ref[...] += jnp.sum(dla * r, axis=0, keepdims=True) * (LRU_C * _sigmoid(-lam))
        dzr = (dla * (LRU_C * ls)) * (r * (1.0 - r))
        dzi = (dh * (mult * xc)) * (ig * (1.0 - ig))
        dbr_ref[...] += jnp.sum(dzr, axis=0, keepdims=True)
        dbi_ref[...] += jnp.sum(dzi, axis=0, keepdims=True)
        dxc_scr[...] = dh * (mult * ig)
        xcb = xc.astype(BF16)
        dz_scr[0] = dzr.astype(BF16)
        dz_scr[1] = dzi.astype(BF16)
        nt_dims = (((1,), (1,)), ((), ()))
        tn_dims = (((0,), (0,)), ((), ()))
        for gq in range(nG):
            sl = slice(gq * GT, (gq + 1) * GT)
            zr_g = dz_scr[0, :, sl]
            zi_g = dz_scr[1, :, sl]
            dxc_scr[:, sl] += (lax.dot_general(zr_g, wr_ref[gq], nt_dims, preferred_element_type=F32)
                               + lax.dot_general(zi_g, wi_ref[gq], nt_dims, preferred_element_type=F32))
            dwr_ref[gq] += lax.dot_general(xcb[:, sl], zr_g, tn_dims, preferred_element_type=F32)
            dwi_ref[gq] += lax.dot_general(xcb[:, sl], zi_g, tn_dims, preferred_element_type=F32)
        dxc = dxc_scr[...]
        dcb_ref[...] += jnp.sum(dxc, axis=0, keepdims=True)
        extx[0:SUBLANES, :] = xbp_ref[...] * keep
        extx[SUBLANES:SUBLANES + T, :] = xb_ref[...]
        extd[0:T, :] = dxc
        dxb = jnp.zeros((T, D), F32)
        for k in range(W):
            dxb = dxb + extd[pl.ds(W - 1 - k, T), :] * cw_ref[k:k + 1, :]
            dcw_ref[k:k + 1, :] += jnp.sum(dxc * extx[pl.ds(SUBLANES - (W - 1) + k, T), :], axis=0, keepdims=True)
        extd[T:T + SUBLANES, :] = dxc[0:SUBLANES, :]
        du_ref[:, D:2 * D] = dxb.astype(BF16)

    rev = lambda i: nT - 1 - i
    tpb = T // SUBLANES
    prev8 = lambda i: jnp.maximum(rev(i) * tpb - 1, 0)
    row = pl.BlockSpec((T, D), lambda i: (rev(i), 0))
    vec = pl.BlockSpec((1, D), lambda i: (0, 0))
    bd = pl.BlockSpec((nG, GT, GT), lambda i: (0, 0, 0))
    vec_o = jax.ShapeDtypeStruct((1, D), F32)
    bd_o = jax.ShapeDtypeStruct((nG, GT, GT), F32)
    return _hosted_call(
        body, "lru_bwd", (nT,),
        [row, row, pl.BlockSpec((T, D), lambda i: (rev(i), 1)), pl.BlockSpec((SUBLANES, D), lambda i: (prev8(i), 1)),
         row, row, row, row, pl.BlockSpec((SUBLANES, D), lambda i: (prev8(i), 0)),
         pl.BlockSpec((W, D), lambda i: (0, 0)), bd, bd, vec],
        [pl.BlockSpec((T, 2 * D), lambda i: (rev(i), 0)), pl.BlockSpec((W, D), lambda i: (0, 0)),
         vec, vec, vec, vec, bd, bd],
        [jax.ShapeDtypeStruct((S, 2 * D), BF16), jax.ShapeDtypeStruct((W, D), F32), vec_o, vec_o, vec_o, vec_o, bd_o, bd_o],
        [pltpu.VMEM((T, D), F32), pltpu.VMEM((T, D), F32), pltpu.VMEM((T + SUBLANES, D), F32),
         pltpu.VMEM((T + SUBLANES, D), F32), pltpu.VMEM((T + SUBLANES, D), F32),
         pltpu.VMEM((T, D), F32), pltpu.VMEM((2, T, D), BF16), pltpu.VMEM((1, D), F32)],
        (dy, u0, u0, u0, xc, r, ig, hs, hs, conv_w, wr_bd, wi_bd, lam), ("arbitrary",), plan)


AUG_ROWS = 16
HEAD_ROWS = 128
LSE_ROW = HEAD_DIM + 6
ONES_ROW_Q = HEAD_DIM + 3
ONES_COL_K = HEAD_DIM
ONES_ROW_V = HEAD_DIM
HEAD_UNROLL = 4


def _split3(x):
    b1 = x.astype(BF16).astype(F32)
    r = x - b1
    b2 = r.astype(BF16).astype(F32)
    return b1, b2, r - b2


def _head_block(x, aug, T):
    row = lax.broadcasted_iota(jnp.int32, (AUG_ROWS, T), 0)
    blk = jnp.zeros((AUG_ROWS, T), F32)
    for i, e in enumerate(aug):
        blk = jnp.where(row == i, e, blk)
    return jnp.concatenate([x, blk, jnp.zeros((HEAD_ROWS - HEAD_DIM - AUG_ROWS, T), F32)], axis=0)


def _tri_matrix(lower):
    i = np.arange(LANES)
    m = (i[:, None] >= i[None, :]) if lower else (i[:, None] <= i[None, :])
    return jnp.asarray(m.astype(np.float32), BF16)


def _lane_cumsum(x, tri_ref, carry, reverse):
    n = x.shape[1] // LANES
    tri = tri_ref[...]
    out = [None] * n
    for j in (range(n - 1, -1, -1) if reverse else range(n)):
        cs = carry
        for part in _split3(x[:, j * LANES:(j + 1) * LANES]):
            cs = cs + jnp.dot(part.astype(BF16), tri, preferred_element_type=F32)
        out[j] = cs
        carry = cs[:, 0:1] if reverse else cs[:, LANES - 1:LANES]
    return jnp.concatenate(out, axis=1), carry


def _head_rows(h):
    return pl.ds(pl.multiple_of(h * HEAD_DIM, HEAD_DIM), HEAD_DIM)


def _fox_prep(ut, b_f, qg, kg, S, D, tq):
    H = D // HEAD_DIM
    T = min(tq, 256)
    per = tq // T
    scale = HEAD_DIM ** -0.5

    def body(q_ref, k_ref, v_ref, f_ref, bf_ref, qg_ref, kg_ref, tri_ref,
             qat_ref, kat_ref, vat_ref, ka_ref, c_scr, ccar):
        @pl.when(pl.program_id(0) == 0)
        def _():
            ccar[...] = jnp.zeros_like(ccar)

        c, carry = _lane_cumsum(_log_sigmoid(f_ref[...] + bf_ref[...]), tri_ref, ccar[...], False)
        c_scr[...] = c
        ccar[...] = carry

        def head(h, _):
            rows = _head_rows(h)
            c1, c2, c3 = _split3(c_scr[pl.ds(h, 1), :])

            def normed(src, gain, mul):
                x = src[rows, :]
                rs = lax.rsqrt(jnp.mean(x * x, axis=0, keepdims=True) + EPS)
                return ((x * rs) * gain[rows, :]) * mul

            qat_ref[h] = _head_block(normed(q_ref, qg_ref, scale), [c1, c2, c3, 1.0, 1.0, 1.0], T).astype(BF16)
            kb = _head_block(normed(k_ref, kg_ref, 1.0), [1.0, 1.0, 1.0, -c1, -c2, -c3, 1.0, 1.0, 1.0], T)
            kat_ref[h] = kb.astype(BF16)
            ka_ref[h] = kb.T.astype(BF16)
            vat_ref[h] = _head_block(v_ref[rows, :], [1.0, 1.0, 1.0], T).astype(BF16)
            return 0

        lax.fori_loop(0, H, head, 0, unroll=HEAD_UNROLL)

    part = lambda j: pl.BlockSpec((D, T), lambda i: (j, i))
    colv = lambda n: pl.BlockSpec((n, 1), lambda i: (0, 0))
    tmaj = lambda r: pl.BlockSpec((H, None, r, T), lambda i: (0, i // per, 0, i % per))
    norm = pl.BlockSpec((H, T, HEAD_ROWS), lambda i: (0, i, 0))
    tshape = lambda r: jax.ShapeDtypeStruct((H, S // tq, r, tq), BF16)
    nshape = jax.ShapeDtypeStruct((H, S, HEAD_ROWS), BF16)
    return pl.pallas_call(
        body, name="fox_prep", grid=(S // T,),
        in_specs=[part(0), part(1), part(2), pl.BlockSpec((LANES, T), lambda i: (3 * D // LANES, i)),
                  colv(LANES), colv(D), colv(D), pl.BlockSpec((LANES, LANES), lambda i: (0, 0))],
        out_specs=[tmaj(HEAD_ROWS), tmaj(HEAD_ROWS), tmaj(HEAD_ROWS), norm],
        out_shape=[tshape(HEAD_ROWS), tshape(HEAD_ROWS), tshape(HEAD_ROWS), nshape],
        scratch_shapes=[pltpu.VMEM((LANES, T), F32), pltpu.VMEM((LANES, 1), F32)],
        compiler_params=_params(("arbitrary",)),
    )(ut, ut, ut, ut, b_f, qg, kg, _tri_matrix(False))


def _fox_bwd_prep(dot, ot, lse, qat, S, D, tq, plan=None):
    H = D // HEAD_DIM
    T = min(tq, 256)
    per = tq // T

    def body(do_ref, o_ref, lse_ref, qat_ref, doat_ref, doa_ref, qat1_ref, qa1_ref):
        row = lax.broadcasted_iota(jnp.int32, (HEAD_ROWS, T), 0)

        def head(h, _):
            rows = _head_rows(h)
            do = do_ref[rows, :].astype(F32)
            delta = jnp.sum(do * o_ref[rows, :], axis=0, keepdims=True)
            db = _head_block(do, list(_split3(-delta)), T)
            doat_ref[h] = db.astype(BF16)
            doa_ref[h] = db.T.astype(BF16)
            qb = qat_ref[h].astype(F32)
            for i, e in enumerate(_split3(-lse_ref[h])):
                qb = jnp.where(row == LSE_ROW + i, e, qb)
            qat1_ref[h] = qb.astype(BF16)
            qa1_ref[h] = qb.T.astype(BF16)
            return 0

        lax.fori_loop(0, H, head, 0, unroll=HEAD_UNROLL)

    chan = pl.BlockSpec((D, T), lambda i: (0, i))
    tmaj = pl.BlockSpec((H, None, HEAD_ROWS, T), lambda i: (0, i // per, 0, i % per))
    norm = pl.BlockSpec((H, T, HEAD_ROWS), lambda i: (0, i, 0))
    tshape = jax.ShapeDtypeStruct((H, S // tq, HEAD_ROWS, tq), BF16)
    nshape = jax.ShapeDtypeStruct((H, S, HEAD_ROWS), BF16)
    return _hosted_call(body, "fox_bwd_prep", (S // T,), [chan, chan, pl.BlockSpec((H, 1, T), lambda i: (0, 0, i)), tmaj],
                        [tmaj, norm, tmaj, norm], [tshape, nshape, tshape, nshape], [], (dot, ot, lse, qat),
                        ("arbitrary",), plan)


def _causal(s, k_axis):
    ki = lax.broadcasted_iota(jnp.int32, s.shape, k_axis)
    qi = lax.broadcasted_iota(jnp.int32, s.shape, 1 - k_axis)
    return jnp.where(ki <= qi, s, NEG_INF)


def _seq_tile(i, t):
    return pl.ds(pl.multiple_of(i * t, t), t)


def _attn_forward(ka, qat, vat, S, D, tq, plan=None):
    H = D // HEAD_DIM
    nq = S // tq

    def body(ka_ref, qat_ref, vat_ref, o_ref, o32_ref, lse_ref, m_scr, acc_scr):
        qi = pl.program_id(1)
        m_scr[...] = jnp.full_like(m_scr, NEG_INF)
        acc_scr[...] = jnp.zeros_like(acc_scr)
        qa = qat_ref[...]

        def span(k0, n, diagonal):
            s = jnp.dot(ka_ref[pl.ds(pl.multiple_of(k0 * tq, tq), n * tq), :], qa, preferred_element_type=F32)
            if diagonal:
                s = _causal(s, 0)
            m_prev = m_scr[...]
            m_new = jnp.maximum(m_prev, jnp.max(s, axis=0, keepdims=True))
            p = jnp.exp(s - m_new).astype(BF16)
            upd = jnp.dot(vat_ref[k0], p[0:tq], preferred_element_type=F32)
            for i in range(1, n):
                upd = upd + jnp.dot(vat_ref[k0 + i], p[i * tq:(i + 1) * tq], preferred_element_type=F32)
            acc_scr[...] = jnp.exp(m_prev - m_new) * acc_scr[...] + upd
            m_scr[...] = m_new

        def off_diagonal_pair(j, _):
            span(2 * j, 2, False)
            return 0

        lax.fori_loop(0, qi // 2, off_diagonal_pair, 0)
        pl.when(qi % 2 == 1)(lambda: span(qi - 1, 1, False))
        span(qi, 1, True)
        l = acc_scr[ONES_ROW_V:ONES_ROW_V + 1, :]
        o = acc_scr[0:HEAD_DIM, :] / l
        o_ref[...] = o.astype(BF16)
        o32_ref[...] = o
        lse_ref[...] = m_scr[...] + jnp.log(l)

    chan = pl.BlockSpec((HEAD_DIM, tq), lambda h, i: (h, i))
    stat = pl.BlockSpec((None, 1, tq), lambda h, i: (h, 0, i))
    return _hosted_call(
        body, "attn_forward", (H, nq),
        [pl.BlockSpec((None, S, HEAD_ROWS), lambda h, i: (h, 0, 0)),
         pl.BlockSpec((None, None, HEAD_ROWS, tq), lambda h, i: (h, i, 0, 0)),
         pl.BlockSpec((None, nq, HEAD_ROWS, tq), lambda h, i: (h, 0, 0, 0))],
        [chan, chan, stat],
        [jax.ShapeDtypeStruct((D, S), BF16), jax.ShapeDtypeStruct((D, S), F32), jax.ShapeDtypeStruct((H, 1, S), F32)],
        [pltpu.VMEM((1, tq), F32), pltpu.VMEM((HEAD_ROWS, tq), F32)],
        (ka, qat, vat), ("arbitrary", "arbitrary"), plan)


def _attn_backward(qa, doa, qat, doat, ka, kat, vat, S, D, tq, plan=None):
    H = D // HEAD_DIM
    nq = S // tq

    def body(qa_ref, doa_ref, qat_ref, doat_ref, ka_ref, kat_ref, vat_ref, dq_ref, dk_ref, dv_ref, dk_scr, dv_scr):
        ki = pl.program_id(1)

        @pl.when(ki == 0)
        def _():
            dq_ref[...] = jnp.zeros_like(dq_ref)

        dk_scr[...] = jnp.zeros_like(dk_scr)
        dv_scr[...] = jnp.zeros_like(dv_scr)
        kt = kat_ref[...]
        vt = vat_ref[...]
        kn = ka_ref[...]

        def span(q0, n, diagonal):
            rows = pl.ds(pl.multiple_of(q0 * tq, tq), n * tq)
            s = jnp.dot(qa_ref[rows, :], kt, preferred_element_type=F32)
            if diagonal:
                s = _causal(s, 1)
            p = jnp.exp(s)
            ds = (p * jnp.dot(doa_ref[rows, :], vt, preferred_element_type=F32)).astype(BF16)
            p = p.astype(BF16)
            for i in range(n):
                part = slice(i * tq, (i + 1) * tq)
                dv_scr[...] += jnp.dot(doat_ref[q0 + i, 0:HEAD_DIM, :], p[part], preferred_element_type=F32)
                dk_scr[...] += jnp.dot(qat_ref[q0 + i], ds[part], preferred_element_type=F32)
            dq_ref[rows, :] += jnp.dot(ds, kn, preferred_element_type=F32)

        def off_diagonal_pair(j, _):
            span(ki + 1 + 2 * j, 2, False)
            return 0

        span(ki, 1, True)
        n_off = nq - 1 - ki
        lax.fori_loop(0, n_off // 2, off_diagonal_pair, 0)
        pl.when(n_off % 2 == 1)(lambda: span(nq - 1, 1, False))
        dk_ref[...] = dk_scr[...]
        dv_ref[...] = dv_scr[...].astype(BF16)

    whole = pl.BlockSpec((None, S, HEAD_ROWS), lambda h, i: (h, 0, 0))
    tiles = pl.BlockSpec((None, nq, HEAD_ROWS, tq), lambda h, i: (h, 0, 0, 0))
    one = pl.BlockSpec((None, None, HEAD_ROWS, tq), lambda h, i: (h, i, 0, 0))
    return _hosted_call(
        body, "attn_backward", (H, nq),
        [whole, whole, tiles, tiles, pl.BlockSpec((None, tq, HEAD_ROWS), lambda h, i: (h, i, 0)), one, one],
        [whole, pl.BlockSpec((None, HEAD_ROWS, tq), lambda h, i: (h, 0, i)),
         pl.BlockSpec((HEAD_DIM, tq), lambda h, i: (h, i))],
        [jax.ShapeDtypeStruct((H, S, HEAD_ROWS), F32), jax.ShapeDtypeStruct((H, HEAD_ROWS, S), F32),
         jax.ShapeDtypeStruct((D, S), BF16)],
        [pltpu.VMEM((HEAD_ROWS, tq), F32), pltpu.VMEM((HEAD_DIM, tq), F32)],
        (qa, doa, qat, doat, ka, kat, vat), ("arbitrary", "arbitrary"), plan)


def _fox_prep_bwd(ut, dq, dkt, dvt, b_f, qg, kg, S, D, tq):
    H = D // HEAD_DIM
    T = min(tq, 256)
    nT = S // T
    NU = 3 * D + LANES
    scale = HEAD_DIM ** -0.5

    def body(q_ref, k_ref, f_ref, dq_ref, dk_ref, dv_ref, bf_ref, qg_ref, kg_ref, tri_ref,
             du_ref, dbf_ref, dqg_ref, dkg_ref, gq_acc, gk_acc, fcar, dc_scr):
        step = pl.program_id(0)

        @pl.when(step == 0)
        def _():
            for ref in (gq_acc, gk_acc, fcar, dbf_ref):
                ref[...] = jnp.zeros_like(ref)

        dc_scr[...] = jnp.zeros_like(dc_scr)

        def head(h, _):
            rows = _head_rows(h)
            dqb = dq_ref[h].T
            dkb = dk_ref[h]
            dc_scr[pl.ds(h, 1), :] = dqb[ONES_COL_K:ONES_COL_K + 1, :] - dkb[ONES_ROW_Q:ONES_ROW_Q + 1, :]
            for src, dsrc, gain, acc, mul, base in ((q_ref, dqb, qg_ref, gq_acc, scale, 0),
                                                    (k_ref, dkb, kg_ref, gk_acc, 1.0, D)):
                x = src[rows, :]
                rs = lax.rsqrt(jnp.mean(x * x, axis=0, keepdims=True) + EPS)
                xhat = x * rs
                dn = dsrc[0:HEAD_DIM, :] * mul
                acc[rows, :] += jnp.sum(dn * xhat, axis=1, keepdims=True)
                dxh = dn * gain[rows, :]
                dx = rs * (dxh - xhat * jnp.mean(dxh * xhat, axis=0, keepdims=True))
                du_ref[pl.ds(pl.multiple_of(base + h * HEAD_DIM, HEAD_DIM), HEAD_DIM), :] = dx.astype(BF16)
            return 0

        lax.fori_loop(0, H, head, 0, unroll=HEAD_UNROLL)
        du_ref[2 * D:3 * D, :] = dv_ref[...]
        dlf, carry = _lane_cumsum(dc_scr[...], tri_ref, fcar[...], True)
        fcar[...] = carry
        dfl = dlf * _sigmoid(-(f_ref[...] + bf_ref[...]))
        dbf_ref[...] += jnp.sum(dfl, axis=1, keepdims=True)
        du_ref[3 * D:NU, :] = dfl.astype(BF16)

        @pl.when(step == nT - 1)
        def _():
            for acc, ref in ((gq_acc, dqg_ref), (gk_acc, dkg_ref)):
                tot = jnp.zeros((HEAD_DIM, 1), F32)
                for h in range(H):
                    tot = tot + acc[h * HEAD_DIM:(h + 1) * HEAD_DIM, :]
                ref[...] = tot

    rev = lambda i: nT - 1 - i
    part = lambda j: pl.BlockSpec((D, T), lambda i: (j, rev(i)))
    colv = lambda n: pl.BlockSpec((n, 1), lambda i: (0, 0))
    return pl.pallas_call(
        body, name="fox_prep_bwd", grid=(nT,),
        in_specs=[part(0), part(1), pl.BlockSpec((LANES, T), lambda i: (3 * D // LANES, rev(i))),
                  pl.BlockSpec((H, T, HEAD_ROWS), lambda i: (0, rev(i), 0)),
                  pl.BlockSpec((H, HEAD_ROWS, T), lambda i: (0, 0, rev(i))), pl.BlockSpec((D, T), lambda i: (0, rev(i))),
                  colv(LANES), colv(D), colv(D), pl.BlockSpec((LANES, LANES), lambda i: (0, 0))],
        out_specs=[pl.BlockSpec((NU, T), lambda i: (0, rev(i))), colv(LANES), colv(HEAD_DIM), colv(HEAD_DIM)],
        out_shape=[jax.ShapeDtypeStruct((NU, S), BF16), jax.ShapeDtypeStruct((LANES, 1), F32),
                   jax.ShapeDtypeStruct((HEAD_DIM, 1), F32), jax.ShapeDtypeStruct((HEAD_DIM, 1), F32)],
        scratch_shapes=[pltpu.VMEM((D, 1), F32), pltpu.VMEM((D, 1), F32), pltpu.VMEM((LANES, 1), F32),
                        pltpu.VMEM((LANES, T), F32)],
        compiler_params=_params(("arbitrary",)),
    )(ut, ut, ut, dq, dkt, dvt, b_f, qg, kg, _tri_matrix(True))


def _block_diag_tiles(w):
    n = w.shape[0]
    per = min(MXU_DIM, n * LRU_BLOCK_DIM) // LRU_BLOCK_DIM
    eye = jnp.eye(per, dtype=w.dtype)
    w5 = w.reshape(n // per, per, LRU_BLOCK_DIM, 1, LRU_BLOCK_DIM) * eye[None, :, None, :, None]
    return w5.reshape(n // per, per * LRU_BLOCK_DIM, per * LRU_BLOCK_DIM).astype(BF16)


def _block_diag_extract(t, n):
    per = t.shape[-1] // LRU_BLOCK_DIM
    eye = jnp.eye(per, dtype=t.dtype)
    t5 = t.reshape(n // per, per, LRU_BLOCK_DIM, per, LRU_BLOCK_DIM) * eye[None, :, None, :, None]
    return t5.sum(axis=3).reshape(n, LRU_BLOCK_DIM, LRU_BLOCK_DIM)


def _local_step(x, tgt, small, wv, grad_view, comm=None):
    S, D = x.shape
    F = 4 * D
    H = D // HEAD_DIM
    nblk = D // LRU_BLOCK_DIM
    NU = 3 * D + LANES
    tq = max(LANES, min(512, S // 4))
    assert S % tq == 0
    vec = lambda a: a.reshape(1, -1).astype(F32)
    col = lambda a: a.reshape(-1, 1).astype(F32)
    mix_g, mlp_g = small["mix_norm"], small["mlp_norm"]
    conv_b = vec(small["lru_conv_b"])
    wr_bd, wi_bd = _block_diag_tiles(small["lru_w_r"][0]), _block_diag_tiles(small["lru_w_i"][0])
    b_r, b_i, lam = vec(small["lru_b_r"]), vec(small["lru_b_i"]), vec(small["lru_lambda"])
    b_f = jnp.pad(col(small["fox_b_f"]), ((0, LANES - H), (0, 0)))
    qg, kg = jnp.tile(col(small["fox_q_gain"]), (H, 1)), jnp.tile(col(small["fox_k_gain"]), (H, 1))
    X = lambda a: _View(a)
    grads = {}
    gout = functools.partial(grad_view, grads)

    def hosted(name, fn, *args):
        plan = comm.before(name, grads) if comm is not None else None
        res, side = fn(*args, plan=plan)
        if plan is not None:
            comm.after(name, side, wv)
        return res

    def hosted_mm(name, *args, **kw):
        plan = comm.before(name, grads) if comm is not None else None
        if plan is None:
            return _matmul(name, *args, **kw)
        res, side = _matmul(name, *args, plan=plan, **kw)
        comm.after(name, side, wv)
        return res

    norm_rows = _pick(S, (512, 256, 128))
    two = lambda: [_fresh(S, D, F32), _fresh(S, D, BF16)]

    def mlp_up(l, hm):
        return hosted_mm(f"mlp{l}_up", X(hm), wv[f"w1_{l}"], S, F, D, outs=[_fresh(S, F, BF16), _fresh(S, F, BF16)],
                         epilogue=_ep_relu2)

    def mlp_bwd(l, xin, hm, z, act, d, db):
        (dz,) = hosted_mm(f"mlp{l}_dact", X(db), wv[f"w2_{l}"], S, F, D, tb=True, outs=[_fresh(S, F, BF16)],
                          epilogue=_ep_drelu2, extras=[X(z)])
        (grads[f"w2_{l}"],) = _matmul(f"mlp{l}_dw2", X(act), X(db), F, D, S, ta=True, outs=[gout(f"w2_{l}")],
                                      epilogue=_ep_store)
        (grads[f"w1_{l}"],) = _matmul(f"mlp{l}_dw1", X(hm), X(dz), D, F, S, ta=True, outs=[gout(f"w1_{l}")],
                                      epilogue=_ep_store)
        return _matmul(f"mlp{l}_dhm", X(dz), wv[f"w1_{l}"], S, D, F, tb=True, outs=two(), n_sums=1,
                       epilogue=_ep_norm_bwd, extras=[X(xin), X(d)], vecs=[mlp_g[l:l + 1]], tm=norm_rows)

    (h0,) = hosted("mix0_norm", _rms_fwd, "mix0_norm", x, mix_g[0:1], S, D)
    (u0,) = hosted_mm("lru_in", X(h0), wv["lru_in"], S, 2 * D, D, outs=[_fresh(S, 2 * D, F32)], epilogue=_ep_store)
    conv_w = small["conv_w"]
    y, xc, r, ig, hs = hosted("lru_fwd", _lru_fwd, u0, conv_w, conv_b, wr_bd, b_r, wi_bd, b_i, lam, S, D)
    x1, hm0 = _matmul("lru_out", X(y), wv["lru_out"], S, D, D, outs=two(), epilogue=_ep_resid_norm, extras=[X(x)],
                      vecs=[mlp_g[0:1]], tm=norm_rows)
    z0, act0 = mlp_up(0, hm0)
    x2, h1 = hosted_mm("mlp0_down", X(act0), wv["w2_0"], S, D, F, outs=two(), epilogue=_ep_resid_norm, extras=[X(x1)],
                       vecs=[mix_g[1:2]], tm=norm_rows)
    (u1,) = _matmul("fox_in", wv["fox_in"], X(h1), NU, S, D, tb=True, outs=[_fresh(NU, S, F32)], epilogue=_ep_store)
    qat, kat, vat, ka = _fox_prep(u1, b_f, qg, kg, S, D, tq)
    o, o32, lse = hosted("attn_forward", _attn_forward, ka, qat, vat, S, D, tq)
    x3, hm1 = _matmul("fox_out", X(o), wv["fox_out"], S, D, D, ta=True, outs=two(), epilogue=_ep_resid_norm,
                      extras=[X(x2)], vecs=[mlp_g[1:2]], tm=norm_rows)
    z1, act1 = mlp_up(1, hm1)
    (x4,) = _matmul("mlp1_down", X(act1), wv["w2_1"], S, D, F, outs=[_fresh(S, D, F32)], epilogue=_ep_resid,
                    extras=[X(x3)])
    loss, d4, d4b = _loss_head(x4, tgt, S, D)

    d3, d3b, dg_mlp1 = mlp_bwd(1, x3, hm1, z1, act1, d4, d4b)
    (do,) = _matmul("fox_dout", wv["fox_out"], X(d3b), D, S, D, tb=True, outs=[_fresh(D, S, BF16)], epilogue=_ep_store)
    (grads["fox_out"],) = _matmul("fox_dwout", X(o), X(d3b), D, D, S, outs=[gout("fox_out")], epilogue=_ep_store)
    doat, doa, qat1, qa1 = hosted("fox_bwd_prep", _fox_bwd_prep, do, o32, lse, qat, S, D, tq)
    dqn, dkn, dv = hosted("attn_backward", _attn_backward, qa1, doa, qat1, doat, ka, kat, vat, S, D, tq)
    du1, dbf, dqg, dkg = _fox_prep_bwd(u1, dqn, dkn, dv, b_f, qg, kg, S, D, tq)
    (grads["fox_in"],) = _matmul("fox_dwin", X(h1), X(du1), D, NU, S, ta=True, tb=True, outs=[gout("fox_in")],
                                 epilogue=_ep_store)
    d2, d2b, dg_mix1 = hosted_mm("fox_dh", X(du1), wv["fox_in"], S, D, NU, ta=True, outs=two(), n_sums=1,
                               epilogue=_ep_norm_bwd, extras=[X(x2), X(d3)], vecs=[mix_g[1:2]], tm=norm_rows)
    d1, d1b, dg_mlp0 = mlp_bwd(0, x1, hm0, z0, act0, d2, d2b)
    (grads["lru_out"],) = _matmul("lru_dwout", X(y), X(d1b), D, D, S, ta=True, outs=[gout("lru_out")],
                                  epilogue=_ep_store)
    (dy,) = hosted_mm("lru_dout", X(d1b), wv["lru_out"], S, D, D, tb=True, outs=[_fresh(S, D, F32)],
                      epilogue=_ep_store)
    du0, dcw, dcb, dlam, dbr, dbi, dwr, dwi = hosted("lru_bwd", _lru_bwd, dy, u0, xc, r, ig, hs, conv_w, wr_bd, wi_bd,
                                                     lam, S, D)
    (grads["lru_in"],) = _matmul("lru_dwin", X(h0), X(du0), D, 2 * D, S, ta=True, outs=[gout("lru_in")],
                                 epilogue=_ep_store)
    gx, dg_mix0 = hosted_mm("lru_dh", X(du0), wv["lru_in"], S, D, 2 * D, tb=True, outs=[_fresh(S, D, F32)], n_sums=1,
                            epilogue=lambda *a: _ep_norm_bwd(*a)[::2], extras=[X(x), X(d1)], vecs=[mix_g[0:1]],
                            tm=norm_rows)

    grads.update(
        mix_norm=jnp.concatenate([dg_mix0, dg_mix1], axis=0), mlp_norm=jnp.concatenate([dg_mlp0, dg_mlp1], axis=0),
        conv_w=dcw, lru_conv_b=dcb, lru_w_r=_block_diag_extract(dwr, nblk)[None], lru_b_r=dbr.reshape(1, nblk, -1),
        lru_w_i=_block_diag_extract(dwi, nblk)[None], lru_b_i=dbi.reshape(1, nblk, -1), lru_lambda=dlam,
        fox_b_f=dbf[:H].reshape(1, H), fox_q_gain=dqg.reshape(1, -1), fox_k_gain=dkg.reshape(1, -1))
    return loss, gx, grads


def _place():
    x, y, c = lax.axis_index("x"), lax.axis_index("y"), lax.axis_index("c")
    chips = [(1 - x, y), (x, 1 - y), (1 - x, 1 - y)]
    return x, y, c, 2 * x + y, chips


BOUNCE_BYTES = 1 << 20


def _bounce_shape(rows, cols, dtype):
    chunk = rows
    while chunk % 2 == 0 and chunk > 16 and chunk * cols * jnp.dtype(dtype).itemsize > BOUNCE_BYTES:
        chunk //= 2
    return pltpu.VMEM((2, chunk, cols), dtype)


def _bounce_copy(src, dst, buf, sem):
    chunk = buf.shape[1]
    n = src.shape[0] // chunk
    cin = lambda i: pltpu.make_async_copy(src.at[pl.ds(i * chunk, chunk)], buf.at[i % 2], sem.at[i % 2])
    cout = lambda i: pltpu.make_async_copy(buf.at[i % 2], dst.at[pl.ds(i * chunk, chunk)], sem.at[2 + i % 2])
    cin(0).start()
    for i in range(n):
        cin(i).wait()
        if i + 1 < n:
            if i >= 1:
                cout(i - 1).wait()
            cin(i + 1).start()
        cout(i).start()
    if n >= 2:
        cout(n - 2).wait()
    cout(n - 1).wait()


def _hbm_call(body, name, arrays, out_shape, n_dma_sems, bounce=()):
    scratch = [pltpu.SemaphoreType.DMA((k,)) for k in n_dma_sems]
    for rows, cols, dtype in bounce:
        scratch += [_bounce_shape(rows, cols, dtype), pltpu.SemaphoreType.DMA((4,))]
    return pl.pallas_call(
        body, name=name, in_specs=[ANY] * len(arrays), out_specs=[ANY] * len(out_shape), out_shape=out_shape,
        scratch_shapes=scratch,
        compiler_params=pltpu.CompilerParams(has_side_effects=True, vmem_limit_bytes=VMEM_LIMIT),
    )(*arrays)


class _Gather:
    def __init__(self, shards):
        n = self.n = len(shards)
        self.operands = list(shards)
        self.out_shape = [jax.ShapeDtypeStruct((N_CHIPS,) + tuple(a.shape), a.dtype) for a in shards]
        self.scratch = [pltpu.SemaphoreType.DMA((3 * n,)) for _ in range(4)]
        for a in shards:
            self.scratch += [_bounce_shape(a.shape[0], a.shape[1], a.dtype), pltpu.SemaphoreType.DMA((4,))]

    def _copies(self, ins, outs, scr):
        send, recv, fsend, frecv = scr[:4]
        x, y, c, s, chips = _place()

        def rows(a, chip_idx, which):
            hr = ins[a].shape[0] // 2
            return outs[a].at[chip_idx, pl.ds(which * hr, hr)]

        def landed(a, j, core):
            return rows(a, 2 * chips[j][0] + chips[j][1], core)

        def ici(a, j, mine):
            hr = ins[a].shape[0] // 2
            src, dst = (ins[a].at[pl.ds(c * hr, hr)], rows(a, s, c)) if mine else (landed(a, j, c),) * 2
            return pltpu.make_async_remote_copy(src_ref=src, dst_ref=dst, send_sem=send.at[3 * a + j],
                                                recv_sem=recv.at[3 * a + j], device_id=(*chips[j], c),
                                                device_id_type=MESH)

        def d2d(a, j, mine):
            ref = landed(a, j, c if mine else 1 - c)
            return pltpu.make_async_remote_copy(src_ref=ref, dst_ref=ref, send_sem=fsend.at[3 * a + j],
                                                recv_sem=frecv.at[3 * a + j], device_id=(x, y, 1 - c),
                                                device_id_type=MESH)

        return ici, d2d, s

    def start(self, ins, outs, scr):
        ici, _, _ = self._copies(ins, outs, scr)
        for a in range(self.n):
            for j in range(3):
                ici(a, j, True).start()

    def middle(self, ins, outs, scr):
        ici, d2d, s = self._copies(ins, outs, scr)
        for a in range(self.n):
            _bounce_copy(ins[a], outs[a].at[s], scr[4 + 2 * a], scr[5 + 2 * a])
        for a in range(self.n):
            for j in range(3):
                ici(a, j, False).wait_recv()
                d2d(a, j, True).start()

    def finish(self, ins, outs, scr):
        ici, d2d, _ = self._copies(ins, outs, scr)
        for a in range(self.n):
            for j in range(3):
                d2d(a, j, False).wait_recv()
        for a in range(self.n):
            for j in range(3):
                ici(a, j, True).wait_send()
                d2d(a, j, True).wait_send()


def _run_plan(name, plan):
    k_in, k_out = len(plan.operands), len(plan.out_shape)

    def body(*refs):
        parts = (refs[:k_in], refs[k_in:k_in + k_out], refs[k_in + k_out:])
        plan.start(*parts)
        plan.middle(*parts)
        plan.finish(*parts)

    return pl.pallas_call(
        body, name=name, in_specs=[ANY] * k_in, out_specs=[ANY] * k_out, out_shape=plan.out_shape,
        scratch_shapes=plan.scratch,
        compiler_params=pltpu.CompilerParams(has_side_effects=True, vmem_limit_bytes=VMEM_LIMIT),
    )(*plan.operands)


def _hosted_call(body, name, grid, in_specs, out_specs, out_shape, scratch_shapes, operands, sem, plan=None):
    if plan is None:
        res = pl.pallas_call(body, name=name, grid=grid, in_specs=in_specs, out_specs=out_specs, out_shape=out_shape,
                             scratch_shapes=scratch_shapes, compiler_params=_params(sem))(*operands)
        return res, None
    n_in, n_out, n_scr = len(in_specs), len(out_specs), len(scratch_shapes)
    k_in, k_out = len(plan.operands), len(plan.out_shape)
    total = int(np.prod(grid))
    late = max(0, total - 1 - max(1, total // 8))

    def hosted(*refs):
        ins, refs = refs[:n_in], refs[n_in:]
        p_ins, refs = refs[:k_in], refs[k_in:]
        outs, refs = refs[:n_out], refs[n_out:]
        p_outs, refs = refs[:k_out], refs[k_out:]
        scr, p_scr = refs[:n_scr], refs[n_scr:]
        step = pl.program_id(0)
        for d in range(1, len(grid)):
            step = step * grid[d] + pl.program_id(d)
        pl.when(step == 0)(lambda: plan.start(p_ins, p_outs, p_scr))
        body(*ins, *outs, *scr)
        pl.when(step == late)(lambda: plan.middle(p_ins, p_outs, p_scr))
        pl.when(step == total - 1)(lambda: plan.finish(p_ins, p_outs, p_scr))

    res = pl.pallas_call(
        hosted, name=name, grid=grid, in_specs=list(in_specs) + [ANY] * k_in, out_specs=list(out_specs) + [ANY] * k_out,
        out_shape=list(out_shape) + plan.out_shape, scratch_shapes=list(scratch_shapes) + plan.scratch,
        compiler_params=pltpu.CompilerParams(dimension_semantics=sem, vmem_limit_bytes=VMEM_LIMIT,
                                             has_side_effects=True),
    )(*operands, *plan.operands)
    return res[:n_out], res[n_out:]


def _all_gather(name, shards):
    return _run_plan(name, _Gather(shards))


class _Swap:
    def __init__(self, arrs):
        self.n = len(arrs)
        self.operands = list(arrs)
        self.out_shape = [jax.ShapeDtypeStruct((a.shape[0], a.shape[1] // 2, a.shape[2]), a.dtype) for a in arrs]
        self.scratch = [pltpu.SemaphoreType.DMA((self.n,)) for _ in range(2)]

    def _copy(self, ins, outs, scr, a):
        x, y, c, _, _ = _place()
        hr = ins[a].shape[1] // 2
        return pltpu.make_async_remote_copy(
            src_ref=ins[a].at[:, pl.ds((1 - c) * hr, hr)], dst_ref=outs[a], send_sem=scr[0].at[a],
            recv_sem=scr[1].at[a], device_id=(x, y, 1 - c), device_id_type=MESH)

    def start(self, ins, outs, scr):
        for a in range(self.n):
            self._copy(ins, outs, scr, a).start()

    def middle(self, ins, outs, scr):
        pass

    def finish(self, ins, outs, scr):
        for a in range(self.n):
            self._copy(ins, outs, scr, a).wait()


class _Scatter:
    def __init__(self, parts):
        n = self.n = len(parts)
        self.operands = list(parts)
        self.out_shape = [jax.ShapeDtypeStruct(a.shape, a.dtype) for a in parts]
        self.scratch = [pltpu.SemaphoreType.DMA((3 * n,)) for _ in range(2)]
        for a in parts:
            self.scratch += [_bounce_shape(a.shape[1], a.shape[2], a.dtype), pltpu.SemaphoreType.DMA((4,))]

    def _copy(self, ins, outs, scr, a, j, mine):
        x, y, c, s, chips = _place()
        t = 2 * chips[j][0] + chips[j][1]
        return pltpu.make_async_remote_copy(
            src_ref=ins[a].at[t], dst_ref=outs[a].at[s if mine else t], send_sem=scr[0].at[3 * a + j],
            recv_sem=scr[1].at[3 * a + j], device_id=(*chips[j], c), device_id_type=MESH)

    def start(self, ins, outs, scr):
        for a in range(self.n):
            for j in range(3):
                self._copy(ins, outs, scr, a, j, True).start()

    def middle(self, ins, outs, scr):
        s = _place()[3]
        for a in range(self.n):
            _bounce_copy(ins[a].at[s], outs[a].at[s], scr[2 + 2 * a], scr[3 + 2 * a])

    def finish(self, ins, outs, scr):
        for a in range(self.n):
            for j in range(3):
                self._copy(ins, outs, scr, a, j, False).wait_recv()
        for a in range(self.n):
            for j in range(3):
                self._copy(ins, outs, scr, a, j, True).wait_send()


def _pair_gather(name, halves):
    n = len(halves)

    def body(*refs):
        ins, outs = refs[:n], refs[n:2 * n]
        send, recv = refs[2 * n:2 * n + 2]
        stage = refs[2 * n + 2:]
        x, y, c, _, _ = _place()
        cps = []
        for a in range(n):
            hr = ins[a].shape[0]
            cp = pltpu.make_async_remote_copy(
                src_ref=ins[a], dst_ref=outs[a].at[pl.ds(c * hr, hr)], send_sem=send.at[a], recv_sem=recv.at[a],
                device_id=(x, y, 1 - c), device_id_type=MESH)
            cp.start()
            cps.append((cp, hr))
        for a, (cp, hr) in enumerate(cps):
            _bounce_copy(ins[a], outs[a].at[pl.ds(c * hr, hr)], stage[2 * a], stage[2 * a + 1])
        for a, (cp, hr) in enumerate(cps):
            cp.wait_send()
            theirs = outs[a].at[pl.ds((1 - c) * hr, hr)]
            pltpu.make_async_remote_copy(src_ref=theirs, dst_ref=theirs, send_sem=send.at[a], recv_sem=recv.at[a],
                                         device_id=(x, y, 1 - c), device_id_type=MESH).wait_recv()

    out_shape = [jax.ShapeDtypeStruct((2 * a.shape[0], a.shape[1]), a.dtype) for a in halves]
    return _hbm_call(body, name, halves, out_shape, (n, n),
                     bounce=[(a.shape[0], a.shape[1], a.dtype) for a in halves])


def _row_tile(rows, cols, itemsize, n_bufs):
    budget = VMEM_LIMIT // 2
    for t in (1024, 512, 256, 128, 64, 32, 16):
        if rows % t == 0 and 2 * n_bufs * t * cols * itemsize <= budget:
            return t
    return rows


def _pair_add(name, g, gsib, core, out_dtype):
    _, r, cols = g.shape
    hr = r // 2
    t = _row_tile(hr, cols, 4, 3)
    per = hr // t

    def body(core_ref, a_ref, b_ref, o_ref):
        o_ref[...] = (a_ref[...].astype(F32) + b_ref[...].astype(F32)).astype(o_ref.dtype)

    grid_spec = pltpu.PrefetchScalarGridSpec(
        num_scalar_prefetch=1, grid=(N_CHIPS, per),
        in_specs=[pl.BlockSpec((None, t, cols), lambda s, i, core: (s, core[0] * per + i, 0)),
                  pl.BlockSpec((None, t, cols), lambda s, i, core: (s, i, 0))],
        out_specs=pl.BlockSpec((None, t, cols), lambda s, i, core: (s, i, 0)))
    return pl.pallas_call(body, name=name, grid_spec=grid_spec,
                          out_shape=jax.ShapeDtypeStruct((N_CHIPS, hr, cols), out_dtype),
                          compiler_params=_params(("arbitrary", "arbitrary")))(core, g, gsib)


def _chip_sum(name, parts):
    _, hr, cols = parts.shape
    t = _row_tile(hr, cols, 4, 5)

    def body(p_ref, o_ref):
        o_ref[...] = ((p_ref[0].astype(F32) + p_ref[1].astype(F32)) + p_ref[2].astype(F32)) + p_ref[3].astype(F32)

    return pl.pallas_call(
        body, name=name, grid=(hr // t,), in_specs=[pl.BlockSpec((N_CHIPS, t, cols), lambda i: (0, i, 0))],
        out_specs=pl.BlockSpec((t, cols), lambda i: (i, 0)), out_shape=jax.ShapeDtypeStruct((hr, cols), F32),
        compiler_params=_params(("arbitrary",)))(parts)


def _pair_partials(tag, arrs, sib, wire_dtypes, core):
    return _Scatter([_pair_add(f"{tag}_pair_add{i}", g, gs, core, dt)
                     for i, (g, gs, dt) in enumerate(zip(arrs, sib, wire_dtypes))])


def _finish_reduce(tag, scattered):
    halves = [_chip_sum(f"{tag}_chip_sum{i}", p) for i, p in enumerate(scattered)]
    return _pair_gather(f"{tag}_pair_gather", halves)


def _adamw(name, w, g_parts, m, v):
    thin = w.ndim == 3
    rows, cols = w.shape[0], w.shape[-1]
    n_parts = len(g_parts)
    part_rows = rows // n_parts
    t = max(d for d in range(1, 257) if part_rows % d == 0) if thin else _row_tile(part_rows, cols, 4, 7 + n_parts)
    per = part_rows // t
    c1 = 1.0 - ADAM_B1 ** ADAM_STEP
    c2 = 1.0 - ADAM_B2 ** ADAM_STEP

    def body(w_ref, m_ref, v_ref, *refs):
        g_refs, (go_ref, d_ref, nm_ref, nv_ref) = refs[:n_parts], refs[n_parts:]
        g = g_refs[0][...]
        for k in range(1, n_parts):
            g = jnp.where(pl.program_id(0) >= k * per, g_refs[k][...], g)
        go_ref[...] = g
        m = ADAM_B1 * m_ref[...] + (1.0 - ADAM_B1) * g
        v = ADAM_B2 * v_ref[...] + (1.0 - ADAM_B2) * (g * g)
        nm_ref[...] = m
        nv_ref[...] = v
        d_ref[...] = -ADAM_LR * ((m / c1) / (jnp.sqrt(v / c2) + ADAM_EPS) + ADAM_WD * w_ref[...])

    block = (t, 1, cols) if thin else (t, cols)
    at = lambda r: (r, 0, 0) if thin else (r, 0)
    spec = pl.BlockSpec(block, lambda i: at(i))
    g_specs = [pl.BlockSpec(block, lambda i, k=k: at(jnp.clip(i - k * per, 0, per - 1))) for k in range(n_parts)]
    shp = jax.ShapeDtypeStruct(w.shape, F32)
    return pl.pallas_call(body, name=name, grid=(rows // t,), in_specs=[spec] * 3 + g_specs, out_specs=[spec] * 4,
                          out_shape=[shp] * 4, compiler_params=_params(("arbitrary",)))(w, m, v, *g_parts)


_WEIGHTS = ["mix_norm", "mlp_norm", "mlp_w1", "mlp_w2", "lru_w_in", "lru_conv_w", "lru_conv_b", "lru_w_r", "lru_b_r",
            "lru_w_i", "lru_b_i", "lru_lambda", "lru_w_out", "fox_w_in", "fox_b_f", "fox_q_gain", "fox_k_gain",
            "fox_w_out"]
_REPLICATED = ["mix_norm", "mlp_norm", "lru_conv_b", "lru_w_r", "lru_b_r", "lru_w_i", "lru_b_i", "lru_lambda",
               "fox_b_f", "fox_q_gain", "fox_k_gain"]
_PACK_TILE = 2 * SUBLANES * LANES


def _as2d(a):
    return a.reshape(-1, a.shape[-1])


def kernel(x, mix_norm, mlp_norm, mlp_w1, mlp_w2, lru_w_in, lru_conv_w, lru_conv_b, lru_w_r, lru_b_r, lru_w_i, lru_b_i, lru_lambda, lru_w_out, fox_w_in, fox_b_f, fox_q_gain, fox_k_gain, fox_w_out, loss_target, m_mix_norm, m_mlp_norm, m_mlp_w1, m_mlp_w2, m_lru_w_in, m_lru_conv_w, m_lru_conv_b, m_lru_w_r, m_lru_b_r, m_lru_w_i, m_lru_b_i, m_lru_lambda, m_lru_w_out, m_fox_w_in, m_fox_b_f, m_fox_q_gain, m_fox_k_gain, m_fox_w_out, v_mix_norm, v_mlp_norm, v_mlp_w1, v_mlp_w2, v_lru_w_in, v_lru_conv_w, v_lru_conv_b, v_lru_w_r, v_lru_b_r, v_lru_w_i, v_lru_b_i, v_lru_lambda, v_lru_w_out, v_fox_w_in, v_fox_b_f, v_fox_q_gain, v_fox_k_gain, v_fox_w_out):
    args = dict(locals())
    W = {n: args[n] for n in _WEIGHTS}
    Mo = {n: args["m_" + n] for n in _WEIGHTS}
    Vo = {n: args["v_" + n] for n in _WEIGHTS}
    S, D = x.shape[1], x.shape[2]
    F = 4 * D
    H = D // HEAD_DIM
    NU = 3 * D + LANES
    FQ, DQ = F // N_CHIPS, D // N_CHIPS
    nfox = fox_w_in.shape[-1]
    chip = 2 * lax.axis_index("x") + lax.axis_index("y")
    core = lax.axis_index("c").astype(jnp.int32).reshape(1)

    cw_flat = jnp.pad(lru_conv_w.reshape(-1), (0, _PACK_TILE - CONV_WIDTH * DQ)).reshape(2 * SUBLANES, LANES)
    w1s, w2s = mlp_w1.astype(BF16), mlp_w2.astype(BF16)
    wv = {}
    small = {n: W[n] for n in _REPLICATED}
    scattered = {}
    members = {"g1": ["w2_1", "w1_1", "fox_out"], "g2": ["fox_in"], "g3": ["w2_0", "w1_0"], "g4": ["lru_out", "lru_in"]}
    swap_at = {"fox_bwd_prep": "g1", "fox_dh": "g2", "lru_dout": "g3"}
    scatter_at = {"attn_backward": "g1", "mlp0_dact": "g2", "lru_bwd": "g3", "lru_dh": "g4"}
    swapped = {}

    def shard_major(name, g):
        if name == "fox_in":
            return jnp.transpose(g[:, :nfox * N_CHIPS].reshape(D, N_CHIPS, nfox), (1, 0, 2))
        return g

    class Comm:
        @staticmethod
        def before(name, grads):
            if name == "mix0_norm":
                return _Gather([lru_w_in[0].astype(BF16)])
            if name == "lru_in":
                return _Gather([lru_w_out[0].astype(BF16), cw_flat])
            if name == "lru_fwd":
                return _Gather([w1s[0]])
            if name == "mlp0_up":
                return _Gather([w2s[0]])
            if name == "mlp0_down":
                return _Gather([fox_w_in[0].astype(BF16)])
            if name == "attn_forward":
                return _Gather([fox_w_out[0].astype(BF16), w1s[1], w2s[1]])
            if name in swap_at:
                group = swap_at[name]
                swapped[group] = [[shard_major(n, grads[n]) for n in members[group]], None]
                return _Swap(swapped[group][0])
            if name in scatter_at:
                group = scatter_at[name]
                if group not in swapped:
                    arrs = [shard_major(n, grads[n]) for n in members[group]]
                    swapped[group] = [arrs, _run_plan(f"{group}_pair_swap", _Swap(arrs))]
                arrs, sib = swapped[group]
                return _pair_partials(group, arrs, sib, [BF16] * len(arrs), core)
            return None

        @staticmethod
        def after(name, res, wv):
            if name == "mix0_norm":
                wv.update(lru_in=_View(res[0], "cs"))
            elif name == "lru_in":
                wv.update(lru_out=_View(res[0], "rs"))
                taps = res[1].reshape(N_CHIPS, -1)[:, :CONV_WIDTH * DQ].reshape(N_CHIPS, CONV_WIDTH, DQ)
                small["conv_w"] = jnp.transpose(taps, (1, 0, 2)).reshape(CONV_WIDTH, D)
            elif name == "lru_fwd":
                wv.update(w1_0=_View(res[0], "cs"))
            elif name == "mlp0_up":
                wv.update(w2_0=_View(res[0], "rs"))
            elif name == "mlp0_down":
                fox_full = jnp.concatenate([res[0][s] for s in range(N_CHIPS)], axis=1)
                fox_full = jnp.pad(fox_full, ((0, 0), (0, NU - fox_full.shape[1])))
                wv.update(fox_in=_View(fox_full.T))
            elif name == "attn_forward":
                wv.update(fox_out=_View(res[0], "rs"), w1_1=_View(res[1], "cs"), w2_1=_View(res[2], "rs"))
            elif name in swap_at:
                swapped[swap_at[name]][1] = res
            else:
                scattered.update(zip(members[scatter_at[name]], res))

    def grad_view(grads, name):
        if name in ("w1_0", "w1_1"):
            return _View(None, "cs", shape=(N_CHIPS, D, FQ), dtype=BF16)
        if name in ("w2_0", "w2_1"):
            return _View(None, "rs", shape=(N_CHIPS, FQ, D), dtype=BF16)
        if name == "lru_in":
            return _View(None, "cs", shape=(N_CHIPS, D, 2 * D // N_CHIPS), dtype=BF16)
        if name in ("lru_out", "fox_out"):
            return _View(None, "rs", shape=(N_CHIPS, DQ, D), dtype=BF16)
        return _View(None, shape=(D, NU), dtype=BF16)

    loss, gx, grads = _local_step(x[0], loss_target[0], small, wv, grad_view, Comm)

    pack_names = _REPLICATED + ["conv_w"]
    flat = jnp.concatenate([grads[n].reshape(-1).astype(F32) for n in pack_names] + [loss.reshape(-1)])
    per_chip = -(-flat.shape[0] // (N_CHIPS * _PACK_TILE)) * _PACK_TILE
    pack = jnp.pad(flat, (0, N_CHIPS * per_chip - flat.shape[0])).reshape(N_CHIPS, per_chip // LANES, LANES)
    pack_sib = _run_plan("pack_pair_swap", _Swap([pack]))
    (scattered["pack"],) = _run_plan("pack_chip_scatter", _pair_partials("pack", [pack], pack_sib, [F32], core))
    order = ["w1_0", "w1_1", "w2_0", "w2_1", "lru_in", "lru_out", "fox_in", "fox_out", "pack"]
    red = dict(zip(order, _finish_reduce("grads", [scattered[n] for n in order])))
    (all_pack,) = _all_gather("gather_small_grads", [red["pack"]])
    all_flat = all_pack.reshape(-1)
    G = {}
    off = 0
    for n in pack_names:
        shape = grads[n].shape if n == "conv_w" else W[n].shape
        size = int(np.prod(shape))
        G[n] = all_flat[off:off + size].reshape(shape)
        off += size
    total = all_flat[off]
    G["lru_conv_w"] = lax.dynamic_slice_in_dim(G.pop("conv_w"), chip * DQ, DQ, axis=1)[None]
    parts = {n: [_as2d(G[n])] for n in G}
    parts.update(mlp_w1=[red["w1_0"], red["w1_1"]], mlp_w2=[red["w2_0"], red["w2_1"]], lru_w_in=[red["lru_in"]],
                 lru_w_out=[red["lru_out"]], fox_w_in=[red["fox_in"]], fox_w_out=[red["fox_out"]])

    delta, new_m, new_v = {}, {}, {}
    for n in _WEIGHTS:
        if W[n].shape[-1] % LANES and W[n].shape[-2] % LANES == 0:
            to_thin = lambda a: jnp.transpose(a, (2, 0, 1))
            res = _adamw(f"adamw_{n}", to_thin(W[n]), [to_thin(parts[n][0][None])], to_thin(Mo[n]), to_thin(Vo[n]))
            G[n], delta[n], new_m[n], new_v[n] = (jnp.transpose(t, (1, 2, 0)) for t in res)
            continue
        go, d, nm, nv = _adamw(f"adamw_{n}", _as2d(W[n]), parts[n], _as2d(Mo[n]), _as2d(Vo[n]))
        G[n], delta[n], new_m[n], new_v[n] = (t.reshape(W[n].shape) for t in (go, d, nm, nv))

    return (total, gx[None], *[G[n] for n in _WEIGHTS], *[delta[n] for n in _WEIGHTS],
            *[new_m[n] for n in _WEIGHTS], *[new_v[n] for n in _WEIGHTS])
```

```python
import functools

import numpy as np
import jax
import jax.numpy as jnp
from jax import lax
from jax.experimental import pallas as pl
from jax.experimental.pallas import tpu as pltpu

F32 = jnp.float32
BF16 = jnp.bfloat16

HEAD_DIM = 64
LRU_BLOCK_DIM = 64
CONV_WIDTH = 4
LRU_C = 8.0
EPS = 1e-6
NEG_INF = -1e30
ADAM_LR = 0.001
ADAM_B1 = 0.9
ADAM_B2 = 0.999
ADAM_EPS = 1e-08
ADAM_WD = 0.01
ADAM_STEP = 10

N_CHIPS = 4
LANES = 128
SUBLANES = 8
MXU_DIM = 256
VMEM_LIMIT = 52 * 1024 * 1024
MESH = pl.DeviceIdType.MESH
ANY = pl.BlockSpec(memory_space=pl.ANY)


def _pick(n, prefs):
    for p in prefs:
        if p <= n and n % p == 0:
            return p
    return n


def _params(sem=None):
    return pltpu.CompilerParams(dimension_semantics=sem, vmem_limit_bytes=VMEM_LIMIT)


class _View:
    def __init__(self, arr, kind="plain", r0=0, rows=None, shape=None, dtype=None):
        self.arr = arr
        self.kind = kind
        self.r0 = r0
        self.shape = tuple(arr.shape) if arr is not None else tuple(shape)
        self.dtype = arr.dtype if arr is not None else dtype
        self.rows = rows if rows is not None else self.shape[-2]

    def limits(self):
        if self.kind == "plain":
            return 0, 0
        rows = int(np.gcd(self.rows, self.r0))
        return rows, (self.shape[-1] if self.kind == "cs" else 0)

    def spec(self, br, bc, fr, fc):
        if self.kind == "plain":
            return pl.BlockSpec((br, bc), lambda *g: (fr(*g), fc(*g)))
        ncol = self.shape[-1]
        r0b = self.r0 // br
        assert self.r0 % br == 0 and self.rows % br == 0 and ncol % bc == 0, (self.shape, self.r0, br, bc)
        if self.kind == "cs":
            per = ncol // bc
            return pl.BlockSpec((None, br, bc), lambda *g: (fc(*g) // per, r0b + fr(*g), fc(*g) % per))
        per = self.rows // br
        return pl.BlockSpec((None, br, bc), lambda *g: (fr(*g) // per, r0b + fr(*g) % per, fc(*g)))


def _bf(x):
    return x if x.dtype == BF16 else x.astype(BF16)


def _matmul(name, A, B, M, N, K, *, ta=False, tb=False, outs, epilogue, extras=(), vecs=(), n_sums=0,
            tm=None, tn=None, tk=None, plan=None):
    lim = {"m": [M], "n": [N], "k": [K]}
    for view, (rdim, cdim) in ([(A, "km" if ta else "mk"), (B, "nk" if tb else "kn")]
                               + [(e, "mn") for e in extras] + [(o, "mn") for o in outs]):
        r_lim, c_lim = view.limits()
        lim[rdim].append(r_lim)
        lim[cdim].append(c_lim)
    tm = tm or _pick(int(np.gcd.reduce(lim["m"])), (1024, 640, 512, 256, 128))
    tn = tn or _pick(int(np.gcd.reduce(lim["n"])), (1024, 640, 512, 256, 128))
    tk = tk or _pick(int(np.gcd.reduce(lim["k"])), (1024, 640, 512, 256, 128))
    nk = K // tk
    gi = lambda i, j, k: i
    gj = lambda i, j, k: j
    gk = lambda i, j, k: k
    a_spec = A.spec(tk, tm, gk, gi) if ta else A.spec(tm, tk, gi, gk)
    b_spec = B.spec(tn, tk, gj, gk) if tb else B.spec(tk, tn, gk, gj)
    ca = 0 if ta else 1
    cb = 1 if tb else 0
    ne, no = len(extras) + len(vecs), len(outs)
    assert n_sums == 0 or tn == N
    row_spec = pl.BlockSpec((1, tn), lambda i, j, k: (0, j))
    in_specs = [a_spec, b_spec] + [e.spec(tm, tn, gi, gj) for e in extras] + [row_spec] * len(vecs)
    operands = [A.arr, B.arr] + [e.arr for e in extras] + list(vecs)
    out_specs = [o.spec(tm, tn, gi, gj) for o in outs] + [row_spec] * n_sums
    out_shape = ([jax.ShapeDtypeStruct(o.shape, o.dtype) for o in outs]
                 + [jax.ShapeDtypeStruct((1, N), F32)] * n_sums)

    def body(*refs):
        a_ref, b_ref = refs[0], refs[1]
        ex = refs[2:2 + ne]
        o_refs = refs[2 + ne:2 + ne + no]
        s_refs = refs[2 + ne + no:2 + ne + no + n_sums]
        first_row_tile = pl.program_id(0) == 0

        def prod():
            return lax.dot_general(_bf(a_ref[...]), _bf(b_ref[...]), (((ca,), (cb,)), ((), ())),
                                   preferred_element_type=F32)

        def finish(acc):
            res = epilogue(acc, *[e[...] for e in ex])
            for o_ref, r in zip(o_refs, res[:no]):
                o_ref[...] = r.astype(o_ref.dtype)
            for s_ref, r in zip(s_refs, res[no:]):
                def assign(s_ref=s_ref, r=r):
                    s_ref[...] = r

                def accumulate(s_ref=s_ref, r=r):
                    s_ref[...] += r

                pl.when(first_row_tile)(assign)
                pl.when(jnp.logical_not(first_row_tile))(accumulate)

        if nk == 1:
            finish(prod())
        else:
            acc_ref = refs[-1]
            k = pl.program_id(2)

            @pl.when(k == 0)
            def _():
                acc_ref[...] = jnp.zeros_like(acc_ref)

            acc_ref[...] += prod()

            @pl.when(k == nk - 1)
            def _():
                finish(acc_ref[...])

    res, side = _hosted_call(body, name, (M // tm, N // tn, nk), in_specs, out_specs, out_shape,
                             [pltpu.VMEM((tm, tn), F32)] if nk > 1 else [], operands,
                             ("arbitrary", "arbitrary", "arbitrary"), plan)
    return res if plan is None else (res, side)


def _ep_store(acc):
    return (acc,)


def _ep_resid(acc, res):
    return (res + acc,)


def _ep_resid_norm(acc, res, g):
    xo = res + acc
    r = lax.rsqrt(jnp.mean(xo * xo, axis=-1, keepdims=True) + EPS)
    return (xo, (xo * r) * g)


def _ep_norm_bwd(acc, x, dres, g):
    r = lax.rsqrt(jnp.mean(x * x, axis=-1, keepdims=True) + EPS)
    xhat = x * r
    dxn = acc * g
    tot = dres + r * (dxn - xhat * jnp.mean(dxn * xhat, axis=-1, keepdims=True))
    return (tot, tot, jnp.sum(acc * xhat, axis=0, keepdims=True))


def _ep_relu2(acc):
    zp = jnp.maximum(acc, 0.0)
    return (acc, zp * zp)


def _ep_drelu2(acc, z):
    return (acc * (2.0 * jnp.maximum(z.astype(F32), 0.0)),)


def _fresh(M, N, dtype):
    return _View(None, shape=(M, N), dtype=dtype)


def _rms_fwd(name, x, g, S, D, plan=None):
    T = _pick(S, (512, 256, 128))

    def body(x_ref, g_ref, h_ref):
        x = x_ref[...]
        r = lax.rsqrt(jnp.mean(x * x, axis=-1, keepdims=True) + EPS)
        h_ref[...] = ((x * r) * g_ref[...]).astype(BF16)

    return _hosted_call(body, name, (S // T,),
                        [pl.BlockSpec((T, D), lambda i: (i, 0)), pl.BlockSpec((1, D), lambda i: (0, 0))],
                        [pl.BlockSpec((T, D), lambda i: (i, 0))], [jax.ShapeDtypeStruct((S, D), BF16)], [], (x, g),
                        ("arbitrary",), plan)


def _loss_head(x, tgt, S, D):
    T = _pick(S, (512, 256, 128))

    def body(x_ref, t_ref, loss_ref, d_ref, db_ref):
        @pl.when(pl.program_id(0) == 0)
        def _():
            loss_ref[...] = jnp.zeros_like(loss_ref)

        e = x_ref[...] - t_ref[...]
        loss_ref[...] += 0.5 * jnp.sum(jnp.mean(e * e, axis=-1, keepdims=True), axis=0, keepdims=True)
        d = e * (1.0 / D)
        d_ref[...] = d
        db_ref[...] = d.astype(BF16)

    row = pl.BlockSpec((T, D), lambda i: (i, 0))
    return pl.pallas_call(
        body, name="loss_head", grid=(S // T,), in_specs=[row, row],
        out_specs=[pl.BlockSpec((1, 1), lambda i: (0, 0)), row, row],
        out_shape=[jax.ShapeDtypeStruct((1, 1), F32), jax.ShapeDtypeStruct((S, D), F32),
                   jax.ShapeDtypeStruct((S, D), BF16)],
        compiler_params=_params(("arbitrary",)),
    )(x, tgt)


def _sigmoid(z):
    return 1.0 / (1.0 + jnp.exp(-z))


def _log_sigmoid(z):
    return jnp.minimum(z, 0.0) - jnp.log(1.0 + jnp.exp(-jnp.abs(z)))


_GELU_K = 0.7978845608028654
_GELU_C = 0.044715


def _gelu(x):
    t = jnp.tanh(_GELU_K * (x + _GELU_C * (x * x * x)))
    return 0.5 * x * (1.0 + t)


def _gelu_and_grad(x):
    x2 = x * x
    t = jnp.tanh(_GELU_K * (x + _GELU_C * (x2 * x)))
    g = 0.5 * x * (1.0 + t)
    dg = 0.5 * (1.0 + t) + 0.5 * x * (1.0 - t * t) * (_GELU_K * (1.0 + 3.0 * _GELU_C * x2))
    return g, dg


def _decay_terms(r, ls):
    la = LRU_C * r * ls
    a = jnp.exp(la)
    a2 = a * a
    mult = jnp.sqrt(-jnp.tanh(la) * (a2 + 1.0))
    return a, a2, mult


def _lru_fwd(u0, conv_w, conv_b, wr_bd, b_r, wi_bd, b_i, lam, S, D, plan=None):
    T = _pick(S, (256, 128))
    GT = wr_bd.shape[-1]
    nG = D // GT

    def body(gb_ref, xb_ref, cw_ref, cb_ref, wr_ref, br_ref, wi_ref, bi_ref, lam_ref,
             y_ref, xc_ref, r_ref, i_ref, hs_ref, ext, a_scr, hcar):
        @pl.when(pl.program_id(0) == 0)
        def _():
            ext[0:SUBLANES, :] = jnp.zeros((SUBLANES, D), F32)
            hcar[...] = jnp.zeros_like(hcar)

        xb = xb_ref[...]
        ext[SUBLANES:SUBLANES + T, :] = xb
        xc = cb_ref[...]
        for k in range(CONV_WIDTH):
            xc = xc + ext[pl.ds(SUBLANES - (CONV_WIDTH - 1) + k, T), :] * cw_ref[k:k + 1, :]
        ext[0:SUBLANES, :] = xb[T - SUBLANES:T, :]
        xc_ref[...] = xc
        xcb = xc.astype(BF16)
        for g in range(nG):
            sl = slice(g * GT, (g + 1) * GT)
            zr = jnp.dot(xcb[:, sl], wr_ref[g], preferred_element_type=F32) + br_ref[:, sl]
            zi = jnp.dot(xcb[:, sl], wi_ref[g], preferred_element_type=F32) + bi_ref[:, sl]
            r_ref[:, sl] = _sigmoid(zr)
            i_ref[:, sl] = _sigmoid(zi)
        r = r_ref[...]
        a, _, mult = _decay_terms(r, _log_sigmoid(lam_ref[...]))
        a_scr[...] = a
        hs_ref[...] = mult * (i_ref[...] * xc)

        def step(t, h):
            h = a_scr[pl.ds(t, 1), :] * h + hs_ref[pl.ds(t, 1), :]
            hs_ref[pl.ds(t, 1), :] = h
            return h

        hcar[...] = lax.fori_loop(0, T, step, hcar[...], unroll=8)
        y_ref[...] = (_gelu(gb_ref[...]) * hs_ref[...]).astype(BF16)

    row = pl.BlockSpec((T, D), lambda i: (i, 0))
    vec = pl.BlockSpec((1, D), lambda i: (0, 0))
    bd = pl.BlockSpec((nG, GT, GT), lambda i: (0, 0, 0))
    f32o = jax.ShapeDtypeStruct((S, D), F32)
    return _hosted_call(
        body, "lru_fwd", (S // T,),
        [row, pl.BlockSpec((T, D), lambda i: (i, 1)), pl.BlockSpec((CONV_WIDTH, D), lambda i: (0, 0)), vec,
         bd, vec, bd, vec, vec],
        [row, row, row, row, row], [jax.ShapeDtypeStruct((S, D), BF16), f32o, f32o, f32o, f32o],
        [pltpu.VMEM((T + SUBLANES, D), F32), pltpu.VMEM((T, D), F32), pltpu.VMEM((1, D), F32)],
        (u0, u0, conv_w, conv_b, wr_bd, b_r, wi_bd, b_i, lam), ("arbitrary",), plan)


def _lru_bwd(dy, u0, xc, r, ig, hs, conv_w, wr_bd, wi_bd, lam, S, D, plan=None):
    T = _pick(S, (128,))
    nT = S // T
    GT = wr_bd.shape[-1]
    nG = D // GT
    W = CONV_WIDTH

    def body(dy_ref, gb_ref, xb_ref, xbp_ref, xc_ref, r_ref, i_ref, hs_ref, hsp_ref, cw_ref, wr_ref, wi_ref, lam_ref,
             du_ref, dcw_ref, dcb_ref, dlam_ref, dbr_ref, dbi_ref, dwr_ref, dwi_ref,
             a_scr, dh_scr, exth, extx, extd, dxc_scr, dz_scr, carry):
        step = pl.program_id(0)
        first_tile = step == nT - 1

        @pl.when(step == 0)
        def _():
            for ref in (dcw_ref, dcb_ref, dlam_ref, dbr_ref, dbi_ref, dwr_ref, dwi_ref, carry):
                ref[...] = jnp.zeros_like(ref)
            extd[T:T + SUBLANES, :] = jnp.zeros((SUBLANES, D), F32)

        hs = hs_ref[...]
        dy = dy_ref[...]
        g, dgelu = _gelu_and_grad(gb_ref[...])
        du_ref[:, 0:D] = (dy * hs * dgelu).astype(BF16)
        r = r_ref[...]
        lam = lam_ref[...]
        ls = _log_sigmoid(lam)
        a, a2, mult = _decay_terms(r, ls)
        a_scr[...] = a
        dh_scr[...] = dy * g

        def rstep(j, c):
            t = T - 1 - j
            d = dh_scr[pl.ds(t, 1), :] + c
            dh_scr[pl.ds(t, 1), :] = d
            return a_scr[pl.ds(t, 1), :] * d

        carry[...] = lax.fori_loop(0, T, rstep, carry[...], unroll=8)
        dh = dh_scr[...]
        keep = jnp.where(first_tile, 0.0, 1.0)
        exth[0:SUBLANES, :] = hsp_ref[...] * keep
        exth[SUBLANES:SUBLANES + T, :] = hs
        hprev = exth[pl.ds(SUBLANES - 1, T), :]
        xc = xc_ref[...]
        ig = i_ref[...]
        da = dh * hprev
        dmult = dh * (ig * xc)
        dla = da * a - dmult * (a2 / mult)
        dlam_ref[...] += jnp.sum(dla * r, axis=0, keepdims=True) * (LRU_C * _sigmoid(-lam))
        dzr = (dla * (LRU_C * ls)) * (r * (1.0 - r))
        dzi = (dh * (mult * xc)) * (ig * (1.0 - ig))
        dbr_ref[...] += jnp.sum(dzr, axis=0, keepdims=True)
        dbi_ref[...] += jnp.sum(dzi, axis=0, keepdims=True)
        dxc_scr[...] = dh * (mult * ig)
        xcb = xc.astype(BF16)
        dz_scr[0] = dzr.astype(BF16)
        dz_scr[1] = dzi.astype(BF16)
        nt_dims = (((1,), (1,)), ((), ()))
        tn_dims = (((0,), (0,)), ((), ()))
        for gq in range(nG):
            sl = slice(gq * GT, (gq + 1) * GT)
            zr_g = dz_scr[0, :, sl]
            zi_g = dz_scr[1, :, sl]
            dxc_scr[:, sl] += (lax.dot_general(zr_g, wr_ref[gq], nt_dims, preferred_element_type=F32)
                               + lax.dot_general(zi_g, wi_ref[gq], nt_dims, preferred_element_type=F32))
            dwr_ref[gq] += lax.dot_general(xcb[:, sl], zr_g, tn_dims, preferred_element_type=F32)
            dwi_ref[gq] += lax.dot_general(xcb[:, sl], zi_g, tn_dims, preferred_element_type=F32)
        dxc = dxc_scr[...]
        dcb_ref[...] += jnp.sum(dxc, axis=0, keepdims=True)
        extx[0:SUBLANES, :] = xbp_ref[...] * keep
        extx[SUBLANES:SUBLANES + T, :] = xb_ref[...]
        extd[0:T, :] = dxc
        dxb = jnp.zeros((T, D), F32)
        for k in range(W):
            dxb = dxb + extd[pl.ds(W - 1 - k, T), :] * cw_ref[k:k + 1, :]
            dcw_ref[k:k + 1, :] += jnp.sum(dxc * extx[pl.ds(SUBLANES - (W - 1) + k, T), :], axis=0, keepdims=True)
        extd[T:T + SUBLANES, :] = dxc[0:SUBLANES, :]
        du_ref[:, D:2 * D] = dxb.astype(BF16)

    rev = lambda i: nT - 1 - i
    tpb = T // SUBLANES
    prev8 = lambda i: jnp.maximum(rev(i) * tpb - 1, 0)
    row = pl.BlockSpec((T, D), lambda i: (rev(i), 0))
    vec = pl.BlockSpec((1, D), lambda i: (0, 0))
    bd = pl.BlockSpec((nG, GT, GT), lambda i: (0, 0, 0))
    vec_o = jax.ShapeDtypeStruct((1, D), F32)
    bd_o = jax.ShapeDtypeStruct((nG, GT, GT), F32)
    return _hosted_call(
        body, "lru_bwd", (nT,),
        [row, row, pl.BlockSpec((T, D), lambda i: (rev(i), 1)), pl.BlockSpec((SUBLANES, D), lambda i: (prev8(i), 1)),
         row, row, row, row, pl.BlockSpec((SUBLANES, D), lambda i: (prev8(i), 0)),
         pl.BlockSpec((W, D), lambda i: (0, 0)), bd, bd, vec],
        [pl.BlockSpec((T, 2 * D), lambda i: (rev(i), 0)), pl.BlockSpec((W, D), lambda i: (0, 0)),
         vec, vec, vec, vec, bd, bd],
        [jax.ShapeDtypeStruct((S, 2 * D), BF16), jax.ShapeDtypeStruct((W, D), F32), vec_o, vec_o, vec_o, vec_o, bd_o, bd_o],
        [pltpu.VMEM((T, D), F32), pltpu.VMEM((T, D), F32), pltpu.VMEM((T + SUBLANES, D), F32),
         pltpu.VMEM((T + SUBLANES, D), F32), pltpu.VMEM((T + SUBLANES, D), F32),
         pltpu.VMEM((T, D), F32), pltpu.VMEM((2, T, D), BF16), pltpu.VMEM((1, D), F32)],
        (dy, u0, u0, u0, xc, r, ig, hs, hs, conv_w, wr_bd, wi_bd, lam), ("arbitrary",), plan)


AUG_ROWS = 16
HEAD_ROWS = 128
LSE_ROW = HEAD_DIM + 6
ONES_ROW_Q = HEAD_DIM + 3
ONES_COL_K = HEAD_DIM
ONES_ROW_V = HEAD_DIM
PREP_LANES = 512
HEAD_UNROLL = 4


def _split3(x):
    b1 = x.astype(BF16).astype(F32)
    r = x - b1
    b2 = r.astype(BF16).astype(F32)
    return b1, b2, r - b2


def _head_block(x, aug, T):
    row = lax.broadcasted_iota(jnp.int32, (AUG_ROWS, T), 0)
    blk = jnp.zeros((AUG_ROWS, T), F32)
    for i, e in enumerate(aug):
        blk = jnp.where(row == i, e, blk)
    return jnp.concatenate([x, blk, jnp.zeros((HEAD_ROWS - HEAD_DIM - AUG_ROWS, T), F32)], axis=0)


def _tri_matrix(lower):
    i = np.arange(LANES)
    m = (i[:, None] >= i[None, :]) if lower else (i[:, None] <= i[None, :])
    return jnp.asarray(m.astype(np.float32), BF16)


def _lane_cumsum(x, tri_ref, carry, reverse):
    n = x.shape[1] // LANES
    tri = tri_ref[...]
    out = [None] * n
    for j in (range(n - 1, -1, -1) if reverse else range(n)):
        cs = carry
        for part in _split3(x[:, j * LANES:(j + 1) * LANES]):
            cs = cs + jnp.dot(part.astype(BF16), tri, preferred_element_type=F32)
        out[j] = cs
        carry = cs[:, 0:1] if reverse else cs[:, LANES - 1:LANES]
    return jnp.concatenate(out, axis=1), carry


def _head_rows(h):
    return pl.ds(pl.multiple_of(h * HEAD_DIM, HEAD_DIM), HEAD_DIM)


def _fox_prep(ut, b_f, qg, kg, S, D, tq):
    H = D // HEAD_DIM
    T = min(tq, PREP_LANES)
    per = tq // T
    scale = HEAD_DIM ** -0.5

    def body(q_ref, k_ref, v_ref, f_ref, bf_ref, qg_ref, kg_ref, tri_ref,
             qat_ref, kat_ref, vat_ref, ka_ref, c_scr, ccar):
        @pl.when(pl.program_id(0) == 0)
        def _():
            ccar[...] = jnp.zeros_like(ccar)

        c, carry = _lane_cumsum(_log_sigmoid(f_ref[...] + bf_ref[...]), tri_ref, ccar[...], False)
        c_scr[...] = c
        ccar[...] = carry

        def head(h, _):
            rows = _head_rows(h)
            c1, c2, c3 = _split3(c_scr[pl.ds(h, 1), :])

            def normed(src, gain, mul):
                x = src[rows, :]
                rs = lax.rsqrt(jnp.mean(x * x, axis=0, keepdims=True) + EPS)
                return ((x * rs) * gain[rows, :]) * mul

            qat_ref[h] = _head_block(normed(q_ref, qg_ref, scale), [c1, c2, c3, 1.0, 1.0, 1.0], T).astype(BF16)
            kb = _head_block(normed(k_ref, kg_ref, 1.0), [1.0, 1.0, 1.0, -c1, -c2, -c3, 1.0, 1.0, 1.0], T)
            kat_ref[h] = kb.astype(BF16)
            ka_ref[h] = kb.T.astype(BF16)
            vat_ref[h] = _head_block(v_ref[rows, :], [1.0, 1.0, 1.0], T).astype(BF16)
            return 0

        lax.fori_loop(0, H, head, 0, unroll=HEAD_UNROLL)

    part = lambda j: pl.BlockSpec((D, T), lambda i: (j, i))
    colv = lambda n: pl.BlockSpec((n, 1), lambda i: (0, 0))
    tmaj = lambda r: pl.BlockSpec((H, None, r, T), lambda i: (0, i // per, 0, i % per))
    norm = pl.BlockSpec((H, T, HEAD_ROWS), lambda i: (0, i, 0))
    tshape = lambda r: jax.ShapeDtypeStruct((H, S // tq, r, tq), BF16)
    nshape = jax.ShapeDtypeStruct((H, S, HEAD_ROWS), BF16)
    return pl.pallas_call(
        body, name="fox_prep", grid=(S // T,),
        in_specs=[part(0), part(1), part(2), pl.BlockSpec((LANES, T), lambda i: (3 * D // LANES, i)),
                  colv(LANES), colv(D), colv(D), pl.BlockSpec((LANES, LANES), lambda i: (0, 0))],
        out_specs=[tmaj(HEAD_ROWS), tmaj(HEAD_ROWS), tmaj(HEAD_ROWS), norm],
        out_shape=[tshape(HEAD_ROWS), tshape(HEAD_ROWS), tshape(HEAD_ROWS), nshape],
        scratch_shapes=[pltpu.VMEM((LANES, T), F32), pltpu.VMEM((LANES, 1), F32)],
        compiler_params=_params(("arbitrary",)),
    )(ut, ut, ut, ut, b_f, qg, kg, _tri_matrix(False))


def _fox_bwd_prep(dot, ot, lse, qat, S, D, tq, plan=None):
    H = D // HEAD_DIM
    T = min(tq, PREP_LANES)
    per = tq // T

    def body(do_ref, o_ref, lse_ref, qat_ref, doat_ref, doa_ref, qat1_ref, qa1_ref):
        row = lax.broadcasted_iota(jnp.int32, (HEAD_ROWS, T), 0)

        def head(h, _):
            rows = _head_rows(h)
            do = do_ref[rows, :].astype(F32)
            delta = jnp.sum(do * o_ref[rows, :], axis=0, keepdims=True)
            db = _head_block(do, list(_split3(-delta)), T)
            doat_ref[h] = db.astype(BF16)
            doa_ref[h] = db.T.astype(BF16)
            qb = qat_ref[h].astype(F32)
            for i, e in enumerate(_split3(-lse_ref[h])):
                qb = jnp.where(row == LSE_ROW + i, e, qb)
            qat1_ref[h] = qb.astype(BF16)
            qa1_ref[h] = qb.T.astype(BF16)
            return 0

        lax.fori_loop(0, H, head, 0, unroll=HEAD_UNROLL)

    chan = pl.BlockSpec((D, T), lambda i: (0, i))
    tmaj = pl.BlockSpec((H, None, HEAD_ROWS, T), lambda i: (0, i // per, 0, i % per))
    norm = pl.BlockSpec((H, T, HEAD_ROWS), lambda i: (0, i, 0))
    tshape = jax.ShapeDtypeStruct((H, S // tq, HEAD_ROWS, tq), BF16)
    nshape = jax.ShapeDtypeStruct((H, S, HEAD_ROWS), BF16)
    return _hosted_call(body, "fox_bwd_prep", (S // T,), [chan, chan, pl.BlockSpec((H, 1, T), lambda i: (0, 0, i)), tmaj],
                        [tmaj, norm, tmaj, norm], [tshape, nshape, tshape, nshape], [], (dot, ot, lse, qat),
                        ("arbitrary",), plan)


def _causal(s, k_axis):
    t = min(s.shape)
    ki = lax.broadcasted_iota(jnp.int32, s.shape, k_axis) - (s.shape[k_axis] - t)
    qi = lax.broadcasted_iota(jnp.int32, s.shape, 1 - k_axis)
    return jnp.where(ki <= qi, s, NEG_INF)


def _seq_tile(i, t):
    return pl.ds(pl.multiple_of(i * t, t), t)


def _attn_forward(ka, qat, vat, S, D, tq, plan=None):
    H = D // HEAD_DIM
    nq = S // tq

    def body(ka_ref, qat_ref, vat_ref, o_ref, o32_ref, lse_ref, m_scr, acc_scr):
        qi = pl.program_id(1)
        m_scr[...] = jnp.full_like(m_scr, NEG_INF)
        acc_scr[...] = jnp.zeros_like(acc_scr)
        qa = qat_ref[...]

        def span(k0, n, diagonal):
            s = jnp.dot(ka_ref[pl.ds(pl.multiple_of(k0 * tq, tq), n * tq), :], qa, preferred_element_type=F32)
            if diagonal:
                s = _causal(s, 0)
            m_prev = m_scr[...]
            m_new = jnp.maximum(m_prev, jnp.max(s, axis=0, keepdims=True))
            p = jnp.exp(s - m_new).astype(BF16)
            upd = jnp.dot(vat_ref[k0], p[0:tq], preferred_element_type=F32)
            for i in range(1, n):
                upd = upd + jnp.dot(vat_ref[k0 + i], p[i * tq:(i + 1) * tq], preferred_element_type=F32)
            acc_scr[...] = jnp.exp(m_prev - m_new) * acc_scr[...] + upd
            m_scr[...] = m_new

        def off_diagonal_pair(j, _):
            span(2 * j, 2, False)
            return 0

        lax.fori_loop(0, qi // 2, off_diagonal_pair, 0)
        pl.when(qi % 2 == 1)(lambda: span(qi - 1, 2, True))
        pl.when(qi % 2 == 0)(lambda: span(qi, 1, True))
        l = acc_scr[ONES_ROW_V:ONES_ROW_V + 1, :]
        o = acc_scr[0:HEAD_DIM, :] / l
        o_ref[...] = o.astype(BF16)
        o32_ref[...] = o
        lse_ref[...] = m_scr[...] + jnp.log(l)

    chan = pl.BlockSpec((HEAD_DIM, tq), lambda h, i: (h, i))
    stat = pl.BlockSpec((None, 1, tq), lambda h, i: (h, 0, i))
    return _hosted_call(
        body, "attn_forward", (H, nq),
        [pl.BlockSpec((None, S, HEAD_ROWS), lambda h, i: (h, 0, 0)),
         pl.BlockSpec((None, None, HEAD_ROWS, tq), lambda h, i: (h, i, 0, 0)),
         pl.BlockSpec((None, nq, HEAD_ROWS, tq), lambda h, i: (h, 0, 0, 0))],
        [chan, chan, stat],
        [jax.ShapeDtypeStruct((D, S), BF16), jax.ShapeDtypeStruct((D, S), F32), jax.ShapeDtypeStruct((H, 1, S), F32)],
        [pltpu.VMEM((1, tq), F32), pltpu.VMEM((HEAD_ROWS, tq), F32)],
        (ka, qat, vat), ("arbitrary", "arbitrary"), plan)


def _attn_backward(qa, doa, qat, doat, ka, kat, vat, S, D, tq, plan=None):
    H = D // HEAD_DIM
    nq = S // tq

    def body(qa_ref, doa_ref, qat_ref, doat_ref, ka_ref, kat_ref, vat_ref, dq_ref, dk_ref, dv_ref, dk_scr, dv_scr):
        ki = pl.program_id(1)

        @pl.when(ki == 0)
        def _():
            dq_ref[...] = jnp.zeros_like(dq_ref)

        dk_scr[...] = jnp.zeros_like(dk_scr)
        dv_scr[...] = jnp.zeros_like(dv_scr)
        kt = kat_ref[...]
        vt = vat_ref[...]
        kn = ka_ref[...]

        def span(q0, n, diagonal):
            rows = pl.ds(pl.multiple_of(q0 * tq, tq), n * tq)
            s = jnp.dot(qa_ref[rows, :], kt, preferred_element_type=F32)
            if diagonal:
                s = _causal(s, 1)
            p = jnp.exp(s)
            ds = (p * jnp.dot(doa_ref[rows, :], vt, preferred_element_type=F32)).astype(BF16)
            p = p.astype(BF16)
            for i in range(n):
                part = slice(i * tq, (i + 1) * tq)
                dv_scr[...] += jnp.dot(doat_ref[q0 + i, 0:HEAD_DIM, :], p[part], preferred_element_type=F32)
                dk_scr[...] += jnp.dot(qat_ref[q0 + i], ds[part], preferred_element_type=F32)
            dq_ref[rows, :] += jnp.dot(ds, kn, preferred_element_type=F32)

        n_off = nq - 1 - ki
        odd = n_off % 2

        def off_diagonal_pair(j, _):
            span(ki + 1 + odd + 2 * j, 2, False)
            return 0

        pl.when(odd == 1)(lambda: span(ki, 2, True))
        pl.when(odd == 0)(lambda: span(ki, 1, True))
        lax.fori_loop(0, n_off // 2, off_diagonal_pair, 0)
        dk_ref[...] = dk_scr[...]
        dv_ref[...] = dv_scr[...].astype(BF16)

    whole = pl.BlockSpec((None, S, HEAD_ROWS), lambda h, i: (h, 0, 0))
    tiles = pl.BlockSpec((None, nq, HEAD_ROWS, tq), lambda h, i: (h, 0, 0, 0))
    one = pl.BlockSpec((None, None, HEAD_ROWS, tq), lambda h, i: (h, i, 0, 0))
    return _hosted_call(
        body, "attn_backward", (H, nq),
        [whole, whole, tiles, tiles, pl.BlockSpec((None, tq, HEAD_ROWS), lambda h, i: (h, i, 0)), one, one],
        [whole, pl.BlockSpec((None, HEAD_ROWS, tq), lambda h, i: (h, 0, i)),
         pl.BlockSpec((HEAD_DIM, tq), lambda h, i: (h, i))],
        [jax.ShapeDtypeStruct((H, S, HEAD_ROWS), F32), jax.ShapeDtypeStruct((H, HEAD_ROWS, S), F32),
         jax.ShapeDtypeStruct((D, S), BF16)],
        [pltpu.VMEM((HEAD_ROWS, tq), F32), pltpu.VMEM((HEAD_DIM, tq), F32)],
        (qa, doa, qat, doat, ka, kat, vat), ("arbitrary", "arbitrary"), plan)


def _fox_prep_bwd(ut, dq, dkt, dvt, b_f, qg, kg, S, D, tq):
    H = D // HEAD_DIM
    T = min(tq, PREP_LANES)
    nT = S // T
    NU = 3 * D + LANES
    scale = HEAD_DIM ** -0.5

    def body(q_ref, k_ref, f_ref, dq_ref, dk_ref, dv_ref, bf_ref, qg_ref, kg_ref, tri_ref,
             du_ref, dbf_ref, dqg_ref, dkg_ref, gq_acc, gk_acc, fcar, dc_scr):
        step = pl.program_id(0)

        @pl.when(step == 0)
        def _():
            for ref in (gq_acc, gk_acc, fcar, dbf_ref):
                ref[...] = jnp.zeros_like(ref)

        dc_scr[...] = jnp.zeros_like(dc_scr)

        def head(h, _):
            rows = _head_rows(h)
            dqb = dq_ref[h].T
            dkb = dk_ref[h]
            dc_scr[pl.ds(h, 1), :] = dqb[ONES_COL_K:ONES_COL_K + 1, :] - dkb[ONES_ROW_Q:ONES_ROW_Q + 1, :]
            for src, dsrc, gain, acc, mul, base in ((q_ref, dqb, qg_ref, gq_acc, scale, 0),
                                                    (k_ref, dkb, kg_ref, gk_acc, 1.0, D)):
                x = src[rows, :]
                rs = lax.rsqrt(jnp.mean(x * x, axis=0, keepdims=True) + EPS)
                xhat = x * rs
                dn = dsrc[0:HEAD_DIM, :] * mul
                acc[rows, :] += jnp.sum(dn * xhat, axis=1, keepdims=True)
                dxh = dn * gain[rows, :]
                dx = rs * (dxh - xhat * jnp.mean(dxh * xhat, axis=0, keepdims=True))
                du_ref[pl.ds(pl.multiple_of(base + h * HEAD_DIM, HEAD_DIM), HEAD_DIM), :] = dx.astype(BF16)
            return 0

        lax.fori_loop(0, H, head, 0, unroll=HEAD_UNROLL)
        du_ref[2 * D:3 * D, :] = dv_ref[...]
        dlf, carry = _lane_cumsum(dc_scr[...], tri_ref, fcar[...], True)
        fcar[...] = carry
        dfl = dlf * _sigmoid(-(f_ref[...] + bf_ref[...]))
        dbf_ref[...] += jnp.sum(dfl, axis=1, keepdims=True)
        du_ref[3 * D:NU, :] = dfl.astype(BF16)

        @pl.when(step == nT - 1)
        def _():
            for acc, ref in ((gq_acc, dqg_ref), (gk_acc, dkg_ref)):
                tot = jnp.zeros((HEAD_DIM, 1), F32)
                for h in range(H):
                    tot = tot + acc[h * HEAD_DIM:(h + 1) * HEAD_DIM, :]
                ref[...] = tot

    rev = lambda i: nT - 1 - i
    part = lambda j: pl.BlockSpec((D, T), lambda i: (j, rev(i)))
    colv = lambda n: pl.BlockSpec((n, 1), lambda i: (0, 0))
    return pl.pallas_call(
        body, name="fox_prep_bwd", grid=(nT,),
        in_specs=[part(0), part(1), pl.BlockSpec((LANES, T), lambda i: (3 * D // LANES, rev(i))),
                  pl.BlockSpec((H, T, HEAD_ROWS), lambda i: (0, rev(i), 0)),
                  pl.BlockSpec((H, HEAD_ROWS, T), lambda i: (0, 0, rev(i))), pl.BlockSpec((D, T), lambda i: (0, rev(i))),
                  colv(LANES), colv(D), colv(D), pl.BlockSpec((LANES, LANES), lambda i: (0, 0))],
        out_specs=[pl.BlockSpec((NU, T), lambda i: (0, rev(i))), colv(LANES), colv(HEAD_DIM), colv(HEAD_DIM)],
        out_shape=[jax.ShapeDtypeStruct((NU, S), BF16), jax.ShapeDtypeStruct((LANES, 1), F32),
                   jax.ShapeDtypeStruct((HEAD_DIM, 1), F32), jax.ShapeDtypeStruct((HEAD_DIM, 1), F32)],
        scratch_shapes=[pltpu.VMEM((D, 1), F32), pltpu.VMEM((D, 1), F32), pltpu.VMEM((LANES, 1), F32),
                        pltpu.VMEM((LANES, T), F32)],
        compiler_params=_params(("arbitrary",)),
    )(ut, ut, ut, dq, dkt, dvt, b_f, qg, kg, _tri_matrix(True))


def _block_diag_tiles(w):
    n = w.shape[0]
    per = min(MXU_DIM, n * LRU_BLOCK_DIM) // LRU_BLOCK_DIM
    eye = jnp.eye(per, dtype=w.dtype)
    w5 = w.reshape(n // per, per, LRU_BLOCK_DIM, 1, LRU_BLOCK_DIM) * eye[None, :, None, :, None]
    return w5.reshape(n // per, per * LRU_BLOCK_DIM, per * LRU_BLOCK_DIM).astype(BF16)


def _block_diag_extract(t, n):
    per = t.shape[-1] // LRU_BLOCK_DIM
    eye = jnp.eye(per, dtype=t.dtype)
    t5 = t.reshape(n // per, per, LRU_BLOCK_DIM, per, LRU_BLOCK_DIM) * eye[None, :, None, :, None]
    return t5.sum(axis=3).reshape(n, LRU_BLOCK_DIM, LRU_BLOCK_DIM)


def _local_step(x, tgt, small, wv, grad_view, comm=None):
    S, D = x.shape
    F = 4 * D
    H = D // HEAD_DIM
    nblk = D // LRU_BLOCK_DIM
    NU = 3 * D + LANES
    tq = max(LANES, min(512, S // 4))
    assert S % tq == 0
    vec = lambda a: a.reshape(1, -1).astype(F32)
    col = lambda a: a.reshape(-1, 1).astype(F32)
    mix_g, mlp_g = small["mix_norm"], small["mlp_norm"]
    conv_b = vec(small["lru_conv_b"])
    wr_bd, wi_bd = _block_diag_tiles(small["lru_w_r"][0]), _block_diag_tiles(small["lru_w_i"][0])
    b_r, b_i, lam = vec(small["lru_b_r"]), vec(small["lru_b_i"]), vec(small["lru_lambda"])
    b_f = jnp.pad(col(small["fox_b_f"]), ((0, LANES - H), (0, 0)))
    qg, kg = jnp.tile(col(small["fox_q_gain"]), (H, 1)), jnp.tile(col(small["fox_k_gain"]), (H, 1))
    X = lambda a: _View(a)
    grads = {}
    gout = functools.partial(grad_view, grads)

    def hosted(name, fn, *args):
        plan = comm.before(name, grads) if comm is not None else None
        res, side = fn(*args, plan=plan)
        if plan is not None:
            comm.after(name, side, wv)
        return res

    def hosted_mm(name, *args, **kw):
        plan = comm.before(name, grads) if comm is not None else None
        if plan is None:
            return _matmul(name, *args, **kw)
        res, side = _matmul(name, *args, plan=plan, **kw)
        comm.after(name, side, wv)
        return res

    norm_rows = _pick(S, (512, 256, 128))
    two = lambda: [_fresh(S, D, F32), _fresh(S, D, BF16)]

    def mlp_up(l, hm):
        return hosted_mm(f"mlp{l}_up", X(hm), wv[f"w1_{l}"], S, F, D, outs=[_fresh(S, F, BF16), _fresh(S, F, BF16)],
                         epilogue=_ep_relu2)

    def mlp_bwd(l, xin, hm, z, act, d, db):
        (dz,) = hosted_mm(f"mlp{l}_dact", X(db), wv[f"w2_{l}"], S, F, D, tb=True, outs=[_fresh(S, F, BF16)],
                          epilogue=_ep_drelu2, extras=[X(z)])
        (grads[f"w2_{l}"],) = _matmul(f"mlp{l}_dw2", X(act), X(db), F, D, S, ta=True, outs=[gout(f"w2_{l}")],
                                      epilogue=_ep_store)
        (grads[f"w1_{l}"],) = _matmul(f"mlp{l}_dw1", X(hm), X(dz), D, F, S, ta=True, outs=[gout(f"w1_{l}")],
                                      epilogue=_ep_store)
        return _matmul(f"mlp{l}_dhm", X(dz), wv[f"w1_{l}"], S, D, F, tb=True, outs=two(), n_sums=1,
                       epilogue=_ep_norm_bwd, extras=[X(xin), X(d)], vecs=[mlp_g[l:l + 1]], tm=norm_rows)

    (h0,) = hosted("mix0_norm", _rms_fwd, "mix0_norm", x, mix_g[0:1], S, D)
    (u0,) = hosted_mm("lru_in", X(h0), wv["lru_in"], S, 2 * D, D, outs=[_fresh(S, 2 * D, F32)], epilogue=_ep_store)
    conv_w = small["conv_w"]
    y, xc, r, ig, hs = hosted("lru_fwd", _lru_fwd, u0, conv_w, conv_b, wr_bd, b_r, wi_bd, b_i, lam, S, D)
    x1, hm0 = _matmul("lru_out", X(y), wv["lru_out"], S, D, D, outs=two(), epilogue=_ep_resid_norm, extras=[X(x)],
                      vecs=[mlp_g[0:1]], tm=norm_rows)
    z0, act0 = mlp_up(0, hm0)
    x2, h1 = hosted_mm("mlp0_down", X(act0), wv["w2_0"], S, D, F, outs=two(), epilogue=_ep_resid_norm, extras=[X(x1)],
                       vecs=[mix_g[1:2]], tm=norm_rows)
    (u1,) = _matmul("fox_in", wv["fox_in"], X(h1), NU, S, D, tb=True, outs=[_fresh(NU, S, F32)], epilogue=_ep_store)
    qat, kat, vat, ka = _fox_prep(u1, b_f, qg, kg, S, D, tq)
    o, o32, lse = hosted("attn_forward", _attn_forward, ka, qat, vat, S, D, tq)
    x3, hm1 = _matmul("fox_out", X(o), wv["fox_out"], S, D, D, ta=True, outs=two(), epilogue=_ep_resid_norm,
                      extras=[X(x2)], vecs=[mlp_g[1:2]], tm=norm_rows)
    z1, act1 = mlp_up(1, hm1)
    (x4,) = _matmul("mlp1_down", X(act1), wv["w2_1"], S, D, F, outs=[_fresh(S, D, F32)], epilogue=_ep_resid,
                    extras=[X(x3)])
    loss, d4, d4b = _loss_head(x4, tgt, S, D)

    d3, d3b, dg_mlp1 = mlp_bwd(1, x3, hm1, z1, act1, d4, d4b)
    (do,) = _matmul("fox_dout", wv["fox_out"], X(d3b), D, S, D, tb=True, outs=[_fresh(D, S, BF16)], epilogue=_ep_store)
    (grads["fox_out"],) = _matmul("fox_dwout", X(o), X(d3b), D, D, S, outs=[gout("fox_out")], epilogue=_ep_store)
    doat, doa, qat1, qa1 = hosted("fox_bwd_prep", _fox_bwd_prep, do, o32, lse, qat, S, D, tq)
    dqn, dkn, dv = hosted("attn_backward", _attn_backward, qa1, doa, qat1, doat, ka, kat, vat, S, D, tq)
    du1, dbf, dqg, dkg = _fox_prep_bwd(u1, dqn, dkn, dv, b_f, qg, kg, S, D, tq)
    (grads["fox_in"],) = _matmul("fox_dwin", X(h1), X(du1), D, NU, S, ta=True, tb=True, outs=[gout("fox_in")],
                                 epilogue=_ep_store)
    d2, d2b, dg_mix1 = hosted_mm("fox_dh", X(du1), wv["fox_in"], S, D, NU, ta=True, outs=two(), n_sums=1,
                               epilogue=_ep_norm_bwd, extras=[X(x2), X(d3)], vecs=[mix_g[1:2]], tm=norm_rows)
    d1, d1b, dg_mlp0 = mlp_bwd(0, x1, hm0, z0, act0, d2, d2b)
    (grads["lru_out"],) = _matmul("lru_dwout", X(y), X(d1b), D, D, S, ta=True, outs=[gout("lru_out")],
                                  epilogue=_ep_store)
    (dy,) = hosted_mm("lru_dout", X(d1b), wv["lru_out"], S, D, D, tb=True, outs=[_fresh(S, D, F32)],
                      epilogue=_ep_store)
    du0, dcw, dcb, dlam, dbr, dbi, dwr, dwi = hosted("lru_bwd", _lru_bwd, dy, u0, xc, r, ig, hs, conv_w, wr_bd, wi_bd,
                                                     lam, S, D)
    (grads["lru_in"],) = _matmul("lru_dwin", X(h0), X(du0), D, 2 * D, S, ta=True, outs=[gout("lru_in")],
                                 epilogue=_ep_store)
    gx, dg_mix0 = hosted_mm("lru_dh", X(du0), wv["lru_in"], S, D, 2 * D, tb=True, outs=[_fresh(S, D, F32)], n_sums=1,
                            epilogue=lambda *a: _ep_norm_bwd(*a)[::2], extras=[X(x), X(d1)], vecs=[mix_g[0:1]],
                            tm=norm_rows)

    grads.update(
        mix_norm=jnp.concatenate([dg_mix0, dg_mix1], axis=0), mlp_norm=jnp.concatenate([dg_mlp0, dg_mlp1], axis=0),
        conv_w=dcw, lru_conv_b=dcb, lru_w_r=_block_diag_extract(dwr, nblk)[None], lru_b_r=dbr.reshape(1, nblk, -1),
        lru_w_i=_block_diag_extract(dwi, nblk)[None], lru_b_i=dbi.reshape(1, nblk, -1), lru_lambda=dlam,
        fox_b_f=dbf[:H].reshape(1, H), fox_q_gain=dqg.reshape(1, -1), fox_k_gain=dkg.reshape(1, -1))
    return loss, gx, grads


def _place():
    x, y, c = lax.axis_index("x"), lax.axis_index("y"), lax.axis_index("c")
    chips = [(1 - x, y), (x, 1 - y), (1 - x, 1 - y)]
    return x, y, c, 2 * x + y, chips


BOUNCE_BYTES = 1 << 20


def _bounce_shape(rows, cols, dtype):
    chunk = rows
    while chunk % 2 == 0 and chunk > 16 and chunk * cols * jnp.dtype(dtype).itemsize > BOUNCE_BYTES:
        chunk //= 2
    return pltpu.VMEM((2, chunk, cols), dtype)


def _bounce_copy(src, dst, buf, sem):
    chunk = buf.shape[1]
    n = src.shape[0] // chunk
    cin = lambda i: pltpu.make_async_copy(src.at[pl.ds(i * chunk, chunk)], buf.at[i % 2], sem.at[i % 2])
    cout = lambda i: pltpu.make_async_copy(buf.at[i % 2], dst.at[pl.ds(i * chunk, chunk)], sem.at[2 + i % 2])
    cin(0).start()
    for i in range(n):
        cin(i).wait()
        if i + 1 < n:
            if i >= 1:
                cout(i - 1).wait()
            cin(i + 1).start()
        cout(i).start()
    if n >= 2:
        cout(n - 2).wait()
    cout(n - 1).wait()


def _hbm_call(body, name, arrays, out_shape, n_dma_sems, bounce=()):
    scratch = [pltpu.SemaphoreType.DMA((k,)) for k in n_dma_sems]
    for rows, cols, dtype in bounce:
        scratch += [_bounce_shape(rows, cols, dtype), pltpu.SemaphoreType.DMA((4,))]
    return pl.pallas_call(
        body, name=name, in_specs=[ANY] * len(arrays), out_specs=[ANY] * len(out_shape), out_shape=out_shape,
        scratch_shapes=scratch,
        compiler_params=pltpu.CompilerParams(has_side_effects=True, vmem_limit_bytes=VMEM_LIMIT),
    )(*arrays)


class _Gather:
    def __init__(self, shards):
        n = self.n = len(shards)
        self.operands = list(shards)
        self.out_shape = [jax.ShapeDtypeStruct((N_CHIPS,) + tuple(a.shape), a.dtype) for a in shards]
        self.scratch = [pltpu.SemaphoreType.DMA((3 * n,)) for _ in range(4)]
        for a in shards:
            self.scratch += [_bounce_shape(a.shape[0], a.shape[1], a.dtype), pltpu.SemaphoreType.DMA((4,))]

    def _copies(self, ins, outs, scr):
        send, recv, fsend, frecv = scr[:4]
        x, y, c, s, chips = _place()

        def rows(a, chip_idx, which):
            hr = ins[a].shape[0] // 2
            return outs[a].at[chip_idx, pl.ds(which * hr, hr)]

        def landed(a, j, core):
            return rows(a, 2 * chips[j][0] + chips[j][1], core)

        def ici(a, j, mine):
            hr = ins[a].shape[0] // 2
            src, dst = (ins[a].at[pl.ds(c * hr, hr)], rows(a, s, c)) if mine else (landed(a, j, c),) * 2
            return pltpu.make_async_remote_copy(src_ref=src, dst_ref=dst, send_sem=send.at[3 * a + j],
                                                recv_sem=recv.at[3 * a + j], device_id=(*chips[j], c),
                                                device_id_type=MESH)

        def d2d(a, j, mine):
            ref = landed(a, j, c if mine else 1 - c)
            return pltpu.make_async_remote_copy(src_ref=ref, dst_ref=ref, send_sem=fsend.at[3 * a + j],
                                                recv_sem=frecv.at[3 * a + j], device_id=(x, y, 1 - c),
                                                device_id_type=MESH)

        return ici, d2d, s

    def start(self, ins, outs, scr):
        ici, _, _ = self._copies(ins, outs, scr)
        for a in range(self.n):
            for j in range(3):
                ici(a, j, True).start()

    def middle(self, ins, outs, scr):
        ici, d2d, s = self._copies(ins, outs, scr)
        for a in range(self.n):
            _bounce_copy(ins[a], outs[a].at[s], scr[4 + 2 * a], scr[5 + 2 * a])
        for a in range(self.n):
            for j in range(3):
                ici(a, j, False).wait_recv()
                d2d(a, j, True).start()

    def finish(self, ins, outs, scr):
        ici, d2d, _ = self._copies(ins, outs, scr)
        for a in range(self.n):
            for j in range(3):
                d2d(a, j, False).wait_recv()
        for a in range(self.n):
            for j in range(3):
                ici(a, j, True).wait_send()
                d2d(a, j, True).wait_send()


def _run_plan(name, plan):
    k_in, k_out = len(plan.operands), len(plan.out_shape)

    def body(*refs):
        parts = (refs[:k_in], refs[k_in:k_in + k_out], refs[k_in + k_out:])
        plan.start(*parts)
        plan.middle(*parts)
        plan.finish(*parts)

    return pl.pallas_call(
        body, name=name, in_specs=[ANY] * k_in, out_specs=[ANY] * k_out, out_shape=plan.out_shape,
        scratch_shapes=plan.scratch,
        compiler_params=pltpu.CompilerParams(has_side_effects=True, vmem_limit_bytes=VMEM_LIMIT),
    )(*plan.operands)


def _hosted_call(body, name, grid, in_specs, out_specs, out_shape, scratch_shapes, operands, sem, plan=None):
    if plan is None:
        res = pl.pallas_call(body, name=name, grid=grid, in_specs=in_specs, out_specs=out_specs, out_shape=out_shape,
                             scratch_shapes=scratch_shapes, compiler_params=_params(sem))(*operands)
        return res, None
    n_in, n_out, n_scr = len(in_specs), len(out_specs), len(scratch_shapes)
    k_in, k_out = len(plan.operands), len(plan.out_shape)
    total = int(np.prod(grid))
    late = max(0, total - 1 - max(1, total // 8))

    def hosted(*refs):
        ins, refs = refs[:n_in], refs[n_in:]
        p_ins, refs = refs[:k_in], refs[k_in:]
        outs, refs = refs[:n_out], refs[n_out:]
        p_outs, refs = refs[:k_out], refs[k_out:]
        scr, p_scr = refs[:n_scr], refs[n_scr:]
        step = pl.program_id(0)
        for d in range(1, len(grid)):
            step = step * grid[d] + pl.program_id(d)
        pl.when(step == 0)(lambda: plan.start(p_ins, p_outs, p_scr))
        body(*ins, *outs, *scr)
        pl.when(step == late)(lambda: plan.middle(p_ins, p_outs, p_scr))
        pl.when(step == total - 1)(lambda: plan.finish(p_ins, p_outs, p_scr))

    res = pl.pallas_call(
        hosted, name=name, grid=grid, in_specs=list(in_specs) + [ANY] * k_in, out_specs=list(out_specs) + [ANY] * k_out,
        out_shape=list(out_shape) + plan.out_shape, scratch_shapes=list(scratch_shapes) + plan.scratch,
        compiler_params=pltpu.CompilerParams(dimension_semantics=sem, vmem_limit_bytes=VMEM_LIMIT,
                                             has_side_effects=True),
    )(*operands, *plan.operands)
    return res[:n_out], res[n_out:]


def _all_gather(name, shards):
    return _run_plan(name, _Gather(shards))


class _Swap:
    def __init__(self, arrs):
        self.n = len(arrs)
        self.operands = list(arrs)
        self.out_shape = [jax.ShapeDtypeStruct((a.shape[0], a.shape[1] // 2, a.shape[2]), a.dtype) for a in arrs]
        self.scratch = [pltpu.SemaphoreType.DMA((self.n,)) for _ in range(2)]

    def _copy(self, ins, outs, scr, a):
        x, y, c, _, _ = _place()
        hr = ins[a].shape[1] // 2
        return pltpu.make_async_remote_copy(
            src_ref=ins[a].at[:, pl.ds((1 - c) * hr, hr)], dst_ref=outs[a], send_sem=scr[0].at[a],
            recv_sem=scr[1].at[a], device_id=(x, y, 1 - c), device_id_type=MESH)

    def start(self, ins, outs, scr):
        for a in range(self.n):
            self._copy(ins, outs, scr, a).start()

    def middle(self, ins, outs, scr):
        pass

    def finish(self, ins, outs, scr):
        for a in range(self.n):
            self._copy(ins, outs, scr, a).wait()


class _Scatter:
    def __init__(self, parts):
        n = self.n = len(parts)
        self.operands = list(parts)
        self.out_shape = [jax.ShapeDtypeStruct(a.shape, a.dtype) for a in parts]
        self.scratch = [pltpu.SemaphoreType.DMA((3 * n,)) for _ in range(2)]
        for a in parts:
            self.scratch += [_bounce_shape(a.shape[1], a.shape[2], a.dtype), pltpu.SemaphoreType.DMA((4,))]

    def _copy(self, ins, outs, scr, a, j, mine):
        x, y, c, s, chips = _place()
        t = 2 * chips[j][0] + chips[j][1]
        return pltpu.make_async_remote_copy(
            src_ref=ins[a].at[t], dst_ref=outs[a].at[s if mine else t], send_sem=scr[0].at[3 * a + j],
            recv_sem=scr[1].at[3 * a + j], device_id=(*chips[j], c), device_id_type=MESH)

    def start(self, ins, outs, scr):
        for a in range(self.n):
            for j in range(3):
                self._copy(ins, outs, scr, a, j, True).start()

    def middle(self, ins, outs, scr):
        s = _place()[3]
        for a in range(self.n):
            _bounce_copy(ins[a].at[s], outs[a].at[s], scr[2 + 2 * a], scr[3 + 2 * a])

    def finish(self, ins, outs, scr):
        for a in range(self.n):
            for j in range(3):
                self._copy(ins, outs, scr, a, j, False).wait_recv()
        for a in range(self.n):
            for j in range(3):
                self._copy(ins, outs, scr, a, j, True).wait_send()


def _pair_gather(name, halves):
    n = len(halves)

    def body(*refs):
        ins, outs = refs[:n], refs[n:2 * n]
        send, recv = refs[2 * n:2 * n + 2]
        stage = refs[2 * n + 2:]
        x, y, c, _, _ = _place()
        cps = []
        for a in range(n):
            hr = ins[a].shape[0]
            cp = pltpu.make_async_remote_copy(
                src_ref=ins[a], dst_ref=outs[a].at[pl.ds(c * hr, hr)], send_sem=send.at[a], recv_sem=recv.at[a],
                device_id=(x, y, 1 - c), device_id_type=MESH)
            cp.start()
            cps.append((cp, hr))
        for a, (cp, hr) in enumerate(cps):
            _bounce_copy(ins[a], outs[a].at[pl.ds(c * hr, hr)], stage[2 * a], stage[2 * a + 1])
        for a, (cp, hr) in enumerate(cps):
            cp.wait_send()
            theirs = outs[a].at[pl.ds((1 - c) * hr, hr)]
            pltpu.make_async_remote_copy(src_ref=theirs, dst_ref=theirs, send_sem=send.at[a], recv_sem=recv.at[a],
                                         device_id=(x, y, 1 - c), device_id_type=MESH).wait_recv()

    out_shape = [jax.ShapeDtypeStruct((2 * a.shape[0], a.shape[1]), a.dtype) for a in halves]
    return _hbm_call(body, name, halves, out_shape, (n, n),
                     bounce=[(a.shape[0], a.shape[1], a.dtype) for a in halves])


def _row_tile(rows, cols, itemsize, n_bufs):
    budget = VMEM_LIMIT // 2
    for t in (1024, 512, 256, 128, 64, 32, 16):
        if rows % t == 0 and 2 * n_bufs * t * cols * itemsize <= budget:
            return t
    return rows


def _pair_add(name, g, gsib, core, out_dtype):
    _, r, cols = g.shape
    hr = r // 2
    t = _row_tile(hr, cols, 4, 3)
    per = hr // t

    def body(core_ref, a_ref, b_ref, o_ref):
        o_ref[...] = (a_ref[...].astype(F32) + b_ref[...].astype(F32)).astype(o_ref.dtype)

    grid_spec = pltpu.PrefetchScalarGridSpec(
        num_scalar_prefetch=1, grid=(N_CHIPS, per),
        in_specs=[pl.BlockSpec((None, t, cols), lambda s, i, core: (s, core[0] * per + i, 0)),
                  pl.BlockSpec((None, t, cols), lambda s, i, core: (s, i, 0))],
        out_specs=pl.BlockSpec((None, t, cols), lambda s, i, core: (s, i, 0)))
    return pl.pallas_call(body, name=name, grid_spec=grid_spec,
                          out_shape=jax.ShapeDtypeStruct((N_CHIPS, hr, cols), out_dtype),
                          compiler_params=_params(("arbitrary", "arbitrary")))(core, g, gsib)


def _chip_sum(name, parts):
    _, hr, cols = parts.shape
    t = _row_tile(hr, cols, 4, 5)

    def body(p_ref, o_ref):
        o_ref[...] = ((p_ref[0].astype(F32) + p_ref[1].astype(F32)) + p_ref[2].astype(F32)) + p_ref[3].astype(F32)

    return pl.pallas_call(
        body, name=name, grid=(hr // t,), in_specs=[pl.BlockSpec((N_CHIPS, t, cols), lambda i: (0, i, 0))],
        out_specs=pl.BlockSpec((t, cols), lambda i: (i, 0)), out_shape=jax.ShapeDtypeStruct((hr, cols), F32),
        compiler_params=_params(("arbitrary",)))(parts)


def _pair_partials(tag, arrs, sib, wire_dtypes, core):
    return _Scatter([_pair_add(f"{tag}_pair_add{i}", g, gs, core, dt)
                     for i, (g, gs, dt) in enumerate(zip(arrs, sib, wire_dtypes))])


def _finish_reduce(tag, scattered):
    halves = [_chip_sum(f"{tag}_chip_sum{i}", p) for i, p in enumerate(scattered)]
    return _pair_gather(f"{tag}_pair_gather", halves)


def _adamw(name, w, g_parts, m, v):
    thin = w.ndim == 3
    rows, cols = w.shape[0], w.shape[-1]
    n_parts = len(g_parts)
    part_rows = rows // n_parts
    t = max(d for d in range(1, 257) if part_rows % d == 0) if thin else _row_tile(part_rows, cols, 4, 7 + n_parts)
    per = part_rows // t
    c1 = 1.0 - ADAM_B1 ** ADAM_STEP
    c2 = 1.0 - ADAM_B2 ** ADAM_STEP

    def body(w_ref, m_ref, v_ref, *refs):
        g_refs, (go_ref, d_ref, nm_ref, nv_ref) = refs[:n_parts], refs[n_parts:]
        g = g_refs[0][...]
        for k in range(1, n_parts):
            g = jnp.where(pl.program_id(0) >= k * per, g_refs[k][...], g)
        go_ref[...] = g
        m = ADAM_B1 * m_ref[...] + (1.0 - ADAM_B1) * g
        v = ADAM_B2 * v_ref[...] + (1.0 - ADAM_B2) * (g * g)
        nm_ref[...] = m
        nv_ref[...] = v
        d_ref[...] = -ADAM_LR * ((m / c1) / (jnp.sqrt(v / c2) + ADAM_EPS) + ADAM_WD * w_ref[...])

    block = (t, 1, cols) if thin else (t, cols)
    at = lambda r: (r, 0, 0) if thin else (r, 0)
    spec = pl.BlockSpec(block, lambda i: at(i))
    g_specs = [pl.BlockSpec(block, lambda i, k=k: at(jnp.clip(i - k * per, 0, per - 1))) for k in range(n_parts)]
    shp = jax.ShapeDtypeStruct(w.shape, F32)
    return pl.pallas_call(body, name=name, grid=(rows // t,), in_specs=[spec] * 3 + g_specs, out_specs=[spec] * 4,
                          out_shape=[shp] * 4, compiler_params=_params(("arbitrary",)))(w, m, v, *g_parts)


_WEIGHTS = ["mix_norm", "mlp_norm", "mlp_w1", "mlp_w2", "lru_w_in", "lru_conv_w", "lru_conv_b", "lru_w_r", "lru_b_r",
            "lru_w_i", "lru_b_i", "lru_lambda", "lru_w_out", "fox_w_in", "fox_b_f", "fox_q_gain", "fox_k_gain",
            "fox_w_out"]
_REPLICATED = ["mix_norm", "mlp_norm", "lru_conv_b", "lru_w_r", "lru_b_r", "lru_w_i", "lru_b_i", "lru_lambda",
               "fox_b_f", "fox_q_gain", "fox_k_gain"]
_PACK_TILE = 2 * SUBLANES * LANES


def _as2d(a):
    return a.reshape(-1, a.shape[-1])


def kernel(x, mix_norm, mlp_norm, mlp_w1, mlp_w2, lru_w_in, lru_conv_w, lru_conv_b, lru_w_r, lru_b_r, lru_w_i, lru_b_i, lru_lambda, lru_w_out, fox_w_in, fox_b_f, fox_q_gain, fox_k_gain, fox_w_out, loss_target, m_mix_norm, m_mlp_norm, m_mlp_w1, m_mlp_w2, m_lru_w_in, m_lru_conv_w, m_lru_conv_b, m_lru_w_r, m_lru_b_r, m_lru_w_i, m_lru_b_i, m_lru_lambda, m_lru_w_out, m_fox_w_in, m_fox_b_f, m_fox_q_gain, m_fox_k_gain, m_fox_w_out, v_mix_norm, v_mlp_norm, v_mlp_w1, v_mlp_w2, v_lru_w_in, v_lru_conv_w, v_lru_conv_b, v_lru_w_r, v_lru_b_r, v_lru_w_i, v_lru_b_i, v_lru_lambda, v_lru_w_out, v_fox_w_in, v_fox_b_f, v_fox_q_gain, v_fox_k_gain, v_fox_w_out):
    args = dict(locals())
    W = {n: args[n] for n in _WEIGHTS}
    Mo = {n: args["m_" + n] for n in _WEIGHTS}
    Vo = {n: args["v_" + n] for n in _WEIGHTS}
    S, D = x.shape[1], x.shape[2]
    F = 4 * D
    H = D // HEAD_DIM
    NU = 3 * D + LANES
    FQ, DQ = F // N_CHIPS, D // N_CHIPS
    nfox = fox_w_in.shape[-1]
    chip = 2 * lax.axis_index("x") + lax.axis_index("y")
    core = lax.axis_index("c").astype(jnp.int32).reshape(1)

    cw_flat = jnp.pad(lru_conv_w.reshape(-1), (0, _PACK_TILE - CONV_WIDTH * DQ)).reshape(2 * SUBLANES, LANES)
    w1s, w2s = mlp_w1.astype(BF16), mlp_w2.astype(BF16)
    wv = {}
    small = {n: W[n] for n in _REPLICATED}
    scattered = {}
    members = {"g1": ["w2_1", "w1_1", "fox_out"], "g2": ["fox_in"], "g3": ["w2_0", "w1_0"], "g4": ["lru_out", "lru_in"]}
    swap_at = {"fox_bwd_prep": "g1", "fox_dh": "g2", "lru_dout": "g3"}
    scatter_at = {"attn_backward": "g1", "mlp0_dact": "g2", "lru_bwd": "g3", "lru_dh": "g4"}
    swapped = {}

    def shard_major(name, g):
        if name == "fox_in":
            return jnp.transpose(g[:, :nfox * N_CHIPS].reshape(D, N_CHIPS, nfox), (1, 0, 2))
        return g

    class Comm:
        @staticmethod
        def before(name, grads):
            if name == "mix0_norm":
                return _Gather([lru_w_in[0].astype(BF16)])
            if name == "lru_in":
                return _Gather([lru_w_out[0].astype(BF16), cw_flat])
            if name == "lru_fwd":
                return _Gather([w1s[0]])
            if name == "mlp0_up":
                return _Gather([w2s[0]])
            if name == "mlp0_down":
                return _Gather([fox_w_in[0].astype(BF16)])
            if name == "attn_forward":
                return _Gather([fox_w_out[0].astype(BF16), w1s[1], w2s[1]])
            if name in swap_at:
                group = swap_at[name]
                swapped[group] = [[shard_major(n, grads[n]) for n in members[group]], None]
                return _Swap(swapped[group][0])
            if name in scatter_at:
                group = scatter_at[name]
                if group not in swapped:
                    arrs = [shard_major(n, grads[n]) for n in members[group]]
                    swapped[group] = [arrs, _run_plan(f"{group}_pair_swap", _Swap(arrs))]
                arrs, sib = swapped[group]
                return _pair_partials(group, arrs, sib, [BF16] * len(arrs), core)
            return None

        @staticmethod
        def after(name, res, wv):
            if name == "mix0_norm":
                wv.update(lru_in=_View(res[0], "cs"))
            elif name == "lru_in":
                wv.update(lru_out=_View(res[0], "rs"))
                taps = res[1].reshape(N_CHIPS, -1)[:, :CONV_WIDTH * DQ].reshape(N_CHIPS, CONV_WIDTH, DQ)
                small["conv_w"] = jnp.transpose(taps, (1, 0, 2)).reshape(CONV_WIDTH, D)
            elif name == "lru_fwd":
                wv.update(w1_0=_View(res[0], "cs"))
            elif name == "mlp0_up":
                wv.update(w2_0=_View(res[0], "rs"))
            elif name == "mlp0_down":
                fox_full = jnp.concatenate([res[0][s] for s in range(N_CHIPS)], axis=1)
                fox_full = jnp.pad(fox_full, ((0, 0), (0, NU - fox_full.shape[1])))
                wv.update(fox_in=_View(fox_full.T))
            elif name == "attn_forward":
                wv.update(fox_out=_View(res[0], "rs"), w1_1=_View(res[1], "cs"), w2_1=_View(res[2], "rs"))
            elif name in swap_at:
                swapped[swap_at[name]][1] = res
            else:
                scattered.update(zip(members[scatter_at[name]], res))

    def grad_view(grads, name):
        if name in ("w1_0", "w1_1"):
            return _View(None, "cs", shape=(N_CHIPS, D, FQ), dtype=BF16)
        if name in ("w2_0", "w2_1"):
            return _View(None, "rs", shape=(N_CHIPS, FQ, D), dtype=BF16)
        if name == "lru_in":
            return _View(None, "cs", shape=(N_CHIPS, D, 2 * D // N_CHIPS), dtype=BF16)
        if name in ("lru_out", "fox_out"):
            return _View(None, "rs", shape=(N_CHIPS, DQ, D), dtype=BF16)
        return _View(None, shape=(D, NU), dtype=BF16)

    loss, gx, grads = _local_step(x[0], loss_target[0], small, wv, grad_view, Comm)

    pack_names = _REPLICATED + ["conv_w"]
    flat = jnp.concatenate([grads[n].reshape(-1).astype(F32) for n in pack_names] + [loss.reshape(-1)])
    per_chip = -(-flat.shape[0] // (N_CHIPS * _PACK_TILE)) * _PACK_TILE
    pack = jnp.pad(flat, (0, N_CHIPS * per_chip - flat.shape[0])).reshape(N_CHIPS, per_chip // LANES, LANES)
    pack_sib = _run_plan("pack_pair_swap", _Swap([pack]))
    (scattered["pack"],) = _run_plan("pack_chip_scatter", _pair_partials("pack", [pack], pack_sib, [F32], core))
    order = ["w1_0", "w1_1", "w2_0", "w2_1", "lru_in", "lru_out", "fox_in", "fox_out", "pack"]
    red = dict(zip(order, _finish_reduce("grads", [scattered[n] for n in order])))
    (all_pack,) = _all_gather("gather_small_grads", [red["pack"]])
    all_flat = all_pack.reshape(-1)
    G = {}
    off = 0
    for n in pack_names:
        shape = grads[n].shape if n == "conv_w" else W[n].shape
        size = int(np.prod(shape))
        G[n] = all_flat[off:off + size].reshape(shape)
        off += size
    total = all_flat[off]
    G["lru_conv_w"] = lax.dynamic_slice_in_dim(G.pop("conv_w"), chip * DQ, DQ, axis=1)[None]
    parts = {n: [_as2d(G[n])] for n in G}
    parts.update(mlp_w1=[red["w1_0"], red["w1_1"]], mlp_w2=[red["w2_0"], red["w2_1"]], lru_w_in=[red["lru_in"]],
                 lru_w_out=[red["lru_out"]], fox_w_in=[red["fox_in"]], fox_w_out=[red["fox_out"]])

    delta, new_m, new_v = {}, {}, {}
    for n in _WEIGHTS:
        if W[n].shape[-1] % LANES and W[n].shape[-2] % LANES == 0:
            to_thin = lambda a: jnp.transpose(a, (2, 0, 1))
            res = _adamw(f"adamw_{n}", to_thin(W[n]), [to_thin(parts[n][0][None])], to_thin(Mo[n]), to_thin(Vo[n]))
            G[n], delta[n], new_m[n], new_v[n] = (jnp.transpose(t, (1, 2, 0)) for t in res)
            continue
        go, d, nm, nv = _adamw(f"adamw_{n}", _as2d(W[n]), parts[n], _as2d(Mo[n]), _as2d(Vo[n]))
        G[n], delta[n], new_m[n], new_v[n] = (t.reshape(W[n].shape) for t in (go, d, nm, nv))

    return (total, gx[None], *[G[n] for n in _WEIGHTS], *[delta[n] for n in _WEIGHTS],
            *[new_m[n] for n in _WEIGHTS], *[new_v[n] for n in _WEIGHTS])
```

```python
import functools

import numpy as np
import jax
import jax.numpy as jnp
from jax import lax
from jax.experimental import pallas as pl
from jax.experimental.pallas import tpu as pltpu

F32 = jnp.float32
BF16 = jnp.bfloat16

HEAD_DIM = 64
LRU_BLOCK_DIM = 64
CONV_WIDTH = 4
LRU_C = 8.0
EPS = 1e-6
NEG_INF = -1e30
ADAM_LR = 0.001
ADAM_B1 = 0.9
ADAM_B2 = 0.999
ADAM_EPS = 1e-08
ADAM_WD = 0.01
ADAM_STEP = 10

N_CHIPS = 4
LANES = 128
SUBLANES = 8
MXU_DIM = 256
VMEM_LIMIT = 52 * 1024 * 1024
MESH = pl.DeviceIdType.MESH
ANY = pl.BlockSpec(memory_space=pl.ANY)


def _pick(n, prefs):
    for p in prefs:
        if p <= n and n % p == 0:
            return p
    return n


def _params(sem=None):
    return pltpu.CompilerParams(dimension_semantics=sem, vmem_limit_bytes=VMEM_LIMIT)


class _View:
    def __init__(self, arr, kind="plain", r0=0, rows=None, shape=None, dtype=None):
        self.arr = arr
        self.kind = kind
        self.r0 = r0
        self.shape = tuple(arr.shape) if arr is not None else tuple(shape)
        self.dtype = arr.dtype if arr is not None else dtype
        self.rows = rows if rows is not None else self.shape[-2]

    def limits(self):
        if self.kind == "plain":
            return 0, 0
        rows = int(np.gcd(self.rows, self.r0))
        return rows, (self.shape[-1] if self.kind == "cs" else 0)

    def spec(self, br, bc, fr, fc):
        if self.kind == "plain":
            return pl.BlockSpec((br, bc), lambda *g: (fr(*g), fc(*g)))
        ncol = self.shape[-1]
        r0b = self.r0 // br
        assert self.r0 % br == 0 and self.rows % br == 0 and ncol % bc == 0, (self.shape, self.r0, br, bc)
        if self.kind == "cs":
            per = ncol // bc
            return pl.BlockSpec((None, br, bc), lambda *g: (fc(*g) // per, r0b + fr(*g), fc(*g) % per))
        per = self.rows // br
        return pl.BlockSpec((None, br, bc), lambda *g: (fr(*g) // per, r0b + fr(*g) % per, fc(*g)))


def _bf(x):
    return x if x.dtype == BF16 else x.astype(BF16)


def _matmul(name, A, B, M, N, K, *, ta=False, tb=False, outs, epilogue, extras=(), vecs=(), n_sums=0,
            tm=None, tn=None, tk=None, plan=None):
    lim = {"m": [M], "n": [N], "k": [K]}
    for view, (rdim, cdim) in ([(A, "km" if ta else "mk"), (B, "nk" if tb else "kn")]
                               + [(e, "mn") for e in extras] + [(o, "mn") for o in outs]):
        r_lim, c_lim = view.limits()
        lim[rdim].append(r_lim)
        lim[cdim].append(c_lim)
    tm = tm or _pick(int(np.gcd.reduce(lim["m"])), (1024, 640, 512, 256, 128))
    tn = tn or _pick(int(np.gcd.reduce(lim["n"])), (1024, 640, 512, 256, 128))
    tk = tk or _pick(int(np.gcd.reduce(lim["k"])), (1024, 640, 512, 256, 128))
    nk = K // tk
    gi = lambda i, j, k: i
    gj = lambda i, j, k: j
    gk = lambda i, j, k: k
    a_spec = A.spec(tk, tm, gk, gi) if ta else A.spec(tm, tk, gi, gk)
    b_spec = B.spec(tn, tk, gj, gk) if tb else B.spec(tk, tn, gk, gj)
    ca = 0 if ta else 1
    cb = 1 if tb else 0
    ne, no = len(extras) + len(vecs), len(outs)
    assert n_sums == 0 or tn == N
    row_spec = pl.BlockSpec((1, tn), lambda i, j, k: (0, j))
    in_specs = [a_spec, b_spec] + [e.spec(tm, tn, gi, gj) for e in extras] + [row_spec] * len(vecs)
    operands = [A.arr, B.arr] + [e.arr for e in extras] + list(vecs)
    out_specs = [o.spec(tm, tn, gi, gj) for o in outs] + [row_spec] * n_sums
    out_shape = ([jax.ShapeDtypeStruct(o.shape, o.dtype) for o in outs]
                 + [jax.ShapeDtypeStruct((1, N), F32)] * n_sums)

    def body(*refs):
        a_ref, b_ref = refs[0], refs[1]
        ex = refs[2:2 + ne]
        o_refs = refs[2 + ne:2 + ne + no]
        s_refs = refs[2 + ne + no:2 + ne + no + n_sums]
        first_row_tile = pl.program_id(0) == 0

        def prod():
            return lax.dot_general(_bf(a_ref[...]), _bf(b_ref[...]), (((ca,), (cb,)), ((), ())),
                                   preferred_element_type=F32)

        def finish(acc):
            res = epilogue(acc, *[e[...] for e in ex])
            for o_ref, r in zip(o_refs, res[:no]):
                o_ref[...] = r.astype(o_ref.dtype)
            for s_ref, r in zip(s_refs, res[no:]):
                def assign(s_ref=s_ref, r=r):
                    s_ref[...] = r

                def accumulate(s_ref=s_ref, r=r):
                    s_ref[...] += r

                pl.when(first_row_tile)(assign)
                pl.when(jnp.logical_not(first_row_tile))(accumulate)

        if nk == 1:
            finish(prod())
        else:
            acc_ref = refs[-1]
            k = pl.program_id(2)

            @pl.when(k == 0)
            def _():
                acc_ref[...] = jnp.zeros_like(acc_ref)

            acc_ref[...] += prod()

            @pl.when(k == nk - 1)
            def _():
                finish(acc_ref[...])

    res, side = _hosted_call(body, name, (M // tm, N // tn, nk), in_specs, out_specs, out_shape,
                             [pltpu.VMEM((tm, tn), F32)] if nk > 1 else [], operands,
                             ("arbitrary", "arbitrary", "arbitrary"), plan)
    return res if plan is None else (res, side)


def _ep_store(acc):
    return (acc,)


def _ep_resid(acc, res):
    return (res + acc,)


def _ep_resid_norm(acc, res, g):
    xo = res + acc
    r = lax.rsqrt(jnp.mean(xo * xo, axis=-1, keepdims=True) + EPS)
    return (xo, (xo * r) * g)


def _ep_norm_bwd(acc, x, dres, g):
    r = lax.rsqrt(jnp.mean(x * x, axis=-1, keepdims=True) + EPS)
    xhat = x * r
    dxn = acc * g
    tot = dres + r * (dxn - xhat * jnp.mean(dxn * xhat, axis=-1, keepdims=True))
    return (tot, tot, jnp.sum(acc * xhat, axis=0, keepdims=True))


def _ep_relu2(acc):
    zp = jnp.maximum(acc, 0.0)
    return (zp * zp,)


def _ep_drelu2(acc, act):
    return (acc * (2.0 * jnp.sqrt(act.astype(F32))),)


def _fresh(M, N, dtype):
    return _View(None, shape=(M, N), dtype=dtype)


def _rms_fwd(name, x, g, S, D, plan=None):
    T = _pick(S, (512, 256, 128))

    def body(x_ref, g_ref, h_ref):
        x = x_ref[...]
        r = lax.rsqrt(jnp.mean(x * x, axis=-1, keepdims=True) + EPS)
        h_ref[...] = ((x * r) * g_ref[...]).astype(BF16)

    return _hosted_call(body, name, (S // T,),
                        [pl.BlockSpec((T, D), lambda i: (i, 0)), pl.BlockSpec((1, D), lambda i: (0, 0))],
                        [pl.BlockSpec((T, D), lambda i: (i, 0))], [jax.ShapeDtypeStruct((S, D), BF16)], [], (x, g),
                        ("arbitrary",), plan)


def _loss_head(x, tgt, S, D):
    T = _pick(S, (512, 256, 128))

    def body(x_ref, t_ref, loss_ref, d_ref, db_ref):
        @pl.when(pl.program_id(0) == 0)
        def _():
            loss_ref[...] = jnp.zeros_like(loss_ref)

        e = x_ref[...] - t_ref[...]
        loss_ref[...] += 0.5 * jnp.sum(jnp.mean(e * e, axis=-1, keepdims=True), axis=0, keepdims=True)
        d = e * (1.0 / D)
        d_ref[...] = d
        db_ref[...] = d.astype(BF16)

    row = pl.BlockSpec((T, D), lambda i: (i, 0))
    return pl.pallas_call(
        body, name="loss_head", grid=(S // T,), in_specs=[row, row],
        out_specs=[pl.BlockSpec((1, 1), lambda i: (0, 0)), row, row],
        out_shape=[jax.ShapeDtypeStruct((1, 1), F32), jax.ShapeDtypeStruct((S, D), F32),
                   jax.ShapeDtypeStruct((S, D), BF16)],
        compiler_params=_params(("arbitrary",)),
    )(x, tgt)


def _sigmoid(z):
    return 1.0 / (1.0 + jnp.exp(-z))


def _log_sigmoid(z):
    return jnp.minimum(z, 0.0) - jnp.log(1.0 + jnp.exp(-jnp.abs(z)))


_GELU_K = 0.7978845608028654
_GELU_C = 0.044715


def _gelu(x):
    t = jnp.tanh(_GELU_K * (x + _GELU_C * (x * x * x)))
    return 0.5 * x * (1.0 + t)


def _gelu_and_grad(x):
    x2 = x * x
    t = jnp.tanh(_GELU_K * (x + _GELU_C * (x2 * x)))
    g = 0.5 * x * (1.0 + t)
    dg = 0.5 * (1.0 + t) + 0.5 * x * (1.0 - t * t) * (_GELU_K * (1.0 + 3.0 * _GELU_C * x2))
    return g, dg


def _decay_terms(r, ls):
    la = LRU_C * r * ls
    a = jnp.exp(la)
    a2 = a * a
    mult = jnp.sqrt(-jnp.tanh(la) * (a2 + 1.0))
    return a, a2, mult


def _lru_fwd(u0, conv_w, conv_b, wr_bd, b_r, wi_bd, b_i, lam, S, D, plan=None):
    T = _pick(S, (256, 128))
    GT = wr_bd.shape[-1]
    nG = D // GT

    def body(gb_ref, xb_ref, cw_ref, cb_ref, wr_ref, br_ref, wi_ref, bi_ref, lam_ref,
             y_ref, xc_ref, r_ref, i_ref, hs_ref, ext, a_scr, hcar):
        @pl.when(pl.program_id(0) == 0)
        def _():
            ext[0:SUBLANES, :] = jnp.zeros((SUBLANES, D), F32)
            hcar[...] = jnp.zeros_like(hcar)

        xb = xb_ref[...]
        ext[SUBLANES:SUBLANES + T, :] = xb
        xc = cb_ref[...]
        for k in range(CONV_WIDTH):
            xc = xc + ext[pl.ds(SUBLANES - (CONV_WIDTH - 1) + k, T), :] * cw_ref[k:k + 1, :]
        ext[0:SUBLANES, :] = xb[T - SUBLANES:T, :]
        xc_ref[...] = xc
        xcb = xc.astype(BF16)
        for g in range(nG):
            sl = slice(g * GT, (g + 1) * GT)
            zr = jnp.dot(xcb[:, sl], wr_ref[g], preferred_element_type=F32) + br_ref[:, sl]
            zi = jnp.dot(xcb[:, sl], wi_ref[g], preferred_element_type=F32) + bi_ref[:, sl]
            r_ref[:, sl] = _sigmoid(zr)
            i_ref[:, sl] = _sigmoid(zi)
        r = r_ref[...]
        a, _, mult = _decay_terms(r, _log_sigmoid(lam_ref[...]))
        a_scr[...] = a
        hs_ref[...] = mult * (i_ref[...] * xc)

        def step(t, h):
            h = a_scr[pl.ds(t, 1), :] * h + hs_ref[pl.ds(t, 1), :]
            hs_ref[pl.ds(t, 1), :] = h
            return h

        hcar[...] = lax.fori_loop(0, T, step, hcar[...], unroll=8)
        y_ref[...] = (_gelu(gb_ref[...]) * hs_ref[...]).astype(BF16)

    row = pl.BlockSpec((T, D), lambda i: (i, 0))
    vec = pl.BlockSpec((1, D), lambda i: (0, 0))
    bd = pl.BlockSpec((nG, GT, GT), lambda i: (0, 0, 0))
    f32o = jax.ShapeDtypeStruct((S, D), F32)
    return _hosted_call(
        body, "lru_fwd", (S // T,),
        [row, pl.BlockSpec((T, D), lambda i: (i, 1)), pl.BlockSpec((CONV_WIDTH, D), lambda i: (0, 0)), vec,
         bd, vec, bd, vec, vec],
        [row, row, row, row, row], [jax.ShapeDtypeStruct((S, D), BF16), f32o, f32o, f32o, f32o],
        [pltpu.VMEM((T + SUBLANES, D), F32), pltpu.VMEM((T, D), F32), pltpu.VMEM((1, D), F32)],
        (u0, u0, conv_w, conv_b, wr_bd, b_r, wi_bd, b_i, lam), ("arbitrary",), plan)


def _lru_bwd(dy, u0, xc, r, ig, hs, conv_w, wr_bd, wi_bd, lam, S, D, plan=None):
    T = _pick(S, (128,))
    nT = S // T
    GT = wr_bd.shape[-1]
    nG = D // GT
    W = CONV_WIDTH

    def body(dy_ref, gb_ref, xb_ref, xbp_ref, xc_ref, r_ref, i_ref, hs_ref, hsp_ref, cw_ref, wr_ref, wi_ref, lam_ref,
             du_ref, dcw_ref, dcb_ref, dlam_ref, dbr_ref, dbi_ref, dwr_ref, dwi_ref,
             a_scr, dh_scr, exth, extx, extd, dxc_scr, dz_scr, carry):
        step = pl.program_id(0)
        first_tile = step == nT - 1

        @pl.when(step == 0)
        def _():
            for ref in (dcw_ref, dcb_ref, dlam_ref, dbr_ref, dbi_ref, dwr_ref, dwi_ref, carry):
                ref[...] = jnp.zeros_like(ref)
            extd[T:T + SUBLANES, :] = jnp.zeros((SUBLANES, D), F32)

        hs = hs_ref[...]
        dy = dy_ref[...]
        g, dgelu = _gelu_and_grad(gb_ref[...])
        du_ref[:, 0:D] = (dy * hs * dgelu).astype(BF16)
        r = r_ref[...]
        lam = lam_ref[...]
        ls = _log_sigmoid(lam)
        a, a2, mult = _decay_terms(r, ls)
        a_scr[...] = a
        dh_scr[...] = dy * g

        def rstep(j, c):
            t = T - 1 - j
            d = dh_scr[pl.ds(t, 1), :] + c
            dh_scr[pl.ds(t, 1), :] = d
            return a_scr[pl.ds(t, 1), :] * d

        carry[...] = lax.fori_loop(0, T, rstep, carry[...], unroll=8)
        dh = dh_scr[...]
        keep = jnp.where(first_tile, 0.0, 1.0)
        exth[0:SUBLANES, :] = hsp_ref[...] * keep
        exth[SUBLANES:SUBLANES + T, :] = hs
        hprev = exth[pl.ds(SUBLANES - 1, T), :]
        xc = xc_ref[...]
        ig = i_ref[...]
        da = dh * hprev
        dmult = dh * (ig * xc)
        dla = da * a - dmult * (a2 / mult)
        dlam_ref[...] += jnp.sum(dla * r, axis=0, keepdims=True) * (LRU_C * _sigmoid(-lam))
        dzr = (dla * (LRU_C * ls)) * (r * (1.0 - r))
        dzi = (dh * (mult * xc)) * (ig * (1.0 - ig))
        dbr_ref[...] += jnp.sum(dzr, axis=0, keepdims=True)
        dbi_ref[...] += jnp.sum(dzi, axis=0, keepdims=True)
        dxc_scr[...] = dh * (mult * ig)
        xcb = xc.astype(BF16)
        dz_scr[0] = dzr.astype(BF16)
        dz_scr[1] = dzi.astype(BF16)
        nt_dims = (((1,), (1,)), ((), ()))
        tn_dims = (((0,), (0,)), ((), ()))
        for gq in range(nG):
            sl = slice(gq * GT, (gq + 1) * GT)
            zr_g = dz_scr[0, :, sl]
            zi_g = dz_scr[1, :, sl]
            dxc_scr[:, sl] += (lax.dot_general(zr_g, wr_ref[gq], nt_dims, preferred_element_type=F32)
                               + lax.dot_general(zi_g, wi_ref[gq], nt_dims, preferred_element_type=F32))
            dwr_ref[gq] += lax.dot_general(xcb[:, sl], zr_g, tn_dims, preferred_element_type=F32)
            dwi_ref[gq] += lax.dot_general(xcb[:, sl], zi_g, tn_dims, preferred_element_type=F32)
        dxc = dxc_scr[...]
        dcb_ref[...] += jnp.sum(dxc, axis=0, keepdims=True)
        extx[0:SUBLANES, :] = xbp_ref[...] * keep
        extx[SUBLANES:SUBLANES + T, :] = xb_ref[...]
        extd[0:T, :] = dxc
        dxb = jnp.zeros((T, D), F32)
        for k in range(W):
            dxb = dxb + extd[pl.ds(W - 1 - k, T), :] * cw_ref[k:k + 1, :]
            dcw_ref[k:k + 1, :] += jnp.sum(dxc * extx[pl.ds(SUBLANES - (W - 1) + k, T), :], axis=0, keepdims=True)
        extd[T:T + SUBLANES, :] = dxc[0:SUBLANES, :]
        du_ref[:, D:2 * D] = dxb.astype(BF16)

    rev = lambda i: nT - 1 - i
    tpb = T // SUBLANES
    prev8 = lambda i: jnp.maximum(rev(i) * tpb - 1, 0)
    row = pl.BlockSpec((T, D), lambda i: (rev(i), 0))
    vec = pl.BlockSpec((1, D), lambda i: (0, 0))
    bd = pl.BlockSpec((nG, GT, GT), lambda i: (0, 0, 0))
    vec_o = jax.ShapeDtypeStruct((1, D), F32)
    bd_o = jax.ShapeDtypeStruct((nG, GT, GT), F32)
    return _hosted_call(
        body, "lru_bwd", (nT,),
        [row, row, pl.BlockSpec((T, D), lambda i: (rev(i), 1)), pl.BlockSpec((SUBLANES, D), lambda i: (prev8(i), 1)),
         row, row, row, row, pl.BlockSpec((SUBLANES, D), lambda i: (prev8(i), 0)),
         pl.BlockSpec((W, D), lambda i: (0, 0)), bd, bd, vec],
        [pl.BlockSpec((T, 2 * D), lambda i: (rev(i), 0)), pl.BlockSpec((W, D), lambda i: (0, 0)),
         vec, vec, vec, vec, bd, bd],
        [jax.ShapeDtypeStruct((S, 2 * D), BF16), jax.ShapeDtypeStruct((W, D), F32), vec_o, vec_o, vec_o, vec_o, bd_o, bd_o],
        [pltpu.VMEM((T, D), F32), pltpu.VMEM((T, D), F32), pltpu.VMEM((T + SUBLANES, D), F32),
         pltpu.VMEM((T + SUBLANES, D), F32), pltpu.VMEM((T + SUBLANES, D), F32),
         pltpu.VMEM((T, D), F32), pltpu.VMEM((2, T, D), BF16), pltpu.VMEM((1, D), F32)],
        (dy, u0, u0, u0, xc, r, ig, hs, hs, conv_w, wr_bd, wi_bd, lam), ("arbitrary",), plan)


AUG_ROWS = 16
HEAD_ROWS = 128
LSE_ROW = HEAD_DIM + 6
ONES_ROW_Q = HEAD_DIM + 3
ONES_COL_K = HEAD_DIM
ONES_ROW_V = HEAD_DIM
PREP_LANES = 512
HEAD_UNROLL = 4


def _split3(x):
    b1 = x.astype(BF16).astype(F32)
    r = x - b1
    b2 = r.astype(BF16).astype(F32)
    return b1, b2, r - b2


def _head_block(x, aug, T):
    row = lax.broadcasted_iota(jnp.int32, (AUG_ROWS, T), 0)
    blk = jnp.zeros((AUG_ROWS, T), F32)
    for i, e in enumerate(aug):
        blk = jnp.where(row == i, e, blk)
    return jnp.concatenate([x, blk, jnp.zeros((HEAD_ROWS - HEAD_DIM - AUG_ROWS, T), F32)], axis=0)


def _tri_matrix(lower):
    i = np.arange(LANES)
    m = (i[:, None] >= i[None, :]) if lower else (i[:, None] <= i[None, :])
    return jnp.asarray(m.astype(np.float32), BF16)


def _lane_cumsum(x, tri_ref, carry, reverse):
    n = x.shape[1] // LANES
    tri = tri_ref[...]
    out = [None] * n
    for j in (range(n - 1, -1, -1) if reverse else range(n)):
        cs = carry
        for part in _split3(x[:, j * LANES:(j + 1) * LANES]):
            cs = cs + jnp.dot(part.astype(BF16), tri, preferred_element_type=F32)
        out[j] = cs
        carry = cs[:, 0:1] if reverse else cs[:, LANES - 1:LANES]
    return jnp.concatenate(out, axis=1), carry


def _head_rows(h):
    return pl.ds(pl.multiple_of(h * HEAD_DIM, HEAD_DIM), HEAD_DIM)


def _fox_prep(ut, b_f, qg, kg, S, D, tq):
    H = D // HEAD_DIM
    T = min(tq, PREP_LANES)
    per = tq // T
    scale = HEAD_DIM ** -0.5

    def body(q_ref, k_ref, v_ref, f_ref, bf_ref, qg_ref, kg_ref, tri_ref,
             qat_ref, kat_ref, vat_ref, ka_ref, c_scr, ccar):
        @pl.when(pl.program_id(0) == 0)
        def _():
            ccar[...] = jnp.zeros_like(ccar)

        c, carry = _lane_cumsum(_log_sigmoid(f_ref[...] + bf_ref[...]), tri_ref, ccar[...], False)
        c_scr[...] = c
        ccar[...] = carry

        def head(h, _):
            rows = _head_rows(h)
            c1, c2, c3 = _split3(c_scr[pl.ds(h, 1), :])

            def normed(src, gain, mul):
                x = src[rows, :]
                rs = lax.rsqrt(jnp.mean(x * x, axis=0, keepdims=True) + EPS)
                return ((x * rs) * gain[rows, :]) * mul

            qat_ref[h] = _head_block(normed(q_ref, qg_ref, scale), [c1, c2, c3, 1.0, 1.0, 1.0], T).astype(BF16)
            kb = _head_block(normed(k_ref, kg_ref, 1.0), [1.0, 1.0, 1.0, -c1, -c2, -c3, 1.0, 1.0, 1.0], T)
            kat_ref[h] = kb.astype(BF16)
            ka_ref[h] = kb.T.astype(BF16)
            vat_ref[h] = _head_block(v_ref[rows, :], [1.0, 1.0, 1.0], T).astype(BF16)
            return 0

        lax.fori_loop(0, H, head, 0, unroll=HEAD_UNROLL)

    part = lambda j: pl.BlockSpec((D, T), lambda i: (j, i))
    colv = lambda n: pl.BlockSpec((n, 1), lambda i: (0, 0))
    tmaj = lambda r: pl.BlockSpec((H, None, r, T), lambda i: (0, i // per, 0, i % per))
    norm = pl.BlockSpec((H, T, HEAD_ROWS), lambda i: (0, i, 0))
    tshape = lambda r: jax.ShapeDtypeStruct((H, S // tq, r, tq), BF16)
    nshape = jax.ShapeDtypeStruct((H, S, HEAD_ROWS), BF16)
    return pl.pallas_call(
        body, name="fox_prep", grid=(S // T,),
        in_specs=[part(0), part(1), part(2), pl.BlockSpec((LANES, T), lambda i: (3 * D // LANES, i)),
                  colv(LANES), colv(D), colv(D), pl.BlockSpec((LANES, LANES), lambda i: (0, 0))],
        out_specs=[tmaj(HEAD_ROWS), tmaj(HEAD_ROWS), tmaj(HEAD_ROWS), norm],
        out_shape=[tshape(HEAD_ROWS), tshape(HEAD_ROWS), tshape(HEAD_ROWS), nshape],
        scratch_shapes=[pltpu.VMEM((LANES, T), F32), pltpu.VMEM((LANES, 1), F32)],
        compiler_params=_params(("arbitrary",)),
    )(ut, ut, ut, ut, b_f, qg, kg, _tri_matrix(False))


def _fox_bwd_prep(dot, ot, lse, qat, S, D, tq, plan=None):
    H = D // HEAD_DIM
    T = min(tq, PREP_LANES)
    per = tq // T

    def body(do_ref, o_ref, lse_ref, qat_ref, doat_ref, doa_ref, qat1_ref, qa1_ref):
        row = lax.broadcasted_iota(jnp.int32, (HEAD_ROWS, T), 0)

        def head(h, _):
            rows = _head_rows(h)
            do = do_ref[rows, :].astype(F32)
            delta = jnp.sum(do * o_ref[rows, :], axis=0, keepdims=True)
            db = _head_block(do, list(_split3(-delta)), T)
            doat_ref[h] = db.astype(BF16)
            doa_ref[h] = db.T.astype(BF16)
            qb = qat_ref[h].astype(F32)
            for i, e in enumerate(_split3(-lse_ref[h])):
                qb = jnp.where(row == LSE_ROW + i, e, qb)
            qat1_ref[h] = qb.astype(BF16)
            qa1_ref[h] = qb.T.astype(BF16)
            return 0

        lax.fori_loop(0, H, head, 0, unroll=HEAD_UNROLL)

    chan = pl.BlockSpec((D, T), lambda i: (0, i))
    tmaj = pl.BlockSpec((H, None, HEAD_ROWS, T), lambda i: (0, i // per, 0, i % per))
    norm = pl.BlockSpec((H, T, HEAD_ROWS), lambda i: (0, i, 0))
    tshape = jax.ShapeDtypeStruct((H, S // tq, HEAD_ROWS, tq), BF16)
    nshape = jax.ShapeDtypeStruct((H, S, HEAD_ROWS), BF16)
    return _hosted_call(body, "fox_bwd_prep", (S // T,), [chan, chan, pl.BlockSpec((H, 1, T), lambda i: (0, 0, i)), tmaj],
                        [tmaj, norm, tmaj, norm], [tshape, nshape, tshape, nshape], [], (dot, ot, lse, qat),
                        ("arbitrary",), plan)


def _causal(s, k_axis):
    t = min(s.shape)
    ki = lax.broadcasted_iota(jnp.int32, s.shape, k_axis) - (s.shape[k_axis] - t)
    qi = lax.broadcasted_iota(jnp.int32, s.shape, 1 - k_axis)
    return jnp.where(ki <= qi, s, NEG_INF)


def _seq_tile(i, t):
    return pl.ds(pl.multiple_of(i * t, t), t)


def _attn_forward(ka, qat, vat, S, D, tq, plan=None):
    H = D // HEAD_DIM
    nq = S // tq

    def body(ka_ref, qat_ref, vat_ref, o_ref, o32_ref, lse_ref, m_scr, acc_scr):
        qi = pl.program_id(1)
        m_scr[...] = jnp.full_like(m_scr, NEG_INF)
        acc_scr[...] = jnp.zeros_like(acc_scr)
        qa = qat_ref[...]

        def span(k0, n, diagonal):
            s = jnp.dot(ka_ref[pl.ds(pl.multiple_of(k0 * tq, tq), n * tq), :], qa, preferred_element_type=F32)
            if diagonal:
                s = _causal(s, 0)
            m_prev = m_scr[...]
            m_new = jnp.maximum(m_prev, jnp.max(s, axis=0, keepdims=True))
            p = jnp.exp(s - m_new).astype(BF16)
            upd = jnp.dot(vat_ref[k0], p[0:tq], preferred_element_type=F32)
            for i in range(1, n):
                upd = upd + jnp.dot(vat_ref[k0 + i], p[i * tq:(i + 1) * tq], preferred_element_type=F32)
            acc_scr[...] = jnp.exp(m_prev - m_new) * acc_scr[...] + upd
            m_scr[...] = m_new

        def off_diagonal_pair(j, _):
            span(2 * j, 2, False)
            return 0

        lax.fori_loop(0, qi // 2, off_diagonal_pair, 0)
        pl.when(qi % 2 == 1)(lambda: span(qi - 1, 2, True))
        pl.when(qi % 2 == 0)(lambda: span(qi, 1, True))
        l = acc_scr[ONES_ROW_V:ONES_ROW_V + 1, :]
        o = acc_scr[0:HEAD_DIM, :] / l
        o_ref[...] = o.astype(BF16)
        o32_ref[...] = o
        lse_ref[...] = m_scr[...] + jnp.log(l)

    chan = pl.BlockSpec((HEAD_DIM, tq), lambda h, i: (h, i))
    stat = pl.BlockSpec((None, 1, tq), lambda h, i: (h, 0, i))
    return _hosted_call(
        body, "attn_forward", (H, nq),
        [pl.BlockSpec((None, S, HEAD_ROWS), lambda h, i: (h, 0, 0)),
         pl.BlockSpec((None, None, HEAD_ROWS, tq), lambda h, i: (h, i, 0, 0)),
         pl.BlockSpec((None, nq, HEAD_ROWS, tq), lambda h, i: (h, 0, 0, 0))],
        [chan, chan, stat],
        [jax.ShapeDtypeStruct((D, S), BF16), jax.ShapeDtypeStruct((D, S), F32), jax.ShapeDtypeStruct((H, 1, S), F32)],
        [pltpu.VMEM((1, tq), F32), pltpu.VMEM((HEAD_ROWS, tq), F32)],
        (ka, qat, vat), ("arbitrary", "arbitrary"), plan)


def _attn_backward(qa, doa, qat, doat, ka, kat, vat, S, D, tq, plan=None):
    H = D // HEAD_DIM
    nq = S // tq

    def body(qa_ref, doa_ref, qat_ref, doat_ref, ka_ref, kat_ref, vat_ref, dq_ref, dk_ref, dv_ref, dk_scr, dv_scr):
        ki = pl.program_id(1)

        @pl.when(ki == 0)
        def _():
            dq_ref[...] = jnp.zeros_like(dq_ref)

        dk_scr[...] = jnp.zeros_like(dk_scr)
        dv_scr[...] = jnp.zeros_like(dv_scr)
        kt = kat_ref[...]
        vt = vat_ref[...]
        kn = ka_ref[...]

        def span(q0, n, diagonal):
            rows = pl.ds(pl.multiple_of(q0 * tq, tq), n * tq)
            s = jnp.dot(qa_ref[rows, :], kt, preferred_element_type=F32)
            if diagonal:
                s = _causal(s, 1)
            p = jnp.exp(s)
            ds = (p * jnp.dot(doa_ref[rows, :], vt, preferred_element_type=F32)).astype(BF16)
            p = p.astype(BF16)
            for i in range(n):
                part = slice(i * tq, (i + 1) * tq)
                dv_scr[...] += jnp.dot(doat_ref[q0 + i, 0:HEAD_DIM, :], p[part], preferred_element_type=F32)
                dk_scr[...] += jnp.dot(qat_ref[q0 + i], ds[part], preferred_element_type=F32)
            dq_ref[rows, :] += jnp.dot(ds, kn, preferred_element_type=F32)

        n_off = nq - 1 - ki
        odd = n_off % 2

        def off_diagonal_pair(j, _):
            span(ki + 1 + odd + 2 * j, 2, False)
            return 0

        pl.when(odd == 1)(lambda: span(ki, 2, True))
        pl.when(odd == 0)(lambda: span(ki, 1, True))
        lax.fori_loop(0, n_off // 2, off_diagonal_pair, 0)
        dk_ref[...] = dk_scr[...]
        dv_ref[...] = dv_scr[...].astype(BF16)

    whole = pl.BlockSpec((None, S, HEAD_ROWS), lambda h, i: (h, 0, 0))
    tiles = pl.BlockSpec((None, nq, HEAD_ROWS, tq), lambda h, i: (h, 0, 0, 0))
    one = pl.BlockSpec((None, None, HEAD_ROWS, tq), lambda h, i: (h, i, 0, 0))
    return _hosted_call(
        body, "attn_backward", (H, nq),
        [whole, whole, tiles, tiles, pl.BlockSpec((None, tq, HEAD_ROWS), lambda h, i: (h, i, 0)), one, one],
        [whole, pl.BlockSpec((None, HEAD_ROWS, tq), lambda h, i: (h, 0, i)),
         pl.BlockSpec((HEAD_DIM, tq), lambda h, i: (h, i))],
        [jax.ShapeDtypeStruct((H, S, HEAD_ROWS), F32), jax.ShapeDtypeStruct((H, HEAD_ROWS, S), F32),
         jax.ShapeDtypeStruct((D, S), BF16)],
        [pltpu.VMEM((HEAD_ROWS, tq), F32), pltpu.VMEM((HEAD_DIM, tq), F32)],
        (qa, doa, qat, doat, ka, kat, vat), ("arbitrary", "arbitrary"), plan)


def _fox_prep_bwd(ut, dq, dkt, dvt, b_f, qg, kg, S, D, tq):
    H = D // HEAD_DIM
    T = min(tq, PREP_LANES)
    nT = S // T
    NU = 3 * D + LANES
    scale = HEAD_DIM ** -0.5

    def body(q_ref, k_ref, f_ref, dq_ref, dk_ref, dv_ref, bf_ref, qg_ref, kg_ref, tri_ref,
             du_ref, dbf_ref, dqg_ref, dkg_ref, gq_acc, gk_acc, fcar, dc_scr):
        step = pl.program_id(0)

        @pl.when(step == 0)
        def _():
            for ref in (gq_acc, gk_acc, fcar, dbf_ref):
                ref[...] = jnp.zeros_like(ref)

        dc_scr[...] = jnp.zeros_like(dc_scr)

        def head(h, _):
            rows = _head_rows(h)
            dqb = dq_ref[h].T
            dkb = dk_ref[h]
            dc_scr[pl.ds(h, 1), :] = dqb[ONES_COL_K:ONES_COL_K + 1, :] - dkb[ONES_ROW_Q:ONES_ROW_Q + 1, :]
            for src, dsrc, gain, acc, mul, base in ((q_ref, dqb, qg_ref, gq_acc, scale, 0),
                                                    (k_ref, dkb, kg_ref, gk_acc, 1.0, D)):
                x = src[rows, :]
                rs = lax.rsqrt(jnp.mean(x * x, axis=0, keepdims=True) + EPS)
                xhat = x * rs
                dn = dsrc[0:HEAD_DIM, :] * mul
                acc[rows, :] += jnp.sum(dn * xhat, axis=1, keepdims=True)
                dxh = dn * gain[rows, :]
                dx = rs * (dxh - xhat * jnp.mean(dxh * xhat, axis=0, keepdims=True))
                du_ref[pl.ds(pl.multiple_of(base + h * HEAD_DIM, HEAD_DIM), HEAD_DIM), :] = dx.astype(BF16)
            return 0

        lax.fori_loop(0, H, head, 0, unroll=HEAD_UNROLL)
        du_ref[2 * D:3 * D, :] = dv_ref[...]
        dlf, carry = _lane_cumsum(dc_scr[...], tri_ref, fcar[...], True)
        fcar[...] = carry
        dfl = dlf * _sigmoid(-(f_ref[...] + bf_ref[...]))
        dbf_ref[...] += jnp.sum(dfl, axis=1, keepdims=True)
        du_ref[3 * D:NU, :] = dfl.astype(BF16)

        @pl.when(step == nT - 1)
        def _():
            for acc, ref in ((gq_acc, dqg_ref), (gk_acc, dkg_ref)):
                tot = jnp.zeros((HEAD_DIM, 1), F32)
                for h in range(H):
                    tot = tot + acc[h * HEAD_DIM:(h + 1) * HEAD_DIM, :]
                ref[...] = tot

    rev = lambda i: nT - 1 - i
    part = lambda j: pl.BlockSpec((D, T), lambda i: (j, rev(i)))
    colv = lambda n: pl.BlockSpec((n, 1), lambda i: (0, 0))
    return pl.pallas_call(
        body, name="fox_prep_bwd", grid=(nT,),
        in_specs=[part(0), part(1), pl.BlockSpec((LANES, T), lambda i: (3 * D // LANES, rev(i))),
                  pl.BlockSpec((H, T, HEAD_ROWS), lambda i: (0, rev(i), 0)),
                  pl.BlockSpec((H, HEAD_ROWS, T), lambda i: (0, 0, rev(i))), pl.BlockSpec((D, T), lambda i: (0, rev(i))),
                  colv(LANES), colv(D), colv(D), pl.BlockSpec((LANES, LANES), lambda i: (0, 0))],
        out_specs=[pl.BlockSpec((NU, T), lambda i: (0, rev(i))), colv(LANES), colv(HEAD_DIM), colv(HEAD_DIM)],
        out_shape=[jax.ShapeDtypeStruct((NU, S), BF16), jax.ShapeDtypeStruct((LANES, 1), F32),
                   jax.ShapeDtypeStruct((HEAD_DIM, 1), F32), jax.ShapeDtypeStruct((HEAD_DIM, 1), F32)],
        scratch_shapes=[pltpu.VMEM((D, 1), F32), pltpu.VMEM((D, 1), F32), pltpu.VMEM((LANES, 1), F32),
                        pltpu.VMEM((LANES, T), F32)],
        compiler_params=_params(("arbitrary",)),
    )(ut, ut, ut, dq, dkt, dvt, b_f, qg, kg, _tri_matrix(True))


def _block_diag_tiles(w):
    n = w.shape[0]
    per = min(MXU_DIM, n * LRU_BLOCK_DIM) // LRU_BLOCK_DIM
    eye = jnp.eye(per, dtype=w.dtype)
    w5 = w.reshape(n // per, per, LRU_BLOCK_DIM, 1, LRU_BLOCK_DIM) * eye[None, :, None, :, None]
    return w5.reshape(n // per, per * LRU_BLOCK_DIM, per * LRU_BLOCK_DIM).astype(BF16)


def _block_diag_extract(t, n):
    per = t.shape[-1] // LRU_BLOCK_DIM
    eye = jnp.eye(per, dtype=t.dtype)
    t5 = t.reshape(n // per, per, LRU_BLOCK_DIM, per, LRU_BLOCK_DIM) * eye[None, :, None, :, None]
    return t5.sum(axis=3).reshape(n, LRU_BLOCK_DIM, LRU_BLOCK_DIM)


def _local_step(x, tgt, small, wv, grad_view, comm=None):
    S, D = x.shape
    F = 4 * D
    H = D // HEAD_DIM
    nblk = D // LRU_BLOCK_DIM
    NU = 3 * D + LANES
    tq = max(LANES, min(512, S // 4))
    assert S % tq == 0
    vec = lambda a: a.reshape(1, -1).astype(F32)
    col = lambda a: a.reshape(-1, 1).astype(F32)
    mix_g, mlp_g = small["mix_norm"], small["mlp_norm"]
    conv_b = vec(small["lru_conv_b"])
    wr_bd, wi_bd = _block_diag_tiles(small["lru_w_r"][0]), _block_diag_tiles(small["lru_w_i"][0])
    b_r, b_i, lam = vec(small["lru_b_r"]), vec(small["lru_b_i"]), vec(small["lru_lambda"])
    b_f = jnp.pad(col(small["fox_b_f"]), ((0, LANES - H), (0, 0)))
    qg, kg = jnp.tile(col(small["fox_q_gain"]), (H, 1)), jnp.tile(col(small["fox_k_gain"]), (H, 1))
    X = lambda a: _View(a)
    grads = {}
    gout = functools.partial(grad_view, grads)

    def hosted(name, fn, *args):
        plan = comm.before(name, grads) if comm is not None else None
        res, side = fn(*args, plan=plan)
        if plan is not None:
            comm.after(name, side, wv)
        return res

    def hosted_mm(name, *args, **kw):
        plan = comm.before(name, grads) if comm is not None else None
        if plan is None:
            return _matmul(name, *args, **kw)
        res, side = _matmul(name, *args, plan=plan, **kw)
        comm.after(name, side, wv)
        return res

    norm_rows = _pick(S, (1024, 512, 256, 128))
    two = lambda: [_fresh(S, D, F32), _fresh(S, D, BF16)]

    def mlp_up(l, hm):
        return hosted_mm(f"mlp{l}_up", X(hm), wv[f"w1_{l}"], S, F, D, outs=[_fresh(S, F, BF16)], epilogue=_ep_relu2)[0]

    def mlp_bwd(l, xin, hm, act, d, db):
        (dz,) = hosted_mm(f"mlp{l}_dact", X(db), wv[f"w2_{l}"], S, F, D, tb=True, outs=[_fresh(S, F, BF16)],
                          epilogue=_ep_drelu2, extras=[X(act)])
        (grads[f"w2_{l}"],) = _matmul(f"mlp{l}_dw2", X(act), X(db), F, D, S, ta=True, outs=[gout(f"w2_{l}")],
                                      epilogue=_ep_store)
        (grads[f"w1_{l}"],) = _matmul(f"mlp{l}_dw1", X(hm), X(dz), D, F, S, ta=True, outs=[gout(f"w1_{l}")],
                                      epilogue=_ep_store)
        return _matmul(f"mlp{l}_dhm", X(dz), wv[f"w1_{l}"], S, D, F, tb=True, outs=two(), n_sums=1,
                       epilogue=_ep_norm_bwd, extras=[X(xin), X(d)], vecs=[mlp_g[l:l + 1]], tm=norm_rows)

    (h0,) = hosted("mix0_norm", _rms_fwd, "mix0_norm", x, mix_g[0:1], S, D)
    (u0,) = hosted_mm("lru_in", X(h0), wv["lru_in"], S, 2 * D, D, outs=[_fresh(S, 2 * D, F32)], epilogue=_ep_store)
    conv_w = small["conv_w"]
    y, xc, r, ig, hs = hosted("lru_fwd", _lru_fwd, u0, conv_w, conv_b, wr_bd, b_r, wi_bd, b_i, lam, S, D)
    x1, hm0 = _matmul("lru_out", X(y), wv["lru_out"], S, D, D, outs=two(), epilogue=_ep_resid_norm, extras=[X(x)],
                      vecs=[mlp_g[0:1]], tm=norm_rows)
    act0 = mlp_up(0, hm0)
    x2, h1 = hosted_mm("mlp0_down", X(act0), wv["w2_0"], S, D, F, outs=two(), epilogue=_ep_resid_norm, extras=[X(x1)],
                       vecs=[mix_g[1:2]], tm=norm_rows)
    (u1,) = _matmul("fox_in", wv["fox_in"], X(h1), NU, S, D, tb=True, outs=[_fresh(NU, S, F32)], epilogue=_ep_store)
    qat, kat, vat, ka = _fox_prep(u1, b_f, qg, kg, S, D, tq)
    o, o32, lse = hosted("attn_forward", _attn_forward, ka, qat, vat, S, D, tq)
    x3, hm1 = _matmul("fox_out", X(o), wv["fox_out"], S, D, D, ta=True, outs=two(), epilogue=_ep_resid_norm,
                      extras=[X(x2)], vecs=[mlp_g[1:2]], tm=norm_rows)
    act1 = mlp_up(1, hm1)
    (x4,) = _matmul("mlp1_down", X(act1), wv["w2_1"], S, D, F, outs=[_fresh(S, D, F32)], epilogue=_ep_resid,
                    extras=[X(x3)])
    loss, d4, d4b = _loss_head(x4, tgt, S, D)

    d3, d3b, dg_mlp1 = mlp_bwd(1, x3, hm1, act1, d4, d4b)
    (do,) = _matmul("fox_dout", wv["fox_out"], X(d3b), D, S, D, tb=True, outs=[_fresh(D, S, BF16)], epilogue=_ep_store)
    (grads["fox_out"],) = _matmul("fox_dwout", X(o), X(d3b), D, D, S, outs=[gout("fox_out")], epilogue=_ep_store)
    doat, doa, qat1, qa1 = hosted("fox_bwd_prep", _fox_bwd_prep, do, o32, lse, qat, S, D, tq)
    dqn, dkn, dv = hosted("attn_backward", _attn_backward, qa1, doa, qat1, doat, ka, kat, vat, S, D, tq)
    du1, dbf, dqg, dkg = _fox_prep_bwd(u1, dqn, dkn, dv, b_f, qg, kg, S, D, tq)
    (grads["fox_in"],) = _matmul("fox_dwin", X(h1), X(du1), D, NU, S, ta=True, tb=True, outs=[gout("fox_in")],
                                 epilogue=_ep_store)
    d2, d2b, dg_mix1 = hosted_mm("fox_dh", X(du1), wv["fox_in"], S, D, NU, ta=True, outs=two(), n_sums=1,
                               epilogue=_ep_norm_bwd, extras=[X(x2), X(d3)], vecs=[mix_g[1:2]], tm=norm_rows)
    d1, d1b, dg_mlp0 = mlp_bwd(0, x1, hm0, act0, d2, d2b)
    (grads["lru_out"],) = _matmul("lru_dwout", X(y), X(d1b), D, D, S, ta=True, outs=[gout("lru_out")],
                                  epilogue=_ep_store)
    (dy,) = hosted_mm("lru_dout", X(d1b), wv["lru_out"], S, D, D, tb=True, outs=[_fresh(S, D, F32)],
                      epilogue=_ep_store)
    du0, dcw, dcb, dlam, dbr, dbi, dwr, dwi = hosted("lru_bwd", _lru_bwd, dy, u0, xc, r, ig, hs, conv_w, wr_bd, wi_bd,
                                                     lam, S, D)
    (grads["lru_in"],) = _matmul("lru_dwin", X(h0), X(du0), D, 2 * D, S, ta=True, outs=[gout("lru_in")],
                                 epilogue=_ep_store)
    gx, dg_mix0 = hosted_mm("lru_dh", X(du0), wv["lru_in"], S, D, 2 * D, tb=True, outs=[_fresh(S, D, F32)], n_sums=1,
                            epilogue=lambda *a: _ep_norm_bwd(*a)[::2], extras=[X(x), X(d1)], vecs=[mix_g[0:1]],
                            tm=norm_rows)

    grads.update(
        mix_norm=jnp.concatenate([dg_mix0, dg_mix1], axis=0), mlp_norm=jnp.concatenate([dg_mlp0, dg_mlp1], axis=0),
        conv_w=dcw, lru_conv_b=dcb, lru_w_r=_block_diag_extract(dwr, nblk)[None], lru_b_r=dbr.reshape(1, nblk, -1),
        lru_w_i=_block_diag_extract(dwi, nblk)[None], lru_b_i=dbi.reshape(1, nblk, -1), lru_lambda=dlam,
        fox_b_f=dbf[:H].reshape(1, H), fox_q_gain=dqg.reshape(1, -1), fox_k_gain=dkg.reshape(1, -1))
    return loss, gx, grads


def _place():
    x, y, c = lax.axis_index("x"), lax.axis_index("y"), lax.axis_index("c")
    chips = [(1 - x, y), (x, 1 - y), (1 - x, 1 - y)]
    return x, y, c, 2 * x + y, chips


BOUNCE_BYTES = 1 << 20


def _bounce_shape(rows, cols, dtype):
    chunk = rows
    while chunk % 2 == 0 and chunk > 16 and chunk * cols * jnp.dtype(dtype).itemsize > BOUNCE_BYTES:
        chunk //= 2
    return pltpu.VMEM((2, chunk, cols), dtype)


def _bounce_copy(src, dst, buf, sem):
    chunk = buf.shape[1]
    n = src.shape[0] // chunk
    cin = lambda i: pltpu.make_async_copy(src.at[pl.ds(i * chunk, chunk)], buf.at[i % 2], sem.at[i % 2])
    cout = lambda i: pltpu.make_async_copy(buf.at[i % 2], dst.at[pl.ds(i * chunk, chunk)], sem.at[2 + i % 2])
    cin(0).start()
    for i in range(n):
        cin(i).wait()
        if i + 1 < n:
            if i >= 1:
                cout(i - 1).wait()
            cin(i + 1).start()
        cout(i).start()
    if n >= 2:
        cout(n - 2).wait()
    cout(n - 1).wait()


def _hbm_call(body, name, arrays, out_shape, n_dma_sems, bounce=()):
    scratch = [pltpu.SemaphoreType.DMA((k,)) for k in n_dma_sems]
    for rows, cols, dtype in bounce:
        scratch += [_bounce_shape(rows, cols, dtype), pltpu.SemaphoreType.DMA((4,))]
    return pl.pallas_call(
        body, name=name, in_specs=[ANY] * len(arrays), out_specs=[ANY] * len(out_shape), out_shape=out_shape,
        scratch_shapes=scratch,
        compiler_params=pltpu.CompilerParams(has_side_effects=True, vmem_limit_bytes=VMEM_LIMIT),
    )(*arrays)


class _Gather:
    def __init__(self, shards):
        n = self.n = len(shards)
        self.operands = list(shards)
        self.out_shape = [jax.ShapeDtypeStruct((N_CHIPS,) + tuple(a.shape), a.dtype) for a in shards]
        self.scratch = [pltpu.SemaphoreType.DMA((3 * n,)) for _ in range(4)]
        for a in shards:
            self.scratch += [_bounce_shape(a.shape[0], a.shape[1], a.dtype), pltpu.SemaphoreType.DMA((4,))]

    def _copies(self, ins, outs, scr):
        send, recv, fsend, frecv = scr[:4]
        x, y, c, s, chips = _place()

        def rows(a, chip_idx, which):
            hr = ins[a].shape[0] // 2
            return outs[a].at[chip_idx, pl.ds(which * hr, hr)]

        def landed(a, j, core):
            return rows(a, 2 * chips[j][0] + chips[j][1], core)

        def ici(a, j, mine):
            hr = ins[a].shape[0] // 2
            src, dst = (ins[a].at[pl.ds(c * hr, hr)], rows(a, s, c)) if mine else (landed(a, j, c),) * 2
            return pltpu.make_async_remote_copy(src_ref=src, dst_ref=dst, send_sem=send.at[3 * a + j],
                                                recv_sem=recv.at[3 * a + j], device_id=(*chips[j], c),
                                                device_id_type=MESH)

        def d2d(a, j, mine):
            ref = landed(a, j, c if mine else 1 - c)
            return pltpu.make_async_remote_copy(src_ref=ref, dst_ref=ref, send_sem=fsend.at[3 * a + j],
                                                recv_sem=frecv.at[3 * a + j], device_id=(x, y, 1 - c),
                                                device_id_type=MESH)

        return ici, d2d, s

    def start(self, ins, outs, scr):
        ici, _, _ = self._copies(ins, outs, scr)
        for a in range(self.n):
            for j in range(3):
                ici(a, j, True).start()

    def middle(self, ins, outs, scr):
        ici, d2d, s = self._copies(ins, outs, scr)
        for a in range(self.n):
            _bounce_copy(ins[a], outs[a].at[s], scr[4 + 2 * a], scr[5 + 2 * a])
        for a in range(self.n):
            for j in range(3):
                ici(a, j, False).wait_recv()
                d2d(a, j, True).start()

    def finish(self, ins, outs, scr):
        ici, d2d, _ = self._copies(ins, outs, scr)
        for a in range(self.n):
            for j in range(3):
                d2d(a, j, False).wait_recv()
        for a in range(self.n):
            for j in range(3):
                ici(a, j, True).wait_send()
                d2d(a, j, True).wait_send()


def _run_plan(name, plan):
    k_in, k_out = len(plan.operands), len(plan.out_shape)

    def body(*refs):
        parts = (refs[:k_in], refs[k_in:k_in + k_out], refs[k_in + k_out:])
        plan.start(*parts)
        plan.middle(*parts)
        plan.finish(*parts)

    return pl.pallas_call(
        body, name=name, in_specs=[ANY] * k_in, out_specs=[ANY] * k_out, out_shape=plan.out_shape,
        scratch_shapes=plan.scratch,
        compiler_params=pltpu.CompilerParams(has_side_effects=True, vmem_limit_bytes=VMEM_LIMIT),
    )(*plan.operands)


def _hosted_call(body, name, grid, in_specs, out_specs, out_shape, scratch_shapes, operands, sem, plan=None):
    if plan is None:
        res = pl.pallas_call(body, name=name, grid=grid, in_specs=in_specs, out_specs=out_specs, out_shape=out_shape,
                             scratch_shapes=scratch_shapes, compiler_params=_params(sem))(*operands)
        return res, None
    n_in, n_out, n_scr = len(in_specs), len(out_specs), len(scratch_shapes)
    k_in, k_out = len(plan.operands), len(plan.out_shape)
    total = int(np.prod(grid))
    late = max(0, total - 1 - max(1, total // 8))

    def hosted(*refs):
        ins, refs = refs[:n_in], refs[n_in:]
        p_ins, refs = refs[:k_in], refs[k_in:]
        outs, refs = refs[:n_out], refs[n_out:]
        p_outs, refs = refs[:k_out], refs[k_out:]
        scr, p_scr = refs[:n_scr], refs[n_scr:]
        step = pl.program_id(0)
        for d in range(1, len(grid)):
            step = step * grid[d] + pl.program_id(d)
        pl.when(step == 0)(lambda: plan.start(p_ins, p_outs, p_scr))
        body(*ins, *outs, *scr)
        pl.when(step == late)(lambda: plan.middle(p_ins, p_outs, p_scr))
        pl.when(step == total - 1)(lambda: plan.finish(p_ins, p_outs, p_scr))

    res = pl.pallas_call(
        hosted, name=name, grid=grid, in_specs=list(in_specs) + [ANY] * k_in, out_specs=list(out_specs) + [ANY] * k_out,
        out_shape=list(out_shape) + plan.out_shape, scratch_shapes=list(scratch_shapes) + plan.scratch,
        compiler_params=pltpu.CompilerParams(dimension_semantics=sem, vmem_limit_bytes=VMEM_LIMIT,
                                             has_side_effects=True),
    )(*operands, *plan.operands)
    return res[:n_out], res[n_out:]


def _all_gather(name, shards):
    return _run_plan(name, _Gather(shards))


class _Swap:
    def __init__(self, arrs):
        self.n = len(arrs)
        self.operands = list(arrs)
        self.out_shape = [jax.ShapeDtypeStruct((a.shape[0], a.shape[1] // 2, a.shape[2]), a.dtype) for a in arrs]
        self.scratch = [pltpu.SemaphoreType.DMA((self.n,)) for _ in range(2)]

    def _copy(self, ins, outs, scr, a):
        x, y, c, _, _ = _place()
        hr = ins[a].shape[1] // 2
        return pltpu.make_async_remote_copy(
            src_ref=ins[a].at[:, pl.ds((1 - c) * hr, hr)], dst_ref=outs[a], send_sem=scr[0].at[a],
            recv_sem=scr[1].at[a], device_id=(x, y, 1 - c), device_id_type=MESH)

    def start(self, ins, outs, scr):
        for a in range(self.n):
            self._copy(ins, outs, scr, a).start()

    def middle(self, ins, outs, scr):
        pass

    def finish(self, ins, outs, scr):
        for a in range(self.n):
            self._copy(ins, outs, scr, a).wait()


class _Scatter:
    def __init__(self, parts):
        n = self.n = len(parts)
        self.operands = list(parts)
        self.out_shape = [jax.ShapeDtypeStruct(a.shape, a.dtype) for a in parts]
        self.scratch = [pltpu.SemaphoreType.DMA((3 * n,)) for _ in range(2)]
        for a in parts:
            self.scratch += [_bounce_shape(a.shape[1], a.shape[2], a.dtype), pltpu.SemaphoreType.DMA((4,))]

    def _copy(self, ins, outs, scr, a, j, mine):
        x, y, c, s, chips = _place()
        t = 2 * chips[j][0] + chips[j][1]
        return pltpu.make_async_remote_copy(
            src_ref=ins[a].at[t], dst_ref=outs[a].at[s if mine else t], send_sem=scr[0].at[3 * a + j],
            recv_sem=scr[1].at[3 * a + j], device_id=(*chips[j], c), device_id_type=MESH)

    def start(self, ins, outs, scr):
        for a in range(self.n):
            for j in range(3):
                self._copy(ins, outs, scr, a, j, True).start()

    def middle(self, ins, outs, scr):
        s = _place()[3]
        for a in range(self.n):
            _bounce_copy(ins[a].at[s], outs[a].at[s], scr[2 + 2 * a], scr[3 + 2 * a])

    def finish(self, ins, outs, scr):
        for a in range(self.n):
            for j in range(3):
                self._copy(ins, outs, scr, a, j, False).wait_recv()
        for a in range(self.n):
            for j in range(3):
                self._copy(ins, outs, scr, a, j, True).wait_send()


def _pair_gather(name, halves):
    n = len(halves)

    def body(*refs):
        ins, outs = refs[:n], refs[n:2 * n]
        send, recv = refs[2 * n:2 * n + 2]
        stage = refs[2 * n + 2:]
        x, y, c, _, _ = _place()
        cps = []
        for a in range(n):
            hr = ins[a].shape[0]
            cp = pltpu.make_async_remote_copy(
                src_ref=ins[a], dst_ref=outs[a].at[pl.ds(c * hr, hr)], send_sem=send.at[a], recv_sem=recv.at[a],
                device_id=(x, y, 1 - c), device_id_type=MESH)
            cp.start()
            cps.append((cp, hr))
        for a, (cp, hr) in enumerate(cps):
            _bounce_copy(ins[a], outs[a].at[pl.ds(c * hr, hr)], stage[2 * a], stage[2 * a + 1])
        for a, (cp, hr) in enumerate(cps):
            cp.wait_send()
            theirs = outs[a].at[pl.ds((1 - c) * hr, hr)]
            pltpu.make_async_remote_copy(src_ref=theirs, dst_ref=theirs, send_sem=send.at[a], recv_sem=recv.at[a],
                                         device_id=(x, y, 1 - c), device_id_type=MESH).wait_recv()

    out_shape = [jax.ShapeDtypeStruct((2 * a.shape[0], a.shape[1]), a.dtype) for a in halves]
    return _hbm_call(body, name, halves, out_shape, (n, n),
                     bounce=[(a.shape[0], a.shape[1], a.dtype) for a in halves])


def _row_tile(rows, cols, itemsize, n_bufs):
    budget = VMEM_LIMIT // 2
    for t in (1024, 512, 256, 128, 64, 32, 16):
        if rows % t == 0 and 2 * n_bufs * t * cols * itemsize <= budget:
            return t
    return rows


def _pair_add(name, g, gsib, core, out_dtype):
    _, r, cols = g.shape
    hr = r // 2
    t = _row_tile(hr, cols, 4, 3)
    per = hr // t

    def body(core_ref, a_ref, b_ref, o_ref):
        o_ref[...] = (a_ref[...].astype(F32) + b_ref[...].astype(F32)).astype(o_ref.dtype)

    grid_spec = pltpu.PrefetchScalarGridSpec(
        num_scalar_prefetch=1, grid=(N_CHIPS, per),
        in_specs=[pl.BlockSpec((None, t, cols), lambda s, i, core: (s, core[0] * per + i, 0)),
                  pl.BlockSpec((None, t, cols), lambda s, i, core: (s, i, 0))],
        out_specs=pl.BlockSpec((None, t, cols), lambda s, i, core: (s, i, 0)))
    return pl.pallas_call(body, name=name, grid_spec=grid_spec,
                          out_shape=jax.ShapeDtypeStruct((N_CHIPS, hr, cols), out_dtype),
                          compiler_params=_params(("arbitrary", "arbitrary")))(core, g, gsib)


def _chip_sum(name, parts):
    _, hr, cols = parts.shape
    t = _row_tile(hr, cols, 4, 5)

    def body(p_ref, o_ref):
        o_ref[...] = ((p_ref[0].astype(F32) + p_ref[1].astype(F32)) + p_ref[2].astype(F32)) + p_ref[3].astype(F32)

    return pl.pallas_call(
        body, name=name, grid=(hr // t,), in_specs=[pl.BlockSpec((N_CHIPS, t, cols), lambda i: (0, i, 0))],
        out_specs=pl.BlockSpec((t, cols), lambda i: (i, 0)), out_shape=jax.ShapeDtypeStruct((hr, cols), F32),
        compiler_params=_params(("arbitrary",)))(parts)


def _pair_partials(tag, arrs, sib, wire_dtypes, core):
    return _Scatter([_pair_add(f"{tag}_pair_add{i}", g, gs, core, dt)
                     for i, (g, gs, dt) in enumerate(zip(arrs, sib, wire_dtypes))])


def _finish_reduce(tag, scattered):
    halves = [_chip_sum(f"{tag}_chip_sum{i}", p) for i, p in enumerate(scattered)]
    return _pair_gather(f"{tag}_pair_gather", halves)


def _adamw(name, w, g_parts, m, v):
    thin = w.ndim == 3
    rows, cols = w.shape[0], w.shape[-1]
    n_parts = len(g_parts)
    part_rows = rows // n_parts
    t = max(d for d in range(1, 257) if part_rows % d == 0) if thin else _row_tile(part_rows, cols, 4, 7 + n_parts)
    per = part_rows // t
    c1 = 1.0 - ADAM_B1 ** ADAM_STEP
    c2 = 1.0 - ADAM_B2 ** ADAM_STEP

    def body(w_ref, m_ref, v_ref, *refs):
        g_refs, (go_ref, d_ref, nm_ref, nv_ref) = refs[:n_parts], refs[n_parts:]
        g = g_refs[0][...]
        for k in range(1, n_parts):
            g = jnp.where(pl.program_id(0) >= k * per, g_refs[k][...], g)
        go_ref[...] = g
        m = ADAM_B1 * m_ref[...] + (1.0 - ADAM_B1) * g
        v = ADAM_B2 * v_ref[...] + (1.0 - ADAM_B2) * (g * g)
        nm_ref[...] = m
        nv_ref[...] = v
        d_ref[...] = -ADAM_LR * ((m / c1) / (jnp.sqrt(v / c2) + ADAM_EPS) + ADAM_WD * w_ref[...])

    block = (t, 1, cols) if thin else (t, cols)
    at = lambda r: (r, 0, 0) if thin else (r, 0)
    spec = pl.BlockSpec(block, lambda i: at(i))
    g_specs = [pl.BlockSpec(block, lambda i, k=k: at(jnp.clip(i - k * per, 0, per - 1))) for k in range(n_parts)]
    shp = jax.ShapeDtypeStruct(w.shape, F32)
    return pl.pallas_call(body, name=name, grid=(rows // t,), in_specs=[spec] * 3 + g_specs, out_specs=[spec] * 4,
                          out_shape=[shp] * 4, compiler_params=_params(("arbitrary",)))(w, m, v, *g_parts)


_WEIGHTS = ["mix_norm", "mlp_norm", "mlp_w1", "mlp_w2", "lru_w_in", "lru_conv_w", "lru_conv_b", "lru_w_r", "lru_b_r",
            "lru_w_i", "lru_b_i", "lru_lambda", "lru_w_out", "fox_w_in", "fox_b_f", "fox_q_gain", "fox_k_gain",
            "fox_w_out"]
_REPLICATED = ["mix_norm", "mlp_norm", "lru_conv_b", "lru_w_r", "lru_b_r", "lru_w_i", "lru_b_i", "lru_lambda",
               "fox_b_f", "fox_q_gain", "fox_k_gain"]
_PACK_TILE = 2 * SUBLANES * LANES


def _as2d(a):
    return a.reshape(-1, a.shape[-1])


def kernel(x, mix_norm, mlp_norm, mlp_w1, mlp_w2, lru_w_in, lru_conv_w, lru_conv_b, lru_w_r, lru_b_r, lru_w_i, lru_b_i, lru_lambda, lru_w_out, fox_w_in, fox_b_f, fox_q_gain, fox_k_gain, fox_w_out, loss_target, m_mix_norm, m_mlp_norm, m_mlp_w1, m_mlp_w2, m_lru_w_in, m_lru_conv_w, m_lru_conv_b, m_lru_w_r, m_lru_b_r, m_lru_w_i, m_lru_b_i, m_lru_lambda, m_lru_w_out, m_fox_w_in, m_fox_b_f, m_fox_q_gain, m_fox_k_gain, m_fox_w_out, v_mix_norm, v_mlp_norm, v_mlp_w1, v_mlp_w2, v_lru_w_in, v_lru_conv_w, v_lru_conv_b, v_lru_w_r, v_lru_b_r, v_lru_w_i, v_lru_b_i, v_lru_lambda, v_lru_w_out, v_fox_w_in, v_fox_b_f, v_fox_q_gain, v_fox_k_gain, v_fox_w_out):
    args = dict(locals())
    W = {n: args[n] for n in _WEIGHTS}
    Mo = {n: args["m_" + n] for n in _WEIGHTS}
    Vo = {n: args["v_" + n] for n in _WEIGHTS}
    S, D = x.shape[1], x.shape[2]
    F = 4 * D
    H = D // HEAD_DIM
    NU = 3 * D + LANES
    FQ, DQ = F // N_CHIPS, D // N_CHIPS
    nfox = fox_w_in.shape[-1]
    chip = 2 * lax.axis_index("x") + lax.axis_index("y")
    core = lax.axis_index("c").astype(jnp.int32).reshape(1)

    cw_flat = jnp.pad(lru_conv_w.reshape(-1), (0, _PACK_TILE - CONV_WIDTH * DQ)).reshape(2 * SUBLANES, LANES)
    w1s, w2s = mlp_w1.astype(BF16), mlp_w2.astype(BF16)
    wv = {}
    small = {n: W[n] for n in _REPLICATED}
    scattered = {}
    members = {"g1": ["w2_1", "w1_1", "fox_out"], "g2": ["fox_in"], "g3": ["w2_0", "w1_0"], "g4": ["lru_out", "lru_in"]}
    swap_at = {"fox_bwd_prep": "g1", "fox_dh": "g2", "lru_dout": "g3"}
    scatter_at = {"attn_backward": "g1", "mlp0_dact": "g2", "lru_bwd": "g3", "lru_dh": "g4"}
    swapped = {}

    def shard_major(name, g):
        if name == "fox_in":
            return jnp.transpose(g[:, :nfox * N_CHIPS].reshape(D, N_CHIPS, nfox), (1, 0, 2))
        return g

    class Comm:
        @staticmethod
        def before(name, grads):
            if name == "mix0_norm":
                return _Gather([lru_w_in[0].astype(BF16)])
            if name == "lru_in":
                return _Gather([lru_w_out[0].astype(BF16), cw_flat])
            if name == "lru_fwd":
                return _Gather([w1s[0]])
            if name == "mlp0_up":
                return _Gather([w2s[0]])
            if name == "mlp0_down":
                return _Gather([fox_w_in[0].astype(BF16)])
            if name == "attn_forward":
                return _Gather([fox_w_out[0].astype(BF16), w1s[1], w2s[1]])
            if name in swap_at:
                group = swap_at[name]
                swapped[group] = [[shard_major(n, grads[n]) for n in members[group]], None]
                return _Swap(swapped[group][0])
            if name in scatter_at:
                group = scatter_at[name]
                if group not in swapped:
                    arrs = [shard_major(n, grads[n]) for n in members[group]]
                    swapped[group] = [arrs, _run_plan(f"{group}_pair_swap", _Swap(arrs))]
                arrs, sib = swapped[group]
                return _pair_partials(group, arrs, sib, [BF16] * len(arrs), core)
            return None

        @staticmethod
        def after(name, res, wv):
            if name == "mix0_norm":
                wv.update(lru_in=_View(res[0], "cs"))
            elif name == "lru_in":
                wv.update(lru_out=_View(res[0], "rs"))
                taps = res[1].reshape(N_CHIPS, -1)[:, :CONV_WIDTH * DQ].reshape(N_CHIPS, CONV_WIDTH, DQ)
                small["conv_w"] = jnp.transpose(taps, (1, 0, 2)).reshape(CONV_WIDTH, D)
            elif name == "lru_fwd":
                wv.update(w1_0=_View(res[0], "cs"))
            elif name == "mlp0_up":
                wv.update(w2_0=_View(res[0], "rs"))
            elif name == "mlp0_down":
                fox_full = jnp.concatenate([res[0][s] for s in range(N_CHIPS)], axis=1)
                fox_full = jnp.pad(fox_full, ((0, 0), (0, NU - fox_full.shape[1])))
                wv.update(fox_in=_View(fox_full.T))
            elif name == "attn_forward":
                wv.update(fox_out=_View(res[0], "rs"), w1_1=_View(res[1], "cs"), w2_1=_View(res[2], "rs"))
            elif name in swap_at:
                swapped[swap_at[name]][1] = res
            else:
                scattered.update(zip(members[scatter_at[name]], res))

    def grad_view(grads, name):
        if name in ("w1_0", "w1_1"):
            return _View(None, "cs", shape=(N_CHIPS, D, FQ), dtype=BF16)
        if name in ("w2_0", "w2_1"):
            return _View(None, "rs", shape=(N_CHIPS, FQ, D), dtype=BF16)
        if name == "lru_in":
            return _View(None, "cs", shape=(N_CHIPS, D, 2 * D // N_CHIPS), dtype=BF16)
        if name in ("lru_out", "fox_out"):
            return _View(None, "rs", shape=(N_CHIPS, DQ, D), dtype=BF16)
        return _View(None, shape=(D, NU), dtype=BF16)

    loss, gx, grads = _local_step(x[0], loss_target[0], small, wv, grad_view, Comm)

    pack_names = _REPLICATED + ["conv_w"]
    flat = jnp.concatenate([grads[n].reshape(-1).astype(F32) for n in pack_names] + [loss.reshape(-1)])
    per_chip = -(-flat.shape[0] // (N_CHIPS * _PACK_TILE)) * _PACK_TILE
    pack = jnp.pad(flat, (0, N_CHIPS * per_chip - flat.shape[0])).reshape(N_CHIPS, per_chip // LANES, LANES)
    pack_sib = _run_plan("pack_pair_swap", _Swap([pack]))
    (scattered["pack"],) = _run_plan("pack_chip_scatter", _pair_partials("pack", [pack], pack_sib, [F32], core))
    order = ["w1_0", "w1_1", "w2_0", "w2_1", "lru_in", "lru_out", "fox_in", "fox_out", "pack"]
    red = dict(zip(order, _finish_reduce("grads", [scattered[n] for n in order])))
    (all_pack,) = _all_gather("gather_small_grads", [red["pack"]])
    all_flat = all_pack.reshape(-1)
    G = {}
    off = 0
    for n in pack_names:
        shape = grads[n].shape if n == "conv_w" else W[n].shape
        size = int(np.prod(shape))
        G[n] = all_flat[off:off + size].reshape(shape)
        off += size
    total = all_flat[off]
    G["lru_conv_w"] = lax.dynamic_slice_in_dim(G.pop("conv_w"), chip * DQ, DQ, axis=1)[None]
    parts = {n: [_as2d(G[n])] for n in G}
    parts.update(mlp_w1=[red["w1_0"], red["w1_1"]], mlp_w2=[red["w2_0"], red["w2_1"]], lru_w_in=[red["lru_in"]],
                 lru_w_out=[red["lru_out"]], fox_w_in=[red["fox_in"]], fox_w_out=[red["fox_out"]])

    delta, new_m, new_v = {}, {}, {}
    for n in _WEIGHTS:
        if W[n].shape[-1] % LANES and W[n].shape[-2] % LANES == 0:
            to_thin = lambda a: jnp.transpose(a, (2, 0, 1))
            res = _adamw(f"adamw_{n}", to_thin(W[n]), [to_thin(parts[n][0][None])], to_thin(Mo[n]), to_thin(Vo[n]))
            G[n], delta[n], new_m[n], new_v[n] = (jnp.transpose(t, (1, 2, 0)) for t in res)
            continue
        go, d, nm, nv = _adamw(f"adamw_{n}", _as2d(W[n]), parts[n], _as2d(Mo[n]), _as2d(Vo[n]))
        G[n], delta[n], new_m[n], new_v[n] = (t.reshape(W[n].shape) for t in (go, d, nm, nv))

    return (total, gx[None], *[G[n] for n in _WEIGHTS], *[delta[n] for n in _WEIGHTS],
            *[new_m[n] for n in _WEIGHTS], *[new_v[n] for n in _WEIGHTS])
```

```python
import functools

import numpy as np
import jax
import jax.numpy as jnp
from jax import lax
from jax.experimental import pallas as pl
from jax.experimental.pallas import tpu as pltpu

F32 = jnp.float32
BF16 = jnp.bfloat16

HEAD_DIM = 64
LRU_BLOCK_DIM = 64
CONV_WIDTH = 4
LRU_C = 8.0
EPS = 1e-6
NEG_INF = -1e30
ADAM_LR = 0.001
ADAM_B1 = 0.9
ADAM_B2 = 0.999
ADAM_EPS = 1e-08
ADAM_WD = 0.01
ADAM_STEP = 10

N_CHIPS = 4
LANES = 128
SUBLANES = 8
MXU_DIM = 256
VMEM_LIMIT = 52 * 1024 * 1024
MESH = pl.DeviceIdType.MESH
ANY = pl.BlockSpec(memory_space=pl.ANY)


def _pick(n, prefs):
    for p in prefs:
        if p <= n and n % p == 0:
            return p
    return n


def _params(sem=None):
    return pltpu.CompilerParams(dimension_semantics=sem, vmem_limit_bytes=VMEM_LIMIT)


class _View:
    def __init__(self, arr, kind="plain", shape=None, dtype=None):
        self.arr = arr
        self.kind = kind
        self.shape = tuple(arr.shape) if arr is not None else tuple(shape)
        self.dtype = arr.dtype if arr is not None else dtype

    def limits(self):
        if self.kind == "plain":
            return 0, 0
        return self.shape[-2], (self.shape[-1] if self.kind == "cs" else 0)

    def spec(self, br, bc, fr, fc):
        if self.kind == "plain":
            return pl.BlockSpec((br, bc), lambda *g: (fr(*g), fc(*g)))
        rows, ncol = self.shape[-2:]
        assert rows % br == 0 and ncol % bc == 0, (self.shape, br, bc)
        if self.kind == "cs":
            per = ncol // bc
            return pl.BlockSpec((None, br, bc), lambda *g: (fc(*g) // per, fr(*g), fc(*g) % per))
        per = rows // br
        return pl.BlockSpec((None, br, bc), lambda *g: (fr(*g) // per, fr(*g) % per, fc(*g)))


def _bf(x):
    return x if x.dtype == BF16 else x.astype(BF16)


def _matmul(name, A, B, M, N, K, *, ta=False, tb=False, outs, epilogue, extras=(), vecs=(), n_sums=0,
            tm=None, tn=None, tk=None, plan=None):
    lim = {"m": [M], "n": [N], "k": [K]}
    for view, (rdim, cdim) in ([(A, "km" if ta else "mk"), (B, "nk" if tb else "kn")]
                               + [(e, "mn") for e in extras] + [(o, "mn") for o in outs]):
        r_lim, c_lim = view.limits()
        lim[rdim].append(r_lim)
        lim[cdim].append(c_lim)
    tm = tm or _pick(int(np.gcd.reduce(lim["m"])), (1024, 640, 512, 256, 128))
    tn = tn or _pick(int(np.gcd.reduce(lim["n"])), (1024, 640, 512, 256, 128))
    tk = tk or _pick(int(np.gcd.reduce(lim["k"])), (1024, 640, 512, 256, 128))
    nk = K // tk
    gi = lambda i, j, k: i
    gj = lambda i, j, k: j
    gk = lambda i, j, k: k
    a_spec = A.spec(tk, tm, gk, gi) if ta else A.spec(tm, tk, gi, gk)
    b_spec = B.spec(tn, tk, gj, gk) if tb else B.spec(tk, tn, gk, gj)
    ca = 0 if ta else 1
    cb = 1 if tb else 0
    ne, no = len(extras) + len(vecs), len(outs)
    assert n_sums == 0 or tn == N
    row_spec = pl.BlockSpec((1, tn), lambda i, j, k: (0, j))
    in_specs = [a_spec, b_spec] + [e.spec(tm, tn, gi, gj) for e in extras] + [row_spec] * len(vecs)
    operands = [A.arr, B.arr] + [e.arr for e in extras] + list(vecs)
    out_specs = [o.spec(tm, tn, gi, gj) for o in outs] + [row_spec] * n_sums
    out_shape = ([jax.ShapeDtypeStruct(o.shape, o.dtype) for o in outs]
                 + [jax.ShapeDtypeStruct((1, N), F32)] * n_sums)

    def body(*refs):
        a_ref, b_ref = refs[0], refs[1]
        ex = refs[2:2 + ne]
        o_refs = refs[2 + ne:2 + ne + no]
        s_refs = refs[2 + ne + no:2 + ne + no + n_sums]
        first_row_tile = pl.program_id(0) == 0

        def prod():
            return lax.dot_general(_bf(a_ref[...]), _bf(b_ref[...]), (((ca,), (cb,)), ((), ())),
                                   preferred_element_type=F32)

        def finish(acc):
            res = epilogue(acc, *[e[...] for e in ex])
            for o_ref, r in zip(o_refs, res[:no]):
                o_ref[...] = r.astype(o_ref.dtype)
            for s_ref, r in zip(s_refs, res[no:]):
                def assign(s_ref=s_ref, r=r):
                    s_ref[...] = r

                def accumulate(s_ref=s_ref, r=r):
                    s_ref[...] += r

                pl.when(first_row_tile)(assign)
                pl.when(jnp.logical_not(first_row_tile))(accumulate)

        if nk == 1:
            finish(prod())
        else:
            acc_ref = refs[-1]
            k = pl.program_id(2)

            @pl.when(k == 0)
            def _():
                acc_ref[...] = jnp.zeros_like(acc_ref)

            acc_ref[...] += prod()

            @pl.when(k == nk - 1)
            def _():
                finish(acc_ref[...])

    res, side = _hosted_call(body, name, (M // tm, N // tn, nk), in_specs, out_specs, out_shape,
                             [pltpu.VMEM((tm, tn), F32)] if nk > 1 else [], operands,
                             ("arbitrary", "arbitrary", "arbitrary"), plan)
    return res if plan is None else (res, side)


def _ep_store(acc):
    return (acc,)


def _ep_resid(acc, res):
    return (res + acc,)


def _ep_resid_norm(acc, res, g):
    xo = res + acc
    r = lax.rsqrt(jnp.mean(xo * xo, axis=-1, keepdims=True) + EPS)
    return (xo, (xo * r) * g)


def _ep_norm_bwd(acc, x, dres, g):
    r = lax.rsqrt(jnp.mean(x * x, axis=-1, keepdims=True) + EPS)
    xhat = x * r
    dxn = acc * g
    tot = dres + r * (dxn - xhat * jnp.mean(dxn * xhat, axis=-1, keepdims=True))
    return (tot, tot, jnp.sum(acc * xhat, axis=0, keepdims=True))


def _ep_relu2(acc):
    zp = jnp.maximum(acc, 0.0)
    return (zp * zp,)


def _ep_drelu2(acc, act):
    return (acc * (2.0 * jnp.sqrt(act.astype(F32))),)


def _fresh(M, N, dtype):
    return _View(None, shape=(M, N), dtype=dtype)


def _rms_fwd(name, x, g, S, D, plan=None):
    T = _pick(S, (512, 256, 128))

    def body(x_ref, g_ref, h_ref):
        x = x_ref[...]
        r = lax.rsqrt(jnp.mean(x * x, axis=-1, keepdims=True) + EPS)
        h_ref[...] = ((x * r) * g_ref[...]).astype(BF16)

    return _hosted_call(body, name, (S // T,),
                        [pl.BlockSpec((T, D), lambda i: (i, 0)), pl.BlockSpec((1, D), lambda i: (0, 0))],
                        [pl.BlockSpec((T, D), lambda i: (i, 0))], [jax.ShapeDtypeStruct((S, D), BF16)], [], (x, g),
                        ("arbitrary",), plan)


def _loss_head(x, tgt, S, D):
    T = _pick(S, (512, 256, 128))

    def body(x_ref, t_ref, loss_ref, d_ref, db_ref):
        @pl.when(pl.program_id(0) == 0)
        def _():
            loss_ref[...] = jnp.zeros_like(loss_ref)

        e = x_ref[...] - t_ref[...]
        loss_ref[...] += 0.5 * jnp.sum(jnp.mean(e * e, axis=-1, keepdims=True), axis=0, keepdims=True)
        d = e * (1.0 / D)
        d_ref[...] = d
        db_ref[...] = d.astype(BF16)

    row = pl.BlockSpec((T, D), lambda i: (i, 0))
    return pl.pallas_call(
        body, name="loss_head", grid=(S // T,), in_specs=[row, row],
        out_specs=[pl.BlockSpec((1, 1), lambda i: (0, 0)), row, row],
        out_shape=[jax.ShapeDtypeStruct((1, 1), F32), jax.ShapeDtypeStruct((S, D), F32),
                   jax.ShapeDtypeStruct((S, D), BF16)],
        compiler_params=_params(("arbitrary",)),
    )(x, tgt)


def _sigmoid(z):
    return 1.0 / (1.0 + jnp.exp(-z))


def _log_sigmoid(z):
    return jnp.minimum(z, 0.0) - jnp.log(1.0 + jnp.exp(-jnp.abs(z)))


_GELU_K = 0.7978845608028654
_GELU_C = 0.044715


def _gelu(x):
    t = jnp.tanh(_GELU_K * (x + _GELU_C * (x * x * x)))
    return 0.5 * x * (1.0 + t)


def _gelu_and_grad(x):
    x2 = x * x
    t = jnp.tanh(_GELU_K * (x + _GELU_C * (x2 * x)))
    g = 0.5 * x * (1.0 + t)
    dg = 0.5 * (1.0 + t) + 0.5 * x * (1.0 - t * t) * (_GELU_K * (1.0 + 3.0 * _GELU_C * x2))
    return g, dg


def _decay_terms(r, ls):
    la = LRU_C * r * ls
    a = jnp.exp(la)
    a2 = a * a
    mult = jnp.sqrt(-jnp.tanh(la) * (a2 + 1.0))
    return a, a2, mult


def _lru_fwd(u0, conv_w, conv_b, wr_bd, b_r, wi_bd, b_i, lam, S, D, plan=None):
    T = _pick(S, (256, 128))
    GT = wr_bd.shape[-1]
    nG = D // GT

    def body(gb_ref, xb_ref, cw_ref, cb_ref, wr_ref, br_ref, wi_ref, bi_ref, lam_ref,
             y_ref, xc_ref, r_ref, i_ref, hs_ref, ext, a_scr, hcar):
        @pl.when(pl.program_id(0) == 0)
        def _():
            ext[0:SUBLANES, :] = jnp.zeros((SUBLANES, D), F32)
            hcar[...] = jnp.zeros_like(hcar)

        xb = xb_ref[...]
        ext[SUBLANES:SUBLANES + T, :] = xb
        xc = cb_ref[...]
        for k in range(CONV_WIDTH):
            xc = xc + ext[pl.ds(SUBLANES - (CONV_WIDTH - 1) + k, T), :] * cw_ref[k:k + 1, :]
        ext[0:SUBLANES, :] = xb[T - SUBLANES:T, :]
        xc_ref[...] = xc
        xcb = xc.astype(BF16)
        for g in range(nG):
            sl = slice(g * GT, (g + 1) * GT)
            zr = jnp.dot(xcb[:, sl], wr_ref[g], preferred_element_type=F32) + br_ref[:, sl]
            zi = jnp.dot(xcb[:, sl], wi_ref[g], preferred_element_type=F32) + bi_ref[:, sl]
            r_ref[:, sl] = _sigmoid(zr)
            i_ref[:, sl] = _sigmoid(zi)
        r = r_ref[...]
        a, _, mult = _decay_terms(r, _log_sigmoid(lam_ref[...]))
        a_scr[...] = a
        hs_ref[...] = mult * (i_ref[...] * xc)

        def step(t, h):
            h = a_scr[pl.ds(t, 1), :] * h + hs_ref[pl.ds(t, 1), :]
            hs_ref[pl.ds(t, 1), :] = h
            return h

        hcar[...] = lax.fori_loop(0, T, step, hcar[...], unroll=8)
        y_ref[...] = (_gelu(gb_ref[...]) * hs_ref[...]).astype(BF16)

    row = pl.BlockSpec((T, D), lambda i: (i, 0))
    vec = pl.BlockSpec((1, D), lambda i: (0, 0))
    bd = pl.BlockSpec((nG, GT, GT), lambda i: (0, 0, 0))
    f32o = jax.ShapeDtypeStruct((S, D), F32)
    return _hosted_call(
        body, "lru_fwd", (S // T,),
        [row, pl.BlockSpec((T, D), lambda i: (i, 1)), pl.BlockSpec((CONV_WIDTH, D), lambda i: (0, 0)), vec,
         bd, vec, bd, vec, vec],
        [row, row, row, row, row], [jax.ShapeDtypeStruct((S, D), BF16), f32o, f32o, f32o, f32o],
        [pltpu.VMEM((T + SUBLANES, D), F32), pltpu.VMEM((T, D), F32), pltpu.VMEM((1, D), F32)],
        (u0, u0, conv_w, conv_b, wr_bd, b_r, wi_bd, b_i, lam), ("arbitrary",), plan)


def _lru_bwd(dy, u0, xc, r, ig, hs, conv_w, wr_bd, wi_bd, lam, S, D, plan=None):
    T = _pick(S, (128,))
    nT = S // T
    GT = wr_bd.shape[-1]
    nG = D // GT
    W = CONV_WIDTH

    def body(dy_ref, gb_ref, xb_ref, xbp_ref, xc_ref, r_ref, i_ref, hs_ref, hsp_ref, cw_ref, wr_ref, wi_ref, lam_ref,
             du_ref, dcw_ref, dcb_ref, dlam_ref, dbr_ref, dbi_ref, dwr_ref, dwi_ref,
             a_scr, dh_scr, exth, extx, extd, dxc_scr, dz_scr, carry):
        step = pl.program_id(0)
        first_tile = step == nT - 1

        @pl.when(step == 0)
        def _():
            for ref in (dcw_ref, dcb_ref, dlam_ref, dbr_ref, dbi_ref, dwr_ref, dwi_ref, carry):
                ref[...] = jnp.zeros_like(ref)
            extd[T:T + SUBLANES, :] = jnp.zeros((SUBLANES, D), F32)

        hs = hs_ref[...]
        dy = dy_ref[...]
        g, dgelu = _gelu_and_grad(gb_ref[...])
        du_ref[:, 0:D] = (dy * hs * dgelu).astype(BF16)
        r = r_ref[...]
        lam = lam_ref[...]
        ls = _log_sigmoid(lam)
        a, a2, mult = _decay_terms(r, ls)
        a_scr[...] = a
        dh_scr[...] = dy * g

        def rstep(j, c):
            t = T - 1 - j
            d = dh_scr[pl.ds(t, 1), :] + c
            dh_scr[pl.ds(t, 1), :] = d
            return a_scr[pl.ds(t, 1), :] * d

        carry[...] = lax.fori_loop(0, T, rstep, carry[...], unroll=8)
        dh = dh_scr[...]
        keep = jnp.where(first_tile, 0.0, 1.0)
        exth[0:SUBLANES, :] = hsp_ref[...] * keep
        exth[SUBLANES:SUBLANES + T, :] = hs
        hprev = exth[pl.ds(SUBLANES - 1, T), :]
        xc = xc_ref[...]
        ig = i_ref[...]
        da = dh * hprev
        dmult = dh * (ig * xc)
        dla = da * a - dmult * (a2 / mult)
        dlam_ref[...] += jnp.sum(dla * r, axis=0, keepdims=True) * (LRU_C * _sigmoid(-lam))
        dzr = (dla * (LRU_C * ls)) * (r * (1.0 - r))
        dzi = (dh * (mult * xc)) * (ig * (1.0 - ig))
        dbr_ref[...] += jnp.sum(dzr, axis=0, keepdims=True)
        dbi_ref[...] += jnp.sum(dzi, axis=0, keepdims=True)
        dxc_scr[...] = dh * (mult * ig)
        xcb = xc.astype(BF16)
        dz_scr[0] = dzr.astype(BF16)
        dz_scr[1] = dzi.astype(BF16)
        nt_dims = (((1,), (1,)), ((), ()))
        tn_dims = (((0,), (0,)), ((), ()))
        for gq in range(nG):
            sl = slice(gq * GT, (gq + 1) * GT)
            zr_g = dz_scr[0, :, sl]
            zi_g = dz_scr[1, :, sl]
            dxc_scr[:, sl] += (lax.dot_general(zr_g, wr_ref[gq], nt_dims, preferred_element_type=F32)
                               + lax.dot_general(zi_g, wi_ref[gq], nt_dims, preferred_element_type=F32))
            dwr_ref[gq] += lax.dot_general(xcb[:, sl], zr_g, tn_dims, preferred_element_type=F32)
            dwi_ref[gq] += lax.dot_general(xcb[:, sl], zi_g, tn_dims, preferred_element_type=F32)
        dxc = dxc_scr[...]
        dcb_ref[...] += jnp.sum(dxc, axis=0, keepdims=True)
        extx[0:SUBLANES, :] = xbp_ref[...] * keep
        extx[SUBLANES:SUBLANES + T, :] = xb_ref[...]
        extd[0:T, :] = dxc
        dxb = jnp.zeros((T, D), F32)
        for k in range(W):
            dxb = dxb + extd[pl.ds(W - 1 - k, T), :] * cw_ref[k:k + 1, :]
            dcw_ref[k:k + 1, :] += jnp.sum(dxc * extx[pl.ds(SUBLANES - (W - 1) + k, T), :], axis=0, keepdims=True)
        extd[T:T + SUBLANES, :] = dxc[0:SUBLANES, :]
        du_ref[:, D:2 * D] = dxb.astype(BF16)

    rev = lambda i: nT - 1 - i
    tpb = T // SUBLANES
    prev8 = lambda i: jnp.maximum(rev(i) * tpb - 1, 0)
    row = pl.BlockSpec((T, D), lambda i: (rev(i), 0))
    vec = pl.BlockSpec((1, D), lambda i: (0, 0))
    bd = pl.BlockSpec((nG, GT, GT), lambda i: (0, 0, 0))
    vec_o = jax.ShapeDtypeStruct((1, D), F32)
    bd_o = jax.ShapeDtypeStruct((nG, GT, GT), F32)
    return _hosted_call(
        body, "lru_bwd", (nT,),
        [row, row, pl.BlockSpec((T, D), lambda i: (rev(i), 1)), pl.BlockSpec((SUBLANES, D), lambda i: (prev8(i), 1)),
         row, row, row, row, pl.BlockSpec((SUBLANES, D), lambda i: (prev8(i), 0)),
         pl.BlockSpec((W, D), lambda i: (0, 0)), bd, bd, vec],
        [pl.BlockSpec((T, 2 * D), lambda i: (rev(i), 0)), pl.BlockSpec((W, D), lambda i: (0, 0)),
         vec, vec, vec, vec, bd, bd],
        [jax.ShapeDtypeStruct((S, 2 * D), BF16), jax.ShapeDtypeStruct((W, D), F32), vec_o, vec_o, vec_o, vec_o, bd_o, bd_o],
        [pltpu.VMEM((T, D), F32), pltpu.VMEM((T, D), F32), pltpu.VMEM((T + SUBLANES, D), F32),
         pltpu.VMEM((T + SUBLANES, D), F32), pltpu.VMEM((T + SUBLANES, D), F32),
         pltpu.VMEM((T, D), F32), pltpu.VMEM((2, T, D), BF16), pltpu.VMEM((1, D), F32)],
        (dy, u0, u0, u0, xc, r, ig, hs, hs, conv_w, wr_bd, wi_bd, lam), ("arbitrary",), plan)


AUG_ROWS = 16
HEAD_ROWS = 128
LSE_ROW = HEAD_DIM + 6
ONES_ROW_Q = HEAD_DIM + 3
ONES_COL_K = HEAD_DIM
ONES_ROW_V = HEAD_DIM
PREP_LANES = 512
HEAD_UNROLL = 4


def _split3(x):
    b1 = x.astype(BF16).astype(F32)
    r = x - b1
    b2 = r.astype(BF16).astype(F32)
    return b1, b2, r - b2


def _head_block(x, aug, T):
    row = lax.broadcasted_iota(jnp.int32, (AUG_ROWS, T), 0)
    blk = jnp.zeros((AUG_ROWS, T), F32)
    for i, e in enumerate(aug):
        blk = jnp.where(row == i, e, blk)
    return jnp.concatenate([x, blk, jnp.zeros((HEAD_ROWS - HEAD_DIM - AUG_ROWS, T), F32)], axis=0)


def _tri_matrix(lower):
    i = np.arange(LANES)
    m = (i[:, None] >= i[None, :]) if lower else (i[:, None] <= i[None, :])
    return jnp.asarray(m.astype(np.float32), BF16)


def _lane_cumsum(x, tri_ref, carry, reverse):
    n = x.shape[1] // LANES
    tri = tri_ref[...]
    out = [None] * n
    for j in (range(n - 1, -1, -1) if reverse else range(n)):
        cs = carry
        for part in _split3(x[:, j * LANES:(j + 1) * LANES]):
            cs = cs + jnp.dot(part.astype(BF16), tri, preferred_element_type=F32)
        out[j] = cs
        carry = cs[:, 0:1] if reverse else cs[:, LANES - 1:LANES]
    return jnp.concatenate(out, axis=1), carry


def _head_rows(h):
    return pl.ds(pl.multiple_of(h * HEAD_DIM, HEAD_DIM), HEAD_DIM)


def _fox_prep(ut, b_f, qg, kg, S, D, tq):
    H = D // HEAD_DIM
    T = min(tq, PREP_LANES)
    per = tq // T
    scale = HEAD_DIM ** -0.5

    def body(q_ref, k_ref, v_ref, f_ref, bf_ref, qg_ref, kg_ref, tri_ref,
             qat_ref, kat_ref, vat_ref, ka_ref, c_scr, ccar):
        @pl.when(pl.program_id(0) == 0)
        def _():
            ccar[...] = jnp.zeros_like(ccar)

        c, carry = _lane_cumsum(_log_sigmoid(f_ref[...] + bf_ref[...]), tri_ref, ccar[...], False)
        c_scr[...] = c
        ccar[...] = carry

        def head(h, _):
            rows = _head_rows(h)
            c1, c2, c3 = _split3(c_scr[pl.ds(h, 1), :])

            def normed(src, gain, mul):
                x = src[rows, :]
                rs = lax.rsqrt(jnp.mean(x * x, axis=0, keepdims=True) + EPS)
                return ((x * rs) * gain[rows, :]) * mul

            qat_ref[h] = _head_block(normed(q_ref, qg_ref, scale), [c1, c2, c3, 1.0, 1.0, 1.0], T).astype(BF16)
            kb = _head_block(normed(k_ref, kg_ref, 1.0), [1.0, 1.0, 1.0, -c1, -c2, -c3, 1.0, 1.0, 1.0], T)
            kat_ref[h] = kb.astype(BF16)
            ka_ref[h] = kb.T.astype(BF16)
            vat_ref[h] = _head_block(v_ref[rows, :], [1.0, 1.0, 1.0], T).astype(BF16)
            return 0

        lax.fori_loop(0, H, head, 0, unroll=HEAD_UNROLL)

    part = lambda j: pl.BlockSpec((D, T), lambda i: (j, i))
    colv = lambda n: pl.BlockSpec((n, 1), lambda i: (0, 0))
    tmaj = lambda r: pl.BlockSpec((H, None, r, T), lambda i: (0, i // per, 0, i % per))
    norm = pl.BlockSpec((H, T, HEAD_ROWS), lambda i: (0, i, 0))
    tshape = lambda r: jax.ShapeDtypeStruct((H, S // tq, r, tq), BF16)
    nshape = jax.ShapeDtypeStruct((H, S, HEAD_ROWS), BF16)
    return pl.pallas_call(
        body, name="fox_prep", grid=(S // T,),
        in_specs=[part(0), part(1), part(2), pl.BlockSpec((LANES, T), lambda i: (3 * D // LANES, i)),
                  colv(LANES), colv(D), colv(D), pl.BlockSpec((LANES, LANES), lambda i: (0, 0))],
        out_specs=[tmaj(HEAD_ROWS), tmaj(HEAD_ROWS), tmaj(HEAD_ROWS), norm],
        out_shape=[tshape(HEAD_ROWS), tshape(HEAD_ROWS), tshape(HEAD_ROWS), nshape],
        scratch_shapes=[pltpu.VMEM((LANES, T), F32), pltpu.VMEM((LANES, 1), F32)],
        compiler_params=_params(("arbitrary",)),
    )(ut, ut, ut, ut, b_f, qg, kg, _tri_matrix(False))


def _fox_bwd_prep(dot, ot, lse, qat, S, D, tq, plan=None):
    H = D // HEAD_DIM
    T = min(tq, PREP_LANES)
    per = tq // T

    def body(do_ref, o_ref, lse_ref, qat_ref, doat_ref, doa_ref, qat1_ref, qa1_ref):
        row = lax.broadcasted_iota(jnp.int32, (HEAD_ROWS, T), 0)

        def head(h, _):
            rows = _head_rows(h)
            do = do_ref[rows, :].astype(F32)
            delta = jnp.sum(do * o_ref[rows, :], axis=0, keepdims=True)
            db = _head_block(do, list(_split3(-delta)), T)
            doat_ref[h] = db.astype(BF16)
            doa_ref[h] = db.T.astype(BF16)
            qb = qat_ref[h].astype(F32)
            for i, e in enumerate(_split3(-lse_ref[h])):
                qb = jnp.where(row == LSE_ROW + i, e, qb)
            qat1_ref[h] = qb.astype(BF16)
            qa1_ref[h] = qb.T.astype(BF16)
            return 0

        lax.fori_loop(0, H, head, 0, unroll=HEAD_UNROLL)

    chan = pl.BlockSpec((D, T), lambda i: (0, i))
    tmaj = pl.BlockSpec((H, None, HEAD_ROWS, T), lambda i: (0, i // per, 0, i % per))
    norm = pl.BlockSpec((H, T, HEAD_ROWS), lambda i: (0, i, 0))
    tshape = jax.ShapeDtypeStruct((H, S // tq, HEAD_ROWS, tq), BF16)
    nshape = jax.ShapeDtypeStruct((H, S, HEAD_ROWS), BF16)
    return _hosted_call(body, "fox_bwd_prep", (S // T,), [chan, chan, pl.BlockSpec((H, 1, T), lambda i: (0, 0, i)), tmaj],
                        [tmaj, norm, tmaj, norm], [tshape, nshape, tshape, nshape], [], (dot, ot, lse, qat),
                        ("arbitrary",), plan)


def _causal(s, k_axis):
    t = min(s.shape)
    ki = lax.broadcasted_iota(jnp.int32, s.shape, k_axis) - (s.shape[k_axis] - t)
    qi = lax.broadcasted_iota(jnp.int32, s.shape, 1 - k_axis)
    return jnp.where(ki <= qi, s, NEG_INF)


def _seq_tile(i, t):
    return pl.ds(pl.multiple_of(i * t, t), t)


def _attn_forward(ka, qat, vat, S, D, tq, plan=None):
    H = D // HEAD_DIM
    nq = S // tq

    def body(ka_ref, qat_ref, vat_ref, o_ref, o32_ref, lse_ref, m_scr, acc_scr):
        qi = pl.program_id(1)
        m_scr[...] = jnp.full_like(m_scr, NEG_INF)
        acc_scr[...] = jnp.zeros_like(acc_scr)
        qa = qat_ref[...]

        def span(k0, n, diagonal):
            s = jnp.dot(ka_ref[pl.ds(pl.multiple_of(k0 * tq, tq), n * tq), :], qa, preferred_element_type=F32)
            if diagonal:
                s = _causal(s, 0)
            m_prev = m_scr[...]
            m_new = jnp.maximum(m_prev, jnp.max(s, axis=0, keepdims=True))
            p = jnp.exp(s - m_new).astype(BF16)
            upd = jnp.dot(vat_ref[k0], p[0:tq], preferred_element_type=F32)
            for i in range(1, n):
                upd = upd + jnp.dot(vat_ref[k0 + i], p[i * tq:(i + 1) * tq], preferred_element_type=F32)
            acc_scr[...] = jnp.exp(m_prev - m_new) * acc_scr[...] + upd
            m_scr[...] = m_new

        def off_diagonal_pair(j, _):
            span(2 * j, 2, False)
            return 0

        lax.fori_loop(0, qi // 2, off_diagonal_pair, 0)
        pl.when(qi % 2 == 1)(lambda: span(qi - 1, 2, True))
        pl.when(qi % 2 == 0)(lambda: span(qi, 1, True))
        l = acc_scr[ONES_ROW_V:ONES_ROW_V + 1, :]
        o = acc_scr[0:HEAD_DIM, :] / l
        o_ref[...] = o.astype(BF16)
        o32_ref[...] = o
        lse_ref[...] = m_scr[...] + jnp.log(l)

    chan = pl.BlockSpec((HEAD_DIM, tq), lambda h, i: (h, i))
    stat = pl.BlockSpec((None, 1, tq), lambda h, i: (h, 0, i))
    return _hosted_call(
        body, "attn_forward", (H, nq),
        [pl.BlockSpec((None, S, HEAD_ROWS), lambda h, i: (h, 0, 0)),
         pl.BlockSpec((None, None, HEAD_ROWS, tq), lambda h, i: (h, i, 0, 0)),
         pl.BlockSpec((None, nq, HEAD_ROWS, tq), lambda h, i: (h, 0, 0, 0))],
        [chan, chan, stat],
        [jax.ShapeDtypeStruct((D, S), BF16), jax.ShapeDtypeStruct((D, S), F32), jax.ShapeDtypeStruct((H, 1, S), F32)],
        [pltpu.VMEM((1, tq), F32), pltpu.VMEM((HEAD_ROWS, tq), F32)],
        (ka, qat, vat), ("arbitrary", "arbitrary"), plan)


def _attn_backward(qa, doa, qat, doat, ka, kat, vat, S, D, tq, plan=None):
    H = D // HEAD_DIM
    nq = S // tq

    def body(qa_ref, doa_ref, qat_ref, doat_ref, ka_ref, kat_ref, vat_ref, dq_ref, dk_ref, dv_ref, dk_scr, dv_scr):
        ki = pl.program_id(1)

        @pl.when(ki == 0)
        def _():
            dq_ref[...] = jnp.zeros_like(dq_ref)

        dk_scr[...] = jnp.zeros_like(dk_scr)
        dv_scr[...] = jnp.zeros_like(dv_scr)
        kt = kat_ref[...]
        vt = vat_ref[...]
        kn = ka_ref[...]

        def span(q0, n, diagonal):
            rows = pl.ds(pl.multiple_of(q0 * tq, tq), n * tq)
            s = jnp.dot(qa_ref[rows, :], kt, preferred_element_type=F32)
            if diagonal:
                s = _causal(s, 1)
            p = jnp.exp(s)
            ds = (p * jnp.dot(doa_ref[rows, :], vt, preferred_element_type=F32)).astype(BF16)
            p = p.astype(BF16)
            for i in range(n):
                part = slice(i * tq, (i + 1) * tq)
                dv_scr[...] += jnp.dot(doat_ref[q0 + i, 0:HEAD_DIM, :], p[part], preferred_element_type=F32)
                dk_scr[...] += jnp.dot(qat_ref[q0 + i], ds[part], preferred_element_type=F32)
            dq_ref[rows, :] += jnp.dot(ds, kn, preferred_element_type=F32)

        n_off = nq - 1 - ki
        odd = n_off % 2

        def off_diagonal_pair(j, _):
            span(ki + 1 + odd + 2 * j, 2, False)
            return 0

        pl.when(odd == 1)(lambda: span(ki, 2, True))
        pl.when(odd == 0)(lambda: span(ki, 1, True))
        lax.fori_loop(0, n_off // 2, off_diagonal_pair, 0)
        dk_ref[...] = dk_scr[...]
        dv_ref[...] = dv_scr[...].astype(BF16)

    whole = pl.BlockSpec((None, S, HEAD_ROWS), lambda h, i: (h, 0, 0))
    tiles = pl.BlockSpec((None, nq, HEAD_ROWS, tq), lambda h, i: (h, 0, 0, 0))
    one = pl.BlockSpec((None, None, HEAD_ROWS, tq), lambda h, i: (h, i, 0, 0))
    return _hosted_call(
        body, "attn_backward", (H, nq),
        [whole, whole, tiles, tiles, pl.BlockSpec((None, tq, HEAD_ROWS), lambda h, i: (h, i, 0)), one, one],
        [whole, pl.BlockSpec((None, HEAD_ROWS, tq), lambda h, i: (h, 0, i)),
         pl.BlockSpec((HEAD_DIM, tq), lambda h, i: (h, i))],
        [jax.ShapeDtypeStruct((H, S, HEAD_ROWS), F32), jax.ShapeDtypeStruct((H, HEAD_ROWS, S), F32),
         jax.ShapeDtypeStruct((D, S), BF16)],
        [pltpu.VMEM((HEAD_ROWS, tq), F32), pltpu.VMEM((HEAD_DIM, tq), F32)],
        (qa, doa, qat, doat, ka, kat, vat), ("arbitrary", "arbitrary"), plan)


def _fox_prep_bwd(ut, dq, dkt, dvt, b_f, qg, kg, S, D, tq):
    H = D // HEAD_DIM
    T = min(tq, PREP_LANES)
    nT = S // T
    NU = 3 * D + LANES
    scale = HEAD_DIM ** -0.5

    def body(q_ref, k_ref, f_ref, dq_ref, dk_ref, dv_ref, bf_ref, qg_ref, kg_ref, tri_ref,
             du_ref, dbf_ref, dqg_ref, dkg_ref, gq_acc, gk_acc, fcar, dc_scr):
        step = pl.program_id(0)

        @pl.when(step == 0)
        def _():
            for ref in (gq_acc, gk_acc, fcar, dbf_ref):
                ref[...] = jnp.zeros_like(ref)

        dc_scr[...] = jnp.zeros_like(dc_scr)

        def head(h, _):
            rows = _head_rows(h)
            dqb = dq_ref[h].T
            dkb = dk_ref[h]
            dc_scr[pl.ds(h, 1), :] = dqb[ONES_COL_K:ONES_COL_K + 1, :] - dkb[ONES_ROW_Q:ONES_ROW_Q + 1, :]
            for src, dsrc, gain, acc, mul, base in ((q_ref, dqb, qg_ref, gq_acc, scale, 0),
                                                    (k_ref, dkb, kg_ref, gk_acc, 1.0, D)):
                x = src[rows, :]
                rs = lax.rsqrt(jnp.mean(x * x, axis=0, keepdims=True) + EPS)
                xhat = x * rs
                dn = dsrc[0:HEAD_DIM, :] * mul
                acc[rows, :] += jnp.sum(dn * xhat, axis=1, keepdims=True)
                dxh = dn * gain[rows, :]
                dx = rs * (dxh - xhat * jnp.mean(dxh * xhat, axis=0, keepdims=True))
                du_ref[pl.ds(pl.multiple_of(base + h * HEAD_DIM, HEAD_DIM), HEAD_DIM), :] = dx.astype(BF16)
            return 0

        lax.fori_loop(0, H, head, 0, unroll=HEAD_UNROLL)
        du_ref[2 * D:3 * D, :] = dv_ref[...]
        dlf, carry = _lane_cumsum(dc_scr[...], tri_ref, fcar[...], True)
        fcar[...] = carry
        dfl = dlf * _sigmoid(-(f_ref[...] + bf_ref[...]))
        dbf_ref[...] += jnp.sum(dfl, axis=1, keepdims=True)
        du_ref[3 * D:NU, :] = dfl.astype(BF16)

        @pl.when(step == nT - 1)
        def _():
            for acc, ref in ((gq_acc, dqg_ref), (gk_acc, dkg_ref)):
                tot = jnp.zeros((HEAD_DIM, 1), F32)
                for h in range(H):
                    tot = tot + acc[h * HEAD_DIM:(h + 1) * HEAD_DIM, :]
                ref[...] = tot

    rev = lambda i: nT - 1 - i
    part = lambda j: pl.BlockSpec((D, T), lambda i: (j, rev(i)))
    colv = lambda n: pl.BlockSpec((n, 1), lambda i: (0, 0))
    return pl.pallas_call(
        body, name="fox_prep_bwd", grid=(nT,),
        in_specs=[part(0), part(1), pl.BlockSpec((LANES, T), lambda i: (3 * D // LANES, rev(i))),
                  pl.BlockSpec((H, T, HEAD_ROWS), lambda i: (0, rev(i), 0)),
                  pl.BlockSpec((H, HEAD_ROWS, T), lambda i: (0, 0, rev(i))), pl.BlockSpec((D, T), lambda i: (0, rev(i))),
                  colv(LANES), colv(D), colv(D), pl.BlockSpec((LANES, LANES), lambda i: (0, 0))],
        out_specs=[pl.BlockSpec((NU, T), lambda i: (0, rev(i))), colv(LANES), colv(HEAD_DIM), colv(HEAD_DIM)],
        out_shape=[jax.ShapeDtypeStruct((NU, S), BF16), jax.ShapeDtypeStruct((LANES, 1), F32),
                   jax.ShapeDtypeStruct((HEAD_DIM, 1), F32), jax.ShapeDtypeStruct((HEAD_DIM, 1), F32)],
        scratch_shapes=[pltpu.VMEM((D, 1), F32), pltpu.VMEM((D, 1), F32), pltpu.VMEM((LANES, 1), F32),
                        pltpu.VMEM((LANES, T), F32)],
        compiler_params=_params(("arbitrary",)),
    )(ut, ut, ut, dq, dkt, dvt, b_f, qg, kg, _tri_matrix(True))


def _block_diag_tiles(w):
    n = w.shape[0]
    per = min(MXU_DIM, n * LRU_BLOCK_DIM) // LRU_BLOCK_DIM
    eye = jnp.eye(per, dtype=w.dtype)
    w5 = w.reshape(n // per, per, LRU_BLOCK_DIM, 1, LRU_BLOCK_DIM) * eye[None, :, None, :, None]
    return w5.reshape(n // per, per * LRU_BLOCK_DIM, per * LRU_BLOCK_DIM).astype(BF16)


def _block_diag_extract(t, n):
    per = t.shape[-1] // LRU_BLOCK_DIM
    eye = jnp.eye(per, dtype=t.dtype)
    t5 = t.reshape(n // per, per, LRU_BLOCK_DIM, per, LRU_BLOCK_DIM) * eye[None, :, None, :, None]
    return t5.sum(axis=3).reshape(n, LRU_BLOCK_DIM, LRU_BLOCK_DIM)


def _local_step(x, tgt, small, wv, grad_view, comm=None):
    S, D = x.shape
    F = 4 * D
    H = D // HEAD_DIM
    nblk = D // LRU_BLOCK_DIM
    NU = 3 * D + LANES
    tq = max(LANES, min(512, S // 4))
    assert S % tq == 0
    vec = lambda a: a.reshape(1, -1).astype(F32)
    col = lambda a: a.reshape(-1, 1).astype(F32)
    mix_g, mlp_g = small["mix_norm"], small["mlp_norm"]
    conv_b = vec(small["lru_conv_b"])
    wr_bd, wi_bd = _block_diag_tiles(small["lru_w_r"][0]), _block_diag_tiles(small["lru_w_i"][0])
    b_r, b_i, lam = vec(small["lru_b_r"]), vec(small["lru_b_i"]), vec(small["lru_lambda"])
    b_f = jnp.pad(col(small["fox_b_f"]), ((0, LANES - H), (0, 0)))
    qg, kg = jnp.tile(col(small["fox_q_gain"]), (H, 1)), jnp.tile(col(small["fox_k_gain"]), (H, 1))
    X = lambda a: _View(a)
    grads = {}
    gout = functools.partial(grad_view, grads)

    def hosted(name, fn, *args):
        plan = comm.before(name, grads) if comm is not None else None
        res, side = fn(*args, plan=plan)
        if plan is not None:
            comm.after(name, side, wv)
        return res

    def hosted_mm(name, *args, **kw):
        plan = comm.before(name, grads) if comm is not None else None
        if plan is None:
            return _matmul(name, *args, **kw)
        res, side = _matmul(name, *args, plan=plan, **kw)
        comm.after(name, side, wv)
        return res

    two = lambda: [_fresh(S, D, F32), _fresh(S, D, BF16)]

    def mlp_up(l, hm):
        return hosted_mm(f"mlp{l}_up", X(hm), wv[f"w1_{l}"], S, F, D, outs=[_fresh(S, F, BF16)], epilogue=_ep_relu2)[0]

    def mlp_bwd(l, xin, hm, act, d, db):
        (dz,) = hosted_mm(f"mlp{l}_dact", X(db), wv[f"w2_{l}"], S, F, D, tb=True, outs=[_fresh(S, F, BF16)],
                          epilogue=_ep_drelu2, extras=[X(act)])
        (grads[f"w2_{l}"],) = _matmul(f"mlp{l}_dw2", X(act), X(db), F, D, S, ta=True, outs=[gout(f"w2_{l}")],
                                      epilogue=_ep_store)
        (grads[f"w1_{l}"],) = _matmul(f"mlp{l}_dw1", X(hm), X(dz), D, F, S, ta=True, outs=[gout(f"w1_{l}")],
                                      epilogue=_ep_store)
        return _matmul(f"mlp{l}_dhm", X(dz), wv[f"w1_{l}"], S, D, F, tb=True, outs=two(), n_sums=1,
                       epilogue=_ep_norm_bwd, extras=[X(xin), X(d)], vecs=[mlp_g[l:l + 1]])

    (h0,) = hosted("mix0_norm", _rms_fwd, "mix0_norm", x, mix_g[0:1], S, D)
    (u0,) = hosted_mm("lru_in", X(h0), wv["lru_in"], S, 2 * D, D, outs=[_fresh(S, 2 * D, F32)], epilogue=_ep_store)
    conv_w = small["conv_w"]
    y, xc, r, ig, hs = hosted("lru_fwd", _lru_fwd, u0, conv_w, conv_b, wr_bd, b_r, wi_bd, b_i, lam, S, D)
    x1, hm0 = _matmul("lru_out", X(y), wv["lru_out"], S, D, D, outs=two(), epilogue=_ep_resid_norm, extras=[X(x)],
                      vecs=[mlp_g[0:1]])
    act0 = mlp_up(0, hm0)
    x2, h1 = hosted_mm("mlp0_down", X(act0), wv["w2_0"], S, D, F, outs=two(), epilogue=_ep_resid_norm, extras=[X(x1)],
                       vecs=[mix_g[1:2]])
    (u1,) = _matmul("fox_in", wv["fox_in"], X(h1), NU, S, D, tb=True, outs=[_fresh(NU, S, F32)], epilogue=_ep_store)
    qat, kat, vat, ka = _fox_prep(u1, b_f, qg, kg, S, D, tq)
    o, o32, lse = hosted("attn_forward", _attn_forward, ka, qat, vat, S, D, tq)
    x3, hm1 = _matmul("fox_out", X(o), wv["fox_out"], S, D, D, ta=True, outs=two(), epilogue=_ep_resid_norm,
                      extras=[X(x2)], vecs=[mlp_g[1:2]])
    act1 = mlp_up(1, hm1)
    (x4,) = _matmul("mlp1_down", X(act1), wv["w2_1"], S, D, F, outs=[_fresh(S, D, F32)], epilogue=_ep_resid,
                    extras=[X(x3)])
    loss, d4, d4b = _loss_head(x4, tgt, S, D)

    d3, d3b, dg_mlp1 = mlp_bwd(1, x3, hm1, act1, d4, d4b)
    (do,) = _matmul("fox_dout", wv["fox_out"], X(d3b), D, S, D, tb=True, outs=[_fresh(D, S, BF16)], epilogue=_ep_store)
    (grads["fox_out"],) = _matmul("fox_dwout", X(o), X(d3b), D, D, S, outs=[gout("fox_out")], epilogue=_ep_store)
    doat, doa, qat1, qa1 = hosted("fox_bwd_prep", _fox_bwd_prep, do, o32, lse, qat, S, D, tq)
    dqn, dkn, dv = hosted("attn_backward", _attn_backward, qa1, doa, qat1, doat, ka, kat, vat, S, D, tq)
    du1, dbf, dqg, dkg = _fox_prep_bwd(u1, dqn, dkn, dv, b_f, qg, kg, S, D, tq)
    (grads["fox_in"],) = _matmul("fox_dwin", X(du1), X(h1), NU, D, S, outs=[gout("fox_in")], epilogue=_ep_store)
    d2, d2b, dg_mix1 = hosted_mm("fox_dh", X(du1), wv["fox_in"], S, D, NU, ta=True, outs=two(), n_sums=1,
                               epilogue=_ep_norm_bwd, extras=[X(x2), X(d3)], vecs=[mix_g[1:2]])
    d1, d1b, dg_mlp0 = mlp_bwd(0, x1, hm0, act0, d2, d2b)
    (grads["lru_out"],) = _matmul("lru_dwout", X(y), X(d1b), D, D, S, ta=True, outs=[gout("lru_out")],
                                  epilogue=_ep_store)
    (dy,) = hosted_mm("lru_dout", X(d1b), wv["lru_out"], S, D, D, tb=True, outs=[_fresh(S, D, F32)],
                      epilogue=_ep_store)
    du0, dcw, dcb, dlam, dbr, dbi, dwr, dwi = hosted("lru_bwd", _lru_bwd, dy, u0, xc, r, ig, hs, conv_w, wr_bd, wi_bd,
                                                     lam, S, D)
    (grads["lru_in"],) = _matmul("lru_dwin", X(h0), X(du0), D, 2 * D, S, ta=True, outs=[gout("lru_in")],
                                 epilogue=_ep_store)
    gx, dg_mix0 = hosted_mm("lru_dh", X(du0), wv["lru_in"], S, D, 2 * D, tb=True, outs=[_fresh(S, D, F32)], n_sums=1,
                            epilogue=lambda *a: _ep_norm_bwd(*a)[::2], extras=[X(x), X(d1)], vecs=[mix_g[0:1]])

    grads.update(
        mix_norm=jnp.concatenate([dg_mix0, dg_mix1], axis=0), mlp_norm=jnp.concatenate([dg_mlp0, dg_mlp1], axis=0),
        conv_w=dcw, lru_conv_b=dcb, lru_w_r=_block_diag_extract(dwr, nblk)[None], lru_b_r=dbr.reshape(1, nblk, -1),
        lru_w_i=_block_diag_extract(dwi, nblk)[None], lru_b_i=dbi.reshape(1, nblk, -1), lru_lambda=dlam,
        fox_b_f=dbf[:H].reshape(1, H), fox_q_gain=dqg.reshape(1, -1), fox_k_gain=dkg.reshape(1, -1))
    return loss, gx, grads


def _place():
    x, y, c = lax.axis_index("x"), lax.axis_index("y"), lax.axis_index("c")
    chips = [(1 - x, y), (x, 1 - y), (1 - x, 1 - y)]
    return x, y, c, 2 * x + y, chips


BOUNCE_BYTES = 1 << 20


def _bounce_shape(rows, cols, dtype):
    chunk = rows
    while chunk % 2 == 0 and chunk > 16 and chunk * cols * jnp.dtype(dtype).itemsize > BOUNCE_BYTES:
        chunk //= 2
    return pltpu.VMEM((2, chunk, cols), dtype)


def _bounce_copy(src, dst, buf, sem):
    chunk = buf.shape[1]
    n = src.shape[0] // chunk
    cin = lambda i: pltpu.make_async_copy(src.at[pl.ds(i * chunk, chunk)], buf.at[i % 2], sem.at[i % 2])
    cout = lambda i: pltpu.make_async_copy(buf.at[i % 2], dst.at[pl.ds(i * chunk, chunk)], sem.at[2 + i % 2])
    cin(0).start()
    for i in range(n):
        cin(i).wait()
        if i + 1 < n:
            if i >= 1:
                cout(i - 1).wait()
            cin(i + 1).start()
        cout(i).start()
    if n >= 2:
        cout(n - 2).wait()
    cout(n - 1).wait()


def _hbm_call(body, name, arrays, out_shape, n_dma_sems, bounce=()):
    scratch = [pltpu.SemaphoreType.DMA((k,)) for k in n_dma_sems]
    for rows, cols, dtype in bounce:
        scratch += [_bounce_shape(rows, cols, dtype), pltpu.SemaphoreType.DMA((4,))]
    return pl.pallas_call(
        body, name=name, in_specs=[ANY] * len(arrays), out_specs=[ANY] * len(out_shape), out_shape=out_shape,
        scratch_shapes=scratch,
        compiler_params=pltpu.CompilerParams(has_side_effects=True, vmem_limit_bytes=VMEM_LIMIT),
    )(*arrays)


class _Gather:
    def __init__(self, shards):
        n = self.n = len(shards)
        self.operands = list(shards)
        self.out_shape = [jax.ShapeDtypeStruct((N_CHIPS,) + tuple(a.shape), a.dtype) for a in shards]
        self.scratch = [pltpu.SemaphoreType.DMA((3 * n,)) for _ in range(4)]
        for a in shards:
            self.scratch += [_bounce_shape(a.shape[0], a.shape[1], a.dtype), pltpu.SemaphoreType.DMA((4,))]

    def _copies(self, ins, outs, scr):
        send, recv, fsend, frecv = scr[:4]
        x, y, c, s, chips = _place()

        def rows(a, chip_idx, which):
            hr = ins[a].shape[0] // 2
            return outs[a].at[chip_idx, pl.ds(which * hr, hr)]

        def landed(a, j, core):
            return rows(a, 2 * chips[j][0] + chips[j][1], core)

        def ici(a, j, mine):
            hr = ins[a].shape[0] // 2
            src, dst = (ins[a].at[pl.ds(c * hr, hr)], rows(a, s, c)) if mine else (landed(a, j, c),) * 2
            return pltpu.make_async_remote_copy(src_ref=src, dst_ref=dst, send_sem=send.at[3 * a + j],
                                                recv_sem=recv.at[3 * a + j], device_id=(*chips[j], c),
                                                device_id_type=MESH)

        def d2d(a, j, mine):
            ref = landed(a, j, c if mine else 1 - c)
            return pltpu.make_async_remote_copy(src_ref=ref, dst_ref=ref, send_sem=fsend.at[3 * a + j],
                                                recv_sem=frecv.at[3 * a + j], device_id=(x, y, 1 - c),
                                                device_id_type=MESH)

        return ici, d2d, s

    def start(self, ins, outs, scr):
        ici, _, _ = self._copies(ins, outs, scr)
        for a in range(self.n):
            for j in range(3):
                ici(a, j, True).start()

    def middle(self, ins, outs, scr):
        ici, d2d, s = self._copies(ins, outs, scr)
        for a in range(self.n):
            _bounce_copy(ins[a], outs[a].at[s], scr[4 + 2 * a], scr[5 + 2 * a])
        for a in range(self.n):
            for j in range(3):
                ici(a, j, False).wait_recv()
                d2d(a, j, True).start()

    def finish(self, ins, outs, scr):
        ici, d2d, _ = self._copies(ins, outs, scr)
        for a in range(self.n):
            for j in range(3):
                d2d(a, j, False).wait_recv()
        for a in range(self.n):
            for j in range(3):
                ici(a, j, True).wait_send()
                d2d(a, j, True).wait_send()


def _run_plan(name, plan):
    k_in, k_out = len(plan.operands), len(plan.out_shape)

    def body(*refs):
        parts = (refs[:k_in], refs[k_in:k_in + k_out], refs[k_in + k_out:])
        plan.start(*parts)
        plan.middle(*parts)
        plan.finish(*parts)

    return pl.pallas_call(
        body, name=name, in_specs=[ANY] * k_in, out_specs=[ANY] * k_out, out_shape=plan.out_shape,
        scratch_shapes=plan.scratch,
        compiler_params=pltpu.CompilerParams(has_side_effects=True, vmem_limit_bytes=VMEM_LIMIT),
    )(*plan.operands)


def _hosted_call(body, name, grid, in_specs, out_specs, out_shape, scratch_shapes, operands, sem, plan=None):
    if plan is None:
        res = pl.pallas_call(body, name=name, grid=grid, in_specs=in_specs, out_specs=out_specs, out_shape=out_shape,
                             scratch_shapes=scratch_shapes, compiler_params=_params(sem))(*operands)
        return res, None
    n_in, n_out, n_scr = len(in_specs), len(out_specs), len(scratch_shapes)
    k_in, k_out = len(plan.operands), len(plan.out_shape)
    total = int(np.prod(grid))
    late = max(0, total - 1 - max(1, total // 8))

    def hosted(*refs):
        ins, refs = refs[:n_in], refs[n_in:]
        p_ins, refs = refs[:k_in], refs[k_in:]
        outs, refs = refs[:n_out], refs[n_out:]
        p_outs, refs = refs[:k_out], refs[k_out:]
        scr, p_scr = refs[:n_scr], refs[n_scr:]
        step = pl.program_id(0)
        for d in range(1, len(grid)):
            step = step * grid[d] + pl.program_id(d)
        pl.when(step == 0)(lambda: plan.start(p_ins, p_outs, p_scr))
        body(*ins, *outs, *scr)
        pl.when(step == late)(lambda: plan.middle(p_ins, p_outs, p_scr))
        pl.when(step == total - 1)(lambda: plan.finish(p_ins, p_outs, p_scr))

    res = pl.pallas_call(
        hosted, name=name, grid=grid, in_specs=list(in_specs) + [ANY] * k_in, out_specs=list(out_specs) + [ANY] * k_out,
        out_shape=list(out_shape) + plan.out_shape, scratch_shapes=list(scratch_shapes) + plan.scratch,
        compiler_params=pltpu.CompilerParams(dimension_semantics=sem, vmem_limit_bytes=VMEM_LIMIT,
                                             has_side_effects=True),
    )(*operands, *plan.operands)
    return res[:n_out], res[n_out:]


def _all_gather(name, shards):
    return _run_plan(name, _Gather(shards))


class _Swap:
    def __init__(self, arrs):
        self.n = len(arrs)
        self.operands = list(arrs)
        self.out_shape = [jax.ShapeDtypeStruct((a.shape[0], a.shape[1] // 2, a.shape[2]), a.dtype) for a in arrs]
        self.scratch = [pltpu.SemaphoreType.DMA((self.n,)) for _ in range(2)]

    def _copy(self, ins, outs, scr, a):
        x, y, c, _, _ = _place()
        hr = ins[a].shape[1] // 2
        return pltpu.make_async_remote_copy(
            src_ref=ins[a].at[:, pl.ds((1 - c) * hr, hr)], dst_ref=outs[a], send_sem=scr[0].at[a],
            recv_sem=scr[1].at[a], device_id=(x, y, 1 - c), device_id_type=MESH)

    def start(self, ins, outs, scr):
        for a in range(self.n):
            self._copy(ins, outs, scr, a).start()

    def middle(self, ins, outs, scr):
        pass

    def finish(self, ins, outs, scr):
        for a in range(self.n):
            self._copy(ins, outs, scr, a).wait()


class _Scatter:
    def __init__(self, parts):
        n = self.n = len(parts)
        self.operands = list(parts)
        self.out_shape = [jax.ShapeDtypeStruct(a.shape, a.dtype) for a in parts]
        self.scratch = [pltpu.SemaphoreType.DMA((3 * n,)) for _ in range(2)]
        for a in parts:
            self.scratch += [_bounce_shape(a.shape[1], a.shape[2], a.dtype), pltpu.SemaphoreType.DMA((4,))]

    def _copy(self, ins, outs, scr, a, j, mine):
        x, y, c, s, chips = _place()
        t = 2 * chips[j][0] + chips[j][1]
        return pltpu.make_async_remote_copy(
            src_ref=ins[a].at[t], dst_ref=outs[a].at[s if mine else t], send_sem=scr[0].at[3 * a + j],
            recv_sem=scr[1].at[3 * a + j], device_id=(*chips[j], c), device_id_type=MESH)

    def start(self, ins, outs, scr):
        for a in range(self.n):
            for j in range(3):
                self._copy(ins, outs, scr, a, j, True).start()

    def middle(self, ins, outs, scr):
        s = _place()[3]
        for a in range(self.n):
            _bounce_copy(ins[a].at[s], outs[a].at[s], scr[2 + 2 * a], scr[3 + 2 * a])

    def finish(self, ins, outs, scr):
        for a in range(self.n):
            for j in range(3):
                self._copy(ins, outs, scr, a, j, False).wait_recv()
        for a in range(self.n):
            for j in range(3):
                self._copy(ins, outs, scr, a, j, True).wait_send()


def _pair_gather(name, halves):
    n = len(halves)

    def body(*refs):
        ins, outs = refs[:n], refs[n:2 * n]
        send, recv = refs[2 * n:2 * n + 2]
        stage = refs[2 * n + 2:]
        x, y, c, _, _ = _place()
        cps = []
        for a in range(n):
            hr = ins[a].shape[0]
            cp = pltpu.make_async_remote_copy(
                src_ref=ins[a], dst_ref=outs[a].at[pl.ds(c * hr, hr)], send_sem=send.at[a], recv_sem=recv.at[a],
                device_id=(x, y, 1 - c), device_id_type=MESH)
            cp.start()
            cps.append((cp, hr))
        for a, (cp, hr) in enumerate(cps):
            _bounce_copy(ins[a], outs[a].at[pl.ds(c * hr, hr)], stage[2 * a], stage[2 * a + 1])
        for a, (cp, hr) in enumerate(cps):
            cp.wait_send()
            theirs = outs[a].at[pl.ds((1 - c) * hr, hr)]
            pltpu.make_async_remote_copy(src_ref=theirs, dst_ref=theirs, send_sem=send.at[a], recv_sem=recv.at[a],
                                         device_id=(x, y, 1 - c), device_id_type=MESH).wait_recv()

    out_shape = [jax.ShapeDtypeStruct((2 * a.shape[0], a.shape[1]), a.dtype) for a in halves]
    return _hbm_call(body, name, halves, out_shape, (n, n),
                     bounce=[(a.shape[0], a.shape[1], a.dtype) for a in halves])


def _row_tile(rows, cols, itemsize, n_bufs):
    budget = VMEM_LIMIT // 2
    for t in range(min(rows, 1024) // 16 * 16, 0, -16):
        if rows % t == 0 and 2 * n_bufs * t * cols * itemsize <= budget:
            return t
    return rows


def _pair_add(name, g, gsib, core, out_dtype):
    _, r, cols = g.shape
    hr = r // 2
    t = _row_tile(hr, cols, 4, 3)
    per = hr // t

    def body(core_ref, a_ref, b_ref, o_ref):
        o_ref[...] = (a_ref[...].astype(F32) + b_ref[...].astype(F32)).astype(o_ref.dtype)

    grid_spec = pltpu.PrefetchScalarGridSpec(
        num_scalar_prefetch=1, grid=(N_CHIPS, per),
        in_specs=[pl.BlockSpec((None, t, cols), lambda s, i, core: (s, core[0] * per + i, 0)),
                  pl.BlockSpec((None, t, cols), lambda s, i, core: (s, i, 0))],
        out_specs=pl.BlockSpec((None, t, cols), lambda s, i, core: (s, i, 0)))
    return pl.pallas_call(body, name=name, grid_spec=grid_spec,
                          out_shape=jax.ShapeDtypeStruct((N_CHIPS, hr, cols), out_dtype),
                          compiler_params=_params(("arbitrary", "arbitrary")))(core, g, gsib)


def _chip_sum(name, parts):
    _, hr, cols = parts.shape
    t = _row_tile(hr, cols, 4, 5)

    def body(p_ref, o_ref):
        o_ref[...] = ((p_ref[0].astype(F32) + p_ref[1].astype(F32)) + p_ref[2].astype(F32)) + p_ref[3].astype(F32)

    return pl.pallas_call(
        body, name=name, grid=(hr // t,), in_specs=[pl.BlockSpec((N_CHIPS, t, cols), lambda i: (0, i, 0))],
        out_specs=pl.BlockSpec((t, cols), lambda i: (i, 0)), out_shape=jax.ShapeDtypeStruct((hr, cols), F32),
        compiler_params=_params(("arbitrary",)))(parts)


def _pair_partials(tag, arrs, sib, wire_dtypes, core):
    return _Scatter([_pair_add(f"{tag}_pair_add{i}", g, gs, core, dt)
                     for i, (g, gs, dt) in enumerate(zip(arrs, sib, wire_dtypes))])


def _finish_reduce(tag, scattered):
    halves = [_chip_sum(f"{tag}_chip_sum{i}", p) for i, p in enumerate(scattered)]
    return _pair_gather(f"{tag}_pair_gather", halves)


def _adamw(name, w, g_parts, m, v):
    thin = w.ndim == 3
    rows, cols = w.shape[0], w.shape[-1]
    n_parts = len(g_parts)
    part_rows = rows // n_parts
    t = max(d for d in range(1, 257) if part_rows % d == 0) if thin else _row_tile(part_rows, cols, 4, 7 + n_parts)
    per = part_rows // t
    c1 = 1.0 - ADAM_B1 ** ADAM_STEP
    c2 = 1.0 - ADAM_B2 ** ADAM_STEP

    def body(w_ref, m_ref, v_ref, *refs):
        g_refs, (go_ref, d_ref, nm_ref, nv_ref) = refs[:n_parts], refs[n_parts:]
        g = g_refs[0][...]
        for k in range(1, n_parts):
            g = jnp.where(pl.program_id(0) >= k * per, g_refs[k][...], g)
        go_ref[...] = g
        m = ADAM_B1 * m_ref[...] + (1.0 - ADAM_B1) * g
        v = ADAM_B2 * v_ref[...] + (1.0 - ADAM_B2) * (g * g)
        nm_ref[...] = m
        nv_ref[...] = v
        d_ref[...] = -ADAM_LR * ((m / c1) / (jnp.sqrt(v / c2) + ADAM_EPS) + ADAM_WD * w_ref[...])

    block = (t, 1, cols) if thin else (t, cols)
    at = lambda r: (r, 0, 0) if thin else (r, 0)
    spec = pl.BlockSpec(block, lambda i: at(i))
    g_specs = [pl.BlockSpec(block, lambda i, k=k: at(jnp.clip(i - k * per, 0, per - 1))) for k in range(n_parts)]
    shp = jax.ShapeDtypeStruct(w.shape, F32)
    return pl.pallas_call(body, name=name, grid=(rows // t,), in_specs=[spec] * 3 + g_specs, out_specs=[spec] * 4,
                          out_shape=[shp] * 4, compiler_params=_params(("arbitrary",)))(w, m, v, *g_parts)


_WEIGHTS = ["mix_norm", "mlp_norm", "mlp_w1", "mlp_w2", "lru_w_in", "lru_conv_w", "lru_conv_b", "lru_w_r", "lru_b_r",
            "lru_w_i", "lru_b_i", "lru_lambda", "lru_w_out", "fox_w_in", "fox_b_f", "fox_q_gain", "fox_k_gain",
            "fox_w_out"]
_REPLICATED = ["mix_norm", "mlp_norm", "lru_conv_b", "lru_w_r", "lru_b_r", "lru_w_i", "lru_b_i", "lru_lambda",
               "fox_b_f", "fox_q_gain", "fox_k_gain"]
_PACK_TILE = 2 * SUBLANES * LANES


def _as2d(a):
    return a.reshape(-1, a.shape[-1])


def kernel(x, mix_norm, mlp_norm, mlp_w1, mlp_w2, lru_w_in, lru_conv_w, lru_conv_b, lru_w_r, lru_b_r, lru_w_i, lru_b_i, lru_lambda, lru_w_out, fox_w_in, fox_b_f, fox_q_gain, fox_k_gain, fox_w_out, loss_target, m_mix_norm, m_mlp_norm, m_mlp_w1, m_mlp_w2, m_lru_w_in, m_lru_conv_w, m_lru_conv_b, m_lru_w_r, m_lru_b_r, m_lru_w_i, m_lru_b_i, m_lru_lambda, m_lru_w_out, m_fox_w_in, m_fox_b_f, m_fox_q_gain, m_fox_k_gain, m_fox_w_out, v_mix_norm, v_mlp_norm, v_mlp_w1, v_mlp_w2, v_lru_w_in, v_lru_conv_w, v_lru_conv_b, v_lru_w_r, v_lru_b_r, v_lru_w_i, v_lru_b_i, v_lru_lambda, v_lru_w_out, v_fox_w_in, v_fox_b_f, v_fox_q_gain, v_fox_k_gain, v_fox_w_out):
    args = dict(locals())
    W = {n: args[n] for n in _WEIGHTS}
    Mo = {n: args["m_" + n] for n in _WEIGHTS}
    Vo = {n: args["v_" + n] for n in _WEIGHTS}
    S, D = x.shape[1], x.shape[2]
    F = 4 * D
    H = D // HEAD_DIM
    NU = 3 * D + LANES
    FQ, DQ = F // N_CHIPS, D // N_CHIPS
    nfox = fox_w_in.shape[-1]
    chip = 2 * lax.axis_index("x") + lax.axis_index("y")
    core = lax.axis_index("c").astype(jnp.int32).reshape(1)

    cw_flat = jnp.pad(lru_conv_w.reshape(-1), (0, _PACK_TILE - CONV_WIDTH * DQ)).reshape(2 * SUBLANES, LANES)
    w1s, w2s = mlp_w1.astype(BF16), mlp_w2.astype(BF16)
    wv = {}
    small = {n: W[n] for n in _REPLICATED}
    scattered = {}
    members = {"g1": ["w2_1", "w1_1", "fox_out"], "g2": ["fox_in"], "g3": ["w2_0", "w1_0"], "g4": ["lru_out", "lru_in"]}
    swap_at = {"fox_bwd_prep": "g1", "fox_dh": "g2", "lru_dout": "g3"}
    scatter_at = {"attn_backward": "g1", "mlp0_dact": "g2", "lru_bwd": "g3", "lru_dh": "g4"}
    swapped = {}

    fox_rows = -(-nfox // (4 * SUBLANES)) * (4 * SUBLANES)
    fox_t = jnp.pad(jnp.transpose(fox_w_in[0]).astype(BF16), ((0, fox_rows - nfox), (0, 0)))

    def shard_major(name, g):
        if name == "fox_in":
            return jnp.pad(g[:nfox * N_CHIPS].reshape(N_CHIPS, nfox, D), ((0, 0), (0, fox_rows - nfox), (0, 0)))
        return g

    class Comm:
        @staticmethod
        def before(name, grads):
            if name == "mix0_norm":
                return _Gather([lru_w_in[0].astype(BF16)])
            if name == "lru_in":
                return _Gather([lru_w_out[0].astype(BF16), cw_flat])
            if name == "lru_fwd":
                return _Gather([w1s[0]])
            if name == "mlp0_up":
                return _Gather([w2s[0]])
            if name == "mlp0_down":
                return _Gather([fox_t])
            if name == "attn_forward":
                return _Gather([fox_w_out[0].astype(BF16), w1s[1], w2s[1]])
            if name in swap_at:
                group = swap_at[name]
                swapped[group] = [[shard_major(n, grads[n]) for n in members[group]], None]
                return _Swap(swapped[group][0])
            if name in scatter_at:
                group = scatter_at[name]
                if group not in swapped:
                    arrs = [shard_major(n, grads[n]) for n in members[group]]
                    swapped[group] = [arrs, _run_plan(f"{group}_pair_swap", _Swap(arrs))]
                arrs, sib = swapped[group]
                return _pair_partials(group, arrs, sib, [BF16] * len(arrs), core)
            return None

        @staticmethod
        def after(name, res, wv):
            if name == "mix0_norm":
                wv.update(lru_in=_View(res[0], "cs"))
            elif name == "lru_in":
                wv.update(lru_out=_View(res[0], "rs"))
                taps = res[1].reshape(N_CHIPS, -1)[:, :CONV_WIDTH * DQ].reshape(N_CHIPS, CONV_WIDTH, DQ)
                small["conv_w"] = jnp.transpose(taps, (1, 0, 2)).reshape(CONV_WIDTH, D)
            elif name == "lru_fwd":
                wv.update(w1_0=_View(res[0], "cs"))
            elif name == "mlp0_up":
                wv.update(w2_0=_View(res[0], "rs"))
            elif name == "mlp0_down":
                fox_full = jnp.concatenate([res[0][s, :nfox] for s in range(N_CHIPS)], axis=0)
                wv.update(fox_in=_View(jnp.pad(fox_full, ((0, NU - fox_full.shape[0]), (0, 0)))))
            elif name == "attn_forward":
                wv.update(fox_out=_View(res[0], "rs"), w1_1=_View(res[1], "cs"), w2_1=_View(res[2], "rs"))
            elif name in swap_at:
                swapped[swap_at[name]][1] = res
            else:
                scattered.update(zip(members[scatter_at[name]], res))

    def grad_view(grads, name):
        if name in ("w1_0", "w1_1"):
            return _View(None, "cs", shape=(N_CHIPS, D, FQ), dtype=BF16)
        if name in ("w2_0", "w2_1"):
            return _View(None, "rs", shape=(N_CHIPS, FQ, D), dtype=BF16)
        if name == "lru_in":
            return _View(None, "cs", shape=(N_CHIPS, D, 2 * D // N_CHIPS), dtype=BF16)
        if name in ("lru_out", "fox_out"):
            return _View(None, "rs", shape=(N_CHIPS, DQ, D), dtype=BF16)
        return _View(None, shape=(NU, D), dtype=BF16)

    loss, gx, grads = _local_step(x[0], loss_target[0], small, wv, grad_view, Comm)

    pack_names = _REPLICATED + ["conv_w"]
    flat = jnp.concatenate([grads[n].reshape(-1).astype(F32) for n in pack_names] + [loss.reshape(-1)])
    per_chip = -(-flat.shape[0] // (N_CHIPS * _PACK_TILE)) * _PACK_TILE
    pack = jnp.pad(flat, (0, N_CHIPS * per_chip - flat.shape[0])).reshape(N_CHIPS, per_chip // LANES, LANES)
    pack_sib = _run_plan("pack_pair_swap", _Swap([pack]))
    (scattered["pack"],) = _run_plan("pack_chip_scatter", _pair_partials("pack", [pack], pack_sib, [F32], core))
    order = ["w1_0", "w1_1", "w2_0", "w2_1", "lru_in", "lru_out", "fox_in", "fox_out", "pack"]
    red = dict(zip(order, _finish_reduce("grads", [scattered[n] for n in order])))
    (all_pack,) = _all_gather("gather_small_grads", [red["pack"]])
    all_flat = all_pack.reshape(-1)
    G = {}
    off = 0
    for n in pack_names:
        shape = grads[n].shape if n == "conv_w" else W[n].shape
        size = int(np.prod(shape))
        G[n] = all_flat[off:off + size].reshape(shape)
        off += size
    total = all_flat[off]
    G["lru_conv_w"] = lax.dynamic_slice_in_dim(G.pop("conv_w"), chip * DQ, DQ, axis=1)[None]
    parts = {n: [_as2d(G[n])] for n in G}
    parts.update(mlp_w1=[red["w1_0"], red["w1_1"]], mlp_w2=[red["w2_0"], red["w2_1"]], lru_w_in=[red["lru_in"]],
                 lru_w_out=[red["lru_out"]], fox_w_in=[red["fox_in"][:nfox, None, :]], fox_w_out=[red["fox_out"]])

    delta, new_m, new_v = {}, {}, {}
    for n in _WEIGHTS:
        if n == "fox_w_in":
            to_thin = lambda a: jnp.transpose(a, (2, 0, 1))
            res = _adamw(f"adamw_{n}", to_thin(W[n]), parts[n], to_thin(Mo[n]), to_thin(Vo[n]))
            G[n], delta[n], new_m[n], new_v[n] = (jnp.transpose(t, (1, 2, 0)) for t in res)
            continue
        go, d, nm, nv = _adamw(f"adamw_{n}", _as2d(W[n]), parts[n], _as2d(Mo[n]), _as2d(Vo[n]))
        G[n], delta[n], new_m[n], new_v[n] = (t.reshape(W[n].shape) for t in (go, d, nm, nv))

    return (total, gx[None], *[G[n] for n in _WEIGHTS], *[delta[n] for n in _WEIGHTS],
            *[new_m[n] for n in _WEIGHTS], *[new_v[n] for n in _WEIGHTS])
```

```python
import functools

import numpy as np
import jax
import jax.numpy as jnp
from jax import lax
from jax.experimental import pallas as pl
from jax.experimental.pallas import tpu as pltpu

F32 = jnp.float32
BF16 = jnp.bfloat16

HEAD_DIM = 64
LRU_BLOCK_DIM = 64
CONV_WIDTH = 4
LRU_C = 8.0
EPS = 1e-6
NEG_INF = -1e30
ADAM_LR = 0.001
ADAM_B1 = 0.9
ADAM_B2 = 0.999
ADAM_EPS = 1e-08
ADAM_WD = 0.01
ADAM_STEP = 10

N_CHIPS = 4
LANES = 128
SUBLANES = 8
MXU_DIM = 256
VMEM_LIMIT = 52 * 1024 * 1024
MESH = pl.DeviceIdType.MESH
ANY = pl.BlockSpec(memory_space=pl.ANY)


def _pick(n, prefs):
    for p in prefs:
        if p <= n and n % p == 0:
            return p
    return n


def _params(sem=None):
    return pltpu.CompilerParams(dimension_semantics=sem, vmem_limit_bytes=VMEM_LIMIT)


class _View:
    def __init__(self, arr, kind="plain", shape=None, dtype=None):
        self.arr = arr
        self.kind = kind
        self.shape = tuple(arr.shape) if arr is not None else tuple(shape)
        self.dtype = arr.dtype if arr is not None else dtype

    def limits(self):
        if self.kind == "plain":
            return 0, 0
        return self.shape[-2], (self.shape[-1] if self.kind == "cs" else 0)

    def spec(self, br, bc, fr, fc):
        if self.kind == "plain":
            return pl.BlockSpec((br, bc), lambda *g: (fr(*g), fc(*g)))
        rows, ncol = self.shape[-2:]
        assert rows % br == 0 and ncol % bc == 0, (self.shape, br, bc)
        if self.kind == "cs":
            per = ncol // bc
            return pl.BlockSpec((None, br, bc), lambda *g: (fc(*g) // per, fr(*g), fc(*g) % per))
        per = rows // br
        return pl.BlockSpec((None, br, bc), lambda *g: (fr(*g) // per, fr(*g) % per, fc(*g)))


def _bf(x):
    return x if x.dtype == BF16 else x.astype(BF16)


def _matmul(name, A, B, M, N, K, *, ta=False, tb=False, outs, epilogue, extras=(), vecs=(), n_sums=0,
            tm=None, tn=None, tk=None, plan=None):
    lim = {"m": [M], "n": [N], "k": [K]}
    for view, (rdim, cdim) in ([(A, "km" if ta else "mk"), (B, "nk" if tb else "kn")]
                               + [(e, "mn") for e in extras] + [(o, "mn") for o in outs]):
        r_lim, c_lim = view.limits()
        lim[rdim].append(r_lim)
        lim[cdim].append(c_lim)
    tm = tm or _pick(int(np.gcd.reduce(lim["m"])), (1024, 640, 512, 256, 128))
    tn = tn or _pick(int(np.gcd.reduce(lim["n"])), (1024, 640, 512, 256, 128))
    tk = tk or _pick(int(np.gcd.reduce(lim["k"])), (1024, 640, 512, 256, 128))
    nk = K // tk
    gi = lambda i, j, k: i
    gj = lambda i, j, k: j
    gk = lambda i, j, k: k
    a_spec = A.spec(tk, tm, gk, gi) if ta else A.spec(tm, tk, gi, gk)
    b_spec = B.spec(tn, tk, gj, gk) if tb else B.spec(tk, tn, gk, gj)
    ca = 0 if ta else 1
    cb = 1 if tb else 0
    ne, no = len(extras) + len(vecs), len(outs)
    assert n_sums == 0 or tn == N
    row_spec = pl.BlockSpec((1, tn), lambda i, j, k: (0, j))
    in_specs = [a_spec, b_spec] + [e.spec(tm, tn, gi, gj) for e in extras] + [row_spec] * len(vecs)
    operands = [A.arr, B.arr] + [e.arr for e in extras] + list(vecs)
    out_specs = [o.spec(tm, tn, gi, gj) for o in outs] + [row_spec] * n_sums
    out_shape = ([jax.ShapeDtypeStruct(o.shape, o.dtype) for o in outs]
                 + [jax.ShapeDtypeStruct((1, N), F32)] * n_sums)

    def body(*refs):
        a_ref, b_ref = refs[0], refs[1]
        ex = refs[2:2 + ne]
        o_refs = refs[2 + ne:2 + ne + no]
        s_refs = refs[2 + ne + no:2 + ne + no + n_sums]
        first_row_tile = pl.program_id(0) == 0

        def prod():
            return lax.dot_general(_bf(a_ref[...]), _bf(b_ref[...]), (((ca,), (cb,)), ((), ())),
                                   preferred_element_type=F32)

        def finish(acc):
            res = epilogue(acc, *[e[...] for e in ex])
            for o_ref, r in zip(o_refs, res[:no]):
                o_ref[...] = r.astype(o_ref.dtype)
            for s_ref, r in zip(s_refs, res[no:]):
                def assign(s_ref=s_ref, r=r):
                    s_ref[...] = r

                def accumulate(s_ref=s_ref, r=r):
                    s_ref[...] += r

                pl.when(first_row_tile)(assign)
                pl.when(jnp.logical_not(first_row_tile))(accumulate)

        if nk == 1:
            finish(prod())
        else:
            acc_ref = refs[-1]
            k = pl.program_id(2)

            @pl.when(k == 0)
            def _():
                acc_ref[...] = jnp.zeros_like(acc_ref)

            acc_ref[...] += prod()

            @pl.when(k == nk - 1)
            def _():
                finish(acc_ref[...])

    res, side = _hosted_call(body, name, (M // tm, N // tn, nk), in_specs, out_specs, out_shape,
                             [pltpu.VMEM((tm, tn), F32)] if nk > 1 else [], operands,
                             ("arbitrary", "arbitrary", "arbitrary"), plan)
    return res if plan is None else (res, side)


def _ep_store(acc):
    return (acc,)


def _ep_resid(acc, res):
    return (res + acc,)


def _ep_resid_norm(acc, res, g):
    xo = res + acc
    r = lax.rsqrt(jnp.mean(xo * xo, axis=-1, keepdims=True) + EPS)
    return (xo, (xo * r) * g)


def _ep_norm_bwd(acc, x, dres, g):
    r = lax.rsqrt(jnp.mean(x * x, axis=-1, keepdims=True) + EPS)
    xhat = x * r
    dxn = acc * g
    tot = dres + r * (dxn - xhat * jnp.mean(dxn * xhat, axis=-1, keepdims=True))
    return (tot, tot, jnp.sum(acc * xhat, axis=0, keepdims=True))


def _ep_relu2(acc):
    zp = jnp.maximum(acc, 0.0)
    return (zp * zp,)


def _ep_drelu2(acc, act):
    return (acc * (2.0 * jnp.sqrt(act.astype(F32))),)


def _fresh(M, N, dtype):
    return _View(None, shape=(M, N), dtype=dtype)


def _rms_fwd(name, x, g, S, D, plan=None):
    T = _pick(S, (512, 256, 128))

    def body(x_ref, g_ref, h_ref):
        x = x_ref[...]
        r = lax.rsqrt(jnp.mean(x * x, axis=-1, keepdims=True) + EPS)
        h_ref[...] = ((x * r) * g_ref[...]).astype(BF16)

    return _hosted_call(body, name, (S // T,),
                        [pl.BlockSpec((T, D), lambda i: (i, 0)), pl.BlockSpec((1, D), lambda i: (0, 0))],
                        [pl.BlockSpec((T, D), lambda i: (i, 0))], [jax.ShapeDtypeStruct((S, D), BF16)], [], (x, g),
                        ("arbitrary",), plan)


def _loss_head(x, tgt, S, D):
    T = _pick(S, (512, 256, 128))

    def body(x_ref, t_ref, loss_ref, d_ref, db_ref):
        @pl.when(pl.program_id(0) == 0)
        def _():
            loss_ref[...] = jnp.zeros_like(loss_ref)

        e = x_ref[...] - t_ref[...]
        loss_ref[...] += 0.5 * jnp.sum(jnp.mean(e * e, axis=-1, keepdims=True), axis=0, keepdims=True)
        d = e * (1.0 / D)
        d_ref[...] = d
        db_ref[...] = d.astype(BF16)

    row = pl.BlockSpec((T, D), lambda i: (i, 0))
    return pl.pallas_call(
        body, name="loss_head", grid=(S // T,), in_specs=[row, row],
        out_specs=[pl.BlockSpec((1, 1), lambda i: (0, 0)), row, row],
        out_shape=[jax.ShapeDtypeStruct((1, 1), F32), jax.ShapeDtypeStruct((S, D), F32),
                   jax.ShapeDtypeStruct((S, D), BF16)],
        compiler_params=_params(("arbitrary",)),
    )(x, tgt)


def _sigmoid(z):
    return 1.0 / (1.0 + jnp.exp(-z))


def _log_sigmoid(z):
    return jnp.minimum(z, 0.0) - jnp.log(1.0 + jnp.exp(-jnp.abs(z)))


_GELU_K = 0.7978845608028654
_GELU_C = 0.044715


def _gelu(x):
    t = jnp.tanh(_GELU_K * (x + _GELU_C * (x * x * x)))
    return 0.5 * x * (1.0 + t)


def _gelu_and_grad(x):
    x2 = x * x
    t = jnp.tanh(_GELU_K * (x + _GELU_C * (x2 * x)))
    g = 0.5 * x * (1.0 + t)
    dg = 0.5 * (1.0 + t) + 0.5 * x * (1.0 - t * t) * (_GELU_K * (1.0 + 3.0 * _GELU_C * x2))
    return g, dg


def _decay_terms(r, ls):
    la = LRU_C * r * ls
    a = jnp.exp(la)
    a2 = a * a
    mult = jnp.sqrt(-jnp.tanh(la) * (a2 + 1.0))
    return a, a2, mult


def _lru_fwd(u0, conv_w, conv_b, wr_bd, b_r, wi_bd, b_i, lam, S, D, plan=None):
    T = _pick(S, (256, 128))
    GT = wr_bd.shape[-1]
    nG = D // GT

    def body(gb_ref, xb_ref, cw_ref, cb_ref, wr_ref, br_ref, wi_ref, bi_ref, lam_ref,
             y_ref, xc_ref, r_ref, i_ref, hs_ref, ext, a_scr, hcar):
        @pl.when(pl.program_id(0) == 0)
        def _():
            ext[0:SUBLANES, :] = jnp.zeros((SUBLANES, D), F32)
            hcar[...] = jnp.zeros_like(hcar)

        xb = xb_ref[...]
        ext[SUBLANES:SUBLANES + T, :] = xb
        xc = cb_ref[...]
        for k in range(CONV_WIDTH):
            xc = xc + ext[pl.ds(SUBLANES - (CONV_WIDTH - 1) + k, T), :] * cw_ref[k:k + 1, :]
        ext[0:SUBLANES, :] = xb[T - SUBLANES:T, :]
        xc_ref[...] = xc
        xcb = xc.astype(BF16)
        for g in range(nG):
            sl = slice(g * GT, (g + 1) * GT)
            zr = jnp.dot(xcb[:, sl], wr_ref[g], preferred_element_type=F32) + br_ref[:, sl]
            zi = jnp.dot(xcb[:, sl], wi_ref[g], preferred_element_type=F32) + bi_ref[:, sl]
            r_ref[:, sl] = _sigmoid(zr)
            i_ref[:, sl] = _sigmoid(zi)
        r = r_ref[...]
        a, _, mult = _decay_terms(r, _log_sigmoid(lam_ref[...]))
        a_scr[...] = a
        hs_ref[...] = mult * (i_ref[...] * xc)

        def step(t, h):
            h = a_scr[pl.ds(t, 1), :] * h + hs_ref[pl.ds(t, 1), :]
            hs_ref[pl.ds(t, 1), :] = h
            return h

        hcar[...] = lax.fori_loop(0, T, step, hcar[...], unroll=8)
        y_ref[...] = (_gelu(gb_ref[...]) * hs_ref[...]).astype(BF16)

    row = pl.BlockSpec((T, D), lambda i: (i, 0))
    vec = pl.BlockSpec((1, D), lambda i: (0, 0))
    bd = pl.BlockSpec((nG, GT, GT), lambda i: (0, 0, 0))
    f32o = jax.ShapeDtypeStruct((S, D), F32)
    return _hosted_call(
        body, "lru_fwd", (S // T,),
        [row, pl.BlockSpec((T, D), lambda i: (i, 1)), pl.BlockSpec((CONV_WIDTH, D), lambda i: (0, 0)), vec,
         bd, vec, bd, vec, vec],
        [row, row, row, row, row], [jax.ShapeDtypeStruct((S, D), BF16), f32o, f32o, f32o, f32o],
        [pltpu.VMEM((T + SUBLANES, D), F32), pltpu.VMEM((T, D), F32), pltpu.VMEM((1, D), F32)],
        (u0, u0, conv_w, conv_b, wr_bd, b_r, wi_bd, b_i, lam), ("arbitrary",), plan)


def _lru_bwd(dy, u0, xc, r, ig, hs, conv_w, wr_bd, wi_bd, lam, S, D, plan=None):
    T = _pick(S, (128,))
    nT = S // T
    GT = wr_bd.shape[-1]
    nG = D // GT
    W = CONV_WIDTH

    def body(dy_ref, gb_ref, xb_ref, xbp_ref, xc_ref, r_ref, i_ref, hs_ref, hsp_ref, cw_ref, wr_ref, wi_ref, lam_ref,
             du_ref, dcw_ref, dcb_ref, dlam_ref, dbr_ref, dbi_ref, dwr_ref, dwi_ref,
             a_scr, dh_scr, exth, extx, extd, dxc_scr, dz_scr, carry):
        step = pl.program_id(0)
        first_tile = step == nT - 1

        @pl.when(step == 0)
        def _():
            for ref in (dcw_ref, dcb_ref, dlam_ref, dbr_ref, dbi_ref, dwr_ref, dwi_ref, carry):
                ref[...] = jnp.zeros_like(ref)
            extd[T:T + SUBLANES, :] = jnp.zeros((SUBLANES, D), F32)

        hs = hs_ref[...]
        dy = dy_ref[...]
        g, dgelu = _gelu_and_grad(gb_ref[...])
        du_ref[:, 0:D] = (dy * hs * dgelu).astype(BF16)
        r = r_ref[...]
        lam = lam_ref[...]
        ls = _log_sigmoid(lam)
        a, a2, mult = _decay_terms(r, ls)
        a_scr[...] = a
        dh_scr[...] = dy * g

        def rstep(j, c):
            t = T - 1 - j
            d = dh_scr[pl.ds(t, 1), :] + c
            dh_scr[pl.ds(t, 1), :] = d
            return a_scr[pl.ds(t, 1), :] * d

        carry[...] = lax.fori_loop(0, T, rstep, carry[...], unroll=8)
        dh = dh_scr[...]
        keep = jnp.where(first_tile, 0.0, 1.0)
        exth[0:SUBLANES, :] = hsp_ref[...] * keep
        exth[SUBLANES:SUBLANES + T, :] = hs
        hprev = exth[pl.ds(SUBLANES - 1, T), :]
        xc = xc_ref[...]
        ig = i_ref[...]
        da = dh * hprev
        dmult = dh * (ig * xc)
        dla = da * a - dmult * (a2 / mult)
        dlam_ref[...] += jnp.sum(dla * r, axis=0, keepdims=True) * (LRU_C * _sigmoid(-lam))
        dzr = (dla * (LRU_C * ls)) * (r * (1.0 - r))
        dzi = (dh * (mult * xc)) * (ig * (1.0 - ig))
        dbr_ref[...] += jnp.sum(dzr, axis=0, keepdims=True)
        dbi_ref[...] += jnp.sum(dzi, axis=0, keepdims=True)
        dxc_scr[...] = dh * (mult * ig)
        xcb = xc.astype(BF16)
        dz_scr[0] = dzr.astype(BF16)
        dz_scr[1] = dzi.astype(BF16)
        nt_dims = (((1,), (1,)), ((), ()))
        tn_dims = (((0,), (0,)), ((), ()))
        for gq in range(nG):
            sl = slice(gq * GT, (gq + 1) * GT)
            zr_g = dz_scr[0, :, sl]
            zi_g = dz_scr[1, :, sl]
            dxc_scr[:, sl] += (lax.dot_general(zr_g, wr_ref[gq], nt_dims, preferred_element_type=F32)
                               + lax.dot_general(zi_g, wi_ref[gq], nt_dims, preferred_element_type=F32))
            dwr_ref[gq] += lax.dot_general(xcb[:, sl], zr_g, tn_dims, preferred_element_type=F32)
            dwi_ref[gq] += lax.dot_general(xcb[:, sl], zi_g, tn_dims, preferred_element_type=F32)
        dxc = dxc_scr[...]
        dcb_ref[...] += jnp.sum(dxc, axis=0, keepdims=True)
        extx[0:SUBLANES, :] = xbp_ref[...] * keep
        extx[SUBLANES:SUBLANES + T, :] = xb_ref[...]
        extd[0:T, :] = dxc
        dxb = jnp.zeros((T, D), F32)
        for k in range(W):
            dxb = dxb + extd[pl.ds(W - 1 - k, T), :] * cw_ref[k:k + 1, :]
            dcw_ref[k:k + 1, :] += jnp.sum(dxc * extx[pl.ds(SUBLANES - (W - 1) + k, T), :], axis=0, keepdims=True)
        extd[T:T + SUBLANES, :] = dxc[0:SUBLANES, :]
        du_ref[:, D:2 * D] = dxb.astype(BF16)

    rev = lambda i: nT - 1 - i
    tpb = T // SUBLANES
    prev8 = lambda i: jnp.maximum(rev(i) * tpb - 1, 0)
    row = pl.BlockSpec((T, D), lambda i: (rev(i), 0))
    vec = pl.BlockSpec((1, D), lambda i: (0, 0))
    bd = pl.BlockSpec((nG, GT, GT), lambda i: (0, 0, 0))
    vec_o = jax.ShapeDtypeStruct((1, D), F32)
    bd_o = jax.ShapeDtypeStruct((nG, GT, GT), F32)
    return _hosted_call(
        body, "lru_bwd", (nT,),
        [row, row, pl.BlockSpec((T, D), lambda i: (rev(i), 1)), pl.BlockSpec((SUBLANES, D), lambda i: (prev8(i), 1)),
         row, row, row, row, pl.BlockSpec((SUBLANES, D), lambda i: (prev8(i), 0)),
         pl.BlockSpec((W, D), lambda i: (0, 0)), bd, bd, vec],
        [pl.BlockSpec((T, 2 * D), lambda i: (rev(i), 0)), pl.BlockSpec((W, D), lambda i: (0, 0)),
         vec, vec, vec, vec, bd, bd],
        [jax.ShapeDtypeStruct((S, 2 * D), BF16), jax.ShapeDtypeStruct((W, D), F32), vec_o, vec_o, vec_o, vec_o, bd_o, bd_o],
        [pltpu.VMEM((T, D), F32), pltpu.VMEM((T, D), F32), pltpu.VMEM((T + SUBLANES, D), F32),
         pltpu.VMEM((T + SUBLANES, D), F32), pltpu.VMEM((T + SUBLANES, D), F32),
         pltpu.VMEM((T, D), F32), pltpu.VMEM((2, T, D), BF16), pltpu.VMEM((1, D), F32)],
        (dy, u0, u0, u0, xc, r, ig, hs, hs, conv_w, wr_bd, wi_bd, lam), ("arbitrary",), plan)


AUG_ROWS = 16
HEAD_ROWS = 128
LSE_ROW = HEAD_DIM + 6
ONES_ROW_Q = HEAD_DIM + 3
ONES_COL_K = HEAD_DIM
ONES_ROW_V = HEAD_DIM
PREP_LANES = 512
HEAD_UNROLL = 4


def _split3(x):
    b1 = x.astype(BF16).astype(F32)
    r = x - b1
    b2 = r.astype(BF16).astype(F32)
    return b1, b2, r - b2


def _head_block(x, aug, T):
    row = lax.broadcasted_iota(jnp.int32, (AUG_ROWS, T), 0)
    blk = jnp.zeros((AUG_ROWS, T), F32)
    for i, e in enumerate(aug):
        blk = jnp.where(row == i, e, blk)
    return jnp.concatenate([x, blk, jnp.zeros((HEAD_ROWS - HEAD_DIM - AUG_ROWS, T), F32)], axis=0)


def _tri_matrix(lower):
    i = np.arange(LANES)
    m = (i[:, None] >= i[None, :]) if lower else (i[:, None] <= i[None, :])
    return jnp.asarray(m.astype(np.float32), BF16)


def _lane_cumsum(x, tri_ref, carry, reverse):
    n = x.shape[1] // LANES
    tri = tri_ref[...]
    out = [None] * n
    for j in (range(n - 1, -1, -1) if reverse else range(n)):
        cs = carry
        for part in _split3(x[:, j * LANES:(j + 1) * LANES]):
            cs = cs + jnp.dot(part.astype(BF16), tri, preferred_element_type=F32)
        out[j] = cs
        carry = cs[:, 0:1] if reverse else cs[:, LANES - 1:LANES]
    return jnp.concatenate(out, axis=1), carry


def _head_rows(h):
    return pl.ds(pl.multiple_of(h * HEAD_DIM, HEAD_DIM), HEAD_DIM)


def _fox_prep(ut, b_f, qg, kg, S, D, tq):
    H = D // HEAD_DIM
    T = min(tq, PREP_LANES)
    per = tq // T
    scale = HEAD_DIM ** -0.5

    def body(q_ref, k_ref, v_ref, f_ref, bf_ref, qg_ref, kg_ref, tri_ref,
             qat_ref, kat_ref, vat_ref, ka_ref, c_scr, ccar):
        @pl.when(pl.program_id(0) == 0)
        def _():
            ccar[...] = jnp.zeros_like(ccar)

        c, carry = _lane_cumsum(_log_sigmoid(f_ref[...] + bf_ref[...]), tri_ref, ccar[...], False)
        c_scr[...] = c
        ccar[...] = carry

        def head(h, _):
            rows = _head_rows(h)
            c1, c2, c3 = _split3(c_scr[pl.ds(h, 1), :])

            def normed(src, gain, mul):
                x = src[rows, :]
                rs = lax.rsqrt(jnp.mean(x * x, axis=0, keepdims=True) + EPS)
                return ((x * rs) * gain[rows, :]) * mul

            qat_ref[h] = _head_block(normed(q_ref, qg_ref, scale), [c1, c2, c3, 1.0, 1.0, 1.0], T).astype(BF16)
            kb = _head_block(normed(k_ref, kg_ref, 1.0), [1.0, 1.0, 1.0, -c1, -c2, -c3, 1.0, 1.0, 1.0], T)
            kat_ref[h] = kb.astype(BF16)
            ka_ref[h] = kb.T.astype(BF16)
            vat_ref[h] = _head_block(v_ref[rows, :], [1.0, 1.0, 1.0], T).astype(BF16)
            return 0

        lax.fori_loop(0, H, head, 0, unroll=HEAD_UNROLL)

    part = lambda j: pl.BlockSpec((D, T), lambda i: (j, i))
    colv = lambda n: pl.BlockSpec((n, 1), lambda i: (0, 0))
    tmaj = lambda r: pl.BlockSpec((H, None, r, T), lambda i: (0, i // per, 0, i % per))
    norm = pl.BlockSpec((H, T, HEAD_ROWS), lambda i: (0, i, 0))
    tshape = lambda r: jax.ShapeDtypeStruct((H, S // tq, r, tq), BF16)
    nshape = jax.ShapeDtypeStruct((H, S, HEAD_ROWS), BF16)
    return pl.pallas_call(
        body, name="fox_prep", grid=(S // T,),
        in_specs=[part(0), part(1), part(2), pl.BlockSpec((LANES, T), lambda i: (3 * D // LANES, i)),
                  colv(LANES), colv(D), colv(D), pl.BlockSpec((LANES, LANES), lambda i: (0, 0))],
        out_specs=[tmaj(HEAD_ROWS), tmaj(HEAD_ROWS), tmaj(HEAD_ROWS), norm],
        out_shape=[tshape(HEAD_ROWS), tshape(HEAD_ROWS), tshape(HEAD_ROWS), nshape],
        scratch_shapes=[pltpu.VMEM((LANES, T), F32), pltpu.VMEM((LANES, 1), F32)],
        compiler_params=_params(("arbitrary",)),
    )(ut, ut, ut, ut, b_f, qg, kg, _tri_matrix(False))


def _fox_bwd_prep(dot, ot, lse, qat, S, D, tq, plan=None):
    H = D // HEAD_DIM
    T = min(tq, PREP_LANES)
    per = tq // T

    def body(do_ref, o_ref, lse_ref, qat_ref, doat_ref, doa_ref, qat1_ref, qa1_ref):
        row = lax.broadcasted_iota(jnp.int32, (HEAD_ROWS, T), 0)

        def head(h, _):
            rows = _head_rows(h)
            do = do_ref[rows, :].astype(F32)
            delta = jnp.sum(do * o_ref[rows, :], axis=0, keepdims=True)
            db = _head_block(do, list(_split3(-delta)), T)
            doat_ref[h] = db.astype(BF16)
            doa_ref[h] = db.T.astype(BF16)
            qb = qat_ref[h].astype(F32)
            for i, e in enumerate(_split3(-lse_ref[h])):
                qb = jnp.where(row == LSE_ROW + i, e, qb)
            qat1_ref[h] = qb.astype(BF16)
            qa1_ref[h] = qb.T.astype(BF16)
            return 0

        lax.fori_loop(0, H, head, 0, unroll=HEAD_UNROLL)

    chan = pl.BlockSpec((D, T), lambda i: (0, i))
    tmaj = pl.BlockSpec((H, None, HEAD_ROWS, T), lambda i: (0, i // per, 0, i % per))
    norm = pl.BlockSpec((H, T, HEAD_ROWS), lambda i: (0, i, 0))
    tshape = jax.ShapeDtypeStruct((H, S // tq, HEAD_ROWS, tq), BF16)
    nshape = jax.ShapeDtypeStruct((H, S, HEAD_ROWS), BF16)
    return _hosted_call(body, "fox_bwd_prep", (S // T,), [chan, chan, pl.BlockSpec((H, 1, T), lambda i: (0, 0, i)), tmaj],
                        [tmaj, norm, tmaj, norm], [tshape, nshape, tshape, nshape], [], (dot, ot, lse, qat),
                        ("arbitrary",), plan)


def _causal(s, k_axis):
    t = min(s.shape)
    ki = lax.broadcasted_iota(jnp.int32, s.shape, k_axis) - (s.shape[k_axis] - t)
    qi = lax.broadcasted_iota(jnp.int32, s.shape, 1 - k_axis)
    return jnp.where(ki <= qi, s, NEG_INF)


def _seq_tile(i, t):
    return pl.ds(pl.multiple_of(i * t, t), t)


def _attn_forward(ka, qat, vat, S, D, tq, plan=None):
    H = D // HEAD_DIM
    nq = S // tq
    G = 2

    def body(ka_ref, qat_ref, vat_ref, o_ref, o32_ref, lse_ref, m_scr, acc_scr):
        qi = pl.program_id(1)
        m_scr[...] = jnp.full_like(m_scr, NEG_INF)
        acc_scr[...] = jnp.zeros_like(acc_scr)

        def span(k0, n, diagonal):
            keys = pl.ds(pl.multiple_of(k0 * tq, tq), n * tq)
            s = [jnp.dot(ka_ref[g, keys, :], qat_ref[g], preferred_element_type=F32) for g in range(G)]
            if diagonal:
                s = [_causal(sg, 0) for sg in s]
            m_prev = [m_scr[g] for g in range(G)]
            m_new = [jnp.maximum(m_prev[g], jnp.max(s[g], axis=0, keepdims=True)) for g in range(G)]
            p = [jnp.exp(s[g] - m_new[g]).astype(BF16) for g in range(G)]
            for g in range(G):
                upd = jnp.dot(vat_ref[g, k0], p[g][0:tq], preferred_element_type=F32)
                for i in range(1, n):
                    upd = upd + jnp.dot(vat_ref[g, k0 + i], p[g][i * tq:(i + 1) * tq], preferred_element_type=F32)
                acc_scr[g] = jnp.exp(m_prev[g] - m_new[g]) * acc_scr[g] + upd
                m_scr[g] = m_new[g]

        def off_diagonal_pair(j, _):
            span(2 * j, 2, False)
            return 0

        lax.fori_loop(0, qi // 2, off_diagonal_pair, 0)
        pl.when(qi % 2 == 1)(lambda: span(qi - 1, 2, True))
        pl.when(qi % 2 == 0)(lambda: span(qi, 1, True))
        for g in range(G):
            l = acc_scr[g, ONES_ROW_V:ONES_ROW_V + 1, :]
            o = acc_scr[g, 0:HEAD_DIM, :] / l
            o_ref[g * HEAD_DIM:(g + 1) * HEAD_DIM, :] = o.astype(BF16)
            o32_ref[g * HEAD_DIM:(g + 1) * HEAD_DIM, :] = o
            lse_ref[g] = m_scr[g] + jnp.log(l)

    chan = pl.BlockSpec((G * HEAD_DIM, tq), lambda h, i: (h, i))
    stat = pl.BlockSpec((G, 1, tq), lambda h, i: (h, 0, i))
    return _hosted_call(
        body, "attn_forward", (H // G, nq),
        [pl.BlockSpec((G, S, HEAD_ROWS), lambda h, i: (h, 0, 0)),
         pl.BlockSpec((G, None, HEAD_ROWS, tq), lambda h, i: (h, i, 0, 0)),
         pl.BlockSpec((G, nq, HEAD_ROWS, tq), lambda h, i: (h, 0, 0, 0))],
        [chan, chan, stat],
        [jax.ShapeDtypeStruct((D, S), BF16), jax.ShapeDtypeStruct((D, S), F32), jax.ShapeDtypeStruct((H, 1, S), F32)],
        [pltpu.VMEM((G, 1, tq), F32), pltpu.VMEM((G, HEAD_ROWS, tq), F32)],
        (ka, qat, vat), ("arbitrary", "arbitrary"), plan)


def _attn_backward(qa, doa, qat, doat, ka, kat, vat, S, D, tq, plan=None):
    H = D // HEAD_DIM
    nq = S // tq

    def body(qa_ref, doa_ref, qat_ref, doat_ref, ka_ref, kat_ref, vat_ref, dq_ref, dk_ref, dv_ref, dk_scr, dv_scr):
        ki = pl.program_id(1)

        @pl.when(ki == 0)
        def _():
            dq_ref[...] = jnp.zeros_like(dq_ref)

        dk_scr[...] = jnp.zeros_like(dk_scr)
        dv_scr[...] = jnp.zeros_like(dv_scr)
        kt = kat_ref[...]
        vt = vat_ref[...]
        kn = ka_ref[...]

        def span(q0, n, diagonal):
            rows = pl.ds(pl.multiple_of(q0 * tq, tq), n * tq)
            s = jnp.dot(qa_ref[rows, :], kt, preferred_element_type=F32)
            if diagonal:
                s = _causal(s, 1)
            p = jnp.exp(s)
            ds = (p * jnp.dot(doa_ref[rows, :], vt, preferred_element_type=F32)).astype(BF16)
            p = p.astype(BF16)
            for i in range(n):
                part = slice(i * tq, (i + 1) * tq)
                dv_scr[...] += jnp.dot(doat_ref[q0 + i, 0:HEAD_DIM, :], p[part], preferred_element_type=F32)
                dk_scr[...] += jnp.dot(qat_ref[q0 + i], ds[part], preferred_element_type=F32)
            dq_ref[rows, :] += jnp.dot(ds, kn, preferred_element_type=F32)

        n_off = nq - 1 - ki
        odd = n_off % 2

        def off_diagonal_pair(j, _):
            span(ki + 1 + odd + 2 * j, 2, False)
            return 0

        pl.when(odd == 1)(lambda: span(ki, 2, True))
        pl.when(odd == 0)(lambda: span(ki, 1, True))
        lax.fori_loop(0, n_off // 2, off_diagonal_pair, 0)
        dk_ref[...] = dk_scr[...]
        dv_ref[...] = dv_scr[...].astype(BF16)

    whole = pl.BlockSpec((None, S, HEAD_ROWS), lambda h, i: (h, 0, 0))
    tiles = pl.BlockSpec((None, nq, HEAD_ROWS, tq), lambda h, i: (h, 0, 0, 0))
    one = pl.BlockSpec((None, None, HEAD_ROWS, tq), lambda h, i: (h, i, 0, 0))
    return _hosted_call(
        body, "attn_backward", (H, nq),
        [whole, whole, tiles, tiles, pl.BlockSpec((None, tq, HEAD_ROWS), lambda h, i: (h, i, 0)), one, one],
        [whole, pl.BlockSpec((None, HEAD_ROWS, tq), lambda h, i: (h, 0, i)),
         pl.BlockSpec((HEAD_DIM, tq), lambda h, i: (h, i))],
        [jax.ShapeDtypeStruct((H, S, HEAD_ROWS), F32), jax.ShapeDtypeStruct((H, HEAD_ROWS, S), F32),
         jax.ShapeDtypeStruct((D, S), BF16)],
        [pltpu.VMEM((HEAD_ROWS, tq), F32), pltpu.VMEM((HEAD_DIM, tq), F32)],
        (qa, doa, qat, doat, ka, kat, vat), ("arbitrary", "arbitrary"), plan)


def _fox_prep_bwd(ut, dq, dkt, dvt, b_f, qg, kg, S, D, tq):
    H = D // HEAD_DIM
    T = min(tq, PREP_LANES)
    nT = S // T
    NU = 3 * D + LANES
    scale = HEAD_DIM ** -0.5

    def body(q_ref, k_ref, f_ref, dq_ref, dk_ref, dv_ref, bf_ref, qg_ref, kg_ref, tri_ref,
             du_ref, dbf_ref, dqg_ref, dkg_ref, gq_acc, gk_acc, fcar, dc_scr):
        step = pl.program_id(0)

        @pl.when(step == 0)
        def _():
            for ref in (gq_acc, gk_acc, fcar, dbf_ref):
                ref[...] = jnp.zeros_like(ref)

        dc_scr[...] = jnp.zeros_like(dc_scr)

        def head(h, _):
            rows = _head_rows(h)
            dqb = dq_ref[h].T
            dkb = dk_ref[h]
            dc_scr[pl.ds(h, 1), :] = dqb[ONES_COL_K:ONES_COL_K + 1, :] - dkb[ONES_ROW_Q:ONES_ROW_Q + 1, :]
            for src, dsrc, gain, acc, mul, base in ((q_ref, dqb, qg_ref, gq_acc, scale, 0),
                                                    (k_ref, dkb, kg_ref, gk_acc, 1.0, D)):
                x = src[rows, :]
                rs = lax.rsqrt(jnp.mean(x * x, axis=0, keepdims=True) + EPS)
                xhat = x * rs
                dn = dsrc[0:HEAD_DIM, :] * mul
                acc[rows, :] += jnp.sum(dn * xhat, axis=1, keepdims=True)
                dxh = dn * gain[rows, :]
                dx = rs * (dxh - xhat * jnp.mean(dxh * xhat, axis=0, keepdims=True))
                du_ref[pl.ds(pl.multiple_of(base + h * HEAD_DIM, HEAD_DIM), HEAD_DIM), :] = dx.astype(BF16)
            return 0

        lax.fori_loop(0, H, head, 0, unroll=HEAD_UNROLL)
        du_ref[2 * D:3 * D, :] = dv_ref[...]
        dlf, carry = _lane_cumsum(dc_scr[...], tri_ref, fcar[...], True)
        fcar[...] = carry
        dfl = dlf * _sigmoid(-(f_ref[...] + bf_ref[...]))
        dbf_ref[...] += jnp.sum(dfl, axis=1, keepdims=True)
        du_ref[3 * D:NU, :] = dfl.astype(BF16)

        @pl.when(step == nT - 1)
        def _():
            for acc, ref in ((gq_acc, dqg_ref), (gk_acc, dkg_ref)):
                tot = jnp.zeros((HEAD_DIM, 1), F32)
                for h in range(H):
                    tot = tot + acc[h * HEAD_DIM:(h + 1) * HEAD_DIM, :]
                ref[...] = tot

    rev = lambda i: nT - 1 - i
    part = lambda j: pl.BlockSpec((D, T), lambda i: (j, rev(i)))
    colv = lambda n: pl.BlockSpec((n, 1), lambda i: (0, 0))
    return pl.pallas_call(
        body, name="fox_prep_bwd", grid=(nT,),
        in_specs=[part(0), part(1), pl.BlockSpec((LANES, T), lambda i: (3 * D // LANES, rev(i))),
                  pl.BlockSpec((H, T, HEAD_ROWS), lambda i: (0, rev(i), 0)),
                  pl.BlockSpec((H, HEAD_ROWS, T), lambda i: (0, 0, rev(i))), pl.BlockSpec((D, T), lambda i: (0, rev(i))),
                  colv(LANES), colv(D), colv(D), pl.BlockSpec((LANES, LANES), lambda i: (0, 0))],
        out_specs=[pl.BlockSpec((NU, T), lambda i: (0, rev(i))), colv(LANES), colv(HEAD_DIM), colv(HEAD_DIM)],
        out_shape=[jax.ShapeDtypeStruct((NU, S), BF16), jax.ShapeDtypeStruct((LANES, 1), F32),
                   jax.ShapeDtypeStruct((HEAD_DIM, 1), F32), jax.ShapeDtypeStruct((HEAD_DIM, 1), F32)],
        scratch_shapes=[pltpu.VMEM((D, 1), F32), pltpu.VMEM((D, 1), F32), pltpu.VMEM((LANES, 1), F32),
                        pltpu.VMEM((LANES, T), F32)],
        compiler_params=_params(("arbitrary",)),
    )(ut, ut, ut, dq, dkt, dvt, b_f, qg, kg, _tri_matrix(True))


def _block_diag_tiles(w):
    n = w.shape[0]
    per = min(MXU_DIM, n * LRU_BLOCK_DIM) // LRU_BLOCK_DIM
    eye = jnp.eye(per, dtype=w.dtype)
    w5 = w.reshape(n // per, per, LRU_BLOCK_DIM, 1, LRU_BLOCK_DIM) * eye[None, :, None, :, None]
    return w5.reshape(n // per, per * LRU_BLOCK_DIM, per * LRU_BLOCK_DIM).astype(BF16)


def _block_diag_extract(t, n):
    per = t.shape[-1] // LRU_BLOCK_DIM
    eye = jnp.eye(per, dtype=t.dtype)
    t5 = t.reshape(n // per, per, LRU_BLOCK_DIM, per, LRU_BLOCK_DIM) * eye[None, :, None, :, None]
    return t5.sum(axis=3).reshape(n, LRU_BLOCK_DIM, LRU_BLOCK_DIM)


def _local_step(x, tgt, small, wv, grad_view, comm=None):
    S, D = x.shape
    F = 4 * D
    H = D // HEAD_DIM
    nblk = D // LRU_BLOCK_DIM
    NU = 3 * D + LANES
    tq = max(LANES, min(512, S // 4))
    assert S % tq == 0
    vec = lambda a: a.reshape(1, -1).astype(F32)
    col = lambda a: a.reshape(-1, 1).astype(F32)
    mix_g, mlp_g = small["mix_norm"], small["mlp_norm"]
    conv_b = vec(small["lru_conv_b"])
    wr_bd, wi_bd = _block_diag_tiles(small["lru_w_r"][0]), _block_diag_tiles(small["lru_w_i"][0])
    b_r, b_i, lam = vec(small["lru_b_r"]), vec(small["lru_b_i"]), vec(small["lru_lambda"])
    b_f = jnp.pad(col(small["fox_b_f"]), ((0, LANES - H), (0, 0)))
    qg, kg = jnp.tile(col(small["fox_q_gain"]), (H, 1)), jnp.tile(col(small["fox_k_gain"]), (H, 1))
    X = lambda a: _View(a)
    grads = {}
    gout = functools.partial(grad_view, grads)

    def hosted(name, fn, *args):
        plan = comm.before(name, grads) if comm is not None else None
        res, side = fn(*args, plan=plan)
        if plan is not None:
            comm.after(name, side, wv)
        return res

    def hosted_mm(name, *args, **kw):
        plan = comm.before(name, grads) if comm is not None else None
        if plan is None:
            return _matmul(name, *args, **kw)
        res, side = _matmul(name, *args, plan=plan, **kw)
        comm.after(name, side, wv)
        return res

    two = lambda: [_fresh(S, D, F32), _fresh(S, D, BF16)]

    def mlp_up(l, hm):
        return hosted_mm(f"mlp{l}_up", X(hm), wv[f"w1_{l}"], S, F, D, outs=[_fresh(S, F, BF16)], epilogue=_ep_relu2)[0]

    def mlp_bwd(l, xin, hm, act, d, db):
        (dz,) = hosted_mm(f"mlp{l}_dact", X(db), wv[f"w2_{l}"], S, F, D, tb=True, outs=[_fresh(S, F, BF16)],
                          epilogue=_ep_drelu2, extras=[X(act)])
        (grads[f"w2_{l}"],) = _matmul(f"mlp{l}_dw2", X(act), X(db), F, D, S, ta=True, outs=[gout(f"w2_{l}")],
                                      epilogue=_ep_store)
        (grads[f"w1_{l}"],) = _matmul(f"mlp{l}_dw1", X(hm), X(dz), D, F, S, ta=True, outs=[gout(f"w1_{l}")],
                                      epilogue=_ep_store)
        return _matmul(f"mlp{l}_dhm", X(dz), wv[f"w1_{l}"], S, D, F, tb=True, outs=two(), n_sums=1,
                       epilogue=_ep_norm_bwd, extras=[X(xin), X(d)], vecs=[mlp_g[l:l + 1]])

    (h0,) = hosted("mix0_norm", _rms_fwd, "mix0_norm", x, mix_g[0:1], S, D)
    (u0,) = hosted_mm("lru_in", X(h0), wv["lru_in"], S, 2 * D, D, outs=[_fresh(S, 2 * D, F32)], epilogue=_ep_store)
    conv_w = small["conv_w"]
    y, xc, r, ig, hs = hosted("lru_fwd", _lru_fwd, u0, conv_w, conv_b, wr_bd, b_r, wi_bd, b_i, lam, S, D)
    x1, hm0 = _matmul("lru_out", X(y), wv["lru_out"], S, D, D, outs=two(), epilogue=_ep_resid_norm, extras=[X(x)],
                      vecs=[mlp_g[0:1]])
    act0 = mlp_up(0, hm0)
    x2, h1 = hosted_mm("mlp0_down", X(act0), wv["w2_0"], S, D, F, outs=two(), epilogue=_ep_resid_norm, extras=[X(x1)],
                       vecs=[mix_g[1:2]])
    (u1,) = _matmul("fox_in", wv["fox_in"], X(h1), NU, S, D, tb=True, outs=[_fresh(NU, S, F32)], epilogue=_ep_store)
    qat, kat, vat, ka = _fox_prep(u1, b_f, qg, kg, S, D, tq)
    o, o32, lse = hosted("attn_forward", _attn_forward, ka, qat, vat, S, D, tq)
    x3, hm1 = _matmul("fox_out", X(o), wv["fox_out"], S, D, D, ta=True, outs=two(), epilogue=_ep_resid_norm,
                      extras=[X(x2)], vecs=[mlp_g[1:2]])
    act1 = mlp_up(1, hm1)
    (x4,) = _matmul("mlp1_down", X(act1), wv["w2_1"], S, D, F, outs=[_fresh(S, D, F32)], epilogue=_ep_resid,
                    extras=[X(x3)])
    loss, d4, d4b = _loss_head(x4, tgt, S, D)

    d3, d3b, dg_mlp1 = mlp_bwd(1, x3, hm1, act1, d4, d4b)
    (do,) = _matmul("fox_dout", wv["fox_out"], X(d3b), D, S, D, tb=True, outs=[_fresh(D, S, BF16)], epilogue=_ep_store)
    (grads["fox_out"],) = _matmul("fox_dwout", X(o), X(d3b), D, D, S, outs=[gout("fox_out")], epilogue=_ep_store)
    doat, doa, qat1, qa1 = hosted("fox_bwd_prep", _fox_bwd_prep, do, o32, lse, qat, S, D, tq)
    dqn, dkn, dv = hosted("attn_backward", _attn_backward, qa1, doa, qat1, doat, ka, kat, vat, S, D, tq)
    du1, dbf, dqg, dkg = _fox_prep_bwd(u1, dqn, dkn, dv, b_f, qg, kg, S, D, tq)
    (grads["fox_in"],) = _matmul("fox_dwin", X(du1), X(h1), NU, D, S, outs=[gout("fox_in")], epilogue=_ep_store)
    d2, d2b, dg_mix1 = hosted_mm("fox_dh", X(du1), wv["fox_in"], S, D, NU, ta=True, outs=two(), n_sums=1,
                               epilogue=_ep_norm_bwd, extras=[X(x2), X(d3)], vecs=[mix_g[1:2]])
    d1, d1b, dg_mlp0 = mlp_bwd(0, x1, hm0, act0, d2, d2b)
    (grads["lru_out"],) = _matmul("lru_dwout", X(y), X(d1b), D, D, S, ta=True, outs=[gout("lru_out")],
                                  epilogue=_ep_store)
    (dy,) = hosted_mm("lru_dout", X(d1b), wv["lru_out"], S, D, D, tb=True, outs=[_fresh(S, D, F32)],
                      epilogue=_ep_store)
    du0, dcw, dcb, dlam, dbr, dbi, dwr, dwi = hosted("lru_bwd", _lru_bwd, dy, u0, xc, r, ig, hs, conv_w, wr_bd, wi_bd,
                                                     lam, S, D)
    (grads["lru_in"],) = _matmul("lru_dwin", X(h0), X(du0), D, 2 * D, S, ta=True, outs=[gout("lru_in")],
                                 epilogue=_ep_store)
    gx, dg_mix0 = hosted_mm("lru_dh", X(du0), wv["lru_in"], S, D, 2 * D, tb=True, outs=[_fresh(S, D, F32)], n_sums=1,
                            epilogue=lambda *a: _ep_norm_bwd(*a)[::2], extras=[X(x), X(d1)], vecs=[mix_g[0:1]])

    grads.update(
        mix_norm=jnp.concatenate([dg_mix0, dg_mix1], axis=0), mlp_norm=jnp.concatenate([dg_mlp0, dg_mlp1], axis=0),
        conv_w=dcw, lru_conv_b=dcb, lru_w_r=_block_diag_extract(dwr, nblk)[None], lru_b_r=dbr.reshape(1, nblk, -1),
        lru_w_i=_block_diag_extract(dwi, nblk)[None], lru_b_i=dbi.reshape(1, nblk, -1), lru_lambda=dlam,
        fox_b_f=dbf[:H].reshape(1, H), fox_q_gain=dqg.reshape(1, -1), fox_k_gain=dkg.reshape(1, -1))
    return loss, gx, grads


def _place():
    x, y, c = lax.axis_index("x"), lax.axis_index("y"), lax.axis_index("c")
    chips = [(1 - x, y), (x, 1 - y), (1 - x, 1 - y)]
    return x, y, c, 2 * x + y, chips


BOUNCE_BYTES = 1 << 20


def _bounce_shape(rows, cols, dtype):
    chunk = rows
    while chunk % 2 == 0 and chunk > 16 and chunk * cols * jnp.dtype(dtype).itemsize > BOUNCE_BYTES:
        chunk //= 2
    return pltpu.VMEM((2, chunk, cols), dtype)


def _bounce_copy(src, dst, buf, sem):
    chunk = buf.shape[1]
    n = src.shape[0] // chunk
    cin = lambda i: pltpu.make_async_copy(src.at[pl.ds(i * chunk, chunk)], buf.at[i % 2], sem.at[i % 2])
    cout = lambda i: pltpu.make_async_copy(buf.at[i % 2], dst.at[pl.ds(i * chunk, chunk)], sem.at[2 + i % 2])
    cin(0).start()
    for i in range(n):
        cin(i).wait()
        if i + 1 < n:
            if i >= 1:
                cout(i - 1).wait()
            cin(i + 1).start()
        cout(i).start()
    if n >= 2:
        cout(n - 2).wait()
    cout(n - 1).wait()


def _hbm_call(body, name, arrays, out_shape, n_dma_sems, bounce=()):
    scratch = [pltpu.SemaphoreType.DMA((k,)) for k in n_dma_sems]
    for rows, cols, dtype in bounce:
        scratch += [_bounce_shape(rows, cols, dtype), pltpu.SemaphoreType.DMA((4,))]
    return pl.pallas_call(
        body, name=name, in_specs=[ANY] * len(arrays), out_specs=[ANY] * len(out_shape), out_shape=out_shape,
        scratch_shapes=scratch,
        compiler_params=pltpu.CompilerParams(has_side_effects=True, vmem_limit_bytes=VMEM_LIMIT),
    )(*arrays)


class _Gather:
    def __init__(self, shards):
        n = self.n = len(shards)
        self.operands = list(shards)
        self.out_shape = [jax.ShapeDtypeStruct((N_CHIPS,) + tuple(a.shape), a.dtype) for a in shards]
        self.scratch = [pltpu.SemaphoreType.DMA((3 * n,)) for _ in range(4)]
        for a in shards:
            self.scratch += [_bounce_shape(a.shape[0], a.shape[1], a.dtype), pltpu.SemaphoreType.DMA((4,))]

    def _copies(self, ins, outs, scr):
        send, recv, fsend, frecv = scr[:4]
        x, y, c, s, chips = _place()

        def rows(a, chip_idx, which):
            hr = ins[a].shape[0] // 2
            return outs[a].at[chip_idx, pl.ds(which * hr, hr)]

        def landed(a, j, core):
            return rows(a, 2 * chips[j][0] + chips[j][1], core)

        def ici(a, j, mine):
            hr = ins[a].shape[0] // 2
            src, dst = (ins[a].at[pl.ds(c * hr, hr)], rows(a, s, c)) if mine else (landed(a, j, c),) * 2
            return pltpu.make_async_remote_copy(src_ref=src, dst_ref=dst, send_sem=send.at[3 * a + j],
                                                recv_sem=recv.at[3 * a + j], device_id=(*chips[j], c),
                                                device_id_type=MESH)

        def d2d(a, j, mine):
            ref = landed(a, j, c if mine else 1 - c)
            return pltpu.make_async_remote_copy(src_ref=ref, dst_ref=ref, send_sem=fsend.at[3 * a + j],
                                                recv_sem=frecv.at[3 * a + j], device_id=(x, y, 1 - c),
                                                device_id_type=MESH)

        return ici, d2d, s

    def start(self, ins, outs, scr):
        ici, _, _ = self._copies(ins, outs, scr)
        for a in range(self.n):
            for j in range(3):
                ici(a, j, True).start()

    def middle(self, ins, outs, scr):
        ici, d2d, s = self._copies(ins, outs, scr)
        for a in range(self.n):
            _bounce_copy(ins[a], outs[a].at[s], scr[4 + 2 * a], scr[5 + 2 * a])
        for a in range(self.n):
            for j in range(3):
                ici(a, j, False).wait_recv()
                d2d(a, j, True).start()

    def finish(self, ins, outs, scr):
        ici, d2d, _ = self._copies(ins, outs, scr)
        for a in range(self.n):
            for j in range(3):
                d2d(a, j, False).wait_recv()
        for a in range(self.n):
            for j in range(3):
                ici(a, j, True).wait_send()
                d2d(a, j, True).wait_send()


def _run_plan(name, plan):
    k_in, k_out = len(plan.operands), len(plan.out_shape)

    def body(*refs):
        parts = (refs[:k_in], refs[k_in:k_in + k_out], refs[k_in + k_out:])
        plan.start(*parts)
        plan.middle(*parts)
        plan.finish(*parts)

    return pl.pallas_call(
        body, name=name, in_specs=[ANY] * k_in, out_specs=[ANY] * k_out, out_shape=plan.out_shape,
        scratch_shapes=plan.scratch,
        compiler_params=pltpu.CompilerParams(has_side_effects=True, vmem_limit_bytes=VMEM_LIMIT),
    )(*plan.operands)


def _hosted_call(body, name, grid, in_specs, out_specs, out_shape, scratch_shapes, operands, sem, plan=None):
    if plan is None:
        res = pl.pallas_call(body, name=name, grid=grid, in_specs=in_specs, out_specs=out_specs, out_shape=out_shape,
                             scratch_shapes=scratch_shapes, compiler_params=_params(sem))(*operands)
        return res, None
    n_in, n_out, n_scr = len(in_specs), len(out_specs), len(scratch_shapes)
    k_in, k_out = len(plan.operands), len(plan.out_shape)
    total = int(np.prod(grid))
    late = max(0, total - 1 - max(1, total // 8))

    def hosted(*refs):
        ins, refs = refs[:n_in], refs[n_in:]
        p_ins, refs = refs[:k_in], refs[k_in:]
        outs, refs = refs[:n_out], refs[n_out:]
        p_outs, refs = refs[:k_out], refs[k_out:]
        scr, p_scr = refs[:n_scr], refs[n_scr:]
        step = pl.program_id(0)
        for d in range(1, len(grid)):
            step = step * grid[d] + pl.program_id(d)
        pl.when(step == 0)(lambda: plan.start(p_ins, p_outs, p_scr))
        body(*ins, *outs, *scr)
        pl.when(step == late)(lambda: plan.middle(p_ins, p_outs, p_scr))
        pl.when(step == total - 1)(lambda: plan.finish(p_ins, p_outs, p_scr))

    res = pl.pallas_call(
        hosted, name=name, grid=grid, in_specs=list(in_specs) + [ANY] * k_in, out_specs=list(out_specs) + [ANY] * k_out,
        out_shape=list(out_shape) + plan.out_shape, scratch_shapes=list(scratch_shapes) + plan.scratch,
        compiler_params=pltpu.CompilerParams(dimension_semantics=sem, vmem_limit_bytes=VMEM_LIMIT,
                                             has_side_effects=True),
    )(*operands, *plan.operands)
    return res[:n_out], res[n_out:]


def _all_gather(name, shards):
    return _run_plan(name, _Gather(shards))


class _Swap:
    def __init__(self, arrs):
        self.n = len(arrs)
        self.operands = list(arrs)
        self.out_shape = [jax.ShapeDtypeStruct((a.shape[0], a.shape[1] // 2, a.shape[2]), a.dtype) for a in arrs]
        self.scratch = [pltpu.SemaphoreType.DMA((self.n,)) for _ in range(2)]

    def _copy(self, ins, outs, scr, a):
        x, y, c, _, _ = _place()
        hr = ins[a].shape[1] // 2
        return pltpu.make_async_remote_copy(
            src_ref=ins[a].at[:, pl.ds((1 - c) * hr, hr)], dst_ref=outs[a], send_sem=scr[0].at[a],
            recv_sem=scr[1].at[a], device_id=(x, y, 1 - c), device_id_type=MESH)

    def start(self, ins, outs, scr):
        for a in range(self.n):
            self._copy(ins, outs, scr, a).start()

    def middle(self, ins, outs, scr):
        pass

    def finish(self, ins, outs, scr):
        for a in range(self.n):
            self._copy(ins, outs, scr, a).wait()


class _Scatter:
    def __init__(self, parts):
        n = self.n = len(parts)
        self.operands = list(parts)
        self.out_shape = [jax.ShapeDtypeStruct(a.shape, a.dtype) for a in parts]
        self.scratch = [pltpu.SemaphoreType.DMA((3 * n,)) for _ in range(2)]
        for a in parts:
            self.scratch += [_bounce_shape(a.shape[1], a.shape[2], a.dtype), pltpu.SemaphoreType.DMA((4,))]

    def _copy(self, ins, outs, scr, a, j, mine):
        x, y, c, s, chips = _place()
        t = 2 * chips[j][0] + chips[j][1]
        return pltpu.make_async_remote_copy(
            src_ref=ins[a].at[t], dst_ref=outs[a].at[s if mine else t], send_sem=scr[0].at[3 * a + j],
            recv_sem=scr[1].at[3 * a + j], device_id=(*chips[j], c), device_id_type=MESH)

    def start(self, ins, outs, scr):
        for a in range(self.n):
            for j in range(3):
                self._copy(ins, outs, scr, a, j, True).start()

    def middle(self, ins, outs, scr):
        s = _place()[3]
        for a in range(self.n):
            _bounce_copy(ins[a].at[s], outs[a].at[s], scr[2 + 2 * a], scr[3 + 2 * a])

    def finish(self, ins, outs, scr):
        for a in range(self.n):
            for j in range(3):
                self._copy(ins, outs, scr, a, j, False).wait_recv()
        for a in range(self.n):
            for j in range(3):
                self._copy(ins, outs, scr, a, j, True).wait_send()


def _pair_gather(name, halves):
    n = len(halves)

    def body(*refs):
        ins, outs = refs[:n], refs[n:2 * n]
        send, recv = refs[2 * n:2 * n + 2]
        stage = refs[2 * n + 2:]
        x, y, c, _, _ = _place()
        cps = []
        for a in range(n):
            hr = ins[a].shape[0]
            cp = pltpu.make_async_remote_copy(
                src_ref=ins[a], dst_ref=outs[a].at[pl.ds(c * hr, hr)], send_sem=send.at[a], recv_sem=recv.at[a],
                device_id=(x, y, 1 - c), device_id_type=MESH)
            cp.start()
            cps.append((cp, hr))
        for a, (cp, hr) in enumerate(cps):
            _bounce_copy(ins[a], outs[a].at[pl.ds(c * hr, hr)], stage[2 * a], stage[2 * a + 1])
        for a, (cp, hr) in enumerate(cps):
            cp.wait_send()
            theirs = outs[a].at[pl.ds((1 - c) * hr, hr)]
            pltpu.make_async_remote_copy(src_ref=theirs, dst_ref=theirs, send_sem=send.at[a], recv_sem=recv.at[a],
                                         device_id=(x, y, 1 - c), device_id_type=MESH).wait_recv()

    out_shape = [jax.ShapeDtypeStruct((2 * a.shape[0], a.shape[1]), a.dtype) for a in halves]
    return _hbm_call(body, name, halves, out_shape, (n, n),
                     bounce=[(a.shape[0], a.shape[1], a.dtype) for a in halves])


def _row_tile(rows, cols, itemsize, n_bufs):
    budget = VMEM_LIMIT // 2
    for t in range(min(rows, 1024) // 16 * 16, 0, -16):
        if rows % t == 0 and 2 * n_bufs * t * cols * itemsize <= budget:
            return t
    return rows


def _pair_add(name, g, gsib, core, out_dtype):
    _, r, cols = g.shape
    hr = r // 2
    t = _row_tile(hr, cols, 4, 3)
    per = hr // t

    def body(core_ref, a_ref, b_ref, o_ref):
        o_ref[...] = (a_ref[...].astype(F32) + b_ref[...].astype(F32)).astype(o_ref.dtype)

    grid_spec = pltpu.PrefetchScalarGridSpec(
        num_scalar_prefetch=1, grid=(N_CHIPS, per),
        in_specs=[pl.BlockSpec((None, t, cols), lambda s, i, core: (s, core[0] * per + i, 0)),
                  pl.BlockSpec((None, t, cols), lambda s, i, core: (s, i, 0))],
        out_specs=pl.BlockSpec((None, t, cols), lambda s, i, core: (s, i, 0)))
    return pl.pallas_call(body, name=name, grid_spec=grid_spec,
                          out_shape=jax.ShapeDtypeStruct((N_CHIPS, hr, cols), out_dtype),
                          compiler_params=_params(("arbitrary", "arbitrary")))(core, g, gsib)


def _chip_sum(name, parts):
    _, hr, cols = parts.shape
    t = _row_tile(hr, cols, 4, 5)

    def body(p_ref, o_ref):
        o_ref[...] = ((p_ref[0].astype(F32) + p_ref[1].astype(F32)) + p_ref[2].astype(F32)) + p_ref[3].astype(F32)

    return pl.pallas_call(
        body, name=name, grid=(hr // t,), in_specs=[pl.BlockSpec((N_CHIPS, t, cols), lambda i: (0, i, 0))],
        out_specs=pl.BlockSpec((t, cols), lambda i: (i, 0)), out_shape=jax.ShapeDtypeStruct((hr, cols), F32),
        compiler_params=_params(("arbitrary",)))(parts)


def _pair_partials(tag, arrs, sib, wire_dtypes, core):
    return _Scatter([_pair_add(f"{tag}_pair_add{i}", g, gs, core, dt)
                     for i, (g, gs, dt) in enumerate(zip(arrs, sib, wire_dtypes))])


def _finish_reduce(tag, scattered):
    halves = [_chip_sum(f"{tag}_chip_sum{i}", p) for i, p in enumerate(scattered)]
    return _pair_gather(f"{tag}_pair_gather", halves)


def _adamw(name, w, g_parts, m, v):
    thin = w.ndim == 3
    rows, cols = w.shape[0], w.shape[-1]
    n_parts = len(g_parts)
    part_rows = rows // n_parts
    t = max(d for d in range(1, 257) if part_rows % d == 0) if thin else _row_tile(part_rows, cols, 4, 7 + n_parts)
    per = part_rows // t
    c1 = 1.0 - ADAM_B1 ** ADAM_STEP
    c2 = 1.0 - ADAM_B2 ** ADAM_STEP

    def body(w_ref, m_ref, v_ref, *refs):
        g_refs, (go_ref, d_ref, nm_ref, nv_ref) = refs[:n_parts], refs[n_parts:]
        g = g_refs[0][...]
        for k in range(1, n_parts):
            g = jnp.where(pl.program_id(0) >= k * per, g_refs[k][...], g)
        go_ref[...] = g
        m = ADAM_B1 * m_ref[...] + (1.0 - ADAM_B1) * g
        v = ADAM_B2 * v_ref[...] + (1.0 - ADAM_B2) * (g * g)
        nm_ref[...] = m
        nv_ref[...] = v
        d_ref[...] = -ADAM_LR * ((m / c1) / (jnp.sqrt(v / c2) + ADAM_EPS) + ADAM_WD * w_ref[...])

    block = (t, 1, cols) if thin else (t, cols)
    at = lambda r: (r, 0, 0) if thin else (r, 0)
    spec = pl.BlockSpec(block, lambda i: at(i))
    g_specs = [pl.BlockSpec(block, lambda i, k=k: at(jnp.clip(i - k * per, 0, per - 1))) for k in range(n_parts)]
    shp = jax.ShapeDtypeStruct(w.shape, F32)
    return pl.pallas_call(body, name=name, grid=(rows // t,), in_specs=[spec] * 3 + g_specs, out_specs=[spec] * 4,
                          out_shape=[shp] * 4, compiler_params=_params(("arbitrary",)))(w, m, v, *g_parts)


_WEIGHTS = ["mix_norm", "mlp_norm", "mlp_w1", "mlp_w2", "lru_w_in", "lru_conv_w", "lru_conv_b", "lru_w_r", "lru_b_r",
            "lru_w_i", "lru_b_i", "lru_lambda", "lru_w_out", "fox_w_in", "fox_b_f", "fox_q_gain", "fox_k_gain",
            "fox_w_out"]
_REPLICATED = ["mix_norm", "mlp_norm", "lru_conv_b", "lru_w_r", "lru_b_r", "lru_w_i", "lru_b_i", "lru_lambda",
               "fox_b_f", "fox_q_gain", "fox_k_gain"]
_PACK_TILE = 2 * SUBLANES * LANES


def _as2d(a):
    return a.reshape(-1, a.shape[-1])


def kernel(x, mix_norm, mlp_norm, mlp_w1, mlp_w2, lru_w_in, lru_conv_w, lru_conv_b, lru_w_r, lru_b_r, lru_w_i, lru_b_i, lru_lambda, lru_w_out, fox_w_in, fox_b_f, fox_q_gain, fox_k_gain, fox_w_out, loss_target, m_mix_norm, m_mlp_norm, m_mlp_w1, m_mlp_w2, m_lru_w_in, m_lru_conv_w, m_lru_conv_b, m_lru_w_r, m_lru_b_r, m_lru_w_i, m_lru_b_i, m_lru_lambda, m_lru_w_out, m_fox_w_in, m_fox_b_f, m_fox_q_gain, m_fox_k_gain, m_fox_w_out, v_mix_norm, v_mlp_norm, v_mlp_w1, v_mlp_w2, v_lru_w_in, v_lru_conv_w, v_lru_conv_b, v_lru_w_r, v_lru_b_r, v_lru_w_i, v_lru_b_i, v_lru_lambda, v_lru_w_out, v_fox_w_in, v_fox_b_f, v_fox_q_gain, v_fox_k_gain, v_fox_w_out):
    args = dict(locals())
    W = {n: args[n] for n in _WEIGHTS}
    Mo = {n: args["m_" + n] for n in _WEIGHTS}
    Vo = {n: args["v_" + n] for n in _WEIGHTS}
    S, D = x.shape[1], x.shape[2]
    F = 4 * D
    H = D // HEAD_DIM
    NU = 3 * D + LANES
    FQ, DQ = F // N_CHIPS, D // N_CHIPS
    nfox = fox_w_in.shape[-1]
    chip = 2 * lax.axis_index("x") + lax.axis_index("y")
    core = lax.axis_index("c").astype(jnp.int32).reshape(1)

    cw_flat = jnp.pad(lru_conv_w.reshape(-1), (0, _PACK_TILE - CONV_WIDTH * DQ)).reshape(2 * SUBLANES, LANES)
    w1s, w2s = mlp_w1.astype(BF16), mlp_w2.astype(BF16)
    wv = {}
    small = {n: W[n] for n in _REPLICATED}
    scattered = {}
    members = {"g1": ["w2_1", "w1_1", "fox_out"], "g2": ["fox_in"], "g3": ["w2_0", "w1_0"], "g4": ["lru_out", "lru_in"]}
    swap_at = {"fox_bwd_prep": "g1", "fox_dh": "g2", "lru_dout": "g3"}
    scatter_at = {"attn_backward": "g1", "mlp0_dact": "g2", "lru_bwd": "g3", "lru_dh": "g4"}
    swapped = {}

    fox_rows = -(-nfox // (4 * SUBLANES)) * (4 * SUBLANES)
    fox_t = jnp.pad(jnp.transpose(fox_w_in[0]).astype(BF16), ((0, fox_rows - nfox), (0, 0)))

    def shard_major(name, g):
        if name == "fox_in":
            return jnp.pad(g[:nfox * N_CHIPS].reshape(N_CHIPS, nfox, D), ((0, 0), (0, fox_rows - nfox), (0, 0)))
        return g

    class Comm:
        @staticmethod
        def before(name, grads):
            if name == "mix0_norm":
                return _Gather([lru_w_in[0].astype(BF16)])
            if name == "lru_in":
                return _Gather([lru_w_out[0].astype(BF16), cw_flat])
            if name == "lru_fwd":
                return _Gather([w1s[0]])
            if name == "mlp0_up":
                return _Gather([w2s[0]])
            if name == "mlp0_down":
                return _Gather([fox_t])
            if name == "attn_forward":
                return _Gather([fox_w_out[0].astype(BF16), w1s[1], w2s[1]])
            if name in swap_at:
                group = swap_at[name]
                swapped[group] = [[shard_major(n, grads[n]) for n in members[group]], None]
                return _Swap(swapped[group][0])
            if name in scatter_at:
                group = scatter_at[name]
                if group not in swapped:
                    arrs = [shard_major(n, grads[n]) for n in members[group]]
                    swapped[group] = [arrs, _run_plan(f"{group}_pair_swap", _Swap(arrs))]
                arrs, sib = swapped[group]
                return _pair_partials(group, arrs, sib, [BF16] * len(arrs), core)
            return None

        @staticmethod
        def after(name, res, wv):
            if name == "mix0_norm":
                wv.update(lru_in=_View(res[0], "cs"))
            elif name == "lru_in":
                wv.update(lru_out=_View(res[0], "rs"))
                taps = res[1].reshape(N_CHIPS, -1)[:, :CONV_WIDTH * DQ].reshape(N_CHIPS, CONV_WIDTH, DQ)
                small["conv_w"] = jnp.transpose(taps, (1, 0, 2)).reshape(CONV_WIDTH, D)
            elif name == "lru_fwd":
                wv.update(w1_0=_View(res[0], "cs"))
            elif name == "mlp0_up":
                wv.update(w2_0=_View(res[0], "rs"))
            elif name == "mlp0_down":
                fox_full = jnp.concatenate([res[0][s, :nfox] for s in range(N_CHIPS)], axis=0)
                wv.update(fox_in=_View(jnp.pad(fox_full, ((0, NU - fox_full.shape[0]), (0, 0)))))
            elif name == "attn_forward":
                wv.update(fox_out=_View(res[0], "rs"), w1_1=_View(res[1], "cs"), w2_1=_View(res[2], "rs"))
            elif name in swap_at:
                swapped[swap_at[name]][1] = res
            else:
                scattered.update(zip(members[scatter_at[name]], res))

    def grad_view(grads, name):
        if name in ("w1_0", "w1_1"):
            return _View(None, "cs", shape=(N_CHIPS, D, FQ), dtype=BF16)
        if name in ("w2_0", "w2_1"):
            return _View(None, "rs", shape=(N_CHIPS, FQ, D), dtype=BF16)
        if name == "lru_in":
            return _View(None, "cs", shape=(N_CHIPS, D, 2 * D // N_CHIPS), dtype=BF16)
        if name in ("lru_out", "fox_out"):
            return _View(None, "rs", shape=(N_CHIPS, DQ, D), dtype=BF16)
        return _View(None, shape=(NU, D), dtype=BF16)

    loss, gx, grads = _local_step(x[0], loss_target[0], small, wv, grad_view, Comm)

    pack_names = _REPLICATED + ["conv_w"]
    flat = jnp.concatenate([grads[n].reshape(-1).astype(F32) for n in pack_names] + [loss.reshape(-1)])
    per_chip = -(-flat.shape[0] // (N_CHIPS * _PACK_TILE)) * _PACK_TILE
    pack = jnp.pad(flat, (0, N_CHIPS * per_chip - flat.shape[0])).reshape(N_CHIPS, per_chip // LANES, LANES)
    pack_sib = _run_plan("pack_pair_swap", _Swap([pack]))
    (scattered["pack"],) = _run_plan("pack_chip_scatter", _pair_partials("pack", [pack], pack_sib, [F32], core))
    order = ["w1_0", "w1_1", "w2_0", "w2_1", "lru_in", "lru_out", "fox_in", "fox_out", "pack"]
    red = dict(zip(order, _finish_reduce("grads", [scattered[n] for n in order])))
    (all_pack,) = _all_gather("gather_small_grads", [red["pack"]])
    all_flat = all_pack.reshape(-1)
    G = {}
    off = 0
    for n in pack_names:
        shape = grads[n].shape if n == "conv_w" else W[n].shape
        size = int(np.prod(shape))
        G[n] = all_flat[off:off + size].reshape(shape)
        off += size
    total = all_flat[off]
    G["lru_conv_w"] = lax.dynamic_slice_in_dim(G.pop("conv_w"), chip * DQ, DQ, axis=1)[None]
    parts = {n: [_as2d(G[n])] for n in G}
    parts.update(mlp_w1=[red["w1_0"], red["w1_1"]], mlp_w2=[red["w2_0"], red["w2_1"]], lru_w_in=[red["lru_in"]],
                 lru_w_out=[red["lru_out"]], fox_w_in=[red["fox_in"][:nfox, None, :]], fox_w_out=[red["fox_out"]])

    delta, new_m, new_v = {}, {}, {}
    for n in _WEIGHTS:
        if n == "fox_w_in":
            to_thin = lambda a: jnp.transpose(a, (2, 0, 1))
            res = _adamw(f"adamw_{n}", to_thin(W[n]), parts[n], to_thin(Mo[n]), to_thin(Vo[n]))
            G[n], delta[n], new_m[n], new_v[n] = (jnp.transpose(t, (1, 2, 0)) for t in res)
            continue
        go, d, nm, nv = _adamw(f"adamw_{n}", _as2d(W[n]), parts[n], _as2d(Mo[n]), _as2d(Vo[n]))
        G[n], delta[n], new_m[n], new_v[n] = (t.reshape(W[n].shape) for t in (go, d, nm, nv))

    return (total, gx[None], *[G[n] for n in _WEIGHTS], *[delta[n] for n in _WEIGHTS],
            *[new_m[n] for n in _WEIGHTS], *[new_v[n] for n in _WEIGHTS])
```

```python
import functools

import numpy as np
import jax
import jax.numpy as jnp
from jax import lax
from jax.experimental import pallas as pl
from jax.experimental.pallas import tpu as pltpu

F32 = jnp.float32
BF16 = jnp.bfloat16

HEAD_DIM = 64
LRU_BLOCK_DIM = 64
CONV_WIDTH = 4
LRU_C = 8.0
EPS = 1e-6
NEG_INF = -1e30
ADAM_LR = 0.001
ADAM_B1 = 0.9
ADAM_B2 = 0.999
ADAM_EPS = 1e-08
ADAM_WD = 0.01
ADAM_STEP = 10

N_CHIPS = 4
LANES = 128
SUBLANES = 8
MXU_DIM = 256
VMEM_LIMIT = 52 * 1024 * 1024
MESH = pl.DeviceIdType.MESH
ANY = pl.BlockSpec(memory_space=pl.ANY)


def _pick(n, prefs):
    for p in prefs:
        if p <= n and n % p == 0:
            return p
    return n


def _params(sem=None):
    return pltpu.CompilerParams(dimension_semantics=sem, vmem_limit_bytes=VMEM_LIMIT)


class _View:
    def __init__(self, arr, kind="plain", shape=None, dtype=None):
        self.arr = arr
        self.kind = kind
        self.shape = tuple(arr.shape) if arr is not None else tuple(shape)
        self.dtype = arr.dtype if arr is not None else dtype

    def limits(self):
        if self.kind == "plain":
            return 0, 0
        return self.shape[-2], (self.shape[-1] if self.kind == "cs" else 0)

    def spec(self, br, bc, fr, fc):
        if self.kind == "plain":
            return pl.BlockSpec((br, bc), lambda *g: (fr(*g), fc(*g)))
        rows, ncol = self.shape[-2:]
        assert rows % br == 0 and ncol % bc == 0, (self.shape, br, bc)
        if self.kind == "cs":
            per = ncol // bc
            return pl.BlockSpec((None, br, bc), lambda *g: (fc(*g) // per, fr(*g), fc(*g) % per))
        per = rows // br
        return pl.BlockSpec((None, br, bc), lambda *g: (fr(*g) // per, fr(*g) % per, fc(*g)))


def _bf(x):
    return x if x.dtype == BF16 else x.astype(BF16)


def _matmul(name, A, B, M, N, K, *, ta=False, tb=False, outs, epilogue, extras=(), vecs=(), n_sums=0,
            tm=None, tn=None, tk=None, plan=None):
    lim = {"m": [M], "n": [N], "k": [K]}
    for view, (rdim, cdim) in ([(A, "km" if ta else "mk"), (B, "nk" if tb else "kn")]
                               + [(e, "mn") for e in extras] + [(o, "mn") for o in outs]):
        r_lim, c_lim = view.limits()
        lim[rdim].append(r_lim)
        lim[cdim].append(c_lim)
    tm = tm or _pick(int(np.gcd.reduce(lim["m"])), (1024, 640, 512, 256, 128))
    tn = tn or _pick(int(np.gcd.reduce(lim["n"])), (1024, 640, 512, 256, 128))
    tk = tk or _pick(int(np.gcd.reduce(lim["k"])), (1024, 640, 512, 256, 128))
    nk = K // tk
    gi = lambda i, j, k: i
    gj = lambda i, j, k: j
    gk = lambda i, j, k: k
    a_spec = A.spec(tk, tm, gk, gi) if ta else A.spec(tm, tk, gi, gk)
    b_spec = B.spec(tn, tk, gj, gk) if tb else B.spec(tk, tn, gk, gj)
    ca = 0 if ta else 1
    cb = 1 if tb else 0
    ne, no = len(extras) + len(vecs), len(outs)
    assert n_sums == 0 or tn == N
    row_spec = pl.BlockSpec((1, tn), lambda i, j, k: (0, j))
    in_specs = [a_spec, b_spec] + [e.spec(tm, tn, gi, gj) for e in extras] + [row_spec] * len(vecs)
    operands = [A.arr, B.arr] + [e.arr for e in extras] + list(vecs)
    out_specs = [o.spec(tm, tn, gi, gj) for o in outs] + [row_spec] * n_sums
    out_shape = ([jax.ShapeDtypeStruct(o.shape, o.dtype) for o in outs]
                 + [jax.ShapeDtypeStruct((1, N), F32)] * n_sums)

    def body(*refs):
        a_ref, b_ref = refs[0], refs[1]
        ex = refs[2:2 + ne]
        o_refs = refs[2 + ne:2 + ne + no]
        s_refs = refs[2 + ne + no:2 + ne + no + n_sums]
        first_row_tile = pl.program_id(0) == 0

        def prod():
            return lax.dot_general(_bf(a_ref[...]), _bf(b_ref[...]), (((ca,), (cb,)), ((), ())),
                                   preferred_element_type=F32)

        def finish(acc):
            res = epilogue(acc, *[e[...] for e in ex])
            for o_ref, r in zip(o_refs, res[:no]):
                o_ref[...] = r.astype(o_ref.dtype)
            for s_ref, r in zip(s_refs, res[no:]):
                def assign(s_ref=s_ref, r=r):
                    s_ref[...] = r

                def accumulate(s_ref=s_ref, r=r):
                    s_ref[...] += r

                pl.when(first_row_tile)(assign)
                pl.when(jnp.logical_not(first_row_tile))(accumulate)

        if nk == 1:
            finish(prod())
        else:
            acc_ref = refs[-1]
            k = pl.program_id(2)

            @pl.when(k == 0)
            def _():
                acc_ref[...] = jnp.zeros_like(acc_ref)

            acc_ref[...] += prod()

            @pl.when(k == nk - 1)
            def _():
                finish(acc_ref[...])

    res, side = _hosted_call(body, name, (M // tm, N // tn, nk), in_specs, out_specs, out_shape,
                             [pltpu.VMEM((tm, tn), F32)] if nk > 1 else [], operands,
                             ("arbitrary", "arbitrary", "arbitrary"), plan)
    return res if plan is None else (res, side)


def _ep_store(acc):
    return (acc,)


def _ep_resid(acc, res):
    return (res + acc,)


def _ep_resid_norm(acc, res, g):
    xo = res + acc
    r = lax.rsqrt(jnp.mean(xo * xo, axis=-1, keepdims=True) + EPS)
    return (xo, (xo * r) * g)


def _ep_norm_bwd(acc, x, dres, g):
    r = lax.rsqrt(jnp.mean(x * x, axis=-1, keepdims=True) + EPS)
    xhat = x * r
    dxn = acc * g
    tot = dres + r * (dxn - xhat * jnp.mean(dxn * xhat, axis=-1, keepdims=True))
    return (tot, tot, jnp.sum(acc * xhat, axis=0, keepdims=True))


def _ep_relu2(acc):
    zp = jnp.maximum(acc, 0.0)
    return (zp * zp,)


def _ep_drelu2(acc, act):
    return (acc * (2.0 * jnp.sqrt(act.astype(F32))),)


def _fresh(M, N, dtype):
    return _View(None, shape=(M, N), dtype=dtype)


def _rms_fwd(name, x, g, S, D, plan=None):
    T = _pick(S, (512, 256, 128))

    def body(x_ref, g_ref, h_ref):
        x = x_ref[...]
        r = lax.rsqrt(jnp.mean(x * x, axis=-1, keepdims=True) + EPS)
        h_ref[...] = ((x * r) * g_ref[...]).astype(BF16)

    return _hosted_call(body, name, (S // T,),
                        [pl.BlockSpec((T, D), lambda i: (i, 0)), pl.BlockSpec((1, D), lambda i: (0, 0))],
                        [pl.BlockSpec((T, D), lambda i: (i, 0))], [jax.ShapeDtypeStruct((S, D), BF16)], [], (x, g),
                        ("arbitrary",), plan)


def _loss_head(x, tgt, S, D):
    T = _pick(S, (512, 256, 128))

    def body(x_ref, t_ref, loss_ref, d_ref, db_ref):
        @pl.when(pl.program_id(0) == 0)
        def _():
            loss_ref[...] = jnp.zeros_like(loss_ref)

        e = x_ref[...] - t_ref[...]
        loss_ref[...] += 0.5 * jnp.sum(jnp.mean(e * e, axis=-1, keepdims=True), axis=0, keepdims=True)
        d = e * (1.0 / D)
        d_ref[...] = d
        db_ref[...] = d.astype(BF16)

    row = pl.BlockSpec((T, D), lambda i: (i, 0))
    return pl.pallas_call(
        body, name="loss_head", grid=(S // T,), in_specs=[row, row],
        out_specs=[pl.BlockSpec((1, 1), lambda i: (0, 0)), row, row],
        out_shape=[jax.ShapeDtypeStruct((1, 1), F32), jax.ShapeDtypeStruct((S, D), F32),
                   jax.ShapeDtypeStruct((S, D), BF16)],
        compiler_params=_params(("arbitrary",)),
    )(x, tgt)


def _sigmoid(z):
    return 1.0 / (1.0 + jnp.exp(-z))


def _log_sigmoid(z):
    return jnp.minimum(z, 0.0) - jnp.log(1.0 + jnp.exp(-jnp.abs(z)))


_GELU_K = 0.7978845608028654
_GELU_C = 0.044715


def _gelu(x):
    t = jnp.tanh(_GELU_K * (x + _GELU_C * (x * x * x)))
    return 0.5 * x * (1.0 + t)


def _gelu_and_grad(x):
    x2 = x * x
    t = jnp.tanh(_GELU_K * (x + _GELU_C * (x2 * x)))
    g = 0.5 * x * (1.0 + t)
    dg = 0.5 * (1.0 + t) + 0.5 * x * (1.0 - t * t) * (_GELU_K * (1.0 + 3.0 * _GELU_C * x2))
    return g, dg


def _decay_terms(r, ls):
    la = LRU_C * r * ls
    a = jnp.exp(la)
    a2 = a * a
    mult = jnp.sqrt(-jnp.tanh(la) * (a2 + 1.0))
    return a, a2, mult


def _lru_fwd(u0, conv_w, conv_b, wr_bd, b_r, wi_bd, b_i, lam, S, D, plan=None):
    T = _pick(S, (256, 128))
    GT = wr_bd.shape[-1]
    nG = D // GT

    def body(gb_ref, xb_ref, cw_ref, cb_ref, wr_ref, br_ref, wi_ref, bi_ref, lam_ref,
             y_ref, xc_ref, r_ref, i_ref, hs_ref, ext, a_scr, hcar):
        @pl.when(pl.program_id(0) == 0)
        def _():
            ext[0:SUBLANES, :] = jnp.zeros((SUBLANES, D), F32)
            hcar[...] = jnp.zeros_like(hcar)

        xb = xb_ref[...]
        ext[SUBLANES:SUBLANES + T, :] = xb
        xc = cb_ref[...]
        for k in range(CONV_WIDTH):
            xc = xc + ext[pl.ds(SUBLANES - (CONV_WIDTH - 1) + k, T), :] * cw_ref[k:k + 1, :]
        ext[0:SUBLANES, :] = xb[T - SUBLANES:T, :]
        xc_ref[...] = xc
        xcb = xc.astype(BF16)
        for g in range(nG):
            sl = slice(g * GT, (g + 1) * GT)
            zr = jnp.dot(xcb[:, sl], wr_ref[g], preferred_element_type=F32) + br_ref[:, sl]
            zi = jnp.dot(xcb[:, sl], wi_ref[g], preferred_element_type=F32) + bi_ref[:, sl]
            r_ref[:, sl] = _sigmoid(zr)
            i_ref[:, sl] = _sigmoid(zi)
        r = r_ref[...]
        a, _, mult = _decay_terms(r, _log_sigmoid(lam_ref[...]))
        a_scr[...] = a
        hs_ref[...] = mult * (i_ref[...] * xc)

        def step(t, h):
            h = a_scr[pl.ds(t, 1), :] * h + hs_ref[pl.ds(t, 1), :]
            hs_ref[pl.ds(t, 1), :] = h
            return h

        hcar[...] = lax.fori_loop(0, T, step, hcar[...], unroll=8)
        y_ref[...] = (_gelu(gb_ref[...]) * hs_ref[...]).astype(BF16)

    row = pl.BlockSpec((T, D), lambda i: (i, 0))
    vec = pl.BlockSpec((1, D), lambda i: (0, 0))
    bd = pl.BlockSpec((nG, GT, GT), lambda i: (0, 0, 0))
    f32o = jax.ShapeDtypeStruct((S, D), F32)
    return _hosted_call(
        body, "lru_fwd", (S // T,),
        [row, pl.BlockSpec((T, D), lambda i: (i, 1)), pl.BlockSpec((CONV_WIDTH, D), lambda i: (0, 0)), vec,
         bd, vec, bd, vec, vec],
        [row, row, row, row, row], [jax.ShapeDtypeStruct((S, D), BF16), f32o, f32o, f32o, f32o],
        [pltpu.VMEM((T + SUBLANES, D), F32), pltpu.VMEM((T, D), F32), pltpu.VMEM((1, D), F32)],
        (u0, u0, conv_w, conv_b, wr_bd, b_r, wi_bd, b_i, lam), ("arbitrary",), plan)


def _lru_bwd(dy, u0, xc, r, ig, hs, conv_w, wr_bd, wi_bd, lam, S, D, plan=None):
    T = _pick(S, (128,))
    nT = S // T
    GT = wr_bd.shape[-1]
    nG = D // GT
    W = CONV_WIDTH

    def body(dy_ref, gb_ref, xb_ref, xbp_ref, xc_ref, r_ref, i_ref, hs_ref, hsp_ref, cw_ref, wr_ref, wi_ref, lam_ref,
             du_ref, dcw_ref, dcb_ref, dlam_ref, dbr_ref, dbi_ref, dwr_ref, dwi_ref,
             a_scr, dh_scr, exth, extx, extd, dxc_scr, dz_scr, carry):
        step = pl.program_id(0)
        first_tile = step == nT - 1

        @pl.when(step == 0)
        def _():
            for ref in (dcw_ref, dcb_ref, dlam_ref, dbr_ref, dbi_ref, dwr_ref, dwi_ref, carry):
                ref[...] = jnp.zeros_like(ref)
            extd[T:T + SUBLANES, :] = jnp.zeros((SUBLANES, D), F32)

        hs = hs_ref[...]
        dy = dy_ref[...]
        g, dgelu = _gelu_and_grad(gb_ref[...])
        du_ref[:, 0:D] = (dy * hs * dgelu).astype(BF16)
        r = r_ref[...]
        lam = lam_ref[...]
        ls = _log_sigmoid(lam)
        a, a2, mult = _decay_terms(r, ls)
        a_scr[...] = a
        dh_scr[...] = dy * g

        def rstep(j, c):
            t = T - 1 - j
            d = dh_scr[pl.ds(t, 1), :] + c
            dh_scr[pl.ds(t, 1), :] = d
            return a_scr[pl.ds(t, 1), :] * d

        carry[...] = lax.fori_loop(0, T, rstep, carry[...], unroll=8)
        dh = dh_scr[...]
        keep = jnp.where(first_tile, 0.0, 1.0)
        exth[0:SUBLANES, :] = hsp_ref[...] * keep
        exth[SUBLANES:SUBLANES + T, :] = hs
        hprev = exth[pl.ds(SUBLANES - 1, T), :]
        xc = xc_ref[...]
        ig = i_ref[...]
        da = dh * hprev
        dmult = dh * (ig * xc)
        dla = da * a - dmult * (a2 / mult)
        dlam_ref[...] += jnp.sum(dla * r, axis=0, keepdims=True) * (LRU_C * _sigmoid(-lam))
        dzr = (dla * (LRU_C * ls)) * (r * (1.0 - r))
        dzi = (dh * (mult * xc)) * (ig * (1.0 - ig))
        dbr_ref[...] += jnp.sum(dzr, axis=0, keepdims=True)
        dbi_ref[...] += jnp.sum(dzi, axis=0, keepdims=True)
        dxc_scr[...] = dh * (mult * ig)
        xcb = xc.astype(BF16)
        dz_scr[0] = dzr.astype(BF16)
        dz_scr[1] = dzi.astype(BF16)
        nt_dims = (((1,), (1,)), ((), ()))
        tn_dims = (((0,), (0,)), ((), ()))
        for gq in range(nG):
            sl = slice(gq * GT, (gq + 1) * GT)
            zr_g = dz_scr[0, :, sl]
            zi_g = dz_scr[1, :, sl]
            dxc_scr[:, sl] += (lax.dot_general(zr_g, wr_ref[gq], nt_dims, preferred_element_type=F32)
                               + lax.dot_general(zi_g, wi_ref[gq], nt_dims, preferred_element_type=F32))
            dwr_ref[gq] += lax.dot_general(xcb[:, sl], zr_g, tn_dims, preferred_element_type=F32)
            dwi_ref[gq] += lax.dot_general(xcb[:, sl], zi_g, tn_dims, preferred_element_type=F32)
        dxc = dxc_scr[...]
        dcb_ref[...] += jnp.sum(dxc, axis=0, keepdims=True)
        extx[0:SUBLANES, :] = xbp_ref[...] * keep
        extx[SUBLANES:SUBLANES + T, :] = xb_ref[...]
        extd[0:T, :] = dxc
        dxb = jnp.zeros((T, D), F32)
        for k in range(W):
            dxb = dxb + extd[pl.ds(W - 1 - k, T), :] * cw_ref[k:k + 1, :]
            dcw_ref[k:k + 1, :] += jnp.sum(dxc * extx[pl.ds(SUBLANES - (W - 1) + k, T), :], axis=0, keepdims=True)
        extd[T:T + SUBLANES, :] = dxc[0:SUBLANES, :]
        du_ref[:, D:2 * D] = dxb.astype(BF16)

    rev = lambda i: nT - 1 - i
    tpb = T // SUBLANES
    prev8 = lambda i: jnp.maximum(rev(i) * tpb - 1, 0)
    row = pl.BlockSpec((T, D), lambda i: (rev(i), 0))
    vec = pl.BlockSpec((1, D), lambda i: (0, 0))
    bd = pl.BlockSpec((nG, GT, GT), lambda i: (0, 0, 0))
    vec_o = jax.ShapeDtypeStruct((1, D), F32)
    bd_o = jax.ShapeDtypeStruct((nG, GT, GT), F32)
    return _hosted_call(
        body, "lru_bwd", (nT,),
        [row, row, pl.BlockSpec((T, D), lambda i: (rev(i), 1)), pl.BlockSpec((SUBLANES, D), lambda i: (prev8(i), 1)),
         row, row, row, row, pl.BlockSpec((SUBLANES, D), lambda i: (prev8(i), 0)),
         pl.BlockSpec((W, D), lambda i: (0, 0)), bd, bd, vec],
        [pl.BlockSpec((T, 2 * D), lambda i: (rev(i), 0)), pl.BlockSpec((W, D), lambda i: (0, 0)),
         vec, vec, vec, vec, bd, bd],
        [jax.ShapeDtypeStruct((S, 2 * D), BF16), jax.ShapeDtypeStruct((W, D), F32), vec_o, vec_o, vec_o, vec_o, bd_o, bd_o],
        [pltpu.VMEM((T, D), F32), pltpu.VMEM((T, D), F32), pltpu.VMEM((T + SUBLANES, D), F32),
         pltpu.VMEM((T + SUBLANES, D), F32), pltpu.VMEM((T + SUBLANES, D), F32),
         pltpu.VMEM((T, D), F32), pltpu.VMEM((2, T, D), BF16), pltpu.VMEM((1, D), F32)],
        (dy, u0, u0, u0, xc, r, ig, hs, hs, conv_w, wr_bd, wi_bd, lam), ("arbitrary",), plan)


AUG_ROWS = 16
HEAD_ROWS = 128
LSE_ROW = HEAD_DIM + 6
ONES_ROW_Q = HEAD_DIM + 3
ONES_COL_K = HEAD_DIM
ONES_ROW_V = HEAD_DIM
PREP_LANES = 512
HEAD_UNROLL = 4


def _split3(x):
    b1 = x.astype(BF16).astype(F32)
    r = x - b1
    b2 = r.astype(BF16).astype(F32)
    return b1, b2, r - b2


def _head_block(x, aug, T):
    row = lax.broadcasted_iota(jnp.int32, (AUG_ROWS, T), 0)
    blk = jnp.zeros((AUG_ROWS, T), F32)
    for i, e in enumerate(aug):
        blk = jnp.where(row == i, e, blk)
    return jnp.concatenate([x, blk, jnp.zeros((HEAD_ROWS - HEAD_DIM - AUG_ROWS, T), F32)], axis=0)


def _tri_matrix(lower):
    i = np.arange(LANES)
    m = (i[:, None] >= i[None, :]) if lower else (i[:, None] <= i[None, :])
    return jnp.asarray(m.astype(np.float32), BF16)


def _lane_cumsum(x, tri_ref, carry, reverse):
    n = x.shape[1] // LANES
    tri = tri_ref[...]
    out = [None] * n
    for j in (range(n - 1, -1, -1) if reverse else range(n)):
        cs = carry
        for part in _split3(x[:, j * LANES:(j + 1) * LANES]):
            cs = cs + jnp.dot(part.astype(BF16), tri, preferred_element_type=F32)
        out[j] = cs
        carry = cs[:, 0:1] if reverse else cs[:, LANES - 1:LANES]
    return jnp.concatenate(out, axis=1), carry


def _head_rows(h):
    return pl.ds(pl.multiple_of(h * HEAD_DIM, HEAD_DIM), HEAD_DIM)


def _fox_prep(ut, b_f, qg, kg, S, D, tq):
    H = D // HEAD_DIM
    T = min(tq, PREP_LANES)
    per = tq // T
    scale = HEAD_DIM ** -0.5

    def body(q_ref, k_ref, v_ref, f_ref, bf_ref, qg_ref, kg_ref, tri_ref,
             qat_ref, kat_ref, vat_ref, ka_ref, c_scr, ccar):
        @pl.when(pl.program_id(0) == 0)
        def _():
            ccar[...] = jnp.zeros_like(ccar)

        c, carry = _lane_cumsum(_log_sigmoid(f_ref[...] + bf_ref[...]), tri_ref, ccar[...], False)
        c_scr[...] = c
        ccar[...] = carry

        def head(h, _):
            rows = _head_rows(h)
            c1, c2, c3 = _split3(c_scr[pl.ds(h, 1), :])

            def normed(src, gain, mul):
                x = src[rows, :]
                rs = lax.rsqrt(jnp.mean(x * x, axis=0, keepdims=True) + EPS)
                return ((x * rs) * gain[rows, :]) * mul

            qat_ref[h] = _head_block(normed(q_ref, qg_ref, scale), [c1, c2, c3, 1.0, 1.0, 1.0], T).astype(BF16)
            kb = _head_block(normed(k_ref, kg_ref, 1.0), [1.0, 1.0, 1.0, -c1, -c2, -c3, 1.0, 1.0, 1.0], T)
            kat_ref[h] = kb.astype(BF16)
            ka_ref[h] = kb.T.astype(BF16)
            vat_ref[h] = _head_block(v_ref[rows, :], [1.0, 1.0, 1.0], T).astype(BF16)
            return 0

        lax.fori_loop(0, H, head, 0, unroll=HEAD_UNROLL)

    part = lambda j: pl.BlockSpec((D, T), lambda i: (j, i))
    colv = lambda n: pl.BlockSpec((n, 1), lambda i: (0, 0))
    tmaj = lambda r: pl.BlockSpec((H, None, r, T), lambda i: (0, i // per, 0, i % per))
    norm = pl.BlockSpec((H, T, HEAD_ROWS), lambda i: (0, i, 0))
    tshape = lambda r: jax.ShapeDtypeStruct((H, S // tq, r, tq), BF16)
    nshape = jax.ShapeDtypeStruct((H, S, HEAD_ROWS), BF16)
    return pl.pallas_call(
        body, name="fox_prep", grid=(S // T,),
        in_specs=[part(0), part(1), part(2), pl.BlockSpec((LANES, T), lambda i: (3 * D // LANES, i)),
                  colv(LANES), colv(D), colv(D), pl.BlockSpec((LANES, LANES), lambda i: (0, 0))],
        out_specs=[tmaj(HEAD_ROWS), tmaj(HEAD_ROWS), tmaj(HEAD_ROWS), norm],
        out_shape=[tshape(HEAD_ROWS), tshape(HEAD_ROWS), tshape(HEAD_ROWS), nshape],
        scratch_shapes=[pltpu.VMEM((LANES, T), F32), pltpu.VMEM((LANES, 1), F32)],
        compiler_params=_params(("arbitrary",)),
    )(ut, ut, ut, ut, b_f, qg, kg, _tri_matrix(False))


def _fox_bwd_prep(dot, ot, lse, qat, S, D, tq, plan=None):
    H = D // HEAD_DIM
    T = min(tq, PREP_LANES)
    per = tq // T

    def body(do_ref, o_ref, lse_ref, qat_ref, doat_ref, doa_ref, qat1_ref, qa1_ref):
        row = lax.broadcasted_iota(jnp.int32, (HEAD_ROWS, T), 0)

        def head(h, _):
            rows = _head_rows(h)
            do = do_ref[rows, :].astype(F32)
            delta = jnp.sum(do * o_ref[rows, :], axis=0, keepdims=True)
            db = _head_block(do, list(_split3(-delta)), T)
            doat_ref[h] = db.astype(BF16)
            doa_ref[h] = db.T.astype(BF16)
            qb = qat_ref[h].astype(F32)
            for i, e in enumerate(_split3(-lse_ref[h])):
                qb = jnp.where(row == LSE_ROW + i, e, qb)
            qat1_ref[h] = qb.astype(BF16)
            qa1_ref[h] = qb.T.astype(BF16)
            return 0

        lax.fori_loop(0, H, head, 0, unroll=HEAD_UNROLL)

    chan = pl.BlockSpec((D, T), lambda i: (0, i))
    tmaj = pl.BlockSpec((H, None, HEAD_ROWS, T), lambda i: (0, i // per, 0, i % per))
    norm = pl.BlockSpec((H, T, HEAD_ROWS), lambda i: (0, i, 0))
    tshape = jax.ShapeDtypeStruct((H, S // tq, HEAD_ROWS, tq), BF16)
    nshape = jax.ShapeDtypeStruct((H, S, HEAD_ROWS), BF16)
    return _hosted_call(body, "fox_bwd_prep", (S // T,), [chan, chan, pl.BlockSpec((H, 1, T), lambda i: (0, 0, i)), tmaj],
                        [tmaj, norm, tmaj, norm], [tshape, nshape, tshape, nshape], [], (dot, ot, lse, qat),
                        ("arbitrary",), plan)


def _causal(s, k_axis):
    t = min(s.shape)
    ki = lax.broadcasted_iota(jnp.int32, s.shape, k_axis) - (s.shape[k_axis] - t)
    qi = lax.broadcasted_iota(jnp.int32, s.shape, 1 - k_axis)
    return jnp.where(ki <= qi, s, NEG_INF)


def _seq_tile(i, t):
    return pl.ds(pl.multiple_of(i * t, t), t)


def _attn_forward(ka, qat, vat, S, D, tq, plan=None):
    H = D // HEAD_DIM
    nq = S // tq
    G = 4

    def body(ka_ref, qat_ref, vat_ref, o_ref, o32_ref, lse_ref, m_scr, acc_scr):
        qi = pl.program_id(1)
        m_scr[...] = jnp.full_like(m_scr, NEG_INF)
        acc_scr[...] = jnp.zeros_like(acc_scr)

        def span(k0, n, diagonal):
            keys = pl.ds(pl.multiple_of(k0 * tq, tq), n * tq)
            s = [jnp.dot(ka_ref[g, keys, :], qat_ref[g], preferred_element_type=F32) for g in range(G)]
            if diagonal:
                s = [_causal(sg, 0) for sg in s]
            m_prev = [m_scr[g] for g in range(G)]
            m_new = [jnp.maximum(m_prev[g], jnp.max(s[g], axis=0, keepdims=True)) for g in range(G)]
            p = [jnp.exp(s[g] - m_new[g]).astype(BF16) for g in range(G)]
            for g in range(G):
                upd = jnp.dot(vat_ref[g, k0], p[g][0:tq], preferred_element_type=F32)
                for i in range(1, n):
                    upd = upd + jnp.dot(vat_ref[g, k0 + i], p[g][i * tq:(i + 1) * tq], preferred_element_type=F32)
                acc_scr[g] = jnp.exp(m_prev[g] - m_new[g]) * acc_scr[g] + upd
                m_scr[g] = m_new[g]

        def off_diagonal_pair(j, _):
            span(2 * j, 2, False)
            return 0

        lax.fori_loop(0, qi // 2, off_diagonal_pair, 0)
        pl.when(qi % 2 == 1)(lambda: span(qi - 1, 2, True))
        pl.when(qi % 2 == 0)(lambda: span(qi, 1, True))
        for g in range(G):
            l = acc_scr[g, ONES_ROW_V:ONES_ROW_V + 1, :]
            o = acc_scr[g, 0:HEAD_DIM, :] / l
            o_ref[g * HEAD_DIM:(g + 1) * HEAD_DIM, :] = o.astype(BF16)
            o32_ref[g * HEAD_DIM:(g + 1) * HEAD_DIM, :] = o
            lse_ref[g] = m_scr[g] + jnp.log(l)

    chan = pl.BlockSpec((G * HEAD_DIM, tq), lambda h, i: (h, i))
    stat = pl.BlockSpec((G, 1, tq), lambda h, i: (h, 0, i))
    return _hosted_call(
        body, "attn_forward", (H // G, nq),
        [pl.BlockSpec((G, S, HEAD_ROWS), lambda h, i: (h, 0, 0)),
         pl.BlockSpec((G, None, HEAD_ROWS, tq), lambda h, i: (h, i, 0, 0)),
         pl.BlockSpec((G, nq, HEAD_ROWS, tq), lambda h, i: (h, 0, 0, 0))],
        [chan, chan, stat],
        [jax.ShapeDtypeStruct((D, S), BF16), jax.ShapeDtypeStruct((D, S), F32), jax.ShapeDtypeStruct((H, 1, S), F32)],
        [pltpu.VMEM((G, 1, tq), F32), pltpu.VMEM((G, HEAD_ROWS, tq), F32)],
        (ka, qat, vat), ("arbitrary", "arbitrary"), plan)


def _attn_backward(qa, doa, qat, doat, ka, kat, vat, S, D, tq, plan=None):
    H = D // HEAD_DIM
    nq = S // tq
    G = 2

    def body(qa_ref, doa_ref, qat_ref, doat_ref, ka_ref, kat_ref, vat_ref, dq_ref, dk_ref, dv_ref, dk_scr, dv_scr):
        ki = pl.program_id(1)

        @pl.when(ki == 0)
        def _():
            dq_ref[...] = jnp.zeros_like(dq_ref)

        dk_scr[...] = jnp.zeros_like(dk_scr)
        dv_scr[...] = jnp.zeros_like(dv_scr)

        def span(q0, n, diagonal):
            rows = pl.ds(pl.multiple_of(q0 * tq, tq), n * tq)
            s = [jnp.dot(qa_ref[g, rows, :], kat_ref[g], preferred_element_type=F32) for g in range(G)]
            if diagonal:
                s = [_causal(sg, 1) for sg in s]
            p = [jnp.exp(sg) for sg in s]
            ds = [(p[g] * jnp.dot(doa_ref[g, rows, :], vat_ref[g], preferred_element_type=F32)).astype(BF16)
                  for g in range(G)]
            p = [pg.astype(BF16) for pg in p]
            for g in range(G):
                for i in range(n):
                    part = slice(i * tq, (i + 1) * tq)
                    dv_scr[g] += jnp.dot(doat_ref[g, q0 + i, 0:HEAD_DIM, :], p[g][part], preferred_element_type=F32)
                    dk_scr[g] += jnp.dot(qat_ref[g, q0 + i], ds[g][part], preferred_element_type=F32)
                dq_ref[g, rows, :] += jnp.dot(ds[g], ka_ref[g], preferred_element_type=F32)

        n_off = nq - 1 - ki
        odd = n_off % 2

        def off_diagonal_pair(j, _):
            span(ki + 1 + odd + 2 * j, 2, False)
            return 0

        pl.when(odd == 1)(lambda: span(ki, 2, True))
        pl.when(odd == 0)(lambda: span(ki, 1, True))
        lax.fori_loop(0, n_off // 2, off_diagonal_pair, 0)
        dk_ref[...] = dk_scr[...]
        for g in range(G):
            dv_ref[g * HEAD_DIM:(g + 1) * HEAD_DIM, :] = dv_scr[g].astype(BF16)

    whole = pl.BlockSpec((G, S, HEAD_ROWS), lambda h, i: (h, 0, 0))
    tiles = pl.BlockSpec((G, nq, HEAD_ROWS, tq), lambda h, i: (h, 0, 0, 0))
    one = pl.BlockSpec((G, None, HEAD_ROWS, tq), lambda h, i: (h, i, 0, 0))
    return _hosted_call(
        body, "attn_backward", (H // G, nq),
        [whole, whole, tiles, tiles, pl.BlockSpec((G, tq, HEAD_ROWS), lambda h, i: (h, i, 0)), one, one],
        [whole, pl.BlockSpec((G, HEAD_ROWS, tq), lambda h, i: (h, 0, i)),
         pl.BlockSpec((G * HEAD_DIM, tq), lambda h, i: (h, i))],
        [jax.ShapeDtypeStruct((H, S, HEAD_ROWS), F32), jax.ShapeDtypeStruct((H, HEAD_ROWS, S), F32),
         jax.ShapeDtypeStruct((D, S), BF16)],
        [pltpu.VMEM((G, HEAD_ROWS, tq), F32), pltpu.VMEM((G, HEAD_DIM, tq), F32)],
        (qa, doa, qat, doat, ka, kat, vat), ("arbitrary", "arbitrary"), plan)


def _fox_prep_bwd(ut, dq, dkt, dvt, b_f, qg, kg, S, D, tq):
    H = D // HEAD_DIM
    T = min(tq, PREP_LANES)
    nT = S // T
    NU = 3 * D + LANES
    scale = HEAD_DIM ** -0.5

    def body(q_ref, k_ref, f_ref, dq_ref, dk_ref, dv_ref, bf_ref, qg_ref, kg_ref, tri_ref,
             du_ref, dbf_ref, dqg_ref, dkg_ref, gq_acc, gk_acc, fcar, dc_scr):
        step = pl.program_id(0)

        @pl.when(step == 0)
        def _():
            for ref in (gq_acc, gk_acc, fcar, dbf_ref):
                ref[...] = jnp.zeros_like(ref)

        dc_scr[...] = jnp.zeros_like(dc_scr)

        def head(h, _):
            rows = _head_rows(h)
            dqb = dq_ref[h].T
            dkb = dk_ref[h]
            dc_scr[pl.ds(h, 1), :] = dqb[ONES_COL_K:ONES_COL_K + 1, :] - dkb[ONES_ROW_Q:ONES_ROW_Q + 1, :]
            for src, dsrc, gain, acc, mul, base in ((q_ref, dqb, qg_ref, gq_acc, scale, 0),
                                                    (k_ref, dkb, kg_ref, gk_acc, 1.0, D)):
                x = src[rows, :]
                rs = lax.rsqrt(jnp.mean(x * x, axis=0, keepdims=True) + EPS)
                xhat = x * rs
                dn = dsrc[0:HEAD_DIM, :] * mul
                acc[rows, :] += jnp.sum(dn * xhat, axis=1, keepdims=True)
                dxh = dn * gain[rows, :]
                dx = rs * (dxh - xhat * jnp.mean(dxh * xhat, axis=0, keepdims=True))
                du_ref[pl.ds(pl.multiple_of(base + h * HEAD_DIM, HEAD_DIM), HEAD_DIM), :] = dx.astype(BF16)
            return 0

        lax.fori_loop(0, H, head, 0, unroll=HEAD_UNROLL)
        du_ref[2 * D:3 * D, :] = dv_ref[...]
        dlf, carry = _lane_cumsum(dc_scr[...], tri_ref, fcar[...], True)
        fcar[...] = carry
        dfl = dlf * _sigmoid(-(f_ref[...] + bf_ref[...]))
        dbf_ref[...] += jnp.sum(dfl, axis=1, keepdims=True)
        du_ref[3 * D:NU, :] = dfl.astype(BF16)

        @pl.when(step == nT - 1)
        def _():
            for acc, ref in ((gq_acc, dqg_ref), (gk_acc, dkg_ref)):
                tot = jnp.zeros((HEAD_DIM, 1), F32)
                for h in range(H):
                    tot = tot + acc[h * HEAD_DIM:(h + 1) * HEAD_DIM, :]
                ref[...] = tot

    rev = lambda i: nT - 1 - i
    part = lambda j: pl.BlockSpec((D, T), lambda i: (j, rev(i)))
    colv = lambda n: pl.BlockSpec((n, 1), lambda i: (0, 0))
    return pl.pallas_call(
        body, name="fox_prep_bwd", grid=(nT,),
        in_specs=[part(0), part(1), pl.BlockSpec((LANES, T), lambda i: (3 * D // LANES, rev(i))),
                  pl.BlockSpec((H, T, HEAD_ROWS), lambda i: (0, rev(i), 0)),
                  pl.BlockSpec((H, HEAD_ROWS, T), lambda i: (0, 0, rev(i))), pl.BlockSpec((D, T), lambda i: (0, rev(i))),
                  colv(LANES), colv(D), colv(D), pl.BlockSpec((LANES, LANES), lambda i: (0, 0))],
        out_specs=[pl.BlockSpec((NU, T), lambda i: (0, rev(i))), colv(LANES), colv(HEAD_DIM), colv(HEAD_DIM)],
        out_shape=[jax.ShapeDtypeStruct((NU, S), BF16), jax.ShapeDtypeStruct((LANES, 1), F32),
                   jax.ShapeDtypeStruct((HEAD_DIM, 1), F32), jax.ShapeDtypeStruct((HEAD_DIM, 1), F32)],
        scratch_shapes=[pltpu.VMEM((D, 1), F32), pltpu.VMEM((D, 1), F32), pltpu.VMEM((LANES, 1), F32),
                        pltpu.VMEM((LANES, T), F32)],
        compiler_params=_params(("arbitrary",)),
    )(ut, ut, ut, dq, dkt, dvt, b_f, qg, kg, _tri_matrix(True))


def _block_diag_tiles(w):
    n = w.shape[0]
    per = min(MXU_DIM, n * LRU_BLOCK_DIM) // LRU_BLOCK_DIM
    eye = jnp.eye(per, dtype=w.dtype)
    w5 = w.reshape(n // per, per, LRU_BLOCK_DIM, 1, LRU_BLOCK_DIM) * eye[None, :, None, :, None]
    return w5.reshape(n // per, per * LRU_BLOCK_DIM, per * LRU_BLOCK_DIM).astype(BF16)


def _block_diag_extract(t, n):
    per = t.shape[-1] // LRU_BLOCK_DIM
    eye = jnp.eye(per, dtype=t.dtype)
    t5 = t.reshape(n // per, per, LRU_BLOCK_DIM, per, LRU_BLOCK_DIM) * eye[None, :, None, :, None]
    return t5.sum(axis=3).reshape(n, LRU_BLOCK_DIM, LRU_BLOCK_DIM)


def _local_step(x, tgt, small, wv, grad_view, comm=None):
    S, D = x.shape
    F = 4 * D
    H = D // HEAD_DIM
    nblk = D // LRU_BLOCK_DIM
    NU = 3 * D + LANES
    tq = max(LANES, min(512, S // 4))
    assert S % tq == 0
    vec = lambda a: a.reshape(1, -1).astype(F32)
    col = lambda a: a.reshape(-1, 1).astype(F32)
    mix_g, mlp_g = small["mix_norm"], small["mlp_norm"]
    conv_b = vec(small["lru_conv_b"])
    wr_bd, wi_bd = _block_diag_tiles(small["lru_w_r"][0]), _block_diag_tiles(small["lru_w_i"][0])
    b_r, b_i, lam = vec(small["lru_b_r"]), vec(small["lru_b_i"]), vec(small["lru_lambda"])
    b_f = jnp.pad(col(small["fox_b_f"]), ((0, LANES - H), (0, 0)))
    qg, kg = jnp.tile(col(small["fox_q_gain"]), (H, 1)), jnp.tile(col(small["fox_k_gain"]), (H, 1))
    X = lambda a: _View(a)
    grads = {}
    gout = functools.partial(grad_view, grads)

    def hosted(name, fn, *args):
        plan = comm.before(name, grads) if comm is not None else None
        res, side = fn(*args, plan=plan)
        if plan is not None:
            comm.after(name, side, wv)
        return res

    def hosted_mm(name, *args, **kw):
        plan = comm.before(name, grads) if comm is not None else None
        if plan is None:
            return _matmul(name, *args, **kw)
        res, side = _matmul(name, *args, plan=plan, **kw)
        comm.after(name, side, wv)
        return res

    two = lambda: [_fresh(S, D, F32), _fresh(S, D, BF16)]

    def mlp_up(l, hm):
        return hosted_mm(f"mlp{l}_up", X(hm), wv[f"w1_{l}"], S, F, D, outs=[_fresh(S, F, BF16)], epilogue=_ep_relu2)[0]

    def mlp_bwd(l, xin, hm, act, d, db):
        (dz,) = hosted_mm(f"mlp{l}_dact", X(db), wv[f"w2_{l}"], S, F, D, tb=True, outs=[_fresh(S, F, BF16)],
                          epilogue=_ep_drelu2, extras=[X(act)])
        (grads[f"w2_{l}"],) = _matmul(f"mlp{l}_dw2", X(act), X(db), F, D, S, ta=True, outs=[gout(f"w2_{l}")],
                                      epilogue=_ep_store)
        (grads[f"w1_{l}"],) = _matmul(f"mlp{l}_dw1", X(hm), X(dz), D, F, S, ta=True, outs=[gout(f"w1_{l}")],
                                      epilogue=_ep_store)
        return _matmul(f"mlp{l}_dhm", X(dz), wv[f"w1_{l}"], S, D, F, tb=True, outs=two(), n_sums=1,
                       epilogue=_ep_norm_bwd, extras=[X(xin), X(d)], vecs=[mlp_g[l:l + 1]])

    (h0,) = hosted("mix0_norm", _rms_fwd, "mix0_norm", x, mix_g[0:1], S, D)
    (u0,) = hosted_mm("lru_in", X(h0), wv["lru_in"], S, 2 * D, D, outs=[_fresh(S, 2 * D, F32)], epilogue=_ep_store)
    conv_w = small["conv_w"]
    y, xc, r, ig, hs = hosted("lru_fwd", _lru_fwd, u0, conv_w, conv_b, wr_bd, b_r, wi_bd, b_i, lam, S, D)
    x1, hm0 = _matmul("lru_out", X(y), wv["lru_out"], S, D, D, outs=two(), epilogue=_ep_resid_norm, extras=[X(x)],
                      vecs=[mlp_g[0:1]])
    act0 = mlp_up(0, hm0)
    x2, h1 = hosted_mm("mlp0_down", X(act0), wv["w2_0"], S, D, F, outs=two(), epilogue=_ep_resid_norm, extras=[X(x1)],
                       vecs=[mix_g[1:2]])
    (u1,) = _matmul("fox_in", wv["fox_in"], X(h1), NU, S, D, tb=True, outs=[_fresh(NU, S, F32)], epilogue=_ep_store)
    qat, kat, vat, ka = _fox_prep(u1, b_f, qg, kg, S, D, tq)
    o, o32, lse = hosted("attn_forward", _attn_forward, ka, qat, vat, S, D, tq)
    x3, hm1 = _matmul("fox_out", X(o), wv["fox_out"], S, D, D, ta=True, outs=two(), epilogue=_ep_resid_norm,
                      extras=[X(x2)], vecs=[mlp_g[1:2]])
    act1 = mlp_up(1, hm1)
    (x4,) = _matmul("mlp1_down", X(act1), wv["w2_1"], S, D, F, outs=[_fresh(S, D, F32)], epilogue=_ep_resid,
                    extras=[X(x3)])
    loss, d4, d4b = _loss_head(x4, tgt, S, D)

    d3, d3b, dg_mlp1 = mlp_bwd(1, x3, hm1, act1, d4, d4b)
    (do,) = _matmul("fox_dout", wv["fox_out"], X(d3b), D, S, D, tb=True, outs=[_fresh(D, S, BF16)], epilogue=_ep_store)
    (grads["fox_out"],) = _matmul("fox_dwout", X(o), X(d3b), D, D, S, outs=[gout("fox_out")], epilogue=_ep_store)
    doat, doa, qat1, qa1 = hosted("fox_bwd_prep", _fox_bwd_prep, do, o32, lse, qat, S, D, tq)
    dqn, dkn, dv = hosted("attn_backward", _attn_backward, qa1, doa, qat1, doat, ka, kat, vat, S, D, tq)
    du1, dbf, dqg, dkg = _fox_prep_bwd(u1, dqn, dkn, dv, b_f, qg, kg, S, D, tq)
    (grads["fox_in"],) = _matmul("fox_dwin", X(du1), X(h1), NU, D, S, outs=[gout("fox_in")], epilogue=_ep_store)
    d2, d2b, dg_mix1 = hosted_mm("fox_dh", X(du1), wv["fox_in"], S, D, NU, ta=True, outs=two(), n_sums=1,
                               epilogue=_ep_norm_bwd, extras=[X(x2), X(d3)], vecs=[mix_g[1:2]])
    d1, d1b, dg_mlp0 = mlp_bwd(0, x1, hm0, act0, d2, d2b)
    (grads["lru_out"],) = _matmul("lru_dwout", X(y), X(d1b), D, D, S, ta=True, outs=[gout("lru_out")],
                                  epilogue=_ep_store)
    (dy,) = hosted_mm("lru_dout", X(d1b), wv["lru_out"], S, D, D, tb=True, outs=[_fresh(S, D, F32)],
                      epilogue=_ep_store)
    du0, dcw, dcb, dlam, dbr, dbi, dwr, dwi = hosted("lru_bwd", _lru_bwd, dy, u0, xc, r, ig, hs, conv_w, wr_bd, wi_bd,
                                                     lam, S, D)
    (grads["lru_in"],) = _matmul("lru_dwin", X(h0), X(du0), D, 2 * D, S, ta=True, outs=[gout("lru_in")],
                                 epilogue=_ep_store)
    gx, dg_mix0 = hosted_mm("lru_dh", X(du0), wv["lru_in"], S, D, 2 * D, tb=True, outs=[_fresh(S, D, F32)], n_sums=1,
                            epilogue=lambda *a: _ep_norm_bwd(*a)[::2], extras=[X(x), X(d1)], vecs=[mix_g[0:1]])

    grads.update(
        mix_norm=jnp.concatenate([dg_mix0, dg_mix1], axis=0), mlp_norm=jnp.concatenate([dg_mlp0, dg_mlp1], axis=0),
        conv_w=dcw, lru_conv_b=dcb, lru_w_r=_block_diag_extract(dwr, nblk)[None], lru_b_r=dbr.reshape(1, nblk, -1),
        lru_w_i=_block_diag_extract(dwi, nblk)[None], lru_b_i=dbi.reshape(1, nblk, -1), lru_lambda=dlam,
        fox_b_f=dbf[:H].reshape(1, H), fox_q_gain=dqg.reshape(1, -1), fox_k_gain=dkg.reshape(1, -1))
    return loss, gx, grads


def _place():
    x, y, c = lax.axis_index("x"), lax.axis_index("y"), lax.axis_index("c")
    chips = [(1 - x, y), (x, 1 - y), (1 - x, 1 - y)]
    return x, y, c, 2 * x + y, chips


BOUNCE_BYTES = 1 << 20


def _bounce_shape(rows, cols, dtype):
    chunk = rows
    while chunk % 2 == 0 and chunk > 16 and chunk * cols * jnp.dtype(dtype).itemsize > BOUNCE_BYTES:
        chunk //= 2
    return pltpu.VMEM((2, chunk, cols), dtype)


def _bounce_copy(src, dst, buf, sem):
    chunk = buf.shape[1]
    n = src.shape[0] // chunk
    cin = lambda i: pltpu.make_async_copy(src.at[pl.ds(i * chunk, chunk)], buf.at[i % 2], sem.at[i % 2])
    cout = lambda i: pltpu.make_async_copy(buf.at[i % 2], dst.at[pl.ds(i * chunk, chunk)], sem.at[2 + i % 2])
    cin(0).start()
    for i in range(n):
        cin(i).wait()
        if i + 1 < n:
            if i >= 1:
                cout(i - 1).wait()
            cin(i + 1).start()
        cout(i).start()
    if n >= 2:
        cout(n - 2).wait()
    cout(n - 1).wait()


def _hbm_call(body, name, arrays, out_shape, n_dma_sems, bounce=()):
    scratch = [pltpu.SemaphoreType.DMA((k,)) for k in n_dma_sems]
    for rows, cols, dtype in bounce:
        scratch += [_bounce_shape(rows, cols, dtype), pltpu.SemaphoreType.DMA((4,))]
    return pl.pallas_call(
        body, name=name, in_specs=[ANY] * len(arrays), out_specs=[ANY] * len(out_shape), out_shape=out_shape,
        scratch_shapes=scratch,
        compiler_params=pltpu.CompilerParams(has_side_effects=True, vmem_limit_bytes=VMEM_LIMIT),
    )(*arrays)


class _Gather:
    def __init__(self, shards):
        n = self.n = len(shards)
        self.operands = list(shards)
        self.out_shape = [jax.ShapeDtypeStruct((N_CHIPS,) + tuple(a.shape), a.dtype) for a in shards]
        self.scratch = [pltpu.SemaphoreType.DMA((3 * n,)) for _ in range(4)]
        for a in shards:
            self.scratch += [_bounce_shape(a.shape[0], a.shape[1], a.dtype), pltpu.SemaphoreType.DMA((4,))]

    def _copies(self, ins, outs, scr):
        send, recv, fsend, frecv = scr[:4]
        x, y, c, s, chips = _place()

        def rows(a, chip_idx, which):
            hr = ins[a].shape[0] // 2
            return outs[a].at[chip_idx, pl.ds(which * hr, hr)]

        def landed(a, j, core):
            return rows(a, 2 * chips[j][0] + chips[j][1], core)

        def ici(a, j, mine):
            hr = ins[a].shape[0] // 2
            src, dst = (ins[a].at[pl.ds(c * hr, hr)], rows(a, s, c)) if mine else (landed(a, j, c),) * 2
            return pltpu.make_async_remote_copy(src_ref=src, dst_ref=dst, send_sem=send.at[3 * a + j],
                                                recv_sem=recv.at[3 * a + j], device_id=(*chips[j], c),
                                                device_id_type=MESH)

        def d2d(a, j, mine):
            ref = landed(a, j, c if mine else 1 - c)
            return pltpu.make_async_remote_copy(src_ref=ref, dst_ref=ref, send_sem=fsend.at[3 * a + j],
                                                recv_sem=frecv.at[3 * a + j], device_id=(x, y, 1 - c),
                                                device_id_type=MESH)

        return ici, d2d, s

    def start(self, ins, outs, scr):
        ici, _, _ = self._copies(ins, outs, scr)
        for a in range(self.n):
            for j in range(3):
                ici(a, j, True).start()

    def middle(self, ins, outs, scr):
        ici, d2d, s = self._copies(ins, outs, scr)
        for a in range(self.n):
            _bounce_copy(ins[a], outs[a].at[s], scr[4 + 2 * a], scr[5 + 2 * a])
        for a in range(self.n):
            for j in range(3):
                ici(a, j, False).wait_recv()
                d2d(a, j, True).start()

    def finish(self, ins, outs, scr):
        ici, d2d, _ = self._copies(ins, outs, scr)
        for a in range(self.n):
            for j in range(3):
                d2d(a, j, False).wait_recv()
        for a in range(self.n):
            for j in range(3):
                ici(a, j, True).wait_send()
                d2d(a, j, True).wait_send()


def _run_plan(name, plan):
    k_in, k_out = len(plan.operands), len(plan.out_shape)

    def body(*refs):
        parts = (refs[:k_in], refs[k_in:k_in + k_out], refs[k_in + k_out:])
        plan.start(*parts)
        plan.middle(*parts)
        plan.finish(*parts)

    return pl.pallas_call(
        body, name=name, in_specs=[ANY] * k_in, out_specs=[ANY] * k_out, out_shape=plan.out_shape,
        scratch_shapes=plan.scratch,
        compiler_params=pltpu.CompilerParams(has_side_effects=True, vmem_limit_bytes=VMEM_LIMIT),
    )(*plan.operands)


def _hosted_call(body, name, grid, in_specs, out_specs, out_shape, scratch_shapes, operands, sem, plan=None):
    if plan is None:
        res = pl.pallas_call(body, name=name, grid=grid, in_specs=in_specs, out_specs=out_specs, out_shape=out_shape,
                             scratch_shapes=scratch_shapes, compiler_params=_params(sem))(*operands)
        return res, None
    n_in, n_out, n_scr = len(in_specs), len(out_specs), len(scratch_shapes)
    k_in, k_out = len(plan.operands), len(plan.out_shape)
    total = int(np.prod(grid))
    late = max(0, total - 1 - max(1, total // 8))

    def hosted(*refs):
        ins, refs = refs[:n_in], refs[n_in:]
        p_ins, refs = refs[:k_in], refs[k_in:]
        outs, refs = refs[:n_out], refs[n_out:]
        p_outs, refs = refs[:k_out], refs[k_out:]
        scr, p_scr = refs[:n_scr], refs[n_scr:]
        step = pl.program_id(0)
        for d in range(1, len(grid)):
            step = step * grid[d] + pl.program_id(d)
        pl.when(step == 0)(lambda: plan.start(p_ins, p_outs, p_scr))
        body(*ins, *outs, *scr)
        pl.when(step == late)(lambda: plan.middle(p_ins, p_outs, p_scr))
        pl.when(step == total - 1)(lambda: plan.finish(p_ins, p_outs, p_scr))

    res = pl.pallas_call(
        hosted, name=name, grid=grid, in_specs=list(in_specs) + [ANY] * k_in, out_specs=list(out_specs) + [ANY] * k_out,
        out_shape=list(out_shape) + plan.out_shape, scratch_shapes=list(scratch_shapes) + plan.scratch,
        compiler_params=pltpu.CompilerParams(dimension_semantics=sem, vmem_limit_bytes=VMEM_LIMIT,
                                             has_side_effects=True),
    )(*operands, *plan.operands)
    return res[:n_out], res[n_out:]


def _all_gather(name, shards):
    return _run_plan(name, _Gather(shards))


class _Swap:
    def __init__(self, arrs):
        self.n = len(arrs)
        self.operands = list(arrs)
        self.out_shape = [jax.ShapeDtypeStruct((a.shape[0], a.shape[1] // 2, a.shape[2]), a.dtype) for a in arrs]
        self.scratch = [pltpu.SemaphoreType.DMA((self.n,)) for _ in range(2)]

    def _copy(self, ins, outs, scr, a):
        x, y, c, _, _ = _place()
        hr = ins[a].shape[1] // 2
        return pltpu.make_async_remote_copy(
            src_ref=ins[a].at[:, pl.ds((1 - c) * hr, hr)], dst_ref=outs[a], send_sem=scr[0].at[a],
            recv_sem=scr[1].at[a], device_id=(x, y, 1 - c), device_id_type=MESH)

    def start(self, ins, outs, scr):
        for a in range(self.n):
            self._copy(ins, outs, scr, a).start()

    def middle(self, ins, outs, scr):
        pass

    def finish(self, ins, outs, scr):
        for a in range(self.n):
            self._copy(ins, outs, scr, a).wait()


class _Scatter:
    def __init__(self, parts):
        n = self.n = len(parts)
        self.operands = list(parts)
        self.out_shape = [jax.ShapeDtypeStruct(a.shape, a.dtype) for a in parts]
        self.scratch = [pltpu.SemaphoreType.DMA((3 * n,)) for _ in range(2)]
        for a in parts:
            self.scratch += [_bounce_shape(a.shape[1], a.shape[2], a.dtype), pltpu.SemaphoreType.DMA((4,))]

    def _copy(self, ins, outs, scr, a, j, mine):
        x, y, c, s, chips = _place()
        t = 2 * chips[j][0] + chips[j][1]
        return pltpu.make_async_remote_copy(
            src_ref=ins[a].at[t], dst_ref=outs[a].at[s if mine else t], send_sem=scr[0].at[3 * a + j],
            recv_sem=scr[1].at[3 * a + j], device_id=(*chips[j], c), device_id_type=MESH)

    def start(self, ins, outs, scr):
        for a in range(self.n):
            for j in range(3):
                self._copy(ins, outs, scr, a, j, True).start()

    def middle(self, ins, outs, scr):
        s = _place()[3]
        for a in range(self.n):
            _bounce_copy(ins[a].at[s], outs[a].at[s], scr[2 + 2 * a], scr[3 + 2 * a])

    def finish(self, ins, outs, scr):
        for a in range(self.n):
            for j in range(3):
                self._copy(ins, outs, scr, a, j, False).wait_recv()
        for a in range(self.n):
            for j in range(3):
                self._copy(ins, outs, scr, a, j, True).wait_send()


def _pair_gather(name, halves):
    n = len(halves)

    def body(*refs):
        ins, outs = refs[:n], refs[n:2 * n]
        send, recv = refs[2 * n:2 * n + 2]
        stage = refs[2 * n + 2:]
        x, y, c, _, _ = _place()
        cps = []
        for a in range(n):
            hr = ins[a].shape[0]
            cp = pltpu.make_async_remote_copy(
                src_ref=ins[a], dst_ref=outs[a].at[pl.ds(c * hr, hr)], send_sem=send.at[a], recv_sem=recv.at[a],
                device_id=(x, y, 1 - c), device_id_type=MESH)
            cp.start()
            cps.append((cp, hr))
        for a, (cp, hr) in enumerate(cps):
            _bounce_copy(ins[a], outs[a].at[pl.ds(c * hr, hr)], stage[2 * a], stage[2 * a + 1])
        for a, (cp, hr) in enumerate(cps):
            cp.wait_send()
            theirs = outs[a].at[pl.ds((1 - c) * hr, hr)]
            pltpu.make_async_remote_copy(src_ref=theirs, dst_ref=theirs, send_sem=send.at[a], recv_sem=recv.at[a],
                                         device_id=(x, y, 1 - c), device_id_type=MESH).wait_recv()

    out_shape = [jax.ShapeDtypeStruct((2 * a.shape[0], a.shape[1]), a.dtype) for a in halves]
    return _hbm_call(body, name, halves, out_shape, (n, n),
                     bounce=[(a.shape[0], a.shape[1], a.dtype) for a in halves])


def _row_tile(rows, cols, itemsize, n_bufs):
    budget = VMEM_LIMIT // 2
    for t in range(min(rows, 1024) // 16 * 16, 0, -16):
        if rows % t == 0 and 2 * n_bufs * t * cols * itemsize <= budget:
            return t
    return rows


def _pair_add(name, g, gsib, core, out_dtype):
    _, r, cols = g.shape
    hr = r // 2
    t = _row_tile(hr, cols, 4, 3)
    per = hr // t

    def body(core_ref, a_ref, b_ref, o_ref):
        o_ref[...] = (a_ref[...].astype(F32) + b_ref[...].astype(F32)).astype(o_ref.dtype)

    grid_spec = pltpu.PrefetchScalarGridSpec(
        num_scalar_prefetch=1, grid=(N_CHIPS, per),
        in_specs=[pl.BlockSpec((None, t, cols), lambda s, i, core: (s, core[0] * per + i, 0)),
                  pl.BlockSpec((None, t, cols), lambda s, i, core: (s, i, 0))],
        out_specs=pl.BlockSpec((None, t, cols), lambda s, i, core: (s, i, 0)))
    return pl.pallas_call(body, name=name, grid_spec=grid_spec,
                          out_shape=jax.ShapeDtypeStruct((N_CHIPS, hr, cols), out_dtype),
                          compiler_params=_params(("arbitrary", "arbitrary")))(core, g, gsib)


def _chip_sum(name, parts):
    _, hr, cols = parts.shape
    t = _row_tile(hr, cols, 4, 5)

    def body(p_ref, o_ref):
        o_ref[...] = ((p_ref[0].astype(F32) + p_ref[1].astype(F32)) + p_ref[2].astype(F32)) + p_ref[3].astype(F32)

    return pl.pallas_call(
        body, name=name, grid=(hr // t,), in_specs=[pl.BlockSpec((N_CHIPS, t, cols), lambda i: (0, i, 0))],
        out_specs=pl.BlockSpec((t, cols), lambda i: (i, 0)), out_shape=jax.ShapeDtypeStruct((hr, cols), F32),
        compiler_params=_params(("arbitrary",)))(parts)


def _pair_partials(tag, arrs, sib, wire_dtypes, core):
    return _Scatter([_pair_add(f"{tag}_pair_add{i}", g, gs, core, dt)
                     for i, (g, gs, dt) in enumerate(zip(arrs, sib, wire_dtypes))])


def _finish_reduce(tag, scattered):
    halves = [_chip_sum(f"{tag}_chip_sum{i}", p) for i, p in enumerate(scattered)]
    return _pair_gather(f"{tag}_pair_gather", halves)


def _adamw(name, w, g_parts, m, v):
    thin = w.ndim == 3
    rows, cols = w.shape[0], w.shape[-1]
    n_parts = len(g_parts)
    part_rows = rows // n_parts
    t = max(d for d in range(1, 257) if part_rows % d == 0) if thin else _row_tile(part_rows, cols, 4, 7 + n_parts)
    per = part_rows // t
    c1 = 1.0 - ADAM_B1 ** ADAM_STEP
    c2 = 1.0 - ADAM_B2 ** ADAM_STEP

    def body(w_ref, m_ref, v_ref, *refs):
        g_refs, (go_ref, d_ref, nm_ref, nv_ref) = refs[:n_parts], refs[n_parts:]
        g = g_refs[0][...]
        for k in range(1, n_parts):
            g = jnp.where(pl.program_id(0) >= k * per, g_refs[k][...], g)
        go_ref[...] = g
        m = ADAM_B1 * m_ref[...] + (1.0 - ADAM_B1) * g
        v = ADAM_B2 * v_ref[...] + (1.0 - ADAM_B2) * (g * g)
        nm_ref[...] = m
        nv_ref[...] = v
        d_ref[...] = -ADAM_LR * ((m / c1) / (jnp.sqrt(v / c2) + ADAM_EPS) + ADAM_WD * w_ref[...])

    block = (t, 1, cols) if thin else (t, cols)
    at = lambda r: (r, 0, 0) if thin else (r, 0)
    spec = pl.BlockSpec(block, lambda i: at(i))
    g_specs = [pl.BlockSpec(block, lambda i, k=k: at(jnp.clip(i - k * per, 0, per - 1))) for k in range(n_parts)]
    shp = jax.ShapeDtypeStruct(w.shape, F32)
    return pl.pallas_call(body, name=name, grid=(rows // t,), in_specs=[spec] * 3 + g_specs, out_specs=[spec] * 4,
                          out_shape=[shp] * 4, compiler_params=_params(("arbitrary",)))(w, m, v, *g_parts)


_WEIGHTS = ["mix_norm", "mlp_norm", "mlp_w1", "mlp_w2", "lru_w_in", "lru_conv_w", "lru_conv_b", "lru_w_r", "lru_b_r",
            "lru_w_i", "lru_b_i", "lru_lambda", "lru_w_out", "fox_w_in", "fox_b_f", "fox_q_gain", "fox_k_gain",
            "fox_w_out"]
_REPLICATED = ["mix_norm", "mlp_norm", "lru_conv_b", "lru_w_r", "lru_b_r", "lru_w_i", "lru_b_i", "lru_lambda",
               "fox_b_f", "fox_q_gain", "fox_k_gain"]
_PACK_TILE = 2 * SUBLANES * LANES


def _as2d(a):
    return a.reshape(-1, a.shape[-1])


def kernel(x, mix_norm, mlp_norm, mlp_w1, mlp_w2, lru_w_in, lru_conv_w, lru_conv_b, lru_w_r, lru_b_r, lru_w_i, lru_b_i, lru_lambda, lru_w_out, fox_w_in, fox_b_f, fox_q_gain, fox_k_gain, fox_w_out, loss_target, m_mix_norm, m_mlp_norm, m_mlp_w1, m_mlp_w2, m_lru_w_in, m_lru_conv_w, m_lru_conv_b, m_lru_w_r, m_lru_b_r, m_lru_w_i, m_lru_b_i, m_lru_lambda, m_lru_w_out, m_fox_w_in, m_fox_b_f, m_fox_q_gain, m_fox_k_gain, m_fox_w_out, v_mix_norm, v_mlp_norm, v_mlp_w1, v_mlp_w2, v_lru_w_in, v_lru_conv_w, v_lru_conv_b, v_lru_w_r, v_lru_b_r, v_lru_w_i, v_lru_b_i, v_lru_lambda, v_lru_w_out, v_fox_w_in, v_fox_b_f, v_fox_q_gain, v_fox_k_gain, v_fox_w_out):
    args = dict(locals())
    W = {n: args[n] for n in _WEIGHTS}
    Mo = {n: args["m_" + n] for n in _WEIGHTS}
    Vo = {n: args["v_" + n] for n in _WEIGHTS}
    S, D = x.shape[1], x.shape[2]
    F = 4 * D
    H = D // HEAD_DIM
    NU = 3 * D + LANES
    FQ, DQ = F // N_CHIPS, D // N_CHIPS
    nfox = fox_w_in.shape[-1]
    chip = 2 * lax.axis_index("x") + lax.axis_index("y")
    core = lax.axis_index("c").astype(jnp.int32).reshape(1)

    cw_flat = jnp.pad(lru_conv_w.reshape(-1), (0, _PACK_TILE - CONV_WIDTH * DQ)).reshape(2 * SUBLANES, LANES)
    w1s, w2s = mlp_w1.astype(BF16), mlp_w2.astype(BF16)
    wv = {}
    small = {n: W[n] for n in _REPLICATED}
    scattered = {}
    members = {"g1": ["w2_1", "w1_1", "fox_out"], "g2": ["fox_in"], "g3": ["w2_0", "w1_0"], "g4": ["lru_out", "lru_in"]}
    swap_at = {"fox_bwd_prep": "g1", "fox_dh": "g2", "lru_dout": "g3"}
    scatter_at = {"attn_backward": "g1", "mlp0_dact": "g2", "lru_bwd": "g3", "lru_dh": "g4"}
    swapped = {}

    fox_rows = -(-nfox // (4 * SUBLANES)) * (4 * SUBLANES)
    fox_t = jnp.pad(jnp.transpose(fox_w_in[0]).astype(BF16), ((0, fox_rows - nfox), (0, 0)))

    def shard_major(name, g):
        if name == "fox_in":
            return jnp.pad(g[:nfox * N_CHIPS].reshape(N_CHIPS, nfox, D), ((0, 0), (0, fox_rows - nfox), (0, 0)))
        return g

    class Comm:
        @staticmethod
        def before(name, grads):
            if name == "mix0_norm":
                return _Gather([lru_w_in[0].astype(BF16)])
            if name == "lru_in":
                return _Gather([lru_w_out[0].astype(BF16), cw_flat])
            if name == "lru_fwd":
                return _Gather([w1s[0]])
            if name == "mlp0_up":
                return _Gather([w2s[0]])
            if name == "mlp0_down":
                return _Gather([fox_t])
            if name == "attn_forward":
                return _Gather([fox_w_out[0].astype(BF16), w1s[1], w2s[1]])
            if name in swap_at:
                group = swap_at[name]
                swapped[group] = [[shard_major(n, grads[n]) for n in members[group]], None]
                return _Swap(swapped[group][0])
            if name in scatter_at:
                group = scatter_at[name]
                if group not in swapped:
                    arrs = [shard_major(n, grads[n]) for n in members[group]]
                    swapped[group] = [arrs, _run_plan(f"{group}_pair_swap", _Swap(arrs))]
                arrs, sib = swapped[group]
                return _pair_partials(group, arrs, sib, [BF16] * len(arrs), core)
            return None

        @staticmethod
        def after(name, res, wv):
            if name == "mix0_norm":
                wv.update(lru_in=_View(res[0], "cs"))
            elif name == "lru_in":
                wv.update(lru_out=_View(res[0], "rs"))
                taps = res[1].reshape(N_CHIPS, -1)[:, :CONV_WIDTH * DQ].reshape(N_CHIPS, CONV_WIDTH, DQ)
                small["conv_w"] = jnp.transpose(taps, (1, 0, 2)).reshape(CONV_WIDTH, D)
            elif name == "lru_fwd":
                wv.update(w1_0=_View(res[0], "cs"))
            elif name == "mlp0_up":
                wv.update(w2_0=_View(res[0], "rs"))
            elif name == "mlp0_down":
                fox_full = jnp.concatenate([res[0][s, :nfox] for s in range(N_CHIPS)], axis=0)
                wv.update(fox_in=_View(jnp.pad(fox_full, ((0, NU - fox_full.shape[0]), (0, 0)))))
            elif name == "attn_forward":
                wv.update(fox_out=_View(res[0], "rs"), w1_1=_View(res[1], "cs"), w2_1=_View(res[2], "rs"))
            elif name in swap_at:
                swapped[swap_at[name]][1] = res
            else:
                scattered.update(zip(members[scatter_at[name]], res))

    def grad_view(grads, name):
        if name in ("w1_0", "w1_1"):
            return _View(None, "cs", shape=(N_CHIPS, D, FQ), dtype=BF16)
        if name in ("w2_0", "w2_1"):
            return _View(None, "rs", shape=(N_CHIPS, FQ, D), dtype=BF16)
        if name == "lru_in":
            return _View(None, "cs", shape=(N_CHIPS, D, 2 * D // N_CHIPS), dtype=BF16)
        if name in ("lru_out", "fox_out"):
            return _View(None, "rs", shape=(N_CHIPS, DQ, D), dtype=BF16)
        return _View(None, shape=(NU, D), dtype=BF16)

    loss, gx, grads = _local_step(x[0], loss_target[0], small, wv, grad_view, Comm)

    pack_names = _REPLICATED + ["conv_w"]
    flat = jnp.concatenate([grads[n].reshape(-1).astype(F32) for n in pack_names] + [loss.reshape(-1)])
    per_chip = -(-flat.shape[0] // (N_CHIPS * _PACK_TILE)) * _PACK_TILE
    pack = jnp.pad(flat, (0, N_CHIPS * per_chip - flat.shape[0])).reshape(N_CHIPS, per_chip // LANES, LANES)
    pack_sib = _run_plan("pack_pair_swap", _Swap([pack]))
    (scattered["pack"],) = _run_plan("pack_chip_scatter", _pair_partials("pack", [pack], pack_sib, [F32], core))
    order = ["w1_0", "w1_1", "w2_0", "w2_1", "lru_in", "lru_out", "fox_in", "fox_out", "pack"]
    red = dict(zip(order, _finish_reduce("grads", [scattered[n] for n in order])))
    (all_pack,) = _all_gather("gather_small_grads", [red["pack"]])
    all_flat = all_pack.reshape(-1)
    G = {}
    off = 0
    for n in pack_names:
        shape = grads[n].shape if n == "conv_w" else W[n].shape
        size = int(np.prod(shape))
        G[n] = all_flat[off:off + size].reshape(shape)
        off += size
    total = all_flat[off]
    G["lru_conv_w"] = lax.dynamic_slice_in_dim(G.pop("conv_w"), chip * DQ, DQ, axis=1)[None]
    parts = {n: [_as2d(G[n])] for n in G}
    parts.update(mlp_w1=[red["w1_0"], red["w1_1"]], mlp_w2=[red["w2_0"], red["w2_1"]], lru_w_in=[red["lru_in"]],
                 lru_w_out=[red["lru_out"]], fox_w_in=[red["fox_in"][:nfox, None, :]], fox_w_out=[red["fox_out"]])

    delta, new_m, new_v = {}, {}, {}
    for n in _WEIGHTS:
        if n == "fox_w_in":
            to_thin = lambda a: jnp.transpose(a, (2, 0, 1))
            res = _adamw(f"adamw_{n}", to_thin(W[n]), parts[n], to_thin(Mo[n]), to_thin(Vo[n]))
            G[n], delta[n], new_m[n], new_v[n] = (jnp.transpose(t, (1, 2, 0)) for t in res)
            continue
        go, d, nm, nv = _adamw(f"adamw_{n}", _as2d(W[n]), parts[n], _as2d(Mo[n]), _as2d(Vo[n]))
        G[n], delta[n], new_m[n], new_v[n] = (t.reshape(W[n].shape) for t in (go, d, nm, nv))

    return (total, gx[None], *[G[n] for n in _WEIGHTS], *[delta[n] for n in _WEIGHTS],
            *[new_m[n] for n in _WEIGHTS], *[new_v[n] for n in _WEIGHTS])
```

```python
import functools

import numpy as np
import jax
import jax.numpy as jnp
from jax import lax
from jax.experimental import pallas as pl
from jax.experimental.pallas import tpu as pltpu

F32 = jnp.float32
BF16 = jnp.bfloat16

HEAD_DIM = 64
LRU_BLOCK_DIM = 64
CONV_WIDTH = 4
LRU_C = 8.0
EPS = 1e-6
NEG_INF = -1e30
ADAM_LR = 0.001
ADAM_B1 = 0.9
ADAM_B2 = 0.999
ADAM_EPS = 1e-08
ADAM_WD = 0.01
ADAM_STEP = 10

N_CHIPS = 4
LANES = 128
SUBLANES = 8
MXU_DIM = 256
VMEM_LIMIT = 52 * 1024 * 1024
MESH = pl.DeviceIdType.MESH
ANY = pl.BlockSpec(memory_space=pl.ANY)


def _pick(n, prefs):
    for p in prefs:
        if p <= n and n % p == 0:
            return p
    return n


def _params(sem=None):
    return pltpu.CompilerParams(dimension_semantics=sem, vmem_limit_bytes=VMEM_LIMIT)


class _View:
    def __init__(self, arr, kind="plain", shape=None, dtype=None):
        self.arr = arr
        self.kind = kind
        self.shape = tuple(arr.shape) if arr is not None else tuple(shape)
        self.dtype = arr.dtype if arr is not None else dtype

    def limits(self):
        if self.kind == "plain":
            return 0, 0
        return self.shape[-2], (self.shape[-1] if self.kind == "cs" else 0)

    def spec(self, br, bc, fr, fc):
        if self.kind == "plain":
            return pl.BlockSpec((br, bc), lambda *g: (fr(*g), fc(*g)))
        rows, ncol = self.shape[-2:]
        assert rows % br == 0 and ncol % bc == 0, (self.shape, br, bc)
        if self.kind == "cs":
            per = ncol // bc
            return pl.BlockSpec((None, br, bc), lambda *g: (fc(*g) // per, fr(*g), fc(*g) % per))
        per = rows // br
        return pl.BlockSpec((None, br, bc), lambda *g: (fr(*g) // per, fr(*g) % per, fc(*g)))


def _bf(x):
    return x if x.dtype == BF16 else x.astype(BF16)


def _matmul(name, A, B, M, N, K, *, ta=False, tb=False, outs, epilogue, extras=(), vecs=(), n_sums=0,
            tm=None, tn=None, tk=None, plan=None):
    lim = {"m": [M], "n": [N], "k": [K]}
    for view, (rdim, cdim) in ([(A, "km" if ta else "mk"), (B, "nk" if tb else "kn")]
                               + [(e, "mn") for e in extras] + [(o, "mn") for o in outs]):
        r_lim, c_lim = view.limits()
        lim[rdim].append(r_lim)
        lim[cdim].append(c_lim)
    lean = n_sums == 0 and all(jnp.dtype(v.dtype).itemsize <= 2 for v in list(extras) + list(outs))
    big_m = (2048,) if lean and K <= 1024 else ()
    big_k = (2048,) if lean and K > 1024 else ()
    tm = tm or _pick(int(np.gcd.reduce(lim["m"])), big_m + (1024, 640, 512, 256, 128))
    tn = tn or _pick(int(np.gcd.reduce(lim["n"])), (1024, 640, 512, 256, 128))
    tk = tk or _pick(int(np.gcd.reduce(lim["k"])), big_k + (1024, 640, 512, 256, 128))
    nk = K // tk
    gi = lambda i, j, k: i
    gj = lambda i, j, k: j
    gk = lambda i, j, k: k
    a_spec = A.spec(tk, tm, gk, gi) if ta else A.spec(tm, tk, gi, gk)
    b_spec = B.spec(tn, tk, gj, gk) if tb else B.spec(tk, tn, gk, gj)
    ca = 0 if ta else 1
    cb = 1 if tb else 0
    ne, no = len(extras) + len(vecs), len(outs)
    assert n_sums == 0 or tn == N
    row_spec = pl.BlockSpec((1, tn), lambda i, j, k: (0, j))
    in_specs = [a_spec, b_spec] + [e.spec(tm, tn, gi, gj) for e in extras] + [row_spec] * len(vecs)
    operands = [A.arr, B.arr] + [e.arr for e in extras] + list(vecs)
    out_specs = [o.spec(tm, tn, gi, gj) for o in outs] + [row_spec] * n_sums
    out_shape = ([jax.ShapeDtypeStruct(o.shape, o.dtype) for o in outs]
                 + [jax.ShapeDtypeStruct((1, N), F32)] * n_sums)

    def body(*refs):
        a_ref, b_ref = refs[0], refs[1]
        ex = refs[2:2 + ne]
        o_refs = refs[2 + ne:2 + ne + no]
        s_refs = refs[2 + ne + no:2 + ne + no + n_sums]
        first_row_tile = pl.program_id(0) == 0

        def prod():
            return lax.dot_general(_bf(a_ref[...]), _bf(b_ref[...]), (((ca,), (cb,)), ((), ())),
                                   preferred_element_type=F32)

        def finish(acc):
            res = epilogue(acc, *[e[...] for e in ex])
            for o_ref, r in zip(o_refs, res[:no]):
                o_ref[...] = r.astype(o_ref.dtype)
            for s_ref, r in zip(s_refs, res[no:]):
                def assign(s_ref=s_ref, r=r):
                    s_ref[...] = r

                def accumulate(s_ref=s_ref, r=r):
                    s_ref[...] += r

                pl.when(first_row_tile)(assign)
                pl.when(jnp.logical_not(first_row_tile))(accumulate)

        if nk == 1:
            finish(prod())
        else:
            acc_ref = refs[-1]
            k = pl.program_id(2)

            @pl.when(k == 0)
            def _():
                acc_ref[...] = jnp.zeros_like(acc_ref)

            acc_ref[...] += prod()

            @pl.when(k == nk - 1)
            def _():
                finish(acc_ref[...])

    res, side = _hosted_call(body, name, (M // tm, N // tn, nk), in_specs, out_specs, out_shape,
                             [pltpu.VMEM((tm, tn), F32)] if nk > 1 else [], operands,
                             ("arbitrary", "arbitrary", "arbitrary"), plan)
    return res if plan is None else (res, side)


def _ep_store(acc):
    return (acc,)


def _ep_resid(acc, res):
    return (res + acc,)


def _ep_resid_norm(acc, res, g):
    xo = res + acc
    r = lax.rsqrt(jnp.mean(xo * xo, axis=-1, keepdims=True) + EPS)
    return (xo, (xo * r) * g)


def _ep_norm_bwd(acc, x, dres, g):
    r = lax.rsqrt(jnp.mean(x * x, axis=-1, keepdims=True) + EPS)
    xhat = x * r
    dxn = acc * g
    tot = dres + r * (dxn - xhat * jnp.mean(dxn * xhat, axis=-1, keepdims=True))
    return (tot, tot, jnp.sum(acc * xhat, axis=0, keepdims=True))


def _ep_relu2(acc):
    zp = jnp.maximum(acc, 0.0)
    return (zp * zp,)


def _ep_drelu2(acc, act):
    return (acc * (2.0 * jnp.sqrt(act.astype(F32))),)


def _fresh(M, N, dtype):
    return _View(None, shape=(M, N), dtype=dtype)


def _rms_fwd(name, x, g, S, D, plan=None):
    T = _pick(S, (512, 256, 128))

    def body(x_ref, g_ref, h_ref):
        x = x_ref[...]
        r = lax.rsqrt(jnp.mean(x * x, axis=-1, keepdims=True) + EPS)
        h_ref[...] = ((x * r) * g_ref[...]).astype(BF16)

    return _hosted_call(body, name, (S // T,),
                        [pl.BlockSpec((T, D), lambda i: (i, 0)), pl.BlockSpec((1, D), lambda i: (0, 0))],
                        [pl.BlockSpec((T, D), lambda i: (i, 0))], [jax.ShapeDtypeStruct((S, D), BF16)], [], (x, g),
                        ("arbitrary",), plan)


def _loss_head(x, tgt, S, D):
    T = _pick(S, (512, 256, 128))

    def body(x_ref, t_ref, loss_ref, d_ref, db_ref):
        @pl.when(pl.program_id(0) == 0)
        def _():
            loss_ref[...] = jnp.zeros_like(loss_ref)

        e = x_ref[...] - t_ref[...]
        loss_ref[...] += 0.5 * jnp.sum(jnp.mean(e * e, axis=-1, keepdims=True), axis=0, keepdims=True)
        d = e * (1.0 / D)
        d_ref[...] = d
        db_ref[...] = d.astype(BF16)

    row = pl.BlockSpec((T, D), lambda i: (i, 0))
    return pl.pallas_call(
        body, name="loss_head", grid=(S // T,), in_specs=[row, row],
        out_specs=[pl.BlockSpec((1, 1), lambda i: (0, 0)), row, row],
        out_shape=[jax.ShapeDtypeStruct((1, 1), F32), jax.ShapeDtypeStruct((S, D), F32),
                   jax.ShapeDtypeStruct((S, D), BF16)],
        compiler_params=_params(("arbitrary",)),
    )(x, tgt)


def _sigmoid(z):
    return 1.0 / (1.0 + jnp.exp(-z))


def _log_sigmoid(z):
    return jnp.minimum(z, 0.0) - jnp.log(1.0 + jnp.exp(-jnp.abs(z)))


_GELU_K = 0.7978845608028654
_GELU_C = 0.044715


def _gelu(x):
    t = jnp.tanh(_GELU_K * (x + _GELU_C * (x * x * x)))
    return 0.5 * x * (1.0 + t)


def _gelu_and_grad(x):
    x2 = x * x
    t = jnp.tanh(_GELU_K * (x + _GELU_C * (x2 * x)))
    g = 0.5 * x * (1.0 + t)
    dg = 0.5 * (1.0 + t) + 0.5 * x * (1.0 - t * t) * (_GELU_K * (1.0 + 3.0 * _GELU_C * x2))
    return g, dg


def _decay_terms(r, ls):
    la = LRU_C * r * ls
    a = jnp.exp(la)
    a2 = a * a
    mult = jnp.sqrt(-jnp.tanh(la) * (a2 + 1.0))
    return a, a2, mult


def _lru_fwd(u0, conv_w, conv_b, wr_bd, b_r, wi_bd, b_i, lam, S, D, plan=None):
    T = _pick(S, (256, 128))
    GT = wr_bd.shape[-1]
    nG = D // GT

    def body(gb_ref, xb_ref, cw_ref, cb_ref, wr_ref, br_ref, wi_ref, bi_ref, lam_ref,
             y_ref, xc_ref, r_ref, i_ref, hs_ref, ext, a_scr, hcar):
        @pl.when(pl.program_id(0) == 0)
        def _():
            ext[0:SUBLANES, :] = jnp.zeros((SUBLANES, D), F32)
            hcar[...] = jnp.zeros_like(hcar)

        xb = xb_ref[...]
        ext[SUBLANES:SUBLANES + T, :] = xb
        xc = cb_ref[...]
        for k in range(CONV_WIDTH):
            xc = xc + ext[pl.ds(SUBLANES - (CONV_WIDTH - 1) + k, T), :] * cw_ref[k:k + 1, :]
        ext[0:SUBLANES, :] = xb[T - SUBLANES:T, :]
        xc_ref[...] = xc
        xcb = xc.astype(BF16)
        for g in range(nG):
            sl = slice(g * GT, (g + 1) * GT)
            zr = jnp.dot(xcb[:, sl], wr_ref[g], preferred_element_type=F32) + br_ref[:, sl]
            zi = jnp.dot(xcb[:, sl], wi_ref[g], preferred_element_type=F32) + bi_ref[:, sl]
            r_ref[:, sl] = _sigmoid(zr)
            i_ref[:, sl] = _sigmoid(zi)
        r = r_ref[...]
        a, _, mult = _decay_terms(r, _log_sigmoid(lam_ref[...]))
        a_scr[...] = a
        hs_ref[...] = mult * (i_ref[...] * xc)

        def step(t, h):
            h = a_scr[pl.ds(t, 1), :] * h + hs_ref[pl.ds(t, 1), :]
            hs_ref[pl.ds(t, 1), :] = h
            return h

        hcar[...] = lax.fori_loop(0, T, step, hcar[...], unroll=8)
        y_ref[...] = (_gelu(gb_ref[...]) * hs_ref[...]).astype(BF16)

    row = pl.BlockSpec((T, D), lambda i: (i, 0))
    vec = pl.BlockSpec((1, D), lambda i: (0, 0))
    bd = pl.BlockSpec((nG, GT, GT), lambda i: (0, 0, 0))
    f32o = jax.ShapeDtypeStruct((S, D), F32)
    return _hosted_call(
        body, "lru_fwd", (S // T,),
        [row, pl.BlockSpec((T, D), lambda i: (i, 1)), pl.BlockSpec((CONV_WIDTH, D), lambda i: (0, 0)), vec,
         bd, vec, bd, vec, vec],
        [row, row, row, row, row], [jax.ShapeDtypeStruct((S, D), BF16), f32o, f32o, f32o, f32o],
        [pltpu.VMEM((T + SUBLANES, D), F32), pltpu.VMEM((T, D), F32), pltpu.VMEM((1, D), F32)],
        (u0, u0, conv_w, conv_b, wr_bd, b_r, wi_bd, b_i, lam), ("arbitrary",), plan)


def _lru_bwd(dy, u0, xc, r, ig, hs, conv_w, wr_bd, wi_bd, lam, S, D, plan=None):
    T = _pick(S, (128,))
    nT = S // T
    GT = wr_bd.shape[-1]
    nG = D // GT
    W = CONV_WIDTH

    def body(dy_ref, gb_ref, xb_ref, xbp_ref, xc_ref, r_ref, i_ref, hs_ref, hsp_ref, cw_ref, wr_ref, wi_ref, lam_ref,
             du_ref, dcw_ref, dcb_ref, dlam_ref, dbr_ref, dbi_ref, dwr_ref, dwi_ref,
             a_scr, dh_scr, exth, extx, extd, dxc_scr, dz_scr, carry):
        step = pl.program_id(0)
        first_tile = step == nT - 1

        @pl.when(step == 0)
        def _():
            for ref in (dcw_ref, dcb_ref, dlam_ref, dbr_ref, dbi_ref, dwr_ref, dwi_ref, carry):
                ref[...] = jnp.zeros_like(ref)
            extd[T:T + SUBLANES, :] = jnp.zeros((SUBLANES, D), F32)

        hs = hs_ref[...]
        dy = dy_ref[...]
        g, dgelu = _gelu_and_grad(gb_ref[...])
        du_ref[:, 0:D] = (dy * hs * dgelu).astype(BF16)
        r = r_ref[...]
        lam = lam_ref[...]
        ls = _log_sigmoid(lam)
        a, a2, mult = _decay_terms(r, ls)
        a_scr[...] = a
        dh_scr[...] = dy * g

        def rstep(j, c):
            t = T - 1 - j
            d = dh_scr[pl.ds(t, 1), :] + c
            dh_scr[pl.ds(t, 1), :] = d
            return a_scr[pl.ds(t, 1), :] * d

        carry[...] = lax.fori_loop(0, T, rstep, carry[...], unroll=8)
        dh = dh_scr[...]
        keep = jnp.where(first_tile, 0.0, 1.0)
        exth[0:SUBLANES, :] = hsp_ref[...] * keep
        exth[SUBLANES:SUBLANES + T, :] = hs
        hprev = exth[pl.ds(SUBLANES - 1, T), :]
        xc = xc_ref[...]
        ig = i_ref[...]
        da = dh * hprev
        dmult = dh * (ig * xc)
        dla = da * a - dmult * (a2 / mult)
        dlam_ref[...] += jnp.sum(dla * r, axis=0, keepdims=True) * (LRU_C * _sigmoid(-lam))
        dzr = (dla * (LRU_C * ls)) * (r * (1.0 - r))
        dzi = (dh * (mult * xc)) * (ig * (1.0 - ig))
        dbr_ref[...] += jnp.sum(dzr, axis=0, keepdims=True)
        dbi_ref[...] += jnp.sum(dzi, axis=0, keepdims=True)
        dxc_scr[...] = dh * (mult * ig)
        xcb = xc.astype(BF16)
        dz_scr[0] = dzr.astype(BF16)
        dz_scr[1] = dzi.astype(BF16)
        nt_dims = (((1,), (1,)), ((), ()))
        tn_dims = (((0,), (0,)), ((), ()))
        for gq in range(nG):
            sl = slice(gq * GT, (gq + 1) * GT)
            zr_g = dz_scr[0, :, sl]
            zi_g = dz_scr[1, :, sl]
            dxc_scr[:, sl] += (lax.dot_general(zr_g, wr_ref[gq], nt_dims, preferred_element_type=F32)
                               + lax.dot_general(zi_g, wi_ref[gq], nt_dims, preferred_element_type=F32))
            dwr_ref[gq] += lax.dot_general(xcb[:, sl], zr_g, tn_dims, preferred_element_type=F32)
            dwi_ref[gq] += lax.dot_general(xcb[:, sl], zi_g, tn_dims, preferred_element_type=F32)
        dxc = dxc_scr[...]
        dcb_ref[...] += jnp.sum(dxc, axis=0, keepdims=True)
        extx[0:SUBLANES, :] = xbp_ref[...] * keep
        extx[SUBLANES:SUBLANES + T, :] = xb_ref[...]
        extd[0:T, :] = dxc
        dxb = jnp.zeros((T, D), F32)
        for k in range(W):
            dxb = dxb + extd[pl.ds(W - 1 - k, T), :] * cw_ref[k:k + 1, :]
            dcw_ref[k:k + 1, :] += jnp.sum(dxc * extx[pl.ds(SUBLANES - (W - 1) + k, T), :], axis=0, keepdims=True)
        extd[T:T + SUBLANES, :] = dxc[0:SUBLANES, :]
        du_ref[:, D:2 * D] = dxb.astype(BF16)

    rev = lambda i: nT - 1 - i
    tpb = T // SUBLANES
    prev8 = lambda i: jnp.maximum(rev(i) * tpb - 1, 0)
    row = pl.BlockSpec((T, D), lambda i: (rev(i), 0))
    vec = pl.BlockSpec((1, D), lambda i: (0, 0))
    bd = pl.BlockSpec((nG, GT, GT), lambda i: (0, 0, 0))
    vec_o = jax.ShapeDtypeStruct((1, D), F32)
    bd_o = jax.ShapeDtypeStruct((nG, GT, GT), F32)
    return _hosted_call(
        body, "lru_bwd", (nT,),
        [row, row, pl.BlockSpec((T, D), lambda i: (rev(i), 1)), pl.BlockSpec((SUBLANES, D), lambda i: (prev8(i), 1)),
         row, row, row, row, pl.BlockSpec((SUBLANES, D), lambda i: (prev8(i), 0)),
         pl.BlockSpec((W, D), lambda i: (0, 0)), bd, bd, vec],
        [pl.BlockSpec((T, 2 * D), lambda i: (rev(i), 0)), pl.BlockSpec((W, D), lambda i: (0, 0)),
         vec, vec, vec, vec, bd, bd],
        [jax.ShapeDtypeStruct((S, 2 * D), BF16), jax.ShapeDtypeStruct((W, D), F32), vec_o, vec_o, vec_o, vec_o, bd_o, bd_o],
        [pltpu.VMEM((T, D), F32), pltpu.VMEM((T, D), F32), pltpu.VMEM((T + SUBLANES, D), F32),
         pltpu.VMEM((T + SUBLANES, D), F32), pltpu.VMEM((T + SUBLANES, D), F32),
         pltpu.VMEM((T, D), F32), pltpu.VMEM((2, T, D), BF16), pltpu.VMEM((1, D), F32)],
        (dy, u0, u0, u0, xc, r, ig, hs, hs, conv_w, wr_bd, wi_bd, lam), ("arbitrary",), plan)


AUG_ROWS = 16
HEAD_ROWS = 128
LSE_ROW = HEAD_DIM + 6
ONES_ROW_Q = HEAD_DIM + 3
ONES_COL_K = HEAD_DIM
ONES_ROW_V = HEAD_DIM
PREP_LANES = 512
HEAD_UNROLL = 4


def _split3(x):
    b1 = x.astype(BF16).astype(F32)
    r = x - b1
    b2 = r.astype(BF16).astype(F32)
    return b1, b2, r - b2


def _head_block(x, aug, T):
    row = lax.broadcasted_iota(jnp.int32, (AUG_ROWS, T), 0)
    blk = jnp.zeros((AUG_ROWS, T), F32)
    for i, e in enumerate(aug):
        blk = jnp.where(row == i, e, blk)
    return jnp.concatenate([x, blk, jnp.zeros((HEAD_ROWS - HEAD_DIM - AUG_ROWS, T), F32)], axis=0)


def _tri_matrix(lower):
    i = np.arange(LANES)
    m = (i[:, None] >= i[None, :]) if lower else (i[:, None] <= i[None, :])
    return jnp.asarray(m.astype(np.float32), BF16)


def _lane_cumsum(x, tri_ref, carry, reverse):
    n = x.shape[1] // LANES
    tri = tri_ref[...]
    out = [None] * n
    for j in (range(n - 1, -1, -1) if reverse else range(n)):
        cs = carry
        for part in _split3(x[:, j * LANES:(j + 1) * LANES]):
            cs = cs + jnp.dot(part.astype(BF16), tri, preferred_element_type=F32)
        out[j] = cs
        carry = cs[:, 0:1] if reverse else cs[:, LANES - 1:LANES]
    return jnp.concatenate(out, axis=1), carry


def _head_rows(h):
    return pl.ds(pl.multiple_of(h * HEAD_DIM, HEAD_DIM), HEAD_DIM)


def _fox_prep(ut, b_f, qg, kg, S, D, tq):
    H = D // HEAD_DIM
    T = min(tq, PREP_LANES)
    per = tq // T
    scale = HEAD_DIM ** -0.5

    def body(q_ref, k_ref, v_ref, f_ref, bf_ref, qg_ref, kg_ref, tri_ref,
             qat_ref, kat_ref, vat_ref, ka_ref, c_scr, ccar):
        @pl.when(pl.program_id(0) == 0)
        def _():
            ccar[...] = jnp.zeros_like(ccar)

        c, carry = _lane_cumsum(_log_sigmoid(f_ref[...] + bf_ref[...]), tri_ref, ccar[...], False)
        c_scr[...] = c
        ccar[...] = carry

        def head(h, _):
            rows = _head_rows(h)
            c1, c2, c3 = _split3(c_scr[pl.ds(h, 1), :])

            def normed(src, gain, mul):
                x = src[rows, :]
                rs = lax.rsqrt(jnp.mean(x * x, axis=0, keepdims=True) + EPS)
                return ((x * rs) * gain[rows, :]) * mul

            qat_ref[h] = _head_block(normed(q_ref, qg_ref, scale), [c1, c2, c3, 1.0, 1.0, 1.0], T).astype(BF16)
            kb = _head_block(normed(k_ref, kg_ref, 1.0), [1.0, 1.0, 1.0, -c1, -c2, -c3, 1.0, 1.0, 1.0], T)
            kat_ref[h] = kb.astype(BF16)
            ka_ref[h] = kb.T.astype(BF16)
            vat_ref[h] = _head_block(v_ref[rows, :], [1.0, 1.0, 1.0], T).astype(BF16)
            return 0

        lax.fori_loop(0, H, head, 0, unroll=HEAD_UNROLL)

    part = lambda j: pl.BlockSpec((D, T), lambda i: (j, i))
    colv = lambda n: pl.BlockSpec((n, 1), lambda i: (0, 0))
    tmaj = lambda r: pl.BlockSpec((H, None, r, T), lambda i: (0, i // per, 0, i % per))
    norm = pl.BlockSpec((H, T, HEAD_ROWS), lambda i: (0, i, 0))
    tshape = lambda r: jax.ShapeDtypeStruct((H, S // tq, r, tq), BF16)
    nshape = jax.ShapeDtypeStruct((H, S, HEAD_ROWS), BF16)
    return pl.pallas_call(
        body, name="fox_prep", grid=(S // T,),
        in_specs=[part(0), part(1), part(2), pl.BlockSpec((LANES, T), lambda i: (3 * D // LANES, i)),
                  colv(LANES), colv(D), colv(D), pl.BlockSpec((LANES, LANES), lambda i: (0, 0))],
        out_specs=[tmaj(HEAD_ROWS), tmaj(HEAD_ROWS), tmaj(HEAD_ROWS), norm],
        out_shape=[tshape(HEAD_ROWS), tshape(HEAD_ROWS), tshape(HEAD_ROWS), nshape],
        scratch_shapes=[pltpu.VMEM((LANES, T), F32), pltpu.VMEM((LANES, 1), F32)],
        compiler_params=_params(("arbitrary",)),
    )(ut, ut, ut, ut, b_f, qg, kg, _tri_matrix(False))


def _fox_bwd_prep(dot, ot, lse, qat, S, D, tq, plan=None):
    H = D // HEAD_DIM
    T = min(tq, PREP_LANES)
    per = tq // T

    def body(do_ref, o_ref, lse_ref, qat_ref, doat_ref, doa_ref, qat1_ref, qa1_ref):
        row = lax.broadcasted_iota(jnp.int32, (HEAD_ROWS, T), 0)

        def head(h, _):
            rows = _head_rows(h)
            do = do_ref[rows, :].astype(F32)
            delta = jnp.sum(do * o_ref[rows, :], axis=0, keepdims=True)
            db = _head_block(do, list(_split3(-delta)), T)
            doat_ref[h] = db.astype(BF16)
            doa_ref[h] = db.T.astype(BF16)
            qb = qat_ref[h].astype(F32)
            for i, e in enumerate(_split3(-lse_ref[h])):
                qb = jnp.where(row == LSE_ROW + i, e, qb)
            qat1_ref[h] = qb.astype(BF16)
            qa1_ref[h] = qb.T.astype(BF16)
            return 0

        lax.fori_loop(0, H, head, 0, unroll=HEAD_UNROLL)

    chan = pl.BlockSpec((D, T), lambda i: (0, i))
    tmaj = pl.BlockSpec((H, None, HEAD_ROWS, T), lambda i: (0, i // per, 0, i % per))
    norm = pl.BlockSpec((H, T, HEAD_ROWS), lambda i: (0, i, 0))
    tshape = jax.ShapeDtypeStruct((H, S // tq, HEAD_ROWS, tq), BF16)
    nshape = jax.ShapeDtypeStruct((H, S, HEAD_ROWS), BF16)
    return _hosted_call(body, "fox_bwd_prep", (S // T,), [chan, chan, pl.BlockSpec((H, 1, T), lambda i: (0, 0, i)), tmaj],
                        [tmaj, norm, tmaj, norm], [tshape, nshape, tshape, nshape], [], (dot, ot, lse, qat),
                        ("arbitrary",), plan)


def _causal(s, k_axis):
    t = min(s.shape)
    ki = lax.broadcasted_iota(jnp.int32, s.shape, k_axis) - (s.shape[k_axis] - t)
    qi = lax.broadcasted_iota(jnp.int32, s.shape, 1 - k_axis)
    return jnp.where(ki <= qi, s, NEG_INF)


def _seq_tile(i, t):
    return pl.ds(pl.multiple_of(i * t, t), t)


def _attn_forward(ka, qat, vat, S, D, tq, plan=None):
    H = D // HEAD_DIM
    nq = S // tq
    G = 4

    def body(ka_ref, qat_ref, vat_ref, o_ref, o32_ref, lse_ref, m_scr, acc_scr):
        qi = pl.program_id(1)
        m_scr[...] = jnp.full_like(m_scr, NEG_INF)
        acc_scr[...] = jnp.zeros_like(acc_scr)

        def span(k0, n, diagonal):
            keys = pl.ds(pl.multiple_of(k0 * tq, tq), n * tq)
            s = [jnp.dot(ka_ref[g, keys, :], qat_ref[g], preferred_element_type=F32) for g in range(G)]
            if diagonal:
                s = [_causal(sg, 0) for sg in s]
            m_prev = [m_scr[g] for g in range(G)]
            m_new = [jnp.maximum(m_prev[g], jnp.max(s[g], axis=0, keepdims=True)) for g in range(G)]
            p = [jnp.exp(s[g] - m_new[g]).astype(BF16) for g in range(G)]
            for g in range(G):
                upd = jnp.dot(vat_ref[g, k0], p[g][0:tq], preferred_element_type=F32)
                for i in range(1, n):
                    upd = upd + jnp.dot(vat_ref[g, k0 + i], p[g][i * tq:(i + 1) * tq], preferred_element_type=F32)
                acc_scr[g] = jnp.exp(m_prev[g] - m_new[g]) * acc_scr[g] + upd
                m_scr[g] = m_new[g]

        def off_diagonal_pair(j, _):
            span(2 * j, 2, False)
            return 0

        lax.fori_loop(0, qi // 2, off_diagonal_pair, 0)
        pl.when(qi % 2 == 1)(lambda: span(qi - 1, 2, True))
        pl.when(qi % 2 == 0)(lambda: span(qi, 1, True))
        for g in range(G):
            l = acc_scr[g, ONES_ROW_V:ONES_ROW_V + 1, :]
            o = acc_scr[g, 0:HEAD_DIM, :] / l
            o_ref[g * HEAD_DIM:(g + 1) * HEAD_DIM, :] = o.astype(BF16)
            o32_ref[g * HEAD_DIM:(g + 1) * HEAD_DIM, :] = o
            lse_ref[g] = m_scr[g] + jnp.log(l)

    chan = pl.BlockSpec((G * HEAD_DIM, tq), lambda h, i: (h, i))
    stat = pl.BlockSpec((G, 1, tq), lambda h, i: (h, 0, i))
    return _hosted_call(
        body, "attn_forward", (H // G, nq),
        [pl.BlockSpec((G, S, HEAD_ROWS), lambda h, i: (h, 0, 0)),
         pl.BlockSpec((G, None, HEAD_ROWS, tq), lambda h, i: (h, i, 0, 0)),
         pl.BlockSpec((G, nq, HEAD_ROWS, tq), lambda h, i: (h, 0, 0, 0))],
        [chan, chan, stat],
        [jax.ShapeDtypeStruct((D, S), BF16), jax.ShapeDtypeStruct((D, S), F32), jax.ShapeDtypeStruct((H, 1, S), F32)],
        [pltpu.VMEM((G, 1, tq), F32), pltpu.VMEM((G, HEAD_ROWS, tq), F32)],
        (ka, qat, vat), ("arbitrary", "arbitrary"), plan)


def _attn_backward(qa, doa, qat, doat, ka, kat, vat, S, D, tq, plan=None):
    H = D // HEAD_DIM
    nq = S // tq
    G = 2

    def body(qa_ref, doa_ref, qat_ref, doat_ref, ka_ref, kat_ref, vat_ref, dq_ref, dk_ref, dv_ref, dk_scr, dv_scr):
        ki = pl.program_id(1)

        @pl.when(ki == 0)
        def _():
            dq_ref[...] = jnp.zeros_like(dq_ref)

        dk_scr[...] = jnp.zeros_like(dk_scr)
        dv_scr[...] = jnp.zeros_like(dv_scr)

        def span(q0, n, diagonal):
            rows = pl.ds(pl.multiple_of(q0 * tq, tq), n * tq)
            s = [jnp.dot(qa_ref[g, rows, :], kat_ref[g], preferred_element_type=F32) for g in range(G)]
            if diagonal:
                s = [_causal(sg, 1) for sg in s]
            p = [jnp.exp(sg) for sg in s]
            ds = [(p[g] * jnp.dot(doa_ref[g, rows, :], vat_ref[g], preferred_element_type=F32)).astype(BF16)
                  for g in range(G)]
            p = [pg.astype(BF16) for pg in p]
            for g in range(G):
                for i in range(n):
                    part = slice(i * tq, (i + 1) * tq)
                    dv_scr[g] += jnp.dot(doat_ref[g, q0 + i, 0:HEAD_DIM, :], p[g][part], preferred_element_type=F32)
                    dk_scr[g] += jnp.dot(qat_ref[g, q0 + i], ds[g][part], preferred_element_type=F32)
                dq_ref[g, rows, :] += jnp.dot(ds[g], ka_ref[g], preferred_element_type=F32)

        n_off = nq - 1 - ki
        odd = n_off % 2

        def off_diagonal_pair(j, _):
            span(ki + 1 + odd + 2 * j, 2, False)
            return 0

        pl.when(odd == 1)(lambda: span(ki, 2, True))
        pl.when(odd == 0)(lambda: span(ki, 1, True))
        lax.fori_loop(0, n_off // 2, off_diagonal_pair, 0)
        dk_ref[...] = dk_scr[...]
        for g in range(G):
            dv_ref[g * HEAD_DIM:(g + 1) * HEAD_DIM, :] = dv_scr[g].astype(BF16)

    whole = pl.BlockSpec((G, S, HEAD_ROWS), lambda h, i: (h, 0, 0))
    tiles = pl.BlockSpec((G, nq, HEAD_ROWS, tq), lambda h, i: (h, 0, 0, 0))
    one = pl.BlockSpec((G, None, HEAD_ROWS, tq), lambda h, i: (h, i, 0, 0))
    return _hosted_call(
        body, "attn_backward", (H // G, nq),
        [whole, whole, tiles, tiles, pl.BlockSpec((G, tq, HEAD_ROWS), lambda h, i: (h, i, 0)), one, one],
        [whole, pl.BlockSpec((G, HEAD_ROWS, tq), lambda h, i: (h, 0, i)),
         pl.BlockSpec((G * HEAD_DIM, tq), lambda h, i: (h, i))],
        [jax.ShapeDtypeStruct((H, S, HEAD_ROWS), F32), jax.ShapeDtypeStruct((H, HEAD_ROWS, S), F32),
         jax.ShapeDtypeStruct((D, S), BF16)],
        [pltpu.VMEM((G, HEAD_ROWS, tq), F32), pltpu.VMEM((G, HEAD_DIM, tq), F32)],
        (qa, doa, qat, doat, ka, kat, vat), ("arbitrary", "arbitrary"), plan)


def _fox_prep_bwd(ut, dq, dkt, dvt, b_f, qg, kg, S, D, tq):
    H = D // HEAD_DIM
    T = min(tq, PREP_LANES)
    nT = S // T
    NU = 3 * D + LANES
    scale = HEAD_DIM ** -0.5

    def body(q_ref, k_ref, f_ref, dq_ref, dk_ref, dv_ref, bf_ref, qg_ref, kg_ref, tri_ref,
             du_ref, dbf_ref, dqg_ref, dkg_ref, gq_acc, gk_acc, fcar, dc_scr):
        step = pl.program_id(0)

        @pl.when(step == 0)
        def _():
            for ref in (gq_acc, gk_acc, fcar, dbf_ref):
                ref[...] = jnp.zeros_like(ref)

        dc_scr[...] = jnp.zeros_like(dc_scr)

        def head(h, _):
            rows = _head_rows(h)
            dqb = dq_ref[h].T
            dkb = dk_ref[h]
            dc_scr[pl.ds(h, 1), :] = dqb[ONES_COL_K:ONES_COL_K + 1, :] - dkb[ONES_ROW_Q:ONES_ROW_Q + 1, :]
            for src, dsrc, gain, acc, mul, base in ((q_ref, dqb, qg_ref, gq_acc, scale, 0),
                                                    (k_ref, dkb, kg_ref, gk_acc, 1.0, D)):
                x = src[rows, :]
                rs = lax.rsqrt(jnp.mean(x * x, axis=0, keepdims=True) + EPS)
                xhat = x * rs
                dn = dsrc[0:HEAD_DIM, :] * mul
                acc[rows, :] += jnp.sum(dn * xhat, axis=1, keepdims=True)
                dxh = dn * gain[rows, :]
                dx = rs * (dxh - xhat * jnp.mean(dxh * xhat, axis=0, keepdims=True))
                du_ref[pl.ds(pl.multiple_of(base + h * HEAD_DIM, HEAD_DIM), HEAD_DIM), :] = dx.astype(BF16)
            return 0

        lax.fori_loop(0, H, head, 0, unroll=HEAD_UNROLL)
        du_ref[2 * D:3 * D, :] = dv_ref[...]
        dlf, carry = _lane_cumsum(dc_scr[...], tri_ref, fcar[...], True)
        fcar[...] = carry
        dfl = dlf * _sigmoid(-(f_ref[...] + bf_ref[...]))
        dbf_ref[...] += jnp.sum(dfl, axis=1, keepdims=True)
        du_ref[3 * D:NU, :] = dfl.astype(BF16)

        @pl.when(step == nT - 1)
        def _():
            for acc, ref in ((gq_acc, dqg_ref), (gk_acc, dkg_ref)):
                tot = jnp.zeros((HEAD_DIM, 1), F32)
                for h in range(H):
                    tot = tot + acc[h * HEAD_DIM:(h + 1) * HEAD_DIM, :]
                ref[...] = tot

    rev = lambda i: nT - 1 - i
    part = lambda j: pl.BlockSpec((D, T), lambda i: (j, rev(i)))
    colv = lambda n: pl.BlockSpec((n, 1), lambda i: (0, 0))
    return pl.pallas_call(
        body, name="fox_prep_bwd", grid=(nT,),
        in_specs=[part(0), part(1), pl.BlockSpec((LANES, T), lambda i: (3 * D // LANES, rev(i))),
                  pl.BlockSpec((H, T, HEAD_ROWS), lambda i: (0, rev(i), 0)),
                  pl.BlockSpec((H, HEAD_ROWS, T), lambda i: (0, 0, rev(i))), pl.BlockSpec((D, T), lambda i: (0, rev(i))),
                  colv(LANES), colv(D), colv(D), pl.BlockSpec((LANES, LANES), lambda i: (0, 0))],
        out_specs=[pl.BlockSpec((NU, T), lambda i: (0, rev(i))), colv(LANES), colv(HEAD_DIM), colv(HEAD_DIM)],
        out_shape=[jax.ShapeDtypeStruct((NU, S), BF16), jax.ShapeDtypeStruct((LANES, 1), F32),
                   jax.ShapeDtypeStruct((HEAD_DIM, 1), F32), jax.ShapeDtypeStruct((HEAD_DIM, 1), F32)],
        scratch_shapes=[pltpu.VMEM((D, 1), F32), pltpu.VMEM((D, 1), F32), pltpu.VMEM((LANES, 1), F32),
                        pltpu.VMEM((LANES, T), F32)],
        compiler_params=_params(("arbitrary",)),
    )(ut, ut, ut, dq, dkt, dvt, b_f, qg, kg, _tri_matrix(True))


def _block_diag_tiles(w):
    n = w.shape[0]
    per = min(MXU_DIM, n * LRU_BLOCK_DIM) // LRU_BLOCK_DIM
    eye = jnp.eye(per, dtype=w.dtype)
    w5 = w.reshape(n // per, per, LRU_BLOCK_DIM, 1, LRU_BLOCK_DIM) * eye[None, :, None, :, None]
    return w5.reshape(n // per, per * LRU_BLOCK_DIM, per * LRU_BLOCK_DIM).astype(BF16)


def _block_diag_extract(t, n):
    per = t.shape[-1] // LRU_BLOCK_DIM
    eye = jnp.eye(per, dtype=t.dtype)
    t5 = t.reshape(n // per, per, LRU_BLOCK_DIM, per, LRU_BLOCK_DIM) * eye[None, :, None, :, None]
    return t5.sum(axis=3).reshape(n, LRU_BLOCK_DIM, LRU_BLOCK_DIM)


def _local_step(x, tgt, small, wv, grad_view, comm=None):
    S, D = x.shape
    F = 4 * D
    H = D // HEAD_DIM
    nblk = D // LRU_BLOCK_DIM
    NU = 3 * D + LANES
    tq = max(LANES, min(512, S // 4))
    assert S % tq == 0
    vec = lambda a: a.reshape(1, -1).astype(F32)
    col = lambda a: a.reshape(-1, 1).astype(F32)
    mix_g, mlp_g = small["mix_norm"], small["mlp_norm"]
    conv_b = vec(small["lru_conv_b"])
    wr_bd, wi_bd = _block_diag_tiles(small["lru_w_r"][0]), _block_diag_tiles(small["lru_w_i"][0])
    b_r, b_i, lam = vec(small["lru_b_r"]), vec(small["lru_b_i"]), vec(small["lru_lambda"])
    b_f = jnp.pad(col(small["fox_b_f"]), ((0, LANES - H), (0, 0)))
    qg, kg = jnp.tile(col(small["fox_q_gain"]), (H, 1)), jnp.tile(col(small["fox_k_gain"]), (H, 1))
    X = lambda a: _View(a)
    grads = {}
    gout = functools.partial(grad_view, grads)

    def hosted(name, fn, *args):
        plan = comm.before(name, grads) if comm is not None else None
        res, side = fn(*args, plan=plan)
        if plan is not None:
            comm.after(name, side, wv)
        return res

    def hosted_mm(name, *args, **kw):
        plan = comm.before(name, grads) if comm is not None else None
        if plan is None:
            return _matmul(name, *args, **kw)
        res, side = _matmul(name, *args, plan=plan, **kw)
        comm.after(name, side, wv)
        return res

    two = lambda: [_fresh(S, D, F32), _fresh(S, D, BF16)]

    def mlp_up(l, hm):
        return hosted_mm(f"mlp{l}_up", X(hm), wv[f"w1_{l}"], S, F, D, outs=[_fresh(S, F, BF16)], epilogue=_ep_relu2)[0]

    def mlp_bwd(l, xin, hm, act, d, db):
        (dz,) = hosted_mm(f"mlp{l}_dact", X(db), wv[f"w2_{l}"], S, F, D, tb=True, outs=[_fresh(S, F, BF16)],
                          epilogue=_ep_drelu2, extras=[X(act)])
        (grads[f"w2_{l}"],) = _matmul(f"mlp{l}_dw2", X(act), X(db), F, D, S, ta=True, outs=[gout(f"w2_{l}")],
                                      epilogue=_ep_store)
        (grads[f"w1_{l}"],) = _matmul(f"mlp{l}_dw1", X(hm), X(dz), D, F, S, ta=True, outs=[gout(f"w1_{l}")],
                                      epilogue=_ep_store)
        return _matmul(f"mlp{l}_dhm", X(dz), wv[f"w1_{l}"], S, D, F, tb=True, outs=two(), n_sums=1,
                       epilogue=_ep_norm_bwd, extras=[X(xin), X(d)], vecs=[mlp_g[l:l + 1]])

    (h0,) = hosted("mix0_norm", _rms_fwd, "mix0_norm", x, mix_g[0:1], S, D)
    (u0,) = hosted_mm("lru_in", X(h0), wv["lru_in"], S, 2 * D, D, outs=[_fresh(S, 2 * D, F32)], epilogue=_ep_store)
    conv_w = small["conv_w"]
    y, xc, r, ig, hs = hosted("lru_fwd", _lru_fwd, u0, conv_w, conv_b, wr_bd, b_r, wi_bd, b_i, lam, S, D)
    x1, hm0 = _matmul("lru_out", X(y), wv["lru_out"], S, D, D, outs=two(), epilogue=_ep_resid_norm, extras=[X(x)],
                      vecs=[mlp_g[0:1]])
    act0 = mlp_up(0, hm0)
    x2, h1 = hosted_mm("mlp0_down", X(act0), wv["w2_0"], S, D, F, outs=two(), epilogue=_ep_resid_norm, extras=[X(x1)],
                       vecs=[mix_g[1:2]])
    (u1,) = _matmul("fox_in", wv["fox_in"], X(h1), NU, S, D, tb=True, outs=[_fresh(NU, S, F32)], epilogue=_ep_store)
    qat, kat, vat, ka = _fox_prep(u1, b_f, qg, kg, S, D, tq)
    o, o32, lse = hosted("attn_forward", _attn_forward, ka, qat, vat, S, D, tq)
    x3, hm1 = _matmul("fox_out", X(o), wv["fox_out"], S, D, D, ta=True, outs=two(), epilogue=_ep_resid_norm,
                      extras=[X(x2)], vecs=[mlp_g[1:2]])
    act1 = mlp_up(1, hm1)
    (x4,) = _matmul("mlp1_down", X(act1), wv["w2_1"], S, D, F, outs=[_fresh(S, D, F32)], epilogue=_ep_resid,
                    extras=[X(x3)])
    loss, d4, d4b = _loss_head(x4, tgt, S, D)

    d3, d3b, dg_mlp1 = mlp_bwd(1, x3, hm1, act1, d4, d4b)
    (do,) = _matmul("fox_dout", wv["fox_out"], X(d3b), D, S, D, tb=True, outs=[_fresh(D, S, BF16)], epilogue=_ep_store)
    (grads["fox_out"],) = _matmul("fox_dwout", X(o), X(d3b), D, D, S, outs=[gout("fox_out")], epilogue=_ep_store)
    doat, doa, qat1, qa1 = hosted("fox_bwd_prep", _fox_bwd_prep, do, o32, lse, qat, S, D, tq)
    dqn, dkn, dv = hosted("attn_backward", _attn_backward, qa1, doa, qat1, doat, ka, kat, vat, S, D, tq)
    du1, dbf, dqg, dkg = _fox_prep_bwd(u1, dqn, dkn, dv, b_f, qg, kg, S, D, tq)
    (grads["fox_in"],) = _matmul("fox_dwin", X(du1), X(h1), NU, D, S, outs=[gout("fox_in")], epilogue=_ep_store)
    d2, d2b, dg_mix1 = hosted_mm("fox_dh", X(du1), wv["fox_in"], S, D, NU, ta=True, outs=two(), n_sums=1,
                               epilogue=_ep_norm_bwd, extras=[X(x2), X(d3)], vecs=[mix_g[1:2]])
    d1, d1b, dg_mlp0 = mlp_bwd(0, x1, hm0, act0, d2, d2b)
    (grads["lru_out"],) = _matmul("lru_dwout", X(y), X(d1b), D, D, S, ta=True, outs=[gout("lru_out")],
                                  epilogue=_ep_store)
    (dy,) = hosted_mm("lru_dout", X(d1b), wv["lru_out"], S, D, D, tb=True, outs=[_fresh(S, D, F32)],
                      epilogue=_ep_store)
    du0, dcw, dcb, dlam, dbr, dbi, dwr, dwi = hosted("lru_bwd", _lru_bwd, dy, u0, xc, r, ig, hs, conv_w, wr_bd, wi_bd,
                                                     lam, S, D)
    (grads["lru_in"],) = _matmul("lru_dwin", X(h0), X(du0), D, 2 * D, S, ta=True, outs=[gout("lru_in")],
                                 epilogue=_ep_store)
    gx, dg_mix0 = hosted_mm("lru_dh", X(du0), wv["lru_in"], S, D, 2 * D, tb=True, outs=[_fresh(S, D, F32)], n_sums=1,
                            epilogue=lambda *a: _ep_norm_bwd(*a)[::2], extras=[X(x), X(d1)], vecs=[mix_g[0:1]])

    grads.update(
        mix_norm=jnp.concatenate([dg_mix0, dg_mix1], axis=0), mlp_norm=jnp.concatenate([dg_mlp0, dg_mlp1], axis=0),
        conv_w=dcw, lru_conv_b=dcb, lru_w_r=_block_diag_extract(dwr, nblk)[None], lru_b_r=dbr.reshape(1, nblk, -1),
        lru_w_i=_block_diag_extract(dwi, nblk)[None], lru_b_i=dbi.reshape(1, nblk, -1), lru_lambda=dlam,
        fox_b_f=dbf[:H].reshape(1, H), fox_q_gain=dqg.reshape(1, -1), fox_k_gain=dkg.reshape(1, -1))
    return loss, gx, grads


def _place():
    x, y, c = lax.axis_index("x"), lax.axis_index("y"), lax.axis_index("c")
    chips = [(1 - x, y), (x, 1 - y), (1 - x, 1 - y)]
    return x, y, c, 2 * x + y, chips


BOUNCE_BYTES = 1 << 20


def _bounce_shape(rows, cols, dtype):
    chunk = rows
    while chunk % 2 == 0 and chunk > 16 and chunk * cols * jnp.dtype(dtype).itemsize > BOUNCE_BYTES:
        chunk //= 2
    return pltpu.VMEM((2, chunk, cols), dtype)


def _bounce_copy(src, dst, buf, sem):
    chunk = buf.shape[1]
    n = src.shape[0] // chunk
    cin = lambda i: pltpu.make_async_copy(src.at[pl.ds(i * chunk, chunk)], buf.at[i % 2], sem.at[i % 2])
    cout = lambda i: pltpu.make_async_copy(buf.at[i % 2], dst.at[pl.ds(i * chunk, chunk)], sem.at[2 + i % 2])
    cin(0).start()
    for i in range(n):
        cin(i).wait()
        if i + 1 < n:
            if i >= 1:
                cout(i - 1).wait()
            cin(i + 1).start()
        cout(i).start()
    if n >= 2:
        cout(n - 2).wait()
    cout(n - 1).wait()


def _hbm_call(body, name, arrays, out_shape, n_dma_sems, bounce=()):
    scratch = [pltpu.SemaphoreType.DMA((k,)) for k in n_dma_sems]
    for rows, cols, dtype in bounce:
        scratch += [_bounce_shape(rows, cols, dtype), pltpu.SemaphoreType.DMA((4,))]
    return pl.pallas_call(
        body, name=name, in_specs=[ANY] * len(arrays), out_specs=[ANY] * len(out_shape), out_shape=out_shape,
        scratch_shapes=scratch,
        compiler_params=pltpu.CompilerParams(has_side_effects=True, vmem_limit_bytes=VMEM_LIMIT),
    )(*arrays)


class _Gather:
    def __init__(self, shards):
        n = self.n = len(shards)
        self.operands = list(shards)
        self.out_shape = [jax.ShapeDtypeStruct((N_CHIPS,) + tuple(a.shape), a.dtype) for a in shards]
        self.scratch = [pltpu.SemaphoreType.DMA((3 * n,)) for _ in range(4)]
        for a in shards:
            self.scratch += [_bounce_shape(a.shape[0], a.shape[1], a.dtype), pltpu.SemaphoreType.DMA((4,))]

    def _copies(self, ins, outs, scr):
        send, recv, fsend, frecv = scr[:4]
        x, y, c, s, chips = _place()

        def rows(a, chip_idx, which):
            hr = ins[a].shape[0] // 2
            return outs[a].at[chip_idx, pl.ds(which * hr, hr)]

        def landed(a, j, core):
            return rows(a, 2 * chips[j][0] + chips[j][1], core)

        def ici(a, j, mine):
            hr = ins[a].shape[0] // 2
            src, dst = (ins[a].at[pl.ds(c * hr, hr)], rows(a, s, c)) if mine else (landed(a, j, c),) * 2
            return pltpu.make_async_remote_copy(src_ref=src, dst_ref=dst, send_sem=send.at[3 * a + j],
                                                recv_sem=recv.at[3 * a + j], device_id=(*chips[j], c),
                                                device_id_type=MESH)

        def d2d(a, j, mine):
            ref = landed(a, j, c if mine else 1 - c)
            return pltpu.make_async_remote_copy(src_ref=ref, dst_ref=ref, send_sem=fsend.at[3 * a + j],
                                                recv_sem=frecv.at[3 * a + j], device_id=(x, y, 1 - c),
                                                device_id_type=MESH)

        return ici, d2d, s

    def start(self, ins, outs, scr):
        ici, _, _ = self._copies(ins, outs, scr)
        for a in range(self.n):
            for j in range(3):
                ici(a, j, True).start()

    def middle(self, ins, outs, scr):
        ici, d2d, s = self._copies(ins, outs, scr)
        for a in range(self.n):
            _bounce_copy(ins[a], outs[a].at[s], scr[4 + 2 * a], scr[5 + 2 * a])
        for a in range(self.n):
            for j in range(3):
                ici(a, j, False).wait_recv()
                d2d(a, j, True).start()

    def finish(self, ins, outs, scr):
        ici, d2d, _ = self._copies(ins, outs, scr)
        for a in range(self.n):
            for j in range(3):
                d2d(a, j, False).wait_recv()
        for a in range(self.n):
            for j in range(3):
                ici(a, j, True).wait_send()
                d2d(a, j, True).wait_send()


def _run_plan(name, plan):
    k_in, k_out = len(plan.operands), len(plan.out_shape)

    def body(*refs):
        parts = (refs[:k_in], refs[k_in:k_in + k_out], refs[k_in + k_out:])
        plan.start(*parts)
        plan.middle(*parts)
        plan.finish(*parts)

    return pl.pallas_call(
        body, name=name, in_specs=[ANY] * k_in, out_specs=[ANY] * k_out, out_shape=plan.out_shape,
        scratch_shapes=plan.scratch,
        compiler_params=pltpu.CompilerParams(has_side_effects=True, vmem_limit_bytes=VMEM_LIMIT),
    )(*plan.operands)


def _hosted_call(body, name, grid, in_specs, out_specs, out_shape, scratch_shapes, operands, sem, plan=None):
    if plan is None:
        res = pl.pallas_call(body, name=name, grid=grid, in_specs=in_specs, out_specs=out_specs, out_shape=out_shape,
                             scratch_shapes=scratch_shapes, compiler_params=_params(sem))(*operands)
        return res, None
    n_in, n_out, n_scr = len(in_specs), len(out_specs), len(scratch_shapes)
    k_in, k_out = len(plan.operands), len(plan.out_shape)
    total = int(np.prod(grid))
    late = max(0, total - 1 - max(1, total // 8))

    def hosted(*refs):
        ins, refs = refs[:n_in], refs[n_in:]
        p_ins, refs = refs[:k_in], refs[k_in:]
        outs, refs = refs[:n_out], refs[n_out:]
        p_outs, refs = refs[:k_out], refs[k_out:]
        scr, p_scr = refs[:n_scr], refs[n_scr:]
        step = pl.program_id(0)
        for d in range(1, len(grid)):
            step = step * grid[d] + pl.program_id(d)
        pl.when(step == 0)(lambda: plan.start(p_ins, p_outs, p_scr))
        body(*ins, *outs, *scr)
        pl.when(step == late)(lambda: plan.middle(p_ins, p_outs, p_scr))
        pl.when(step == total - 1)(lambda: plan.finish(p_ins, p_outs, p_scr))

    res = pl.pallas_call(
        hosted, name=name, grid=grid, in_specs=list(in_specs) + [ANY] * k_in, out_specs=list(out_specs) + [ANY] * k_out,
        out_shape=list(out_shape) + plan.out_shape, scratch_shapes=list(scratch_shapes) + plan.scratch,
        compiler_params=pltpu.CompilerParams(dimension_semantics=sem, vmem_limit_bytes=VMEM_LIMIT,
                                             has_side_effects=True),
    )(*operands, *plan.operands)
    return res[:n_out], res[n_out:]


def _all_gather(name, shards):
    return _run_plan(name, _Gather(shards))


class _Swap:
    def __init__(self, arrs):
        self.n = len(arrs)
        self.operands = list(arrs)
        self.out_shape = [jax.ShapeDtypeStruct((a.shape[0], a.shape[1] // 2, a.shape[2]), a.dtype) for a in arrs]
        self.scratch = [pltpu.SemaphoreType.DMA((self.n,)) for _ in range(2)]

    def _copy(self, ins, outs, scr, a):
        x, y, c, _, _ = _place()
        hr = ins[a].shape[1] // 2
        return pltpu.make_async_remote_copy(
            src_ref=ins[a].at[:, pl.ds((1 - c) * hr, hr)], dst_ref=outs[a], send_sem=scr[0].at[a],
            recv_sem=scr[1].at[a], device_id=(x, y, 1 - c), device_id_type=MESH)

    def start(self, ins, outs, scr):
        for a in range(self.n):
            self._copy(ins, outs, scr, a).start()

    def middle(self, ins, outs, scr):
        pass

    def finish(self, ins, outs, scr):
        for a in range(self.n):
            self._copy(ins, outs, scr, a).wait()


class _Scatter:
    def __init__(self, parts):
        n = self.n = len(parts)
        self.operands = list(parts)
        self.out_shape = [jax.ShapeDtypeStruct(a.shape, a.dtype) for a in parts]
        self.scratch = [pltpu.SemaphoreType.DMA((3 * n,)) for _ in range(2)]
        for a in parts:
            self.scratch += [_bounce_shape(a.shape[1], a.shape[2], a.dtype), pltpu.SemaphoreType.DMA((4,))]

    def _copy(self, ins, outs, scr, a, j, mine):
        x, y, c, s, chips = _place()
        t = 2 * chips[j][0] + chips[j][1]
        return pltpu.make_async_remote_copy(
            src_ref=ins[a].at[t], dst_ref=outs[a].at[s if mine else t], send_sem=scr[0].at[3 * a + j],
            recv_sem=scr[1].at[3 * a + j], device_id=(*chips[j], c), device_id_type=MESH)

    def start(self, ins, outs, scr):
        for a in range(self.n):
            for j in range(3):
                self._copy(ins, outs, scr, a, j, True).start()

    def middle(self, ins, outs, scr):
        s = _place()[3]
        for a in range(self.n):
            _bounce_copy(ins[a].at[s], outs[a].at[s], scr[2 + 2 * a], scr[3 + 2 * a])

    def finish(self, ins, outs, scr):
        for a in range(self.n):
            for j in range(3):
                self._copy(ins, outs, scr, a, j, False).wait_recv()
        for a in range(self.n):
            for j in range(3):
                self._copy(ins, outs, scr, a, j, True).wait_send()


def _pair_gather(name, halves):
    n = len(halves)

    def body(*refs):
        ins, outs = refs[:n], refs[n:2 * n]
        send, recv = refs[2 * n:2 * n + 2]
        stage = refs[2 * n + 2:]
        x, y, c, _, _ = _place()
        cps = []
        for a in range(n):
            hr = ins[a].shape[0]
            cp = pltpu.make_async_remote_copy(
                src_ref=ins[a], dst_ref=outs[a].at[pl.ds(c * hr, hr)], send_sem=send.at[a], recv_sem=recv.at[a],
                device_id=(x, y, 1 - c), device_id_type=MESH)
            cp.start()
            cps.append((cp, hr))
        for a, (cp, hr) in enumerate(cps):
            _bounce_copy(ins[a], outs[a].at[pl.ds(c * hr, hr)], stage[2 * a], stage[2 * a + 1])
        for a, (cp, hr) in enumerate(cps):
            cp.wait_send()
            theirs = outs[a].at[pl.ds((1 - c) * hr, hr)]
            pltpu.make_async_remote_copy(src_ref=theirs, dst_ref=theirs, send_sem=send.at[a], recv_sem=recv.at[a],
                                         device_id=(x, y, 1 - c), device_id_type=MESH).wait_recv()

    out_shape = [jax.ShapeDtypeStruct((2 * a.shape[0], a.shape[1]), a.dtype) for a in halves]
    return _hbm_call(body, name, halves, out_shape, (n, n),
                     bounce=[(a.shape[0], a.shape[1], a.dtype) for a in halves])


def _row_tile(rows, cols, itemsize, n_bufs):
    budget = VMEM_LIMIT // 2
    for t in range(min(rows, 1024) // 16 * 16, 0, -16):
        if rows % t == 0 and 2 * n_bufs * t * cols * itemsize <= budget:
            return t
    return rows


def _pair_add(name, g, gsib, core, out_dtype):
    _, r, cols = g.shape
    hr = r // 2
    t = _row_tile(hr, cols, 4, 3)
    per = hr // t

    def body(core_ref, a_ref, b_ref, o_ref):
        o_ref[...] = (a_ref[...].astype(F32) + b_ref[...].astype(F32)).astype(o_ref.dtype)

    grid_spec = pltpu.PrefetchScalarGridSpec(
        num_scalar_prefetch=1, grid=(N_CHIPS, per),
        in_specs=[pl.BlockSpec((None, t, cols), lambda s, i, core: (s, core[0] * per + i, 0)),
                  pl.BlockSpec((None, t, cols), lambda s, i, core: (s, i, 0))],
        out_specs=pl.BlockSpec((None, t, cols), lambda s, i, core: (s, i, 0)))
    return pl.pallas_call(body, name=name, grid_spec=grid_spec,
                          out_shape=jax.ShapeDtypeStruct((N_CHIPS, hr, cols), out_dtype),
                          compiler_params=_params(("arbitrary", "arbitrary")))(core, g, gsib)


def _chip_sum(name, parts):
    _, hr, cols = parts.shape
    t = _row_tile(hr, cols, 4, 5)

    def body(p_ref, o_ref):
        o_ref[...] = ((p_ref[0].astype(F32) + p_ref[1].astype(F32)) + p_ref[2].astype(F32)) + p_ref[3].astype(F32)

    return pl.pallas_call(
        body, name=name, grid=(hr // t,), in_specs=[pl.BlockSpec((N_CHIPS, t, cols), lambda i: (0, i, 0))],
        out_specs=pl.BlockSpec((t, cols), lambda i: (i, 0)), out_shape=jax.ShapeDtypeStruct((hr, cols), F32),
        compiler_params=_params(("arbitrary",)))(parts)


def _pair_partials(tag, arrs, sib, wire_dtypes, core):
    return _Scatter([_pair_add(f"{tag}_pair_add{i}", g, gs, core, dt)
                     for i, (g, gs, dt) in enumerate(zip(arrs, sib, wire_dtypes))])


def _finish_reduce(tag, scattered):
    halves = [_chip_sum(f"{tag}_chip_sum{i}", p) for i, p in enumerate(scattered)]
    return _pair_gather(f"{tag}_pair_gather", halves)


def _adamw(name, w, g_parts, m, v):
    thin = w.ndim == 3
    rows, cols = w.shape[0], w.shape[-1]
    n_parts = len(g_parts)
    part_rows = rows // n_parts
    t = max(d for d in range(1, 257) if part_rows % d == 0) if thin else _row_tile(part_rows, cols, 4, 7 + n_parts)
    per = part_rows // t
    c1 = 1.0 - ADAM_B1 ** ADAM_STEP
    c2 = 1.0 - ADAM_B2 ** ADAM_STEP

    def body(w_ref, m_ref, v_ref, *refs):
        g_refs, (go_ref, d_ref, nm_ref, nv_ref) = refs[:n_parts], refs[n_parts:]
        g = g_refs[0][...]
        for k in range(1, n_parts):
            g = jnp.where(pl.program_id(0) >= k * per, g_refs[k][...], g)
        go_ref[...] = g
        m = ADAM_B1 * m_ref[...] + (1.0 - ADAM_B1) * g
        v = ADAM_B2 * v_ref[...] + (1.0 - ADAM_B2) * (g * g)
        nm_ref[...] = m
        nv_ref[...] = v
        d_ref[...] = -ADAM_LR * ((m / c1) / (jnp.sqrt(v / c2) + ADAM_EPS) + ADAM_WD * w_ref[...])

    block = (t, 1, cols) if thin else (t, cols)
    at = lambda r: (r, 0, 0) if thin else (r, 0)
    spec = pl.BlockSpec(block, lambda i: at(i))
    g_specs = [pl.BlockSpec(block, lambda i, k=k: at(jnp.clip(i - k * per, 0, per - 1))) for k in range(n_parts)]
    shp = jax.ShapeDtypeStruct(w.shape, F32)
    return pl.pallas_call(body, name=name, grid=(rows // t,), in_specs=[spec] * 3 + g_specs, out_specs=[spec] * 4,
                          out_shape=[shp] * 4, compiler_params=_params(("arbitrary",)))(w, m, v, *g_parts)


_WEIGHTS = ["mix_norm", "mlp_norm", "mlp_w1", "mlp_w2", "lru_w_in", "lru_conv_w", "lru_conv_b", "lru_w_r", "lru_b_r",
            "lru_w_i", "lru_b_i", "lru_lambda", "lru_w_out", "fox_w_in", "fox_b_f", "fox_q_gain", "fox_k_gain",
            "fox_w_out"]
_REPLICATED = ["mix_norm", "mlp_norm", "lru_conv_b", "lru_w_r", "lru_b_r", "lru_w_i", "lru_b_i", "lru_lambda",
               "fox_b_f", "fox_q_gain", "fox_k_gain"]
_PACK_TILE = 2 * SUBLANES * LANES


def _as2d(a):
    return a.reshape(-1, a.shape[-1])


def kernel(x, mix_norm, mlp_norm, mlp_w1, mlp_w2, lru_w_in, lru_conv_w, lru_conv_b, lru_w_r, lru_b_r, lru_w_i, lru_b_i, lru_lambda, lru_w_out, fox_w_in, fox_b_f, fox_q_gain, fox_k_gain, fox_w_out, loss_target, m_mix_norm, m_mlp_norm, m_mlp_w1, m_mlp_w2, m_lru_w_in, m_lru_conv_w, m_lru_conv_b, m_lru_w_r, m_lru_b_r, m_lru_w_i, m_lru_b_i, m_lru_lambda, m_lru_w_out, m_fox_w_in, m_fox_b_f, m_fox_q_gain, m_fox_k_gain, m_fox_w_out, v_mix_norm, v_mlp_norm, v_mlp_w1, v_mlp_w2, v_lru_w_in, v_lru_conv_w, v_lru_conv_b, v_lru_w_r, v_lru_b_r, v_lru_w_i, v_lru_b_i, v_lru_lambda, v_lru_w_out, v_fox_w_in, v_fox_b_f, v_fox_q_gain, v_fox_k_gain, v_fox_w_out):
    args = dict(locals())
    W = {n: args[n] for n in _WEIGHTS}
    Mo = {n: args["m_" + n] for n in _WEIGHTS}
    Vo = {n: args["v_" + n] for n in _WEIGHTS}
    S, D = x.shape[1], x.shape[2]
    F = 4 * D
    H = D // HEAD_DIM
    NU = 3 * D + LANES
    FQ, DQ = F // N_CHIPS, D // N_CHIPS
    nfox = fox_w_in.shape[-1]
    chip = 2 * lax.axis_index("x") + lax.axis_index("y")
    core = lax.axis_index("c").astype(jnp.int32).reshape(1)

    cw_flat = jnp.pad(lru_conv_w.reshape(-1), (0, _PACK_TILE - CONV_WIDTH * DQ)).reshape(2 * SUBLANES, LANES)
    w1s, w2s = mlp_w1.astype(BF16), mlp_w2.astype(BF16)
    wv = {}
    small = {n: W[n] for n in _REPLICATED}
    scattered = {}
    members = {"g1": ["w2_1", "w1_1", "fox_out"], "g2": ["fox_in"], "g3": ["w2_0", "w1_0"], "g4": ["lru_out", "lru_in"]}
    swap_at = {"fox_bwd_prep": "g1", "fox_dh": "g2", "lru_dout": "g3"}
    scatter_at = {"attn_backward": "g1", "mlp0_dact": "g2", "lru_bwd": "g3", "lru_dh": "g4"}
    swapped = {}

    fox_rows = -(-nfox // (4 * SUBLANES)) * (4 * SUBLANES)
    fox_t = jnp.pad(jnp.transpose(fox_w_in[0]).astype(BF16), ((0, fox_rows - nfox), (0, 0)))

    def shard_major(name, g):
        if name == "fox_in":
            return jnp.pad(g[:nfox * N_CHIPS].reshape(N_CHIPS, nfox, D), ((0, 0), (0, fox_rows - nfox), (0, 0)))
        return g

    class Comm:
        @staticmethod
        def before(name, grads):
            if name == "mix0_norm":
                return _Gather([lru_w_in[0].astype(BF16)])
            if name == "lru_in":
                return _Gather([lru_w_out[0].astype(BF16), cw_flat])
            if name == "lru_fwd":
                return _Gather([w1s[0]])
            if name == "mlp0_up":
                return _Gather([w2s[0]])
            if name == "mlp0_down":
                return _Gather([fox_t])
            if name == "attn_forward":
                return _Gather([fox_w_out[0].astype(BF16), w1s[1], w2s[1]])
            if name in swap_at:
                group = swap_at[name]
                swapped[group] = [[shard_major(n, grads[n]) for n in members[group]], None]
                return _Swap(swapped[group][0])
            if name in scatter_at:
                group = scatter_at[name]
                if group not in swapped:
                    arrs = [shard_major(n, grads[n]) for n in members[group]]
                    swapped[group] = [arrs, _run_plan(f"{group}_pair_swap", _Swap(arrs))]
                arrs, sib = swapped[group]
                return _pair_partials(group, arrs, sib, [BF16] * len(arrs), core)
            return None

        @staticmethod
        def after(name, res, wv):
            if name == "mix0_norm":
                wv.update(lru_in=_View(res[0], "cs"))
            elif name == "lru_in":
                wv.update(lru_out=_View(res[0], "rs"))
                taps = res[1].reshape(N_CHIPS, -1)[:, :CONV_WIDTH * DQ].reshape(N_CHIPS, CONV_WIDTH, DQ)
                small["conv_w"] = jnp.transpose(taps, (1, 0, 2)).reshape(CONV_WIDTH, D)
            elif name == "lru_fwd":
                wv.update(w1_0=_View(res[0], "cs"))
            elif name == "mlp0_up":
                wv.update(w2_0=_View(res[0], "rs"))
            elif name == "mlp0_down":
                fox_full = jnp.concatenate([res[0][s, :nfox] for s in range(N_CHIPS)], axis=0)
                wv.update(fox_in=_View(jnp.pad(fox_full, ((0, NU - fox_full.shape[0]), (0, 0)))))
            elif name == "attn_forward":
                wv.update(fox_out=_View(res[0], "rs"), w1_1=_View(res[1], "cs"), w2_1=_View(res[2], "rs"))
            elif name in swap_at:
                swapped[swap_at[name]][1] = res
            else:
                scattered.update(zip(members[scatter_at[name]], res))

    def grad_view(grads, name):
        if name in ("w1_0", "w1_1"):
            return _View(None, "cs", shape=(N_CHIPS, D, FQ), dtype=BF16)
        if name in ("w2_0", "w2_1"):
            return _View(None, "rs", shape=(N_CHIPS, FQ, D), dtype=BF16)
        if name == "lru_in":
            return _View(None, "cs", shape=(N_CHIPS, D, 2 * D // N_CHIPS), dtype=BF16)
        if name in ("lru_out", "fox_out"):
            return _View(None, "rs", shape=(N_CHIPS, DQ, D), dtype=BF16)
        return _View(None, shape=(NU, D), dtype=BF16)

    loss, gx, grads = _local_step(x[0], loss_target[0], small, wv, grad_view, Comm)

    pack_names = _REPLICATED + ["conv_w"]
    flat = jnp.concatenate([grads[n].reshape(-1).astype(F32) for n in pack_names] + [loss.reshape(-1)])
    per_chip = -(-flat.shape[0] // (N_CHIPS * _PACK_TILE)) * _PACK_TILE
    pack = jnp.pad(flat, (0, N_CHIPS * per_chip - flat.shape[0])).reshape(N_CHIPS, per_chip // LANES, LANES)
    pack_sib = _run_plan("pack_pair_swap", _Swap([pack]))
    (scattered["pack"],) = _run_plan("pack_chip_scatter", _pair_partials("pack", [pack], pack_sib, [F32], core))
    order = ["w1_0", "w1_1", "w2_0", "w2_1", "lru_in", "lru_out", "fox_in", "fox_out", "pack"]
    red = dict(zip(order, _finish_reduce("grads", [scattered[n] for n in order])))
    (all_pack,) = _all_gather("gather_small_grads", [red["pack"]])
    all_flat = all_pack.reshape(-1)
    G = {}
    off = 0
    for n in pack_names:
        shape = grads[n].shape if n == "conv_w" else W[n].shape
        size = int(np.prod(shape))
        G[n] = all_flat[off:off + size].reshape(shape)
        off += size
    total = all_flat[off]
    G["lru_conv_w"] = lax.dynamic_slice_in_dim(G.pop("conv_w"), chip * DQ, DQ, axis=1)[None]
    parts = {n: [_as2d(G[n])] for n in G}
    parts.update(mlp_w1=[red["w1_0"], red["w1_1"]], mlp_w2=[red["w2_0"], red["w2_1"]], lru_w_in=[red["lru_in"]],
                 lru_w_out=[red["lru_out"]], fox_w_in=[red["fox_in"][:nfox, None, :]], fox_w_out=[red["fox_out"]])

    delta, new_m, new_v = {}, {}, {}
    for n in _WEIGHTS:
        if n == "fox_w_in":
            to_thin = lambda a: jnp.transpose(a, (2, 0, 1))
            res = _adamw(f"adamw_{n}", to_thin(W[n]), parts[n], to_thin(Mo[n]), to_thin(Vo[n]))
            G[n], delta[n], new_m[n], new_v[n] = (jnp.transpose(t, (1, 2, 0)) for t in res)
            continue
        go, d, nm, nv = _adamw(f"adamw_{n}", _as2d(W[n]), parts[n], _as2d(Mo[n]), _as2d(Vo[n]))
        G[n], delta[n], new_m[n], new_v[n] = (t.reshape(W[n].shape) for t in (go, d, nm, nv))

    return (total, gx[None], *[G[n] for n in _WEIGHTS], *[delta[n] for n in _WEIGHTS],
            *[new_m[n] for n in _WEIGHTS], *[new_v[n] for n in _WEIGHTS])
```

```python
import functools

import numpy as np
import jax
import jax.numpy as jnp
from jax import lax
from jax.experimental import pallas as pl
from jax.experimental.pallas import tpu as pltpu

F32 = jnp.float32
BF16 = jnp.bfloat16

HEAD_DIM = 64
LRU_BLOCK_DIM = 64
CONV_WIDTH = 4
LRU_C = 8.0
EPS = 1e-6
NEG_INF = -1e30
ADAM_LR = 0.001
ADAM_B1 = 0.9
ADAM_B2 = 0.999
ADAM_EPS = 1e-08
ADAM_WD = 0.01
ADAM_STEP = 10

N_CHIPS = 4
LANES = 128
SUBLANES = 8
MXU_DIM = 256
VMEM_LIMIT = 52 * 1024 * 1024
MATMUL_TILES = (1024, 640, 512, 256, 128)
MATMUL_VMEM = VMEM_LIMIT * 4 // 5
MESH = pl.DeviceIdType.MESH
ANY = pl.BlockSpec(memory_space=pl.ANY)


def _pick(n, prefs):
    for p in prefs:
        if p <= n and n % p == 0:
            return p
    return n


def _params(sem=None):
    return pltpu.CompilerParams(dimension_semantics=sem, vmem_limit_bytes=VMEM_LIMIT)


class _View:
    def __init__(self, arr, kind="plain", shape=None, dtype=None):
        self.arr = arr
        self.kind = kind
        self.shape = tuple(arr.shape) if arr is not None else tuple(shape)
        self.dtype = arr.dtype if arr is not None else dtype

    def limits(self):
        if self.kind == "plain":
            return 0, 0
        return self.shape[-2], (self.shape[-1] if self.kind == "cs" else 0)

    def spec(self, br, bc, fr, fc):
        if self.kind == "plain":
            return pl.BlockSpec((br, bc), lambda *g: (fr(*g), fc(*g)))
        rows, ncol = self.shape[-2:]
        assert rows % br == 0 and ncol % bc == 0, (self.shape, br, bc)
        if self.kind == "cs":
            per = ncol // bc
            return pl.BlockSpec((None, br, bc), lambda *g: (fc(*g) // per, fr(*g), fc(*g) % per))
        per = rows // br
        return pl.BlockSpec((None, br, bc), lambda *g: (fr(*g) // per, fr(*g) % per, fc(*g)))


def _bf(x):
    return x if x.dtype == BF16 else x.astype(BF16)


def _matmul(name, A, B, M, N, K, *, ta=False, tb=False, outs, epilogue, extras=(), vecs=(), n_sums=0,
            tm=None, tn=None, tk=None, plan=None):
    lim = {"m": [M], "n": [N], "k": [K]}
    for view, (rdim, cdim) in ([(A, "km" if ta else "mk"), (B, "nk" if tb else "kn")]
                               + [(e, "mn") for e in extras] + [(o, "mn") for o in outs]):
        r_lim, c_lim = view.limits()
        lim[rdim].append(r_lim)
        lim[cdim].append(c_lim)
    cap = {d: int(np.gcd.reduce(lim[d])) for d in "mnk"}
    tm = tm or _pick(cap["m"], MATMUL_TILES)
    tn = tn or _pick(cap["n"], MATMUL_TILES)
    tk = tk or _pick(cap["k"], MATMUL_TILES)

    def vmem_bytes(tm, tk):
        size = lambda v: jnp.dtype(v.dtype).itemsize
        tiles = tm * tk * size(A) + tk * tn * size(B) + tm * tn * sum(size(v) for v in list(extras) + list(outs))
        return 2 * tiles + (tm * tn * 4 if K > tk else 0)

    if cap["k"] % (2 * tk) == 0 and vmem_bytes(tm, 2 * tk) <= MATMUL_VMEM:
        tk *= 2
    elif K == tk and cap["m"] % (2 * tm) == 0 and vmem_bytes(2 * tm, tk) <= MATMUL_VMEM:
        tm *= 2
    nk = K // tk
    gi = lambda i, j, k: i
    gj = lambda i, j, k: j
    gk = lambda i, j, k: k
    a_spec = A.spec(tk, tm, gk, gi) if ta else A.spec(tm, tk, gi, gk)
    b_spec = B.spec(tn, tk, gj, gk) if tb else B.spec(tk, tn, gk, gj)
    ca = 0 if ta else 1
    cb = 1 if tb else 0
    ne, no = len(extras) + len(vecs), len(outs)
    assert n_sums == 0 or tn == N
    row_spec = pl.BlockSpec((1, tn), lambda i, j, k: (0, j))
    in_specs = [a_spec, b_spec] + [e.spec(tm, tn, gi, gj) for e in extras] + [row_spec] * len(vecs)
    operands = [A.arr, B.arr] + [e.arr for e in extras] + list(vecs)
    out_specs = [o.spec(tm, tn, gi, gj) for o in outs] + [row_spec] * n_sums
    out_shape = ([jax.ShapeDtypeStruct(o.shape, o.dtype) for o in outs]
                 + [jax.ShapeDtypeStruct((1, N), F32)] * n_sums)

    def body(*refs):
        a_ref, b_ref = refs[0], refs[1]
        ex = refs[2:2 + ne]
        o_refs = refs[2 + ne:2 + ne + no]
        s_refs = refs[2 + ne + no:2 + ne + no + n_sums]
        first_row_tile = pl.program_id(0) == 0

        def prod():
            return lax.dot_general(_bf(a_ref[...]), _bf(b_ref[...]), (((ca,), (cb,)), ((), ())),
                                   preferred_element_type=F32)

        def finish(acc):
            res = epilogue(acc, *[e[...] for e in ex])
            for o_ref, r in zip(o_refs, res[:no]):
                o_ref[...] = r.astype(o_ref.dtype)
            for s_ref, r in zip(s_refs, res[no:]):
                def assign(s_ref=s_ref, r=r):
                    s_ref[...] = r

                def accumulate(s_ref=s_ref, r=r):
                    s_ref[...] += r

                pl.when(first_row_tile)(assign)
                pl.when(jnp.logical_not(first_row_tile))(accumulate)

        if nk == 1:
            finish(prod())
        else:
            acc_ref = refs[-1]
            k = pl.program_id(2)

            @pl.when(k == 0)
            def _():
                acc_ref[...] = jnp.zeros_like(acc_ref)

            acc_ref[...] += prod()

            @pl.when(k == nk - 1)
            def _():
                finish(acc_ref[...])

    res, side = _hosted_call(body, name, (M // tm, N // tn, nk), in_specs, out_specs, out_shape,
                             [pltpu.VMEM((tm, tn), F32)] if nk > 1 else [], operands,
                             ("arbitrary", "arbitrary", "arbitrary"), plan)
    return res if plan is None else (res, side)


def _ep_store(acc):
    return (acc,)


def _ep_resid(acc, res):
    return (res + acc,)


def _ep_resid_norm(acc, res, g):
    xo = res + acc
    r = lax.rsqrt(jnp.mean(xo * xo, axis=-1, keepdims=True) + EPS)
    return (xo, (xo * r) * g)


def _ep_norm_bwd(acc, x, dres, g):
    r = lax.rsqrt(jnp.mean(x * x, axis=-1, keepdims=True) + EPS)
    xhat = x * r
    dxn = acc * g
    tot = dres + r * (dxn - xhat * jnp.mean(dxn * xhat, axis=-1, keepdims=True))
    return (tot, tot, jnp.sum(acc * xhat, axis=0, keepdims=True))


def _ep_relu2(acc):
    zp = jnp.maximum(acc, 0.0)
    return (zp * zp,)


def _ep_drelu2(acc, act):
    return (acc * (2.0 * jnp.sqrt(act.astype(F32))),)


def _fresh(M, N, dtype):
    return _View(None, shape=(M, N), dtype=dtype)


def _rms_fwd(name, x, g, S, D, plan=None):
    T = _pick(S, (512, 256, 128))

    def body(x_ref, g_ref, h_ref):
        x = x_ref[...]
        r = lax.rsqrt(jnp.mean(x * x, axis=-1, keepdims=True) + EPS)
        h_ref[...] = ((x * r) * g_ref[...]).astype(BF16)

    return _hosted_call(body, name, (S // T,),
                        [pl.BlockSpec((T, D), lambda i: (i, 0)), pl.BlockSpec((1, D), lambda i: (0, 0))],
                        [pl.BlockSpec((T, D), lambda i: (i, 0))], [jax.ShapeDtypeStruct((S, D), BF16)], [], (x, g),
                        ("arbitrary",), plan)


def _loss_head(x, tgt, S, D):
    T = _pick(S, (512, 256, 128))

    def body(x_ref, t_ref, loss_ref, d_ref, db_ref):
        @pl.when(pl.program_id(0) == 0)
        def _():
            loss_ref[...] = jnp.zeros_like(loss_ref)

        e = x_ref[...] - t_ref[...]
        loss_ref[...] += 0.5 * jnp.sum(jnp.mean(e * e, axis=-1, keepdims=True), axis=0, keepdims=True)
        d = e * (1.0 / D)
        d_ref[...] = d
        db_ref[...] = d.astype(BF16)

    row = pl.BlockSpec((T, D), lambda i: (i, 0))
    return pl.pallas_call(
        body, name="loss_head", grid=(S // T,), in_specs=[row, row],
        out_specs=[pl.BlockSpec((1, 1), lambda i: (0, 0)), row, row],
        out_shape=[jax.ShapeDtypeStruct((1, 1), F32), jax.ShapeDtypeStruct((S, D), F32),
                   jax.ShapeDtypeStruct((S, D), BF16)],
        compiler_params=_params(("arbitrary",)),
    )(x, tgt)


def _sigmoid(z):
    return 1.0 / (1.0 + jnp.exp(-z))


def _log_sigmoid(z):
    return jnp.minimum(z, 0.0) - jnp.log(1.0 + jnp.exp(-jnp.abs(z)))


_GELU_K = 0.7978845608028654
_GELU_C = 0.044715


def _gelu(x):
    t = jnp.tanh(_GELU_K * (x + _GELU_C * (x * x * x)))
    return 0.5 * x * (1.0 + t)


def _gelu_and_grad(x):
    x2 = x * x
    t = jnp.tanh(_GELU_K * (x + _GELU_C * (x2 * x)))
    g = 0.5 * x * (1.0 + t)
    dg = 0.5 * (1.0 + t) + 0.5 * x * (1.0 - t * t) * (_GELU_K * (1.0 + 3.0 * _GELU_C * x2))
    return g, dg


def _decay_terms(r, ls):
    la = LRU_C * r * ls
    a = jnp.exp(la)
    a2 = a * a
    mult = jnp.sqrt(-jnp.tanh(la) * (a2 + 1.0))
    return a, a2, mult


def _lru_fwd(u0, conv_w, conv_b, wr_bd, b_r, wi_bd, b_i, lam, S, D, plan=None):
    T = _pick(S, (256, 128))
    GT = wr_bd.shape[-1]
    nG = D // GT

    def body(gb_ref, xb_ref, cw_ref, cb_ref, wr_ref, br_ref, wi_ref, bi_ref, lam_ref,
             y_ref, xc_ref, r_ref, i_ref, hs_ref, ext, a_scr, hcar):
        @pl.when(pl.program_id(0) == 0)
        def _():
            ext[0:SUBLANES, :] = jnp.zeros((SUBLANES, D), F32)
            hcar[...] = jnp.zeros_like(hcar)

        xb = xb_ref[...]
        ext[SUBLANES:SUBLANES + T, :] = xb
        xc = cb_ref[...]
        for k in range(CONV_WIDTH):
            xc = xc + ext[pl.ds(SUBLANES - (CONV_WIDTH - 1) + k, T), :] * cw_ref[k:k + 1, :]
        ext[0:SUBLANES, :] = xb[T - SUBLANES:T, :]
        xc_ref[...] = xc
        xcb = xc.astype(BF16)
        for g in range(nG):
            sl = slice(g * GT, (g + 1) * GT)
            zr = jnp.dot(xcb[:, sl], wr_ref[g], preferred_element_type=F32) + br_ref[:, sl]
            zi = jnp.dot(xcb[:, sl], wi_ref[g], preferred_element_type=F32) + bi_ref[:, sl]
            r_ref[:, sl] = _sigmoid(zr)
            i_ref[:, sl] = _sigmoid(zi)
        r = r_ref[...]
        a, _, mult = _decay_terms(r, _log_sigmoid(lam_ref[...]))
        a_scr[...] = a
        hs_ref[...] = mult * (i_ref[...] * xc)

        def step(t, h):
            h = a_scr[pl.ds(t, 1), :] * h + hs_ref[pl.ds(t, 1), :]
            hs_ref[pl.ds(t, 1), :] = h
            return h

        hcar[...] = lax.fori_loop(0, T, step, hcar[...], unroll=8)
        y_ref[...] = (_gelu(gb_ref[...]) * hs_ref[...]).astype(BF16)

    row = pl.BlockSpec((T, D), lambda i: (i, 0))
    vec = pl.BlockSpec((1, D), lambda i: (0, 0))
    bd = pl.BlockSpec((nG, GT, GT), lambda i: (0, 0, 0))
    f32o = jax.ShapeDtypeStruct((S, D), F32)
    return _hosted_call(
        body, "lru_fwd", (S // T,),
        [row, pl.BlockSpec((T, D), lambda i: (i, 1)), pl.BlockSpec((CONV_WIDTH, D), lambda i: (0, 0)), vec,
         bd, vec, bd, vec, vec],
        [row, row, row, row, row], [jax.ShapeDtypeStruct((S, D), BF16), f32o, f32o, f32o, f32o],
        [pltpu.VMEM((T + SUBLANES, D), F32), pltpu.VMEM((T, D), F32), pltpu.VMEM((1, D), F32)],
        (u0, u0, conv_w, conv_b, wr_bd, b_r, wi_bd, b_i, lam), ("arbitrary",), plan)


def _lru_bwd(dy, u0, xc, r, ig, hs, conv_w, wr_bd, wi_bd, lam, S, D, plan=None):
    T = _pick(S, (128,))
    nT = S // T
    GT = wr_bd.shape[-1]
    nG = D // GT
    W = CONV_WIDTH

    def body(dy_ref, gb_ref, xb_ref, xbp_ref, xc_ref, r_ref, i_ref, hs_ref, hsp_ref, cw_ref, wr_ref, wi_ref, lam_ref,
             du_ref, dcw_ref, dcb_ref, dlam_ref, dbr_ref, dbi_ref, dwr_ref, dwi_ref,
             a_scr, dh_scr, exth, extx, extd, dxc_scr, dz_scr, carry):
        step = pl.program_id(0)
        first_tile = step == nT - 1

        @pl.when(step == 0)
        def _():
            for ref in (dcw_ref, dcb_ref, dlam_ref, dbr_ref, dbi_ref, dwr_ref, dwi_ref, carry):
                ref[...] = jnp.zeros_like(ref)
            extd[T:T + SUBLANES, :] = jnp.zeros((SUBLANES, D), F32)

        hs = hs_ref[...]
        dy = dy_ref[...]
        g, dgelu = _gelu_and_grad(gb_ref[...])
        du_ref[:, 0:D] = (dy * hs * dgelu).astype(BF16)
        r = r_ref[...]
        lam = lam_ref[...]
        ls = _log_sigmoid(lam)
        a, a2, mult = _decay_terms(r, ls)
        a_scr[...] = a
        dh_scr[...] = dy * g

        def rstep(j, c):
            t = T - 1 - j
            d = dh_scr[pl.ds(t, 1), :] + c
            dh_scr[pl.ds(t, 1), :] = d
            return a_scr[pl.ds(t, 1), :] * d

        carry[...] = lax.fori_loop(0, T, rstep, carry[...], unroll=8)
        dh = dh_scr[...]
        keep = jnp.where(first_tile, 0.0, 1.0)
        exth[0:SUBLANES, :] = hsp_ref[...] * keep
        exth[SUBLANES:SUBLANES + T, :] = hs
        hprev = exth[pl.ds(SUBLANES - 1, T), :]
        xc = xc_ref[...]
        ig = i_ref[...]
        da = dh * hprev
        dmult = dh * (ig * xc)
        dla = da * a - dmult * (a2 / mult)
        dlam_ref[...] += jnp.sum(dla * r, axis=0, keepdims=True) * (LRU_C * _sigmoid(-lam))
        dzr = (dla * (LRU_C * ls)) * (r * (1.0 - r))
        dzi = (dh * (mult * xc)) * (ig * (1.0 - ig))
        dbr_ref[...] += jnp.sum(dzr, axis=0, keepdims=True)
        dbi_ref[...] += jnp.sum(dzi, axis=0, keepdims=True)
        dxc_scr[...] = dh * (mult * ig)
        xcb = xc.astype(BF16)
        dz_scr[0] = dzr.astype(BF16)
        dz_scr[1] = dzi.astype(BF16)
        nt_dims = (((1,), (1,)), ((), ()))
        tn_dims = (((0,), (0,)), ((), ()))
        for gq in range(nG):
            sl = slice(gq * GT, (gq + 1) * GT)
            zr_g = dz_scr[0, :, sl]
            zi_g = dz_scr[1, :, sl]
            dxc_scr[:, sl] += (lax.dot_general(zr_g, wr_ref[gq], nt_dims, preferred_element_type=F32)
                               + lax.dot_general(zi_g, wi_ref[gq], nt_dims, preferred_element_type=F32))
            dwr_ref[gq] += lax.dot_general(xcb[:, sl], zr_g, tn_dims, preferred_element_type=F32)
            dwi_ref[gq] += lax.dot_general(xcb[:, sl], zi_g, tn_dims, preferred_element_type=F32)
        dxc = dxc_scr[...]
        dcb_ref[...] += jnp.sum(dxc, axis=0, keepdims=True)
        extx[0:SUBLANES, :] = xbp_ref[...] * keep
        extx[SUBLANES:SUBLANES + T, :] = xb_ref[...]
        extd[0:T, :] = dxc
        dxb = jnp.zeros((T, D), F32)
        for k in range(W):
            dxb = dxb + extd[pl.ds(W - 1 - k, T), :] * cw_ref[k:k + 1, :]
            dcw_ref[k:k + 1, :] += jnp.sum(dxc * extx[pl.ds(SUBLANES - (W - 1) + k, T), :], axis=0, keepdims=True)
        extd[T:T + SUBLANES, :] = dxc[0:SUBLANES, :]
        du_ref[:, D:2 * D] = dxb.astype(BF16)

    rev = lambda i: nT - 1 - i
    tpb = T // SUBLANES
    prev8 = lambda i: jnp.maximum(rev(i) * tpb - 1, 0)
    row = pl.BlockSpec((T, D), lambda i: (rev(i), 0))
    vec = pl.BlockSpec((1, D), lambda i: (0, 0))
    bd = pl.BlockSpec((nG, GT, GT), lambda i: (0, 0, 0))
    vec_o = jax.ShapeDtypeStruct((1, D), F32)
    bd_o = jax.ShapeDtypeStruct((nG, GT, GT), F32)
    return _hosted_call(
        body, "lru_bwd", (nT,),
        [row, row, pl.BlockSpec((T, D), lambda i: (rev(i), 1)), pl.BlockSpec((SUBLANES, D), lambda i: (prev8(i), 1)),
         row, row, row, row, pl.BlockSpec((SUBLANES, D), lambda i: (prev8(i), 0)),
         pl.BlockSpec((W, D), lambda i: (0, 0)), bd, bd, vec],
        [pl.BlockSpec((T, 2 * D), lambda i: (rev(i), 0)), pl.BlockSpec((W, D), lambda i: (0, 0)),
         vec, vec, vec, vec, bd, bd],
        [jax.ShapeDtypeStruct((S, 2 * D), BF16), jax.ShapeDtypeStruct((W, D), F32), vec_o, vec_o, vec_o, vec_o, bd_o, bd_o],
        [pltpu.VMEM((T, D), F32), pltpu.VMEM((T, D), F32), pltpu.VMEM((T + SUBLANES, D), F32),
         pltpu.VMEM((T + SUBLANES, D), F32), pltpu.VMEM((T + SUBLANES, D), F32),
         pltpu.VMEM((T, D), F32), pltpu.VMEM((2, T, D), BF16), pltpu.VMEM((1, D), F32)],
        (dy, u0, u0, u0, xc, r, ig, hs, hs, conv_w, wr_bd, wi_bd, lam), ("arbitrary",), plan)


AUG_ROWS = 16
HEAD_ROWS = 128
LSE_ROW = HEAD_DIM + 6
ONES_ROW_Q = HEAD_DIM + 3
ONES_COL_K = HEAD_DIM
ONES_ROW_V = HEAD_DIM
PREP_LANES = 512
HEAD_UNROLL = 4


def _split3(x):
    b1 = x.astype(BF16).astype(F32)
    r = x - b1
    b2 = r.astype(BF16).astype(F32)
    return b1, b2, r - b2


def _head_block(x, aug, T):
    row = lax.broadcasted_iota(jnp.int32, (AUG_ROWS, T), 0)
    blk = jnp.zeros((AUG_ROWS, T), F32)
    for i, e in enumerate(aug):
        blk = jnp.where(row == i, e, blk)
    return jnp.concatenate([x, blk, jnp.zeros((HEAD_ROWS - HEAD_DIM - AUG_ROWS, T), F32)], axis=0)


def _tri_matrix(lower):
    i = np.arange(LANES)
    m = (i[:, None] >= i[None, :]) if lower else (i[:, None] <= i[None, :])
    return jnp.asarray(m.astype(np.float32), BF16)


def _lane_cumsum(x, tri_ref, carry, reverse):
    n = x.shape[1] // LANES
    tri = tri_ref[...]
    out = [None] * n
    for j in (range(n - 1, -1, -1) if reverse else range(n)):
        cs = carry
        for part in _split3(x[:, j * LANES:(j + 1) * LANES]):
            cs = cs + jnp.dot(part.astype(BF16), tri, preferred_element_type=F32)
        out[j] = cs
        carry = cs[:, 0:1] if reverse else cs[:, LANES - 1:LANES]
    return jnp.concatenate(out, axis=1), carry


def _head_rows(h):
    return pl.ds(pl.multiple_of(h * HEAD_DIM, HEAD_DIM), HEAD_DIM)


def _fox_prep(ut, b_f, qg, kg, S, D, tq):
    H = D // HEAD_DIM
    T = min(tq, PREP_LANES)
    per = tq // T
    scale = HEAD_DIM ** -0.5

    def body(q_ref, k_ref, v_ref, f_ref, bf_ref, qg_ref, kg_ref, tri_ref,
             qat_ref, kat_ref, vat_ref, ka_ref, c_scr, ccar):
        @pl.when(pl.program_id(0) == 0)
        def _():
            ccar[...] = jnp.zeros_like(ccar)

        c, carry = _lane_cumsum(_log_sigmoid(f_ref[...] + bf_ref[...]), tri_ref, ccar[...], False)
        c_scr[...] = c
        ccar[...] = carry

        def head(h, _):
            rows = _head_rows(h)
            c1, c2, c3 = _split3(c_scr[pl.ds(h, 1), :])

            def normed(src, gain, mul):
                x = src[rows, :]
                rs = lax.rsqrt(jnp.mean(x * x, axis=0, keepdims=True) + EPS)
                return ((x * rs) * gain[rows, :]) * mul

            qat_ref[h] = _head_block(normed(q_ref, qg_ref, scale), [c1, c2, c3, 1.0, 1.0, 1.0], T).astype(BF16)
            kb = _head_block(normed(k_ref, kg_ref, 1.0), [1.0, 1.0, 1.0, -c1, -c2, -c3, 1.0, 1.0, 1.0], T)
            kat_ref[h] = kb.astype(BF16)
            ka_ref[h] = kb.T.astype(BF16)
            vat_ref[h] = _head_block(v_ref[rows, :], [1.0, 1.0, 1.0], T).astype(BF16)
            return 0

        lax.fori_loop(0, H, head, 0, unroll=HEAD_UNROLL)

    part = lambda j: pl.BlockSpec((D, T), lambda i: (j, i))
    colv = lambda n: pl.BlockSpec((n, 1), lambda i: (0, 0))
    tmaj = lambda r: pl.BlockSpec((H, None, r, T), lambda i: (0, i // per, 0, i % per))
    norm = pl.BlockSpec((H, T, HEAD_ROWS), lambda i: (0, i, 0))
    tshape = lambda r: jax.ShapeDtypeStruct((H, S // tq, r, tq), BF16)
    nshape = jax.ShapeDtypeStruct((H, S, HEAD_ROWS), BF16)
    return pl.pallas_call(
        body, name="fox_prep", grid=(S // T,),
        in_specs=[part(0), part(1), part(2), pl.BlockSpec((LANES, T), lambda i: (3 * D // LANES, i)),
                  colv(LANES), colv(D), colv(D), pl.BlockSpec((LANES, LANES), lambda i: (0, 0))],
        out_specs=[tmaj(HEAD_ROWS), tmaj(HEAD_ROWS), tmaj(HEAD_ROWS), norm],
        out_shape=[tshape(HEAD_ROWS), tshape(HEAD_ROWS), tshape(HEAD_ROWS), nshape],
        scratch_shapes=[pltpu.VMEM((LANES, T), F32), pltpu.VMEM((LANES, 1), F32)],
        compiler_params=_params(("arbitrary",)),
    )(ut, ut, ut, ut, b_f, qg, kg, _tri_matrix(False))


def _fox_bwd_prep(dot, ot, lse, qat, S, D, tq, plan=None):
    H = D // HEAD_DIM
    T = min(tq, PREP_LANES)
    per = tq // T

    def body(do_ref, o_ref, lse_ref, qat_ref, doat_ref, doa_ref, qat1_ref, qa1_ref):
        row = lax.broadcasted_iota(jnp.int32, (HEAD_ROWS, T), 0)

        def head(h, _):
            rows = _head_rows(h)
            do = do_ref[rows, :].astype(F32)
            delta = jnp.sum(do * o_ref[rows, :], axis=0, keepdims=True)
            db = _head_block(do, list(_split3(-delta)), T)
            doat_ref[h] = db.astype(BF16)
            doa_ref[h] = db.T.astype(BF16)
            qb = qat_ref[h].astype(F32)
            for i, e in enumerate(_split3(-lse_ref[h])):
                qb = jnp.where(row == LSE_ROW + i, e, qb)
            qat1_ref[h] = qb.astype(BF16)
            qa1_ref[h] = qb.T.astype(BF16)
            return 0

        lax.fori_loop(0, H, head, 0, unroll=HEAD_UNROLL)

    chan = pl.BlockSpec((D, T), lambda i: (0, i))
    tmaj = pl.BlockSpec((H, None, HEAD_ROWS, T), lambda i: (0, i // per, 0, i % per))
    norm = pl.BlockSpec((H, T, HEAD_ROWS), lambda i: (0, i, 0))
    tshape = jax.ShapeDtypeStruct((H, S // tq, HEAD_ROWS, tq), BF16)
    nshape = jax.ShapeDtypeStruct((H, S, HEAD_ROWS), BF16)
    return _hosted_call(body, "fox_bwd_prep", (S // T,), [chan, chan, pl.BlockSpec((H, 1, T), lambda i: (0, 0, i)), tmaj],
                        [tmaj, norm, tmaj, norm], [tshape, nshape, tshape, nshape], [], (dot, ot, lse, qat),
                        ("arbitrary",), plan)


def _causal(s, k_axis):
    t = min(s.shape)
    ki = lax.broadcasted_iota(jnp.int32, s.shape, k_axis) - (s.shape[k_axis] - t)
    qi = lax.broadcasted_iota(jnp.int32, s.shape, 1 - k_axis)
    return jnp.where(ki <= qi, s, NEG_INF)


def _seq_tile(i, t):
    return pl.ds(pl.multiple_of(i * t, t), t)


def _attn_forward(ka, qat, vat, S, D, tq, plan=None):
    H = D // HEAD_DIM
    nq = S // tq
    G = 4

    def body(ka_ref, qat_ref, vat_ref, o_ref, o32_ref, lse_ref, m_scr, acc_scr):
        qi = pl.program_id(1)
        m_scr[...] = jnp.full_like(m_scr, NEG_INF)
        acc_scr[...] = jnp.zeros_like(acc_scr)

        def span(k0, n, diagonal):
            keys = pl.ds(pl.multiple_of(k0 * tq, tq), n * tq)
            s = [jnp.dot(ka_ref[g, keys, :], qat_ref[g], preferred_element_type=F32) for g in range(G)]
            if diagonal:
                s = [_causal(sg, 0) for sg in s]
            m_prev = [m_scr[g] for g in range(G)]
            m_new = [jnp.maximum(m_prev[g], jnp.max(s[g], axis=0, keepdims=True)) for g in range(G)]
            p = [jnp.exp(s[g] - m_new[g]).astype(BF16) for g in range(G)]
            for g in range(G):
                upd = jnp.dot(vat_ref[g, k0], p[g][0:tq], preferred_element_type=F32)
                for i in range(1, n):
                    upd = upd + jnp.dot(vat_ref[g, k0 + i], p[g][i * tq:(i + 1) * tq], preferred_element_type=F32)
                acc_scr[g] = jnp.exp(m_prev[g] - m_new[g]) * acc_scr[g] + upd
                m_scr[g] = m_new[g]

        def off_diagonal_pair(j, _):
            span(2 * j, 2, False)
            return 0

        lax.fori_loop(0, qi // 2, off_diagonal_pair, 0)
        pl.when(qi % 2 == 1)(lambda: span(qi - 1, 2, True))
        pl.when(qi % 2 == 0)(lambda: span(qi, 1, True))
        for g in range(G):
            l = acc_scr[g, ONES_ROW_V:ONES_ROW_V + 1, :]
            o = acc_scr[g, 0:HEAD_DIM, :] / l
            o_ref[g * HEAD_DIM:(g + 1) * HEAD_DIM, :] = o.astype(BF16)
            o32_ref[g * HEAD_DIM:(g + 1) * HEAD_DIM, :] = o
            lse_ref[g] = m_scr[g] + jnp.log(l)

    chan = pl.BlockSpec((G * HEAD_DIM, tq), lambda h, i: (h, i))
    stat = pl.BlockSpec((G, 1, tq), lambda h, i: (h, 0, i))
    return _hosted_call(
        body, "attn_forward", (H // G, nq),
        [pl.BlockSpec((G, S, HEAD_ROWS), lambda h, i: (h, 0, 0)),
         pl.BlockSpec((G, None, HEAD_ROWS, tq), lambda h, i: (h, i, 0, 0)),
         pl.BlockSpec((G, nq, HEAD_ROWS, tq), lambda h, i: (h, 0, 0, 0))],
        [chan, chan, stat],
        [jax.ShapeDtypeStruct((D, S), BF16), jax.ShapeDtypeStruct((D, S), F32), jax.ShapeDtypeStruct((H, 1, S), F32)],
        [pltpu.VMEM((G, 1, tq), F32), pltpu.VMEM((G, HEAD_ROWS, tq), F32)],
        (ka, qat, vat), ("arbitrary", "arbitrary"), plan)


def _attn_backward(qa, doa, qat, doat, ka, kat, vat, S, D, tq, plan=None):
    H = D // HEAD_DIM
    nq = S // tq
    G = 2

    def body(qa_ref, doa_ref, qat_ref, doat_ref, ka_ref, kat_ref, vat_ref, dq_ref, dk_ref, dv_ref, dk_scr, dv_scr):
        ki = pl.program_id(1)

        @pl.when(ki == 0)
        def _():
            dq_ref[...] = jnp.zeros_like(dq_ref)

        dk_scr[...] = jnp.zeros_like(dk_scr)
        dv_scr[...] = jnp.zeros_like(dv_scr)

        def span(q0, n, diagonal):
            rows = pl.ds(pl.multiple_of(q0 * tq, tq), n * tq)
            s = [jnp.dot(qa_ref[g, rows, :], kat_ref[g], preferred_element_type=F32) for g in range(G)]
            if diagonal:
                s = [_causal(sg, 1) for sg in s]
            p = [jnp.exp(sg) for sg in s]
            ds = [(p[g] * jnp.dot(doa_ref[g, rows, :], vat_ref[g], preferred_element_type=F32)).astype(BF16)
                  for g in range(G)]
            p = [pg.astype(BF16) for pg in p]
            for g in range(G):
                for i in range(n):
                    part = slice(i * tq, (i + 1) * tq)
                    dv_scr[g] += jnp.dot(doat_ref[g, q0 + i, 0:HEAD_DIM, :], p[g][part], preferred_element_type=F32)
                    dk_scr[g] += jnp.dot(qat_ref[g, q0 + i], ds[g][part], preferred_element_type=F32)
                dq_ref[g, rows, :] += jnp.dot(ds[g], ka_ref[g], preferred_element_type=F32)

        n_off = nq - 1 - ki
        odd = n_off % 2

        def off_diagonal_pair(j, _):
            span(ki + 1 + odd + 2 * j, 2, False)
            return 0

        pl.when(odd == 1)(lambda: span(ki, 2, True))
        pl.when(odd == 0)(lambda: span(ki, 1, True))
        lax.fori_loop(0, n_off // 2, off_diagonal_pair, 0)
        dk_ref[...] = dk_scr[...]
        for g in range(G):
            dv_ref[g * HEAD_DIM:(g + 1) * HEAD_DIM, :] = dv_scr[g].astype(BF16)

    whole = pl.BlockSpec((G, S, HEAD_ROWS), lambda h, i: (h, 0, 0))
    tiles = pl.BlockSpec((G, nq, HEAD_ROWS, tq), lambda h, i: (h, 0, 0, 0))
    one = pl.BlockSpec((G, None, HEAD_ROWS, tq), lambda h, i: (h, i, 0, 0))
    return _hosted_call(
        body, "attn_backward", (H // G, nq),
        [whole, whole, tiles, tiles, pl.BlockSpec((G, tq, HEAD_ROWS), lambda h, i: (h, i, 0)), one, one],
        [whole, pl.BlockSpec((G, HEAD_ROWS, tq), lambda h, i: (h, 0, i)),
         pl.BlockSpec((G * HEAD_DIM, tq), lambda h, i: (h, i))],
        [jax.ShapeDtypeStruct((H, S, HEAD_ROWS), F32), jax.ShapeDtypeStruct((H, HEAD_ROWS, S), F32),
         jax.ShapeDtypeStruct((D, S), BF16)],
        [pltpu.VMEM((G, HEAD_ROWS, tq), F32), pltpu.VMEM((G, HEAD_DIM, tq), F32)],
        (qa, doa, qat, doat, ka, kat, vat), ("arbitrary", "arbitrary"), plan)


def _fox_prep_bwd(ut, dq, dkt, dvt, b_f, qg, kg, S, D, tq):
    H = D // HEAD_DIM
    T = min(tq, PREP_LANES)
    nT = S // T
    NU = 3 * D + LANES
    scale = HEAD_DIM ** -0.5

    def body(q_ref, k_ref, f_ref, dq_ref, dk_ref, dv_ref, bf_ref, qg_ref, kg_ref, tri_ref,
             du_ref, dbf_ref, dqg_ref, dkg_ref, gq_acc, gk_acc, fcar, dc_scr):
        step = pl.program_id(0)

        @pl.when(step == 0)
        def _():
            for ref in (gq_acc, gk_acc, fcar, dbf_ref):
                ref[...] = jnp.zeros_like(ref)

        dc_scr[...] = jnp.zeros_like(dc_scr)

        def head(h, _):
            rows = _head_rows(h)
            dqb = dq_ref[h].T
            dkb = dk_ref[h]
            dc_scr[pl.ds(h, 1), :] = dqb[ONES_COL_K:ONES_COL_K + 1, :] - dkb[ONES_ROW_Q:ONES_ROW_Q + 1, :]
            for src, dsrc, gain, acc, mul, base in ((q_ref, dqb, qg_ref, gq_acc, scale, 0),
                                                    (k_ref, dkb, kg_ref, gk_acc, 1.0, D)):
                x = src[rows, :]
                rs = lax.rsqrt(jnp.mean(x * x, axis=0, keepdims=True) + EPS)
                xhat = x * rs
                dn = dsrc[0:HEAD_DIM, :] * mul
                acc[rows, :] += jnp.sum(dn * xhat, axis=1, keepdims=True)
                dxh = dn * gain[rows, :]
                dx = rs * (dxh - xhat * jnp.mean(dxh * xhat, axis=0, keepdims=True))
                du_ref[pl.ds(pl.multiple_of(base + h * HEAD_DIM, HEAD_DIM), HEAD_DIM), :] = dx.astype(BF16)
            return 0

        lax.fori_loop(0, H, head, 0, unroll=HEAD_UNROLL)
        du_ref[2 * D:3 * D, :] = dv_ref[...]
        dlf, carry = _lane_cumsum(dc_scr[...], tri_ref, fcar[...], True)
        fcar[...] = carry
        dfl = dlf * _sigmoid(-(f_ref[...] + bf_ref[...]))
        dbf_ref[...] += jnp.sum(dfl, axis=1, keepdims=True)
        du_ref[3 * D:NU, :] = dfl.astype(BF16)

        @pl.when(step == nT - 1)
        def _():
            for acc, ref in ((gq_acc, dqg_ref), (gk_acc, dkg_ref)):
                tot = jnp.zeros((HEAD_DIM, 1), F32)
                for h in range(H):
                    tot = tot + acc[h * HEAD_DIM:(h + 1) * HEAD_DIM, :]
                ref[...] = tot

    rev = lambda i: nT - 1 - i
    part = lambda j: pl.BlockSpec((D, T), lambda i: (j, rev(i)))
    colv = lambda n: pl.BlockSpec((n, 1), lambda i: (0, 0))
    return pl.pallas_call(
        body, name="fox_prep_bwd", grid=(nT,),
        in_specs=[part(0), part(1), pl.BlockSpec((LANES, T), lambda i: (3 * D // LANES, rev(i))),
                  pl.BlockSpec((H, T, HEAD_ROWS), lambda i: (0, rev(i), 0)),
                  pl.BlockSpec((H, HEAD_ROWS, T), lambda i: (0, 0, rev(i))), pl.BlockSpec((D, T), lambda i: (0, rev(i))),
                  colv(LANES), colv(D), colv(D), pl.BlockSpec((LANES, LANES), lambda i: (0, 0))],
        out_specs=[pl.BlockSpec((NU, T), lambda i: (0, rev(i))), colv(LANES), colv(HEAD_DIM), colv(HEAD_DIM)],
        out_shape=[jax.ShapeDtypeStruct((NU, S), BF16), jax.ShapeDtypeStruct((LANES, 1), F32),
                   jax.ShapeDtypeStruct((HEAD_DIM, 1), F32), jax.ShapeDtypeStruct((HEAD_DIM, 1), F32)],
        scratch_shapes=[pltpu.VMEM((D, 1), F32), pltpu.VMEM((D, 1), F32), pltpu.VMEM((LANES, 1), F32),
                        pltpu.VMEM((LANES, T), F32)],
        compiler_params=_params(("arbitrary",)),
    )(ut, ut, ut, dq, dkt, dvt, b_f, qg, kg, _tri_matrix(True))


def _block_diag_tiles(w):
    n = w.shape[0]
    per = min(MXU_DIM, n * LRU_BLOCK_DIM) // LRU_BLOCK_DIM
    eye = jnp.eye(per, dtype=w.dtype)
    w5 = w.reshape(n // per, per, LRU_BLOCK_DIM, 1, LRU_BLOCK_DIM) * eye[None, :, None, :, None]
    return w5.reshape(n // per, per * LRU_BLOCK_DIM, per * LRU_BLOCK_DIM).astype(BF16)


def _block_diag_extract(t, n):
    per = t.shape[-1] // LRU_BLOCK_DIM
    eye = jnp.eye(per, dtype=t.dtype)
    t5 = t.reshape(n // per, per, LRU_BLOCK_DIM, per, LRU_BLOCK_DIM) * eye[None, :, None, :, None]
    return t5.sum(axis=3).reshape(n, LRU_BLOCK_DIM, LRU_BLOCK_DIM)


def _local_step(x, tgt, small, wv, grad_view, comm=None):
    S, D = x.shape
    F = 4 * D
    H = D // HEAD_DIM
    nblk = D // LRU_BLOCK_DIM
    NU = 3 * D + LANES
    tq = max(LANES, min(512, S // 4))
    assert S % tq == 0
    vec = lambda a: a.reshape(1, -1).astype(F32)
    col = lambda a: a.reshape(-1, 1).astype(F32)
    mix_g, mlp_g = small["mix_norm"], small["mlp_norm"]
    conv_b = vec(small["lru_conv_b"])
    wr_bd, wi_bd = _block_diag_tiles(small["lru_w_r"][0]), _block_diag_tiles(small["lru_w_i"][0])
    b_r, b_i, lam = vec(small["lru_b_r"]), vec(small["lru_b_i"]), vec(small["lru_lambda"])
    b_f = jnp.pad(col(small["fox_b_f"]), ((0, LANES - H), (0, 0)))
    qg, kg = jnp.tile(col(small["fox_q_gain"]), (H, 1)), jnp.tile(col(small["fox_k_gain"]), (H, 1))
    X = lambda a: _View(a)
    grads = {}
    gout = functools.partial(grad_view, grads)

    def hosted(name, fn, *args):
        plan = comm.before(name, grads) if comm is not None else None
        res, side = fn(*args, plan=plan)
        if plan is not None:
            comm.after(name, side, wv)
        return res

    def hosted_mm(name, *args, **kw):
        plan = comm.before(name, grads) if comm is not None else None
        if plan is None:
            return _matmul(name, *args, **kw)
        res, side = _matmul(name, *args, plan=plan, **kw)
        comm.after(name, side, wv)
        return res

    two = lambda: [_fresh(S, D, F32), _fresh(S, D, BF16)]

    def mlp_up(l, hm):
        return hosted_mm(f"mlp{l}_up", X(hm), wv[f"w1_{l}"], S, F, D, outs=[_fresh(S, F, BF16)], epilogue=_ep_relu2)[0]

    def mlp_bwd(l, xin, hm, act, d, db):
        (dz,) = hosted_mm(f"mlp{l}_dact", X(db), wv[f"w2_{l}"], S, F, D, tb=True, outs=[_fresh(S, F, BF16)],
                          epilogue=_ep_drelu2, extras=[X(act)])
        (grads[f"w2_{l}"],) = _matmul(f"mlp{l}_dw2", X(act), X(db), F, D, S, ta=True, outs=[gout(f"w2_{l}")],
                                      epilogue=_ep_store)
        (grads[f"w1_{l}"],) = _matmul(f"mlp{l}_dw1", X(hm), X(dz), D, F, S, ta=True, outs=[gout(f"w1_{l}")],
                                      epilogue=_ep_store)
        return _matmul(f"mlp{l}_dhm", X(dz), wv[f"w1_{l}"], S, D, F, tb=True, outs=two(), n_sums=1,
                       epilogue=_ep_norm_bwd, extras=[X(xin), X(d)], vecs=[mlp_g[l:l + 1]])

    (h0,) = hosted("mix0_norm", _rms_fwd, "mix0_norm", x, mix_g[0:1], S, D)
    (u0,) = hosted_mm("lru_in", X(h0), wv["lru_in"], S, 2 * D, D, outs=[_fresh(S, 2 * D, F32)], epilogue=_ep_store)
    conv_w = small["conv_w"]
    y, xc, r, ig, hs = hosted("lru_fwd", _lru_fwd, u0, conv_w, conv_b, wr_bd, b_r, wi_bd, b_i, lam, S, D)
    x1, hm0 = _matmul("lru_out", X(y), wv["lru_out"], S, D, D, outs=two(), epilogue=_ep_resid_norm, extras=[X(x)],
                      vecs=[mlp_g[0:1]])
    act0 = mlp_up(0, hm0)
    x2, h1 = hosted_mm("mlp0_down", X(act0), wv["w2_0"], S, D, F, outs=two(), epilogue=_ep_resid_norm, extras=[X(x1)],
                       vecs=[mix_g[1:2]])
    (u1,) = _matmul("fox_in", wv["fox_in"], X(h1), NU, S, D, tb=True, outs=[_fresh(NU, S, F32)], epilogue=_ep_store)
    qat, kat, vat, ka = _fox_prep(u1, b_f, qg, kg, S, D, tq)
    o, o32, lse = hosted("attn_forward", _attn_forward, ka, qat, vat, S, D, tq)
    x3, hm1 = _matmul("fox_out", X(o), wv["fox_out"], S, D, D, ta=True, outs=two(), epilogue=_ep_resid_norm,
                      extras=[X(x2)], vecs=[mlp_g[1:2]])
    act1 = mlp_up(1, hm1)
    (x4,) = _matmul("mlp1_down", X(act1), wv["w2_1"], S, D, F, outs=[_fresh(S, D, F32)], epilogue=_ep_resid,
                    extras=[X(x3)])
    loss, d4, d4b = _loss_head(x4, tgt, S, D)

    d3, d3b, dg_mlp1 = mlp_bwd(1, x3, hm1, act1, d4, d4b)
    (do,) = _matmul("fox_dout", wv["fox_out"], X(d3b), D, S, D, tb=True, outs=[_fresh(D, S, BF16)], epilogue=_ep_store)
    (grads["fox_out"],) = _matmul("fox_dwout", X(o), X(d3b), D, D, S, outs=[gout("fox_out")], epilogue=_ep_store)
    doat, doa, qat1, qa1 = hosted("fox_bwd_prep", _fox_bwd_prep, do, o32, lse, qat, S, D, tq)
    dqn, dkn, dv = hosted("attn_backward", _attn_backward, qa1, doa, qat1, doat, ka, kat, vat, S, D, tq)
    du1, dbf, dqg, dkg = _fox_prep_bwd(u1, dqn, dkn, dv, b_f, qg, kg, S, D, tq)
    (grads["fox_in"],) = _matmul("fox_dwin", X(du1), X(h1), NU, D, S, outs=[gout("fox_in")], epilogue=_ep_store)
    d2, d2b, dg_mix1 = hosted_mm("fox_dh", X(du1), wv["fox_in"], S, D, NU, ta=True, outs=two(), n_sums=1,
                               epilogue=_ep_norm_bwd, extras=[X(x2), X(d3)], vecs=[mix_g[1:2]])
    d1, d1b, dg_mlp0 = mlp_bwd(0, x1, hm0, act0, d2, d2b)
    (grads["lru_out"],) = _matmul("lru_dwout", X(y), X(d1b), D, D, S, ta=True, outs=[gout("lru_out")],
                                  epilogue=_ep_store)
    (dy,) = hosted_mm("lru_dout", X(d1b), wv["lru_out"], S, D, D, tb=True, outs=[_fresh(S, D, F32)],
                      epilogue=_ep_store)
    du0, dcw, dcb, dlam, dbr, dbi, dwr, dwi = hosted("lru_bwd", _lru_bwd, dy, u0, xc, r, ig, hs, conv_w, wr_bd, wi_bd,
                                                     lam, S, D)
    (grads["lru_in"],) = _matmul("lru_dwin", X(h0), X(du0), D, 2 * D, S, ta=True, outs=[gout("lru_in")],
                                 epilogue=_ep_store)
    gx, dg_mix0 = hosted_mm("lru_dh", X(du0), wv["lru_in"], S, D, 2 * D, tb=True, outs=[_fresh(S, D, F32)], n_sums=1,
                            epilogue=lambda *a: _ep_norm_bwd(*a)[::2], extras=[X(x), X(d1)], vecs=[mix_g[0:1]])

    grads.update(
        mix_norm=jnp.concatenate([dg_mix0, dg_mix1], axis=0), mlp_norm=jnp.concatenate([dg_mlp0, dg_mlp1], axis=0),
        conv_w=dcw, lru_conv_b=dcb, lru_w_r=_block_diag_extract(dwr, nblk)[None], lru_b_r=dbr.reshape(1, nblk, -1),
        lru_w_i=_block_diag_extract(dwi, nblk)[None], lru_b_i=dbi.reshape(1, nblk, -1), lru_lambda=dlam,
        fox_b_f=dbf[:H].reshape(1, H), fox_q_gain=dqg.reshape(1, -1), fox_k_gain=dkg.reshape(1, -1))
    return loss, gx, grads


def _place():
    x, y, c = lax.axis_index("x"), lax.axis_index("y"), lax.axis_index("c")
    chips = [(1 - x, y), (x, 1 - y), (1 - x, 1 - y)]
    return x, y, c, 2 * x + y, chips


BOUNCE_BYTES = 1 << 20


def _bounce_shape(rows, cols, dtype):
    chunk = rows
    while chunk % 2 == 0 and chunk > 16 and chunk * cols * jnp.dtype(dtype).itemsize > BOUNCE_BYTES:
        chunk //= 2
    return pltpu.VMEM((2, chunk, cols), dtype)


def _bounce_copy(src, dst, buf, sem):
    chunk = buf.shape[1]
    n = src.shape[0] // chunk
    cin = lambda i: pltpu.make_async_copy(src.at[pl.ds(i * chunk, chunk)], buf.at[i % 2], sem.at[i % 2])
    cout = lambda i: pltpu.make_async_copy(buf.at[i % 2], dst.at[pl.ds(i * chunk, chunk)], sem.at[2 + i % 2])
    cin(0).start()
    for i in range(n):
        cin(i).wait()
        if i + 1 < n:
            if i >= 1:
                cout(i - 1).wait()
            cin(i + 1).start()
        cout(i).start()
    if n >= 2:
        cout(n - 2).wait()
    cout(n - 1).wait()


def _hbm_call(body, name, arrays, out_shape, n_dma_sems, bounce=()):
    scratch = [pltpu.SemaphoreType.DMA((k,)) for k in n_dma_sems]
    for rows, cols, dtype in bounce:
        scratch += [_bounce_shape(rows, cols, dtype), pltpu.SemaphoreType.DMA((4,))]
    return pl.pallas_call(
        body, name=name, in_specs=[ANY] * len(arrays), out_specs=[ANY] * len(out_shape), out_shape=out_shape,
        scratch_shapes=scratch,
        compiler_params=pltpu.CompilerParams(has_side_effects=True, vmem_limit_bytes=VMEM_LIMIT),
    )(*arrays)


class _Gather:
    def __init__(self, shards):
        n = self.n = len(shards)
        self.operands = list(shards)
        self.out_shape = [jax.ShapeDtypeStruct((N_CHIPS,) + tuple(a.shape), a.dtype) for a in shards]
        self.scratch = [pltpu.SemaphoreType.DMA((3 * n,)) for _ in range(4)]
        for a in shards:
            self.scratch += [_bounce_shape(a.shape[0], a.shape[1], a.dtype), pltpu.SemaphoreType.DMA((4,))]

    def _copies(self, ins, outs, scr):
        send, recv, fsend, frecv = scr[:4]
        x, y, c, s, chips = _place()

        def rows(a, chip_idx, which):
            hr = ins[a].shape[0] // 2
            return outs[a].at[chip_idx, pl.ds(which * hr, hr)]

        def landed(a, j, core):
            return rows(a, 2 * chips[j][0] + chips[j][1], core)

        def ici(a, j, mine):
            hr = ins[a].shape[0] // 2
            src, dst = (ins[a].at[pl.ds(c * hr, hr)], rows(a, s, c)) if mine else (landed(a, j, c),) * 2
            return pltpu.make_async_remote_copy(src_ref=src, dst_ref=dst, send_sem=send.at[3 * a + j],
                                                recv_sem=recv.at[3 * a + j], device_id=(*chips[j], c),
                                                device_id_type=MESH)

        def d2d(a, j, mine):
            ref = landed(a, j, c if mine else 1 - c)
            return pltpu.make_async_remote_copy(src_ref=ref, dst_ref=ref, send_sem=fsend.at[3 * a + j],
                                                recv_sem=frecv.at[3 * a + j], device_id=(x, y, 1 - c),
                                                device_id_type=MESH)

        return ici, d2d, s

    def start(self, ins, outs, scr):
        ici, _, _ = self._copies(ins, outs, scr)
        for a in range(self.n):
            for j in range(3):
                ici(a, j, True).start()

    def middle(self, ins, outs, scr):
        ici, d2d, s = self._copies(ins, outs, scr)
        for a in range(self.n):
            _bounce_copy(ins[a], outs[a].at[s], scr[4 + 2 * a], scr[5 + 2 * a])
        for a in range(self.n):
            for j in range(3):
                ici(a, j, False).wait_recv()
                d2d(a, j, True).start()

    def finish(self, ins, outs, scr):
        ici, d2d, _ = self._copies(ins, outs, scr)
        for a in range(self.n):
            for j in range(3):
                d2d(a, j, False).wait_recv()
        for a in range(self.n):
            for j in range(3):
                ici(a, j, True).wait_send()
                d2d(a, j, True).wait_send()


def _run_plan(name, plan):
    k_in, k_out = len(plan.operands), len(plan.out_shape)

    def body(*refs):
        parts = (refs[:k_in], refs[k_in:k_in + k_out], refs[k_in + k_out:])
        plan.start(*parts)
        plan.middle(*parts)
        plan.finish(*parts)

    return pl.pallas_call(
        body, name=name, in_specs=[ANY] * k_in, out_specs=[ANY] * k_out, out_shape=plan.out_shape,
        scratch_shapes=plan.scratch,
        compiler_params=pltpu.CompilerParams(has_side_effects=True, vmem_limit_bytes=VMEM_LIMIT),
    )(*plan.operands)


def _hosted_call(body, name, grid, in_specs, out_specs, out_shape, scratch_shapes, operands, sem, plan=None):
    if plan is None:
        res = pl.pallas_call(body, name=name, grid=grid, in_specs=in_specs, out_specs=out_specs, out_shape=out_shape,
                             scratch_shapes=scratch_shapes, compiler_params=_params(sem))(*operands)
        return res, None
    n_in, n_out, n_scr = len(in_specs), len(out_specs), len(scratch_shapes)
    k_in, k_out = len(plan.operands), len(plan.out_shape)
    total = int(np.prod(grid))
    late = max(0, total - 1 - max(1, total // 8))

    def hosted(*refs):
        ins, refs = refs[:n_in], refs[n_in:]
        p_ins, refs = refs[:k_in], refs[k_in:]
        outs, refs = refs[:n_out], refs[n_out:]
        p_outs, refs = refs[:k_out], refs[k_out:]
        scr, p_scr = refs[:n_scr], refs[n_scr:]
        step = pl.program_id(0)
        for d in range(1, len(grid)):
            step = step * grid[d] + pl.program_id(d)
        pl.when(step == 0)(lambda: plan.start(p_ins, p_outs, p_scr))
        body(*ins, *outs, *scr)
        pl.when(step == late)(lambda: plan.middle(p_ins, p_outs, p_scr))
        pl.when(step == total - 1)(lambda: plan.finish(p_ins, p_outs, p_scr))

    res = pl.pallas_call(
        hosted, name=name, grid=grid, in_specs=list(in_specs) + [ANY] * k_in, out_specs=list(out_specs) + [ANY] * k_out,
        out_shape=list(out_shape) + plan.out_shape, scratch_shapes=list(scratch_shapes) + plan.scratch,
        compiler_params=pltpu.CompilerParams(dimension_semantics=sem, vmem_limit_bytes=VMEM_LIMIT,
                                             has_side_effects=True),
    )(*operands, *plan.operands)
    return res[:n_out], res[n_out:]


def _all_gather(name, shards):
    return _run_plan(name, _Gather(shards))


class _Swap:
    def __init__(self, arrs):
        self.n = len(arrs)
        self.operands = list(arrs)
        self.out_shape = [jax.ShapeDtypeStruct((a.shape[0], a.shape[1] // 2, a.shape[2]), a.dtype) for a in arrs]
        self.scratch = [pltpu.SemaphoreType.DMA((self.n,)) for _ in range(2)]

    def _copy(self, ins, outs, scr, a):
        x, y, c, _, _ = _place()
        hr = ins[a].shape[1] // 2
        return pltpu.make_async_remote_copy(
            src_ref=ins[a].at[:, pl.ds((1 - c) * hr, hr)], dst_ref=outs[a], send_sem=scr[0].at[a],
            recv_sem=scr[1].at[a], device_id=(x, y, 1 - c), device_id_type=MESH)

    def start(self, ins, outs, scr):
        for a in range(self.n):
            self._copy(ins, outs, scr, a).start()

    def middle(self, ins, outs, scr):
        pass

    def finish(self, ins, outs, scr):
        for a in range(self.n):
            self._copy(ins, outs, scr, a).wait()


class _Scatter:
    def __init__(self, parts):
        n = self.n = len(parts)
        self.operands = list(parts)
        self.out_shape = [jax.ShapeDtypeStruct(a.shape, a.dtype) for a in parts]
        self.scratch = [pltpu.SemaphoreType.DMA((3 * n,)) for _ in range(2)]
        for a in parts:
            self.scratch += [_bounce_shape(a.shape[1], a.shape[2], a.dtype), pltpu.SemaphoreType.DMA((4,))]

    def _copy(self, ins, outs, scr, a, j, mine):
        x, y, c, s, chips = _place()
        t = 2 * chips[j][0] + chips[j][1]
        return pltpu.make_async_remote_copy(
            src_ref=ins[a].at[t], dst_ref=outs[a].at[s if mine else t], send_sem=scr[0].at[3 * a + j],
            recv_sem=scr[1].at[3 * a + j], device_id=(*chips[j], c), device_id_type=MESH)

    def start(self, ins, outs, scr):
        for a in range(self.n):
            for j in range(3):
                self._copy(ins, outs, scr, a, j, True).start()

    def middle(self, ins, outs, scr):
        s = _place()[3]
        for a in range(self.n):
            _bounce_copy(ins[a].at[s], outs[a].at[s], scr[2 + 2 * a], scr[3 + 2 * a])

    def finish(self, ins, outs, scr):
        for a in range(self.n):
            for j in range(3):
                self._copy(ins, outs, scr, a, j, False).wait_recv()
        for a in range(self.n):
            for j in range(3):
                self._copy(ins, outs, scr, a, j, True).wait_send()


def _pair_gather(name, halves):
    n = len(halves)

    def body(*refs):
        ins, outs = refs[:n], refs[n:2 * n]
        send, recv = refs[2 * n:2 * n + 2]
        stage = refs[2 * n + 2:]
        x, y, c, _, _ = _place()
        cps = []
        for a in range(n):
            hr = ins[a].shape[0]
            cp = pltpu.make_async_remote_copy(
                src_ref=ins[a], dst_ref=outs[a].at[pl.ds(c * hr, hr)], send_sem=send.at[a], recv_sem=recv.at[a],
                device_id=(x, y, 1 - c), device_id_type=MESH)
            cp.start()
            cps.append((cp, hr))
        for a, (cp, hr) in enumerate(cps):
            _bounce_copy(ins[a], outs[a].at[pl.ds(c * hr, hr)], stage[2 * a], stage[2 * a + 1])
        for a, (cp, hr) in enumerate(cps):
            cp.wait_send()
            theirs = outs[a].at[pl.ds((1 - c) * hr, hr)]
            pltpu.make_async_remote_copy(src_ref=theirs, dst_ref=theirs, send_sem=send.at[a], recv_sem=recv.at[a],
                                         device_id=(x, y, 1 - c), device_id_type=MESH).wait_recv()

    out_shape = [jax.ShapeDtypeStruct((2 * a.shape[0], a.shape[1]), a.dtype) for a in halves]
    return _hbm_call(body, name, halves, out_shape, (n, n),
                     bounce=[(a.shape[0], a.shape[1], a.dtype) for a in halves])


def _row_tile(rows, cols, itemsize, n_bufs):
    budget = VMEM_LIMIT // 2
    for t in range(min(rows, 1024) // 16 * 16, 0, -16):
        if rows % t == 0 and 2 * n_bufs * t * cols * itemsize <= budget:
            return t
    return rows


def _pair_add(name, g, gsib, core, out_dtype):
    _, r, cols = g.shape
    hr = r // 2
    t = _row_tile(hr, cols, 4, 3)
    per = hr // t

    def body(core_ref, a_ref, b_ref, o_ref):
        o_ref[...] = (a_ref[...].astype(F32) + b_ref[...].astype(F32)).astype(o_ref.dtype)

    grid_spec = pltpu.PrefetchScalarGridSpec(
        num_scalar_prefetch=1, grid=(N_CHIPS, per),
        in_specs=[pl.BlockSpec((None, t, cols), lambda s, i, core: (s, core[0] * per + i, 0)),
                  pl.BlockSpec((None, t, cols), lambda s, i, core: (s, i, 0))],
        out_specs=pl.BlockSpec((None, t, cols), lambda s, i, core: (s, i, 0)))
    return pl.pallas_call(body, name=name, grid_spec=grid_spec,
                          out_shape=jax.ShapeDtypeStruct((N_CHIPS, hr, cols), out_dtype),
                          compiler_params=_params(("arbitrary", "arbitrary")))(core, g, gsib)


def _chip_sum(name, parts):
    _, hr, cols = parts.shape
    t = _row_tile(hr, cols, 4, 5)

    def body(p_ref, o_ref):
        o_ref[...] = ((p_ref[0].astype(F32) + p_ref[1].astype(F32)) + p_ref[2].astype(F32)) + p_ref[3].astype(F32)

    return pl.pallas_call(
        body, name=name, grid=(hr // t,), in_specs=[pl.BlockSpec((N_CHIPS, t, cols), lambda i: (0, i, 0))],
        out_specs=pl.BlockSpec((t, cols), lambda i: (i, 0)), out_shape=jax.ShapeDtypeStruct((hr, cols), F32),
        compiler_params=_params(("arbitrary",)))(parts)


def _pair_partials(tag, arrs, sib, wire_dtypes, core):
    return _Scatter([_pair_add(f"{tag}_pair_add{i}", g, gs, core, dt)
                     for i, (g, gs, dt) in enumerate(zip(arrs, sib, wire_dtypes))])


def _finish_reduce(tag, scattered):
    halves = [_chip_sum(f"{tag}_chip_sum{i}", p) for i, p in enumerate(scattered)]
    return _pair_gather(f"{tag}_pair_gather", halves)


def _adamw(name, w, g_parts, m, v):
    thin = w.ndim == 3
    rows, cols = w.shape[0], w.shape[-1]
    n_parts = len(g_parts)
    part_rows = rows // n_parts
    t = max(d for d in range(1, 257) if part_rows % d == 0) if thin else _row_tile(part_rows, cols, 4, 7 + n_parts)
    per = part_rows // t
    c1 = 1.0 - ADAM_B1 ** ADAM_STEP
    c2 = 1.0 - ADAM_B2 ** ADAM_STEP

    def body(w_ref, m_ref, v_ref, *refs):
        g_refs, (go_ref, d_ref, nm_ref, nv_ref) = refs[:n_parts], refs[n_parts:]
        g = g_refs[0][...]
        for k in range(1, n_parts):
            g = jnp.where(pl.program_id(0) >= k * per, g_refs[k][...], g)
        go_ref[...] = g
        m = ADAM_B1 * m_ref[...] + (1.0 - ADAM_B1) * g
        v = ADAM_B2 * v_ref[...] + (1.0 - ADAM_B2) * (g * g)
        nm_ref[...] = m
        nv_ref[...] = v
        d_ref[...] = -ADAM_LR * ((m / c1) / (jnp.sqrt(v / c2) + ADAM_EPS) + ADAM_WD * w_ref[...])

    block = (t, 1, cols) if thin else (t, cols)
    at = lambda r: (r, 0, 0) if thin else (r, 0)
    spec = pl.BlockSpec(block, lambda i: at(i))
    g_specs = [pl.BlockSpec(block, lambda i, k=k: at(jnp.clip(i - k * per, 0, per - 1))) for k in range(n_parts)]
    shp = jax.ShapeDtypeStruct(w.shape, F32)
    return pl.pallas_call(body, name=name, grid=(rows // t,), in_specs=[spec] * 3 + g_specs, out_specs=[spec] * 4,
                          out_shape=[shp] * 4, compiler_params=_params(("arbitrary",)))(w, m, v, *g_parts)


_WEIGHTS = ["mix_norm", "mlp_norm", "mlp_w1", "mlp_w2", "lru_w_in", "lru_conv_w", "lru_conv_b", "lru_w_r", "lru_b_r",
            "lru_w_i", "lru_b_i", "lru_lambda", "lru_w_out", "fox_w_in", "fox_b_f", "fox_q_gain", "fox_k_gain",
            "fox_w_out"]
_REPLICATED = ["mix_norm", "mlp_norm", "lru_conv_b", "lru_w_r", "lru_b_r", "lru_w_i", "lru_b_i", "lru_lambda",
               "fox_b_f", "fox_q_gain", "fox_k_gain"]
_PACK_TILE = 2 * SUBLANES * LANES


def _as2d(a):
    return a.reshape(-1, a.shape[-1])


def kernel(x, mix_norm, mlp_norm, mlp_w1, mlp_w2, lru_w_in, lru_conv_w, lru_conv_b, lru_w_r, lru_b_r, lru_w_i, lru_b_i, lru_lambda, lru_w_out, fox_w_in, fox_b_f, fox_q_gain, fox_k_gain, fox_w_out, loss_target, m_mix_norm, m_mlp_norm, m_mlp_w1, m_mlp_w2, m_lru_w_in, m_lru_conv_w, m_lru_conv_b, m_lru_w_r, m_lru_b_r, m_lru_w_i, m_lru_b_i, m_lru_lambda, m_lru_w_out, m_fox_w_in, m_fox_b_f, m_fox_q_gain, m_fox_k_gain, m_fox_w_out, v_mix_norm, v_mlp_norm, v_mlp_w1, v_mlp_w2, v_lru_w_in, v_lru_conv_w, v_lru_conv_b, v_lru_w_r, v_lru_b_r, v_lru_w_i, v_lru_b_i, v_lru_lambda, v_lru_w_out, v_fox_w_in, v_fox_b_f, v_fox_q_gain, v_fox_k_gain, v_fox_w_out):
    args = dict(locals())
    W = {n: args[n] for n in _WEIGHTS}
    Mo = {n: args["m_" + n] for n in _WEIGHTS}
    Vo = {n: args["v_" + n] for n in _WEIGHTS}
    S, D = x.shape[1], x.shape[2]
    F = 4 * D
    H = D // HEAD_DIM
    NU = 3 * D + LANES
    FQ, DQ = F // N_CHIPS, D // N_CHIPS
    nfox = fox_w_in.shape[-1]
    chip = 2 * lax.axis_index("x") + lax.axis_index("y")
    core = lax.axis_index("c").astype(jnp.int32).reshape(1)

    cw_flat = jnp.pad(lru_conv_w.reshape(-1), (0, _PACK_TILE - CONV_WIDTH * DQ)).reshape(2 * SUBLANES, LANES)
    w1s, w2s = mlp_w1.astype(BF16), mlp_w2.astype(BF16)
    wv = {}
    small = {n: W[n] for n in _REPLICATED}
    scattered = {}
    members = {"g1": ["w2_1", "w1_1", "fox_out"], "g2": ["fox_in"], "g3": ["w2_0", "w1_0"], "g4": ["lru_out", "lru_in"]}
    swap_at = {"fox_bwd_prep": "g1", "fox_dh": "g2", "lru_dout": "g3"}
    scatter_at = {"attn_backward": "g1", "mlp0_dact": "g2", "lru_bwd": "g3", "lru_dh": "g4"}
    swapped = {}

    fox_rows = -(-nfox // (4 * SUBLANES)) * (4 * SUBLANES)
    fox_t = jnp.pad(jnp.transpose(fox_w_in[0]).astype(BF16), ((0, fox_rows - nfox), (0, 0)))

    def shard_major(name, g):
        if name == "fox_in":
            return jnp.pad(g[:nfox * N_CHIPS].reshape(N_CHIPS, nfox, D), ((0, 0), (0, fox_rows - nfox), (0, 0)))
        return g

    class Comm:
        @staticmethod
        def before(name, grads):
            if name == "mix0_norm":
                return _Gather([lru_w_in[0].astype(BF16)])
            if name == "lru_in":
                return _Gather([lru_w_out[0].astype(BF16), cw_flat])
            if name == "lru_fwd":
                return _Gather([w1s[0]])
            if name == "mlp0_up":
                return _Gather([w2s[0]])
            if name == "mlp0_down":
                return _Gather([fox_t])
            if name == "attn_forward":
                return _Gather([fox_w_out[0].astype(BF16), w1s[1], w2s[1]])
            if name in swap_at:
                group = swap_at[name]
                swapped[group] = [[shard_major(n, grads[n]) for n in members[group]], None]
                return _Swap(swapped[group][0])
            if name in scatter_at:
                group = scatter_at[name]
                if group not in swapped:
                    arrs = [shard_major(n, grads[n]) for n in members[group]]
                    swapped[group] = [arrs, _run_plan(f"{group}_pair_swap", _Swap(arrs))]
                arrs, sib = swapped[group]
                return _pair_partials(group, arrs, sib, [BF16] * len(arrs), core)
            return None

        @staticmethod
        def after(name, res, wv):
            if name == "mix0_norm":
                wv.update(lru_in=_View(res[0], "cs"))
            elif name == "lru_in":
                wv.update(lru_out=_View(res[0], "rs"))
                taps = res[1].reshape(N_CHIPS, -1)[:, :CONV_WIDTH * DQ].reshape(N_CHIPS, CONV_WIDTH, DQ)
                small["conv_w"] = jnp.transpose(taps, (1, 0, 2)).reshape(CONV_WIDTH, D)
            elif name == "lru_fwd":
                wv.update(w1_0=_View(res[0], "cs"))
            elif name == "mlp0_up":
                wv.update(w2_0=_View(res[0], "rs"))
            elif name == "mlp0_down":
                fox_full = jnp.concatenate([res[0][s, :nfox] for s in range(N_CHIPS)], axis=0)
                wv.update(fox_in=_View(jnp.pad(fox_full, ((0, NU - fox_full.shape[0]), (0, 0)))))
            elif name == "attn_forward":
                wv.update(fox_out=_View(res[0], "rs"), w1_1=_View(res[1], "cs"), w2_1=_View(res[2], "rs"))
            elif name in swap_at:
                swapped[swap_at[name]][1] = res
            else:
                scattered.update(zip(members[scatter_at[name]], res))

    def grad_view(grads, name):
        if name in ("w1_0", "w1_1"):
            return _View(None, "cs", shape=(N_CHIPS, D, FQ), dtype=BF16)
        if name in ("w2_0", "w2_1"):
            return _View(None, "rs", shape=(N_CHIPS, FQ, D), dtype=BF16)
        if name == "lru_in":
            return _View(None, "cs", shape=(N_CHIPS, D, 2 * D // N_CHIPS), dtype=BF16)
        if name in ("lru_out", "fox_out"):
            return _View(None, "rs", shape=(N_CHIPS, DQ, D), dtype=BF16)
        return _View(None, shape=(NU, D), dtype=BF16)

    loss, gx, grads = _local_step(x[0], loss_target[0], small, wv, grad_view, Comm)

    pack_names = _REPLICATED + ["conv_w"]
    flat = jnp.concatenate([grads[n].reshape(-1).astype(F32) for n in pack_names] + [loss.reshape(-1)])
    per_chip = -(-flat.shape[0] // (N_CHIPS * _PACK_TILE)) * _PACK_TILE
    pack = jnp.pad(flat, (0, N_CHIPS * per_chip - flat.shape[0])).reshape(N_CHIPS, per_chip // LANES, LANES)
    pack_sib = _run_plan("pack_pair_swap", _Swap([pack]))
    (scattered["pack"],) = _run_plan("pack_chip_scatter", _pair_partials("pack", [pack], pack_sib, [F32], core))
    order = ["w1_0", "w1_1", "w2_0", "w2_1", "lru_in", "lru_out", "fox_in", "fox_out", "pack"]
    red = dict(zip(order, _finish_reduce("grads", [scattered[n] for n in order])))
    (all_pack,) = _all_gather("gather_small_grads", [red["pack"]])
    all_flat = all_pack.reshape(-1)
    G = {}
    off = 0
    for n in pack_names:
        shape = grads[n].shape if n == "conv_w" else W[n].shape
        size = int(np.prod(shape))
        G[n] = all_flat[off:off + size].reshape(shape)
        off += size
    total = all_flat[off]
    G["lru_conv_w"] = lax.dynamic_slice_in_dim(G.pop("conv_w"), chip * DQ, DQ, axis=1)[None]
    parts = {n: [_as2d(G[n])] for n in G}
    parts.update(mlp_w1=[red["w1_0"], red["w1_1"]], mlp_w2=[red["w2_0"], red["w2_1"]], lru_w_in=[red["lru_in"]],
                 lru_w_out=[red["lru_out"]], fox_w_in=[red["fox_in"][:nfox, None, :]], fox_w_out=[red["fox_out"]])

    delta, new_m, new_v = {}, {}, {}
    for n in _WEIGHTS:
        if n == "fox_w_in":
            to_thin = lambda a: jnp.transpose(a, (2, 0, 1))
            res = _adamw(f"adamw_{n}", to_thin(W[n]), parts[n], to_thin(Mo[n]), to_thin(Vo[n]))
            G[n], delta[n], new_m[n], new_v[n] = (jnp.transpose(t, (1, 2, 0)) for t in res)
            continue
        go, d, nm, nv = _adamw(f"adamw_{n}", _as2d(W[n]), parts[n], _as2d(Mo[n]), _as2d(Vo[n]))
        G[n], delta[n], new_m[n], new_v[n] = (t.reshape(W[n].shape) for t in (go, d, nm, nv))

    return (total, gx[None], *[G[n] for n in _WEIGHTS], *[delta[n] for n in _WEIGHTS],
            *[new_m[n] for n in _WEIGHTS], *[new_v[n] for n in _WEIGHTS])
```

```python
import functools

import numpy as np
import jax
import jax.numpy as jnp
from jax import lax
from jax.experimental import pallas as pl
from jax.experimental.pallas import tpu as pltpu

F32 = jnp.float32
BF16 = jnp.bfloat16

HEAD_DIM = 64
LRU_BLOCK_DIM = 64
CONV_WIDTH = 4
LRU_C = 8.0
EPS = 1e-6
NEG_INF = -1e30
ADAM_LR = 0.001
ADAM_B1 = 0.9
ADAM_B2 = 0.999
ADAM_EPS = 1e-08
ADAM_WD = 0.01
ADAM_STEP = 10

N_CHIPS = 4
LANES = 128
SUBLANES = 8
MXU_DIM = 256
VMEM_LIMIT = 52 * 1024 * 1024
MATMUL_TILES = (1024, 640, 512, 256, 128)
MATMUL_VMEM = VMEM_LIMIT * 4 // 5
MESH = pl.DeviceIdType.MESH
ANY = pl.BlockSpec(memory_space=pl.ANY)


def _pick(n, prefs):
    for p in prefs:
        if p <= n and n % p == 0:
            return p
    return n


def _params(sem=None):
    return pltpu.CompilerParams(dimension_semantics=sem, vmem_limit_bytes=VMEM_LIMIT)


class _View:
    def __init__(self, arr, kind="plain", shape=None, dtype=None):
        self.arr = arr
        self.kind = kind
        self.shape = tuple(arr.shape) if arr is not None else tuple(shape)
        self.dtype = arr.dtype if arr is not None else dtype

    def limits(self):
        if self.kind == "plain":
            return 0, 0
        return self.shape[-2], (self.shape[-1] if self.kind == "cs" else 0)

    def spec(self, br, bc, fr, fc):
        if self.kind == "plain":
            return pl.BlockSpec((br, bc), lambda *g: (fr(*g), fc(*g)))
        rows, ncol = self.shape[-2:]
        assert rows % br == 0 and ncol % bc == 0, (self.shape, br, bc)
        if self.kind == "cs":
            per = ncol // bc
            return pl.BlockSpec((None, br, bc), lambda *g: (fc(*g) // per, fr(*g), fc(*g) % per))
        per = rows // br
        return pl.BlockSpec((None, br, bc), lambda *g: (fr(*g) // per, fr(*g) % per, fc(*g)))


def _bf(x):
    return x if x.dtype == BF16 else x.astype(BF16)


def _matmul(name, A, B, M, N, K, *, ta=False, tb=False, outs, epilogue, extras=(), vecs=(), n_sums=0,
            tm=None, tn=None, tk=None, plan=None):
    lim = {"m": [M], "n": [N], "k": [K]}
    for view, (rdim, cdim) in ([(A, "km" if ta else "mk"), (B, "nk" if tb else "kn")]
                               + [(e, "mn") for e in extras] + [(o, "mn") for o in outs]):
        r_lim, c_lim = view.limits()
        lim[rdim].append(r_lim)
        lim[cdim].append(c_lim)
    cap = {d: int(np.gcd.reduce(lim[d])) for d in "mnk"}
    tm = tm or _pick(cap["m"], MATMUL_TILES)
    tn = tn or _pick(cap["n"], MATMUL_TILES)
    tk = tk or _pick(cap["k"], MATMUL_TILES)

    def vmem_bytes(tm, tk):
        size = lambda v: jnp.dtype(v.dtype).itemsize
        tiles = tm * tk * size(A) + tk * tn * size(B) + tm * tn * sum(size(v) for v in list(extras) + list(outs))
        return 2 * tiles + (tm * tn * 4 if K > tk else 0)

    if cap["k"] % (2 * tk) == 0 and vmem_bytes(tm, 2 * tk) <= MATMUL_VMEM:
        tk *= 2
    elif K == tk and cap["m"] % (2 * tm) == 0 and vmem_bytes(2 * tm, tk) <= MATMUL_VMEM:
        tm *= 2
    nk = K // tk
    gi = lambda i, j, k: i
    gj = lambda i, j, k: j
    gk = lambda i, j, k: k
    a_spec = A.spec(tk, tm, gk, gi) if ta else A.spec(tm, tk, gi, gk)
    b_spec = B.spec(tn, tk, gj, gk) if tb else B.spec(tk, tn, gk, gj)
    ca = 0 if ta else 1
    cb = 1 if tb else 0
    ne, no = len(extras) + len(vecs), len(outs)
    assert n_sums == 0 or tn == N
    row_spec = pl.BlockSpec((1, tn), lambda i, j, k: (0, j))
    in_specs = [a_spec, b_spec] + [e.spec(tm, tn, gi, gj) for e in extras] + [row_spec] * len(vecs)
    operands = [A.arr, B.arr] + [e.arr for e in extras] + list(vecs)
    out_specs = [o.spec(tm, tn, gi, gj) for o in outs] + [row_spec] * n_sums
    out_shape = ([jax.ShapeDtypeStruct(o.shape, o.dtype) for o in outs]
                 + [jax.ShapeDtypeStruct((1, N), F32)] * n_sums)

    def body(*refs):
        a_ref, b_ref = refs[0], refs[1]
        ex = refs[2:2 + ne]
        o_refs = refs[2 + ne:2 + ne + no]
        s_refs = refs[2 + ne + no:2 + ne + no + n_sums]
        first_row_tile = pl.program_id(0) == 0

        def prod():
            return lax.dot_general(_bf(a_ref[...]), _bf(b_ref[...]), (((ca,), (cb,)), ((), ())),
                                   preferred_element_type=F32)

        def finish(acc):
            res = epilogue(acc, *[e[...] for e in ex])
            for o_ref, r in zip(o_refs, res[:no]):
                o_ref[...] = r.astype(o_ref.dtype)
            for s_ref, r in zip(s_refs, res[no:]):
                def assign(s_ref=s_ref, r=r):
                    s_ref[...] = r

                def accumulate(s_ref=s_ref, r=r):
                    s_ref[...] += r

                pl.when(first_row_tile)(assign)
                pl.when(jnp.logical_not(first_row_tile))(accumulate)

        if nk == 1:
            finish(prod())
        else:
            acc_ref = refs[-1]
            k = pl.program_id(2)

            @pl.when(k == 0)
            def _():
                acc_ref[...] = jnp.zeros_like(acc_ref)

            acc_ref[...] += prod()

            @pl.when(k == nk - 1)
            def _():
                finish(acc_ref[...])

    res, side = _hosted_call(body, name, (M // tm, N // tn, nk), in_specs, out_specs, out_shape,
                             [pltpu.VMEM((tm, tn), F32)] if nk > 1 else [], operands,
                             ("arbitrary", "arbitrary", "arbitrary"), plan)
    return res if plan is None else (res, side)


def _ep_store(acc):
    return (acc,)


def _ep_resid(acc, res):
    return (res + acc,)


def _ep_resid_norm(acc, res, g):
    xo = res + acc
    r = lax.rsqrt(jnp.mean(xo * xo, axis=-1, keepdims=True) + EPS)
    return (xo, (xo * r) * g)


def _ep_norm_bwd(acc, x, dres, g):
    r = lax.rsqrt(jnp.mean(x * x, axis=-1, keepdims=True) + EPS)
    xhat = x * r
    dxn = acc * g
    tot = dres + r * (dxn - xhat * jnp.mean(dxn * xhat, axis=-1, keepdims=True))
    return (tot, tot, jnp.sum(acc * xhat, axis=0, keepdims=True))


def _ep_relu2(acc):
    zp = jnp.maximum(acc, 0.0)
    return (zp * zp,)


def _ep_drelu2(acc, act):
    return (acc * (2.0 * jnp.sqrt(act.astype(F32))),)


def _fresh(M, N, dtype):
    return _View(None, shape=(M, N), dtype=dtype)


def _rms_fwd(name, x, g, S, D, plan=None):
    T = _pick(S, (512, 256, 128))

    def body(x_ref, g_ref, h_ref):
        x = x_ref[...]
        r = lax.rsqrt(jnp.mean(x * x, axis=-1, keepdims=True) + EPS)
        h_ref[...] = ((x * r) * g_ref[...]).astype(BF16)

    return _hosted_call(body, name, (S // T,),
                        [pl.BlockSpec((T, D), lambda i: (i, 0)), pl.BlockSpec((1, D), lambda i: (0, 0))],
                        [pl.BlockSpec((T, D), lambda i: (i, 0))], [jax.ShapeDtypeStruct((S, D), BF16)], [], (x, g),
                        ("arbitrary",), plan)


def _loss_head(x, tgt, S, D):
    T = _pick(S, (512, 256, 128))

    def body(x_ref, t_ref, loss_ref, d_ref, db_ref):
        @pl.when(pl.program_id(0) == 0)
        def _():
            loss_ref[...] = jnp.zeros_like(loss_ref)

        e = x_ref[...] - t_ref[...]
        loss_ref[...] += 0.5 * jnp.sum(jnp.mean(e * e, axis=-1, keepdims=True), axis=0, keepdims=True)
        d = e * (1.0 / D)
        d_ref[...] = d
        db_ref[...] = d.astype(BF16)

    row = pl.BlockSpec((T, D), lambda i: (i, 0))
    return pl.pallas_call(
        body, name="loss_head", grid=(S // T,), in_specs=[row, row],
        out_specs=[pl.BlockSpec((1, 1), lambda i: (0, 0)), row, row],
        out_shape=[jax.ShapeDtypeStruct((1, 1), F32), jax.ShapeDtypeStruct((S, D), F32),
                   jax.ShapeDtypeStruct((S, D), BF16)],
        compiler_params=_params(("arbitrary",)),
    )(x, tgt)


def _sigmoid(z):
    return 0.5 * jnp.tanh(0.5 * z) + 0.5


def _log_sigmoid(z):
    return jnp.minimum(z, 0.0) - jnp.log(1.0 + jnp.exp(-jnp.abs(z)))


_GELU_K = 0.7978845608028654
_GELU_C = 0.044715


def _gelu(x):
    t = jnp.tanh(_GELU_K * (x + _GELU_C * (x * x * x)))
    return 0.5 * x * (1.0 + t)


def _gelu_and_grad(x):
    x2 = x * x
    t = jnp.tanh(_GELU_K * (x + _GELU_C * (x2 * x)))
    g = 0.5 * x * (1.0 + t)
    dg = 0.5 * (1.0 + t) + 0.5 * x * (1.0 - t * t) * (_GELU_K * (1.0 + 3.0 * _GELU_C * x2))
    return g, dg


def _decay_terms(r, ls):
    la = LRU_C * r * ls
    a = jnp.exp(la)
    a2 = a * a
    mult = jnp.sqrt(-jnp.tanh(la) * (a2 + 1.0))
    return a, a2, mult


def _lru_fwd(u0, conv_w, conv_b, wr_bd, b_r, wi_bd, b_i, lam, S, D, plan=None):
    T = _pick(S, (256, 128))
    GT = wr_bd.shape[-1]
    nG = D // GT

    def body(gb_ref, xb_ref, cw_ref, cb_ref, wr_ref, br_ref, wi_ref, bi_ref, lam_ref,
             y_ref, xc_ref, r_ref, i_ref, hs_ref, ext, a_scr, hcar):
        @pl.when(pl.program_id(0) == 0)
        def _():
            ext[0:SUBLANES, :] = jnp.zeros((SUBLANES, D), F32)
            hcar[...] = jnp.zeros_like(hcar)

        xb = xb_ref[...]
        ext[SUBLANES:SUBLANES + T, :] = xb
        xc = cb_ref[...]
        for k in range(CONV_WIDTH):
            xc = xc + ext[pl.ds(SUBLANES - (CONV_WIDTH - 1) + k, T), :] * cw_ref[k:k + 1, :]
        ext[0:SUBLANES, :] = xb[T - SUBLANES:T, :]
        xc_ref[...] = xc
        xcb = xc.astype(BF16)
        for g in range(nG):
            sl = slice(g * GT, (g + 1) * GT)
            zr = jnp.dot(xcb[:, sl], wr_ref[g], preferred_element_type=F32) + br_ref[:, sl]
            zi = jnp.dot(xcb[:, sl], wi_ref[g], preferred_element_type=F32) + bi_ref[:, sl]
            r_ref[:, sl] = _sigmoid(zr)
            i_ref[:, sl] = _sigmoid(zi)
        r = r_ref[...]
        a, _, mult = _decay_terms(r, _log_sigmoid(lam_ref[...]))
        a_scr[...] = a
        hs_ref[...] = mult * (i_ref[...] * xc)

        def step(t, h):
            h = a_scr[pl.ds(t, 1), :] * h + hs_ref[pl.ds(t, 1), :]
            hs_ref[pl.ds(t, 1), :] = h
            return h

        hcar[...] = lax.fori_loop(0, T, step, hcar[...], unroll=8)
        y_ref[...] = (_gelu(gb_ref[...]) * hs_ref[...]).astype(BF16)

    row = pl.BlockSpec((T, D), lambda i: (i, 0))
    vec = pl.BlockSpec((1, D), lambda i: (0, 0))
    bd = pl.BlockSpec((nG, GT, GT), lambda i: (0, 0, 0))
    f32o = jax.ShapeDtypeStruct((S, D), F32)
    return _hosted_call(
        body, "lru_fwd", (S // T,),
        [row, pl.BlockSpec((T, D), lambda i: (i, 1)), pl.BlockSpec((CONV_WIDTH, D), lambda i: (0, 0)), vec,
         bd, vec, bd, vec, vec],
        [row, row, row, row, row], [jax.ShapeDtypeStruct((S, D), BF16), f32o, f32o, f32o, f32o],
        [pltpu.VMEM((T + SUBLANES, D), F32), pltpu.VMEM((T, D), F32), pltpu.VMEM((1, D), F32)],
        (u0, u0, conv_w, conv_b, wr_bd, b_r, wi_bd, b_i, lam), ("arbitrary",), plan)


def _lru_bwd(dy, u0, xc, r, ig, hs, conv_w, wr_bd, wi_bd, lam, S, D, plan=None):
    T = _pick(S, (128,))
    nT = S // T
    GT = wr_bd.shape[-1]
    nG = D // GT
    W = CONV_WIDTH

    def body(dy_ref, gb_ref, xb_ref, xbp_ref, xc_ref, r_ref, i_ref, hs_ref, hsp_ref, cw_ref, wr_ref, wi_ref, lam_ref,
             du_ref, dcw_ref, dcb_ref, dlam_ref, dbr_ref, dbi_ref, dwr_ref, dwi_ref,
             a_scr, dh_scr, exth, extx, extd, dxc_scr, dz_scr, carry):
        step = pl.program_id(0)
        first_tile = step == nT - 1

        @pl.when(step == 0)
        def _():
            for ref in (dcw_ref, dcb_ref, dlam_ref, dbr_ref, dbi_ref, dwr_ref, dwi_ref, carry):
                ref[...] = jnp.zeros_like(ref)
            extd[T:T + SUBLANES, :] = jnp.zeros((SUBLANES, D), F32)

        hs = hs_ref[...]
        dy = dy_ref[...]
        g, dgelu = _gelu_and_grad(gb_ref[...])
        du_ref[:, 0:D] = (dy * hs * dgelu).astype(BF16)
        r = r_ref[...]
        lam = lam_ref[...]
        ls = _log_sigmoid(lam)
        a, a2, mult = _decay_terms(r, ls)
        a_scr[...] = a
        dh_scr[...] = dy * g

        def rstep(j, c):
            t = T - 1 - j
            d = dh_scr[pl.ds(t, 1), :] + c
            dh_scr[pl.ds(t, 1), :] = d
            return a_scr[pl.ds(t, 1), :] * d

        carry[...] = lax.fori_loop(0, T, rstep, carry[...], unroll=8)
        dh = dh_scr[...]
        keep = jnp.where(first_tile, 0.0, 1.0)
        exth[0:SUBLANES, :] = hsp_ref[...] * keep
        exth[SUBLANES:SUBLANES + T, :] = hs
        hprev = exth[pl.ds(SUBLANES - 1, T), :]
        xc = xc_ref[...]
        ig = i_ref[...]
        da = dh * hprev
        dmult = dh * (ig * xc)
        dla = da * a - dmult * (a2 / mult)
        dlam_ref[...] += jnp.sum(dla * r, axis=0, keepdims=True) * (LRU_C * _sigmoid(-lam))
        dzr = (dla * (LRU_C * ls)) * (r * (1.0 - r))
        dzi = (dh * (mult * xc)) * (ig * (1.0 - ig))
        dbr_ref[...] += jnp.sum(dzr, axis=0, keepdims=True)
        dbi_ref[...] += jnp.sum(dzi, axis=0, keepdims=True)
        dxc_scr[...] = dh * (mult * ig)
        xcb = xc.astype(BF16)
        dz_scr[0] = dzr.astype(BF16)
        dz_scr[1] = dzi.astype(BF16)
        nt_dims = (((1,), (1,)), ((), ()))
        tn_dims = (((0,), (0,)), ((), ()))
        for gq in range(nG):
            sl = slice(gq * GT, (gq + 1) * GT)
            zr_g = dz_scr[0, :, sl]
            zi_g = dz_scr[1, :, sl]
            dxc_scr[:, sl] += (lax.dot_general(zr_g, wr_ref[gq], nt_dims, preferred_element_type=F32)
                               + lax.dot_general(zi_g, wi_ref[gq], nt_dims, preferred_element_type=F32))
            dwr_ref[gq] += lax.dot_general(xcb[:, sl], zr_g, tn_dims, preferred_element_type=F32)
            dwi_ref[gq] += lax.dot_general(xcb[:, sl], zi_g, tn_dims, preferred_element_type=F32)
        dxc = dxc_scr[...]
        dcb_ref[...] += jnp.sum(dxc, axis=0, keepdims=True)
        extx[0:SUBLANES, :] = xbp_ref[...] * keep
        extx[SUBLANES:SUBLANES + T, :] = xb_ref[...]
        extd[0:T, :] = dxc
        dxb = jnp.zeros((T, D), F32)
        for k in range(W):
            dxb = dxb + extd[pl.ds(W - 1 - k, T), :] * cw_ref[k:k + 1, :]
            dcw_ref[k:k + 1, :] += jnp.sum(dxc * extx[pl.ds(SUBLANES - (W - 1) + k, T), :], axis=0, keepdims=True)
        extd[T:T + SUBLANES, :] = dxc[0:SUBLANES, :]
        du_ref[:, D:2 * D] = dxb.astype(BF16)

    rev = lambda i: nT - 1 - i
    tpb = T // SUBLANES
    prev8 = lambda i: jnp.maximum(rev(i) * tpb - 1, 0)
    row = pl.BlockSpec((T, D), lambda i: (rev(i), 0))
    vec = pl.BlockSpec((1, D), lambda i: (0, 0))
    bd = pl.BlockSpec((nG, GT, GT), lambda i: (0, 0, 0))
    vec_o = jax.ShapeDtypeStruct((1, D), F32)
    bd_o = jax.ShapeDtypeStruct((nG, GT, GT), F32)
    return _hosted_call(
        body, "lru_bwd", (nT,),
        [row, row, pl.BlockSpec((T, D), lambda i: (rev(i), 1)), pl.BlockSpec((SUBLANES, D), lambda i: (prev8(i), 1)),
         row, row, row, row, pl.BlockSpec((SUBLANES, D), lambda i: (prev8(i), 0)),
         pl.BlockSpec((W, D), lambda i: (0, 0)), bd, bd, vec],
        [pl.BlockSpec((T, 2 * D), lambda i: (rev(i), 0)), pl.BlockSpec((W, D), lambda i: (0, 0)),
         vec, vec, vec, vec, bd, bd],
        [jax.ShapeDtypeStruct((S, 2 * D), BF16), jax.ShapeDtypeStruct((W, D), F32), vec_o, vec_o, vec_o, vec_o, bd_o, bd_o],
        [pltpu.VMEM((T, D), F32), pltpu.VMEM((T, D), F32), pltpu.VMEM((T + SUBLANES, D), F32),
         pltpu.VMEM((T + SUBLANES, D), F32), pltpu.VMEM((T + SUBLANES, D), F32),
         pltpu.VMEM((T, D), F32), pltpu.VMEM((2, T, D), BF16), pltpu.VMEM((1, D), F32)],
        (dy, u0, u0, u0, xc, r, ig, hs, hs, conv_w, wr_bd, wi_bd, lam), ("arbitrary",), plan)


AUG_ROWS = 16
HEAD_ROWS = 128
LSE_ROW = HEAD_DIM + 6
ONES_ROW_Q = HEAD_DIM + 3
ONES_COL_K = HEAD_DIM
ONES_ROW_V = HEAD_DIM
PREP_LANES = 512
HEAD_UNROLL = 4


def _split3(x):
    b1 = x.astype(BF16).astype(F32)
    r = x - b1
    b2 = r.astype(BF16).astype(F32)
    return b1, b2, r - b2


def _head_block(x, aug, T):
    row = lax.broadcasted_iota(jnp.int32, (AUG_ROWS, T), 0)
    blk = jnp.zeros((AUG_ROWS, T), F32)
    for i, e in enumerate(aug):
        blk = jnp.where(row == i, e, blk)
    return jnp.concatenate([x, blk, jnp.zeros((HEAD_ROWS - HEAD_DIM - AUG_ROWS, T), F32)], axis=0)


def _tri_matrix(lower):
    i = np.arange(LANES)
    m = (i[:, None] >= i[None, :]) if lower else (i[:, None] <= i[None, :])
    return jnp.asarray(m.astype(np.float32), BF16)


def _lane_cumsum(x, tri_ref, carry, reverse):
    n = x.shape[1] // LANES
    tri = tri_ref[...]
    out = [None] * n
    for j in (range(n - 1, -1, -1) if reverse else range(n)):
        cs = carry
        for part in _split3(x[:, j * LANES:(j + 1) * LANES]):
            cs = cs + jnp.dot(part.astype(BF16), tri, preferred_element_type=F32)
        out[j] = cs
        carry = cs[:, 0:1] if reverse else cs[:, LANES - 1:LANES]
    return jnp.concatenate(out, axis=1), carry


def _head_rows(h):
    return pl.ds(pl.multiple_of(h * HEAD_DIM, HEAD_DIM), HEAD_DIM)


def _fox_prep(ut, b_f, qg, kg, S, D, tq):
    H = D // HEAD_DIM
    T = min(tq, PREP_LANES)
    per = tq // T
    scale = HEAD_DIM ** -0.5

    def body(q_ref, k_ref, v_ref, f_ref, bf_ref, qg_ref, kg_ref, tri_ref,
             qat_ref, kat_ref, vat_ref, ka_ref, c_scr, ccar):
        @pl.when(pl.program_id(0) == 0)
        def _():
            ccar[...] = jnp.zeros_like(ccar)

        c, carry = _lane_cumsum(_log_sigmoid(f_ref[...] + bf_ref[...]), tri_ref, ccar[...], False)
        c_scr[...] = c
        ccar[...] = carry

        def head(h, _):
            rows = _head_rows(h)
            c1, c2, c3 = _split3(c_scr[pl.ds(h, 1), :])

            def normed(src, gain, mul):
                x = src[rows, :]
                rs = lax.rsqrt(jnp.mean(x * x, axis=0, keepdims=True) + EPS)
                return ((x * rs) * gain[rows, :]) * mul

            qat_ref[h] = _head_block(normed(q_ref, qg_ref, scale), [c1, c2, c3, 1.0, 1.0, 1.0], T).astype(BF16)
            kb = _head_block(normed(k_ref, kg_ref, 1.0), [1.0, 1.0, 1.0, -c1, -c2, -c3, 1.0, 1.0, 1.0], T)
            kat_ref[h] = kb.astype(BF16)
            ka_ref[h] = kb.T.astype(BF16)
            vat_ref[h] = _head_block(v_ref[rows, :], [1.0, 1.0, 1.0], T).astype(BF16)
            return 0

        lax.fori_loop(0, H, head, 0, unroll=HEAD_UNROLL)

    part = lambda j: pl.BlockSpec((D, T), lambda i: (j, i))
    colv = lambda n: pl.BlockSpec((n, 1), lambda i: (0, 0))
    tmaj = lambda r: pl.BlockSpec((H, None, r, T), lambda i: (0, i // per, 0, i % per))
    norm = pl.BlockSpec((H, T, HEAD_ROWS), lambda i: (0, i, 0))
    tshape = lambda r: jax.ShapeDtypeStruct((H, S // tq, r, tq), BF16)
    nshape = jax.ShapeDtypeStruct((H, S, HEAD_ROWS), BF16)
    return pl.pallas_call(
        body, name="fox_prep", grid=(S // T,),
        in_specs=[part(0), part(1), part(2), pl.BlockSpec((LANES, T), lambda i: (3 * D // LANES, i)),
                  colv(LANES), colv(D), colv(D), pl.BlockSpec((LANES, LANES), lambda i: (0, 0))],
        out_specs=[tmaj(HEAD_ROWS), tmaj(HEAD_ROWS), tmaj(HEAD_ROWS), norm],
        out_shape=[tshape(HEAD_ROWS), tshape(HEAD_ROWS), tshape(HEAD_ROWS), nshape],
        scratch_shapes=[pltpu.VMEM((LANES, T), F32), pltpu.VMEM((LANES, 1), F32)],
        compiler_params=_params(("arbitrary",)),
    )(ut, ut, ut, ut, b_f, qg, kg, _tri_matrix(False))


def _fox_bwd_prep(dot, ot, lse, qat, S, D, tq, plan=None):
    H = D // HEAD_DIM
    T = min(tq, PREP_LANES)
    per = tq // T

    def body(do_ref, o_ref, lse_ref, qat_ref, doat_ref, doa_ref, qat1_ref, qa1_ref):
        row = lax.broadcasted_iota(jnp.int32, (HEAD_ROWS, T), 0)

        def head(h, _):
            rows = _head_rows(h)
            do = do_ref[rows, :].astype(F32)
            delta = jnp.sum(do * o_ref[rows, :], axis=0, keepdims=True)
            db = _head_block(do, list(_split3(-delta)), T)
            doat_ref[h] = db.astype(BF16)
            doa_ref[h] = db.T.astype(BF16)
            qb = qat_ref[h].astype(F32)
            for i, e in enumerate(_split3(-lse_ref[h])):
                qb = jnp.where(row == LSE_ROW + i, e, qb)
            qat1_ref[h] = qb.astype(BF16)
            qa1_ref[h] = qb.T.astype(BF16)
            return 0

        lax.fori_loop(0, H, head, 0, unroll=HEAD_UNROLL)

    chan = pl.BlockSpec((D, T), lambda i: (0, i))
    tmaj = pl.BlockSpec((H, None, HEAD_ROWS, T), lambda i: (0, i // per, 0, i % per))
    norm = pl.BlockSpec((H, T, HEAD_ROWS), lambda i: (0, i, 0))
    tshape = jax.ShapeDtypeStruct((H, S // tq, HEAD_ROWS, tq), BF16)
    nshape = jax.ShapeDtypeStruct((H, S, HEAD_ROWS), BF16)
    return _hosted_call(body, "fox_bwd_prep", (S // T,), [chan, chan, pl.BlockSpec((H, 1, T), lambda i: (0, 0, i)), tmaj],
                        [tmaj, norm, tmaj, norm], [tshape, nshape, tshape, nshape], [], (dot, ot, lse, qat),
                        ("arbitrary",), plan)


def _causal(s, k_axis):
    t = min(s.shape)
    ki = lax.broadcasted_iota(jnp.int32, s.shape, k_axis) - (s.shape[k_axis] - t)
    qi = lax.broadcasted_iota(jnp.int32, s.shape, 1 - k_axis)
    return jnp.where(ki <= qi, s, NEG_INF)


def _attn_forward(ka, qat, vat, S, D, tq, plan=None):
    H = D // HEAD_DIM
    nq = S // tq
    G = 4

    def body(ka_ref, qat_ref, vat_ref, o_ref, o32_ref, lse_ref, m_scr, acc_scr):
        qi = pl.program_id(1)
        m_scr[...] = jnp.full_like(m_scr, NEG_INF)
        acc_scr[...] = jnp.zeros_like(acc_scr)

        def span(k0, n, diagonal):
            keys = pl.ds(pl.multiple_of(k0 * tq, tq), n * tq)
            s = [jnp.dot(ka_ref[g, keys, :], qat_ref[g], preferred_element_type=F32) for g in range(G)]
            if diagonal:
                s = [_causal(sg, 0) for sg in s]
            m_prev = [m_scr[g] for g in range(G)]
            m_new = [jnp.maximum(m_prev[g], jnp.max(s[g], axis=0, keepdims=True)) for g in range(G)]
            p = [jnp.exp(s[g] - m_new[g]).astype(BF16) for g in range(G)]
            for g in range(G):
                upd = jnp.dot(vat_ref[g, k0], p[g][0:tq], preferred_element_type=F32)
                for i in range(1, n):
                    upd = upd + jnp.dot(vat_ref[g, k0 + i], p[g][i * tq:(i + 1) * tq], preferred_element_type=F32)
                acc_scr[g] = jnp.exp(m_prev[g] - m_new[g]) * acc_scr[g] + upd
                m_scr[g] = m_new[g]

        def off_diagonal_pair(j, _):
            span(2 * j, 2, False)
            return 0

        lax.fori_loop(0, qi // 2, off_diagonal_pair, 0)
        pl.when(qi % 2 == 1)(lambda: span(qi - 1, 2, True))
        pl.when(qi % 2 == 0)(lambda: span(qi, 1, True))
        for g in range(G):
            l = acc_scr[g, ONES_ROW_V:ONES_ROW_V + 1, :]
            o = acc_scr[g, 0:HEAD_DIM, :] / l
            o_ref[g * HEAD_DIM:(g + 1) * HEAD_DIM, :] = o.astype(BF16)
            o32_ref[g * HEAD_DIM:(g + 1) * HEAD_DIM, :] = o
            lse_ref[g] = m_scr[g] + jnp.log(l)

    chan = pl.BlockSpec((G * HEAD_DIM, tq), lambda h, i: (h, i))
    stat = pl.BlockSpec((G, 1, tq), lambda h, i: (h, 0, i))
    return _hosted_call(
        body, "attn_forward", (H // G, nq),
        [pl.BlockSpec((G, S, HEAD_ROWS), lambda h, i: (h, 0, 0)),
         pl.BlockSpec((G, None, HEAD_ROWS, tq), lambda h, i: (h, i, 0, 0)),
         pl.BlockSpec((G, nq, HEAD_ROWS, tq), lambda h, i: (h, 0, 0, 0))],
        [chan, chan, stat],
        [jax.ShapeDtypeStruct((D, S), BF16), jax.ShapeDtypeStruct((D, S), F32), jax.ShapeDtypeStruct((H, 1, S), F32)],
        [pltpu.VMEM((G, 1, tq), F32), pltpu.VMEM((G, HEAD_ROWS, tq), F32)],
        (ka, qat, vat), ("arbitrary", "arbitrary"), plan)


def _attn_backward(qa, doa, qat, doat, ka, kat, vat, S, D, tq, plan=None):
    H = D // HEAD_DIM
    nq = S // tq
    G = 2

    def body(qa_ref, doa_ref, qat_ref, doat_ref, ka_ref, kat_ref, vat_ref, dq_ref, dk_ref, dv_ref, dk_scr, dv_scr):
        ki = pl.program_id(1)

        @pl.when(ki == 0)
        def _():
            dq_ref[...] = jnp.zeros_like(dq_ref)

        dk_scr[...] = jnp.zeros_like(dk_scr)
        dv_scr[...] = jnp.zeros_like(dv_scr)

        def span(q0, n, diagonal):
            rows = pl.ds(pl.multiple_of(q0 * tq, tq), n * tq)
            s = [jnp.dot(qa_ref[g, rows, :], kat_ref[g], preferred_element_type=F32) for g in range(G)]
            if diagonal:
                s = [_causal(sg, 1) for sg in s]
            p = [jnp.exp(sg) for sg in s]
            ds = [(p[g] * jnp.dot(doa_ref[g, rows, :], vat_ref[g], preferred_element_type=F32)).astype(BF16)
                  for g in range(G)]
            p = [pg.astype(BF16) for pg in p]
            for g in range(G):
                for i in range(n):
                    part = slice(i * tq, (i + 1) * tq)
                    dv_scr[g] += jnp.dot(doat_ref[g, q0 + i, 0:HEAD_DIM, :], p[g][part], preferred_element_type=F32)
                    dk_scr[g] += jnp.dot(qat_ref[g, q0 + i], ds[g][part], preferred_element_type=F32)
                dq_ref[g, rows, :] += jnp.dot(ds[g], ka_ref[g], preferred_element_type=F32)

        n_off = nq - 1 - ki
        odd = n_off % 2

        def off_diagonal_pair(j, _):
            span(ki + 1 + odd + 2 * j, 2, False)
            return 0

        pl.when(odd == 1)(lambda: span(ki, 2, True))
        pl.when(odd == 0)(lambda: span(ki, 1, True))
        lax.fori_loop(0, n_off // 2, off_diagonal_pair, 0)
        dk_ref[...] = dk_scr[...]
        for g in range(G):
            dv_ref[g * HEAD_DIM:(g + 1) * HEAD_DIM, :] = dv_scr[g].astype(BF16)

    whole = pl.BlockSpec((G, S, HEAD_ROWS), lambda h, i: (h, 0, 0))
    tiles = pl.BlockSpec((G, nq, HEAD_ROWS, tq), lambda h, i: (h, 0, 0, 0))
    one = pl.BlockSpec((G, None, HEAD_ROWS, tq), lambda h, i: (h, i, 0, 0))
    return _hosted_call(
        body, "attn_backward", (H // G, nq),
        [whole, whole, tiles, tiles, pl.BlockSpec((G, tq, HEAD_ROWS), lambda h, i: (h, i, 0)), one, one],
        [whole, pl.BlockSpec((G, HEAD_ROWS, tq), lambda h, i: (h, 0, i)),
         pl.BlockSpec((G * HEAD_DIM, tq), lambda h, i: (h, i))],
        [jax.ShapeDtypeStruct((H, S, HEAD_ROWS), F32), jax.ShapeDtypeStruct((H, HEAD_ROWS, S), F32),
         jax.ShapeDtypeStruct((D, S), BF16)],
        [pltpu.VMEM((G, HEAD_ROWS, tq), F32), pltpu.VMEM((G, HEAD_DIM, tq), F32)],
        (qa, doa, qat, doat, ka, kat, vat), ("arbitrary", "arbitrary"), plan)


def _fox_prep_bwd(ut, dq, dkt, dvt, b_f, qg, kg, S, D, tq):
    H = D // HEAD_DIM
    T = min(tq, PREP_LANES)
    nT = S // T
    NU = 3 * D + LANES
    scale = HEAD_DIM ** -0.5

    def body(q_ref, k_ref, f_ref, dq_ref, dk_ref, dv_ref, bf_ref, qg_ref, kg_ref, tri_ref,
             du_ref, dbf_ref, dqg_ref, dkg_ref, gq_acc, gk_acc, fcar, dc_scr):
        step = pl.program_id(0)

        @pl.when(step == 0)
        def _():
            for ref in (gq_acc, gk_acc, fcar, dbf_ref):
                ref[...] = jnp.zeros_like(ref)

        dc_scr[...] = jnp.zeros_like(dc_scr)

        def head(h, _):
            rows = _head_rows(h)
            dqb = dq_ref[h].T
            dkb = dk_ref[h]
            dc_scr[pl.ds(h, 1), :] = dqb[ONES_COL_K:ONES_COL_K + 1, :] - dkb[ONES_ROW_Q:ONES_ROW_Q + 1, :]
            for src, dsrc, gain, acc, mul, base in ((q_ref, dqb, qg_ref, gq_acc, scale, 0),
                                                    (k_ref, dkb, kg_ref, gk_acc, 1.0, D)):
                x = src[rows, :]
                rs = lax.rsqrt(jnp.mean(x * x, axis=0, keepdims=True) + EPS)
                xhat = x * rs
                dn = dsrc[0:HEAD_DIM, :] * mul
                acc[rows, :] += jnp.sum(dn * xhat, axis=1, keepdims=True)
                dxh = dn * gain[rows, :]
                dx = rs * (dxh - xhat * jnp.mean(dxh * xhat, axis=0, keepdims=True))
                du_ref[pl.ds(pl.multiple_of(base + h * HEAD_DIM, HEAD_DIM), HEAD_DIM), :] = dx.astype(BF16)
            return 0

        lax.fori_loop(0, H, head, 0, unroll=HEAD_UNROLL)
        du_ref[2 * D:3 * D, :] = dv_ref[...]
        dlf, carry = _lane_cumsum(dc_scr[...], tri_ref, fcar[...], True)
        fcar[...] = carry
        dfl = dlf * _sigmoid(-(f_ref[...] + bf_ref[...]))
        dbf_ref[...] += jnp.sum(dfl, axis=1, keepdims=True)
        du_ref[3 * D:NU, :] = dfl.astype(BF16)

        @pl.when(step == nT - 1)
        def _():
            for acc, ref in ((gq_acc, dqg_ref), (gk_acc, dkg_ref)):
                tot = jnp.zeros((HEAD_DIM, 1), F32)
                for h in range(H):
                    tot = tot + acc[h * HEAD_DIM:(h + 1) * HEAD_DIM, :]
                ref[...] = tot

    rev = lambda i: nT - 1 - i
    part = lambda j: pl.BlockSpec((D, T), lambda i: (j, rev(i)))
    colv = lambda n: pl.BlockSpec((n, 1), lambda i: (0, 0))
    return pl.pallas_call(
        body, name="fox_prep_bwd", grid=(nT,),
        in_specs=[part(0), part(1), pl.BlockSpec((LANES, T), lambda i: (3 * D // LANES, rev(i))),
                  pl.BlockSpec((H, T, HEAD_ROWS), lambda i: (0, rev(i), 0)),
                  pl.BlockSpec((H, HEAD_ROWS, T), lambda i: (0, 0, rev(i))), pl.BlockSpec((D, T), lambda i: (0, rev(i))),
                  colv(LANES), colv(D), colv(D), pl.BlockSpec((LANES, LANES), lambda i: (0, 0))],
        out_specs=[pl.BlockSpec((NU, T), lambda i: (0, rev(i))), colv(LANES), colv(HEAD_DIM), colv(HEAD_DIM)],
        out_shape=[jax.ShapeDtypeStruct((NU, S), BF16), jax.ShapeDtypeStruct((LANES, 1), F32),
                   jax.ShapeDtypeStruct((HEAD_DIM, 1), F32), jax.ShapeDtypeStruct((HEAD_DIM, 1), F32)],
        scratch_shapes=[pltpu.VMEM((D, 1), F32), pltpu.VMEM((D, 1), F32), pltpu.VMEM((LANES, 1), F32),
                        pltpu.VMEM((LANES, T), F32)],
        compiler_params=_params(("arbitrary",)),
    )(ut, ut, ut, dq, dkt, dvt, b_f, qg, kg, _tri_matrix(True))


def _block_diag_tiles(w):
    n = w.shape[0]
    per = min(MXU_DIM, n * LRU_BLOCK_DIM) // LRU_BLOCK_DIM
    eye = jnp.eye(per, dtype=w.dtype)
    w5 = w.reshape(n // per, per, LRU_BLOCK_DIM, 1, LRU_BLOCK_DIM) * eye[None, :, None, :, None]
    return w5.reshape(n // per, per * LRU_BLOCK_DIM, per * LRU_BLOCK_DIM).astype(BF16)


def _block_diag_extract(t, n):
    per = t.shape[-1] // LRU_BLOCK_DIM
    eye = jnp.eye(per, dtype=t.dtype)
    t5 = t.reshape(n // per, per, LRU_BLOCK_DIM, per, LRU_BLOCK_DIM) * eye[None, :, None, :, None]
    return t5.sum(axis=3).reshape(n, LRU_BLOCK_DIM, LRU_BLOCK_DIM)


def _local_step(x, tgt, small, wv, grad_view, comm=None):
    S, D = x.shape
    F = 4 * D
    H = D // HEAD_DIM
    nblk = D // LRU_BLOCK_DIM
    NU = 3 * D + LANES
    tq = max(LANES, min(512, S // 4))
    assert S % tq == 0
    vec = lambda a: a.reshape(1, -1).astype(F32)
    col = lambda a: a.reshape(-1, 1).astype(F32)
    mix_g, mlp_g = small["mix_norm"], small["mlp_norm"]
    conv_b = vec(small["lru_conv_b"])
    wr_bd, wi_bd = _block_diag_tiles(small["lru_w_r"][0]), _block_diag_tiles(small["lru_w_i"][0])
    b_r, b_i, lam = vec(small["lru_b_r"]), vec(small["lru_b_i"]), vec(small["lru_lambda"])
    b_f = jnp.pad(col(small["fox_b_f"]), ((0, LANES - H), (0, 0)))
    qg, kg = jnp.tile(col(small["fox_q_gain"]), (H, 1)), jnp.tile(col(small["fox_k_gain"]), (H, 1))
    X = lambda a: _View(a)
    grads = {}
    gout = functools.partial(grad_view, grads)

    def hosted(name, fn, *args):
        plan = comm.before(name, grads) if comm is not None else None
        res, side = fn(*args, plan=plan)
        if plan is not None:
            comm.after(name, side, wv)
        return res

    def hosted_mm(name, *args, **kw):
        plan = comm.before(name, grads) if comm is not None else None
        if plan is None:
            return _matmul(name, *args, **kw)
        res, side = _matmul(name, *args, plan=plan, **kw)
        comm.after(name, side, wv)
        return res

    two = lambda: [_fresh(S, D, F32), _fresh(S, D, BF16)]

    def mlp_up(l, hm):
        return hosted_mm(f"mlp{l}_up", X(hm), wv[f"w1_{l}"], S, F, D, outs=[_fresh(S, F, BF16)], epilogue=_ep_relu2)[0]

    def mlp_bwd(l, xin, hm, act, d, db):
        (dz,) = hosted_mm(f"mlp{l}_dact", X(db), wv[f"w2_{l}"], S, F, D, tb=True, outs=[_fresh(S, F, BF16)],
                          epilogue=_ep_drelu2, extras=[X(act)])
        (grads[f"w2_{l}"],) = _matmul(f"mlp{l}_dw2", X(act), X(db), F, D, S, ta=True, outs=[gout(f"w2_{l}")],
                                      epilogue=_ep_store)
        (grads[f"w1_{l}"],) = _matmul(f"mlp{l}_dw1", X(hm), X(dz), D, F, S, ta=True, outs=[gout(f"w1_{l}")],
                                      epilogue=_ep_store)
        return _matmul(f"mlp{l}_dhm", X(dz), wv[f"w1_{l}"], S, D, F, tb=True, outs=two(), n_sums=1,
                       epilogue=_ep_norm_bwd, extras=[X(xin), X(d)], vecs=[mlp_g[l:l + 1]])

    (h0,) = hosted("mix0_norm", _rms_fwd, "mix0_norm", x, mix_g[0:1], S, D)
    (u0,) = hosted_mm("lru_in", X(h0), wv["lru_in"], S, 2 * D, D, outs=[_fresh(S, 2 * D, F32)], epilogue=_ep_store)
    conv_w = small["conv_w"]
    y, xc, r, ig, hs = hosted("lru_fwd", _lru_fwd, u0, conv_w, conv_b, wr_bd, b_r, wi_bd, b_i, lam, S, D)
    x1, hm0 = _matmul("lru_out", X(y), wv["lru_out"], S, D, D, outs=two(), epilogue=_ep_resid_norm, extras=[X(x)],
                      vecs=[mlp_g[0:1]])
    act0 = mlp_up(0, hm0)
    x2, h1 = hosted_mm("mlp0_down", X(act0), wv["w2_0"], S, D, F, outs=two(), epilogue=_ep_resid_norm, extras=[X(x1)],
                       vecs=[mix_g[1:2]])
    (u1,) = _matmul("fox_in", wv["fox_in"], X(h1), NU, S, D, tb=True, outs=[_fresh(NU, S, F32)], epilogue=_ep_store)
    qat, kat, vat, ka = _fox_prep(u1, b_f, qg, kg, S, D, tq)
    o, o32, lse = hosted("attn_forward", _attn_forward, ka, qat, vat, S, D, tq)
    x3, hm1 = _matmul("fox_out", X(o), wv["fox_out"], S, D, D, ta=True, outs=two(), epilogue=_ep_resid_norm,
                      extras=[X(x2)], vecs=[mlp_g[1:2]])
    act1 = mlp_up(1, hm1)
    (x4,) = _matmul("mlp1_down", X(act1), wv["w2_1"], S, D, F, outs=[_fresh(S, D, F32)], epilogue=_ep_resid,
                    extras=[X(x3)])
    loss, d4, d4b = _loss_head(x4, tgt, S, D)

    d3, d3b, dg_mlp1 = mlp_bwd(1, x3, hm1, act1, d4, d4b)
    (do,) = _matmul("fox_dout", wv["fox_out"], X(d3b), D, S, D, tb=True, outs=[_fresh(D, S, BF16)], epilogue=_ep_store)
    (grads["fox_out"],) = _matmul("fox_dwout", X(o), X(d3b), D, D, S, outs=[gout("fox_out")], epilogue=_ep_store)
    doat, doa, qat1, qa1 = hosted("fox_bwd_prep", _fox_bwd_prep, do, o32, lse, qat, S, D, tq)
    dqn, dkn, dv = hosted("attn_backward", _attn_backward, qa1, doa, qat1, doat, ka, kat, vat, S, D, tq)
    du1, dbf, dqg, dkg = _fox_prep_bwd(u1, dqn, dkn, dv, b_f, qg, kg, S, D, tq)
    (grads["fox_in"],) = _matmul("fox_dwin", X(du1), X(h1), NU, D, S, outs=[gout("fox_in")], epilogue=_ep_store)
    d2, d2b, dg_mix1 = hosted_mm("fox_dh", X(du1), wv["fox_in"], S, D, NU, ta=True, outs=two(), n_sums=1,
                               epilogue=_ep_norm_bwd, extras=[X(x2), X(d3)], vecs=[mix_g[1:2]])
    d1, d1b, dg_mlp0 = mlp_bwd(0, x1, hm0, act0, d2, d2b)
    (grads["lru_out"],) = _matmul("lru_dwout", X(y), X(d1b), D, D, S, ta=True, outs=[gout("lru_out")],
                                  epilogue=_ep_store)
    (dy,) = hosted_mm("lru_dout", X(d1b), wv["lru_out"], S, D, D, tb=True, outs=[_fresh(S, D, F32)],
                      epilogue=_ep_store)
    du0, dcw, dcb, dlam, dbr, dbi, dwr, dwi = hosted("lru_bwd", _lru_bwd, dy, u0, xc, r, ig, hs, conv_w, wr_bd, wi_bd,
                                                     lam, S, D)
    (grads["lru_in"],) = _matmul("lru_dwin", X(h0), X(du0), D, 2 * D, S, ta=True, outs=[gout("lru_in")],
                                 epilogue=_ep_store)
    gx, dg_mix0 = hosted_mm("lru_dh", X(du0), wv["lru_in"], S, D, 2 * D, tb=True, outs=[_fresh(S, D, F32)], n_sums=1,
                            epilogue=lambda *a: _ep_norm_bwd(*a)[::2], extras=[X(x), X(d1)], vecs=[mix_g[0:1]])

    grads.update(
        mix_norm=jnp.concatenate([dg_mix0, dg_mix1], axis=0), mlp_norm=jnp.concatenate([dg_mlp0, dg_mlp1], axis=0),
        conv_w=dcw, lru_conv_b=dcb, lru_w_r=_block_diag_extract(dwr, nblk)[None], lru_b_r=dbr.reshape(1, nblk, -1),
        lru_w_i=_block_diag_extract(dwi, nblk)[None], lru_b_i=dbi.reshape(1, nblk, -1), lru_lambda=dlam,
        fox_b_f=dbf[:H].reshape(1, H), fox_q_gain=dqg.reshape(1, -1), fox_k_gain=dkg.reshape(1, -1))
    return loss, gx, grads


def _place():
    x, y, c = lax.axis_index("x"), lax.axis_index("y"), lax.axis_index("c")
    chips = [(1 - x, y), (x, 1 - y), (1 - x, 1 - y)]
    return x, y, c, 2 * x + y, chips


BOUNCE_BYTES = 1 << 20


def _bounce_shape(rows, cols, dtype):
    chunk = rows
    while chunk % 2 == 0 and chunk > 16 and chunk * cols * jnp.dtype(dtype).itemsize > BOUNCE_BYTES:
        chunk //= 2
    return pltpu.VMEM((2, chunk, cols), dtype)


def _bounce_copy(src, dst, buf, sem):
    chunk = buf.shape[1]
    n = src.shape[0] // chunk
    cin = lambda i: pltpu.make_async_copy(src.at[pl.ds(i * chunk, chunk)], buf.at[i % 2], sem.at[i % 2])
    cout = lambda i: pltpu.make_async_copy(buf.at[i % 2], dst.at[pl.ds(i * chunk, chunk)], sem.at[2 + i % 2])
    cin(0).start()
    for i in range(n):
        cin(i).wait()
        if i + 1 < n:
            if i >= 1:
                cout(i - 1).wait()
            cin(i + 1).start()
        cout(i).start()
    if n >= 2:
        cout(n - 2).wait()
    cout(n - 1).wait()


def _hbm_call(body, name, arrays, out_shape, n_dma_sems, bounce=()):
    scratch = [pltpu.SemaphoreType.DMA((k,)) for k in n_dma_sems]
    for rows, cols, dtype in bounce:
        scratch += [_bounce_shape(rows, cols, dtype), pltpu.SemaphoreType.DMA((4,))]
    return pl.pallas_call(
        body, name=name, in_specs=[ANY] * len(arrays), out_specs=[ANY] * len(out_shape), out_shape=out_shape,
        scratch_shapes=scratch,
        compiler_params=pltpu.CompilerParams(has_side_effects=True, vmem_limit_bytes=VMEM_LIMIT),
    )(*arrays)


class _Gather:
    def __init__(self, shards):
        n = self.n = len(shards)
        self.operands = list(shards)
        self.out_shape = [jax.ShapeDtypeStruct((N_CHIPS,) + tuple(a.shape), a.dtype) for a in shards]
        self.scratch = [pltpu.SemaphoreType.DMA((3 * n,)) for _ in range(4)]
        for a in shards:
            self.scratch += [_bounce_shape(a.shape[0], a.shape[1], a.dtype), pltpu.SemaphoreType.DMA((4,))]

    def _copies(self, ins, outs, scr):
        send, recv, fsend, frecv = scr[:4]
        x, y, c, s, chips = _place()

        def rows(a, chip_idx, which):
            hr = ins[a].shape[0] // 2
            return outs[a].at[chip_idx, pl.ds(which * hr, hr)]

        def landed(a, j, core):
            return rows(a, 2 * chips[j][0] + chips[j][1], core)

        def ici(a, j, mine):
            hr = ins[a].shape[0] // 2
            src, dst = (ins[a].at[pl.ds(c * hr, hr)], rows(a, s, c)) if mine else (landed(a, j, c),) * 2
            return pltpu.make_async_remote_copy(src_ref=src, dst_ref=dst, send_sem=send.at[3 * a + j],
                                                recv_sem=recv.at[3 * a + j], device_id=(*chips[j], c),
                                                device_id_type=MESH)

        def d2d(a, j, mine):
            ref = landed(a, j, c if mine else 1 - c)
            return pltpu.make_async_remote_copy(src_ref=ref, dst_ref=ref, send_sem=fsend.at[3 * a + j],
                                                recv_sem=frecv.at[3 * a + j], device_id=(x, y, 1 - c),
                                                device_id_type=MESH)

        return ici, d2d, s

    def start(self, ins, outs, scr):
        ici, _, _ = self._copies(ins, outs, scr)
        for a in range(self.n):
            for j in range(3):
                ici(a, j, True).start()

    def middle(self, ins, outs, scr):
        ici, d2d, s = self._copies(ins, outs, scr)
        for a in range(self.n):
            _bounce_copy(ins[a], outs[a].at[s], scr[4 + 2 * a], scr[5 + 2 * a])
        for a in range(self.n):
            for j in range(3):
                ici(a, j, False).wait_recv()
                d2d(a, j, True).start()

    def finish(self, ins, outs, scr):
        ici, d2d, _ = self._copies(ins, outs, scr)
        for a in range(self.n):
            for j in range(3):
                d2d(a, j, False).wait_recv()
        for a in range(self.n):
            for j in range(3):
                ici(a, j, True).wait_send()
                d2d(a, j, True).wait_send()


def _run_plan(name, plan):
    k_in, k_out = len(plan.operands), len(plan.out_shape)

    def body(*refs):
        parts = (refs[:k_in], refs[k_in:k_in + k_out], refs[k_in + k_out:])
        plan.start(*parts)
        plan.middle(*parts)
        plan.finish(*parts)

    return pl.pallas_call(
        body, name=name, in_specs=[ANY] * k_in, out_specs=[ANY] * k_out, out_shape=plan.out_shape,
        scratch_shapes=plan.scratch,
        compiler_params=pltpu.CompilerParams(has_side_effects=True, vmem_limit_bytes=VMEM_LIMIT),
    )(*plan.operands)


def _hosted_call(body, name, grid, in_specs, out_specs, out_shape, scratch_shapes, operands, sem, plan=None):
    if plan is None:
        res = pl.pallas_call(body, name=name, grid=grid, in_specs=in_specs, out_specs=out_specs, out_shape=out_shape,
                             scratch_shapes=scratch_shapes, compiler_params=_params(sem))(*operands)
        return res, None
    n_in, n_out, n_scr = len(in_specs), len(out_specs), len(scratch_shapes)
    k_in, k_out = len(plan.operands), len(plan.out_shape)
    total = int(np.prod(grid))
    late = max(0, total - 1 - max(1, total // 8))

    def hosted(*refs):
        ins, refs = refs[:n_in], refs[n_in:]
        p_ins, refs = refs[:k_in], refs[k_in:]
        outs, refs = refs[:n_out], refs[n_out:]
        p_outs, refs = refs[:k_out], refs[k_out:]
        scr, p_scr = refs[:n_scr], refs[n_scr:]
        step = pl.program_id(0)
        for d in range(1, len(grid)):
            step = step * grid[d] + pl.program_id(d)
        pl.when(step == 0)(lambda: plan.start(p_ins, p_outs, p_scr))
        body(*ins, *outs, *scr)
        pl.when(step == late)(lambda: plan.middle(p_ins, p_outs, p_scr))
        pl.when(step == total - 1)(lambda: plan.finish(p_ins, p_outs, p_scr))

    res = pl.pallas_call(
        hosted, name=name, grid=grid, in_specs=list(in_specs) + [ANY] * k_in, out_specs=list(out_specs) + [ANY] * k_out,
        out_shape=list(out_shape) + plan.out_shape, scratch_shapes=list(scratch_shapes) + plan.scratch,
        compiler_params=pltpu.CompilerParams(dimension_semantics=sem, vmem_limit_bytes=VMEM_LIMIT,
                                             has_side_effects=True),
    )(*operands, *plan.operands)
    return res[:n_out], res[n_out:]


def _all_gather(name, shards):
    return _run_plan(name, _Gather(shards))


class _Swap:
    def __init__(self, arrs):
        self.n = len(arrs)
        self.operands = list(arrs)
        self.out_shape = [jax.ShapeDtypeStruct((a.shape[0], a.shape[1] // 2, a.shape[2]), a.dtype) for a in arrs]
        self.scratch = [pltpu.SemaphoreType.DMA((self.n,)) for _ in range(2)]

    def _copy(self, ins, outs, scr, a):
        x, y, c, _, _ = _place()
        hr = ins[a].shape[1] // 2
        return pltpu.make_async_remote_copy(
            src_ref=ins[a].at[:, pl.ds((1 - c) * hr, hr)], dst_ref=outs[a], send_sem=scr[0].at[a],
            recv_sem=scr[1].at[a], device_id=(x, y, 1 - c), device_id_type=MESH)

    def start(self, ins, outs, scr):
        for a in range(self.n):
            self._copy(ins, outs, scr, a).start()

    def middle(self, ins, outs, scr):
        pass

    def finish(self, ins, outs, scr):
        for a in range(self.n):
            self._copy(ins, outs, scr, a).wait()


class _Scatter:
    def __init__(self, parts):
        n = self.n = len(parts)
        self.operands = list(parts)
        self.out_shape = [jax.ShapeDtypeStruct(a.shape, a.dtype) for a in parts]
        self.scratch = [pltpu.SemaphoreType.DMA((3 * n,)) for _ in range(2)]
        for a in parts:
            self.scratch += [_bounce_shape(a.shape[1], a.shape[2], a.dtype), pltpu.SemaphoreType.DMA((4,))]

    def _copy(self, ins, outs, scr, a, j, mine):
        x, y, c, s, chips = _place()
        t = 2 * chips[j][0] + chips[j][1]
        return pltpu.make_async_remote_copy(
            src_ref=ins[a].at[t], dst_ref=outs[a].at[s if mine else t], send_sem=scr[0].at[3 * a + j],
            recv_sem=scr[1].at[3 * a + j], device_id=(*chips[j], c), device_id_type=MESH)

    def start(self, ins, outs, scr):
        for a in range(self.n):
            for j in range(3):
                self._copy(ins, outs, scr, a, j, True).start()

    def middle(self, ins, outs, scr):
        s = _place()[3]
        for a in range(self.n):
            _bounce_copy(ins[a].at[s], outs[a].at[s], scr[2 + 2 * a], scr[3 + 2 * a])

    def finish(self, ins, outs, scr):
        for a in range(self.n):
            for j in range(3):
                self._copy(ins, outs, scr, a, j, False).wait_recv()
        for a in range(self.n):
            for j in range(3):
                self._copy(ins, outs, scr, a, j, True).wait_send()


def _pair_gather(name, halves):
    n = len(halves)

    def body(*refs):
        ins, outs = refs[:n], refs[n:2 * n]
        send, recv = refs[2 * n:2 * n + 2]
        stage = refs[2 * n + 2:]
        x, y, c, _, _ = _place()
        cps = []
        for a in range(n):
            hr = ins[a].shape[0]
            cp = pltpu.make_async_remote_copy(
                src_ref=ins[a], dst_ref=outs[a].at[pl.ds(c * hr, hr)], send_sem=send.at[a], recv_sem=recv.at[a],
                device_id=(x, y, 1 - c), device_id_type=MESH)
            cp.start()
            cps.append((cp, hr))
        for a, (cp, hr) in enumerate(cps):
            _bounce_copy(ins[a], outs[a].at[pl.ds(c * hr, hr)], stage[2 * a], stage[2 * a + 1])
        for a, (cp, hr) in enumerate(cps):
            cp.wait_send()
            theirs = outs[a].at[pl.ds((1 - c) * hr, hr)]
            pltpu.make_async_remote_copy(src_ref=theirs, dst_ref=theirs, send_sem=send.at[a], recv_sem=recv.at[a],
                                         device_id=(x, y, 1 - c), device_id_type=MESH).wait_recv()

    out_shape = [jax.ShapeDtypeStruct((2 * a.shape[0], a.shape[1]), a.dtype) for a in halves]
    return _hbm_call(body, name, halves, out_shape, (n, n),
                     bounce=[(a.shape[0], a.shape[1], a.dtype) for a in halves])


def _row_tile(rows, cols, itemsize, n_bufs):
    budget = VMEM_LIMIT // 2
    for t in range(min(rows, 1024) // 16 * 16, 0, -16):
        if rows % t == 0 and 2 * n_bufs * t * cols * itemsize <= budget:
            return t
    return rows


def _pair_add(name, g, gsib, core, out_dtype):
    _, r, cols = g.shape
    hr = r // 2
    t = _row_tile(hr, cols, 4, 3)
    per = hr // t

    def body(core_ref, a_ref, b_ref, o_ref):
        o_ref[...] = (a_ref[...].astype(F32) + b_ref[...].astype(F32)).astype(o_ref.dtype)

    grid_spec = pltpu.PrefetchScalarGridSpec(
        num_scalar_prefetch=1, grid=(N_CHIPS, per),
        in_specs=[pl.BlockSpec((None, t, cols), lambda s, i, core: (s, core[0] * per + i, 0)),
                  pl.BlockSpec((None, t, cols), lambda s, i, core: (s, i, 0))],
        out_specs=pl.BlockSpec((None, t, cols), lambda s, i, core: (s, i, 0)))
    return pl.pallas_call(body, name=name, grid_spec=grid_spec,
                          out_shape=jax.ShapeDtypeStruct((N_CHIPS, hr, cols), out_dtype),
                          compiler_params=_params(("arbitrary", "arbitrary")))(core, g, gsib)


def _chip_sum(name, parts):
    _, hr, cols = parts.shape
    t = _row_tile(hr, cols, 4, 5)

    def body(p_ref, o_ref):
        o_ref[...] = ((p_ref[0].astype(F32) + p_ref[1].astype(F32)) + p_ref[2].astype(F32)) + p_ref[3].astype(F32)

    return pl.pallas_call(
        body, name=name, grid=(hr // t,), in_specs=[pl.BlockSpec((N_CHIPS, t, cols), lambda i: (0, i, 0))],
        out_specs=pl.BlockSpec((t, cols), lambda i: (i, 0)), out_shape=jax.ShapeDtypeStruct((hr, cols), F32),
        compiler_params=_params(("arbitrary",)))(parts)


def _pair_partials(tag, arrs, sib, wire_dtypes, core):
    return _Scatter([_pair_add(f"{tag}_pair_add{i}", g, gs, core, dt)
                     for i, (g, gs, dt) in enumerate(zip(arrs, sib, wire_dtypes))])


def _finish_reduce(tag, scattered):
    halves = [_chip_sum(f"{tag}_chip_sum{i}", p) for i, p in enumerate(scattered)]
    return _pair_gather(f"{tag}_pair_gather", halves)


def _adamw(name, w, g_parts, m, v):
    thin = w.ndim == 3
    rows, cols = w.shape[0], w.shape[-1]
    n_parts = len(g_parts)
    part_rows = rows // n_parts
    t = max(d for d in range(1, 257) if part_rows % d == 0) if thin else _row_tile(part_rows, cols, 4, 7 + n_parts)
    per = part_rows // t
    c1 = 1.0 - ADAM_B1 ** ADAM_STEP
    c2 = 1.0 - ADAM_B2 ** ADAM_STEP

    def body(w_ref, m_ref, v_ref, *refs):
        g_refs, (go_ref, d_ref, nm_ref, nv_ref) = refs[:n_parts], refs[n_parts:]
        g = g_refs[0][...]
        for k in range(1, n_parts):
            g = jnp.where(pl.program_id(0) >= k * per, g_refs[k][...], g)
        go_ref[...] = g
        m = ADAM_B1 * m_ref[...] + (1.0 - ADAM_B1) * g
        v = ADAM_B2 * v_ref[...] + (1.0 - ADAM_B2) * (g * g)
        nm_ref[...] = m
        nv_ref[...] = v
        d_ref[...] = -ADAM_LR * ((m / c1) / (jnp.sqrt(v / c2) + ADAM_EPS) + ADAM_WD * w_ref[...])

    block = (t, 1, cols) if thin else (t, cols)
    at = lambda r: (r, 0, 0) if thin else (r, 0)
    spec = pl.BlockSpec(block, lambda i: at(i))
    g_specs = [pl.BlockSpec(block, lambda i, k=k: at(jnp.clip(i - k * per, 0, per - 1))) for k in range(n_parts)]
    shp = jax.ShapeDtypeStruct(w.shape, F32)
    return pl.pallas_call(body, name=name, grid=(rows // t,), in_specs=[spec] * 3 + g_specs, out_specs=[spec] * 4,
                          out_shape=[shp] * 4, compiler_params=_params(("arbitrary",)))(w, m, v, *g_parts)


_WEIGHTS = ["mix_norm", "mlp_norm", "mlp_w1", "mlp_w2", "lru_w_in", "lru_conv_w", "lru_conv_b", "lru_w_r", "lru_b_r",
            "lru_w_i", "lru_b_i", "lru_lambda", "lru_w_out", "fox_w_in", "fox_b_f", "fox_q_gain", "fox_k_gain",
            "fox_w_out"]
_REPLICATED = ["mix_norm", "mlp_norm", "lru_conv_b", "lru_w_r", "lru_b_r", "lru_w_i", "lru_b_i", "lru_lambda",
               "fox_b_f", "fox_q_gain", "fox_k_gain"]
_PACK_TILE = 2 * SUBLANES * LANES


def _as2d(a):
    return a.reshape(-1, a.shape[-1])


def kernel(x, mix_norm, mlp_norm, mlp_w1, mlp_w2, lru_w_in, lru_conv_w, lru_conv_b, lru_w_r, lru_b_r, lru_w_i, lru_b_i, lru_lambda, lru_w_out, fox_w_in, fox_b_f, fox_q_gain, fox_k_gain, fox_w_out, loss_target, m_mix_norm, m_mlp_norm, m_mlp_w1, m_mlp_w2, m_lru_w_in, m_lru_conv_w, m_lru_conv_b, m_lru_w_r, m_lru_b_r, m_lru_w_i, m_lru_b_i, m_lru_lambda, m_lru_w_out, m_fox_w_in, m_fox_b_f, m_fox_q_gain, m_fox_k_gain, m_fox_w_out, v_mix_norm, v_mlp_norm, v_mlp_w1, v_mlp_w2, v_lru_w_in, v_lru_conv_w, v_lru_conv_b, v_lru_w_r, v_lru_b_r, v_lru_w_i, v_lru_b_i, v_lru_lambda, v_lru_w_out, v_fox_w_in, v_fox_b_f, v_fox_q_gain, v_fox_k_gain, v_fox_w_out):
    args = dict(locals())
    W = {n: args[n] for n in _WEIGHTS}
    Mo = {n: args["m_" + n] for n in _WEIGHTS}
    Vo = {n: args["v_" + n] for n in _WEIGHTS}
    S, D = x.shape[1], x.shape[2]
    F = 4 * D
    H = D // HEAD_DIM
    NU = 3 * D + LANES
    FQ, DQ = F // N_CHIPS, D // N_CHIPS
    nfox = fox_w_in.shape[-1]
    chip = 2 * lax.axis_index("x") + lax.axis_index("y")
    core = lax.axis_index("c").astype(jnp.int32).reshape(1)

    cw_flat = jnp.pad(lru_conv_w.reshape(-1), (0, _PACK_TILE - CONV_WIDTH * DQ)).reshape(2 * SUBLANES, LANES)
    w1s, w2s = mlp_w1.astype(BF16), mlp_w2.astype(BF16)
    wv = {}
    small = {n: W[n] for n in _REPLICATED}
    scattered = {}
    members = {"g1": ["w2_1", "w1_1", "fox_out"], "g2": ["fox_in"], "g3": ["w2_0", "w1_0"], "g4": ["lru_out", "lru_in"]}
    swap_at = {"fox_bwd_prep": "g1", "fox_dh": "g2", "lru_dout": "g3"}
    scatter_at = {"attn_backward": "g1", "mlp0_dact": "g2", "lru_bwd": "g3", "lru_dh": "g4"}
    swapped = {}

    fox_rows = -(-nfox // (4 * SUBLANES)) * (4 * SUBLANES)
    fox_t = jnp.pad(jnp.transpose(fox_w_in[0]).astype(BF16), ((0, fox_rows - nfox), (0, 0)))

    def shard_major(name, g):
        if name == "fox_in":
            return jnp.pad(g[:nfox * N_CHIPS].reshape(N_CHIPS, nfox, D), ((0, 0), (0, fox_rows - nfox), (0, 0)))
        return g

    class Comm:
        @staticmethod
        def before(name, grads):
            if name == "mix0_norm":
                return _Gather([lru_w_in[0].astype(BF16)])
            if name == "lru_in":
                return _Gather([lru_w_out[0].astype(BF16), cw_flat])
            if name == "lru_fwd":
                return _Gather([w1s[0]])
            if name == "mlp0_up":
                return _Gather([w2s[0]])
            if name == "mlp0_down":
                return _Gather([fox_t])
            if name == "attn_forward":
                return _Gather([fox_w_out[0].astype(BF16), w1s[1], w2s[1]])
            if name in swap_at:
                group = swap_at[name]
                swapped[group] = [[shard_major(n, grads[n]) for n in members[group]], None]
                return _Swap(swapped[group][0])
            if name in scatter_at:
                group = scatter_at[name]
                if group not in swapped:
                    arrs = [shard_major(n, grads[n]) for n in members[group]]
                    swapped[group] = [arrs, _run_plan(f"{group}_pair_swap", _Swap(arrs))]
                arrs, sib = swapped[group]
                return _pair_partials(group, arrs, sib, [BF16] * len(arrs), core)
            return None

        @staticmethod
        def after(name, res, wv):
            if name == "mix0_norm":
                wv.update(lru_in=_View(res[0], "cs"))
            elif name == "lru_in":
                wv.update(lru_out=_View(res[0], "rs"))
                taps = res[1].reshape(N_CHIPS, -1)[:, :CONV_WIDTH * DQ].reshape(N_CHIPS, CONV_WIDTH, DQ)
                small["conv_w"] = jnp.transpose(taps, (1, 0, 2)).reshape(CONV_WIDTH, D)
            elif name == "lru_fwd":
                wv.update(w1_0=_View(res[0], "cs"))
            elif name == "mlp0_up":
                wv.update(w2_0=_View(res[0], "rs"))
            elif name == "mlp0_down":
                fox_full = jnp.concatenate([res[0][s, :nfox] for s in range(N_CHIPS)], axis=0)
                wv.update(fox_in=_View(jnp.pad(fox_full, ((0, NU - fox_full.shape[0]), (0, 0)))))
            elif name == "attn_forward":
                wv.update(fox_out=_View(res[0], "rs"), w1_1=_View(res[1], "cs"), w2_1=_View(res[2], "rs"))
            elif name in swap_at:
                swapped[swap_at[name]][1] = res
            else:
                scattered.update(zip(members[scatter_at[name]], res))

    def grad_view(grads, name):
        if name in ("w1_0", "w1_1"):
            return _View(None, "cs", shape=(N_CHIPS, D, FQ), dtype=BF16)
        if name in ("w2_0", "w2_1"):
            return _View(None, "rs", shape=(N_CHIPS, FQ, D), dtype=BF16)
        if name == "lru_in":
            return _View(None, "cs", shape=(N_CHIPS, D, 2 * D // N_CHIPS), dtype=BF16)
        if name in ("lru_out", "fox_out"):
            return _View(None, "rs", shape=(N_CHIPS, DQ, D), dtype=BF16)
        return _View(None, shape=(NU, D), dtype=BF16)

    loss, gx, grads = _local_step(x[0], loss_target[0], small, wv, grad_view, Comm)

    pack_names = _REPLICATED + ["conv_w"]
    flat = jnp.concatenate([grads[n].reshape(-1).astype(F32) for n in pack_names] + [loss.reshape(-1)])
    per_chip = -(-flat.shape[0] // (N_CHIPS * _PACK_TILE)) * _PACK_TILE
    pack = jnp.pad(flat, (0, N_CHIPS * per_chip - flat.shape[0])).reshape(N_CHIPS, per_chip // LANES, LANES)
    pack_sib = _run_plan("pack_pair_swap", _Swap([pack]))
    (scattered["pack"],) = _run_plan("pack_chip_scatter", _pair_partials("pack", [pack], pack_sib, [F32], core))
    order = ["w1_0", "w1_1", "w2_0", "w2_1", "lru_in", "lru_out", "fox_in", "fox_out", "pack"]
    red = dict(zip(order, _finish_reduce("grads", [scattered[n] for n in order])))
    (all_pack,) = _all_gather("gather_small_grads", [red["pack"]])
    all_flat = all_pack.reshape(-1)
    G = {}
    off = 0
    for n in pack_names:
        shape = grads[n].shape if n == "conv_w" else W[n].shape
        size = int(np.prod(shape))
        G[n] = all_flat[off:off + size].reshape(shape)
        off += size
    total = all_flat[off]
    G["lru_conv_w"] = lax.dynamic_slice_in_dim(G.pop("conv_w"), chip * DQ, DQ, axis=1)[None]
    parts = {n: [_as2d(G[n])] for n in G}
    parts.update(mlp_w1=[red["w1_0"], red["w1_1"]], mlp_w2=[red["w2_0"], red["w2_1"]], lru_w_in=[red["lru_in"]],
                 lru_w_out=[red["lru_out"]], fox_w_in=[red["fox_in"][:nfox, None, :]], fox_w_out=[red["fox_out"]])

    delta, new_m, new_v = {}, {}, {}
    for n in _WEIGHTS:
        if n == "fox_w_in":
            to_thin = lambda a: jnp.transpose(a, (2, 0, 1))
            res = _adamw(f"adamw_{n}", to_thin(W[n]), parts[n], to_thin(Mo[n]), to_thin(Vo[n]))
            G[n], delta[n], new_m[n], new_v[n] = (jnp.transpose(t, (1, 2, 0)) for t in res)
            continue
        go, d, nm, nv = _adamw(f"adamw_{n}", _as2d(W[n]), parts[n], _as2d(Mo[n]), _as2d(Vo[n]))
        G[n], delta[n], new_m[n], new_v[n] = (t.reshape(W[n].shape) for t in (go, d, nm, nv))

    return (total, gx[None], *[G[n] for n in _WEIGHTS], *[delta[n] for n in _WEIGHTS],
            *[new_m[n] for n in _WEIGHTS], *[new_v[n] for n in _WEIGHTS])
```

```python
import functools

import numpy as np
import jax
import jax.numpy as jnp
from jax import lax
from jax.experimental import pallas as pl
from jax.experimental.pallas import tpu as pltpu

F32 = jnp.float32
BF16 = jnp.bfloat16

HEAD_DIM = 64
LRU_BLOCK_DIM = 64
CONV_WIDTH = 4
LRU_C = 8.0
EPS = 1e-6
NEG_INF = -1e30
ADAM_LR = 0.001
ADAM_B1 = 0.9
ADAM_B2 = 0.999
ADAM_EPS = 1e-08
ADAM_WD = 0.01
ADAM_STEP = 10

N_CHIPS = 4
LANES = 128
SUBLANES = 8
MXU_DIM = 256
VMEM_LIMIT = 52 * 1024 * 1024
MATMUL_TILES = (1024, 640, 512, 256, 128)
MATMUL_VMEM = VMEM_LIMIT * 4 // 5
MESH = pl.DeviceIdType.MESH
ANY = pl.BlockSpec(memory_space=pl.ANY)


def _pick(n, prefs):
    for p in prefs:
        if p <= n and n % p == 0:
            return p
    return n


def _params(sem=None):
    return pltpu.CompilerParams(dimension_semantics=sem, vmem_limit_bytes=VMEM_LIMIT)


class _View:
    def __init__(self, arr, kind="plain", shape=None, dtype=None):
        self.arr = arr
        self.kind = kind
        self.shape = tuple(arr.shape) if arr is not None else tuple(shape)
        self.dtype = arr.dtype if arr is not None else dtype

    def limits(self):
        if self.kind == "plain":
            return 0, 0
        return self.shape[-2], (self.shape[-1] if self.kind == "cs" else 0)

    def spec(self, br, bc, fr, fc):
        if self.kind == "plain":
            return pl.BlockSpec((br, bc), lambda *g: (fr(*g), fc(*g)))
        rows, ncol = self.shape[-2:]
        assert rows % br == 0 and ncol % bc == 0, (self.shape, br, bc)
        if self.kind == "cs":
            per = ncol // bc
            return pl.BlockSpec((None, br, bc), lambda *g: (fc(*g) // per, fr(*g), fc(*g) % per))
        per = rows // br
        return pl.BlockSpec((None, br, bc), lambda *g: (fr(*g) // per, fr(*g) % per, fc(*g)))


def _bf(x):
    return x if x.dtype == BF16 else x.astype(BF16)


def _matmul(name, A, B, M, N, K, *, ta=False, tb=False, outs, epilogue, extras=(), vecs=(), n_sums=0,
            tm=None, tn=None, tk=None, plan=None):
    lim = {"m": [M], "n": [N], "k": [K]}
    for view, (rdim, cdim) in ([(A, "km" if ta else "mk"), (B, "nk" if tb else "kn")]
                               + [(e, "mn") for e in extras] + [(o, "mn") for o in outs]):
        r_lim, c_lim = view.limits()
        lim[rdim].append(r_lim)
        lim[cdim].append(c_lim)
    cap = {d: int(np.gcd.reduce(lim[d])) for d in "mnk"}
    tm = tm or _pick(cap["m"], MATMUL_TILES)
    tn = tn or _pick(cap["n"], MATMUL_TILES)
    tk = tk or _pick(cap["k"], MATMUL_TILES)

    def vmem_bytes(tm, tk):
        size = lambda v: jnp.dtype(v.dtype).itemsize
        tiles = tm * tk * size(A) + tk * tn * size(B) + tm * tn * sum(size(v) for v in list(extras) + list(outs))
        return 2 * tiles + (tm * tn * 4 if K > tk else 0)

    if cap["k"] % (2 * tk) == 0 and vmem_bytes(tm, 2 * tk) <= MATMUL_VMEM:
        tk *= 2
    elif K == tk and cap["m"] % (2 * tm) == 0 and vmem_bytes(2 * tm, tk) <= MATMUL_VMEM:
        tm *= 2
    nk = K // tk
    gi = lambda i, j, k: i
    gj = lambda i, j, k: j
    gk = lambda i, j, k: k
    a_spec = A.spec(tk, tm, gk, gi) if ta else A.spec(tm, tk, gi, gk)
    b_spec = B.spec(tn, tk, gj, gk) if tb else B.spec(tk, tn, gk, gj)
    ca = 0 if ta else 1
    cb = 1 if tb else 0
    ne, no = len(extras) + len(vecs), len(outs)
    assert n_sums == 0 or tn == N
    row_spec = pl.BlockSpec((1, tn), lambda i, j, k: (0, j))
    in_specs = [a_spec, b_spec] + [e.spec(tm, tn, gi, gj) for e in extras] + [row_spec] * len(vecs)
    operands = [A.arr, B.arr] + [e.arr for e in extras] + list(vecs)
    out_specs = [o.spec(tm, tn, gi, gj) for o in outs] + [row_spec] * n_sums
    out_shape = ([jax.ShapeDtypeStruct(o.shape, o.dtype) for o in outs]
                 + [jax.ShapeDtypeStruct((1, N), F32)] * n_sums)

    def body(*refs):
        a_ref, b_ref = refs[0], refs[1]
        ex = refs[2:2 + ne]
        o_refs = refs[2 + ne:2 + ne + no]
        s_refs = refs[2 + ne + no:2 + ne + no + n_sums]
        first_row_tile = pl.program_id(0) == 0

        def prod():
            return lax.dot_general(_bf(a_ref[...]), _bf(b_ref[...]), (((ca,), (cb,)), ((), ())),
                                   preferred_element_type=F32)

        def finish(acc):
            res = epilogue(acc, *[e[...] for e in ex])
            for o_ref, r in zip(o_refs, res[:no]):
                o_ref[...] = r.astype(o_ref.dtype)
            for s_ref, r in zip(s_refs, res[no:]):
                def assign(s_ref=s_ref, r=r):
                    s_ref[...] = r

                def accumulate(s_ref=s_ref, r=r):
                    s_ref[...] += r

                pl.when(first_row_tile)(assign)
                pl.when(jnp.logical_not(first_row_tile))(accumulate)

        if nk == 1:
            finish(prod())
        else:
            acc_ref = refs[-1]
            k = pl.program_id(2)

            @pl.when(k == 0)
            def _():
                acc_ref[...] = jnp.zeros_like(acc_ref)

            acc_ref[...] += prod()

            @pl.when(k == nk - 1)
            def _():
                finish(acc_ref[...])

    res, side = _hosted_call(body, name, (M // tm, N // tn, nk), in_specs, out_specs, out_shape,
                             [pltpu.VMEM((tm, tn), F32)] if nk > 1 else [], operands,
                             ("arbitrary", "arbitrary", "arbitrary"), plan)
    return res if plan is None else (res, side)


def _ep_store(acc):
    return (acc,)


def _ep_resid(acc, res):
    return (res + acc,)


def _ep_resid_norm(acc, res, g):
    xo = res + acc
    r = lax.rsqrt(jnp.mean(xo * xo, axis=-1, keepdims=True) + EPS)
    return (xo, (xo * r) * g)


def _ep_norm_bwd(acc, x, dres, g):
    r = lax.rsqrt(jnp.mean(x * x, axis=-1, keepdims=True) + EPS)
    xhat = x * r
    dxn = acc * g
    tot = dres + r * (dxn - xhat * jnp.mean(dxn * xhat, axis=-1, keepdims=True))
    return (tot, tot, jnp.sum(acc * xhat, axis=0, keepdims=True))


def _ep_relu2(acc):
    zp = jnp.maximum(acc, 0.0)
    return (zp * zp,)


def _ep_drelu2(acc, act):
    return (acc * (2.0 * jnp.sqrt(act.astype(F32))),)


def _fresh(M, N, dtype):
    return _View(None, shape=(M, N), dtype=dtype)


def _rms_fwd(name, x, g, S, D, plan=None):
    T = _pick(S, (512, 256, 128))

    def body(x_ref, g_ref, h_ref):
        x = x_ref[...]
        r = lax.rsqrt(jnp.mean(x * x, axis=-1, keepdims=True) + EPS)
        h_ref[...] = ((x * r) * g_ref[...]).astype(BF16)

    return _hosted_call(body, name, (S // T,),
                        [pl.BlockSpec((T, D), lambda i: (i, 0)), pl.BlockSpec((1, D), lambda i: (0, 0))],
                        [pl.BlockSpec((T, D), lambda i: (i, 0))], [jax.ShapeDtypeStruct((S, D), BF16)], [], (x, g),
                        ("arbitrary",), plan)


def _loss_head(x, tgt, S, D):
    T = _pick(S, (512, 256, 128))

    def body(x_ref, t_ref, loss_ref, d_ref, db_ref):
        @pl.when(pl.program_id(0) == 0)
        def _():
            loss_ref[...] = jnp.zeros_like(loss_ref)

        e = x_ref[...] - t_ref[...]
        loss_ref[...] += 0.5 * jnp.sum(jnp.mean(e * e, axis=-1, keepdims=True), axis=0, keepdims=True)
        d = e * (1.0 / D)
        d_ref[...] = d
        db_ref[...] = d.astype(BF16)

    row = pl.BlockSpec((T, D), lambda i: (i, 0))
    return pl.pallas_call(
        body, name="loss_head", grid=(S // T,), in_specs=[row, row],
        out_specs=[pl.BlockSpec((1, 1), lambda i: (0, 0)), row, row],
        out_shape=[jax.ShapeDtypeStruct((1, 1), F32), jax.ShapeDtypeStruct((S, D), F32),
                   jax.ShapeDtypeStruct((S, D), BF16)],
        compiler_params=_params(("arbitrary",)),
    )(x, tgt)


def _sigmoid(z):
    return 1.0 / (1.0 + jnp.exp(-z))


def _log_sigmoid(z):
    return jnp.minimum(z, 0.0) - jnp.log(1.0 + jnp.exp(-jnp.abs(z)))


_GELU_K = 0.7978845608028654
_GELU_C = 0.044715


def _gelu(x):
    t = jnp.tanh(_GELU_K * (x + _GELU_C * (x * x * x)))
    return 0.5 * x * (1.0 + t)


def _gelu_and_grad(x):
    x2 = x * x
    t = jnp.tanh(_GELU_K * (x + _GELU_C * (x2 * x)))
    g = 0.5 * x * (1.0 + t)
    dg = 0.5 * (1.0 + t) + 0.5 * x * (1.0 - t * t) * (_GELU_K * (1.0 + 3.0 * _GELU_C * x2))
    return g, dg


def _decay_terms(r, ls):
    la = LRU_C * r * ls
    a = jnp.exp(la)
    a2 = a * a
    mult = jnp.sqrt(-jnp.tanh(la) * (a2 + 1.0))
    return a, a2, mult


def _lru_fwd(u0, conv_w, conv_b, wr_bd, b_r, wi_bd, b_i, lam, S, D, plan=None):
    T = _pick(S, (256, 128))
    GT = wr_bd.shape[-1]
    nG = D // GT

    def body(gb_ref, xb_ref, cw_ref, cb_ref, wr_ref, br_ref, wi_ref, bi_ref, lam_ref,
             y_ref, xc_ref, r_ref, i_ref, hs_ref, ext, a_scr, hcar):
        @pl.when(pl.program_id(0) == 0)
        def _():
            ext[0:SUBLANES, :] = jnp.zeros((SUBLANES, D), F32)
            hcar[...] = jnp.zeros_like(hcar)

        xb = xb_ref[...]
        ext[SUBLANES:SUBLANES + T, :] = xb
        xc = cb_ref[...]
        for k in range(CONV_WIDTH):
            xc = xc + ext[pl.ds(SUBLANES - (CONV_WIDTH - 1) + k, T), :] * cw_ref[k:k + 1, :]
        ext[0:SUBLANES, :] = xb[T - SUBLANES:T, :]
        xc_ref[...] = xc
        xcb = xc.astype(BF16)
        for g in range(nG):
            sl = slice(g * GT, (g + 1) * GT)
            zr = jnp.dot(xcb[:, sl], wr_ref[g], preferred_element_type=F32) + br_ref[:, sl]
            zi = jnp.dot(xcb[:, sl], wi_ref[g], preferred_element_type=F32) + bi_ref[:, sl]
            r_ref[:, sl] = _sigmoid(zr)
            i_ref[:, sl] = _sigmoid(zi)
        r = r_ref[...]
        a, _, mult = _decay_terms(r, _log_sigmoid(lam_ref[...]))
        a_scr[...] = a
        hs_ref[...] = mult * (i_ref[...] * xc)

        def step(t, h):
            h = a_scr[pl.ds(t, 1), :] * h + hs_ref[pl.ds(t, 1), :]
            hs_ref[pl.ds(t, 1), :] = h
            return h

        hcar[...] = lax.fori_loop(0, T, step, hcar[...], unroll=8)
        y_ref[...] = (_gelu(gb_ref[...]) * hs_ref[...]).astype(BF16)

    row = pl.BlockSpec((T, D), lambda i: (i, 0))
    vec = pl.BlockSpec((1, D), lambda i: (0, 0))
    bd = pl.BlockSpec((nG, GT, GT), lambda i: (0, 0, 0))
    f32o = jax.ShapeDtypeStruct((S, D), F32)
    return _hosted_call(
        body, "lru_fwd", (S // T,),
        [row, pl.BlockSpec((T, D), lambda i: (i, 1)), pl.BlockSpec((CONV_WIDTH, D), lambda i: (0, 0)), vec,
         bd, vec, bd, vec, vec],
        [row, row, row, row, row], [jax.ShapeDtypeStruct((S, D), BF16), f32o, f32o, f32o, f32o],
        [pltpu.VMEM((T + SUBLANES, D), F32), pltpu.VMEM((T, D), F32), pltpu.VMEM((1, D), F32)],
        (u0, u0, conv_w, conv_b, wr_bd, b_r, wi_bd, b_i, lam), ("arbitrary",), plan)


def _lru_bwd(dy, u0, xc, r, ig, hs, conv_w, wr_bd, wi_bd, lam, S, D, plan=None):
    T = _pick(S, (128,))
    nT = S // T
    GT = wr_bd.shape[-1]
    nG = D // GT
    W = CONV_WIDTH

    def body(dy_ref, gb_ref, xb_ref, xbp_ref, xc_ref, r_ref, i_ref, hs_ref, hsp_ref, cw_ref, wr_ref, wi_ref, lam_ref,
             du_ref, dcw_ref, dcb_ref, dlam_ref, dbr_ref, dbi_ref, dwr_ref, dwi_ref,
             a_scr, dh_scr, exth, extx, extd, dxc_scr, dz_scr, carry):
        step = pl.program_id(0)
        first_tile = step == nT - 1

        @pl.when(step == 0)
        def _():
            for ref in (dcw_ref, dcb_ref, dlam_ref, dbr_ref, dbi_ref, dwr_ref, dwi_ref, carry):
                ref[...] = jnp.zeros_like(ref)
            extd[T:T + SUBLANES, :] = jnp.zeros((SUBLANES, D), F32)

        hs = hs_ref[...]
        dy = dy_ref[...]
        g, dgelu = _gelu_and_grad(gb_ref[...])
        du_ref[:, 0:D] = (dy * hs * dgelu).astype(BF16)
        r = r_ref[...]
        lam = lam_ref[...]
        ls = _log_sigmoid(lam)
        a, a2, mult = _decay_terms(r, ls)
        a_scr[...] = a
        dh_scr[...] = dy * g

        def rstep(j, c):
            t = T - 1 - j
            d = dh_scr[pl.ds(t, 1), :] + c
            dh_scr[pl.ds(t, 1), :] = d
            return a_scr[pl.ds(t, 1), :] * d

        carry[...] = lax.fori_loop(0, T, rstep, carry[...], unroll=8)
        dh = dh_scr[...]
        keep = jnp.where(first_tile, 0.0, 1.0)
        exth[0:SUBLANES, :] = hsp_ref[...] * keep
        exth[SUBLANES:SUBLANES + T, :] = hs
        hprev = exth[pl.ds(SUBLANES - 1, T), :]
        xc = xc_ref[...]
        ig = i_ref[...]
        da = dh * hprev
        dmult = dh * (ig * xc)
        dla = da * a - dmult * (a2 / mult)
        dlam_ref[...] += jnp.sum(dla * r, axis=0, keepdims=True) * (LRU_C * _sigmoid(-lam))
        dzr = (dla * (LRU_C * ls)) * (r * (1.0 - r))
        dzi = (dh * (mult * xc)) * (ig * (1.0 - ig))
        dbr_ref[...] += jnp.sum(dzr, axis=0, keepdims=True)
        dbi_ref[...] += jnp.sum(dzi, axis=0, keepdims=True)
        dxc_scr[...] = dh * (mult * ig)
        xcb = xc.astype(BF16)
        dz_scr[0] = dzr.astype(BF16)
        dz_scr[1] = dzi.astype(BF16)
        nt_dims = (((1,), (1,)), ((), ()))
        tn_dims = (((0,), (0,)), ((), ()))
        for gq in range(nG):
            sl = slice(gq * GT, (gq + 1) * GT)
            zr_g = dz_scr[0, :, sl]
            zi_g = dz_scr[1, :, sl]
            dxc_scr[:, sl] += (lax.dot_general(zr_g, wr_ref[gq], nt_dims, preferred_element_type=F32)
                               + lax.dot_general(zi_g, wi_ref[gq], nt_dims, preferred_element_type=F32))
            dwr_ref[gq] += lax.dot_general(xcb[:, sl], zr_g, tn_dims, preferred_element_type=F32)
            dwi_ref[gq] += lax.dot_general(xcb[:, sl], zi_g, tn_dims, preferred_element_type=F32)
        dxc = dxc_scr[...]
        dcb_ref[...] += jnp.sum(dxc, axis=0, keepdims=True)
        extx[0:SUBLANES, :] = xbp_ref[...] * keep
        extx[SUBLANES:SUBLANES + T, :] = xb_ref[...]
        extd[0:T, :] = dxc
        dxb = jnp.zeros((T, D), F32)
        for k in range(W):
            dxb = dxb + extd[pl.ds(W - 1 - k, T), :] * cw_ref[k:k + 1, :]
            dcw_ref[k:k + 1, :] += jnp.sum(dxc * extx[pl.ds(SUBLANES - (W - 1) + k, T), :], axis=0, keepdims=True)
        extd[T:T + SUBLANES, :] = dxc[0:SUBLANES, :]
        du_ref[:, D:2 * D] = dxb.astype(BF16)

    rev = lambda i: nT - 1 - i
    tpb = T // SUBLANES
    prev8 = lambda i: jnp.maximum(rev(i) * tpb - 1, 0)
    row = pl.BlockSpec((T, D), lambda i: (rev(i), 0))
    vec = pl.BlockSpec((1, D), lambda i: (0, 0))
    bd = pl.BlockSpec((nG, GT, GT), lambda i: (0, 0, 0))
    vec_o = jax.ShapeDtypeStruct((1, D), F32)
    bd_o = jax.ShapeDtypeStruct((nG, GT, GT), F32)
    return _hosted_call(
        body, "lru_bwd", (nT,),
        [row, row, pl.BlockSpec((T, D), lambda i: (rev(i), 1)), pl.BlockSpec((SUBLANES, D), lambda i: (prev8(i), 1)),
         row, row, row, row, pl.BlockSpec((SUBLANES, D), lambda i: (prev8(i), 0)),
         pl.BlockSpec((W, D), lambda i: (0, 0)), bd, bd, vec],
        [pl.BlockSpec((T, 2 * D), lambda i: (rev(i), 0)), pl.BlockSpec((W, D), lambda i: (0, 0)),
         vec, vec, vec, vec, bd, bd],
        [jax.ShapeDtypeStruct((S, 2 * D), BF16), jax.ShapeDtypeStruct((W, D), F32), vec_o, vec_o, vec_o, vec_o, bd_o, bd_o],
        [pltpu.VMEM((T, D), F32), pltpu.VMEM((T, D), F32), pltpu.VMEM((T + SUBLANES, D), F32),
         pltpu.VMEM((T + SUBLANES, D), F32), pltpu.VMEM((T + SUBLANES, D), F32),
         pltpu.VMEM((T, D), F32), pltpu.VMEM((2, T, D), BF16), pltpu.VMEM((1, D), F32)],
        (dy, u0, u0, u0, xc, r, ig, hs, hs, conv_w, wr_bd, wi_bd, lam), ("arbitrary",), plan)


AUG_ROWS = 16
HEAD_ROWS = 128
LSE_ROW = HEAD_DIM + 6
ONES_ROW_Q = HEAD_DIM + 3
ONES_COL_K = HEAD_DIM
ONES_ROW_V = HEAD_DIM
PREP_LANES = 512
HEAD_UNROLL = 8


def _split3(x):
    b1 = x.astype(BF16).astype(F32)
    r = x - b1
    b2 = r.astype(BF16).astype(F32)
    return b1, b2, r - b2


def _head_block(x, aug, T):
    row = lax.broadcasted_iota(jnp.int32, (AUG_ROWS, T), 0)
    blk = jnp.zeros((AUG_ROWS, T), F32)
    for i, e in enumerate(aug):
        blk = jnp.where(row == i, e, blk)
    return jnp.concatenate([x, blk, jnp.zeros((HEAD_ROWS - HEAD_DIM - AUG_ROWS, T), F32)], axis=0)


def _tri_matrix(lower):
    i = np.arange(LANES)
    m = (i[:, None] >= i[None, :]) if lower else (i[:, None] <= i[None, :])
    return jnp.asarray(m.astype(np.float32), BF16)


def _lane_cumsum(x, tri_ref, carry, reverse):
    n = x.shape[1] // LANES
    tri = tri_ref[...]
    out = [None] * n
    for j in (range(n - 1, -1, -1) if reverse else range(n)):
        cs = carry
        for part in _split3(x[:, j * LANES:(j + 1) * LANES]):
            cs = cs + jnp.dot(part.astype(BF16), tri, preferred_element_type=F32)
        out[j] = cs
        carry = cs[:, 0:1] if reverse else cs[:, LANES - 1:LANES]
    return jnp.concatenate(out, axis=1), carry


def _head_rows(h):
    return pl.ds(pl.multiple_of(h * HEAD_DIM, HEAD_DIM), HEAD_DIM)


def _fox_prep(ut, b_f, qg, kg, S, D, tq):
    H = D // HEAD_DIM
    T = min(tq, PREP_LANES)
    per = tq // T
    scale = HEAD_DIM ** -0.5

    def body(q_ref, k_ref, v_ref, f_ref, bf_ref, qg_ref, kg_ref, tri_ref,
             qat_ref, kat_ref, vat_ref, ka_ref, c_scr, ccar):
        @pl.when(pl.program_id(0) == 0)
        def _():
            ccar[...] = jnp.zeros_like(ccar)

        c, carry = _lane_cumsum(_log_sigmoid(f_ref[...] + bf_ref[...]), tri_ref, ccar[...], False)
        c_scr[...] = c
        ccar[...] = carry

        def head(h, _):
            rows = _head_rows(h)
            c1, c2, c3 = _split3(c_scr[pl.ds(h, 1), :])

            def normed(src, gain, mul):
                x = src[rows, :]
                rs = lax.rsqrt(jnp.mean(x * x, axis=0, keepdims=True) + EPS)
                return ((x * rs) * gain[rows, :]) * mul

            qat_ref[h] = _head_block(normed(q_ref, qg_ref, scale), [c1, c2, c3, 1.0, 1.0, 1.0], T).astype(BF16)
            kb = _head_block(normed(k_ref, kg_ref, 1.0), [1.0, 1.0, 1.0, -c1, -c2, -c3, 1.0, 1.0, 1.0], T)
            kat_ref[h] = kb.astype(BF16)
            ka_ref[h] = kb.T.astype(BF16)
            vat_ref[h] = _head_block(v_ref[rows, :], [1.0, 1.0, 1.0], T).astype(BF16)
            return 0

        lax.fori_loop(0, H, head, 0, unroll=min(HEAD_UNROLL, H))

    part = lambda j: pl.BlockSpec((D, T), lambda i: (j, i))
    colv = lambda n: pl.BlockSpec((n, 1), lambda i: (0, 0))
    tmaj = lambda r: pl.BlockSpec((H, None, r, T), lambda i: (0, i // per, 0, i % per))
    norm = pl.BlockSpec((H, T, HEAD_ROWS), lambda i: (0, i, 0))
    tshape = lambda r: jax.ShapeDtypeStruct((H, S // tq, r, tq), BF16)
    nshape = jax.ShapeDtypeStruct((H, S, HEAD_ROWS), BF16)
    return pl.pallas_call(
        body, name="fox_prep", grid=(S // T,),
        in_specs=[part(0), part(1), part(2), pl.BlockSpec((LANES, T), lambda i: (3 * D // LANES, i)),
                  colv(LANES), colv(D), colv(D), pl.BlockSpec((LANES, LANES), lambda i: (0, 0))],
        out_specs=[tmaj(HEAD_ROWS), tmaj(HEAD_ROWS), tmaj(HEAD_ROWS), norm],
        out_shape=[tshape(HEAD_ROWS), tshape(HEAD_ROWS), tshape(HEAD_ROWS), nshape],
        scratch_shapes=[pltpu.VMEM((LANES, T), F32), pltpu.VMEM((LANES, 1), F32)],
        compiler_params=_params(("arbitrary",)),
    )(ut, ut, ut, ut, b_f, qg, kg, _tri_matrix(False))


def _fox_bwd_prep(dot, ot, lse, qat, S, D, tq, plan=None):
    H = D // HEAD_DIM
    T = min(tq, PREP_LANES)
    per = tq // T

    def body(do_ref, o_ref, lse_ref, qat_ref, doat_ref, doa_ref, qat1_ref, qa1_ref):
        row = lax.broadcasted_iota(jnp.int32, (HEAD_ROWS, T), 0)

        def head(h, _):
            rows = _head_rows(h)
            do = do_ref[rows, :].astype(F32)
            delta = jnp.sum(do * o_ref[rows, :], axis=0, keepdims=True)
            db = _head_block(do, list(_split3(-delta)), T)
            doat_ref[h] = db.astype(BF16)
            doa_ref[h] = db.T.astype(BF16)
            qb = qat_ref[h].astype(F32)
            for i, e in enumerate(_split3(-lse_ref[h])):
                qb = jnp.where(row == LSE_ROW + i, e, qb)
            qat1_ref[h] = qb.astype(BF16)
            qa1_ref[h] = qb.T.astype(BF16)
            return 0

        lax.fori_loop(0, H, head, 0, unroll=min(HEAD_UNROLL, H))

    chan = pl.BlockSpec((D, T), lambda i: (0, i))
    tmaj = pl.BlockSpec((H, None, HEAD_ROWS, T), lambda i: (0, i // per, 0, i % per))
    norm = pl.BlockSpec((H, T, HEAD_ROWS), lambda i: (0, i, 0))
    tshape = jax.ShapeDtypeStruct((H, S // tq, HEAD_ROWS, tq), BF16)
    nshape = jax.ShapeDtypeStruct((H, S, HEAD_ROWS), BF16)
    return _hosted_call(body, "fox_bwd_prep", (S // T,), [chan, chan, pl.BlockSpec((H, 1, T), lambda i: (0, 0, i)), tmaj],
                        [tmaj, norm, tmaj, norm], [tshape, nshape, tshape, nshape], [], (dot, ot, lse, qat),
                        ("arbitrary",), plan)


def _causal(s, k_axis):
    t = min(s.shape)
    ki = lax.broadcasted_iota(jnp.int32, s.shape, k_axis) - (s.shape[k_axis] - t)
    qi = lax.broadcasted_iota(jnp.int32, s.shape, 1 - k_axis)
    return jnp.where(ki <= qi, s, NEG_INF)


def _attn_forward(ka, qat, vat, S, D, tq, plan=None):
    H = D // HEAD_DIM
    nq = S // tq
    G = 4

    def body(ka_ref, qat_ref, vat_ref, o_ref, o32_ref, lse_ref, m_scr, acc_scr):
        qi = pl.program_id(1)
        m_scr[...] = jnp.full_like(m_scr, NEG_INF)
        acc_scr[...] = jnp.zeros_like(acc_scr)

        def span(k0, n, diagonal):
            keys = pl.ds(pl.multiple_of(k0 * tq, tq), n * tq)
            s = [jnp.dot(ka_ref[g, keys, :], qat_ref[g], preferred_element_type=F32) for g in range(G)]
            if diagonal:
                s = [_causal(sg, 0) for sg in s]
            m_prev = [m_scr[g] for g in range(G)]
            m_new = [jnp.maximum(m_prev[g], jnp.max(s[g], axis=0, keepdims=True)) for g in range(G)]
            p = [jnp.exp(s[g] - m_new[g]).astype(BF16) for g in range(G)]
            for g in range(G):
                upd = jnp.dot(vat_ref[g, k0], p[g][0:tq], preferred_element_type=F32)
                for i in range(1, n):
                    upd = upd + jnp.dot(vat_ref[g, k0 + i], p[g][i * tq:(i + 1) * tq], preferred_element_type=F32)
                acc_scr[g] = jnp.exp(m_prev[g] - m_new[g]) * acc_scr[g] + upd
                m_scr[g] = m_new[g]

        def off_diagonal_pair(j, _):
            span(2 * j, 2, False)
            return 0

        lax.fori_loop(0, qi // 2, off_diagonal_pair, 0)
        pl.when(qi % 2 == 1)(lambda: span(qi - 1, 2, True))
        pl.when(qi % 2 == 0)(lambda: span(qi, 1, True))
        for g in range(G):
            l = acc_scr[g, ONES_ROW_V:ONES_ROW_V + 1, :]
            o = acc_scr[g, 0:HEAD_DIM, :] / l
            o_ref[g * HEAD_DIM:(g + 1) * HEAD_DIM, :] = o.astype(BF16)
            o32_ref[g * HEAD_DIM:(g + 1) * HEAD_DIM, :] = o
            lse_ref[g] = m_scr[g] + jnp.log(l)

    chan = pl.BlockSpec((G * HEAD_DIM, tq), lambda h, i: (h, i))
    stat = pl.BlockSpec((G, 1, tq), lambda h, i: (h, 0, i))
    return _hosted_call(
        body, "attn_forward", (H // G, nq),
        [pl.BlockSpec((G, S, HEAD_ROWS), lambda h, i: (h, 0, 0)),
         pl.BlockSpec((G, None, HEAD_ROWS, tq), lambda h, i: (h, i, 0, 0)),
         pl.BlockSpec((G, nq, HEAD_ROWS, tq), lambda h, i: (h, 0, 0, 0))],
        [chan, chan, stat],
        [jax.ShapeDtypeStruct((D, S), BF16), jax.ShapeDtypeStruct((D, S), F32), jax.ShapeDtypeStruct((H, 1, S), F32)],
        [pltpu.VMEM((G, 1, tq), F32), pltpu.VMEM((G, HEAD_ROWS, tq), F32)],
        (ka, qat, vat), ("arbitrary", "arbitrary"), plan)


def _attn_backward(qa, doa, qat, doat, ka, kat, vat, S, D, tq, plan=None):
    H = D // HEAD_DIM
    nq = S // tq
    G = 2

    def body(qa_ref, doa_ref, qat_ref, doat_ref, ka_ref, kat_ref, vat_ref, dq_ref, dk_ref, dv_ref, dk_scr, dv_scr):
        ki = pl.program_id(1)

        @pl.when(ki == 0)
        def _():
            dq_ref[...] = jnp.zeros_like(dq_ref)

        dk_scr[...] = jnp.zeros_like(dk_scr)
        dv_scr[...] = jnp.zeros_like(dv_scr)

        def span(q0, n, diagonal):
            rows = pl.ds(pl.multiple_of(q0 * tq, tq), n * tq)
            s = [jnp.dot(qa_ref[g, rows, :], kat_ref[g], preferred_element_type=F32) for g in range(G)]
            if diagonal:
                s = [_causal(sg, 1) for sg in s]
            p = [jnp.exp(sg) for sg in s]
            ds = [(p[g] * jnp.dot(doa_ref[g, rows, :], vat_ref[g], preferred_element_type=F32)).astype(BF16)
                  for g in range(G)]
            p = [pg.astype(BF16) for pg in p]
            for g in range(G):
                for i in range(n):
                    part = slice(i * tq, (i + 1) * tq)
                    dv_scr[g] += jnp.dot(doat_ref[g, q0 + i, 0:HEAD_DIM, :], p[g][part], preferred_element_type=F32)
                    dk_scr[g] += jnp.dot(qat_ref[g, q0 + i], ds[g][part], preferred_element_type=F32)
                dq_ref[g, rows, :] += jnp.dot(ds[g], ka_ref[g], preferred_element_type=F32)

        n_off = nq - 1 - ki
        odd = n_off % 2

        def off_diagonal_pair(j, _):
            span(ki + 1 + odd + 2 * j, 2, False)
            return 0

        pl.when(odd == 1)(lambda: span(ki, 2, True))
        pl.when(odd == 0)(lambda: span(ki, 1, True))
        lax.fori_loop(0, n_off // 2, off_diagonal_pair, 0)
        dk_ref[...] = dk_scr[...]
        for g in range(G):
            dv_ref[g * HEAD_DIM:(g + 1) * HEAD_DIM, :] = dv_scr[g].astype(BF16)

    whole = pl.BlockSpec((G, S, HEAD_ROWS), lambda h, i: (h, 0, 0))
    tiles = pl.BlockSpec((G, nq, HEAD_ROWS, tq), lambda h, i: (h, 0, 0, 0))
    one = pl.BlockSpec((G, None, HEAD_ROWS, tq), lambda h, i: (h, i, 0, 0))
    return _hosted_call(
        body, "attn_backward", (H // G, nq),
        [whole, whole, tiles, tiles, pl.BlockSpec((G, tq, HEAD_ROWS), lambda h, i: (h, i, 0)), one, one],
        [whole, pl.BlockSpec((G, HEAD_ROWS, tq), lambda h, i: (h, 0, i)),
         pl.BlockSpec((G * HEAD_DIM, tq), lambda h, i: (h, i))],
        [jax.ShapeDtypeStruct((H, S, HEAD_ROWS), F32), jax.ShapeDtypeStruct((H, HEAD_ROWS, S), F32),
         jax.ShapeDtypeStruct((D, S), BF16)],
        [pltpu.VMEM((G, HEAD_ROWS, tq), F32), pltpu.VMEM((G, HEAD_DIM, tq), F32)],
        (qa, doa, qat, doat, ka, kat, vat), ("arbitrary", "arbitrary"), plan)


def _fox_prep_bwd(ut, dq, dkt, dvt, b_f, qg, kg, S, D, tq):
    H = D // HEAD_DIM
    T = min(tq, PREP_LANES)
    nT = S // T
    NU = 3 * D + LANES
    scale = HEAD_DIM ** -0.5

    def body(q_ref, k_ref, f_ref, dq_ref, dk_ref, dv_ref, bf_ref, qg_ref, kg_ref, tri_ref,
             du_ref, dbf_ref, dqg_ref, dkg_ref, gq_acc, gk_acc, fcar, dc_scr):
        step = pl.program_id(0)

        @pl.when(step == 0)
        def _():
            for ref in (gq_acc, gk_acc, fcar, dbf_ref):
                ref[...] = jnp.zeros_like(ref)

        dc_scr[...] = jnp.zeros_like(dc_scr)

        def head(h, _):
            rows = _head_rows(h)
            dqb = dq_ref[h].T
            dkb = dk_ref[h]
            dc_scr[pl.ds(h, 1), :] = dqb[ONES_COL_K:ONES_COL_K + 1, :] - dkb[ONES_ROW_Q:ONES_ROW_Q + 1, :]
            for src, dsrc, gain, acc, mul, base in ((q_ref, dqb, qg_ref, gq_acc, scale, 0),
                                                    (k_ref, dkb, kg_ref, gk_acc, 1.0, D)):
                x = src[rows, :]
                rs = lax.rsqrt(jnp.mean(x * x, axis=0, keepdims=True) + EPS)
                xhat = x * rs
                dn = dsrc[0:HEAD_DIM, :] * mul
                acc[rows, :] += jnp.sum(dn * xhat, axis=1, keepdims=True)
                dxh = dn * gain[rows, :]
                dx = rs * (dxh - xhat * jnp.mean(dxh * xhat, axis=0, keepdims=True))
                du_ref[pl.ds(pl.multiple_of(base + h * HEAD_DIM, HEAD_DIM), HEAD_DIM), :] = dx.astype(BF16)
            return 0

        lax.fori_loop(0, H, head, 0, unroll=min(HEAD_UNROLL, H))
        du_ref[2 * D:3 * D, :] = dv_ref[...]
        dlf, carry = _lane_cumsum(dc_scr[...], tri_ref, fcar[...], True)
        fcar[...] = carry
        dfl = dlf * _sigmoid(-(f_ref[...] + bf_ref[...]))
        dbf_ref[...] += jnp.sum(dfl, axis=1, keepdims=True)
        du_ref[3 * D:NU, :] = dfl.astype(BF16)

        @pl.when(step == nT - 1)
        def _():
            for acc, ref in ((gq_acc, dqg_ref), (gk_acc, dkg_ref)):
                tot = jnp.zeros((HEAD_DIM, 1), F32)
                for h in range(H):
                    tot = tot + acc[h * HEAD_DIM:(h + 1) * HEAD_DIM, :]
                ref[...] = tot

    rev = lambda i: nT - 1 - i
    part = lambda j: pl.BlockSpec((D, T), lambda i: (j, rev(i)))
    colv = lambda n: pl.BlockSpec((n, 1), lambda i: (0, 0))
    return pl.pallas_call(
        body, name="fox_prep_bwd", grid=(nT,),
        in_specs=[part(0), part(1), pl.BlockSpec((LANES, T), lambda i: (3 * D // LANES, rev(i))),
                  pl.BlockSpec((H, T, HEAD_ROWS), lambda i: (0, rev(i), 0)),
                  pl.BlockSpec((H, HEAD_ROWS, T), lambda i: (0, 0, rev(i))), pl.BlockSpec((D, T), lambda i: (0, rev(i))),
                  colv(LANES), colv(D), colv(D), pl.BlockSpec((LANES, LANES), lambda i: (0, 0))],
        out_specs=[pl.BlockSpec((NU, T), lambda i: (0, rev(i))), colv(LANES), colv(HEAD_DIM), colv(HEAD_DIM)],
        out_shape=[jax.ShapeDtypeStruct((NU, S), BF16), jax.ShapeDtypeStruct((LANES, 1), F32),
                   jax.ShapeDtypeStruct((HEAD_DIM, 1), F32), jax.ShapeDtypeStruct((HEAD_DIM, 1), F32)],
        scratch_shapes=[pltpu.VMEM((D, 1), F32), pltpu.VMEM((D, 1), F32), pltpu.VMEM((LANES, 1), F32),
                        pltpu.VMEM((LANES, T), F32)],
        compiler_params=_params(("arbitrary",)),
    )(ut, ut, ut, dq, dkt, dvt, b_f, qg, kg, _tri_matrix(True))


def _block_diag_tiles(w):
    n = w.shape[0]
    per = min(MXU_DIM, n * LRU_BLOCK_DIM) // LRU_BLOCK_DIM
    eye = jnp.eye(per, dtype=w.dtype)
    w5 = w.reshape(n // per, per, LRU_BLOCK_DIM, 1, LRU_BLOCK_DIM) * eye[None, :, None, :, None]
    return w5.reshape(n // per, per * LRU_BLOCK_DIM, per * LRU_BLOCK_DIM).astype(BF16)


def _block_diag_extract(t, n):
    per = t.shape[-1] // LRU_BLOCK_DIM
    eye = jnp.eye(per, dtype=t.dtype)
    t5 = t.reshape(n // per, per, LRU_BLOCK_DIM, per, LRU_BLOCK_DIM) * eye[None, :, None, :, None]
    return t5.sum(axis=3).reshape(n, LRU_BLOCK_DIM, LRU_BLOCK_DIM)


def _local_step(x, tgt, small, wv, grad_view, comm=None):
    S, D = x.shape
    F = 4 * D
    H = D // HEAD_DIM
    nblk = D // LRU_BLOCK_DIM
    NU = 3 * D + LANES
    tq = max(LANES, min(512, S // 4))
    assert S % tq == 0
    vec = lambda a: a.reshape(1, -1).astype(F32)
    col = lambda a: a.reshape(-1, 1).astype(F32)
    mix_g, mlp_g = small["mix_norm"], small["mlp_norm"]
    conv_b = vec(small["lru_conv_b"])
    wr_bd, wi_bd = _block_diag_tiles(small["lru_w_r"][0]), _block_diag_tiles(small["lru_w_i"][0])
    b_r, b_i, lam = vec(small["lru_b_r"]), vec(small["lru_b_i"]), vec(small["lru_lambda"])
    b_f = jnp.pad(col(small["fox_b_f"]), ((0, LANES - H), (0, 0)))
    qg, kg = jnp.tile(col(small["fox_q_gain"]), (H, 1)), jnp.tile(col(small["fox_k_gain"]), (H, 1))
    X = lambda a: _View(a)
    grads = {}
    gout = functools.partial(grad_view, grads)

    def hosted(name, fn, *args):
        plan = comm.before(name, grads) if comm is not None else None
        res, side = fn(*args, plan=plan)
        if plan is not None:
            comm.after(name, side, wv)
        return res

    def hosted_mm(name, *args, **kw):
        plan = comm.before(name, grads) if comm is not None else None
        if plan is None:
            return _matmul(name, *args, **kw)
        res, side = _matmul(name, *args, plan=plan, **kw)
        comm.after(name, side, wv)
        return res

    two = lambda: [_fresh(S, D, F32), _fresh(S, D, BF16)]

    def mlp_up(l, hm):
        return hosted_mm(f"mlp{l}_up", X(hm), wv[f"w1_{l}"], S, F, D, outs=[_fresh(S, F, BF16)], epilogue=_ep_relu2)[0]

    def mlp_bwd(l, xin, hm, act, d, db):
        (dz,) = hosted_mm(f"mlp{l}_dact", X(db), wv[f"w2_{l}"], S, F, D, tb=True, outs=[_fresh(S, F, BF16)],
                          epilogue=_ep_drelu2, extras=[X(act)])
        (grads[f"w2_{l}"],) = _matmul(f"mlp{l}_dw2", X(act), X(db), F, D, S, ta=True, outs=[gout(f"w2_{l}")],
                                      epilogue=_ep_store)
        (grads[f"w1_{l}"],) = _matmul(f"mlp{l}_dw1", X(hm), X(dz), D, F, S, ta=True, outs=[gout(f"w1_{l}")],
                                      epilogue=_ep_store)
        return _matmul(f"mlp{l}_dhm", X(dz), wv[f"w1_{l}"], S, D, F, tb=True, outs=two(), n_sums=1,
                       epilogue=_ep_norm_bwd, extras=[X(xin), X(d)], vecs=[mlp_g[l:l + 1]])

    (h0,) = hosted("mix0_norm", _rms_fwd, "mix0_norm", x, mix_g[0:1], S, D)
    (u0,) = hosted_mm("lru_in", X(h0), wv["lru_in"], S, 2 * D, D, outs=[_fresh(S, 2 * D, F32)], epilogue=_ep_store)
    conv_w = small["conv_w"]
    y, xc, r, ig, hs = hosted("lru_fwd", _lru_fwd, u0, conv_w, conv_b, wr_bd, b_r, wi_bd, b_i, lam, S, D)
    x1, hm0 = _matmul("lru_out", X(y), wv["lru_out"], S, D, D, outs=two(), epilogue=_ep_resid_norm, extras=[X(x)],
                      vecs=[mlp_g[0:1]])
    act0 = mlp_up(0, hm0)
    x2, h1 = hosted_mm("mlp0_down", X(act0), wv["w2_0"], S, D, F, outs=two(), epilogue=_ep_resid_norm, extras=[X(x1)],
                       vecs=[mix_g[1:2]])
    (u1,) = _matmul("fox_in", wv["fox_in"], X(h1), NU, S, D, tb=True, outs=[_fresh(NU, S, F32)], epilogue=_ep_store)
    qat, kat, vat, ka = _fox_prep(u1, b_f, qg, kg, S, D, tq)
    o, o32, lse = hosted("attn_forward", _attn_forward, ka, qat, vat, S, D, tq)
    x3, hm1 = _matmul("fox_out", X(o), wv["fox_out"], S, D, D, ta=True, outs=two(), epilogue=_ep_resid_norm,
                      extras=[X(x2)], vecs=[mlp_g[1:2]])
    act1 = mlp_up(1, hm1)
    (x4,) = _matmul("mlp1_down", X(act1), wv["w2_1"], S, D, F, outs=[_fresh(S, D, F32)], epilogue=_ep_resid,
                    extras=[X(x3)])
    loss, d4, d4b = _loss_head(x4, tgt, S, D)

    d3, d3b, dg_mlp1 = mlp_bwd(1, x3, hm1, act1, d4, d4b)
    (do,) = _matmul("fox_dout", wv["fox_out"], X(d3b), D, S, D, tb=True, outs=[_fresh(D, S, BF16)], epilogue=_ep_store)
    (grads["fox_out"],) = _matmul("fox_dwout", X(o), X(d3b), D, D, S, outs=[gout("fox_out")], epilogue=_ep_store)
    doat, doa, qat1, qa1 = hosted("fox_bwd_prep", _fox_bwd_prep, do, o32, lse, qat, S, D, tq)
    dqn, dkn, dv = hosted("attn_backward", _attn_backward, qa1, doa, qat1, doat, ka, kat, vat, S, D, tq)
    du1, dbf, dqg, dkg = _fox_prep_bwd(u1, dqn, dkn, dv, b_f, qg, kg, S, D, tq)
    (grads["fox_in"],) = _matmul("fox_dwin", X(du1), X(h1), NU, D, S, outs=[gout("fox_in")], epilogue=_ep_store)
    d2, d2b, dg_mix1 = hosted_mm("fox_dh", X(du1), wv["fox_in"], S, D, NU, ta=True, outs=two(), n_sums=1,
                               epilogue=_ep_norm_bwd, extras=[X(x2), X(d3)], vecs=[mix_g[1:2]])
    d1, d1b, dg_mlp0 = mlp_bwd(0, x1, hm0, act0, d2, d2b)
    (grads["lru_out"],) = _matmul("lru_dwout", X(y), X(d1b), D, D, S, ta=True, outs=[gout("lru_out")],
                                  epilogue=_ep_store)
    (dy,) = hosted_mm("lru_dout", X(d1b), wv["lru_out"], S, D, D, tb=True, outs=[_fresh(S, D, F32)],
                      epilogue=_ep_store)
    du0, dcw, dcb, dlam, dbr, dbi, dwr, dwi = hosted("lru_bwd", _lru_bwd, dy, u0, xc, r, ig, hs, conv_w, wr_bd, wi_bd,
                                                     lam, S, D)
    (grads["lru_in"],) = _matmul("lru_dwin", X(h0), X(du0), D, 2 * D, S, ta=True, outs=[gout("lru_in")],
                                 epilogue=_ep_store)
    gx, dg_mix0 = hosted_mm("lru_dh", X(du0), wv["lru_in"], S, D, 2 * D, tb=True, outs=[_fresh(S, D, F32)], n_sums=1,
                            epilogue=lambda *a: _ep_norm_bwd(*a)[::2], extras=[X(x), X(d1)], vecs=[mix_g[0:1]])

    grads.update(
        mix_norm=jnp.concatenate([dg_mix0, dg_mix1], axis=0), mlp_norm=jnp.concatenate([dg_mlp0, dg_mlp1], axis=0),
        conv_w=dcw, lru_conv_b=dcb, lru_w_r=_block_diag_extract(dwr, nblk)[None], lru_b_r=dbr.reshape(1, nblk, -1),
        lru_w_i=_block_diag_extract(dwi, nblk)[None], lru_b_i=dbi.reshape(1, nblk, -1), lru_lambda=dlam,
        fox_b_f=dbf[:H].reshape(1, H), fox_q_gain=dqg.reshape(1, -1), fox_k_gain=dkg.reshape(1, -1))
    return loss, gx, grads


def _place():
    x, y, c = lax.axis_index("x"), lax.axis_index("y"), lax.axis_index("c")
    chips = [(1 - x, y), (x, 1 - y), (1 - x, 1 - y)]
    return x, y, c, 2 * x + y, chips


BOUNCE_BYTES = 1 << 20


def _bounce_shape(rows, cols, dtype):
    chunk = rows
    while chunk % 2 == 0 and chunk > 16 and chunk * cols * jnp.dtype(dtype).itemsize > BOUNCE_BYTES:
        chunk //= 2
    return pltpu.VMEM((2, chunk, cols), dtype)


def _bounce_copy(src, dst, buf, sem):
    chunk = buf.shape[1]
    n = src.shape[0] // chunk
    cin = lambda i: pltpu.make_async_copy(src.at[pl.ds(i * chunk, chunk)], buf.at[i % 2], sem.at[i % 2])
    cout = lambda i: pltpu.make_async_copy(buf.at[i % 2], dst.at[pl.ds(i * chunk, chunk)], sem.at[2 + i % 2])
    cin(0).start()
    for i in range(n):
        cin(i).wait()
        if i + 1 < n:
            if i >= 1:
                cout(i - 1).wait()
            cin(i + 1).start()
        cout(i).start()
    if n >= 2:
        cout(n - 2).wait()
    cout(n - 1).wait()


def _hbm_call(body, name, arrays, out_shape, n_dma_sems, bounce=()):
    scratch = [pltpu.SemaphoreType.DMA((k,)) for k in n_dma_sems]
    for rows, cols, dtype in bounce:
        scratch += [_bounce_shape(rows, cols, dtype), pltpu.SemaphoreType.DMA((4,))]
    return pl.pallas_call(
        body, name=name, in_specs=[ANY] * len(arrays), out_specs=[ANY] * len(out_shape), out_shape=out_shape,
        scratch_shapes=scratch,
        compiler_params=pltpu.CompilerParams(has_side_effects=True, vmem_limit_bytes=VMEM_LIMIT),
    )(*arrays)


class _Gather:
    def __init__(self, shards):
        n = self.n = len(shards)
        self.operands = list(shards)
        self.out_shape = [jax.ShapeDtypeStruct((N_CHIPS,) + tuple(a.shape), a.dtype) for a in shards]
        self.scratch = [pltpu.SemaphoreType.DMA((3 * n,)) for _ in range(4)]
        for a in shards:
            self.scratch += [_bounce_shape(a.shape[0], a.shape[1], a.dtype), pltpu.SemaphoreType.DMA((4,))]

    def _copies(self, ins, outs, scr):
        send, recv, fsend, frecv = scr[:4]
        x, y, c, s, chips = _place()

        def rows(a, chip_idx, which):
            hr = ins[a].shape[0] // 2
            return outs[a].at[chip_idx, pl.ds(which * hr, hr)]

        def landed(a, j, core):
            return rows(a, 2 * chips[j][0] + chips[j][1], core)

        def ici(a, j, mine):
            hr = ins[a].shape[0] // 2
            src, dst = (ins[a].at[pl.ds(c * hr, hr)], rows(a, s, c)) if mine else (landed(a, j, c),) * 2
            return pltpu.make_async_remote_copy(src_ref=src, dst_ref=dst, send_sem=send.at[3 * a + j],
                                                recv_sem=recv.at[3 * a + j], device_id=(*chips[j], c),
                                                device_id_type=MESH)

        def d2d(a, j, mine):
            ref = landed(a, j, c if mine else 1 - c)
            return pltpu.make_async_remote_copy(src_ref=ref, dst_ref=ref, send_sem=fsend.at[3 * a + j],
                                                recv_sem=frecv.at[3 * a + j], device_id=(x, y, 1 - c),
                                                device_id_type=MESH)

        return ici, d2d, s

    def start(self, ins, outs, scr):
        ici, _, _ = self._copies(ins, outs, scr)
        for a in range(self.n):
            for j in range(3):
                ici(a, j, True).start()

    def middle(self, ins, outs, scr):
        ici, d2d, s = self._copies(ins, outs, scr)
        for a in range(self.n):
            _bounce_copy(ins[a], outs[a].at[s], scr[4 + 2 * a], scr[5 + 2 * a])
        for a in range(self.n):
            for j in range(3):
                ici(a, j, False).wait_recv()
                d2d(a, j, True).start()

    def finish(self, ins, outs, scr):
        ici, d2d, _ = self._copies(ins, outs, scr)
        for a in range(self.n):
            for j in range(3):
                d2d(a, j, False).wait_recv()
        for a in range(self.n):
            for j in range(3):
                ici(a, j, True).wait_send()
                d2d(a, j, True).wait_send()


def _run_plan(name, plan):
    k_in, k_out = len(plan.operands), len(plan.out_shape)

    def body(*refs):
        parts = (refs[:k_in], refs[k_in:k_in + k_out], refs[k_in + k_out:])
        plan.start(*parts)
        plan.middle(*parts)
        plan.finish(*parts)

    return pl.pallas_call(
        body, name=name, in_specs=[ANY] * k_in, out_specs=[ANY] * k_out, out_shape=plan.out_shape,
        scratch_shapes=plan.scratch,
        compiler_params=pltpu.CompilerParams(has_side_effects=True, vmem_limit_bytes=VMEM_LIMIT),
    )(*plan.operands)


def _hosted_call(body, name, grid, in_specs, out_specs, out_shape, scratch_shapes, operands, sem, plan=None):
    if plan is None:
        res = pl.pallas_call(body, name=name, grid=grid, in_specs=in_specs, out_specs=out_specs, out_shape=out_shape,
                             scratch_shapes=scratch_shapes, compiler_params=_params(sem))(*operands)
        return res, None
    n_in, n_out, n_scr = len(in_specs), len(out_specs), len(scratch_shapes)
    k_in, k_out = len(plan.operands), len(plan.out_shape)
    total = int(np.prod(grid))
    late = max(0, total - 1 - max(1, total // 8))

    def hosted(*refs):
        ins, refs = refs[:n_in], refs[n_in:]
        p_ins, refs = refs[:k_in], refs[k_in:]
        outs, refs = refs[:n_out], refs[n_out:]
        p_outs, refs = refs[:k_out], refs[k_out:]
        scr, p_scr = refs[:n_scr], refs[n_scr:]
        step = pl.program_id(0)
        for d in range(1, len(grid)):
            step = step * grid[d] + pl.program_id(d)
        pl.when(step == 0)(lambda: plan.start(p_ins, p_outs, p_scr))
        body(*ins, *outs, *scr)
        pl.when(step == late)(lambda: plan.middle(p_ins, p_outs, p_scr))
        pl.when(step == total - 1)(lambda: plan.finish(p_ins, p_outs, p_scr))

    res = pl.pallas_call(
        hosted, name=name, grid=grid, in_specs=list(in_specs) + [ANY] * k_in, out_specs=list(out_specs) + [ANY] * k_out,
        out_shape=list(out_shape) + plan.out_shape, scratch_shapes=list(scratch_shapes) + plan.scratch,
        compiler_params=pltpu.CompilerParams(dimension_semantics=sem, vmem_limit_bytes=VMEM_LIMIT,
                                             has_side_effects=True),
    )(*operands, *plan.operands)
    return res[:n_out], res[n_out:]


def _all_gather(name, shards):
    return _run_plan(name, _Gather(shards))


class _Swap:
    def __init__(self, arrs):
        self.n = len(arrs)
        self.operands = list(arrs)
        self.out_shape = [jax.ShapeDtypeStruct((a.shape[0], a.shape[1] // 2, a.shape[2]), a.dtype) for a in arrs]
        self.scratch = [pltpu.SemaphoreType.DMA((self.n,)) for _ in range(2)]

    def _copy(self, ins, outs, scr, a):
        x, y, c, _, _ = _place()
        hr = ins[a].shape[1] // 2
        return pltpu.make_async_remote_copy(
            src_ref=ins[a].at[:, pl.ds((1 - c) * hr, hr)], dst_ref=outs[a], send_sem=scr[0].at[a],
            recv_sem=scr[1].at[a], device_id=(x, y, 1 - c), device_id_type=MESH)

    def start(self, ins, outs, scr):
        for a in range(self.n):
            self._copy(ins, outs, scr, a).start()

    def middle(self, ins, outs, scr):
        pass

    def finish(self, ins, outs, scr):
        for a in range(self.n):
            self._copy(ins, outs, scr, a).wait()


class _Scatter:
    def __init__(self, parts):
        n = self.n = len(parts)
        self.operands = list(parts)
        self.out_shape = [jax.ShapeDtypeStruct(a.shape, a.dtype) for a in parts]
        self.scratch = [pltpu.SemaphoreType.DMA((3 * n,)) for _ in range(2)]
        for a in parts:
            self.scratch += [_bounce_shape(a.shape[1], a.shape[2], a.dtype), pltpu.SemaphoreType.DMA((4,))]

    def _copy(self, ins, outs, scr, a, j, mine):
        x, y, c, s, chips = _place()
        t = 2 * chips[j][0] + chips[j][1]
        return pltpu.make_async_remote_copy(
            src_ref=ins[a].at[t], dst_ref=outs[a].at[s if mine else t], send_sem=scr[0].at[3 * a + j],
            recv_sem=scr[1].at[3 * a + j], device_id=(*chips[j], c), device_id_type=MESH)

    def start(self, ins, outs, scr):
        for a in range(self.n):
            for j in range(3):
                self._copy(ins, outs, scr, a, j, True).start()

    def middle(self, ins, outs, scr):
        s = _place()[3]
        for a in range(self.n):
            _bounce_copy(ins[a].at[s], outs[a].at[s], scr[2 + 2 * a], scr[3 + 2 * a])

    def finish(self, ins, outs, scr):
        for a in range(self.n):
            for j in range(3):
                self._copy(ins, outs, scr, a, j, False).wait_recv()
        for a in range(self.n):
            for j in range(3):
                self._copy(ins, outs, scr, a, j, True).wait_send()


def _pair_gather(name, halves):
    n = len(halves)

    def body(*refs):
        ins, outs = refs[:n], refs[n:2 * n]
        send, recv = refs[2 * n:2 * n + 2]
        stage = refs[2 * n + 2:]
        x, y, c, _, _ = _place()
        cps = []
        for a in range(n):
            hr = ins[a].shape[0]
            cp = pltpu.make_async_remote_copy(
                src_ref=ins[a], dst_ref=outs[a].at[pl.ds(c * hr, hr)], send_sem=send.at[a], recv_sem=recv.at[a],
                device_id=(x, y, 1 - c), device_id_type=MESH)
            cp.start()
            cps.append((cp, hr))
        for a, (cp, hr) in enumerate(cps):
            _bounce_copy(ins[a], outs[a].at[pl.ds(c * hr, hr)], stage[2 * a], stage[2 * a + 1])
        for a, (cp, hr) in enumerate(cps):
            cp.wait_send()
            theirs = outs[a].at[pl.ds((1 - c) * hr, hr)]
            pltpu.make_async_remote_copy(src_ref=theirs, dst_ref=theirs, send_sem=send.at[a], recv_sem=recv.at[a],
                                         device_id=(x, y, 1 - c), device_id_type=MESH).wait_recv()

    out_shape = [jax.ShapeDtypeStruct((2 * a.shape[0], a.shape[1]), a.dtype) for a in halves]
    return _hbm_call(body, name, halves, out_shape, (n, n),
                     bounce=[(a.shape[0], a.shape[1], a.dtype) for a in halves])


def _row_tile(rows, cols, itemsize, n_bufs):
    budget = VMEM_LIMIT // 2
    for t in range(min(rows, 1024) // 16 * 16, 0, -16):
        if rows % t == 0 and 2 * n_bufs * t * cols * itemsize <= budget:
            return t
    return rows


def _pair_add(name, g, gsib, core, out_dtype):
    _, r, cols = g.shape
    hr = r // 2
    t = _row_tile(hr, cols, 4, 3)
    per = hr // t

    def body(core_ref, a_ref, b_ref, o_ref):
        o_ref[...] = (a_ref[...].astype(F32) + b_ref[...].astype(F32)).astype(o_ref.dtype)

    grid_spec = pltpu.PrefetchScalarGridSpec(
        num_scalar_prefetch=1, grid=(N_CHIPS, per),
        in_specs=[pl.BlockSpec((None, t, cols), lambda s, i, core: (s, core[0] * per + i, 0)),
                  pl.BlockSpec((None, t, cols), lambda s, i, core: (s, i, 0))],
        out_specs=pl.BlockSpec((None, t, cols), lambda s, i, core: (s, i, 0)))
    return pl.pallas_call(body, name=name, grid_spec=grid_spec,
                          out_shape=jax.ShapeDtypeStruct((N_CHIPS, hr, cols), out_dtype),
                          compiler_params=_params(("arbitrary", "arbitrary")))(core, g, gsib)


def _chip_sum(name, parts):
    _, hr, cols = parts.shape
    t = _row_tile(hr, cols, 4, 5)

    def body(p_ref, o_ref):
        o_ref[...] = ((p_ref[0].astype(F32) + p_ref[1].astype(F32)) + p_ref[2].astype(F32)) + p_ref[3].astype(F32)

    return pl.pallas_call(
        body, name=name, grid=(hr // t,), in_specs=[pl.BlockSpec((N_CHIPS, t, cols), lambda i: (0, i, 0))],
        out_specs=pl.BlockSpec((t, cols), lambda i: (i, 0)), out_shape=jax.ShapeDtypeStruct((hr, cols), F32),
        compiler_params=_params(("arbitrary",)))(parts)


def _pair_partials(tag, arrs, sib, wire_dtypes, core):
    return _Scatter([_pair_add(f"{tag}_pair_add{i}", g, gs, core, dt)
                     for i, (g, gs, dt) in enumerate(zip(arrs, sib, wire_dtypes))])


def _finish_reduce(tag, scattered):
    halves = [_chip_sum(f"{tag}_chip_sum{i}", p) for i, p in enumerate(scattered)]
    return _pair_gather(f"{tag}_pair_gather", halves)


def _adamw(name, w, g_parts, m, v):
    thin = w.ndim == 3
    rows, cols = w.shape[0], w.shape[-1]
    n_parts = len(g_parts)
    part_rows = rows // n_parts
    t = max(d for d in range(1, 257) if part_rows % d == 0) if thin else _row_tile(part_rows, cols, 4, 7 + n_parts)
    per = part_rows // t
    c1 = 1.0 - ADAM_B1 ** ADAM_STEP
    c2 = 1.0 - ADAM_B2 ** ADAM_STEP

    def body(w_ref, m_ref, v_ref, *refs):
        g_refs, (go_ref, d_ref, nm_ref, nv_ref) = refs[:n_parts], refs[n_parts:]
        g = g_refs[0][...]
        for k in range(1, n_parts):
            g = jnp.where(pl.program_id(0) >= k * per, g_refs[k][...], g)
        go_ref[...] = g
        m = ADAM_B1 * m_ref[...] + (1.0 - ADAM_B1) * g
        v = ADAM_B2 * v_ref[...] + (1.0 - ADAM_B2) * (g * g)
        nm_ref[...] = m
        nv_ref[...] = v
        d_ref[...] = -ADAM_LR * ((m / c1) / (jnp.sqrt(v / c2) + ADAM_EPS) + ADAM_WD * w_ref[...])

    block = (t, 1, cols) if thin else (t, cols)
    at = lambda r: (r, 0, 0) if thin else (r, 0)
    spec = pl.BlockSpec(block, lambda i: at(i))
    g_specs = [pl.BlockSpec(block, lambda i, k=k: at(jnp.clip(i - k * per, 0, per - 1))) for k in range(n_parts)]
    shp = jax.ShapeDtypeStruct(w.shape, F32)
    return pl.pallas_call(body, name=name, grid=(rows // t,), in_specs=[spec] * 3 + g_specs, out_specs=[spec] * 4,
                          out_shape=[shp] * 4, compiler_params=_params(("arbitrary",)))(w, m, v, *g_parts)


_WEIGHTS = ["mix_norm", "mlp_norm", "mlp_w1", "mlp_w2", "lru_w_in", "lru_conv_w", "lru_conv_b", "lru_w_r", "lru_b_r",
            "lru_w_i", "lru_b_i", "lru_lambda", "lru_w_out", "fox_w_in", "fox_b_f", "fox_q_gain", "fox_k_gain",
            "fox_w_out"]
_REPLICATED = ["mix_norm", "mlp_norm", "lru_conv_b", "lru_w_r", "lru_b_r", "lru_w_i", "lru_b_i", "lru_lambda",
               "fox_b_f", "fox_q_gain", "fox_k_gain"]
_PACK_TILE = 2 * SUBLANES * LANES


def _as2d(a):
    return a.reshape(-1, a.shape[-1])


def kernel(x, mix_norm, mlp_norm, mlp_w1, mlp_w2, lru_w_in, lru_conv_w, lru_conv_b, lru_w_r, lru_b_r, lru_w_i, lru_b_i, lru_lambda, lru_w_out, fox_w_in, fox_b_f, fox_q_gain, fox_k_gain, fox_w_out, loss_target, m_mix_norm, m_mlp_norm, m_mlp_w1, m_mlp_w2, m_lru_w_in, m_lru_conv_w, m_lru_conv_b, m_lru_w_r, m_lru_b_r, m_lru_w_i, m_lru_b_i, m_lru_lambda, m_lru_w_out, m_fox_w_in, m_fox_b_f, m_fox_q_gain, m_fox_k_gain, m_fox_w_out, v_mix_norm, v_mlp_norm, v_mlp_w1, v_mlp_w2, v_lru_w_in, v_lru_conv_w, v_lru_conv_b, v_lru_w_r, v_lru_b_r, v_lru_w_i, v_lru_b_i, v_lru_lambda, v_lru_w_out, v_fox_w_in, v_fox_b_f, v_fox_q_gain, v_fox_k_gain, v_fox_w_out):
    args = dict(locals())
    W = {n: args[n] for n in _WEIGHTS}
    Mo = {n: args["m_" + n] for n in _WEIGHTS}
    Vo = {n: args["v_" + n] for n in _WEIGHTS}
    S, D = x.shape[1], x.shape[2]
    F = 4 * D
    H = D // HEAD_DIM
    NU = 3 * D + LANES
    FQ, DQ = F // N_CHIPS, D // N_CHIPS
    nfox = fox_w_in.shape[-1]
    chip = 2 * lax.axis_index("x") + lax.axis_index("y")
    core = lax.axis_index("c").astype(jnp.int32).reshape(1)

    cw_flat = jnp.pad(lru_conv_w.reshape(-1), (0, _PACK_TILE - CONV_WIDTH * DQ)).reshape(2 * SUBLANES, LANES)
    w1s, w2s = mlp_w1.astype(BF16), mlp_w2.astype(BF16)
    wv = {}
    small = {n: W[n] for n in _REPLICATED}
    scattered = {}
    members = {"g1": ["w2_1", "w1_1", "fox_out"], "g2": ["fox_in"], "g3": ["w2_0", "w1_0"], "g4": ["lru_out", "lru_in"]}
    swap_at = {"fox_bwd_prep": "g1", "fox_dh": "g2", "lru_dout": "g3"}
    scatter_at = {"attn_backward": "g1", "mlp0_dact": "g2", "lru_bwd": "g3", "lru_dh": "g4"}
    swapped = {}

    fox_rows = -(-nfox // (4 * SUBLANES)) * (4 * SUBLANES)
    fox_t = jnp.pad(jnp.transpose(fox_w_in[0]).astype(BF16), ((0, fox_rows - nfox), (0, 0)))

    def shard_major(name, g):
        if name == "fox_in":
            return jnp.pad(g[:nfox * N_CHIPS].reshape(N_CHIPS, nfox, D), ((0, 0), (0, fox_rows - nfox), (0, 0)))
        return g

    class Comm:
        @staticmethod
        def before(name, grads):
            if name == "mix0_norm":
                return _Gather([lru_w_in[0].astype(BF16)])
            if name == "lru_in":
                return _Gather([lru_w_out[0].astype(BF16), cw_flat])
            if name == "lru_fwd":
                return _Gather([w1s[0]])
            if name == "mlp0_up":
                return _Gather([w2s[0]])
            if name == "mlp0_down":
                return _Gather([fox_t])
            if name == "attn_forward":
                return _Gather([fox_w_out[0].astype(BF16), w1s[1], w2s[1]])
            if name in swap_at:
                group = swap_at[name]
                swapped[group] = [[shard_major(n, grads[n]) for n in members[group]], None]
                return _Swap(swapped[group][0])
            if name in scatter_at:
                group = scatter_at[name]
                if group not in swapped:
                    arrs = [shard_major(n, grads[n]) for n in members[group]]
                    swapped[group] = [arrs, _run_plan(f"{group}_pair_swap", _Swap(arrs))]
                arrs, sib = swapped[group]
                return _pair_partials(group, arrs, sib, [BF16] * len(arrs), core)
            return None

        @staticmethod
        def after(name, res, wv):
            if name == "mix0_norm":
                wv.update(lru_in=_View(res[0], "cs"))
            elif name == "lru_in":
                wv.update(lru_out=_View(res[0], "rs"))
                taps = res[1].reshape(N_CHIPS, -1)[:, :CONV_WIDTH * DQ].reshape(N_CHIPS, CONV_WIDTH, DQ)
                small["conv_w"] = jnp.transpose(taps, (1, 0, 2)).reshape(CONV_WIDTH, D)
            elif name == "lru_fwd":
                wv.update(w1_0=_View(res[0], "cs"))
            elif name == "mlp0_up":
                wv.update(w2_0=_View(res[0], "rs"))
            elif name == "mlp0_down":
                fox_full = jnp.concatenate([res[0][s, :nfox] for s in range(N_CHIPS)], axis=0)
                wv.update(fox_in=_View(jnp.pad(fox_full, ((0, NU - fox_full.shape[0]), (0, 0)))))
            elif name == "attn_forward":
                wv.update(fox_out=_View(res[0], "rs"), w1_1=_View(res[1], "cs"), w2_1=_View(res[2], "rs"))
            elif name in swap_at:
                swapped[swap_at[name]][1] = res
            else:
                scattered.update(zip(members[scatter_at[name]], res))

    def grad_view(grads, name):
        if name in ("w1_0", "w1_1"):
            return _View(None, "cs", shape=(N_CHIPS, D, FQ), dtype=BF16)
        if name in ("w2_0", "w2_1"):
            return _View(None, "rs", shape=(N_CHIPS, FQ, D), dtype=BF16)
        if name == "lru_in":
            return _View(None, "cs", shape=(N_CHIPS, D, 2 * D // N_CHIPS), dtype=BF16)
        if name in ("lru_out", "fox_out"):
            return _View(None, "rs", shape=(N_CHIPS, DQ, D), dtype=BF16)
        return _View(None, shape=(NU, D), dtype=BF16)

    loss, gx, grads = _local_step(x[0], loss_target[0], small, wv, grad_view, Comm)

    pack_names = _REPLICATED + ["conv_w"]
    flat = jnp.concatenate([grads[n].reshape(-1).astype(F32) for n in pack_names] + [loss.reshape(-1)])
    per_chip = -(-flat.shape[0] // (N_CHIPS * _PACK_TILE)) * _PACK_TILE
    pack = jnp.pad(flat, (0, N_CHIPS * per_chip - flat.shape[0])).reshape(N_CHIPS, per_chip // LANES, LANES)
    pack_sib = _run_plan("pack_pair_swap", _Swap([pack]))
    (scattered["pack"],) = _run_plan("pack_chip_scatter", _pair_partials("pack", [pack], pack_sib, [F32], core))
    order = ["w1_0", "w1_1", "w2_0", "w2_1", "lru_in", "lru_out", "fox_in", "fox_out", "pack"]
    red = dict(zip(order, _finish_reduce("grads", [scattered[n] for n in order])))
    (all_pack,) = _all_gather("gather_small_grads", [red["pack"]])
    all_flat = all_pack.reshape(-1)
    G = {}
    off = 0
    for n in pack_names:
        shape = grads[n].shape if n == "conv_w" else W[n].shape
        size = int(np.prod(shape))
        G[n] = all_flat[off:off + size].reshape(shape)
        off += size
    total = all_flat[off]
    G["lru_conv_w"] = lax.dynamic_slice_in_dim(G.pop("conv_w"), chip * DQ, DQ, axis=1)[None]
    parts = {n: [_as2d(G[n])] for n in G}
    parts.update(mlp_w1=[red["w1_0"], red["w1_1"]], mlp_w2=[red["w2_0"], red["w2_1"]], lru_w_in=[red["lru_in"]],
                 lru_w_out=[red["lru_out"]], fox_w_in=[red["fox_in"][:nfox, None, :]], fox_w_out=[red["fox_out"]])

    delta, new_m, new_v = {}, {}, {}
    for n in _WEIGHTS:
        if n == "fox_w_in":
            to_thin = lambda a: jnp.transpose(a, (2, 0, 1))
            res = _adamw(f"adamw_{n}", to_thin(W[n]), parts[n], to_thin(Mo[n]), to_thin(Vo[n]))
            G[n], delta[n], new_m[n], new_v[n] = (jnp.transpose(t, (1, 2, 0)) for t in res)
            continue
        go, d, nm, nv = _adamw(f"adamw_{n}", _as2d(W[n]), parts[n], _as2d(Mo[n]), _as2d(Vo[n]))
        G[n], delta[n], new_m[n], new_v[n] = (t.reshape(W[n].shape) for t in (go, d, nm, nv))

    return (total, gx[None], *[G[n] for n in _WEIGHTS], *[delta[n] for n in _WEIGHTS],
            *[new_m[n] for n in _WEIGHTS], *[new_v[n] for n in _WEIGHTS])
```

```python
import functools

import numpy as np
import jax
import jax.numpy as jnp
from jax import lax
from jax.experimental import pallas as pl
from jax.experimental.pallas import tpu as pltpu

F32 = jnp.float32
BF16 = jnp.bfloat16

HEAD_DIM = 64
LRU_BLOCK_DIM = 64
CONV_WIDTH = 4
LRU_C = 8.0
EPS = 1e-6
NEG_INF = -1e30
ADAM_LR = 0.001
ADAM_B1 = 0.9
ADAM_B2 = 0.999
ADAM_EPS = 1e-08
ADAM_WD = 0.01
ADAM_STEP = 10

N_CHIPS = 4
LANES = 128
SUBLANES = 8
MXU_DIM = 256
VMEM_LIMIT = 52 * 1024 * 1024
MATMUL_TILES = (1024, 640, 512, 256, 128)
MATMUL_VMEM = VMEM_LIMIT * 4 // 5
MESH = pl.DeviceIdType.MESH
ANY = pl.BlockSpec(memory_space=pl.ANY)


def _pick(n, prefs):
    for p in prefs:
        if p <= n and n % p == 0:
            return p
    return n


def _params(sem=None):
    return pltpu.CompilerParams(dimension_semantics=sem, vmem_limit_bytes=VMEM_LIMIT)


class _View:
    def __init__(self, arr, kind="plain", shape=None, dtype=None):
        self.arr = arr
        self.kind = kind
        self.shape = tuple(arr.shape) if arr is not None else tuple(shape)
        self.dtype = arr.dtype if arr is not None else dtype

    def limits(self):
        if self.kind == "plain":
            return 0, 0
        return self.shape[-2], (self.shape[-1] if self.kind == "cs" else 0)

    def spec(self, br, bc, fr, fc):
        if self.kind == "plain":
            return pl.BlockSpec((br, bc), lambda *g: (fr(*g), fc(*g)))
        rows, ncol = self.shape[-2:]
        assert rows % br == 0 and ncol % bc == 0, (self.shape, br, bc)
        if self.kind == "cs":
            per = ncol // bc
            return pl.BlockSpec((None, br, bc), lambda *g: (fc(*g) // per, fr(*g), fc(*g) % per))
        per = rows // br
        return pl.BlockSpec((None, br, bc), lambda *g: (fr(*g) // per, fr(*g) % per, fc(*g)))


def _bf(x):
    return x if x.dtype == BF16 else x.astype(BF16)


def _matmul(name, A, B, M, N, K, *, ta=False, tb=False, outs, epilogue, extras=(), vecs=(), n_sums=0,
            tm=None, tn=None, tk=None, plan=None):
    lim = {"m": [M], "n": [N], "k": [K]}
    for view, (rdim, cdim) in ([(A, "km" if ta else "mk"), (B, "nk" if tb else "kn")]
                               + [(e, "mn") for e in extras] + [(o, "mn") for o in outs]):
        r_lim, c_lim = view.limits()
        lim[rdim].append(r_lim)
        lim[cdim].append(c_lim)
    cap = {d: int(np.gcd.reduce(lim[d])) for d in "mnk"}
    tm = tm or _pick(cap["m"], MATMUL_TILES)
    tn = tn or _pick(cap["n"], MATMUL_TILES)
    tk = tk or _pick(cap["k"], MATMUL_TILES)

    def vmem_bytes(tm, tk):
        size = lambda v: jnp.dtype(v.dtype).itemsize
        tiles = tm * tk * size(A) + tk * tn * size(B) + tm * tn * sum(size(v) for v in list(extras) + list(outs))
        return 2 * tiles + (tm * tn * 4 if K > tk else 0)

    if cap["k"] % (2 * tk) == 0 and vmem_bytes(tm, 2 * tk) <= MATMUL_VMEM:
        tk *= 2
    elif K == tk and cap["m"] % (2 * tm) == 0 and vmem_bytes(2 * tm, tk) <= MATMUL_VMEM:
        tm *= 2
    nk = K // tk
    gi = lambda i, j, k: i
    gj = lambda i, j, k: j
    gk = lambda i, j, k: k
    a_spec = A.spec(tk, tm, gk, gi) if ta else A.spec(tm, tk, gi, gk)
    b_spec = B.spec(tn, tk, gj, gk) if tb else B.spec(tk, tn, gk, gj)
    ca = 0 if ta else 1
    cb = 1 if tb else 0
    ne, no = len(extras) + len(vecs), len(outs)
    assert n_sums == 0 or tn == N
    row_spec = pl.BlockSpec((1, tn), lambda i, j, k: (0, j))
    in_specs = [a_spec, b_spec] + [e.spec(tm, tn, gi, gj) for e in extras] + [row_spec] * len(vecs)
    operands = [A.arr, B.arr] + [e.arr for e in extras] + list(vecs)
    out_specs = [o.spec(tm, tn, gi, gj) for o in outs] + [row_spec] * n_sums
    out_shape = ([jax.ShapeDtypeStruct(o.shape, o.dtype) for o in outs]
                 + [jax.ShapeDtypeStruct((1, N), F32)] * n_sums)

    def body(*refs):
        a_ref, b_ref = refs[0], refs[1]
        ex = refs[2:2 + ne]
        o_refs = refs[2 + ne:2 + ne + no]
        s_refs = refs[2 + ne + no:2 + ne + no + n_sums]
        first_row_tile = pl.program_id(0) == 0

        def prod():
            return lax.dot_general(_bf(a_ref[...]), _bf(b_ref[...]), (((ca,), (cb,)), ((), ())),
                                   preferred_element_type=F32)

        def finish(acc):
            res = epilogue(acc, *[e[...] for e in ex])
            for o_ref, r in zip(o_refs, res[:no]):
                o_ref[...] = r.astype(o_ref.dtype)
            for s_ref, r in zip(s_refs, res[no:]):
                def assign(s_ref=s_ref, r=r):
                    s_ref[...] = r

                def accumulate(s_ref=s_ref, r=r):
                    s_ref[...] += r

                pl.when(first_row_tile)(assign)
                pl.when(jnp.logical_not(first_row_tile))(accumulate)

        if nk == 1:
            finish(prod())
        else:
            acc_ref = refs[-1]
            k = pl.program_id(2)

            @pl.when(k == 0)
            def _():
                acc_ref[...] = jnp.zeros_like(acc_ref)

            acc_ref[...] += prod()

            @pl.when(k == nk - 1)
            def _():
                finish(acc_ref[...])

    res, side = _hosted_call(body, name, (M // tm, N // tn, nk), in_specs, out_specs, out_shape,
                             [pltpu.VMEM((tm, tn), F32)] if nk > 1 else [], operands,
                             ("arbitrary", "arbitrary", "arbitrary"), plan)
    return res if plan is None else (res, side)


def _ep_store(acc):
    return (acc,)


def _ep_resid(acc, res):
    return (res + acc,)


def _ep_resid_norm(acc, res, g):
    xo = res + acc
    r = lax.rsqrt(jnp.mean(xo * xo, axis=-1, keepdims=True) + EPS)
    return (xo, (xo * r) * g)


def _ep_norm_bwd(acc, x, dres, g):
    r = lax.rsqrt(jnp.mean(x * x, axis=-1, keepdims=True) + EPS)
    xhat = x * r
    dxn = acc * g
    tot = dres + r * (dxn - xhat * jnp.mean(dxn * xhat, axis=-1, keepdims=True))
    return (tot, tot, jnp.sum(acc * xhat, axis=0, keepdims=True))


def _ep_relu2(acc):
    zp = jnp.maximum(acc, 0.0)
    return (zp * zp,)


def _ep_drelu2(acc, act):
    return (acc * (2.0 * jnp.sqrt(act.astype(F32))),)


def _fresh(M, N, dtype):
    return _View(None, shape=(M, N), dtype=dtype)


def _rms_fwd(name, x, g, S, D, plan=None):
    T = _pick(S, (512, 256, 128))

    def body(x_ref, g_ref, h_ref):
        x = x_ref[...]
        r = lax.rsqrt(jnp.mean(x * x, axis=-1, keepdims=True) + EPS)
        h_ref[...] = ((x * r) * g_ref[...]).astype(BF16)

    return _hosted_call(body, name, (S // T,),
                        [pl.BlockSpec((T, D), lambda i: (i, 0)), pl.BlockSpec((1, D), lambda i: (0, 0))],
                        [pl.BlockSpec((T, D), lambda i: (i, 0))], [jax.ShapeDtypeStruct((S, D), BF16)], [], (x, g),
                        ("arbitrary",), plan)


def _loss_head(x, tgt, S, D):
    T = _pick(S, (512, 256, 128))

    def body(x_ref, t_ref, loss_ref, d_ref, db_ref):
        @pl.when(pl.program_id(0) == 0)
        def _():
            loss_ref[...] = jnp.zeros_like(loss_ref)

        e = x_ref[...] - t_ref[...]
        loss_ref[...] += 0.5 * jnp.sum(jnp.mean(e * e, axis=-1, keepdims=True), axis=0, keepdims=True)
        d = e * (1.0 / D)
        d_ref[...] = d
        db_ref[...] = d.astype(BF16)

    row = pl.BlockSpec((T, D), lambda i: (i, 0))
    return pl.pallas_call(
        body, name="loss_head", grid=(S // T,), in_specs=[row, row],
        out_specs=[pl.BlockSpec((1, 1), lambda i: (0, 0)), row, row],
        out_shape=[jax.ShapeDtypeStruct((1, 1), F32), jax.ShapeDtypeStruct((S, D), F32),
                   jax.ShapeDtypeStruct((S, D), BF16)],
        compiler_params=_params(("arbitrary",)),
    )(x, tgt)


def _sigmoid(z):
    return 1.0 / (1.0 + jnp.exp(-z))


def _log_sigmoid(z):
    return jnp.minimum(z, 0.0) - jnp.log(1.0 + jnp.exp(-jnp.abs(z)))


_GELU_K = 0.7978845608028654
_GELU_C = 0.044715


def _gelu(x):
    t = jnp.tanh(_GELU_K * (x + _GELU_C * (x * x * x)))
    return 0.5 * x * (1.0 + t)


def _gelu_and_grad(x):
    x2 = x * x
    t = jnp.tanh(_GELU_K * (x + _GELU_C * (x2 * x)))
    g = 0.5 * x * (1.0 + t)
    dg = 0.5 * (1.0 + t) + 0.5 * x * (1.0 - t * t) * (_GELU_K * (1.0 + 3.0 * _GELU_C * x2))
    return g, dg


def _decay_terms(r, ls):
    la = LRU_C * r * ls
    a = jnp.exp(la)
    a2 = a * a
    mult = jnp.sqrt(-jnp.tanh(la) * (a2 + 1.0))
    return a, a2, mult


def _lru_fwd(u0, conv_w, conv_b, wr_bd, b_r, wi_bd, b_i, lam, S, D, plan=None):
    T = _pick(S, (256, 128))
    GT = wr_bd.shape[-1]
    nG = D // GT

    def body(gb_ref, xb_ref, cw_ref, cb_ref, wr_ref, br_ref, wi_ref, bi_ref, lam_ref,
             y_ref, xc_ref, r_ref, i_ref, hs_ref, ext, a_scr, hcar):
        @pl.when(pl.program_id(0) == 0)
        def _():
            ext[0:SUBLANES, :] = jnp.zeros((SUBLANES, D), F32)
            hcar[...] = jnp.zeros_like(hcar)

        xb = xb_ref[...]
        ext[SUBLANES:SUBLANES + T, :] = xb
        xc = cb_ref[...]
        for k in range(CONV_WIDTH):
            xc = xc + ext[pl.ds(SUBLANES - (CONV_WIDTH - 1) + k, T), :] * cw_ref[k:k + 1, :]
        ext[0:SUBLANES, :] = xb[T - SUBLANES:T, :]
        xc_ref[...] = xc
        xcb = xc.astype(BF16)
        for g in range(nG):
            sl = slice(g * GT, (g + 1) * GT)
            zr = jnp.dot(xcb[:, sl], wr_ref[g], preferred_element_type=F32) + br_ref[:, sl]
            zi = jnp.dot(xcb[:, sl], wi_ref[g], preferred_element_type=F32) + bi_ref[:, sl]
            r_ref[:, sl] = _sigmoid(zr)
            i_ref[:, sl] = _sigmoid(zi)
        r = r_ref[...]
        a, _, mult = _decay_terms(r, _log_sigmoid(lam_ref[...]))
        a_scr[...] = a
        hs_ref[...] = mult * (i_ref[...] * xc)

        def step(t, h):
            h = a_scr[pl.ds(t, 1), :] * h + hs_ref[pl.ds(t, 1), :]
            hs_ref[pl.ds(t, 1), :] = h
            return h

        hcar[...] = lax.fori_loop(0, T, step, hcar[...], unroll=8)
        y_ref[...] = (_gelu(gb_ref[...]) * hs_ref[...]).astype(BF16)

    row = pl.BlockSpec((T, D), lambda i: (i, 0))
    vec = pl.BlockSpec((1, D), lambda i: (0, 0))
    bd = pl.BlockSpec((nG, GT, GT), lambda i: (0, 0, 0))
    f32o = jax.ShapeDtypeStruct((S, D), F32)
    return _hosted_call(
        body, "lru_fwd", (S // T,),
        [row, pl.BlockSpec((T, D), lambda i: (i, 1)), pl.BlockSpec((CONV_WIDTH, D), lambda i: (0, 0)), vec,
         bd, vec, bd, vec, vec],
        [row, row, row, row, row], [jax.ShapeDtypeStruct((S, D), BF16), f32o, f32o, f32o, f32o],
        [pltpu.VMEM((T + SUBLANES, D), F32), pltpu.VMEM((T, D), F32), pltpu.VMEM((1, D), F32)],
        (u0, u0, conv_w, conv_b, wr_bd, b_r, wi_bd, b_i, lam), ("arbitrary",), plan)


def _lru_bwd(dy, u0, xc, r, ig, hs, conv_w, wr_bd, wi_bd, lam, S, D, plan=None):
    T = _pick(S, (128,))
    nT = S // T
    GT = wr_bd.shape[-1]
    nG = D // GT
    W = CONV_WIDTH

    def body(dy_ref, gb_ref, xb_ref, xbp_ref, xc_ref, r_ref, i_ref, hs_ref, hsp_ref, cw_ref, wr_ref, wi_ref, lam_ref,
             du_ref, dcw_ref, dcb_ref, dlam_ref, dbr_ref, dbi_ref, dwr_ref, dwi_ref,
             a_scr, dh_scr, exth, extx, extd, dxc_scr, dz_scr, carry):
        step = pl.program_id(0)
        first_tile = step == nT - 1

        @pl.when(step == 0)
        def _():
            for ref in (dcw_ref, dcb_ref, dlam_ref, dbr_ref, dbi_ref, dwr_ref, dwi_ref, carry):
                ref[...] = jnp.zeros_like(ref)
            extd[T:T + SUBLANES, :] = jnp.zeros((SUBLANES, D), F32)

        hs = hs_ref[...]
        dy = dy_ref[...]
        g, dgelu = _gelu_and_grad(gb_ref[...])
        du_ref[:, 0:D] = (dy * hs * dgelu).astype(BF16)
        r = r_ref[...]
        lam = lam_ref[...]
        ls = _log_sigmoid(lam)
        a, a2, mult = _decay_terms(r, ls)
        a_scr[...] = a
        dh_scr[...] = dy * g

        def rstep(j, c):
            t = T - 1 - j
            d = dh_scr[pl.ds(t, 1), :] + c
            dh_scr[pl.ds(t, 1), :] = d
            return a_scr[pl.ds(t, 1), :] * d

        carry[...] = lax.fori_loop(0, T, rstep, carry[...], unroll=8)
        dh = dh_scr[...]
        keep = jnp.where(first_tile, 0.0, 1.0)
        exth[0:SUBLANES, :] = hsp_ref[...] * keep
        exth[SUBLANES:SUBLANES + T, :] = hs
        hprev = exth[pl.ds(SUBLANES - 1, T), :]
        xc = xc_ref[...]
        ig = i_ref[...]
        da = dh * hprev
        dmult = dh * (ig * xc)
        dla = da * a - dmult * (a2 / mult)
        dlam_ref[...] += jnp.sum(dla * r, axis=0, keepdims=True) * (LRU_C * _sigmoid(-lam))
        dzr = (dla * (LRU_C * ls)) * (r * (1.0 - r))
        dzi = (dh * (mult * xc)) * (ig * (1.0 - ig))
        dbr_ref[...] += jnp.sum(dzr, axis=0, keepdims=True)
        dbi_ref[...] += jnp.sum(dzi, axis=0, keepdims=True)
        dxc_scr[...] = dh * (mult * ig)
        xcb = xc.astype(BF16)
        dz_scr[0] = dzr.astype(BF16)
        dz_scr[1] = dzi.astype(BF16)
        nt_dims = (((1,), (1,)), ((), ()))
        tn_dims = (((0,), (0,)), ((), ()))
        for gq in range(nG):
            sl = slice(gq * GT, (gq + 1) * GT)
            zr_g = dz_scr[0, :, sl]
            zi_g = dz_scr[1, :, sl]
            dxc_scr[:, sl] += (lax.dot_general(zr_g, wr_ref[gq], nt_dims, preferred_element_type=F32)
                               + lax.dot_general(zi_g, wi_ref[gq], nt_dims, preferred_element_type=F32))
            dwr_ref[gq] += lax.dot_general(xcb[:, sl], zr_g, tn_dims, preferred_element_type=F32)
            dwi_ref[gq] += lax.dot_general(xcb[:, sl], zi_g, tn_dims, preferred_element_type=F32)
        dxc = dxc_scr[...]
        dcb_ref[...] += jnp.sum(dxc, axis=0, keepdims=True)
        extx[0:SUBLANES, :] = xbp_ref[...] * keep
        extx[SUBLANES:SUBLANES + T, :] = xb_ref[...]
        extd[0:T, :] = dxc
        dxb = jnp.zeros((T, D), F32)
        for k in range(W):
            dxb = dxb + extd[pl.ds(W - 1 - k, T), :] * cw_ref[k:k + 1, :]
            dcw_ref[k:k + 1, :] += jnp.sum(dxc * extx[pl.ds(SUBLANES - (W - 1) + k, T), :], axis=0, keepdims=True)
        extd[T:T + SUBLANES, :] = dxc[0:SUBLANES, :]
        du_ref[:, D:2 * D] = dxb.astype(BF16)

    rev = lambda i: nT - 1 - i
    tpb = T // SUBLANES
    prev8 = lambda i: jnp.maximum(rev(i) * tpb - 1, 0)
    row = pl.BlockSpec((T, D), lambda i: (rev(i), 0))
    vec = pl.BlockSpec((1, D), lambda i: (0, 0))
    bd = pl.BlockSpec((nG, GT, GT), lambda i: (0, 0, 0))
    vec_o = jax.ShapeDtypeStruct((1, D), F32)
    bd_o = jax.ShapeDtypeStruct((nG, GT, GT), F32)
    return _hosted_call(
        body, "lru_bwd", (nT,),
        [row, row, pl.BlockSpec((T, D), lambda i: (rev(i), 1)), pl.BlockSpec((SUBLANES, D), lambda i: (prev8(i), 1)),
         row, row, row, row, pl.BlockSpec((SUBLANES, D), lambda i: (prev8(i), 0)),
         pl.BlockSpec((W, D), lambda i: (0, 0)), bd, bd, vec],
        [pl.BlockSpec((T, 2 * D), lambda i: (rev(i), 0)), pl.BlockSpec((W, D), lambda i: (0, 0)),
         vec, vec, vec, vec, bd, bd],
        [jax.ShapeDtypeStruct((S, 2 * D), BF16), jax.ShapeDtypeStruct((W, D), F32), vec_o, vec_o, vec_o, vec_o, bd_o, bd_o],
        [pltpu.VMEM((T, D), F32), pltpu.VMEM((T, D), F32), pltpu.VMEM((T + SUBLANES, D), F32),
         pltpu.VMEM((T + SUBLANES, D), F32), pltpu.VMEM((T + SUBLANES, D), F32),
         pltpu.VMEM((T, D), F32), pltpu.VMEM((2, T, D), BF16), pltpu.VMEM((1, D), F32)],
        (dy, u0, u0, u0, xc, r, ig, hs, hs, conv_w, wr_bd, wi_bd, lam), ("arbitrary",), plan)


AUG_ROWS = 16
HEAD_ROWS = 128
LSE_ROW = HEAD_DIM + 6
ONES_ROW_Q = HEAD_DIM + 3
ONES_COL_K = HEAD_DIM
ONES_ROW_V = HEAD_DIM
PREP_LANES = 512
HEAD_UNROLL = 4


def _split3(x):
    b1 = x.astype(BF16).astype(F32)
    r = x - b1
    b2 = r.astype(BF16).astype(F32)
    return b1, b2, r - b2


def _head_block(x, aug, T):
    row = lax.broadcasted_iota(jnp.int32, (AUG_ROWS, T), 0)
    blk = jnp.zeros((AUG_ROWS, T), F32)
    for i, e in enumerate(aug):
        blk = jnp.where(row == i, e, blk)
    return jnp.concatenate([x, blk, jnp.zeros((HEAD_ROWS - HEAD_DIM - AUG_ROWS, T), F32)], axis=0)


def _tri_matrix(lower):
    i = np.arange(LANES)
    m = (i[:, None] >= i[None, :]) if lower else (i[:, None] <= i[None, :])
    return jnp.asarray(m.astype(np.float32), BF16)


def _lane_cumsum(x, tri_ref, carry, reverse):
    n = x.shape[1] // LANES
    tri = tri_ref[...]
    out = [None] * n
    for j in (range(n - 1, -1, -1) if reverse else range(n)):
        cs = carry
        for part in _split3(x[:, j * LANES:(j + 1) * LANES]):
            cs = cs + jnp.dot(part.astype(BF16), tri, preferred_element_type=F32)
        out[j] = cs
        carry = cs[:, 0:1] if reverse else cs[:, LANES - 1:LANES]
    return jnp.concatenate(out, axis=1), carry


def _head_rows(h):
    return pl.ds(pl.multiple_of(h * HEAD_DIM, HEAD_DIM), HEAD_DIM)


def _fox_prep(ut, b_f, qg, kg, S, D, tq):
    H = D // HEAD_DIM
    T = min(tq, PREP_LANES)
    per = tq // T
    scale = HEAD_DIM ** -0.5

    def body(q_ref, k_ref, v_ref, f_ref, bf_ref, qg_ref, kg_ref, tri_ref,
             qat_ref, kat_ref, vat_ref, ka_ref, c_scr, ccar):
        @pl.when(pl.program_id(0) == 0)
        def _():
            ccar[...] = jnp.zeros_like(ccar)

        c, carry = _lane_cumsum(_log_sigmoid(f_ref[...] + bf_ref[...]), tri_ref, ccar[...], False)
        c_scr[...] = c
        ccar[...] = carry

        def head(h, _):
            rows = _head_rows(h)
            c1, c2, c3 = _split3(c_scr[pl.ds(h, 1), :])

            def normed(src, gain, mul):
                x = src[rows, :]
                rs = lax.rsqrt(jnp.mean(x * x, axis=0, keepdims=True) + EPS)
                return ((x * rs) * gain[rows, :]) * mul

            qat_ref[h] = _head_block(normed(q_ref, qg_ref, scale), [c1, c2, c3, 1.0, 1.0, 1.0], T).astype(BF16)
            kb = _head_block(normed(k_ref, kg_ref, 1.0), [1.0, 1.0, 1.0, -c1, -c2, -c3, 1.0, 1.0, 1.0], T)
            kat_ref[h] = kb.astype(BF16)
            ka_ref[h] = kb.T.astype(BF16)
            vat_ref[h] = _head_block(v_ref[rows, :], [1.0, 1.0, 1.0], T).astype(BF16)
            return 0

        lax.fori_loop(0, H, head, 0, unroll=min(HEAD_UNROLL, H))

    part = lambda j: pl.BlockSpec((D, T), lambda i: (j, i))
    colv = lambda n: pl.BlockSpec((n, 1), lambda i: (0, 0))
    tmaj = lambda r: pl.BlockSpec((H, None, r, T), lambda i: (0, i // per, 0, i % per))
    norm = pl.BlockSpec((H, T, HEAD_ROWS), lambda i: (0, i, 0))
    tshape = lambda r: jax.ShapeDtypeStruct((H, S // tq, r, tq), BF16)
    nshape = jax.ShapeDtypeStruct((H, S, HEAD_ROWS), BF16)
    return pl.pallas_call(
        body, name="fox_prep", grid=(S // T,),
        in_specs=[part(0), part(1), part(2), pl.BlockSpec((LANES, T), lambda i: (3 * D // LANES, i)),
                  colv(LANES), colv(D), colv(D), pl.BlockSpec((LANES, LANES), lambda i: (0, 0))],
        out_specs=[tmaj(HEAD_ROWS), tmaj(HEAD_ROWS), tmaj(HEAD_ROWS), norm],
        out_shape=[tshape(HEAD_ROWS), tshape(HEAD_ROWS), tshape(HEAD_ROWS), nshape],
        scratch_shapes=[pltpu.VMEM((LANES, T), F32), pltpu.VMEM((LANES, 1), F32)],
        compiler_params=_params(("arbitrary",)),
    )(ut, ut, ut, ut, b_f, qg, kg, _tri_matrix(False))


def _fox_bwd_prep(dot, ot, lse, qat, S, D, tq, plan=None):
    H = D // HEAD_DIM
    T = min(tq, PREP_LANES)
    per = tq // T

    def body(do_ref, o_ref, lse_ref, qat_ref, doat_ref, doa_ref, qat1_ref, qa1_ref):
        row = lax.broadcasted_iota(jnp.int32, (HEAD_ROWS, T), 0)

        def head(h, _):
            rows = _head_rows(h)
            do = do_ref[rows, :].astype(F32)
            delta = jnp.sum(do * o_ref[rows, :], axis=0, keepdims=True)
            db = _head_block(do, list(_split3(-delta)), T)
            doat_ref[h] = db.astype(BF16)
            doa_ref[h] = db.T.astype(BF16)
            qb = qat_ref[h].astype(F32)
            for i, e in enumerate(_split3(-lse_ref[h])):
                qb = jnp.where(row == LSE_ROW + i, e, qb)
            qat1_ref[h] = qb.astype(BF16)
            qa1_ref[h] = qb.T.astype(BF16)
            return 0

        lax.fori_loop(0, H, head, 0, unroll=min(HEAD_UNROLL, H))

    chan = pl.BlockSpec((D, T), lambda i: (0, i))
    tmaj = pl.BlockSpec((H, None, HEAD_ROWS, T), lambda i: (0, i // per, 0, i % per))
    norm = pl.BlockSpec((H, T, HEAD_ROWS), lambda i: (0, i, 0))
    tshape = jax.ShapeDtypeStruct((H, S // tq, HEAD_ROWS, tq), BF16)
    nshape = jax.ShapeDtypeStruct((H, S, HEAD_ROWS), BF16)
    return _hosted_call(body, "fox_bwd_prep", (S // T,), [chan, chan, pl.BlockSpec((H, 1, T), lambda i: (0, 0, i)), tmaj],
                        [tmaj, norm, tmaj, norm], [tshape, nshape, tshape, nshape], [], (dot, ot, lse, qat),
                        ("arbitrary",), plan)


def _causal(s, k_axis, d):
    ki = lax.broadcasted_iota(jnp.int32, s.shape, k_axis) - (s.shape[k_axis] - d)
    qi = lax.broadcasted_iota(jnp.int32, s.shape, 1 - k_axis)
    return jnp.where(ki <= qi, s, NEG_INF)


def _attn_forward(ka, qat, vat, S, D, tq, plan=None):
    H = D // HEAD_DIM
    nq = S // tq
    G = 4

    def body(ka_ref, qat_ref, vat_ref, o_ref, o32_ref, lse_ref, m_scr, acc_scr):
        qi = pl.program_id(1)
        m_scr[...] = jnp.full_like(m_scr, NEG_INF)
        acc_scr[...] = jnp.zeros_like(acc_scr)

        hq = tq // 2
        everything, lower, upper = slice(0, tq), slice(0, hq), slice(hq, tq)

        def update(k_start, k_len, cols, pieces, diag):
            keys = pl.ds(pl.multiple_of(k_start, hq), k_len)
            s = [jnp.dot(ka_ref[g, keys, :], qat_ref[g, :, cols], preferred_element_type=F32) for g in range(G)]
            if diag:
                s = [_causal(sg, 0, diag) for sg in s]
            m_prev = [m_scr[g, :, cols] for g in range(G)]
            m_new = [jnp.maximum(m_prev[g], jnp.max(s[g], axis=0, keepdims=True)) for g in range(G)]
            p = [jnp.exp(s[g] - m_new[g]).astype(BF16) for g in range(G)]
            for g in range(G):
                upd = sum(jnp.dot(vat_ref[g, t, :, lanes], p[g][rows], preferred_element_type=F32)
                          for t, lanes, rows in pieces)
                acc_scr[g, :, cols] = jnp.exp(m_prev[g] - m_new[g]) * acc_scr[g, :, cols] + upd
                m_scr[g, :, cols] = m_new[g]

        def off_diagonal_pair(j, _):
            update(2 * j * tq, 2 * tq, everything, [(2 * j, everything, everything), (2 * j + 1, everything, slice(tq, 2 * tq))], 0)
            return 0

        lax.fori_loop(0, qi // 2, off_diagonal_pair, 0)
        pl.when(qi % 2 == 1)(lambda: update((qi - 1) * tq, tq + hq, everything,
                                            [(qi - 1, everything, everything), (qi, lower, slice(tq, tq + hq))], hq))
        pl.when(qi % 2 == 0)(lambda: update(qi * tq, hq, everything, [(qi, lower, lower)], hq))
        update(qi * tq + hq, hq, upper, [(qi, upper, lower)], hq)
        for g in range(G):
            l = acc_scr[g, ONES_ROW_V:ONES_ROW_V + 1, :]
            o = acc_scr[g, 0:HEAD_DIM, :] / l
            o_ref[g * HEAD_DIM:(g + 1) * HEAD_DIM, :] = o.astype(BF16)
            o32_ref[g * HEAD_DIM:(g + 1) * HEAD_DIM, :] = o
            lse_ref[g] = m_scr[g] + jnp.log(l)

    chan = pl.BlockSpec((G * HEAD_DIM, tq), lambda h, i: (h, i))
    stat = pl.BlockSpec((G, 1, tq), lambda h, i: (h, 0, i))
    return _hosted_call(
        body, "attn_forward", (H // G, nq),
        [pl.BlockSpec((G, S, HEAD_ROWS), lambda h, i: (h, 0, 0)),
         pl.BlockSpec((G, None, HEAD_ROWS, tq), lambda h, i: (h, i, 0, 0)),
         pl.BlockSpec((G, nq, HEAD_ROWS, tq), lambda h, i: (h, 0, 0, 0))],
        [chan, chan, stat],
        [jax.ShapeDtypeStruct((D, S), BF16), jax.ShapeDtypeStruct((D, S), F32), jax.ShapeDtypeStruct((H, 1, S), F32)],
        [pltpu.VMEM((G, 1, tq), F32), pltpu.VMEM((G, HEAD_ROWS, tq), F32)],
        (ka, qat, vat), ("arbitrary", "arbitrary"), plan)


def _attn_backward(qa, doa, qat, doat, ka, kat, vat, S, D, tq, plan=None):
    H = D // HEAD_DIM
    nq = S // tq
    G = 2

    def body(qa_ref, doa_ref, qat_ref, doat_ref, ka_ref, kat_ref, vat_ref, dq_ref, dk_ref, dv_ref, dk_scr, dv_scr):
        ki = pl.program_id(1)

        @pl.when(ki == 0)
        def _():
            dq_ref[...] = jnp.zeros_like(dq_ref)

        dk_scr[...] = jnp.zeros_like(dk_scr)
        dv_scr[...] = jnp.zeros_like(dv_scr)

        hq = tq // 2
        everything, lower, upper = slice(0, tq), slice(0, hq), slice(hq, tq)

        def block(q_start, q_len, keys, pieces, diag):
            rows = pl.ds(pl.multiple_of(q_start, hq), q_len)
            s = [jnp.dot(qa_ref[g, rows, :], kat_ref[g, :, keys], preferred_element_type=F32) for g in range(G)]
            if diag:
                s = [_causal(sg, 1, diag) for sg in s]
            p = [jnp.exp(sg) for sg in s]
            ds = [(p[g] * jnp.dot(doa_ref[g, rows, :], vat_ref[g, :, keys], preferred_element_type=F32)).astype(BF16)
                  for g in range(G)]
            p = [pg.astype(BF16) for pg in p]
            for g in range(G):
                for t, lanes, part in pieces:
                    dv_scr[g, :, keys] += jnp.dot(doat_ref[g, t, 0:HEAD_DIM, lanes], p[g][part],
                                                  preferred_element_type=F32)
                    dk_scr[g, :, keys] += jnp.dot(qat_ref[g, t, :, lanes], ds[g][part], preferred_element_type=F32)
                dq_ref[g, rows, :] += jnp.dot(ds[g], ka_ref[g, keys, :], preferred_element_type=F32)

        n_off = nq - 1 - ki
        odd = n_off % 2

        def off_diagonal_pair(j, _):
            q0 = ki + 1 + odd + 2 * j
            block(q0 * tq, 2 * tq, everything, [(q0, everything, everything), (q0 + 1, everything, slice(tq, 2 * tq))], 0)
            return 0

        block(ki * tq, hq, lower, [(ki, lower, lower)], hq)
        pl.when(odd == 1)(lambda: block(ki * tq + hq, hq + tq, everything,
                                        [(ki, upper, lower), (ki + 1, everything, slice(hq, hq + tq))], hq))
        pl.when(odd == 0)(lambda: block(ki * tq + hq, hq, everything, [(ki, upper, lower)], hq))
        lax.fori_loop(0, n_off // 2, off_diagonal_pair, 0)
        dk_ref[...] = dk_scr[...]
        for g in range(G):
            dv_ref[g * HEAD_DIM:(g + 1) * HEAD_DIM, :] = dv_scr[g].astype(BF16)

    whole = pl.BlockSpec((G, S, HEAD_ROWS), lambda h, i: (h, 0, 0))
    tiles = pl.BlockSpec((G, nq, HEAD_ROWS, tq), lambda h, i: (h, 0, 0, 0))
    one = pl.BlockSpec((G, None, HEAD_ROWS, tq), lambda h, i: (h, i, 0, 0))
    return _hosted_call(
        body, "attn_backward", (H // G, nq),
        [whole, whole, tiles, tiles, pl.BlockSpec((G, tq, HEAD_ROWS), lambda h, i: (h, i, 0)), one, one],
        [whole, pl.BlockSpec((G, HEAD_ROWS, tq), lambda h, i: (h, 0, i)),
         pl.BlockSpec((G * HEAD_DIM, tq), lambda h, i: (h, i))],
        [jax.ShapeDtypeStruct((H, S, HEAD_ROWS), F32), jax.ShapeDtypeStruct((H, HEAD_ROWS, S), F32),
         jax.ShapeDtypeStruct((D, S), BF16)],
        [pltpu.VMEM((G, HEAD_ROWS, tq), F32), pltpu.VMEM((G, HEAD_DIM, tq), F32)],
        (qa, doa, qat, doat, ka, kat, vat), ("arbitrary", "arbitrary"), plan)


def _fox_prep_bwd(ut, dq, dkt, dvt, b_f, qg, kg, S, D, tq):
    H = D // HEAD_DIM
    T = min(tq, PREP_LANES)
    nT = S // T
    NU = 3 * D + LANES
    scale = HEAD_DIM ** -0.5

    def body(q_ref, k_ref, f_ref, dq_ref, dk_ref, dv_ref, bf_ref, qg_ref, kg_ref, tri_ref,
             du_ref, dbf_ref, dqg_ref, dkg_ref, gq_acc, gk_acc, fcar, dc_scr):
        step = pl.program_id(0)

        @pl.when(step == 0)
        def _():
            for ref in (gq_acc, gk_acc, fcar, dbf_ref):
                ref[...] = jnp.zeros_like(ref)

        dc_scr[...] = jnp.zeros_like(dc_scr)

        def head(h, _):
            rows = _head_rows(h)
            dqb = dq_ref[h].T
            dkb = dk_ref[h]
            dc_scr[pl.ds(h, 1), :] = dqb[ONES_COL_K:ONES_COL_K + 1, :] - dkb[ONES_ROW_Q:ONES_ROW_Q + 1, :]
            for src, dsrc, gain, acc, mul, base in ((q_ref, dqb, qg_ref, gq_acc, scale, 0),
                                                    (k_ref, dkb, kg_ref, gk_acc, 1.0, D)):
                x = src[rows, :]
                rs = lax.rsqrt(jnp.mean(x * x, axis=0, keepdims=True) + EPS)
                xhat = x * rs
                dn = dsrc[0:HEAD_DIM, :] * mul
                acc[rows, :] += jnp.sum(dn * xhat, axis=1, keepdims=True)
                dxh = dn * gain[rows, :]
                dx = rs * (dxh - xhat * jnp.mean(dxh * xhat, axis=0, keepdims=True))
                du_ref[pl.ds(pl.multiple_of(base + h * HEAD_DIM, HEAD_DIM), HEAD_DIM), :] = dx.astype(BF16)
            return 0

        lax.fori_loop(0, H, head, 0, unroll=min(HEAD_UNROLL, H))
        du_ref[2 * D:3 * D, :] = dv_ref[...]
        dlf, carry = _lane_cumsum(dc_scr[...], tri_ref, fcar[...], True)
        fcar[...] = carry
        dfl = dlf * _sigmoid(-(f_ref[...] + bf_ref[...]))
        dbf_ref[...] += jnp.sum(dfl, axis=1, keepdims=True)
        du_ref[3 * D:NU, :] = dfl.astype(BF16)

        @pl.when(step == nT - 1)
        def _():
            for acc, ref in ((gq_acc, dqg_ref), (gk_acc, dkg_ref)):
                tot = jnp.zeros((HEAD_DIM, 1), F32)
                for h in range(H):
                    tot = tot + acc[h * HEAD_DIM:(h + 1) * HEAD_DIM, :]
                ref[...] = tot

    rev = lambda i: nT - 1 - i
    part = lambda j: pl.BlockSpec((D, T), lambda i: (j, rev(i)))
    colv = lambda n: pl.BlockSpec((n, 1), lambda i: (0, 0))
    return pl.pallas_call(
        body, name="fox_prep_bwd", grid=(nT,),
        in_specs=[part(0), part(1), pl.BlockSpec((LANES, T), lambda i: (3 * D // LANES, rev(i))),
                  pl.BlockSpec((H, T, HEAD_ROWS), lambda i: (0, rev(i), 0)),
                  pl.BlockSpec((H, HEAD_ROWS, T), lambda i: (0, 0, rev(i))), pl.BlockSpec((D, T), lambda i: (0, rev(i))),
                  colv(LANES), colv(D), colv(D), pl.BlockSpec((LANES, LANES), lambda i: (0, 0))],
        out_specs=[pl.BlockSpec((NU, T), lambda i: (0, rev(i))), colv(LANES), colv(HEAD_DIM), colv(HEAD_DIM)],
        out_shape=[jax.ShapeDtypeStruct((NU, S), BF16), jax.ShapeDtypeStruct((LANES, 1), F32),
                   jax.ShapeDtypeStruct((HEAD_DIM, 1), F32), jax.ShapeDtypeStruct((HEAD_DIM, 1), F32)],
        scratch_shapes=[pltpu.VMEM((D, 1), F32), pltpu.VMEM((D, 1), F32), pltpu.VMEM((LANES, 1), F32),
                        pltpu.VMEM((LANES, T), F32)],
        compiler_params=_params(("arbitrary",)),
    )(ut, ut, ut, dq, dkt, dvt, b_f, qg, kg, _tri_matrix(True))


def _block_diag_tiles(w):
    n = w.shape[0]
    per = min(MXU_DIM, n * LRU_BLOCK_DIM) // LRU_BLOCK_DIM
    eye = jnp.eye(per, dtype=w.dtype)
    w5 = w.reshape(n // per, per, LRU_BLOCK_DIM, 1, LRU_BLOCK_DIM) * eye[None, :, None, :, None]
    return w5.reshape(n // per, per * LRU_BLOCK_DIM, per * LRU_BLOCK_DIM).astype(BF16)


def _block_diag_extract(t, n):
    per = t.shape[-1] // LRU_BLOCK_DIM
    eye = jnp.eye(per, dtype=t.dtype)
    t5 = t.reshape(n // per, per, LRU_BLOCK_DIM, per, LRU_BLOCK_DIM) * eye[None, :, None, :, None]
    return t5.sum(axis=3).reshape(n, LRU_BLOCK_DIM, LRU_BLOCK_DIM)


def _local_step(x, tgt, small, wv, grad_view, comm=None):
    S, D = x.shape
    F = 4 * D
    H = D // HEAD_DIM
    nblk = D // LRU_BLOCK_DIM
    NU = 3 * D + LANES
    tq = max(LANES, min(512, S // 4))
    assert S % tq == 0
    vec = lambda a: a.reshape(1, -1).astype(F32)
    col = lambda a: a.reshape(-1, 1).astype(F32)
    mix_g, mlp_g = small["mix_norm"], small["mlp_norm"]
    conv_b = vec(small["lru_conv_b"])
    wr_bd, wi_bd = _block_diag_tiles(small["lru_w_r"][0]), _block_diag_tiles(small["lru_w_i"][0])
    b_r, b_i, lam = vec(small["lru_b_r"]), vec(small["lru_b_i"]), vec(small["lru_lambda"])
    b_f = jnp.pad(col(small["fox_b_f"]), ((0, LANES - H), (0, 0)))
    qg, kg = jnp.tile(col(small["fox_q_gain"]), (H, 1)), jnp.tile(col(small["fox_k_gain"]), (H, 1))
    X = lambda a: _View(a)
    grads = {}
    gout = functools.partial(grad_view, grads)

    def hosted(name, fn, *args):
        plan = comm.before(name, grads) if comm is not None else None
        res, side = fn(*args, plan=plan)
        if plan is not None:
            comm.after(name, side, wv)
        return res

    def hosted_mm(name, *args, **kw):
        plan = comm.before(name, grads) if comm is not None else None
        if plan is None:
            return _matmul(name, *args, **kw)
        res, side = _matmul(name, *args, plan=plan, **kw)
        comm.after(name, side, wv)
        return res

    two = lambda: [_fresh(S, D, F32), _fresh(S, D, BF16)]

    def mlp_up(l, hm):
        return hosted_mm(f"mlp{l}_up", X(hm), wv[f"w1_{l}"], S, F, D, outs=[_fresh(S, F, BF16)], epilogue=_ep_relu2)[0]

    def mlp_bwd(l, xin, hm, act, d, db):
        (dz,) = hosted_mm(f"mlp{l}_dact", X(db), wv[f"w2_{l}"], S, F, D, tb=True, outs=[_fresh(S, F, BF16)],
                          epilogue=_ep_drelu2, extras=[X(act)])
        (grads[f"w2_{l}"],) = _matmul(f"mlp{l}_dw2", X(act), X(db), F, D, S, ta=True, outs=[gout(f"w2_{l}")],
                                      epilogue=_ep_store)
        (grads[f"w1_{l}"],) = _matmul(f"mlp{l}_dw1", X(hm), X(dz), D, F, S, ta=True, outs=[gout(f"w1_{l}")],
                                      epilogue=_ep_store)
        return _matmul(f"mlp{l}_dhm", X(dz), wv[f"w1_{l}"], S, D, F, tb=True, outs=two(), n_sums=1,
                       epilogue=_ep_norm_bwd, extras=[X(xin), X(d)], vecs=[mlp_g[l:l + 1]])

    (h0,) = hosted("mix0_norm", _rms_fwd, "mix0_norm", x, mix_g[0:1], S, D)
    (u0,) = hosted_mm("lru_in", X(h0), wv["lru_in"], S, 2 * D, D, outs=[_fresh(S, 2 * D, F32)], epilogue=_ep_store)
    conv_w = small["conv_w"]
    y, xc, r, ig, hs = hosted("lru_fwd", _lru_fwd, u0, conv_w, conv_b, wr_bd, b_r, wi_bd, b_i, lam, S, D)
    x1, hm0 = _matmul("lru_out", X(y), wv["lru_out"], S, D, D, outs=two(), epilogue=_ep_resid_norm, extras=[X(x)],
                      vecs=[mlp_g[0:1]])
    act0 = mlp_up(0, hm0)
    x2, h1 = hosted_mm("mlp0_down", X(act0), wv["w2_0"], S, D, F, outs=two(), epilogue=_ep_resid_norm, extras=[X(x1)],
                       vecs=[mix_g[1:2]])
    (u1,) = _matmul("fox_in", wv["fox_in"], X(h1), NU, S, D, tb=True, outs=[_fresh(NU, S, F32)], epilogue=_ep_store)
    qat, kat, vat, ka = _fox_prep(u1, b_f, qg, kg, S, D, tq)
    o, o32, lse = hosted("attn_forward", _attn_forward, ka, qat, vat, S, D, tq)
    x3, hm1 = _matmul("fox_out", X(o), wv["fox_out"], S, D, D, ta=True, outs=two(), epilogue=_ep_resid_norm,
                      extras=[X(x2)], vecs=[mlp_g[1:2]])
    act1 = mlp_up(1, hm1)
    (x4,) = _matmul("mlp1_down", X(act1), wv["w2_1"], S, D, F, outs=[_fresh(S, D, F32)], epilogue=_ep_resid,
                    extras=[X(x3)])
    loss, d4, d4b = _loss_head(x4, tgt, S, D)

    d3, d3b, dg_mlp1 = mlp_bwd(1, x3, hm1, act1, d4, d4b)
    (do,) = _matmul("fox_dout", wv["fox_out"], X(d3b), D, S, D, tb=True, outs=[_fresh(D, S, BF16)], epilogue=_ep_store)
    (grads["fox_out"],) = _matmul("fox_dwout", X(o), X(d3b), D, D, S, outs=[gout("fox_out")], epilogue=_ep_store)
    doat, doa, qat1, qa1 = hosted("fox_bwd_prep", _fox_bwd_prep, do, o32, lse, qat, S, D, tq)
    dqn, dkn, dv = hosted("attn_backward", _attn_backward, qa1, doa, qat1, doat, ka, kat, vat, S, D, tq)
    du1, dbf, dqg, dkg = _fox_prep_bwd(u1, dqn, dkn, dv, b_f, qg, kg, S, D, tq)
    (grads["fox_in"],) = _matmul("fox_dwin", X(du1), X(h1), NU, D, S, outs=[gout("fox_in")], epilogue=_ep_store)
    d2, d2b, dg_mix1 = hosted_mm("fox_dh", X(du1), wv["fox_in"], S, D, NU, ta=True, outs=two(), n_sums=1,
                               epilogue=_ep_norm_bwd, extras=[X(x2), X(d3)], vecs=[mix_g[1:2]])
    d1, d1b, dg_mlp0 = mlp_bwd(0, x1, hm0, act0, d2, d2b)
    (grads["lru_out"],) = _matmul("lru_dwout", X(y), X(d1b), D, D, S, ta=True, outs=[gout("lru_out")],
                                  epilogue=_ep_store)
    (dy,) = hosted_mm("lru_dout", X(d1b), wv["lru_out"], S, D, D, tb=True, outs=[_fresh(S, D, F32)],
                      epilogue=_ep_store)
    du0, dcw, dcb, dlam, dbr, dbi, dwr, dwi = hosted("lru_bwd", _lru_bwd, dy, u0, xc, r, ig, hs, conv_w, wr_bd, wi_bd,
                                                     lam, S, D)
    (grads["lru_in"],) = _matmul("lru_dwin", X(h0), X(du0), D, 2 * D, S, ta=True, outs=[gout("lru_in")],
                                 epilogue=_ep_store)
    gx, dg_mix0 = hosted_mm("lru_dh", X(du0), wv["lru_in"], S, D, 2 * D, tb=True, outs=[_fresh(S, D, F32)], n_sums=1,
                            epilogue=lambda *a: _ep_norm_bwd(*a)[::2], extras=[X(x), X(d1)], vecs=[mix_g[0:1]])

    grads.update(
        mix_norm=jnp.concatenate([dg_mix0, dg_mix1], axis=0), mlp_norm=jnp.concatenate([dg_mlp0, dg_mlp1], axis=0),
        conv_w=dcw, lru_conv_b=dcb, lru_w_r=_block_diag_extract(dwr, nblk)[None], lru_b_r=dbr.reshape(1, nblk, -1),
        lru_w_i=_block_diag_extract(dwi, nblk)[None], lru_b_i=dbi.reshape(1, nblk, -1), lru_lambda=dlam,
        fox_b_f=dbf[:H].reshape(1, H), fox_q_gain=dqg.reshape(1, -1), fox_k_gain=dkg.reshape(1, -1))
    return loss, gx, grads


def _place():
    x, y, c = lax.axis_index("x"), lax.axis_index("y"), lax.axis_index("c")
    chips = [(1 - x, y), (x, 1 - y), (1 - x, 1 - y)]
    return x, y, c, 2 * x + y, chips


BOUNCE_BYTES = 1 << 20


def _bounce_shape(rows, cols, dtype):
    chunk = rows
    while chunk % 2 == 0 and chunk > 16 and chunk * cols * jnp.dtype(dtype).itemsize > BOUNCE_BYTES:
        chunk //= 2
    return pltpu.VMEM((2, chunk, cols), dtype)


def _bounce_copy(src, dst, buf, sem):
    chunk = buf.shape[1]
    n = src.shape[0] // chunk
    cin = lambda i: pltpu.make_async_copy(src.at[pl.ds(i * chunk, chunk)], buf.at[i % 2], sem.at[i % 2])
    cout = lambda i: pltpu.make_async_copy(buf.at[i % 2], dst.at[pl.ds(i * chunk, chunk)], sem.at[2 + i % 2])
    cin(0).start()
    for i in range(n):
        cin(i).wait()
        if i + 1 < n:
            if i >= 1:
                cout(i - 1).wait()
            cin(i + 1).start()
        cout(i).start()
    if n >= 2:
        cout(n - 2).wait()
    cout(n - 1).wait()


def _hbm_call(body, name, arrays, out_shape, n_dma_sems, bounce=()):
    scratch = [pltpu.SemaphoreType.DMA((k,)) for k in n_dma_sems]
    for rows, cols, dtype in bounce:
        scratch += [_bounce_shape(rows, cols, dtype), pltpu.SemaphoreType.DMA((4,))]
    return pl.pallas_call(
        body, name=name, in_specs=[ANY] * len(arrays), out_specs=[ANY] * len(out_shape), out_shape=out_shape,
        scratch_shapes=scratch,
        compiler_params=pltpu.CompilerParams(has_side_effects=True, vmem_limit_bytes=VMEM_LIMIT),
    )(*arrays)


class _Gather:
    def __init__(self, shards):
        n = self.n = len(shards)
        self.operands = list(shards)
        self.out_shape = [jax.ShapeDtypeStruct((N_CHIPS,) + tuple(a.shape), a.dtype) for a in shards]
        self.scratch = [pltpu.SemaphoreType.DMA((3 * n,)) for _ in range(4)]
        for a in shards:
            self.scratch += [_bounce_shape(a.shape[0], a.shape[1], a.dtype), pltpu.SemaphoreType.DMA((4,))]

    def _copies(self, ins, outs, scr):
        send, recv, fsend, frecv = scr[:4]
        x, y, c, s, chips = _place()

        def rows(a, chip_idx, which):
            hr = ins[a].shape[0] // 2
            return outs[a].at[chip_idx, pl.ds(which * hr, hr)]

        def landed(a, j, core):
            return rows(a, 2 * chips[j][0] + chips[j][1], core)

        def ici(a, j, mine):
            hr = ins[a].shape[0] // 2
            src, dst = (ins[a].at[pl.ds(c * hr, hr)], rows(a, s, c)) if mine else (landed(a, j, c),) * 2
            return pltpu.make_async_remote_copy(src_ref=src, dst_ref=dst, send_sem=send.at[3 * a + j],
                                                recv_sem=recv.at[3 * a + j], device_id=(*chips[j], c),
                                                device_id_type=MESH)

        def d2d(a, j, mine):
            ref = landed(a, j, c if mine else 1 - c)
            return pltpu.make_async_remote_copy(src_ref=ref, dst_ref=ref, send_sem=fsend.at[3 * a + j],
                                                recv_sem=frecv.at[3 * a + j], device_id=(x, y, 1 - c),
                                                device_id_type=MESH)

        return ici, d2d, s

    def start(self, ins, outs, scr):
        ici, _, _ = self._copies(ins, outs, scr)
        for a in range(self.n):
            for j in range(3):
                ici(a, j, True).start()

    def middle(self, ins, outs, scr):
        ici, d2d, s = self._copies(ins, outs, scr)
        for a in range(self.n):
            _bounce_copy(ins[a], outs[a].at[s], scr[4 + 2 * a], scr[5 + 2 * a])
        for a in range(self.n):
            for j in range(3):
                ici(a, j, False).wait_recv()
                d2d(a, j, True).start()

    def finish(self, ins, outs, scr):
        ici, d2d, _ = self._copies(ins, outs, scr)
        for a in range(self.n):
            for j in range(3):
                d2d(a, j, False).wait_recv()
        for a in range(self.n):
            for j in range(3):
                ici(a, j, True).wait_send()
                d2d(a, j, True).wait_send()


def _run_plan(name, plan):
    k_in, k_out = len(plan.operands), len(plan.out_shape)

    def body(*refs):
        parts = (refs[:k_in], refs[k_in:k_in + k_out], refs[k_in + k_out:])
        plan.start(*parts)
        plan.middle(*parts)
        plan.finish(*parts)

    return pl.pallas_call(
        body, name=name, in_specs=[ANY] * k_in, out_specs=[ANY] * k_out, out_shape=plan.out_shape,
        scratch_shapes=plan.scratch,
        compiler_params=pltpu.CompilerParams(has_side_effects=True, vmem_limit_bytes=VMEM_LIMIT),
    )(*plan.operands)


def _hosted_call(body, name, grid, in_specs, out_specs, out_shape, scratch_shapes, operands, sem, plan=None):
    if plan is None:
        res = pl.pallas_call(body, name=name, grid=grid, in_specs=in_specs, out_specs=out_specs, out_shape=out_shape,
                             scratch_shapes=scratch_shapes, compiler_params=_params(sem))(*operands)
        return res, None
    n_in, n_out, n_scr = len(in_specs), len(out_specs), len(scratch_shapes)
    k_in, k_out = len(plan.operands), len(plan.out_shape)
    total = int(np.prod(grid))
    late = max(0, total - 1 - max(1, total // 8))

    def hosted(*refs):
        ins, refs = refs[:n_in], refs[n_in:]
        p_ins, refs = refs[:k_in], refs[k_in:]
        outs, refs = refs[:n_out], refs[n_out:]
        p_outs, refs = refs[:k_out], refs[k_out:]
        scr, p_scr = refs[:n_scr], refs[n_scr:]
        step = pl.program_id(0)
        for d in range(1, len(grid)):
            step = step * grid[d] + pl.program_id(d)
        pl.when(step == 0)(lambda: plan.start(p_ins, p_outs, p_scr))
        body(*ins, *outs, *scr)
        pl.when(step == late)(lambda: plan.middle(p_ins, p_outs, p_scr))
        pl.when(step == total - 1)(lambda: plan.finish(p_ins, p_outs, p_scr))

    res = pl.pallas_call(
        hosted, name=name, grid=grid, in_specs=list(in_specs) + [ANY] * k_in, out_specs=list(out_specs) + [ANY] * k_out,
        out_shape=list(out_shape) + plan.out_shape, scratch_shapes=list(scratch_shapes) + plan.scratch,
        compiler_params=pltpu.CompilerParams(dimension_semantics=sem, vmem_limit_bytes=VMEM_LIMIT,
                                             has_side_effects=True),
    )(*operands, *plan.operands)
    return res[:n_out], res[n_out:]


def _all_gather(name, shards):
    return _run_plan(name, _Gather(shards))


class _Swap:
    def __init__(self, arrs):
        self.n = len(arrs)
        self.operands = list(arrs)
        self.out_shape = [jax.ShapeDtypeStruct((a.shape[0], a.shape[1] // 2, a.shape[2]), a.dtype) for a in arrs]
        self.scratch = [pltpu.SemaphoreType.DMA((self.n,)) for _ in range(2)]

    def _copy(self, ins, outs, scr, a):
        x, y, c, _, _ = _place()
        hr = ins[a].shape[1] // 2
        return pltpu.make_async_remote_copy(
            src_ref=ins[a].at[:, pl.ds((1 - c) * hr, hr)], dst_ref=outs[a], send_sem=scr[0].at[a],
            recv_sem=scr[1].at[a], device_id=(x, y, 1 - c), device_id_type=MESH)

    def start(self, ins, outs, scr):
        for a in range(self.n):
            self._copy(ins, outs, scr, a).start()

    def middle(self, ins, outs, scr):
        pass

    def finish(self, ins, outs, scr):
        for a in range(self.n):
            self._copy(ins, outs, scr, a).wait()


class _Scatter:
    def __init__(self, parts):
        n = self.n = len(parts)
        self.operands = list(parts)
        self.out_shape = [jax.ShapeDtypeStruct(a.shape, a.dtype) for a in parts]
        self.scratch = [pltpu.SemaphoreType.DMA((3 * n,)) for _ in range(2)]
        for a in parts:
            self.scratch += [_bounce_shape(a.shape[1], a.shape[2], a.dtype), pltpu.SemaphoreType.DMA((4,))]

    def _copy(self, ins, outs, scr, a, j, mine):
        x, y, c, s, chips = _place()
        t = 2 * chips[j][0] + chips[j][1]
        return pltpu.make_async_remote_copy(
            src_ref=ins[a].at[t], dst_ref=outs[a].at[s if mine else t], send_sem=scr[0].at[3 * a + j],
            recv_sem=scr[1].at[3 * a + j], device_id=(*chips[j], c), device_id_type=MESH)

    def start(self, ins, outs, scr):
        for a in range(self.n):
            for j in range(3):
                self._copy(ins, outs, scr, a, j, True).start()

    def middle(self, ins, outs, scr):
        s = _place()[3]
        for a in range(self.n):
            _bounce_copy(ins[a].at[s], outs[a].at[s], scr[2 + 2 * a], scr[3 + 2 * a])

    def finish(self, ins, outs, scr):
        for a in range(self.n):
            for j in range(3):
                self._copy(ins, outs, scr, a, j, False).wait_recv()
        for a in range(self.n):
            for j in range(3):
                self._copy(ins, outs, scr, a, j, True).wait_send()


def _pair_gather(name, halves):
    n = len(halves)

    def body(*refs):
        ins, outs = refs[:n], refs[n:2 * n]
        send, recv = refs[2 * n:2 * n + 2]
        stage = refs[2 * n + 2:]
        x, y, c, _, _ = _place()
        cps = []
        for a in range(n):
            hr = ins[a].shape[0]
            cp = pltpu.make_async_remote_copy(
                src_ref=ins[a], dst_ref=outs[a].at[pl.ds(c * hr, hr)], send_sem=send.at[a], recv_sem=recv.at[a],
                device_id=(x, y, 1 - c), device_id_type=MESH)
            cp.start()
            cps.append((cp, hr))
        for a, (cp, hr) in enumerate(cps):
            _bounce_copy(ins[a], outs[a].at[pl.ds(c * hr, hr)], stage[2 * a], stage[2 * a + 1])
        for a, (cp, hr) in enumerate(cps):
            cp.wait_send()
            theirs = outs[a].at[pl.ds((1 - c) * hr, hr)]
            pltpu.make_async_remote_copy(src_ref=theirs, dst_ref=theirs, send_sem=send.at[a], recv_sem=recv.at[a],
                                         device_id=(x, y, 1 - c), device_id_type=MESH).wait_recv()

    out_shape = [jax.ShapeDtypeStruct((2 * a.shape[0], a.shape[1]), a.dtype) for a in halves]
    return _hbm_call(body, name, halves, out_shape, (n, n),
                     bounce=[(a.shape[0], a.shape[1], a.dtype) for a in halves])


def _row_tile(rows, cols, itemsize, n_bufs):
    budget = VMEM_LIMIT // 2
    for t in range(min(rows, 1024) // 16 * 16, 0, -16):
        if rows % t == 0 and 2 * n_bufs * t * cols * itemsize <= budget:
            return t
    return rows


def _pair_add(name, g, gsib, core, out_dtype):
    _, r, cols = g.shape
    hr = r // 2
    t = _row_tile(hr, cols, 4, 3)
    per = hr // t

    def body(core_ref, a_ref, b_ref, o_ref):
        o_ref[...] = (a_ref[...].astype(F32) + b_ref[...].astype(F32)).astype(o_ref.dtype)

    grid_spec = pltpu.PrefetchScalarGridSpec(
        num_scalar_prefetch=1, grid=(N_CHIPS, per),
        in_specs=[pl.BlockSpec((None, t, cols), lambda s, i, core: (s, core[0] * per + i, 0)),
                  pl.BlockSpec((None, t, cols), lambda s, i, core: (s, i, 0))],
        out_specs=pl.BlockSpec((None, t, cols), lambda s, i, core: (s, i, 0)))
    return pl.pallas_call(body, name=name, grid_spec=grid_spec,
                          out_shape=jax.ShapeDtypeStruct((N_CHIPS, hr, cols), out_dtype),
                          compiler_params=_params(("arbitrary", "arbitrary")))(core, g, gsib)


def _chip_sum(name, parts):
    _, hr, cols = parts.shape
    t = _row_tile(hr, cols, 4, 5)

    def body(p_ref, o_ref):
        o_ref[...] = ((p_ref[0].astype(F32) + p_ref[1].astype(F32)) + p_ref[2].astype(F32)) + p_ref[3].astype(F32)

    return pl.pallas_call(
        body, name=name, grid=(hr // t,), in_specs=[pl.BlockSpec((N_CHIPS, t, cols), lambda i: (0, i, 0))],
        out_specs=pl.BlockSpec((t, cols), lambda i: (i, 0)), out_shape=jax.ShapeDtypeStruct((hr, cols), F32),
        compiler_params=_params(("arbitrary",)))(parts)


def _pair_partials(tag, arrs, sib, wire_dtypes, core):
    return _Scatter([_pair_add(f"{tag}_pair_add{i}", g, gs, core, dt)
                     for i, (g, gs, dt) in enumerate(zip(arrs, sib, wire_dtypes))])


def _finish_reduce(tag, scattered):
    halves = [_chip_sum(f"{tag}_chip_sum{i}", p) for i, p in enumerate(scattered)]
    return _pair_gather(f"{tag}_pair_gather", halves)


def _adamw(name, w, g_parts, m, v):
    thin = w.ndim == 3
    rows, cols = w.shape[0], w.shape[-1]
    n_parts = len(g_parts)
    part_rows = rows // n_parts
    t = max(d for d in range(1, 257) if part_rows % d == 0) if thin else _row_tile(part_rows, cols, 4, 7 + n_parts)
    per = part_rows // t
    c1 = 1.0 - ADAM_B1 ** ADAM_STEP
    c2 = 1.0 - ADAM_B2 ** ADAM_STEP

    def body(w_ref, m_ref, v_ref, *refs):
        g_refs, (go_ref, d_ref, nm_ref, nv_ref) = refs[:n_parts], refs[n_parts:]
        g = g_refs[0][...]
        for k in range(1, n_parts):
            g = jnp.where(pl.program_id(0) >= k * per, g_refs[k][...], g)
        go_ref[...] = g
        m = ADAM_B1 * m_ref[...] + (1.0 - ADAM_B1) * g
        v = ADAM_B2 * v_ref[...] + (1.0 - ADAM_B2) * (g * g)
        nm_ref[...] = m
        nv_ref[...] = v
        d_ref[...] = -ADAM_LR * ((m / c1) / (jnp.sqrt(v / c2) + ADAM_EPS) + ADAM_WD * w_ref[...])

    block = (t, 1, cols) if thin else (t, cols)
    at = lambda r: (r, 0, 0) if thin else (r, 0)
    spec = pl.BlockSpec(block, lambda i: at(i))
    g_specs = [pl.BlockSpec(block, lambda i, k=k: at(jnp.clip(i - k * per, 0, per - 1))) for k in range(n_parts)]
    shp = jax.ShapeDtypeStruct(w.shape, F32)
    return pl.pallas_call(body, name=name, grid=(rows // t,), in_specs=[spec] * 3 + g_specs, out_specs=[spec] * 4,
                          out_shape=[shp] * 4, compiler_params=_params(("arbitrary",)))(w, m, v, *g_parts)


_WEIGHTS = ["mix_norm", "mlp_norm", "mlp_w1", "mlp_w2", "lru_w_in", "lru_conv_w", "lru_conv_b", "lru_w_r", "lru_b_r",
            "lru_w_i", "lru_b_i", "lru_lambda", "lru_w_out", "fox_w_in", "fox_b_f", "fox_q_gain", "fox_k_gain",
            "fox_w_out"]
_REPLICATED = ["mix_norm", "mlp_norm", "lru_conv_b", "lru_w_r", "lru_b_r", "lru_w_i", "lru_b_i", "lru_lambda",
               "fox_b_f", "fox_q_gain", "fox_k_gain"]
_PACK_TILE = 2 * SUBLANES * LANES


def _as2d(a):
    return a.reshape(-1, a.shape[-1])


def kernel(x, mix_norm, mlp_norm, mlp_w1, mlp_w2, lru_w_in, lru_conv_w, lru_conv_b, lru_w_r, lru_b_r, lru_w_i, lru_b_i, lru_lambda, lru_w_out, fox_w_in, fox_b_f, fox_q_gain, fox_k_gain, fox_w_out, loss_target, m_mix_norm, m_mlp_norm, m_mlp_w1, m_mlp_w2, m_lru_w_in, m_lru_conv_w, m_lru_conv_b, m_lru_w_r, m_lru_b_r, m_lru_w_i, m_lru_b_i, m_lru_lambda, m_lru_w_out, m_fox_w_in, m_fox_b_f, m_fox_q_gain, m_fox_k_gain, m_fox_w_out, v_mix_norm, v_mlp_norm, v_mlp_w1, v_mlp_w2, v_lru_w_in, v_lru_conv_w, v_lru_conv_b, v_lru_w_r, v_lru_b_r, v_lru_w_i, v_lru_b_i, v_lru_lambda, v_lru_w_out, v_fox_w_in, v_fox_b_f, v_fox_q_gain, v_fox_k_gain, v_fox_w_out):
    args = dict(locals())
    W = {n: args[n] for n in _WEIGHTS}
    Mo = {n: args["m_" + n] for n in _WEIGHTS}
    Vo = {n: args["v_" + n] for n in _WEIGHTS}
    S, D = x.shape[1], x.shape[2]
    F = 4 * D
    H = D // HEAD_DIM
    NU = 3 * D + LANES
    FQ, DQ = F // N_CHIPS, D // N_CHIPS
    nfox = fox_w_in.shape[-1]
    chip = 2 * lax.axis_index("x") + lax.axis_index("y")
    core = lax.axis_index("c").astype(jnp.int32).reshape(1)

    cw_flat = jnp.pad(lru_conv_w.reshape(-1), (0, _PACK_TILE - CONV_WIDTH * DQ)).reshape(2 * SUBLANES, LANES)
    w1s, w2s = mlp_w1.astype(BF16), mlp_w2.astype(BF16)
    wv = {}
    small = {n: W[n] for n in _REPLICATED}
    scattered = {}
    members = {"g1": ["w2_1", "w1_1", "fox_out"], "g2": ["fox_in"], "g3": ["w2_0", "w1_0"], "g4": ["lru_out", "lru_in"]}
    swap_at = {"fox_bwd_prep": "g1", "fox_dh": "g2", "lru_dout": "g3"}
    scatter_at = {"attn_backward": "g1", "mlp0_dact": "g2", "lru_bwd": "g3", "lru_dh": "g4"}
    swapped = {}

    fox_rows = -(-nfox // (4 * SUBLANES)) * (4 * SUBLANES)
    fox_t = jnp.pad(jnp.transpose(fox_w_in[0]).astype(BF16), ((0, fox_rows - nfox), (0, 0)))

    def shard_major(name, g):
        if name == "fox_in":
            return jnp.pad(g[:nfox * N_CHIPS].reshape(N_CHIPS, nfox, D), ((0, 0), (0, fox_rows - nfox), (0, 0)))
        return g

    class Comm:
        @staticmethod
        def before(name, grads):
            if name == "mix0_norm":
                return _Gather([lru_w_in[0].astype(BF16)])
            if name == "lru_in":
                return _Gather([lru_w_out[0].astype(BF16), cw_flat])
            if name == "lru_fwd":
                return _Gather([w1s[0]])
            if name == "mlp0_up":
                return _Gather([w2s[0]])
            if name == "mlp0_down":
                return _Gather([fox_t])
            if name == "attn_forward":
                return _Gather([fox_w_out[0].astype(BF16), w1s[1], w2s[1]])
            if name in swap_at:
                group = swap_at[name]
                swapped[group] = [[shard_major(n, grads[n]) for n in members[group]], None]
                return _Swap(swapped[group][0])
            if name in scatter_at:
                group = scatter_at[name]
                if group not in swapped:
                    arrs = [shard_major(n, grads[n]) for n in members[group]]
                    swapped[group] = [arrs, _run_plan(f"{group}_pair_swap", _Swap(arrs))]
                arrs, sib = swapped[group]
                return _pair_partials(group, arrs, sib, [BF16] * len(arrs), core)
            return None

        @staticmethod
        def after(name, res, wv):
            if name == "mix0_norm":
                wv.update(lru_in=_View(res[0], "cs"))
            elif name == "lru_in":
                wv.update(lru_out=_View(res[0], "rs"))
                taps = res[1].reshape(N_CHIPS, -1)[:, :CONV_WIDTH * DQ].reshape(N_CHIPS, CONV_WIDTH, DQ)
                small["conv_w"] = jnp.transpose(taps, (1, 0, 2)).reshape(CONV_WIDTH, D)
            elif name == "lru_fwd":
                wv.update(w1_0=_View(res[0], "cs"))
            elif name == "mlp0_up":
                wv.update(w2_0=_View(res[0], "rs"))
            elif name == "mlp0_down":
                fox_full = jnp.concatenate([res[0][s, :nfox] for s in range(N_CHIPS)], axis=0)
                wv.update(fox_in=_View(jnp.pad(fox_full, ((0, NU - fox_full.shape[0]), (0, 0)))))
            elif name == "attn_forward":
                wv.update(fox_out=_View(res[0], "rs"), w1_1=_View(res[1], "cs"), w2_1=_View(res[2], "rs"))
            elif name in swap_at:
                swapped[swap_at[name]][1] = res
            else:
                scattered.update(zip(members[scatter_at[name]], res))

    def grad_view(grads, name):
        if name in ("w1_0", "w1_1"):
            return _View(None, "cs", shape=(N_CHIPS, D, FQ), dtype=BF16)
        if name in ("w2_0", "w2_1"):
            return _View(None, "rs", shape=(N_CHIPS, FQ, D), dtype=BF16)
        if name == "lru_in":
            return _View(None, "cs", shape=(N_CHIPS, D, 2 * D // N_CHIPS), dtype=BF16)
        if name in ("lru_out", "fox_out"):
            return _View(None, "rs", shape=(N_CHIPS, DQ, D), dtype=BF16)
        return _View(None, shape=(NU, D), dtype=BF16)

    loss, gx, grads = _local_step(x[0], loss_target[0], small, wv, grad_view, Comm)

    pack_names = _REPLICATED + ["conv_w"]
    flat = jnp.concatenate([grads[n].reshape(-1).astype(F32) for n in pack_names] + [loss.reshape(-1)])
    per_chip = -(-flat.shape[0] // (N_CHIPS * _PACK_TILE)) * _PACK_TILE
    pack = jnp.pad(flat, (0, N_CHIPS * per_chip - flat.shape[0])).reshape(N_CHIPS, per_chip // LANES, LANES)
    pack_sib = _run_plan("pack_pair_swap", _Swap([pack]))
    (scattered["pack"],) = _run_plan("pack_chip_scatter", _pair_partials("pack", [pack], pack_sib, [F32], core))
    order = ["w1_0", "w1_1", "w2_0", "w2_1", "lru_in", "lru_out", "fox_in", "fox_out", "pack"]
    red = dict(zip(order, _finish_reduce("grads", [scattered[n] for n in order])))
    (all_pack,) = _all_gather("gather_small_grads", [red["pack"]])
    all_flat = all_pack.reshape(-1)
    G = {}
    off = 0
    for n in pack_names:
        shape = grads[n].shape if n == "conv_w" else W[n].shape
        size = int(np.prod(shape))
        G[n] = all_flat[off:off + size].reshape(shape)
        off += size
    total = all_flat[off]
    G["lru_conv_w"] = lax.dynamic_slice_in_dim(G.pop("conv_w"), chip * DQ, DQ, axis=1)[None]
    parts = {n: [_as2d(G[n])] for n in G}
    parts.update(mlp_w1=[red["w1_0"], red["w1_1"]], mlp_w2=[red["w2_0"], red["w2_1"]], lru_w_in=[red["lru_in"]],
                 lru_w_out=[red["lru_out"]], fox_w_in=[red["fox_in"][:nfox, None, :]], fox_w_out=[red["fox_out"]])

    delta, new_m, new_v = {}, {}, {}
    for n in _WEIGHTS:
        if n == "fox_w_in":
            to_thin = lambda a: jnp.transpose(a, (2, 0, 1))
            res = _adamw(f"adamw_{n}", to_thin(W[n]), parts[n], to_thin(Mo[n]), to_thin(Vo[n]))
            G[n], delta[n], new_m[n], new_v[n] = (jnp.transpose(t, (1, 2, 0)) for t in res)
            continue
        go, d, nm, nv = _adamw(f"adamw_{n}", _as2d(W[n]), parts[n], _as2d(Mo[n]), _as2d(Vo[n]))
        G[n], delta[n], new_m[n], new_v[n] = (t.reshape(W[n].shape) for t in (go, d, nm, nv))

    return (total, gx[None], *[G[n] for n in _WEIGHTS], *[delta[n] for n in _WEIGHTS],
            *[new_m[n] for n in _WEIGHTS], *[new_v[n] for n in _WEIGHTS])
```

```python
import functools

import numpy as np
import jax
import jax.numpy as jnp
from jax import lax
from jax.experimental import pallas as pl
from jax.experimental.pallas import tpu as pltpu

F32 = jnp.float32
BF16 = jnp.bfloat16

HEAD_DIM = 64
LRU_BLOCK_DIM = 64
CONV_WIDTH = 4
LRU_C = 8.0
EPS = 1e-6
NEG_INF = -1e30
ADAM_LR = 0.001
ADAM_B1 = 0.9
ADAM_B2 = 0.999
ADAM_EPS = 1e-08
ADAM_WD = 0.01
ADAM_STEP = 10

N_CHIPS = 4
LANES = 128
SUBLANES = 8
MXU_DIM = 256
VMEM_LIMIT = 52 * 1024 * 1024
MATMUL_TILES = (1024, 640, 512, 256, 128)
MATMUL_VMEM = VMEM_LIMIT * 4 // 5
MESH = pl.DeviceIdType.MESH
ANY = pl.BlockSpec(memory_space=pl.ANY)


def _pick(n, prefs):
    for p in prefs:
        if p <= n and n % p == 0:
            return p
    return n


def _params(sem=None):
    return pltpu.CompilerParams(dimension_semantics=sem, vmem_limit_bytes=VMEM_LIMIT)


class _View:
    def __init__(self, arr, kind="plain", shape=None, dtype=None):
        self.arr = arr
        self.kind = kind
        self.shape = tuple(arr.shape) if arr is not None else tuple(shape)
        self.dtype = arr.dtype if arr is not None else dtype

    def limits(self):
        if self.kind == "plain":
            return 0, 0
        return self.shape[-2], (self.shape[-1] if self.kind == "cs" else 0)

    def spec(self, br, bc, fr, fc):
        if self.kind == "plain":
            return pl.BlockSpec((br, bc), lambda *g: (fr(*g), fc(*g)))
        rows, ncol = self.shape[-2:]
        assert rows % br == 0 and ncol % bc == 0, (self.shape, br, bc)
        if self.kind == "cs":
            per = ncol // bc
            return pl.BlockSpec((None, br, bc), lambda *g: (fc(*g) // per, fr(*g), fc(*g) % per))
        per = rows // br
        return pl.BlockSpec((None, br, bc), lambda *g: (fr(*g) // per, fr(*g) % per, fc(*g)))


def _bf(x):
    return x if x.dtype == BF16 else x.astype(BF16)


def _matmul(name, A, B, M, N, K, *, ta=False, tb=False, outs, epilogue, extras=(), vecs=(), n_sums=0,
            tm=None, tn=None, tk=None, plan=None):
    lim = {"m": [M], "n": [N], "k": [K]}
    for view, (rdim, cdim) in ([(A, "km" if ta else "mk"), (B, "nk" if tb else "kn")]
                               + [(e, "mn") for e in extras] + [(o, "mn") for o in outs]):
        r_lim, c_lim = view.limits()
        lim[rdim].append(r_lim)
        lim[cdim].append(c_lim)
    cap = {d: int(np.gcd.reduce(lim[d])) for d in "mnk"}
    tm = tm or _pick(cap["m"], MATMUL_TILES)
    tn = tn or _pick(cap["n"], MATMUL_TILES)
    tk = tk or _pick(cap["k"], MATMUL_TILES)

    def vmem_bytes(tm, tk):
        size = lambda v: jnp.dtype(v.dtype).itemsize
        tiles = tm * tk * size(A) + tk * tn * size(B) + tm * tn * sum(size(v) for v in list(extras) + list(outs))
        return 2 * tiles + (tm * tn * 4 if K > tk else 0)

    if cap["k"] % (2 * tk) == 0 and vmem_bytes(tm, 2 * tk) <= MATMUL_VMEM:
        tk *= 2
    elif K == tk and cap["m"] % (2 * tm) == 0 and vmem_bytes(2 * tm, tk) <= MATMUL_VMEM:
        tm *= 2
    nk = K // tk
    gi = lambda i, j, k: i
    gj = lambda i, j, k: j
    gk = lambda i, j, k: k
    a_spec = A.spec(tk, tm, gk, gi) if ta else A.spec(tm, tk, gi, gk)
    b_spec = B.spec(tn, tk, gj, gk) if tb else B.spec(tk, tn, gk, gj)
    ca = 0 if ta else 1
    cb = 1 if tb else 0
    ne, no = len(extras) + len(vecs), len(outs)
    assert n_sums == 0 or tn == N
    row_spec = pl.BlockSpec((1, tn), lambda i, j, k: (0, j))
    in_specs = [a_spec, b_spec] + [e.spec(tm, tn, gi, gj) for e in extras] + [row_spec] * len(vecs)
    operands = [A.arr, B.arr] + [e.arr for e in extras] + list(vecs)
    out_specs = [o.spec(tm, tn, gi, gj) for o in outs] + [row_spec] * n_sums
    out_shape = ([jax.ShapeDtypeStruct(o.shape, o.dtype) for o in outs]
                 + [jax.ShapeDtypeStruct((1, N), F32)] * n_sums)

    def body(*refs):
        a_ref, b_ref = refs[0], refs[1]
        ex = refs[2:2 + ne]
        o_refs = refs[2 + ne:2 + ne + no]
        s_refs = refs[2 + ne + no:2 + ne + no + n_sums]
        first_row_tile = pl.program_id(0) == 0

        def prod():
            return lax.dot_general(_bf(a_ref[...]), _bf(b_ref[...]), (((ca,), (cb,)), ((), ())),
                                   preferred_element_type=F32)

        def finish(acc):
            res = epilogue(acc, *[e[...] for e in ex])
            for o_ref, r in zip(o_refs, res[:no]):
                o_ref[...] = r.astype(o_ref.dtype)
            for s_ref, r in zip(s_refs, res[no:]):
                def assign(s_ref=s_ref, r=r):
                    s_ref[...] = r

                def accumulate(s_ref=s_ref, r=r):
                    s_ref[...] += r

                pl.when(first_row_tile)(assign)
                pl.when(jnp.logical_not(first_row_tile))(accumulate)

        if nk == 1:
            finish(prod())
        else:
            acc_ref = refs[-1]
            k = pl.program_id(2)

            @pl.when(k == 0)
            def _():
                acc_ref[...] = jnp.zeros_like(acc_ref)

            acc_ref[...] += prod()

            @pl.when(k == nk - 1)
            def _():
                finish(acc_ref[...])

    res, side = _hosted_call(body, name, (M // tm, N // tn, nk), in_specs, out_specs, out_shape,
                             [pltpu.VMEM((tm, tn), F32)] if nk > 1 else [], operands,
                             ("arbitrary", "arbitrary", "arbitrary"), plan)
    return res if plan is None else (res, side)


def _ep_store(acc):
    return (acc,)


def _ep_resid(acc, res):
    return (res + acc,)


def _ep_resid_norm(acc, res, g):
    xo = res + acc
    r = lax.rsqrt(jnp.mean(xo * xo, axis=-1, keepdims=True) + EPS)
    return (xo, (xo * r) * g)


def _ep_norm_bwd(acc, x, dres, g):
    r = lax.rsqrt(jnp.mean(x * x, axis=-1, keepdims=True) + EPS)
    xhat = x * r
    dxn = acc * g
    tot = dres + r * (dxn - xhat * jnp.mean(dxn * xhat, axis=-1, keepdims=True))
    return (tot, tot, jnp.sum(acc * xhat, axis=0, keepdims=True))


def _ep_relu2(acc):
    zp = jnp.maximum(acc, 0.0)
    return (zp * zp,)


def _ep_drelu2(acc, act):
    return (acc * (2.0 * jnp.sqrt(act.astype(F32))),)


def _fresh(M, N, dtype):
    return _View(None, shape=(M, N), dtype=dtype)


def _rms_fwd(name, x, g, S, D, plan=None):
    T = _pick(S, (512, 256, 128))

    def body(x_ref, g_ref, h_ref):
        x = x_ref[...]
        r = lax.rsqrt(jnp.mean(x * x, axis=-1, keepdims=True) + EPS)
        h_ref[...] = ((x * r) * g_ref[...]).astype(BF16)

    return _hosted_call(body, name, (S // T,),
                        [pl.BlockSpec((T, D), lambda i: (i, 0)), pl.BlockSpec((1, D), lambda i: (0, 0))],
                        [pl.BlockSpec((T, D), lambda i: (i, 0))], [jax.ShapeDtypeStruct((S, D), BF16)], [], (x, g),
                        ("arbitrary",), plan)


def _loss_head(x, tgt, S, D):
    T = _pick(S, (512, 256, 128))

    def body(x_ref, t_ref, loss_ref, d_ref, db_ref):
        @pl.when(pl.program_id(0) == 0)
        def _():
            loss_ref[...] = jnp.zeros_like(loss_ref)

        e = x_ref[...] - t_ref[...]
        loss_ref[...] += 0.5 * jnp.sum(jnp.mean(e * e, axis=-1, keepdims=True), axis=0, keepdims=True)
        d = e * (1.0 / D)
        d_ref[...] = d
        db_ref[...] = d.astype(BF16)

    row = pl.BlockSpec((T, D), lambda i: (i, 0))
    return pl.pallas_call(
        body, name="loss_head", grid=(S // T,), in_specs=[row, row],
        out_specs=[pl.BlockSpec((1, 1), lambda i: (0, 0)), row, row],
        out_shape=[jax.ShapeDtypeStruct((1, 1), F32), jax.ShapeDtypeStruct((S, D), F32),
                   jax.ShapeDtypeStruct((S, D), BF16)],
        compiler_params=_params(("arbitrary",)),
    )(x, tgt)


def _sigmoid(z):
    return 1.0 / (1.0 + jnp.exp(-z))


def _log_sigmoid(z):
    return jnp.minimum(z, 0.0) - jnp.log(1.0 + jnp.exp(-jnp.abs(z)))


_GELU_K = 0.7978845608028654
_GELU_C = 0.044715


def _gelu(x):
    t = jnp.tanh(_GELU_K * (x + _GELU_C * (x * x * x)))
    return 0.5 * x * (1.0 + t)


def _gelu_and_grad(x):
    x2 = x * x
    t = jnp.tanh(_GELU_K * (x + _GELU_C * (x2 * x)))
    g = 0.5 * x * (1.0 + t)
    dg = 0.5 * (1.0 + t) + 0.5 * x * (1.0 - t * t) * (_GELU_K * (1.0 + 3.0 * _GELU_C * x2))
    return g, dg


def _decay_terms(r, ls):
    la = LRU_C * r * ls
    a = jnp.exp(la)
    a2 = a * a
    mult = jnp.sqrt(-jnp.tanh(la) * (a2 + 1.0))
    return a, a2, mult


def _lru_fwd(u0, conv_w, conv_b, wr_bd, b_r, wi_bd, b_i, lam, S, D, plan=None):
    T = _pick(S, (256, 128))
    GT = wr_bd.shape[-1]
    nG = D // GT

    def body(gb_ref, xb_ref, cw_ref, cb_ref, wr_ref, br_ref, wi_ref, bi_ref, lam_ref,
             y_ref, xc_ref, r_ref, i_ref, hs_ref, ext, a_scr, hcar):
        @pl.when(pl.program_id(0) == 0)
        def _():
            ext[0:SUBLANES, :] = jnp.zeros((SUBLANES, D), F32)
            hcar[...] = jnp.zeros_like(hcar)

        xb = xb_ref[...]
        ext[SUBLANES:SUBLANES + T, :] = xb
        xc = cb_ref[...]
        for k in range(CONV_WIDTH):
            xc = xc + ext[pl.ds(SUBLANES - (CONV_WIDTH - 1) + k, T), :] * cw_ref[k:k + 1, :]
        ext[0:SUBLANES, :] = xb[T - SUBLANES:T, :]
        xc_ref[...] = xc
        xcb = xc.astype(BF16)
        for g in range(nG):
            sl = slice(g * GT, (g + 1) * GT)
            zr = jnp.dot(xcb[:, sl], wr_ref[g], preferred_element_type=F32) + br_ref[:, sl]
            zi = jnp.dot(xcb[:, sl], wi_ref[g], preferred_element_type=F32) + bi_ref[:, sl]
            r_ref[:, sl] = _sigmoid(zr)
            i_ref[:, sl] = _sigmoid(zi)
        r = r_ref[...]
        a, _, mult = _decay_terms(r, _log_sigmoid(lam_ref[...]))
        a_scr[...] = a
        hs_ref[...] = mult * (i_ref[...] * xc)

        def step(t, h):
            h = a_scr[pl.ds(t, 1), :] * h + hs_ref[pl.ds(t, 1), :]
            hs_ref[pl.ds(t, 1), :] = h
            return h

        hcar[...] = lax.fori_loop(0, T, step, hcar[...], unroll=8)
        y_ref[...] = (_gelu(gb_ref[...]) * hs_ref[...]).astype(BF16)

    row = pl.BlockSpec((T, D), lambda i: (i, 0))
    vec = pl.BlockSpec((1, D), lambda i: (0, 0))
    bd = pl.BlockSpec((nG, GT, GT), lambda i: (0, 0, 0))
    f32o = jax.ShapeDtypeStruct((S, D), F32)
    return _hosted_call(
        body, "lru_fwd", (S // T,),
        [row, pl.BlockSpec((T, D), lambda i: (i, 1)), pl.BlockSpec((CONV_WIDTH, D), lambda i: (0, 0)), vec,
         bd, vec, bd, vec, vec],
        [row, row, row, row, row], [jax.ShapeDtypeStruct((S, D), BF16), f32o, f32o, f32o, f32o],
        [pltpu.VMEM((T + SUBLANES, D), F32), pltpu.VMEM((T, D), F32), pltpu.VMEM((1, D), F32)],
        (u0, u0, conv_w, conv_b, wr_bd, b_r, wi_bd, b_i, lam), ("arbitrary",), plan)


def _lru_bwd(dy, u0, xc, r, ig, hs, conv_w, wr_bd, wi_bd, lam, S, D, plan=None):
    T = _pick(S, (128,))
    nT = S // T
    GT = wr_bd.shape[-1]
    nG = D // GT
    W = CONV_WIDTH

    def body(dy_ref, gb_ref, xb_ref, xbp_ref, xc_ref, r_ref, i_ref, hs_ref, hsp_ref, cw_ref, wr_ref, wi_ref, lam_ref,
             du_ref, dcw_ref, dcb_ref, dlam_ref, dbr_ref, dbi_ref, dwr_ref, dwi_ref,
             a_scr, dh_scr, exth, extx, extd, dxc_scr, dz_scr, carry):
        step = pl.program_id(0)
        first_tile = step == nT - 1

        @pl.when(step == 0)
        def _():
            for ref in (dcw_ref, dcb_ref, dlam_ref, dbr_ref, dbi_ref, dwr_ref, dwi_ref, carry):
                ref[...] = jnp.zeros_like(ref)
            extd[T:T + SUBLANES, :] = jnp.zeros((SUBLANES, D), F32)

        hs = hs_ref[...]
        dy = dy_ref[...]
        g, dgelu = _gelu_and_grad(gb_ref[...])
        du_ref[:, 0:D] = (dy * hs * dgelu).astype(BF16)
        r = r_ref[...]
        lam = lam_ref[...]
        ls = _log_sigmoid(lam)
        a, a2, mult = _decay_terms(r, ls)
        a_scr[...] = a
        dh_scr[...] = dy * g

        def rstep(j, c):
            t = T - 1 - j
            d = dh_scr[pl.ds(t, 1), :] + c
            dh_scr[pl.ds(t, 1), :] = d
            return a_scr[pl.ds(t, 1), :] * d

        carry[...] = lax.fori_loop(0, T, rstep, carry[...], unroll=8)
        dh = dh_scr[...]
        keep = jnp.where(first_tile, 0.0, 1.0)
        exth[0:SUBLANES, :] = hsp_ref[...] * keep
        exth[SUBLANES:SUBLANES + T, :] = hs
        hprev = exth[pl.ds(SUBLANES - 1, T), :]
        xc = xc_ref[...]
        ig = i_ref[...]
        da = dh * hprev
        dmult = dh * (ig * xc)
        dla = da * a - dmult * (a2 / mult)
        dlam_ref[...] += jnp.sum(dla * r, axis=0, keepdims=True) * (LRU_C * _sigmoid(-lam))
        dzr = (dla * (LRU_C * ls)) * (r * (1.0 - r))
        dzi = (dh * (mult * xc)) * (ig * (1.0 - ig))
        dbr_ref[...] += jnp.sum(dzr, axis=0, keepdims=True)
        dbi_ref[...] += jnp.sum(dzi, axis=0, keepdims=True)
        dxc_scr[...] = dh * (mult * ig)
        xcb = xc.astype(BF16)
        dz_scr[0] = dzr.astype(BF16)
        dz_scr[1] = dzi.astype(BF16)
        nt_dims = (((1,), (1,)), ((), ()))
        tn_dims = (((0,), (0,)), ((), ()))
        for gq in range(nG):
            sl = slice(gq * GT, (gq + 1) * GT)
            zr_g = dz_scr[0, :, sl]
            zi_g = dz_scr[1, :, sl]
            dxc_scr[:, sl] += (lax.dot_general(zr_g, wr_ref[gq], nt_dims, preferred_element_type=F32)
                               + lax.dot_general(zi_g, wi_ref[gq], nt_dims, preferred_element_type=F32))
            dwr_ref[gq] += lax.dot_general(xcb[:, sl], zr_g, tn_dims, preferred_element_type=F32)
            dwi_ref[gq] += lax.dot_general(xcb[:, sl], zi_g, tn_dims, preferred_element_type=F32)
        dxc = dxc_scr[...]
        dcb_ref[...] += jnp.sum(dxc, axis=0, keepdims=True)
        extx[0:SUBLANES, :] = xbp_ref[...] * keep
        extx[SUBLANES:SUBLANES + T, :] = xb_ref[...]
        extd[0:T, :] = dxc
        dxb = jnp.zeros((T, D), F32)
        for k in range(W):
            dxb = dxb + extd[pl.ds(W - 1 - k, T), :] * cw_ref[k:k + 1, :]
            dcw_ref[k:k + 1, :] += jnp.sum(dxc * extx[pl.ds(SUBLANES - (W - 1) + k, T), :], axis=0, keepdims=True)
        extd[T:T + SUBLANES, :] = dxc[0:SUBLANES, :]
        du_ref[:, D:2 * D] = dxb.astype(BF16)

    rev = lambda i: nT - 1 - i
    tpb = T // SUBLANES
    prev8 = lambda i: jnp.maximum(rev(i) * tpb - 1, 0)
    row = pl.BlockSpec((T, D), lambda i: (rev(i), 0))
    vec = pl.BlockSpec((1, D), lambda i: (0, 0))
    bd = pl.BlockSpec((nG, GT, GT), lambda i: (0, 0, 0))
    vec_o = jax.ShapeDtypeStruct((1, D), F32)
    bd_o = jax.ShapeDtypeStruct((nG, GT, GT), F32)
    return _hosted_call(
        body, "lru_bwd", (nT,),
        [row, row, pl.BlockSpec((T, D), lambda i: (rev(i), 1)), pl.BlockSpec((SUBLANES, D), lambda i: (prev8(i), 1)),
         row, row, row, row, pl.BlockSpec((SUBLANES, D), lambda i: (prev8(i), 0)),
         pl.BlockSpec((W, D), lambda i: (0, 0)), bd, bd, vec],
        [pl.BlockSpec((T, 2 * D), lambda i: (rev(i), 0)), pl.BlockSpec((W, D), lambda i: (0, 0)),
         vec, vec, vec, vec, bd, bd],
        [jax.ShapeDtypeStruct((S, 2 * D), BF16), jax.ShapeDtypeStruct((W, D), F32), vec_o, vec_o, vec_o, vec_o, bd_o, bd_o],
        [pltpu.VMEM((T, D), F32), pltpu.VMEM((T, D), F32), pltpu.VMEM((T + SUBLANES, D), F32),
         pltpu.VMEM((T + SUBLANES, D), F32), pltpu.VMEM((T + SUBLANES, D), F32),
         pltpu.VMEM((T, D), F32), pltpu.VMEM((2, T, D), BF16), pltpu.VMEM((1, D), F32)],
        (dy, u0, u0, u0, xc, r, ig, hs, hs, conv_w, wr_bd, wi_bd, lam), ("arbitrary",), plan)


AUG_ROWS = 16
HEAD_ROWS = 128
LSE_ROW = HEAD_DIM + 6
ONES_ROW_Q = HEAD_DIM + 3
ONES_COL_K = HEAD_DIM
ONES_ROW_V = HEAD_DIM
PREP_LANES = 512
HEAD_UNROLL = 4


def _split3(x):
    b1 = x.astype(BF16).astype(F32)
    r = x - b1
    b2 = r.astype(BF16).astype(F32)
    return b1, b2, r - b2


def _head_block(x, aug, T):
    row = lax.broadcasted_iota(jnp.int32, (AUG_ROWS, T), 0)
    blk = jnp.zeros((AUG_ROWS, T), F32)
    for i, e in enumerate(aug):
        blk = jnp.where(row == i, e, blk)
    return jnp.concatenate([x, blk, jnp.zeros((HEAD_ROWS - HEAD_DIM - AUG_ROWS, T), F32)], axis=0)


def _tri_matrix(lower):
    i = np.arange(LANES)
    m = (i[:, None] >= i[None, :]) if lower else (i[:, None] <= i[None, :])
    return jnp.asarray(m.astype(np.float32), BF16)


def _lane_cumsum(x, tri_ref, carry, reverse):
    n = x.shape[1] // LANES
    tri = tri_ref[...]
    out = [None] * n
    for j in (range(n - 1, -1, -1) if reverse else range(n)):
        cs = carry
        for part in _split3(x[:, j * LANES:(j + 1) * LANES]):
            cs = cs + jnp.dot(part.astype(BF16), tri, preferred_element_type=F32)
        out[j] = cs
        carry = cs[:, 0:1] if reverse else cs[:, LANES - 1:LANES]
    return jnp.concatenate(out, axis=1), carry


def _head_rows(h):
    return pl.ds(pl.multiple_of(h * HEAD_DIM, HEAD_DIM), HEAD_DIM)


def _fox_prep(ut, b_f, qg, kg, S, D, tq):
    H = D // HEAD_DIM
    T = min(tq, PREP_LANES)
    per = tq // T
    scale = HEAD_DIM ** -0.5

    def body(q_ref, k_ref, v_ref, f_ref, bf_ref, qg_ref, kg_ref, tri_ref,
             qat_ref, kat_ref, vat_ref, ka_ref, c_scr, ccar):
        @pl.when(pl.program_id(0) == 0)
        def _():
            ccar[...] = jnp.zeros_like(ccar)

        c, carry = _lane_cumsum(_log_sigmoid(f_ref[...] + bf_ref[...]), tri_ref, ccar[...], False)
        c_scr[...] = c
        ccar[...] = carry

        def head(h, _):
            rows = _head_rows(h)
            c1, c2, c3 = _split3(c_scr[pl.ds(h, 1), :])

            def normed(src, gain, mul):
                x = src[rows, :]
                rs = lax.rsqrt(jnp.mean(x * x, axis=0, keepdims=True) + EPS)
                return ((x * rs) * gain[rows, :]) * mul

            qat_ref[h] = _head_block(normed(q_ref, qg_ref, scale), [c1, c2, c3, 1.0, 1.0, 1.0], T).astype(BF16)
            kb = _head_block(normed(k_ref, kg_ref, 1.0), [1.0, 1.0, 1.0, -c1, -c2, -c3, 1.0, 1.0, 1.0], T)
            kat_ref[h] = kb.astype(BF16)
            ka_ref[h] = kb.T.astype(BF16)
            vat_ref[h] = _head_block(v_ref[rows, :], [1.0, 1.0, 1.0], T).astype(BF16)
            return 0

        lax.fori_loop(0, H, head, 0, unroll=min(HEAD_UNROLL, H))

    part = lambda j: pl.BlockSpec((D, T), lambda i: (j, i))
    colv = lambda n: pl.BlockSpec((n, 1), lambda i: (0, 0))
    tmaj = lambda r: pl.BlockSpec((H, None, r, T), lambda i: (0, i // per, 0, i % per))
    norm = pl.BlockSpec((H, T, HEAD_ROWS), lambda i: (0, i, 0))
    tshape = lambda r: jax.ShapeDtypeStruct((H, S // tq, r, tq), BF16)
    nshape = jax.ShapeDtypeStruct((H, S, HEAD_ROWS), BF16)
    return pl.pallas_call(
        body, name="fox_prep", grid=(S // T,),
        in_specs=[part(0), part(1), part(2), pl.BlockSpec((LANES, T), lambda i: (3 * D // LANES, i)),
                  colv(LANES), colv(D), colv(D), pl.BlockSpec((LANES, LANES), lambda i: (0, 0))],
        out_specs=[tmaj(HEAD_ROWS), tmaj(HEAD_ROWS), tmaj(HEAD_ROWS), norm],
        out_shape=[tshape(HEAD_ROWS), tshape(HEAD_ROWS), tshape(HEAD_ROWS), nshape],
        scratch_shapes=[pltpu.VMEM((LANES, T), F32), pltpu.VMEM((LANES, 1), F32)],
        compiler_params=_params(("arbitrary",)),
    )(ut, ut, ut, ut, b_f, qg, kg, _tri_matrix(False))


def _fox_bwd_prep(dot, ot, lse, qat, S, D, tq, plan=None):
    H = D // HEAD_DIM
    T = min(tq, PREP_LANES)
    per = tq // T

    def body(do_ref, o_ref, lse_ref, qat_ref, doat_ref, doa_ref, qat1_ref, qa1_ref):
        row = lax.broadcasted_iota(jnp.int32, (HEAD_ROWS, T), 0)

        def head(h, _):
            rows = _head_rows(h)
            do = do_ref[rows, :].astype(F32)
            delta = jnp.sum(do * o_ref[rows, :], axis=0, keepdims=True)
            db = _head_block(do, list(_split3(-delta)), T)
            doat_ref[h] = db.astype(BF16)
            doa_ref[h] = db.T.astype(BF16)
            qb = qat_ref[h].astype(F32)
            for i, e in enumerate(_split3(-lse_ref[h])):
                qb = jnp.where(row == LSE_ROW + i, e, qb)
            qat1_ref[h] = qb.astype(BF16)
            qa1_ref[h] = qb.T.astype(BF16)
            return 0

        lax.fori_loop(0, H, head, 0, unroll=min(HEAD_UNROLL, H))

    chan = pl.BlockSpec((D, T), lambda i: (0, i))
    tmaj = pl.BlockSpec((H, None, HEAD_ROWS, T), lambda i: (0, i // per, 0, i % per))
    norm = pl.BlockSpec((H, T, HEAD_ROWS), lambda i: (0, i, 0))
    tshape = jax.ShapeDtypeStruct((H, S // tq, HEAD_ROWS, tq), BF16)
    nshape = jax.ShapeDtypeStruct((H, S, HEAD_ROWS), BF16)
    return _hosted_call(body, "fox_bwd_prep", (S // T,), [chan, chan, pl.BlockSpec((H, 1, T), lambda i: (0, 0, i)), tmaj],
                        [tmaj, norm, tmaj, norm], [tshape, nshape, tshape, nshape], [], (dot, ot, lse, qat),
                        ("arbitrary",), plan)


def _causal(s, k_axis):
    t = min(s.shape)
    ki = lax.broadcasted_iota(jnp.int32, s.shape, k_axis) - (s.shape[k_axis] - t)
    qi = lax.broadcasted_iota(jnp.int32, s.shape, 1 - k_axis)
    return jnp.where(ki <= qi, s, NEG_INF)


def _attn_forward(ka, qat, vat, S, D, tq, plan=None):
    H = D // HEAD_DIM
    nq = S // tq
    G = 4

    def body(ka_ref, qat_ref, vat_ref, o_ref, o32_ref, lse_ref, m_scr, acc_scr):
        qi = pl.program_id(1)
        m_scr[...] = jnp.full_like(m_scr, NEG_INF)
        acc_scr[...] = jnp.zeros_like(acc_scr)

        def span(k0, n, diagonal):
            keys = pl.ds(pl.multiple_of(k0 * tq, tq), n * tq)
            s = [jnp.dot(ka_ref[g, keys, :], qat_ref[g], preferred_element_type=F32) for g in range(G)]
            if diagonal:
                s = [_causal(sg, 0) for sg in s]
            m_prev = [m_scr[g] for g in range(G)]
            m_new = [jnp.maximum(m_prev[g], jnp.max(s[g], axis=0, keepdims=True)) for g in range(G)]
            p = [jnp.exp(s[g] - m_new[g]).astype(BF16) for g in range(G)]
            for g in range(G):
                upd = jnp.dot(vat_ref[g, k0], p[g][0:tq], preferred_element_type=F32)
                for i in range(1, n):
                    upd = upd + jnp.dot(vat_ref[g, k0 + i], p[g][i * tq:(i + 1) * tq], preferred_element_type=F32)
                acc_scr[g] = jnp.exp(m_prev[g] - m_new[g]) * acc_scr[g] + upd
                m_scr[g] = m_new[g]

        def off_diagonal_pair(j, _):
            span(2 * j, 2, False)
            return 0

        lax.fori_loop(0, qi // 2, off_diagonal_pair, 0)
        pl.when(qi % 2 == 1)(lambda: span(qi - 1, 2, True))
        pl.when(qi % 2 == 0)(lambda: span(qi, 1, True))
        for g in range(G):
            l = acc_scr[g, ONES_ROW_V:ONES_ROW_V + 1, :]
            o = acc_scr[g, 0:HEAD_DIM, :] / l
            o_ref[g * HEAD_DIM:(g + 1) * HEAD_DIM, :] = o.astype(BF16)
            o32_ref[g * HEAD_DIM:(g + 1) * HEAD_DIM, :] = o
            lse_ref[g] = m_scr[g] + jnp.log(l)

    chan = pl.BlockSpec((G * HEAD_DIM, tq), lambda h, i: (h, i))
    stat = pl.BlockSpec((G, 1, tq), lambda h, i: (h, 0, i))
    return _hosted_call(
        body, "attn_forward", (H // G, nq),
        [pl.BlockSpec((G, S, HEAD_ROWS), lambda h, i: (h, 0, 0)),
         pl.BlockSpec((G, None, HEAD_ROWS, tq), lambda h, i: (h, i, 0, 0)),
         pl.BlockSpec((G, nq, HEAD_ROWS, tq), lambda h, i: (h, 0, 0, 0))],
        [chan, chan, stat],
        [jax.ShapeDtypeStruct((D, S), BF16), jax.ShapeDtypeStruct((D, S), F32), jax.ShapeDtypeStruct((H, 1, S), F32)],
        [pltpu.VMEM((G, 1, tq), F32), pltpu.VMEM((G, HEAD_ROWS, tq), F32)],
        (ka, qat, vat), ("arbitrary", "arbitrary"), plan)


def _attn_backward(qa, doa, qat, doat, ka, kat, vat, S, D, tq, plan=None):
    H = D // HEAD_DIM
    nq = S // tq
    G = 2

    def body(qa_ref, doa_ref, qat_ref, doat_ref, ka_ref, kat_ref, vat_ref, dq_ref, dk_ref, dv_ref, dk_scr, dv_scr):
        ki = pl.program_id(1)

        @pl.when(ki == 0)
        def _():
            dq_ref[...] = jnp.zeros_like(dq_ref)

        dk_scr[...] = jnp.zeros_like(dk_scr)
        dv_scr[...] = jnp.zeros_like(dv_scr)

        def span(q0, n, diagonal):
            rows = pl.ds(pl.multiple_of(q0 * tq, tq), n * tq)
            s = [jnp.dot(qa_ref[g, rows, :], kat_ref[g], preferred_element_type=F32) for g in range(G)]
            if diagonal:
                s = [_causal(sg, 1) for sg in s]
            p = [jnp.exp(sg) for sg in s]
            ds = [(p[g] * jnp.dot(doa_ref[g, rows, :], vat_ref[g], preferred_element_type=F32)).astype(BF16)
                  for g in range(G)]
            p = [pg.astype(BF16) for pg in p]
            for g in range(G):
                for i in range(n):
                    part = slice(i * tq, (i + 1) * tq)
                    dv_scr[g] += jnp.dot(doat_ref[g, q0 + i, 0:HEAD_DIM, :], p[g][part], preferred_element_type=F32)
                    dk_scr[g] += jnp.dot(qat_ref[g, q0 + i], ds[g][part], preferred_element_type=F32)
                dq_ref[g, rows, :] += jnp.dot(ds[g], ka_ref[g], preferred_element_type=F32)

        n_off = nq - 1 - ki
        odd = n_off % 2

        def off_diagonal_pair(j, _):
            span(ki + 1 + odd + 2 * j, 2, False)
            return 0

        pl.when(odd == 1)(lambda: span(ki, 2, True))
        pl.when(odd == 0)(lambda: span(ki, 1, True))
        lax.fori_loop(0, n_off // 2, off_diagonal_pair, 0)
        dk_ref[...] = dk_scr[...]
        for g in range(G):
            dv_ref[g * HEAD_DIM:(g + 1) * HEAD_DIM, :] = dv_scr[g].astype(BF16)

    whole = pl.BlockSpec((G, S, HEAD_ROWS), lambda h, i: (h, 0, 0))
    tiles = pl.BlockSpec((G, nq, HEAD_ROWS, tq), lambda h, i: (h, 0, 0, 0))
    one = pl.BlockSpec((G, None, HEAD_ROWS, tq), lambda h, i: (h, i, 0, 0))
    return _hosted_call(
        body, "attn_backward", (H // G, nq),
        [whole, whole, tiles, tiles, pl.BlockSpec((G, tq, HEAD_ROWS), lambda h, i: (h, i, 0)), one, one],
        [whole, pl.BlockSpec((G, HEAD_ROWS, tq), lambda h, i: (h, 0, i)),
         pl.BlockSpec((G * HEAD_DIM, tq), lambda h, i: (h, i))],
        [jax.ShapeDtypeStruct((H, S, HEAD_ROWS), F32), jax.ShapeDtypeStruct((H, HEAD_ROWS, S), F32),
         jax.ShapeDtypeStruct((D, S), BF16)],
        [pltpu.VMEM((G, HEAD_ROWS, tq), F32), pltpu.VMEM((G, HEAD_DIM, tq), F32)],
        (qa, doa, qat, doat, ka, kat, vat), ("arbitrary", "arbitrary"), plan)


def _fox_prep_bwd(ut, dq, dkt, dvt, b_f, qg, kg, S, D, tq):
    H = D // HEAD_DIM
    T = min(tq, PREP_LANES)
    nT = S // T
    NU = 3 * D + LANES
    scale = HEAD_DIM ** -0.5

    def body(q_ref, k_ref, f_ref, dq_ref, dk_ref, dv_ref, bf_ref, qg_ref, kg_ref, tri_ref,
             du_ref, dbf_ref, dqg_ref, dkg_ref, gq_acc, gk_acc, fcar, dc_scr):
        step = pl.program_id(0)

        @pl.when(step == 0)
        def _():
            for ref in (gq_acc, gk_acc, fcar, dbf_ref):
                ref[...] = jnp.zeros_like(ref)

        dc_scr[...] = jnp.zeros_like(dc_scr)

        def head(h, _):
            rows = _head_rows(h)
            dqb = dq_ref[h].T
            dkb = dk_ref[h]
            dc_scr[pl.ds(h, 1), :] = dqb[ONES_COL_K:ONES_COL_K + 1, :] - dkb[ONES_ROW_Q:ONES_ROW_Q + 1, :]
            for src, dsrc, gain, acc, mul, base in ((q_ref, dqb, qg_ref, gq_acc, scale, 0),
                                                    (k_ref, dkb, kg_ref, gk_acc, 1.0, D)):
                x = src[rows, :]
                rs = lax.rsqrt(jnp.mean(x * x, axis=0, keepdims=True) + EPS)
                xhat = x * rs
                dn = dsrc[0:HEAD_DIM, :] * mul
                acc[rows, :] += jnp.sum(dn * xhat, axis=1, keepdims=True)
                dxh = dn * gain[rows, :]
                dx = rs * (dxh - xhat * jnp.mean(dxh * xhat, axis=0, keepdims=True))
                du_ref[pl.ds(pl.multiple_of(base + h * HEAD_DIM, HEAD_DIM), HEAD_DIM), :] = dx.astype(BF16)
            return 0

        lax.fori_loop(0, H, head, 0, unroll=min(HEAD_UNROLL, H))
        du_ref[2 * D:3 * D, :] = dv_ref[...]
        dlf, carry = _lane_cumsum(dc_scr[...], tri_ref, fcar[...], True)
        fcar[...] = carry
        dfl = dlf * _sigmoid(-(f_ref[...] + bf_ref[...]))
        dbf_ref[...] += jnp.sum(dfl, axis=1, keepdims=True)
        du_ref[3 * D:NU, :] = dfl.astype(BF16)

        @pl.when(step == nT - 1)
        def _():
            for acc, ref in ((gq_acc, dqg_ref), (gk_acc, dkg_ref)):
                tot = jnp.zeros((HEAD_DIM, 1), F32)
                for h in range(H):
                    tot = tot + acc[h * HEAD_DIM:(h + 1) * HEAD_DIM, :]
                ref[...] = tot

    rev = lambda i: nT - 1 - i
    part = lambda j: pl.BlockSpec((D, T), lambda i: (j, rev(i)))
    colv = lambda n: pl.BlockSpec((n, 1), lambda i: (0, 0))
    return pl.pallas_call(
        body, name="fox_prep_bwd", grid=(nT,),
        in_specs=[part(0), part(1), pl.BlockSpec((LANES, T), lambda i: (3 * D // LANES, rev(i))),
                  pl.BlockSpec((H, T, HEAD_ROWS), lambda i: (0, rev(i), 0)),
                  pl.BlockSpec((H, HEAD_ROWS, T), lambda i: (0, 0, rev(i))), pl.BlockSpec((D, T), lambda i: (0, rev(i))),
                  colv(LANES), colv(D), colv(D), pl.BlockSpec((LANES, LANES), lambda i: (0, 0))],
        out_specs=[pl.BlockSpec((NU, T), lambda i: (0, rev(i))), colv(LANES), colv(HEAD_DIM), colv(HEAD_DIM)],
        out_shape=[jax.ShapeDtypeStruct((NU, S), BF16), jax.ShapeDtypeStruct((LANES, 1), F32),
                   jax.ShapeDtypeStruct((HEAD_DIM, 1), F32), jax.ShapeDtypeStruct((HEAD_DIM, 1), F32)],
        scratch_shapes=[pltpu.VMEM((D, 1), F32), pltpu.VMEM((D, 1), F32), pltpu.VMEM((LANES, 1), F32),
                        pltpu.VMEM((LANES, T), F32)],
        compiler_params=_params(("arbitrary",)),
    )(ut, ut, ut, dq, dkt, dvt, b_f, qg, kg, _tri_matrix(True))


def _block_diag_tiles(w):
    n = w.shape[0]
    per = min(MXU_DIM, n * LRU_BLOCK_DIM) // LRU_BLOCK_DIM
    eye = jnp.eye(per, dtype=w.dtype)
    w5 = w.reshape(n // per, per, LRU_BLOCK_DIM, 1, LRU_BLOCK_DIM) * eye[None, :, None, :, None]
    return w5.reshape(n // per, per * LRU_BLOCK_DIM, per * LRU_BLOCK_DIM).astype(BF16)


def _block_diag_extract(t, n):
    per = t.shape[-1] // LRU_BLOCK_DIM
    eye = jnp.eye(per, dtype=t.dtype)
    t5 = t.reshape(n // per, per, LRU_BLOCK_DIM, per, LRU_BLOCK_DIM) * eye[None, :, None, :, None]
    return t5.sum(axis=3).reshape(n, LRU_BLOCK_DIM, LRU_BLOCK_DIM)


def _local_step(x, tgt, small, wv, grad_view, comm=None):
    S, D = x.shape
    F = 4 * D
    H = D // HEAD_DIM
    nblk = D // LRU_BLOCK_DIM
    NU = 3 * D + LANES
    tq = max(LANES, min(512, S // 4))
    assert S % tq == 0
    vec = lambda a: a.reshape(1, -1).astype(F32)
    col = lambda a: a.reshape(-1, 1).astype(F32)
    mix_g, mlp_g = small["mix_norm"], small["mlp_norm"]
    conv_b = vec(small["lru_conv_b"])
    wr_bd, wi_bd = _block_diag_tiles(small["lru_w_r"][0]), _block_diag_tiles(small["lru_w_i"][0])
    b_r, b_i, lam = vec(small["lru_b_r"]), vec(small["lru_b_i"]), vec(small["lru_lambda"])
    b_f = jnp.pad(col(small["fox_b_f"]), ((0, LANES - H), (0, 0)))
    qg, kg = jnp.tile(col(small["fox_q_gain"]), (H, 1)), jnp.tile(col(small["fox_k_gain"]), (H, 1))
    X = lambda a: _View(a)
    grads = {}
    gout = functools.partial(grad_view, grads)

    def hosted(name, fn, *args):
        plan = comm.before(name, grads) if comm is not None else None
        res, side = fn(*args, plan=plan)
        if plan is not None:
            comm.after(name, side, wv)
        return res

    def hosted_mm(name, *args, **kw):
        plan = comm.before(name, grads) if comm is not None else None
        if plan is None:
            return _matmul(name, *args, **kw)
        res, side = _matmul(name, *args, plan=plan, **kw)
        comm.after(name, side, wv)
        return res

    two = lambda: [_fresh(S, D, F32), _fresh(S, D, BF16)]

    def mlp_up(l, hm):
        return hosted_mm(f"mlp{l}_up", X(hm), wv[f"w1_{l}"], S, F, D, outs=[_fresh(S, F, BF16)], epilogue=_ep_relu2)[0]

    def mlp_bwd(l, xin, hm, act, d, db):
        (dz,) = hosted_mm(f"mlp{l}_dact", X(db), wv[f"w2_{l}"], S, F, D, tb=True, outs=[_fresh(S, F, BF16)],
                          epilogue=_ep_drelu2, extras=[X(act)])
        (grads[f"w2_{l}"],) = _matmul(f"mlp{l}_dw2", X(act), X(db), F, D, S, ta=True, outs=[gout(f"w2_{l}")],
                                      epilogue=_ep_store)
        (grads[f"w1_{l}"],) = _matmul(f"mlp{l}_dw1", X(hm), X(dz), D, F, S, ta=True, outs=[gout(f"w1_{l}")],
                                      epilogue=_ep_store)
        return _matmul(f"mlp{l}_dhm", X(dz), wv[f"w1_{l}"], S, D, F, tb=True, outs=two(), n_sums=1,
                       epilogue=_ep_norm_bwd, extras=[X(xin), X(d)], vecs=[mlp_g[l:l + 1]])

    (h0,) = hosted("mix0_norm", _rms_fwd, "mix0_norm", x, mix_g[0:1], S, D)
    (u0,) = hosted_mm("lru_in", X(h0), wv["lru_in"], S, 2 * D, D, outs=[_fresh(S, 2 * D, F32)], epilogue=_ep_store)
    conv_w = small["conv_w"]
    y, xc, r, ig, hs = hosted("lru_fwd", _lru_fwd, u0, conv_w, conv_b, wr_bd, b_r, wi_bd, b_i, lam, S, D)
    x1, hm0 = _matmul("lru_out", X(y), wv["lru_out"], S, D, D, outs=two(), epilogue=_ep_resid_norm, extras=[X(x)],
                      vecs=[mlp_g[0:1]])
    act0 = mlp_up(0, hm0)
    x2, h1 = hosted_mm("mlp0_down", X(act0), wv["w2_0"], S, D, F, outs=two(), epilogue=_ep_resid_norm, extras=[X(x1)],
                       vecs=[mix_g[1:2]])
    (u1,) = _matmul("fox_in", wv["fox_in"], X(h1), NU, S, D, tb=True, outs=[_fresh(NU, S, F32)], epilogue=_ep_store)
    qat, kat, vat, ka = _fox_prep(u1, b_f, qg, kg, S, D, tq)
    o, o32, lse = hosted("attn_forward", _attn_forward, ka, qat, vat, S, D, tq)
    x3, hm1 = _matmul("fox_out", X(o), wv["fox_out"], S, D, D, ta=True, outs=two(), epilogue=_ep_resid_norm,
                      extras=[X(x2)], vecs=[mlp_g[1:2]])
    act1 = mlp_up(1, hm1)
    (x4,) = _matmul("mlp1_down", X(act1), wv["w2_1"], S, D, F, outs=[_fresh(S, D, F32)], epilogue=_ep_resid,
                    extras=[X(x3)])
    loss, d4, d4b = _loss_head(x4, tgt, S, D)

    d3, d3b, dg_mlp1 = mlp_bwd(1, x3, hm1, act1, d4, d4b)
    (do,) = _matmul("fox_dout", wv["fox_out"], X(d3b), D, S, D, tb=True, outs=[_fresh(D, S, BF16)], epilogue=_ep_store)
    (grads["fox_out"],) = _matmul("fox_dwout", X(o), X(d3b), D, D, S, outs=[gout("fox_out")], epilogue=_ep_store)
    doat, doa, qat1, qa1 = hosted("fox_bwd_prep", _fox_bwd_prep, do, o32, lse, qat, S, D, tq)
    dqn, dkn, dv = hosted("attn_backward", _attn_backward, qa1, doa, qat1, doat, ka, kat, vat, S, D, tq)
    du1, dbf, dqg, dkg = _fox_prep_bwd(u1, dqn, dkn, dv, b_f, qg, kg, S, D, tq)
    (grads["fox_in"],) = _matmul("fox_dwin", X(du1), X(h1), NU, D, S, outs=[gout("fox_in")], epilogue=_ep_store)
    d2, d2b, dg_mix1 = hosted_mm("fox_dh", X(du1), wv["fox_in"], S, D, NU, ta=True, outs=two(), n_sums=1,
                               epilogue=_ep_norm_bwd, extras=[X(x2), X(d3)], vecs=[mix_g[1:2]])
    d1, d1b, dg_mlp0 = mlp_bwd(0, x1, hm0, act0, d2, d2b)
    (grads["lru_out"],) = _matmul("lru_dwout", X(y), X(d1b), D, D, S, ta=True, outs=[gout("lru_out")],
                                  epilogue=_ep_store)
    (dy,) = hosted_mm("lru_dout", X(d1b), wv["lru_out"], S, D, D, tb=True, outs=[_fresh(S, D, F32)],
                      epilogue=_ep_store)
    du0, dcw, dcb, dlam, dbr, dbi, dwr, dwi = hosted("lru_bwd", _lru_bwd, dy, u0, xc, r, ig, hs, conv_w, wr_bd, wi_bd,
                                                     lam, S, D)
    (grads["lru_in"],) = hosted_mm("lru_dwin", X(h0), X(du0), D, 2 * D, S, ta=True, outs=[gout("lru_in")],
                                   epilogue=_ep_store)
    gx, dg_mix0 = hosted_mm("lru_dh", X(du0), wv["lru_in"], S, D, 2 * D, tb=True, outs=[_fresh(S, D, F32)], n_sums=1,
                            epilogue=lambda *a: _ep_norm_bwd(*a)[::2], extras=[X(x), X(d1)], vecs=[mix_g[0:1]])

    grads.update(
        mix_norm=jnp.concatenate([dg_mix0, dg_mix1], axis=0), mlp_norm=jnp.concatenate([dg_mlp0, dg_mlp1], axis=0),
        conv_w=dcw, lru_conv_b=dcb, lru_w_r=_block_diag_extract(dwr, nblk)[None], lru_b_r=dbr.reshape(1, nblk, -1),
        lru_w_i=_block_diag_extract(dwi, nblk)[None], lru_b_i=dbi.reshape(1, nblk, -1), lru_lambda=dlam,
        fox_b_f=dbf[:H].reshape(1, H), fox_q_gain=dqg.reshape(1, -1), fox_k_gain=dkg.reshape(1, -1))
    return loss, gx, grads


def _place():
    x, y, c = lax.axis_index("x"), lax.axis_index("y"), lax.axis_index("c")
    chips = [(1 - x, y), (x, 1 - y), (1 - x, 1 - y)]
    return x, y, c, 2 * x + y, chips


BOUNCE_BYTES = 1 << 20


def _bounce_shape(rows, cols, dtype):
    chunk = rows
    while chunk % 2 == 0 and chunk > 16 and chunk * cols * jnp.dtype(dtype).itemsize > BOUNCE_BYTES:
        chunk //= 2
    return pltpu.VMEM((2, chunk, cols), dtype)


def _bounce_copy(src, dst, buf, sem):
    chunk = buf.shape[1]
    n = src.shape[0] // chunk
    cin = lambda i: pltpu.make_async_copy(src.at[pl.ds(i * chunk, chunk)], buf.at[i % 2], sem.at[i % 2])
    cout = lambda i: pltpu.make_async_copy(buf.at[i % 2], dst.at[pl.ds(i * chunk, chunk)], sem.at[2 + i % 2])
    cin(0).start()
    for i in range(n):
        cin(i).wait()
        if i + 1 < n:
            if i >= 1:
                cout(i - 1).wait()
            cin(i + 1).start()
        cout(i).start()
    if n >= 2:
        cout(n - 2).wait()
    cout(n - 1).wait()


class _Gather:
    def __init__(self, shards):
        n = self.n = len(shards)
        self.operands = list(shards)
        self.out_shape = [jax.ShapeDtypeStruct((N_CHIPS,) + tuple(a.shape), a.dtype) for a in shards]
        self.scratch = [pltpu.SemaphoreType.DMA((3 * n,)) for _ in range(4)]
        for a in shards:
            self.scratch += [_bounce_shape(a.shape[0], a.shape[1], a.dtype), pltpu.SemaphoreType.DMA((4,))]

    def _copies(self, ins, outs, scr):
        send, recv, fsend, frecv = scr[:4]
        x, y, c, s, chips = _place()

        def rows(a, chip_idx, which):
            hr = ins[a].shape[0] // 2
            return outs[a].at[chip_idx, pl.ds(which * hr, hr)]

        def landed(a, j, core):
            return rows(a, 2 * chips[j][0] + chips[j][1], core)

        def ici(a, j, mine):
            hr = ins[a].shape[0] // 2
            src, dst = (ins[a].at[pl.ds(c * hr, hr)], rows(a, s, c)) if mine else (landed(a, j, c),) * 2
            return pltpu.make_async_remote_copy(src_ref=src, dst_ref=dst, send_sem=send.at[3 * a + j],
                                                recv_sem=recv.at[3 * a + j], device_id=(*chips[j], c),
                                                device_id_type=MESH)

        def d2d(a, j, mine):
            ref = landed(a, j, c if mine else 1 - c)
            return pltpu.make_async_remote_copy(src_ref=ref, dst_ref=ref, send_sem=fsend.at[3 * a + j],
                                                recv_sem=frecv.at[3 * a + j], device_id=(x, y, 1 - c),
                                                device_id_type=MESH)

        return ici, d2d, s

    def start(self, ins, outs, scr):
        ici, _, _ = self._copies(ins, outs, scr)
        for a in range(self.n):
            for j in range(3):
                ici(a, j, True).start()

    def middle(self, ins, outs, scr):
        ici, d2d, s = self._copies(ins, outs, scr)
        for a in range(self.n):
            _bounce_copy(ins[a], outs[a].at[s], scr[4 + 2 * a], scr[5 + 2 * a])
        for a in range(self.n):
            for j in range(3):
                ici(a, j, False).wait_recv()
                d2d(a, j, True).start()

    def finish(self, ins, outs, scr):
        ici, d2d, _ = self._copies(ins, outs, scr)
        for a in range(self.n):
            for j in range(3):
                d2d(a, j, False).wait_recv()
        for a in range(self.n):
            for j in range(3):
                ici(a, j, True).wait_send()
                d2d(a, j, True).wait_send()


def _run_plan(name, plan):
    k_in, k_out = len(plan.operands), len(plan.out_shape)

    def body(*refs):
        parts = (refs[:k_in], refs[k_in:k_in + k_out], refs[k_in + k_out:])
        plan.start(*parts)
        plan.middle(*parts)
        plan.finish(*parts)

    return pl.pallas_call(
        body, name=name, in_specs=[ANY] * k_in, out_specs=[ANY] * k_out, out_shape=plan.out_shape,
        scratch_shapes=plan.scratch,
        compiler_params=pltpu.CompilerParams(has_side_effects=True, vmem_limit_bytes=VMEM_LIMIT),
    )(*plan.operands)


def _hosted_call(body, name, grid, in_specs, out_specs, out_shape, scratch_shapes, operands, sem, plan=None):
    if plan is None:
        res = pl.pallas_call(body, name=name, grid=grid, in_specs=in_specs, out_specs=out_specs, out_shape=out_shape,
                             scratch_shapes=scratch_shapes, compiler_params=_params(sem))(*operands)
        return res, None
    n_in, n_out, n_scr = len(in_specs), len(out_specs), len(scratch_shapes)
    k_in, k_out = len(plan.operands), len(plan.out_shape)
    total = int(np.prod(grid))
    late = max(0, total - 1 - max(1, total // 8))

    def hosted(*refs):
        ins, refs = refs[:n_in], refs[n_in:]
        p_ins, refs = refs[:k_in], refs[k_in:]
        outs, refs = refs[:n_out], refs[n_out:]
        p_outs, refs = refs[:k_out], refs[k_out:]
        scr, p_scr = refs[:n_scr], refs[n_scr:]
        step = pl.program_id(0)
        for d in range(1, len(grid)):
            step = step * grid[d] + pl.program_id(d)
        pl.when(step == 0)(lambda: plan.start(p_ins, p_outs, p_scr))
        body(*ins, *outs, *scr)
        pl.when(step == late)(lambda: plan.middle(p_ins, p_outs, p_scr))
        pl.when(step == total - 1)(lambda: plan.finish(p_ins, p_outs, p_scr))

    res = pl.pallas_call(
        hosted, name=name, grid=grid, in_specs=list(in_specs) + [ANY] * k_in, out_specs=list(out_specs) + [ANY] * k_out,
        out_shape=list(out_shape) + plan.out_shape, scratch_shapes=list(scratch_shapes) + plan.scratch,
        compiler_params=pltpu.CompilerParams(dimension_semantics=sem, vmem_limit_bytes=VMEM_LIMIT,
                                             has_side_effects=True),
    )(*operands, *plan.operands)
    return res[:n_out], res[n_out:]


def _all_gather(name, shards):
    return _run_plan(name, _Gather(shards))


class _Swap:
    def __init__(self, arrs):
        self.n = len(arrs)
        self.operands = list(arrs)
        self.out_shape = [jax.ShapeDtypeStruct((a.shape[0], a.shape[1] // 2, a.shape[2]), a.dtype) for a in arrs]
        self.scratch = [pltpu.SemaphoreType.DMA((self.n,)) for _ in range(2)]

    def _copy(self, ins, outs, scr, a):
        x, y, c, _, _ = _place()
        hr = ins[a].shape[1] // 2
        return pltpu.make_async_remote_copy(
            src_ref=ins[a].at[:, pl.ds((1 - c) * hr, hr)], dst_ref=outs[a], send_sem=scr[0].at[a],
            recv_sem=scr[1].at[a], device_id=(x, y, 1 - c), device_id_type=MESH)

    def start(self, ins, outs, scr):
        for a in range(self.n):
            self._copy(ins, outs, scr, a).start()

    def middle(self, ins, outs, scr):
        pass

    def finish(self, ins, outs, scr):
        for a in range(self.n):
            self._copy(ins, outs, scr, a).wait()


class _Scatter:
    def __init__(self, parts):
        n = self.n = len(parts)
        self.operands = list(parts)
        self.out_shape = [jax.ShapeDtypeStruct(a.shape, a.dtype) for a in parts]
        self.scratch = [pltpu.SemaphoreType.DMA((3 * n,)) for _ in range(2)]
        for a in parts:
            self.scratch += [_bounce_shape(a.shape[1], a.shape[2], a.dtype), pltpu.SemaphoreType.DMA((4,))]

    def _copy(self, ins, outs, scr, a, j, mine):
        x, y, c, s, chips = _place()
        t = 2 * chips[j][0] + chips[j][1]
        return pltpu.make_async_remote_copy(
            src_ref=ins[a].at[t], dst_ref=outs[a].at[s if mine else t], send_sem=scr[0].at[3 * a + j],
            recv_sem=scr[1].at[3 * a + j], device_id=(*chips[j], c), device_id_type=MESH)

    def start(self, ins, outs, scr):
        for a in range(self.n):
            for j in range(3):
                self._copy(ins, outs, scr, a, j, True).start()

    def middle(self, ins, outs, scr):
        s = _place()[3]
        for a in range(self.n):
            _bounce_copy(ins[a].at[s], outs[a].at[s], scr[2 + 2 * a], scr[3 + 2 * a])

    def finish(self, ins, outs, scr):
        for a in range(self.n):
            for j in range(3):
                self._copy(ins, outs, scr, a, j, False).wait_recv()
        for a in range(self.n):
            for j in range(3):
                self._copy(ins, outs, scr, a, j, True).wait_send()


class _PairGather:
    def __init__(self, halves):
        self.n = len(halves)
        self.operands = list(halves)
        self.out_shape = [jax.ShapeDtypeStruct((2 * a.shape[0], a.shape[1]), a.dtype) for a in halves]
        self.scratch = [pltpu.SemaphoreType.DMA((self.n,)) for _ in range(2)]
        for a in halves:
            self.scratch += [_bounce_shape(a.shape[0], a.shape[1], a.dtype), pltpu.SemaphoreType.DMA((4,))]

    def _copy(self, ins, outs, scr, a, mine):
        x, y, c, _, _ = _place()
        hr = ins[a].shape[0]
        dst = outs[a].at[pl.ds((c if mine else 1 - c) * hr, hr)]
        return pltpu.make_async_remote_copy(src_ref=ins[a] if mine else dst, dst_ref=dst, send_sem=scr[0].at[a],
                                            recv_sem=scr[1].at[a], device_id=(x, y, 1 - c), device_id_type=MESH)

    def start(self, ins, outs, scr):
        for a in range(self.n):
            self._copy(ins, outs, scr, a, True).start()

    def middle(self, ins, outs, scr):
        c = _place()[2]
        for a in range(self.n):
            hr = ins[a].shape[0]
            _bounce_copy(ins[a], outs[a].at[pl.ds(c * hr, hr)], scr[2 + 2 * a], scr[3 + 2 * a])

    def finish(self, ins, outs, scr):
        for a in range(self.n):
            self._copy(ins, outs, scr, a, True).wait_send()
            self._copy(ins, outs, scr, a, False).wait_recv()


def _row_tile(rows, cols, itemsize, n_bufs):
    budget = VMEM_LIMIT // 2
    for t in range(min(rows, 1024) // 16 * 16, 0, -16):
        if rows % t == 0 and 2 * n_bufs * t * cols * itemsize <= budget:
            return t
    return rows


def _pair_add(name, g, gsib, core, out_dtype):
    _, r, cols = g.shape
    hr = r // 2
    t = _row_tile(hr, cols, 4, 3)
    per = hr // t

    def body(core_ref, a_ref, b_ref, o_ref):
        o_ref[...] = (a_ref[...].astype(F32) + b_ref[...].astype(F32)).astype(o_ref.dtype)

    grid_spec = pltpu.PrefetchScalarGridSpec(
        num_scalar_prefetch=1, grid=(N_CHIPS, per),
        in_specs=[pl.BlockSpec((None, t, cols), lambda s, i, core: (s, core[0] * per + i, 0)),
                  pl.BlockSpec((None, t, cols), lambda s, i, core: (s, i, 0))],
        out_specs=pl.BlockSpec((None, t, cols), lambda s, i, core: (s, i, 0)))
    return pl.pallas_call(body, name=name, grid_spec=grid_spec,
                          out_shape=jax.ShapeDtypeStruct((N_CHIPS, hr, cols), out_dtype),
                          compiler_params=_params(("arbitrary", "arbitrary")))(core, g, gsib)


def _chip_sum(name, parts):
    _, hr, cols = parts.shape
    t = _row_tile(hr, cols, 4, 5)

    def body(p_ref, o_ref):
        o_ref[...] = ((p_ref[0].astype(F32) + p_ref[1].astype(F32)) + p_ref[2].astype(F32)) + p_ref[3].astype(F32)

    return pl.pallas_call(
        body, name=name, grid=(hr // t,), in_specs=[pl.BlockSpec((N_CHIPS, t, cols), lambda i: (0, i, 0))],
        out_specs=pl.BlockSpec((t, cols), lambda i: (i, 0)), out_shape=jax.ShapeDtypeStruct((hr, cols), F32),
        compiler_params=_params(("arbitrary",)))(parts)


def _pair_partials(tag, arrs, sib, wire_dtypes, core):
    return _Scatter([_pair_add(f"{tag}_pair_add{i}", g, gs, core, dt)
                     for i, (g, gs, dt) in enumerate(zip(arrs, sib, wire_dtypes))])


def _finish_reduce(tag, scattered):
    return _PairGather([_chip_sum(f"{tag}_chip_sum{i}", p) for i, p in enumerate(scattered)])


def _adamw(name, w, g_parts, m, v):
    thin = w.ndim == 3
    rows, cols = w.shape[0], w.shape[-1]
    n_parts = len(g_parts)
    part_rows = rows // n_parts
    t = max(d for d in range(1, 257) if part_rows % d == 0) if thin else _row_tile(part_rows, cols, 4, 7 + n_parts)
    per = part_rows // t
    c1 = 1.0 - ADAM_B1 ** ADAM_STEP
    c2 = 1.0 - ADAM_B2 ** ADAM_STEP

    def body(w_ref, m_ref, v_ref, *refs):
        g_refs, (go_ref, d_ref, nm_ref, nv_ref) = refs[:n_parts], refs[n_parts:]
        g = g_refs[0][...]
        for k in range(1, n_parts):
            g = jnp.where(pl.program_id(0) >= k * per, g_refs[k][...], g)
        go_ref[...] = g
        m = ADAM_B1 * m_ref[...] + (1.0 - ADAM_B1) * g
        v = ADAM_B2 * v_ref[...] + (1.0 - ADAM_B2) * (g * g)
        nm_ref[...] = m
        nv_ref[...] = v
        d_ref[...] = -ADAM_LR * ((m / c1) / (jnp.sqrt(v / c2) + ADAM_EPS) + ADAM_WD * w_ref[...])

    block = (t, 1, cols) if thin else (t, cols)
    at = lambda r: (r, 0, 0) if thin else (r, 0)
    spec = pl.BlockSpec(block, lambda i: at(i))
    g_specs = [pl.BlockSpec(block, lambda i, k=k: at(jnp.clip(i - k * per, 0, per - 1))) for k in range(n_parts)]
    shp = jax.ShapeDtypeStruct(w.shape, F32)
    return pl.pallas_call(body, name=name, grid=(rows // t,), in_specs=[spec] * 3 + g_specs, out_specs=[spec] * 4,
                          out_shape=[shp] * 4, compiler_params=_params(("arbitrary",)))(w, m, v, *g_parts)


_WEIGHTS = ["mix_norm", "mlp_norm", "mlp_w1", "mlp_w2", "lru_w_in", "lru_conv_w", "lru_conv_b", "lru_w_r", "lru_b_r",
            "lru_w_i", "lru_b_i", "lru_lambda", "lru_w_out", "fox_w_in", "fox_b_f", "fox_q_gain", "fox_k_gain",
            "fox_w_out"]
_REPLICATED = ["mix_norm", "mlp_norm", "lru_conv_b", "lru_w_r", "lru_b_r", "lru_w_i", "lru_b_i", "lru_lambda",
               "fox_b_f", "fox_q_gain", "fox_k_gain"]
_PACK_TILE = 2 * SUBLANES * LANES


def _as2d(a):
    return a.reshape(-1, a.shape[-1])


def kernel(x, mix_norm, mlp_norm, mlp_w1, mlp_w2, lru_w_in, lru_conv_w, lru_conv_b, lru_w_r, lru_b_r, lru_w_i, lru_b_i, lru_lambda, lru_w_out, fox_w_in, fox_b_f, fox_q_gain, fox_k_gain, fox_w_out, loss_target, m_mix_norm, m_mlp_norm, m_mlp_w1, m_mlp_w2, m_lru_w_in, m_lru_conv_w, m_lru_conv_b, m_lru_w_r, m_lru_b_r, m_lru_w_i, m_lru_b_i, m_lru_lambda, m_lru_w_out, m_fox_w_in, m_fox_b_f, m_fox_q_gain, m_fox_k_gain, m_fox_w_out, v_mix_norm, v_mlp_norm, v_mlp_w1, v_mlp_w2, v_lru_w_in, v_lru_conv_w, v_lru_conv_b, v_lru_w_r, v_lru_b_r, v_lru_w_i, v_lru_b_i, v_lru_lambda, v_lru_w_out, v_fox_w_in, v_fox_b_f, v_fox_q_gain, v_fox_k_gain, v_fox_w_out):
    args = dict(locals())
    W = {n: args[n] for n in _WEIGHTS}
    Mo = {n: args["m_" + n] for n in _WEIGHTS}
    Vo = {n: args["v_" + n] for n in _WEIGHTS}
    S, D = x.shape[1], x.shape[2]
    F = 4 * D
    H = D // HEAD_DIM
    NU = 3 * D + LANES
    FQ, DQ = F // N_CHIPS, D // N_CHIPS
    nfox = fox_w_in.shape[-1]
    chip = 2 * lax.axis_index("x") + lax.axis_index("y")
    core = lax.axis_index("c").astype(jnp.int32).reshape(1)

    cw_flat = jnp.pad(lru_conv_w.reshape(-1), (0, _PACK_TILE - CONV_WIDTH * DQ)).reshape(2 * SUBLANES, LANES)
    w1s, w2s = mlp_w1.astype(BF16), mlp_w2.astype(BF16)
    wv = {}
    small = {n: W[n] for n in _REPLICATED}
    scattered = {}
    members = {"g1": ["w2_1", "w1_1", "fox_out"], "g2": ["fox_in"], "g3": ["w2_0", "w1_0"], "g4": ["lru_out", "lru_in"]}
    swap_at = {"fox_bwd_prep": "g1", "fox_dh": "g2", "lru_dout": "g3"}
    scatter_at = {"attn_backward": "g1", "mlp0_dact": "g2", "lru_bwd": "g3", "lru_dh": "g4"}
    swapped = {}
    early = [n for g in ("g1", "g2", "g3") for n in members[g]]
    red = {}

    fox_rows = -(-nfox // (4 * SUBLANES)) * (4 * SUBLANES)
    fox_t = jnp.pad(jnp.transpose(fox_w_in[0]).astype(BF16), ((0, fox_rows - nfox), (0, 0)))

    def shard_major(name, g):
        if name == "fox_in":
            return jnp.pad(g[:nfox * N_CHIPS].reshape(N_CHIPS, nfox, D), ((0, 0), (0, fox_rows - nfox), (0, 0)))
        return g

    class Comm:
        @staticmethod
        def before(name, grads):
            if name == "mix0_norm":
                return _Gather([lru_w_in[0].astype(BF16)])
            if name == "lru_in":
                return _Gather([lru_w_out[0].astype(BF16), cw_flat])
            if name == "lru_fwd":
                return _Gather([w1s[0]])
            if name == "mlp0_up":
                return _Gather([w2s[0]])
            if name == "mlp0_down":
                return _Gather([fox_t])
            if name == "attn_forward":
                return _Gather([fox_w_out[0].astype(BF16), w1s[1], w2s[1]])
            if name in swap_at:
                group = swap_at[name]
                swapped[group] = [[shard_major(n, grads[n]) for n in members[group]], None]
                return _Swap(swapped[group][0])
            if name in scatter_at:
                group = scatter_at[name]
                if group not in swapped:
                    arrs = [shard_major(n, grads[n]) for n in members[group]]
                    swapped[group] = [arrs, _run_plan(f"{group}_pair_swap", _Swap(arrs))]
                arrs, sib = swapped[group]
                return _pair_partials(group, arrs, sib, [BF16] * len(arrs), core)
            if name == "lru_dwin":
                return _finish_reduce("early", [scattered[n] for n in early])
            return None

        @staticmethod
        def after(name, res, wv):
            if name == "mix0_norm":
                wv.update(lru_in=_View(res[0], "cs"))
            elif name == "lru_in":
                wv.update(lru_out=_View(res[0], "rs"))
                taps = res[1].reshape(N_CHIPS, -1)[:, :CONV_WIDTH * DQ].reshape(N_CHIPS, CONV_WIDTH, DQ)
                small["conv_w"] = jnp.transpose(taps, (1, 0, 2)).reshape(CONV_WIDTH, D)
            elif name == "lru_fwd":
                wv.update(w1_0=_View(res[0], "cs"))
            elif name == "mlp0_up":
                wv.update(w2_0=_View(res[0], "rs"))
            elif name == "mlp0_down":
                fox_full = jnp.concatenate([res[0][s, :nfox] for s in range(N_CHIPS)], axis=0)
                wv.update(fox_in=_View(jnp.pad(fox_full, ((0, NU - fox_full.shape[0]), (0, 0)))))
            elif name == "attn_forward":
                wv.update(fox_out=_View(res[0], "rs"), w1_1=_View(res[1], "cs"), w2_1=_View(res[2], "rs"))
            elif name in swap_at:
                swapped[swap_at[name]][1] = res
            elif name == "lru_dwin":
                red.update(zip(early, res))
            else:
                scattered.update(zip(members[scatter_at[name]], res))

    def grad_view(grads, name):
        if name in ("w1_0", "w1_1"):
            return _View(None, "cs", shape=(N_CHIPS, D, FQ), dtype=BF16)
        if name in ("w2_0", "w2_1"):
            return _View(None, "rs", shape=(N_CHIPS, FQ, D), dtype=BF16)
        if name == "lru_in":
            return _View(None, "cs", shape=(N_CHIPS, D, 2 * D // N_CHIPS), dtype=BF16)
        if name in ("lru_out", "fox_out"):
            return _View(None, "rs", shape=(N_CHIPS, DQ, D), dtype=BF16)
        return _View(None, shape=(NU, D), dtype=BF16)

    loss, gx, grads = _local_step(x[0], loss_target[0], small, wv, grad_view, Comm)

    pack_names = _REPLICATED + ["conv_w"]
    flat = jnp.concatenate([grads[n].reshape(-1).astype(F32) for n in pack_names] + [loss.reshape(-1)])
    per_chip = -(-flat.shape[0] // (N_CHIPS * _PACK_TILE)) * _PACK_TILE
    pack = jnp.pad(flat, (0, N_CHIPS * per_chip - flat.shape[0])).reshape(N_CHIPS, per_chip // LANES, LANES)
    pack_sib = _run_plan("pack_pair_swap", _Swap([pack]))
    (scattered["pack"],) = _run_plan("pack_chip_scatter", _pair_partials("pack", [pack], pack_sib, [F32], core))
    late = members["g4"] + ["pack"]
    red.update(zip(late, _run_plan("grads_pair_gather", _finish_reduce("grads", [scattered[n] for n in late]))))
    (all_pack,) = _all_gather("gather_small_grads", [red["pack"]])
    all_flat = all_pack.reshape(-1)
    G = {}
    off = 0
    for n in pack_names:
        shape = grads[n].shape if n == "conv_w" else W[n].shape
        size = int(np.prod(shape))
        G[n] = all_flat[off:off + size].reshape(shape)
        off += size
    total = all_flat[off]
    G["lru_conv_w"] = lax.dynamic_slice_in_dim(G.pop("conv_w"), chip * DQ, DQ, axis=1)[None]
    parts = {n: [_as2d(G[n])] for n in G}
    parts.update(mlp_w1=[red["w1_0"], red["w1_1"]], mlp_w2=[red["w2_0"], red["w2_1"]], lru_w_in=[red["lru_in"]],
                 lru_w_out=[red["lru_out"]], fox_w_in=[red["fox_in"][:nfox, None, :]], fox_w_out=[red["fox_out"]])

    delta, new_m, new_v = {}, {}, {}
    for n in _WEIGHTS:
        if n == "fox_w_in":
            to_thin = lambda a: jnp.transpose(a, (2, 0, 1))
            res = _adamw(f"adamw_{n}", to_thin(W[n]), parts[n], to_thin(Mo[n]), to_thin(Vo[n]))
            G[n], delta[n], new_m[n], new_v[n] = (jnp.transpose(t, (1, 2, 0)) for t in res)
            continue
        go, d, nm, nv = _adamw(f"adamw_{n}", _as2d(W[n]), parts[n], _as2d(Mo[n]), _as2d(Vo[n]))
        G[n], delta[n], new_m[n], new_v[n] = (t.reshape(W[n].shape) for t in (go, d, nm, nv))

    return (total, gx[None], *[G[n] for n in _WEIGHTS], *[delta[n] for n in _WEIGHTS],
            *[new_m[n] for n in _WEIGHTS], *[new_v[n] for n in _WEIGHTS])
```

```python
import functools

import numpy as np
import jax
import jax.numpy as jnp
from jax import lax
from jax.experimental import pallas as pl
from jax.experimental.pallas import tpu as pltpu

F32 = jnp.float32
BF16 = jnp.bfloat16

HEAD_DIM = 64
LRU_BLOCK_DIM = 64
CONV_WIDTH = 4
LRU_C = 8.0
EPS = 1e-6
NEG_INF = -1e30
ADAM_LR = 0.001
ADAM_B1 = 0.9
ADAM_B2 = 0.999
ADAM_EPS = 1e-08
ADAM_WD = 0.01
ADAM_STEP = 10

N_CHIPS = 4
LANES = 128
SUBLANES = 8
MXU_DIM = 256
VMEM_LIMIT = 52 * 1024 * 1024
MATMUL_TILES = (1024, 640, 512, 256, 128)
MATMUL_VMEM = VMEM_LIMIT * 4 // 5
MESH = pl.DeviceIdType.MESH
ANY = pl.BlockSpec(memory_space=pl.ANY)


def _pick(n, prefs):
    for p in prefs:
        if p <= n and n % p == 0:
            return p
    return n


def _params(sem=None):
    return pltpu.CompilerParams(dimension_semantics=sem, vmem_limit_bytes=VMEM_LIMIT)


class _View:
    def __init__(self, arr, kind="plain", shape=None, dtype=None):
        self.arr = arr
        self.kind = kind
        self.shape = tuple(arr.shape) if arr is not None else tuple(shape)
        self.dtype = arr.dtype if arr is not None else dtype

    def limits(self):
        if self.kind == "plain":
            return 0, 0
        return self.shape[-2], (self.shape[-1] if self.kind == "cs" else 0)

    def spec(self, br, bc, fr, fc):
        if self.kind == "plain":
            return pl.BlockSpec((br, bc), lambda *g: (fr(*g), fc(*g)))
        rows, ncol = self.shape[-2:]
        assert rows % br == 0 and ncol % bc == 0, (self.shape, br, bc)
        if self.kind == "cs":
            per = ncol // bc
            return pl.BlockSpec((None, br, bc), lambda *g: (fc(*g) // per, fr(*g), fc(*g) % per))
        per = rows // br
        return pl.BlockSpec((None, br, bc), lambda *g: (fr(*g) // per, fr(*g) % per, fc(*g)))


def _bf(x):
    return x if x.dtype == BF16 else x.astype(BF16)


def _matmul(name, A, B, M, N, K, *, ta=False, tb=False, outs, epilogue, extras=(), vecs=(), n_sums=0,
            tm=None, tn=None, tk=None, plan=None):
    lim = {"m": [M], "n": [N], "k": [K]}
    for view, (rdim, cdim) in ([(A, "km" if ta else "mk"), (B, "nk" if tb else "kn")]
                               + [(e, "mn") for e in extras] + [(o, "mn") for o in outs]):
        r_lim, c_lim = view.limits()
        lim[rdim].append(r_lim)
        lim[cdim].append(c_lim)
    cap = {d: int(np.gcd.reduce(lim[d])) for d in "mnk"}
    tm = tm or _pick(cap["m"], MATMUL_TILES)
    tn = tn or _pick(cap["n"], MATMUL_TILES)
    tk = tk or _pick(cap["k"], MATMUL_TILES)

    def vmem_bytes(tm, tk):
        size = lambda v: jnp.dtype(v.dtype).itemsize
        tiles = tm * tk * size(A) + tk * tn * size(B) + tm * tn * sum(size(v) for v in list(extras) + list(outs))
        return 2 * tiles + (tm * tn * 4 if K > tk else 0)

    if cap["k"] % (2 * tk) == 0 and vmem_bytes(tm, 2 * tk) <= MATMUL_VMEM:
        tk *= 2
    elif K == tk and cap["m"] % (2 * tm) == 0 and vmem_bytes(2 * tm, tk) <= MATMUL_VMEM:
        tm *= 2
    nk = K // tk
    gi = lambda i, j, k: i
    gj = lambda i, j, k: j
    gk = lambda i, j, k: k
    a_spec = A.spec(tk, tm, gk, gi) if ta else A.spec(tm, tk, gi, gk)
    b_spec = B.spec(tn, tk, gj, gk) if tb else B.spec(tk, tn, gk, gj)
    ca = 0 if ta else 1
    cb = 1 if tb else 0
    ne, no = len(extras) + len(vecs), len(outs)
    assert n_sums == 0 or tn == N
    row_spec = pl.BlockSpec((1, tn), lambda i, j, k: (0, j))
    in_specs = [a_spec, b_spec] + [e.spec(tm, tn, gi, gj) for e in extras] + [row_spec] * len(vecs)
    operands = [A.arr, B.arr] + [e.arr for e in extras] + list(vecs)
    out_specs = [o.spec(tm, tn, gi, gj) for o in outs] + [row_spec] * n_sums
    out_shape = ([jax.ShapeDtypeStruct(o.shape, o.dtype) for o in outs]
                 + [jax.ShapeDtypeStruct((1, N), F32)] * n_sums)

    def body(*refs):
        a_ref, b_ref = refs[0], refs[1]
        ex = refs[2:2 + ne]
        o_refs = refs[2 + ne:2 + ne + no]
        s_refs = refs[2 + ne + no:2 + ne + no + n_sums]
        first_row_tile = pl.program_id(0) == 0

        def prod():
            return lax.dot_general(_bf(a_ref[...]), _bf(b_ref[...]), (((ca,), (cb,)), ((), ())),
                                   preferred_element_type=F32)

        def finish(acc):
            res = epilogue(acc, *[e[...] for e in ex])
            for o_ref, r in zip(o_refs, res[:no]):
                o_ref[...] = r.astype(o_ref.dtype)
            for s_ref, r in zip(s_refs, res[no:]):
                def assign(s_ref=s_ref, r=r):
                    s_ref[...] = r

                def accumulate(s_ref=s_ref, r=r):
                    s_ref[...] += r

                pl.when(first_row_tile)(assign)
                pl.when(jnp.logical_not(first_row_tile))(accumulate)

        if nk == 1:
            finish(prod())
        else:
            acc_ref = refs[-1]
            k = pl.program_id(2)

            @pl.when(k == 0)
            def _():
                acc_ref[...] = jnp.zeros_like(acc_ref)

            acc_ref[...] += prod()

            @pl.when(k == nk - 1)
            def _():
                finish(acc_ref[...])

    res, side = _hosted_call(body, name, (M // tm, N // tn, nk), in_specs, out_specs, out_shape,
                             [pltpu.VMEM((tm, tn), F32)] if nk > 1 else [], operands,
                             ("arbitrary", "arbitrary", "arbitrary"), plan)
    return res if plan is None else (res, side)


def _ep_store(acc):
    return (acc,)


def _ep_resid(acc, res):
    return (res + acc,)


def _ep_resid_norm(acc, res, g):
    xo = res + acc
    r = lax.rsqrt(jnp.mean(xo * xo, axis=-1, keepdims=True) + EPS)
    return (xo, (xo * r) * g)


def _ep_norm_bwd(acc, x, dres, g):
    r = lax.rsqrt(jnp.mean(x * x, axis=-1, keepdims=True) + EPS)
    xhat = x * r
    dxn = acc * g
    tot = dres + r * (dxn - xhat * jnp.mean(dxn * xhat, axis=-1, keepdims=True))
    return (tot, tot, jnp.sum(acc * xhat, axis=0, keepdims=True))


def _ep_relu2(acc):
    zp = jnp.maximum(acc, 0.0)
    return (zp * zp,)


def _ep_drelu2(acc, act):
    return (acc * (2.0 * jnp.sqrt(act.astype(F32))),)


def _fresh(M, N, dtype):
    return _View(None, shape=(M, N), dtype=dtype)


def _rms_fwd(name, x, g, S, D, plan=None):
    T = _pick(S, (512, 256, 128))

    def body(x_ref, g_ref, h_ref):
        x = x_ref[...]
        r = lax.rsqrt(jnp.mean(x * x, axis=-1, keepdims=True) + EPS)
        h_ref[...] = ((x * r) * g_ref[...]).astype(BF16)

    return _hosted_call(body, name, (S // T,),
                        [pl.BlockSpec((T, D), lambda i: (i, 0)), pl.BlockSpec((1, D), lambda i: (0, 0))],
                        [pl.BlockSpec((T, D), lambda i: (i, 0))], [jax.ShapeDtypeStruct((S, D), BF16)], [], (x, g),
                        ("arbitrary",), plan)


def _loss_head(x, tgt, S, D):
    T = _pick(S, (512, 256, 128))

    def body(x_ref, t_ref, loss_ref, d_ref, db_ref):
        @pl.when(pl.program_id(0) == 0)
        def _():
            loss_ref[...] = jnp.zeros_like(loss_ref)

        e = x_ref[...] - t_ref[...]
        loss_ref[...] += 0.5 * jnp.sum(jnp.mean(e * e, axis=-1, keepdims=True), axis=0, keepdims=True)
        d = e * (1.0 / D)
        d_ref[...] = d
        db_ref[...] = d.astype(BF16)

    row = pl.BlockSpec((T, D), lambda i: (i, 0))
    return pl.pallas_call(
        body, name="loss_head", grid=(S // T,), in_specs=[row, row],
        out_specs=[pl.BlockSpec((1, 1), lambda i: (0, 0)), row, row],
        out_shape=[jax.ShapeDtypeStruct((1, 1), F32), jax.ShapeDtypeStruct((S, D), F32),
                   jax.ShapeDtypeStruct((S, D), BF16)],
        compiler_params=_params(("arbitrary",)),
    )(x, tgt)


def _sigmoid(z):
    return 1.0 / (1.0 + jnp.exp(-z))


def _log_sigmoid(z):
    return jnp.minimum(z, 0.0) - jnp.log(1.0 + jnp.exp(-jnp.abs(z)))


_GELU_K = 0.7978845608028654
_GELU_C = 0.044715


def _gelu(x):
    t = jnp.tanh(_GELU_K * (x + _GELU_C * (x * x * x)))
    return 0.5 * x * (1.0 + t)


def _gelu_and_grad(x):
    x2 = x * x
    t = jnp.tanh(_GELU_K * (x + _GELU_C * (x2 * x)))
    g = 0.5 * x * (1.0 + t)
    dg = 0.5 * (1.0 + t) + 0.5 * x * (1.0 - t * t) * (_GELU_K * (1.0 + 3.0 * _GELU_C * x2))
    return g, dg


def _decay_terms(r, ls):
    la = LRU_C * r * ls
    a = jnp.exp(la)
    a2 = a * a
    mult = jnp.sqrt(-jnp.tanh(la) * (a2 + 1.0))
    return a, a2, mult


def _group_scan(a, b, sub, reverse):
    k = 1
    while k < SUBLANES:
        inside = sub < SUBLANES - k if reverse else sub >= k
        shift = SUBLANES - k if reverse else k
        b = b + a * jnp.where(inside, pltpu.roll(b, shift, 0), 0.0)
        a = a * jnp.where(inside, pltpu.roll(a, shift, 0), 1.0)
        k *= 2
    return a, b


def _lru_fwd(u0, conv_w, conv_b, wr_bd, b_r, wi_bd, b_i, lam, S, D, plan=None):
    T = _pick(S, (256, 128))
    GT = wr_bd.shape[-1]
    nG = D // GT

    def body(gb_ref, xb_ref, cw_ref, cb_ref, wr_ref, br_ref, wi_ref, bi_ref, lam_ref,
             y_ref, xc_ref, r_ref, i_ref, hs_ref, ext, a_scr, hcar):
        @pl.when(pl.program_id(0) == 0)
        def _():
            ext[0:SUBLANES, :] = jnp.zeros((SUBLANES, D), F32)
            hcar[...] = jnp.zeros_like(hcar)

        xb = xb_ref[...]
        ext[SUBLANES:SUBLANES + T, :] = xb
        xc = cb_ref[...]
        for k in range(CONV_WIDTH):
            xc = xc + ext[pl.ds(SUBLANES - (CONV_WIDTH - 1) + k, T), :] * cw_ref[k:k + 1, :]
        ext[0:SUBLANES, :] = xb[T - SUBLANES:T, :]
        xc_ref[...] = xc
        xcb = xc.astype(BF16)
        for g in range(nG):
            sl = slice(g * GT, (g + 1) * GT)
            zr = jnp.dot(xcb[:, sl], wr_ref[g], preferred_element_type=F32) + br_ref[:, sl]
            zi = jnp.dot(xcb[:, sl], wi_ref[g], preferred_element_type=F32) + bi_ref[:, sl]
            r_ref[:, sl] = _sigmoid(zr)
            i_ref[:, sl] = _sigmoid(zi)
        r = r_ref[...]
        a, _, mult = _decay_terms(r, _log_sigmoid(lam_ref[...]))
        a_scr[...] = a
        hs_ref[...] = mult * (i_ref[...] * xc)

        sub = lax.broadcasted_iota(jnp.int32, (SUBLANES, D), 0)

        def step(j, h):
            rows = pl.ds(pl.multiple_of(j * SUBLANES, SUBLANES), SUBLANES)
            A, B = _group_scan(a_scr[rows, :], hs_ref[rows, :], sub, False)
            hg = A * h + B
            hs_ref[rows, :] = hg
            return hg[SUBLANES - 1:SUBLANES, :]

        hcar[...] = lax.fori_loop(0, T // SUBLANES, step, hcar[...], unroll=2)
        y_ref[...] = (_gelu(gb_ref[...]) * hs_ref[...]).astype(BF16)

    row = pl.BlockSpec((T, D), lambda i: (i, 0))
    vec = pl.BlockSpec((1, D), lambda i: (0, 0))
    bd = pl.BlockSpec((nG, GT, GT), lambda i: (0, 0, 0))
    f32o = jax.ShapeDtypeStruct((S, D), F32)
    return _hosted_call(
        body, "lru_fwd", (S // T,),
        [row, pl.BlockSpec((T, D), lambda i: (i, 1)), pl.BlockSpec((CONV_WIDTH, D), lambda i: (0, 0)), vec,
         bd, vec, bd, vec, vec],
        [row, row, row, row, row], [jax.ShapeDtypeStruct((S, D), BF16), f32o, f32o, f32o, f32o],
        [pltpu.VMEM((T + SUBLANES, D), F32), pltpu.VMEM((T, D), F32), pltpu.VMEM((1, D), F32)],
        (u0, u0, conv_w, conv_b, wr_bd, b_r, wi_bd, b_i, lam), ("arbitrary",), plan)


def _lru_bwd(dy, u0, xc, r, ig, hs, conv_w, wr_bd, wi_bd, lam, S, D, plan=None):
    T = _pick(S, (128,))
    nT = S // T
    GT = wr_bd.shape[-1]
    nG = D // GT
    W = CONV_WIDTH

    def body(dy_ref, gb_ref, xb_ref, xbp_ref, xc_ref, r_ref, i_ref, hs_ref, hsp_ref, cw_ref, wr_ref, wi_ref, lam_ref,
             du_ref, dcw_ref, dcb_ref, dlam_ref, dbr_ref, dbi_ref, dwr_ref, dwi_ref,
             a_scr, dh_scr, exth, extx, extd, dxc_scr, dz_scr, carry):
        step = pl.program_id(0)
        first_tile = step == nT - 1

        @pl.when(step == 0)
        def _():
            for ref in (dcw_ref, dcb_ref, dlam_ref, dbr_ref, dbi_ref, dwr_ref, dwi_ref, carry):
                ref[...] = jnp.zeros_like(ref)
            extd[T:T + SUBLANES, :] = jnp.zeros((SUBLANES, D), F32)

        hs = hs_ref[...]
        dy = dy_ref[...]
        g, dgelu = _gelu_and_grad(gb_ref[...])
        du_ref[:, 0:D] = (dy * hs * dgelu).astype(BF16)
        r = r_ref[...]
        lam = lam_ref[...]
        ls = _log_sigmoid(lam)
        a, a2, mult = _decay_terms(r, ls)
        a_scr[...] = a
        dh_scr[...] = dy * g

        sub = lax.broadcasted_iota(jnp.int32, (SUBLANES, D), 0)

        def rstep(j, c):
            rows = pl.ds(pl.multiple_of(T - (j + 1) * SUBLANES, SUBLANES), SUBLANES)
            a_g = a_scr[rows, :]
            a_next = jnp.where(sub < SUBLANES - 1, pltpu.roll(a_g, SUBLANES - 1, 0), 1.0)
            A, B = _group_scan(a_next, dh_scr[rows, :], sub, True)
            d = A * c + B
            dh_scr[rows, :] = d
            return a_g[0:1, :] * d[0:1, :]

        carry[...] = lax.fori_loop(0, T // SUBLANES, rstep, carry[...], unroll=2)
        dh = dh_scr[...]
        keep = jnp.where(first_tile, 0.0, 1.0)
        exth[0:SUBLANES, :] = hsp_ref[...] * keep
        exth[SUBLANES:SUBLANES + T, :] = hs
        hprev = exth[pl.ds(SUBLANES - 1, T), :]
        xc = xc_ref[...]
        ig = i_ref[...]
        da = dh * hprev
        dmult = dh * (ig * xc)
        dla = da * a - dmult * (a2 / mult)
        dlam_ref[...] += jnp.sum(dla * r, axis=0, keepdims=True) * (LRU_C * _sigmoid(-lam))
        dzr = (dla * (LRU_C * ls)) * (r * (1.0 - r))
        dzi = (dh * (mult * xc)) * (ig * (1.0 - ig))
        dbr_ref[...] += jnp.sum(dzr, axis=0, keepdims=True)
        dbi_ref[...] += jnp.sum(dzi, axis=0, keepdims=True)
        dxc_scr[...] = dh * (mult * ig)
        xcb = xc.astype(BF16)
        dz_scr[0] = dzr.astype(BF16)
        dz_scr[1] = dzi.astype(BF16)
        nt_dims = (((1,), (1,)), ((), ()))
        tn_dims = (((0,), (0,)), ((), ()))
        for gq in range(nG):
            sl = slice(gq * GT, (gq + 1) * GT)
            zr_g = dz_scr[0, :, sl]
            zi_g = dz_scr[1, :, sl]
            dxc_scr[:, sl] += (lax.dot_general(zr_g, wr_ref[gq], nt_dims, preferred_element_type=F32)
                               + lax.dot_general(zi_g, wi_ref[gq], nt_dims, preferred_element_type=F32))
            dwr_ref[gq] += lax.dot_general(xcb[:, sl], zr_g, tn_dims, preferred_element_type=F32)
            dwi_ref[gq] += lax.dot_general(xcb[:, sl], zi_g, tn_dims, preferred_element_type=F32)
        dxc = dxc_scr[...]
        dcb_ref[...] += jnp.sum(dxc, axis=0, keepdims=True)
        extx[0:SUBLANES, :] = xbp_ref[...] * keep
        extx[SUBLANES:SUBLANES + T, :] = xb_ref[...]
        extd[0:T, :] = dxc
        dxb = jnp.zeros((T, D), F32)
        for k in range(W):
            dxb = dxb + extd[pl.ds(W - 1 - k, T), :] * cw_ref[k:k + 1, :]
            dcw_ref[k:k + 1, :] += jnp.sum(dxc * extx[pl.ds(SUBLANES - (W - 1) + k, T), :], axis=0, keepdims=True)
        extd[T:T + SUBLANES, :] = dxc[0:SUBLANES, :]
        du_ref[:, D:2 * D] = dxb.astype(BF16)

    rev = lambda i: nT - 1 - i
    tpb = T // SUBLANES
    prev8 = lambda i: jnp.maximum(rev(i) * tpb - 1, 0)
    row = pl.BlockSpec((T, D), lambda i: (rev(i), 0))
    vec = pl.BlockSpec((1, D), lambda i: (0, 0))
    bd = pl.BlockSpec((nG, GT, GT), lambda i: (0, 0, 0))
    vec_o = jax.ShapeDtypeStruct((1, D), F32)
    bd_o = jax.ShapeDtypeStruct((nG, GT, GT), F32)
    return _hosted_call(
        body, "lru_bwd", (nT,),
        [row, row, pl.BlockSpec((T, D), lambda i: (rev(i), 1)), pl.BlockSpec((SUBLANES, D), lambda i: (prev8(i), 1)),
         row, row, row, row, pl.BlockSpec((SUBLANES, D), lambda i: (prev8(i), 0)),
         pl.BlockSpec((W, D), lambda i: (0, 0)), bd, bd, vec],
        [pl.BlockSpec((T, 2 * D), lambda i: (rev(i), 0)), pl.BlockSpec((W, D), lambda i: (0, 0)),
         vec, vec, vec, vec, bd, bd],
        [jax.ShapeDtypeStruct((S, 2 * D), BF16), jax.ShapeDtypeStruct((W, D), F32), vec_o, vec_o, vec_o, vec_o, bd_o, bd_o],
        [pltpu.VMEM((T, D), F32), pltpu.VMEM((T, D), F32), pltpu.VMEM((T + SUBLANES, D), F32),
         pltpu.VMEM((T + SUBLANES, D), F32), pltpu.VMEM((T + SUBLANES, D), F32),
         pltpu.VMEM((T, D), F32), pltpu.VMEM((2, T, D), BF16), pltpu.VMEM((1, D), F32)],
        (dy, u0, u0, u0, xc, r, ig, hs, hs, conv_w, wr_bd, wi_bd, lam), ("arbitrary",), plan)


AUG_ROWS = 16
HEAD_ROWS = 128
LSE_ROW = HEAD_DIM + 6
ONES_ROW_Q = HEAD_DIM + 3
ONES_COL_K = HEAD_DIM
ONES_ROW_V = HEAD_DIM
PREP_LANES = 512
HEAD_UNROLL = 4


def _split3(x):
    b1 = x.astype(BF16).astype(F32)
    r = x - b1
    b2 = r.astype(BF16).astype(F32)
    return b1, b2, r - b2


def _head_block(x, aug, T):
    row = lax.broadcasted_iota(jnp.int32, (AUG_ROWS, T), 0)
    blk = jnp.zeros((AUG_ROWS, T), F32)
    for i, e in enumerate(aug):
        blk = jnp.where(row == i, e, blk)
    return jnp.concatenate([x, blk, jnp.zeros((HEAD_ROWS - HEAD_DIM - AUG_ROWS, T), F32)], axis=0)


def _tri_matrix(lower):
    i = np.arange(LANES)
    m = (i[:, None] >= i[None, :]) if lower else (i[:, None] <= i[None, :])
    return jnp.asarray(m.astype(np.float32), BF16)


def _lane_cumsum(x, tri_ref, carry, reverse):
    n = x.shape[1] // LANES
    tri = tri_ref[...]
    out = [None] * n
    for j in (range(n - 1, -1, -1) if reverse else range(n)):
        cs = carry
        for part in _split3(x[:, j * LANES:(j + 1) * LANES]):
            cs = cs + jnp.dot(part.astype(BF16), tri, preferred_element_type=F32)
        out[j] = cs
        carry = cs[:, 0:1] if reverse else cs[:, LANES - 1:LANES]
    return jnp.concatenate(out, axis=1), carry


def _head_rows(h):
    return pl.ds(pl.multiple_of(h * HEAD_DIM, HEAD_DIM), HEAD_DIM)


def _fox_prep(ut, b_f, qg, kg, S, D, tq):
    H = D // HEAD_DIM
    T = min(tq, PREP_LANES)
    per = tq // T
    scale = HEAD_DIM ** -0.5

    def body(q_ref, k_ref, v_ref, f_ref, bf_ref, qg_ref, kg_ref, tri_ref,
             qat_ref, kat_ref, vat_ref, ka_ref, c_scr, ccar):
        @pl.when(pl.program_id(0) == 0)
        def _():
            ccar[...] = jnp.zeros_like(ccar)

        c, carry = _lane_cumsum(_log_sigmoid(f_ref[...] + bf_ref[...]), tri_ref, ccar[...], False)
        c_scr[...] = c
        ccar[...] = carry

        def head(h, _):
            rows = _head_rows(h)
            c1, c2, c3 = _split3(c_scr[pl.ds(h, 1), :])

            def normed(src, gain, mul):
                x = src[rows, :]
                rs = lax.rsqrt(jnp.mean(x * x, axis=0, keepdims=True) + EPS)
                return ((x * rs) * gain[rows, :]) * mul

            qat_ref[h] = _head_block(normed(q_ref, qg_ref, scale), [c1, c2, c3, 1.0, 1.0, 1.0], T).astype(BF16)
            kb = _head_block(normed(k_ref, kg_ref, 1.0), [1.0, 1.0, 1.0, -c1, -c2, -c3, 1.0, 1.0, 1.0], T)
            kat_ref[h] = kb.astype(BF16)
            ka_ref[h] = kb.T.astype(BF16)
            vat_ref[h] = _head_block(v_ref[rows, :], [1.0, 1.0, 1.0], T).astype(BF16)
            return 0

        lax.fori_loop(0, H, head, 0, unroll=min(HEAD_UNROLL, H))

    part = lambda j: pl.BlockSpec((D, T), lambda i: (j, i))
    colv = lambda n: pl.BlockSpec((n, 1), lambda i: (0, 0))
    tmaj = lambda r: pl.BlockSpec((H, None, r, T), lambda i: (0, i // per, 0, i % per))
    norm = pl.BlockSpec((H, T, HEAD_ROWS), lambda i: (0, i, 0))
    tshape = lambda r: jax.ShapeDtypeStruct((H, S // tq, r, tq), BF16)
    nshape = jax.ShapeDtypeStruct((H, S, HEAD_ROWS), BF16)
    return pl.pallas_call(
        body, name="fox_prep", grid=(S // T,),
        in_specs=[part(0), part(1), part(2), pl.BlockSpec((LANES, T), lambda i: (3 * D // LANES, i)),
                  colv(LANES), colv(D), colv(D), pl.BlockSpec((LANES, LANES), lambda i: (0, 0))],
        out_specs=[tmaj(HEAD_ROWS), tmaj(HEAD_ROWS), tmaj(HEAD_ROWS), norm],
        out_shape=[tshape(HEAD_ROWS), tshape(HEAD_ROWS), tshape(HEAD_ROWS), nshape],
        scratch_shapes=[pltpu.VMEM((LANES, T), F32), pltpu.VMEM((LANES, 1), F32)],
        compiler_params=_params(("arbitrary",)),
    )(ut, ut, ut, ut, b_f, qg, kg, _tri_matrix(False))


def _fox_bwd_prep(dot, ot, lse, qat, S, D, tq, plan=None):
    H = D // HEAD_DIM
    T = min(tq, PREP_LANES)
    per = tq // T

    def body(do_ref, o_ref, lse_ref, qat_ref, doat_ref, doa_ref, qat1_ref, qa1_ref):
        row = lax.broadcasted_iota(jnp.int32, (HEAD_ROWS, T), 0)

        def head(h, _):
            rows = _head_rows(h)
            do = do_ref[rows, :].astype(F32)
            delta = jnp.sum(do * o_ref[rows, :], axis=0, keepdims=True)
            db = _head_block(do, list(_split3(-delta)), T)
            doat_ref[h] = db.astype(BF16)
            doa_ref[h] = db.T.astype(BF16)
            qb = qat_ref[h].astype(F32)
            for i, e in enumerate(_split3(-lse_ref[h])):
                qb = jnp.where(row == LSE_ROW + i, e, qb)
            qat1_ref[h] = qb.astype(BF16)
            qa1_ref[h] = qb.T.astype(BF16)
            return 0

        lax.fori_loop(0, H, head, 0, unroll=min(HEAD_UNROLL, H))

    chan = pl.BlockSpec((D, T), lambda i: (0, i))
    tmaj = pl.BlockSpec((H, None, HEAD_ROWS, T), lambda i: (0, i // per, 0, i % per))
    norm = pl.BlockSpec((H, T, HEAD_ROWS), lambda i: (0, i, 0))
    tshape = jax.ShapeDtypeStruct((H, S // tq, HEAD_ROWS, tq), BF16)
    nshape = jax.ShapeDtypeStruct((H, S, HEAD_ROWS), BF16)
    return _hosted_call(body, "fox_bwd_prep", (S // T,), [chan, chan, pl.BlockSpec((H, 1, T), lambda i: (0, 0, i)), tmaj],
                        [tmaj, norm, tmaj, norm], [tshape, nshape, tshape, nshape], [], (dot, ot, lse, qat),
                        ("arbitrary",), plan)


def _causal(s, k_axis):
    t = min(s.shape)
    ki = lax.broadcasted_iota(jnp.int32, s.shape, k_axis) - (s.shape[k_axis] - t)
    qi = lax.broadcasted_iota(jnp.int32, s.shape, 1 - k_axis)
    return jnp.where(ki <= qi, s, NEG_INF)


def _attn_forward(ka, qat, vat, S, D, tq, plan=None):
    H = D // HEAD_DIM
    nq = S // tq
    G = 4

    def body(ka_ref, qat_ref, vat_ref, o_ref, o32_ref, lse_ref, m_scr, acc_scr):
        qi = pl.program_id(1)
        m_scr[...] = jnp.full_like(m_scr, NEG_INF)
        acc_scr[...] = jnp.zeros_like(acc_scr)

        def span(k0, n, diagonal):
            keys = pl.ds(pl.multiple_of(k0 * tq, tq), n * tq)
            s = [jnp.dot(ka_ref[g, keys, :], qat_ref[g], preferred_element_type=F32) for g in range(G)]
            if diagonal:
                s = [_causal(sg, 0) for sg in s]
            m_prev = [m_scr[g] for g in range(G)]
            m_new = [jnp.maximum(m_prev[g], jnp.max(s[g], axis=0, keepdims=True)) for g in range(G)]
            p = [jnp.exp(s[g] - m_new[g]).astype(BF16) for g in range(G)]
            for g in range(G):
                upd = jnp.dot(vat_ref[g, k0], p[g][0:tq], preferred_element_type=F32)
                for i in range(1, n):
                    upd = upd + jnp.dot(vat_ref[g, k0 + i], p[g][i * tq:(i + 1) * tq], preferred_element_type=F32)
                acc_scr[g] = jnp.exp(m_prev[g] - m_new[g]) * acc_scr[g] + upd
                m_scr[g] = m_new[g]

        def off_diagonal_pair(j, _):
            span(2 * j, 2, False)
            return 0

        lax.fori_loop(0, qi // 2, off_diagonal_pair, 0)
        pl.when(qi % 2 == 1)(lambda: span(qi - 1, 2, True))
        pl.when(qi % 2 == 0)(lambda: span(qi, 1, True))
        for g in range(G):
            l = acc_scr[g, ONES_ROW_V:ONES_ROW_V + 1, :]
            o = acc_scr[g, 0:HEAD_DIM, :] / l
            o_ref[g * HEAD_DIM:(g + 1) * HEAD_DIM, :] = o.astype(BF16)
            o32_ref[g * HEAD_DIM:(g + 1) * HEAD_DIM, :] = o
            lse_ref[g] = m_scr[g] + jnp.log(l)

    chan = pl.BlockSpec((G * HEAD_DIM, tq), lambda h, i: (h, i))
    stat = pl.BlockSpec((G, 1, tq), lambda h, i: (h, 0, i))
    return _hosted_call(
        body, "attn_forward", (H // G, nq),
        [pl.BlockSpec((G, S, HEAD_ROWS), lambda h, i: (h, 0, 0)),
         pl.BlockSpec((G, None, HEAD_ROWS, tq), lambda h, i: (h, i, 0, 0)),
         pl.BlockSpec((G, nq, HEAD_ROWS, tq), lambda h, i: (h, 0, 0, 0))],
        [chan, chan, stat],
        [jax.ShapeDtypeStruct((D, S), BF16), jax.ShapeDtypeStruct((D, S), F32), jax.ShapeDtypeStruct((H, 1, S), F32)],
        [pltpu.VMEM((G, 1, tq), F32), pltpu.VMEM((G, HEAD_ROWS, tq), F32)],
        (ka, qat, vat), ("arbitrary", "arbitrary"), plan)


def _attn_backward(qa, doa, qat, doat, ka, kat, vat, S, D, tq, plan=None):
    H = D // HEAD_DIM
    nq = S // tq
    G = 2

    def body(qa_ref, doa_ref, qat_ref, doat_ref, ka_ref, kat_ref, vat_ref, dq_ref, dk_ref, dv_ref, dk_scr, dv_scr):
        ki = pl.program_id(1)

        @pl.when(ki == 0)
        def _():
            dq_ref[...] = jnp.zeros_like(dq_ref)

        dk_scr[...] = jnp.zeros_like(dk_scr)
        dv_scr[...] = jnp.zeros_like(dv_scr)

        def span(q0, n, diagonal):
            rows = pl.ds(pl.multiple_of(q0 * tq, tq), n * tq)
            s = [jnp.dot(qa_ref[g, rows, :], kat_ref[g], preferred_element_type=F32) for g in range(G)]
            if diagonal:
                s = [_causal(sg, 1) for sg in s]
            p = [jnp.exp(sg) for sg in s]
            ds = [(p[g] * jnp.dot(doa_ref[g, rows, :], vat_ref[g], preferred_element_type=F32)).astype(BF16)
                  for g in range(G)]
            p = [pg.astype(BF16) for pg in p]
            for g in range(G):
                for i in range(n):
                    part = slice(i * tq, (i + 1) * tq)
                    dv_scr[g] += jnp.dot(doat_ref[g, q0 + i, 0:HEAD_DIM, :], p[g][part], preferred_element_type=F32)
                    dk_scr[g] += jnp.dot(qat_ref[g, q0 + i], ds[g][part], preferred_element_type=F32)
                dq_ref[g, rows, :] += jnp.dot(ds[g], ka_ref[g], preferred_element_type=F32)

        n_off = nq - 1 - ki
        odd = n_off % 2

        def off_diagonal_pair(j, _):
            span(ki + 1 + odd + 2 * j, 2, False)
            return 0

        pl.when(odd == 1)(lambda: span(ki, 2, True))
        pl.when(odd == 0)(lambda: span(ki, 1, True))
        lax.fori_loop(0, n_off // 2, off_diagonal_pair, 0)
        dk_ref[...] = dk_scr[...]
        for g in range(G):
            dv_ref[g * HEAD_DIM:(g + 1) * HEAD_DIM, :] = dv_scr[g].astype(BF16)

    whole = pl.BlockSpec((G, S, HEAD_ROWS), lambda h, i: (h, 0, 0))
    tiles = pl.BlockSpec((G, nq, HEAD_ROWS, tq), lambda h, i: (h, 0, 0, 0))
    one = pl.BlockSpec((G, None, HEAD_ROWS, tq), lambda h, i: (h, i, 0, 0))
    return _hosted_call(
        body, "attn_backward", (H // G, nq),
        [whole, whole, tiles, tiles, pl.BlockSpec((G, tq, HEAD_ROWS), lambda h, i: (h, i, 0)), one, one],
        [whole, pl.BlockSpec((G, HEAD_ROWS, tq), lambda h, i: (h, 0, i)),
         pl.BlockSpec((G * HEAD_DIM, tq), lambda h, i: (h, i))],
        [jax.ShapeDtypeStruct((H, S, HEAD_ROWS), F32), jax.ShapeDtypeStruct((H, HEAD_ROWS, S), F32),
         jax.ShapeDtypeStruct((D, S), BF16)],
        [pltpu.VMEM((G, HEAD_ROWS, tq), F32), pltpu.VMEM((G, HEAD_DIM, tq), F32)],
        (qa, doa, qat, doat, ka, kat, vat), ("arbitrary", "arbitrary"), plan)


def _fox_prep_bwd(ut, dq, dkt, dvt, b_f, qg, kg, S, D, tq):
    H = D // HEAD_DIM
    T = min(tq, PREP_LANES)
    nT = S // T
    NU = 3 * D + LANES
    scale = HEAD_DIM ** -0.5

    def body(q_ref, k_ref, f_ref, dq_ref, dk_ref, dv_ref, bf_ref, qg_ref, kg_ref, tri_ref,
             du_ref, dbf_ref, dqg_ref, dkg_ref, gq_acc, gk_acc, fcar, dc_scr):
        step = pl.program_id(0)

        @pl.when(step == 0)
        def _():
            for ref in (gq_acc, gk_acc, fcar, dbf_ref):
                ref[...] = jnp.zeros_like(ref)

        dc_scr[...] = jnp.zeros_like(dc_scr)

        def head(h, _):
            rows = _head_rows(h)
            dqb = dq_ref[h].T
            dkb = dk_ref[h]
            dc_scr[pl.ds(h, 1), :] = dqb[ONES_COL_K:ONES_COL_K + 1, :] - dkb[ONES_ROW_Q:ONES_ROW_Q + 1, :]
            for src, dsrc, gain, acc, mul, base in ((q_ref, dqb, qg_ref, gq_acc, scale, 0),
                                                    (k_ref, dkb, kg_ref, gk_acc, 1.0, D)):
                x = src[rows, :]
                rs = lax.rsqrt(jnp.mean(x * x, axis=0, keepdims=True) + EPS)
                xhat = x * rs
                dn = dsrc[0:HEAD_DIM, :] * mul
                acc[rows, :] += jnp.sum(dn * xhat, axis=1, keepdims=True)
                dxh = dn * gain[rows, :]
                dx = rs * (dxh - xhat * jnp.mean(dxh * xhat, axis=0, keepdims=True))
                du_ref[pl.ds(pl.multiple_of(base + h * HEAD_DIM, HEAD_DIM), HEAD_DIM), :] = dx.astype(BF16)
            return 0

        lax.fori_loop(0, H, head, 0, unroll=min(HEAD_UNROLL, H))
        du_ref[2 * D:3 * D, :] = dv_ref[...]
        dlf, carry = _lane_cumsum(dc_scr[...], tri_ref, fcar[...], True)
        fcar[...] = carry
        dfl = dlf * _sigmoid(-(f_ref[...] + bf_ref[...]))
        dbf_ref[...] += jnp.sum(dfl, axis=1, keepdims=True)
        du_ref[3 * D:NU, :] = dfl.astype(BF16)

        @pl.when(step == nT - 1)
        def _():
            for acc, ref in ((gq_acc, dqg_ref), (gk_acc, dkg_ref)):
                tot = jnp.zeros((HEAD_DIM, 1), F32)
                for h in range(H):
                    tot = tot + acc[h * HEAD_DIM:(h + 1) * HEAD_DIM, :]
                ref[...] = tot

    rev = lambda i: nT - 1 - i
    part = lambda j: pl.BlockSpec((D, T), lambda i: (j, rev(i)))
    colv = lambda n: pl.BlockSpec((n, 1), lambda i: (0, 0))
    return pl.pallas_call(
        body, name="fox_prep_bwd", grid=(nT,),
        in_specs=[part(0), part(1), pl.BlockSpec((LANES, T), lambda i: (3 * D // LANES, rev(i))),
                  pl.BlockSpec((H, T, HEAD_ROWS), lambda i: (0, rev(i), 0)),
                  pl.BlockSpec((H, HEAD_ROWS, T), lambda i: (0, 0, rev(i))), pl.BlockSpec((D, T), lambda i: (0, rev(i))),
                  colv(LANES), colv(D), colv(D), pl.BlockSpec((LANES, LANES), lambda i: (0, 0))],
        out_specs=[pl.BlockSpec((NU, T), lambda i: (0, rev(i))), colv(LANES), colv(HEAD_DIM), colv(HEAD_DIM)],
        out_shape=[jax.ShapeDtypeStruct((NU, S), BF16), jax.ShapeDtypeStruct((LANES, 1), F32),
                   jax.ShapeDtypeStruct((HEAD_DIM, 1), F32), jax.ShapeDtypeStruct((HEAD_DIM, 1), F32)],
        scratch_shapes=[pltpu.VMEM((D, 1), F32), pltpu.VMEM((D, 1), F32), pltpu.VMEM((LANES, 1), F32),
                        pltpu.VMEM((LANES, T), F32)],
        compiler_params=_params(("arbitrary",)),
    )(ut, ut, ut, dq, dkt, dvt, b_f, qg, kg, _tri_matrix(True))


def _block_diag_tiles(w):
    n = w.shape[0]
    per = min(MXU_DIM, n * LRU_BLOCK_DIM) // LRU_BLOCK_DIM
    eye = jnp.eye(per, dtype=w.dtype)
    w5 = w.reshape(n // per, per, LRU_BLOCK_DIM, 1, LRU_BLOCK_DIM) * eye[None, :, None, :, None]
    return w5.reshape(n // per, per * LRU_BLOCK_DIM, per * LRU_BLOCK_DIM).astype(BF16)


def _block_diag_extract(t, n):
    per = t.shape[-1] // LRU_BLOCK_DIM
    eye = jnp.eye(per, dtype=t.dtype)
    t5 = t.reshape(n // per, per, LRU_BLOCK_DIM, per, LRU_BLOCK_DIM) * eye[None, :, None, :, None]
    return t5.sum(axis=3).reshape(n, LRU_BLOCK_DIM, LRU_BLOCK_DIM)


def _local_step(x, tgt, small, wv, grad_view, comm=None):
    S, D = x.shape
    F = 4 * D
    H = D // HEAD_DIM
    nblk = D // LRU_BLOCK_DIM
    NU = 3 * D + LANES
    tq = max(LANES, min(512, S // 4))
    assert S % tq == 0
    vec = lambda a: a.reshape(1, -1).astype(F32)
    col = lambda a: a.reshape(-1, 1).astype(F32)
    mix_g, mlp_g = small["mix_norm"], small["mlp_norm"]
    conv_b = vec(small["lru_conv_b"])
    wr_bd, wi_bd = _block_diag_tiles(small["lru_w_r"][0]), _block_diag_tiles(small["lru_w_i"][0])
    b_r, b_i, lam = vec(small["lru_b_r"]), vec(small["lru_b_i"]), vec(small["lru_lambda"])
    b_f = jnp.pad(col(small["fox_b_f"]), ((0, LANES - H), (0, 0)))
    qg, kg = jnp.tile(col(small["fox_q_gain"]), (H, 1)), jnp.tile(col(small["fox_k_gain"]), (H, 1))
    X = lambda a: _View(a)
    grads = {}
    gout = functools.partial(grad_view, grads)

    def hosted(name, fn, *args):
        plan = comm.before(name, grads) if comm is not None else None
        res, side = fn(*args, plan=plan)
        if plan is not None:
            comm.after(name, side, wv)
        return res

    def hosted_mm(name, *args, **kw):
        plan = comm.before(name, grads) if comm is not None else None
        if plan is None:
            return _matmul(name, *args, **kw)
        res, side = _matmul(name, *args, plan=plan, **kw)
        comm.after(name, side, wv)
        return res

    two = lambda: [_fresh(S, D, F32), _fresh(S, D, BF16)]

    def mlp_up(l, hm):
        return hosted_mm(f"mlp{l}_up", X(hm), wv[f"w1_{l}"], S, F, D, outs=[_fresh(S, F, BF16)], epilogue=_ep_relu2)[0]

    def mlp_bwd(l, xin, hm, act, d, db):
        (dz,) = hosted_mm(f"mlp{l}_dact", X(db), wv[f"w2_{l}"], S, F, D, tb=True, outs=[_fresh(S, F, BF16)],
                          epilogue=_ep_drelu2, extras=[X(act)])
        (grads[f"w2_{l}"],) = _matmul(f"mlp{l}_dw2", X(act), X(db), F, D, S, ta=True, outs=[gout(f"w2_{l}")],
                                      epilogue=_ep_store)
        (grads[f"w1_{l}"],) = _matmul(f"mlp{l}_dw1", X(hm), X(dz), D, F, S, ta=True, outs=[gout(f"w1_{l}")],
                                      epilogue=_ep_store)
        return _matmul(f"mlp{l}_dhm", X(dz), wv[f"w1_{l}"], S, D, F, tb=True, outs=two(), n_sums=1,
                       epilogue=_ep_norm_bwd, extras=[X(xin), X(d)], vecs=[mlp_g[l:l + 1]])

    (h0,) = hosted("mix0_norm", _rms_fwd, "mix0_norm", x, mix_g[0:1], S, D)
    (u0,) = hosted_mm("lru_in", X(h0), wv["lru_in"], S, 2 * D, D, outs=[_fresh(S, 2 * D, F32)], epilogue=_ep_store)
    conv_w = small["conv_w"]
    y, xc, r, ig, hs = hosted("lru_fwd", _lru_fwd, u0, conv_w, conv_b, wr_bd, b_r, wi_bd, b_i, lam, S, D)
    x1, hm0 = _matmul("lru_out", X(y), wv["lru_out"], S, D, D, outs=two(), epilogue=_ep_resid_norm, extras=[X(x)],
                      vecs=[mlp_g[0:1]])
    act0 = mlp_up(0, hm0)
    x2, h1 = hosted_mm("mlp0_down", X(act0), wv["w2_0"], S, D, F, outs=two(), epilogue=_ep_resid_norm, extras=[X(x1)],
                       vecs=[mix_g[1:2]])
    (u1,) = _matmul("fox_in", wv["fox_in"], X(h1), NU, S, D, tb=True, outs=[_fresh(NU, S, F32)], epilogue=_ep_store)
    qat, kat, vat, ka = _fox_prep(u1, b_f, qg, kg, S, D, tq)
    o, o32, lse = hosted("attn_forward", _attn_forward, ka, qat, vat, S, D, tq)
    x3, hm1 = _matmul("fox_out", X(o), wv["fox_out"], S, D, D, ta=True, outs=two(), epilogue=_ep_resid_norm,
                      extras=[X(x2)], vecs=[mlp_g[1:2]])
    act1 = mlp_up(1, hm1)
    (x4,) = _matmul("mlp1_down", X(act1), wv["w2_1"], S, D, F, outs=[_fresh(S, D, F32)], epilogue=_ep_resid,
                    extras=[X(x3)])
    loss, d4, d4b = _loss_head(x4, tgt, S, D)

    d3, d3b, dg_mlp1 = mlp_bwd(1, x3, hm1, act1, d4, d4b)
    (do,) = _matmul("fox_dout", wv["fox_out"], X(d3b), D, S, D, tb=True, outs=[_fresh(D, S, BF16)], epilogue=_ep_store)
    (grads["fox_out"],) = _matmul("fox_dwout", X(o), X(d3b), D, D, S, outs=[gout("fox_out")], epilogue=_ep_store)
    doat, doa, qat1, qa1 = hosted("fox_bwd_prep", _fox_bwd_prep, do, o32, lse, qat, S, D, tq)
    dqn, dkn, dv = hosted("attn_backward", _attn_backward, qa1, doa, qat1, doat, ka, kat, vat, S, D, tq)
    du1, dbf, dqg, dkg = _fox_prep_bwd(u1, dqn, dkn, dv, b_f, qg, kg, S, D, tq)
    (grads["fox_in"],) = _matmul("fox_dwin", X(du1), X(h1), NU, D, S, outs=[gout("fox_in")], epilogue=_ep_store)
    d2, d2b, dg_mix1 = hosted_mm("fox_dh", X(du1), wv["fox_in"], S, D, NU, ta=True, outs=two(), n_sums=1,
                               epilogue=_ep_norm_bwd, extras=[X(x2), X(d3)], vecs=[mix_g[1:2]])
    d1, d1b, dg_mlp0 = mlp_bwd(0, x1, hm0, act0, d2, d2b)
    (grads["lru_out"],) = _matmul("lru_dwout", X(y), X(d1b), D, D, S, ta=True, outs=[gout("lru_out")],
                                  epilogue=_ep_store)
    (dy,) = hosted_mm("lru_dout", X(d1b), wv["lru_out"], S, D, D, tb=True, outs=[_fresh(S, D, F32)],
                      epilogue=_ep_store)
    du0, dcw, dcb, dlam, dbr, dbi, dwr, dwi = hosted("lru_bwd", _lru_bwd, dy, u0, xc, r, ig, hs, conv_w, wr_bd, wi_bd,
                                                     lam, S, D)
    (grads["lru_in"],) = _matmul("lru_dwin", X(h0), X(du0), D, 2 * D, S, ta=True, outs=[gout("lru_in")],
                                 epilogue=_ep_store)
    gx, dg_mix0 = hosted_mm("lru_dh", X(du0), wv["lru_in"], S, D, 2 * D, tb=True, outs=[_fresh(S, D, F32)], n_sums=1,
                            epilogue=lambda *a: _ep_norm_bwd(*a)[::2], extras=[X(x), X(d1)], vecs=[mix_g[0:1]])

    grads.update(
        mix_norm=jnp.concatenate([dg_mix0, dg_mix1], axis=0), mlp_norm=jnp.concatenate([dg_mlp0, dg_mlp1], axis=0),
        conv_w=dcw, lru_conv_b=dcb, lru_w_r=_block_diag_extract(dwr, nblk)[None], lru_b_r=dbr.reshape(1, nblk, -1),
        lru_w_i=_block_diag_extract(dwi, nblk)[None], lru_b_i=dbi.reshape(1, nblk, -1), lru_lambda=dlam,
        fox_b_f=dbf[:H].reshape(1, H), fox_q_gain=dqg.reshape(1, -1), fox_k_gain=dkg.reshape(1, -1))
    return loss, gx, grads


def _place():
    x, y, c = lax.axis_index("x"), lax.axis_index("y"), lax.axis_index("c")
    chips = [(1 - x, y), (x, 1 - y), (1 - x, 1 - y)]
    return x, y, c, 2 * x + y, chips


BOUNCE_BYTES = 1 << 20


def _bounce_shape(rows, cols, dtype):
    chunk = rows
    while chunk % 2 == 0 and chunk > 16 and chunk * cols * jnp.dtype(dtype).itemsize > BOUNCE_BYTES:
        chunk //= 2
    return pltpu.VMEM((2, chunk, cols), dtype)


def _bounce_copy(src, dst, buf, sem):
    chunk = buf.shape[1]
    n = src.shape[0] // chunk
    cin = lambda i: pltpu.make_async_copy(src.at[pl.ds(i * chunk, chunk)], buf.at[i % 2], sem.at[i % 2])
    cout = lambda i: pltpu.make_async_copy(buf.at[i % 2], dst.at[pl.ds(i * chunk, chunk)], sem.at[2 + i % 2])
    cin(0).start()
    for i in range(n):
        cin(i).wait()
        if i + 1 < n:
            if i >= 1:
                cout(i - 1).wait()
            cin(i + 1).start()
        cout(i).start()
    if n >= 2:
        cout(n - 2).wait()
    cout(n - 1).wait()


def _hbm_call(body, name, arrays, out_shape, n_dma_sems, bounce=()):
    scratch = [pltpu.SemaphoreType.DMA((k,)) for k in n_dma_sems]
    for rows, cols, dtype in bounce:
        scratch += [_bounce_shape(rows, cols, dtype), pltpu.SemaphoreType.DMA((4,))]
    return pl.pallas_call(
        body, name=name, in_specs=[ANY] * len(arrays), out_specs=[ANY] * len(out_shape), out_shape=out_shape,
        scratch_shapes=scratch,
        compiler_params=pltpu.CompilerParams(has_side_effects=True, vmem_limit_bytes=VMEM_LIMIT),
    )(*arrays)


class _Gather:
    def __init__(self, shards):
        n = self.n = len(shards)
        self.operands = list(shards)
        self.out_shape = [jax.ShapeDtypeStruct((N_CHIPS,) + tuple(a.shape), a.dtype) for a in shards]
        self.scratch = [pltpu.SemaphoreType.DMA((3 * n,)) for _ in range(4)]
        for a in shards:
            self.scratch += [_bounce_shape(a.shape[0], a.shape[1], a.dtype), pltpu.SemaphoreType.DMA((4,))]

    def _copies(self, ins, outs, scr):
        send, recv, fsend, frecv = scr[:4]
        x, y, c, s, chips = _place()

        def rows(a, chip_idx, which):
            hr = ins[a].shape[0] // 2
            return outs[a].at[chip_idx, pl.ds(which * hr, hr)]

        def landed(a, j, core):
            return rows(a, 2 * chips[j][0] + chips[j][1], core)

        def ici(a, j, mine):
            hr = ins[a].shape[0] // 2
            src, dst = (ins[a].at[pl.ds(c * hr, hr)], rows(a, s, c)) if mine else (landed(a, j, c),) * 2
            return pltpu.make_async_remote_copy(src_ref=src, dst_ref=dst, send_sem=send.at[3 * a + j],
                                                recv_sem=recv.at[3 * a + j], device_id=(*chips[j], c),
                                                device_id_type=MESH)

        def d2d(a, j, mine):
            ref = landed(a, j, c if mine else 1 - c)
            return pltpu.make_async_remote_copy(src_ref=ref, dst_ref=ref, send_sem=fsend.at[3 * a + j],
                                                recv_sem=frecv.at[3 * a + j], device_id=(x, y, 1 - c),
                                                device_id_type=MESH)

        return ici, d2d, s

    def start(self, ins, outs, scr):
        ici, _, _ = self._copies(ins, outs, scr)
        for a in range(self.n):
            for j in range(3):
                ici(a, j, True).start()

    def middle(self, ins, outs, scr):
        ici, d2d, s = self._copies(ins, outs, scr)
        for a in range(self.n):
            _bounce_copy(ins[a], outs[a].at[s], scr[4 + 2 * a], scr[5 + 2 * a])
        for a in range(self.n):
            for j in range(3):
                ici(a, j, False).wait_recv()
                d2d(a, j, True).start()

    def finish(self, ins, outs, scr):
        ici, d2d, _ = self._copies(ins, outs, scr)
        for a in range(self.n):
            for j in range(3):
                d2d(a, j, False).wait_recv()
        for a in range(self.n):
            for j in range(3):
                ici(a, j, True).wait_send()
                d2d(a, j, True).wait_send()


def _run_plan(name, plan):
    k_in, k_out = len(plan.operands), len(plan.out_shape)

    def body(*refs):
        parts = (refs[:k_in], refs[k_in:k_in + k_out], refs[k_in + k_out:])
        plan.start(*parts)
        plan.middle(*parts)
        plan.finish(*parts)

    return pl.pallas_call(
        body, name=name, in_specs=[ANY] * k_in, out_specs=[ANY] * k_out, out_shape=plan.out_shape,
        scratch_shapes=plan.scratch,
        compiler_params=pltpu.CompilerParams(has_side_effects=True, vmem_limit_bytes=VMEM_LIMIT),
    )(*plan.operands)


def _hosted_call(body, name, grid, in_specs, out_specs, out_shape, scratch_shapes, operands, sem, plan=None):
    if plan is None:
        res = pl.pallas_call(body, name=name, grid=grid, in_specs=in_specs, out_specs=out_specs, out_shape=out_shape,
                             scratch_shapes=scratch_shapes, compiler_params=_params(sem))(*operands)
        return res, None
    n_in, n_out, n_scr = len(in_specs), len(out_specs), len(scratch_shapes)
    k_in, k_out = len(plan.operands), len(plan.out_shape)
    total = int(np.prod(grid))
    late = max(0, total - 1 - max(1, total // 8))

    def hosted(*refs):
        ins, refs = refs[:n_in], refs[n_in:]
        p_ins, refs = refs[:k_in], refs[k_in:]
        outs, refs = refs[:n_out], refs[n_out:]
        p_outs, refs = refs[:k_out], refs[k_out:]
        scr, p_scr = refs[:n_scr], refs[n_scr:]
        step = pl.program_id(0)
        for d in range(1, len(grid)):
            step = step * grid[d] + pl.program_id(d)
        pl.when(step == 0)(lambda: plan.start(p_ins, p_outs, p_scr))
        body(*ins, *outs, *scr)
        pl.when(step == late)(lambda: plan.middle(p_ins, p_outs, p_scr))
        pl.when(step == total - 1)(lambda: plan.finish(p_ins, p_outs, p_scr))

    res = pl.pallas_call(
        hosted, name=name, grid=grid, in_specs=list(in_specs) + [ANY] * k_in, out_specs=list(out_specs) + [ANY] * k_out,
        out_shape=list(out_shape) + plan.out_shape, scratch_shapes=list(scratch_shapes) + plan.scratch,
        compiler_params=pltpu.CompilerParams(dimension_semantics=sem, vmem_limit_bytes=VMEM_LIMIT,
                                             has_side_effects=True),
    )(*operands, *plan.operands)
    return res[:n_out], res[n_out:]


def _all_gather(name, shards):
    return _run_plan(name, _Gather(shards))


class _Swap:
    def __init__(self, arrs):
        self.n = len(arrs)
        self.operands = list(arrs)
        self.out_shape = [jax.ShapeDtypeStruct((a.shape[0], a.shape[1] // 2, a.shape[2]), a.dtype) for a in arrs]
        self.scratch = [pltpu.SemaphoreType.DMA((self.n,)) for _ in range(2)]

    def _copy(self, ins, outs, scr, a):
        x, y, c, _, _ = _place()
        hr = ins[a].shape[1] // 2
        return pltpu.make_async_remote_copy(
            src_ref=ins[a].at[:, pl.ds((1 - c) * hr, hr)], dst_ref=outs[a], send_sem=scr[0].at[a],
            recv_sem=scr[1].at[a], device_id=(x, y, 1 - c), device_id_type=MESH)

    def start(self, ins, outs, scr):
        for a in range(self.n):
            self._copy(ins, outs, scr, a).start()

    def middle(self, ins, outs, scr):
        pass

    def finish(self, ins, outs, scr):
        for a in range(self.n):
            self._copy(ins, outs, scr, a).wait()


class _Scatter:
    def __init__(self, parts):
        n = self.n = len(parts)
        self.operands = list(parts)
        self.out_shape = [jax.ShapeDtypeStruct(a.shape, a.dtype) for a in parts]
        self.scratch = [pltpu.SemaphoreType.DMA((3 * n,)) for _ in range(2)]
        for a in parts:
            self.scratch += [_bounce_shape(a.shape[1], a.shape[2], a.dtype), pltpu.SemaphoreType.DMA((4,))]

    def _copy(self, ins, outs, scr, a, j, mine):
        x, y, c, s, chips = _place()
        t = 2 * chips[j][0] + chips[j][1]
        return pltpu.make_async_remote_copy(
            src_ref=ins[a].at[t], dst_ref=outs[a].at[s if mine else t], send_sem=scr[0].at[3 * a + j],
            recv_sem=scr[1].at[3 * a + j], device_id=(*chips[j], c), device_id_type=MESH)

    def start(self, ins, outs, scr):
        for a in range(self.n):
            for j in range(3):
                self._copy(ins, outs, scr, a, j, True).start()

    def middle(self, ins, outs, scr):
        s = _place()[3]
        for a in range(self.n):
            _bounce_copy(ins[a].at[s], outs[a].at[s], scr[2 + 2 * a], scr[3 + 2 * a])

    def finish(self, ins, outs, scr):
        for a in range(self.n):
            for j in range(3):
                self._copy(ins, outs, scr, a, j, False).wait_recv()
        for a in range(self.n):
            for j in range(3):
                self._copy(ins, outs, scr, a, j, True).wait_send()


def _pair_gather(name, halves):
    n = len(halves)

    def body(*refs):
        ins, outs = refs[:n], refs[n:2 * n]
        send, recv = refs[2 * n:2 * n + 2]
        stage = refs[2 * n + 2:]
        x, y, c, _, _ = _place()
        cps = []
        for a in range(n):
            hr = ins[a].shape[0]
            cp = pltpu.make_async_remote_copy(
                src_ref=ins[a], dst_ref=outs[a].at[pl.ds(c * hr, hr)], send_sem=send.at[a], recv_sem=recv.at[a],
                device_id=(x, y, 1 - c), device_id_type=MESH)
            cp.start()
            cps.append((cp, hr))
        for a, (cp, hr) in enumerate(cps):
            _bounce_copy(ins[a], outs[a].at[pl.ds(c * hr, hr)], stage[2 * a], stage[2 * a + 1])
        for a, (cp, hr) in enumerate(cps):
            cp.wait_send()
            theirs = outs[a].at[pl.ds((1 - c) * hr, hr)]
            pltpu.make_async_remote_copy(src_ref=theirs, dst_ref=theirs, send_sem=send.at[a], recv_sem=recv.at[a],
                                         device_id=(x, y, 1 - c), device_id_type=MESH).wait_recv()

    out_shape = [jax.ShapeDtypeStruct((2 * a.shape[0], a.shape[1]), a.dtype) for a in halves]
    return _hbm_call(body, name, halves, out_shape, (n, n),
                     bounce=[(a.shape[0], a.shape[1], a.dtype) for a in halves])


def _row_tile(rows, cols, itemsize, n_bufs):
    budget = VMEM_LIMIT // 2
    for t in range(min(rows, 1024) // 16 * 16, 0, -16):
        if rows % t == 0 and 2 * n_bufs * t * cols * itemsize <= budget:
            return t
    return rows


def _pair_add(name, g, gsib, core, out_dtype):
    _, r, cols = g.shape
    hr = r // 2
    t = _row_tile(hr, cols, 4, 3)
    per = hr // t

    def body(core_ref, a_ref, b_ref, o_ref):
        o_ref[...] = (a_ref[...].astype(F32) + b_ref[...].astype(F32)).astype(o_ref.dtype)

    grid_spec = pltpu.PrefetchScalarGridSpec(
        num_scalar_prefetch=1, grid=(N_CHIPS, per),
        in_specs=[pl.BlockSpec((None, t, cols), lambda s, i, core: (s, core[0] * per + i, 0)),
                  pl.BlockSpec((None, t, cols), lambda s, i, core: (s, i, 0))],
        out_specs=pl.BlockSpec((None, t, cols), lambda s, i, core: (s, i, 0)))
    return pl.pallas_call(body, name=name, grid_spec=grid_spec,
                          out_shape=jax.ShapeDtypeStruct((N_CHIPS, hr, cols), out_dtype),
                          compiler_params=_params(("arbitrary", "arbitrary")))(core, g, gsib)


def _chip_sum(name, parts):
    _, hr, cols = parts.shape
    t = _row_tile(hr, cols, 4, 5)

    def body(p_ref, o_ref):
        o_ref[...] = ((p_ref[0].astype(F32) + p_ref[1].astype(F32)) + p_ref[2].astype(F32)) + p_ref[3].astype(F32)

    return pl.pallas_call(
        body, name=name, grid=(hr // t,), in_specs=[pl.BlockSpec((N_CHIPS, t, cols), lambda i: (0, i, 0))],
        out_specs=pl.BlockSpec((t, cols), lambda i: (i, 0)), out_shape=jax.ShapeDtypeStruct((hr, cols), F32),
        compiler_params=_params(("arbitrary",)))(parts)


def _pair_partials(tag, arrs, sib, wire_dtypes, core):
    return _Scatter([_pair_add(f"{tag}_pair_add{i}", g, gs, core, dt)
                     for i, (g, gs, dt) in enumerate(zip(arrs, sib, wire_dtypes))])


def _finish_reduce(tag, scattered):
    halves = [_chip_sum(f"{tag}_chip_sum{i}", p) for i, p in enumerate(scattered)]
    return _pair_gather(f"{tag}_pair_gather", halves)


def _adamw(name, w, g_parts, m, v):
    thin = w.ndim == 3
    rows, cols = w.shape[0], w.shape[-1]
    n_parts = len(g_parts)
    part_rows = rows // n_parts
    t = max(d for d in range(1, 257) if part_rows % d == 0) if thin else _row_tile(part_rows, cols, 4, 7 + n_parts)
    per = part_rows // t
    c1 = 1.0 - ADAM_B1 ** ADAM_STEP
    c2 = 1.0 - ADAM_B2 ** ADAM_STEP

    def body(w_ref, m_ref, v_ref, *refs):
        g_refs, (go_ref, d_ref, nm_ref, nv_ref) = refs[:n_parts], refs[n_parts:]
        g = g_refs[0][...]
        for k in range(1, n_parts):
            g = jnp.where(pl.program_id(0) >= k * per, g_refs[k][...], g)
        go_ref[...] = g
        m = ADAM_B1 * m_ref[...] + (1.0 - ADAM_B1) * g
        v = ADAM_B2 * v_ref[...] + (1.0 - ADAM_B2) * (g * g)
        nm_ref[...] = m
        nv_ref[...] = v
        d_ref[...] = -ADAM_LR * ((m / c1) / (jnp.sqrt(v / c2) + ADAM_EPS) + ADAM_WD * w_ref[...])

    block = (t, 1, cols) if thin else (t, cols)
    at = lambda r: (r, 0, 0) if thin else (r, 0)
    spec = pl.BlockSpec(block, lambda i: at(i))
    g_specs = [pl.BlockSpec(block, lambda i, k=k: at(jnp.clip(i - k * per, 0, per - 1))) for k in range(n_parts)]
    shp = jax.ShapeDtypeStruct(w.shape, F32)
    return pl.pallas_call(body, name=name, grid=(rows // t,), in_specs=[spec] * 3 + g_specs, out_specs=[spec] * 4,
                          out_shape=[shp] * 4, compiler_params=_params(("arbitrary",)))(w, m, v, *g_parts)


_WEIGHTS = ["mix_norm", "mlp_norm", "mlp_w1", "mlp_w2", "lru_w_in", "lru_conv_w", "lru_conv_b", "lru_w_r", "lru_b_r",
            "lru_w_i", "lru_b_i", "lru_lambda", "lru_w_out", "fox_w_in", "fox_b_f", "fox_q_gain", "fox_k_gain",
            "fox_w_out"]
_REPLICATED = ["mix_norm", "mlp_norm", "lru_conv_b", "lru_w_r", "lru_b_r", "lru_w_i", "lru_b_i", "lru_lambda",
               "fox_b_f", "fox_q_gain", "fox_k_gain"]
_PACK_TILE = 2 * SUBLANES * LANES


def _as2d(a):
    return a.reshape(-1, a.shape[-1])


def kernel(x, mix_norm, mlp_norm, mlp_w1, mlp_w2, lru_w_in, lru_conv_w, lru_conv_b, lru_w_r, lru_b_r, lru_w_i, lru_b_i, lru_lambda, lru_w_out, fox_w_in, fox_b_f, fox_q_gain, fox_k_gain, fox_w_out, loss_target, m_mix_norm, m_mlp_norm, m_mlp_w1, m_mlp_w2, m_lru_w_in, m_lru_conv_w, m_lru_conv_b, m_lru_w_r, m_lru_b_r, m_lru_w_i, m_lru_b_i, m_lru_lambda, m_lru_w_out, m_fox_w_in, m_fox_b_f, m_fox_q_gain, m_fox_k_gain, m_fox_w_out, v_mix_norm, v_mlp_norm, v_mlp_w1, v_mlp_w2, v_lru_w_in, v_lru_conv_w, v_lru_conv_b, v_lru_w_r, v_lru_b_r, v_lru_w_i, v_lru_b_i, v_lru_lambda, v_lru_w_out, v_fox_w_in, v_fox_b_f, v_fox_q_gain, v_fox_k_gain, v_fox_w_out):
    args = dict(locals())
    W = {n: args[n] for n in _WEIGHTS}
    Mo = {n: args["m_" + n] for n in _WEIGHTS}
    Vo = {n: args["v_" + n] for n in _WEIGHTS}
    S, D = x.shape[1], x.shape[2]
    F = 4 * D
    H = D // HEAD_DIM
    NU = 3 * D + LANES
    FQ, DQ = F // N_CHIPS, D // N_CHIPS
    nfox = fox_w_in.shape[-1]
    chip = 2 * lax.axis_index("x") + lax.axis_index("y")
    core = lax.axis_index("c").astype(jnp.int32).reshape(1)

    cw_flat = jnp.pad(lru_conv_w.reshape(-1), (0, _PACK_TILE - CONV_WIDTH * DQ)).reshape(2 * SUBLANES, LANES)
    w1s, w2s = mlp_w1.astype(BF16), mlp_w2.astype(BF16)
    wv = {}
    small = {n: W[n] for n in _REPLICATED}
    scattered = {}
    members = {"g1": ["w2_1", "w1_1", "fox_out"], "g2": ["fox_in"], "g3": ["w2_0", "w1_0"], "g4": ["lru_out", "lru_in"]}
    swap_at = {"fox_bwd_prep": "g1", "fox_dh": "g2", "lru_dout": "g3"}
    scatter_at = {"attn_backward": "g1", "mlp0_dact": "g2", "lru_bwd": "g3", "lru_dh": "g4"}
    swapped = {}

    fox_rows = -(-nfox // (4 * SUBLANES)) * (4 * SUBLANES)
    fox_t = jnp.pad(jnp.transpose(fox_w_in[0]).astype(BF16), ((0, fox_rows - nfox), (0, 0)))

    def shard_major(name, g):
        if name == "fox_in":
            return jnp.pad(g[:nfox * N_CHIPS].reshape(N_CHIPS, nfox, D), ((0, 0), (0, fox_rows - nfox), (0, 0)))
        return g

    class Comm:
        @staticmethod
        def before(name, grads):
            if name == "mix0_norm":
                return _Gather([lru_w_in[0].astype(BF16)])
            if name == "lru_in":
                return _Gather([lru_w_out[0].astype(BF16), cw_flat])
            if name == "lru_fwd":
                return _Gather([w1s[0]])
            if name == "mlp0_up":
                return _Gather([w2s[0]])
            if name == "mlp0_down":
                return _Gather([fox_t])
            if name == "attn_forward":
                return _Gather([fox_w_out[0].astype(BF16), w1s[1], w2s[1]])
            if name in swap_at:
                group = swap_at[name]
                swapped[group] = [[shard_major(n, grads[n]) for n in members[group]], None]
                return _Swap(swapped[group][0])
            if name in scatter_at:
                group = scatter_at[name]
                if group not in swapped:
                    arrs = [shard_major(n, grads[n]) for n in members[group]]
                    swapped[group] = [arrs, _run_plan(f"{group}_pair_swap", _Swap(arrs))]
                arrs, sib = swapped[group]
                return _pair_partials(group, arrs, sib, [BF16] * len(arrs), core)
            return None

        @staticmethod
        def after(name, res, wv):
            if name == "mix0_norm":
                wv.update(lru_in=_View(res[0], "cs"))
            elif name == "lru_in":
                wv.update(lru_out=_View(res[0], "rs"))
                taps = res[1].reshape(N_CHIPS, -1)[:, :CONV_WIDTH * DQ].reshape(N_CHIPS, CONV_WIDTH, DQ)
                small["conv_w"] = jnp.transpose(taps, (1, 0, 2)).reshape(CONV_WIDTH, D)
            elif name == "lru_fwd":
                wv.update(w1_0=_View(res[0], "cs"))
            elif name == "mlp0_up":
                wv.update(w2_0=_View(res[0], "rs"))
            elif name == "mlp0_down":
                fox_full = jnp.concatenate([res[0][s, :nfox] for s in range(N_CHIPS)], axis=0)
                wv.update(fox_in=_View(jnp.pad(fox_full, ((0, NU - fox_full.shape[0]), (0, 0)))))
            elif name == "attn_forward":
                wv.update(fox_out=_View(res[0], "rs"), w1_1=_View(res[1], "cs"), w2_1=_View(res[2], "rs"))
            elif name in swap_at:
                swapped[swap_at[name]][1] = res
            else:
                scattered.update(zip(members[scatter_at[name]], res))

    def grad_view(grads, name):
        if name in ("w1_0", "w1_1"):
            return _View(None, "cs", shape=(N_CHIPS, D, FQ), dtype=BF16)
        if name in ("w2_0", "w2_1"):
            return _View(None, "rs", shape=(N_CHIPS, FQ, D), dtype=BF16)
        if name == "lru_in":
            return _View(None, "cs", shape=(N_CHIPS, D, 2 * D // N_CHIPS), dtype=BF16)
        if name in ("lru_out", "fox_out"):
            return _View(None, "rs", shape=(N_CHIPS, DQ, D), dtype=BF16)
        return _View(None, shape=(NU, D), dtype=BF16)

    loss, gx, grads = _local_step(x[0], loss_target[0], small, wv, grad_view, Comm)

    pack_names = _REPLICATED + ["conv_w"]
    flat = jnp.concatenate([grads[n].reshape(-1).astype(F32) for n in pack_names] + [loss.reshape(-1)])
    per_chip = -(-flat.shape[0] // (N_CHIPS * _PACK_TILE)) * _PACK_TILE
    pack = jnp.pad(flat, (0, N_CHIPS * per_chip - flat.shape[0])).reshape(N_CHIPS, per_chip // LANES, LANES)
    pack_sib = _run_plan("pack_pair_swap", _Swap([pack]))
    (scattered["pack"],) = _run_plan("pack_chip_scatter", _pair_partials("pack", [pack], pack_sib, [F32], core))
    order = ["w1_0", "w1_1", "w2_0", "w2_1", "lru_in", "lru_out", "fox_in", "fox_out", "pack"]
    red = dict(zip(order, _finish_reduce("grads", [scattered[n] for n in order])))
    (all_pack,) = _all_gather("gather_small_grads", [red["pack"]])
    all_flat = all_pack.reshape(-1)
    G = {}
    off = 0
    for n in pack_names:
        shape = grads[n].shape if n == "conv_w" else W[n].shape
        size = int(np.prod(shape))
        G[n] = all_flat[off:off + size].reshape(shape)
        off += size
    total = all_flat[off]
    G["lru_conv_w"] = lax.dynamic_slice_in_dim(G.pop("conv_w"), chip * DQ, DQ, axis=1)[None]
    parts = {n: [_as2d(G[n])] for n in G}
    parts.update(mlp_w1=[red["w1_0"], red["w1_1"]], mlp_w2=[red["w2_0"], red["w2_1"]], lru_w_in=[red["lru_in"]],
                 lru_w_out=[red["lru_out"]], fox_w_in=[red["fox_in"][:nfox, None, :]], fox_w_out=[red["fox_out"]])

    delta, new_m, new_v = {}, {}, {}
    for n in _WEIGHTS:
        if n == "fox_w_in":
            to_thin = lambda a: jnp.transpose(a, (2, 0, 1))
            res = _adamw(f"adamw_{n}", to_thin(W[n]), parts[n], to_thin(Mo[n]), to_thin(Vo[n]))
            G[n], delta[n], new_m[n], new_v[n] = (jnp.transpose(t, (1, 2, 0)) for t in res)
            continue
        go, d, nm, nv = _adamw(f"adamw_{n}", _as2d(W[n]), parts[n], _as2d(Mo[n]), _as2d(Vo[n]))
        G[n], delta[n], new_m[n], new_v[n] = (t.reshape(W[n].shape) for t in (go, d, nm, nv))

    return (total, gx[None], *[G[n] for n in _WEIGHTS], *[delta[n] for n in _WEIGHTS],
            *[new_m[n] for n in _WEIGHTS], *[new_v[n] for n in _WEIGHTS])
```

```python
import functools

import numpy as np
import jax
import jax.numpy as jnp
from jax import lax
from jax.experimental import pallas as pl
from jax.experimental.pallas import tpu as pltpu

F32 = jnp.float32
BF16 = jnp.bfloat16

HEAD_DIM = 64
LRU_BLOCK_DIM = 64
CONV_WIDTH = 4
LRU_C = 8.0
EPS = 1e-6
NEG_INF = -1e30
ADAM_LR = 0.001
ADAM_B1 = 0.9
ADAM_B2 = 0.999
ADAM_EPS = 1e-08
ADAM_WD = 0.01
ADAM_STEP = 10

N_CHIPS = 4
LANES = 128
SUBLANES = 8
MXU_DIM = 256
VMEM_LIMIT = 52 * 1024 * 1024
MATMUL_TILES = (1024, 640, 512, 256, 128)
MATMUL_VMEM = VMEM_LIMIT * 4 // 5
MESH = pl.DeviceIdType.MESH
ANY = pl.BlockSpec(memory_space=pl.ANY)


def _pick(n, prefs):
    for p in prefs:
        if p <= n and n % p == 0:
            return p
    return n


def _params(sem=None):
    return pltpu.CompilerParams(dimension_semantics=sem, vmem_limit_bytes=VMEM_LIMIT)


class _View:
    def __init__(self, arr, kind="plain", shape=None, dtype=None):
        self.arr = arr
        self.kind = kind
        self.shape = tuple(arr.shape) if arr is not None else tuple(shape)
        self.dtype = arr.dtype if arr is not None else dtype

    def limits(self):
        if self.kind == "plain":
            return 0, 0
        return self.shape[-2], (self.shape[-1] if self.kind == "cs" else 0)

    def spec(self, br, bc, fr, fc):
        if self.kind == "plain":
            return pl.BlockSpec((br, bc), lambda *g: (fr(*g), fc(*g)))
        rows, ncol = self.shape[-2:]
        assert rows % br == 0 and ncol % bc == 0, (self.shape, br, bc)
        if self.kind == "cs":
            per = ncol // bc
            return pl.BlockSpec((None, br, bc), lambda *g: (fc(*g) // per, fr(*g), fc(*g) % per))
        per = rows // br
        return pl.BlockSpec((None, br, bc), lambda *g: (fr(*g) // per, fr(*g) % per, fc(*g)))


def _bf(x):
    return x if x.dtype == BF16 else x.astype(BF16)


def _matmul(name, A, B, M, N, K, *, ta=False, tb=False, outs, epilogue, extras=(), vecs=(), n_sums=0,
            tm=None, tn=None, tk=None, plan=None):
    lim = {"m": [M], "n": [N], "k": [K]}
    for view, (rdim, cdim) in ([(A, "km" if ta else "mk"), (B, "nk" if tb else "kn")]
                               + [(e, "mn") for e in extras] + [(o, "mn") for o in outs]):
        r_lim, c_lim = view.limits()
        lim[rdim].append(r_lim)
        lim[cdim].append(c_lim)
    cap = {d: int(np.gcd.reduce(lim[d])) for d in "mnk"}
    tm = tm or _pick(cap["m"], MATMUL_TILES)
    tn = tn or _pick(cap["n"], MATMUL_TILES)
    tk = tk or _pick(cap["k"], MATMUL_TILES)

    def vmem_bytes(tm, tk):
        size = lambda v: jnp.dtype(v.dtype).itemsize
        tiles = tm * tk * size(A) + tk * tn * size(B) + tm * tn * sum(size(v) for v in list(extras) + list(outs))
        return 2 * tiles + (tm * tn * 4 if K > tk else 0)

    if cap["k"] % (2 * tk) == 0 and vmem_bytes(tm, 2 * tk) <= MATMUL_VMEM:
        tk *= 2
    elif K == tk and cap["m"] % (2 * tm) == 0 and vmem_bytes(2 * tm, tk) <= MATMUL_VMEM:
        tm *= 2
    nk = K // tk
    gi = lambda i, j, k: i
    gj = lambda i, j, k: j
    gk = lambda i, j, k: k
    a_spec = A.spec(tk, tm, gk, gi) if ta else A.spec(tm, tk, gi, gk)
    b_spec = B.spec(tn, tk, gj, gk) if tb else B.spec(tk, tn, gk, gj)
    ca = 0 if ta else 1
    cb = 1 if tb else 0
    ne, no = len(extras) + len(vecs), len(outs)
    assert n_sums == 0 or tn == N
    row_spec = pl.BlockSpec((1, tn), lambda i, j, k: (0, j))
    in_specs = [a_spec, b_spec] + [e.spec(tm, tn, gi, gj) for e in extras] + [row_spec] * len(vecs)
    operands = [A.arr, B.arr] + [e.arr for e in extras] + list(vecs)
    out_specs = [o.spec(tm, tn, gi, gj) for o in outs] + [row_spec] * n_sums
    out_shape = ([jax.ShapeDtypeStruct(o.shape, o.dtype) for o in outs]
                 + [jax.ShapeDtypeStruct((1, N), F32)] * n_sums)

    def body(*refs):
        a_ref, b_ref = refs[0], refs[1]
        ex = refs[2:2 + ne]
        o_refs = refs[2 + ne:2 + ne + no]
        s_refs = refs[2 + ne + no:2 + ne + no + n_sums]
        first_row_tile = pl.program_id(0) == 0

        def prod():
            return lax.dot_general(_bf(a_ref[...]), _bf(b_ref[...]), (((ca,), (cb,)), ((), ())),
                                   preferred_element_type=F32)

        def finish(acc):
            res = epilogue(acc, *[e[...] for e in ex])
            for o_ref, r in zip(o_refs, res[:no]):
                o_ref[...] = r.astype(o_ref.dtype)
            for s_ref, r in zip(s_refs, res[no:]):
                def assign(s_ref=s_ref, r=r):
                    s_ref[...] = r

                def accumulate(s_ref=s_ref, r=r):
                    s_ref[...] += r

                pl.when(first_row_tile)(assign)
                pl.when(jnp.logical_not(first_row_tile))(accumulate)

        if nk == 1:
            finish(prod())
        else:
            acc_ref = refs[-1]
            k = pl.program_id(2)

            @pl.when(k == 0)
            def _():
                acc_ref[...] = jnp.zeros_like(acc_ref)

            acc_ref[...] += prod()

            @pl.when(k == nk - 1)
            def _():
                finish(acc_ref[...])

    res, side = _hosted_call(body, name, (M // tm, N // tn, nk), in_specs, out_specs, out_shape,
                             [pltpu.VMEM((tm, tn), F32)] if nk > 1 else [], operands,
                             ("arbitrary", "arbitrary", "arbitrary"), plan)
    return res if plan is None else (res, side)


def _ep_store(acc):
    return (acc,)


def _ep_resid(acc, res):
    return (res + acc,)


def _ep_resid_norm(acc, res, g):
    xo = res + acc
    r = lax.rsqrt(jnp.mean(xo * xo, axis=-1, keepdims=True) + EPS)
    return (xo, (xo * r) * g)


def _ep_norm_bwd(acc, x, dres, g):
    r = lax.rsqrt(jnp.mean(x * x, axis=-1, keepdims=True) + EPS)
    xhat = x * r
    dxn = acc * g
    tot = dres + r * (dxn - xhat * jnp.mean(dxn * xhat, axis=-1, keepdims=True))
    return (tot, tot, jnp.sum(acc * xhat, axis=0, keepdims=True))


def _ep_relu2(acc):
    zp = jnp.maximum(acc, 0.0)
    return (zp * zp,)


def _ep_drelu2(acc, act):
    return (acc * (2.0 * jnp.sqrt(act.astype(F32))),)


def _fresh(M, N, dtype):
    return _View(None, shape=(M, N), dtype=dtype)


def _rms_fwd(name, x, g, S, D, plan=None):
    T = _pick(S, (512, 256, 128))

    def body(x_ref, g_ref, h_ref):
        x = x_ref[...]
        r = lax.rsqrt(jnp.mean(x * x, axis=-1, keepdims=True) + EPS)
        h_ref[...] = ((x * r) * g_ref[...]).astype(BF16)

    return _hosted_call(body, name, (S // T,),
                        [pl.BlockSpec((T, D), lambda i: (i, 0)), pl.BlockSpec((1, D), lambda i: (0, 0))],
                        [pl.BlockSpec((T, D), lambda i: (i, 0))], [jax.ShapeDtypeStruct((S, D), BF16)], [], (x, g),
                        ("arbitrary",), plan)


def _loss_head(x, tgt, S, D):
    T = _pick(S, (512, 256, 128))

    def body(x_ref, t_ref, loss_ref, d_ref, db_ref):
        @pl.when(pl.program_id(0) == 0)
        def _():
            loss_ref[...] = jnp.zeros_like(loss_ref)

        e = x_ref[...] - t_ref[...]
        loss_ref[...] += 0.5 * jnp.sum(jnp.mean(e * e, axis=-1, keepdims=True), axis=0, keepdims=True)
        d = e * (1.0 / D)
        d_ref[...] = d
        db_ref[...] = d.astype(BF16)

    row = pl.BlockSpec((T, D), lambda i: (i, 0))
    return pl.pallas_call(
        body, name="loss_head", grid=(S // T,), in_specs=[row, row],
        out_specs=[pl.BlockSpec((1, 1), lambda i: (0, 0)), row, row],
        out_shape=[jax.ShapeDtypeStruct((1, 1), F32), jax.ShapeDtypeStruct((S, D), F32),
                   jax.ShapeDtypeStruct((S, D), BF16)],
        compiler_params=_params(("arbitrary",)),
    )(x, tgt)


def _sigmoid(z):
    return 1.0 / (1.0 + jnp.exp(-z))


def _log_sigmoid(z):
    return jnp.minimum(z, 0.0) - jnp.log(1.0 + jnp.exp(-jnp.abs(z)))


_GELU_K = 0.7978845608028654
_GELU_C = 0.044715


def _gelu(x):
    t = jnp.tanh(_GELU_K * (x + _GELU_C * (x * x * x)))
    return 0.5 * x * (1.0 + t)


def _gelu_and_grad(x):
    x2 = x * x
    t = jnp.tanh(_GELU_K * (x + _GELU_C * (x2 * x)))
    g = 0.5 * x * (1.0 + t)
    dg = 0.5 * (1.0 + t) + 0.5 * x * (1.0 - t * t) * (_GELU_K * (1.0 + 3.0 * _GELU_C * x2))
    return g, dg


def _decay_terms(r, ls):
    la = LRU_C * r * ls
    a = jnp.exp(la)
    a2 = a * a
    mult = jnp.sqrt(-jnp.tanh(la) * (a2 + 1.0))
    return a, a2, mult


def _group_scan(a, b, sub, reverse):
    k = 1
    while k < SUBLANES:
        inside = sub < SUBLANES - k if reverse else sub >= k
        shift = SUBLANES - k if reverse else k
        b = b + a * jnp.where(inside, pltpu.roll(b, shift, 0), 0.0)
        a = a * jnp.where(inside, pltpu.roll(a, shift, 0), 1.0)
        k *= 2
    return a, b


def _lru_fwd(u0, conv_w, conv_b, wr_bd, b_r, wi_bd, b_i, lam, S, D, plan=None):
    T = _pick(S, (256, 128))
    GT = wr_bd.shape[-1]
    nG = D // GT

    def body(gb_ref, xb_ref, cw_ref, cb_ref, wr_ref, br_ref, wi_ref, bi_ref, lam_ref,
             y_ref, xc_ref, r_ref, i_ref, hs_ref, ext, a_scr, hcar):
        @pl.when(pl.program_id(0) == 0)
        def _():
            ext[0:SUBLANES, :] = jnp.zeros((SUBLANES, D), F32)
            hcar[...] = jnp.zeros_like(hcar)

        xb = xb_ref[...]
        ext[SUBLANES:SUBLANES + T, :] = xb
        xc = cb_ref[...]
        for k in range(CONV_WIDTH):
            xc = xc + ext[pl.ds(SUBLANES - (CONV_WIDTH - 1) + k, T), :] * cw_ref[k:k + 1, :]
        ext[0:SUBLANES, :] = xb[T - SUBLANES:T, :]
        xc_ref[...] = xc
        xcb = xc.astype(BF16)
        for g in range(nG):
            sl = slice(g * GT, (g + 1) * GT)
            zr = jnp.dot(xcb[:, sl], wr_ref[g], preferred_element_type=F32) + br_ref[:, sl]
            zi = jnp.dot(xcb[:, sl], wi_ref[g], preferred_element_type=F32) + bi_ref[:, sl]
            r_ref[:, sl] = _sigmoid(zr)
            i_ref[:, sl] = _sigmoid(zi)
        r = r_ref[...]
        a, _, mult = _decay_terms(r, _log_sigmoid(lam_ref[...]))
        a_scr[...] = a
        hs_ref[...] = mult * (i_ref[...] * xc)

        sub = lax.broadcasted_iota(jnp.int32, (SUBLANES, D), 0)

        def step(j, h):
            rows = pl.ds(pl.multiple_of(j * SUBLANES, SUBLANES), SUBLANES)
            A, B = _group_scan(a_scr[rows, :], hs_ref[rows, :], sub, False)
            hg = A * h + B
            hs_ref[rows, :] = hg
            return hg[SUBLANES - 1:SUBLANES, :]

        hcar[...] = lax.fori_loop(0, T // SUBLANES, step, hcar[...], unroll=2)
        y_ref[...] = (_gelu(gb_ref[...]) * hs_ref[...]).astype(BF16)

    row = pl.BlockSpec((T, D), lambda i: (i, 0))
    vec = pl.BlockSpec((1, D), lambda i: (0, 0))
    bd = pl.BlockSpec((nG, GT, GT), lambda i: (0, 0, 0))
    f32o = jax.ShapeDtypeStruct((S, D), F32)
    return _hosted_call(
        body, "lru_fwd", (S // T,),
        [row, pl.BlockSpec((T, D), lambda i: (i, 1)), pl.BlockSpec((CONV_WIDTH, D), lambda i: (0, 0)), vec,
         bd, vec, bd, vec, vec],
        [row, row, row, row, row], [jax.ShapeDtypeStruct((S, D), BF16), f32o, f32o, f32o, f32o],
        [pltpu.VMEM((T + SUBLANES, D), F32), pltpu.VMEM((T, D), F32), pltpu.VMEM((1, D), F32)],
        (u0, u0, conv_w, conv_b, wr_bd, b_r, wi_bd, b_i, lam), ("arbitrary",), plan)


def _lru_bwd(dy, u0, xc, r, ig, hs, conv_w, wr_bd, wi_bd, lam, S, D, plan=None):
    T = _pick(S, (128,))
    nT = S // T
    GT = wr_bd.shape[-1]
    nG = D // GT
    W = CONV_WIDTH

    def body(dy_ref, gb_ref, xb_ref, xbp_ref, xc_ref, r_ref, i_ref, hs_ref, hsp_ref, cw_ref, wr_ref, wi_ref, lam_ref,
             du_ref, dcw_ref, dcb_ref, dlam_ref, dbr_ref, dbi_ref, dwr_ref, dwi_ref,
             a_scr, dh_scr, exth, extx, extd, dxc_scr, dz_scr, carry):
        step = pl.program_id(0)
        first_tile = step == nT - 1

        @pl.when(step == 0)
        def _():
            for ref in (dcw_ref, dcb_ref, dlam_ref, dbr_ref, dbi_ref, dwr_ref, dwi_ref, carry):
                ref[...] = jnp.zeros_like(ref)
            extd[T:T + SUBLANES, :] = jnp.zeros((SUBLANES, D), F32)

        hs = hs_ref[...]
        dy = dy_ref[...]
        g, dgelu = _gelu_and_grad(gb_ref[...])
        du_ref[:, 0:D] = (dy * hs * dgelu).astype(BF16)
        r = r_ref[...]
        lam = lam_ref[...]
        ls = _log_sigmoid(lam)
        a, a2, mult = _decay_terms(r, ls)
        a_scr[...] = a
        dh_scr[...] = dy * g

        sub = lax.broadcasted_iota(jnp.int32, (SUBLANES, D), 0)

        def rstep(j, c):
            rows = pl.ds(pl.multiple_of(T - (j + 1) * SUBLANES, SUBLANES), SUBLANES)
            a_g = a_scr[rows, :]
            a_next = jnp.where(sub < SUBLANES - 1, pltpu.roll(a_g, SUBLANES - 1, 0), 1.0)
            A, B = _group_scan(a_next, dh_scr[rows, :], sub, True)
            d = A * c + B
            dh_scr[rows, :] = d
            return a_g[0:1, :] * d[0:1, :]

        carry[...] = lax.fori_loop(0, T // SUBLANES, rstep, carry[...], unroll=2)
        dh = dh_scr[...]
        keep = jnp.where(first_tile, 0.0, 1.0)
        exth[0:SUBLANES, :] = hsp_ref[...] * keep
        exth[SUBLANES:SUBLANES + T, :] = hs
        hprev = exth[pl.ds(SUBLANES - 1, T), :]
        xc = xc_ref[...]
        ig = i_ref[...]
        da = dh * hprev
        dmult = dh * (ig * xc)
        dla = da * a - dmult * (a2 / mult)
        dlam_ref[...] += jnp.sum(dla * r, axis=0, keepdims=True) * (LRU_C * _sigmoid(-lam))
        dzr = (dla * (LRU_C * ls)) * (r * (1.0 - r))
        dzi = (dh * (mult * xc)) * (ig * (1.0 - ig))
        dbr_ref[...] += jnp.sum(dzr, axis=0, keepdims=True)
        dbi_ref[...] += jnp.sum(dzi, axis=0, keepdims=True)
        dxc_scr[...] = dh * (mult * ig)
        xcb = xc.astype(BF16)
        dz_scr[0] = dzr.astype(BF16)
        dz_scr[1] = dzi.astype(BF16)
        nt_dims = (((1,), (1,)), ((), ()))
        tn_dims = (((0,), (0,)), ((), ()))
        for gq in range(nG):
            sl = slice(gq * GT, (gq + 1) * GT)
            zr_g = dz_scr[0, :, sl]
            zi_g = dz_scr[1, :, sl]
            dxc_scr[:, sl] += (lax.dot_general(zr_g, wr_ref[gq], nt_dims, preferred_element_type=F32)
                               + lax.dot_general(zi_g, wi_ref[gq], nt_dims, preferred_element_type=F32))
            dwr_ref[gq] += lax.dot_general(xcb[:, sl], zr_g, tn_dims, preferred_element_type=F32)
            dwi_ref[gq] += lax.dot_general(xcb[:, sl], zi_g, tn_dims, preferred_element_type=F32)
        dxc = dxc_scr[...]
        dcb_ref[...] += jnp.sum(dxc, axis=0, keepdims=True)
        extx[0:SUBLANES, :] = xbp_ref[...] * keep
        extx[SUBLANES:SUBLANES + T, :] = xb_ref[...]
        extd[0:T, :] = dxc
        dxb = jnp.zeros((T, D), F32)
        for k in range(W):
            dxb = dxb + extd[pl.ds(W - 1 - k, T), :] * cw_ref[k:k + 1, :]
            dcw_ref[k:k + 1, :] += jnp.sum(dxc * extx[pl.ds(SUBLANES - (W - 1) + k, T), :], axis=0, keepdims=True)
        extd[T:T + SUBLANES, :] = dxc[0:SUBLANES, :]
        du_ref[:, D:2 * D] = dxb.astype(BF16)

    rev = lambda i: nT - 1 - i
    tpb = T // SUBLANES
    prev8 = lambda i: jnp.maximum(rev(i) * tpb - 1, 0)
    row = pl.BlockSpec((T, D), lambda i: (rev(i), 0))
    vec = pl.BlockSpec((1, D), lambda i: (0, 0))
    bd = pl.BlockSpec((nG, GT, GT), lambda i: (0, 0, 0))
    vec_o = jax.ShapeDtypeStruct((1, D), F32)
    bd_o = jax.ShapeDtypeStruct((nG, GT, GT), F32)
    return _hosted_call(
        body, "lru_bwd", (nT,),
        [row, row, pl.BlockSpec((T, D), lambda i: (rev(i), 1)), pl.BlockSpec((SUBLANES, D), lambda i: (prev8(i), 1)),
         row, row, row, row, pl.BlockSpec((SUBLANES, D), lambda i: (prev8(i), 0)),
         pl.BlockSpec((W, D), lambda i: (0, 0)), bd, bd, vec],
        [pl.BlockSpec((T, 2 * D), lambda i: (rev(i), 0)), pl.BlockSpec((W, D), lambda i: (0, 0)),
         vec, vec, vec, vec, bd, bd],
        [jax.ShapeDtypeStruct((S, 2 * D), BF16), jax.ShapeDtypeStruct((W, D), F32), vec_o, vec_o, vec_o, vec_o, bd_o, bd_o],
        [pltpu.VMEM((T, D), F32), pltpu.VMEM((T, D), F32), pltpu.VMEM((T + SUBLANES, D), F32),
         pltpu.VMEM((T + SUBLANES, D), F32), pltpu.VMEM((T + SUBLANES, D), F32),
         pltpu.VMEM((T, D), F32), pltpu.VMEM((2, T, D), BF16), pltpu.VMEM((1, D), F32)],
        (dy, u0, u0, u0, xc, r, ig, hs, hs, conv_w, wr_bd, wi_bd, lam), ("arbitrary",), plan)


AUG_ROWS = 16
HEAD_ROWS = 128
LSE_ROW = HEAD_DIM + 6
ONES_ROW_Q = HEAD_DIM + 3
ONES_COL_K = HEAD_DIM
ONES_ROW_V = HEAD_DIM
PREP_LANES = 512
HEAD_UNROLL = 4


def _split3(x):
    b1 = x.astype(BF16).astype(F32)
    r = x - b1
    b2 = r.astype(BF16).astype(F32)
    return b1, b2, r - b2


def _head_block(x, aug, T):
    row = lax.broadcasted_iota(jnp.int32, (AUG_ROWS, T), 0)
    blk = jnp.zeros((AUG_ROWS, T), F32)
    for i, e in enumerate(aug):
        blk = jnp.where(row == i, e, blk)
    return jnp.concatenate([x, blk, jnp.zeros((HEAD_ROWS - HEAD_DIM - AUG_ROWS, T), F32)], axis=0)


def _tri_matrix(lower):
    i = np.arange(LANES)
    m = (i[:, None] >= i[None, :]) if lower else (i[:, None] <= i[None, :])
    return jnp.asarray(m.astype(np.float32), BF16)


def _lane_cumsum(x, tri_ref, carry, reverse):
    n = x.shape[1] // LANES
    tri = tri_ref[...]
    out = [None] * n
    for j in (range(n - 1, -1, -1) if reverse else range(n)):
        cs = carry
        for part in _split3(x[:, j * LANES:(j + 1) * LANES]):
            cs = cs + jnp.dot(part.astype(BF16), tri, preferred_element_type=F32)
        out[j] = cs
        carry = cs[:, 0:1] if reverse else cs[:, LANES - 1:LANES]
    return jnp.concatenate(out, axis=1), carry


def _head_rows(h):
    return pl.ds(pl.multiple_of(h * HEAD_DIM, HEAD_DIM), HEAD_DIM)


def _fox_prep(ut, b_f, qg, kg, S, D, tq):
    H = D // HEAD_DIM
    T = min(tq, PREP_LANES)
    per = tq // T
    scale = HEAD_DIM ** -0.5

    def body(q_ref, k_ref, v_ref, f_ref, bf_ref, qg_ref, kg_ref, tri_ref,
             qat_ref, kat_ref, vat_ref, ka_ref, c_scr, ccar):
        @pl.when(pl.program_id(0) == 0)
        def _():
            ccar[...] = jnp.zeros_like(ccar)

        c, carry = _lane_cumsum(_log_sigmoid(f_ref[...] + bf_ref[...]), tri_ref, ccar[...], False)
        c_scr[...] = c
        ccar[...] = carry

        def head(h, _):
            rows = _head_rows(h)
            c1, c2, c3 = _split3(c_scr[pl.ds(h, 1), :])

            def normed(src, gain, mul):
                x = src[rows, :]
                rs = lax.rsqrt(jnp.mean(x * x, axis=0, keepdims=True) + EPS)
                return ((x * rs) * gain[rows, :]) * mul

            qat_ref[h] = _head_block(normed(q_ref, qg_ref, scale), [c1, c2, c3, 1.0, 1.0, 1.0], T).astype(BF16)
            kb = _head_block(normed(k_ref, kg_ref, 1.0), [1.0, 1.0, 1.0, -c1, -c2, -c3, 1.0, 1.0, 1.0], T)
            kat_ref[h] = kb.astype(BF16)
            ka_ref[h] = kb.T.astype(BF16)
            vat_ref[h] = _head_block(v_ref[rows, :], [1.0, 1.0, 1.0], T).astype(BF16)
            return 0

        lax.fori_loop(0, H, head, 0, unroll=min(HEAD_UNROLL, H))

    part = lambda j: pl.BlockSpec((D, T), lambda i: (j, i))
    colv = lambda n: pl.BlockSpec((n, 1), lambda i: (0, 0))
    tmaj = lambda r: pl.BlockSpec((H, None, r, T), lambda i: (0, i // per, 0, i % per))
    norm = pl.BlockSpec((H, T, HEAD_ROWS), lambda i: (0, i, 0))
    tshape = lambda r: jax.ShapeDtypeStruct((H, S // tq, r, tq), BF16)
    nshape = jax.ShapeDtypeStruct((H, S, HEAD_ROWS), BF16)
    return pl.pallas_call(
        body, name="fox_prep", grid=(S // T,),
        in_specs=[part(0), part(1), part(2), pl.BlockSpec((LANES, T), lambda i: (3 * D // LANES, i)),
                  colv(LANES), colv(D), colv(D), pl.BlockSpec((LANES, LANES), lambda i: (0, 0))],
        out_specs=[tmaj(HEAD_ROWS), tmaj(HEAD_ROWS), tmaj(HEAD_ROWS), norm],
        out_shape=[tshape(HEAD_ROWS), tshape(HEAD_ROWS), tshape(HEAD_ROWS), nshape],
        scratch_shapes=[pltpu.VMEM((LANES, T), F32), pltpu.VMEM((LANES, 1), F32)],
        compiler_params=_params(("arbitrary",)),
    )(ut, ut, ut, ut, b_f, qg, kg, _tri_matrix(False))


def _fox_bwd_prep(dot, ot, lse, qat, S, D, tq, plan=None):
    H = D // HEAD_DIM
    T = min(tq, PREP_LANES)
    per = tq // T

    def body(do_ref, o_ref, lse_ref, qat_ref, doat_ref, doa_ref, qat1_ref, qa1_ref):
        row = lax.broadcasted_iota(jnp.int32, (HEAD_ROWS, T), 0)

        def head(h, _):
            rows = _head_rows(h)
            do = do_ref[rows, :].astype(F32)
            delta = jnp.sum(do * o_ref[rows, :], axis=0, keepdims=True)
            db = _head_block(do, list(_split3(-delta)), T)
            doat_ref[h] = db.astype(BF16)
            doa_ref[h] = db.T.astype(BF16)
            qb = qat_ref[h].astype(F32)
            for i, e in enumerate(_split3(-lse_ref[h])):
                qb = jnp.where(row == LSE_ROW + i, e, qb)
            qat1_ref[h] = qb.astype(BF16)
            qa1_ref[h] = qb.T.astype(BF16)
            return 0

        lax.fori_loop(0, H, head, 0, unroll=min(HEAD_UNROLL, H))

    chan = pl.BlockSpec((D, T), lambda i: (0, i))
    tmaj = pl.BlockSpec((H, None, HEAD_ROWS, T), lambda i: (0, i // per, 0, i % per))
    norm = pl.BlockSpec((H, T, HEAD_ROWS), lambda i: (0, i, 0))
    tshape = jax.ShapeDtypeStruct((H, S // tq, HEAD_ROWS, tq), BF16)
    nshape = jax.ShapeDtypeStruct((H, S, HEAD_ROWS), BF16)
    return _hosted_call(body, "fox_bwd_prep", (S // T,), [chan, chan, pl.BlockSpec((H, 1, T), lambda i: (0, 0, i)), tmaj],
                        [tmaj, norm, tmaj, norm], [tshape, nshape, tshape, nshape], [], (dot, ot, lse, qat),
                        ("arbitrary",), plan)


def _causal(s, k_axis):
    t = min(s.shape)
    ki = lax.broadcasted_iota(jnp.int32, s.shape, k_axis) - (s.shape[k_axis] - t)
    qi = lax.broadcasted_iota(jnp.int32, s.shape, 1 - k_axis)
    return jnp.where(ki <= qi, s, NEG_INF)


def _attn_forward(ka, qat, vat, S, D, tq, plan=None):
    H = D // HEAD_DIM
    nq = S // tq
    G = 4

    def body(ka_ref, qat_ref, vat_ref, o_ref, o32_ref, lse_ref, m_scr, acc_scr):
        qi = pl.program_id(1)
        m_scr[...] = jnp.full_like(m_scr, NEG_INF)
        acc_scr[...] = jnp.zeros_like(acc_scr)

        def span(k0, n, diagonal):
            keys = pl.ds(pl.multiple_of(k0 * tq, tq), n * tq)
            s = [jnp.dot(ka_ref[g, keys, :], qat_ref[g], preferred_element_type=F32) for g in range(G)]
            if diagonal:
                s = [_causal(sg, 0) for sg in s]
            m_prev = [m_scr[g] for g in range(G)]
            m_new = [jnp.maximum(m_prev[g], jnp.max(s[g], axis=0, keepdims=True)) for g in range(G)]
            p = [jnp.exp(s[g] - m_new[g]).astype(BF16) for g in range(G)]
            for g in range(G):
                upd = jnp.dot(vat_ref[g, k0], p[g][0:tq], preferred_element_type=F32)
                for i in range(1, n):
                    upd = upd + jnp.dot(vat_ref[g, k0 + i], p[g][i * tq:(i + 1) * tq], preferred_element_type=F32)
                acc_scr[g] = jnp.exp(m_prev[g] - m_new[g]) * acc_scr[g] + upd
                m_scr[g] = m_new[g]

        def off_diagonal_pair(j, _):
            span(2 * j, 2, False)
            return 0

        lax.fori_loop(0, qi // 2, off_diagonal_pair, 0)
        pl.when(qi % 2 == 1)(lambda: span(qi - 1, 2, True))
        pl.when(qi % 2 == 0)(lambda: span(qi, 1, True))
        for g in range(G):
            l = acc_scr[g, ONES_ROW_V:ONES_ROW_V + 1, :]
            o = acc_scr[g, 0:HEAD_DIM, :] / l
            o_ref[g * HEAD_DIM:(g + 1) * HEAD_DIM, :] = o.astype(BF16)
            o32_ref[g * HEAD_DIM:(g + 1) * HEAD_DIM, :] = o
            lse_ref[g] = m_scr[g] + jnp.log(l)

    chan = pl.BlockSpec((G * HEAD_DIM, tq), lambda h, i: (h, i))
    stat = pl.BlockSpec((G, 1, tq), lambda h, i: (h, 0, i))
    return _hosted_call(
        body, "attn_forward", (H // G, nq),
        [pl.BlockSpec((G, S, HEAD_ROWS), lambda h, i: (h, 0, 0)),
         pl.BlockSpec((G, None, HEAD_ROWS, tq), lambda h, i: (h, i, 0, 0)),
         pl.BlockSpec((G, nq, HEAD_ROWS, tq), lambda h, i: (h, 0, 0, 0))],
        [chan, chan, stat],
        [jax.ShapeDtypeStruct((D, S), BF16), jax.ShapeDtypeStruct((D, S), F32), jax.ShapeDtypeStruct((H, 1, S), F32)],
        [pltpu.VMEM((G, 1, tq), F32), pltpu.VMEM((G, HEAD_ROWS, tq), F32)],
        (ka, qat, vat), ("arbitrary", "arbitrary"), plan)


def _attn_backward(qa, doa, qat, doat, ka, kat, vat, S, D, tq, plan=None):
    H = D // HEAD_DIM
    nq = S // tq
    G = 2

    def body(qa_ref, doa_ref, qat_ref, doat_ref, ka_ref, kat_ref, vat_ref, dq_ref, dk_ref, dv_ref, dk_scr, dv_scr):
        ki = pl.program_id(1)

        @pl.when(ki == 0)
        def _():
            dq_ref[...] = jnp.zeros_like(dq_ref)

        dk_scr[...] = jnp.zeros_like(dk_scr)
        dv_scr[...] = jnp.zeros_like(dv_scr)

        def span(q0, n, diagonal):
            rows = pl.ds(pl.multiple_of(q0 * tq, tq), n * tq)
            s = [jnp.dot(qa_ref[g, rows, :], kat_ref[g], preferred_element_type=F32) for g in range(G)]
            if diagonal:
                s = [_causal(sg, 1) for sg in s]
            p = [jnp.exp(sg) for sg in s]
            ds = [(p[g] * jnp.dot(doa_ref[g, rows, :], vat_ref[g], preferred_element_type=F32)).astype(BF16)
                  for g in range(G)]
            p = [pg.astype(BF16) for pg in p]
            for g in range(G):
                for i in range(n):
                    part = slice(i * tq, (i + 1) * tq)
                    dv_scr[g] += jnp.dot(doat_ref[g, q0 + i, 0:HEAD_DIM, :], p[g][part], preferred_element_type=F32)
                    dk_scr[g] += jnp.dot(qat_ref[g, q0 + i], ds[g][part], preferred_element_type=F32)
                dq_ref[g, rows, :] += jnp.dot(ds[g], ka_ref[g], preferred_element_type=F32)

        n_off = nq - 1 - ki
        odd = n_off % 2

        def off_diagonal_pair(j, _):
            span(ki + 1 + odd + 2 * j, 2, False)
            return 0

        pl.when(odd == 1)(lambda: span(ki, 2, True))
        pl.when(odd == 0)(lambda: span(ki, 1, True))
        lax.fori_loop(0, n_off // 2, off_diagonal_pair, 0)
        dk_ref[...] = dk_scr[...]
        for g in range(G):
            dv_ref[g * HEAD_DIM:(g + 1) * HEAD_DIM, :] = dv_scr[g].astype(BF16)

    whole = pl.BlockSpec((G, S, HEAD_ROWS), lambda h, i: (h, 0, 0))
    tiles = pl.BlockSpec((G, nq, HEAD_ROWS, tq), lambda h, i: (h, 0, 0, 0))
    one = pl.BlockSpec((G, None, HEAD_ROWS, tq), lambda h, i: (h, i, 0, 0))
    return _hosted_call(
        body, "attn_backward", (H // G, nq),
        [whole, whole, tiles, tiles, pl.BlockSpec((G, tq, HEAD_ROWS), lambda h, i: (h, i, 0)), one, one],
        [whole, pl.BlockSpec((G, HEAD_ROWS, tq), lambda h, i: (h, 0, i)),
         pl.BlockSpec((G * HEAD_DIM, tq), lambda h, i: (h, i))],
        [jax.ShapeDtypeStruct((H, S, HEAD_ROWS), F32), jax.ShapeDtypeStruct((H, HEAD_ROWS, S), F32),
         jax.ShapeDtypeStruct((D, S), BF16)],
        [pltpu.VMEM((G, HEAD_ROWS, tq), F32), pltpu.VMEM((G, HEAD_DIM, tq), F32)],
        (qa, doa, qat, doat, ka, kat, vat), ("arbitrary", "arbitrary"), plan)


def _fox_prep_bwd(ut, dq, dkt, dvt, b_f, qg, kg, S, D, tq):
    H = D // HEAD_DIM
    T = min(tq, PREP_LANES)
    nT = S // T
    NU = 3 * D + LANES
    scale = HEAD_DIM ** -0.5

    def body(q_ref, k_ref, f_ref, dq_ref, dk_ref, dv_ref, bf_ref, qg_ref, kg_ref, tri_ref,
             du_ref, dbf_ref, dqg_ref, dkg_ref, gq_acc, gk_acc, fcar, dc_scr):
        step = pl.program_id(0)

        @pl.when(step == 0)
        def _():
            for ref in (gq_acc, gk_acc, fcar, dbf_ref):
                ref[...] = jnp.zeros_like(ref)

        dc_scr[...] = jnp.zeros_like(dc_scr)

        def head(h, _):
            rows = _head_rows(h)
            dqb = dq_ref[h].T
            dkb = dk_ref[h]
            dc_scr[pl.ds(h, 1), :] = dqb[ONES_COL_K:ONES_COL_K + 1, :] - dkb[ONES_ROW_Q:ONES_ROW_Q + 1, :]
            for src, dsrc, gain, acc, mul, base in ((q_ref, dqb, qg_ref, gq_acc, scale, 0),
                                                    (k_ref, dkb, kg_ref, gk_acc, 1.0, D)):
                x = src[rows, :]
                rs = lax.rsqrt(jnp.mean(x * x, axis=0, keepdims=True) + EPS)
                xhat = x * rs
                dn = dsrc[0:HEAD_DIM, :] * mul
                acc[rows, :] += jnp.sum(dn * xhat, axis=1, keepdims=True)
                dxh = dn * gain[rows, :]
                dx = rs * (dxh - xhat * jnp.mean(dxh * xhat, axis=0, keepdims=True))
                du_ref[pl.ds(pl.multiple_of(base + h * HEAD_DIM, HEAD_DIM), HEAD_DIM), :] = dx.astype(BF16)
            return 0

        lax.fori_loop(0, H, head, 0, unroll=min(HEAD_UNROLL, H))
        du_ref[2 * D:3 * D, :] = dv_ref[...]
        dlf, carry = _lane_cumsum(dc_scr[...], tri_ref, fcar[...], True)
        fcar[...] = carry
        dfl = dlf * _sigmoid(-(f_ref[...] + bf_ref[...]))
        dbf_ref[...] += jnp.sum(dfl, axis=1, keepdims=True)
        du_ref[3 * D:NU, :] = dfl.astype(BF16)

        @pl.when(step == nT - 1)
        def _():
            for acc, ref in ((gq_acc, dqg_ref), (gk_acc, dkg_ref)):
                tot = jnp.zeros((HEAD_DIM, 1), F32)
                for h in range(H):
                    tot = tot + acc[h * HEAD_DIM:(h + 1) * HEAD_DIM, :]
                ref[...] = tot

    rev = lambda i: nT - 1 - i
    part = lambda j: pl.BlockSpec((D, T), lambda i: (j, rev(i)))
    colv = lambda n: pl.BlockSpec((n, 1), lambda i: (0, 0))
    return pl.pallas_call(
        body, name="fox_prep_bwd", grid=(nT,),
        in_specs=[part(0), part(1), pl.BlockSpec((LANES, T), lambda i: (3 * D // LANES, rev(i))),
                  pl.BlockSpec((H, T, HEAD_ROWS), lambda i: (0, rev(i), 0)),
                  pl.BlockSpec((H, HEAD_ROWS, T), lambda i: (0, 0, rev(i))), pl.BlockSpec((D, T), lambda i: (0, rev(i))),
                  colv(LANES), colv(D), colv(D), pl.BlockSpec((LANES, LANES), lambda i: (0, 0))],
        out_specs=[pl.BlockSpec((NU, T), lambda i: (0, rev(i))), colv(LANES), colv(HEAD_DIM), colv(HEAD_DIM)],
        out_shape=[jax.ShapeDtypeStruct((NU, S), BF16), jax.ShapeDtypeStruct((LANES, 1), F32),
                   jax.ShapeDtypeStruct((HEAD_DIM, 1), F32), jax.ShapeDtypeStruct((HEAD_DIM, 1), F32)],
        scratch_shapes=[pltpu.VMEM((D, 1), F32), pltpu.VMEM((D, 1), F32), pltpu.VMEM((LANES, 1), F32),
                        pltpu.VMEM((LANES, T), F32)],
        compiler_params=_params(("arbitrary",)),
    )(ut, ut, ut, dq, dkt, dvt, b_f, qg, kg, _tri_matrix(True))


def _block_diag_tiles(w):
    n = w.shape[0]
    per = min(MXU_DIM, n * LRU_BLOCK_DIM) // LRU_BLOCK_DIM
    eye = jnp.eye(per, dtype=w.dtype)
    w5 = w.reshape(n // per, per, LRU_BLOCK_DIM, 1, LRU_BLOCK_DIM) * eye[None, :, None, :, None]
    return w5.reshape(n // per, per * LRU_BLOCK_DIM, per * LRU_BLOCK_DIM).astype(BF16)


def _block_diag_extract(t, n):
    per = t.shape[-1] // LRU_BLOCK_DIM
    eye = jnp.eye(per, dtype=t.dtype)
    t5 = t.reshape(n // per, per, LRU_BLOCK_DIM, per, LRU_BLOCK_DIM) * eye[None, :, None, :, None]
    return t5.sum(axis=3).reshape(n, LRU_BLOCK_DIM, LRU_BLOCK_DIM)


def _local_step(x, tgt, small, wv, grad_view, comm=None):
    S, D = x.shape
    F = 4 * D
    H = D // HEAD_DIM
    nblk = D // LRU_BLOCK_DIM
    NU = 3 * D + LANES
    tq = max(LANES, min(512, S // 4))
    assert S % tq == 0
    vec = lambda a: a.reshape(1, -1).astype(F32)
    col = lambda a: a.reshape(-1, 1).astype(F32)
    mix_g, mlp_g = small["mix_norm"], small["mlp_norm"]
    conv_b = vec(small["lru_conv_b"])
    wr_bd, wi_bd = _block_diag_tiles(small["lru_w_r"][0]), _block_diag_tiles(small["lru_w_i"][0])
    b_r, b_i, lam = vec(small["lru_b_r"]), vec(small["lru_b_i"]), vec(small["lru_lambda"])
    b_f = jnp.pad(col(small["fox_b_f"]), ((0, LANES - H), (0, 0)))
    qg, kg = jnp.tile(col(small["fox_q_gain"]), (H, 1)), jnp.tile(col(small["fox_k_gain"]), (H, 1))
    X = lambda a: _View(a)
    grads = {}
    gout = functools.partial(grad_view, grads)

    def hosted(name, fn, *args):
        plan = comm.before(name, grads) if comm is not None else None
        res, side = fn(*args, plan=plan)
        if plan is not None:
            comm.after(name, side, wv)
        return res

    def hosted_mm(name, *args, **kw):
        plan = comm.before(name, grads) if comm is not None else None
        if plan is None:
            return _matmul(name, *args, **kw)
        res, side = _matmul(name, *args, plan=plan, **kw)
        comm.after(name, side, wv)
        return res

    two = lambda: [_fresh(S, D, F32), _fresh(S, D, BF16)]

    def mlp_up(l, hm):
        return hosted_mm(f"mlp{l}_up", X(hm), wv[f"w1_{l}"], S, F, D, outs=[_fresh(S, F, BF16)], epilogue=_ep_relu2)[0]

    def mlp_bwd(l, xin, hm, act, d, db):
        (dz,) = hosted_mm(f"mlp{l}_dact", X(db), wv[f"w2_{l}"], S, F, D, tb=True, outs=[_fresh(S, F, BF16)],
                          epilogue=_ep_drelu2, extras=[X(act)])
        (grads[f"w2_{l}"],) = _matmul(f"mlp{l}_dw2", X(act), X(db), F, D, S, ta=True, outs=[gout(f"w2_{l}")],
                                      epilogue=_ep_store)
        (grads[f"w1_{l}"],) = _matmul(f"mlp{l}_dw1", X(hm), X(dz), D, F, S, ta=True, outs=[gout(f"w1_{l}")],
                                      epilogue=_ep_store)
        return _matmul(f"mlp{l}_dhm", X(dz), wv[f"w1_{l}"], S, D, F, tb=True, outs=two(), n_sums=1,
                       epilogue=_ep_norm_bwd, extras=[X(xin), X(d)], vecs=[mlp_g[l:l + 1]])

    (h0,) = hosted("mix0_norm", _rms_fwd, "mix0_norm", x, mix_g[0:1], S, D)
    (u0,) = hosted_mm("lru_in", X(h0), wv["lru_in"], S, 2 * D, D, outs=[_fresh(S, 2 * D, F32)], epilogue=_ep_store)
    conv_w = small["conv_w"]
    y, xc, r, ig, hs = hosted("lru_fwd", _lru_fwd, u0, conv_w, conv_b, wr_bd, b_r, wi_bd, b_i, lam, S, D)
    x1, hm0 = _matmul("lru_out", X(y), wv["lru_out"], S, D, D, outs=two(), epilogue=_ep_resid_norm, extras=[X(x)],
                      vecs=[mlp_g[0:1]])
    act0 = mlp_up(0, hm0)
    x2, h1 = hosted_mm("mlp0_down", X(act0), wv["w2_0"], S, D, F, outs=two(), epilogue=_ep_resid_norm, extras=[X(x1)],
                       vecs=[mix_g[1:2]])
    (u1,) = _matmul("fox_in", wv["fox_in"], X(h1), NU, S, D, tb=True, outs=[_fresh(NU, S, F32)], epilogue=_ep_store)
    qat, kat, vat, ka = _fox_prep(u1, b_f, qg, kg, S, D, tq)
    o, o32, lse = hosted("attn_forward", _attn_forward, ka, qat, vat, S, D, tq)
    x3, hm1 = _matmul("fox_out", X(o), wv["fox_out"], S, D, D, ta=True, outs=two(), epilogue=_ep_resid_norm,
                      extras=[X(x2)], vecs=[mlp_g[1:2]])
    act1 = mlp_up(1, hm1)
    (x4,) = _matmul("mlp1_down", X(act1), wv["w2_1"], S, D, F, outs=[_fresh(S, D, F32)], epilogue=_ep_resid,
                    extras=[X(x3)])
    loss, d4, d4b = _loss_head(x4, tgt, S, D)

    d3, d3b, dg_mlp1 = mlp_bwd(1, x3, hm1, act1, d4, d4b)
    (do,) = _matmul("fox_dout", wv["fox_out"], X(d3b), D, S, D, tb=True, outs=[_fresh(D, S, BF16)], epilogue=_ep_store)
    (grads["fox_out"],) = _matmul("fox_dwout", X(o), X(d3b), D, D, S, outs=[gout("fox_out")], epilogue=_ep_store)
    doat, doa, qat1, qa1 = hosted("fox_bwd_prep", _fox_bwd_prep, do, o32, lse, qat, S, D, tq)
    dqn, dkn, dv = hosted("attn_backward", _attn_backward, qa1, doa, qat1, doat, ka, kat, vat, S, D, tq)
    du1, dbf, dqg, dkg = _fox_prep_bwd(u1, dqn, dkn, dv, b_f, qg, kg, S, D, tq)
    (grads["fox_in"],) = _matmul("fox_dwin", X(du1), X(h1), NU, D, S, outs=[gout("fox_in")], epilogue=_ep_store)
    d2, d2b, dg_mix1 = hosted_mm("fox_dh", X(du1), wv["fox_in"], S, D, NU, ta=True, outs=two(), n_sums=1,
                               epilogue=_ep_norm_bwd, extras=[X(x2), X(d3)], vecs=[mix_g[1:2]])
    d1, d1b, dg_mlp0 = mlp_bwd(0, x1, hm0, act0, d2, d2b)
    (grads["lru_out"],) = _matmul("lru_dwout", X(y), X(d1b), D, D, S, ta=True, outs=[gout("lru_out")],
                                  epilogue=_ep_store)
    (dy,) = hosted_mm("lru_dout", X(d1b), wv["lru_out"], S, D, D, tb=True, outs=[_fresh(S, D, F32)],
                      epilogue=_ep_store)
    du0, dcw, dcb, dlam, dbr, dbi, dwr, dwi = hosted("lru_bwd", _lru_bwd, dy, u0, xc, r, ig, hs, conv_w, wr_bd, wi_bd,
                                                     lam, S, D)
    (grads["lru_in"],) = _matmul("lru_dwin", X(h0), X(du0), D, 2 * D, S, ta=True, outs=[gout("lru_in")],
                                 epilogue=_ep_store)
    gx, dg_mix0 = hosted_mm("lru_dh", X(du0), wv["lru_in"], S, D, 2 * D, tb=True, outs=[_fresh(S, D, F32)], n_sums=1,
                            epilogue=lambda *a: _ep_norm_bwd(*a)[::2], extras=[X(x), X(d1)], vecs=[mix_g[0:1]])

    grads.update(
        mix_norm=jnp.concatenate([dg_mix0, dg_mix1], axis=0), mlp_norm=jnp.concatenate([dg_mlp0, dg_mlp1], axis=0),
        conv_w=dcw, lru_conv_b=dcb, lru_w_r=_block_diag_extract(dwr, nblk)[None], lru_b_r=dbr.reshape(1, nblk, -1),
        lru_w_i=_block_diag_extract(dwi, nblk)[None], lru_b_i=dbi.reshape(1, nblk, -1), lru_lambda=dlam,
        fox_b_f=dbf[:H].reshape(1, H), fox_q_gain=dqg.reshape(1, -1), fox_k_gain=dkg.reshape(1, -1))
    return loss, gx, grads


def _place():
    x, y, c = lax.axis_index("x"), lax.axis_index("y"), lax.axis_index("c")
    chips = [(1 - x, y), (x, 1 - y), (1 - x, 1 - y)]
    return x, y, c, 2 * x + y, chips


BOUNCE_BYTES = 1 << 20


def _bounce_shape(rows, cols, dtype):
    chunk = rows
    while chunk % 2 == 0 and chunk > 16 and chunk * cols * jnp.dtype(dtype).itemsize > BOUNCE_BYTES:
        chunk //= 2
    return pltpu.VMEM((2, chunk, cols), dtype)


def _bounce_copy(src, dst, buf, sem):
    chunk = buf.shape[1]
    n = src.shape[0] // chunk
    cin = lambda i: pltpu.make_async_copy(src.at[pl.ds(i * chunk, chunk)], buf.at[i % 2], sem.at[i % 2])
    cout = lambda i: pltpu.make_async_copy(buf.at[i % 2], dst.at[pl.ds(i * chunk, chunk)], sem.at[2 + i % 2])
    cin(0).start()
    for i in range(n):
        cin(i).wait()
        if i + 1 < n:
            if i >= 1:
                cout(i - 1).wait()
            cin(i + 1).start()
        cout(i).start()
    if n >= 2:
        cout(n - 2).wait()
    cout(n - 1).wait()


def _hbm_call(body, name, arrays, out_shape, n_dma_sems, bounce=()):
    scratch = [pltpu.SemaphoreType.DMA((k,)) for k in n_dma_sems]
    for rows, cols, dtype in bounce:
        scratch += [_bounce_shape(rows, cols, dtype), pltpu.SemaphoreType.DMA((4,))]
    return pl.pallas_call(
        body, name=name, in_specs=[ANY] * len(arrays), out_specs=[ANY] * len(out_shape), out_shape=out_shape,
        scratch_shapes=scratch,
        compiler_params=pltpu.CompilerParams(has_side_effects=True, vmem_limit_bytes=VMEM_LIMIT),
    )(*arrays)


class _Gather:
    def __init__(self, shards):
        n = self.n = len(shards)
        self.operands = list(shards)
        self.out_shape = [jax.ShapeDtypeStruct((N_CHIPS,) + tuple(a.shape), a.dtype) for a in shards]
        self.scratch = [pltpu.SemaphoreType.DMA((3 * n,)) for _ in range(4)]
        for a in shards:
            self.scratch += [_bounce_shape(a.shape[0], a.shape[1], a.dtype), pltpu.SemaphoreType.DMA((4,))]

    def _copies(self, ins, outs, scr):
        send, recv, fsend, frecv = scr[:4]
        x, y, c, s, chips = _place()

        def rows(a, chip_idx, which):
            hr = ins[a].shape[0] // 2
            return outs[a].at[chip_idx, pl.ds(which * hr, hr)]

        def landed(a, j, core):
            return rows(a, 2 * chips[j][0] + chips[j][1], core)

        def ici(a, j, mine):
            hr = ins[a].shape[0] // 2
            src, dst = (ins[a].at[pl.ds(c * hr, hr)], rows(a, s, c)) if mine else (landed(a, j, c),) * 2
            return pltpu.make_async_remote_copy(src_ref=src, dst_ref=dst, send_sem=send.at[3 * a + j],
                                                recv_sem=recv.at[3 * a + j], device_id=(*chips[j], c),
                                                device_id_type=MESH)

        def d2d(a, j, mine):
            ref = landed(a, j, c if mine else 1 - c)
            return pltpu.make_async_remote_copy(src_ref=ref, dst_ref=ref, send_sem=fsend.at[3 * a + j],
                                                recv_sem=frecv.at[3 * a + j], device_id=(x, y, 1 - c),
                                                device_id_type=MESH)

        return ici, d2d, s

    def start(self, ins, outs, scr):
        ici, _, _ = self._copies(ins, outs, scr)
        for a in range(self.n):
            for j in range(3):
                ici(a, j, True).start()

    def middle(self, ins, outs, scr):
        ici, d2d, s = self._copies(ins, outs, scr)
        for a in range(self.n):
            _bounce_copy(ins[a], outs[a].at[s], scr[4 + 2 * a], scr[5 + 2 * a])
        for a in range(self.n):
            for j in range(3):
                ici(a, j, False).wait_recv()
                d2d(a, j, True).start()

    def finish(self, ins, outs, scr):
        ici, d2d, _ = self._copies(ins, outs, scr)
        for a in range(self.n):
            for j in range(3):
                d2d(a, j, False).wait_recv()
        for a in range(self.n):
            for j in range(3):
                ici(a, j, True).wait_send()
                d2d(a, j, True).wait_send()


def _run_plan(name, plan):
    k_in, k_out = len(plan.operands), len(plan.out_shape)

    def body(*refs):
        parts = (refs[:k_in], refs[k_in:k_in + k_out], refs[k_in + k_out:])
        plan.start(*parts)
        plan.middle(*parts)
        plan.finish(*parts)

    return pl.pallas_call(
        body, name=name, in_specs=[ANY] * k_in, out_specs=[ANY] * k_out, out_shape=plan.out_shape,
        scratch_shapes=plan.scratch,
        compiler_params=pltpu.CompilerParams(has_side_effects=True, vmem_limit_bytes=VMEM_LIMIT),
    )(*plan.operands)


def _hosted_call(body, name, grid, in_specs, out_specs, out_shape, scratch_shapes, operands, sem, plan=None):
    if plan is None:
        res = pl.pallas_call(body, name=name, grid=grid, in_specs=in_specs, out_specs=out_specs, out_shape=out_shape,
                             scratch_shapes=scratch_shapes, compiler_params=_params(sem))(*operands)
        return res, None
    n_in, n_out, n_scr = len(in_specs), len(out_specs), len(scratch_shapes)
    k_in, k_out = len(plan.operands), len(plan.out_shape)
    total = int(np.prod(grid))
    late = max(0, total - 1 - max(1, total // 8))

    def hosted(*refs):
        ins, refs = refs[:n_in], refs[n_in:]
        p_ins, refs = refs[:k_in], refs[k_in:]
        outs, refs = refs[:n_out], refs[n_out:]
        p_outs, refs = refs[:k_out], refs[k_out:]
        scr, p_scr = refs[:n_scr], refs[n_scr:]
        step = pl.program_id(0)
        for d in range(1, len(grid)):
            step = step * grid[d] + pl.program_id(d)
        pl.when(step == 0)(lambda: plan.start(p_ins, p_outs, p_scr))
        body(*ins, *outs, *scr)
        pl.when(step == late)(lambda: plan.middle(p_ins, p_outs, p_scr))
        pl.when(step == total - 1)(lambda: plan.finish(p_ins, p_outs, p_scr))

    res = pl.pallas_call(
        hosted, name=name, grid=grid, in_specs=list(in_specs) + [ANY] * k_in, out_specs=list(out_specs) + [ANY] * k_out,
        out_shape=list(out_shape) + plan.out_shape, scratch_shapes=list(scratch_shapes) + plan.scratch,
        compiler_params=pltpu.CompilerParams(dimension_semantics=sem, vmem_limit_bytes=VMEM_LIMIT,
                                             has_side_effects=True),
    )(*operands, *plan.operands)
    return res[:n_out], res[n_out:]


class _Swap:
    def __init__(self, arrs):
        self.n = len(arrs)
        self.operands = list(arrs)
        self.out_shape = [jax.ShapeDtypeStruct((a.shape[0], a.shape[1] // 2, a.shape[2]), a.dtype) for a in arrs]
        self.scratch = [pltpu.SemaphoreType.DMA((self.n,)) for _ in range(2)]

    def _copy(self, ins, outs, scr, a):
        x, y, c, _, _ = _place()
        hr = ins[a].shape[1] // 2
        return pltpu.make_async_remote_copy(
            src_ref=ins[a].at[:, pl.ds((1 - c) * hr, hr)], dst_ref=outs[a], send_sem=scr[0].at[a],
            recv_sem=scr[1].at[a], device_id=(x, y, 1 - c), device_id_type=MESH)

    def start(self, ins, outs, scr):
        for a in range(self.n):
            self._copy(ins, outs, scr, a).start()

    def middle(self, ins, outs, scr):
        pass

    def finish(self, ins, outs, scr):
        for a in range(self.n):
            self._copy(ins, outs, scr, a).wait()


class _Scatter:
    def __init__(self, parts):
        n = self.n = len(parts)
        self.operands = list(parts)
        self.out_shape = [jax.ShapeDtypeStruct(a.shape, a.dtype) for a in parts]
        self.scratch = [pltpu.SemaphoreType.DMA((3 * n,)) for _ in range(2)]
        for a in parts:
            self.scratch += [_bounce_shape(a.shape[1], a.shape[2], a.dtype), pltpu.SemaphoreType.DMA((4,))]

    def _copy(self, ins, outs, scr, a, j, mine):
        x, y, c, s, chips = _place()
        t = 2 * chips[j][0] + chips[j][1]
        return pltpu.make_async_remote_copy(
            src_ref=ins[a].at[t], dst_ref=outs[a].at[s if mine else t], send_sem=scr[0].at[3 * a + j],
            recv_sem=scr[1].at[3 * a + j], device_id=(*chips[j], c), device_id_type=MESH)

    def start(self, ins, outs, scr):
        for a in range(self.n):
            for j in range(3):
                self._copy(ins, outs, scr, a, j, True).start()

    def middle(self, ins, outs, scr):
        s = _place()[3]
        for a in range(self.n):
            _bounce_copy(ins[a].at[s], outs[a].at[s], scr[2 + 2 * a], scr[3 + 2 * a])

    def finish(self, ins, outs, scr):
        for a in range(self.n):
            for j in range(3):
                self._copy(ins, outs, scr, a, j, False).wait_recv()
        for a in range(self.n):
            for j in range(3):
                self._copy(ins, outs, scr, a, j, True).wait_send()


def _pair_gather(name, halves):
    n = len(halves)

    def body(*refs):
        ins, outs = refs[:n], refs[n:2 * n]
        send, recv = refs[2 * n:2 * n + 2]
        stage = refs[2 * n + 2:]
        x, y, c, _, _ = _place()
        cps = []
        for a in range(n):
            hr = ins[a].shape[0]
            cp = pltpu.make_async_remote_copy(
                src_ref=ins[a], dst_ref=outs[a].at[pl.ds(c * hr, hr)], send_sem=send.at[a], recv_sem=recv.at[a],
                device_id=(x, y, 1 - c), device_id_type=MESH)
            cp.start()
            cps.append((cp, hr))
        for a, (cp, hr) in enumerate(cps):
            _bounce_copy(ins[a], outs[a].at[pl.ds(c * hr, hr)], stage[2 * a], stage[2 * a + 1])
        for a, (cp, hr) in enumerate(cps):
            cp.wait_send()
            theirs = outs[a].at[pl.ds((1 - c) * hr, hr)]
            pltpu.make_async_remote_copy(src_ref=theirs, dst_ref=theirs, send_sem=send.at[a], recv_sem=recv.at[a],
                                         device_id=(x, y, 1 - c), device_id_type=MESH).wait_recv()

    out_shape = [jax.ShapeDtypeStruct((2 * a.shape[0], a.shape[1]), a.dtype) for a in halves]
    return _hbm_call(body, name, halves, out_shape, (n, n),
                     bounce=[(a.shape[0], a.shape[1], a.dtype) for a in halves])


def _all_reduce_small(name, v):
    shape = jax.ShapeDtypeStruct(v.shape, v.dtype)

    def body(v_ref, out_ref, sib_ref, pair_ref, chips_ref, mine, theirs, others, send, recv, local):
        x, y, c, _, chips = _place()

        def fetch(src, dst):
            cp = pltpu.make_async_copy(src, dst, local.at[0])
            cp.start()
            cp.wait()

        swap = pltpu.make_async_remote_copy(src_ref=v_ref, dst_ref=sib_ref, send_sem=send.at[3], recv_sem=recv.at[3],
                                            device_id=(x, y, 1 - c), device_id_type=MESH)
        swap.start()
        fetch(v_ref, mine)
        swap.wait_send()
        swap.wait_recv()
        fetch(sib_ref, theirs)
        mine[...] = mine[...] + theirs[...]
        fetch(mine, pair_ref)
        sends = [pltpu.make_async_remote_copy(src_ref=pair_ref, dst_ref=chips_ref.at[j], send_sem=send.at[j],
                                              recv_sem=recv.at[j], device_id=(*chips[j], c), device_id_type=MESH)
                 for j in range(N_CHIPS - 1)]
        for cp in sends:
            cp.start()
        for cp in sends:
            cp.wait_recv()
        for cp in sends:
            cp.wait_send()
        fetch(chips_ref, others)
        out_ref[...] = (mine[...] + others[1]) + (others[0] + others[2])

    landing = jax.ShapeDtypeStruct((N_CHIPS - 1, *v.shape), v.dtype)
    return pl.pallas_call(
        body, name=name, in_specs=[ANY], out_specs=[pl.BlockSpec(memory_space=pltpu.VMEM), ANY, ANY, ANY],
        out_shape=[shape, shape, shape, landing],
        scratch_shapes=[pltpu.VMEM(v.shape, v.dtype), pltpu.VMEM(v.shape, v.dtype), pltpu.VMEM(landing.shape, v.dtype),
                        pltpu.SemaphoreType.DMA((N_CHIPS,)), pltpu.SemaphoreType.DMA((N_CHIPS,)),
                        pltpu.SemaphoreType.DMA((1,))],
        compiler_params=pltpu.CompilerParams(has_side_effects=True, vmem_limit_bytes=VMEM_LIMIT),
    )(v)[0]


def _row_tile(rows, cols, itemsize, n_bufs):
    budget = VMEM_LIMIT // 2
    for t in range(min(rows, 1024) // 16 * 16, 0, -16):
        if rows % t == 0 and 2 * n_bufs * t * cols * itemsize <= budget:
            return t
    return rows


def _pair_add(name, g, gsib, core):
    _, r, cols = g.shape
    hr = r // 2
    t = _row_tile(hr, cols, 4, 3)
    per = hr // t

    def body(core_ref, a_ref, b_ref, o_ref):
        o_ref[...] = (a_ref[...].astype(F32) + b_ref[...].astype(F32)).astype(o_ref.dtype)

    grid_spec = pltpu.PrefetchScalarGridSpec(
        num_scalar_prefetch=1, grid=(N_CHIPS, per),
        in_specs=[pl.BlockSpec((None, t, cols), lambda s, i, core: (s, core[0] * per + i, 0)),
                  pl.BlockSpec((None, t, cols), lambda s, i, core: (s, i, 0))],
        out_specs=pl.BlockSpec((None, t, cols), lambda s, i, core: (s, i, 0)))
    return pl.pallas_call(body, name=name, grid_spec=grid_spec,
                          out_shape=jax.ShapeDtypeStruct((N_CHIPS, hr, cols), BF16),
                          compiler_params=_params(("arbitrary", "arbitrary")))(core, g, gsib)


def _chip_sum(name, parts):
    _, hr, cols = parts.shape
    t = _row_tile(hr, cols, 4, 5)

    def body(p_ref, o_ref):
        o_ref[...] = ((p_ref[0].astype(F32) + p_ref[1].astype(F32)) + p_ref[2].astype(F32)) + p_ref[3].astype(F32)

    return pl.pallas_call(
        body, name=name, grid=(hr // t,), in_specs=[pl.BlockSpec((N_CHIPS, t, cols), lambda i: (0, i, 0))],
        out_specs=pl.BlockSpec((t, cols), lambda i: (i, 0)), out_shape=jax.ShapeDtypeStruct((hr, cols), F32),
        compiler_params=_params(("arbitrary",)))(parts)


def _pair_partials(tag, arrs, sib, core):
    return _Scatter([_pair_add(f"{tag}_pair_add{i}", g, gs, core) for i, (g, gs) in enumerate(zip(arrs, sib))])


def _finish_reduce(tag, scattered):
    halves = [_chip_sum(f"{tag}_chip_sum{i}", p) for i, p in enumerate(scattered)]
    return _pair_gather(f"{tag}_pair_gather", halves)


def _adamw(name, w, g_parts, m, v):
    thin = w.ndim == 3
    rows, cols = w.shape[0], w.shape[-1]
    n_parts = len(g_parts)
    part_rows = rows // n_parts
    t = max(d for d in range(1, 257) if part_rows % d == 0) if thin else _row_tile(part_rows, cols, 4, 7 + n_parts)
    per = part_rows // t
    c1 = 1.0 - ADAM_B1 ** ADAM_STEP
    c2 = 1.0 - ADAM_B2 ** ADAM_STEP

    def body(w_ref, m_ref, v_ref, *refs):
        g_refs, (go_ref, d_ref, nm_ref, nv_ref) = refs[:n_parts], refs[n_parts:]
        g = g_refs[0][...]
        for k in range(1, n_parts):
            g = jnp.where(pl.program_id(0) >= k * per, g_refs[k][...], g)
        go_ref[...] = g
        m = ADAM_B1 * m_ref[...] + (1.0 - ADAM_B1) * g
        v = ADAM_B2 * v_ref[...] + (1.0 - ADAM_B2) * (g * g)
        nm_ref[...] = m
        nv_ref[...] = v
        d_ref[...] = -ADAM_LR * ((m / c1) / (jnp.sqrt(v / c2) + ADAM_EPS) + ADAM_WD * w_ref[...])

    block = (t, 1, cols) if thin else (t, cols)
    at = lambda r: (r, 0, 0) if thin else (r, 0)
    spec = pl.BlockSpec(block, lambda i: at(i))
    g_specs = [pl.BlockSpec(block, lambda i, k=k: at(jnp.clip(i - k * per, 0, per - 1))) for k in range(n_parts)]
    shp = jax.ShapeDtypeStruct(w.shape, F32)
    return pl.pallas_call(body, name=name, grid=(rows // t,), in_specs=[spec] * 3 + g_specs, out_specs=[spec] * 4,
                          out_shape=[shp] * 4, compiler_params=_params(("arbitrary",)))(w, m, v, *g_parts)


_WEIGHTS = ["mix_norm", "mlp_norm", "mlp_w1", "mlp_w2", "lru_w_in", "lru_conv_w", "lru_conv_b", "lru_w_r", "lru_b_r",
            "lru_w_i", "lru_b_i", "lru_lambda", "lru_w_out", "fox_w_in", "fox_b_f", "fox_q_gain", "fox_k_gain",
            "fox_w_out"]
_REPLICATED = ["mix_norm", "mlp_norm", "lru_conv_b", "lru_w_r", "lru_b_r", "lru_w_i", "lru_b_i", "lru_lambda",
               "fox_b_f", "fox_q_gain", "fox_k_gain"]
_PACK_TILE = 2 * SUBLANES * LANES


def _as2d(a):
    return a.reshape(-1, a.shape[-1])


def kernel(x, mix_norm, mlp_norm, mlp_w1, mlp_w2, lru_w_in, lru_conv_w, lru_conv_b, lru_w_r, lru_b_r, lru_w_i, lru_b_i, lru_lambda, lru_w_out, fox_w_in, fox_b_f, fox_q_gain, fox_k_gain, fox_w_out, loss_target, m_mix_norm, m_mlp_norm, m_mlp_w1, m_mlp_w2, m_lru_w_in, m_lru_conv_w, m_lru_conv_b, m_lru_w_r, m_lru_b_r, m_lru_w_i, m_lru_b_i, m_lru_lambda, m_lru_w_out, m_fox_w_in, m_fox_b_f, m_fox_q_gain, m_fox_k_gain, m_fox_w_out, v_mix_norm, v_mlp_norm, v_mlp_w1, v_mlp_w2, v_lru_w_in, v_lru_conv_w, v_lru_conv_b, v_lru_w_r, v_lru_b_r, v_lru_w_i, v_lru_b_i, v_lru_lambda, v_lru_w_out, v_fox_w_in, v_fox_b_f, v_fox_q_gain, v_fox_k_gain, v_fox_w_out):
    args = dict(locals())
    W = {n: args[n] for n in _WEIGHTS}
    Mo = {n: args["m_" + n] for n in _WEIGHTS}
    Vo = {n: args["v_" + n] for n in _WEIGHTS}
    S, D = x.shape[1], x.shape[2]
    F = 4 * D
    H = D // HEAD_DIM
    NU = 3 * D + LANES
    FQ, DQ = F // N_CHIPS, D // N_CHIPS
    nfox = fox_w_in.shape[-1]
    chip = 2 * lax.axis_index("x") + lax.axis_index("y")
    core = lax.axis_index("c").astype(jnp.int32).reshape(1)

    cw_flat = jnp.pad(lru_conv_w.reshape(-1), (0, _PACK_TILE - CONV_WIDTH * DQ)).reshape(2 * SUBLANES, LANES)
    w1s, w2s = mlp_w1.astype(BF16), mlp_w2.astype(BF16)
    wv = {}
    small = {n: W[n] for n in _REPLICATED}
    scattered = {}
    members = {"g1": ["w2_1", "w1_1", "fox_out"], "g2": ["fox_in"], "g3": ["w2_0", "w1_0"], "g4": ["lru_out", "lru_in"]}
    swap_at = {"fox_bwd_prep": "g1", "fox_dh": "g2", "lru_dout": "g3"}
    scatter_at = {"attn_backward": "g1", "mlp0_dact": "g2", "lru_bwd": "g3", "lru_dh": "g4"}
    swapped = {}

    fox_rows = -(-nfox // (4 * SUBLANES)) * (4 * SUBLANES)
    fox_t = jnp.pad(jnp.transpose(fox_w_in[0]).astype(BF16), ((0, fox_rows - nfox), (0, 0)))

    def shard_major(name, g):
        if name == "fox_in":
            return jnp.pad(g[:nfox * N_CHIPS].reshape(N_CHIPS, nfox, D), ((0, 0), (0, fox_rows - nfox), (0, 0)))
        return g

    class Comm:
        @staticmethod
        def before(name, grads):
            if name == "mix0_norm":
                return _Gather([lru_w_in[0].astype(BF16)])
            if name == "lru_in":
                return _Gather([lru_w_out[0].astype(BF16), cw_flat])
            if name == "lru_fwd":
                return _Gather([w1s[0]])
            if name == "mlp0_up":
                return _Gather([w2s[0]])
            if name == "mlp0_down":
                return _Gather([fox_t])
            if name == "attn_forward":
                return _Gather([fox_w_out[0].astype(BF16), w1s[1], w2s[1]])
            if name in swap_at:
                group = swap_at[name]
                swapped[group] = [[shard_major(n, grads[n]) for n in members[group]], None]
                return _Swap(swapped[group][0])
            if name in scatter_at:
                group = scatter_at[name]
                if group not in swapped:
                    arrs = [shard_major(n, grads[n]) for n in members[group]]
                    swapped[group] = [arrs, _run_plan(f"{group}_pair_swap", _Swap(arrs))]
                arrs, sib = swapped[group]
                return _pair_partials(group, arrs, sib, core)
            return None

        @staticmethod
        def after(name, res, wv):
            if name == "mix0_norm":
                wv.update(lru_in=_View(res[0], "cs"))
            elif name == "lru_in":
                wv.update(lru_out=_View(res[0], "rs"))
                taps = res[1].reshape(N_CHIPS, -1)[:, :CONV_WIDTH * DQ].reshape(N_CHIPS, CONV_WIDTH, DQ)
                small["conv_w"] = jnp.transpose(taps, (1, 0, 2)).reshape(CONV_WIDTH, D)
            elif name == "lru_fwd":
                wv.update(w1_0=_View(res[0], "cs"))
            elif name == "mlp0_up":
                wv.update(w2_0=_View(res[0], "rs"))
            elif name == "mlp0_down":
                fox_full = jnp.concatenate([res[0][s, :nfox] for s in range(N_CHIPS)], axis=0)
                wv.update(fox_in=_View(jnp.pad(fox_full, ((0, NU - fox_full.shape[0]), (0, 0)))))
            elif name == "attn_forward":
                wv.update(fox_out=_View(res[0], "rs"), w1_1=_View(res[1], "cs"), w2_1=_View(res[2], "rs"))
            elif name in swap_at:
                swapped[swap_at[name]][1] = res
            else:
                scattered.update(zip(members[scatter_at[name]], res))

    def grad_view(grads, name):
        if name in ("w1_0", "w1_1"):
            return _View(None, "cs", shape=(N_CHIPS, D, FQ), dtype=BF16)
        if name in ("w2_0", "w2_1"):
            return _View(None, "rs", shape=(N_CHIPS, FQ, D), dtype=BF16)
        if name == "lru_in":
            return _View(None, "cs", shape=(N_CHIPS, D, 2 * D // N_CHIPS), dtype=BF16)
        if name in ("lru_out", "fox_out"):
            return _View(None, "rs", shape=(N_CHIPS, DQ, D), dtype=BF16)
        return _View(None, shape=(NU, D), dtype=BF16)

    loss, gx, grads = _local_step(x[0], loss_target[0], small, wv, grad_view, Comm)

    pack_names = _REPLICATED + ["conv_w"]
    flat = jnp.concatenate([grads[n].reshape(-1).astype(F32) for n in pack_names] + [loss.reshape(-1)])
    pack = jnp.pad(flat, (0, -flat.shape[0] % _PACK_TILE)).reshape(-1, LANES)
    all_flat = _all_reduce_small("small_grads_all_reduce", pack).reshape(-1)
    order = ["w1_0", "w1_1", "w2_0", "w2_1", "lru_in", "lru_out", "fox_in", "fox_out"]
    red = dict(zip(order, _finish_reduce("grads", [scattered[n] for n in order])))
    G = {}
    off = 0
    for n in pack_names:
        shape = grads[n].shape if n == "conv_w" else W[n].shape
        size = int(np.prod(shape))
        G[n] = all_flat[off:off + size].reshape(shape)
        off += size
    total = all_flat[off]
    G["lru_conv_w"] = lax.dynamic_slice_in_dim(G.pop("conv_w"), chip * DQ, DQ, axis=1)[None]
    parts = {n: [_as2d(G[n])] for n in G}
    parts.update(mlp_w1=[red["w1_0"], red["w1_1"]], mlp_w2=[red["w2_0"], red["w2_1"]], lru_w_in=[red["lru_in"]],
                 lru_w_out=[red["lru_out"]], fox_w_in=[red["fox_in"][:nfox, None, :]], fox_w_out=[red["fox_out"]])

    delta, new_m, new_v = {}, {}, {}
    for n in _WEIGHTS:
        if n == "fox_w_in":
            to_thin = lambda a: jnp.transpose(a, (2, 0, 1))
            res = _adamw(f"adamw_{n}", to_thin(W[n]), parts[n], to_thin(Mo[n]), to_thin(Vo[n]))
            G[n], delta[n], new_m[n], new_v[n] = (jnp.transpose(t, (1, 2, 0)) for t in res)
            continue
        go, d, nm, nv = _adamw(f"adamw_{n}", _as2d(W[n]), parts[n], _as2d(Mo[n]), _as2d(Vo[n]))
        G[n], delta[n], new_m[n], new_v[n] = (t.reshape(W[n].shape) for t in (go, d, nm, nv))

    return (total, gx[None], *[G[n] for n in _WEIGHTS], *[delta[n] for n in _WEIGHTS],
            *[new_m[n] for n in _WEIGHTS], *[new_v[n] for n in _WEIGHTS])
```

```python
import functools

import numpy as np
import jax
import jax.numpy as jnp
from jax import lax
from jax.experimental import pallas as pl
from jax.experimental.pallas import tpu as pltpu

F32 = jnp.float32
BF16 = jnp.bfloat16

HEAD_DIM = 64
LRU_BLOCK_DIM = 64
CONV_WIDTH = 4
LRU_C = 8.0
EPS = 1e-6
NEG_INF = -1e30
ADAM_LR = 0.001
ADAM_B1 = 0.9
ADAM_B2 = 0.999
ADAM_EPS = 1e-08
ADAM_WD = 0.01
ADAM_STEP = 10

N_CHIPS = 4
LANES = 128
SUBLANES = 8
MXU_DIM = 256
VMEM_LIMIT = 52 * 1024 * 1024
MATMUL_TILES = (1024, 640, 512, 256, 128)
MATMUL_VMEM = VMEM_LIMIT * 4 // 5
MESH = pl.DeviceIdType.MESH
ANY = pl.BlockSpec(memory_space=pl.ANY)


def _pick(n, prefs):
    for p in prefs:
        if p <= n and n % p == 0:
            return p
    return n


def _params(sem=None):
    return pltpu.CompilerParams(dimension_semantics=sem, vmem_limit_bytes=VMEM_LIMIT)


class _View:
    def __init__(self, arr, kind="plain", shape=None, dtype=None):
        self.arr = arr
        self.kind = kind
        self.shape = tuple(arr.shape) if arr is not None else tuple(shape)
        self.dtype = arr.dtype if arr is not None else dtype

    def limits(self):
        if self.kind == "plain":
            return 0, 0
        return self.shape[-2], (self.shape[-1] if self.kind == "cs" else 0)

    def spec(self, br, bc, fr, fc):
        if self.kind == "plain":
            return pl.BlockSpec((br, bc), lambda *g: (fr(*g), fc(*g)))
        rows, ncol = self.shape[-2:]
        assert rows % br == 0 and ncol % bc == 0, (self.shape, br, bc)
        if self.kind == "cs":
            per = ncol // bc
            return pl.BlockSpec((None, br, bc), lambda *g: (fc(*g) // per, fr(*g), fc(*g) % per))
        per = rows // br
        return pl.BlockSpec((None, br, bc), lambda *g: (fr(*g) // per, fr(*g) % per, fc(*g)))


def _bf(x):
    return x if x.dtype == BF16 else x.astype(BF16)


def _matmul(name, A, B, M, N, K, *, ta=False, tb=False, outs, epilogue, extras=(), vecs=(), n_sums=0,
            tm=None, tn=None, tk=None, plan=None):
    lim = {"m": [M], "n": [N], "k": [K]}
    for view, (rdim, cdim) in ([(A, "km" if ta else "mk"), (B, "nk" if tb else "kn")]
                               + [(e, "mn") for e in extras] + [(o, "mn") for o in outs]):
        r_lim, c_lim = view.limits()
        lim[rdim].append(r_lim)
        lim[cdim].append(c_lim)
    cap = {d: int(np.gcd.reduce(lim[d])) for d in "mnk"}
    tm = tm or _pick(cap["m"], MATMUL_TILES)
    tn = tn or _pick(cap["n"], MATMUL_TILES)
    tk = tk or _pick(cap["k"], MATMUL_TILES)

    def vmem_bytes(tm, tk):
        size = lambda v: jnp.dtype(v.dtype).itemsize
        tiles = tm * tk * size(A) + tk * tn * size(B) + tm * tn * sum(size(v) for v in list(extras) + list(outs))
        return 2 * tiles + (tm * tn * 4 if K > tk else 0)

    if cap["k"] % (2 * tk) == 0 and vmem_bytes(tm, 2 * tk) <= MATMUL_VMEM:
        tk *= 2
    elif K == tk and cap["m"] % (2 * tm) == 0 and vmem_bytes(2 * tm, tk) <= MATMUL_VMEM:
        tm *= 2
    nk = K // tk
    gi = lambda i, j, k: i
    gj = lambda i, j, k: j
    gk = lambda i, j, k: k
    a_spec = A.spec(tk, tm, gk, gi) if ta else A.spec(tm, tk, gi, gk)
    b_spec = B.spec(tn, tk, gj, gk) if tb else B.spec(tk, tn, gk, gj)
    ca = 0 if ta else 1
    cb = 1 if tb else 0
    ne, no = len(extras) + len(vecs), len(outs)
    assert n_sums == 0 or tn == N
    row_spec = pl.BlockSpec((1, tn), lambda i, j, k: (0, j))
    in_specs = [a_spec, b_spec] + [e.spec(tm, tn, gi, gj) for e in extras] + [row_spec] * len(vecs)
    operands = [A.arr, B.arr] + [e.arr for e in extras] + list(vecs)
    out_specs = [o.spec(tm, tn, gi, gj) for o in outs] + [row_spec] * n_sums
    out_shape = ([jax.ShapeDtypeStruct(o.shape, o.dtype) for o in outs]
                 + [jax.ShapeDtypeStruct((1, N), F32)] * n_sums)

    def body(*refs):
        a_ref, b_ref = refs[0], refs[1]
        ex = refs[2:2 + ne]
        o_refs = refs[2 + ne:2 + ne + no]
        s_refs = refs[2 + ne + no:2 + ne + no + n_sums]
        first_row_tile = pl.program_id(0) == 0

        def prod():
            return lax.dot_general(_bf(a_ref[...]), _bf(b_ref[...]), (((ca,), (cb,)), ((), ())),
                                   preferred_element_type=F32)

        def finish(acc):
            res = epilogue(acc, *[e[...] for e in ex])
            for o_ref, r in zip(o_refs, res[:no]):
                o_ref[...] = r.astype(o_ref.dtype)
            for s_ref, r in zip(s_refs, res[no:]):
                def assign(s_ref=s_ref, r=r):
                    s_ref[...] = r

                def accumulate(s_ref=s_ref, r=r):
                    s_ref[...] += r

                pl.when(first_row_tile)(assign)
                pl.when(jnp.logical_not(first_row_tile))(accumulate)

        if nk == 1:
            finish(prod())
        else:
            acc_ref = refs[-1]
            k = pl.program_id(2)

            @pl.when(k == 0)
            def _():
                acc_ref[...] = jnp.zeros_like(acc_ref)

            acc_ref[...] += prod()

            @pl.when(k == nk - 1)
            def _():
                finish(acc_ref[...])

    res, side = _hosted_call(body, name, (M // tm, N // tn, nk), in_specs, out_specs, out_shape,
                             [pltpu.VMEM((tm, tn), F32)] if nk > 1 else [], operands,
                             ("arbitrary", "arbitrary", "arbitrary"), plan)
    return res if plan is None else (res, side)


def _ep_store(acc):
    return (acc,)


def _ep_resid(acc, res):
    return (res + acc,)


def _ep_resid_norm(acc, res, g):
    xo = res + acc
    r = lax.rsqrt(jnp.mean(xo * xo, axis=-1, keepdims=True) + EPS)
    return (xo, (xo * r) * g)


def _ep_norm_bwd(acc, x, dres, g):
    r = lax.rsqrt(jnp.mean(x * x, axis=-1, keepdims=True) + EPS)
    xhat = x * r
    dxn = acc * g
    tot = dres + r * (dxn - xhat * jnp.mean(dxn * xhat, axis=-1, keepdims=True))
    return (tot, tot, jnp.sum(acc * xhat, axis=0, keepdims=True))


def _ep_relu2(acc):
    zp = jnp.maximum(acc, 0.0)
    return (zp * zp,)


def _ep_drelu2(acc, act):
    return (acc * (2.0 * jnp.sqrt(act.astype(F32))),)


def _fresh(M, N, dtype):
    return _View(None, shape=(M, N), dtype=dtype)


def _rms_fwd(name, x, g, S, D, plan=None):
    T = _pick(S, (512, 256, 128))

    def body(x_ref, g_ref, h_ref):
        x = x_ref[...]
        r = lax.rsqrt(jnp.mean(x * x, axis=-1, keepdims=True) + EPS)
        h_ref[...] = ((x * r) * g_ref[...]).astype(BF16)

    return _hosted_call(body, name, (S // T,),
                        [pl.BlockSpec((T, D), lambda i: (i, 0)), pl.BlockSpec((1, D), lambda i: (0, 0))],
                        [pl.BlockSpec((T, D), lambda i: (i, 0))], [jax.ShapeDtypeStruct((S, D), BF16)], [], (x, g),
                        ("arbitrary",), plan)


def _loss_head(x, tgt, S, D):
    T = _pick(S, (512, 256, 128))

    def body(x_ref, t_ref, loss_ref, d_ref, db_ref):
        @pl.when(pl.program_id(0) == 0)
        def _():
            loss_ref[...] = jnp.zeros_like(loss_ref)

        e = x_ref[...] - t_ref[...]
        loss_ref[...] += 0.5 * jnp.sum(jnp.mean(e * e, axis=-1, keepdims=True), axis=0, keepdims=True)
        d = e * (1.0 / D)
        d_ref[...] = d
        db_ref[...] = d.astype(BF16)

    row = pl.BlockSpec((T, D), lambda i: (i, 0))
    return pl.pallas_call(
        body, name="loss_head", grid=(S // T,), in_specs=[row, row],
        out_specs=[pl.BlockSpec((1, 1), lambda i: (0, 0)), row, row],
        out_shape=[jax.ShapeDtypeStruct((1, 1), F32), jax.ShapeDtypeStruct((S, D), F32),
                   jax.ShapeDtypeStruct((S, D), BF16)],
        compiler_params=_params(("arbitrary",)),
    )(x, tgt)


def _sigmoid(z):
    return 1.0 / (1.0 + jnp.exp(-z))


def _log_sigmoid(z):
    return jnp.minimum(z, 0.0) - jnp.log(1.0 + jnp.exp(-jnp.abs(z)))


_GELU_K = 0.7978845608028654
_GELU_C = 0.044715


def _gelu(x):
    t = jnp.tanh(_GELU_K * (x + _GELU_C * (x * x * x)))
    return 0.5 * x * (1.0 + t)


def _gelu_and_grad(x):
    x2 = x * x
    t = jnp.tanh(_GELU_K * (x + _GELU_C * (x2 * x)))
    g = 0.5 * x * (1.0 + t)
    dg = 0.5 * (1.0 + t) + 0.5 * x * (1.0 - t * t) * (_GELU_K * (1.0 + 3.0 * _GELU_C * x2))
    return g, dg


def _decay_terms(r, ls):
    la = LRU_C * r * ls
    a = jnp.exp(la)
    a2 = a * a
    mult = jnp.sqrt(-jnp.tanh(la) * (a2 + 1.0))
    return a, a2, mult


def _group_scan(a, b, sub, reverse):
    k = 1
    while k < SUBLANES:
        inside = sub < SUBLANES - k if reverse else sub >= k
        shift = SUBLANES - k if reverse else k
        b = b + a * jnp.where(inside, pltpu.roll(b, shift, 0), 0.0)
        a = a * jnp.where(inside, pltpu.roll(a, shift, 0), 1.0)
        k *= 2
    return a, b


def _lru_fwd(u0, conv_w, conv_b, wr_bd, b_r, wi_bd, b_i, lam, S, D, plan=None):
    T = _pick(S, (256, 128))
    GT = wr_bd.shape[-1]
    nG = D // GT

    def body(gb_ref, xb_ref, cw_ref, cb_ref, wr_ref, br_ref, wi_ref, bi_ref, lam_ref,
             y_ref, xc_ref, r_ref, i_ref, hs_ref, ext, a_scr, hcar):
        @pl.when(pl.program_id(0) == 0)
        def _():
            ext[0:SUBLANES, :] = jnp.zeros((SUBLANES, D), F32)
            hcar[...] = jnp.zeros_like(hcar)

        xb = xb_ref[...]
        ext[SUBLANES:SUBLANES + T, :] = xb
        xc = cb_ref[...]
        for k in range(CONV_WIDTH):
            xc = xc + ext[pl.ds(SUBLANES - (CONV_WIDTH - 1) + k, T), :] * cw_ref[k:k + 1, :]
        ext[0:SUBLANES, :] = xb[T - SUBLANES:T, :]
        xc_ref[...] = xc
        xcb = xc.astype(BF16)
        for g in range(nG):
            sl = slice(g * GT, (g + 1) * GT)
            zr = jnp.dot(xcb[:, sl], wr_ref[g], preferred_element_type=F32) + br_ref[:, sl]
            zi = jnp.dot(xcb[:, sl], wi_ref[g], preferred_element_type=F32) + bi_ref[:, sl]
            r_ref[:, sl] = _sigmoid(zr)
            i_ref[:, sl] = _sigmoid(zi)
        r = r_ref[...]
        a, _, mult = _decay_terms(r, _log_sigmoid(lam_ref[...]))
        a_scr[...] = a
        hs_ref[...] = mult * (i_ref[...] * xc)

        sub = lax.broadcasted_iota(jnp.int32, (SUBLANES, D), 0)

        def step(j, h):
            rows = pl.ds(pl.multiple_of(j * SUBLANES, SUBLANES), SUBLANES)
            A, B = _group_scan(a_scr[rows, :], hs_ref[rows, :], sub, False)
            hg = A * h + B
            hs_ref[rows, :] = hg
            return hg[SUBLANES - 1:SUBLANES, :]

        hcar[...] = lax.fori_loop(0, T // SUBLANES, step, hcar[...], unroll=2)
        y_ref[...] = (_gelu(gb_ref[...]) * hs_ref[...]).astype(BF16)

    row = pl.BlockSpec((T, D), lambda i: (i, 0))
    vec = pl.BlockSpec((1, D), lambda i: (0, 0))
    bd = pl.BlockSpec((nG, GT, GT), lambda i: (0, 0, 0))
    f32o = jax.ShapeDtypeStruct((S, D), F32)
    return _hosted_call(
        body, "lru_fwd", (S // T,),
        [row, pl.BlockSpec((T, D), lambda i: (i, 1)), pl.BlockSpec((CONV_WIDTH, D), lambda i: (0, 0)), vec,
         bd, vec, bd, vec, vec],
        [row, row, row, row, row], [jax.ShapeDtypeStruct((S, D), BF16), f32o, f32o, f32o, f32o],
        [pltpu.VMEM((T + SUBLANES, D), F32), pltpu.VMEM((T, D), F32), pltpu.VMEM((1, D), F32)],
        (u0, u0, conv_w, conv_b, wr_bd, b_r, wi_bd, b_i, lam), ("arbitrary",), plan)


def _lru_bwd(dy, u0, xc, r, ig, hs, conv_w, wr_bd, wi_bd, lam, S, D, plan=None):
    T = _pick(S, (128,))
    nT = S // T
    GT = wr_bd.shape[-1]
    nG = D // GT
    W = CONV_WIDTH

    def body(dy_ref, gb_ref, xb_ref, xbp_ref, xc_ref, r_ref, i_ref, hs_ref, hsp_ref, cw_ref, wr_ref, wi_ref, lam_ref,
             du_ref, dcw_ref, dcb_ref, dlam_ref, dbr_ref, dbi_ref, dwr_ref, dwi_ref,
             a_scr, dh_scr, exth, extx, extd, mult_scr, carry):
        step = pl.program_id(0)
        first_tile = step == nT - 1

        @pl.when(step == 0)
        def _():
            for ref in (dcw_ref, dcb_ref, dlam_ref, dbr_ref, dbi_ref, dwr_ref, dwi_ref, carry):
                ref[...] = jnp.zeros_like(ref)
            extd[T:T + SUBLANES, :] = jnp.zeros((SUBLANES, D), F32)

        lam = lam_ref[...]
        ls = _log_sigmoid(lam)
        groups = [slice(gq * GT, (gq + 1) * GT) for gq in range(nG)]
        for sl in groups:
            dy = dy_ref[:, sl]
            g, dgelu = _gelu_and_grad(gb_ref[:, sl])
            du_ref[:, sl] = (dy * hs_ref[:, sl] * dgelu).astype(BF16)
            a, _, mult = _decay_terms(r_ref[:, sl], ls[:, sl])
            a_scr[:, sl] = a
            mult_scr[:, sl] = mult
            dh_scr[:, sl] = dy * g

        sub = lax.broadcasted_iota(jnp.int32, (SUBLANES, D), 0)

        def rstep(j, c):
            rows = pl.ds(pl.multiple_of(T - (j + 1) * SUBLANES, SUBLANES), SUBLANES)
            a_g = a_scr[rows, :]
            a_next = jnp.where(sub < SUBLANES - 1, pltpu.roll(a_g, SUBLANES - 1, 0), 1.0)
            A, B = _group_scan(a_next, dh_scr[rows, :], sub, True)
            d = A * c + B
            dh_scr[rows, :] = d
            return a_g[0:1, :] * d[0:1, :]

        carry[...] = lax.fori_loop(0, T // SUBLANES, rstep, carry[...], unroll=2)
        keep = jnp.where(first_tile, 0.0, 1.0)
        exth[0:SUBLANES, :] = hsp_ref[...] * keep
        exth[SUBLANES:SUBLANES + T, :] = hs_ref[...]
        extx[0:SUBLANES, :] = xbp_ref[...] * keep
        extx[SUBLANES:SUBLANES + T, :] = xb_ref[...]
        nt_dims = (((1,), (1,)), ((), ()))
        tn_dims = (((0,), (0,)), ((), ()))
        for gq, sl in enumerate(groups):
            dh, r, ig, xc = dh_scr[:, sl], r_ref[:, sl], i_ref[:, sl], xc_ref[:, sl]
            a, mult = a_scr[:, sl], mult_scr[:, sl]
            da = dh * exth[pl.ds(SUBLANES - 1, T), sl]
            dmult = dh * (ig * xc)
            dla = da * a - dmult * ((a * a) / mult)
            dlam_ref[:, sl] += jnp.sum(dla * r, axis=0, keepdims=True) * (LRU_C * _sigmoid(-lam[:, sl]))
            dzr = (dla * (LRU_C * ls[:, sl])) * (r * (1.0 - r))
            dzi = (dh * (mult * xc)) * (ig * (1.0 - ig))
            dbr_ref[:, sl] += jnp.sum(dzr, axis=0, keepdims=True)
            dbi_ref[:, sl] += jnp.sum(dzi, axis=0, keepdims=True)
            xcb, zr_g, zi_g = xc.astype(BF16), dzr.astype(BF16), dzi.astype(BF16)
            dxc = dh * (mult * ig) + (lax.dot_general(zr_g, wr_ref[gq], nt_dims, preferred_element_type=F32)
                                      + lax.dot_general(zi_g, wi_ref[gq], nt_dims, preferred_element_type=F32))
            dwr_ref[gq] += lax.dot_general(xcb, zr_g, tn_dims, preferred_element_type=F32)
            dwi_ref[gq] += lax.dot_general(xcb, zi_g, tn_dims, preferred_element_type=F32)
            dcb_ref[:, sl] += jnp.sum(dxc, axis=0, keepdims=True)
            extd[0:T, sl] = dxc
            dxb = jnp.zeros((T, GT), F32)
            for k in range(W):
                dxb = dxb + extd[pl.ds(W - 1 - k, T), sl] * cw_ref[k:k + 1, sl]
                dcw_ref[k:k + 1, sl] += jnp.sum(dxc * extx[pl.ds(SUBLANES - (W - 1) + k, T), sl], axis=0,
                                                keepdims=True)
            extd[T:T + SUBLANES, sl] = dxc[0:SUBLANES, :]
            du_ref[:, D + gq * GT:D + (gq + 1) * GT] = dxb.astype(BF16)

    rev = lambda i: nT - 1 - i
    tpb = T // SUBLANES
    prev8 = lambda i: jnp.maximum(rev(i) * tpb - 1, 0)
    row = pl.BlockSpec((T, D), lambda i: (rev(i), 0))
    vec = pl.BlockSpec((1, D), lambda i: (0, 0))
    bd = pl.BlockSpec((nG, GT, GT), lambda i: (0, 0, 0))
    vec_o = jax.ShapeDtypeStruct((1, D), F32)
    bd_o = jax.ShapeDtypeStruct((nG, GT, GT), F32)
    return _hosted_call(
        body, "lru_bwd", (nT,),
        [row, row, pl.BlockSpec((T, D), lambda i: (rev(i), 1)), pl.BlockSpec((SUBLANES, D), lambda i: (prev8(i), 1)),
         row, row, row, row, pl.BlockSpec((SUBLANES, D), lambda i: (prev8(i), 0)),
         pl.BlockSpec((W, D), lambda i: (0, 0)), bd, bd, vec],
        [pl.BlockSpec((T, 2 * D), lambda i: (rev(i), 0)), pl.BlockSpec((W, D), lambda i: (0, 0)),
         vec, vec, vec, vec, bd, bd],
        [jax.ShapeDtypeStruct((S, 2 * D), BF16), jax.ShapeDtypeStruct((W, D), F32), vec_o, vec_o, vec_o, vec_o, bd_o, bd_o],
        [pltpu.VMEM((T, D), F32), pltpu.VMEM((T, D), F32), pltpu.VMEM((T + SUBLANES, D), F32),
         pltpu.VMEM((T + SUBLANES, D), F32), pltpu.VMEM((T + SUBLANES, D), F32),
         pltpu.VMEM((T, D), F32), pltpu.VMEM((1, D), F32)],
        (dy, u0, u0, u0, xc, r, ig, hs, hs, conv_w, wr_bd, wi_bd, lam), ("arbitrary",), plan)


AUG_ROWS = 16
HEAD_ROWS = 128
LSE_ROW = HEAD_DIM + 6
ONES_ROW_Q = HEAD_DIM + 3
ONES_COL_K = HEAD_DIM
ONES_ROW_V = HEAD_DIM
PREP_LANES = 512
HEAD_UNROLL = 4


def _split3(x):
    b1 = x.astype(BF16).astype(F32)
    r = x - b1
    b2 = r.astype(BF16).astype(F32)
    return b1, b2, r - b2


def _head_block(x, aug, T):
    row = lax.broadcasted_iota(jnp.int32, (AUG_ROWS, T), 0)
    blk = jnp.zeros((AUG_ROWS, T), F32)
    for i, e in enumerate(aug):
        blk = jnp.where(row == i, e, blk)
    return jnp.concatenate([x, blk, jnp.zeros((HEAD_ROWS - HEAD_DIM - AUG_ROWS, T), F32)], axis=0)


def _tri_matrix(lower):
    i = np.arange(LANES)
    m = (i[:, None] >= i[None, :]) if lower else (i[:, None] <= i[None, :])
    return jnp.asarray(m.astype(np.float32), BF16)


def _lane_cumsum(x, tri_ref, carry, reverse):
    n = x.shape[1] // LANES
    tri = tri_ref[...]
    out = [None] * n
    for j in (range(n - 1, -1, -1) if reverse else range(n)):
        cs = carry
        for part in _split3(x[:, j * LANES:(j + 1) * LANES]):
            cs = cs + jnp.dot(part.astype(BF16), tri, preferred_element_type=F32)
        out[j] = cs
        carry = cs[:, 0:1] if reverse else cs[:, LANES - 1:LANES]
    return jnp.concatenate(out, axis=1), carry


def _head_rows(h):
    return pl.ds(pl.multiple_of(h * HEAD_DIM, HEAD_DIM), HEAD_DIM)


def _fox_prep(ut, b_f, qg, kg, S, D, tq):
    H = D // HEAD_DIM
    T = min(tq, PREP_LANES)
    per = tq // T
    scale = HEAD_DIM ** -0.5

    def body(q_ref, k_ref, v_ref, f_ref, bf_ref, qg_ref, kg_ref, tri_ref,
             qat_ref, kat_ref, vat_ref, ka_ref, c_scr, ccar):
        @pl.when(pl.program_id(0) == 0)
        def _():
            ccar[...] = jnp.zeros_like(ccar)

        c, carry = _lane_cumsum(_log_sigmoid(f_ref[...] + bf_ref[...]), tri_ref, ccar[...], False)
        c_scr[...] = c
        ccar[...] = carry

        def head(h, _):
            rows = _head_rows(h)
            c1, c2, c3 = _split3(c_scr[pl.ds(h, 1), :])

            def normed(src, gain, mul):
                x = src[rows, :]
                rs = lax.rsqrt(jnp.mean(x * x, axis=0, keepdims=True) + EPS)
                return ((x * rs) * gain[rows, :]) * mul

            qat_ref[h] = _head_block(normed(q_ref, qg_ref, scale), [c1, c2, c3, 1.0, 1.0, 1.0], T).astype(BF16)
            kb = _head_block(normed(k_ref, kg_ref, 1.0), [1.0, 1.0, 1.0, -c1, -c2, -c3, 1.0, 1.0, 1.0], T)
            kat_ref[h] = kb.astype(BF16)
            ka_ref[h] = kb.T.astype(BF16)
            vat_ref[h] = _head_block(v_ref[rows, :], [1.0, 1.0, 1.0], T).astype(BF16)
            return 0

        lax.fori_loop(0, H, head, 0, unroll=min(HEAD_UNROLL, H))

    part = lambda j: pl.BlockSpec((D, T), lambda i: (j, i))
    colv = lambda n: pl.BlockSpec((n, 1), lambda i: (0, 0))
    tmaj = lambda r: pl.BlockSpec((H, None, r, T), lambda i: (0, i // per, 0, i % per))
    norm = pl.BlockSpec((H, T, HEAD_ROWS), lambda i: (0, i, 0))
    tshape = lambda r: jax.ShapeDtypeStruct((H, S // tq, r, tq), BF16)
    nshape = jax.ShapeDtypeStruct((H, S, HEAD_ROWS), BF16)
    return pl.pallas_call(
        body, name="fox_prep", grid=(S // T,),
        in_specs=[part(0), part(1), part(2), pl.BlockSpec((LANES, T), lambda i: (3 * D // LANES, i)),
                  colv(LANES), colv(D), colv(D), pl.BlockSpec((LANES, LANES), lambda i: (0, 0))],
        out_specs=[tmaj(HEAD_ROWS), tmaj(HEAD_ROWS), tmaj(HEAD_ROWS), norm],
        out_shape=[tshape(HEAD_ROWS), tshape(HEAD_ROWS), tshape(HEAD_ROWS), nshape],
        scratch_shapes=[pltpu.VMEM((LANES, T), F32), pltpu.VMEM((LANES, 1), F32)],
        compiler_params=_params(("arbitrary",)),
    )(ut, ut, ut, ut, b_f, qg, kg, _tri_matrix(False))


def _fox_bwd_prep(dot, ot, lse, qat, S, D, tq, plan=None):
    H = D // HEAD_DIM
    T = min(tq, PREP_LANES)
    per = tq // T

    def body(do_ref, o_ref, lse_ref, qat_ref, doat_ref, doa_ref, qat1_ref, qa1_ref):
        row = lax.broadcasted_iota(jnp.int32, (HEAD_ROWS, T), 0)

        def head(h, _):
            rows = _head_rows(h)
            do = do_ref[rows, :].astype(F32)
            delta = jnp.sum(do * o_ref[rows, :], axis=0, keepdims=True)
            db = _head_block(do, list(_split3(-delta)), T)
            doat_ref[h] = db.astype(BF16)
            doa_ref[h] = db.T.astype(BF16)
            qb = qat_ref[h].astype(F32)
            for i, e in enumerate(_split3(-lse_ref[h])):
                qb = jnp.where(row == LSE_ROW + i, e, qb)
            qat1_ref[h] = qb.astype(BF16)
            qa1_ref[h] = qb.T.astype(BF16)
            return 0

        lax.fori_loop(0, H, head, 0, unroll=min(HEAD_UNROLL, H))

    chan = pl.BlockSpec((D, T), lambda i: (0, i))
    tmaj = pl.BlockSpec((H, None, HEAD_ROWS, T), lambda i: (0, i // per, 0, i % per))
    norm = pl.BlockSpec((H, T, HEAD_ROWS), lambda i: (0, i, 0))
    tshape = jax.ShapeDtypeStruct((H, S // tq, HEAD_ROWS, tq), BF16)
    nshape = jax.ShapeDtypeStruct((H, S, HEAD_ROWS), BF16)
    return _hosted_call(body, "fox_bwd_prep", (S // T,), [chan, chan, pl.BlockSpec((H, 1, T), lambda i: (0, 0, i)), tmaj],
                        [tmaj, norm, tmaj, norm], [tshape, nshape, tshape, nshape], [], (dot, ot, lse, qat),
                        ("arbitrary",), plan)


def _causal(s, k_axis):
    t = min(s.shape)
    ki = lax.broadcasted_iota(jnp.int32, s.shape, k_axis) - (s.shape[k_axis] - t)
    qi = lax.broadcasted_iota(jnp.int32, s.shape, 1 - k_axis)
    return jnp.where(ki <= qi, s, NEG_INF)


def _attn_forward(ka, qat, vat, S, D, tq, plan=None):
    H = D // HEAD_DIM
    nq = S // tq
    G = 4

    def body(ka_ref, qat_ref, vat_ref, o_ref, o32_ref, lse_ref, m_scr, acc_scr):
        qi = pl.program_id(1)
        m_scr[...] = jnp.full_like(m_scr, NEG_INF)
        acc_scr[...] = jnp.zeros_like(acc_scr)

        def span(k0, n, diagonal):
            keys = pl.ds(pl.multiple_of(k0 * tq, tq), n * tq)
            s = [jnp.dot(ka_ref[g, keys, :], qat_ref[g], preferred_element_type=F32) for g in range(G)]
            if diagonal:
                s = [_causal(sg, 0) for sg in s]
            m_prev = [m_scr[g] for g in range(G)]
            m_new = [jnp.maximum(m_prev[g], jnp.max(s[g], axis=0, keepdims=True)) for g in range(G)]
            p = [jnp.exp(s[g] - m_new[g]).astype(BF16) for g in range(G)]
            for g in range(G):
                upd = jnp.dot(vat_ref[g, k0], p[g][0:tq], preferred_element_type=F32)
                for i in range(1, n):
                    upd = upd + jnp.dot(vat_ref[g, k0 + i], p[g][i * tq:(i + 1) * tq], preferred_element_type=F32)
                acc_scr[g] = jnp.exp(m_prev[g] - m_new[g]) * acc_scr[g] + upd
                m_scr[g] = m_new[g]

        def off_diagonal_pair(j, _):
            span(2 * j, 2, False)
            return 0

        lax.fori_loop(0, qi // 2, off_diagonal_pair, 0)
        pl.when(qi % 2 == 1)(lambda: span(qi - 1, 2, True))
        pl.when(qi % 2 == 0)(lambda: span(qi, 1, True))
        for g in range(G):
            l = acc_scr[g, ONES_ROW_V:ONES_ROW_V + 1, :]
            o = acc_scr[g, 0:HEAD_DIM, :] / l
            o_ref[g * HEAD_DIM:(g + 1) * HEAD_DIM, :] = o.astype(BF16)
            o32_ref[g * HEAD_DIM:(g + 1) * HEAD_DIM, :] = o
            lse_ref[g] = m_scr[g] + jnp.log(l)

    chan = pl.BlockSpec((G * HEAD_DIM, tq), lambda h, i: (h, i))
    stat = pl.BlockSpec((G, 1, tq), lambda h, i: (h, 0, i))
    return _hosted_call(
        body, "attn_forward", (H // G, nq),
        [pl.BlockSpec((G, S, HEAD_ROWS), lambda h, i: (h, 0, 0)),
         pl.BlockSpec((G, None, HEAD_ROWS, tq), lambda h, i: (h, i, 0, 0)),
         pl.BlockSpec((G, nq, HEAD_ROWS, tq), lambda h, i: (h, 0, 0, 0))],
        [chan, chan, stat],
        [jax.ShapeDtypeStruct((D, S), BF16), jax.ShapeDtypeStruct((D, S), F32), jax.ShapeDtypeStruct((H, 1, S), F32)],
        [pltpu.VMEM((G, 1, tq), F32), pltpu.VMEM((G, HEAD_ROWS, tq), F32)],
        (ka, qat, vat), ("arbitrary", "arbitrary"), plan)


def _attn_backward(qa, doa, qat, doat, ka, kat, vat, S, D, tq, plan=None):
    H = D // HEAD_DIM
    nq = S // tq
    G = 2

    def body(qa_ref, doa_ref, qat_ref, doat_ref, ka_ref, kat_ref, vat_ref, dq_ref, dk_ref, dv_ref, dk_scr, dv_scr):
        ki = pl.program_id(1)

        @pl.when(ki == 0)
        def _():
            dq_ref[...] = jnp.zeros_like(dq_ref)

        dk_scr[...] = jnp.zeros_like(dk_scr)
        dv_scr[...] = jnp.zeros_like(dv_scr)

        def span(q0, n, diagonal):
            rows = pl.ds(pl.multiple_of(q0 * tq, tq), n * tq)
            s = [jnp.dot(qa_ref[g, rows, :], kat_ref[g], preferred_element_type=F32) for g in range(G)]
            if diagonal:
                s = [_causal(sg, 1) for sg in s]
            p = [jnp.exp(sg) for sg in s]
            ds = [(p[g] * jnp.dot(doa_ref[g, rows, :], vat_ref[g], preferred_element_type=F32)).astype(BF16)
                  for g in range(G)]
            p = [pg.astype(BF16) for pg in p]
            for g in range(G):
                for i in range(n):
                    part = slice(i * tq, (i + 1) * tq)
                    dv_scr[g] += jnp.dot(doat_ref[g, q0 + i, 0:HEAD_DIM, :], p[g][part], preferred_element_type=F32)
                    dk_scr[g] += jnp.dot(qat_ref[g, q0 + i], ds[g][part], preferred_element_type=F32)
                dq_ref[g, rows, :] += jnp.dot(ds[g], ka_ref[g], preferred_element_type=F32)

        n_off = nq - 1 - ki
        odd = n_off % 2

        def off_diagonal_pair(j, _):
            span(ki + 1 + odd + 2 * j, 2, False)
            return 0

        pl.when(odd == 1)(lambda: span(ki, 2, True))
        pl.when(odd == 0)(lambda: span(ki, 1, True))
        lax.fori_loop(0, n_off // 2, off_diagonal_pair, 0)
        dk_ref[...] = dk_scr[...]
        for g in range(G):
            dv_ref[g * HEAD_DIM:(g + 1) * HEAD_DIM, :] = dv_scr[g].astype(BF16)

    whole = pl.BlockSpec((G, S, HEAD_ROWS), lambda h, i: (h, 0, 0))
    tiles = pl.BlockSpec((G, nq, HEAD_ROWS, tq), lambda h, i: (h, 0, 0, 0))
    one = pl.BlockSpec((G, None, HEAD_ROWS, tq), lambda h, i: (h, i, 0, 0))
    return _hosted_call(
        body, "attn_backward", (H // G, nq),
        [whole, whole, tiles, tiles, pl.BlockSpec((G, tq, HEAD_ROWS), lambda h, i: (h, i, 0)), one, one],
        [whole, pl.BlockSpec((G, HEAD_ROWS, tq), lambda h, i: (h, 0, i)),
         pl.BlockSpec((G * HEAD_DIM, tq), lambda h, i: (h, i))],
        [jax.ShapeDtypeStruct((H, S, HEAD_ROWS), F32), jax.ShapeDtypeStruct((H, HEAD_ROWS, S), F32),
         jax.ShapeDtypeStruct((D, S), BF16)],
        [pltpu.VMEM((G, HEAD_ROWS, tq), F32), pltpu.VMEM((G, HEAD_DIM, tq), F32)],
        (qa, doa, qat, doat, ka, kat, vat), ("arbitrary", "arbitrary"), plan)


def _fox_prep_bwd(ut, dq, dkt, dvt, b_f, qg, kg, S, D, tq):
    H = D // HEAD_DIM
    T = min(tq, PREP_LANES)
    nT = S // T
    NU = 3 * D + LANES
    scale = HEAD_DIM ** -0.5

    def body(q_ref, k_ref, f_ref, dq_ref, dk_ref, dv_ref, bf_ref, qg_ref, kg_ref, tri_ref,
             du_ref, dbf_ref, dqg_ref, dkg_ref, gq_acc, gk_acc, fcar, dc_scr):
        step = pl.program_id(0)

        @pl.when(step == 0)
        def _():
            for ref in (gq_acc, gk_acc, fcar, dbf_ref):
                ref[...] = jnp.zeros_like(ref)

        dc_scr[...] = jnp.zeros_like(dc_scr)

        def head(h, _):
            rows = _head_rows(h)
            dqb = dq_ref[h].T
            dkb = dk_ref[h]
            dc_scr[pl.ds(h, 1), :] = dqb[ONES_COL_K:ONES_COL_K + 1, :] - dkb[ONES_ROW_Q:ONES_ROW_Q + 1, :]
            for src, dsrc, gain, acc, mul, base in ((q_ref, dqb, qg_ref, gq_acc, scale, 0),
                                                    (k_ref, dkb, kg_ref, gk_acc, 1.0, D)):
                x = src[rows, :]
                rs = lax.rsqrt(jnp.mean(x * x, axis=0, keepdims=True) + EPS)
                xhat = x * rs
                dn = dsrc[0:HEAD_DIM, :] * mul
                acc[rows, :] += jnp.sum(dn * xhat, axis=1, keepdims=True)
                dxh = dn * gain[rows, :]
                dx = rs * (dxh - xhat * jnp.mean(dxh * xhat, axis=0, keepdims=True))
                du_ref[pl.ds(pl.multiple_of(base + h * HEAD_DIM, HEAD_DIM), HEAD_DIM), :] = dx.astype(BF16)
            return 0

        lax.fori_loop(0, H, head, 0, unroll=min(HEAD_UNROLL, H))
        du_ref[2 * D:3 * D, :] = dv_ref[...]
        dlf, carry = _lane_cumsum(dc_scr[...], tri_ref, fcar[...], True)
        fcar[...] = carry
        dfl = dlf * _sigmoid(-(f_ref[...] + bf_ref[...]))
        dbf_ref[...] += jnp.sum(dfl, axis=1, keepdims=True)
        du_ref[3 * D:NU, :] = dfl.astype(BF16)

        @pl.when(step == nT - 1)
        def _():
            for acc, ref in ((gq_acc, dqg_ref), (gk_acc, dkg_ref)):
                tot = jnp.zeros((HEAD_DIM, 1), F32)
                for h in range(H):
                    tot = tot + acc[h * HEAD_DIM:(h + 1) * HEAD_DIM, :]
                ref[...] = tot

    rev = lambda i: nT - 1 - i
    part = lambda j: pl.BlockSpec((D, T), lambda i: (j, rev(i)))
    colv = lambda n: pl.BlockSpec((n, 1), lambda i: (0, 0))
    return pl.pallas_call(
        body, name="fox_prep_bwd", grid=(nT,),
        in_specs=[part(0), part(1), pl.BlockSpec((LANES, T), lambda i: (3 * D // LANES, rev(i))),
                  pl.BlockSpec((H, T, HEAD_ROWS), lambda i: (0, rev(i), 0)),
                  pl.BlockSpec((H, HEAD_ROWS, T), lambda i: (0, 0, rev(i))), pl.BlockSpec((D, T), lambda i: (0, rev(i))),
                  colv(LANES), colv(D), colv(D), pl.BlockSpec((LANES, LANES), lambda i: (0, 0))],
        out_specs=[pl.BlockSpec((NU, T), lambda i: (0, rev(i))), colv(LANES), colv(HEAD_DIM), colv(HEAD_DIM)],
        out_shape=[jax.ShapeDtypeStruct((NU, S), BF16), jax.ShapeDtypeStruct((LANES, 1), F32),
                   jax.ShapeDtypeStruct((HEAD_DIM, 1), F32), jax.ShapeDtypeStruct((HEAD_DIM, 1), F32)],
        scratch_shapes=[pltpu.VMEM((D, 1), F32), pltpu.VMEM((D, 1), F32), pltpu.VMEM((LANES, 1), F32),
                        pltpu.VMEM((LANES, T), F32)],
        compiler_params=_params(("arbitrary",)),
    )(ut, ut, ut, dq, dkt, dvt, b_f, qg, kg, _tri_matrix(True))


def _block_diag_tiles(w):
    n = w.shape[0]
    per = min(MXU_DIM, n * LRU_BLOCK_DIM) // LRU_BLOCK_DIM
    eye = jnp.eye(per, dtype=w.dtype)
    w5 = w.reshape(n // per, per, LRU_BLOCK_DIM, 1, LRU_BLOCK_DIM) * eye[None, :, None, :, None]
    return w5.reshape(n // per, per * LRU_BLOCK_DIM, per * LRU_BLOCK_DIM).astype(BF16)


def _block_diag_extract(t, n):
    per = t.shape[-1] // LRU_BLOCK_DIM
    eye = jnp.eye(per, dtype=t.dtype)
    t5 = t.reshape(n // per, per, LRU_BLOCK_DIM, per, LRU_BLOCK_DIM) * eye[None, :, None, :, None]
    return t5.sum(axis=3).reshape(n, LRU_BLOCK_DIM, LRU_BLOCK_DIM)


def _local_step(x, tgt, small, wv, grad_view, comm=None):
    S, D = x.shape
    F = 4 * D
    H = D // HEAD_DIM
    nblk = D // LRU_BLOCK_DIM
    NU = 3 * D + LANES
    tq = max(LANES, min(512, S // 4))
    assert S % tq == 0
    vec = lambda a: a.reshape(1, -1).astype(F32)
    col = lambda a: a.reshape(-1, 1).astype(F32)
    mix_g, mlp_g = small["mix_norm"], small["mlp_norm"]
    conv_b = vec(small["lru_conv_b"])
    wr_bd, wi_bd = _block_diag_tiles(small["lru_w_r"][0]), _block_diag_tiles(small["lru_w_i"][0])
    b_r, b_i, lam = vec(small["lru_b_r"]), vec(small["lru_b_i"]), vec(small["lru_lambda"])
    b_f = jnp.pad(col(small["fox_b_f"]), ((0, LANES - H), (0, 0)))
    qg, kg = jnp.tile(col(small["fox_q_gain"]), (H, 1)), jnp.tile(col(small["fox_k_gain"]), (H, 1))
    X = lambda a: _View(a)
    grads = {}
    gout = functools.partial(grad_view, grads)

    def hosted(name, fn, *args):
        plan = comm.before(name, grads) if comm is not None else None
        res, side = fn(*args, plan=plan)
        if plan is not None:
            comm.after(name, side, wv)
        return res

    def hosted_mm(name, *args, **kw):
        plan = comm.before(name, grads) if comm is not None else None
        if plan is None:
            return _matmul(name, *args, **kw)
        res, side = _matmul(name, *args, plan=plan, **kw)
        comm.after(name, side, wv)
        return res

    two = lambda: [_fresh(S, D, F32), _fresh(S, D, BF16)]

    def mlp_up(l, hm):
        return hosted_mm(f"mlp{l}_up", X(hm), wv[f"w1_{l}"], S, F, D, outs=[_fresh(S, F, BF16)], epilogue=_ep_relu2)[0]

    def mlp_bwd(l, xin, hm, act, d, db):
        (dz,) = hosted_mm(f"mlp{l}_dact", X(db), wv[f"w2_{l}"], S, F, D, tb=True, outs=[_fresh(S, F, BF16)],
                          epilogue=_ep_drelu2, extras=[X(act)])
        (grads[f"w2_{l}"],) = _matmul(f"mlp{l}_dw2", X(act), X(db), F, D, S, ta=True, outs=[gout(f"w2_{l}")],
                                      epilogue=_ep_store)
        (grads[f"w1_{l}"],) = _matmul(f"mlp{l}_dw1", X(hm), X(dz), D, F, S, ta=True, outs=[gout(f"w1_{l}")],
                                      epilogue=_ep_store)
        return _matmul(f"mlp{l}_dhm", X(dz), wv[f"w1_{l}"], S, D, F, tb=True, outs=two(), n_sums=1,
                       epilogue=_ep_norm_bwd, extras=[X(xin), X(d)], vecs=[mlp_g[l:l + 1]])

    (h0,) = hosted("mix0_norm", _rms_fwd, "mix0_norm", x, mix_g[0:1], S, D)
    (u0,) = hosted_mm("lru_in", X(h0), wv["lru_in"], S, 2 * D, D, outs=[_fresh(S, 2 * D, F32)], epilogue=_ep_store)
    conv_w = small["conv_w"]
    y, xc, r, ig, hs = hosted("lru_fwd", _lru_fwd, u0, conv_w, conv_b, wr_bd, b_r, wi_bd, b_i, lam, S, D)
    x1, hm0 = _matmul("lru_out", X(y), wv["lru_out"], S, D, D, outs=two(), epilogue=_ep_resid_norm, extras=[X(x)],
                      vecs=[mlp_g[0:1]])
    act0 = mlp_up(0, hm0)
    x2, h1 = hosted_mm("mlp0_down", X(act0), wv["w2_0"], S, D, F, outs=two(), epilogue=_ep_resid_norm, extras=[X(x1)],
                       vecs=[mix_g[1:2]])
    (u1,) = _matmul("fox_in", wv["fox_in"], X(h1), NU, S, D, tb=True, outs=[_fresh(NU, S, F32)], epilogue=_ep_store)
    qat, kat, vat, ka = _fox_prep(u1, b_f, qg, kg, S, D, tq)
    o, o32, lse = hosted("attn_forward", _attn_forward, ka, qat, vat, S, D, tq)
    x3, hm1 = _matmul("fox_out", X(o), wv["fox_out"], S, D, D, ta=True, outs=two(), epilogue=_ep_resid_norm,
                      extras=[X(x2)], vecs=[mlp_g[1:2]])
    act1 = mlp_up(1, hm1)
    (x4,) = _matmul("mlp1_down", X(act1), wv["w2_1"], S, D, F, outs=[_fresh(S, D, F32)], epilogue=_ep_resid,
                    extras=[X(x3)])
    loss, d4, d4b = _loss_head(x4, tgt, S, D)

    d3, d3b, dg_mlp1 = mlp_bwd(1, x3, hm1, act1, d4, d4b)
    (do,) = _matmul("fox_dout", wv["fox_out"], X(d3b), D, S, D, tb=True, outs=[_fresh(D, S, BF16)], epilogue=_ep_store)
    (grads["fox_out"],) = _matmul("fox_dwout", X(o), X(d3b), D, D, S, outs=[gout("fox_out")], epilogue=_ep_store)
    doat, doa, qat1, qa1 = hosted("fox_bwd_prep", _fox_bwd_prep, do, o32, lse, qat, S, D, tq)
    dqn, dkn, dv = hosted("attn_backward", _attn_backward, qa1, doa, qat1, doat, ka, kat, vat, S, D, tq)
    du1, dbf, dqg, dkg = _fox_prep_bwd(u1, dqn, dkn, dv, b_f, qg, kg, S, D, tq)
    (grads["fox_in"],) = _matmul("fox_dwin", X(du1), X(h1), NU, D, S, outs=[gout("fox_in")], epilogue=_ep_store)
    d2, d2b, dg_mix1 = hosted_mm("fox_dh", X(du1), wv["fox_in"], S, D, NU, ta=True, outs=two(), n_sums=1,
                               epilogue=_ep_norm_bwd, extras=[X(x2), X(d3)], vecs=[mix_g[1:2]])
    d1, d1b, dg_mlp0 = mlp_bwd(0, x1, hm0, act0, d2, d2b)
    (grads["lru_out"],) = _matmul("lru_dwout", X(y), X(d1b), D, D, S, ta=True, outs=[gout("lru_out")],
                                  epilogue=_ep_store)
    (dy,) = hosted_mm("lru_dout", X(d1b), wv["lru_out"], S, D, D, tb=True, outs=[_fresh(S, D, F32)],
                      epilogue=_ep_store)
    du0, dcw, dcb, dlam, dbr, dbi, dwr, dwi = hosted("lru_bwd", _lru_bwd, dy, u0, xc, r, ig, hs, conv_w, wr_bd, wi_bd,
                                                     lam, S, D)
    (grads["lru_in"],) = _matmul("lru_dwin", X(h0), X(du0), D, 2 * D, S, ta=True, outs=[gout("lru_in")],
                                 epilogue=_ep_store)
    gx, dg_mix0 = hosted_mm("lru_dh", X(du0), wv["lru_in"], S, D, 2 * D, tb=True, outs=[_fresh(S, D, F32)], n_sums=1,
                            epilogue=lambda *a: _ep_norm_bwd(*a)[::2], extras=[X(x), X(d1)], vecs=[mix_g[0:1]])

    grads.update(
        mix_norm=jnp.concatenate([dg_mix0, dg_mix1], axis=0), mlp_norm=jnp.concatenate([dg_mlp0, dg_mlp1], axis=0),
        conv_w=dcw, lru_conv_b=dcb, lru_w_r=_block_diag_extract(dwr, nblk)[None], lru_b_r=dbr.reshape(1, nblk, -1),
        lru_w_i=_block_diag_extract(dwi, nblk)[None], lru_b_i=dbi.reshape(1, nblk, -1), lru_lambda=dlam,
        fox_b_f=dbf[:H].reshape(1, H), fox_q_gain=dqg.reshape(1, -1), fox_k_gain=dkg.reshape(1, -1))
    return loss, gx, grads


def _place():
    x, y, c = lax.axis_index("x"), lax.axis_index("y"), lax.axis_index("c")
    chips = [(1 - x, y), (x, 1 - y), (1 - x, 1 - y)]
    return x, y, c, 2 * x + y, chips


BOUNCE_BYTES = 1 << 20


def _bounce_shape(rows, cols, dtype):
    chunk = rows
    while chunk % 2 == 0 and chunk > 16 and chunk * cols * jnp.dtype(dtype).itemsize > BOUNCE_BYTES:
        chunk //= 2
    return pltpu.VMEM((2, chunk, cols), dtype)


def _bounce_copy(src, dst, buf, sem):
    chunk = buf.shape[1]
    n = src.shape[0] // chunk
    cin = lambda i: pltpu.make_async_copy(src.at[pl.ds(i * chunk, chunk)], buf.at[i % 2], sem.at[i % 2])
    cout = lambda i: pltpu.make_async_copy(buf.at[i % 2], dst.at[pl.ds(i * chunk, chunk)], sem.at[2 + i % 2])
    cin(0).start()
    for i in range(n):
        cin(i).wait()
        if i + 1 < n:
            if i >= 1:
                cout(i - 1).wait()
            cin(i + 1).start()
        cout(i).start()
    if n >= 2:
        cout(n - 2).wait()
    cout(n - 1).wait()


def _hbm_call(body, name, arrays, out_shape, n_dma_sems, bounce=()):
    scratch = [pltpu.SemaphoreType.DMA((k,)) for k in n_dma_sems]
    for rows, cols, dtype in bounce:
        scratch += [_bounce_shape(rows, cols, dtype), pltpu.SemaphoreType.DMA((4,))]
    return pl.pallas_call(
        body, name=name, in_specs=[ANY] * len(arrays), out_specs=[ANY] * len(out_shape), out_shape=out_shape,
        scratch_shapes=scratch,
        compiler_params=pltpu.CompilerParams(has_side_effects=True, vmem_limit_bytes=VMEM_LIMIT),
    )(*arrays)


class _Gather:
    def __init__(self, shards):
        n = self.n = len(shards)
        self.operands = list(shards)
        self.out_shape = [jax.ShapeDtypeStruct((N_CHIPS,) + tuple(a.shape), a.dtype) for a in shards]
        self.scratch = [pltpu.SemaphoreType.DMA((3 * n,)) for _ in range(4)]
        for a in shards:
            self.scratch += [_bounce_shape(a.shape[0], a.shape[1], a.dtype), pltpu.SemaphoreType.DMA((4,))]

    def _copies(self, ins, outs, scr):
        send, recv, fsend, frecv = scr[:4]
        x, y, c, s, chips = _place()

        def rows(a, chip_idx, which):
            hr = ins[a].shape[0] // 2
            return outs[a].at[chip_idx, pl.ds(which * hr, hr)]

        def landed(a, j, core):
            return rows(a, 2 * chips[j][0] + chips[j][1], core)

        def ici(a, j, mine):
            hr = ins[a].shape[0] // 2
            src, dst = (ins[a].at[pl.ds(c * hr, hr)], rows(a, s, c)) if mine else (landed(a, j, c),) * 2
            return pltpu.make_async_remote_copy(src_ref=src, dst_ref=dst, send_sem=send.at[3 * a + j],
                                                recv_sem=recv.at[3 * a + j], device_id=(*chips[j], c),
                                                device_id_type=MESH)

        def d2d(a, j, mine):
            ref = landed(a, j, c if mine else 1 - c)
            return pltpu.make_async_remote_copy(src_ref=ref, dst_ref=ref, send_sem=fsend.at[3 * a + j],
                                                recv_sem=frecv.at[3 * a + j], device_id=(x, y, 1 - c),
                                                device_id_type=MESH)

        return ici, d2d, s

    def start(self, ins, outs, scr):
        ici, _, _ = self._copies(ins, outs, scr)
        for a in range(self.n):
            for j in range(3):
                ici(a, j, True).start()

    def middle(self, ins, outs, scr):
        ici, d2d, s = self._copies(ins, outs, scr)
        for a in range(self.n):
            _bounce_copy(ins[a], outs[a].at[s], scr[4 + 2 * a], scr[5 + 2 * a])
        for a in range(self.n):
            for j in range(3):
                ici(a, j, False).wait_recv()
                d2d(a, j, True).start()

    def finish(self, ins, outs, scr):
        ici, d2d, _ = self._copies(ins, outs, scr)
        for a in range(self.n):
            for j in range(3):
                d2d(a, j, False).wait_recv()
        for a in range(self.n):
            for j in range(3):
                ici(a, j, True).wait_send()
                d2d(a, j, True).wait_send()


def _run_plan(name, plan):
    k_in, k_out = len(plan.operands), len(plan.out_shape)

    def body(*refs):
        parts = (refs[:k_in], refs[k_in:k_in + k_out], refs[k_in + k_out:])
        plan.start(*parts)
        plan.middle(*parts)
        plan.finish(*parts)

    return pl.pallas_call(
        body, name=name, in_specs=[ANY] * k_in, out_specs=[ANY] * k_out, out_shape=plan.out_shape,
        scratch_shapes=plan.scratch,
        compiler_params=pltpu.CompilerParams(has_side_effects=True, vmem_limit_bytes=VMEM_LIMIT),
    )(*plan.operands)


def _hosted_call(body, name, grid, in_specs, out_specs, out_shape, scratch_shapes, operands, sem, plan=None):
    if plan is None:
        res = pl.pallas_call(body, name=name, grid=grid, in_specs=in_specs, out_specs=out_specs, out_shape=out_shape,
                             scratch_shapes=scratch_shapes, compiler_params=_params(sem))(*operands)
        return res, None
    n_in, n_out, n_scr = len(in_specs), len(out_specs), len(scratch_shapes)
    k_in, k_out = len(plan.operands), len(plan.out_shape)
    total = int(np.prod(grid))
    late = max(0, total - 1 - max(1, total // 8))

    def hosted(*refs):
        ins, refs = refs[:n_in], refs[n_in:]
        p_ins, refs = refs[:k_in], refs[k_in:]
        outs, refs = refs[:n_out], refs[n_out:]
        p_outs, refs = refs[:k_out], refs[k_out:]
        scr, p_scr = refs[:n_scr], refs[n_scr:]
        step = pl.program_id(0)
        for d in range(1, len(grid)):
            step = step * grid[d] + pl.program_id(d)
        pl.when(step == 0)(lambda: plan.start(p_ins, p_outs, p_scr))
        body(*ins, *outs, *scr)
        pl.when(step == late)(lambda: plan.middle(p_ins, p_outs, p_scr))
        pl.when(step == total - 1)(lambda: plan.finish(p_ins, p_outs, p_scr))

    res = pl.pallas_call(
        hosted, name=name, grid=grid, in_specs=list(in_specs) + [ANY] * k_in, out_specs=list(out_specs) + [ANY] * k_out,
        out_shape=list(out_shape) + plan.out_shape, scratch_shapes=list(scratch_shapes) + plan.scratch,
        compiler_params=pltpu.CompilerParams(dimension_semantics=sem, vmem_limit_bytes=VMEM_LIMIT,
                                             has_side_effects=True),
    )(*operands, *plan.operands)
    return res[:n_out], res[n_out:]


def _all_gather(name, shards):
    return _run_plan(name, _Gather(shards))


class _Swap:
    def __init__(self, arrs):
        self.n = len(arrs)
        self.operands = list(arrs)
        self.out_shape = [jax.ShapeDtypeStruct((a.shape[0], a.shape[1] // 2, a.shape[2]), a.dtype) for a in arrs]
        self.scratch = [pltpu.SemaphoreType.DMA((self.n,)) for _ in range(2)]

    def _copy(self, ins, outs, scr, a):
        x, y, c, _, _ = _place()
        hr = ins[a].shape[1] // 2
        return pltpu.make_async_remote_copy(
            src_ref=ins[a].at[:, pl.ds((1 - c) * hr, hr)], dst_ref=outs[a], send_sem=scr[0].at[a],
            recv_sem=scr[1].at[a], device_id=(x, y, 1 - c), device_id_type=MESH)

    def start(self, ins, outs, scr):
        for a in range(self.n):
            self._copy(ins, outs, scr, a).start()

    def middle(self, ins, outs, scr):
        pass

    def finish(self, ins, outs, scr):
        for a in range(self.n):
            self._copy(ins, outs, scr, a).wait()


class _Scatter:
    def __init__(self, parts):
        n = self.n = len(parts)
        self.operands = list(parts)
        self.out_shape = [jax.ShapeDtypeStruct(a.shape, a.dtype) for a in parts]
        self.scratch = [pltpu.SemaphoreType.DMA((3 * n,)) for _ in range(2)]
        for a in parts:
            self.scratch += [_bounce_shape(a.shape[1], a.shape[2], a.dtype), pltpu.SemaphoreType.DMA((4,))]

    def _copy(self, ins, outs, scr, a, j, mine):
        x, y, c, s, chips = _place()
        t = 2 * chips[j][0] + chips[j][1]
        return pltpu.make_async_remote_copy(
            src_ref=ins[a].at[t], dst_ref=outs[a].at[s if mine else t], send_sem=scr[0].at[3 * a + j],
            recv_sem=scr[1].at[3 * a + j], device_id=(*chips[j], c), device_id_type=MESH)

    def start(self, ins, outs, scr):
        for a in range(self.n):
            for j in range(3):
                self._copy(ins, outs, scr, a, j, True).start()

    def middle(self, ins, outs, scr):
        s = _place()[3]
        for a in range(self.n):
            _bounce_copy(ins[a].at[s], outs[a].at[s], scr[2 + 2 * a], scr[3 + 2 * a])

    def finish(self, ins, outs, scr):
        for a in range(self.n):
            for j in range(3):
                self._copy(ins, outs, scr, a, j, False).wait_recv()
        for a in range(self.n):
            for j in range(3):
                self._copy(ins, outs, scr, a, j, True).wait_send()


def _pair_gather(name, halves):
    n = len(halves)

    def body(*refs):
        ins, outs = refs[:n], refs[n:2 * n]
        send, recv = refs[2 * n:2 * n + 2]
        stage = refs[2 * n + 2:]
        x, y, c, _, _ = _place()
        cps = []
        for a in range(n):
            hr = ins[a].shape[0]
            cp = pltpu.make_async_remote_copy(
                src_ref=ins[a], dst_ref=outs[a].at[pl.ds(c * hr, hr)], send_sem=send.at[a], recv_sem=recv.at[a],
                device_id=(x, y, 1 - c), device_id_type=MESH)
            cp.start()
            cps.append((cp, hr))
        for a, (cp, hr) in enumerate(cps):
            _bounce_copy(ins[a], outs[a].at[pl.ds(c * hr, hr)], stage[2 * a], stage[2 * a + 1])
        for a, (cp, hr) in enumerate(cps):
            cp.wait_send()
            theirs = outs[a].at[pl.ds((1 - c) * hr, hr)]
            pltpu.make_async_remote_copy(src_ref=theirs, dst_ref=theirs, send_sem=send.at[a], recv_sem=recv.at[a],
                                         device_id=(x, y, 1 - c), device_id_type=MESH).wait_recv()

    out_shape = [jax.ShapeDtypeStruct((2 * a.shape[0], a.shape[1]), a.dtype) for a in halves]
    return _hbm_call(body, name, halves, out_shape, (n, n),
                     bounce=[(a.shape[0], a.shape[1], a.dtype) for a in halves])


def _row_tile(rows, cols, itemsize, n_bufs):
    budget = VMEM_LIMIT // 2
    for t in range(min(rows, 1024) // 16 * 16, 0, -16):
        if rows % t == 0 and 2 * n_bufs * t * cols * itemsize <= budget:
            return t
    return rows


def _pair_add(name, g, gsib, core, out_dtype):
    _, r, cols = g.shape
    hr = r // 2
    t = _row_tile(hr, cols, 4, 3)
    per = hr // t

    def body(core_ref, a_ref, b_ref, o_ref):
        o_ref[...] = (a_ref[...].astype(F32) + b_ref[...].astype(F32)).astype(o_ref.dtype)

    grid_spec = pltpu.PrefetchScalarGridSpec(
        num_scalar_prefetch=1, grid=(N_CHIPS, per),
        in_specs=[pl.BlockSpec((None, t, cols), lambda s, i, core: (s, core[0] * per + i, 0)),
                  pl.BlockSpec((None, t, cols), lambda s, i, core: (s, i, 0))],
        out_specs=pl.BlockSpec((None, t, cols), lambda s, i, core: (s, i, 0)))
    return pl.pallas_call(body, name=name, grid_spec=grid_spec,
                          out_shape=jax.ShapeDtypeStruct((N_CHIPS, hr, cols), out_dtype),
                          compiler_params=_params(("arbitrary", "arbitrary")))(core, g, gsib)


def _chip_sum(name, parts):
    _, hr, cols = parts.shape
    t = _row_tile(hr, cols, 4, 5)

    def body(p_ref, o_ref):
        o_ref[...] = ((p_ref[0].astype(F32) + p_ref[1].astype(F32)) + p_ref[2].astype(F32)) + p_ref[3].astype(F32)

    return pl.pallas_call(
        body, name=name, grid=(hr // t,), in_specs=[pl.BlockSpec((N_CHIPS, t, cols), lambda i: (0, i, 0))],
        out_specs=pl.BlockSpec((t, cols), lambda i: (i, 0)), out_shape=jax.ShapeDtypeStruct((hr, cols), F32),
        compiler_params=_params(("arbitrary",)))(parts)


def _pair_partials(tag, arrs, sib, wire_dtypes, core):
    return _Scatter([_pair_add(f"{tag}_pair_add{i}", g, gs, core, dt)
                     for i, (g, gs, dt) in enumerate(zip(arrs, sib, wire_dtypes))])


def _finish_reduce(tag, scattered):
    halves = [_chip_sum(f"{tag}_chip_sum{i}", p) for i, p in enumerate(scattered)]
    return _pair_gather(f"{tag}_pair_gather", halves)


def _adamw(name, w, g_parts, m, v):
    thin = w.ndim == 3
    rows, cols = w.shape[0], w.shape[-1]
    n_parts = len(g_parts)
    part_rows = rows // n_parts
    t = max(d for d in range(1, 257) if part_rows % d == 0) if thin else _row_tile(part_rows, cols, 4, 7 + n_parts)
    per = part_rows // t
    c1 = 1.0 - ADAM_B1 ** ADAM_STEP
    c2 = 1.0 - ADAM_B2 ** ADAM_STEP

    def body(w_ref, m_ref, v_ref, *refs):
        g_refs, (go_ref, d_ref, nm_ref, nv_ref) = refs[:n_parts], refs[n_parts:]
        g = g_refs[0][...]
        for k in range(1, n_parts):
            g = jnp.where(pl.program_id(0) >= k * per, g_refs[k][...], g)
        go_ref[...] = g
        m = ADAM_B1 * m_ref[...] + (1.0 - ADAM_B1) * g
        v = ADAM_B2 * v_ref[...] + (1.0 - ADAM_B2) * (g * g)
        nm_ref[...] = m
        nv_ref[...] = v
        d_ref[...] = -ADAM_LR * ((m / c1) / (jnp.sqrt(v / c2) + ADAM_EPS) + ADAM_WD * w_ref[...])

    block = (t, 1, cols) if thin else (t, cols)
    at = lambda r: (r, 0, 0) if thin else (r, 0)
    spec = pl.BlockSpec(block, lambda i: at(i))
    g_specs = [pl.BlockSpec(block, lambda i, k=k: at(jnp.clip(i - k * per, 0, per - 1))) for k in range(n_parts)]
    shp = jax.ShapeDtypeStruct(w.shape, F32)
    return pl.pallas_call(body, name=name, grid=(rows // t,), in_specs=[spec] * 3 + g_specs, out_specs=[spec] * 4,
                          out_shape=[shp] * 4, compiler_params=_params(("arbitrary",)))(w, m, v, *g_parts)


_WEIGHTS = ["mix_norm", "mlp_norm", "mlp_w1", "mlp_w2", "lru_w_in", "lru_conv_w", "lru_conv_b", "lru_w_r", "lru_b_r",
            "lru_w_i", "lru_b_i", "lru_lambda", "lru_w_out", "fox_w_in", "fox_b_f", "fox_q_gain", "fox_k_gain",
            "fox_w_out"]
_REPLICATED = ["mix_norm", "mlp_norm", "lru_conv_b", "lru_w_r", "lru_b_r", "lru_w_i", "lru_b_i", "lru_lambda",
               "fox_b_f", "fox_q_gain", "fox_k_gain"]
_PACK_TILE = 2 * SUBLANES * LANES


def _as2d(a):
    return a.reshape(-1, a.shape[-1])


def kernel(x, mix_norm, mlp_norm, mlp_w1, mlp_w2, lru_w_in, lru_conv_w, lru_conv_b, lru_w_r, lru_b_r, lru_w_i, lru_b_i, lru_lambda, lru_w_out, fox_w_in, fox_b_f, fox_q_gain, fox_k_gain, fox_w_out, loss_target, m_mix_norm, m_mlp_norm, m_mlp_w1, m_mlp_w2, m_lru_w_in, m_lru_conv_w, m_lru_conv_b, m_lru_w_r, m_lru_b_r, m_lru_w_i, m_lru_b_i, m_lru_lambda, m_lru_w_out, m_fox_w_in, m_fox_b_f, m_fox_q_gain, m_fox_k_gain, m_fox_w_out, v_mix_norm, v_mlp_norm, v_mlp_w1, v_mlp_w2, v_lru_w_in, v_lru_conv_w, v_lru_conv_b, v_lru_w_r, v_lru_b_r, v_lru_w_i, v_lru_b_i, v_lru_lambda, v_lru_w_out, v_fox_w_in, v_fox_b_f, v_fox_q_gain, v_fox_k_gain, v_fox_w_out):
    args = dict(locals())
    W = {n: args[n] for n in _WEIGHTS}
    Mo = {n: args["m_" + n] for n in _WEIGHTS}
    Vo = {n: args["v_" + n] for n in _WEIGHTS}
    S, D = x.shape[1], x.shape[2]
    F = 4 * D
    H = D // HEAD_DIM
    NU = 3 * D + LANES
    FQ, DQ = F // N_CHIPS, D // N_CHIPS
    nfox = fox_w_in.shape[-1]
    chip = 2 * lax.axis_index("x") + lax.axis_index("y")
    core = lax.axis_index("c").astype(jnp.int32).reshape(1)

    cw_flat = jnp.pad(lru_conv_w.reshape(-1), (0, _PACK_TILE - CONV_WIDTH * DQ)).reshape(2 * SUBLANES, LANES)
    w1s, w2s = mlp_w1.astype(BF16), mlp_w2.astype(BF16)
    wv = {}
    small = {n: W[n] for n in _REPLICATED}
    scattered = {}
    members = {"g1": ["w2_1", "w1_1", "fox_out"], "g2": ["fox_in"], "g3": ["w2_0", "w1_0"], "g4": ["lru_out", "lru_in"]}
    swap_at = {"fox_bwd_prep": "g1", "fox_dh": "g2", "lru_dout": "g3"}
    scatter_at = {"attn_backward": "g1", "mlp0_dact": "g2", "lru_bwd": "g3", "lru_dh": "g4"}
    swapped = {}

    fox_rows = -(-nfox // (4 * SUBLANES)) * (4 * SUBLANES)
    fox_t = jnp.pad(jnp.transpose(fox_w_in[0]).astype(BF16), ((0, fox_rows - nfox), (0, 0)))

    def shard_major(name, g):
        if name == "fox_in":
            return jnp.pad(g[:nfox * N_CHIPS].reshape(N_CHIPS, nfox, D), ((0, 0), (0, fox_rows - nfox), (0, 0)))
        return g

    class Comm:
        @staticmethod
        def before(name, grads):
            if name == "mix0_norm":
                return _Gather([lru_w_in[0].astype(BF16)])
            if name == "lru_in":
                return _Gather([lru_w_out[0].astype(BF16), cw_flat])
            if name == "lru_fwd":
                return _Gather([w1s[0]])
            if name == "mlp0_up":
                return _Gather([w2s[0]])
            if name == "mlp0_down":
                return _Gather([fox_t])
            if name == "attn_forward":
                return _Gather([fox_w_out[0].astype(BF16), w1s[1], w2s[1]])
            if name in swap_at:
                group = swap_at[name]
                swapped[group] = [[shard_major(n, grads[n]) for n in members[group]], None]
                return _Swap(swapped[group][0])
            if name in scatter_at:
                group = scatter_at[name]
                if group not in swapped:
                    arrs = [shard_major(n, grads[n]) for n in members[group]]
                    swapped[group] = [arrs, _run_plan(f"{group}_pair_swap", _Swap(arrs))]
                arrs, sib = swapped[group]
                return _pair_partials(group, arrs, sib, [BF16] * len(arrs), core)
            return None

        @staticmethod
        def after(name, res, wv):
            if name == "mix0_norm":
                wv.update(lru_in=_View(res[0], "cs"))
            elif name == "lru_in":
                wv.update(lru_out=_View(res[0], "rs"))
                taps = res[1].reshape(N_CHIPS, -1)[:, :CONV_WIDTH * DQ].reshape(N_CHIPS, CONV_WIDTH, DQ)
                small["conv_w"] = jnp.transpose(taps, (1, 0, 2)).reshape(CONV_WIDTH, D)
            elif name == "lru_fwd":
                wv.update(w1_0=_View(res[0], "cs"))
            elif name == "mlp0_up":
                wv.update(w2_0=_View(res[0], "rs"))
            elif name == "mlp0_down":
                fox_full = jnp.concatenate([res[0][s, :nfox] for s in range(N_CHIPS)], axis=0)
                wv.update(fox_in=_View(jnp.pad(fox_full, ((0, NU - fox_full.shape[0]), (0, 0)))))
            elif name == "attn_forward":
                wv.update(fox_out=_View(res[0], "rs"), w1_1=_View(res[1], "cs"), w2_1=_View(res[2], "rs"))
            elif name in swap_at:
                swapped[swap_at[name]][1] = res
            else:
                scattered.update(zip(members[scatter_at[name]], res))

    def grad_view(grads, name):
        if name in ("w1_0", "w1_1"):
            return _View(None, "cs", shape=(N_CHIPS, D, FQ), dtype=BF16)
        if name in ("w2_0", "w2_1"):
            return _View(None, "rs", shape=(N_CHIPS, FQ, D), dtype=BF16)
        if name == "lru_in":
            return _View(None, "cs", shape=(N_CHIPS, D, 2 * D // N_CHIPS), dtype=BF16)
        if name in ("lru_out", "fox_out"):
            return _View(None, "rs", shape=(N_CHIPS, DQ, D), dtype=BF16)
        return _View(None, shape=(NU, D), dtype=BF16)

    loss, gx, grads = _local_step(x[0], loss_target[0], small, wv, grad_view, Comm)

    pack_names = _REPLICATED + ["conv_w"]
    flat = jnp.concatenate([grads[n].reshape(-1).astype(F32) for n in pack_names] + [loss.reshape(-1)])
    per_chip = -(-flat.shape[0] // (N_CHIPS * _PACK_TILE)) * _PACK_TILE
    pack = jnp.pad(flat, (0, N_CHIPS * per_chip - flat.shape[0])).reshape(N_CHIPS, per_chip // LANES, LANES)
    pack_sib = _run_plan("pack_pair_swap", _Swap([pack]))
    (scattered["pack"],) = _run_plan("pack_chip_scatter", _pair_partials("pack", [pack], pack_sib, [F32], core))
    order = ["w1_0", "w1_1", "w2_0", "w2_1", "lru_in", "lru_out", "fox_in", "fox_out", "pack"]
    red = dict(zip(order, _finish_reduce("grads", [scattered[n] for n in order])))
    (all_pack,) = _all_gather("gather_small_grads", [red["pack"]])
    all_flat = all_pack.reshape(-1)
    G = {}
    off = 0
    for n in pack_names:
        shape = grads[n].shape if n == "conv_w" else W[n].shape
        size = int(np.prod(shape))
        G[n] = all_flat[off:off + size].reshape(shape)
        off += size
    total = all_flat[off]
    G["lru_conv_w"] = lax.dynamic_slice_in_dim(G.pop("conv_w"), chip * DQ, DQ, axis=1)[None]
    parts = {n: [_as2d(G[n])] for n in G}
    parts.update(mlp_w1=[red["w1_0"], red["w1_1"]], mlp_w2=[red["w2_0"], red["w2_1"]], lru_w_in=[red["lru_in"]],
                 lru_w_out=[red["lru_out"]], fox_w_in=[red["fox_in"][:nfox, None, :]], fox_w_out=[red["fox_out"]])

    delta, new_m, new_v = {}, {}, {}
    for n in _WEIGHTS:
        if n == "fox_w_in":
            to_thin = lambda a: jnp.transpose(a, (2, 0, 1))
            res = _adamw(f"adamw_{n}", to_thin(W[n]), parts[n], to_thin(Mo[n]), to_thin(Vo[n]))
            G[n], delta[n], new_m[n], new_v[n] = (jnp.transpose(t, (1, 2, 0)) for t in res)
            continue
        go, d, nm, nv = _adamw(f"adamw_{n}", _as2d(W[n]), parts[n], _as2d(Mo[n]), _as2d(Vo[n]))
        G[n], delta[n], new_m[n], new_v[n] = (t.reshape(W[n].shape) for t in (go, d, nm, nv))

    return (total, gx[None], *[G[n] for n in _WEIGHTS], *[delta[n] for n in _WEIGHTS],
            *[new_m[n] for n in _WEIGHTS], *[new_v[n] for n in _WEIGHTS])
```
